```python
import math
import jax, jax.numpy as jnp
from jax import lax
import numpy as np

D_MODEL = 1024
BATCH = 8
SEQ = 4096
DEPTH = 1

N_MEM = 256
RMS_EPS = 1e-6
NEG_INF = -1e30

HG_EXPAND = 128
HG_HEADS = D_MODEL // HG_EXPAND
HG_DK = HG_EXPAND
HG_DV = D_MODEL // HG_HEADS
HG_WIDTH = HG_HEADS * HG_DV
HG_CHUNK = 64
HG_SCALE = HG_DK ** -0.5

DA_CONFIGS = ((128, 1), (512, 4), (2048, 16))
DA_HEADS_PER_GROUP = 4
DA_HEADS = DA_HEADS_PER_GROUP * len(DA_CONFIGS)
DA_HEAD_DIM = D_MODEL // 8
DA_QKV_WIDTH = DA_HEADS * DA_HEAD_DIM
DA_WIDTH = DA_HEADS_PER_GROUP * DA_HEAD_DIM
DA_SCALE = DA_HEAD_DIM ** -0.5

MEM_HEADS = 4
MEM_HEAD_DIM = D_MODEL // 8
MEM_WIDTH = MEM_HEADS * MEM_HEAD_DIM
MEM_SCALE = MEM_HEAD_DIM ** -0.5

D_FF = ((8 * D_MODEL // 3 + 255) // 256) * 256

IN_SPLITS = (HG_WIDTH,) * 5 + (DA_QKV_WIDTH,) * 3 + (MEM_WIDTH,) + (D_MODEL,) * 3
IN_COLS = sum(IN_SPLITS)
IN_SPLIT_POINTS = tuple(int(p) for p in np.cumsum(IN_SPLITS)[:-1])

kernel_name = "hybrid_hgrn2_dilated_memory_block"


def rmsnorm(x, gain):
    xf = x.astype(jnp.float32)
    y = xf * lax.rsqrt(jnp.mean(xf * xf, axis=-1, keepdims=True) + RMS_EPS)
    return (y * gain.astype(jnp.float32)).astype(x.dtype)


def alibi_slopes(n):
    return (2.0 ** (-8.0 * np.arange(1, n + 1) / n)).astype(np.float32)


def gla_chunkwise(q, k, v, log_f):
    B, H, L, dk = q.shape
    dv = v.shape[-1]
    C = HG_CHUNK
    n = L // C
    q, k, log_f = [t.reshape(B, H, n, C, dk) for t in (q, k, log_f)]
    v = v.reshape(B, H, n, C, dv)
    b = jnp.cumsum(log_f.astype(jnp.float32), axis=3)
    b_last = b[:, :, :, -1:, :]
    q_in = q * jnp.exp(b)
    k_in = k * jnp.exp(-b)
    k_st = k * jnp.exp(b_last - b)
    causal = jnp.tril(jnp.ones((C, C), dtype=bool))
    a = jnp.where(causal, jnp.einsum('bhnti,bhnsi->bhnts', q_in, k_in), 0.0)
    o_intra = jnp.einsum('bhnts,bhnsj->bhntj', a, v)
    chunk_state = jnp.einsum('bhnsi,bhnsj->bhnij', k_st, v)
    decay = jnp.exp(b_last[:, :, :, 0, :])

    def step(S, inp):
        d, s = inp
        return d[..., None] * S + s, S

    S0 = jnp.zeros((B, H, dk, dv), jnp.float32)
    _, S_in = lax.scan(step, S0, (jnp.moveaxis(decay, 2, 0), jnp.moveaxis(chunk_state, 2, 0)))
    S_in = jnp.moveaxis(S_in, 0, 2)
    o_inter = jnp.einsum('bhnti,bhnij->bhntj', q_in, S_in)
    return (o_intra + o_inter).reshape(B, H, L, dv)


def hgrn2_mixer(q, f_fw, f_bw, inp, gate, lb_fw, lb_bw, norm_gain):
    B, L, _ = q.shape

    def to_heads(t):
        return t.reshape(B, L, HG_HEADS, -1).transpose(0, 2, 1, 3)

    qh = to_heads(jax.nn.silu(q)) * HG_SCALE
    vh = to_heads(inp)

    def direction(f_logit, lb, flip):
        forget = lb + (1.0 - lb) * jax.nn.sigmoid(f_logit.astype(jnp.float32))
        kh, logf = to_heads(1.0 - forget), to_heads(jnp.log(forget))
        if flip:
            rev = lambda t: jnp.flip(t, axis=2)
            return rev(gla_chunkwise(rev(qh), rev(kh), rev(vh), rev(logf)))
        return gla_chunkwise(qh, kh, vh, logf)

    o = direction(f_fw, lb_fw, False) + direction(f_bw, lb_bw, True)
    o = o.transpose(0, 2, 1, 3)
    o = rmsnorm(o, norm_gain) * jax.nn.silu(gate.reshape(B, L, HG_HEADS, HG_DV).astype(jnp.float32))
    return o.reshape(B, L, HG_WIDTH).astype(q.dtype)


def dilated_group(q, k, v, dilation, radius, slopes):
    B, L, Hg, dh = q.shape
    d, P = dilation, radius
    Ld = L // d
    nb = -(-Ld // P)
    Lp = nb * P

    def residues(t):
        return t.reshape(B, Ld, d, Hg, dh).transpose(0, 3, 2, 1, 4)

    qr, kr, vr = residues(q), residues(k), residues(v)
    qb = jnp.pad(qr, ((0, 0),) * 3 + ((0, Lp - Ld), (0, 0))).reshape(B, Hg, d, nb, P, dh)
    kv_pad = ((0, 0),) * 3 + ((P, Lp - Ld + P), (0, 0))

    def key_blocks(t):
        tb = jnp.pad(t, kv_pad).reshape(B, Hg, d, nb + 2, P, dh)
        return jnp.concatenate([tb[:, :, :, :-2], tb[:, :, :, 1:-1], tb[:, :, :, 2:]], axis=4)

    kb, vb = key_blocks(kr), key_blocks(vr)
    qi = jnp.arange(P)[:, None]
    kj = jnp.arange(3 * P)[None, :]
    rel = kj - P - qi
    s_pos = jnp.arange(nb)[:, None, None] * P + kj[None] - P
    valid = (jnp.abs(rel) <= radius)[None] & (s_pos >= 0) & (s_pos < Ld)
    dist = (d * jnp.abs(rel)).astype(jnp.float32)
    slopes = jnp.asarray(slopes, jnp.float32)

    scores = jnp.einsum('bhrnid,bhrnjd->bhrnij', qb, kb).astype(jnp.float32) * DA_SCALE
    scores = scores - slopes[:, None, None, None, None] * dist
    scores = jnp.where(valid, scores, NEG_INF)
    lse = jax.nn.logsumexp(scores, axis=-1)
    p = jnp.exp(scores - lse[..., None])
    o = jnp.einsum('bhrnij,bhrnjd->bhrnid', p.astype(vb.dtype), vb)
    o = o.reshape(B, Hg, d, Lp, dh)[:, :, :, :Ld]
    o = o.transpose(0, 3, 2, 1, 4).reshape(B, L, Hg, dh)
    lse = lse.reshape(B, Hg, d, Lp)[:, :, :, :Ld].transpose(0, 3, 2, 1).reshape(B, L, Hg)
    return o, lse


def dilated_mixer(q, k, v, q_gain, k_gain):
    B, L, _ = q.shape
    q = rmsnorm(q.reshape(B, L, DA_HEADS, DA_HEAD_DIM), q_gain)
    k = rmsnorm(k.reshape(B, L, DA_HEADS, DA_HEAD_DIM), k_gain)
    v = v.reshape(B, L, DA_HEADS, DA_HEAD_DIM)
    slopes = alibi_slopes(DA_HEADS)
    outs, lses = [], []
    for g, (window, dilation) in enumerate(DA_CONFIGS):
        lo, hi = g * DA_HEADS_PER_GROUP, (g + 1) * DA_HEADS_PER_GROUP
        o, lse = dilated_group(q[:, :, lo:hi], k[:, :, lo:hi], v[:, :, lo:hi],
                               dilation, window // (2 * dilation), slopes[lo:hi])
        outs.append(o)
        lses.append(lse)
    w = jax.nn.softmax(jnp.stack(lses, axis=0), axis=0)
    o = jnp.sum(w[..., None] * jnp.stack(outs, axis=0).astype(jnp.float32), axis=0)
    return o.reshape(B, L, DA_WIDTH).astype(q.dtype)


def memory_mixer(q, mem_n, w_kv, q_gain, k_gain):
    B, L, _ = q.shape
    M = mem_n.shape[1]
    k, v = jnp.split(mem_n @ w_kv, 2, axis=-1)
    qh = rmsnorm(q.reshape(B, L, MEM_HEADS, MEM_HEAD_DIM), q_gain)
    kh = rmsnorm(k.reshape(B, M, MEM_HEADS, MEM_HEAD_DIM), k_gain)
    vh = v.reshape(B, M, MEM_HEADS, MEM_HEAD_DIM)
    s = jnp.einsum('blhd,bmhd->bhlm', qh, kh).astype(jnp.float32) * MEM_SCALE
    p = jax.nn.softmax(s, axis=-1)
    o = jnp.einsum('bhlm,bmhd->blhd', p.astype(vh.dtype), vh)
    return o.reshape(B, L, MEM_WIDTH)


def _fwd_setup_inputs(seed: int = 0) -> dict:
    key = jax.random.key(seed)
    ks = jax.random.split(key, 22)

    def w(k, shape, fan_in):
        return jax.random.normal(k, shape, jnp.float32) * fan_in ** -0.5

    def gain(k, shape):
        return 1.0 + 0.02 * jax.random.normal(k, shape, jnp.float32)

    return {
        "x": jax.random.normal(ks[0], (BATCH, SEQ, D_MODEL), jnp.float32),
        "mem": jax.random.normal(ks[1], (BATCH, N_MEM, D_MODEL), jnp.float32),
        "norm_mix_gain": gain(ks[2], (DEPTH, D_MODEL)),
        "norm_mem_gain": gain(ks[3], (DEPTH, D_MODEL)),
        "w_in": w(ks[4], (DEPTH, D_MODEL, IN_COLS), D_MODEL),
        "lb_logits_fw": 0.1 * jax.random.normal(ks[5], (DEPTH + 1, HG_WIDTH), jnp.float32),
        "lb_logits_bw": 0.1 * jax.random.normal(ks[6], (DEPTH + 1, HG_WIDTH), jnp.float32),
        "hg_norm_gain": gain(ks[7], (DEPTH, HG_DV)),
        "da_q_gain": gain(ks[8], (DEPTH, DA_HEAD_DIM)),
        "da_k_gain": gain(ks[9], (DEPTH, DA_HEAD_DIM)),
        "w_mem_kv": w(ks[10], (DEPTH, D_MODEL, 2 * MEM_WIDTH), D_MODEL),
        "mem_q_gain": gain(ks[11], (DEPTH, MEM_HEAD_DIM)),
        "mem_k_gain": gain(ks[12], (DEPTH, MEM_HEAD_DIM)),
        "w_proj_hg": w(ks[13], (DEPTH, HG_WIDTH, D_MODEL), HG_WIDTH),
        "w_proj_da": w(ks[14], (DEPTH, DA_WIDTH, D_MODEL), DA_WIDTH),
        "w_proj_mem": w(ks[15], (DEPTH, MEM_WIDTH, D_MODEL), MEM_WIDTH),
        "w_out": w(ks[16], (DEPTH, D_MODEL, D_MODEL), D_MODEL),
        "norm_ffn_gain": gain(ks[17], (DEPTH, D_MODEL)),
        "w_ffn_in": w(ks[18], (DEPTH, D_MODEL, 2 * D_FF), D_MODEL),
        "w_ffn_out": w(ks[19], (DEPTH, D_FF, D_MODEL), D_FF),
    }


def _fwd_reference(x, mem, norm_mix_gain, norm_mem_gain, w_in, lb_logits_fw, lb_logits_bw,
              hg_norm_gain, da_q_gain, da_k_gain, w_mem_kv, mem_q_gain, mem_k_gain,
              w_proj_hg, w_proj_da, w_proj_mem, w_out, norm_ffn_gain, w_ffn_in, w_ffn_out):
    lb_fw_table = jnp.cumsum(jax.nn.softmax(lb_logits_fw.astype(jnp.float32), axis=0), axis=0)
    lb_bw_table = jnp.cumsum(jax.nn.softmax(lb_logits_bw.astype(jnp.float32), axis=0), axis=0)
    for l in range(DEPTH):
        h = rmsnorm(x, norm_mix_gain[l])
        proj = h @ w_in[l]
        (hg_q, hg_f_fw, hg_f_bw, hg_i, hg_g, da_q, da_k, da_v, mem_q,
         gate_hg, gate_da, gate_mem) = jnp.split(proj, IN_SPLIT_POINTS, axis=-1)

        o_hg = hgrn2_mixer(hg_q, hg_f_fw, hg_f_bw, hg_i, hg_g,
                           lb_fw_table[l], lb_bw_table[l], hg_norm_gain[l])
        o_da = dilated_mixer(da_q, da_k, da_v, da_q_gain[l], da_k_gain[l])
        mem_n = rmsnorm(mem, norm_mem_gain[l])
        o_mem = memory_mixer(mem_q, mem_n, w_mem_kv[l], mem_q_gain[l], mem_k_gain[l])

        merged = (jax.nn.sigmoid(gate_hg) * (o_hg @ w_proj_hg[l])
                  + jax.nn.sigmoid(gate_da) * (o_da @ w_proj_da[l])
                  + jax.nn.sigmoid(gate_mem) * (o_mem @ w_proj_mem[l]))
        x = x + merged @ w_out[l]

        h = rmsnorm(x, norm_ffn_gain[l])
        a, b = jnp.split(h @ w_ffn_in[l], 2, axis=-1)
        x = x + (jax.nn.silu(a) * b) @ w_ffn_out[l]
    return x


import jax as _jax
import jax.numpy as _jnp

TWIN_FORMAT = 'train_step'
FWD_PARAMS = ['x', 'mem', 'norm_mix_gain', 'norm_mem_gain', 'w_in', 'lb_logits_fw', 'lb_logits_bw', 'hg_norm_gain', 'da_q_gain', 'da_k_gain', 'w_mem_kv', 'mem_q_gain', 'mem_k_gain', 'w_proj_hg', 'w_proj_da', 'w_proj_mem', 'w_out', 'norm_ffn_gain', 'w_ffn_in', 'w_ffn_out']
TWIN_WEIGHTS = ['norm_mix_gain', 'norm_mem_gain', 'w_in', 'lb_logits_fw', 'lb_logits_bw', 'hg_norm_gain', 'da_q_gain', 'da_k_gain', 'w_mem_kv', 'mem_q_gain', 'mem_k_gain', 'w_proj_hg', 'w_proj_da', 'w_proj_mem', 'w_out', 'norm_ffn_gain', 'w_ffn_in', 'w_ffn_out']
TWIN_DIFF_INPUT = 'x'
TWIN_INPUTS = ['x', 'mem', 'norm_mix_gain', 'norm_mem_gain', 'w_in', 'lb_logits_fw', 'lb_logits_bw', 'hg_norm_gain', 'da_q_gain', 'da_k_gain', 'w_mem_kv', 'mem_q_gain', 'mem_k_gain', 'w_proj_hg', 'w_proj_da', 'w_proj_mem', 'w_out', 'norm_ffn_gain', 'w_ffn_in', 'w_ffn_out', 'loss_target', 'm_norm_mix_gain', 'm_norm_mem_gain', 'm_w_in', 'm_lb_logits_fw', 'm_lb_logits_bw', 'm_hg_norm_gain', 'm_da_q_gain', 'm_da_k_gain', 'm_w_mem_kv', 'm_mem_q_gain', 'm_mem_k_gain', 'm_w_proj_hg', 'm_w_proj_da', 'm_w_proj_mem', 'm_w_out', 'm_norm_ffn_gain', 'm_w_ffn_in', 'm_w_ffn_out', 'v_norm_mix_gain', 'v_norm_mem_gain', 'v_w_in', 'v_lb_logits_fw', 'v_lb_logits_bw', 'v_hg_norm_gain', 'v_da_q_gain', 'v_da_k_gain', 'v_w_mem_kv', 'v_mem_q_gain', 'v_mem_k_gain', 'v_w_proj_hg', 'v_w_proj_da', 'v_w_proj_mem', 'v_w_out', 'v_norm_ffn_gain', 'v_w_ffn_in', 'v_w_ffn_out']
TWIN_OUTPUTS = ['loss', 'grad_x', 'grad_norm_mix_gain', 'grad_norm_mem_gain', 'grad_w_in', 'grad_lb_logits_fw', 'grad_lb_logits_bw', 'grad_hg_norm_gain', 'grad_da_q_gain', 'grad_da_k_gain', 'grad_w_mem_kv', 'grad_mem_q_gain', 'grad_mem_k_gain', 'grad_w_proj_hg', 'grad_w_proj_da', 'grad_w_proj_mem', 'grad_w_out', 'grad_norm_ffn_gain', 'grad_w_ffn_in', 'grad_w_ffn_out', 'delta_norm_mix_gain', 'delta_norm_mem_gain', 'delta_w_in', 'delta_lb_logits_fw', 'delta_lb_logits_bw', 'delta_hg_norm_gain', 'delta_da_q_gain', 'delta_da_k_gain', 'delta_w_mem_kv', 'delta_mem_q_gain', 'delta_mem_k_gain', 'delta_w_proj_hg', 'delta_w_proj_da', 'delta_w_proj_mem', 'delta_w_out', 'delta_norm_ffn_gain', 'delta_w_ffn_in', 'delta_w_ffn_out', 'new_m_norm_mix_gain', 'new_m_norm_mem_gain', 'new_m_w_in', 'new_m_lb_logits_fw', 'new_m_lb_logits_bw', 'new_m_hg_norm_gain', 'new_m_da_q_gain', 'new_m_da_k_gain', 'new_m_w_mem_kv', 'new_m_mem_q_gain', 'new_m_mem_k_gain', 'new_m_w_proj_hg', 'new_m_w_proj_da', 'new_m_w_proj_mem', 'new_m_w_out', 'new_m_norm_ffn_gain', 'new_m_w_ffn_in', 'new_m_w_ffn_out', 'new_v_norm_mix_gain', 'new_v_norm_mem_gain', 'new_v_w_in', 'new_v_lb_logits_fw', 'new_v_lb_logits_bw', 'new_v_hg_norm_gain', 'new_v_da_q_gain', 'new_v_da_k_gain', 'new_v_w_mem_kv', 'new_v_mem_q_gain', 'new_v_mem_k_gain', 'new_v_w_proj_hg', 'new_v_w_proj_da', 'new_v_w_proj_mem', 'new_v_w_out', 'new_v_norm_ffn_gain', 'new_v_w_ffn_in', 'new_v_w_ffn_out']
TWIN_LEAF_KINDS = {'loss': 'loss', 'grad_x': 'grad_x', 'grad_norm_mix_gain': 'grad_w', 'grad_norm_mem_gain': 'grad_w', 'grad_w_in': 'grad_w', 'grad_lb_logits_fw': 'grad_w', 'grad_lb_logits_bw': 'grad_w', 'grad_hg_norm_gain': 'grad_w', 'grad_da_q_gain': 'grad_w', 'grad_da_k_gain': 'grad_w', 'grad_w_mem_kv': 'grad_w', 'grad_mem_q_gain': 'grad_w', 'grad_mem_k_gain': 'grad_w', 'grad_w_proj_hg': 'grad_w', 'grad_w_proj_da': 'grad_w', 'grad_w_proj_mem': 'grad_w', 'grad_w_out': 'grad_w', 'grad_norm_ffn_gain': 'grad_w', 'grad_w_ffn_in': 'grad_w', 'grad_w_ffn_out': 'grad_w', 'delta_norm_mix_gain': 'delta_w', 'delta_norm_mem_gain': 'delta_w', 'delta_w_in': 'delta_w', 'delta_lb_logits_fw': 'delta_w', 'delta_lb_logits_bw': 'delta_w', 'delta_hg_norm_gain': 'delta_w', 'delta_da_q_gain': 'delta_w', 'delta_da_k_gain': 'delta_w', 'delta_w_mem_kv': 'delta_w', 'delta_mem_q_gain': 'delta_w', 'delta_mem_k_gain': 'delta_w', 'delta_w_proj_hg': 'delta_w', 'delta_w_proj_da': 'delta_w', 'delta_w_proj_mem': 'delta_w', 'delta_w_out': 'delta_w', 'delta_norm_ffn_gain': 'delta_w', 'delta_w_ffn_in': 'delta_w', 'delta_w_ffn_out': 'delta_w', 'new_m_norm_mix_gain': 'new_m', 'new_m_norm_mem_gain': 'new_m', 'new_m_w_in': 'new_m', 'new_m_lb_logits_fw': 'new_m', 'new_m_lb_logits_bw': 'new_m', 'new_m_hg_norm_gain': 'new_m', 'new_m_da_q_gain': 'new_m', 'new_m_da_k_gain': 'new_m', 'new_m_w_mem_kv': 'new_m', 'new_m_mem_q_gain': 'new_m', 'new_m_mem_k_gain': 'new_m', 'new_m_w_proj_hg': 'new_m', 'new_m_w_proj_da': 'new_m', 'new_m_w_proj_mem': 'new_m', 'new_m_w_out': 'new_m', 'new_m_norm_ffn_gain': 'new_m', 'new_m_w_ffn_in': 'new_m', 'new_m_w_ffn_out': 'new_m', 'new_v_norm_mix_gain': 'new_v', 'new_v_norm_mem_gain': 'new_v', 'new_v_w_in': 'new_v', 'new_v_lb_logits_fw': 'new_v', 'new_v_lb_logits_bw': 'new_v', 'new_v_hg_norm_gain': 'new_v', 'new_v_da_q_gain': 'new_v', 'new_v_da_k_gain': 'new_v', 'new_v_w_mem_kv': 'new_v', 'new_v_mem_q_gain': 'new_v', 'new_v_mem_k_gain': 'new_v', 'new_v_w_proj_hg': 'new_v', 'new_v_w_proj_da': 'new_v', 'new_v_w_proj_mem': 'new_v', 'new_v_w_out': 'new_v', 'new_v_norm_ffn_gain': 'new_v', 'new_v_w_ffn_in': 'new_v', 'new_v_w_ffn_out': 'new_v'}


def _forward(args):
    return _fwd_reference(*[args[k] for k in FWD_PARAMS])


def _output_shape():
    out = _jax.eval_shape(lambda: _forward(_fwd_setup_inputs(0)))
    return out.shape, out.dtype

N_MICROBATCH = 1
ADAM_LR = 0.001
ADAM_B1 = 0.9
ADAM_B2 = 0.999
ADAM_EPS = 1e-08
ADAM_WD = 0.01
ADAM_STEP = 10
PER_EXAMPLE_BATCH_AXIS = {'x': 0, 'mem': 0, 'loss_target': 0}
SHARED_INPUTS = []
_WEIGHT_DTYPES = {'norm_mix_gain': _jnp.float32, 'norm_mem_gain': _jnp.float32, 'w_in': _jnp.float32, 'lb_logits_fw': _jnp.float32, 'lb_logits_bw': _jnp.float32, 'hg_norm_gain': _jnp.float32, 'da_q_gain': _jnp.float32, 'da_k_gain': _jnp.float32, 'w_mem_kv': _jnp.float32, 'mem_q_gain': _jnp.float32, 'mem_k_gain': _jnp.float32, 'w_proj_hg': _jnp.float32, 'w_proj_da': _jnp.float32, 'w_proj_mem': _jnp.float32, 'w_out': _jnp.float32, 'norm_ffn_gain': _jnp.float32, 'w_ffn_in': _jnp.float32, 'w_ffn_out': _jnp.float32}
MOMENT_SCALE = {'norm_mix_gain': 4.333701e+00, 'norm_mem_gain': 9.932111e-02, 'w_in': 6.346454e-02, 'lb_logits_fw': 7.398245e-03, 'lb_logits_bw': 7.395939e-03, 'hg_norm_gain': 2.613163e+01, 'da_q_gain': 1.680047e+00, 'da_k_gain': 1.683404e+00, 'w_mem_kv': 5.912382e-02, 'mem_q_gain': 7.105685e-01, 'mem_k_gain': 7.028634e-01, 'w_proj_hg': 1.567786e-01, 'w_proj_da': 5.220735e-02, 'w_proj_mem': 5.410421e-02, 'w_out': 1.716495e-01, 'norm_ffn_gain': 2.466854e+01, 'w_ffn_in': 1.552791e-01, 'w_ffn_out': 2.429169e-01}


def _to_microbatches(a, axis):
    t = _jnp.moveaxis(a, axis, 0)
    t = t.reshape((N_MICROBATCH, t.shape[0] // N_MICROBATCH) + t.shape[1:])
    return _jnp.moveaxis(t, 1, axis + 1)


def setup_inputs(seed: int = 0) -> dict:
    inp = _fwd_setup_inputs(seed)
    key = _jax.random.fold_in(_jax.random.key(seed), 7919)
    shape, _ = _output_shape()
    out = dict(inp)
    out["loss_target"] = _jax.random.normal(_jax.random.fold_in(key, 0), shape, _jnp.float32)
    for i, name in enumerate(TWIN_WEIGHTS):
        w = inp[name].astype(_jnp.float32)
        if MOMENT_SCALE is None:
            s = _jnp.sqrt(_jnp.mean(_jnp.square(w)) + 1e-30)
        else:
            s = MOMENT_SCALE[name]
        km, kv = _jax.random.split(_jax.random.fold_in(key, i + 1))
        out[name] = w
        out["m_" + name] = s * _jax.random.normal(km, w.shape, _jnp.float32)
        out["v_" + name] = (s * s) * _jax.random.uniform(kv, w.shape, _jnp.float32, 0.5, 1.5)
    if N_MICROBATCH > 1:
        for name, axis in PER_EXAMPLE_BATCH_AXIS.items():
            out[name] = _to_microbatches(out[name], axis)
    return {'x': out['x'], 'mem': out['mem'], 'norm_mix_gain': out['norm_mix_gain'], 'norm_mem_gain': out['norm_mem_gain'], 'w_in': out['w_in'], 'lb_logits_fw': out['lb_logits_fw'], 'lb_logits_bw': out['lb_logits_bw'], 'hg_norm_gain': out['hg_norm_gain'], 'da_q_gain': out['da_q_gain'], 'da_k_gain': out['da_k_gain'], 'w_mem_kv': out['w_mem_kv'], 'mem_q_gain': out['mem_q_gain'], 'mem_k_gain': out['mem_k_gain'], 'w_proj_hg': out['w_proj_hg'], 'w_proj_da': out['w_proj_da'], 'w_proj_mem': out['w_proj_mem'], 'w_out': out['w_out'], 'norm_ffn_gain': out['norm_ffn_gain'], 'w_ffn_in': out['w_ffn_in'], 'w_ffn_out': out['w_ffn_out'], 'loss_target': out['loss_target'], 'm_norm_mix_gain': out['m_norm_mix_gain'], 'm_norm_mem_gain': out['m_norm_mem_gain'], 'm_w_in': out['m_w_in'], 'm_lb_logits_fw': out['m_lb_logits_fw'], 'm_lb_logits_bw': out['m_lb_logits_bw'], 'm_hg_norm_gain': out['m_hg_norm_gain'], 'm_da_q_gain': out['m_da_q_gain'], 'm_da_k_gain': out['m_da_k_gain'], 'm_w_mem_kv': out['m_w_mem_kv'], 'm_mem_q_gain': out['m_mem_q_gain'], 'm_mem_k_gain': out['m_mem_k_gain'], 'm_w_proj_hg': out['m_w_proj_hg'], 'm_w_proj_da': out['m_w_proj_da'], 'm_w_proj_mem': out['m_w_proj_mem'], 'm_w_out': out['m_w_out'], 'm_norm_ffn_gain': out['m_norm_ffn_gain'], 'm_w_ffn_in': out['m_w_ffn_in'], 'm_w_ffn_out': out['m_w_ffn_out'], 'v_norm_mix_gain': out['v_norm_mix_gain'], 'v_norm_mem_gain': out['v_norm_mem_gain'], 'v_w_in': out['v_w_in'], 'v_lb_logits_fw': out['v_lb_logits_fw'], 'v_lb_logits_bw': out['v_lb_logits_bw'], 'v_hg_norm_gain': out['v_hg_norm_gain'], 'v_da_q_gain': out['v_da_q_gain'], 'v_da_k_gain': out['v_da_k_gain'], 'v_w_mem_kv': out['v_w_mem_kv'], 'v_mem_q_gain': out['v_mem_q_gain'], 'v_mem_k_gain': out['v_mem_k_gain'], 'v_w_proj_hg': out['v_w_proj_hg'], 'v_w_proj_da': out['v_w_proj_da'], 'v_w_proj_mem': out['v_w_proj_mem'], 'v_w_out': out['v_w_out'], 'v_norm_ffn_gain': out['v_norm_ffn_gain'], 'v_w_ffn_in': out['v_w_ffn_in'], 'v_w_ffn_out': out['v_w_ffn_out']}


def _loss(weights, diff, rest, loss_target):
    with _jax.named_scope("forward"):
        args = {**rest, TWIN_DIFF_INPUT: diff, **{k: w.astype(_WEIGHT_DTYPES[k]) for k, w in weights.items()}}
        y = _forward(args)
    with _jax.named_scope("loss_head"):
        err = _jnp.square(y.astype(_jnp.float32) - loss_target)
        return 0.5 * _jnp.sum(_jnp.mean(err, axis=-1)) if err.ndim else 0.5 * err


def _adamw(w, g, m, v):
    m = ADAM_B1 * m + (1.0 - ADAM_B1) * g
    v = ADAM_B2 * v + (1.0 - ADAM_B2) * _jnp.square(g)
    m_hat = m / (1.0 - ADAM_B1 ** ADAM_STEP)
    v_hat = v / (1.0 - ADAM_B2 ** ADAM_STEP)
    delta = -ADAM_LR * (m_hat / (_jnp.sqrt(v_hat) + ADAM_EPS) + ADAM_WD * w)
    return delta, m, v


def reference(x, mem, norm_mix_gain, norm_mem_gain, w_in, lb_logits_fw, lb_logits_bw, hg_norm_gain, da_q_gain, da_k_gain, w_mem_kv, mem_q_gain, mem_k_gain, w_proj_hg, w_proj_da, w_proj_mem, w_out, norm_ffn_gain, w_ffn_in, w_ffn_out, loss_target, m_norm_mix_gain, m_norm_mem_gain, m_w_in, m_lb_logits_fw, m_lb_logits_bw, m_hg_norm_gain, m_da_q_gain, m_da_k_gain, m_w_mem_kv, m_mem_q_gain, m_mem_k_gain, m_w_proj_hg, m_w_proj_da, m_w_proj_mem, m_w_out, m_norm_ffn_gain, m_w_ffn_in, m_w_ffn_out, v_norm_mix_gain, v_norm_mem_gain, v_w_in, v_lb_logits_fw, v_lb_logits_bw, v_hg_norm_gain, v_da_q_gain, v_da_k_gain, v_w_mem_kv, v_mem_q_gain, v_mem_k_gain, v_w_proj_hg, v_w_proj_da, v_w_proj_mem, v_w_out, v_norm_ffn_gain, v_w_ffn_in, v_w_ffn_out):
    given = dict(x=x, mem=mem, norm_mix_gain=norm_mix_gain, norm_mem_gain=norm_mem_gain, w_in=w_in, lb_logits_fw=lb_logits_fw, lb_logits_bw=lb_logits_bw, hg_norm_gain=hg_norm_gain, da_q_gain=da_q_gain, da_k_gain=da_k_gain, w_mem_kv=w_mem_kv, mem_q_gain=mem_q_gain, mem_k_gain=mem_k_gain, w_proj_hg=w_proj_hg, w_proj_da=w_proj_da, w_proj_mem=w_proj_mem, w_out=w_out, norm_ffn_gain=norm_ffn_gain, w_ffn_in=w_ffn_in, w_ffn_out=w_ffn_out, loss_target=loss_target, m_norm_mix_gain=m_norm_mix_gain, m_norm_mem_gain=m_norm_mem_gain, m_w_in=m_w_in, m_lb_logits_fw=m_lb_logits_fw, m_lb_logits_bw=m_lb_logits_bw, m_hg_norm_gain=m_hg_norm_gain, m_da_q_gain=m_da_q_gain, m_da_k_gain=m_da_k_gain, m_w_mem_kv=m_w_mem_kv, m_mem_q_gain=m_mem_q_gain, m_mem_k_gain=m_mem_k_gain, m_w_proj_hg=m_w_proj_hg, m_w_proj_da=m_w_proj_da, m_w_proj_mem=m_w_proj_mem, m_w_out=m_w_out, m_norm_ffn_gain=m_norm_ffn_gain, m_w_ffn_in=m_w_ffn_in, m_w_ffn_out=m_w_ffn_out, v_norm_mix_gain=v_norm_mix_gain, v_norm_mem_gain=v_norm_mem_gain, v_w_in=v_w_in, v_lb_logits_fw=v_lb_logits_fw, v_lb_logits_bw=v_lb_logits_bw, v_hg_norm_gain=v_hg_norm_gain, v_da_q_gain=v_da_q_gain, v_da_k_gain=v_da_k_gain, v_w_mem_kv=v_w_mem_kv, v_mem_q_gain=v_mem_q_gain, v_mem_k_gain=v_mem_k_gain, v_w_proj_hg=v_w_proj_hg, v_w_proj_da=v_w_proj_da, v_w_proj_mem=v_w_proj_mem, v_w_out=v_w_out, v_norm_ffn_gain=v_norm_ffn_gain, v_w_ffn_in=v_w_ffn_in, v_w_ffn_out=v_w_ffn_out)
    weights = {n: given[n] for n in TWIN_WEIGHTS}
    shared = {n: given[n] for n in SHARED_INPUTS}
    per_example = {n: given[n] for n in ['x', 'mem']}
    grad_fn = _jax.value_and_grad(_loss, argnums=(0, 1))

    def one_microbatch(ex, loss_target):
        ex = dict(ex)
        diff = ex.pop(TWIN_DIFF_INPUT)
        return grad_fn(weights, diff, {**shared, **ex}, loss_target)

    if N_MICROBATCH == 1:
        loss, (grad_w, grad_x) = one_microbatch(per_example, given["loss_target"])
    else:
        def body(carry, xs):
            loss_sum, grad_sum = carry
            l_k, (gw_k, gx_k) = one_microbatch(xs[0], xs[1])
            with _jax.named_scope("update"):
                return (loss_sum + l_k, _jax.tree.map(_jnp.add, grad_sum, gw_k)), gx_k

        init = (_jnp.zeros((), _jnp.float32), _jax.tree.map(_jnp.zeros_like, weights))
        (loss, grad_w), grad_x = _jax.lax.scan(body, init, (per_example, given["loss_target"]))
    with _jax.named_scope("update"):
        delta_w, new_m, new_v = {}, {}, {}
        for n in TWIN_WEIGHTS:
            delta_w[n], new_m[n], new_v[n] = _adamw(weights[n], grad_w[n], given["m_" + n], given["v_" + n])
    return (loss, grad_x, *[grad_w[n] for n in TWIN_WEIGHTS], *[delta_w[n] for n in TWIN_WEIGHTS],
            *[new_m[n] for n in TWIN_WEIGHTS], *[new_v[n] for n in TWIN_WEIGHTS])
```

```python
import functools
import math

import numpy as np
import jax
import jax.numpy as jnp
from jax import lax
from jax.experimental import pallas as pl
from jax.experimental.pallas import tpu as pltpu

F32 = jnp.float32
BF16 = jnp.bfloat16
MESH = pl.DeviceIdType.MESH

D_MODEL = 1024
HEAD = 128
HG_HEADS = 8
DA_GROUPS = ((1, 64), (4, 64), (16, 64))
DA_GH = 4
MEM_HEADS = 4
N_MEM = 256
D_FF = 2816
CHUNK = 64
RMS_EPS = 1e-6
NEG_INF = -1e30
HG_SCALE = HEAD ** -0.5
ATT_SCALE = HEAD ** -0.5
VMEM_LIMIT_V7X = 48 * 1024 * 1024

C_HQ, C_FF, C_FB, C_HI, C_HG = 0, 1024, 2048, 3072, 4096
C_DQ, C_DK, C_DV, C_MQ = 5120, 6656, 8192, 9728
C_GHG, C_GDA, C_GMEM = 10240, 11264, 12288
IN_COLS = 13312

ADAM_LR, ADAM_B1, ADAM_B2, ADAM_EPS, ADAM_WD, ADAM_STEP = 0.001, 0.9, 0.999, 1e-08, 0.01, 10


def _params(sem, vmem=VMEM_LIMIT_V7X):
    return pltpu.CompilerParams(dimension_semantics=sem, vmem_limit_bytes=vmem)


def _dot(a, b):
    return jnp.dot(a.astype(BF16), b.astype(BF16), preferred_element_type=F32)


def _dot_nt(a, b):
    return lax.dot_general(a.astype(BF16), b.astype(BF16), (((1,), (1,)), ((), ())),
                           preferred_element_type=F32)


def _dot_tn(a, b):
    return lax.dot_general(a.astype(BF16), b.astype(BF16), (((0,), (0,)), ((), ())),
                           preferred_element_type=F32)


def _sigmoid(v):
    return jax.nn.sigmoid(v)


def _rms(v):
    r = lax.rsqrt(jnp.mean(v * v, axis=-1, keepdims=True) + RMS_EPS)
    return v * r, r


def _rms_bwd(dy, xhat, r, gain):
    dxh = dy * gain
    dx = r * (dxh - xhat * jnp.mean(dxh * xhat, axis=-1, keepdims=True))
    return dx, dy * xhat


def _matmul(pairs, mode, out_dtype, *, tm, tn, tk=None, residual=None, name):
    b_offs = [pr[2] if len(pr) > 2 else 0 for pr in pairs]
    pairs = [pr[:2] for pr in pairs]
    a0, b0 = pairs[0]
    if mode == "nn":
        (m, kk), n = a0.shape, b0.shape[1]
    elif mode == "nt":
        (m, kk), n = a0.shape, b0.shape[0]
    else:
        (kk, m), n = a0.shape, b0.shape[1]
    assert mode == "nt" or not any(b_offs)
    tm, tn = min(tm, m), min(tn, n)
    tk = kk if tk is None else tk
    nk = kk // tk
    assert m % tm == 0 and n % tn == 0 and kk % tk == 0, (name, m, n, kk)
    n_p = len(pairs)
    if mode == "tn":
        a_spec = pl.BlockSpec((tk, tm), lambda i, j, k: (k, i))
    else:
        a_spec = pl.BlockSpec((tm, tk), lambda i, j, k: (i, k))
    if mode == "nt":
        b_specs = [pl.BlockSpec((tn, tk), lambda i, j, k, o=o: (j, o * nk + k)) for o in b_offs]
    else:
        b_specs = [pl.BlockSpec((tk, tn), lambda i, j, k: (k, j))] * n_p
    o_spec = pl.BlockSpec((tm, tn), lambda i, j, k: (i, j))
    dot = {"nn": _dot, "nt": _dot_nt, "tn": _dot_tn}[mode]
    has_res = residual is not None

    def body(*refs):
        a_refs, b_refs = refs[:n_p], refs[n_p:2 * n_p]
        pos = 2 * n_p
        res_ref = refs[pos] if has_res else None
        pos += int(has_res)
        o_ref = refs[pos]
        part = dot(a_refs[0][...], b_refs[0][...])
        for a_r, b_r in zip(a_refs[1:], b_refs[1:]):
            part += dot(a_r[...], b_r[...])

        def finish(total):
            if has_res:
                total = total + res_ref[...]
            o_ref[...] = total.astype(out_dtype)

        if nk == 1:
            finish(part)
        else:
            acc_ref = refs[pos + 1]
            k = pl.program_id(2)

            @pl.when(k == 0)
            def _():
                acc_ref[...] = part

            @pl.when(k > 0)
            def _():
                acc_ref[...] += part

            @pl.when(k == nk - 1)
            def _():
                finish(acc_ref[...])

    ins = [a for a, _ in pairs] + [b for _, b in pairs]
    in_specs = [a_spec] * n_p + b_specs
    if has_res:
        ins.append(residual)
        in_specs.append(o_spec)
    return pl.pallas_call(
        body, name=name, grid=(m // tm, n // tn, nk),
        in_specs=in_specs, out_specs=o_spec,
        out_shape=jax.ShapeDtypeStruct((m, n), out_dtype),
        scratch_shapes=[pltpu.VMEM((tm, tn), F32)] if nk > 1 else [],
        compiler_params=_params(("parallel", "parallel", "arbitrary")),
    )(*ins)


def _rms_fwd(x, gain, *, name):
    rows, dm = x.shape
    tm = min(512, rows)

    def body(x_ref, g_ref, h_ref):
        xhat, _ = _rms(x_ref[...])
        h_ref[...] = (xhat * g_ref[...]).astype(BF16)

    return pl.pallas_call(
        body, name=name, grid=(rows // tm,),
        in_specs=[pl.BlockSpec((tm, dm), lambda i: (i, 0)), pl.BlockSpec((1, dm), lambda i: (0, 0))],
        out_specs=pl.BlockSpec((tm, dm), lambda i: (i, 0)),
        out_shape=jax.ShapeDtypeStruct((rows, dm), BF16),
        compiler_params=_params(("parallel",)),
    )(x, gain)


def _rms_bwd_rows(dh, x, gain, dres, *, name):
    rows, dm = x.shape
    tm = min(512, rows)
    has_res = dres is not None

    def body(*refs):
        dh_ref, x_ref, g_ref = refs[:3]
        res_ref = refs[3] if has_res else None
        dx_ref, dxb_ref, dg_ref = refs[3 + int(has_res):]
        xhat, r = _rms(x_ref[...])
        dx, dgr = _rms_bwd(dh_ref[...], xhat, r, g_ref[...])
        if has_res:
            dx = dx + res_ref[...]
        dx_ref[...] = dx
        dxb_ref[...] = dx.astype(BF16)

        @pl.when(pl.program_id(0) == 0)
        def _():
            dg_ref[...] = jnp.zeros_like(dg_ref)

        dg_ref[...] += jnp.sum(dgr, axis=0, keepdims=True)

    row = pl.BlockSpec((tm, dm), lambda i: (i, 0))
    vec = pl.BlockSpec((1, dm), lambda i: (0, 0))
    return pl.pallas_call(
        body, name=name, grid=(rows // tm,),
        in_specs=[row, row, vec] + ([row] if has_res else []),
        out_specs=[row, row, vec],
        out_shape=[jax.ShapeDtypeStruct((rows, dm), F32), jax.ShapeDtypeStruct((rows, dm), BF16),
                   jax.ShapeDtypeStruct((1, dm), F32)],
        compiler_params=_params(("arbitrary",)),
    )(*([dh, x, gain] + ([dres] if has_res else [])))


def _loss_grad(y, tgt, *, name):
    rows, dm = y.shape
    tm = min(512, rows)
    steps = rows // tm

    def body(y_ref, t_ref, dy_ref, dyb_ref, loss_ref, acc_ref):
        i = pl.program_id(0)
        diff = y_ref[...] - t_ref[...]
        dy = diff * (1.0 / dm)
        dy_ref[...] = dy
        dyb_ref[...] = dy.astype(BF16)

        @pl.when(i == 0)
        def _():
            acc_ref[...] = jnp.zeros_like(acc_ref)

        acc_ref[...] += jnp.sum(diff * diff, axis=0, keepdims=True)

        @pl.when(i == steps - 1)
        def _():
            loss_ref[...] = jnp.full((1, HEAD), 0.5 / dm, F32) * jnp.sum(acc_ref[...])

    row = pl.BlockSpec((tm, dm), lambda i: (i, 0))
    return pl.pallas_call(
        body, name=name, grid=(steps,),
        in_specs=[row, row],
        out_specs=[row, row, pl.BlockSpec((1, HEAD), lambda i: (0, 0))],
        out_shape=[jax.ShapeDtypeStruct((rows, dm), F32), jax.ShapeDtypeStruct((rows, dm), BF16),
                   jax.ShapeDtypeStruct((1, HEAD), F32)],
        scratch_shapes=[pltpu.VMEM((1, dm), F32)],
        compiler_params=_params(("arbitrary",)),
    )(y, tgt)


def _gla_chunk_terms(q_raw, f_logit, lb, tri_f, rev):
    del rev
    sig = _sigmoid(f_logit)
    forget = lb + (1.0 - lb) * sig
    k = 1.0 - forget
    logf = jnp.log(forget)
    b = jnp.dot(tri_f, logf, precision=lax.Precision.HIGHEST, preferred_element_type=F32)
    bl = jnp.sum(logf, axis=0, keepdims=True)
    qs = _sigmoid(q_raw)
    q = q_raw * qs * HG_SCALE
    eb = jnp.exp(b)
    qt = q * eb
    kt = k * jnp.exp(-b)
    kh = k * jnp.exp(bl - b)
    return sig, forget, k, b, bl, qs, eb, qt, kt, kh


def _tri_mask(rev):
    row = lax.broadcasted_iota(jnp.int32, (CHUNK, CHUNK), 0)
    col = lax.broadcasted_iota(jnp.int32, (CHUNK, CHUNK), 1)
    return (row <= col) if rev else (row >= col)


def _gla_fwd(proj, lb, *, f_off, rev, name):
    rows = proj.shape[0]
    tb = min(256, rows)
    nb, ncb = rows // tb, tb // CHUNK

    def tmap(n):
        return nb - 1 - n if rev else n

    def body(q_ref, f_ref, v_ref, lb_ref, o_ref, st_ref, s_ref):
        @pl.when(pl.program_id(1) == 0)
        def _():
            s_ref[...] = jnp.zeros_like(s_ref)

        tri = _tri_mask(rev)
        tri_f = tri.astype(F32)
        lbv = lb_ref[...]
        for cc in range(ncb):
            c = ncb - 1 - cc if rev else cc
            sl = pl.ds(c * CHUNK, CHUNK)
            v = v_ref[sl, :]
            _, _, _, _, bl, _, _, qt, kt, kh = _gla_chunk_terms(q_ref[sl, :], f_ref[sl, :], lbv, tri_f, rev)
            s_t = s_ref[...]
            st_ref[c] = s_t
            a = jnp.where(tri, _dot_nt(qt, kt), 0.0)
            o_ref[sl, :] = _dot(a, v) + _dot_nt(qt, s_t)
            s_ref[...] = jnp.exp(bl) * s_t + _dot_tn(v, kh)

    col = lambda off: pl.BlockSpec((tb, HEAD), lambda h, n: (tmap(n), off // HEAD + h))
    return pl.pallas_call(
        body, name=name, grid=(HG_HEADS, nb),
        in_specs=[col(C_HQ), col(f_off), col(C_HI), pl.BlockSpec((1, HEAD), lambda h, n: (0, h))],
        out_specs=[pl.BlockSpec((tb, HEAD), lambda h, n: (tmap(n), h)),
                   pl.BlockSpec((None, ncb, HEAD, HEAD), lambda h, n: (h, tmap(n), 0, 0))],
        out_shape=[jax.ShapeDtypeStruct((rows, HG_HEADS * HEAD), F32),
                   jax.ShapeDtypeStruct((HG_HEADS, rows // CHUNK, HEAD, HEAD), F32)],
        scratch_shapes=[pltpu.VMEM((HEAD, HEAD), F32)],
        compiler_params=_params(("parallel", "arbitrary")),
    )(proj, proj, proj, lb)


def _gla_bwd(proj, lb, do, states, prev, *, f_off, rev, name):
    rows = proj.shape[0]
    tb = min(256, rows)
    nb, ncb = rows // tb, tb // CHUNK
    has_prev = prev is not None
    qv_dtype = BF16 if has_prev else F32

    def tmap(n):
        return n if rev else nb - 1 - n

    def body(*refs):
        q_ref, f_ref, v_ref, lb_ref, do_ref, st_ref = refs[:6]
        pq_ref, pv_ref = refs[6:8] if has_prev else (None, None)
        dq_ref, df_ref, dv_ref, dlb_ref, ds_ref = refs[6 + 2 * int(has_prev):]

        @pl.when(pl.program_id(1) == 0)
        def _():
            ds_ref[...] = jnp.zeros_like(ds_ref)
            dlb_ref[...] = jnp.zeros_like(dlb_ref)

        tri = _tri_mask(rev)
        tri_f = tri.astype(F32)
        tri_ft = _tri_mask(not rev).astype(F32)
        lbv = lb_ref[...]
        dlb = jnp.zeros((1, HEAD), F32)
        for cc in range(ncb):
            c = cc if rev else ncb - 1 - cc
            sl = pl.ds(c * CHUNK, CHUNK)
            q_raw, v, dout = q_ref[sl, :], v_ref[sl, :], do_ref[sl, :]
            sig, forget, k, b, bl, qs, eb, qt, kt, kh = _gla_chunk_terms(q_raw, f_ref[sl, :], lbv, tri_f, rev)
            s_t = st_ref[c]
            ds_t = ds_ref[...]
            ebl = jnp.exp(bl)
            a = jnp.where(tri, _dot_nt(qt, kt), 0.0)
            da = jnp.where(tri, _dot_nt(dout, v), 0.0)
            dv = _dot_tn(a, dout) + _dot_nt(kh, ds_t)
            dqt = _dot(da, kt) + _dot(dout, s_t)
            dkt = _dot_tn(da, qt)
            dkh = _dot(v, ds_t)
            ds_ref[...] = _dot_tn(dout, qt) + ds_t * ebl
            debl = jnp.sum(ds_t * s_t, axis=0, keepdims=True)
            dq = dqt * eb
            dk = dkt * jnp.exp(-b) + dkh * jnp.exp(bl - b)
            db = qt * dqt - kt * dkt - kh * dkh
            dbl = jnp.sum(kh * dkh, axis=0, keepdims=True) + ebl * debl
            dlogf = jnp.dot(tri_ft, db, precision=lax.Precision.HIGHEST, preferred_element_type=F32) + dbl
            dforget = dlogf / forget - dk
            df_ref[sl, :] = (dforget * (1.0 - lbv) * sig * (1.0 - sig)).astype(BF16)
            dlb = dlb + jnp.sum(dforget * (1.0 - sig), axis=0, keepdims=True)
            dqr = dq * (HG_SCALE * qs * (1.0 + q_raw * (1.0 - qs)))
            if has_prev:
                dqr = dqr + pq_ref[sl, :]
                dv = dv + pv_ref[sl, :]
            dq_ref[sl, :] = dqr.astype(qv_dtype)
            dv_ref[sl, :] = dv.astype(qv_dtype)
        dlb_ref[...] += dlb

    col = lambda off: pl.BlockSpec((tb, HEAD), lambda h, n: (tmap(n), off // HEAD + h))
    blk = pl.BlockSpec((tb, HEAD), lambda h, n: (tmap(n), h))
    vec = pl.BlockSpec((1, HEAD), lambda h, n: (0, h))
    wide = HG_HEADS * HEAD
    return pl.pallas_call(
        body, name=name, grid=(HG_HEADS, nb),
        in_specs=[col(C_HQ), col(f_off), col(C_HI), vec, blk,
                  pl.BlockSpec((None, ncb, HEAD, HEAD), lambda h, n: (h, tmap(n), 0, 0))]
                 + ([blk, blk] if has_prev else []),
        out_specs=[blk, blk, blk, vec],
        out_shape=[jax.ShapeDtypeStruct((rows, wide), qv_dtype), jax.ShapeDtypeStruct((rows, wide), BF16),
                   jax.ShapeDtypeStruct((rows, wide), qv_dtype), jax.ShapeDtypeStruct((1, wide), F32)],
        scratch_shapes=[pltpu.VMEM((HEAD, HEAD), F32)],
        compiler_params=_params(("parallel", "arbitrary")),
    )(*([proj, proj, proj, lb, do, states] + (list(prev) if has_prev else [])))


def _hg_out_fwd(o_fw, o_bw, proj, gain, *, name):
    rows = o_fw.shape[0]
    tm = min(512, rows)
    wide = HG_HEADS * HEAD

    def body(a_ref, b_ref, g_ref, gain_ref, o_ref):
        for h in range(HG_HEADS):
            sl = slice(h * HEAD, (h + 1) * HEAD)
            xhat, _ = _rms(a_ref[:, sl] + b_ref[:, sl])
            gate = g_ref[:, sl]
            o_ref[:, sl] = (xhat * gain_ref[...] * (gate * _sigmoid(gate))).astype(BF16)

    row = pl.BlockSpec((tm, wide), lambda i: (i, 0))
    return pl.pallas_call(
        body, name=name, grid=(rows // tm,),
        in_specs=[row, row, pl.BlockSpec((tm, wide), lambda i: (i, C_HG // wide)),
                  pl.BlockSpec((1, HEAD), lambda i: (0, 0))],
        out_specs=row, out_shape=jax.ShapeDtypeStruct((rows, wide), BF16),
        compiler_params=_params(("parallel",)),
    )(o_fw, o_bw, proj, gain)


def _hg_out_bwd(dout, o_fw, o_bw, proj, gain, *, name):
    rows = o_fw.shape[0]
    tm = min(512, rows)
    wide = HG_HEADS * HEAD

    def body(d_ref, a_ref, b_ref, g_ref, gain_ref, dgate_ref, do_ref, dgain_ref):
        @pl.when(pl.program_id(0) == 0)
        def _():
            dgain_ref[...] = jnp.zeros_like(dgain_ref)

        dgain = jnp.zeros((1, HEAD), F32)
        for h in range(HG_HEADS):
            sl = slice(h * HEAD, (h + 1) * HEAD)
            xhat, r = _rms(a_ref[:, sl] + b_ref[:, sl])
            gate, dy = g_ref[:, sl], d_ref[:, sl]
            gs = _sigmoid(gate)
            dgate_ref[:, sl] = (dy * xhat * gain_ref[...] * (gs * (1.0 + gate * (1.0 - gs)))).astype(BF16)
            dx, dgr = _rms_bwd(dy * (gate * gs), xhat, r, gain_ref[...])
            do_ref[:, sl] = dx
            dgain = dgain + jnp.sum(dgr, axis=0, keepdims=True)
        dgain_ref[...] += dgain

    row = pl.BlockSpec((tm, wide), lambda i: (i, 0))
    vec = pl.BlockSpec((1, HEAD), lambda i: (0, 0))
    return pl.pallas_call(
        body, name=name, grid=(rows // tm,),
        in_specs=[row, row, row, pl.BlockSpec((tm, wide), lambda i: (i, C_HG // wide)), vec],
        out_specs=[row, row, vec],
        out_shape=[jax.ShapeDtypeStruct((rows, wide), BF16), jax.ShapeDtypeStruct((rows, wide), F32),
                   jax.ShapeDtypeStruct((1, HEAD), F32)],
        compiler_params=_params(("arbitrary",)),
    )(dout, o_fw, o_bw, proj, gain)


def _strided_rows(ref, r, count, d):
    return ref[...] if d == 1 else ref[pl.ds(r, count, stride=d), :]


def _store_strided(ref, r, count, d, val):
    if d == 1:
        ref[...] = val
    else:
        ref[pl.ds(r, count, stride=d), :] = val


def _da_prep(proj, q_gain, k_gain, g, *, name):
    d = DA_GROUPS[g][0]
    rows = proj.shape[0]
    rb = min(512, rows)
    tn = rb // d

    def body(q_ref, k_ref, v_ref, qg_ref, kg_ref, qo_ref, ko_ref, vo_ref):
        for r in range(d):
            qhat, _ = _rms(_strided_rows(q_ref, r, tn, d))
            khat, _ = _rms(_strided_rows(k_ref, r, tn, d))
            qo_ref[r] = (qhat * qg_ref[...]).astype(BF16)
            ko_ref[r] = (khat * kg_ref[...]).astype(BF16)
            vo_ref[r] = _strided_rows(v_ref, r, tn, d).astype(BF16)

    col = lambda off: pl.BlockSpec((rb, HEAD), lambda h, i: (i, (off + g * DA_GH * HEAD) // HEAD + h))
    vec = pl.BlockSpec((1, HEAD), lambda h, i: (0, 0))
    out = pl.BlockSpec((None, d, tn, HEAD), lambda h, i: (h, 0, i, 0))
    shape = jax.ShapeDtypeStruct((DA_GH, d, rows // d, HEAD), BF16)
    return pl.pallas_call(
        body, name=name, grid=(DA_GH, rows // rb),
        in_specs=[col(C_DQ), col(C_DK), col(C_DV), vec, vec],
        out_specs=[out, out, out], out_shape=[shape, shape, shape],
        compiler_params=_params(("parallel", "parallel")),
    )(proj, proj, proj, q_gain, k_gain)


def _band_scores(q, k, jj, i, t, ld, slope, d, radius):
    s = _dot_nt(q, k) * ATT_SCALE
    row = lax.broadcasted_iota(jnp.int32, (t, t), 0)
    col = lax.broadcasted_iota(jnp.int32, (t, t), 1)
    rel = (jj - 1) * t + col - row
    kpos = (i + (jj - 1)) * t + col
    valid = (jnp.abs(rel) <= radius) & (kpos >= 0) & (kpos < ld)
    return jnp.where(valid, s - slope * (d * jnp.abs(rel)).astype(F32), NEG_INF)


def _slopes(g):
    idx = np.arange(g * DA_GH + 1, (g + 1) * DA_GH + 1)
    s = (2.0 ** (-8.0 * idx / (DA_GH * len(DA_GROUPS)))).astype(np.float32)
    return jnp.asarray(np.broadcast_to(s[:, None, None], (DA_GH, 8, HEAD)).copy())


def _band_fwd(qr, kr, vr, g, *, name):
    d, radius = DA_GROUPS[g]
    _, _, ld, _ = qr.shape
    t = min(HEAD, ld)
    nb = ld // t

    def body(q_ref, k0, k1, k2, v0, v1, v2, sl_ref, o_ref, lse_ref):
        i = pl.program_id(2)
        slope = sl_ref[0:1, 0:1]
        q = q_ref[...]
        s = [_band_scores(q, kk[...], jj, i, t, ld, slope, d, radius) for jj, kk in enumerate((k0, k1, k2))]
        m = jnp.maximum(jnp.maximum(jnp.max(s[0], axis=-1, keepdims=True), jnp.max(s[1], axis=-1, keepdims=True)),
                        jnp.max(s[2], axis=-1, keepdims=True))
        p = [jnp.exp(sj - m) for sj in s]
        l = sum(jnp.sum(pj, axis=-1, keepdims=True) for pj in p)
        o = sum(_dot(pj, vv[...]) for pj, vv in zip(p, (v0, v1, v2)))
        o_ref[...] = o / l
        lse_ref[...] = jnp.broadcast_to(m + jnp.log(l), (t, HEAD))

    own = pl.BlockSpec((None, None, t, HEAD), lambda h, r, i: (h, r, i, 0))
    prv = pl.BlockSpec((None, None, t, HEAD), lambda h, r, i: (h, r, jnp.maximum(i - 1, 0), 0))
    nxt = pl.BlockSpec((None, None, t, HEAD), lambda h, r, i: (h, r, jnp.minimum(i + 1, nb - 1), 0))
    shape = jax.ShapeDtypeStruct(qr.shape, F32)
    return pl.pallas_call(
        body, name=name, grid=(DA_GH, d, nb),
        in_specs=[own, prv, own, nxt, prv, own, nxt, pl.BlockSpec((None, 8, HEAD), lambda h, r, i: (h, 0, 0))],
        out_specs=[own, own], out_shape=[shape, shape],
        compiler_params=_params(("parallel", "parallel", "parallel")),
    )(qr, kr, kr, kr, vr, vr, vr, _slopes(g))


def _band_bwd(qr, kr, vr, dor, lser, deltar, g, *, name):
    d, radius = DA_GROUPS[g]
    _, _, ld, _ = qr.shape
    t = min(HEAD, ld)
    nb = ld // t

    def body(q_ref, k_ref, v_ref, do_ref, lse_ref, dl_ref, sl_ref, dq_ref, dk_ref, dv_ref):
        i = pl.program_id(2)

        @pl.when(i == 0)
        def _():
            dk_ref[...] = jnp.zeros_like(dk_ref)
            dv_ref[...] = jnp.zeros_like(dv_ref)

        slope = sl_ref[0:1, 0:1]
        q, dout = q_ref[...], do_ref[...]
        lse, delta = lse_ref[:, 0:1], dl_ref[:, 0:1]
        dq = jnp.zeros((t, HEAD), F32)
        for jj in range(3):
            kb = jnp.clip(i + (jj - 1), 0, nb - 1)
            rows = pl.ds(pl.multiple_of(kb * t, t), t)
            k, v = k_ref[rows, :], v_ref[rows, :]
            p = jnp.exp(_band_scores(q, k, jj, i, t, ld, slope, d, radius) - lse)
            ds = p * (_dot_nt(dout, v) - delta)
            dq = dq + _dot(ds, k)
            dk_ref[rows, :] += _dot_tn(ds, q) * ATT_SCALE
            dv_ref[rows, :] += _dot_tn(p, dout)
        dq_ref[...] = dq * ATT_SCALE

    own = pl.BlockSpec((None, None, t, HEAD), lambda h, r, i: (h, r, i, 0))
    seq = pl.BlockSpec((None, None, ld, HEAD), lambda h, r, i: (h, r, 0, 0))
    shape = jax.ShapeDtypeStruct(qr.shape, F32)
    return pl.pallas_call(
        body, name=name, grid=(DA_GH, d, nb),
        in_specs=[own, seq, seq, own, own, own, pl.BlockSpec((None, 8, HEAD), lambda h, r, i: (h, 0, 0))],
        out_specs=[own, seq, seq], out_shape=[shape, shape, shape],
        compiler_params=_params(("parallel", "parallel", "arbitrary")),
    )(qr, kr, vr, dor, lser, deltar, _slopes(g))


def _da_merge(outs, lses, rows, *, name):
    rb = min(512, rows)

    def body(*refs):
        o_refs, l_refs = refs[0:3], refs[3:6]
        o_ref, lse_ref = refs[6:8]
        on_refs, ln_refs = refs[8:11], refs[11:14]
        for g, (d, _) in enumerate(DA_GROUPS):
            tn = rb // d
            for r in range(d):
                _store_strided(on_refs[g], r, tn, d, o_refs[g][r])
                _store_strided(ln_refs[g], r, tn, d, l_refs[g][r])
        l0, l1, l2 = ln_refs[0][...], ln_refs[1][...], ln_refs[2][...]
        m = jnp.maximum(jnp.maximum(l0, l1), l2)
        e0, e1, e2 = jnp.exp(l0 - m), jnp.exp(l1 - m), jnp.exp(l2 - m)
        tot = e0 + e1 + e2
        o_ref[...] = (e0 * on_refs[0][...] + e1 * on_refs[1][...] + e2 * on_refs[2][...]) / tot
        lse_ref[...] = m + jnp.log(tot)

    res = lambda d: pl.BlockSpec((None, d, rb // d, HEAD), lambda h, i: (h, 0, i, 0))
    nat = pl.BlockSpec((rb, HEAD), lambda h, i: (i, h))
    shape = jax.ShapeDtypeStruct((rows, DA_GH * HEAD), F32)
    return pl.pallas_call(
        body, name=name, grid=(DA_GH, rows // rb),
        in_specs=[res(d) for d, _ in DA_GROUPS] * 2,
        out_specs=[nat, nat], out_shape=[shape, shape],
        scratch_shapes=[pltpu.VMEM((rb, HEAD), F32)] * 6,
        compiler_params=_params(("parallel", "parallel")),
    )(*outs, *lses)


def _da_bwd_prep(dout, o, lse, *, name):
    rows = o.shape[0]
    rb = min(512, rows)

    def body(d_ref, o_ref, l_ref, *outs):
        delta_ref = outs[9]
        delta_ref[...] = jnp.broadcast_to(jnp.sum(d_ref[...] * o_ref[...], axis=-1, keepdims=True), (rb, HEAD))
        for g, (d, _) in enumerate(DA_GROUPS):
            tn = rb // d
            for r in range(d):
                outs[3 * g][r] = _strided_rows(d_ref, r, tn, d).astype(BF16)
                outs[3 * g + 1][r] = _strided_rows(l_ref, r, tn, d)
                outs[3 * g + 2][r] = _strided_rows(delta_ref, r, tn, d)

    nat = pl.BlockSpec((rb, HEAD), lambda h, i: (i, h))
    out_specs, out_shape = [], []
    for d, _ in DA_GROUPS:
        for dt in (BF16, F32, F32):
            out_specs.append(pl.BlockSpec((None, d, rb // d, HEAD), lambda h, i: (h, 0, i, 0)))
            out_shape.append(jax.ShapeDtypeStruct((DA_GH, d, rows // d, HEAD), dt))
    return pl.pallas_call(
        body, name=name, grid=(DA_GH, rows // rb),
        in_specs=[nat, nat, nat], out_specs=out_specs, out_shape=out_shape,
        scratch_shapes=[pltpu.VMEM((rb, HEAD), F32)],
        compiler_params=_params(("parallel", "parallel")),
    )(dout, o, lse)


def _da_prep_bwd(dqr, dkr, dvr, proj, q_gain, k_gain, g, *, name):
    d = DA_GROUPS[g][0]
    rows = proj.shape[0]
    rb = min(512, rows)
    tn = rb // d

    def body(dq_ref, dk_ref, dv_ref, q_ref, k_ref, qg_ref, kg_ref, oq_ref, ok_ref, ov_ref, gq_ref, gk_ref, *nat_refs):
        @pl.when((pl.program_id(0) == 0) & (pl.program_id(1) == 0))
        def _():
            gq_ref[...] = jnp.zeros_like(gq_ref)
            gk_ref[...] = jnp.zeros_like(gk_ref)

        for j, src in enumerate((dq_ref, dk_ref, dv_ref)):
            for r in range(d):
                _store_strided(nat_refs[j], r, tn, d, src[r])
        ov_ref[...] = nat_refs[2][...].astype(BF16)
        for j, (x_ref, gn_ref, out_ref, acc_ref) in enumerate(((q_ref, qg_ref, oq_ref, gq_ref),
                                                                (k_ref, kg_ref, ok_ref, gk_ref))):
            xhat, r = _rms(x_ref[...])
            dx, dgr = _rms_bwd(nat_refs[j][...], xhat, r, gn_ref[...])
            out_ref[...] = dx.astype(BF16)
            acc_ref[...] += jnp.sum(dgr, axis=0, keepdims=True)

    res = pl.BlockSpec((None, d, tn, HEAD), lambda h, i: (h, 0, i, 0))
    col = lambda off: pl.BlockSpec((rb, HEAD), lambda h, i: (i, (off + g * DA_GH * HEAD) // HEAD + h))
    vec = pl.BlockSpec((1, HEAD), lambda h, i: (0, 0))
    nat = pl.BlockSpec((rb, HEAD), lambda h, i: (i, h))
    shape = jax.ShapeDtypeStruct((rows, DA_GH * HEAD), BF16)
    gshape = jax.ShapeDtypeStruct((1, HEAD), F32)
    return pl.pallas_call(
        body, name=name, grid=(DA_GH, rows // rb),
        in_specs=[res, res, res, col(C_DQ), col(C_DK), vec, vec],
        out_specs=[nat, nat, nat, vec, vec], out_shape=[shape, shape, shape, gshape, gshape],
        scratch_shapes=[pltpu.VMEM((rb, HEAD), F32)] * 3,
        compiler_params=_params(("arbitrary", "arbitrary")),
    )(dqr, dkr, dvr, proj, proj, q_gain, k_gain)


def _mem_fwd(proj, kv, q_gain, k_gain, *, name):
    rows = proj.shape[0]
    tm = min(512, rows)
    n_mem = kv.shape[0]

    def body(q_ref, k_ref, v_ref, qg_ref, kg_ref, o_ref):
        qhat, _ = _rms(q_ref[...])
        khat, _ = _rms(k_ref[...])
        s = _dot_nt(qhat * qg_ref[...], khat * kg_ref[...]) * ATT_SCALE
        p = jnp.exp(s - jnp.max(s, axis=-1, keepdims=True))
        p = p / jnp.sum(p, axis=-1, keepdims=True)
        o_ref[...] = _dot(p, v_ref[...]).astype(BF16)

    vec = pl.BlockSpec((1, HEAD), lambda i, h: (0, 0))
    return pl.pallas_call(
        body, name=name, grid=(rows // tm, MEM_HEADS),
        in_specs=[pl.BlockSpec((tm, HEAD), lambda i, h: (i, C_MQ // HEAD + h)),
                  pl.BlockSpec((n_mem, HEAD), lambda i, h: (0, h)),
                  pl.BlockSpec((n_mem, HEAD), lambda i, h: (0, MEM_HEADS + h)), vec, vec],
        out_specs=pl.BlockSpec((tm, HEAD), lambda i, h: (i, h)),
        out_shape=jax.ShapeDtypeStruct((rows, MEM_HEADS * HEAD), BF16),
        compiler_params=_params(("parallel", "parallel")),
    )(proj, kv, kv, q_gain, k_gain)


def _mem_bwd(dout, proj, kv, q_gain, k_gain, *, name):
    rows = proj.shape[0]
    tm = min(512, rows)
    steps = rows // tm
    n_mem = kv.shape[0]

    def body(d_ref, q_ref, k_ref, v_ref, qg_ref, kg_ref, dq_ref, dk_ref, dv_ref, gq_ref, gk_ref, dkn_ref):
        h, i = pl.program_id(0), pl.program_id(1)

        @pl.when((h == 0) & (i == 0))
        def _():
            gq_ref[...] = jnp.zeros_like(gq_ref)
            gk_ref[...] = jnp.zeros_like(gk_ref)

        @pl.when(i == 0)
        def _():
            dkn_ref[...] = jnp.zeros_like(dkn_ref)
            dv_ref[...] = jnp.zeros_like(dv_ref)

        qhat, rq = _rms(q_ref[...])
        khat, rk = _rms(k_ref[...])
        qn, kn = qhat * qg_ref[...], khat * kg_ref[...]
        s = _dot_nt(qn, kn) * ATT_SCALE
        p = jnp.exp(s - jnp.max(s, axis=-1, keepdims=True))
        p = p / jnp.sum(p, axis=-1, keepdims=True)
        dout = d_ref[...]
        dp = _dot_nt(dout, v_ref[...])
        ds = p * (dp - jnp.sum(p * dp, axis=-1, keepdims=True))
        dv_ref[...] += _dot_tn(p, dout)
        dkn_ref[...] += _dot_tn(ds, qn) * ATT_SCALE
        dq, dgr = _rms_bwd(_dot(ds, kn) * ATT_SCALE, qhat, rq, qg_ref[...])
        dq_ref[...] = dq.astype(BF16)
        gq_ref[...] += jnp.sum(dgr, axis=0, keepdims=True)

        @pl.when(i == steps - 1)
        def _():
            dk, dgk = _rms_bwd(dkn_ref[...], khat, rk, kg_ref[...])
            dk_ref[...] = dk
            gk_ref[...] += jnp.sum(dgk, axis=0, keepdims=True)

    vec = pl.BlockSpec((1, HEAD), lambda h, i: (0, 0))
    memh = pl.BlockSpec((n_mem, HEAD), lambda h, i: (0, h))
    tok = pl.BlockSpec((tm, HEAD), lambda h, i: (i, h))
    gshape = jax.ShapeDtypeStruct((1, HEAD), F32)
    return pl.pallas_call(
        body, name=name, grid=(MEM_HEADS, steps),
        in_specs=[tok, pl.BlockSpec((tm, HEAD), lambda h, i: (i, C_MQ // HEAD + h)), memh,
                  pl.BlockSpec((n_mem, HEAD), lambda h, i: (0, MEM_HEADS + h)), vec, vec],
        out_specs=[tok, memh, memh, vec, vec],
        out_shape=[jax.ShapeDtypeStruct((rows, MEM_HEADS * HEAD), BF16),
                   jax.ShapeDtypeStruct((n_mem, MEM_HEADS * HEAD), F32),
                   jax.ShapeDtypeStruct((n_mem, MEM_HEADS * HEAD), F32), gshape, gshape],
        scratch_shapes=[pltpu.VMEM((n_mem, HEAD), F32)],
        compiler_params=_params(("arbitrary", "arbitrary")),
    )(dout, proj, kv, kv, q_gain, k_gain)


def _branch_fwd(o_hg, o_da, o_mem, proj, wp_hg, wp_da, wp_mem, *, name):
    rows = o_hg.shape[0]
    tm = min(256, rows)

    def body(a_ref, b_ref, c_ref, ga_ref, gb_ref, gc_ref, wa_ref, wb_ref, wc_ref, o_ref):
        merged = _sigmoid(ga_ref[...]) * _dot(a_ref[...], wa_ref[...])
        merged += _sigmoid(gb_ref[...]) * _dot(b_ref[...], wb_ref[...])
        merged += _sigmoid(gc_ref[...]) * _dot(c_ref[...], wc_ref[...])
        o_ref[...] = merged.astype(BF16)

    row = lambda w: pl.BlockSpec((tm, w), lambda i: (i, 0))
    gate = lambda off: pl.BlockSpec((tm, D_MODEL), lambda i: (i, off // D_MODEL))
    full = lambda a: pl.BlockSpec(a.shape, lambda i: (0, 0))
    return pl.pallas_call(
        body, name=name, grid=(rows // tm,),
        in_specs=[row(o_hg.shape[1]), row(o_da.shape[1]), row(o_mem.shape[1]),
                  gate(C_GHG), gate(C_GDA), gate(C_GMEM), full(wp_hg), full(wp_da), full(wp_mem)],
        out_specs=row(D_MODEL), out_shape=jax.ShapeDtypeStruct((rows, D_MODEL), BF16),
        compiler_params=_params(("parallel",)),
    )(o_hg, o_da, o_mem, proj, proj, proj, wp_hg, wp_da, wp_mem)


def _branch_bwd(dm, o_hg, o_da, o_mem, proj, wp_hg, wp_da, wp_mem, *, name):
    rows = o_hg.shape[0]
    tm = min(256, rows)

    def body(dm_ref, a_ref, b_ref, c_ref, ga_ref, gb_ref, gc_ref, wa_ref, wb_ref, wc_ref, *outs):
        dmv = dm_ref[...]
        for j, (o_ref, g_ref, w_ref) in enumerate(((a_ref, ga_ref, wa_ref), (b_ref, gb_ref, wb_ref),
                                                   (c_ref, gc_ref, wc_ref))):
            z = _dot(o_ref[...], w_ref[...])
            gs = _sigmoid(g_ref[...])
            dz = (dmv * gs).astype(BF16)
            outs[3 * j][...] = (dmv * z * gs * (1.0 - gs)).astype(BF16)
            outs[3 * j + 1][...] = dz
            outs[3 * j + 2][...] = _dot_nt(dz, w_ref[...])

    row = lambda w: pl.BlockSpec((tm, w), lambda i: (i, 0))
    gate = lambda off: pl.BlockSpec((tm, D_MODEL), lambda i: (i, off // D_MODEL))
    full = lambda a: pl.BlockSpec(a.shape, lambda i: (0, 0))
    out_specs, out_shape = [], []
    for o in (o_hg, o_da, o_mem):
        out_specs += [row(D_MODEL), row(D_MODEL), row(o.shape[1])]
        out_shape += [jax.ShapeDtypeStruct((rows, D_MODEL), BF16), jax.ShapeDtypeStruct((rows, D_MODEL), BF16),
                      jax.ShapeDtypeStruct((rows, o.shape[1]), F32)]
    return pl.pallas_call(
        body, name=name, grid=(rows // tm,),
        in_specs=[row(D_MODEL), row(o_hg.shape[1]), row(o_da.shape[1]), row(o_mem.shape[1]),
                  gate(C_GHG), gate(C_GDA), gate(C_GMEM), full(wp_hg), full(wp_da), full(wp_mem)],
        out_specs=out_specs, out_shape=out_shape,
        compiler_params=_params(("parallel",)),
    )(dm, o_hg, o_da, o_mem, proj, proj, proj, wp_hg, wp_da, wp_mem)


def _ffn_in(h2, w_ab, *, name):
    rows, dff = h2.shape[0], w_ab.shape[1] // 2
    tm, tn = min(512, rows), 256

    def body(h_ref, wa_ref, wb_ref, a_ref, b_ref, u_ref):
        a = _dot(h_ref[...], wa_ref[...])
        b = _dot(h_ref[...], wb_ref[...])
        a_ref[...] = a
        b_ref[...] = b
        u_ref[...] = (a * _sigmoid(a) * b).astype(BF16)

    out = pl.BlockSpec((tm, tn), lambda i, j: (i, j))
    return pl.pallas_call(
        body, name=name, grid=(rows // tm, dff // tn),
        in_specs=[pl.BlockSpec((tm, D_MODEL), lambda i, j: (i, 0)),
                  pl.BlockSpec((D_MODEL, tn), lambda i, j: (0, j)),
                  pl.BlockSpec((D_MODEL, tn), lambda i, j: (0, dff // tn + j))],
        out_specs=[out, out, out],
        out_shape=[jax.ShapeDtypeStruct((rows, dff), F32), jax.ShapeDtypeStruct((rows, dff), F32),
                   jax.ShapeDtypeStruct((rows, dff), BF16)],
        compiler_params=_params(("parallel", "parallel")),
    )(h2, w_ab, w_ab)


def _ffn_act_bwd(dy, w_out, a, b, *, name):
    rows, dff = a.shape
    tm, tn = min(512, rows), 256

    def body(dy_ref, w_ref, a_ref, b_ref, da_ref, db_ref):
        du = _dot_nt(dy_ref[...], w_ref[...])
        av, bv = a_ref[...], b_ref[...]
        sa = _sigmoid(av)
        da_ref[...] = (du * bv * sa * (1.0 + av * (1.0 - sa))).astype(BF16)
        db_ref[...] = (du * av * sa).astype(BF16)

    tile = pl.BlockSpec((tm, tn), lambda i, j: (i, j))
    return pl.pallas_call(
        body, name=name, grid=(rows // tm, dff // tn),
        in_specs=[pl.BlockSpec((tm, D_MODEL), lambda i, j: (i, 0)),
                  pl.BlockSpec((tn, D_MODEL), lambda i, j: (j, 0)), tile, tile],
        out_specs=[tile, tile],
        out_shape=[jax.ShapeDtypeStruct((rows, dff), BF16), jax.ShapeDtypeStruct((rows, dff), BF16)],
        compiler_params=_params(("parallel", "parallel")),
    )(dy, w_out, a, b)


def _lower_bound(lb_fw, lb_bw, *, name):
    def body(a_ref, b_ref, oa_ref, ob_ref):
        for src, dst in ((a_ref, oa_ref), (b_ref, ob_ref)):
            dst[...] = _sigmoid(src[0:1, :] - src[1:2, :])

    shape = jax.ShapeDtypeStruct((1, lb_fw.shape[1]), F32)
    return pl.pallas_call(body, name=name, out_shape=[shape, shape])(lb_fw, lb_bw)


def _local_step(x, mem, tgt, p, w):
    rows = x.shape[0]
    lb_fw, lb_bw = _lower_bound(p["lb_logits_fw"], p["lb_logits_bw"], name="lower_bound")

    h = _rms_fwd(x, p["norm_mix_gain"], name="norm_mix")
    proj = _matmul([(h, w["w_in"])], "nn", F32, tm=512, tn=512, name="proj_in")
    o_fw, st_fw = _gla_fwd(proj, lb_fw, f_off=C_FF, rev=False, name="gla_fwd_fw")
    o_bw, st_bw = _gla_fwd(proj, lb_bw, f_off=C_FB, rev=True, name="gla_fwd_bw")
    o_hg = _hg_out_fwd(o_fw, o_bw, proj, p["hg_norm_gain"], name="hg_out")

    qkv_r, outs, lses = [], [], []
    for g in range(len(DA_GROUPS)):
        qr, kr, vr = _da_prep(proj, p["da_q_gain"], p["da_k_gain"], g, name=f"da_prep{g}")
        og, lg = _band_fwd(qr, kr, vr, g, name=f"band_fwd{g}")
        qkv_r.append((qr, kr, vr))
        outs.append(og)
        lses.append(lg)
    o_da, lse_da = _da_merge(outs, lses, rows, name="da_merge")

    mem_n = _rms_fwd(mem, p["norm_mem_gain"], name="norm_mem")
    kv = _matmul([(mem_n, w["w_mem_kv"])], "nn", F32, tm=256, tn=512, name="mem_kv")
    o_mem = _mem_fwd(proj, kv, p["mem_q_gain"], p["mem_k_gain"], name="mem_attn")

    merged = _branch_fwd(o_hg, o_da, o_mem, proj, w["w_proj_hg"], w["w_proj_da"], w["w_proj_mem"],
                         name="branch_merge")
    x1 = _matmul([(merged, w["w_out"])], "nn", F32, tm=512, tn=512, residual=x, name="mix_out")
    h2 = _rms_fwd(x1, p["norm_ffn_gain"], name="norm_ffn")
    a, b, u = _ffn_in(h2, w["w_ffn_in"], name="ffn_in")
    y = _matmul([(u, w["w_ffn_out"])], "nn", F32, tm=512, tn=512, residual=x1, name="ffn_out")
    dy, dy_b, loss = _loss_grad(y, tgt, name="loss")

    gw, gs = {}, {}
    gw["w_ffn_out"] = _matmul([(u, dy_b)], "tn", F32, tm=256, tn=512, name="g_ffn_out")
    da, db = _ffn_act_bwd(dy_b, w["w_ffn_out"], a, b, name="ffn_act_bwd")
    gw["w_ffn_a"] = _matmul([(h2, da)], "tn", F32, tm=512, tn=256, name="g_ffn_a")
    gw["w_ffn_b"] = _matmul([(h2, db)], "tn", F32, tm=512, tn=256, name="g_ffn_b")
    dh2 = _matmul([(da, w["w_ffn_in"], 0), (db, w["w_ffn_in"], 1)], "nt", F32, tm=256, tn=512, name="d_h2")
    dx1, dx1_b, gs["norm_ffn_gain"] = _rms_bwd_rows(dh2, x1, p["norm_ffn_gain"], dy, name="norm_ffn_bwd")
    gw["w_out"] = _matmul([(merged, dx1_b)], "tn", F32, tm=512, tn=512, name="g_out")
    dmerged = _matmul([(dx1_b, w["w_out"])], "nt", F32, tm=512, tn=512, name="d_merged")
    (dg_hg, dz_hg, do_hg, dg_da, dz_da, do_da, dg_mem, dz_mem, do_mem) = _branch_bwd(
        dmerged, o_hg, o_da, o_mem, proj, w["w_proj_hg"], w["w_proj_da"], w["w_proj_mem"], name="branch_bwd")
    gw["w_proj_hg"] = _matmul([(o_hg, dz_hg)], "tn", F32, tm=512, tn=512, name="g_proj_hg")
    gw["w_proj_da"] = _matmul([(o_da, dz_da)], "tn", F32, tm=512, tn=512, name="g_proj_da")
    gw["w_proj_mem"] = _matmul([(o_mem, dz_mem)], "tn", F32, tm=512, tn=512, name="g_proj_mem")

    dq_mem, dk_mem, dv_mem, gs["mem_q_gain"], gs["mem_k_gain"] = _mem_bwd(
        do_mem, proj, kv, p["mem_q_gain"], p["mem_k_gain"], name="mem_attn_bwd")
    dkv = jnp.concatenate([dk_mem, dv_mem], axis=1).astype(BF16)
    gw["w_mem_kv"] = _matmul([(mem_n, dkv)], "tn", F32, tm=512, tn=512, name="g_mem_kv")
    dmem_n = _matmul([(dkv, w["w_mem_kv"])], "nt", F32, tm=256, tn=512, name="d_mem_n")
    _, _, gs["norm_mem_gain"] = _rms_bwd_rows(dmem_n, mem, p["norm_mem_gain"], None, name="norm_mem_bwd")

    prep = _da_bwd_prep(do_da, o_da, lse_da, name="da_bwd_prep")
    d_da, gq_parts, gk_parts = [], [], []
    for g in range(len(DA_GROUPS)):
        qr, kr, vr = qkv_r[g]
        dor, lser, deltar = prep[3 * g:3 * g + 3]
        dqr, dkr, dvr = _band_bwd(qr, kr, vr, dor, lser, deltar, g, name=f"band_bwd{g}")
        dq, dk, dv, gq, gk = _da_prep_bwd(dqr, dkr, dvr, proj, p["da_q_gain"], p["da_k_gain"], g,
                                          name=f"da_prep_bwd{g}")
        d_da.append((dq, dk, dv))
        gq_parts.append(gq)
        gk_parts.append(gk)

    dg_hgate, do_gla, gs["hg_norm_gain"] = _hg_out_bwd(do_hg, o_fw, o_bw, proj, p["hg_norm_gain"], name="hg_out_bwd")
    dq_f, dfl_fw, dv_f, dlb_fw = _gla_bwd(proj, lb_fw, do_gla, st_fw, None, f_off=C_FF, rev=False, name="gla_bwd_fw")
    dq_hg, dfl_bw, dv_hg, dlb_bw = _gla_bwd(proj, lb_bw, do_gla, st_bw, (dq_f, dv_f), f_off=C_FB, rev=True,
                                            name="gla_bwd_bw")

    dproj = jnp.concatenate(
        [dq_hg, dfl_fw, dfl_bw, dv_hg, dg_hgate]
        + [t[0] for t in d_da] + [t[1] for t in d_da] + [t[2] for t in d_da]
        + [dq_mem, dg_hg, dg_da, dg_mem], axis=1)
    gw["w_in"] = _matmul([(h, dproj)], "tn", F32, tm=512, tn=512, name="g_in")
    dh = _matmul([(dproj, w["w_in"])], "nt", F32, tm=512, tn=512, tk=1024, name="d_h")
    grad_x, _, gs["norm_mix_gain"] = _rms_bwd_rows(dh, x, p["norm_mix_gain"], dx1, name="norm_mix_bwd")

    small = _small_pack(gs, gq_parts, gk_parts, dlb_fw, dlb_bw, lb_fw, lb_bw, name="small_pack")
    return loss, grad_x, gw, small


def _small_pack(gs, gq_parts, gk_parts, dlb_fw, dlb_bw, lb_fw, lb_bw, *, name):
    def body(g_mix, g_mem, g_ffn, dfw, dbw, lfw, lbw, g_hg, q0, q1, q2, k0, k1, k2, g_mq, g_mk, o_ref):
        o_ref[0:1, :] = g_mix[...]
        o_ref[1:2, :] = g_mem[...]
        o_ref[2:3, :] = g_ffn[...]
        for base, d_ref, l_ref in ((3, dfw, lfw), (5, dbw, lbw)):
            lbv = l_ref[...]
            t = d_ref[...] * lbv * (1.0 - lbv)
            o_ref[base:base + 1, :] = t
            o_ref[base + 1:base + 2, :] = -t
        o_ref[7:8, :] = jnp.zeros((1, D_MODEL), F32)
        o_ref[7:8, 0:HEAD] = g_hg[...]
        o_ref[7:8, HEAD:2 * HEAD] = q0[...] + q1[...] + q2[...]
        o_ref[7:8, 2 * HEAD:3 * HEAD] = k0[...] + k1[...] + k2[...]
        o_ref[7:8, 3 * HEAD:4 * HEAD] = g_mq[...]
        o_ref[7:8, 4 * HEAD:5 * HEAD] = g_mk[...]

    return pl.pallas_call(body, name=name, out_shape=jax.ShapeDtypeStruct((8, D_MODEL), F32))(
        gs["norm_mix_gain"], gs["norm_mem_gain"], gs["norm_ffn_gain"], dlb_fw, dlb_bw, lb_fw, lb_bw,
        gs["hg_norm_gain"], *gq_parts, *gk_parts, gs["mem_q_gain"], gs["mem_k_gain"])


def _row_tile(rows, cols, n_arrays):
    budget = (16 * 1024 * 1024) // (2 * 4 * cols * n_arrays)
    tr = rows
    while tr > budget and tr % 2 == 0 and (tr // 2) % 16 == 0:
        tr //= 2
    return tr


def _cast_bf16(a, *, name):
    rows, cols = a.shape
    tr = _row_tile(rows, cols, 2)

    def body(a_ref, o_ref):
        o_ref[...] = a_ref[...].astype(BF16)

    spec = pl.BlockSpec((tr, cols), lambda i: (i, 0))
    return pl.pallas_call(body, name=name, grid=(rows // tr,), in_specs=[spec], out_specs=spec,
                          out_shape=jax.ShapeDtypeStruct((rows, cols), BF16),
                          compiler_params=_params(("parallel",)))(a)


def _add_halves(pairs, *, name):
    rows = pairs[0][0].shape[0]
    widths = [m.shape[1] for m, _ in pairs]
    tr = _row_tile(rows, sum(widths), 3)

    def body(*refs):
        o_ref = refs[-1]
        off = 0
        for j, wd in enumerate(widths):
            o_ref[:, off:off + wd] = refs[2 * j][...] + refs[2 * j + 1][...]
            off += wd

    in_specs, ins = [], []
    for (m, s), wd in zip(pairs, widths):
        in_specs += [pl.BlockSpec((tr, wd), lambda i: (i, 0))] * 2
        ins += [m, s]
    return pl.pallas_call(body, name=name, grid=(rows // tr,), in_specs=in_specs,
                          out_specs=pl.BlockSpec((tr, sum(widths)), lambda i: (i, 0)),
                          out_shape=jax.ShapeDtypeStruct((rows, sum(widths)), F32),
                          compiler_params=_params(("parallel",)))(*ins)


def _add_slots(rb, *, name):
    _, rows, cols = rb.shape
    tr = _row_tile(rows, cols, 5)

    def body(r0, r1, r2, r3, o_ref):
        o_ref[...] = ((r0[...] + r1[...]) + r2[...]) + r3[...]

    slot = lambda s: pl.BlockSpec((None, tr, cols), lambda i: (s, i, 0))
    return pl.pallas_call(body, name=name, grid=(rows // tr,), in_specs=[slot(s) for s in range(4)],
                          out_specs=pl.BlockSpec((tr, cols), lambda i: (i, 0)),
                          out_shape=jax.ShapeDtypeStruct((rows, cols), F32),
                          compiler_params=_params(("parallel",)))(rb, rb, rb, rb)


def _adamw(w, g, m, v, *, name):
    rows, cols = w.shape
    tr = _row_tile(rows, cols, 7) if rows % 16 == 0 else rows
    c1 = 1.0 - ADAM_B1 ** ADAM_STEP
    c2 = 1.0 - ADAM_B2 ** ADAM_STEP

    def body(w_ref, g_ref, m_ref, v_ref, d_ref, mo_ref, vo_ref):
        gv = g_ref[...]
        mn = ADAM_B1 * m_ref[...] + (1.0 - ADAM_B1) * gv
        vn = ADAM_B2 * v_ref[...] + (1.0 - ADAM_B2) * (gv * gv)
        mo_ref[...] = mn
        vo_ref[...] = vn
        d_ref[...] = -ADAM_LR * ((mn / c1) / (jnp.sqrt(vn / c2) + ADAM_EPS) + ADAM_WD * w_ref[...])

    spec = pl.BlockSpec((tr, cols), lambda i: (i, 0))
    shape = jax.ShapeDtypeStruct((rows, cols), F32)
    return pl.pallas_call(body, name=name, grid=(rows // tr,), in_specs=[spec] * 4, out_specs=[spec] * 3,
                          out_shape=[shape] * 3, compiler_params=_params(("parallel",)))(w, g, m, v)


W_SPECS = (
    ("w_in", 1024, IN_COLS, 1, IN_COLS // 4),
    ("w_mem_kv", 1024, 1024, 0, 256),
    ("w_proj_hg", 1024, 1024, 0, 256),
    ("w_proj_da", 512, 1024, 1, 256),
    ("w_proj_mem", 512, 1024, 1, 256),
    ("w_out", 1024, 1024, 0, 256),
    ("w_ffn_in", 1024, 2 * D_FF, 1, 2 * D_FF // 4),
    ("w_ffn_out", D_FF, 1024, 0, D_FF // 4),
)
CHIP_FLIPS = ((1, 0), (0, 1), (1, 1))
ANY = pl.BlockSpec(memory_space=pl.ANY)


def _place():
    x, y, c = lax.axis_index("x"), lax.axis_index("y"), lax.axis_index("c")
    return x, y, c, 2 * x + y


def _flip(v, f):
    return 1 - v if f else v


def _slab(ref, axis, idx, size):
    start = pl.multiple_of(idx * size, size)
    return ref.at[pl.ds(start, size), :] if axis == 0 else ref.at[:, pl.ds(start, size)]


def _half_spec(rows, cols, axis):
    return (0, rows // 2) if axis == 1 else (1, cols // 2)


def _gather_weights(shards):
    n = len(W_SPECS)

    def body(*refs):
        ins, outs = refs[:n], refs[n:2 * n]
        send_sems, recv_sems, local_sems = refs[2 * n:]
        x, y, c, p = _place()
        local, remote = [], []
        for wi, (_, _, _, axis, size) in enumerate(W_SPECS):
            mine = _slab(outs[wi], axis, p, size)
            local.append(pltpu.make_async_copy(ins[wi], mine, local_sems.at[wi]))
            for k, (fx, fy) in enumerate(CHIP_FLIPS):
                remote.append(pltpu.make_async_remote_copy(
                    src_ref=ins[wi], dst_ref=mine, send_sem=send_sems.at[3 * wi + k],
                    recv_sem=recv_sems.at[3 * wi + k], device_id=(_flip(x, fx), _flip(y, fy), c),
                    device_id_type=MESH))
        for cp in local + remote:
            cp.start()
        for cp in remote:
            cp.wait_recv()
        for cp in remote:
            cp.wait_send()
        for cp in local:
            cp.wait()

    return pl.pallas_call(
        body, name="gather_weights", in_specs=[ANY] * n, out_specs=[ANY] * n,
        out_shape=[jax.ShapeDtypeStruct((r, cc), BF16) for _, r, cc, _, _ in W_SPECS],
        scratch_shapes=[pltpu.SemaphoreType.DMA((3 * n,)), pltpu.SemaphoreType.DMA((3 * n,)),
                        pltpu.SemaphoreType.DMA((n,))],
    )(*shards)


def _sibling_exchange(grads):
    n = len(grads)
    halves = [_half_spec(g.shape[0], g.shape[1], axis) for g, axis in grads]

    def body(*refs):
        ins, mine_out, sib_out = refs[:n], refs[n:2 * n], refs[2 * n:3 * n]
        send_sems, recv_sems, local_sems = refs[3 * n:]
        x, y, c, _ = _place()
        local, remote = [], []
        for i, (haxis, hsize) in enumerate(halves):
            local.append(pltpu.make_async_copy(_slab(ins[i], haxis, c, hsize), mine_out[i], local_sems.at[i]))
            remote.append(pltpu.make_async_remote_copy(
                src_ref=_slab(ins[i], haxis, 1 - c, hsize), dst_ref=sib_out[i], send_sem=send_sems.at[i],
                recv_sem=recv_sems.at[i], device_id=(x, y, 1 - c), device_id_type=MESH))
        for cp in local + remote:
            cp.start()
        for cp in remote:
            cp.wait_recv()
        for cp in remote:
            cp.wait_send()
        for cp in local:
            cp.wait()

    shapes = []
    for (g, _), (haxis, hsize) in zip(grads, halves):
        shapes.append(jax.ShapeDtypeStruct((hsize, g.shape[1]) if haxis == 0 else (g.shape[0], hsize), F32))
    res = pl.pallas_call(
        body, name="grad_sibling_exchange", in_specs=[ANY] * n, out_specs=[ANY] * (2 * n),
        out_shape=shapes + shapes,
        scratch_shapes=[pltpu.SemaphoreType.DMA((n,))] * 3,
    )(*[g for g, _ in grads])
    return list(zip(res[:n], res[n:]))


def _chip_exchange(parts):
    n = len(parts)

    def body(*refs):
        ins, outs = refs[:n], refs[n:2 * n]
        send_sems, recv_sems, local_sems = refs[2 * n:]
        x, y, c, p = _place()
        local, remote = [], []
        for i, (_, axis, size) in enumerate(parts):
            local.append(pltpu.make_async_copy(_slab(ins[i], axis, p, size), outs[i].at[p], local_sems.at[i]))
            for k, (fx, fy) in enumerate(CHIP_FLIPS):
                px, py = _flip(x, fx), _flip(y, fy)
                remote.append(pltpu.make_async_remote_copy(
                    src_ref=_slab(ins[i], axis, 2 * px + py, size), dst_ref=outs[i].at[p],
                    send_sem=send_sems.at[3 * i + k], recv_sem=recv_sems.at[3 * i + k],
                    device_id=(px, py, c), device_id_type=MESH))
        for cp in local + remote:
            cp.start()
        for cp in remote:
            cp.wait_recv()
        for cp in remote:
            cp.wait_send()
        for cp in local:
            cp.wait()

    shapes = []
    for a, axis, size in parts:
        shapes.append(jax.ShapeDtypeStruct((4, size, a.shape[1]) if axis == 0 else (4, a.shape[0], size), F32))
    return pl.pallas_call(
        body, name="grad_chip_exchange", in_specs=[ANY] * n, out_specs=[ANY] * n, out_shape=shapes,
        scratch_shapes=[pltpu.SemaphoreType.DMA((3 * n,)), pltpu.SemaphoreType.DMA((3 * n,)),
                        pltpu.SemaphoreType.DMA((n,))],
    )(*[a for a, _, _ in parts])


def _sibling_share(sums):
    n = len(sums)

    def body(*refs):
        ins, outs = refs[:n], refs[n:2 * n]
        send_sems, recv_sems, local_sems = refs[2 * n:]
        x, y, c, _ = _place()
        local, remote = [], []
        for i, (s, haxis) in enumerate(sums):
            place = _slab(outs[i], haxis, c, s.shape[haxis])
            local.append(pltpu.make_async_copy(ins[i], place, local_sems.at[i]))
            remote.append(pltpu.make_async_remote_copy(
                src_ref=ins[i], dst_ref=place, send_sem=send_sems.at[i], recv_sem=recv_sems.at[i],
                device_id=(x, y, 1 - c), device_id_type=MESH))
        for cp in local + remote:
            cp.start()
        for cp in remote:
            cp.wait_recv()
        for cp in remote:
            cp.wait_send()
        for cp in local:
            cp.wait()

    shapes = []
    for s, haxis in sums:
        r, cc = s.shape
        shapes.append(jax.ShapeDtypeStruct((2 * r, cc) if haxis == 0 else (r, 2 * cc), F32))
    return pl.pallas_call(
        body, name="grad_sibling_share", in_specs=[ANY] * n, out_specs=[ANY] * n, out_shape=shapes,
        scratch_shapes=[pltpu.SemaphoreType.DMA((n,))] * 3,
    )(*[s for s, _ in sums])


def _reduce_scatter(gw):
    grads = []
    for name, _, _, axis, _ in W_SPECS:
        if name == "w_ffn_in":
            grads += [(gw["w_ffn_a"], axis), (gw["w_ffn_b"], axis)]
        else:
            grads.append((gw[name], axis))
    halves = _sibling_exchange(grads)
    parts, j = [], 0
    for name, _, _, axis, size in W_SPECS:
        take = 2 if name == "w_ffn_in" else 1
        parts.append((_add_halves(halves[j:j + take], name=f"half_sum_{name}"), axis, size))
        j += take
    slots = _chip_exchange(parts)
    sums = []
    for (name, rows, cols, axis, _), rb in zip(W_SPECS, slots):
        sums.append((_add_slots(rb, name=f"chip_sum_{name}"), _half_spec(rows, cols, axis)[0]))
    return dict(zip([s[0] for s in W_SPECS], _sibling_share(sums)))


def _small_allreduce(sv):
    def body(sv_ref, o_ref, slots_ref, send_sems, recv_sems):
        x, y, c, _ = _place()
        me = 4 * x + 2 * y + c
        slots_ref[me] = sv_ref[...]
        copies = []
        for k in range(1, 8):
            fx, fy, fc = (k >> 2) & 1, (k >> 1) & 1, k & 1
            copies.append(pltpu.make_async_remote_copy(
                src_ref=sv_ref, dst_ref=slots_ref.at[me], send_sem=send_sems.at[k - 1],
                recv_sem=recv_sems.at[k - 1], device_id=(_flip(x, fx), _flip(y, fy), _flip(c, fc)),
                device_id_type=MESH))
        for cp in copies:
            cp.start()
        for cp in copies:
            cp.wait_recv()
        for cp in copies:
            cp.wait_send()
        total = slots_ref[0]
        for s in range(1, 8):
            total = total + slots_ref[s]
        o_ref[...] = total

    vm = pl.BlockSpec(memory_space=pltpu.VMEM)
    return pl.pallas_call(
        body, name="small_allreduce", in_specs=[vm], out_specs=vm,
        out_shape=jax.ShapeDtypeStruct(sv.shape, F32),
        scratch_shapes=[pltpu.VMEM((8,) + sv.shape, F32), pltpu.SemaphoreType.DMA((7,)),
                        pltpu.SemaphoreType.DMA((7,))],
    )(sv)


SMALL_ROWS = (("norm_mix_gain", 0), ("norm_mem_gain", 1), ("norm_ffn_gain", 2))
SMALL_LB = (("lb_logits_fw", 3), ("lb_logits_bw", 5))
SMALL_HEAD = ("hg_norm_gain", "da_q_gain", "da_k_gain", "mem_q_gain", "mem_k_gain")


def _pack_small(d):
    last = jnp.concatenate([d[n] for n in SMALL_HEAD] + [jnp.zeros((1, D_MODEL - HEAD * len(SMALL_HEAD)), F32)], axis=1)
    return jnp.concatenate([d["norm_mix_gain"], d["norm_mem_gain"], d["norm_ffn_gain"],
                            d["lb_logits_fw"], d["lb_logits_bw"], last], axis=0)


def _unpack_small(a):
    out = {n: a[r:r + 1] for n, r in SMALL_ROWS}
    out.update({n: a[r:r + 2] for n, r in SMALL_LB})
    out.update({n: a[7:8, j * HEAD:(j + 1) * HEAD] for j, n in enumerate(SMALL_HEAD)})
    return out


PARAM_ORDER = ("norm_mix_gain", "norm_mem_gain", "w_in", "lb_logits_fw", "lb_logits_bw", "hg_norm_gain",
               "da_q_gain", "da_k_gain", "w_mem_kv", "mem_q_gain", "mem_k_gain", "w_proj_hg", "w_proj_da",
               "w_proj_mem", "w_out", "norm_ffn_gain", "w_ffn_in", "w_ffn_out")


def kernel(x, mem, norm_mix_gain, norm_mem_gain, w_in, lb_logits_fw, lb_logits_bw, hg_norm_gain, da_q_gain, da_k_gain, w_mem_kv, mem_q_gain, mem_k_gain, w_proj_hg, w_proj_da, w_proj_mem, w_out, norm_ffn_gain, w_ffn_in, w_ffn_out, loss_target, m_norm_mix_gain, m_norm_mem_gain, m_w_in, m_lb_logits_fw, m_lb_logits_bw, m_hg_norm_gain, m_da_q_gain, m_da_k_gain, m_w_mem_kv, m_mem_q_gain, m_mem_k_gain, m_w_proj_hg, m_w_proj_da, m_w_proj_mem, m_w_out, m_norm_ffn_gain, m_w_ffn_in, m_w_ffn_out, v_norm_mix_gain, v_norm_mem_gain, v_w_in, v_lb_logits_fw, v_lb_logits_bw, v_hg_norm_gain, v_da_q_gain, v_da_k_gain, v_w_mem_kv, v_mem_q_gain, v_mem_k_gain, v_w_proj_hg, v_w_proj_da, v_w_proj_mem, v_w_out, v_norm_ffn_gain, v_w_ffn_in, v_w_ffn_out):
    args = dict(locals())
    mats = tuple(s[0] for s in W_SPECS)
    flat = lambda a: a.reshape(a.shape[-2:])
    w = {n: flat(args[n]) for n in mats}
    m = {n: flat(args["m_" + n]) for n in mats}
    v = {n: flat(args["v_" + n]) for n in mats}
    small = {n: args[n] for n in PARAM_ORDER if n not in mats}

    full = _gather_weights([_cast_bf16(w[n], name=f"cast_{n}") for n in mats])
    loss, grad_x, gw, small_grads = _local_step(x[0], mem[0], loss_target[0], small, dict(zip(mats, full)))
    grads = _reduce_scatter(gw)
    small_sum = _small_allreduce(small_grads)

    delta, new_m, new_v = {}, {}, {}
    for n in mats:
        delta[n], new_m[n], new_v[n] = _adamw(w[n], grads[n], m[n], v[n], name=f"adamw_{n}")
    packed = _adamw(_pack_small(small), small_sum,
                    _pack_small({n: args["m_" + n] for n in small}),
                    _pack_small({n: args["v_" + n] for n in small}), name="adamw_small")
    grads.update(_unpack_small(small_sum))
    for dst, src in zip((delta, new_m, new_v), packed):
        dst.update(_unpack_small(src))

    def shaped(d, n):
        return d[n].reshape(args[n].shape)

    loss_sum = lax.psum(loss[0, 0], ("x", "y", "c"))
    return (loss_sum, grad_x[None], *[shaped(grads, n) for n in PARAM_ORDER], *[shaped(delta, n) for n in PARAM_ORDER],
            *[shaped(new_m, n) for n in PARAM_ORDER], *[shaped(new_v, n) for n in PARAM_ORDER])
```

```python
import functools
import math

import numpy as np
import jax
import jax.numpy as jnp
from jax import lax
from jax.experimental import pallas as pl
from jax.experimental.pallas import tpu as pltpu

F32 = jnp.float32
BF16 = jnp.bfloat16
MESH = pl.DeviceIdType.MESH

D_MODEL = 1024
HEAD = 128
HG_HEADS = 8
DA_GROUPS = ((1, 64), (4, 64), (16, 64))
DA_GH = 4
MEM_HEADS = 4
N_MEM = 256
D_FF = 2816
CHUNK = 64
RMS_EPS = 1e-6
NEG_INF = -1e30
HG_SCALE = HEAD ** -0.5
ATT_SCALE = HEAD ** -0.5
VMEM_LIMIT_V7X = 48 * 1024 * 1024

C_HQ, C_FF, C_FB, C_HI, C_HG = 0, 1024, 2048, 3072, 4096
C_DQ, C_DK, C_DV, C_MQ = 5120, 6656, 8192, 9728
C_GHG, C_GDA, C_GMEM = 10240, 11264, 12288
IN_COLS = 13312

ADAM_LR, ADAM_B1, ADAM_B2, ADAM_EPS, ADAM_WD, ADAM_STEP = 0.001, 0.9, 0.999, 1e-08, 0.01, 10


def _params(sem, vmem=VMEM_LIMIT_V7X):
    return pltpu.CompilerParams(dimension_semantics=sem, vmem_limit_bytes=vmem)


def _dot(a, b):
    return jnp.dot(a.astype(BF16), b.astype(BF16), preferred_element_type=F32)


def _dot_nt(a, b):
    return lax.dot_general(a.astype(BF16), b.astype(BF16), (((1,), (1,)), ((), ())),
                           preferred_element_type=F32)


def _dot_tn(a, b):
    return lax.dot_general(a.astype(BF16), b.astype(BF16), (((0,), (0,)), ((), ())),
                           preferred_element_type=F32)


def _sigmoid(v):
    return jax.nn.sigmoid(v)


def _rms(v):
    r = lax.rsqrt(jnp.mean(v * v, axis=-1, keepdims=True) + RMS_EPS)
    return v * r, r


def _rms_bwd(dy, xhat, r, gain):
    dxh = dy * gain
    dx = r * (dxh - xhat * jnp.mean(dxh * xhat, axis=-1, keepdims=True))
    return dx, dy * xhat


def _matmul(pairs, mode, out_dtype, *, tm, tn, tk=None, residual=None, name):
    b_offs = [pr[2] if len(pr) > 2 else 0 for pr in pairs]
    pairs = [pr[:2] for pr in pairs]
    a0, b0 = pairs[0]
    if mode == "nn":
        (m, kk), n = a0.shape, b0.shape[1]
    elif mode == "nt":
        (m, kk), n = a0.shape, b0.shape[0]
    else:
        (kk, m), n = a0.shape, b0.shape[1]
    assert mode == "nt" or not any(b_offs)
    tm, tn = min(tm, m), min(tn, n)
    tk = kk if tk is None else tk
    nk = kk // tk
    assert m % tm == 0 and n % tn == 0 and kk % tk == 0, (name, m, n, kk)
    n_p = len(pairs)
    if mode == "tn":
        a_spec = pl.BlockSpec((tk, tm), lambda i, j, k: (k, i))
    else:
        a_spec = pl.BlockSpec((tm, tk), lambda i, j, k: (i, k))
    if mode == "nt":
        b_specs = [pl.BlockSpec((tn, tk), lambda i, j, k, o=o: (j, o * nk + k)) for o in b_offs]
    else:
        b_specs = [pl.BlockSpec((tk, tn), lambda i, j, k: (k, j))] * n_p
    o_spec = pl.BlockSpec((tm, tn), lambda i, j, k: (i, j))
    dot = {"nn": _dot, "nt": _dot_nt, "tn": _dot_tn}[mode]
    has_res = residual is not None

    def body(*refs):
        a_refs, b_refs = refs[:n_p], refs[n_p:2 * n_p]
        pos = 2 * n_p
        res_ref = refs[pos] if has_res else None
        pos += int(has_res)
        o_ref = refs[pos]
        part = dot(a_refs[0][...], b_refs[0][...])
        for a_r, b_r in zip(a_refs[1:], b_refs[1:]):
            part += dot(a_r[...], b_r[...])

        def finish(total):
            if has_res:
                total = total + res_ref[...]
            o_ref[...] = total.astype(out_dtype)

        if nk == 1:
            finish(part)
        else:
            acc_ref = refs[pos + 1]
            k = pl.program_id(2)

            @pl.when(k == 0)
            def _():
                acc_ref[...] = part

            @pl.when(k > 0)
            def _():
                acc_ref[...] += part

            @pl.when(k == nk - 1)
            def _():
                finish(acc_ref[...])

    ins = [a for a, _ in pairs] + [b for _, b in pairs]
    in_specs = [a_spec] * n_p + b_specs
    if has_res:
        ins.append(residual)
        in_specs.append(o_spec)
    return pl.pallas_call(
        body, name=name, grid=(m // tm, n // tn, nk),
        in_specs=in_specs, out_specs=o_spec,
        out_shape=jax.ShapeDtypeStruct((m, n), out_dtype),
        scratch_shapes=[pltpu.VMEM((tm, tn), F32)] if nk > 1 else [],
        compiler_params=_params(("parallel", "parallel", "arbitrary")),
    )(*ins)


def _rms_fwd(x, gain, *, name):
    rows, dm = x.shape
    tm = min(512, rows)

    def body(x_ref, g_ref, h_ref):
        xhat, _ = _rms(x_ref[...])
        h_ref[...] = (xhat * g_ref[...]).astype(BF16)

    return pl.pallas_call(
        body, name=name, grid=(rows // tm,),
        in_specs=[pl.BlockSpec((tm, dm), lambda i: (i, 0)), pl.BlockSpec((1, dm), lambda i: (0, 0))],
        out_specs=pl.BlockSpec((tm, dm), lambda i: (i, 0)),
        out_shape=jax.ShapeDtypeStruct((rows, dm), BF16),
        compiler_params=_params(("parallel",)),
    )(x, gain)


def _rms_bwd_rows(dh, x, gain, dres, *, name):
    rows, dm = x.shape
    tm = min(512, rows)
    has_res = dres is not None

    def body(*refs):
        dh_ref, x_ref, g_ref = refs[:3]
        res_ref = refs[3] if has_res else None
        dx_ref, dxb_ref, dg_ref = refs[3 + int(has_res):]
        xhat, r = _rms(x_ref[...])
        dx, dgr = _rms_bwd(dh_ref[...], xhat, r, g_ref[...])
        if has_res:
            dx = dx + res_ref[...]
        dx_ref[...] = dx
        dxb_ref[...] = dx.astype(BF16)

        @pl.when(pl.program_id(0) == 0)
        def _():
            dg_ref[...] = jnp.zeros_like(dg_ref)

        dg_ref[...] += jnp.sum(dgr, axis=0, keepdims=True)

    row = pl.BlockSpec((tm, dm), lambda i: (i, 0))
    vec = pl.BlockSpec((1, dm), lambda i: (0, 0))
    return pl.pallas_call(
        body, name=name, grid=(rows // tm,),
        in_specs=[row, row, vec] + ([row] if has_res else []),
        out_specs=[row, row, vec],
        out_shape=[jax.ShapeDtypeStruct((rows, dm), F32), jax.ShapeDtypeStruct((rows, dm), BF16),
                   jax.ShapeDtypeStruct((1, dm), F32)],
        compiler_params=_params(("arbitrary",)),
    )(*([dh, x, gain] + ([dres] if has_res else [])))


def _loss_grad(y, tgt, *, name):
    rows, dm = y.shape
    tm = min(512, rows)
    steps = rows // tm

    def body(y_ref, t_ref, dy_ref, dyb_ref, loss_ref, acc_ref):
        i = pl.program_id(0)
        diff = y_ref[...] - t_ref[...]
        dy = diff * (1.0 / dm)
        dy_ref[...] = dy
        dyb_ref[...] = dy.astype(BF16)

        @pl.when(i == 0)
        def _():
            acc_ref[...] = jnp.zeros_like(acc_ref)

        acc_ref[...] += jnp.sum(diff * diff, axis=0, keepdims=True)

        @pl.when(i == steps - 1)
        def _():
            loss_ref[...] = jnp.full((1, HEAD), 0.5 / dm, F32) * jnp.sum(acc_ref[...])

    row = pl.BlockSpec((tm, dm), lambda i: (i, 0))
    return pl.pallas_call(
        body, name=name, grid=(steps,),
        in_specs=[row, row],
        out_specs=[row, row, pl.BlockSpec((1, HEAD), lambda i: (0, 0))],
        out_shape=[jax.ShapeDtypeStruct((rows, dm), F32), jax.ShapeDtypeStruct((rows, dm), BF16),
                   jax.ShapeDtypeStruct((1, HEAD), F32)],
        scratch_shapes=[pltpu.VMEM((1, dm), F32)],
        compiler_params=_params(("arbitrary",)),
    )(y, tgt)


def _gla_chunk_terms(q_raw, f_logit, lb, tri_f, rev):
    del rev
    sig = _sigmoid(f_logit)
    forget = lb + (1.0 - lb) * sig
    k = 1.0 - forget
    logf = jnp.log(forget)
    b = jnp.dot(tri_f, logf, precision=lax.Precision.HIGHEST, preferred_element_type=F32)
    bl = jnp.sum(logf, axis=0, keepdims=True)
    qs = _sigmoid(q_raw)
    q = q_raw * qs * HG_SCALE
    eb = jnp.exp(b)
    qt = q * eb
    kt = k * jnp.exp(-b)
    kh = k * jnp.exp(bl - b)
    return sig, forget, k, b, bl, qs, eb, qt, kt, kh


def _tri_mask(rev):
    row = lax.broadcasted_iota(jnp.int32, (CHUNK, CHUNK), 0)
    col = lax.broadcasted_iota(jnp.int32, (CHUNK, CHUNK), 1)
    return (row <= col) if rev else (row >= col)


def _gla_fwd(proj, lb, *, f_off, rev, name):
    rows = proj.shape[0]
    tb = min(256, rows)
    nb, ncb = rows // tb, tb // CHUNK

    def tmap(n):
        return nb - 1 - n if rev else n

    def body(q_ref, f_ref, v_ref, lb_ref, o_ref, st_ref, s_ref):
        @pl.when(pl.program_id(1) == 0)
        def _():
            s_ref[...] = jnp.zeros_like(s_ref)

        tri = _tri_mask(rev)
        tri_f = tri.astype(F32)
        lbv = lb_ref[...]
        for cc in range(ncb):
            c = ncb - 1 - cc if rev else cc
            sl = pl.ds(c * CHUNK, CHUNK)
            v = v_ref[sl, :]
            _, _, _, _, bl, _, _, qt, kt, kh = _gla_chunk_terms(q_ref[sl, :], f_ref[sl, :], lbv, tri_f, rev)
            s_t = s_ref[...]
            st_ref[c] = s_t
            a = jnp.where(tri, _dot_nt(qt, kt), 0.0)
            o_ref[sl, :] = _dot(a, v) + _dot_nt(qt, s_t)
            s_ref[...] = jnp.exp(bl) * s_t + _dot_tn(v, kh)

    col = lambda off: pl.BlockSpec((tb, HEAD), lambda h, n: (tmap(n), off // HEAD + h))
    return pl.pallas_call(
        body, name=name, grid=(HG_HEADS, nb),
        in_specs=[col(C_HQ), col(f_off), col(C_HI), pl.BlockSpec((1, HEAD), lambda h, n: (0, h))],
        out_specs=[pl.BlockSpec((tb, HEAD), lambda h, n: (tmap(n), h)),
                   pl.BlockSpec((None, ncb, HEAD, HEAD), lambda h, n: (h, tmap(n), 0, 0))],
        out_shape=[jax.ShapeDtypeStruct((rows, HG_HEADS * HEAD), F32),
                   jax.ShapeDtypeStruct((HG_HEADS, rows // CHUNK, HEAD, HEAD), F32)],
        scratch_shapes=[pltpu.VMEM((HEAD, HEAD), F32)],
        compiler_params=_params(("parallel", "arbitrary")),
    )(proj, proj, proj, lb)


def _gla_bwd(proj, lb, do, states, prev, *, f_off, rev, name):
    rows = proj.shape[0]
    tb = min(256, rows)
    nb, ncb = rows // tb, tb // CHUNK
    has_prev = prev is not None
    qv_dtype = BF16 if has_prev else F32

    def tmap(n):
        return n if rev else nb - 1 - n

    def body(*refs):
        q_ref, f_ref, v_ref, lb_ref, do_ref, st_ref = refs[:6]
        pq_ref, pv_ref = refs[6:8] if has_prev else (None, None)
        dq_ref, df_ref, dv_ref, dlb_ref, ds_ref = refs[6 + 2 * int(has_prev):]

        @pl.when(pl.program_id(1) == 0)
        def _():
            ds_ref[...] = jnp.zeros_like(ds_ref)
            dlb_ref[...] = jnp.zeros_like(dlb_ref)

        tri = _tri_mask(rev)
        tri_f = tri.astype(F32)
        tri_ft = _tri_mask(not rev).astype(F32)
        lbv = lb_ref[...]
        dlb = jnp.zeros((1, HEAD), F32)
        for cc in range(ncb):
            c = cc if rev else ncb - 1 - cc
            sl = pl.ds(c * CHUNK, CHUNK)
            q_raw, v, dout = q_ref[sl, :], v_ref[sl, :], do_ref[sl, :]
            sig, forget, k, b, bl, qs, eb, qt, kt, kh = _gla_chunk_terms(q_raw, f_ref[sl, :], lbv, tri_f, rev)
            s_t = st_ref[c]
            ds_t = ds_ref[...]
            ebl = jnp.exp(bl)
            a = jnp.where(tri, _dot_nt(qt, kt), 0.0)
            da = jnp.where(tri, _dot_nt(dout, v), 0.0)
            dv = _dot_tn(a, dout) + _dot_nt(kh, ds_t)
            dqt = _dot(da, kt) + _dot(dout, s_t)
            dkt = _dot_tn(da, qt)
            dkh = _dot(v, ds_t)
            ds_ref[...] = _dot_tn(dout, qt) + ds_t * ebl
            debl = jnp.sum(ds_t * s_t, axis=0, keepdims=True)
            dq = dqt * eb
            dk = dkt * jnp.exp(-b) + dkh * jnp.exp(bl - b)
            db = qt * dqt - kt * dkt - kh * dkh
            dbl = jnp.sum(kh * dkh, axis=0, keepdims=True) + ebl * debl
            dlogf = jnp.dot(tri_ft, db, precision=lax.Precision.HIGHEST, preferred_element_type=F32) + dbl
            dforget = dlogf / forget - dk
            df_ref[sl, :] = (dforget * (1.0 - lbv) * sig * (1.0 - sig)).astype(BF16)
            dlb = dlb + jnp.sum(dforget * (1.0 - sig), axis=0, keepdims=True)
            dqr = dq * (HG_SCALE * qs * (1.0 + q_raw * (1.0 - qs)))
            if has_prev:
                dqr = dqr + pq_ref[sl, :]
                dv = dv + pv_ref[sl, :]
            dq_ref[sl, :] = dqr.astype(qv_dtype)
            dv_ref[sl, :] = dv.astype(qv_dtype)
        dlb_ref[...] += dlb

    col = lambda off: pl.BlockSpec((tb, HEAD), lambda h, n: (tmap(n), off // HEAD + h))
    blk = pl.BlockSpec((tb, HEAD), lambda h, n: (tmap(n), h))
    vec = pl.BlockSpec((1, HEAD), lambda h, n: (0, h))
    wide = HG_HEADS * HEAD
    return pl.pallas_call(
        body, name=name, grid=(HG_HEADS, nb),
        in_specs=[col(C_HQ), col(f_off), col(C_HI), vec, blk,
                  pl.BlockSpec((None, ncb, HEAD, HEAD), lambda h, n: (h, tmap(n), 0, 0))]
                 + ([blk, blk] if has_prev else []),
        out_specs=[blk, blk, blk, vec],
        out_shape=[jax.ShapeDtypeStruct((rows, wide), qv_dtype), jax.ShapeDtypeStruct((rows, wide), BF16),
                   jax.ShapeDtypeStruct((rows, wide), qv_dtype), jax.ShapeDtypeStruct((1, wide), F32)],
        scratch_shapes=[pltpu.VMEM((HEAD, HEAD), F32)],
        compiler_params=_params(("parallel", "arbitrary")),
    )(*([proj, proj, proj, lb, do, states] + (list(prev) if has_prev else [])))


def _hg_out_fwd(o_fw, o_bw, proj, gain, *, name):
    rows = o_fw.shape[0]
    tm = min(512, rows)
    wide = HG_HEADS * HEAD

    def body(a_ref, b_ref, g_ref, gain_ref, o_ref):
        for h in range(HG_HEADS):
            sl = slice(h * HEAD, (h + 1) * HEAD)
            xhat, _ = _rms(a_ref[:, sl] + b_ref[:, sl])
            gate = g_ref[:, sl]
            o_ref[:, sl] = (xhat * gain_ref[...] * (gate * _sigmoid(gate))).astype(BF16)

    row = pl.BlockSpec((tm, wide), lambda i: (i, 0))
    return pl.pallas_call(
        body, name=name, grid=(rows // tm,),
        in_specs=[row, row, pl.BlockSpec((tm, wide), lambda i: (i, C_HG // wide)),
                  pl.BlockSpec((1, HEAD), lambda i: (0, 0))],
        out_specs=row, out_shape=jax.ShapeDtypeStruct((rows, wide), BF16),
        compiler_params=_params(("parallel",)),
    )(o_fw, o_bw, proj, gain)


def _hg_out_bwd(dout, o_fw, o_bw, proj, gain, *, name):
    rows = o_fw.shape[0]
    tm = min(512, rows)
    wide = HG_HEADS * HEAD

    def body(d_ref, a_ref, b_ref, g_ref, gain_ref, dgate_ref, do_ref, dgain_ref):
        @pl.when(pl.program_id(0) == 0)
        def _():
            dgain_ref[...] = jnp.zeros_like(dgain_ref)

        dgain = jnp.zeros((1, HEAD), F32)
        for h in range(HG_HEADS):
            sl = slice(h * HEAD, (h + 1) * HEAD)
            xhat, r = _rms(a_ref[:, sl] + b_ref[:, sl])
            gate, dy = g_ref[:, sl], d_ref[:, sl]
            gs = _sigmoid(gate)
            dgate_ref[:, sl] = (dy * xhat * gain_ref[...] * (gs * (1.0 + gate * (1.0 - gs)))).astype(BF16)
            dx, dgr = _rms_bwd(dy * (gate * gs), xhat, r, gain_ref[...])
            do_ref[:, sl] = dx
            dgain = dgain + jnp.sum(dgr, axis=0, keepdims=True)
        dgain_ref[...] += dgain

    row = pl.BlockSpec((tm, wide), lambda i: (i, 0))
    vec = pl.BlockSpec((1, HEAD), lambda i: (0, 0))
    return pl.pallas_call(
        body, name=name, grid=(rows // tm,),
        in_specs=[row, row, row, pl.BlockSpec((tm, wide), lambda i: (i, C_HG // wide)), vec],
        out_specs=[row, row, vec],
        out_shape=[jax.ShapeDtypeStruct((rows, wide), BF16), jax.ShapeDtypeStruct((rows, wide), F32),
                   jax.ShapeDtypeStruct((1, HEAD), F32)],
        compiler_params=_params(("arbitrary",)),
    )(dout, o_fw, o_bw, proj, gain)


def _strided_rows(ref, r, count, d):
    return ref[...] if d == 1 else ref[pl.ds(r, count, stride=d), :]


def _store_strided(ref, r, count, d, val):
    if d == 1:
        ref[...] = val
    else:
        ref[pl.ds(r, count, stride=d), :] = val


def _da_prep(proj, q_gain, k_gain, g, *, name):
    d = DA_GROUPS[g][0]
    rows = proj.shape[0]
    rb = min(512, rows)
    tn = rb // d

    def body(q_ref, k_ref, v_ref, qg_ref, kg_ref, qo_ref, ko_ref, vo_ref):
        for r in range(d):
            qhat, _ = _rms(_strided_rows(q_ref, r, tn, d))
            khat, _ = _rms(_strided_rows(k_ref, r, tn, d))
            qo_ref[r] = (qhat * qg_ref[...]).astype(BF16)
            ko_ref[r] = (khat * kg_ref[...]).astype(BF16)
            vo_ref[r] = _strided_rows(v_ref, r, tn, d).astype(BF16)

    col = lambda off: pl.BlockSpec((rb, HEAD), lambda h, i: (i, (off + g * DA_GH * HEAD) // HEAD + h))
    vec = pl.BlockSpec((1, HEAD), lambda h, i: (0, 0))
    out = pl.BlockSpec((None, d, tn, HEAD), lambda h, i: (h, 0, i, 0))
    shape = jax.ShapeDtypeStruct((DA_GH, d, rows // d, HEAD), BF16)
    return pl.pallas_call(
        body, name=name, grid=(DA_GH, rows // rb),
        in_specs=[col(C_DQ), col(C_DK), col(C_DV), vec, vec],
        out_specs=[out, out, out], out_shape=[shape, shape, shape],
        compiler_params=_params(("parallel", "parallel")),
    )(proj, proj, proj, q_gain, k_gain)


def _band_scores(q, k, jj, i, t, ld, slope, d, radius):
    s = _dot_nt(q, k) * ATT_SCALE
    row = lax.broadcasted_iota(jnp.int32, (t, t), 0)
    col = lax.broadcasted_iota(jnp.int32, (t, t), 1)
    rel = (jj - 1) * t + col - row
    kpos = (i + (jj - 1)) * t + col
    valid = (jnp.abs(rel) <= radius) & (kpos >= 0) & (kpos < ld)
    return jnp.where(valid, s - slope * (d * jnp.abs(rel)).astype(F32), NEG_INF)


def _slopes(g):
    idx = np.arange(g * DA_GH + 1, (g + 1) * DA_GH + 1)
    s = (2.0 ** (-8.0 * idx / (DA_GH * len(DA_GROUPS)))).astype(np.float32)
    return jnp.asarray(np.broadcast_to(s[:, None, None], (DA_GH, 8, HEAD)).copy())


def _band_fwd(qr, kr, vr, g, *, name):
    d, radius = DA_GROUPS[g]
    _, _, ld, _ = qr.shape
    t = min(HEAD, ld)
    nb = ld // t

    def body(q_ref, k0, k1, k2, v0, v1, v2, sl_ref, o_ref, lse_ref):
        i = pl.program_id(2)
        slope = sl_ref[0:1, 0:1]
        q = q_ref[...]
        s = [_band_scores(q, kk[...], jj, i, t, ld, slope, d, radius) for jj, kk in enumerate((k0, k1, k2))]
        m = jnp.maximum(jnp.maximum(jnp.max(s[0], axis=-1, keepdims=True), jnp.max(s[1], axis=-1, keepdims=True)),
                        jnp.max(s[2], axis=-1, keepdims=True))
        p = [jnp.exp(sj - m) for sj in s]
        l = sum(jnp.sum(pj, axis=-1, keepdims=True) for pj in p)
        o = sum(_dot(pj, vv[...]) for pj, vv in zip(p, (v0, v1, v2)))
        o_ref[...] = o / l
        lse_ref[...] = jnp.broadcast_to(m + jnp.log(l), (t, HEAD))

    own = pl.BlockSpec((None, None, t, HEAD), lambda h, r, i: (h, r, i, 0))
    prv = pl.BlockSpec((None, None, t, HEAD), lambda h, r, i: (h, r, jnp.maximum(i - 1, 0), 0))
    nxt = pl.BlockSpec((None, None, t, HEAD), lambda h, r, i: (h, r, jnp.minimum(i + 1, nb - 1), 0))
    shape = jax.ShapeDtypeStruct(qr.shape, F32)
    return pl.pallas_call(
        body, name=name, grid=(DA_GH, d, nb),
        in_specs=[own, prv, own, nxt, prv, own, nxt, pl.BlockSpec((None, 8, HEAD), lambda h, r, i: (h, 0, 0))],
        out_specs=[own, own], out_shape=[shape, shape],
        compiler_params=_params(("parallel", "parallel", "parallel")),
    )(qr, kr, kr, kr, vr, vr, vr, _slopes(g))


def _band_bwd(qr, kr, vr, dor, lser, deltar, g, *, name):
    d, radius = DA_GROUPS[g]
    _, _, ld, _ = qr.shape
    t = min(HEAD, ld)
    nb = ld // t

    def body(q_ref, k_ref, v_ref, do_ref, lse_ref, dl_ref, sl_ref, dq_ref, dk_ref, dv_ref):
        i = pl.program_id(2)

        @pl.when(i == 0)
        def _():
            dk_ref[...] = jnp.zeros_like(dk_ref)
            dv_ref[...] = jnp.zeros_like(dv_ref)

        slope = sl_ref[0:1, 0:1]
        q, dout = q_ref[...], do_ref[...]
        lse, delta = lse_ref[:, 0:1], dl_ref[:, 0:1]
        dq = jnp.zeros((t, HEAD), F32)
        for jj in range(3):
            kb = jnp.clip(i + (jj - 1), 0, nb - 1)
            rows = pl.ds(pl.multiple_of(kb * t, t), t)
            k, v = k_ref[rows, :], v_ref[rows, :]
            p = jnp.exp(_band_scores(q, k, jj, i, t, ld, slope, d, radius) - lse)
            ds = p * (_dot_nt(dout, v) - delta)
            dq = dq + _dot(ds, k)
            dk_ref[rows, :] += _dot_tn(ds, q) * ATT_SCALE
            dv_ref[rows, :] += _dot_tn(p, dout)
        dq_ref[...] = dq * ATT_SCALE

    own = pl.BlockSpec((None, None, t, HEAD), lambda h, r, i: (h, r, i, 0))
    seq = pl.BlockSpec((None, None, ld, HEAD), lambda h, r, i: (h, r, 0, 0))
    shape = jax.ShapeDtypeStruct(qr.shape, F32)
    return pl.pallas_call(
        body, name=name, grid=(DA_GH, d, nb),
        in_specs=[own, seq, seq, own, own, own, pl.BlockSpec((None, 8, HEAD), lambda h, r, i: (h, 0, 0))],
        out_specs=[own, seq, seq], out_shape=[shape, shape, shape],
        compiler_params=_params(("parallel", "parallel", "arbitrary")),
    )(qr, kr, vr, dor, lser, deltar, _slopes(g))


def _da_merge(outs, lses, rows, *, name):
    rb = min(512, rows)

    def body(*refs):
        o_refs, l_refs = refs[0:3], refs[3:6]
        o_ref, lse_ref = refs[6:8]
        on_refs, ln_refs = refs[8:11], refs[11:14]
        for g, (d, _) in enumerate(DA_GROUPS):
            tn = rb // d
            for r in range(d):
                _store_strided(on_refs[g], r, tn, d, o_refs[g][r])
                _store_strided(ln_refs[g], r, tn, d, l_refs[g][r])
        l0, l1, l2 = ln_refs[0][...], ln_refs[1][...], ln_refs[2][...]
        m = jnp.maximum(jnp.maximum(l0, l1), l2)
        e0, e1, e2 = jnp.exp(l0 - m), jnp.exp(l1 - m), jnp.exp(l2 - m)
        tot = e0 + e1 + e2
        o_ref[...] = (e0 * on_refs[0][...] + e1 * on_refs[1][...] + e2 * on_refs[2][...]) / tot
        lse_ref[...] = m + jnp.log(tot)

    res = lambda d: pl.BlockSpec((None, d, rb // d, HEAD), lambda h, i: (h, 0, i, 0))
    nat = pl.BlockSpec((rb, HEAD), lambda h, i: (i, h))
    shape = jax.ShapeDtypeStruct((rows, DA_GH * HEAD), F32)
    return pl.pallas_call(
        body, name=name, grid=(DA_GH, rows // rb),
        in_specs=[res(d) for d, _ in DA_GROUPS] * 2,
        out_specs=[nat, nat], out_shape=[shape, shape],
        scratch_shapes=[pltpu.VMEM((rb, HEAD), F32)] * 6,
        compiler_params=_params(("parallel", "parallel")),
    )(*outs, *lses)


def _da_bwd_prep(dout, o, lse, *, name):
    rows = o.shape[0]
    rb = min(512, rows)

    def body(d_ref, o_ref, l_ref, *outs):
        delta_ref = outs[9]
        delta_ref[...] = jnp.broadcast_to(jnp.sum(d_ref[...] * o_ref[...], axis=-1, keepdims=True), (rb, HEAD))
        for g, (d, _) in enumerate(DA_GROUPS):
            tn = rb // d
            for r in range(d):
                outs[3 * g][r] = _strided_rows(d_ref, r, tn, d).astype(BF16)
                outs[3 * g + 1][r] = _strided_rows(l_ref, r, tn, d)
                outs[3 * g + 2][r] = _strided_rows(delta_ref, r, tn, d)

    nat = pl.BlockSpec((rb, HEAD), lambda h, i: (i, h))
    out_specs, out_shape = [], []
    for d, _ in DA_GROUPS:
        for dt in (BF16, F32, F32):
            out_specs.append(pl.BlockSpec((None, d, rb // d, HEAD), lambda h, i: (h, 0, i, 0)))
            out_shape.append(jax.ShapeDtypeStruct((DA_GH, d, rows // d, HEAD), dt))
    return pl.pallas_call(
        body, name=name, grid=(DA_GH, rows // rb),
        in_specs=[nat, nat, nat], out_specs=out_specs, out_shape=out_shape,
        scratch_shapes=[pltpu.VMEM((rb, HEAD), F32)],
        compiler_params=_params(("parallel", "parallel")),
    )(dout, o, lse)


def _da_prep_bwd(dqr, dkr, dvr, proj, q_gain, k_gain, g, *, name):
    d = DA_GROUPS[g][0]
    rows = proj.shape[0]
    rb = min(512, rows)
    tn = rb // d

    def body(dq_ref, dk_ref, dv_ref, q_ref, k_ref, qg_ref, kg_ref, oq_ref, ok_ref, ov_ref, gq_ref, gk_ref, *nat_refs):
        @pl.when((pl.program_id(0) == 0) & (pl.program_id(1) == 0))
        def _():
            gq_ref[...] = jnp.zeros_like(gq_ref)
            gk_ref[...] = jnp.zeros_like(gk_ref)

        for j, src in enumerate((dq_ref, dk_ref, dv_ref)):
            for r in range(d):
                _store_strided(nat_refs[j], r, tn, d, src[r])
        ov_ref[...] = nat_refs[2][...].astype(BF16)
        for j, (x_ref, gn_ref, out_ref, acc_ref) in enumerate(((q_ref, qg_ref, oq_ref, gq_ref),
                                                                (k_ref, kg_ref, ok_ref, gk_ref))):
            xhat, r = _rms(x_ref[...])
            dx, dgr = _rms_bwd(nat_refs[j][...], xhat, r, gn_ref[...])
            out_ref[...] = dx.astype(BF16)
            acc_ref[...] += jnp.sum(dgr, axis=0, keepdims=True)

    res = pl.BlockSpec((None, d, tn, HEAD), lambda h, i: (h, 0, i, 0))
    col = lambda off: pl.BlockSpec((rb, HEAD), lambda h, i: (i, (off + g * DA_GH * HEAD) // HEAD + h))
    vec = pl.BlockSpec((1, HEAD), lambda h, i: (0, 0))
    nat = pl.BlockSpec((rb, HEAD), lambda h, i: (i, h))
    shape = jax.ShapeDtypeStruct((rows, DA_GH * HEAD), BF16)
    gshape = jax.ShapeDtypeStruct((1, HEAD), F32)
    return pl.pallas_call(
        body, name=name, grid=(DA_GH, rows // rb),
        in_specs=[res, res, res, col(C_DQ), col(C_DK), vec, vec],
        out_specs=[nat, nat, nat, vec, vec], out_shape=[shape, shape, shape, gshape, gshape],
        scratch_shapes=[pltpu.VMEM((rb, HEAD), F32)] * 3,
        compiler_params=_params(("arbitrary", "arbitrary")),
    )(dqr, dkr, dvr, proj, proj, q_gain, k_gain)


def _mem_fwd(proj, kv, q_gain, k_gain, *, name):
    rows = proj.shape[0]
    tm = min(512, rows)
    n_mem = kv.shape[0]

    def body(q_ref, k_ref, v_ref, qg_ref, kg_ref, o_ref):
        qhat, _ = _rms(q_ref[...])
        khat, _ = _rms(k_ref[...])
        s = _dot_nt(qhat * qg_ref[...], khat * kg_ref[...]) * ATT_SCALE
        p = jnp.exp(s - jnp.max(s, axis=-1, keepdims=True))
        p = p / jnp.sum(p, axis=-1, keepdims=True)
        o_ref[...] = _dot(p, v_ref[...]).astype(BF16)

    vec = pl.BlockSpec((1, HEAD), lambda i, h: (0, 0))
    return pl.pallas_call(
        body, name=name, grid=(rows // tm, MEM_HEADS),
        in_specs=[pl.BlockSpec((tm, HEAD), lambda i, h: (i, C_MQ // HEAD + h)),
                  pl.BlockSpec((n_mem, HEAD), lambda i, h: (0, h)),
                  pl.BlockSpec((n_mem, HEAD), lambda i, h: (0, MEM_HEADS + h)), vec, vec],
        out_specs=pl.BlockSpec((tm, HEAD), lambda i, h: (i, h)),
        out_shape=jax.ShapeDtypeStruct((rows, MEM_HEADS * HEAD), BF16),
        compiler_params=_params(("parallel", "parallel")),
    )(proj, kv, kv, q_gain, k_gain)


def _mem_bwd(dout, proj, kv, q_gain, k_gain, *, name):
    rows = proj.shape[0]
    tm = min(512, rows)
    steps = rows // tm
    n_mem = kv.shape[0]

    def body(d_ref, q_ref, k_ref, v_ref, qg_ref, kg_ref, dq_ref, dk_ref, dv_ref, gq_ref, gk_ref, dkn_ref):
        h, i = pl.program_id(0), pl.program_id(1)

        @pl.when((h == 0) & (i == 0))
        def _():
            gq_ref[...] = jnp.zeros_like(gq_ref)
            gk_ref[...] = jnp.zeros_like(gk_ref)

        @pl.when(i == 0)
        def _():
            dkn_ref[...] = jnp.zeros_like(dkn_ref)
            dv_ref[...] = jnp.zeros_like(dv_ref)

        qhat, rq = _rms(q_ref[...])
        khat, rk = _rms(k_ref[...])
        qn, kn = qhat * qg_ref[...], khat * kg_ref[...]
        s = _dot_nt(qn, kn) * ATT_SCALE
        p = jnp.exp(s - jnp.max(s, axis=-1, keepdims=True))
        p = p / jnp.sum(p, axis=-1, keepdims=True)
        dout = d_ref[...]
        dp = _dot_nt(dout, v_ref[...])
        ds = p * (dp - jnp.sum(p * dp, axis=-1, keepdims=True))
        dv_ref[...] += _dot_tn(p, dout)
        dkn_ref[...] += _dot_tn(ds, qn) * ATT_SCALE
        dq, dgr = _rms_bwd(_dot(ds, kn) * ATT_SCALE, qhat, rq, qg_ref[...])
        dq_ref[...] = dq.astype(BF16)
        gq_ref[...] += jnp.sum(dgr, axis=0, keepdims=True)

        @pl.when(i == steps - 1)
        def _():
            dk, dgk = _rms_bwd(dkn_ref[...], khat, rk, kg_ref[...])
            dk_ref[...] = dk
            gk_ref[...] += jnp.sum(dgk, axis=0, keepdims=True)

    vec = pl.BlockSpec((1, HEAD), lambda h, i: (0, 0))
    memh = pl.BlockSpec((n_mem, HEAD), lambda h, i: (0, h))
    tok = pl.BlockSpec((tm, HEAD), lambda h, i: (i, h))
    gshape = jax.ShapeDtypeStruct((1, HEAD), F32)
    return pl.pallas_call(
        body, name=name, grid=(MEM_HEADS, steps),
        in_specs=[tok, pl.BlockSpec((tm, HEAD), lambda h, i: (i, C_MQ // HEAD + h)), memh,
                  pl.BlockSpec((n_mem, HEAD), lambda h, i: (0, MEM_HEADS + h)), vec, vec],
        out_specs=[tok, memh, memh, vec, vec],
        out_shape=[jax.ShapeDtypeStruct((rows, MEM_HEADS * HEAD), BF16),
                   jax.ShapeDtypeStruct((n_mem, MEM_HEADS * HEAD), F32),
                   jax.ShapeDtypeStruct((n_mem, MEM_HEADS * HEAD), F32), gshape, gshape],
        scratch_shapes=[pltpu.VMEM((n_mem, HEAD), F32)],
        compiler_params=_params(("arbitrary", "arbitrary")),
    )(dout, proj, kv, kv, q_gain, k_gain)


def _branch_fwd(o_hg, o_da, o_mem, proj, wp_hg, wp_da, wp_mem, *, name):
    rows = o_hg.shape[0]
    tm = min(256, rows)

    def body(a_ref, b_ref, c_ref, ga_ref, gb_ref, gc_ref, wa_ref, wb_ref, wc_ref, o_ref):
        merged = _sigmoid(ga_ref[...]) * _dot(a_ref[...], wa_ref[...])
        merged += _sigmoid(gb_ref[...]) * _dot(b_ref[...], wb_ref[...])
        merged += _sigmoid(gc_ref[...]) * _dot(c_ref[...], wc_ref[...])
        o_ref[...] = merged.astype(BF16)

    row = lambda w: pl.BlockSpec((tm, w), lambda i: (i, 0))
    gate = lambda off: pl.BlockSpec((tm, D_MODEL), lambda i: (i, off // D_MODEL))
    full = lambda a: pl.BlockSpec(a.shape, lambda i: (0, 0))
    return pl.pallas_call(
        body, name=name, grid=(rows // tm,),
        in_specs=[row(o_hg.shape[1]), row(o_da.shape[1]), row(o_mem.shape[1]),
                  gate(C_GHG), gate(C_GDA), gate(C_GMEM), full(wp_hg), full(wp_da), full(wp_mem)],
        out_specs=row(D_MODEL), out_shape=jax.ShapeDtypeStruct((rows, D_MODEL), BF16),
        compiler_params=_params(("parallel",)),
    )(o_hg, o_da, o_mem, proj, proj, proj, wp_hg, wp_da, wp_mem)


def _branch_bwd(dm, o_hg, o_da, o_mem, proj, wp_hg, wp_da, wp_mem, *, name):
    rows = o_hg.shape[0]
    tm = min(256, rows)

    def body(dm_ref, a_ref, b_ref, c_ref, ga_ref, gb_ref, gc_ref, wa_ref, wb_ref, wc_ref, *outs):
        dmv = dm_ref[...]
        for j, (o_ref, g_ref, w_ref) in enumerate(((a_ref, ga_ref, wa_ref), (b_ref, gb_ref, wb_ref),
                                                   (c_ref, gc_ref, wc_ref))):
            z = _dot(o_ref[...], w_ref[...])
            gs = _sigmoid(g_ref[...])
            dz = (dmv * gs).astype(BF16)
            outs[3 * j][...] = (dmv * z * gs * (1.0 - gs)).astype(BF16)
            outs[3 * j + 1][...] = dz
            outs[3 * j + 2][...] = _dot_nt(dz, w_ref[...])

    row = lambda w: pl.BlockSpec((tm, w), lambda i: (i, 0))
    gate = lambda off: pl.BlockSpec((tm, D_MODEL), lambda i: (i, off // D_MODEL))
    full = lambda a: pl.BlockSpec(a.shape, lambda i: (0, 0))
    out_specs, out_shape = [], []
    for o in (o_hg, o_da, o_mem):
        out_specs += [row(D_MODEL), row(D_MODEL), row(o.shape[1])]
        out_shape += [jax.ShapeDtypeStruct((rows, D_MODEL), BF16), jax.ShapeDtypeStruct((rows, D_MODEL), BF16),
                      jax.ShapeDtypeStruct((rows, o.shape[1]), F32)]
    return pl.pallas_call(
        body, name=name, grid=(rows // tm,),
        in_specs=[row(D_MODEL), row(o_hg.shape[1]), row(o_da.shape[1]), row(o_mem.shape[1]),
                  gate(C_GHG), gate(C_GDA), gate(C_GMEM), full(wp_hg), full(wp_da), full(wp_mem)],
        out_specs=out_specs, out_shape=out_shape,
        compiler_params=_params(("parallel",)),
    )(dm, o_hg, o_da, o_mem, proj, proj, proj, wp_hg, wp_da, wp_mem)


def _ffn_in(h2, w_ab, *, name):
    rows, dff = h2.shape[0], w_ab.shape[1] // 2
    tm, tn = min(512, rows), 256

    def body(h_ref, wa_ref, wb_ref, a_ref, b_ref, u_ref):
        a = _dot(h_ref[...], wa_ref[...])
        b = _dot(h_ref[...], wb_ref[...])
        a_ref[...] = a
        b_ref[...] = b
        u_ref[...] = (a * _sigmoid(a) * b).astype(BF16)

    out = pl.BlockSpec((tm, tn), lambda i, j: (i, j))
    return pl.pallas_call(
        body, name=name, grid=(rows // tm, dff // tn),
        in_specs=[pl.BlockSpec((tm, D_MODEL), lambda i, j: (i, 0)),
                  pl.BlockSpec((D_MODEL, tn), lambda i, j: (0, j)),
                  pl.BlockSpec((D_MODEL, tn), lambda i, j: (0, dff // tn + j))],
        out_specs=[out, out, out],
        out_shape=[jax.ShapeDtypeStruct((rows, dff), F32), jax.ShapeDtypeStruct((rows, dff), F32),
                   jax.ShapeDtypeStruct((rows, dff), BF16)],
        compiler_params=_params(("parallel", "parallel")),
    )(h2, w_ab, w_ab)


def _ffn_act_bwd(dy, w_out, a, b, *, name):
    rows, dff = a.shape
    tm, tn = min(512, rows), 256

    def body(dy_ref, w_ref, a_ref, b_ref, da_ref, db_ref):
        du = _dot_nt(dy_ref[...], w_ref[...])
        av, bv = a_ref[...], b_ref[...]
        sa = _sigmoid(av)
        da_ref[...] = (du * bv * sa * (1.0 + av * (1.0 - sa))).astype(BF16)
        db_ref[...] = (du * av * sa).astype(BF16)

    tile = pl.BlockSpec((tm, tn), lambda i, j: (i, j))
    return pl.pallas_call(
        body, name=name, grid=(rows // tm, dff // tn),
        in_specs=[pl.BlockSpec((tm, D_MODEL), lambda i, j: (i, 0)),
                  pl.BlockSpec((tn, D_MODEL), lambda i, j: (j, 0)), tile, tile],
        out_specs=[tile, tile],
        out_shape=[jax.ShapeDtypeStruct((rows, dff), BF16), jax.ShapeDtypeStruct((rows, dff), BF16)],
        compiler_params=_params(("parallel", "parallel")),
    )(dy, w_out, a, b)


def _lower_bound(lb_fw, lb_bw, *, name):
    def body(a_ref, b_ref, oa_ref, ob_ref):
        for src, dst in ((a_ref, oa_ref), (b_ref, ob_ref)):
            dst[...] = _sigmoid(src[0:1, :] - src[1:2, :])

    shape = jax.ShapeDtypeStruct((1, lb_fw.shape[1]), F32)
    return pl.pallas_call(body, name=name, out_shape=[shape, shape])(lb_fw, lb_bw)


def _local_step(x, mem, tgt, p, w):
    rows = x.shape[0]
    lb_fw, lb_bw = _lower_bound(p["lb_logits_fw"], p["lb_logits_bw"], name="lower_bound")

    h = _rms_fwd(x, p["norm_mix_gain"], name="norm_mix")
    proj = _matmul([(h, w["w_in"])], "nn", F32, tm=512, tn=512, name="proj_in")
    o_fw, st_fw = _gla_fwd(proj, lb_fw, f_off=C_FF, rev=False, name="gla_fwd_fw")
    o_bw, st_bw = _gla_fwd(proj, lb_bw, f_off=C_FB, rev=True, name="gla_fwd_bw")
    o_hg = _hg_out_fwd(o_fw, o_bw, proj, p["hg_norm_gain"], name="hg_out")

    qkv_r, outs, lses = [], [], []
    for g in range(len(DA_GROUPS)):
        qr, kr, vr = _da_prep(proj, p["da_q_gain"], p["da_k_gain"], g, name=f"da_prep{g}")
        og, lg = _band_fwd(qr, kr, vr, g, name=f"band_fwd{g}")
        qkv_r.append((qr, kr, vr))
        outs.append(og)
        lses.append(lg)
    o_da, lse_da = _da_merge(outs, lses, rows, name="da_merge")

    mem_n = _rms_fwd(mem, p["norm_mem_gain"], name="norm_mem")
    kv = _matmul([(mem_n, w["w_mem_kv"])], "nn", F32, tm=256, tn=512, name="mem_kv")
    o_mem = _mem_fwd(proj, kv, p["mem_q_gain"], p["mem_k_gain"], name="mem_attn")

    merged = _branch_fwd(o_hg, o_da, o_mem, proj, w["w_proj_hg"], w["w_proj_da"], w["w_proj_mem"],
                         name="branch_merge")
    x1 = _matmul([(merged, w["w_out"])], "nn", F32, tm=512, tn=512, residual=x, name="mix_out")
    h2 = _rms_fwd(x1, p["norm_ffn_gain"], name="norm_ffn")
    a, b, u = _ffn_in(h2, w["w_ffn_in"], name="ffn_in")
    y = _matmul([(u, w["w_ffn_out"])], "nn", F32, tm=512, tn=512, residual=x1, name="ffn_out")
    dy, dy_b, loss = _loss_grad(y, tgt, name="loss")

    gw, gs = {}, {}
    gw["w_ffn_out"] = _matmul([(u, dy_b)], "tn", F32, tm=256, tn=512, name="g_ffn_out")
    da, db = _ffn_act_bwd(dy_b, w["w_ffn_out"], a, b, name="ffn_act_bwd")
    gw["w_ffn_a"] = _matmul([(h2, da)], "tn", F32, tm=512, tn=256, name="g_ffn_a")
    gw["w_ffn_b"] = _matmul([(h2, db)], "tn", F32, tm=512, tn=256, name="g_ffn_b")
    dh2 = _matmul([(da, w["w_ffn_in"], 0), (db, w["w_ffn_in"], 1)], "nt", F32, tm=256, tn=512, name="d_h2")
    dx1, dx1_b, gs["norm_ffn_gain"] = _rms_bwd_rows(dh2, x1, p["norm_ffn_gain"], dy, name="norm_ffn_bwd")
    gw["w_out"] = _matmul([(merged, dx1_b)], "tn", F32, tm=512, tn=512, name="g_out")
    dmerged = _matmul([(dx1_b, w["w_out"])], "nt", F32, tm=512, tn=512, name="d_merged")
    (dg_hg, dz_hg, do_hg, dg_da, dz_da, do_da, dg_mem, dz_mem, do_mem) = _branch_bwd(
        dmerged, o_hg, o_da, o_mem, proj, w["w_proj_hg"], w["w_proj_da"], w["w_proj_mem"], name="branch_bwd")
    gw["w_proj_hg"] = _matmul([(o_hg, dz_hg)], "tn", F32, tm=512, tn=512, name="g_proj_hg")
    gw["w_proj_da"] = _matmul([(o_da, dz_da)], "tn", F32, tm=512, tn=512, name="g_proj_da")
    gw["w_proj_mem"] = _matmul([(o_mem, dz_mem)], "tn", F32, tm=512, tn=512, name="g_proj_mem")

    dq_mem, dk_mem, dv_mem, gs["mem_q_gain"], gs["mem_k_gain"] = _mem_bwd(
        do_mem, proj, kv, p["mem_q_gain"], p["mem_k_gain"], name="mem_attn_bwd")
    dkv = jnp.concatenate([dk_mem, dv_mem], axis=1).astype(BF16)
    gw["w_mem_kv"] = _matmul([(mem_n, dkv)], "tn", F32, tm=512, tn=512, name="g_mem_kv")
    dmem_n = _matmul([(dkv, w["w_mem_kv"])], "nt", F32, tm=256, tn=512, name="d_mem_n")
    _, _, gs["norm_mem_gain"] = _rms_bwd_rows(dmem_n, mem, p["norm_mem_gain"], None, name="norm_mem_bwd")

    prep = _da_bwd_prep(do_da, o_da, lse_da, name="da_bwd_prep")
    d_da, gq_parts, gk_parts = [], [], []
    for g in range(len(DA_GROUPS)):
        qr, kr, vr = qkv_r[g]
        dor, lser, deltar = prep[3 * g:3 * g + 3]
        dqr, dkr, dvr = _band_bwd(qr, kr, vr, dor, lser, deltar, g, name=f"band_bwd{g}")
        dq, dk, dv, gq, gk = _da_prep_bwd(dqr, dkr, dvr, proj, p["da_q_gain"], p["da_k_gain"], g,
                                          name=f"da_prep_bwd{g}")
        d_da.append((dq, dk, dv))
        gq_parts.append(gq)
        gk_parts.append(gk)

    dg_hgate, do_gla, gs["hg_norm_gain"] = _hg_out_bwd(do_hg, o_fw, o_bw, proj, p["hg_norm_gain"], name="hg_out_bwd")
    dq_f, dfl_fw, dv_f, dlb_fw = _gla_bwd(proj, lb_fw, do_gla, st_fw, None, f_off=C_FF, rev=False, name="gla_bwd_fw")
    dq_hg, dfl_bw, dv_hg, dlb_bw = _gla_bwd(proj, lb_bw, do_gla, st_bw, (dq_f, dv_f), f_off=C_FB, rev=True,
                                            name="gla_bwd_bw")

    dproj = jnp.concatenate(
        [dq_hg, dfl_fw, dfl_bw, dv_hg, dg_hgate]
        + [t[0] for t in d_da] + [t[1] for t in d_da] + [t[2] for t in d_da]
        + [dq_mem, dg_hg, dg_da, dg_mem], axis=1)
    gw["w_in"] = _matmul([(h, dproj)], "tn", F32, tm=512, tn=512, name="g_in")
    dh = _matmul([(dproj, w["w_in"])], "nt", F32, tm=512, tn=512, tk=1024, name="d_h")
    grad_x, _, gs["norm_mix_gain"] = _rms_bwd_rows(dh, x, p["norm_mix_gain"], dx1, name="norm_mix_bwd")

    small = _small_pack(gs, gq_parts, gk_parts, dlb_fw, dlb_bw, lb_fw, lb_bw, name="small_pack")
    return loss, grad_x, gw, small


def _small_pack(gs, gq_parts, gk_parts, dlb_fw, dlb_bw, lb_fw, lb_bw, *, name):
    def body(g_mix, g_mem, g_ffn, dfw, dbw, lfw, lbw, g_hg, q0, q1, q2, k0, k1, k2, g_mq, g_mk, o_ref):
        o_ref[0:1, :] = g_mix[...]
        o_ref[1:2, :] = g_mem[...]
        o_ref[2:3, :] = g_ffn[...]
        for base, d_ref, l_ref in ((3, dfw, lfw), (5, dbw, lbw)):
            lbv = l_ref[...]
            t = d_ref[...] * lbv * (1.0 - lbv)
            o_ref[base:base + 1, :] = t
            o_ref[base + 1:base + 2, :] = -t
        o_ref[7:8, :] = jnp.zeros((1, D_MODEL), F32)
        o_ref[7:8, 0:HEAD] = g_hg[...]
        o_ref[7:8, HEAD:2 * HEAD] = q0[...] + q1[...] + q2[...]
        o_ref[7:8, 2 * HEAD:3 * HEAD] = k0[...] + k1[...] + k2[...]
        o_ref[7:8, 3 * HEAD:4 * HEAD] = g_mq[...]
        o_ref[7:8, 4 * HEAD:5 * HEAD] = g_mk[...]

    return pl.pallas_call(body, name=name, out_shape=jax.ShapeDtypeStruct((8, D_MODEL), F32))(
        gs["norm_mix_gain"], gs["norm_mem_gain"], gs["norm_ffn_gain"], dlb_fw, dlb_bw, lb_fw, lb_bw,
        gs["hg_norm_gain"], *gq_parts, *gk_parts, gs["mem_q_gain"], gs["mem_k_gain"])


def _row_tile(rows, cols, n_arrays):
    budget = (16 * 1024 * 1024) // (2 * 4 * cols * n_arrays)
    tr = rows
    while tr > budget and tr % 2 == 0 and (tr // 2) % 16 == 0:
        tr //= 2
    return tr


def _cast_bf16(a, *, name):
    rows, cols = a.shape
    tr = _row_tile(rows, cols, 2)

    def body(a_ref, o_ref):
        o_ref[...] = a_ref[...].astype(BF16)

    spec = pl.BlockSpec((tr, cols), lambda i: (i, 0))
    return pl.pallas_call(body, name=name, grid=(rows // tr,), in_specs=[spec], out_specs=spec,
                          out_shape=jax.ShapeDtypeStruct((rows, cols), BF16),
                          compiler_params=_params(("parallel",)))(a)


def _add_halves(pairs, *, name):
    rows = pairs[0][0].shape[0]
    widths = [m.shape[1] for m, _ in pairs]
    tr = _row_tile(rows, sum(widths), 3)

    def body(*refs):
        o_ref = refs[-1]
        off = 0
        for j, wd in enumerate(widths):
            o_ref[:, off:off + wd] = refs[2 * j][...] + refs[2 * j + 1][...]
            off += wd

    in_specs, ins = [], []
    for (m, s), wd in zip(pairs, widths):
        in_specs += [pl.BlockSpec((tr, wd), lambda i: (i, 0))] * 2
        ins += [m, s]
    return pl.pallas_call(body, name=name, grid=(rows // tr,), in_specs=in_specs,
                          out_specs=pl.BlockSpec((tr, sum(widths)), lambda i: (i, 0)),
                          out_shape=jax.ShapeDtypeStruct((rows, sum(widths)), F32),
                          compiler_params=_params(("parallel",)))(*ins)


def _add_slots(rb, *, name):
    _, rows, cols = rb.shape
    tr = _row_tile(rows, cols, 5)

    def body(r0, r1, r2, r3, o_ref):
        o_ref[...] = ((r0[...] + r1[...]) + r2[...]) + r3[...]

    slot = lambda s: pl.BlockSpec((None, tr, cols), lambda i: (s, i, 0))
    return pl.pallas_call(body, name=name, grid=(rows // tr,), in_specs=[slot(s) for s in range(4)],
                          out_specs=pl.BlockSpec((tr, cols), lambda i: (i, 0)),
                          out_shape=jax.ShapeDtypeStruct((rows, cols), F32),
                          compiler_params=_params(("parallel",)))(rb, rb, rb, rb)


def _adamw(w, g, m, v, *, name):
    rows, cols = w.shape
    tr = _row_tile(rows, cols, 7) if rows % 16 == 0 else rows
    c1 = 1.0 - ADAM_B1 ** ADAM_STEP
    c2 = 1.0 - ADAM_B2 ** ADAM_STEP

    def body(w_ref, g_ref, m_ref, v_ref, d_ref, mo_ref, vo_ref):
        gv = g_ref[...]
        mn = ADAM_B1 * m_ref[...] + (1.0 - ADAM_B1) * gv
        vn = ADAM_B2 * v_ref[...] + (1.0 - ADAM_B2) * (gv * gv)
        mo_ref[...] = mn
        vo_ref[...] = vn
        d_ref[...] = -ADAM_LR * ((mn / c1) / (jnp.sqrt(vn / c2) + ADAM_EPS) + ADAM_WD * w_ref[...])

    spec = pl.BlockSpec((tr, cols), lambda i: (i, 0))
    shape = jax.ShapeDtypeStruct((rows, cols), F32)
    return pl.pallas_call(body, name=name, grid=(rows // tr,), in_specs=[spec] * 4, out_specs=[spec] * 3,
                          out_shape=[shape] * 3, compiler_params=_params(("parallel",)))(w, g, m, v)


W_SPECS = (
    ("w_in", 1024, IN_COLS, 1, IN_COLS // 4),
    ("w_mem_kv", 1024, 1024, 0, 256),
    ("w_proj_hg", 1024, 1024, 0, 256),
    ("w_proj_da", 512, 1024, 1, 256),
    ("w_proj_mem", 512, 1024, 1, 256),
    ("w_out", 1024, 1024, 0, 256),
    ("w_ffn_in", 1024, 2 * D_FF, 1, 2 * D_FF // 4),
    ("w_ffn_out", D_FF, 1024, 0, D_FF // 4),
)
CHIP_FLIPS = ((1, 0), (0, 1), (1, 1))
ANY = pl.BlockSpec(memory_space=pl.ANY)
DMA_CHUNK_BYTES = 1 << 20


def _place():
    x, y, c = lax.axis_index("x"), lax.axis_index("y"), lax.axis_index("c")
    return x, y, c, 2 * x + y


def _flip(v, f):
    return 1 - v if f else v


def _slab(ref, axis, idx, size):
    start = pl.multiple_of(idx * size, size)
    return ref.at[pl.ds(start, size), :] if axis == 0 else ref.at[:, pl.ds(start, size)]


def _chunked(make, src, dst):
    rows, cols = src.shape
    row_bytes = cols * jnp.dtype(src.dtype).itemsize
    k = 1
    while rows % (2 * k) == 0 and (rows // (2 * k)) % 16 == 0 and (rows // k) * row_bytes > DMA_CHUNK_BYTES:
        k *= 2
    cr = rows // k
    parts = [make(src.at[pl.ds(j * cr, cr), :], dst.at[pl.ds(j * cr, cr), :]) for j in range(k)]
    return parts, make(src, dst)


def _run_copies(local, remote):
    for parts, _ in local + remote:
        for cp in parts:
            cp.start()
    for _, whole in remote:
        whole.wait_recv()
    for _, whole in remote:
        whole.wait_send()
    for _, whole in local:
        whole.wait()


def _half_spec(rows, cols, axis):
    return (0, rows // 2) if axis == 1 else (1, cols // 2)


def _gather_weights(shards):
    n = len(W_SPECS)

    def body(*refs):
        ins, outs = refs[:n], refs[n:2 * n]
        send_sems, recv_sems, local_sems = refs[2 * n:]
        x, y, c, p = _place()
        local, remote = [], []
        for wi, (_, _, _, axis, size) in enumerate(W_SPECS):
            mine = _slab(outs[wi], axis, p, size)
            local.append(_chunked(lambda s, d, wi=wi: pltpu.make_async_copy(s, d, local_sems.at[wi]), ins[wi], mine))
            for k, (fx, fy) in enumerate(CHIP_FLIPS):
                remote.append(_chunked(lambda s, d, j=3 * wi + k, fx=fx, fy=fy: pltpu.make_async_remote_copy(
                    src_ref=s, dst_ref=d, send_sem=send_sems.at[j], recv_sem=recv_sems.at[j],
                    device_id=(_flip(x, fx), _flip(y, fy), c), device_id_type=MESH), ins[wi], mine))
        _run_copies(local, remote)

    return pl.pallas_call(
        body, name="gather_weights", in_specs=[ANY] * n, out_specs=[ANY] * n,
        out_shape=[jax.ShapeDtypeStruct((r, cc), BF16) for _, r, cc, _, _ in W_SPECS],
        scratch_shapes=[pltpu.SemaphoreType.DMA((3 * n,)), pltpu.SemaphoreType.DMA((3 * n,)),
                        pltpu.SemaphoreType.DMA((n,))],
    )(*shards)


def _sibling_exchange(grads):
    n = len(grads)
    halves = [_half_spec(g.shape[0], g.shape[1], axis) for g, axis in grads]

    def body(*refs):
        ins, mine_out, sib_out = refs[:n], refs[n:2 * n], refs[2 * n:3 * n]
        send_sems, recv_sems, local_sems = refs[3 * n:]
        x, y, c, _ = _place()
        local, remote = [], []
        for i, (haxis, hsize) in enumerate(halves):
            local.append(_chunked(lambda s, d, i=i: pltpu.make_async_copy(s, d, local_sems.at[i]),
                                  _slab(ins[i], haxis, c, hsize), mine_out[i]))
            remote.append(_chunked(lambda s, d, i=i: pltpu.make_async_remote_copy(
                src_ref=s, dst_ref=d, send_sem=send_sems.at[i], recv_sem=recv_sems.at[i],
                device_id=(x, y, 1 - c), device_id_type=MESH), _slab(ins[i], haxis, 1 - c, hsize), sib_out[i]))
        _run_copies(local, remote)

    shapes = []
    for (g, _), (haxis, hsize) in zip(grads, halves):
        shapes.append(jax.ShapeDtypeStruct((hsize, g.shape[1]) if haxis == 0 else (g.shape[0], hsize), F32))
    res = pl.pallas_call(
        body, name="grad_sibling_exchange", in_specs=[ANY] * n, out_specs=[ANY] * (2 * n),
        out_shape=shapes + shapes,
        scratch_shapes=[pltpu.SemaphoreType.DMA((n,))] * 3,
    )(*[g for g, _ in grads])
    return list(zip(res[:n], res[n:]))


def _chip_exchange(parts):
    n = len(parts)

    def body(*refs):
        ins, outs = refs[:n], refs[n:2 * n]
        send_sems, recv_sems, local_sems = refs[2 * n:]
        x, y, c, p = _place()
        local, remote = [], []
        for i, (_, axis, size) in enumerate(parts):
            local.append(_chunked(lambda s, d, i=i: pltpu.make_async_copy(s, d, local_sems.at[i]),
                                  _slab(ins[i], axis, p, size), outs[i].at[p]))
            for k, (fx, fy) in enumerate(CHIP_FLIPS):
                px, py = _flip(x, fx), _flip(y, fy)
                remote.append(_chunked(lambda s, d, j=3 * i + k, px=px, py=py: pltpu.make_async_remote_copy(
                    src_ref=s, dst_ref=d, send_sem=send_sems.at[j], recv_sem=recv_sems.at[j],
                    device_id=(px, py, c), device_id_type=MESH), _slab(ins[i], axis, 2 * px + py, size), outs[i].at[p]))
        _run_copies(local, remote)

    shapes = []
    for a, axis, size in parts:
        shapes.append(jax.ShapeDtypeStruct((4, size, a.shape[1]) if axis == 0 else (4, a.shape[0], size), F32))
    return pl.pallas_call(
        body, name="grad_chip_exchange", in_specs=[ANY] * n, out_specs=[ANY] * n, out_shape=shapes,
        scratch_shapes=[pltpu.SemaphoreType.DMA((3 * n,)), pltpu.SemaphoreType.DMA((3 * n,)),
                        pltpu.SemaphoreType.DMA((n,))],
    )(*[a for a, _, _ in parts])


def _sibling_share(sums):
    n = len(sums)

    def body(*refs):
        ins, outs = refs[:n], refs[n:2 * n]
        send_sems, recv_sems, local_sems = refs[2 * n:]
        x, y, c, _ = _place()
        local, remote = [], []
        for i, (s, haxis) in enumerate(sums):
            place = _slab(outs[i], haxis, c, s.shape[haxis])
            local.append(_chunked(lambda s, d, i=i: pltpu.make_async_copy(s, d, local_sems.at[i]), ins[i], place))
            remote.append(_chunked(lambda s, d, i=i: pltpu.make_async_remote_copy(
                src_ref=s, dst_ref=d, send_sem=send_sems.at[i], recv_sem=recv_sems.at[i],
                device_id=(x, y, 1 - c), device_id_type=MESH), ins[i], place))
        _run_copies(local, remote)

    shapes = []
    for s, haxis in sums:
        r, cc = s.shape
        shapes.append(jax.ShapeDtypeStruct((2 * r, cc) if haxis == 0 else (r, 2 * cc), F32))
    return pl.pallas_call(
        body, name="grad_sibling_share", in_specs=[ANY] * n, out_specs=[ANY] * n, out_shape=shapes,
        scratch_shapes=[pltpu.SemaphoreType.DMA((n,))] * 3,
    )(*[s for s, _ in sums])


def _reduce_scatter(gw):
    grads = []
    for name, _, _, axis, _ in W_SPECS:
        if name == "w_ffn_in":
            grads += [(gw["w_ffn_a"], axis), (gw["w_ffn_b"], axis)]
        else:
            grads.append((gw[name], axis))
    halves = _sibling_exchange(grads)
    parts, j = [], 0
    for name, _, _, axis, size in W_SPECS:
        take = 2 if name == "w_ffn_in" else 1
        parts.append((_add_halves(halves[j:j + take], name=f"half_sum_{name}"), axis, size))
        j += take
    slots = _chip_exchange(parts)
    sums = []
    for (name, rows, cols, axis, _), rb in zip(W_SPECS, slots):
        sums.append((_add_slots(rb, name=f"chip_sum_{name}"), _half_spec(rows, cols, axis)[0]))
    return dict(zip([s[0] for s in W_SPECS], _sibling_share(sums)))


def _small_allreduce(sv):
    def body(sv_ref, o_ref, slots_ref, send_sems, recv_sems):
        x, y, c, _ = _place()
        me = 4 * x + 2 * y + c
        slots_ref[me] = sv_ref[...]
        copies = []
        for k in range(1, 8):
            fx, fy, fc = (k >> 2) & 1, (k >> 1) & 1, k & 1
            copies.append(pltpu.make_async_remote_copy(
                src_ref=sv_ref, dst_ref=slots_ref.at[me], send_sem=send_sems.at[k - 1],
                recv_sem=recv_sems.at[k - 1], device_id=(_flip(x, fx), _flip(y, fy), _flip(c, fc)),
                device_id_type=MESH))
        for cp in copies:
            cp.start()
        for cp in copies:
            cp.wait_recv()
        for cp in copies:
            cp.wait_send()
        total = slots_ref[0]
        for s in range(1, 8):
            total = total + slots_ref[s]
        o_ref[...] = total

    vm = pl.BlockSpec(memory_space=pltpu.VMEM)
    return pl.pallas_call(
        body, name="small_allreduce", in_specs=[vm], out_specs=vm,
        out_shape=jax.ShapeDtypeStruct(sv.shape, F32),
        scratch_shapes=[pltpu.VMEM((8,) + sv.shape, F32), pltpu.SemaphoreType.DMA((7,)),
                        pltpu.SemaphoreType.DMA((7,))],
    )(sv)


SMALL_ROWS = (("norm_mix_gain", 0), ("norm_mem_gain", 1), ("norm_ffn_gain", 2))
SMALL_LB = (("lb_logits_fw", 3), ("lb_logits_bw", 5))
SMALL_HEAD = ("hg_norm_gain", "da_q_gain", "da_k_gain", "mem_q_gain", "mem_k_gain")


def _pack_small(d):
    last = jnp.concatenate([d[n] for n in SMALL_HEAD] + [jnp.zeros((1, D_MODEL - HEAD * len(SMALL_HEAD)), F32)], axis=1)
    return jnp.concatenate([d["norm_mix_gain"], d["norm_mem_gain"], d["norm_ffn_gain"],
                            d["lb_logits_fw"], d["lb_logits_bw"], last], axis=0)


def _unpack_small(a):
    out = {n: a[r:r + 1] for n, r in SMALL_ROWS}
    out.update({n: a[r:r + 2] for n, r in SMALL_LB})
    out.update({n: a[7:8, j * HEAD:(j + 1) * HEAD] for j, n in enumerate(SMALL_HEAD)})
    return out


PARAM_ORDER = ("norm_mix_gain", "norm_mem_gain", "w_in", "lb_logits_fw", "lb_logits_bw", "hg_norm_gain",
               "da_q_gain", "da_k_gain", "w_mem_kv", "mem_q_gain", "mem_k_gain", "w_proj_hg", "w_proj_da",
               "w_proj_mem", "w_out", "norm_ffn_gain", "w_ffn_in", "w_ffn_out")


def kernel(x, mem, norm_mix_gain, norm_mem_gain, w_in, lb_logits_fw, lb_logits_bw, hg_norm_gain, da_q_gain, da_k_gain, w_mem_kv, mem_q_gain, mem_k_gain, w_proj_hg, w_proj_da, w_proj_mem, w_out, norm_ffn_gain, w_ffn_in, w_ffn_out, loss_target, m_norm_mix_gain, m_norm_mem_gain, m_w_in, m_lb_logits_fw, m_lb_logits_bw, m_hg_norm_gain, m_da_q_gain, m_da_k_gain, m_w_mem_kv, m_mem_q_gain, m_mem_k_gain, m_w_proj_hg, m_w_proj_da, m_w_proj_mem, m_w_out, m_norm_ffn_gain, m_w_ffn_in, m_w_ffn_out, v_norm_mix_gain, v_norm_mem_gain, v_w_in, v_lb_logits_fw, v_lb_logits_bw, v_hg_norm_gain, v_da_q_gain, v_da_k_gain, v_w_mem_kv, v_mem_q_gain, v_mem_k_gain, v_w_proj_hg, v_w_proj_da, v_w_proj_mem, v_w_out, v_norm_ffn_gain, v_w_ffn_in, v_w_ffn_out):
    args = dict(locals())
    mats = tuple(s[0] for s in W_SPECS)
    flat = lambda a: a.reshape(a.shape[-2:])
    w = {n: flat(args[n]) for n in mats}
    m = {n: flat(args["m_" + n]) for n in mats}
    v = {n: flat(args["v_" + n]) for n in mats}
    small = {n: args[n] for n in PARAM_ORDER if n not in mats}

    full = _gather_weights([_cast_bf16(w[n], name=f"cast_{n}") for n in mats])
    loss, grad_x, gw, small_grads = _local_step(x[0], mem[0], loss_target[0], small, dict(zip(mats, full)))
    grads = _reduce_scatter(gw)
    small_sum = _small_allreduce(small_grads)

    delta, new_m, new_v = {}, {}, {}
    for n in mats:
        delta[n], new_m[n], new_v[n] = _adamw(w[n], grads[n], m[n], v[n], name=f"adamw_{n}")
    packed = _adamw(_pack_small(small), small_sum,
                    _pack_small({n: args["m_" + n] for n in small}),
                    _pack_small({n: args["v_" + n] for n in small}), name="adamw_small")
    grads.update(_unpack_small(small_sum))
    for dst, src in zip((delta, new_m, new_v), packed):
        dst.update(_unpack_small(src))

    def shaped(d, n):
        return d[n].reshape(args[n].shape)

    loss_sum = lax.psum(loss[0, 0], ("x", "y", "c"))
    return (loss_sum, grad_x[None], *[shaped(grads, n) for n in PARAM_ORDER], *[shaped(delta, n) for n in PARAM_ORDER],
            *[shaped(new_m, n) for n in PARAM_ORDER], *[shaped(new_v, n) for n in PARAM_ORDER])
```

```python
import functools
import math

import numpy as np
import jax
import jax.numpy as jnp
from jax import lax
from jax.experimental import pallas as pl
from jax.experimental.pallas import tpu as pltpu

F32 = jnp.float32
BF16 = jnp.bfloat16
MESH = pl.DeviceIdType.MESH

D_MODEL = 1024
HEAD = 128
HG_HEADS = 8
DA_GROUPS = ((1, 64), (4, 64), (16, 64))
DA_GH = 4
MEM_HEADS = 4
N_MEM = 256
D_FF = 2816
CHUNK = 64
RMS_EPS = 1e-6
NEG_INF = -1e30
HG_SCALE = HEAD ** -0.5
ATT_SCALE = HEAD ** -0.5
VMEM_LIMIT_V7X = 48 * 1024 * 1024

C_HQ, C_FF, C_FB, C_HI, C_HG = 0, 1024, 2048, 3072, 4096
C_DQ, C_DK, C_DV, C_MQ = 5120, 6656, 8192, 9728
C_GHG, C_GDA, C_GMEM = 10240, 11264, 12288
IN_COLS = 13312

ADAM_LR, ADAM_B1, ADAM_B2, ADAM_EPS, ADAM_WD, ADAM_STEP = 0.001, 0.9, 0.999, 1e-08, 0.01, 10


def _params(sem, vmem=VMEM_LIMIT_V7X):
    return pltpu.CompilerParams(dimension_semantics=sem, vmem_limit_bytes=vmem)


def _dot(a, b):
    return jnp.dot(a.astype(BF16), b.astype(BF16), preferred_element_type=F32)


def _dot_nt(a, b):
    return lax.dot_general(a.astype(BF16), b.astype(BF16), (((1,), (1,)), ((), ())),
                           preferred_element_type=F32)


def _dot_tn(a, b):
    return lax.dot_general(a.astype(BF16), b.astype(BF16), (((0,), (0,)), ((), ())),
                           preferred_element_type=F32)


def _sigmoid(v):
    return jax.nn.sigmoid(v)


def _rms(v):
    r = lax.rsqrt(jnp.mean(v * v, axis=-1, keepdims=True) + RMS_EPS)
    return v * r, r


def _rms_bwd(dy, xhat, r, gain):
    dxh = dy * gain
    dx = r * (dxh - xhat * jnp.mean(dxh * xhat, axis=-1, keepdims=True))
    return dx, dy * xhat


def _matmul(pairs, mode, out_dtype, *, tm, tn, tk=None, residual=None, name):
    b_offs = [pr[2] if len(pr) > 2 else 0 for pr in pairs]
    pairs = [pr[:2] for pr in pairs]
    a0, b0 = pairs[0]
    if mode == "nn":
        (m, kk), n = a0.shape, b0.shape[1]
    elif mode == "nt":
        (m, kk), n = a0.shape, b0.shape[0]
    else:
        (kk, m), n = a0.shape, b0.shape[1]
    assert mode == "nt" or not any(b_offs)
    tm, tn = min(tm, m), min(tn, n)
    tk = kk if tk is None else tk
    nk = kk // tk
    assert m % tm == 0 and n % tn == 0 and kk % tk == 0, (name, m, n, kk)
    n_p = len(pairs)
    if mode == "tn":
        a_spec = pl.BlockSpec((tk, tm), lambda i, j, k: (k, i))
    else:
        a_spec = pl.BlockSpec((tm, tk), lambda i, j, k: (i, k))
    if mode == "nt":
        b_specs = [pl.BlockSpec((tn, tk), lambda i, j, k, o=o: (j, o * nk + k)) for o in b_offs]
    else:
        b_specs = [pl.BlockSpec((tk, tn), lambda i, j, k: (k, j))] * n_p
    o_spec = pl.BlockSpec((tm, tn), lambda i, j, k: (i, j))
    dot = {"nn": _dot, "nt": _dot_nt, "tn": _dot_tn}[mode]
    has_res = residual is not None

    def body(*refs):
        a_refs, b_refs = refs[:n_p], refs[n_p:2 * n_p]
        pos = 2 * n_p
        res_ref = refs[pos] if has_res else None
        pos += int(has_res)
        o_ref = refs[pos]
        part = dot(a_refs[0][...], b_refs[0][...])
        for a_r, b_r in zip(a_refs[1:], b_refs[1:]):
            part += dot(a_r[...], b_r[...])

        def finish(total):
            if has_res:
                total = total + res_ref[...]
            o_ref[...] = total.astype(out_dtype)

        if nk == 1:
            finish(part)
        else:
            acc_ref = refs[pos + 1]
            k = pl.program_id(2)

            @pl.when(k == 0)
            def _():
                acc_ref[...] = part

            @pl.when(k > 0)
            def _():
                acc_ref[...] += part

            @pl.when(k == nk - 1)
            def _():
                finish(acc_ref[...])

    ins = [a for a, _ in pairs] + [b for _, b in pairs]
    in_specs = [a_spec] * n_p + b_specs
    if has_res:
        ins.append(residual)
        in_specs.append(o_spec)
    return pl.pallas_call(
        body, name=name, grid=(m // tm, n // tn, nk),
        in_specs=in_specs, out_specs=o_spec,
        out_shape=jax.ShapeDtypeStruct((m, n), out_dtype),
        scratch_shapes=[pltpu.VMEM((tm, tn), F32)] if nk > 1 else [],
        compiler_params=_params(("parallel", "parallel", "arbitrary")),
    )(*ins)


def _rms_fwd(x, gain, *, name):
    rows, dm = x.shape
    tm = min(512, rows)

    def body(x_ref, g_ref, h_ref):
        xhat, _ = _rms(x_ref[...])
        h_ref[...] = (xhat * g_ref[...]).astype(BF16)

    return pl.pallas_call(
        body, name=name, grid=(rows // tm,),
        in_specs=[pl.BlockSpec((tm, dm), lambda i: (i, 0)), pl.BlockSpec((1, dm), lambda i: (0, 0))],
        out_specs=pl.BlockSpec((tm, dm), lambda i: (i, 0)),
        out_shape=jax.ShapeDtypeStruct((rows, dm), BF16),
        compiler_params=_params(("parallel",)),
    )(x, gain)


def _rms_bwd_rows(dh, x, gain, dres, *, name):
    rows, dm = x.shape
    tm = min(512, rows)
    has_res = dres is not None

    def body(*refs):
        dh_ref, x_ref, g_ref = refs[:3]
        res_ref = refs[3] if has_res else None
        dx_ref, dxb_ref, dg_ref = refs[3 + int(has_res):]
        xhat, r = _rms(x_ref[...])
        dx, dgr = _rms_bwd(dh_ref[...], xhat, r, g_ref[...])
        if has_res:
            dx = dx + res_ref[...]
        dx_ref[...] = dx
        dxb_ref[...] = dx.astype(BF16)

        @pl.when(pl.program_id(0) == 0)
        def _():
            dg_ref[...] = jnp.zeros_like(dg_ref)

        dg_ref[...] += jnp.sum(dgr, axis=0, keepdims=True)

    row = pl.BlockSpec((tm, dm), lambda i: (i, 0))
    vec = pl.BlockSpec((1, dm), lambda i: (0, 0))
    return pl.pallas_call(
        body, name=name, grid=(rows // tm,),
        in_specs=[row, row, vec] + ([row] if has_res else []),
        out_specs=[row, row, vec],
        out_shape=[jax.ShapeDtypeStruct((rows, dm), F32), jax.ShapeDtypeStruct((rows, dm), BF16),
                   jax.ShapeDtypeStruct((1, dm), F32)],
        compiler_params=_params(("arbitrary",)),
    )(*([dh, x, gain] + ([dres] if has_res else [])))


def _loss_grad(y, tgt, *, name):
    rows, dm = y.shape
    tm = min(512, rows)
    steps = rows // tm

    def body(y_ref, t_ref, dy_ref, dyb_ref, loss_ref, acc_ref):
        i = pl.program_id(0)
        diff = y_ref[...] - t_ref[...]
        dy = diff * (1.0 / dm)
        dy_ref[...] = dy
        dyb_ref[...] = dy.astype(BF16)

        @pl.when(i == 0)
        def _():
            acc_ref[...] = jnp.zeros_like(acc_ref)

        acc_ref[...] += jnp.sum(diff * diff, axis=0, keepdims=True)

        @pl.when(i == steps - 1)
        def _():
            loss_ref[...] = jnp.full((1, HEAD), 0.5 / dm, F32) * jnp.sum(acc_ref[...])

    row = pl.BlockSpec((tm, dm), lambda i: (i, 0))
    return pl.pallas_call(
        body, name=name, grid=(steps,),
        in_specs=[row, row],
        out_specs=[row, row, pl.BlockSpec((1, HEAD), lambda i: (0, 0))],
        out_shape=[jax.ShapeDtypeStruct((rows, dm), F32), jax.ShapeDtypeStruct((rows, dm), BF16),
                   jax.ShapeDtypeStruct((1, HEAD), F32)],
        scratch_shapes=[pltpu.VMEM((1, dm), F32)],
        compiler_params=_params(("arbitrary",)),
    )(y, tgt)


def _gla_chunk_terms(q_raw, f_logit, lb, tri_f, rev):
    del rev
    sig = _sigmoid(f_logit)
    forget = lb + (1.0 - lb) * sig
    k = 1.0 - forget
    logf = jnp.log(forget)
    b = jnp.dot(tri_f, logf, precision=lax.Precision.HIGHEST, preferred_element_type=F32)
    bl = jnp.sum(logf, axis=0, keepdims=True)
    qs = _sigmoid(q_raw)
    q = q_raw * qs * HG_SCALE
    eb = jnp.exp(b)
    qt = q * eb
    kt = k * jnp.exp(-b)
    kh = k * jnp.exp(bl - b)
    return sig, forget, k, b, bl, qs, eb, qt, kt, kh


def _tri_mask(rev):
    row = lax.broadcasted_iota(jnp.int32, (CHUNK, CHUNK), 0)
    col = lax.broadcasted_iota(jnp.int32, (CHUNK, CHUNK), 1)
    return (row <= col) if rev else (row >= col)


def _gla_fwd(proj, lb, *, f_off, rev, name):
    rows = proj.shape[0]
    tb = min(256, rows)
    nb, ncb = rows // tb, tb // CHUNK

    def tmap(n):
        return nb - 1 - n if rev else n

    def body(q_ref, f_ref, v_ref, lb_ref, o_ref, st_ref, s_ref):
        @pl.when(pl.program_id(1) == 0)
        def _():
            s_ref[...] = jnp.zeros_like(s_ref)

        tri = _tri_mask(rev)
        tri_f = tri.astype(F32)
        lbv = lb_ref[...]
        for cc in range(ncb):
            c = ncb - 1 - cc if rev else cc
            sl = pl.ds(c * CHUNK, CHUNK)
            v = v_ref[sl, :]
            _, _, _, _, bl, _, _, qt, kt, kh = _gla_chunk_terms(q_ref[sl, :], f_ref[sl, :], lbv, tri_f, rev)
            s_t = s_ref[...]
            st_ref[c] = s_t
            a = jnp.where(tri, _dot_nt(qt, kt), 0.0)
            o_ref[sl, :] = _dot(a, v) + _dot_nt(qt, s_t)
            s_ref[...] = jnp.exp(bl) * s_t + _dot_tn(v, kh)

    col = lambda off: pl.BlockSpec((tb, HEAD), lambda h, n: (tmap(n), off // HEAD + h))
    return pl.pallas_call(
        body, name=name, grid=(HG_HEADS, nb),
        in_specs=[col(C_HQ), col(f_off), col(C_HI), pl.BlockSpec((1, HEAD), lambda h, n: (0, h))],
        out_specs=[pl.BlockSpec((tb, HEAD), lambda h, n: (tmap(n), h)),
                   pl.BlockSpec((None, ncb, HEAD, HEAD), lambda h, n: (h, tmap(n), 0, 0))],
        out_shape=[jax.ShapeDtypeStruct((rows, HG_HEADS * HEAD), F32),
                   jax.ShapeDtypeStruct((HG_HEADS, rows // CHUNK, HEAD, HEAD), F32)],
        scratch_shapes=[pltpu.VMEM((HEAD, HEAD), F32)],
        compiler_params=_params(("parallel", "arbitrary")),
    )(proj, proj, proj, lb)


def _gla_bwd(proj, lb, do, states, prev, *, f_off, rev, name):
    rows = proj.shape[0]
    tb = min(256, rows)
    nb, ncb = rows // tb, tb // CHUNK
    has_prev = prev is not None
    qv_dtype = BF16 if has_prev else F32

    def tmap(n):
        return n if rev else nb - 1 - n

    def body(*refs):
        q_ref, f_ref, v_ref, lb_ref, do_ref, st_ref = refs[:6]
        pq_ref, pv_ref = refs[6:8] if has_prev else (None, None)
        dq_ref, df_ref, dv_ref, dlb_ref, ds_ref = refs[6 + 2 * int(has_prev):]

        @pl.when(pl.program_id(1) == 0)
        def _():
            ds_ref[...] = jnp.zeros_like(ds_ref)
            dlb_ref[...] = jnp.zeros_like(dlb_ref)

        tri = _tri_mask(rev)
        tri_f = tri.astype(F32)
        tri_ft = _tri_mask(not rev).astype(F32)
        lbv = lb_ref[...]
        dlb = jnp.zeros((1, HEAD), F32)
        for cc in range(ncb):
            c = cc if rev else ncb - 1 - cc
            sl = pl.ds(c * CHUNK, CHUNK)
            q_raw, v, dout = q_ref[sl, :], v_ref[sl, :], do_ref[sl, :]
            sig, forget, k, b, bl, qs, eb, qt, kt, kh = _gla_chunk_terms(q_raw, f_ref[sl, :], lbv, tri_f, rev)
            s_t = st_ref[c]
            ds_t = ds_ref[...]
            ebl = jnp.exp(bl)
            a = jnp.where(tri, _dot_nt(qt, kt), 0.0)
            da = jnp.where(tri, _dot_nt(dout, v), 0.0)
            dv = _dot_tn(a, dout) + _dot_nt(kh, ds_t)
            dqt = _dot(da, kt) + _dot(dout, s_t)
            dkt = _dot_tn(da, qt)
            dkh = _dot(v, ds_t)
            ds_ref[...] = _dot_tn(dout, qt) + ds_t * ebl
            debl = jnp.sum(ds_t * s_t, axis=0, keepdims=True)
            dq = dqt * eb
            dk = dkt * jnp.exp(-b) + dkh * jnp.exp(bl - b)
            db = qt * dqt - kt * dkt - kh * dkh
            dbl = jnp.sum(kh * dkh, axis=0, keepdims=True) + ebl * debl
            dlogf = jnp.dot(tri_ft, db, precision=lax.Precision.HIGHEST, preferred_element_type=F32) + dbl
            dforget = dlogf / forget - dk
            df_ref[sl, :] = (dforget * (1.0 - lbv) * sig * (1.0 - sig)).astype(BF16)
            dlb = dlb + jnp.sum(dforget * (1.0 - sig), axis=0, keepdims=True)
            dqr = dq * (HG_SCALE * qs * (1.0 + q_raw * (1.0 - qs)))
            if has_prev:
                dqr = dqr + pq_ref[sl, :]
                dv = dv + pv_ref[sl, :]
            dq_ref[sl, :] = dqr.astype(qv_dtype)
            dv_ref[sl, :] = dv.astype(qv_dtype)
        dlb_ref[...] += dlb

    col = lambda off: pl.BlockSpec((tb, HEAD), lambda h, n: (tmap(n), off // HEAD + h))
    blk = pl.BlockSpec((tb, HEAD), lambda h, n: (tmap(n), h))
    vec = pl.BlockSpec((1, HEAD), lambda h, n: (0, h))
    wide = HG_HEADS * HEAD
    return pl.pallas_call(
        body, name=name, grid=(HG_HEADS, nb),
        in_specs=[col(C_HQ), col(f_off), col(C_HI), vec, blk,
                  pl.BlockSpec((None, ncb, HEAD, HEAD), lambda h, n: (h, tmap(n), 0, 0))]
                 + ([blk, blk] if has_prev else []),
        out_specs=[blk, blk, blk, vec],
        out_shape=[jax.ShapeDtypeStruct((rows, wide), qv_dtype), jax.ShapeDtypeStruct((rows, wide), BF16),
                   jax.ShapeDtypeStruct((rows, wide), qv_dtype), jax.ShapeDtypeStruct((1, wide), F32)],
        scratch_shapes=[pltpu.VMEM((HEAD, HEAD), F32)],
        compiler_params=_params(("parallel", "arbitrary")),
    )(*([proj, proj, proj, lb, do, states] + (list(prev) if has_prev else [])))


def _hg_out_fwd(o_fw, o_bw, proj, gain, *, name):
    rows = o_fw.shape[0]
    tm = min(512, rows)
    wide = HG_HEADS * HEAD

    def body(a_ref, b_ref, g_ref, gain_ref, o_ref):
        for h in range(HG_HEADS):
            sl = slice(h * HEAD, (h + 1) * HEAD)
            xhat, _ = _rms(a_ref[:, sl] + b_ref[:, sl])
            gate = g_ref[:, sl]
            o_ref[:, sl] = (xhat * gain_ref[...] * (gate * _sigmoid(gate))).astype(BF16)

    row = pl.BlockSpec((tm, wide), lambda i: (i, 0))
    return pl.pallas_call(
        body, name=name, grid=(rows // tm,),
        in_specs=[row, row, pl.BlockSpec((tm, wide), lambda i: (i, C_HG // wide)),
                  pl.BlockSpec((1, HEAD), lambda i: (0, 0))],
        out_specs=row, out_shape=jax.ShapeDtypeStruct((rows, wide), BF16),
        compiler_params=_params(("parallel",)),
    )(o_fw, o_bw, proj, gain)


def _hg_out_bwd(dout, o_fw, o_bw, proj, gain, *, name):
    rows = o_fw.shape[0]
    tm = min(512, rows)
    wide = HG_HEADS * HEAD

    def body(d_ref, a_ref, b_ref, g_ref, gain_ref, dgate_ref, do_ref, dgain_ref):
        @pl.when(pl.program_id(0) == 0)
        def _():
            dgain_ref[...] = jnp.zeros_like(dgain_ref)

        dgain = jnp.zeros((1, HEAD), F32)
        for h in range(HG_HEADS):
            sl = slice(h * HEAD, (h + 1) * HEAD)
            xhat, r = _rms(a_ref[:, sl] + b_ref[:, sl])
            gate, dy = g_ref[:, sl], d_ref[:, sl]
            gs = _sigmoid(gate)
            dgate_ref[:, sl] = (dy * xhat * gain_ref[...] * (gs * (1.0 + gate * (1.0 - gs)))).astype(BF16)
            dx, dgr = _rms_bwd(dy * (gate * gs), xhat, r, gain_ref[...])
            do_ref[:, sl] = dx
            dgain = dgain + jnp.sum(dgr, axis=0, keepdims=True)
        dgain_ref[...] += dgain

    row = pl.BlockSpec((tm, wide), lambda i: (i, 0))
    vec = pl.BlockSpec((1, HEAD), lambda i: (0, 0))
    return pl.pallas_call(
        body, name=name, grid=(rows // tm,),
        in_specs=[row, row, row, pl.BlockSpec((tm, wide), lambda i: (i, C_HG // wide)), vec],
        out_specs=[row, row, vec],
        out_shape=[jax.ShapeDtypeStruct((rows, wide), BF16), jax.ShapeDtypeStruct((rows, wide), F32),
                   jax.ShapeDtypeStruct((1, HEAD), F32)],
        compiler_params=_params(("arbitrary",)),
    )(dout, o_fw, o_bw, proj, gain)


def _strided_rows(ref, r, count, d):
    return ref[...] if d == 1 else ref[pl.ds(r, count, stride=d), :]


def _store_strided(ref, r, count, d, val):
    if d == 1:
        ref[...] = val
    else:
        ref[pl.ds(r, count, stride=d), :] = val


def _da_prep(proj, q_gain, k_gain, g, *, name):
    d = DA_GROUPS[g][0]
    rows = proj.shape[0]
    rb = min(512, rows)
    tn = rb // d

    def body(q_ref, k_ref, v_ref, qg_ref, kg_ref, qo_ref, ko_ref, vo_ref):
        for r in range(d):
            qhat, _ = _rms(_strided_rows(q_ref, r, tn, d))
            khat, _ = _rms(_strided_rows(k_ref, r, tn, d))
            qo_ref[r] = (qhat * qg_ref[...]).astype(BF16)
            ko_ref[r] = (khat * kg_ref[...]).astype(BF16)
            vo_ref[r] = _strided_rows(v_ref, r, tn, d).astype(BF16)

    col = lambda off: pl.BlockSpec((rb, HEAD), lambda h, i: (i, (off + g * DA_GH * HEAD) // HEAD + h))
    vec = pl.BlockSpec((1, HEAD), lambda h, i: (0, 0))
    out = pl.BlockSpec((None, d, tn, HEAD), lambda h, i: (h, 0, i, 0))
    shape = jax.ShapeDtypeStruct((DA_GH, d, rows // d, HEAD), BF16)
    return pl.pallas_call(
        body, name=name, grid=(DA_GH, rows // rb),
        in_specs=[col(C_DQ), col(C_DK), col(C_DV), vec, vec],
        out_specs=[out, out, out], out_shape=[shape, shape, shape],
        compiler_params=_params(("parallel", "parallel")),
    )(proj, proj, proj, q_gain, k_gain)


def _band_scores(q, k, jj, i, t, ld, slope, d, radius):
    s = _dot_nt(q, k) * ATT_SCALE
    row = lax.broadcasted_iota(jnp.int32, (t, t), 0)
    col = lax.broadcasted_iota(jnp.int32, (t, t), 1)
    rel = (jj - 1) * t + col - row
    kpos = (i + (jj - 1)) * t + col
    valid = (jnp.abs(rel) <= radius) & (kpos >= 0) & (kpos < ld)
    return jnp.where(valid, s - slope * (d * jnp.abs(rel)).astype(F32), NEG_INF)


def _slopes(g):
    idx = np.arange(g * DA_GH + 1, (g + 1) * DA_GH + 1)
    s = (2.0 ** (-8.0 * idx / (DA_GH * len(DA_GROUPS)))).astype(np.float32)
    return jnp.asarray(np.broadcast_to(s[:, None, None], (DA_GH, 8, HEAD)).copy())


def _band_fwd(qr, kr, vr, g, *, name):
    d, radius = DA_GROUPS[g]
    _, _, ld, _ = qr.shape
    t = min(HEAD, ld)
    nb = ld // t

    def body(q_ref, k0, k1, k2, v0, v1, v2, sl_ref, o_ref, lse_ref):
        i = pl.program_id(2)
        slope = sl_ref[0:1, 0:1]
        q = q_ref[...]
        s = [_band_scores(q, kk[...], jj, i, t, ld, slope, d, radius) for jj, kk in enumerate((k0, k1, k2))]
        m = jnp.maximum(jnp.maximum(jnp.max(s[0], axis=-1, keepdims=True), jnp.max(s[1], axis=-1, keepdims=True)),
                        jnp.max(s[2], axis=-1, keepdims=True))
        p = [jnp.exp(sj - m) for sj in s]
        l = sum(jnp.sum(pj, axis=-1, keepdims=True) for pj in p)
        o = sum(_dot(pj, vv[...]) for pj, vv in zip(p, (v0, v1, v2)))
        o_ref[...] = o / l
        lse_ref[...] = jnp.broadcast_to(m + jnp.log(l), (t, HEAD))

    own = pl.BlockSpec((None, None, t, HEAD), lambda h, r, i: (h, r, i, 0))
    prv = pl.BlockSpec((None, None, t, HEAD), lambda h, r, i: (h, r, jnp.maximum(i - 1, 0), 0))
    nxt = pl.BlockSpec((None, None, t, HEAD), lambda h, r, i: (h, r, jnp.minimum(i + 1, nb - 1), 0))
    shape = jax.ShapeDtypeStruct(qr.shape, F32)
    return pl.pallas_call(
        body, name=name, grid=(DA_GH, d, nb),
        in_specs=[own, prv, own, nxt, prv, own, nxt, pl.BlockSpec((None, 8, HEAD), lambda h, r, i: (h, 0, 0))],
        out_specs=[own, own], out_shape=[shape, shape],
        compiler_params=_params(("parallel", "parallel", "parallel")),
    )(qr, kr, kr, kr, vr, vr, vr, _slopes(g))


def _band_bwd(qr, kr, vr, dor, lser, deltar, g, *, name):
    d, radius = DA_GROUPS[g]
    _, _, ld, _ = qr.shape
    t = min(HEAD, ld)
    nb = ld // t

    def body(q_ref, k_ref, v_ref, do_ref, lse_ref, dl_ref, sl_ref, dq_ref, dk_ref, dv_ref):
        i = pl.program_id(2)

        @pl.when(i == 0)
        def _():
            dk_ref[...] = jnp.zeros_like(dk_ref)
            dv_ref[...] = jnp.zeros_like(dv_ref)

        slope = sl_ref[0:1, 0:1]
        q, dout = q_ref[...], do_ref[...]
        lse, delta = lse_ref[:, 0:1], dl_ref[:, 0:1]
        dq = jnp.zeros((t, HEAD), F32)
        for jj in range(3):
            kb = jnp.clip(i + (jj - 1), 0, nb - 1)
            rows = pl.ds(pl.multiple_of(kb * t, t), t)
            k, v = k_ref[rows, :], v_ref[rows, :]
            p = jnp.exp(_band_scores(q, k, jj, i, t, ld, slope, d, radius) - lse)
            ds = p * (_dot_nt(dout, v) - delta)
            dq = dq + _dot(ds, k)
            dk_ref[rows, :] += _dot_tn(ds, q) * ATT_SCALE
            dv_ref[rows, :] += _dot_tn(p, dout)
        dq_ref[...] = dq * ATT_SCALE

    own = pl.BlockSpec((None, None, t, HEAD), lambda h, r, i: (h, r, i, 0))
    seq = pl.BlockSpec((None, None, ld, HEAD), lambda h, r, i: (h, r, 0, 0))
    shape = jax.ShapeDtypeStruct(qr.shape, F32)
    return pl.pallas_call(
        body, name=name, grid=(DA_GH, d, nb),
        in_specs=[own, seq, seq, own, own, own, pl.BlockSpec((None, 8, HEAD), lambda h, r, i: (h, 0, 0))],
        out_specs=[own, seq, seq], out_shape=[shape, shape, shape],
        compiler_params=_params(("parallel", "parallel", "arbitrary")),
    )(qr, kr, vr, dor, lser, deltar, _slopes(g))


def _da_merge(outs, lses, rows, *, name):
    rb = min(512, rows)

    def body(*refs):
        o_refs, l_refs = refs[0:3], refs[3:6]
        o_ref, lse_ref = refs[6:8]
        on_refs, ln_refs = refs[8:11], refs[11:14]
        for g, (d, _) in enumerate(DA_GROUPS):
            tn = rb // d
            for r in range(d):
                _store_strided(on_refs[g], r, tn, d, o_refs[g][r])
                _store_strided(ln_refs[g], r, tn, d, l_refs[g][r])
        l0, l1, l2 = ln_refs[0][...], ln_refs[1][...], ln_refs[2][...]
        m = jnp.maximum(jnp.maximum(l0, l1), l2)
        e0, e1, e2 = jnp.exp(l0 - m), jnp.exp(l1 - m), jnp.exp(l2 - m)
        tot = e0 + e1 + e2
        o_ref[...] = (e0 * on_refs[0][...] + e1 * on_refs[1][...] + e2 * on_refs[2][...]) / tot
        lse_ref[...] = m + jnp.log(tot)

    res = lambda d: pl.BlockSpec((None, d, rb // d, HEAD), lambda h, i: (h, 0, i, 0))
    nat = pl.BlockSpec((rb, HEAD), lambda h, i: (i, h))
    shape = jax.ShapeDtypeStruct((rows, DA_GH * HEAD), F32)
    return pl.pallas_call(
        body, name=name, grid=(DA_GH, rows // rb),
        in_specs=[res(d) for d, _ in DA_GROUPS] * 2,
        out_specs=[nat, nat], out_shape=[shape, shape],
        scratch_shapes=[pltpu.VMEM((rb, HEAD), F32)] * 6,
        compiler_params=_params(("parallel", "parallel")),
    )(*outs, *lses)


def _da_bwd_prep(dout, o, lse, *, name):
    rows = o.shape[0]
    rb = min(512, rows)

    def body(d_ref, o_ref, l_ref, *outs):
        delta_ref = outs[9]
        delta_ref[...] = jnp.broadcast_to(jnp.sum(d_ref[...] * o_ref[...], axis=-1, keepdims=True), (rb, HEAD))
        for g, (d, _) in enumerate(DA_GROUPS):
            tn = rb // d
            for r in range(d):
                outs[3 * g][r] = _strided_rows(d_ref, r, tn, d).astype(BF16)
                outs[3 * g + 1][r] = _strided_rows(l_ref, r, tn, d)
                outs[3 * g + 2][r] = _strided_rows(delta_ref, r, tn, d)

    nat = pl.BlockSpec((rb, HEAD), lambda h, i: (i, h))
    out_specs, out_shape = [], []
    for d, _ in DA_GROUPS:
        for dt in (BF16, F32, F32):
            out_specs.append(pl.BlockSpec((None, d, rb // d, HEAD), lambda h, i: (h, 0, i, 0)))
            out_shape.append(jax.ShapeDtypeStruct((DA_GH, d, rows // d, HEAD), dt))
    return pl.pallas_call(
        body, name=name, grid=(DA_GH, rows // rb),
        in_specs=[nat, nat, nat], out_specs=out_specs, out_shape=out_shape,
        scratch_shapes=[pltpu.VMEM((rb, HEAD), F32)],
        compiler_params=_params(("parallel", "parallel")),
    )(dout, o, lse)


def _da_prep_bwd(dqr, dkr, dvr, proj, q_gain, k_gain, g, *, name):
    d = DA_GROUPS[g][0]
    rows = proj.shape[0]
    rb = min(512, rows)
    tn = rb // d

    def body(dq_ref, dk_ref, dv_ref, q_ref, k_ref, qg_ref, kg_ref, oq_ref, ok_ref, ov_ref, gq_ref, gk_ref, *nat_refs):
        @pl.when((pl.program_id(0) == 0) & (pl.program_id(1) == 0))
        def _():
            gq_ref[...] = jnp.zeros_like(gq_ref)
            gk_ref[...] = jnp.zeros_like(gk_ref)

        for j, src in enumerate((dq_ref, dk_ref, dv_ref)):
            for r in range(d):
                _store_strided(nat_refs[j], r, tn, d, src[r])
        ov_ref[...] = nat_refs[2][...].astype(BF16)
        for j, (x_ref, gn_ref, out_ref, acc_ref) in enumerate(((q_ref, qg_ref, oq_ref, gq_ref),
                                                                (k_ref, kg_ref, ok_ref, gk_ref))):
            xhat, r = _rms(x_ref[...])
            dx, dgr = _rms_bwd(nat_refs[j][...], xhat, r, gn_ref[...])
            out_ref[...] = dx.astype(BF16)
            acc_ref[...] += jnp.sum(dgr, axis=0, keepdims=True)

    res = pl.BlockSpec((None, d, tn, HEAD), lambda h, i: (h, 0, i, 0))
    col = lambda off: pl.BlockSpec((rb, HEAD), lambda h, i: (i, (off + g * DA_GH * HEAD) // HEAD + h))
    vec = pl.BlockSpec((1, HEAD), lambda h, i: (0, 0))
    nat = pl.BlockSpec((rb, HEAD), lambda h, i: (i, h))
    shape = jax.ShapeDtypeStruct((rows, DA_GH * HEAD), BF16)
    gshape = jax.ShapeDtypeStruct((1, HEAD), F32)
    return pl.pallas_call(
        body, name=name, grid=(DA_GH, rows // rb),
        in_specs=[res, res, res, col(C_DQ), col(C_DK), vec, vec],
        out_specs=[nat, nat, nat, vec, vec], out_shape=[shape, shape, shape, gshape, gshape],
        scratch_shapes=[pltpu.VMEM((rb, HEAD), F32)] * 3,
        compiler_params=_params(("arbitrary", "arbitrary")),
    )(dqr, dkr, dvr, proj, proj, q_gain, k_gain)


def _mem_fwd(proj, kv, q_gain, k_gain, *, name):
    rows = proj.shape[0]
    tm = min(512, rows)
    n_mem = kv.shape[0]

    def body(q_ref, k_ref, v_ref, qg_ref, kg_ref, o_ref):
        qhat, _ = _rms(q_ref[...])
        khat, _ = _rms(k_ref[...])
        s = _dot_nt(qhat * qg_ref[...], khat * kg_ref[...]) * ATT_SCALE
        p = jnp.exp(s - jnp.max(s, axis=-1, keepdims=True))
        p = p / jnp.sum(p, axis=-1, keepdims=True)
        o_ref[...] = _dot(p, v_ref[...]).astype(BF16)

    vec = pl.BlockSpec((1, HEAD), lambda i, h: (0, 0))
    return pl.pallas_call(
        body, name=name, grid=(rows // tm, MEM_HEADS),
        in_specs=[pl.BlockSpec((tm, HEAD), lambda i, h: (i, C_MQ // HEAD + h)),
                  pl.BlockSpec((n_mem, HEAD), lambda i, h: (0, h)),
                  pl.BlockSpec((n_mem, HEAD), lambda i, h: (0, MEM_HEADS + h)), vec, vec],
        out_specs=pl.BlockSpec((tm, HEAD), lambda i, h: (i, h)),
        out_shape=jax.ShapeDtypeStruct((rows, MEM_HEADS * HEAD), BF16),
        compiler_params=_params(("parallel", "parallel")),
    )(proj, kv, kv, q_gain, k_gain)


def _mem_bwd(dout, proj, kv, q_gain, k_gain, *, name):
    rows = proj.shape[0]
    tm = min(512, rows)
    steps = rows // tm
    n_mem = kv.shape[0]

    def body(d_ref, q_ref, k_ref, v_ref, qg_ref, kg_ref, dq_ref, dk_ref, dv_ref, gq_ref, gk_ref, dkn_ref):
        h, i = pl.program_id(0), pl.program_id(1)

        @pl.when((h == 0) & (i == 0))
        def _():
            gq_ref[...] = jnp.zeros_like(gq_ref)
            gk_ref[...] = jnp.zeros_like(gk_ref)

        @pl.when(i == 0)
        def _():
            dkn_ref[...] = jnp.zeros_like(dkn_ref)
            dv_ref[...] = jnp.zeros_like(dv_ref)

        qhat, rq = _rms(q_ref[...])
        khat, rk = _rms(k_ref[...])
        qn, kn = qhat * qg_ref[...], khat * kg_ref[...]
        s = _dot_nt(qn, kn) * ATT_SCALE
        p = jnp.exp(s - jnp.max(s, axis=-1, keepdims=True))
        p = p / jnp.sum(p, axis=-1, keepdims=True)
        dout = d_ref[...]
        dp = _dot_nt(dout, v_ref[...])
        ds = p * (dp - jnp.sum(p * dp, axis=-1, keepdims=True))
        dv_ref[...] += _dot_tn(p, dout)
        dkn_ref[...] += _dot_tn(ds, qn) * ATT_SCALE
        dq, dgr = _rms_bwd(_dot(ds, kn) * ATT_SCALE, qhat, rq, qg_ref[...])
        dq_ref[...] = dq.astype(BF16)
        gq_ref[...] += jnp.sum(dgr, axis=0, keepdims=True)

        @pl.when(i == steps - 1)
        def _():
            dk, dgk = _rms_bwd(dkn_ref[...], khat, rk, kg_ref[...])
            dk_ref[...] = dk
            gk_ref[...] += jnp.sum(dgk, axis=0, keepdims=True)

    vec = pl.BlockSpec((1, HEAD), lambda h, i: (0, 0))
    memh = pl.BlockSpec((n_mem, HEAD), lambda h, i: (0, h))
    tok = pl.BlockSpec((tm, HEAD), lambda h, i: (i, h))
    gshape = jax.ShapeDtypeStruct((1, HEAD), F32)
    return pl.pallas_call(
        body, name=name, grid=(MEM_HEADS, steps),
        in_specs=[tok, pl.BlockSpec((tm, HEAD), lambda h, i: (i, C_MQ // HEAD + h)), memh,
                  pl.BlockSpec((n_mem, HEAD), lambda h, i: (0, MEM_HEADS + h)), vec, vec],
        out_specs=[tok, memh, memh, vec, vec],
        out_shape=[jax.ShapeDtypeStruct((rows, MEM_HEADS * HEAD), BF16),
                   jax.ShapeDtypeStruct((n_mem, MEM_HEADS * HEAD), F32),
                   jax.ShapeDtypeStruct((n_mem, MEM_HEADS * HEAD), F32), gshape, gshape],
        scratch_shapes=[pltpu.VMEM((n_mem, HEAD), F32)],
        compiler_params=_params(("arbitrary", "arbitrary")),
    )(dout, proj, kv, kv, q_gain, k_gain)


def _branch_fwd(o_hg, o_da, o_mem, proj, wp_hg, wp_da, wp_mem, *, name):
    rows = o_hg.shape[0]
    tm = min(256, rows)

    def body(a_ref, b_ref, c_ref, ga_ref, gb_ref, gc_ref, wa_ref, wb_ref, wc_ref, o_ref):
        merged = _sigmoid(ga_ref[...]) * _dot(a_ref[...], wa_ref[...])
        merged += _sigmoid(gb_ref[...]) * _dot(b_ref[...], wb_ref[...])
        merged += _sigmoid(gc_ref[...]) * _dot(c_ref[...], wc_ref[...])
        o_ref[...] = merged.astype(BF16)

    row = lambda w: pl.BlockSpec((tm, w), lambda i: (i, 0))
    gate = lambda off: pl.BlockSpec((tm, D_MODEL), lambda i: (i, off // D_MODEL))
    full = lambda a: pl.BlockSpec(a.shape, lambda i: (0, 0))
    return pl.pallas_call(
        body, name=name, grid=(rows // tm,),
        in_specs=[row(o_hg.shape[1]), row(o_da.shape[1]), row(o_mem.shape[1]),
                  gate(C_GHG), gate(C_GDA), gate(C_GMEM), full(wp_hg), full(wp_da), full(wp_mem)],
        out_specs=row(D_MODEL), out_shape=jax.ShapeDtypeStruct((rows, D_MODEL), BF16),
        compiler_params=_params(("parallel",)),
    )(o_hg, o_da, o_mem, proj, proj, proj, wp_hg, wp_da, wp_mem)


def _branch_bwd(dm, o_hg, o_da, o_mem, proj, wp_hg, wp_da, wp_mem, *, name):
    rows = o_hg.shape[0]
    tm = min(256, rows)

    def body(dm_ref, a_ref, b_ref, c_ref, ga_ref, gb_ref, gc_ref, wa_ref, wb_ref, wc_ref, *outs):
        dmv = dm_ref[...]
        for j, (o_ref, g_ref, w_ref) in enumerate(((a_ref, ga_ref, wa_ref), (b_ref, gb_ref, wb_ref),
                                                   (c_ref, gc_ref, wc_ref))):
            z = _dot(o_ref[...], w_ref[...])
            gs = _sigmoid(g_ref[...])
            dz = (dmv * gs).astype(BF16)
            outs[3 * j][...] = (dmv * z * gs * (1.0 - gs)).astype(BF16)
            outs[3 * j + 1][...] = dz
            outs[3 * j + 2][...] = _dot_nt(dz, w_ref[...])

    row = lambda w: pl.BlockSpec((tm, w), lambda i: (i, 0))
    gate = lambda off: pl.BlockSpec((tm, D_MODEL), lambda i: (i, off // D_MODEL))
    full = lambda a: pl.BlockSpec(a.shape, lambda i: (0, 0))
    out_specs, out_shape = [], []
    for o in (o_hg, o_da, o_mem):
        out_specs += [row(D_MODEL), row(D_MODEL), row(o.shape[1])]
        out_shape += [jax.ShapeDtypeStruct((rows, D_MODEL), BF16), jax.ShapeDtypeStruct((rows, D_MODEL), BF16),
                      jax.ShapeDtypeStruct((rows, o.shape[1]), F32)]
    return pl.pallas_call(
        body, name=name, grid=(rows // tm,),
        in_specs=[row(D_MODEL), row(o_hg.shape[1]), row(o_da.shape[1]), row(o_mem.shape[1]),
                  gate(C_GHG), gate(C_GDA), gate(C_GMEM), full(wp_hg), full(wp_da), full(wp_mem)],
        out_specs=out_specs, out_shape=out_shape,
        compiler_params=_params(("parallel",)),
    )(dm, o_hg, o_da, o_mem, proj, proj, proj, wp_hg, wp_da, wp_mem)


def _ffn_in(h2, w_ab, *, name):
    rows, dff = h2.shape[0], w_ab.shape[1] // 2
    tm, tn = min(512, rows), 256

    def body(h_ref, wa_ref, wb_ref, a_ref, b_ref, u_ref):
        a = _dot(h_ref[...], wa_ref[...])
        b = _dot(h_ref[...], wb_ref[...])
        a_ref[...] = a
        b_ref[...] = b
        u_ref[...] = (a * _sigmoid(a) * b).astype(BF16)

    out = pl.BlockSpec((tm, tn), lambda i, j: (i, j))
    return pl.pallas_call(
        body, name=name, grid=(rows // tm, dff // tn),
        in_specs=[pl.BlockSpec((tm, D_MODEL), lambda i, j: (i, 0)),
                  pl.BlockSpec((D_MODEL, tn), lambda i, j: (0, j)),
                  pl.BlockSpec((D_MODEL, tn), lambda i, j: (0, dff // tn + j))],
        out_specs=[out, out, out],
        out_shape=[jax.ShapeDtypeStruct((rows, dff), F32), jax.ShapeDtypeStruct((rows, dff), F32),
                   jax.ShapeDtypeStruct((rows, dff), BF16)],
        compiler_params=_params(("parallel", "parallel")),
    )(h2, w_ab, w_ab)


def _ffn_act_bwd(dy, w_out, a, b, *, name):
    rows, dff = a.shape
    tm, tn = min(512, rows), 256

    def body(dy_ref, w_ref, a_ref, b_ref, da_ref, db_ref):
        du = _dot_nt(dy_ref[...], w_ref[...])
        av, bv = a_ref[...], b_ref[...]
        sa = _sigmoid(av)
        da_ref[...] = (du * bv * sa * (1.0 + av * (1.0 - sa))).astype(BF16)
        db_ref[...] = (du * av * sa).astype(BF16)

    tile = pl.BlockSpec((tm, tn), lambda i, j: (i, j))
    return pl.pallas_call(
        body, name=name, grid=(rows // tm, dff // tn),
        in_specs=[pl.BlockSpec((tm, D_MODEL), lambda i, j: (i, 0)),
                  pl.BlockSpec((tn, D_MODEL), lambda i, j: (j, 0)), tile, tile],
        out_specs=[tile, tile],
        out_shape=[jax.ShapeDtypeStruct((rows, dff), BF16), jax.ShapeDtypeStruct((rows, dff), BF16)],
        compiler_params=_params(("parallel", "parallel")),
    )(dy, w_out, a, b)


def _lower_bound(lb_fw, lb_bw, *, name):
    def body(a_ref, b_ref, oa_ref, ob_ref):
        for src, dst in ((a_ref, oa_ref), (b_ref, ob_ref)):
            dst[...] = _sigmoid(src[0:1, :] - src[1:2, :])

    shape = jax.ShapeDtypeStruct((1, lb_fw.shape[1]), F32)
    return pl.pallas_call(body, name=name, out_shape=[shape, shape])(lb_fw, lb_bw)


def _local_step(x, mem, tgt, p, w):
    rows = x.shape[0]
    lb_fw, lb_bw = _lower_bound(p["lb_logits_fw"], p["lb_logits_bw"], name="lower_bound")

    h = _rms_fwd(x, p["norm_mix_gain"], name="norm_mix")
    proj = _matmul([(h, w["w_in"])], "nn", F32, tm=512, tn=512, name="proj_in")
    o_fw, st_fw = _gla_fwd(proj, lb_fw, f_off=C_FF, rev=False, name="gla_fwd_fw")
    o_bw, st_bw = _gla_fwd(proj, lb_bw, f_off=C_FB, rev=True, name="gla_fwd_bw")
    o_hg = _hg_out_fwd(o_fw, o_bw, proj, p["hg_norm_gain"], name="hg_out")

    qkv_r, outs, lses = [], [], []
    for g in range(len(DA_GROUPS)):
        qr, kr, vr = _da_prep(proj, p["da_q_gain"], p["da_k_gain"], g, name=f"da_prep{g}")
        og, lg = _band_fwd(qr, kr, vr, g, name=f"band_fwd{g}")
        qkv_r.append((qr, kr, vr))
        outs.append(og)
        lses.append(lg)
    o_da, lse_da = _da_merge(outs, lses, rows, name="da_merge")

    mem_n = _rms_fwd(mem, p["norm_mem_gain"], name="norm_mem")
    kv = _matmul([(mem_n, w["w_mem_kv"])], "nn", F32, tm=256, tn=512, name="mem_kv")
    o_mem = _mem_fwd(proj, kv, p["mem_q_gain"], p["mem_k_gain"], name="mem_attn")

    merged = _branch_fwd(o_hg, o_da, o_mem, proj, w["w_proj_hg"], w["w_proj_da"], w["w_proj_mem"],
                         name="branch_merge")
    x1 = _matmul([(merged, w["w_out"])], "nn", F32, tm=512, tn=512, residual=x, name="mix_out")
    h2 = _rms_fwd(x1, p["norm_ffn_gain"], name="norm_ffn")
    a, b, u = _ffn_in(h2, w["w_ffn_in"], name="ffn_in")
    y = _matmul([(u, w["w_ffn_out"])], "nn", F32, tm=512, tn=512, residual=x1, name="ffn_out")
    dy, dy_b, loss = _loss_grad(y, tgt, name="loss")

    gw, gs = {}, {}
    gw["w_ffn_out"] = _matmul([(u, dy_b)], "tn", F32, tm=256, tn=512, name="g_ffn_out")
    da, db = _ffn_act_bwd(dy_b, w["w_ffn_out"], a, b, name="ffn_act_bwd")
    gw["w_ffn_a"] = _matmul([(h2, da)], "tn", F32, tm=512, tn=256, name="g_ffn_a")
    gw["w_ffn_b"] = _matmul([(h2, db)], "tn", F32, tm=512, tn=256, name="g_ffn_b")
    dh2 = _matmul([(da, w["w_ffn_in"], 0), (db, w["w_ffn_in"], 1)], "nt", F32, tm=256, tn=512, name="d_h2")
    dx1, dx1_b, gs["norm_ffn_gain"] = _rms_bwd_rows(dh2, x1, p["norm_ffn_gain"], dy, name="norm_ffn_bwd")
    gw["w_out"] = _matmul([(merged, dx1_b)], "tn", F32, tm=512, tn=512, name="g_out")
    dmerged = _matmul([(dx1_b, w["w_out"])], "nt", F32, tm=512, tn=512, name="d_merged")
    (dg_hg, dz_hg, do_hg, dg_da, dz_da, do_da, dg_mem, dz_mem, do_mem) = _branch_bwd(
        dmerged, o_hg, o_da, o_mem, proj, w["w_proj_hg"], w["w_proj_da"], w["w_proj_mem"], name="branch_bwd")
    gw["w_proj_hg"] = _matmul([(o_hg, dz_hg)], "tn", F32, tm=512, tn=512, name="g_proj_hg")
    gw["w_proj_da"] = _matmul([(o_da, dz_da)], "tn", F32, tm=512, tn=512, name="g_proj_da")
    gw["w_proj_mem"] = _matmul([(o_mem, dz_mem)], "tn", F32, tm=512, tn=512, name="g_proj_mem")

    dq_mem, dk_mem, dv_mem, gs["mem_q_gain"], gs["mem_k_gain"] = _mem_bwd(
        do_mem, proj, kv, p["mem_q_gain"], p["mem_k_gain"], name="mem_attn_bwd")
    dkv = jnp.concatenate([dk_mem, dv_mem], axis=1).astype(BF16)
    gw["w_mem_kv"] = _matmul([(mem_n, dkv)], "tn", F32, tm=512, tn=512, name="g_mem_kv")
    dmem_n = _matmul([(dkv, w["w_mem_kv"])], "nt", F32, tm=256, tn=512, name="d_mem_n")
    _, _, gs["norm_mem_gain"] = _rms_bwd_rows(dmem_n, mem, p["norm_mem_gain"], None, name="norm_mem_bwd")

    prep = _da_bwd_prep(do_da, o_da, lse_da, name="da_bwd_prep")
    d_da, gq_parts, gk_parts = [], [], []
    for g in range(len(DA_GROUPS)):
        qr, kr, vr = qkv_r[g]
        dor, lser, deltar = prep[3 * g:3 * g + 3]
        dqr, dkr, dvr = _band_bwd(qr, kr, vr, dor, lser, deltar, g, name=f"band_bwd{g}")
        dq, dk, dv, gq, gk = _da_prep_bwd(dqr, dkr, dvr, proj, p["da_q_gain"], p["da_k_gain"], g,
                                          name=f"da_prep_bwd{g}")
        d_da.append((dq, dk, dv))
        gq_parts.append(gq)
        gk_parts.append(gk)

    dg_hgate, do_gla, gs["hg_norm_gain"] = _hg_out_bwd(do_hg, o_fw, o_bw, proj, p["hg_norm_gain"], name="hg_out_bwd")
    dq_f, dfl_fw, dv_f, dlb_fw = _gla_bwd(proj, lb_fw, do_gla, st_fw, None, f_off=C_FF, rev=False, name="gla_bwd_fw")
    dq_hg, dfl_bw, dv_hg, dlb_bw = _gla_bwd(proj, lb_bw, do_gla, st_bw, (dq_f, dv_f), f_off=C_FB, rev=True,
                                            name="gla_bwd_bw")

    dproj = jnp.concatenate(
        [dq_hg, dfl_fw, dfl_bw, dv_hg, dg_hgate]
        + [t[0] for t in d_da] + [t[1] for t in d_da] + [t[2] for t in d_da]
        + [dq_mem, dg_hg, dg_da, dg_mem], axis=1)
    gw["w_in"] = _matmul([(h, dproj)], "tn", F32, tm=512, tn=512, name="g_in")
    dh = _matmul([(dproj, w["w_in"])], "nt", F32, tm=512, tn=512, tk=1024, name="d_h")
    grad_x, _, gs["norm_mix_gain"] = _rms_bwd_rows(dh, x, p["norm_mix_gain"], dx1, name="norm_mix_bwd")

    small = _small_pack(gs, gq_parts, gk_parts, dlb_fw, dlb_bw, lb_fw, lb_bw, name="small_pack")
    return loss, grad_x, gw, small


def _small_pack(gs, gq_parts, gk_parts, dlb_fw, dlb_bw, lb_fw, lb_bw, *, name):
    def body(g_mix, g_mem, g_ffn, dfw, dbw, lfw, lbw, g_hg, q0, q1, q2, k0, k1, k2, g_mq, g_mk, o_ref):
        o_ref[0:1, :] = g_mix[...]
        o_ref[1:2, :] = g_mem[...]
        o_ref[2:3, :] = g_ffn[...]
        for base, d_ref, l_ref in ((3, dfw, lfw), (5, dbw, lbw)):
            lbv = l_ref[...]
            t = d_ref[...] * lbv * (1.0 - lbv)
            o_ref[base:base + 1, :] = t
            o_ref[base + 1:base + 2, :] = -t
        o_ref[7:8, :] = jnp.zeros((1, D_MODEL), F32)
        o_ref[7:8, 0:HEAD] = g_hg[...]
        o_ref[7:8, HEAD:2 * HEAD] = q0[...] + q1[...] + q2[...]
        o_ref[7:8, 2 * HEAD:3 * HEAD] = k0[...] + k1[...] + k2[...]
        o_ref[7:8, 3 * HEAD:4 * HEAD] = g_mq[...]
        o_ref[7:8, 4 * HEAD:5 * HEAD] = g_mk[...]

    return pl.pallas_call(body, name=name, out_shape=jax.ShapeDtypeStruct((8, D_MODEL), F32))(
        gs["norm_mix_gain"], gs["norm_mem_gain"], gs["norm_ffn_gain"], dlb_fw, dlb_bw, lb_fw, lb_bw,
        gs["hg_norm_gain"], *gq_parts, *gk_parts, gs["mem_q_gain"], gs["mem_k_gain"])


def _row_tile(rows, cols, n_arrays):
    budget = (16 * 1024 * 1024) // (2 * 4 * cols * n_arrays)
    tr = rows
    while tr > budget and tr % 2 == 0 and (tr // 2) % 16 == 0:
        tr //= 2
    return tr


def _cast_into_full(a, chip, rows, cols, axis, *, name):
    sr, sc = a.shape
    tr = _row_tile(sr, sc, 2)

    def body(chip_ref, a_ref, o_ref):
        del chip_ref
        o_ref[...] = a_ref[...].astype(BF16)

    if axis == 1:
        out_map = lambda i, chip_ref: (i, chip_ref[0])
    else:
        out_map = lambda i, chip_ref: (chip_ref[0] * (sr // tr) + i, 0)
    return pl.pallas_call(
        body, name=name,
        grid_spec=pltpu.PrefetchScalarGridSpec(
            num_scalar_prefetch=1, grid=(sr // tr,),
            in_specs=[pl.BlockSpec((tr, sc), lambda i, chip_ref: (i, 0))],
            out_specs=pl.BlockSpec((tr, sc), out_map)),
        out_shape=jax.ShapeDtypeStruct((rows, cols), BF16),
        compiler_params=_params(("parallel",)))(chip, a)


def _add_halves(items, *, name):
    rows = items[0][3].shape[0]
    widths = [ra.shape[1] for _, _, _, ra in items]
    tr = _row_tile(rows, sum(widths), 4)

    def body(*refs):
        o_ref = refs[-1]
        first = lax.axis_index("c") == 0
        off = 0
        for j, wd in enumerate(widths):
            h0, h1, ra = refs[3 * j:3 * j + 3]
            o_ref[:, off:off + wd] = (jnp.where(first, h0[...], h1[...]) + ra[...]).astype(BF16)
            off += wd

    in_specs, ins = [], []
    for (g, haxis, hsize, ra), wd in zip(items, widths):
        if haxis == 0:
            in_specs += [pl.BlockSpec((tr, wd), lambda i: (i, 0)),
                         pl.BlockSpec((tr, wd), lambda i, o=hsize // tr: (o + i, 0))]
        else:
            in_specs += [pl.BlockSpec((tr, wd), lambda i: (i, 0)), pl.BlockSpec((tr, wd), lambda i: (i, 1))]
        in_specs.append(pl.BlockSpec((tr, wd), lambda i: (i, 0)))
        ins += [g, g, ra]
    return pl.pallas_call(body, name=name, grid=(rows // tr,), in_specs=in_specs,
                          out_specs=pl.BlockSpec((tr, sum(widths)), lambda i: (i, 0)),
                          out_shape=jax.ShapeDtypeStruct((rows, sum(widths)), BF16),
                          compiler_params=_params(("parallel",)))(*ins)


def _add_slots(rb, *, name):
    _, rows, cols = rb.shape
    tr = _row_tile(rows, cols, 5)

    def body(r0, r1, r2, r3, o_ref):
        o_ref[...] = ((r0[...].astype(F32) + r1[...].astype(F32)) + r2[...].astype(F32)) + r3[...].astype(F32)

    slot = lambda s: pl.BlockSpec((None, tr, cols), lambda i: (s, i, 0))
    return pl.pallas_call(body, name=name, grid=(rows // tr,), in_specs=[slot(s) for s in range(4)],
                          out_specs=pl.BlockSpec((tr, cols), lambda i: (i, 0)),
                          out_shape=jax.ShapeDtypeStruct((rows, cols), F32),
                          compiler_params=_params(("parallel",)))(rb, rb, rb, rb)


def _adamw(w, g, m, v, *, name):
    rows, cols = w.shape
    tr = _row_tile(rows, cols, 7) if rows % 16 == 0 else rows
    c1 = 1.0 - ADAM_B1 ** ADAM_STEP
    c2 = 1.0 - ADAM_B2 ** ADAM_STEP

    def body(w_ref, g_ref, m_ref, v_ref, d_ref, mo_ref, vo_ref):
        gv = g_ref[...]
        mn = ADAM_B1 * m_ref[...] + (1.0 - ADAM_B1) * gv
        vn = ADAM_B2 * v_ref[...] + (1.0 - ADAM_B2) * (gv * gv)
        mo_ref[...] = mn
        vo_ref[...] = vn
        d_ref[...] = -ADAM_LR * ((mn / c1) / (jnp.sqrt(vn / c2) + ADAM_EPS) + ADAM_WD * w_ref[...])

    spec = pl.BlockSpec((tr, cols), lambda i: (i, 0))
    shape = jax.ShapeDtypeStruct((rows, cols), F32)
    return pl.pallas_call(body, name=name, grid=(rows // tr,), in_specs=[spec] * 4, out_specs=[spec] * 3,
                          out_shape=[shape] * 3, compiler_params=_params(("parallel",)))(w, g, m, v)


W_SPECS = (
    ("w_in", 1024, IN_COLS, 1, IN_COLS // 4),
    ("w_mem_kv", 1024, 1024, 0, 256),
    ("w_proj_hg", 1024, 1024, 0, 256),
    ("w_proj_da", 512, 1024, 1, 256),
    ("w_proj_mem", 512, 1024, 1, 256),
    ("w_out", 1024, 1024, 0, 256),
    ("w_ffn_in", 1024, 2 * D_FF, 1, 2 * D_FF // 4),
    ("w_ffn_out", D_FF, 1024, 0, D_FF // 4),
)
CHIP_FLIPS = ((1, 0), (0, 1), (1, 1))
ANY = pl.BlockSpec(memory_space=pl.ANY)
DMA_CHUNK_BYTES = 1 << 20
STAGE_BYTES = 2 << 20


def _place():
    x, y, c = lax.axis_index("x"), lax.axis_index("y"), lax.axis_index("c")
    return x, y, c, 2 * x + y


def _flip(v, f):
    return 1 - v if f else v


def _slab(ref, axis, idx, size):
    start = pl.multiple_of(idx * size, size)
    return ref.at[pl.ds(start, size), :] if axis == 0 else ref.at[:, pl.ds(start, size)]


def _chunked(make, src, dst):
    rows, cols = src.shape
    row_bytes = cols * jnp.dtype(src.dtype).itemsize
    k = 1
    while rows % (2 * k) == 0 and (rows // (2 * k)) % 16 == 0 and (rows // k) * row_bytes > DMA_CHUNK_BYTES:
        k *= 2
    cr = rows // k
    parts = [make(src.at[pl.ds(j * cr, cr), :], dst.at[pl.ds(j * cr, cr), :]) for j in range(k)]
    return parts, make(src, dst)


def _run_copies(local, remote):
    for parts, _ in local + remote:
        for cp in parts:
            cp.start()
    for _, whole in remote:
        whole.wait_recv()
    for _, whole in remote:
        whole.wait_send()
    for _, whole in local:
        whole.wait()


def _half_spec(rows, cols, axis):
    return (0, rows // 2) if axis == 1 else (1, cols // 2)


def _staged(src, remote_dst, local_dst, sibling, load_sems, send_sems, store_sems, recv_sem):
    rows, cols = src.shape
    row_bytes = cols * jnp.dtype(src.dtype).itemsize
    k = 1
    while (rows // k) * row_bytes > STAGE_BYTES and rows % (2 * k) == 0 and (rows // (2 * k)) % 16 == 0:
        k *= 2
    cr = rows // k
    piece = lambda ref, j: ref.at[pl.ds(j * cr, cr), :]

    def run(buf):
        loads = [pltpu.make_async_copy(piece(src, j), buf.at[j % 2], load_sems.at[j % 2]) for j in range(k)]
        outs = [[pltpu.make_async_remote_copy(src_ref=buf.at[j % 2], dst_ref=piece(remote_dst, j),
                                              send_sem=send_sems.at[j % 2], recv_sem=recv_sem,
                                              device_id=sibling, device_id_type=MESH)] for j in range(k)]
        if local_dst is not None:
            for j in range(k):
                outs[j].append(pltpu.make_async_copy(buf.at[j % 2], piece(local_dst, j), store_sems.at[j % 2]))

        def drained(j):
            outs[j][0].wait_send()
            for cp in outs[j][1:]:
                cp.wait()

        loads[0].start()
        for j in range(k):
            loads[j].wait()
            for cp in outs[j]:
                cp.start()
            if j + 1 < k:
                if j >= 1:
                    drained(j - 1)
                loads[j + 1].start()
        for j in range(max(0, k - 2), k):
            drained(j)

    pl.run_scoped(run, pltpu.VMEM((2, cr, cols), src.dtype))


def _landed(ref, recv_sem, send_sem):
    pltpu.make_async_remote_copy(src_ref=ref, dst_ref=ref, send_sem=send_sem, recv_sem=recv_sem,
                                 device_id=(lax.axis_index("x"), lax.axis_index("y"), lax.axis_index("c")),
                                 device_id_type=MESH).wait_recv()


def _gather_weights(fulls):
    n = len(W_SPECS)

    def body(*refs):
        outs = refs[n:2 * n]
        ici_send, ici_recv, load_sems, d2d_send, d2d_recv = refs[2 * n:]
        x, y, c, p = _place()

        def half_slab(wi, chip, half):
            _, rows, cols, axis, size = W_SPECS[wi]
            haxis, hsize = _half_spec(rows, cols, axis)
            return _slab(_slab(outs[wi], axis, chip, size), haxis, half, hsize)

        sent = []
        for wi in range(n):
            mine = half_slab(wi, p, c)
            for k, (fx, fy) in enumerate(CHIP_FLIPS):
                sent.append(_chunked(lambda s, d, j=3 * wi + k, fx=fx, fy=fy: pltpu.make_async_remote_copy(
                    src_ref=s, dst_ref=d, send_sem=ici_send.at[j], recv_sem=ici_recv.at[j],
                    device_id=(_flip(x, fx), _flip(y, fy), c), device_id_type=MESH), mine, mine))
        for parts, _ in sent:
            for cp in parts:
                cp.start()
        for k, (fx, fy) in enumerate(CHIP_FLIPS):
            q = 2 * _flip(x, fx) + _flip(y, fy)
            for wi in range(n):
                sent[3 * wi + k][1].wait_recv()
                got = half_slab(wi, q, c)
                _staged(got, got, None, (x, y, 1 - c), load_sems, d2d_send, None, d2d_recv.at[3 * wi + k])
        for _, whole in sent:
            whole.wait_send()
        for k, (fx, fy) in enumerate(CHIP_FLIPS):
            q = 2 * _flip(x, fx) + _flip(y, fy)
            for wi in range(n):
                _landed(half_slab(wi, q, 1 - c), d2d_recv.at[3 * wi + k], d2d_send.at[0])

    return pl.pallas_call(
        body, name="gather_weights", in_specs=[ANY] * n, out_specs=[ANY] * n,
        out_shape=[jax.ShapeDtypeStruct(f.shape, f.dtype) for f in fulls],
        input_output_aliases={i: i for i in range(n)},
        scratch_shapes=[pltpu.SemaphoreType.DMA((3 * n,)), pltpu.SemaphoreType.DMA((3 * n,)),
                        pltpu.SemaphoreType.DMA((2,)), pltpu.SemaphoreType.DMA((2,)),
                        pltpu.SemaphoreType.DMA((3 * n,))],
    )(*fulls)


def _sibling_exchange(grads):
    n = len(grads)

    def body(*refs):
        ins, outs = refs[:n], refs[n:2 * n]
        load_sems, send_sems, recv_sems = refs[2 * n:]
        x, y, c, _ = _place()
        for i, (_, haxis, hsize) in enumerate(grads):
            _staged(_slab(ins[i], haxis, 1 - c, hsize), outs[i], None, (x, y, 1 - c),
                    load_sems, send_sems, None, recv_sems.at[i])
        for i in range(n):
            _landed(outs[i], recv_sems.at[i], send_sems.at[0])

    shapes = [jax.ShapeDtypeStruct((hsize, g.shape[1]) if haxis == 0 else (g.shape[0], hsize), F32)
              for g, haxis, hsize in grads]
    return pl.pallas_call(
        body, name="grad_sibling_exchange", in_specs=[ANY] * n, out_specs=[ANY] * n, out_shape=shapes,
        scratch_shapes=[pltpu.SemaphoreType.DMA((2,)), pltpu.SemaphoreType.DMA((2,)),
                        pltpu.SemaphoreType.DMA((n,))],
    )(*[g for g, _, _ in grads])


def _chip_exchange(parts):
    n = len(parts)

    def body(*refs):
        ins, outs = refs[:n], refs[n:2 * n]
        send_sems, recv_sems, local_sems = refs[2 * n:]
        x, y, c, p = _place()
        local, remote = [], []
        for i, (_, axis, size) in enumerate(parts):
            local.append(_chunked(lambda s, d, i=i: pltpu.make_async_copy(s, d, local_sems.at[i]),
                                  _slab(ins[i], axis, p, size), outs[i].at[p]))
            for k, (fx, fy) in enumerate(CHIP_FLIPS):
                px, py = _flip(x, fx), _flip(y, fy)
                remote.append(_chunked(lambda s, d, j=3 * i + k, px=px, py=py: pltpu.make_async_remote_copy(
                    src_ref=s, dst_ref=d, send_sem=send_sems.at[j], recv_sem=recv_sems.at[j],
                    device_id=(px, py, c), device_id_type=MESH), _slab(ins[i], axis, 2 * px + py, size), outs[i].at[p]))
        _run_copies(local, remote)

    shapes = []
    for a, axis, size in parts:
        shapes.append(jax.ShapeDtypeStruct((4, size, a.shape[1]) if axis == 0 else (4, a.shape[0], size), a.dtype))
    return pl.pallas_call(
        body, name="grad_chip_exchange", in_specs=[ANY] * n, out_specs=[ANY] * n, out_shape=shapes,
        scratch_shapes=[pltpu.SemaphoreType.DMA((3 * n,)), pltpu.SemaphoreType.DMA((3 * n,)),
                        pltpu.SemaphoreType.DMA((n,))],
    )(*[a for a, _, _ in parts])


def _sibling_share(sums):
    n = len(sums)

    def body(*refs):
        ins, outs = refs[:n], refs[n:2 * n]
        load_sems, send_sems, store_sems, recv_sems = refs[2 * n:]
        x, y, c, _ = _place()
        for i, (s, haxis) in enumerate(sums):
            place = _slab(outs[i], haxis, c, s.shape[haxis])
            _staged(ins[i], place, place, (x, y, 1 - c), load_sems, send_sems, store_sems, recv_sems.at[i])
        for i, (s, haxis) in enumerate(sums):
            _landed(_slab(outs[i], haxis, 1 - c, s.shape[haxis]), recv_sems.at[i], send_sems.at[0])

    shapes = []
    for s, haxis in sums:
        r, cc = s.shape
        shapes.append(jax.ShapeDtypeStruct((2 * r, cc) if haxis == 0 else (r, 2 * cc), F32))
    return pl.pallas_call(
        body, name="grad_sibling_share", in_specs=[ANY] * n, out_specs=[ANY] * n, out_shape=shapes,
        scratch_shapes=[pltpu.SemaphoreType.DMA((2,)), pltpu.SemaphoreType.DMA((2,)),
                        pltpu.SemaphoreType.DMA((2,)), pltpu.SemaphoreType.DMA((n,))],
    )(*[s for s, _ in sums])


def _reduce_scatter(gw):
    grads = []
    for name, _, _, axis, _ in W_SPECS:
        for g in ((gw["w_ffn_a"], gw["w_ffn_b"]) if name == "w_ffn_in" else (gw[name],)):
            grads.append((g,) + _half_spec(g.shape[0], g.shape[1], axis))
    theirs = _sibling_exchange(grads)
    parts, j = [], 0
    for name, _, _, axis, size in W_SPECS:
        take = 2 if name == "w_ffn_in" else 1
        items = [grads[i] + (theirs[i],) for i in range(j, j + take)]
        parts.append((_add_halves(items, name=f"half_sum_{name}"), axis, size))
        j += take
    slots = _chip_exchange(parts)
    sums = []
    for (name, rows, cols, axis, _), rb in zip(W_SPECS, slots):
        sums.append((_add_slots(rb, name=f"chip_sum_{name}"), _half_spec(rows, cols, axis)[0]))
    return dict(zip([s[0] for s in W_SPECS], _sibling_share(sums)))


def _small_allreduce(sv):
    def body(sv_ref, o_ref, slots_ref, send_sems, recv_sems):
        x, y, c, _ = _place()
        me = 4 * x + 2 * y + c
        slots_ref[me] = sv_ref[...]
        copies = []
        for k in range(1, 8):
            fx, fy, fc = (k >> 2) & 1, (k >> 1) & 1, k & 1
            copies.append(pltpu.make_async_remote_copy(
                src_ref=sv_ref, dst_ref=slots_ref.at[me], send_sem=send_sems.at[k - 1],
                recv_sem=recv_sems.at[k - 1], device_id=(_flip(x, fx), _flip(y, fy), _flip(c, fc)),
                device_id_type=MESH))
        for cp in copies:
            cp.start()
        for cp in copies:
            cp.wait_recv()
        for cp in copies:
            cp.wait_send()
        total = slots_ref[0]
        for s in range(1, 8):
            total = total + slots_ref[s]
        o_ref[...] = total

    vm = pl.BlockSpec(memory_space=pltpu.VMEM)
    return pl.pallas_call(
        body, name="small_allreduce", in_specs=[vm], out_specs=vm,
        out_shape=jax.ShapeDtypeStruct(sv.shape, F32),
        scratch_shapes=[pltpu.VMEM((8,) + sv.shape, F32), pltpu.SemaphoreType.DMA((7,)),
                        pltpu.SemaphoreType.DMA((7,))],
    )(sv)


SMALL_ROWS = (("norm_mix_gain", 0), ("norm_mem_gain", 1), ("norm_ffn_gain", 2))
SMALL_LB = (("lb_logits_fw", 3), ("lb_logits_bw", 5))
SMALL_HEAD = ("hg_norm_gain", "da_q_gain", "da_k_gain", "mem_q_gain", "mem_k_gain")


def _pack_small(d):
    last = jnp.concatenate([d[n] for n in SMALL_HEAD] + [jnp.zeros((1, D_MODEL - HEAD * len(SMALL_HEAD)), F32)], axis=1)
    return jnp.concatenate([d["norm_mix_gain"], d["norm_mem_gain"], d["norm_ffn_gain"],
                            d["lb_logits_fw"], d["lb_logits_bw"], last], axis=0)


def _unpack_small(a):
    out = {n: a[r:r + 1] for n, r in SMALL_ROWS}
    out.update({n: a[r:r + 2] for n, r in SMALL_LB})
    out.update({n: a[7:8, j * HEAD:(j + 1) * HEAD] for j, n in enumerate(SMALL_HEAD)})
    return out


PARAM_ORDER = ("norm_mix_gain", "norm_mem_gain", "w_in", "lb_logits_fw", "lb_logits_bw", "hg_norm_gain",
               "da_q_gain", "da_k_gain", "w_mem_kv", "mem_q_gain", "mem_k_gain", "w_proj_hg", "w_proj_da",
               "w_proj_mem", "w_out", "norm_ffn_gain", "w_ffn_in", "w_ffn_out")


def kernel(x, mem, norm_mix_gain, norm_mem_gain, w_in, lb_logits_fw, lb_logits_bw, hg_norm_gain, da_q_gain, da_k_gain, w_mem_kv, mem_q_gain, mem_k_gain, w_proj_hg, w_proj_da, w_proj_mem, w_out, norm_ffn_gain, w_ffn_in, w_ffn_out, loss_target, m_norm_mix_gain, m_norm_mem_gain, m_w_in, m_lb_logits_fw, m_lb_logits_bw, m_hg_norm_gain, m_da_q_gain, m_da_k_gain, m_w_mem_kv, m_mem_q_gain, m_mem_k_gain, m_w_proj_hg, m_w_proj_da, m_w_proj_mem, m_w_out, m_norm_ffn_gain, m_w_ffn_in, m_w_ffn_out, v_norm_mix_gain, v_norm_mem_gain, v_w_in, v_lb_logits_fw, v_lb_logits_bw, v_hg_norm_gain, v_da_q_gain, v_da_k_gain, v_w_mem_kv, v_mem_q_gain, v_mem_k_gain, v_w_proj_hg, v_w_proj_da, v_w_proj_mem, v_w_out, v_norm_ffn_gain, v_w_ffn_in, v_w_ffn_out):
    args = dict(locals())
    mats = tuple(s[0] for s in W_SPECS)
    flat = lambda a: a.reshape(a.shape[-2:])
    w = {n: flat(args[n]) for n in mats}
    m = {n: flat(args["m_" + n]) for n in mats}
    v = {n: flat(args["v_" + n]) for n in mats}
    small = {n: args[n] for n in PARAM_ORDER if n not in mats}

    chip = (2 * lax.axis_index("x") + lax.axis_index("y")).astype(jnp.int32).reshape(1)
    full = _gather_weights([_cast_into_full(w[n], chip, rows, cols, axis, name=f"cast_{n}")
                            for n, rows, cols, axis, _ in W_SPECS])
    loss, grad_x, gw, small_grads = _local_step(x[0], mem[0], loss_target[0], small, dict(zip(mats, full)))
    grads = _reduce_scatter(gw)
    small_sum = _small_allreduce(small_grads)

    delta, new_m, new_v = {}, {}, {}
    for n in mats:
        delta[n], new_m[n], new_v[n] = _adamw(w[n], grads[n], m[n], v[n], name=f"adamw_{n}")
    packed = _adamw(_pack_small(small), small_sum,
                    _pack_small({n: args["m_" + n] for n in small}),
                    _pack_small({n: args["v_" + n] for n in small}), name="adamw_small")
    grads.update(_unpack_small(small_sum))
    for dst, src in zip((delta, new_m, new_v), packed):
        dst.update(_unpack_small(src))

    def shaped(d, n):
        return d[n].reshape(args[n].shape)

    loss_sum = lax.psum(loss[0, 0], ("x", "y", "c"))
    return (loss_sum, grad_x[None], *[shaped(grads, n) for n in PARAM_ORDER], *[shaped(delta, n) for n in PARAM_ORDER],
            *[shaped(new_m, n) for n in PARAM_ORDER], *[shaped(new_v, n) for n in PARAM_ORDER])
```

```python
import functools
import math

import numpy as np
import jax
import jax.numpy as jnp
from jax import lax
from jax.experimental import pallas as pl
from jax.experimental.pallas import tpu as pltpu

F32 = jnp.float32
BF16 = jnp.bfloat16
MESH = pl.DeviceIdType.MESH

D_MODEL = 1024
HEAD = 128
HG_HEADS = 8
DA_GROUPS = ((1, 64), (4, 64), (16, 64))
DA_GH = 4
MEM_HEADS = 4
N_MEM = 256
D_FF = 2816
CHUNK = 64
BAND_QBLOCKS = 4
RMS_EPS = 1e-6
NEG_INF = -1e30
HG_SCALE = HEAD ** -0.5
ATT_SCALE = HEAD ** -0.5
VMEM_LIMIT_V7X = 48 * 1024 * 1024

C_HQ, C_FF, C_FB, C_HI, C_HG = 0, 1024, 2048, 3072, 4096
C_DQ, C_DK, C_DV, C_MQ = 5120, 6656, 8192, 9728
C_GHG, C_GDA, C_GMEM = 10240, 11264, 12288
IN_COLS = 13312

ADAM_LR, ADAM_B1, ADAM_B2, ADAM_EPS, ADAM_WD, ADAM_STEP = 0.001, 0.9, 0.999, 1e-08, 0.01, 10


def _params(sem, vmem=VMEM_LIMIT_V7X):
    return pltpu.CompilerParams(dimension_semantics=sem, vmem_limit_bytes=vmem)


def _dot(a, b):
    return jnp.dot(a.astype(BF16), b.astype(BF16), preferred_element_type=F32)


def _dot_nt(a, b):
    return lax.dot_general(a.astype(BF16), b.astype(BF16), (((1,), (1,)), ((), ())),
                           preferred_element_type=F32)


def _dot_tn(a, b):
    return lax.dot_general(a.astype(BF16), b.astype(BF16), (((0,), (0,)), ((), ())),
                           preferred_element_type=F32)


def _sigmoid(v):
    return jax.nn.sigmoid(v.astype(F32))


def _rms(v):
    v = v.astype(F32)
    r = lax.rsqrt(jnp.mean(v * v, axis=-1, keepdims=True) + RMS_EPS)
    return v * r, r


def _rms_bwd(dy, xhat, r, gain):
    dxh = dy * gain
    dx = r * (dxh - xhat * jnp.mean(dxh * xhat, axis=-1, keepdims=True))
    return dx, dy * xhat


def _matmul(pairs, mode, out_dtype, *, tm, tn, tk=None, residual=None, name):
    b_offs = [pr[2] if len(pr) > 2 else 0 for pr in pairs]
    pairs = [pr[:2] for pr in pairs]
    a0, b0 = pairs[0]
    if mode == "nn":
        (m, kk), n = a0.shape, b0.shape[1]
    elif mode == "nt":
        (m, kk), n = a0.shape, b0.shape[0]
    else:
        (kk, m), n = a0.shape, b0.shape[1]
    assert mode == "nt" or not any(b_offs)
    tm, tn = min(tm, m), min(tn, n)
    tk = kk if tk is None else tk
    nk = kk // tk
    assert m % tm == 0 and n % tn == 0 and kk % tk == 0, (name, m, n, kk)
    n_p = len(pairs)
    if mode == "tn":
        a_spec = pl.BlockSpec((tk, tm), lambda i, j, k: (k, i))
    else:
        a_spec = pl.BlockSpec((tm, tk), lambda i, j, k: (i, k))
    if mode == "nt":
        b_specs = [pl.BlockSpec((tn, tk), lambda i, j, k, o=o: (j, o * nk + k)) for o in b_offs]
    else:
        b_specs = [pl.BlockSpec((tk, tn), lambda i, j, k: (k, j))] * n_p
    o_spec = pl.BlockSpec((tm, tn), lambda i, j, k: (i, j))
    dot = {"nn": _dot, "nt": _dot_nt, "tn": _dot_tn}[mode]
    has_res = residual is not None

    def body(*refs):
        a_refs, b_refs = refs[:n_p], refs[n_p:2 * n_p]
        pos = 2 * n_p
        res_ref = refs[pos] if has_res else None
        pos += int(has_res)
        o_ref = refs[pos]
        part = dot(a_refs[0][...], b_refs[0][...])
        for a_r, b_r in zip(a_refs[1:], b_refs[1:]):
            part += dot(a_r[...], b_r[...])

        def finish(total):
            if has_res:
                total = total + res_ref[...]
            o_ref[...] = total.astype(out_dtype)

        if nk == 1:
            finish(part)
        else:
            acc_ref = refs[pos + 1]
            k = pl.program_id(2)

            @pl.when(k == 0)
            def _():
                acc_ref[...] = part

            @pl.when(k > 0)
            def _():
                acc_ref[...] += part

            @pl.when(k == nk - 1)
            def _():
                finish(acc_ref[...])

    ins = [a for a, _ in pairs] + [b for _, b in pairs]
    in_specs = [a_spec] * n_p + b_specs
    if has_res:
        ins.append(residual)
        in_specs.append(o_spec)
    return pl.pallas_call(
        body, name=name, grid=(m // tm, n // tn, nk),
        in_specs=in_specs, out_specs=o_spec,
        out_shape=jax.ShapeDtypeStruct((m, n), out_dtype),
        scratch_shapes=[pltpu.VMEM((tm, tn), F32)] if nk > 1 else [],
        compiler_params=_params(("parallel", "parallel", "arbitrary")),
    )(*ins)


def _rms_fwd(x, gain, *, name):
    rows, dm = x.shape
    tm = min(512, rows)

    def body(x_ref, g_ref, h_ref):
        xhat, _ = _rms(x_ref[...])
        h_ref[...] = (xhat * g_ref[...]).astype(BF16)

    return pl.pallas_call(
        body, name=name, grid=(rows // tm,),
        in_specs=[pl.BlockSpec((tm, dm), lambda i: (i, 0)), pl.BlockSpec((1, dm), lambda i: (0, 0))],
        out_specs=pl.BlockSpec((tm, dm), lambda i: (i, 0)),
        out_shape=jax.ShapeDtypeStruct((rows, dm), BF16),
        compiler_params=_params(("parallel",)),
    )(x, gain)


def _rms_bwd_rows(dh, x, gain, dres, *, name):
    rows, dm = x.shape
    tm = min(512, rows)
    has_res = dres is not None

    def body(*refs):
        dh_ref, x_ref, g_ref = refs[:3]
        res_ref = refs[3] if has_res else None
        dx_ref, dxb_ref, dg_ref = refs[3 + int(has_res):]
        xhat, r = _rms(x_ref[...])
        dx, dgr = _rms_bwd(dh_ref[...], xhat, r, g_ref[...])
        if has_res:
            dx = dx + res_ref[...]
        dx_ref[...] = dx
        dxb_ref[...] = dx.astype(BF16)

        @pl.when(pl.program_id(0) == 0)
        def _():
            dg_ref[...] = jnp.zeros_like(dg_ref)

        dg_ref[...] += jnp.sum(dgr, axis=0, keepdims=True)

    row = pl.BlockSpec((tm, dm), lambda i: (i, 0))
    vec = pl.BlockSpec((1, dm), lambda i: (0, 0))
    return pl.pallas_call(
        body, name=name, grid=(rows // tm,),
        in_specs=[row, row, vec] + ([row] if has_res else []),
        out_specs=[row, row, vec],
        out_shape=[jax.ShapeDtypeStruct((rows, dm), F32), jax.ShapeDtypeStruct((rows, dm), BF16),
                   jax.ShapeDtypeStruct((1, dm), F32)],
        compiler_params=_params(("arbitrary",)),
    )(*([dh, x, gain] + ([dres] if has_res else [])))


def _loss_grad(y, tgt, *, name):
    rows, dm = y.shape
    tm = min(512, rows)
    steps = rows // tm

    def body(y_ref, t_ref, dy_ref, dyb_ref, loss_ref, acc_ref):
        i = pl.program_id(0)
        diff = y_ref[...] - t_ref[...]
        dy = diff * (1.0 / dm)
        dy_ref[...] = dy
        dyb_ref[...] = dy.astype(BF16)

        @pl.when(i == 0)
        def _():
            acc_ref[...] = jnp.zeros_like(acc_ref)

        acc_ref[...] += jnp.sum(diff * diff, axis=0, keepdims=True)

        @pl.when(i == steps - 1)
        def _():
            loss_ref[...] = jnp.full((1, HEAD), 0.5 / dm, F32) * jnp.sum(acc_ref[...])

    row = pl.BlockSpec((tm, dm), lambda i: (i, 0))
    return pl.pallas_call(
        body, name=name, grid=(steps,),
        in_specs=[row, row],
        out_specs=[row, row, pl.BlockSpec((1, HEAD), lambda i: (0, 0))],
        out_shape=[jax.ShapeDtypeStruct((rows, dm), F32), jax.ShapeDtypeStruct((rows, dm), BF16),
                   jax.ShapeDtypeStruct((1, HEAD), F32)],
        scratch_shapes=[pltpu.VMEM((1, dm), F32)],
        compiler_params=_params(("arbitrary",)),
    )(y, tgt)


def _gla_block_terms(q_raw, f_logit, lb, tri):
    sig = _sigmoid(f_logit)
    forget = lb + (1.0 - lb) * sig
    k = 1.0 - forget
    b = _masked_sum(tri, jnp.log(forget))
    qs = _sigmoid(q_raw)
    eb = jnp.exp(b)
    emb = jnp.exp(-b)
    qt = (q_raw * qs * HG_SCALE) * eb
    kt = k * emb
    return sig, forget, k, b, qs, eb, emb, qt, kt


def _masked_sum(mask, v):
    mb = mask.astype(BF16)
    hi = v.astype(BF16)
    r1 = v - hi.astype(F32)
    mid = r1.astype(BF16)
    lo = (r1 - mid.astype(F32)).astype(BF16)
    dot = lambda t: jnp.dot(mb, t, preferred_element_type=F32)
    return dot(hi) + dot(mid) + dot(lo)


def _tri_mask(n, rev):
    row = lax.broadcasted_iota(jnp.int32, (n, n), 0)
    col = lax.broadcasted_iota(jnp.int32, (n, n), 1)
    shift = CHUNK.bit_length() - 1
    same = jnp.right_shift(row, shift) == jnp.right_shift(col, shift)
    return same & ((row <= col) if rev else (row >= col))


def _chunk_order(ncb, rev):
    order = range(ncb - 1, -1, -1) if rev else range(ncb)
    return [(c, c * CHUNK if rev else c * CHUNK + CHUNK - 1) for c in order]


def _gla_fwd(proj, lb, *, f_off, rev, name):
    rows = proj.shape[0]
    tb = min(256, rows)
    nb, ncb = rows // tb, tb // CHUNK

    def tmap(n):
        return nb - 1 - n if rev else n

    def body(q_ref, f_ref, v_ref, lb_ref, o_ref, st_ref, s_ref):
        @pl.when(pl.program_id(1) == 0)
        def _():
            s_ref[...] = jnp.zeros_like(s_ref)

        tri = _tri_mask(tb, rev)
        v = v_ref[...]
        _, _, k, b, _, _, _, qt, kt = _gla_block_terms(q_ref[...].astype(F32), f_ref[...].astype(F32),
                                                       lb_ref[...], tri)
        o_intra = _dot(jnp.where(tri, _dot_nt(qt, kt), 0.0), v)
        chunks = []
        for c, last in _chunk_order(ncb, rev):
            sl = slice(c * CHUNK, (c + 1) * CHUNK)
            bl = b[last:last + 1, :]
            kh = k[sl] * jnp.exp(bl - b[sl])
            chunks.append((c, sl, jnp.exp(bl), _dot_tn(v[sl], kh)))
        s_t = s_ref[...]
        for c, sl, ebl, kv in chunks:
            st_ref[c] = s_t
            o_ref[sl, :] = o_intra[sl] + _dot_nt(qt[sl], s_t)
            s_t = ebl * s_t + kv
        s_ref[...] = s_t

    col = lambda off: pl.BlockSpec((tb, HEAD), lambda h, n: (tmap(n), off // HEAD + h))
    return pl.pallas_call(
        body, name=name, grid=(HG_HEADS, nb),
        in_specs=[col(C_HQ), col(f_off), col(C_HI), pl.BlockSpec((1, HEAD), lambda h, n: (0, h))],
        out_specs=[pl.BlockSpec((tb, HEAD), lambda h, n: (tmap(n), h)),
                   pl.BlockSpec((None, ncb, HEAD, HEAD), lambda h, n: (h, tmap(n), 0, 0))],
        out_shape=[jax.ShapeDtypeStruct((rows, HG_HEADS * HEAD), F32),
                   jax.ShapeDtypeStruct((HG_HEADS, rows // CHUNK, HEAD, HEAD), F32)],
        scratch_shapes=[pltpu.VMEM((HEAD, HEAD), F32)],
        compiler_params=_params(("parallel", "arbitrary")),
    )(proj, proj, proj, lb)


def _gla_bwd(proj, lb, do, states, prev, *, f_off, rev, name):
    rows = proj.shape[0]
    tb = min(256, rows)
    nb, ncb = rows // tb, tb // CHUNK
    has_prev = prev is not None
    qv_dtype = BF16 if has_prev else F32

    def tmap(n):
        return n if rev else nb - 1 - n

    def body(*refs):
        q_ref, f_ref, v_ref, lb_ref, do_ref, st_ref = refs[:6]
        pq_ref, pv_ref = refs[6:8] if has_prev else (None, None)
        (dq_ref, df_ref, dv_ref, dlb_ref,
         ds_ref, dqt_scr, dk_scr, db_scr, dbl_scr, dv_scr) = refs[6 + 2 * int(has_prev):]

        @pl.when(pl.program_id(1) == 0)
        def _():
            ds_ref[...] = jnp.zeros_like(ds_ref)
            dlb_ref[...] = jnp.zeros_like(dlb_ref)

        tri = _tri_mask(tb, rev)
        lbv = lb_ref[...]
        q_raw = q_ref[...].astype(F32)
        v, dout = v_ref[...], do_ref[...].astype(BF16)
        sig, forget, k, b, qs, eb, emb, qt, kt = _gla_block_terms(q_raw, f_ref[...].astype(F32), lbv, tri)
        a = jnp.where(tri, _dot_nt(qt, kt), 0.0)
        da = jnp.where(tri, _dot_nt(dout, v), 0.0)
        dv_intra = _dot_tn(a, dout)
        dqt_intra = _dot(da, kt)
        dkt = _dot_tn(da, qt)
        chunks = []
        for c, last in reversed(_chunk_order(ncb, rev)):
            sl = slice(c * CHUNK, (c + 1) * CHUNK)
            bl = b[last:last + 1, :]
            e = jnp.exp(bl - b[sl])
            s_t = st_ref[c]
            dqt_scr[sl, :] = dqt_intra[sl] + _dot(dout[sl], s_t)
            chunks.append((sl, jnp.exp(bl), e, k[sl] * e, s_t, _dot_tn(dout[sl], qt[sl])))
        ds_t = ds_ref[...]
        for sl, ebl, e, kh, s_t, grow in chunks:
            dkh = _dot(v[sl], ds_t)
            dv_scr[sl, :] = dv_intra[sl] + _dot_nt(kh, ds_t)
            dk_scr[sl, :] = dkt[sl] * emb[sl] + dkh * e
            khd = kh * dkh
            dbl = jnp.sum(khd, axis=0, keepdims=True) + ebl * jnp.sum(ds_t * s_t, axis=0, keepdims=True)
            db_scr[sl, :] = khd
            dbl_scr[sl, :] = jnp.broadcast_to(dbl, (CHUNK, HEAD))
            ds_t = grow + ds_t * ebl
        ds_ref[...] = ds_t
        dqt = dqt_scr[...]
        tri_t = _tri_mask(tb, not rev)
        dlogf = _masked_sum(tri_t, qt * dqt - kt * dkt - db_scr[...]) + dbl_scr[...]
        dforget = dlogf / forget - dk_scr[...]
        df_ref[...] = (dforget * (1.0 - lbv) * sig * (1.0 - sig)).astype(BF16)
        dlb_ref[...] += jnp.sum(dforget * (1.0 - sig), axis=0, keepdims=True)
        dqr = dqt * eb * (HG_SCALE * qs * (1.0 + q_raw * (1.0 - qs)))
        dv = dv_scr[...]
        if has_prev:
            dqr = dqr + pq_ref[...]
            dv = dv + pv_ref[...]
        dq_ref[...] = dqr.astype(qv_dtype)
        dv_ref[...] = dv.astype(qv_dtype)

    col = lambda off: pl.BlockSpec((tb, HEAD), lambda h, n: (tmap(n), off // HEAD + h))
    blk = pl.BlockSpec((tb, HEAD), lambda h, n: (tmap(n), h))
    vec = pl.BlockSpec((1, HEAD), lambda h, n: (0, h))
    wide = HG_HEADS * HEAD
    return pl.pallas_call(
        body, name=name, grid=(HG_HEADS, nb),
        in_specs=[col(C_HQ), col(f_off), col(C_HI), vec, blk,
                  pl.BlockSpec((None, ncb, HEAD, HEAD), lambda h, n: (h, tmap(n), 0, 0))]
                 + ([blk, blk] if has_prev else []),
        out_specs=[blk, blk, blk, vec],
        out_shape=[jax.ShapeDtypeStruct((rows, wide), qv_dtype), jax.ShapeDtypeStruct((rows, wide), BF16),
                   jax.ShapeDtypeStruct((rows, wide), qv_dtype), jax.ShapeDtypeStruct((1, wide), F32)],
        scratch_shapes=[pltpu.VMEM((HEAD, HEAD), F32)] + [pltpu.VMEM((tb, HEAD), F32)] * 5,
        compiler_params=_params(("parallel", "arbitrary")),
    )(*([proj, proj, proj, lb, do, states] + (list(prev) if has_prev else [])))


def _hg_out_fwd(o_fw, o_bw, proj, gain, *, name):
    rows = o_fw.shape[0]
    tm = min(512, rows)
    wide = HG_HEADS * HEAD

    def body(a_ref, b_ref, g_ref, gain_ref, o_ref):
        for h in range(HG_HEADS):
            sl = slice(h * HEAD, (h + 1) * HEAD)
            xhat, _ = _rms(a_ref[:, sl] + b_ref[:, sl])
            gate = g_ref[:, sl].astype(F32)
            o_ref[:, sl] = (xhat * gain_ref[...] * (gate * _sigmoid(gate))).astype(BF16)

    row = pl.BlockSpec((tm, wide), lambda i: (i, 0))
    return pl.pallas_call(
        body, name=name, grid=(rows // tm,),
        in_specs=[row, row, pl.BlockSpec((tm, wide), lambda i: (i, C_HG // wide)),
                  pl.BlockSpec((1, HEAD), lambda i: (0, 0))],
        out_specs=row, out_shape=jax.ShapeDtypeStruct((rows, wide), BF16),
        compiler_params=_params(("parallel",)),
    )(o_fw, o_bw, proj, gain)


def _hg_out_bwd(dout, o_fw, o_bw, proj, gain, *, name):
    rows = o_fw.shape[0]
    tm = min(512, rows)
    wide = HG_HEADS * HEAD

    def body(d_ref, a_ref, b_ref, g_ref, gain_ref, dgate_ref, do_ref, dgain_ref):
        @pl.when(pl.program_id(0) == 0)
        def _():
            dgain_ref[...] = jnp.zeros_like(dgain_ref)

        dgain = jnp.zeros((1, HEAD), F32)
        for h in range(HG_HEADS):
            sl = slice(h * HEAD, (h + 1) * HEAD)
            xhat, r = _rms(a_ref[:, sl] + b_ref[:, sl])
            gate, dy = g_ref[:, sl].astype(F32), d_ref[:, sl]
            gs = _sigmoid(gate)
            dgate_ref[:, sl] = (dy * xhat * gain_ref[...] * (gs * (1.0 + gate * (1.0 - gs)))).astype(BF16)
            dx, dgr = _rms_bwd(dy * (gate * gs), xhat, r, gain_ref[...])
            do_ref[:, sl] = dx
            dgain = dgain + jnp.sum(dgr, axis=0, keepdims=True)
        dgain_ref[...] += dgain

    row = pl.BlockSpec((tm, wide), lambda i: (i, 0))
    vec = pl.BlockSpec((1, HEAD), lambda i: (0, 0))
    return pl.pallas_call(
        body, name=name, grid=(rows // tm,),
        in_specs=[row, row, row, pl.BlockSpec((tm, wide), lambda i: (i, C_HG // wide)), vec],
        out_specs=[row, row, vec],
        out_shape=[jax.ShapeDtypeStruct((rows, wide), BF16), jax.ShapeDtypeStruct((rows, wide), F32),
                   jax.ShapeDtypeStruct((1, HEAD), F32)],
        compiler_params=_params(("arbitrary",)),
    )(dout, o_fw, o_bw, proj, gain)


def _strided_rows(ref, r, count, d):
    return ref[...] if d == 1 else ref[pl.ds(r, count, stride=d), :]


def _store_strided(ref, r, count, d, val):
    if d == 1:
        ref[...] = val
    else:
        ref[pl.ds(r, count, stride=d), :] = val


def _da_prep(proj, q_gain, k_gain, g, *, name):
    d = DA_GROUPS[g][0]
    rows = proj.shape[0]
    rb = min(512, rows)
    tn = rb // d

    def body(q_ref, k_ref, v_ref, qg_ref, kg_ref, qo_ref, ko_ref, vo_ref, qf_ref, kf_ref, vf_ref):
        for src, dst in ((q_ref, qf_ref), (k_ref, kf_ref), (v_ref, vf_ref)):
            dst[...] = src[...].astype(F32)
        for r in range(d):
            qhat, _ = _rms(_strided_rows(qf_ref, r, tn, d))
            khat, _ = _rms(_strided_rows(kf_ref, r, tn, d))
            qo_ref[r] = (qhat * qg_ref[...]).astype(BF16)
            ko_ref[r] = (khat * kg_ref[...]).astype(BF16)
            vo_ref[r] = _strided_rows(vf_ref, r, tn, d).astype(BF16)

    col = lambda off: pl.BlockSpec((rb, HEAD), lambda h, i: (i, (off + g * DA_GH * HEAD) // HEAD + h))
    vec = pl.BlockSpec((1, HEAD), lambda h, i: (0, 0))
    out = pl.BlockSpec((None, d, tn, HEAD), lambda h, i: (h, 0, i, 0))
    shape = jax.ShapeDtypeStruct((DA_GH, d, rows // d, HEAD), BF16)
    return pl.pallas_call(
        body, name=name, grid=(DA_GH, rows // rb),
        in_specs=[col(C_DQ), col(C_DK), col(C_DV), vec, vec],
        out_specs=[out, out, out], out_shape=[shape, shape, shape],
        scratch_shapes=[pltpu.VMEM((rb, HEAD), F32)] * 3,
        compiler_params=_params(("parallel", "parallel")),
    )(proj, proj, proj, q_gain, k_gain)


def _slopes(g):
    idx = np.arange(g * DA_GH + 1, (g + 1) * DA_GH + 1)
    s = (2.0 ** (-8.0 * idx / (DA_GH * len(DA_GROUPS)))).astype(np.float32)
    return jnp.asarray(np.broadcast_to(s[:, None, None], (DA_GH, 8, HEAD)).copy())


def _band_bias(t, jj, slope, d, radius):
    row = lax.broadcasted_iota(jnp.int32, (t, t), 0)
    col = lax.broadcasted_iota(jnp.int32, (t, t), 1)
    rel = jnp.abs((jj - 1) * t + col - row)
    return jnp.where(rel <= radius, -slope * (d * rel).astype(F32), NEG_INF)


def _band_fwd(qr, kr, vr, g, *, name):
    d, radius = DA_GROUPS[g]
    _, _, ld, _ = qr.shape
    t = min(HEAD, ld)
    qb = min(BAND_QBLOCKS, ld // t)
    nb, nbt = ld // (qb * t), ld // t

    def body(q_ref, kp_ref, ko_ref, kn_ref, vp_ref, vo_ref, vn_ref, sl_ref, o_ref, lse_ref):
        i = pl.program_id(2)
        slope = sl_ref[0:1, 0:1]
        bias = [_band_bias(t, jj, slope, d, radius) for jj in range(3)]

        def block(own_ref, prev_ref, next_ref, kk):
            if kk < 0:
                return prev_ref[...]
            return next_ref[...] if kk == qb else own_ref[kk * t:(kk + 1) * t, :]

        for j in range(qb):
            q = q_ref[j * t:(j + 1) * t, :]
            s = []
            for jj in range(3):
                sj = _dot_nt(q, block(ko_ref, kp_ref, kn_ref, j + jj - 1)) * ATT_SCALE + bias[jj]
                kblock = i * qb + j + jj - 1
                s.append(jnp.where((kblock >= 0) & (kblock < nbt), sj, NEG_INF))
            m = jnp.maximum(jnp.maximum(jnp.max(s[0], axis=-1, keepdims=True), jnp.max(s[1], axis=-1, keepdims=True)),
                            jnp.max(s[2], axis=-1, keepdims=True))
            p = [jnp.exp(sj - m) for sj in s]
            l = sum(jnp.sum(pj, axis=-1, keepdims=True) for pj in p)
            o = sum(_dot(pj, block(vo_ref, vp_ref, vn_ref, j + jj - 1)) for jj, pj in enumerate(p))
            o_ref[j * t:(j + 1) * t, :] = o / l
            lse_ref[j * t:(j + 1) * t, :] = jnp.broadcast_to(m + jnp.log(l), (t, HEAD))

    own = pl.BlockSpec((None, None, qb * t, HEAD), lambda h, r, i: (h, r, i, 0))
    prv = pl.BlockSpec((None, None, t, HEAD), lambda h, r, i: (h, r, jnp.maximum(i * qb - 1, 0), 0))
    nxt = pl.BlockSpec((None, None, t, HEAD), lambda h, r, i: (h, r, jnp.minimum((i + 1) * qb, nbt - 1), 0))
    shape = jax.ShapeDtypeStruct(qr.shape, F32)
    return pl.pallas_call(
        body, name=name, grid=(DA_GH, d, nb),
        in_specs=[own, prv, own, nxt, prv, own, nxt, pl.BlockSpec((None, 8, HEAD), lambda h, r, i: (h, 0, 0))],
        out_specs=[own, own], out_shape=[shape, shape],
        compiler_params=_params(("parallel", "parallel", "parallel")),
    )(qr, kr, kr, kr, vr, vr, vr, _slopes(g))


def _band_bwd(qr, kr, vr, dor, lser, deltar, g, *, name):
    d, radius = DA_GROUPS[g]
    _, _, ld, _ = qr.shape
    t = min(HEAD, ld)
    qb = min(BAND_QBLOCKS, ld // t)
    nb, nbt = ld // (qb * t), ld // t

    def body(q_ref, k_ref, v_ref, do_ref, lse_ref, dl_ref, sl_ref, dq_ref, dk_ref, dv_ref):
        i = pl.program_id(2)

        @pl.when(i == 0)
        def _():
            dk_ref[...] = jnp.zeros_like(dk_ref)
            dv_ref[...] = jnp.zeros_like(dv_ref)

        slope = sl_ref[0:1, 0:1]
        bias = [_band_bias(t, jj, slope, d, radius) for jj in range(3)]
        rows = [pl.ds(pl.multiple_of(jnp.clip(i * qb + kk - 1, 0, nbt - 1) * t, t), t) for kk in range(qb + 2)]
        ks = [k_ref[r, :] for r in rows]
        vs = [v_ref[r, :] for r in rows]
        dks, dvs = [None] * (qb + 2), [None] * (qb + 2)
        for j in range(qb):
            sl = slice(j * t, (j + 1) * t)
            q, dout = q_ref[sl, :], do_ref[sl, :]
            lse, delta = lse_ref[sl, 0:1], dl_ref[sl, 0:1]
            dq = None
            for jj in range(3):
                kk = j + jj
                kblock = i * qb + kk - 1
                sj = _dot_nt(q, ks[kk]) * ATT_SCALE + bias[jj]
                p = jnp.exp(jnp.where((kblock >= 0) & (kblock < nbt), sj, NEG_INF) - lse)
                ds = p * (_dot_nt(dout, vs[kk]) - delta)
                part = _dot(ds, ks[kk])
                dq = part if dq is None else dq + part
                dkp, dvp = _dot_tn(ds, q), _dot_tn(p, dout)
                dks[kk] = dkp if dks[kk] is None else dks[kk] + dkp
                dvs[kk] = dvp if dvs[kk] is None else dvs[kk] + dvp
            dq_ref[sl, :] = dq * ATT_SCALE
        for kk in range(qb + 2):
            dk_ref[rows[kk], :] += dks[kk] * ATT_SCALE
            dv_ref[rows[kk], :] += dvs[kk]

    own = pl.BlockSpec((None, None, qb * t, HEAD), lambda h, r, i: (h, r, i, 0))
    seq = pl.BlockSpec((None, None, ld, HEAD), lambda h, r, i: (h, r, 0, 0))
    shape = jax.ShapeDtypeStruct(qr.shape, F32)
    return pl.pallas_call(
        body, name=name, grid=(DA_GH, d, nb),
        in_specs=[own, seq, seq, own, own, own, pl.BlockSpec((None, 8, HEAD), lambda h, r, i: (h, 0, 0))],
        out_specs=[own, seq, seq], out_shape=[shape, shape, shape],
        compiler_params=_params(("parallel", "parallel", "arbitrary")),
    )(qr, kr, vr, dor, lser, deltar, _slopes(g))


def _da_merge(outs, lses, rows, *, name):
    rb = min(512, rows)

    def body(*refs):
        o_refs, l_refs = refs[0:3], refs[3:6]
        o_ref, lse_ref = refs[6:8]
        on_refs, ln_refs = refs[8:11], refs[11:14]
        for g, (d, _) in enumerate(DA_GROUPS):
            tn = rb // d
            for r in range(d):
                _store_strided(on_refs[g], r, tn, d, o_refs[g][r])
                _store_strided(ln_refs[g], r, tn, d, l_refs[g][r])
        l0, l1, l2 = ln_refs[0][...], ln_refs[1][...], ln_refs[2][...]
        m = jnp.maximum(jnp.maximum(l0, l1), l2)
        e0, e1, e2 = jnp.exp(l0 - m), jnp.exp(l1 - m), jnp.exp(l2 - m)
        tot = e0 + e1 + e2
        o_ref[...] = (e0 * on_refs[0][...] + e1 * on_refs[1][...] + e2 * on_refs[2][...]) / tot
        lse_ref[...] = m + jnp.log(tot)

    res = lambda d: pl.BlockSpec((None, d, rb // d, HEAD), lambda h, i: (h, 0, i, 0))
    nat = pl.BlockSpec((rb, HEAD), lambda h, i: (i, h))
    shape = jax.ShapeDtypeStruct((rows, DA_GH * HEAD), F32)
    return pl.pallas_call(
        body, name=name, grid=(DA_GH, rows // rb),
        in_specs=[res(d) for d, _ in DA_GROUPS] * 2,
        out_specs=[nat, nat], out_shape=[shape, shape],
        scratch_shapes=[pltpu.VMEM((rb, HEAD), F32)] * 6,
        compiler_params=_params(("parallel", "parallel")),
    )(*outs, *lses)


def _da_bwd_prep(dout, o, lse, *, name):
    rows = o.shape[0]
    rb = min(512, rows)

    def body(d_ref, o_ref, l_ref, *outs):
        delta_ref = outs[9]
        delta_ref[...] = jnp.broadcast_to(jnp.sum(d_ref[...] * o_ref[...], axis=-1, keepdims=True), (rb, HEAD))
        for g, (d, _) in enumerate(DA_GROUPS):
            tn = rb // d
            for r in range(d):
                outs[3 * g][r] = _strided_rows(d_ref, r, tn, d).astype(BF16)
                outs[3 * g + 1][r] = _strided_rows(l_ref, r, tn, d)
                outs[3 * g + 2][r] = _strided_rows(delta_ref, r, tn, d)

    nat = pl.BlockSpec((rb, HEAD), lambda h, i: (i, h))
    out_specs, out_shape = [], []
    for d, _ in DA_GROUPS:
        for dt in (BF16, F32, F32):
            out_specs.append(pl.BlockSpec((None, d, rb // d, HEAD), lambda h, i: (h, 0, i, 0)))
            out_shape.append(jax.ShapeDtypeStruct((DA_GH, d, rows // d, HEAD), dt))
    return pl.pallas_call(
        body, name=name, grid=(DA_GH, rows // rb),
        in_specs=[nat, nat, nat], out_specs=out_specs, out_shape=out_shape,
        scratch_shapes=[pltpu.VMEM((rb, HEAD), F32)],
        compiler_params=_params(("parallel", "parallel")),
    )(dout, o, lse)


def _da_prep_bwd(dqr, dkr, dvr, proj, q_gain, k_gain, g, *, name):
    d = DA_GROUPS[g][0]
    rows = proj.shape[0]
    rb = min(512, rows)
    tn = rb // d

    def body(dq_ref, dk_ref, dv_ref, q_ref, k_ref, qg_ref, kg_ref, oq_ref, ok_ref, ov_ref, gq_ref, gk_ref, *nat_refs):
        @pl.when((pl.program_id(0) == 0) & (pl.program_id(1) == 0))
        def _():
            gq_ref[...] = jnp.zeros_like(gq_ref)
            gk_ref[...] = jnp.zeros_like(gk_ref)

        for j, src in enumerate((dq_ref, dk_ref, dv_ref)):
            for r in range(d):
                _store_strided(nat_refs[j], r, tn, d, src[r])
        ov_ref[...] = nat_refs[2][...].astype(BF16)
        for j, (x_ref, gn_ref, out_ref, acc_ref) in enumerate(((q_ref, qg_ref, oq_ref, gq_ref),
                                                                (k_ref, kg_ref, ok_ref, gk_ref))):
            xhat, r = _rms(x_ref[...])
            dx, dgr = _rms_bwd(nat_refs[j][...], xhat, r, gn_ref[...])
            out_ref[...] = dx.astype(BF16)
            acc_ref[...] += jnp.sum(dgr, axis=0, keepdims=True)

    res = pl.BlockSpec((None, d, tn, HEAD), lambda h, i: (h, 0, i, 0))
    col = lambda off: pl.BlockSpec((rb, HEAD), lambda h, i: (i, (off + g * DA_GH * HEAD) // HEAD + h))
    vec = pl.BlockSpec((1, HEAD), lambda h, i: (0, 0))
    nat = pl.BlockSpec((rb, HEAD), lambda h, i: (i, h))
    shape = jax.ShapeDtypeStruct((rows, DA_GH * HEAD), BF16)
    gshape = jax.ShapeDtypeStruct((1, HEAD), F32)
    return pl.pallas_call(
        body, name=name, grid=(DA_GH, rows // rb),
        in_specs=[res, res, res, col(C_DQ), col(C_DK), vec, vec],
        out_specs=[nat, nat, nat, vec, vec], out_shape=[shape, shape, shape, gshape, gshape],
        scratch_shapes=[pltpu.VMEM((rb, HEAD), F32)] * 3,
        compiler_params=_params(("arbitrary", "arbitrary")),
    )(dqr, dkr, dvr, proj, proj, q_gain, k_gain)


def _mem_fwd(proj, kv, q_gain, k_gain, *, name):
    rows = proj.shape[0]
    tm = min(512, rows)
    n_mem = kv.shape[0]

    def body(q_ref, k_ref, v_ref, qg_ref, kg_ref, o_ref):
        qhat, _ = _rms(q_ref[...])
        khat, _ = _rms(k_ref[...])
        s = _dot_nt(qhat * qg_ref[...], khat * kg_ref[...]) * ATT_SCALE
        p = jnp.exp(s - jnp.max(s, axis=-1, keepdims=True))
        p = p / jnp.sum(p, axis=-1, keepdims=True)
        o_ref[...] = _dot(p, v_ref[...]).astype(BF16)

    vec = pl.BlockSpec((1, HEAD), lambda i, h: (0, 0))
    return pl.pallas_call(
        body, name=name, grid=(rows // tm, MEM_HEADS),
        in_specs=[pl.BlockSpec((tm, HEAD), lambda i, h: (i, C_MQ // HEAD + h)),
                  pl.BlockSpec((n_mem, HEAD), lambda i, h: (0, h)),
                  pl.BlockSpec((n_mem, HEAD), lambda i, h: (0, MEM_HEADS + h)), vec, vec],
        out_specs=pl.BlockSpec((tm, HEAD), lambda i, h: (i, h)),
        out_shape=jax.ShapeDtypeStruct((rows, MEM_HEADS * HEAD), BF16),
        compiler_params=_params(("parallel", "parallel")),
    )(proj, kv, kv, q_gain, k_gain)


def _mem_bwd(dout, proj, kv, q_gain, k_gain, *, name):
    rows = proj.shape[0]
    tm = min(512, rows)
    steps = rows // tm
    n_mem = kv.shape[0]

    def body(d_ref, q_ref, k_ref, v_ref, qg_ref, kg_ref, dq_ref, dk_ref, dv_ref, gq_ref, gk_ref, dkn_ref):
        h, i = pl.program_id(0), pl.program_id(1)

        @pl.when((h == 0) & (i == 0))
        def _():
            gq_ref[...] = jnp.zeros_like(gq_ref)
            gk_ref[...] = jnp.zeros_like(gk_ref)

        @pl.when(i == 0)
        def _():
            dkn_ref[...] = jnp.zeros_like(dkn_ref)
            dv_ref[...] = jnp.zeros_like(dv_ref)

        qhat, rq = _rms(q_ref[...])
        khat, rk = _rms(k_ref[...])
        qn, kn = qhat * qg_ref[...], khat * kg_ref[...]
        s = _dot_nt(qn, kn) * ATT_SCALE
        p = jnp.exp(s - jnp.max(s, axis=-1, keepdims=True))
        p = p / jnp.sum(p, axis=-1, keepdims=True)
        dout = d_ref[...]
        dp = _dot_nt(dout, v_ref[...])
        ds = p * (dp - jnp.sum(p * dp, axis=-1, keepdims=True))
        dv_ref[...] += _dot_tn(p, dout)
        dkn_ref[...] += _dot_tn(ds, qn) * ATT_SCALE
        dq, dgr = _rms_bwd(_dot(ds, kn) * ATT_SCALE, qhat, rq, qg_ref[...])
        dq_ref[...] = dq.astype(BF16)
        gq_ref[...] += jnp.sum(dgr, axis=0, keepdims=True)

        @pl.when(i == steps - 1)
        def _():
            dk, dgk = _rms_bwd(dkn_ref[...], khat, rk, kg_ref[...])
            dk_ref[...] = dk
            gk_ref[...] += jnp.sum(dgk, axis=0, keepdims=True)

    vec = pl.BlockSpec((1, HEAD), lambda h, i: (0, 0))
    memh = pl.BlockSpec((n_mem, HEAD), lambda h, i: (0, h))
    tok = pl.BlockSpec((tm, HEAD), lambda h, i: (i, h))
    gshape = jax.ShapeDtypeStruct((1, HEAD), F32)
    return pl.pallas_call(
        body, name=name, grid=(MEM_HEADS, steps),
        in_specs=[tok, pl.BlockSpec((tm, HEAD), lambda h, i: (i, C_MQ // HEAD + h)), memh,
                  pl.BlockSpec((n_mem, HEAD), lambda h, i: (0, MEM_HEADS + h)), vec, vec],
        out_specs=[tok, memh, memh, vec, vec],
        out_shape=[jax.ShapeDtypeStruct((rows, MEM_HEADS * HEAD), BF16),
                   jax.ShapeDtypeStruct((n_mem, MEM_HEADS * HEAD), F32),
                   jax.ShapeDtypeStruct((n_mem, MEM_HEADS * HEAD), F32), gshape, gshape],
        scratch_shapes=[pltpu.VMEM((n_mem, HEAD), F32)],
        compiler_params=_params(("arbitrary", "arbitrary")),
    )(dout, proj, kv, kv, q_gain, k_gain)


def _branch_fwd(o_hg, o_da, o_mem, proj, wp_hg, wp_da, wp_mem, *, name):
    rows = o_hg.shape[0]
    tm = min(256, rows)

    def body(a_ref, b_ref, c_ref, ga_ref, gb_ref, gc_ref, wa_ref, wb_ref, wc_ref, o_ref):
        merged = _sigmoid(ga_ref[...]) * _dot(a_ref[...], wa_ref[...])
        merged += _sigmoid(gb_ref[...]) * _dot(b_ref[...], wb_ref[...])
        merged += _sigmoid(gc_ref[...]) * _dot(c_ref[...], wc_ref[...])
        o_ref[...] = merged.astype(BF16)

    row = lambda w: pl.BlockSpec((tm, w), lambda i: (i, 0))
    gate = lambda off: pl.BlockSpec((tm, D_MODEL), lambda i: (i, off // D_MODEL))
    full = lambda a: pl.BlockSpec(a.shape, lambda i: (0, 0))
    return pl.pallas_call(
        body, name=name, grid=(rows // tm,),
        in_specs=[row(o_hg.shape[1]), row(o_da.shape[1]), row(o_mem.shape[1]),
                  gate(C_GHG), gate(C_GDA), gate(C_GMEM), full(wp_hg), full(wp_da), full(wp_mem)],
        out_specs=row(D_MODEL), out_shape=jax.ShapeDtypeStruct((rows, D_MODEL), BF16),
        compiler_params=_params(("parallel",)),
    )(o_hg, o_da, o_mem, proj, proj, proj, wp_hg, wp_da, wp_mem)


def _branch_bwd(dm, o_hg, o_da, o_mem, proj, wp_hg, wp_da, wp_mem, *, name):
    rows = o_hg.shape[0]
    tm = min(256, rows)

    def body(dm_ref, a_ref, b_ref, c_ref, ga_ref, gb_ref, gc_ref, wa_ref, wb_ref, wc_ref, *outs):
        dmv = dm_ref[...]
        for j, (o_ref, g_ref, w_ref) in enumerate(((a_ref, ga_ref, wa_ref), (b_ref, gb_ref, wb_ref),
                                                   (c_ref, gc_ref, wc_ref))):
            z = _dot(o_ref[...], w_ref[...])
            gs = _sigmoid(g_ref[...])
            dz = (dmv * gs).astype(BF16)
            outs[3 * j][...] = (dmv * z * gs * (1.0 - gs)).astype(BF16)
            outs[3 * j + 1][...] = dz
            outs[3 * j + 2][...] = _dot_nt(dz, w_ref[...])

    row = lambda w: pl.BlockSpec((tm, w), lambda i: (i, 0))
    gate = lambda off: pl.BlockSpec((tm, D_MODEL), lambda i: (i, off // D_MODEL))
    full = lambda a: pl.BlockSpec(a.shape, lambda i: (0, 0))
    out_specs, out_shape = [], []
    for o in (o_hg, o_da, o_mem):
        out_specs += [row(D_MODEL), row(D_MODEL), row(o.shape[1])]
        out_shape += [jax.ShapeDtypeStruct((rows, D_MODEL), BF16), jax.ShapeDtypeStruct((rows, D_MODEL), BF16),
                      jax.ShapeDtypeStruct((rows, o.shape[1]), F32)]
    return pl.pallas_call(
        body, name=name, grid=(rows // tm,),
        in_specs=[row(D_MODEL), row(o_hg.shape[1]), row(o_da.shape[1]), row(o_mem.shape[1]),
                  gate(C_GHG), gate(C_GDA), gate(C_GMEM), full(wp_hg), full(wp_da), full(wp_mem)],
        out_specs=out_specs, out_shape=out_shape,
        compiler_params=_params(("parallel",)),
    )(dm, o_hg, o_da, o_mem, proj, proj, proj, wp_hg, wp_da, wp_mem)


def _ffn_in(h2, w_ab, *, name):
    rows, dff = h2.shape[0], w_ab.shape[1] // 2
    tm, tn = min(1024, rows), 256

    def body(h_ref, wa_ref, wb_ref, a_ref, b_ref, u_ref):
        a = _dot(h_ref[...], wa_ref[...])
        b = _dot(h_ref[...], wb_ref[...])
        a_ref[...] = a.astype(BF16)
        b_ref[...] = b.astype(BF16)
        u_ref[...] = (a * _sigmoid(a) * b).astype(BF16)

    out = pl.BlockSpec((tm, tn), lambda i, j: (i, j))
    return pl.pallas_call(
        body, name=name, grid=(rows // tm, dff // tn),
        in_specs=[pl.BlockSpec((tm, D_MODEL), lambda i, j: (i, 0)),
                  pl.BlockSpec((D_MODEL, tn), lambda i, j: (0, j)),
                  pl.BlockSpec((D_MODEL, tn), lambda i, j: (0, dff // tn + j))],
        out_specs=[out, out, out],
        out_shape=[jax.ShapeDtypeStruct((rows, dff), BF16)] * 3,
        compiler_params=_params(("parallel", "parallel")),
    )(h2, w_ab, w_ab)


def _ffn_act_bwd(dy, w_out, a, b, *, name):
    rows, dff = a.shape
    tm, tn = min(1024, rows), 256

    def body(dy_ref, w_ref, a_ref, b_ref, da_ref, db_ref):
        du = _dot_nt(dy_ref[...], w_ref[...])
        av, bv = a_ref[...].astype(F32), b_ref[...].astype(F32)
        sa = _sigmoid(av)
        da_ref[...] = (du * bv * sa * (1.0 + av * (1.0 - sa))).astype(BF16)
        db_ref[...] = (du * av * sa).astype(BF16)

    tile = pl.BlockSpec((tm, tn), lambda i, j: (i, j))
    return pl.pallas_call(
        body, name=name, grid=(rows // tm, dff // tn),
        in_specs=[pl.BlockSpec((tm, D_MODEL), lambda i, j: (i, 0)),
                  pl.BlockSpec((tn, D_MODEL), lambda i, j: (j, 0)), tile, tile],
        out_specs=[tile, tile],
        out_shape=[jax.ShapeDtypeStruct((rows, dff), BF16), jax.ShapeDtypeStruct((rows, dff), BF16)],
        compiler_params=_params(("parallel", "parallel")),
    )(dy, w_out, a, b)


def _lower_bound(lb_fw, lb_bw, *, name):
    def body(a_ref, b_ref, oa_ref, ob_ref):
        for src, dst in ((a_ref, oa_ref), (b_ref, ob_ref)):
            dst[...] = _sigmoid(src[0:1, :] - src[1:2, :])

    shape = jax.ShapeDtypeStruct((1, lb_fw.shape[1]), F32)
    return pl.pallas_call(body, name=name, out_shape=[shape, shape])(lb_fw, lb_bw)


def _local_step(x, mem, tgt, p, w):
    rows = x.shape[0]
    lb_fw, lb_bw = _lower_bound(p["lb_logits_fw"], p["lb_logits_bw"], name="lower_bound")

    h = _rms_fwd(x, p["norm_mix_gain"], name="norm_mix")
    proj = _matmul([(h, w["w_in"])], "nn", BF16, tm=1024, tn=512, name="proj_in")
    o_fw, st_fw = _gla_fwd(proj, lb_fw, f_off=C_FF, rev=False, name="gla_fwd_fw")
    o_bw, st_bw = _gla_fwd(proj, lb_bw, f_off=C_FB, rev=True, name="gla_fwd_bw")
    o_hg = _hg_out_fwd(o_fw, o_bw, proj, p["hg_norm_gain"], name="hg_out")

    qkv_r, outs, lses = [], [], []
    for g in range(len(DA_GROUPS)):
        qr, kr, vr = _da_prep(proj, p["da_q_gain"], p["da_k_gain"], g, name=f"da_prep{g}")
        og, lg = _band_fwd(qr, kr, vr, g, name=f"band_fwd{g}")
        qkv_r.append((qr, kr, vr))
        outs.append(og)
        lses.append(lg)
    o_da, lse_da = _da_merge(outs, lses, rows, name="da_merge")

    mem_n = _rms_fwd(mem, p["norm_mem_gain"], name="norm_mem")
    kv = _matmul([(mem_n, w["w_mem_kv"])], "nn", F32, tm=256, tn=512, name="mem_kv")
    o_mem = _mem_fwd(proj, kv, p["mem_q_gain"], p["mem_k_gain"], name="mem_attn")

    merged = _branch_fwd(o_hg, o_da, o_mem, proj, w["w_proj_hg"], w["w_proj_da"], w["w_proj_mem"],
                         name="branch_merge")
    x1 = _matmul([(merged, w["w_out"])], "nn", F32, tm=512, tn=512, residual=x, name="mix_out")
    h2 = _rms_fwd(x1, p["norm_ffn_gain"], name="norm_ffn")
    a, b, u = _ffn_in(h2, w["w_ffn_in"], name="ffn_in")
    y = _matmul([(u, w["w_ffn_out"])], "nn", F32, tm=512, tn=512, residual=x1, name="ffn_out")
    dy, dy_b, loss = _loss_grad(y, tgt, name="loss")

    gw, gs = {}, {}
    gw["w_ffn_out"] = _matmul([(u, dy_b)], "tn", F32, tm=256, tn=512, name="g_ffn_out")
    da, db = _ffn_act_bwd(dy_b, w["w_ffn_out"], a, b, name="ffn_act_bwd")
    gw["w_ffn_a"] = _matmul([(h2, da)], "tn", F32, tm=512, tn=256, name="g_ffn_a")
    gw["w_ffn_b"] = _matmul([(h2, db)], "tn", F32, tm=512, tn=256, name="g_ffn_b")
    dh2 = _matmul([(da, w["w_ffn_in"], 0), (db, w["w_ffn_in"], 1)], "nt", F32, tm=256, tn=512, name="d_h2")
    dx1, dx1_b, gs["norm_ffn_gain"] = _rms_bwd_rows(dh2, x1, p["norm_ffn_gain"], dy, name="norm_ffn_bwd")
    gw["w_out"] = _matmul([(merged, dx1_b)], "tn", F32, tm=512, tn=512, name="g_out")
    dmerged = _matmul([(dx1_b, w["w_out"])], "nt", F32, tm=512, tn=512, name="d_merged")
    (dg_hg, dz_hg, do_hg, dg_da, dz_da, do_da, dg_mem, dz_mem, do_mem) = _branch_bwd(
        dmerged, o_hg, o_da, o_mem, proj, w["w_proj_hg"], w["w_proj_da"], w["w_proj_mem"], name="branch_bwd")
    gw["w_proj_hg"] = _matmul([(o_hg, dz_hg)], "tn", F32, tm=512, tn=512, name="g_proj_hg")
    gw["w_proj_da"] = _matmul([(o_da, dz_da)], "tn", F32, tm=512, tn=512, name="g_proj_da")
    gw["w_proj_mem"] = _matmul([(o_mem, dz_mem)], "tn", F32, tm=512, tn=512, name="g_proj_mem")

    dq_mem, dk_mem, dv_mem, gs["mem_q_gain"], gs["mem_k_gain"] = _mem_bwd(
        do_mem, proj, kv, p["mem_q_gain"], p["mem_k_gain"], name="mem_attn_bwd")
    dkv = jnp.concatenate([dk_mem, dv_mem], axis=1).astype(BF16)
    gw["w_mem_kv"] = _matmul([(mem_n, dkv)], "tn", F32, tm=512, tn=512, name="g_mem_kv")
    dmem_n = _matmul([(dkv, w["w_mem_kv"])], "nt", F32, tm=256, tn=512, name="d_mem_n")
    _, _, gs["norm_mem_gain"] = _rms_bwd_rows(dmem_n, mem, p["norm_mem_gain"], None, name="norm_mem_bwd")

    prep = _da_bwd_prep(do_da, o_da, lse_da, name="da_bwd_prep")
    d_da, gq_parts, gk_parts = [], [], []
    for g in range(len(DA_GROUPS)):
        qr, kr, vr = qkv_r[g]
        dor, lser, deltar = prep[3 * g:3 * g + 3]
        dqr, dkr, dvr = _band_bwd(qr, kr, vr, dor, lser, deltar, g, name=f"band_bwd{g}")
        dq, dk, dv, gq, gk = _da_prep_bwd(dqr, dkr, dvr, proj, p["da_q_gain"], p["da_k_gain"], g,
                                          name=f"da_prep_bwd{g}")
        d_da.append((dq, dk, dv))
        gq_parts.append(gq)
        gk_parts.append(gk)

    dg_hgate, do_gla, gs["hg_norm_gain"] = _hg_out_bwd(do_hg, o_fw, o_bw, proj, p["hg_norm_gain"], name="hg_out_bwd")
    dq_f, dfl_fw, dv_f, dlb_fw = _gla_bwd(proj, lb_fw, do_gla, st_fw, None, f_off=C_FF, rev=False, name="gla_bwd_fw")
    dq_hg, dfl_bw, dv_hg, dlb_bw = _gla_bwd(proj, lb_bw, do_gla, st_bw, (dq_f, dv_f), f_off=C_FB, rev=True,
                                            name="gla_bwd_bw")

    dproj = jnp.concatenate(
        [dq_hg, dfl_fw, dfl_bw, dv_hg, dg_hgate]
        + [t[0] for t in d_da] + [t[1] for t in d_da] + [t[2] for t in d_da]
        + [dq_mem, dg_hg, dg_da, dg_mem], axis=1)
    gw["w_in"] = _matmul([(h, dproj)], "tn", F32, tm=512, tn=512, name="g_in")
    dh = _matmul([(dproj, w["w_in"])], "nt", F32, tm=1024, tn=1024, tk=1024, name="d_h")
    grad_x, _, gs["norm_mix_gain"] = _rms_bwd_rows(dh, x, p["norm_mix_gain"], dx1, name="norm_mix_bwd")

    small = _small_pack(gs, gq_parts, gk_parts, dlb_fw, dlb_bw, lb_fw, lb_bw, name="small_pack")
    return loss, grad_x, gw, small


def _small_pack(gs, gq_parts, gk_parts, dlb_fw, dlb_bw, lb_fw, lb_bw, *, name):
    def body(g_mix, g_mem, g_ffn, dfw, dbw, lfw, lbw, g_hg, q0, q1, q2, k0, k1, k2, g_mq, g_mk, o_ref):
        o_ref[0:1, :] = g_mix[...]
        o_ref[1:2, :] = g_mem[...]
        o_ref[2:3, :] = g_ffn[...]
        for base, d_ref, l_ref in ((3, dfw, lfw), (5, dbw, lbw)):
            lbv = l_ref[...]
            t = d_ref[...] * lbv * (1.0 - lbv)
            o_ref[base:base + 1, :] = t
            o_ref[base + 1:base + 2, :] = -t
        o_ref[7:8, :] = jnp.zeros((1, D_MODEL), F32)
        o_ref[7:8, 0:HEAD] = g_hg[...]
        o_ref[7:8, HEAD:2 * HEAD] = q0[...] + q1[...] + q2[...]
        o_ref[7:8, 2 * HEAD:3 * HEAD] = k0[...] + k1[...] + k2[...]
        o_ref[7:8, 3 * HEAD:4 * HEAD] = g_mq[...]
        o_ref[7:8, 4 * HEAD:5 * HEAD] = g_mk[...]

    return pl.pallas_call(body, name=name, out_shape=jax.ShapeDtypeStruct((8, D_MODEL), F32))(
        gs["norm_mix_gain"], gs["norm_mem_gain"], gs["norm_ffn_gain"], dlb_fw, dlb_bw, lb_fw, lb_bw,
        gs["hg_norm_gain"], *gq_parts, *gk_parts, gs["mem_q_gain"], gs["mem_k_gain"])


def _row_tile(rows, cols, n_arrays):
    budget = (16 * 1024 * 1024) // (2 * 4 * cols * n_arrays)
    tr = rows
    while tr > budget and tr % 2 == 0 and (tr // 2) % 16 == 0:
        tr //= 2
    return tr


def _cast_into_full(a, chip, rows, cols, axis, *, name):
    sr, sc = a.shape
    tr = _row_tile(sr, sc, 2)

    def body(chip_ref, a_ref, o_ref):
        del chip_ref
        o_ref[...] = a_ref[...].astype(BF16)

    if axis == 1:
        out_map = lambda i, chip_ref: (i, chip_ref[0])
    else:
        out_map = lambda i, chip_ref: (chip_ref[0] * (sr // tr) + i, 0)
    return pl.pallas_call(
        body, name=name,
        grid_spec=pltpu.PrefetchScalarGridSpec(
            num_scalar_prefetch=1, grid=(sr // tr,),
            in_specs=[pl.BlockSpec((tr, sc), lambda i, chip_ref: (i, 0))],
            out_specs=pl.BlockSpec((tr, sc), out_map)),
        out_shape=jax.ShapeDtypeStruct((rows, cols), BF16),
        compiler_params=_params(("parallel",)))(chip, a)


def _add_halves(items, *, name):
    rows = items[0][3].shape[0]
    widths = [ra.shape[1] for _, _, _, ra in items]
    tr = _row_tile(rows, sum(widths), 4)

    def body(*refs):
        o_ref = refs[-1]
        first = lax.axis_index("c") == 0
        off = 0
        for j, wd in enumerate(widths):
            h0, h1, ra = refs[3 * j:3 * j + 3]
            o_ref[:, off:off + wd] = (jnp.where(first, h0[...], h1[...]) + ra[...]).astype(BF16)
            off += wd

    in_specs, ins = [], []
    for (g, haxis, hsize, ra), wd in zip(items, widths):
        if haxis == 0:
            in_specs += [pl.BlockSpec((tr, wd), lambda i: (i, 0)),
                         pl.BlockSpec((tr, wd), lambda i, o=hsize // tr: (o + i, 0))]
        else:
            in_specs += [pl.BlockSpec((tr, wd), lambda i: (i, 0)), pl.BlockSpec((tr, wd), lambda i: (i, 1))]
        in_specs.append(pl.BlockSpec((tr, wd), lambda i: (i, 0)))
        ins += [g, g, ra]
    return pl.pallas_call(body, name=name, grid=(rows // tr,), in_specs=in_specs,
                          out_specs=pl.BlockSpec((tr, sum(widths)), lambda i: (i, 0)),
                          out_shape=jax.ShapeDtypeStruct((rows, sum(widths)), BF16),
                          compiler_params=_params(("parallel",)))(*ins)


def _add_slots(rb, *, name):
    _, rows, cols = rb.shape
    tr = _row_tile(rows, cols, 5)

    def body(r0, r1, r2, r3, o_ref):
        o_ref[...] = ((r0[...].astype(F32) + r1[...].astype(F32)) + r2[...].astype(F32)) + r3[...].astype(F32)

    slot = lambda s: pl.BlockSpec((None, tr, cols), lambda i: (s, i, 0))
    return pl.pallas_call(body, name=name, grid=(rows // tr,), in_specs=[slot(s) for s in range(4)],
                          out_specs=pl.BlockSpec((tr, cols), lambda i: (i, 0)),
                          out_shape=jax.ShapeDtypeStruct((rows, cols), F32),
                          compiler_params=_params(("parallel",)))(rb, rb, rb, rb)


def _adamw(w, g, m, v, *, name):
    rows, cols = w.shape
    tr = _row_tile(rows, cols, 7) if rows % 16 == 0 else rows
    c1 = 1.0 - ADAM_B1 ** ADAM_STEP
    c2 = 1.0 - ADAM_B2 ** ADAM_STEP

    def body(w_ref, g_ref, m_ref, v_ref, d_ref, mo_ref, vo_ref):
        gv = g_ref[...]
        mn = ADAM_B1 * m_ref[...] + (1.0 - ADAM_B1) * gv
        vn = ADAM_B2 * v_ref[...] + (1.0 - ADAM_B2) * (gv * gv)
        mo_ref[...] = mn
        vo_ref[...] = vn
        d_ref[...] = -ADAM_LR * ((mn / c1) / (jnp.sqrt(vn / c2) + ADAM_EPS) + ADAM_WD * w_ref[...])

    spec = pl.BlockSpec((tr, cols), lambda i: (i, 0))
    shape = jax.ShapeDtypeStruct((rows, cols), F32)
    return pl.pallas_call(body, name=name, grid=(rows // tr,), in_specs=[spec] * 4, out_specs=[spec] * 3,
                          out_shape=[shape] * 3, compiler_params=_params(("parallel",)))(w, g, m, v)


W_SPECS = (
    ("w_in", 1024, IN_COLS, 1, IN_COLS // 4),
    ("w_mem_kv", 1024, 1024, 0, 256),
    ("w_proj_hg", 1024, 1024, 0, 256),
    ("w_proj_da", 512, 1024, 1, 256),
    ("w_proj_mem", 512, 1024, 1, 256),
    ("w_out", 1024, 1024, 0, 256),
    ("w_ffn_in", 1024, 2 * D_FF, 1, 2 * D_FF // 4),
    ("w_ffn_out", D_FF, 1024, 0, D_FF // 4),
)
CHIP_FLIPS = ((1, 0), (0, 1), (1, 1))
ANY = pl.BlockSpec(memory_space=pl.ANY)
DMA_CHUNK_BYTES = 1 << 20
STAGE_BYTES = 2 << 20


def _place():
    x, y, c = lax.axis_index("x"), lax.axis_index("y"), lax.axis_index("c")
    return x, y, c, 2 * x + y


def _flip(v, f):
    return 1 - v if f else v


def _slab(ref, axis, idx, size):
    start = pl.multiple_of(idx * size, size)
    return ref.at[pl.ds(start, size), :] if axis == 0 else ref.at[:, pl.ds(start, size)]


def _chunked(make, src, dst):
    rows, cols = src.shape
    row_bytes = cols * jnp.dtype(src.dtype).itemsize
    k = 1
    while rows % (2 * k) == 0 and (rows // (2 * k)) % 16 == 0 and (rows // k) * row_bytes > DMA_CHUNK_BYTES:
        k *= 2
    cr = rows // k
    parts = [make(src.at[pl.ds(j * cr, cr), :], dst.at[pl.ds(j * cr, cr), :]) for j in range(k)]
    return parts, make(src, dst)


def _run_copies(local, remote):
    for parts, _ in local + remote:
        for cp in parts:
            cp.start()
    for _, whole in remote:
        whole.wait_recv()
    for _, whole in remote:
        whole.wait_send()
    for _, whole in local:
        whole.wait()


def _half_spec(rows, cols, axis):
    return (0, rows // 2) if axis == 1 else (1, cols // 2)


def _staged(src, remote_dst, local_dst, sibling, load_sems, send_sems, store_sems, recv_sem):
    rows, cols = src.shape
    row_bytes = cols * jnp.dtype(src.dtype).itemsize
    k = 1
    while (rows // k) * row_bytes > STAGE_BYTES and rows % (2 * k) == 0 and (rows // (2 * k)) % 16 == 0:
        k *= 2
    cr = rows // k
    piece = lambda ref, j: ref.at[pl.ds(j * cr, cr), :]

    def run(buf):
        loads = [pltpu.make_async_copy(piece(src, j), buf.at[j % 2], load_sems.at[j % 2]) for j in range(k)]
        outs = [[pltpu.make_async_remote_copy(src_ref=buf.at[j % 2], dst_ref=piece(remote_dst, j),
                                              send_sem=send_sems.at[j % 2], recv_sem=recv_sem,
                                              device_id=sibling, device_id_type=MESH)] for j in range(k)]
        if local_dst is not None:
            for j in range(k):
                outs[j].append(pltpu.make_async_copy(buf.at[j % 2], piece(local_dst, j), store_sems.at[j % 2]))

        def drained(j):
            outs[j][0].wait_send()
            for cp in outs[j][1:]:
                cp.wait()

        loads[0].start()
        for j in range(k):
            loads[j].wait()
            for cp in outs[j]:
                cp.start()
            if j + 1 < k:
                if j >= 1:
                    drained(j - 1)
                loads[j + 1].start()
        for j in range(max(0, k - 2), k):
            drained(j)

    pl.run_scoped(run, pltpu.VMEM((2, cr, cols), src.dtype))


def _landed(ref, recv_sem, send_sem):
    pltpu.make_async_remote_copy(src_ref=ref, dst_ref=ref, send_sem=send_sem, recv_sem=recv_sem,
                                 device_id=(lax.axis_index("x"), lax.axis_index("y"), lax.axis_index("c")),
                                 device_id_type=MESH).wait_recv()


def _gather_weights(fulls):
    n = len(W_SPECS)

    def body(*refs):
        outs = refs[n:2 * n]
        ici_send, ici_recv, load_sems, d2d_send, d2d_recv = refs[2 * n:]
        x, y, c, p = _place()

        def half_slab(wi, chip, half):
            _, rows, cols, axis, size = W_SPECS[wi]
            haxis, hsize = _half_spec(rows, cols, axis)
            return _slab(_slab(outs[wi], axis, chip, size), haxis, half, hsize)

        sent = []
        for wi in range(n):
            mine = half_slab(wi, p, c)
            for k, (fx, fy) in enumerate(CHIP_FLIPS):
                sent.append(_chunked(lambda s, d, j=3 * wi + k, fx=fx, fy=fy: pltpu.make_async_remote_copy(
                    src_ref=s, dst_ref=d, send_sem=ici_send.at[j], recv_sem=ici_recv.at[j],
                    device_id=(_flip(x, fx), _flip(y, fy), c), device_id_type=MESH), mine, mine))
        for parts, _ in sent:
            for cp in parts:
                cp.start()
        for k, (fx, fy) in enumerate(CHIP_FLIPS):
            q = 2 * _flip(x, fx) + _flip(y, fy)
            for wi in range(n):
                sent[3 * wi + k][1].wait_recv()
                got = half_slab(wi, q, c)
                _staged(got, got, None, (x, y, 1 - c), load_sems, d2d_send, None, d2d_recv.at[3 * wi + k])
        for _, whole in sent:
            whole.wait_send()
        for k, (fx, fy) in enumerate(CHIP_FLIPS):
            q = 2 * _flip(x, fx) + _flip(y, fy)
            for wi in range(n):
                _landed(half_slab(wi, q, 1 - c), d2d_recv.at[3 * wi + k], d2d_send.at[0])

    return pl.pallas_call(
        body, name="gather_weights", in_specs=[ANY] * n, out_specs=[ANY] * n,
        out_shape=[jax.ShapeDtypeStruct(f.shape, f.dtype) for f in fulls],
        input_output_aliases={i: i for i in range(n)},
        scratch_shapes=[pltpu.SemaphoreType.DMA((3 * n,)), pltpu.SemaphoreType.DMA((3 * n,)),
                        pltpu.SemaphoreType.DMA((2,)), pltpu.SemaphoreType.DMA((2,)),
                        pltpu.SemaphoreType.DMA((3 * n,))],
    )(*fulls)


def _sibling_exchange(grads):
    n = len(grads)

    def body(*refs):
        ins, outs = refs[:n], refs[n:2 * n]
        load_sems, send_sems, recv_sems = refs[2 * n:]
        x, y, c, _ = _place()
        for i, (_, haxis, hsize) in enumerate(grads):
            _staged(_slab(ins[i], haxis, 1 - c, hsize), outs[i], None, (x, y, 1 - c),
                    load_sems, send_sems, None, recv_sems.at[i])
        for i in range(n):
            _landed(outs[i], recv_sems.at[i], send_sems.at[0])

    shapes = [jax.ShapeDtypeStruct((hsize, g.shape[1]) if haxis == 0 else (g.shape[0], hsize), F32)
              for g, haxis, hsize in grads]
    return pl.pallas_call(
        body, name="grad_sibling_exchange", in_specs=[ANY] * n, out_specs=[ANY] * n, out_shape=shapes,
        scratch_shapes=[pltpu.SemaphoreType.DMA((2,)), pltpu.SemaphoreType.DMA((2,)),
                        pltpu.SemaphoreType.DMA((n,))],
    )(*[g for g, _, _ in grads])


def _chip_exchange(parts):
    n = len(parts)

    def body(*refs):
        ins, outs = refs[:n], refs[n:2 * n]
        send_sems, recv_sems, local_sems = refs[2 * n:]
        x, y, c, p = _place()
        local, remote = [], []
        for i, (_, axis, size) in enumerate(parts):
            local.append(_chunked(lambda s, d, i=i: pltpu.make_async_copy(s, d, local_sems.at[i]),
                                  _slab(ins[i], axis, p, size), outs[i].at[p]))
            for k, (fx, fy) in enumerate(CHIP_FLIPS):
                px, py = _flip(x, fx), _flip(y, fy)
                remote.append(_chunked(lambda s, d, j=3 * i + k, px=px, py=py: pltpu.make_async_remote_copy(
                    src_ref=s, dst_ref=d, send_sem=send_sems.at[j], recv_sem=recv_sems.at[j],
                    device_id=(px, py, c), device_id_type=MESH), _slab(ins[i], axis, 2 * px + py, size), outs[i].at[p]))
        _run_copies(local, remote)

    shapes = []
    for a, axis, size in parts:
        shapes.append(jax.ShapeDtypeStruct((4, size, a.shape[1]) if axis == 0 else (4, a.shape[0], size), a.dtype))
    return pl.pallas_call(
        body, name="grad_chip_exchange", in_specs=[ANY] * n, out_specs=[ANY] * n, out_shape=shapes,
        scratch_shapes=[pltpu.SemaphoreType.DMA((3 * n,)), pltpu.SemaphoreType.DMA((3 * n,)),
                        pltpu.SemaphoreType.DMA((n,))],
    )(*[a for a, _, _ in parts])


def _sibling_share(sums):
    n = len(sums)

    def body(*refs):
        ins, outs = refs[:n], refs[n:2 * n]
        load_sems, send_sems, store_sems, recv_sems = refs[2 * n:]
        x, y, c, _ = _place()
        for i, (s, haxis) in enumerate(sums):
            place = _slab(outs[i], haxis, c, s.shape[haxis])
            _staged(ins[i], place, place, (x, y, 1 - c), load_sems, send_sems, store_sems, recv_sems.at[i])
        for i, (s, haxis) in enumerate(sums):
            _landed(_slab(outs[i], haxis, 1 - c, s.shape[haxis]), recv_sems.at[i], send_sems.at[0])

    shapes = []
    for s, haxis in sums:
        r, cc = s.shape
        shapes.append(jax.ShapeDtypeStruct((2 * r, cc) if haxis == 0 else (r, 2 * cc), F32))
    return pl.pallas_call(
        body, name="grad_sibling_share", in_specs=[ANY] * n, out_specs=[ANY] * n, out_shape=shapes,
        scratch_shapes=[pltpu.SemaphoreType.DMA((2,)), pltpu.SemaphoreType.DMA((2,)),
                        pltpu.SemaphoreType.DMA((2,)), pltpu.SemaphoreType.DMA((n,))],
    )(*[s for s, _ in sums])


def _reduce_scatter(gw):
    grads = []
    for name, _, _, axis, _ in W_SPECS:
        for g in ((gw["w_ffn_a"], gw["w_ffn_b"]) if name == "w_ffn_in" else (gw[name],)):
            grads.append((g,) + _half_spec(g.shape[0], g.shape[1], axis))
    theirs = _sibling_exchange(grads)
    parts, j = [], 0
    for name, _, _, axis, size in W_SPECS:
        take = 2 if name == "w_ffn_in" else 1
        items = [grads[i] + (theirs[i],) for i in range(j, j + take)]
        parts.append((_add_halves(items, name=f"half_sum_{name}"), axis, size))
        j += take
    slots = _chip_exchange(parts)
    sums = []
    for (name, rows, cols, axis, _), rb in zip(W_SPECS, slots):
        sums.append((_add_slots(rb, name=f"chip_sum_{name}"), _half_spec(rows, cols, axis)[0]))
    return dict(zip([s[0] for s in W_SPECS], _sibling_share(sums)))


def _small_allreduce(sv):
    def body(sv_ref, o_ref, slots_ref, send_sems, recv_sems):
        x, y, c, _ = _place()
        me = 4 * x + 2 * y + c
        slots_ref[me] = sv_ref[...]
        copies = []
        for k in range(1, 8):
            fx, fy, fc = (k >> 2) & 1, (k >> 1) & 1, k & 1
            copies.append(pltpu.make_async_remote_copy(
                src_ref=sv_ref, dst_ref=slots_ref.at[me], send_sem=send_sems.at[k - 1],
                recv_sem=recv_sems.at[k - 1], device_id=(_flip(x, fx), _flip(y, fy), _flip(c, fc)),
                device_id_type=MESH))
        for cp in copies:
            cp.start()
        for cp in copies:
            cp.wait_recv()
        for cp in copies:
            cp.wait_send()
        total = slots_ref[0]
        for s in range(1, 8):
            total = total + slots_ref[s]
        o_ref[...] = total

    vm = pl.BlockSpec(memory_space=pltpu.VMEM)
    return pl.pallas_call(
        body, name="small_allreduce", in_specs=[vm], out_specs=vm,
        out_shape=jax.ShapeDtypeStruct(sv.shape, F32),
        scratch_shapes=[pltpu.VMEM((8,) + sv.shape, F32), pltpu.SemaphoreType.DMA((7,)),
                        pltpu.SemaphoreType.DMA((7,))],
    )(sv)


SMALL_ROWS = (("norm_mix_gain", 0), ("norm_mem_gain", 1), ("norm_ffn_gain", 2))
SMALL_LB = (("lb_logits_fw", 3), ("lb_logits_bw", 5))
SMALL_HEAD = ("hg_norm_gain", "da_q_gain", "da_k_gain", "mem_q_gain", "mem_k_gain")


def _pack_small(d):
    last = jnp.concatenate([d[n] for n in SMALL_HEAD] + [jnp.zeros((1, D_MODEL - HEAD * len(SMALL_HEAD)), F32)], axis=1)
    return jnp.concatenate([d["norm_mix_gain"], d["norm_mem_gain"], d["norm_ffn_gain"],
                            d["lb_logits_fw"], d["lb_logits_bw"], last], axis=0)


def _unpack_small(a):
    out = {n: a[r:r + 1] for n, r in SMALL_ROWS}
    out.update({n: a[r:r + 2] for n, r in SMALL_LB})
    out.update({n: a[7:8, j * HEAD:(j + 1) * HEAD] for j, n in enumerate(SMALL_HEAD)})
    return out


PARAM_ORDER = ("norm_mix_gain", "norm_mem_gain", "w_in", "lb_logits_fw", "lb_logits_bw", "hg_norm_gain",
               "da_q_gain", "da_k_gain", "w_mem_kv", "mem_q_gain", "mem_k_gain", "w_proj_hg", "w_proj_da",
               "w_proj_mem", "w_out", "norm_ffn_gain", "w_ffn_in", "w_ffn_out")


def kernel(x, mem, norm_mix_gain, norm_mem_gain, w_in, lb_logits_fw, lb_logits_bw, hg_norm_gain, da_q_gain, da_k_gain, w_mem_kv, mem_q_gain, mem_k_gain, w_proj_hg, w_proj_da, w_proj_mem, w_out, norm_ffn_gain, w_ffn_in, w_ffn_out, loss_target, m_norm_mix_gain, m_norm_mem_gain, m_w_in, m_lb_logits_fw, m_lb_logits_bw, m_hg_norm_gain, m_da_q_gain, m_da_k_gain, m_w_mem_kv, m_mem_q_gain, m_mem_k_gain, m_w_proj_hg, m_w_proj_da, m_w_proj_mem, m_w_out, m_norm_ffn_gain, m_w_ffn_in, m_w_ffn_out, v_norm_mix_gain, v_norm_mem_gain, v_w_in, v_lb_logits_fw, v_lb_logits_bw, v_hg_norm_gain, v_da_q_gain, v_da_k_gain, v_w_mem_kv, v_mem_q_gain, v_mem_k_gain, v_w_proj_hg, v_w_proj_da, v_w_proj_mem, v_w_out, v_norm_ffn_gain, v_w_ffn_in, v_w_ffn_out):
    args = dict(locals())
    mats = tuple(s[0] for s in W_SPECS)
    flat = lambda a: a.reshape(a.shape[-2:])
    w = {n: flat(args[n]) for n in mats}
    m = {n: flat(args["m_" + n]) for n in mats}
    v = {n: flat(args["v_" + n]) for n in mats}
    small = {n: args[n] for n in PARAM_ORDER if n not in mats}

    chip = (2 * lax.axis_index("x") + lax.axis_index("y")).astype(jnp.int32).reshape(1)
    full = _gather_weights([_cast_into_full(w[n], chip, rows, cols, axis, name=f"cast_{n}")
                            for n, rows, cols, axis, _ in W_SPECS])
    loss, grad_x, gw, small_grads = _local_step(x[0], mem[0], loss_target[0], small, dict(zip(mats, full)))
    grads = _reduce_scatter(gw)
    small_sum = _small_allreduce(small_grads)

    delta, new_m, new_v = {}, {}, {}
    for n in mats:
        delta[n], new_m[n], new_v[n] = _adamw(w[n], grads[n], m[n], v[n], name=f"adamw_{n}")
    packed = _adamw(_pack_small(small), small_sum,
                    _pack_small({n: args["m_" + n] for n in small}),
                    _pack_small({n: args["v_" + n] for n in small}), name="adamw_small")
    grads.update(_unpack_small(small_sum))
    for dst, src in zip((delta, new_m, new_v), packed):
        dst.update(_unpack_small(src))

    def shaped(d, n):
        return d[n].reshape(args[n].shape)

    loss_sum = lax.psum(loss[0, 0], ("x", "y", "c"))
    return (loss_sum, grad_x[None], *[shaped(grads, n) for n in PARAM_ORDER], *[shaped(delta, n) for n in PARAM_ORDER],
            *[shaped(new_m, n) for n in PARAM_ORDER], *[shaped(new_v, n) for n in PARAM_ORDER])
```

```python
import functools
import math

import numpy as np
import jax
import jax.numpy as jnp
from jax import lax
from jax.experimental import pallas as pl
from jax.experimental.pallas import tpu as pltpu

F32 = jnp.float32
BF16 = jnp.bfloat16
MESH = pl.DeviceIdType.MESH

D_MODEL = 1024
HEAD = 128
HG_HEADS = 8
DA_GROUPS = ((1, 64), (4, 64), (16, 64))
DA_GH = 4
MEM_HEADS = 4
N_MEM = 256
D_FF = 2816
CHUNK = 64
BAND_QBLOCKS = 4
GLA_HEADS_PER_STEP = 2
RMS_EPS = 1e-6
NEG_INF = -1e30
HG_SCALE = HEAD ** -0.5
ATT_SCALE = HEAD ** -0.5
VMEM_LIMIT_V7X = 48 * 1024 * 1024

C_HQ, C_FF, C_FB, C_HI, C_HG = 0, 1024, 2048, 3072, 4096
C_DQ, C_DK, C_DV, C_MQ = 5120, 6656, 8192, 9728
C_GHG, C_GDA, C_GMEM = 10240, 11264, 12288
IN_COLS = 13312

ADAM_LR, ADAM_B1, ADAM_B2, ADAM_EPS, ADAM_WD, ADAM_STEP = 0.001, 0.9, 0.999, 1e-08, 0.01, 10


def _params(sem, vmem=VMEM_LIMIT_V7X):
    return pltpu.CompilerParams(dimension_semantics=sem, vmem_limit_bytes=vmem)


def _dot(a, b):
    return jnp.dot(a.astype(BF16), b.astype(BF16), preferred_element_type=F32)


def _dot_nt(a, b):
    return lax.dot_general(a.astype(BF16), b.astype(BF16), (((1,), (1,)), ((), ())),
                           preferred_element_type=F32)


def _dot_tn(a, b):
    return lax.dot_general(a.astype(BF16), b.astype(BF16), (((0,), (0,)), ((), ())),
                           preferred_element_type=F32)


def _sigmoid(v):
    return jax.nn.sigmoid(v.astype(F32))


def _ones(rows, cols):
    return (lax.broadcasted_iota(jnp.int32, (rows, cols), 0) >= 0).astype(BF16)


def _lane_sum(v):
    ones = _ones(HEAD, HEAD)
    hi = v.astype(BF16)
    mid = (v - hi.astype(F32)).astype(BF16)
    return jnp.dot(hi, ones, preferred_element_type=F32) + jnp.dot(mid, ones, preferred_element_type=F32)


def _row_mean(v):
    if v.shape[-1] == HEAD:
        return _lane_sum(v) * (1.0 / HEAD)
    return jnp.mean(v, axis=-1, keepdims=True)


def _rms(v):
    v = v.astype(F32)
    r = lax.rsqrt(_row_mean(v * v) + RMS_EPS)
    return v * r, r


def _rms_bwd(dy, xhat, r, gain):
    dxh = dy * gain
    dx = r * (dxh - xhat * _row_mean(dxh * xhat))
    return dx, dy * xhat


def _matmul(pairs, mode, out_dtype, *, tm, tn, tk=None, residual=None, name):
    b_offs = [pr[2] if len(pr) > 2 else 0 for pr in pairs]
    pairs = [pr[:2] for pr in pairs]
    a0, b0 = pairs[0]
    if mode == "nn":
        (m, kk), n = a0.shape, b0.shape[1]
    elif mode == "nt":
        (m, kk), n = a0.shape, b0.shape[0]
    else:
        (kk, m), n = a0.shape, b0.shape[1]
    assert mode == "nt" or not any(b_offs)
    tm, tn = min(tm, m), min(tn, n)
    tk = kk if tk is None else tk
    nk = kk // tk
    assert m % tm == 0 and n % tn == 0 and kk % tk == 0, (name, m, n, kk)
    n_p = len(pairs)
    if mode == "tn":
        a_spec = pl.BlockSpec((tk, tm), lambda i, j, k: (k, i))
    else:
        a_spec = pl.BlockSpec((tm, tk), lambda i, j, k: (i, k))
    if mode == "nt":
        b_specs = [pl.BlockSpec((tn, tk), lambda i, j, k, o=o: (j, o * nk + k)) for o in b_offs]
    else:
        b_specs = [pl.BlockSpec((tk, tn), lambda i, j, k: (k, j))] * n_p
    o_spec = pl.BlockSpec((tm, tn), lambda i, j, k: (i, j))
    dot = {"nn": _dot, "nt": _dot_nt, "tn": _dot_tn}[mode]
    has_res = residual is not None

    def body(*refs):
        a_refs, b_refs = refs[:n_p], refs[n_p:2 * n_p]
        pos = 2 * n_p
        res_ref = refs[pos] if has_res else None
        pos += int(has_res)
        o_ref = refs[pos]
        part = dot(a_refs[0][...], b_refs[0][...])
        for a_r, b_r in zip(a_refs[1:], b_refs[1:]):
            part += dot(a_r[...], b_r[...])

        def finish(total):
            if has_res:
                total = total + res_ref[...]
            o_ref[...] = total.astype(out_dtype)

        if nk == 1:
            finish(part)
        else:
            acc_ref = refs[pos + 1]
            k = pl.program_id(2)

            @pl.when(k == 0)
            def _():
                acc_ref[...] = part

            @pl.when(k > 0)
            def _():
                acc_ref[...] += part

            @pl.when(k == nk - 1)
            def _():
                finish(acc_ref[...])

    ins = [a for a, _ in pairs] + [b for _, b in pairs]
    in_specs = [a_spec] * n_p + b_specs
    if has_res:
        ins.append(residual)
        in_specs.append(o_spec)
    return pl.pallas_call(
        body, name=name, grid=(m // tm, n // tn, nk),
        in_specs=in_specs, out_specs=o_spec,
        out_shape=jax.ShapeDtypeStruct((m, n), out_dtype),
        scratch_shapes=[pltpu.VMEM((tm, tn), F32)] if nk > 1 else [],
        compiler_params=_params(("parallel", "parallel", "arbitrary")),
    )(*ins)


def _rms_fwd(x, gain, *, name):
    rows, dm = x.shape
    tm = min(512, rows)

    def body(x_ref, g_ref, h_ref):
        xhat, _ = _rms(x_ref[...])
        h_ref[...] = (xhat * g_ref[...]).astype(BF16)

    return pl.pallas_call(
        body, name=name, grid=(rows // tm,),
        in_specs=[pl.BlockSpec((tm, dm), lambda i: (i, 0)), pl.BlockSpec((1, dm), lambda i: (0, 0))],
        out_specs=pl.BlockSpec((tm, dm), lambda i: (i, 0)),
        out_shape=jax.ShapeDtypeStruct((rows, dm), BF16),
        compiler_params=_params(("parallel",)),
    )(x, gain)


def _rms_bwd_rows(dh, x, gain, dres, *, name):
    rows, dm = x.shape
    tm = min(512, rows)
    has_res = dres is not None

    def body(*refs):
        dh_ref, x_ref, g_ref = refs[:3]
        res_ref = refs[3] if has_res else None
        dx_ref, dxb_ref, dg_ref = refs[3 + int(has_res):]
        xhat, r = _rms(x_ref[...])
        dx, dgr = _rms_bwd(dh_ref[...], xhat, r, g_ref[...])
        if has_res:
            dx = dx + res_ref[...]
        dx_ref[...] = dx
        dxb_ref[...] = dx.astype(BF16)

        @pl.when(pl.program_id(0) == 0)
        def _():
            dg_ref[...] = jnp.zeros_like(dg_ref)

        dg_ref[...] += jnp.sum(dgr, axis=0, keepdims=True)

    row = pl.BlockSpec((tm, dm), lambda i: (i, 0))
    vec = pl.BlockSpec((1, dm), lambda i: (0, 0))
    return pl.pallas_call(
        body, name=name, grid=(rows // tm,),
        in_specs=[row, row, vec] + ([row] if has_res else []),
        out_specs=[row, row, vec],
        out_shape=[jax.ShapeDtypeStruct((rows, dm), F32), jax.ShapeDtypeStruct((rows, dm), BF16),
                   jax.ShapeDtypeStruct((1, dm), F32)],
        compiler_params=_params(("arbitrary",)),
    )(*([dh, x, gain] + ([dres] if has_res else [])))


def _loss_grad(y, tgt, *, name):
    rows, dm = y.shape
    tm = min(512, rows)
    steps = rows // tm

    def body(y_ref, t_ref, dy_ref, dyb_ref, loss_ref, acc_ref):
        i = pl.program_id(0)
        diff = y_ref[...] - t_ref[...]
        dy = diff * (1.0 / dm)
        dy_ref[...] = dy
        dyb_ref[...] = dy.astype(BF16)

        @pl.when(i == 0)
        def _():
            acc_ref[...] = jnp.zeros_like(acc_ref)

        acc_ref[...] += jnp.sum(diff * diff, axis=0, keepdims=True)

        @pl.when(i == steps - 1)
        def _():
            loss_ref[...] = jnp.full((1, HEAD), 0.5 / dm, F32) * jnp.sum(acc_ref[...])

    row = pl.BlockSpec((tm, dm), lambda i: (i, 0))
    return pl.pallas_call(
        body, name=name, grid=(steps,),
        in_specs=[row, row],
        out_specs=[row, row, pl.BlockSpec((1, HEAD), lambda i: (0, 0))],
        out_shape=[jax.ShapeDtypeStruct((rows, dm), F32), jax.ShapeDtypeStruct((rows, dm), BF16),
                   jax.ShapeDtypeStruct((1, HEAD), F32)],
        scratch_shapes=[pltpu.VMEM((1, dm), F32)],
        compiler_params=_params(("arbitrary",)),
    )(y, tgt)


def _gla_block_terms(q_raw, f_logit, lb, tri):
    sig = _sigmoid(f_logit)
    forget = lb + (1.0 - lb) * sig
    k = 1.0 - forget
    b = _masked_sum(tri, jnp.log(forget))
    qs = _sigmoid(q_raw)
    eb = jnp.exp(b)
    emb = jnp.exp(-b)
    qt = (q_raw * qs * HG_SCALE) * eb
    kt = k * emb
    return sig, forget, k, b, qs, eb, emb, qt, kt


def _masked_sum(mask, v):
    mb = mask.astype(BF16)
    hi = v.astype(BF16)
    mid = (v - hi.astype(F32)).astype(BF16)
    return jnp.dot(mb, hi, preferred_element_type=F32) + jnp.dot(mb, mid, preferred_element_type=F32)


def _tri_mask(n, rev):
    row = lax.broadcasted_iota(jnp.int32, (n, n), 0)
    col = lax.broadcasted_iota(jnp.int32, (n, n), 1)
    shift = CHUNK.bit_length() - 1
    same = jnp.right_shift(row, shift) == jnp.right_shift(col, shift)
    return same & ((row <= col) if rev else (row >= col))


def _chunk_order(ncb, rev):
    order = range(ncb - 1, -1, -1) if rev else range(ncb)
    return [(c, c * CHUNK if rev else c * CHUNK + CHUNK - 1) for c in order]


def _gla_fwd(proj, lb, *, f_off, rev, name):
    rows = proj.shape[0]
    tb = min(256, rows)
    nb, ncb = rows // tb, tb // CHUNK

    def tmap(n):
        return nb - 1 - n if rev else n

    def body(q_ref, f_ref, v_ref, lb_ref, o_ref, st_ref, s_ref):
        @pl.when(pl.program_id(1) == 0)
        def _():
            s_ref[...] = jnp.zeros_like(s_ref)

        tri = _tri_mask(tb, rev)
        for hh in range(GLA_HEADS_PER_STEP):
            cs = slice(hh * HEAD, (hh + 1) * HEAD)
            v = v_ref[:, cs]
            _, _, k, b, _, _, _, qt, kt = _gla_block_terms(q_ref[:, cs].astype(F32), f_ref[:, cs].astype(F32),
                                                           lb_ref[:, cs], tri)
            o_intra = _dot(jnp.where(tri, _dot_nt(qt, kt), 0.0), v)
            chunks = []
            for c, last in _chunk_order(ncb, rev):
                sl = slice(c * CHUNK, (c + 1) * CHUNK)
                bl = b[last:last + 1, :]
                kh = k[sl] * jnp.exp(bl - b[sl])
                chunks.append((c, sl, jnp.exp(bl), _dot_tn(v[sl], kh)))
            s_t = s_ref[hh]
            for c, sl, ebl, kv in chunks:
                st_ref[hh, c] = s_t
                o_ref[sl, cs] = o_intra[sl] + _dot_nt(qt[sl], s_t)
                s_t = ebl * s_t + kv
            s_ref[hh] = s_t

    hps = GLA_HEADS_PER_STEP
    col = lambda off: pl.BlockSpec((tb, hps * HEAD), lambda h, n: (tmap(n), off // (hps * HEAD) + h))
    return pl.pallas_call(
        body, name=name, grid=(HG_HEADS // hps, nb),
        in_specs=[col(C_HQ), col(f_off), col(C_HI), pl.BlockSpec((1, hps * HEAD), lambda h, n: (0, h))],
        out_specs=[pl.BlockSpec((tb, hps * HEAD), lambda h, n: (tmap(n), h)),
                   pl.BlockSpec((hps, ncb, HEAD, HEAD), lambda h, n: (h, tmap(n), 0, 0))],
        out_shape=[jax.ShapeDtypeStruct((rows, HG_HEADS * HEAD), F32),
                   jax.ShapeDtypeStruct((HG_HEADS, rows // CHUNK, HEAD, HEAD), F32)],
        scratch_shapes=[pltpu.VMEM((hps, HEAD, HEAD), F32)],
        compiler_params=_params(("parallel", "arbitrary")),
    )(proj, proj, proj, lb)


def _gla_bwd(proj, lb, do, states, prev, *, f_off, rev, name):
    rows = proj.shape[0]
    tb = min(256, rows)
    nb, ncb = rows // tb, tb // CHUNK
    has_prev = prev is not None
    qv_dtype = BF16 if has_prev else F32

    def tmap(n):
        return n if rev else nb - 1 - n

    def body(*refs):
        q_ref, f_ref, v_ref, lb_ref, do_ref, st_ref = refs[:6]
        pq_ref, pv_ref = refs[6:8] if has_prev else (None, None)
        (dq_ref, df_ref, dv_ref, dlb_ref,
         ds_ref, dqt_scr, dk_scr, db_scr, dbl_scr, dv_scr) = refs[6 + 2 * int(has_prev):]

        @pl.when(pl.program_id(1) == 0)
        def _():
            ds_ref[...] = jnp.zeros_like(ds_ref)
            dlb_ref[...] = jnp.zeros_like(dlb_ref)

        tri = _tri_mask(tb, rev)
        tri_t = _tri_mask(tb, not rev)
        for hh in range(GLA_HEADS_PER_STEP):
            cs = slice(hh * HEAD, (hh + 1) * HEAD)
            lbv = lb_ref[:, cs]
            q_raw = q_ref[:, cs].astype(F32)
            v, dout = v_ref[:, cs], do_ref[:, cs].astype(BF16)
            sig, forget, k, b, qs, eb, emb, qt, kt = _gla_block_terms(q_raw, f_ref[:, cs].astype(F32), lbv, tri)
            a = jnp.where(tri, _dot_nt(qt, kt), 0.0)
            da = jnp.where(tri, _dot_nt(dout, v), 0.0)
            dv_intra = _dot_tn(a, dout)
            dqt_intra = _dot(da, kt)
            dkt = _dot_tn(da, qt)
            chunks = []
            for c, last in reversed(_chunk_order(ncb, rev)):
                sl = slice(c * CHUNK, (c + 1) * CHUNK)
                bl = b[last:last + 1, :]
                e = jnp.exp(bl - b[sl])
                s_t = st_ref[hh, c]
                dqt_scr[sl, cs] = dqt_intra[sl] + _dot(dout[sl], s_t)
                chunks.append((sl, jnp.exp(bl), e, k[sl] * e, s_t, _dot_tn(dout[sl], qt[sl])))
            ds_t = ds_ref[hh]
            for sl, ebl, e, kh, s_t, grow in chunks:
                dkh = _dot(v[sl], ds_t)
                dv_scr[sl, cs] = dv_intra[sl] + _dot_nt(kh, ds_t)
                dk_scr[sl, cs] = dkt[sl] * emb[sl] + dkh * e
                khd = kh * dkh
                dbl = jnp.sum(khd, axis=0, keepdims=True) + ebl * jnp.sum(ds_t * s_t, axis=0, keepdims=True)
                db_scr[sl, cs] = khd
                dbl_scr[sl, cs] = jnp.broadcast_to(dbl, (CHUNK, HEAD))
                ds_t = grow + ds_t * ebl
            ds_ref[hh] = ds_t
            dqt = dqt_scr[:, cs]
            dlogf = _masked_sum(tri_t, qt * dqt - kt * dkt - db_scr[:, cs]) + dbl_scr[:, cs]
            dforget = dlogf / forget - dk_scr[:, cs]
            df_ref[:, cs] = (dforget * (1.0 - lbv) * sig * (1.0 - sig)).astype(BF16)
            dlb_ref[:, cs] += jnp.sum(dforget * (1.0 - sig), axis=0, keepdims=True)
            dqr = dqt * eb * (HG_SCALE * qs * (1.0 + q_raw * (1.0 - qs)))
            dv = dv_scr[:, cs]
            if has_prev:
                dqr = dqr + pq_ref[:, cs]
                dv = dv + pv_ref[:, cs]
            dq_ref[:, cs] = dqr.astype(qv_dtype)
            dv_ref[:, cs] = dv.astype(qv_dtype)

    hps = GLA_HEADS_PER_STEP
    col = lambda off: pl.BlockSpec((tb, hps * HEAD), lambda h, n: (tmap(n), off // (hps * HEAD) + h))
    blk = pl.BlockSpec((tb, hps * HEAD), lambda h, n: (tmap(n), h))
    vec = pl.BlockSpec((1, hps * HEAD), lambda h, n: (0, h))
    wide = HG_HEADS * HEAD
    return pl.pallas_call(
        body, name=name, grid=(HG_HEADS // hps, nb),
        in_specs=[col(C_HQ), col(f_off), col(C_HI), vec, blk,
                  pl.BlockSpec((hps, ncb, HEAD, HEAD), lambda h, n: (h, tmap(n), 0, 0))]
                 + ([blk, blk] if has_prev else []),
        out_specs=[blk, blk, blk, vec],
        out_shape=[jax.ShapeDtypeStruct((rows, wide), qv_dtype), jax.ShapeDtypeStruct((rows, wide), BF16),
                   jax.ShapeDtypeStruct((rows, wide), qv_dtype), jax.ShapeDtypeStruct((1, wide), F32)],
        scratch_shapes=[pltpu.VMEM((hps, HEAD, HEAD), F32)] + [pltpu.VMEM((tb, hps * HEAD), F32)] * 5,
        compiler_params=_params(("parallel", "arbitrary")),
    )(*([proj, proj, proj, lb, do, states] + (list(prev) if has_prev else [])))


def _hg_out_fwd(o_fw, o_bw, proj, gain, *, name):
    rows = o_fw.shape[0]
    tm = min(512, rows)
    wide = HG_HEADS * HEAD

    def body(a_ref, b_ref, g_ref, gain_ref, o_ref):
        for h in range(HG_HEADS):
            sl = slice(h * HEAD, (h + 1) * HEAD)
            xhat, _ = _rms(a_ref[:, sl] + b_ref[:, sl])
            gate = g_ref[:, sl].astype(F32)
            o_ref[:, sl] = (xhat * gain_ref[...] * (gate * _sigmoid(gate))).astype(BF16)

    row = pl.BlockSpec((tm, wide), lambda i: (i, 0))
    return pl.pallas_call(
        body, name=name, grid=(rows // tm,),
        in_specs=[row, row, pl.BlockSpec((tm, wide), lambda i: (i, C_HG // wide)),
                  pl.BlockSpec((1, HEAD), lambda i: (0, 0))],
        out_specs=row, out_shape=jax.ShapeDtypeStruct((rows, wide), BF16),
        compiler_params=_params(("parallel",)),
    )(o_fw, o_bw, proj, gain)


def _hg_out_bwd(dout, o_fw, o_bw, proj, gain, *, name):
    rows = o_fw.shape[0]
    tm = min(512, rows)
    wide = HG_HEADS * HEAD

    def body(d_ref, a_ref, b_ref, g_ref, gain_ref, dgate_ref, do_ref, dgain_ref):
        @pl.when(pl.program_id(0) == 0)
        def _():
            dgain_ref[...] = jnp.zeros_like(dgain_ref)

        dgain = jnp.zeros((1, HEAD), F32)
        for h in range(HG_HEADS):
            sl = slice(h * HEAD, (h + 1) * HEAD)
            xhat, r = _rms(a_ref[:, sl] + b_ref[:, sl])
            gate, dy = g_ref[:, sl].astype(F32), d_ref[:, sl]
            gs = _sigmoid(gate)
            dgate_ref[:, sl] = (dy * xhat * gain_ref[...] * (gs * (1.0 + gate * (1.0 - gs)))).astype(BF16)
            dx, dgr = _rms_bwd(dy * (gate * gs), xhat, r, gain_ref[...])
            do_ref[:, sl] = dx
            dgain = dgain + jnp.sum(dgr, axis=0, keepdims=True)
        dgain_ref[...] += dgain

    row = pl.BlockSpec((tm, wide), lambda i: (i, 0))
    vec = pl.BlockSpec((1, HEAD), lambda i: (0, 0))
    return pl.pallas_call(
        body, name=name, grid=(rows // tm,),
        in_specs=[row, row, row, pl.BlockSpec((tm, wide), lambda i: (i, C_HG // wide)), vec],
        out_specs=[row, row, vec],
        out_shape=[jax.ShapeDtypeStruct((rows, wide), BF16), jax.ShapeDtypeStruct((rows, wide), F32),
                   jax.ShapeDtypeStruct((1, HEAD), F32)],
        compiler_params=_params(("arbitrary",)),
    )(dout, o_fw, o_bw, proj, gain)


def _strided_rows(ref, r, count, d):
    return ref[...] if d == 1 else ref[pl.ds(r, count, stride=d), :]


def _store_strided(ref, r, count, d, val):
    if d == 1:
        ref[...] = val
    else:
        ref[pl.ds(r, count, stride=d), :] = val


def _da_prep(proj, q_gain, k_gain, g, *, name):
    d = DA_GROUPS[g][0]
    rows = proj.shape[0]
    rb = min(512, rows)
    tn = rb // d

    def body(q_ref, k_ref, v_ref, qg_ref, kg_ref, qo_ref, ko_ref, vo_ref, qf_ref, kf_ref, vf_ref):
        for src, dst in ((q_ref, qf_ref), (k_ref, kf_ref), (v_ref, vf_ref)):
            dst[...] = src[...].astype(F32)
        for r in range(d):
            qhat, _ = _rms(_strided_rows(qf_ref, r, tn, d))
            khat, _ = _rms(_strided_rows(kf_ref, r, tn, d))
            qo_ref[r] = (qhat * qg_ref[...]).astype(BF16)
            ko_ref[r] = (khat * kg_ref[...]).astype(BF16)
            vo_ref[r] = _strided_rows(vf_ref, r, tn, d).astype(BF16)

    col = lambda off: pl.BlockSpec((rb, HEAD), lambda h, i: (i, (off + g * DA_GH * HEAD) // HEAD + h))
    vec = pl.BlockSpec((1, HEAD), lambda h, i: (0, 0))
    out = pl.BlockSpec((None, d, tn, HEAD), lambda h, i: (h, 0, i, 0))
    shape = jax.ShapeDtypeStruct((DA_GH, d, rows // d, HEAD), BF16)
    return pl.pallas_call(
        body, name=name, grid=(DA_GH, rows // rb),
        in_specs=[col(C_DQ), col(C_DK), col(C_DV), vec, vec],
        out_specs=[out, out, out], out_shape=[shape, shape, shape],
        scratch_shapes=[pltpu.VMEM((rb, HEAD), F32)] * 3,
        compiler_params=_params(("parallel", "parallel")),
    )(proj, proj, proj, q_gain, k_gain)


def _slopes(g):
    idx = np.arange(g * DA_GH + 1, (g + 1) * DA_GH + 1)
    s = (2.0 ** (-8.0 * idx / (DA_GH * len(DA_GROUPS)))).astype(np.float32)
    return jnp.asarray(np.broadcast_to(s[:, None, None], (DA_GH, 8, HEAD)).copy())


def _band_bias(t, jj, slope, d, radius):
    row = lax.broadcasted_iota(jnp.int32, (t, t), 0)
    col = lax.broadcasted_iota(jnp.int32, (t, t), 1)
    rel = jnp.abs((jj - 1) * t + col - row)
    return jnp.where(rel <= radius, -slope * (d * rel).astype(F32), NEG_INF)


def _band_fwd(qr, kr, vr, g, *, name):
    d, radius = DA_GROUPS[g]
    _, _, ld, _ = qr.shape
    t = min(HEAD, ld)
    qb = min(BAND_QBLOCKS, ld // t)
    nb, nbt = ld // (qb * t), ld // t

    def body(q_ref, kp_ref, ko_ref, kn_ref, vp_ref, vo_ref, vn_ref, sl_ref, o_ref, lse_ref):
        i = pl.program_id(2)
        slope = sl_ref[0:1, 0:1]
        bias = [_band_bias(t, jj, slope, d, radius) for jj in range(3)]

        def block(own_ref, prev_ref, next_ref, kk):
            if kk < 0:
                return prev_ref[...]
            return next_ref[...] if kk == qb else own_ref[kk * t:(kk + 1) * t, :]

        for j in range(qb):
            q = q_ref[j * t:(j + 1) * t, :]
            s = []
            for jj in range(3):
                sj = _dot_nt(q, block(ko_ref, kp_ref, kn_ref, j + jj - 1)) * ATT_SCALE + bias[jj]
                kblock = i * qb + j + jj - 1
                s.append(jnp.where((kblock >= 0) & (kblock < nbt), sj, NEG_INF))
            m = jnp.maximum(jnp.maximum(jnp.max(s[0], axis=-1, keepdims=True), jnp.max(s[1], axis=-1, keepdims=True)),
                            jnp.max(s[2], axis=-1, keepdims=True))
            p = [jnp.exp(sj - m).astype(BF16) for sj in s]
            ones = _ones(t, HEAD)
            l = sum(jnp.dot(pj, ones, preferred_element_type=F32) for pj in p)
            o = sum(_dot(pj, block(vo_ref, vp_ref, vn_ref, j + jj - 1)) for jj, pj in enumerate(p))
            o_ref[j * t:(j + 1) * t, :] = o / l
            lse_ref[j * t:(j + 1) * t, :] = m + jnp.log(l)

    own = pl.BlockSpec((None, None, qb * t, HEAD), lambda h, r, i: (h, r, i, 0))
    prv = pl.BlockSpec((None, None, t, HEAD), lambda h, r, i: (h, r, jnp.maximum(i * qb - 1, 0), 0))
    nxt = pl.BlockSpec((None, None, t, HEAD), lambda h, r, i: (h, r, jnp.minimum((i + 1) * qb, nbt - 1), 0))
    shape = jax.ShapeDtypeStruct(qr.shape, F32)
    return pl.pallas_call(
        body, name=name, grid=(DA_GH, d, nb),
        in_specs=[own, prv, own, nxt, prv, own, nxt, pl.BlockSpec((None, 8, HEAD), lambda h, r, i: (h, 0, 0))],
        out_specs=[own, own], out_shape=[shape, shape],
        compiler_params=_params(("parallel", "parallel", "parallel")),
    )(qr, kr, kr, kr, vr, vr, vr, _slopes(g))


def _band_bwd(qr, kr, vr, dor, lser, deltar, g, *, name):
    d, radius = DA_GROUPS[g]
    _, _, ld, _ = qr.shape
    t = min(HEAD, ld)
    qb = min(BAND_QBLOCKS, ld // t)
    nb, nbt = ld // (qb * t), ld // t

    def body(q_ref, k_ref, v_ref, do_ref, lse_ref, dl_ref, sl_ref, dq_ref, dk_ref, dv_ref):
        i = pl.program_id(2)

        @pl.when(i == 0)
        def _():
            dk_ref[...] = jnp.zeros_like(dk_ref)
            dv_ref[...] = jnp.zeros_like(dv_ref)

        slope = sl_ref[0:1, 0:1]
        bias = [_band_bias(t, jj, slope, d, radius) for jj in range(3)]
        rows = [pl.ds(pl.multiple_of(jnp.clip(i * qb + kk - 1, 0, nbt - 1) * t, t), t) for kk in range(qb + 2)]
        ks = [k_ref[r, :] for r in rows]
        vs = [v_ref[r, :] for r in rows]
        dks, dvs = [None] * (qb + 2), [None] * (qb + 2)
        for j in range(qb):
            sl = slice(j * t, (j + 1) * t)
            q, dout = q_ref[sl, :], do_ref[sl, :]
            lse, delta = lse_ref[sl, 0:1], dl_ref[sl, 0:1]
            dq = None
            for jj in range(3):
                kk = j + jj
                kblock = i * qb + kk - 1
                sj = _dot_nt(q, ks[kk]) * ATT_SCALE + bias[jj]
                p = jnp.exp(jnp.where((kblock >= 0) & (kblock < nbt), sj, NEG_INF) - lse)
                ds = p * (_dot_nt(dout, vs[kk]) - delta)
                part = _dot(ds, ks[kk])
                dq = part if dq is None else dq + part
                dkp, dvp = _dot_tn(ds, q), _dot_tn(p, dout)
                dks[kk] = dkp if dks[kk] is None else dks[kk] + dkp
                dvs[kk] = dvp if dvs[kk] is None else dvs[kk] + dvp
            dq_ref[sl, :] = dq * ATT_SCALE
        for kk in range(qb + 2):
            dk_ref[rows[kk], :] += dks[kk] * ATT_SCALE
            dv_ref[rows[kk], :] += dvs[kk]

    own = pl.BlockSpec((None, None, qb * t, HEAD), lambda h, r, i: (h, r, i, 0))
    seq = pl.BlockSpec((None, None, ld, HEAD), lambda h, r, i: (h, r, 0, 0))
    shape = jax.ShapeDtypeStruct(qr.shape, F32)
    return pl.pallas_call(
        body, name=name, grid=(DA_GH, d, nb),
        in_specs=[own, seq, seq, own, own, own, pl.BlockSpec((None, 8, HEAD), lambda h, r, i: (h, 0, 0))],
        out_specs=[own, seq, seq], out_shape=[shape, shape, shape],
        compiler_params=_params(("parallel", "parallel", "arbitrary")),
    )(qr, kr, vr, dor, lser, deltar, _slopes(g))


def _da_merge(outs, lses, rows, *, name):
    rb = min(512, rows)

    def body(*refs):
        o_refs, l_refs = refs[0:3], refs[3:6]
        o_ref, lse_ref = refs[6:8]
        on_refs, ln_refs = refs[8:11], refs[11:14]
        for g, (d, _) in enumerate(DA_GROUPS):
            tn = rb // d
            for r in range(d):
                _store_strided(on_refs[g], r, tn, d, o_refs[g][r])
                _store_strided(ln_refs[g], r, tn, d, l_refs[g][r])
        l0, l1, l2 = ln_refs[0][...], ln_refs[1][...], ln_refs[2][...]
        m = jnp.maximum(jnp.maximum(l0, l1), l2)
        e0, e1, e2 = jnp.exp(l0 - m), jnp.exp(l1 - m), jnp.exp(l2 - m)
        tot = e0 + e1 + e2
        o_ref[...] = (e0 * on_refs[0][...] + e1 * on_refs[1][...] + e2 * on_refs[2][...]) / tot
        lse_ref[...] = m + jnp.log(tot)

    res = lambda d: pl.BlockSpec((None, d, rb // d, HEAD), lambda h, i: (h, 0, i, 0))
    nat = pl.BlockSpec((rb, HEAD), lambda h, i: (i, h))
    shape = jax.ShapeDtypeStruct((rows, DA_GH * HEAD), F32)
    return pl.pallas_call(
        body, name=name, grid=(DA_GH, rows // rb),
        in_specs=[res(d) for d, _ in DA_GROUPS] * 2,
        out_specs=[nat, nat], out_shape=[shape, shape],
        scratch_shapes=[pltpu.VMEM((rb, HEAD), F32)] * 6,
        compiler_params=_params(("parallel", "parallel")),
    )(*outs, *lses)


def _da_bwd_prep(dout, o, lse, *, name):
    rows = o.shape[0]
    rb = min(512, rows)

    def body(d_ref, o_ref, l_ref, *outs):
        delta_ref = outs[9]
        delta_ref[...] = _lane_sum(d_ref[...] * o_ref[...])
        for g, (d, _) in enumerate(DA_GROUPS):
            tn = rb // d
            for r in range(d):
                outs[3 * g][r] = _strided_rows(d_ref, r, tn, d).astype(BF16)
                outs[3 * g + 1][r] = _strided_rows(l_ref, r, tn, d)
                outs[3 * g + 2][r] = _strided_rows(delta_ref, r, tn, d)

    nat = pl.BlockSpec((rb, HEAD), lambda h, i: (i, h))
    out_specs, out_shape = [], []
    for d, _ in DA_GROUPS:
        for dt in (BF16, F32, F32):
            out_specs.append(pl.BlockSpec((None, d, rb // d, HEAD), lambda h, i: (h, 0, i, 0)))
            out_shape.append(jax.ShapeDtypeStruct((DA_GH, d, rows // d, HEAD), dt))
    return pl.pallas_call(
        body, name=name, grid=(DA_GH, rows // rb),
        in_specs=[nat, nat, nat], out_specs=out_specs, out_shape=out_shape,
        scratch_shapes=[pltpu.VMEM((rb, HEAD), F32)],
        compiler_params=_params(("parallel", "parallel")),
    )(dout, o, lse)


def _da_prep_bwd(dqr, dkr, dvr, proj, q_gain, k_gain, g, *, name):
    d = DA_GROUPS[g][0]
    rows = proj.shape[0]
    rb = min(512, rows)
    tn = rb // d

    def body(dq_ref, dk_ref, dv_ref, q_ref, k_ref, qg_ref, kg_ref, oq_ref, ok_ref, ov_ref, gq_ref, gk_ref, *nat_refs):
        @pl.when((pl.program_id(0) == 0) & (pl.program_id(1) == 0))
        def _():
            gq_ref[...] = jnp.zeros_like(gq_ref)
            gk_ref[...] = jnp.zeros_like(gk_ref)

        for j, src in enumerate((dq_ref, dk_ref, dv_ref)):
            for r in range(d):
                _store_strided(nat_refs[j], r, tn, d, src[r])
        ov_ref[...] = nat_refs[2][...].astype(BF16)
        for j, (x_ref, gn_ref, out_ref, acc_ref) in enumerate(((q_ref, qg_ref, oq_ref, gq_ref),
                                                                (k_ref, kg_ref, ok_ref, gk_ref))):
            xhat, r = _rms(x_ref[...])
            dx, dgr = _rms_bwd(nat_refs[j][...], xhat, r, gn_ref[...])
            out_ref[...] = dx.astype(BF16)
            acc_ref[...] += jnp.sum(dgr, axis=0, keepdims=True)

    res = pl.BlockSpec((None, d, tn, HEAD), lambda h, i: (h, 0, i, 0))
    col = lambda off: pl.BlockSpec((rb, HEAD), lambda h, i: (i, (off + g * DA_GH * HEAD) // HEAD + h))
    vec = pl.BlockSpec((1, HEAD), lambda h, i: (0, 0))
    nat = pl.BlockSpec((rb, HEAD), lambda h, i: (i, h))
    shape = jax.ShapeDtypeStruct((rows, DA_GH * HEAD), BF16)
    gshape = jax.ShapeDtypeStruct((1, HEAD), F32)
    return pl.pallas_call(
        body, name=name, grid=(DA_GH, rows // rb),
        in_specs=[res, res, res, col(C_DQ), col(C_DK), vec, vec],
        out_specs=[nat, nat, nat, vec, vec], out_shape=[shape, shape, shape, gshape, gshape],
        scratch_shapes=[pltpu.VMEM((rb, HEAD), F32)] * 3,
        compiler_params=_params(("arbitrary", "arbitrary")),
    )(dqr, dkr, dvr, proj, proj, q_gain, k_gain)


def _mem_fwd(proj, kv, q_gain, k_gain, *, name):
    rows = proj.shape[0]
    tm = min(512, rows)
    n_mem = kv.shape[0]

    def body(q_ref, k_ref, v_ref, qg_ref, kg_ref, o_ref):
        qhat, _ = _rms(q_ref[...])
        khat, _ = _rms(k_ref[...])
        s = _dot_nt(qhat * qg_ref[...], khat * kg_ref[...]) * ATT_SCALE
        p = jnp.exp(s - jnp.max(s, axis=-1, keepdims=True))
        p = p / jnp.sum(p, axis=-1, keepdims=True)
        o_ref[...] = _dot(p, v_ref[...]).astype(BF16)

    vec = pl.BlockSpec((1, HEAD), lambda i, h: (0, 0))
    return pl.pallas_call(
        body, name=name, grid=(rows // tm, MEM_HEADS),
        in_specs=[pl.BlockSpec((tm, HEAD), lambda i, h: (i, C_MQ // HEAD + h)),
                  pl.BlockSpec((n_mem, HEAD), lambda i, h: (0, h)),
                  pl.BlockSpec((n_mem, HEAD), lambda i, h: (0, MEM_HEADS + h)), vec, vec],
        out_specs=pl.BlockSpec((tm, HEAD), lambda i, h: (i, h)),
        out_shape=jax.ShapeDtypeStruct((rows, MEM_HEADS * HEAD), BF16),
        compiler_params=_params(("parallel", "parallel")),
    )(proj, kv, kv, q_gain, k_gain)


def _mem_bwd(dout, proj, kv, q_gain, k_gain, *, name):
    rows = proj.shape[0]
    tm = min(512, rows)
    steps = rows // tm
    n_mem = kv.shape[0]

    def body(d_ref, q_ref, k_ref, v_ref, qg_ref, kg_ref, dq_ref, dk_ref, dv_ref, gq_ref, gk_ref, dkn_ref):
        h, i = pl.program_id(0), pl.program_id(1)

        @pl.when((h == 0) & (i == 0))
        def _():
            gq_ref[...] = jnp.zeros_like(gq_ref)
            gk_ref[...] = jnp.zeros_like(gk_ref)

        @pl.when(i == 0)
        def _():
            dkn_ref[...] = jnp.zeros_like(dkn_ref)
            dv_ref[...] = jnp.zeros_like(dv_ref)

        qhat, rq = _rms(q_ref[...])
        khat, rk = _rms(k_ref[...])
        qn, kn = qhat * qg_ref[...], khat * kg_ref[...]
        s = _dot_nt(qn, kn) * ATT_SCALE
        p = jnp.exp(s - jnp.max(s, axis=-1, keepdims=True))
        p = p / jnp.sum(p, axis=-1, keepdims=True)
        dout = d_ref[...]
        dp = _dot_nt(dout, v_ref[...])
        ds = p * (dp - jnp.sum(p * dp, axis=-1, keepdims=True))
        dv_ref[...] += _dot_tn(p, dout)
        dkn_ref[...] += _dot_tn(ds, qn) * ATT_SCALE
        dq, dgr = _rms_bwd(_dot(ds, kn) * ATT_SCALE, qhat, rq, qg_ref[...])
        dq_ref[...] = dq.astype(BF16)
        gq_ref[...] += jnp.sum(dgr, axis=0, keepdims=True)

        @pl.when(i == steps - 1)
        def _():
            dk, dgk = _rms_bwd(dkn_ref[...], khat, rk, kg_ref[...])
            dk_ref[...] = dk
            gk_ref[...] += jnp.sum(dgk, axis=0, keepdims=True)

    vec = pl.BlockSpec((1, HEAD), lambda h, i: (0, 0))
    memh = pl.BlockSpec((n_mem, HEAD), lambda h, i: (0, h))
    tok = pl.BlockSpec((tm, HEAD), lambda h, i: (i, h))
    gshape = jax.ShapeDtypeStruct((1, HEAD), F32)
    return pl.pallas_call(
        body, name=name, grid=(MEM_HEADS, steps),
        in_specs=[tok, pl.BlockSpec((tm, HEAD), lambda h, i: (i, C_MQ // HEAD + h)), memh,
                  pl.BlockSpec((n_mem, HEAD), lambda h, i: (0, MEM_HEADS + h)), vec, vec],
        out_specs=[tok, memh, memh, vec, vec],
        out_shape=[jax.ShapeDtypeStruct((rows, MEM_HEADS * HEAD), BF16),
                   jax.ShapeDtypeStruct((n_mem, MEM_HEADS * HEAD), F32),
                   jax.ShapeDtypeStruct((n_mem, MEM_HEADS * HEAD), F32), gshape, gshape],
        scratch_shapes=[pltpu.VMEM((n_mem, HEAD), F32)],
        compiler_params=_params(("arbitrary", "arbitrary")),
    )(dout, proj, kv, kv, q_gain, k_gain)


def _branch_fwd(o_hg, o_da, o_mem, proj, wp_hg, wp_da, wp_mem, *, name):
    rows = o_hg.shape[0]
    tm = min(256, rows)

    def body(a_ref, b_ref, c_ref, ga_ref, gb_ref, gc_ref, wa_ref, wb_ref, wc_ref, o_ref):
        merged = _sigmoid(ga_ref[...]) * _dot(a_ref[...], wa_ref[...])
        merged += _sigmoid(gb_ref[...]) * _dot(b_ref[...], wb_ref[...])
        merged += _sigmoid(gc_ref[...]) * _dot(c_ref[...], wc_ref[...])
        o_ref[...] = merged.astype(BF16)

    row = lambda w: pl.BlockSpec((tm, w), lambda i: (i, 0))
    gate = lambda off: pl.BlockSpec((tm, D_MODEL), lambda i: (i, off // D_MODEL))
    full = lambda a: pl.BlockSpec(a.shape, lambda i: (0, 0))
    return pl.pallas_call(
        body, name=name, grid=(rows // tm,),
        in_specs=[row(o_hg.shape[1]), row(o_da.shape[1]), row(o_mem.shape[1]),
                  gate(C_GHG), gate(C_GDA), gate(C_GMEM), full(wp_hg), full(wp_da), full(wp_mem)],
        out_specs=row(D_MODEL), out_shape=jax.ShapeDtypeStruct((rows, D_MODEL), BF16),
        compiler_params=_params(("parallel",)),
    )(o_hg, o_da, o_mem, proj, proj, proj, wp_hg, wp_da, wp_mem)


def _branch_bwd(dm, o_hg, o_da, o_mem, proj, wp_hg, wp_da, wp_mem, *, name):
    rows = o_hg.shape[0]
    tm = min(256, rows)

    def body(dm_ref, a_ref, b_ref, c_ref, ga_ref, gb_ref, gc_ref, wa_ref, wb_ref, wc_ref, *outs):
        dmv = dm_ref[...]
        for j, (o_ref, g_ref, w_ref) in enumerate(((a_ref, ga_ref, wa_ref), (b_ref, gb_ref, wb_ref),
                                                   (c_ref, gc_ref, wc_ref))):
            z = _dot(o_ref[...], w_ref[...])
            gs = _sigmoid(g_ref[...])
            dz = (dmv * gs).astype(BF16)
            outs[3 * j][...] = (dmv * z * gs * (1.0 - gs)).astype(BF16)
            outs[3 * j + 1][...] = dz
            outs[3 * j + 2][...] = _dot_nt(dz, w_ref[...])

    row = lambda w: pl.BlockSpec((tm, w), lambda i: (i, 0))
    gate = lambda off: pl.BlockSpec((tm, D_MODEL), lambda i: (i, off // D_MODEL))
    full = lambda a: pl.BlockSpec(a.shape, lambda i: (0, 0))
    out_specs, out_shape = [], []
    for o in (o_hg, o_da, o_mem):
        out_specs += [row(D_MODEL), row(D_MODEL), row(o.shape[1])]
        out_shape += [jax.ShapeDtypeStruct((rows, D_MODEL), BF16), jax.ShapeDtypeStruct((rows, D_MODEL), BF16),
                      jax.ShapeDtypeStruct((rows, o.shape[1]), F32)]
    return pl.pallas_call(
        body, name=name, grid=(rows // tm,),
        in_specs=[row(D_MODEL), row(o_hg.shape[1]), row(o_da.shape[1]), row(o_mem.shape[1]),
                  gate(C_GHG), gate(C_GDA), gate(C_GMEM), full(wp_hg), full(wp_da), full(wp_mem)],
        out_specs=out_specs, out_shape=out_shape,
        compiler_params=_params(("parallel",)),
    )(dm, o_hg, o_da, o_mem, proj, proj, proj, wp_hg, wp_da, wp_mem)


def _ffn_in(h2, w_ab, *, name):
    rows, dff = h2.shape[0], w_ab.shape[1] // 2
    tm, tn = min(1024, rows), 256

    def body(h_ref, wa_ref, wb_ref, a_ref, b_ref, u_ref):
        a = _dot(h_ref[...], wa_ref[...])
        b = _dot(h_ref[...], wb_ref[...])
        a_ref[...] = a.astype(BF16)
        b_ref[...] = b.astype(BF16)
        u_ref[...] = (a * _sigmoid(a) * b).astype(BF16)

    out = pl.BlockSpec((tm, tn), lambda i, j: (i, j))
    return pl.pallas_call(
        body, name=name, grid=(rows // tm, dff // tn),
        in_specs=[pl.BlockSpec((tm, D_MODEL), lambda i, j: (i, 0)),
                  pl.BlockSpec((D_MODEL, tn), lambda i, j: (0, j)),
                  pl.BlockSpec((D_MODEL, tn), lambda i, j: (0, dff // tn + j))],
        out_specs=[out, out, out],
        out_shape=[jax.ShapeDtypeStruct((rows, dff), BF16)] * 3,
        compiler_params=_params(("parallel", "parallel")),
    )(h2, w_ab, w_ab)


def _ffn_act_bwd(dy, w_out, a, b, *, name):
    rows, dff = a.shape
    tm, tn = min(1024, rows), 256

    def body(dy_ref, w_ref, a_ref, b_ref, da_ref, db_ref):
        du = _dot_nt(dy_ref[...], w_ref[...])
        av, bv = a_ref[...].astype(F32), b_ref[...].astype(F32)
        sa = _sigmoid(av)
        da_ref[...] = (du * bv * sa * (1.0 + av * (1.0 - sa))).astype(BF16)
        db_ref[...] = (du * av * sa).astype(BF16)

    tile = pl.BlockSpec((tm, tn), lambda i, j: (i, j))
    return pl.pallas_call(
        body, name=name, grid=(rows // tm, dff // tn),
        in_specs=[pl.BlockSpec((tm, D_MODEL), lambda i, j: (i, 0)),
                  pl.BlockSpec((tn, D_MODEL), lambda i, j: (j, 0)), tile, tile],
        out_specs=[tile, tile],
        out_shape=[jax.ShapeDtypeStruct((rows, dff), BF16), jax.ShapeDtypeStruct((rows, dff), BF16)],
        compiler_params=_params(("parallel", "parallel")),
    )(dy, w_out, a, b)


def _lower_bound(lb_fw, lb_bw, *, name):
    def body(a_ref, b_ref, oa_ref, ob_ref):
        for src, dst in ((a_ref, oa_ref), (b_ref, ob_ref)):
            dst[...] = _sigmoid(src[0:1, :] - src[1:2, :])

    shape = jax.ShapeDtypeStruct((1, lb_fw.shape[1]), F32)
    return pl.pallas_call(body, name=name, out_shape=[shape, shape])(lb_fw, lb_bw)


def _local_step(x, mem, tgt, p, w):
    rows = x.shape[0]
    lb_fw, lb_bw = _lower_bound(p["lb_logits_fw"], p["lb_logits_bw"], name="lower_bound")

    h = _rms_fwd(x, p["norm_mix_gain"], name="norm_mix")
    proj = _matmul([(h, w["w_in"])], "nn", BF16, tm=1024, tn=512, name="proj_in")
    o_fw, st_fw = _gla_fwd(proj, lb_fw, f_off=C_FF, rev=False, name="gla_fwd_fw")
    o_bw, st_bw = _gla_fwd(proj, lb_bw, f_off=C_FB, rev=True, name="gla_fwd_bw")
    o_hg = _hg_out_fwd(o_fw, o_bw, proj, p["hg_norm_gain"], name="hg_out")

    qkv_r, outs, lses = [], [], []
    for g in range(len(DA_GROUPS)):
        qr, kr, vr = _da_prep(proj, p["da_q_gain"], p["da_k_gain"], g, name=f"da_prep{g}")
        og, lg = _band_fwd(qr, kr, vr, g, name=f"band_fwd{g}")
        qkv_r.append((qr, kr, vr))
        outs.append(og)
        lses.append(lg)
    o_da, lse_da = _da_merge(outs, lses, rows, name="da_merge")

    mem_n = _rms_fwd(mem, p["norm_mem_gain"], name="norm_mem")
    kv = _matmul([(mem_n, w["w_mem_kv"])], "nn", F32, tm=256, tn=512, name="mem_kv")
    o_mem = _mem_fwd(proj, kv, p["mem_q_gain"], p["mem_k_gain"], name="mem_attn")

    merged = _branch_fwd(o_hg, o_da, o_mem, proj, w["w_proj_hg"], w["w_proj_da"], w["w_proj_mem"],
                         name="branch_merge")
    x1 = _matmul([(merged, w["w_out"])], "nn", F32, tm=512, tn=512, residual=x, name="mix_out")
    h2 = _rms_fwd(x1, p["norm_ffn_gain"], name="norm_ffn")
    a, b, u = _ffn_in(h2, w["w_ffn_in"], name="ffn_in")
    y = _matmul([(u, w["w_ffn_out"])], "nn", F32, tm=512, tn=512, residual=x1, name="ffn_out")
    dy, dy_b, loss = _loss_grad(y, tgt, name="loss")

    gw, gs = {}, {}
    gw["w_ffn_out"] = _matmul([(u, dy_b)], "tn", F32, tm=256, tn=512, name="g_ffn_out")
    da, db = _ffn_act_bwd(dy_b, w["w_ffn_out"], a, b, name="ffn_act_bwd")
    gw["w_ffn_a"] = _matmul([(h2, da)], "tn", F32, tm=512, tn=256, name="g_ffn_a")
    gw["w_ffn_b"] = _matmul([(h2, db)], "tn", F32, tm=512, tn=256, name="g_ffn_b")
    dh2 = _matmul([(da, w["w_ffn_in"], 0), (db, w["w_ffn_in"], 1)], "nt", F32, tm=256, tn=512, name="d_h2")
    dx1, dx1_b, gs["norm_ffn_gain"] = _rms_bwd_rows(dh2, x1, p["norm_ffn_gain"], dy, name="norm_ffn_bwd")
    gw["w_out"] = _matmul([(merged, dx1_b)], "tn", F32, tm=512, tn=512, name="g_out")
    dmerged = _matmul([(dx1_b, w["w_out"])], "nt", F32, tm=512, tn=512, name="d_merged")
    (dg_hg, dz_hg, do_hg, dg_da, dz_da, do_da, dg_mem, dz_mem, do_mem) = _branch_bwd(
        dmerged, o_hg, o_da, o_mem, proj, w["w_proj_hg"], w["w_proj_da"], w["w_proj_mem"], name="branch_bwd")
    gw["w_proj_hg"] = _matmul([(o_hg, dz_hg)], "tn", F32, tm=512, tn=512, name="g_proj_hg")
    gw["w_proj_da"] = _matmul([(o_da, dz_da)], "tn", F32, tm=512, tn=512, name="g_proj_da")
    gw["w_proj_mem"] = _matmul([(o_mem, dz_mem)], "tn", F32, tm=512, tn=512, name="g_proj_mem")

    dq_mem, dk_mem, dv_mem, gs["mem_q_gain"], gs["mem_k_gain"] = _mem_bwd(
        do_mem, proj, kv, p["mem_q_gain"], p["mem_k_gain"], name="mem_attn_bwd")
    dkv = jnp.concatenate([dk_mem, dv_mem], axis=1).astype(BF16)
    gw["w_mem_kv"] = _matmul([(mem_n, dkv)], "tn", F32, tm=512, tn=512, name="g_mem_kv")
    dmem_n = _matmul([(dkv, w["w_mem_kv"])], "nt", F32, tm=256, tn=512, name="d_mem_n")
    _, _, gs["norm_mem_gain"] = _rms_bwd_rows(dmem_n, mem, p["norm_mem_gain"], None, name="norm_mem_bwd")

    prep = _da_bwd_prep(do_da, o_da, lse_da, name="da_bwd_prep")
    d_da, gq_parts, gk_parts = [], [], []
    for g in range(len(DA_GROUPS)):
        qr, kr, vr = qkv_r[g]
        dor, lser, deltar = prep[3 * g:3 * g + 3]
        dqr, dkr, dvr = _band_bwd(qr, kr, vr, dor, lser, deltar, g, name=f"band_bwd{g}")
        dq, dk, dv, gq, gk = _da_prep_bwd(dqr, dkr, dvr, proj, p["da_q_gain"], p["da_k_gain"], g,
                                          name=f"da_prep_bwd{g}")
        d_da.append((dq, dk, dv))
        gq_parts.append(gq)
        gk_parts.append(gk)

    dg_hgate, do_gla, gs["hg_norm_gain"] = _hg_out_bwd(do_hg, o_fw, o_bw, proj, p["hg_norm_gain"], name="hg_out_bwd")
    dq_f, dfl_fw, dv_f, dlb_fw = _gla_bwd(proj, lb_fw, do_gla, st_fw, None, f_off=C_FF, rev=False, name="gla_bwd_fw")
    dq_hg, dfl_bw, dv_hg, dlb_bw = _gla_bwd(proj, lb_bw, do_gla, st_bw, (dq_f, dv_f), f_off=C_FB, rev=True,
                                            name="gla_bwd_bw")

    dproj = jnp.concatenate(
        [dq_hg, dfl_fw, dfl_bw, dv_hg, dg_hgate]
        + [t[0] for t in d_da] + [t[1] for t in d_da] + [t[2] for t in d_da]
        + [dq_mem, dg_hg, dg_da, dg_mem], axis=1)
    gw["w_in"] = _matmul([(h, dproj)], "tn", F32, tm=512, tn=512, name="g_in")
    dh = _matmul([(dproj, w["w_in"])], "nt", F32, tm=1024, tn=1024, tk=1024, name="d_h")
    grad_x, _, gs["norm_mix_gain"] = _rms_bwd_rows(dh, x, p["norm_mix_gain"], dx1, name="norm_mix_bwd")

    small = _small_pack(gs, gq_parts, gk_parts, dlb_fw, dlb_bw, lb_fw, lb_bw, name="small_pack")
    return loss, grad_x, gw, small


def _small_pack(gs, gq_parts, gk_parts, dlb_fw, dlb_bw, lb_fw, lb_bw, *, name):
    def body(g_mix, g_mem, g_ffn, dfw, dbw, lfw, lbw, g_hg, q0, q1, q2, k0, k1, k2, g_mq, g_mk, o_ref):
        o_ref[0:1, :] = g_mix[...]
        o_ref[1:2, :] = g_mem[...]
        o_ref[2:3, :] = g_ffn[...]
        for base, d_ref, l_ref in ((3, dfw, lfw), (5, dbw, lbw)):
            lbv = l_ref[...]
            t = d_ref[...] * lbv * (1.0 - lbv)
            o_ref[base:base + 1, :] = t
            o_ref[base + 1:base + 2, :] = -t
        o_ref[7:8, :] = jnp.zeros((1, D_MODEL), F32)
        o_ref[7:8, 0:HEAD] = g_hg[...]
        o_ref[7:8, HEAD:2 * HEAD] = q0[...] + q1[...] + q2[...]
        o_ref[7:8, 2 * HEAD:3 * HEAD] = k0[...] + k1[...] + k2[...]
        o_ref[7:8, 3 * HEAD:4 * HEAD] = g_mq[...]
        o_ref[7:8, 4 * HEAD:5 * HEAD] = g_mk[...]

    return pl.pallas_call(body, name=name, out_shape=jax.ShapeDtypeStruct((8, D_MODEL), F32))(
        gs["norm_mix_gain"], gs["norm_mem_gain"], gs["norm_ffn_gain"], dlb_fw, dlb_bw, lb_fw, lb_bw,
        gs["hg_norm_gain"], *gq_parts, *gk_parts, gs["mem_q_gain"], gs["mem_k_gain"])


def _row_tile(rows, cols, n_arrays):
    budget = (16 * 1024 * 1024) // (2 * 4 * cols * n_arrays)
    tr = rows
    while tr > budget and tr % 2 == 0 and (tr // 2) % 16 == 0:
        tr //= 2
    return tr


def _cast_into_full(a, chip, rows, cols, axis, *, name):
    sr, sc = a.shape
    tr = _row_tile(sr, sc, 2)

    def body(chip_ref, a_ref, o_ref):
        del chip_ref
        o_ref[...] = a_ref[...].astype(BF16)

    if axis == 1:
        out_map = lambda i, chip_ref: (i, chip_ref[0])
    else:
        out_map = lambda i, chip_ref: (chip_ref[0] * (sr // tr) + i, 0)
    return pl.pallas_call(
        body, name=name,
        grid_spec=pltpu.PrefetchScalarGridSpec(
            num_scalar_prefetch=1, grid=(sr // tr,),
            in_specs=[pl.BlockSpec((tr, sc), lambda i, chip_ref: (i, 0))],
            out_specs=pl.BlockSpec((tr, sc), out_map)),
        out_shape=jax.ShapeDtypeStruct((rows, cols), BF16),
        compiler_params=_params(("parallel",)))(chip, a)


def _add_halves(items, *, name):
    rows = items[0][3].shape[0]
    widths = [ra.shape[1] for _, _, _, ra in items]
    tr = _row_tile(rows, sum(widths), 4)

    def body(*refs):
        o_ref = refs[-1]
        first = lax.axis_index("c") == 0
        off = 0
        for j, wd in enumerate(widths):
            h0, h1, ra = refs[3 * j:3 * j + 3]
            o_ref[:, off:off + wd] = (jnp.where(first, h0[...], h1[...]) + ra[...]).astype(BF16)
            off += wd

    in_specs, ins = [], []
    for (g, haxis, hsize, ra), wd in zip(items, widths):
        if haxis == 0:
            in_specs += [pl.BlockSpec((tr, wd), lambda i: (i, 0)),
                         pl.BlockSpec((tr, wd), lambda i, o=hsize // tr: (o + i, 0))]
        else:
            in_specs += [pl.BlockSpec((tr, wd), lambda i: (i, 0)), pl.BlockSpec((tr, wd), lambda i: (i, 1))]
        in_specs.append(pl.BlockSpec((tr, wd), lambda i: (i, 0)))
        ins += [g, g, ra]
    return pl.pallas_call(body, name=name, grid=(rows // tr,), in_specs=in_specs,
                          out_specs=pl.BlockSpec((tr, sum(widths)), lambda i: (i, 0)),
                          out_shape=jax.ShapeDtypeStruct((rows, sum(widths)), BF16),
                          compiler_params=_params(("parallel",)))(*ins)


def _add_slots(rb, *, name):
    _, rows, cols = rb.shape
    tr = _row_tile(rows, cols, 5)

    def body(r0, r1, r2, r3, o_ref):
        o_ref[...] = ((r0[...].astype(F32) + r1[...].astype(F32)) + r2[...].astype(F32)) + r3[...].astype(F32)

    slot = lambda s: pl.BlockSpec((None, tr, cols), lambda i: (s, i, 0))
    return pl.pallas_call(body, name=name, grid=(rows // tr,), in_specs=[slot(s) for s in range(4)],
                          out_specs=pl.BlockSpec((tr, cols), lambda i: (i, 0)),
                          out_shape=jax.ShapeDtypeStruct((rows, cols), F32),
                          compiler_params=_params(("parallel",)))(rb, rb, rb, rb)


def _adamw(w, g, m, v, *, name):
    rows, cols = w.shape
    tr = _row_tile(rows, cols, 7) if rows % 16 == 0 else rows
    c1 = 1.0 - ADAM_B1 ** ADAM_STEP
    c2 = 1.0 - ADAM_B2 ** ADAM_STEP

    def body(w_ref, g_ref, m_ref, v_ref, d_ref, mo_ref, vo_ref):
        gv = g_ref[...]
        mn = ADAM_B1 * m_ref[...] + (1.0 - ADAM_B1) * gv
        vn = ADAM_B2 * v_ref[...] + (1.0 - ADAM_B2) * (gv * gv)
        mo_ref[...] = mn
        vo_ref[...] = vn
        d_ref[...] = -ADAM_LR * ((mn / c1) / (jnp.sqrt(vn / c2) + ADAM_EPS) + ADAM_WD * w_ref[...])

    spec = pl.BlockSpec((tr, cols), lambda i: (i, 0))
    shape = jax.ShapeDtypeStruct((rows, cols), F32)
    return pl.pallas_call(body, name=name, grid=(rows // tr,), in_specs=[spec] * 4, out_specs=[spec] * 3,
                          out_shape=[shape] * 3, compiler_params=_params(("parallel",)))(w, g, m, v)


W_SPECS = (
    ("w_in", 1024, IN_COLS, 1, IN_COLS // 4),
    ("w_mem_kv", 1024, 1024, 0, 256),
    ("w_proj_hg", 1024, 1024, 0, 256),
    ("w_proj_da", 512, 1024, 1, 256),
    ("w_proj_mem", 512, 1024, 1, 256),
    ("w_out", 1024, 1024, 0, 256),
    ("w_ffn_in", 1024, 2 * D_FF, 1, 2 * D_FF // 4),
    ("w_ffn_out", D_FF, 1024, 0, D_FF // 4),
)
CHIP_FLIPS = ((1, 0), (0, 1), (1, 1))
ANY = pl.BlockSpec(memory_space=pl.ANY)
DMA_CHUNK_BYTES = 1 << 20
STAGE_BYTES = 2 << 20


def _place():
    x, y, c = lax.axis_index("x"), lax.axis_index("y"), lax.axis_index("c")
    return x, y, c, 2 * x + y


def _flip(v, f):
    return 1 - v if f else v


def _slab(ref, axis, idx, size):
    start = pl.multiple_of(idx * size, size)
    return ref.at[pl.ds(start, size), :] if axis == 0 else ref.at[:, pl.ds(start, size)]


def _chunked(make, src, dst):
    rows, cols = src.shape
    row_bytes = cols * jnp.dtype(src.dtype).itemsize
    k = 1
    while rows % (2 * k) == 0 and (rows // (2 * k)) % 16 == 0 and (rows // k) * row_bytes > DMA_CHUNK_BYTES:
        k *= 2
    cr = rows // k
    parts = [make(src.at[pl.ds(j * cr, cr), :], dst.at[pl.ds(j * cr, cr), :]) for j in range(k)]
    return parts, make(src, dst)


def _run_copies(local, remote):
    for parts, _ in local + remote:
        for cp in parts:
            cp.start()
    for _, whole in remote:
        whole.wait_recv()
    for _, whole in remote:
        whole.wait_send()
    for _, whole in local:
        whole.wait()


def _half_spec(rows, cols, axis):
    return (0, rows // 2) if axis == 1 else (1, cols // 2)


def _staged(src, remote_dst, local_dst, sibling, load_sems, send_sems, store_sems, recv_sem):
    rows, cols = src.shape
    row_bytes = cols * jnp.dtype(src.dtype).itemsize
    k = 1
    while (rows // k) * row_bytes > STAGE_BYTES and rows % (2 * k) == 0 and (rows // (2 * k)) % 16 == 0:
        k *= 2
    cr = rows // k
    piece = lambda ref, j: ref.at[pl.ds(j * cr, cr), :]

    def run(buf):
        loads = [pltpu.make_async_copy(piece(src, j), buf.at[j % 2], load_sems.at[j % 2]) for j in range(k)]
        outs = [[pltpu.make_async_remote_copy(src_ref=buf.at[j % 2], dst_ref=piece(remote_dst, j),
                                              send_sem=send_sems.at[j % 2], recv_sem=recv_sem,
                                              device_id=sibling, device_id_type=MESH)] for j in range(k)]
        if local_dst is not None:
            for j in range(k):
                outs[j].append(pltpu.make_async_copy(buf.at[j % 2], piece(local_dst, j), store_sems.at[j % 2]))

        def drained(j):
            outs[j][0].wait_send()
            for cp in outs[j][1:]:
                cp.wait()

        loads[0].start()
        for j in range(k):
            loads[j].wait()
            for cp in outs[j]:
                cp.start()
            if j + 1 < k:
                if j >= 1:
                    drained(j - 1)
                loads[j + 1].start()
        for j in range(max(0, k - 2), k):
            drained(j)

    pl.run_scoped(run, pltpu.VMEM((2, cr, cols), src.dtype))


def _landed(ref, recv_sem, send_sem):
    pltpu.make_async_remote_copy(src_ref=ref, dst_ref=ref, send_sem=send_sem, recv_sem=recv_sem,
                                 device_id=(lax.axis_index("x"), lax.axis_index("y"), lax.axis_index("c")),
                                 device_id_type=MESH).wait_recv()


def _gather_weights(fulls):
    n = len(W_SPECS)

    def body(*refs):
        outs = refs[n:2 * n]
        ici_send, ici_recv, load_sems, d2d_send, d2d_recv = refs[2 * n:]
        x, y, c, p = _place()

        def half_slab(wi, chip, half):
            _, rows, cols, axis, size = W_SPECS[wi]
            haxis, hsize = _half_spec(rows, cols, axis)
            return _slab(_slab(outs[wi], axis, chip, size), haxis, half, hsize)

        sent = []
        for wi in range(n):
            mine = half_slab(wi, p, c)
            for k, (fx, fy) in enumerate(CHIP_FLIPS):
                sent.append(_chunked(lambda s, d, j=3 * wi + k, fx=fx, fy=fy: pltpu.make_async_remote_copy(
                    src_ref=s, dst_ref=d, send_sem=ici_send.at[j], recv_sem=ici_recv.at[j],
                    device_id=(_flip(x, fx), _flip(y, fy), c), device_id_type=MESH), mine, mine))
        for parts, _ in sent:
            for cp in parts:
                cp.start()
        for k, (fx, fy) in enumerate(CHIP_FLIPS):
            q = 2 * _flip(x, fx) + _flip(y, fy)
            for wi in range(n):
                sent[3 * wi + k][1].wait_recv()
                got = half_slab(wi, q, c)
                _staged(got, got, None, (x, y, 1 - c), load_sems, d2d_send, None, d2d_recv.at[3 * wi + k])
        for _, whole in sent:
            whole.wait_send()
        for k, (fx, fy) in enumerate(CHIP_FLIPS):
            q = 2 * _flip(x, fx) + _flip(y, fy)
            for wi in range(n):
                _landed(half_slab(wi, q, 1 - c), d2d_recv.at[3 * wi + k], d2d_send.at[0])

    return pl.pallas_call(
        body, name="gather_weights", in_specs=[ANY] * n, out_specs=[ANY] * n,
        out_shape=[jax.ShapeDtypeStruct(f.shape, f.dtype) for f in fulls],
        input_output_aliases={i: i for i in range(n)},
        scratch_shapes=[pltpu.SemaphoreType.DMA((3 * n,)), pltpu.SemaphoreType.DMA((3 * n,)),
                        pltpu.SemaphoreType.DMA((2,)), pltpu.SemaphoreType.DMA((2,)),
                        pltpu.SemaphoreType.DMA((3 * n,))],
    )(*fulls)


def _sibling_exchange(grads):
    n = len(grads)

    def body(*refs):
        ins, outs = refs[:n], refs[n:2 * n]
        load_sems, send_sems, recv_sems = refs[2 * n:]
        x, y, c, _ = _place()
        for i, (_, haxis, hsize) in enumerate(grads):
            _staged(_slab(ins[i], haxis, 1 - c, hsize), outs[i], None, (x, y, 1 - c),
                    load_sems, send_sems, None, recv_sems.at[i])
        for i in range(n):
            _landed(outs[i], recv_sems.at[i], send_sems.at[0])

    shapes = [jax.ShapeDtypeStruct((hsize, g.shape[1]) if haxis == 0 else (g.shape[0], hsize), F32)
              for g, haxis, hsize in grads]
    return pl.pallas_call(
        body, name="grad_sibling_exchange", in_specs=[ANY] * n, out_specs=[ANY] * n, out_shape=shapes,
        scratch_shapes=[pltpu.SemaphoreType.DMA((2,)), pltpu.SemaphoreType.DMA((2,)),
                        pltpu.SemaphoreType.DMA((n,))],
    )(*[g for g, _, _ in grads])


def _chip_exchange(parts):
    n = len(parts)

    def body(*refs):
        ins, outs = refs[:n], refs[n:2 * n]
        send_sems, recv_sems, local_sems = refs[2 * n:]
        x, y, c, p = _place()
        local, remote = [], []
        for i, (_, axis, size) in enumerate(parts):
            local.append(_chunked(lambda s, d, i=i: pltpu.make_async_copy(s, d, local_sems.at[i]),
                                  _slab(ins[i], axis, p, size), outs[i].at[p]))
            for k, (fx, fy) in enumerate(CHIP_FLIPS):
                px, py = _flip(x, fx), _flip(y, fy)
                remote.append(_chunked(lambda s, d, j=3 * i + k, px=px, py=py: pltpu.make_async_remote_copy(
                    src_ref=s, dst_ref=d, send_sem=send_sems.at[j], recv_sem=recv_sems.at[j],
                    device_id=(px, py, c), device_id_type=MESH), _slab(ins[i], axis, 2 * px + py, size), outs[i].at[p]))
        _run_copies(local, remote)

    shapes = []
    for a, axis, size in parts:
        shapes.append(jax.ShapeDtypeStruct((4, size, a.shape[1]) if axis == 0 else (4, a.shape[0], size), a.dtype))
    return pl.pallas_call(
        body, name="grad_chip_exchange", in_specs=[ANY] * n, out_specs=[ANY] * n, out_shape=shapes,
        scratch_shapes=[pltpu.SemaphoreType.DMA((3 * n,)), pltpu.SemaphoreType.DMA((3 * n,)),
                        pltpu.SemaphoreType.DMA((n,))],
    )(*[a for a, _, _ in parts])


def _sibling_share(sums):
    n = len(sums)

    def body(*refs):
        ins, outs = refs[:n], refs[n:2 * n]
        load_sems, send_sems, store_sems, recv_sems = refs[2 * n:]
        x, y, c, _ = _place()
        for i, (s, haxis) in enumerate(sums):
            place = _slab(outs[i], haxis, c, s.shape[haxis])
            _staged(ins[i], place, place, (x, y, 1 - c), load_sems, send_sems, store_sems, recv_sems.at[i])
        for i, (s, haxis) in enumerate(sums):
            _landed(_slab(outs[i], haxis, 1 - c, s.shape[haxis]), recv_sems.at[i], send_sems.at[0])

    shapes = []
    for s, haxis in sums:
        r, cc = s.shape
        shapes.append(jax.ShapeDtypeStruct((2 * r, cc) if haxis == 0 else (r, 2 * cc), F32))
    return pl.pallas_call(
        body, name="grad_sibling_share", in_specs=[ANY] * n, out_specs=[ANY] * n, out_shape=shapes,
        scratch_shapes=[pltpu.SemaphoreType.DMA((2,)), pltpu.SemaphoreType.DMA((2,)),
                        pltpu.SemaphoreType.DMA((2,)), pltpu.SemaphoreType.DMA((n,))],
    )(*[s for s, _ in sums])


def _reduce_scatter(gw):
    grads = []
    for name, _, _, axis, _ in W_SPECS:
        for g in ((gw["w_ffn_a"], gw["w_ffn_b"]) if name == "w_ffn_in" else (gw[name],)):
            grads.append((g,) + _half_spec(g.shape[0], g.shape[1], axis))
    theirs = _sibling_exchange(grads)
    parts, j = [], 0
    for name, _, _, axis, size in W_SPECS:
        take = 2 if name == "w_ffn_in" else 1
        items = [grads[i] + (theirs[i],) for i in range(j, j + take)]
        parts.append((_add_halves(items, name=f"half_sum_{name}"), axis, size))
        j += take
    slots = _chip_exchange(parts)
    sums = []
    for (name, rows, cols, axis, _), rb in zip(W_SPECS, slots):
        sums.append((_add_slots(rb, name=f"chip_sum_{name}"), _half_spec(rows, cols, axis)[0]))
    return dict(zip([s[0] for s in W_SPECS], _sibling_share(sums)))


def _small_allreduce(sv):
    def body(sv_ref, o_ref, slots_ref, send_sems, recv_sems):
        x, y, c, _ = _place()
        me = 4 * x + 2 * y + c
        slots_ref[me] = sv_ref[...]
        copies = []
        for k in range(1, 8):
            fx, fy, fc = (k >> 2) & 1, (k >> 1) & 1, k & 1
            copies.append(pltpu.make_async_remote_copy(
                src_ref=sv_ref, dst_ref=slots_ref.at[me], send_sem=send_sems.at[k - 1],
                recv_sem=recv_sems.at[k - 1], device_id=(_flip(x, fx), _flip(y, fy), _flip(c, fc)),
                device_id_type=MESH))
        for cp in copies:
            cp.start()
        for cp in copies:
            cp.wait_recv()
        for cp in copies:
            cp.wait_send()
        total = slots_ref[0]
        for s in range(1, 8):
            total = total + slots_ref[s]
        o_ref[...] = total

    vm = pl.BlockSpec(memory_space=pltpu.VMEM)
    return pl.pallas_call(
        body, name="small_allreduce", in_specs=[vm], out_specs=vm,
        out_shape=jax.ShapeDtypeStruct(sv.shape, F32),
        scratch_shapes=[pltpu.VMEM((8,) + sv.shape, F32), pltpu.SemaphoreType.DMA((7,)),
                        pltpu.SemaphoreType.DMA((7,))],
    )(sv)


SMALL_ROWS = (("norm_mix_gain", 0), ("norm_mem_gain", 1), ("norm_ffn_gain", 2))
SMALL_LB = (("lb_logits_fw", 3), ("lb_logits_bw", 5))
SMALL_HEAD = ("hg_norm_gain", "da_q_gain", "da_k_gain", "mem_q_gain", "mem_k_gain")


def _pack_small(d):
    last = jnp.concatenate([d[n] for n in SMALL_HEAD] + [jnp.zeros((1, D_MODEL - HEAD * len(SMALL_HEAD)), F32)], axis=1)
    return jnp.concatenate([d["norm_mix_gain"], d["norm_mem_gain"], d["norm_ffn_gain"],
                            d["lb_logits_fw"], d["lb_logits_bw"], last], axis=0)


def _unpack_small(a):
    out = {n: a[r:r + 1] for n, r in SMALL_ROWS}
    out.update({n: a[r:r + 2] for n, r in SMALL_LB})
    out.update({n: a[7:8, j * HEAD:(j + 1) * HEAD] for j, n in enumerate(SMALL_HEAD)})
    return out


PARAM_ORDER = ("norm_mix_gain", "norm_mem_gain", "w_in", "lb_logits_fw", "lb_logits_bw", "hg_norm_gain",
               "da_q_gain", "da_k_gain", "w_mem_kv", "mem_q_gain", "mem_k_gain", "w_proj_hg", "w_proj_da",
               "w_proj_mem", "w_out", "norm_ffn_gain", "w_ffn_in", "w_ffn_out")


def kernel(x, mem, norm_mix_gain, norm_mem_gain, w_in, lb_logits_fw, lb_logits_bw, hg_norm_gain, da_q_gain, da_k_gain, w_mem_kv, mem_q_gain, mem_k_gain, w_proj_hg, w_proj_da, w_proj_mem, w_out, norm_ffn_gain, w_ffn_in, w_ffn_out, loss_target, m_norm_mix_gain, m_norm_mem_gain, m_w_in, m_lb_logits_fw, m_lb_logits_bw, m_hg_norm_gain, m_da_q_gain, m_da_k_gain, m_w_mem_kv, m_mem_q_gain, m_mem_k_gain, m_w_proj_hg, m_w_proj_da, m_w_proj_mem, m_w_out, m_norm_ffn_gain, m_w_ffn_in, m_w_ffn_out, v_norm_mix_gain, v_norm_mem_gain, v_w_in, v_lb_logits_fw, v_lb_logits_bw, v_hg_norm_gain, v_da_q_gain, v_da_k_gain, v_w_mem_kv, v_mem_q_gain, v_mem_k_gain, v_w_proj_hg, v_w_proj_da, v_w_proj_mem, v_w_out, v_norm_ffn_gain, v_w_ffn_in, v_w_ffn_out):
    args = dict(locals())
    mats = tuple(s[0] for s in W_SPECS)
    flat = lambda a: a.reshape(a.shape[-2:])
    w = {n: flat(args[n]) for n in mats}
    m = {n: flat(args["m_" + n]) for n in mats}
    v = {n: flat(args["v_" + n]) for n in mats}
    small = {n: args[n] for n in PARAM_ORDER if n not in mats}

    chip = (2 * lax.axis_index("x") + lax.axis_index("y")).astype(jnp.int32).reshape(1)
    full = _gather_weights([_cast_into_full(w[n], chip, rows, cols, axis, name=f"cast_{n}")
                            for n, rows, cols, axis, _ in W_SPECS])
    loss, grad_x, gw, small_grads = _local_step(x[0], mem[0], loss_target[0], small, dict(zip(mats, full)))
    grads = _reduce_scatter(gw)
    small_sum = _small_allreduce(small_grads)

    delta, new_m, new_v = {}, {}, {}
    for n in mats:
        delta[n], new_m[n], new_v[n] = _adamw(w[n], grads[n], m[n], v[n], name=f"adamw_{n}")
    packed = _adamw(_pack_small(small), small_sum,
                    _pack_small({n: args["m_" + n] for n in small}),
                    _pack_small({n: args["v_" + n] for n in small}), name="adamw_small")
    grads.update(_unpack_small(small_sum))
    for dst, src in zip((delta, new_m, new_v), packed):
        dst.update(_unpack_small(src))

    def shaped(d, n):
        return d[n].reshape(args[n].shape)

    loss_sum = lax.psum(loss[0, 0], ("x", "y", "c"))
    return (loss_sum, grad_x[None], *[shaped(grads, n) for n in PARAM_ORDER], *[shaped(delta, n) for n in PARAM_ORDER],
            *[shaped(new_m, n) for n in PARAM_ORDER], *[shaped(new_v, n) for n in PARAM_ORDER])
```

```python
import functools
import math

import numpy as np
import jax
import jax.numpy as jnp
from jax import lax
from jax.experimental import pallas as pl
from jax.experimental.pallas import tpu as pltpu

F32 = jnp.float32
BF16 = jnp.bfloat16
MESH = pl.DeviceIdType.MESH

D_MODEL = 1024
HEAD = 128
HG_HEADS = 8
DA_GROUPS = ((1, 64), (4, 64), (16, 64))
DA_GH = 4
MEM_HEADS = 4
N_MEM = 256
D_FF = 2816
CHUNK = 64
BAND_QBLOCKS = 4
GLA_HEADS_PER_STEP = 2
RMS_EPS = 1e-6
NEG_INF = -1e30
HG_SCALE = HEAD ** -0.5
ATT_SCALE = HEAD ** -0.5
VMEM_LIMIT_V7X = 48 * 1024 * 1024

C_HQ, C_FF, C_FB, C_HI, C_HG = 0, 1024, 2048, 3072, 4096
C_DQ, C_DK, C_DV, C_MQ = 5120, 6656, 8192, 9728
C_GHG, C_GDA, C_GMEM = 10240, 11264, 12288
IN_COLS = 13312

ADAM_LR, ADAM_B1, ADAM_B2, ADAM_EPS, ADAM_WD, ADAM_STEP = 0.001, 0.9, 0.999, 1e-08, 0.01, 10


def _params(sem, vmem=VMEM_LIMIT_V7X):
    return pltpu.CompilerParams(dimension_semantics=sem, vmem_limit_bytes=vmem)


def _dot(a, b):
    return jnp.dot(a.astype(BF16), b.astype(BF16), preferred_element_type=F32)


def _dot_nt(a, b):
    return lax.dot_general(a.astype(BF16), b.astype(BF16), (((1,), (1,)), ((), ())),
                           preferred_element_type=F32)


def _dot_tn(a, b):
    return lax.dot_general(a.astype(BF16), b.astype(BF16), (((0,), (0,)), ((), ())),
                           preferred_element_type=F32)


def _sigmoid(v):
    return jax.nn.sigmoid(v.astype(F32))


def _ones(rows, cols):
    return (lax.broadcasted_iota(jnp.int32, (rows, cols), 0) >= 0).astype(BF16)


def _lane_sum(v):
    ones = _ones(HEAD, HEAD)
    hi = v.astype(BF16)
    mid = (v - hi.astype(F32)).astype(BF16)
    return jnp.dot(hi, ones, preferred_element_type=F32) + jnp.dot(mid, ones, preferred_element_type=F32)


def _row_mean(v):
    if v.shape[-1] == HEAD:
        return _lane_sum(v) * (1.0 / HEAD)
    return jnp.mean(v, axis=-1, keepdims=True)


def _rms(v):
    v = v.astype(F32)
    r = lax.rsqrt(_row_mean(v * v) + RMS_EPS)
    return v * r, r


def _rms_bwd(dy, xhat, r, gain):
    dxh = dy * gain
    dx = r * (dxh - xhat * _row_mean(dxh * xhat))
    return dx, dy * xhat


def _matmul(pairs, mode, out_dtype, *, tm, tn, tk=None, residual=None, name):
    b_offs = [pr[2] if len(pr) > 2 else 0 for pr in pairs]
    pairs = [pr[:2] for pr in pairs]
    a0, b0 = pairs[0]
    if mode == "nn":
        (m, kk), n = a0.shape, b0.shape[1]
    elif mode == "nt":
        (m, kk), n = a0.shape, b0.shape[0]
    else:
        (kk, m), n = a0.shape, b0.shape[1]
    assert mode == "nt" or not any(b_offs)
    tm, tn = min(tm, m), min(tn, n)
    tk = kk if tk is None else tk
    nk = kk // tk
    assert m % tm == 0 and n % tn == 0 and kk % tk == 0, (name, m, n, kk)
    n_p = len(pairs)
    if mode == "tn":
        a_spec = pl.BlockSpec((tk, tm), lambda i, j, k: (k, i))
    else:
        a_spec = pl.BlockSpec((tm, tk), lambda i, j, k: (i, k))
    if mode == "nt":
        b_specs = [pl.BlockSpec((tn, tk), lambda i, j, k, o=o: (j, o * nk + k)) for o in b_offs]
    else:
        b_specs = [pl.BlockSpec((tk, tn), lambda i, j, k: (k, j))] * n_p
    o_spec = pl.BlockSpec((tm, tn), lambda i, j, k: (i, j))
    dot = {"nn": _dot, "nt": _dot_nt, "tn": _dot_tn}[mode]
    has_res = residual is not None

    def body(*refs):
        a_refs, b_refs = refs[:n_p], refs[n_p:2 * n_p]
        pos = 2 * n_p
        res_ref = refs[pos] if has_res else None
        pos += int(has_res)
        o_ref = refs[pos]
        part = dot(a_refs[0][...], b_refs[0][...])
        for a_r, b_r in zip(a_refs[1:], b_refs[1:]):
            part += dot(a_r[...], b_r[...])

        def finish(total):
            if has_res:
                total = total + res_ref[...]
            o_ref[...] = total.astype(out_dtype)

        if nk == 1:
            finish(part)
        else:
            acc_ref = refs[pos + 1]
            k = pl.program_id(2)

            @pl.when(k == 0)
            def _():
                acc_ref[...] = part

            @pl.when(k > 0)
            def _():
                acc_ref[...] += part

            @pl.when(k == nk - 1)
            def _():
                finish(acc_ref[...])

    ins = [a for a, _ in pairs] + [b for _, b in pairs]
    in_specs = [a_spec] * n_p + b_specs
    if has_res:
        ins.append(residual)
        in_specs.append(o_spec)
    return pl.pallas_call(
        body, name=name, grid=(m // tm, n // tn, nk),
        in_specs=in_specs, out_specs=o_spec,
        out_shape=jax.ShapeDtypeStruct((m, n), out_dtype),
        scratch_shapes=[pltpu.VMEM((tm, tn), F32)] if nk > 1 else [],
        compiler_params=_params(("parallel", "parallel", "arbitrary")),
    )(*ins)


def _rms_fwd(x, gain, *, name):
    rows, dm = x.shape
    tm = min(512, rows)

    def body(x_ref, g_ref, h_ref):
        xhat, _ = _rms(x_ref[...])
        h_ref[...] = (xhat * g_ref[...]).astype(BF16)

    return pl.pallas_call(
        body, name=name, grid=(rows // tm,),
        in_specs=[pl.BlockSpec((tm, dm), lambda i: (i, 0)), pl.BlockSpec((1, dm), lambda i: (0, 0))],
        out_specs=pl.BlockSpec((tm, dm), lambda i: (i, 0)),
        out_shape=jax.ShapeDtypeStruct((rows, dm), BF16),
        compiler_params=_params(("parallel",)),
    )(x, gain)


def _rms_bwd_rows(dh, x, gain, dres, *, name):
    rows, dm = x.shape
    tm = min(512, rows)
    has_res = dres is not None

    def body(*refs):
        dh_ref, x_ref, g_ref = refs[:3]
        res_ref = refs[3] if has_res else None
        dx_ref, dxb_ref, dg_ref = refs[3 + int(has_res):]
        xhat, r = _rms(x_ref[...])
        dx, dgr = _rms_bwd(dh_ref[...], xhat, r, g_ref[...])
        if has_res:
            dx = dx + res_ref[...]
        dx_ref[...] = dx
        dxb_ref[...] = dx.astype(BF16)

        @pl.when(pl.program_id(0) == 0)
        def _():
            dg_ref[...] = jnp.zeros_like(dg_ref)

        dg_ref[...] += jnp.sum(dgr, axis=0, keepdims=True)

    row = pl.BlockSpec((tm, dm), lambda i: (i, 0))
    vec = pl.BlockSpec((1, dm), lambda i: (0, 0))
    return pl.pallas_call(
        body, name=name, grid=(rows // tm,),
        in_specs=[row, row, vec] + ([row] if has_res else []),
        out_specs=[row, row, vec],
        out_shape=[jax.ShapeDtypeStruct((rows, dm), F32), jax.ShapeDtypeStruct((rows, dm), BF16),
                   jax.ShapeDtypeStruct((1, dm), F32)],
        compiler_params=_params(("arbitrary",)),
    )(*([dh, x, gain] + ([dres] if has_res else [])))


def _loss_grad(y, tgt, *, name):
    rows, dm = y.shape
    tm = min(512, rows)
    steps = rows // tm

    def body(y_ref, t_ref, dy_ref, dyb_ref, loss_ref, acc_ref):
        i = pl.program_id(0)
        diff = y_ref[...] - t_ref[...]
        dy = diff * (1.0 / dm)
        dy_ref[...] = dy
        dyb_ref[...] = dy.astype(BF16)

        @pl.when(i == 0)
        def _():
            acc_ref[...] = jnp.zeros_like(acc_ref)

        acc_ref[...] += jnp.sum(diff * diff, axis=0, keepdims=True)

        @pl.when(i == steps - 1)
        def _():
            loss_ref[...] = jnp.full((1, HEAD), 0.5 / dm, F32) * jnp.sum(acc_ref[...])

    row = pl.BlockSpec((tm, dm), lambda i: (i, 0))
    return pl.pallas_call(
        body, name=name, grid=(steps,),
        in_specs=[row, row],
        out_specs=[row, row, pl.BlockSpec((1, HEAD), lambda i: (0, 0))],
        out_shape=[jax.ShapeDtypeStruct((rows, dm), F32), jax.ShapeDtypeStruct((rows, dm), BF16),
                   jax.ShapeDtypeStruct((1, HEAD), F32)],
        scratch_shapes=[pltpu.VMEM((1, dm), F32)],
        compiler_params=_params(("arbitrary",)),
    )(y, tgt)


def _gla_block_terms(q_raw, f_logit, lb, tri):
    sig = _sigmoid(f_logit)
    forget = lb + (1.0 - lb) * sig
    k = 1.0 - forget
    b = _masked_sum(tri, jnp.log(forget))
    qs = _sigmoid(q_raw)
    eb = jnp.exp(b)
    emb = jnp.exp(-b)
    qt = (q_raw * qs * HG_SCALE) * eb
    kt = k * emb
    return sig, forget, k, b, qs, eb, emb, qt, kt


def _masked_sum(mask, v):
    mb = mask.astype(BF16)
    hi = v.astype(BF16)
    mid = (v - hi.astype(F32)).astype(BF16)
    return jnp.dot(mb, hi, preferred_element_type=F32) + jnp.dot(mb, mid, preferred_element_type=F32)


def _tri_mask(n, rev):
    row = lax.broadcasted_iota(jnp.int32, (n, n), 0)
    col = lax.broadcasted_iota(jnp.int32, (n, n), 1)
    shift = CHUNK.bit_length() - 1
    same = jnp.right_shift(row, shift) == jnp.right_shift(col, shift)
    return same & ((row <= col) if rev else (row >= col))


def _chunk_order(ncb, rev):
    order = range(ncb - 1, -1, -1) if rev else range(ncb)
    return [(c, c * CHUNK if rev else c * CHUNK + CHUNK - 1) for c in order]


def _gla_fwd(proj, lb, *, f_off, rev, name):
    rows = proj.shape[0]
    tb = min(256, rows)
    nb, ncb = rows // tb, tb // CHUNK

    def tmap(n):
        return nb - 1 - n if rev else n

    def body(q_ref, f_ref, v_ref, lb_ref, o_ref, st_ref, s_ref):
        @pl.when(pl.program_id(1) == 0)
        def _():
            s_ref[...] = jnp.zeros_like(s_ref)

        tri = _tri_mask(tb, rev)
        for hh in range(GLA_HEADS_PER_STEP):
            cs = slice(hh * HEAD, (hh + 1) * HEAD)
            v = v_ref[:, cs]
            _, _, k, b, _, _, _, qt, kt = _gla_block_terms(q_ref[:, cs].astype(F32), f_ref[:, cs].astype(F32),
                                                           lb_ref[:, cs], tri)
            o_intra = _dot(jnp.where(tri, _dot_nt(qt, kt), 0.0), v)
            chunks = []
            for c, last in _chunk_order(ncb, rev):
                sl = slice(c * CHUNK, (c + 1) * CHUNK)
                bl = b[last:last + 1, :]
                kh = k[sl] * jnp.exp(bl - b[sl])
                chunks.append((c, sl, jnp.exp(bl), _dot_tn(v[sl], kh)))
            s_t = s_ref[hh]
            for c, sl, ebl, kv in chunks:
                st_ref[hh, c] = s_t
                o_ref[sl, cs] = o_intra[sl] + _dot_nt(qt[sl], s_t)
                s_t = ebl * s_t + kv
            s_ref[hh] = s_t

    hps = GLA_HEADS_PER_STEP
    col = lambda off: pl.BlockSpec((tb, hps * HEAD), lambda h, n: (tmap(n), off // (hps * HEAD) + h))
    return pl.pallas_call(
        body, name=name, grid=(HG_HEADS // hps, nb),
        in_specs=[col(C_HQ), col(f_off), col(C_HI), pl.BlockSpec((1, hps * HEAD), lambda h, n: (0, h))],
        out_specs=[pl.BlockSpec((tb, hps * HEAD), lambda h, n: (tmap(n), h)),
                   pl.BlockSpec((hps, ncb, HEAD, HEAD), lambda h, n: (h, tmap(n), 0, 0))],
        out_shape=[jax.ShapeDtypeStruct((rows, HG_HEADS * HEAD), F32),
                   jax.ShapeDtypeStruct((HG_HEADS, rows // CHUNK, HEAD, HEAD), F32)],
        scratch_shapes=[pltpu.VMEM((hps, HEAD, HEAD), F32)],
        compiler_params=_params(("parallel", "arbitrary")),
    )(proj, proj, proj, lb)


def _gla_bwd(proj, lb, do, states, prev, *, f_off, rev, name):
    rows = proj.shape[0]
    tb = min(256, rows)
    nb, ncb = rows // tb, tb // CHUNK
    has_prev = prev is not None
    qv_dtype = BF16 if has_prev else F32

    def tmap(n):
        return n if rev else nb - 1 - n

    def body(*refs):
        q_ref, f_ref, v_ref, lb_ref, do_ref, st_ref = refs[:6]
        pq_ref, pv_ref = refs[6:8] if has_prev else (None, None)
        (dq_ref, df_ref, dv_ref, dlb_ref,
         ds_ref, dqt_scr, dk_scr, db_scr, dbl_scr, dv_scr) = refs[6 + 2 * int(has_prev):]

        @pl.when(pl.program_id(1) == 0)
        def _():
            ds_ref[...] = jnp.zeros_like(ds_ref)
            dlb_ref[...] = jnp.zeros_like(dlb_ref)

        tri = _tri_mask(tb, rev)
        tri_t = _tri_mask(tb, not rev)
        for hh in range(GLA_HEADS_PER_STEP):
            cs = slice(hh * HEAD, (hh + 1) * HEAD)
            lbv = lb_ref[:, cs]
            q_raw = q_ref[:, cs].astype(F32)
            v, dout = v_ref[:, cs], do_ref[:, cs].astype(BF16)
            sig, forget, k, b, qs, eb, emb, qt, kt = _gla_block_terms(q_raw, f_ref[:, cs].astype(F32), lbv, tri)
            a = jnp.where(tri, _dot_nt(qt, kt), 0.0)
            da = jnp.where(tri, _dot_nt(dout, v), 0.0)
            dv_intra = _dot_tn(a, dout)
            dqt_intra = _dot(da, kt)
            dkt = _dot_tn(da, qt)
            chunks = []
            for c, last in reversed(_chunk_order(ncb, rev)):
                sl = slice(c * CHUNK, (c + 1) * CHUNK)
                bl = b[last:last + 1, :]
                e = jnp.exp(bl - b[sl])
                s_t = st_ref[hh, c]
                dqt_scr[sl, cs] = dqt_intra[sl] + _dot(dout[sl], s_t)
                chunks.append((sl, jnp.exp(bl), e, k[sl] * e, s_t, _dot_tn(dout[sl], qt[sl])))
            ds_t = ds_ref[hh]
            for sl, ebl, e, kh, s_t, grow in chunks:
                dkh = _dot(v[sl], ds_t)
                dv_scr[sl, cs] = dv_intra[sl] + _dot_nt(kh, ds_t)
                dk_scr[sl, cs] = dkt[sl] * emb[sl] + dkh * e
                khd = kh * dkh
                dbl = jnp.sum(khd, axis=0, keepdims=True) + ebl * jnp.sum(ds_t * s_t, axis=0, keepdims=True)
                db_scr[sl, cs] = khd
                dbl_scr[sl, cs] = jnp.broadcast_to(dbl, (CHUNK, HEAD))
                ds_t = grow + ds_t * ebl
            ds_ref[hh] = ds_t
            dqt = dqt_scr[:, cs]
            dlogf = _masked_sum(tri_t, qt * dqt - kt * dkt - db_scr[:, cs]) + dbl_scr[:, cs]
            dforget = dlogf / forget - dk_scr[:, cs]
            df_ref[:, cs] = (dforget * (1.0 - lbv) * sig * (1.0 - sig)).astype(BF16)
            dlb_ref[:, cs] += jnp.sum(dforget * (1.0 - sig), axis=0, keepdims=True)
            dqr = dqt * eb * (HG_SCALE * qs * (1.0 + q_raw * (1.0 - qs)))
            dv = dv_scr[:, cs]
            if has_prev:
                dqr = dqr + pq_ref[:, cs]
                dv = dv + pv_ref[:, cs]
            dq_ref[:, cs] = dqr.astype(qv_dtype)
            dv_ref[:, cs] = dv.astype(qv_dtype)

    hps = GLA_HEADS_PER_STEP
    col = lambda off: pl.BlockSpec((tb, hps * HEAD), lambda h, n: (tmap(n), off // (hps * HEAD) + h))
    blk = pl.BlockSpec((tb, hps * HEAD), lambda h, n: (tmap(n), h))
    vec = pl.BlockSpec((1, hps * HEAD), lambda h, n: (0, h))
    wide = HG_HEADS * HEAD
    return pl.pallas_call(
        body, name=name, grid=(HG_HEADS // hps, nb),
        in_specs=[col(C_HQ), col(f_off), col(C_HI), vec, blk,
                  pl.BlockSpec((hps, ncb, HEAD, HEAD), lambda h, n: (h, tmap(n), 0, 0))]
                 + ([blk, blk] if has_prev else []),
        out_specs=[blk, blk, blk, vec],
        out_shape=[jax.ShapeDtypeStruct((rows, wide), qv_dtype), jax.ShapeDtypeStruct((rows, wide), BF16),
                   jax.ShapeDtypeStruct((rows, wide), qv_dtype), jax.ShapeDtypeStruct((1, wide), F32)],
        scratch_shapes=[pltpu.VMEM((hps, HEAD, HEAD), F32)] + [pltpu.VMEM((tb, hps * HEAD), F32)] * 5,
        compiler_params=_params(("parallel", "arbitrary")),
    )(*([proj, proj, proj, lb, do, states] + (list(prev) if has_prev else [])))


def _hg_out_fwd(o_fw, o_bw, proj, gain, *, name):
    rows = o_fw.shape[0]
    tm = min(512, rows)
    wide = HG_HEADS * HEAD

    def body(a_ref, b_ref, g_ref, gain_ref, o_ref):
        for h in range(HG_HEADS):
            sl = slice(h * HEAD, (h + 1) * HEAD)
            xhat, _ = _rms(a_ref[:, sl] + b_ref[:, sl])
            gate = g_ref[:, sl].astype(F32)
            o_ref[:, sl] = (xhat * gain_ref[...] * (gate * _sigmoid(gate))).astype(BF16)

    row = pl.BlockSpec((tm, wide), lambda i: (i, 0))
    return pl.pallas_call(
        body, name=name, grid=(rows // tm,),
        in_specs=[row, row, pl.BlockSpec((tm, wide), lambda i: (i, C_HG // wide)),
                  pl.BlockSpec((1, HEAD), lambda i: (0, 0))],
        out_specs=row, out_shape=jax.ShapeDtypeStruct((rows, wide), BF16),
        compiler_params=_params(("parallel",)),
    )(o_fw, o_bw, proj, gain)


def _hg_out_bwd(dout, o_fw, o_bw, proj, gain, *, name):
    rows = o_fw.shape[0]
    tm = min(512, rows)
    wide = HG_HEADS * HEAD

    def body(d_ref, a_ref, b_ref, g_ref, gain_ref, dgate_ref, do_ref, dgain_ref):
        @pl.when(pl.program_id(0) == 0)
        def _():
            dgain_ref[...] = jnp.zeros_like(dgain_ref)

        dgain = jnp.zeros((1, HEAD), F32)
        for h in range(HG_HEADS):
            sl = slice(h * HEAD, (h + 1) * HEAD)
            xhat, r = _rms(a_ref[:, sl] + b_ref[:, sl])
            gate, dy = g_ref[:, sl].astype(F32), d_ref[:, sl]
            gs = _sigmoid(gate)
            dgate_ref[:, sl] = (dy * xhat * gain_ref[...] * (gs * (1.0 + gate * (1.0 - gs)))).astype(BF16)
            dx, dgr = _rms_bwd(dy * (gate * gs), xhat, r, gain_ref[...])
            do_ref[:, sl] = dx
            dgain = dgain + jnp.sum(dgr, axis=0, keepdims=True)
        dgain_ref[...] += dgain

    row = pl.BlockSpec((tm, wide), lambda i: (i, 0))
    vec = pl.BlockSpec((1, HEAD), lambda i: (0, 0))
    return pl.pallas_call(
        body, name=name, grid=(rows // tm,),
        in_specs=[row, row, row, pl.BlockSpec((tm, wide), lambda i: (i, C_HG // wide)), vec],
        out_specs=[row, row, vec],
        out_shape=[jax.ShapeDtypeStruct((rows, wide), BF16), jax.ShapeDtypeStruct((rows, wide), F32),
                   jax.ShapeDtypeStruct((1, HEAD), F32)],
        compiler_params=_params(("arbitrary",)),
    )(dout, o_fw, o_bw, proj, gain)


def _strided_rows(ref, r, count, d):
    return ref[...] if d == 1 else ref[pl.ds(r, count, stride=d), :]


def _store_strided(ref, r, count, d, val):
    if d == 1:
        ref[...] = val
    else:
        ref[pl.ds(r, count, stride=d), :] = val


def _da_prep(proj, q_gain, k_gain, g, *, name):
    d = DA_GROUPS[g][0]
    rows = proj.shape[0]
    rb = min(512, rows)
    tn = rb // d

    def body(q_ref, k_ref, v_ref, qg_ref, kg_ref, qo_ref, ko_ref, vo_ref, qf_ref, kf_ref, vf_ref):
        for h in range(DA_GH):
            cs = slice(h * HEAD, (h + 1) * HEAD)
            for src, dst in ((q_ref, qf_ref), (k_ref, kf_ref), (v_ref, vf_ref)):
                dst[...] = src[:, cs].astype(F32)
            for r in range(d):
                qhat, _ = _rms(_strided_rows(qf_ref, r, tn, d))
                khat, _ = _rms(_strided_rows(kf_ref, r, tn, d))
                qo_ref[h, r] = (qhat * qg_ref[...]).astype(BF16)
                ko_ref[h, r] = (khat * kg_ref[...]).astype(BF16)
                vo_ref[h, r] = _strided_rows(vf_ref, r, tn, d).astype(BF16)

    wide = DA_GH * HEAD
    col = lambda off: pl.BlockSpec((rb, wide), lambda i: (i, off // wide + g))
    vec = pl.BlockSpec((1, HEAD), lambda i: (0, 0))
    out = pl.BlockSpec((DA_GH, d, tn, HEAD), lambda i: (0, 0, i, 0))
    shape = jax.ShapeDtypeStruct((DA_GH, d, rows // d, HEAD), BF16)
    return pl.pallas_call(
        body, name=name, grid=(rows // rb,),
        in_specs=[col(C_DQ), col(C_DK), col(C_DV), vec, vec],
        out_specs=[out, out, out], out_shape=[shape, shape, shape],
        scratch_shapes=[pltpu.VMEM((rb, HEAD), F32)] * 3,
        compiler_params=_params(("parallel",)),
    )(proj, proj, proj, q_gain, k_gain)


def _slopes(g):
    idx = np.arange(g * DA_GH + 1, (g + 1) * DA_GH + 1)
    s = (2.0 ** (-8.0 * idx / (DA_GH * len(DA_GROUPS)))).astype(np.float32)
    return jnp.asarray(np.broadcast_to(s[:, None, None], (DA_GH, 8, HEAD)).copy())


def _band_window(ld, t, radius):
    win = min(2 * t, ld)
    assert t // 2 >= radius or win == ld
    return win


def _band_scores(q, k, q0, start, slope, d, radius):
    t, win = q.shape[0], k.shape[0]
    row = lax.broadcasted_iota(jnp.int32, (t, win), 0)
    col = lax.broadcasted_iota(jnp.int32, (t, win), 1)
    rel = jnp.abs((start - q0) + col - row)
    return _dot_nt(q, k) * ATT_SCALE + jnp.where(rel <= radius, -slope * (d * rel).astype(F32), NEG_INF)


def _band_fwd(qr, kr, vr, g, *, name):
    d, radius = DA_GROUPS[g]
    _, _, ld, _ = qr.shape
    t = min(HEAD, ld)
    qb = min(BAND_QBLOCKS, ld // t)
    win = _band_window(ld, t, radius)

    def body(q_ref, k_ref, v_ref, sl_ref, o_ref, lse_ref):
        i = pl.program_id(2)
        slope = sl_ref[0:1, 0:1]
        for j in range(qb):
            sl = slice(j * t, (j + 1) * t)
            q0 = (i * qb + j) * t
            start = pl.multiple_of(jnp.clip(q0 - t // 2, 0, ld - win), t // 2)
            s = _band_scores(q_ref[sl, :], k_ref[pl.ds(start, win), :], q0, start, slope, d, radius)
            m = jnp.max(s, axis=-1, keepdims=True)
            p = jnp.exp(s - m).astype(BF16)
            l = jnp.dot(p, _ones(win, HEAD), preferred_element_type=F32)
            o_ref[sl, :] = _dot(p, v_ref[pl.ds(start, win), :]) / l
            lse_ref[sl, :] = m + jnp.log(l)

    own = pl.BlockSpec((None, None, qb * t, HEAD), lambda h, r, i: (h, r, i, 0))
    seq = pl.BlockSpec((None, None, ld, HEAD), lambda h, r, i: (h, r, 0, 0))
    shape = jax.ShapeDtypeStruct(qr.shape, F32)
    return pl.pallas_call(
        body, name=name, grid=(DA_GH, d, ld // (qb * t)),
        in_specs=[own, seq, seq, pl.BlockSpec((None, 8, HEAD), lambda h, r, i: (h, 0, 0))],
        out_specs=[own, own], out_shape=[shape, shape],
        compiler_params=_params(("parallel", "parallel", "parallel")),
    )(qr, kr, vr, _slopes(g))


def _band_bwd(qr, kr, vr, dor, lser, deltar, g, *, name):
    d, radius = DA_GROUPS[g]
    _, _, ld, _ = qr.shape
    t = min(HEAD, ld)
    qb = min(BAND_QBLOCKS, ld // t)
    win = _band_window(ld, t, radius)

    def body(q_ref, k_ref, v_ref, do_ref, lse_ref, dl_ref, sl_ref, dq_ref, dk_ref, dv_ref):
        i = pl.program_id(2)

        @pl.when(i == 0)
        def _():
            dk_ref[...] = jnp.zeros_like(dk_ref)
            dv_ref[...] = jnp.zeros_like(dv_ref)

        slope = sl_ref[0:1, 0:1]
        for j in range(qb):
            sl = slice(j * t, (j + 1) * t)
            q0 = (i * qb + j) * t
            start = pl.multiple_of(jnp.clip(q0 - t // 2, 0, ld - win), t // 2)
            rows = pl.ds(start, win)
            q, dout, k, v = q_ref[sl, :], do_ref[sl, :], k_ref[rows, :], v_ref[rows, :]
            p = jnp.exp(_band_scores(q, k, q0, start, slope, d, radius) - lse_ref[sl, 0:1])
            ds = p * (_dot_nt(dout, v) - dl_ref[sl, 0:1])
            dq_ref[sl, :] = _dot(ds, k) * ATT_SCALE
            dk_ref[rows, :] += _dot_tn(ds, q) * ATT_SCALE
            dv_ref[rows, :] += _dot_tn(p, dout)

    own = pl.BlockSpec((None, None, qb * t, HEAD), lambda h, r, i: (h, r, i, 0))
    seq = pl.BlockSpec((None, None, ld, HEAD), lambda h, r, i: (h, r, 0, 0))
    shape = jax.ShapeDtypeStruct(qr.shape, F32)
    return pl.pallas_call(
        body, name=name, grid=(DA_GH, d, ld // (qb * t)),
        in_specs=[own, seq, seq, own, own, own, pl.BlockSpec((None, 8, HEAD), lambda h, r, i: (h, 0, 0))],
        out_specs=[own, seq, seq], out_shape=[shape, shape, shape],
        compiler_params=_params(("parallel", "parallel", "arbitrary")),
    )(qr, kr, vr, dor, lser, deltar, _slopes(g))


def _da_merge(outs, lses, rows, *, name):
    rb = min(512, rows)
    wide = DA_GH * HEAD

    def body(*refs):
        o_refs, l_refs = refs[0:3], refs[3:6]
        o_ref, lse_ref = refs[6:8]
        on_refs, ln_refs = refs[8:11], refs[11:14]
        for h in range(DA_GH):
            cs = slice(h * HEAD, (h + 1) * HEAD)
            for g, (d, _) in enumerate(DA_GROUPS):
                tn = rb // d
                for r in range(d):
                    _store_strided(on_refs[g], r, tn, d, o_refs[g][h, r])
                    _store_strided(ln_refs[g], r, tn, d, l_refs[g][h, r])
            l0, l1, l2 = ln_refs[0][...], ln_refs[1][...], ln_refs[2][...]
            m = jnp.maximum(jnp.maximum(l0, l1), l2)
            e0, e1, e2 = jnp.exp(l0 - m), jnp.exp(l1 - m), jnp.exp(l2 - m)
            tot = e0 + e1 + e2
            o_ref[:, cs] = (e0 * on_refs[0][...] + e1 * on_refs[1][...] + e2 * on_refs[2][...]) / tot
            lse_ref[:, cs] = m + jnp.log(tot)

    res = lambda d: pl.BlockSpec((DA_GH, d, rb // d, HEAD), lambda i: (0, 0, i, 0))
    nat = pl.BlockSpec((rb, wide), lambda i: (i, 0))
    shape = jax.ShapeDtypeStruct((rows, wide), F32)
    return pl.pallas_call(
        body, name=name, grid=(rows // rb,),
        in_specs=[res(d) for d, _ in DA_GROUPS] * 2,
        out_specs=[nat, nat], out_shape=[shape, shape],
        scratch_shapes=[pltpu.VMEM((rb, HEAD), F32)] * 6,
        compiler_params=_params(("parallel",)),
    )(*outs, *lses)


def _da_bwd_prep(dout, o, lse, *, name):
    rows = o.shape[0]
    rb = min(512, rows)
    wide = DA_GH * HEAD

    def body(d_ref, o_ref, l_ref, *outs):
        d_scr, l_scr, delta_scr = outs[9:12]
        for h in range(DA_GH):
            cs = slice(h * HEAD, (h + 1) * HEAD)
            dv = d_ref[:, cs]
            d_scr[...] = dv
            l_scr[...] = l_ref[:, cs]
            delta_scr[...] = _lane_sum(dv * o_ref[:, cs])
            for g, (d, _) in enumerate(DA_GROUPS):
                tn = rb // d
                for r in range(d):
                    outs[3 * g][h, r] = _strided_rows(d_scr, r, tn, d).astype(BF16)
                    outs[3 * g + 1][h, r] = _strided_rows(l_scr, r, tn, d)
                    outs[3 * g + 2][h, r] = _strided_rows(delta_scr, r, tn, d)

    nat = pl.BlockSpec((rb, wide), lambda i: (i, 0))
    out_specs, out_shape = [], []
    for d, _ in DA_GROUPS:
        for dt in (BF16, F32, F32):
            out_specs.append(pl.BlockSpec((DA_GH, d, rb // d, HEAD), lambda i: (0, 0, i, 0)))
            out_shape.append(jax.ShapeDtypeStruct((DA_GH, d, rows // d, HEAD), dt))
    return pl.pallas_call(
        body, name=name, grid=(rows // rb,),
        in_specs=[nat, nat, nat], out_specs=out_specs, out_shape=out_shape,
        scratch_shapes=[pltpu.VMEM((rb, HEAD), F32)] * 3,
        compiler_params=_params(("parallel",)),
    )(dout, o, lse)


def _da_prep_bwd(dqr, dkr, dvr, proj, q_gain, k_gain, g, *, name):
    d = DA_GROUPS[g][0]
    rows = proj.shape[0]
    rb = min(512, rows)
    tn = rb // d
    wide = DA_GH * HEAD

    def body(dq_ref, dk_ref, dv_ref, q_ref, k_ref, qg_ref, kg_ref, oq_ref, ok_ref, ov_ref, gq_ref, gk_ref, *nat_refs):
        @pl.when(pl.program_id(0) == 0)
        def _():
            gq_ref[...] = jnp.zeros_like(gq_ref)
            gk_ref[...] = jnp.zeros_like(gk_ref)

        for h in range(DA_GH):
            cs = slice(h * HEAD, (h + 1) * HEAD)
            for j, src in enumerate((dq_ref, dk_ref, dv_ref)):
                for r in range(d):
                    _store_strided(nat_refs[j], r, tn, d, src[h, r])
            ov_ref[:, cs] = nat_refs[2][...].astype(BF16)
            for j, (x_ref, gn_ref, out_ref, acc_ref) in enumerate(((q_ref, qg_ref, oq_ref, gq_ref),
                                                                    (k_ref, kg_ref, ok_ref, gk_ref))):
                xhat, r = _rms(x_ref[:, cs])
                dx, dgr = _rms_bwd(nat_refs[j][...], xhat, r, gn_ref[...])
                out_ref[:, cs] = dx.astype(BF16)
                acc_ref[...] += jnp.sum(dgr, axis=0, keepdims=True)

    res = pl.BlockSpec((DA_GH, d, tn, HEAD), lambda i: (0, 0, i, 0))
    col = lambda off: pl.BlockSpec((rb, wide), lambda i: (i, off // wide + g))
    vec = pl.BlockSpec((1, HEAD), lambda i: (0, 0))
    nat = pl.BlockSpec((rb, wide), lambda i: (i, 0))
    shape = jax.ShapeDtypeStruct((rows, wide), BF16)
    gshape = jax.ShapeDtypeStruct((1, HEAD), F32)
    return pl.pallas_call(
        body, name=name, grid=(rows // rb,),
        in_specs=[res, res, res, col(C_DQ), col(C_DK), vec, vec],
        out_specs=[nat, nat, nat, vec, vec], out_shape=[shape, shape, shape, gshape, gshape],
        scratch_shapes=[pltpu.VMEM((rb, HEAD), F32)] * 3,
        compiler_params=_params(("arbitrary",)),
    )(dqr, dkr, dvr, proj, proj, q_gain, k_gain)


def _mem_fwd(proj, kv, q_gain, k_gain, *, name):
    rows = proj.shape[0]
    tm = min(512, rows)
    n_mem = kv.shape[0]

    def body(q_ref, k_ref, v_ref, qg_ref, kg_ref, o_ref):
        qhat, _ = _rms(q_ref[...])
        khat, _ = _rms(k_ref[...])
        s = _dot_nt(qhat * qg_ref[...], khat * kg_ref[...]) * ATT_SCALE
        p = jnp.exp(s - jnp.max(s, axis=-1, keepdims=True))
        p = p / jnp.sum(p, axis=-1, keepdims=True)
        o_ref[...] = _dot(p, v_ref[...]).astype(BF16)

    vec = pl.BlockSpec((1, HEAD), lambda i, h: (0, 0))
    return pl.pallas_call(
        body, name=name, grid=(rows // tm, MEM_HEADS),
        in_specs=[pl.BlockSpec((tm, HEAD), lambda i, h: (i, C_MQ // HEAD + h)),
                  pl.BlockSpec((n_mem, HEAD), lambda i, h: (0, h)),
                  pl.BlockSpec((n_mem, HEAD), lambda i, h: (0, MEM_HEADS + h)), vec, vec],
        out_specs=pl.BlockSpec((tm, HEAD), lambda i, h: (i, h)),
        out_shape=jax.ShapeDtypeStruct((rows, MEM_HEADS * HEAD), BF16),
        compiler_params=_params(("parallel", "parallel")),
    )(proj, kv, kv, q_gain, k_gain)


def _mem_bwd(dout, proj, kv, q_gain, k_gain, *, name):
    rows = proj.shape[0]
    tm = min(512, rows)
    steps = rows // tm
    n_mem = kv.shape[0]

    def body(d_ref, q_ref, k_ref, v_ref, qg_ref, kg_ref, dq_ref, dk_ref, dv_ref, gq_ref, gk_ref, dkn_ref):
        h, i = pl.program_id(0), pl.program_id(1)

        @pl.when((h == 0) & (i == 0))
        def _():
            gq_ref[...] = jnp.zeros_like(gq_ref)
            gk_ref[...] = jnp.zeros_like(gk_ref)

        @pl.when(i == 0)
        def _():
            dkn_ref[...] = jnp.zeros_like(dkn_ref)
            dv_ref[...] = jnp.zeros_like(dv_ref)

        qhat, rq = _rms(q_ref[...])
        khat, rk = _rms(k_ref[...])
        qn, kn = qhat * qg_ref[...], khat * kg_ref[...]
        s = _dot_nt(qn, kn) * ATT_SCALE
        p = jnp.exp(s - jnp.max(s, axis=-1, keepdims=True))
        p = p / jnp.sum(p, axis=-1, keepdims=True)
        dout = d_ref[...]
        dp = _dot_nt(dout, v_ref[...])
        ds = p * (dp - jnp.sum(p * dp, axis=-1, keepdims=True))
        dv_ref[...] += _dot_tn(p, dout)
        dkn_ref[...] += _dot_tn(ds, qn) * ATT_SCALE
        dq, dgr = _rms_bwd(_dot(ds, kn) * ATT_SCALE, qhat, rq, qg_ref[...])
        dq_ref[...] = dq.astype(BF16)
        gq_ref[...] += jnp.sum(dgr, axis=0, keepdims=True)

        @pl.when(i == steps - 1)
        def _():
            dk, dgk = _rms_bwd(dkn_ref[...], khat, rk, kg_ref[...])
            dk_ref[...] = dk
            gk_ref[...] += jnp.sum(dgk, axis=0, keepdims=True)

    vec = pl.BlockSpec((1, HEAD), lambda h, i: (0, 0))
    memh = pl.BlockSpec((n_mem, HEAD), lambda h, i: (0, h))
    tok = pl.BlockSpec((tm, HEAD), lambda h, i: (i, h))
    gshape = jax.ShapeDtypeStruct((1, HEAD), F32)
    return pl.pallas_call(
        body, name=name, grid=(MEM_HEADS, steps),
        in_specs=[tok, pl.BlockSpec((tm, HEAD), lambda h, i: (i, C_MQ // HEAD + h)), memh,
                  pl.BlockSpec((n_mem, HEAD), lambda h, i: (0, MEM_HEADS + h)), vec, vec],
        out_specs=[tok, memh, memh, vec, vec],
        out_shape=[jax.ShapeDtypeStruct((rows, MEM_HEADS * HEAD), BF16),
                   jax.ShapeDtypeStruct((n_mem, MEM_HEADS * HEAD), F32),
                   jax.ShapeDtypeStruct((n_mem, MEM_HEADS * HEAD), F32), gshape, gshape],
        scratch_shapes=[pltpu.VMEM((n_mem, HEAD), F32)],
        compiler_params=_params(("arbitrary", "arbitrary")),
    )(dout, proj, kv, kv, q_gain, k_gain)


def _branch_fwd(o_hg, o_da, o_mem, proj, wp_hg, wp_da, wp_mem, *, name):
    rows = o_hg.shape[0]
    tm = min(256, rows)

    def body(a_ref, b_ref, c_ref, ga_ref, gb_ref, gc_ref, wa_ref, wb_ref, wc_ref, o_ref):
        merged = _sigmoid(ga_ref[...]) * _dot(a_ref[...], wa_ref[...])
        merged += _sigmoid(gb_ref[...]) * _dot(b_ref[...], wb_ref[...])
        merged += _sigmoid(gc_ref[...]) * _dot(c_ref[...], wc_ref[...])
        o_ref[...] = merged.astype(BF16)

    row = lambda w: pl.BlockSpec((tm, w), lambda i: (i, 0))
    gate = lambda off: pl.BlockSpec((tm, D_MODEL), lambda i: (i, off // D_MODEL))
    full = lambda a: pl.BlockSpec(a.shape, lambda i: (0, 0))
    return pl.pallas_call(
        body, name=name, grid=(rows // tm,),
        in_specs=[row(o_hg.shape[1]), row(o_da.shape[1]), row(o_mem.shape[1]),
                  gate(C_GHG), gate(C_GDA), gate(C_GMEM), full(wp_hg), full(wp_da), full(wp_mem)],
        out_specs=row(D_MODEL), out_shape=jax.ShapeDtypeStruct((rows, D_MODEL), BF16),
        compiler_params=_params(("parallel",)),
    )(o_hg, o_da, o_mem, proj, proj, proj, wp_hg, wp_da, wp_mem)


def _branch_bwd(dm, o_hg, o_da, o_mem, proj, wp_hg, wp_da, wp_mem, *, name):
    rows = o_hg.shape[0]
    tm = min(256, rows)

    def body(dm_ref, a_ref, b_ref, c_ref, ga_ref, gb_ref, gc_ref, wa_ref, wb_ref, wc_ref, *outs):
        dmv = dm_ref[...]
        for j, (o_ref, g_ref, w_ref) in enumerate(((a_ref, ga_ref, wa_ref), (b_ref, gb_ref, wb_ref),
                                                   (c_ref, gc_ref, wc_ref))):
            z = _dot(o_ref[...], w_ref[...])
            gs = _sigmoid(g_ref[...])
            dz = (dmv * gs).astype(BF16)
            outs[3 * j][...] = (dmv * z * gs * (1.0 - gs)).astype(BF16)
            outs[3 * j + 1][...] = dz
            outs[3 * j + 2][...] = _dot_nt(dz, w_ref[...])

    row = lambda w: pl.BlockSpec((tm, w), lambda i: (i, 0))
    gate = lambda off: pl.BlockSpec((tm, D_MODEL), lambda i: (i, off // D_MODEL))
    full = lambda a: pl.BlockSpec(a.shape, lambda i: (0, 0))
    out_specs, out_shape = [], []
    for o in (o_hg, o_da, o_mem):
        out_specs += [row(D_MODEL), row(D_MODEL), row(o.shape[1])]
        out_shape += [jax.ShapeDtypeStruct((rows, D_MODEL), BF16), jax.ShapeDtypeStruct((rows, D_MODEL), BF16),
                      jax.ShapeDtypeStruct((rows, o.shape[1]), F32)]
    return pl.pallas_call(
        body, name=name, grid=(rows // tm,),
        in_specs=[row(D_MODEL), row(o_hg.shape[1]), row(o_da.shape[1]), row(o_mem.shape[1]),
                  gate(C_GHG), gate(C_GDA), gate(C_GMEM), full(wp_hg), full(wp_da), full(wp_mem)],
        out_specs=out_specs, out_shape=out_shape,
        compiler_params=_params(("parallel",)),
    )(dm, o_hg, o_da, o_mem, proj, proj, proj, wp_hg, wp_da, wp_mem)


def _ffn_in(h2, w_ab, *, name):
    rows, dff = h2.shape[0], w_ab.shape[1] // 2
    tm, tn = min(1024, rows), 256

    def body(h_ref, wa_ref, wb_ref, a_ref, b_ref, u_ref):
        a = _dot(h_ref[...], wa_ref[...])
        b = _dot(h_ref[...], wb_ref[...])
        a_ref[...] = a.astype(BF16)
        b_ref[...] = b.astype(BF16)
        u_ref[...] = (a * _sigmoid(a) * b).astype(BF16)

    out = pl.BlockSpec((tm, tn), lambda i, j: (i, j))
    return pl.pallas_call(
        body, name=name, grid=(rows // tm, dff // tn),
        in_specs=[pl.BlockSpec((tm, D_MODEL), lambda i, j: (i, 0)),
                  pl.BlockSpec((D_MODEL, tn), lambda i, j: (0, j)),
                  pl.BlockSpec((D_MODEL, tn), lambda i, j: (0, dff // tn + j))],
        out_specs=[out, out, out],
        out_shape=[jax.ShapeDtypeStruct((rows, dff), BF16)] * 3,
        compiler_params=_params(("parallel", "parallel")),
    )(h2, w_ab, w_ab)


def _ffn_act_bwd(dy, w_out, a, b, *, name):
    rows, dff = a.shape
    tm, tn = min(1024, rows), 256

    def body(dy_ref, w_ref, a_ref, b_ref, da_ref, db_ref):
        du = _dot_nt(dy_ref[...], w_ref[...])
        av, bv = a_ref[...].astype(F32), b_ref[...].astype(F32)
        sa = _sigmoid(av)
        da_ref[...] = (du * bv * sa * (1.0 + av * (1.0 - sa))).astype(BF16)
        db_ref[...] = (du * av * sa).astype(BF16)

    tile = pl.BlockSpec((tm, tn), lambda i, j: (i, j))
    return pl.pallas_call(
        body, name=name, grid=(rows // tm, dff // tn),
        in_specs=[pl.BlockSpec((tm, D_MODEL), lambda i, j: (i, 0)),
                  pl.BlockSpec((tn, D_MODEL), lambda i, j: (j, 0)), tile, tile],
        out_specs=[tile, tile],
        out_shape=[jax.ShapeDtypeStruct((rows, dff), BF16), jax.ShapeDtypeStruct((rows, dff), BF16)],
        compiler_params=_params(("parallel", "parallel")),
    )(dy, w_out, a, b)


def _lower_bound(lb_fw, lb_bw, *, name):
    def body(a_ref, b_ref, oa_ref, ob_ref):
        for src, dst in ((a_ref, oa_ref), (b_ref, ob_ref)):
            dst[...] = _sigmoid(src[0:1, :] - src[1:2, :])

    shape = jax.ShapeDtypeStruct((1, lb_fw.shape[1]), F32)
    return pl.pallas_call(body, name=name, out_shape=[shape, shape])(lb_fw, lb_bw)


def _local_step(x, mem, tgt, p, w):
    rows = x.shape[0]
    lb_fw, lb_bw = _lower_bound(p["lb_logits_fw"], p["lb_logits_bw"], name="lower_bound")

    h = _rms_fwd(x, p["norm_mix_gain"], name="norm_mix")
    proj = _matmul([(h, w["w_in"])], "nn", BF16, tm=1024, tn=512, name="proj_in")
    o_fw, st_fw = _gla_fwd(proj, lb_fw, f_off=C_FF, rev=False, name="gla_fwd_fw")
    o_bw, st_bw = _gla_fwd(proj, lb_bw, f_off=C_FB, rev=True, name="gla_fwd_bw")
    o_hg = _hg_out_fwd(o_fw, o_bw, proj, p["hg_norm_gain"], name="hg_out")

    qkv_r, outs, lses = [], [], []
    for g in range(len(DA_GROUPS)):
        qr, kr, vr = _da_prep(proj, p["da_q_gain"], p["da_k_gain"], g, name=f"da_prep{g}")
        og, lg = _band_fwd(qr, kr, vr, g, name=f"band_fwd{g}")
        qkv_r.append((qr, kr, vr))
        outs.append(og)
        lses.append(lg)
    o_da, lse_da = _da_merge(outs, lses, rows, name="da_merge")

    mem_n = _rms_fwd(mem, p["norm_mem_gain"], name="norm_mem")
    kv = _matmul([(mem_n, w["w_mem_kv"])], "nn", F32, tm=256, tn=512, name="mem_kv")
    o_mem = _mem_fwd(proj, kv, p["mem_q_gain"], p["mem_k_gain"], name="mem_attn")

    merged = _branch_fwd(o_hg, o_da, o_mem, proj, w["w_proj_hg"], w["w_proj_da"], w["w_proj_mem"],
                         name="branch_merge")
    x1 = _matmul([(merged, w["w_out"])], "nn", F32, tm=512, tn=512, residual=x, name="mix_out")
    h2 = _rms_fwd(x1, p["norm_ffn_gain"], name="norm_ffn")
    a, b, u = _ffn_in(h2, w["w_ffn_in"], name="ffn_in")
    y = _matmul([(u, w["w_ffn_out"])], "nn", F32, tm=512, tn=512, residual=x1, name="ffn_out")
    dy, dy_b, loss = _loss_grad(y, tgt, name="loss")

    gw, gs = {}, {}
    gw["w_ffn_out"] = _matmul([(u, dy_b)], "tn", F32, tm=256, tn=512, name="g_ffn_out")
    da, db = _ffn_act_bwd(dy_b, w["w_ffn_out"], a, b, name="ffn_act_bwd")
    gw["w_ffn_a"] = _matmul([(h2, da)], "tn", F32, tm=512, tn=256, name="g_ffn_a")
    gw["w_ffn_b"] = _matmul([(h2, db)], "tn", F32, tm=512, tn=256, name="g_ffn_b")
    dh2 = _matmul([(da, w["w_ffn_in"], 0), (db, w["w_ffn_in"], 1)], "nt", F32, tm=256, tn=512, name="d_h2")
    dx1, dx1_b, gs["norm_ffn_gain"] = _rms_bwd_rows(dh2, x1, p["norm_ffn_gain"], dy, name="norm_ffn_bwd")
    gw["w_out"] = _matmul([(merged, dx1_b)], "tn", F32, tm=512, tn=512, name="g_out")
    dmerged = _matmul([(dx1_b, w["w_out"])], "nt", F32, tm=512, tn=512, name="d_merged")
    (dg_hg, dz_hg, do_hg, dg_da, dz_da, do_da, dg_mem, dz_mem, do_mem) = _branch_bwd(
        dmerged, o_hg, o_da, o_mem, proj, w["w_proj_hg"], w["w_proj_da"], w["w_proj_mem"], name="branch_bwd")
    gw["w_proj_hg"] = _matmul([(o_hg, dz_hg)], "tn", F32, tm=512, tn=512, name="g_proj_hg")
    gw["w_proj_da"] = _matmul([(o_da, dz_da)], "tn", F32, tm=512, tn=512, name="g_proj_da")
    gw["w_proj_mem"] = _matmul([(o_mem, dz_mem)], "tn", F32, tm=512, tn=512, name="g_proj_mem")

    dq_mem, dk_mem, dv_mem, gs["mem_q_gain"], gs["mem_k_gain"] = _mem_bwd(
        do_mem, proj, kv, p["mem_q_gain"], p["mem_k_gain"], name="mem_attn_bwd")
    dkv = jnp.concatenate([dk_mem, dv_mem], axis=1).astype(BF16)
    gw["w_mem_kv"] = _matmul([(mem_n, dkv)], "tn", F32, tm=512, tn=512, name="g_mem_kv")
    dmem_n = _matmul([(dkv, w["w_mem_kv"])], "nt", F32, tm=256, tn=512, name="d_mem_n")
    _, _, gs["norm_mem_gain"] = _rms_bwd_rows(dmem_n, mem, p["norm_mem_gain"], None, name="norm_mem_bwd")

    prep = _da_bwd_prep(do_da, o_da, lse_da, name="da_bwd_prep")
    d_da, gq_parts, gk_parts = [], [], []
    for g in range(len(DA_GROUPS)):
        qr, kr, vr = qkv_r[g]
        dor, lser, deltar = prep[3 * g:3 * g + 3]
        dqr, dkr, dvr = _band_bwd(qr, kr, vr, dor, lser, deltar, g, name=f"band_bwd{g}")
        dq, dk, dv, gq, gk = _da_prep_bwd(dqr, dkr, dvr, proj, p["da_q_gain"], p["da_k_gain"], g,
                                          name=f"da_prep_bwd{g}")
        d_da.append((dq, dk, dv))
        gq_parts.append(gq)
        gk_parts.append(gk)

    dg_hgate, do_gla, gs["hg_norm_gain"] = _hg_out_bwd(do_hg, o_fw, o_bw, proj, p["hg_norm_gain"], name="hg_out_bwd")
    dq_f, dfl_fw, dv_f, dlb_fw = _gla_bwd(proj, lb_fw, do_gla, st_fw, None, f_off=C_FF, rev=False, name="gla_bwd_fw")
    dq_hg, dfl_bw, dv_hg, dlb_bw = _gla_bwd(proj, lb_bw, do_gla, st_bw, (dq_f, dv_f), f_off=C_FB, rev=True,
                                            name="gla_bwd_bw")

    dproj = jnp.concatenate(
        [dq_hg, dfl_fw, dfl_bw, dv_hg, dg_hgate]
        + [t[0] for t in d_da] + [t[1] for t in d_da] + [t[2] for t in d_da]
        + [dq_mem, dg_hg, dg_da, dg_mem], axis=1)
    gw["w_in"] = _matmul([(h, dproj)], "tn", F32, tm=512, tn=512, name="g_in")
    dh = _matmul([(dproj, w["w_in"])], "nt", F32, tm=1024, tn=1024, tk=1024, name="d_h")
    grad_x, _, gs["norm_mix_gain"] = _rms_bwd_rows(dh, x, p["norm_mix_gain"], dx1, name="norm_mix_bwd")

    small = _small_pack(gs, gq_parts, gk_parts, dlb_fw, dlb_bw, lb_fw, lb_bw, name="small_pack")
    return loss, grad_x, gw, small


def _small_pack(gs, gq_parts, gk_parts, dlb_fw, dlb_bw, lb_fw, lb_bw, *, name):
    def body(g_mix, g_mem, g_ffn, dfw, dbw, lfw, lbw, g_hg, q0, q1, q2, k0, k1, k2, g_mq, g_mk, o_ref):
        o_ref[0:1, :] = g_mix[...]
        o_ref[1:2, :] = g_mem[...]
        o_ref[2:3, :] = g_ffn[...]
        for base, d_ref, l_ref in ((3, dfw, lfw), (5, dbw, lbw)):
            lbv = l_ref[...]
            t = d_ref[...] * lbv * (1.0 - lbv)
            o_ref[base:base + 1, :] = t
            o_ref[base + 1:base + 2, :] = -t
        o_ref[7:8, :] = jnp.zeros((1, D_MODEL), F32)
        o_ref[7:8, 0:HEAD] = g_hg[...]
        o_ref[7:8, HEAD:2 * HEAD] = q0[...] + q1[...] + q2[...]
        o_ref[7:8, 2 * HEAD:3 * HEAD] = k0[...] + k1[...] + k2[...]
        o_ref[7:8, 3 * HEAD:4 * HEAD] = g_mq[...]
        o_ref[7:8, 4 * HEAD:5 * HEAD] = g_mk[...]

    return pl.pallas_call(body, name=name, out_shape=jax.ShapeDtypeStruct((8, D_MODEL), F32))(
        gs["norm_mix_gain"], gs["norm_mem_gain"], gs["norm_ffn_gain"], dlb_fw, dlb_bw, lb_fw, lb_bw,
        gs["hg_norm_gain"], *gq_parts, *gk_parts, gs["mem_q_gain"], gs["mem_k_gain"])


def _row_tile(rows, cols, n_arrays):
    budget = (16 * 1024 * 1024) // (2 * 4 * cols * n_arrays)
    tr = rows
    while tr > budget and tr % 2 == 0 and (tr // 2) % 16 == 0:
        tr //= 2
    return tr


def _cast_into_full(a, chip, rows, cols, axis, *, name):
    sr, sc = a.shape
    tr = _row_tile(sr, sc, 2)

    def body(chip_ref, a_ref, o_ref):
        del chip_ref
        o_ref[...] = a_ref[...].astype(BF16)

    if axis == 1:
        out_map = lambda i, chip_ref: (i, chip_ref[0])
    else:
        out_map = lambda i, chip_ref: (chip_ref[0] * (sr // tr) + i, 0)
    return pl.pallas_call(
        body, name=name,
        grid_spec=pltpu.PrefetchScalarGridSpec(
            num_scalar_prefetch=1, grid=(sr // tr,),
            in_specs=[pl.BlockSpec((tr, sc), lambda i, chip_ref: (i, 0))],
            out_specs=pl.BlockSpec((tr, sc), out_map)),
        out_shape=jax.ShapeDtypeStruct((rows, cols), BF16),
        compiler_params=_params(("parallel",)))(chip, a)


def _add_halves(items, *, name):
    rows = items[0][3].shape[0]
    widths = [ra.shape[1] for _, _, _, ra in items]
    tr = _row_tile(rows, sum(widths), 4)

    def body(*refs):
        o_ref = refs[-1]
        first = lax.axis_index("c") == 0
        off = 0
        for j, wd in enumerate(widths):
            h0, h1, ra = refs[3 * j:3 * j + 3]
            o_ref[:, off:off + wd] = (jnp.where(first, h0[...], h1[...]) + ra[...]).astype(BF16)
            off += wd

    in_specs, ins = [], []
    for (g, haxis, hsize, ra), wd in zip(items, widths):
        if haxis == 0:
            in_specs += [pl.BlockSpec((tr, wd), lambda i: (i, 0)),
                         pl.BlockSpec((tr, wd), lambda i, o=hsize // tr: (o + i, 0))]
        else:
            in_specs += [pl.BlockSpec((tr, wd), lambda i: (i, 0)), pl.BlockSpec((tr, wd), lambda i: (i, 1))]
        in_specs.append(pl.BlockSpec((tr, wd), lambda i: (i, 0)))
        ins += [g, g, ra]
    return pl.pallas_call(body, name=name, grid=(rows // tr,), in_specs=in_specs,
                          out_specs=pl.BlockSpec((tr, sum(widths)), lambda i: (i, 0)),
                          out_shape=jax.ShapeDtypeStruct((rows, sum(widths)), BF16),
                          compiler_params=_params(("parallel",)))(*ins)


def _add_slots(rb, *, name):
    _, rows, cols = rb.shape
    tr = _row_tile(rows, cols, 5)

    def body(r0, r1, r2, r3, o_ref):
        o_ref[...] = ((r0[...].astype(F32) + r1[...].astype(F32)) + r2[...].astype(F32)) + r3[...].astype(F32)

    slot = lambda s: pl.BlockSpec((None, tr, cols), lambda i: (s, i, 0))
    return pl.pallas_call(body, name=name, grid=(rows // tr,), in_specs=[slot(s) for s in range(4)],
                          out_specs=pl.BlockSpec((tr, cols), lambda i: (i, 0)),
                          out_shape=jax.ShapeDtypeStruct((rows, cols), F32),
                          compiler_params=_params(("parallel",)))(rb, rb, rb, rb)


def _adamw(w, g, m, v, *, name):
    rows, cols = w.shape
    tr = _row_tile(rows, cols, 7) if rows % 16 == 0 else rows
    c1 = 1.0 - ADAM_B1 ** ADAM_STEP
    c2 = 1.0 - ADAM_B2 ** ADAM_STEP

    def body(w_ref, g_ref, m_ref, v_ref, d_ref, mo_ref, vo_ref):
        gv = g_ref[...]
        mn = ADAM_B1 * m_ref[...] + (1.0 - ADAM_B1) * gv
        vn = ADAM_B2 * v_ref[...] + (1.0 - ADAM_B2) * (gv * gv)
        mo_ref[...] = mn
        vo_ref[...] = vn
        d_ref[...] = -ADAM_LR * ((mn / c1) / (jnp.sqrt(vn / c2) + ADAM_EPS) + ADAM_WD * w_ref[...])

    spec = pl.BlockSpec((tr, cols), lambda i: (i, 0))
    shape = jax.ShapeDtypeStruct((rows, cols), F32)
    return pl.pallas_call(body, name=name, grid=(rows // tr,), in_specs=[spec] * 4, out_specs=[spec] * 3,
                          out_shape=[shape] * 3, compiler_params=_params(("parallel",)))(w, g, m, v)


W_SPECS = (
    ("w_in", 1024, IN_COLS, 1, IN_COLS // 4),
    ("w_mem_kv", 1024, 1024, 0, 256),
    ("w_proj_hg", 1024, 1024, 0, 256),
    ("w_proj_da", 512, 1024, 1, 256),
    ("w_proj_mem", 512, 1024, 1, 256),
    ("w_out", 1024, 1024, 0, 256),
    ("w_ffn_in", 1024, 2 * D_FF, 1, 2 * D_FF // 4),
    ("w_ffn_out", D_FF, 1024, 0, D_FF // 4),
)
CHIP_FLIPS = ((1, 0), (0, 1), (1, 1))
ANY = pl.BlockSpec(memory_space=pl.ANY)
DMA_CHUNK_BYTES = 1 << 20
STAGE_BYTES = 2 << 20


def _place():
    x, y, c = lax.axis_index("x"), lax.axis_index("y"), lax.axis_index("c")
    return x, y, c, 2 * x + y


def _flip(v, f):
    return 1 - v if f else v


def _slab(ref, axis, idx, size):
    start = pl.multiple_of(idx * size, size)
    return ref.at[pl.ds(start, size), :] if axis == 0 else ref.at[:, pl.ds(start, size)]


def _chunked(make, src, dst):
    rows, cols = src.shape
    row_bytes = cols * jnp.dtype(src.dtype).itemsize
    k = 1
    while rows % (2 * k) == 0 and (rows // (2 * k)) % 16 == 0 and (rows // k) * row_bytes > DMA_CHUNK_BYTES:
        k *= 2
    cr = rows // k
    parts = [make(src.at[pl.ds(j * cr, cr), :], dst.at[pl.ds(j * cr, cr), :]) for j in range(k)]
    return parts, make(src, dst)


def _run_copies(local, remote):
    for parts, _ in local + remote:
        for cp in parts:
            cp.start()
    for _, whole in remote:
        whole.wait_recv()
    for _, whole in remote:
        whole.wait_send()
    for _, whole in local:
        whole.wait()


def _half_spec(rows, cols, axis):
    return (0, rows // 2) if axis == 1 else (1, cols // 2)


def _staged(src, remote_dst, local_dst, sibling, load_sems, send_sems, store_sems, recv_sem):
    rows, cols = src.shape
    row_bytes = cols * jnp.dtype(src.dtype).itemsize
    k = 1
    while (rows // k) * row_bytes > STAGE_BYTES and rows % (2 * k) == 0 and (rows // (2 * k)) % 16 == 0:
        k *= 2
    cr = rows // k
    piece = lambda ref, j: ref.at[pl.ds(j * cr, cr), :]

    def run(buf):
        loads = [pltpu.make_async_copy(piece(src, j), buf.at[j % 2], load_sems.at[j % 2]) for j in range(k)]
        outs = [[pltpu.make_async_remote_copy(src_ref=buf.at[j % 2], dst_ref=piece(remote_dst, j),
                                              send_sem=send_sems.at[j % 2], recv_sem=recv_sem,
                                              device_id=sibling, device_id_type=MESH)] for j in range(k)]
        if local_dst is not None:
            for j in range(k):
                outs[j].append(pltpu.make_async_copy(buf.at[j % 2], piece(local_dst, j), store_sems.at[j % 2]))

        def drained(j):
            outs[j][0].wait_send()
            for cp in outs[j][1:]:
                cp.wait()

        loads[0].start()
        for j in range(k):
            loads[j].wait()
            for cp in outs[j]:
                cp.start()
            if j + 1 < k:
                if j >= 1:
                    drained(j - 1)
                loads[j + 1].start()
        for j in range(max(0, k - 2), k):
            drained(j)

    pl.run_scoped(run, pltpu.VMEM((2, cr, cols), src.dtype))


def _landed(ref, recv_sem, send_sem):
    pltpu.make_async_remote_copy(src_ref=ref, dst_ref=ref, send_sem=send_sem, recv_sem=recv_sem,
                                 device_id=(lax.axis_index("x"), lax.axis_index("y"), lax.axis_index("c")),
                                 device_id_type=MESH).wait_recv()


def _gather_weights(fulls):
    n = len(W_SPECS)

    def body(*refs):
        outs = refs[n:2 * n]
        ici_send, ici_recv, load_sems, d2d_send, d2d_recv = refs[2 * n:]
        x, y, c, p = _place()

        def half_slab(wi, chip, half):
            _, rows, cols, axis, size = W_SPECS[wi]
            haxis, hsize = _half_spec(rows, cols, axis)
            return _slab(_slab(outs[wi], axis, chip, size), haxis, half, hsize)

        sent = []
        for wi in range(n):
            mine = half_slab(wi, p, c)
            for k, (fx, fy) in enumerate(CHIP_FLIPS):
                sent.append(_chunked(lambda s, d, j=3 * wi + k, fx=fx, fy=fy: pltpu.make_async_remote_copy(
                    src_ref=s, dst_ref=d, send_sem=ici_send.at[j], recv_sem=ici_recv.at[j],
                    device_id=(_flip(x, fx), _flip(y, fy), c), device_id_type=MESH), mine, mine))
        for parts, _ in sent:
            for cp in parts:
                cp.start()
        for k, (fx, fy) in enumerate(CHIP_FLIPS):
            q = 2 * _flip(x, fx) + _flip(y, fy)
            for wi in range(n):
                sent[3 * wi + k][1].wait_recv()
                got = half_slab(wi, q, c)
                _staged(got, got, None, (x, y, 1 - c), load_sems, d2d_send, None, d2d_recv.at[3 * wi + k])
        for _, whole in sent:
            whole.wait_send()
        for k, (fx, fy) in enumerate(CHIP_FLIPS):
            q = 2 * _flip(x, fx) + _flip(y, fy)
            for wi in range(n):
                _landed(half_slab(wi, q, 1 - c), d2d_recv.at[3 * wi + k], d2d_send.at[0])

    return pl.pallas_call(
        body, name="gather_weights", in_specs=[ANY] * n, out_specs=[ANY] * n,
        out_shape=[jax.ShapeDtypeStruct(f.shape, f.dtype) for f in fulls],
        input_output_aliases={i: i for i in range(n)},
        scratch_shapes=[pltpu.SemaphoreType.DMA((3 * n,)), pltpu.SemaphoreType.DMA((3 * n,)),
                        pltpu.SemaphoreType.DMA((2,)), pltpu.SemaphoreType.DMA((2,)),
                        pltpu.SemaphoreType.DMA((3 * n,))],
    )(*fulls)


def _sibling_exchange(grads):
    n = len(grads)

    def body(*refs):
        ins, outs = refs[:n], refs[n:2 * n]
        load_sems, send_sems, recv_sems = refs[2 * n:]
        x, y, c, _ = _place()
        for i, (_, haxis, hsize) in enumerate(grads):
            _staged(_slab(ins[i], haxis, 1 - c, hsize), outs[i], None, (x, y, 1 - c),
                    load_sems, send_sems, None, recv_sems.at[i])
        for i in range(n):
            _landed(outs[i], recv_sems.at[i], send_sems.at[0])

    shapes = [jax.ShapeDtypeStruct((hsize, g.shape[1]) if haxis == 0 else (g.shape[0], hsize), F32)
              for g, haxis, hsize in grads]
    return pl.pallas_call(
        body, name="grad_sibling_exchange", in_specs=[ANY] * n, out_specs=[ANY] * n, out_shape=shapes,
        scratch_shapes=[pltpu.SemaphoreType.DMA((2,)), pltpu.SemaphoreType.DMA((2,)),
                        pltpu.SemaphoreType.DMA((n,))],
    )(*[g for g, _, _ in grads])


def _chip_exchange(parts):
    n = len(parts)

    def body(*refs):
        ins, outs = refs[:n], refs[n:2 * n]
        send_sems, recv_sems, local_sems = refs[2 * n:]
        x, y, c, p = _place()
        local, remote = [], []
        for i, (_, axis, size) in enumerate(parts):
            local.append(_chunked(lambda s, d, i=i: pltpu.make_async_copy(s, d, local_sems.at[i]),
                                  _slab(ins[i], axis, p, size), outs[i].at[p]))
            for k, (fx, fy) in enumerate(CHIP_FLIPS):
                px, py = _flip(x, fx), _flip(y, fy)
                remote.append(_chunked(lambda s, d, j=3 * i + k, px=px, py=py: pltpu.make_async_remote_copy(
                    src_ref=s, dst_ref=d, send_sem=send_sems.at[j], recv_sem=recv_sems.at[j],
                    device_id=(px, py, c), device_id_type=MESH), _slab(ins[i], axis, 2 * px + py, size), outs[i].at[p]))
        _run_copies(local, remote)

    shapes = []
    for a, axis, size in parts:
        shapes.append(jax.ShapeDtypeStruct((4, size, a.shape[1]) if axis == 0 else (4, a.shape[0], size), a.dtype))
    return pl.pallas_call(
        body, name="grad_chip_exchange", in_specs=[ANY] * n, out_specs=[ANY] * n, out_shape=shapes,
        scratch_shapes=[pltpu.SemaphoreType.DMA((3 * n,)), pltpu.SemaphoreType.DMA((3 * n,)),
                        pltpu.SemaphoreType.DMA((n,))],
    )(*[a for a, _, _ in parts])


def _sibling_share(sums):
    n = len(sums)

    def body(*refs):
        ins, outs = refs[:n], refs[n:2 * n]
        load_sems, send_sems, store_sems, recv_sems = refs[2 * n:]
        x, y, c, _ = _place()
        for i, (s, haxis) in enumerate(sums):
            place = _slab(outs[i], haxis, c, s.shape[haxis])
            _staged(ins[i], place, place, (x, y, 1 - c), load_sems, send_sems, store_sems, recv_sems.at[i])
        for i, (s, haxis) in enumerate(sums):
            _landed(_slab(outs[i], haxis, 1 - c, s.shape[haxis]), recv_sems.at[i], send_sems.at[0])

    shapes = []
    for s, haxis in sums:
        r, cc = s.shape
        shapes.append(jax.ShapeDtypeStruct((2 * r, cc) if haxis == 0 else (r, 2 * cc), F32))
    return pl.pallas_call(
        body, name="grad_sibling_share", in_specs=[ANY] * n, out_specs=[ANY] * n, out_shape=shapes,
        scratch_shapes=[pltpu.SemaphoreType.DMA((2,)), pltpu.SemaphoreType.DMA((2,)),
                        pltpu.SemaphoreType.DMA((2,)), pltpu.SemaphoreType.DMA((n,))],
    )(*[s for s, _ in sums])


def _reduce_scatter(gw):
    grads = []
    for name, _, _, axis, _ in W_SPECS:
        for g in ((gw["w_ffn_a"], gw["w_ffn_b"]) if name == "w_ffn_in" else (gw[name],)):
            grads.append((g,) + _half_spec(g.shape[0], g.shape[1], axis))
    theirs = _sibling_exchange(grads)
    parts, j = [], 0
    for name, _, _, axis, size in W_SPECS:
        take = 2 if name == "w_ffn_in" else 1
        items = [grads[i] + (theirs[i],) for i in range(j, j + take)]
        parts.append((_add_halves(items, name=f"half_sum_{name}"), axis, size))
        j += take
    slots = _chip_exchange(parts)
    sums = []
    for (name, rows, cols, axis, _), rb in zip(W_SPECS, slots):
        sums.append((_add_slots(rb, name=f"chip_sum_{name}"), _half_spec(rows, cols, axis)[0]))
    return dict(zip([s[0] for s in W_SPECS], _sibling_share(sums)))


def _small_allreduce(sv):
    def body(sv_ref, o_ref, slots_ref, send_sems, recv_sems):
        x, y, c, _ = _place()
        me = 4 * x + 2 * y + c
        slots_ref[me] = sv_ref[...]
        copies = []
        for k in range(1, 8):
            fx, fy, fc = (k >> 2) & 1, (k >> 1) & 1, k & 1
            copies.append(pltpu.make_async_remote_copy(
                src_ref=sv_ref, dst_ref=slots_ref.at[me], send_sem=send_sems.at[k - 1],
                recv_sem=recv_sems.at[k - 1], device_id=(_flip(x, fx), _flip(y, fy), _flip(c, fc)),
                device_id_type=MESH))
        for cp in copies:
            cp.start()
        for cp in copies:
            cp.wait_recv()
        for cp in copies:
            cp.wait_send()
        total = slots_ref[0]
        for s in range(1, 8):
            total = total + slots_ref[s]
        o_ref[...] = total

    vm = pl.BlockSpec(memory_space=pltpu.VMEM)
    return pl.pallas_call(
        body, name="small_allreduce", in_specs=[vm], out_specs=vm,
        out_shape=jax.ShapeDtypeStruct(sv.shape, F32),
        scratch_shapes=[pltpu.VMEM((8,) + sv.shape, F32), pltpu.SemaphoreType.DMA((7,)),
                        pltpu.SemaphoreType.DMA((7,))],
    )(sv)


SMALL_ROWS = (("norm_mix_gain", 0), ("norm_mem_gain", 1), ("norm_ffn_gain", 2))
SMALL_LB = (("lb_logits_fw", 3), ("lb_logits_bw", 5))
SMALL_HEAD = ("hg_norm_gain", "da_q_gain", "da_k_gain", "mem_q_gain", "mem_k_gain")


def _pack_small(d):
    last = jnp.concatenate([d[n] for n in SMALL_HEAD] + [jnp.zeros((1, D_MODEL - HEAD * len(SMALL_HEAD)), F32)], axis=1)
    return jnp.concatenate([d["norm_mix_gain"], d["norm_mem_gain"], d["norm_ffn_gain"],
                            d["lb_logits_fw"], d["lb_logits_bw"], last], axis=0)


def _unpack_small(a):
    out = {n: a[r:r + 1] for n, r in SMALL_ROWS}
    out.update({n: a[r:r + 2] for n, r in SMALL_LB})
    out.update({n: a[7:8, j * HEAD:(j + 1) * HEAD] for j, n in enumerate(SMALL_HEAD)})
    return out


PARAM_ORDER = ("norm_mix_gain", "norm_mem_gain", "w_in", "lb_logits_fw", "lb_logits_bw", "hg_norm_gain",
               "da_q_gain", "da_k_gain", "w_mem_kv", "mem_q_gain", "mem_k_gain", "w_proj_hg", "w_proj_da",
               "w_proj_mem", "w_out", "norm_ffn_gain", "w_ffn_in", "w_ffn_out")


def kernel(x, mem, norm_mix_gain, norm_mem_gain, w_in, lb_logits_fw, lb_logits_bw, hg_norm_gain, da_q_gain, da_k_gain, w_mem_kv, mem_q_gain, mem_k_gain, w_proj_hg, w_proj_da, w_proj_mem, w_out, norm_ffn_gain, w_ffn_in, w_ffn_out, loss_target, m_norm_mix_gain, m_norm_mem_gain, m_w_in, m_lb_logits_fw, m_lb_logits_bw, m_hg_norm_gain, m_da_q_gain, m_da_k_gain, m_w_mem_kv, m_mem_q_gain, m_mem_k_gain, m_w_proj_hg, m_w_proj_da, m_w_proj_mem, m_w_out, m_norm_ffn_gain, m_w_ffn_in, m_w_ffn_out, v_norm_mix_gain, v_norm_mem_gain, v_w_in, v_lb_logits_fw, v_lb_logits_bw, v_hg_norm_gain, v_da_q_gain, v_da_k_gain, v_w_mem_kv, v_mem_q_gain, v_mem_k_gain, v_w_proj_hg, v_w_proj_da, v_w_proj_mem, v_w_out, v_norm_ffn_gain, v_w_ffn_in, v_w_ffn_out):
    args = dict(locals())
    mats = tuple(s[0] for s in W_SPECS)
    flat = lambda a: a.reshape(a.shape[-2:])
    w = {n: flat(args[n]) for n in mats}
    m = {n: flat(args["m_" + n]) for n in mats}
    v = {n: flat(args["v_" + n]) for n in mats}
    small = {n: args[n] for n in PARAM_ORDER if n not in mats}

    chip = (2 * lax.axis_index("x") + lax.axis_index("y")).astype(jnp.int32).reshape(1)
    full = _gather_weights([_cast_into_full(w[n], chip, rows, cols, axis, name=f"cast_{n}")
                            for n, rows, cols, axis, _ in W_SPECS])
    loss, grad_x, gw, small_grads = _local_step(x[0], mem[0], loss_target[0], small, dict(zip(mats, full)))
    grads = _reduce_scatter(gw)
    small_sum = _small_allreduce(small_grads)

    delta, new_m, new_v = {}, {}, {}
    for n in mats:
        delta[n], new_m[n], new_v[n] = _adamw(w[n], grads[n], m[n], v[n], name=f"adamw_{n}")
    packed = _adamw(_pack_small(small), small_sum,
                    _pack_small({n: args["m_" + n] for n in small}),
                    _pack_small({n: args["v_" + n] for n in small}), name="adamw_small")
    grads.update(_unpack_small(small_sum))
    for dst, src in zip((delta, new_m, new_v), packed):
        dst.update(_unpack_small(src))

    def shaped(d, n):
        return d[n].reshape(args[n].shape)

    loss_sum = lax.psum(loss[0, 0], ("x", "y", "c"))
    return (loss_sum, grad_x[None], *[shaped(grads, n) for n in PARAM_ORDER], *[shaped(delta, n) for n in PARAM_ORDER],
            *[shaped(new_m, n) for n in PARAM_ORDER], *[shaped(new_v, n) for n in PARAM_ORDER])
```

```python
import functools
import math

import numpy as np
import jax
import jax.numpy as jnp
from jax import lax
from jax.experimental import pallas as pl
from jax.experimental.pallas import tpu as pltpu

F32 = jnp.float32
BF16 = jnp.bfloat16
MESH = pl.DeviceIdType.MESH

D_MODEL = 1024
HEAD = 128
HG_HEADS = 8
DA_GROUPS = ((1, 64), (4, 64), (16, 64))
DA_GH = 4
MEM_HEADS = 4
N_MEM = 256
D_FF = 2816
CHUNK = 64
BAND_QBLOCKS = 4
GLA_HEADS_PER_STEP = 2
RMS_EPS = 1e-6
NEG_INF = -1e30
HG_SCALE = HEAD ** -0.5
ATT_SCALE = HEAD ** -0.5
VMEM_LIMIT_V7X = 48 * 1024 * 1024

C_HQ, C_FF, C_FB, C_HI, C_HG = 0, 1024, 2048, 3072, 4096
C_DQ, C_DK, C_DV, C_MQ = 5120, 6656, 8192, 9728
C_GHG, C_GDA, C_GMEM = 10240, 11264, 12288
IN_COLS = 13312

ADAM_LR, ADAM_B1, ADAM_B2, ADAM_EPS, ADAM_WD, ADAM_STEP = 0.001, 0.9, 0.999, 1e-08, 0.01, 10


def _params(sem, vmem=VMEM_LIMIT_V7X):
    return pltpu.CompilerParams(dimension_semantics=sem, vmem_limit_bytes=vmem)


def _dot(a, b):
    return jnp.dot(a.astype(BF16), b.astype(BF16), preferred_element_type=F32)


def _dot_nt(a, b):
    return lax.dot_general(a.astype(BF16), b.astype(BF16), (((1,), (1,)), ((), ())),
                           preferred_element_type=F32)


def _dot_tn(a, b):
    return lax.dot_general(a.astype(BF16), b.astype(BF16), (((0,), (0,)), ((), ())),
                           preferred_element_type=F32)


def _sigmoid(v):
    return jax.nn.sigmoid(v.astype(F32))


def _ones(rows, cols):
    return (lax.broadcasted_iota(jnp.int32, (rows, cols), 0) >= 0).astype(BF16)


def _lane_sum(v):
    ones = _ones(HEAD, HEAD)
    hi = v.astype(BF16)
    mid = (v - hi.astype(F32)).astype(BF16)
    return jnp.dot(hi, ones, preferred_element_type=F32) + jnp.dot(mid, ones, preferred_element_type=F32)


def _row_mean(v):
    if v.shape[-1] == HEAD:
        return _lane_sum(v) * (1.0 / HEAD)
    return jnp.mean(v, axis=-1, keepdims=True)


def _rms(v):
    v = v.astype(F32)
    r = lax.rsqrt(_row_mean(v * v) + RMS_EPS)
    return v * r, r


def _rms_bwd(dy, xhat, r, gain):
    dxh = dy * gain
    dx = r * (dxh - xhat * _row_mean(dxh * xhat))
    return dx, dy * xhat


def _matmul(pairs, mode, out_dtype, *, tm, tn, tk=None, residual=None, name):
    b_offs = [pr[2] if len(pr) > 2 else 0 for pr in pairs]
    pairs = [pr[:2] for pr in pairs]
    a0, b0 = pairs[0]
    if mode == "nn":
        (m, kk), n = a0.shape, b0.shape[1]
    elif mode == "nt":
        (m, kk), n = a0.shape, b0.shape[0]
    else:
        (kk, m), n = a0.shape, b0.shape[1]
    assert mode == "nt" or not any(b_offs)
    tm, tn = min(tm, m), min(tn, n)
    tk = kk if tk is None else tk
    nk = kk // tk
    assert m % tm == 0 and n % tn == 0 and kk % tk == 0, (name, m, n, kk)
    n_p = len(pairs)
    if mode == "tn":
        a_spec = pl.BlockSpec((tk, tm), lambda i, j, k: (k, i))
    else:
        a_spec = pl.BlockSpec((tm, tk), lambda i, j, k: (i, k))
    if mode == "nt":
        b_specs = [pl.BlockSpec((tn, tk), lambda i, j, k, o=o: (j, o * nk + k)) for o in b_offs]
    else:
        b_specs = [pl.BlockSpec((tk, tn), lambda i, j, k: (k, j))] * n_p
    o_spec = pl.BlockSpec((tm, tn), lambda i, j, k: (i, j))
    dot = {"nn": _dot, "nt": _dot_nt, "tn": _dot_tn}[mode]
    has_res = residual is not None

    def body(*refs):
        a_refs, b_refs = refs[:n_p], refs[n_p:2 * n_p]
        pos = 2 * n_p
        res_ref = refs[pos] if has_res else None
        pos += int(has_res)
        o_ref = refs[pos]
        part = dot(a_refs[0][...], b_refs[0][...])
        for a_r, b_r in zip(a_refs[1:], b_refs[1:]):
            part += dot(a_r[...], b_r[...])

        def finish(total):
            if has_res:
                total = total + res_ref[...]
            o_ref[...] = total.astype(out_dtype)

        if nk == 1:
            finish(part)
        else:
            acc_ref = refs[pos + 1]
            k = pl.program_id(2)

            @pl.when(k == 0)
            def _():
                acc_ref[...] = part

            @pl.when(k > 0)
            def _():
                acc_ref[...] += part

            @pl.when(k == nk - 1)
            def _():
                finish(acc_ref[...])

    ins = [a for a, _ in pairs] + [b for _, b in pairs]
    in_specs = [a_spec] * n_p + b_specs
    if has_res:
        ins.append(residual)
        in_specs.append(o_spec)
    return pl.pallas_call(
        body, name=name, grid=(m // tm, n // tn, nk),
        in_specs=in_specs, out_specs=o_spec,
        out_shape=jax.ShapeDtypeStruct((m, n), out_dtype),
        scratch_shapes=[pltpu.VMEM((tm, tn), F32)] if nk > 1 else [],
        compiler_params=_params(("parallel", "parallel", "arbitrary")),
    )(*ins)


def _rms_fwd(x, gain, *, name):
    rows, dm = x.shape
    tm = min(512, rows)

    def body(x_ref, g_ref, h_ref):
        xhat, _ = _rms(x_ref[...])
        h_ref[...] = (xhat * g_ref[...]).astype(BF16)

    return pl.pallas_call(
        body, name=name, grid=(rows // tm,),
        in_specs=[pl.BlockSpec((tm, dm), lambda i: (i, 0)), pl.BlockSpec((1, dm), lambda i: (0, 0))],
        out_specs=pl.BlockSpec((tm, dm), lambda i: (i, 0)),
        out_shape=jax.ShapeDtypeStruct((rows, dm), BF16),
        compiler_params=_params(("parallel",)),
    )(x, gain)


def _rms_bwd_rows(dh, x, gain, dres, *, name):
    rows, dm = x.shape
    tm = min(512, rows)
    has_res = dres is not None

    def body(*refs):
        dh_ref, x_ref, g_ref = refs[:3]
        res_ref = refs[3] if has_res else None
        dx_ref, dxb_ref, dg_ref = refs[3 + int(has_res):]
        xhat, r = _rms(x_ref[...])
        dx, dgr = _rms_bwd(dh_ref[...], xhat, r, g_ref[...])
        if has_res:
            dx = dx + res_ref[...]
        dx_ref[...] = dx
        dxb_ref[...] = dx.astype(BF16)

        @pl.when(pl.program_id(0) == 0)
        def _():
            dg_ref[...] = jnp.zeros_like(dg_ref)

        dg_ref[...] += jnp.sum(dgr, axis=0, keepdims=True)

    row = pl.BlockSpec((tm, dm), lambda i: (i, 0))
    vec = pl.BlockSpec((1, dm), lambda i: (0, 0))
    return pl.pallas_call(
        body, name=name, grid=(rows // tm,),
        in_specs=[row, row, vec] + ([row] if has_res else []),
        out_specs=[row, row, vec],
        out_shape=[jax.ShapeDtypeStruct((rows, dm), F32), jax.ShapeDtypeStruct((rows, dm), BF16),
                   jax.ShapeDtypeStruct((1, dm), F32)],
        compiler_params=_params(("arbitrary",)),
    )(*([dh, x, gain] + ([dres] if has_res else [])))


def _loss_grad(y, tgt, *, name):
    rows, dm = y.shape
    tm = min(512, rows)
    steps = rows // tm

    def body(y_ref, t_ref, dy_ref, dyb_ref, loss_ref, acc_ref):
        i = pl.program_id(0)
        diff = y_ref[...] - t_ref[...]
        dy = diff * (1.0 / dm)
        dy_ref[...] = dy
        dyb_ref[...] = dy.astype(BF16)

        @pl.when(i == 0)
        def _():
            acc_ref[...] = jnp.zeros_like(acc_ref)

        acc_ref[...] += jnp.sum(diff * diff, axis=0, keepdims=True)

        @pl.when(i == steps - 1)
        def _():
            loss_ref[...] = jnp.full((1, HEAD), 0.5 / dm, F32) * jnp.sum(acc_ref[...])

    row = pl.BlockSpec((tm, dm), lambda i: (i, 0))
    return pl.pallas_call(
        body, name=name, grid=(steps,),
        in_specs=[row, row],
        out_specs=[row, row, pl.BlockSpec((1, HEAD), lambda i: (0, 0))],
        out_shape=[jax.ShapeDtypeStruct((rows, dm), F32), jax.ShapeDtypeStruct((rows, dm), BF16),
                   jax.ShapeDtypeStruct((1, HEAD), F32)],
        scratch_shapes=[pltpu.VMEM((1, dm), F32)],
        compiler_params=_params(("arbitrary",)),
    )(y, tgt)


def _gla_block_terms(q_raw, f_logit, lb, rev):
    sig = _sigmoid(f_logit)
    forget = lb + (1.0 - lb) * sig
    k = 1.0 - forget
    b = _chunk_cumsum(jnp.log(forget), rev)
    qs = _sigmoid(q_raw)
    eb = jnp.exp(b)
    emb = jnp.exp(-b)
    qt = (q_raw * qs * HG_SCALE) * eb
    kt = k * emb
    return sig, forget, k, b, qs, eb, emb, qt, kt


def _chunk_cumsum(v, rev):
    n = v.shape[0]
    pos = lax.broadcasted_iota(jnp.int32, v.shape, 0) & (CHUNK - 1)
    step = 1
    while step < CHUNK:
        if rev:
            shifted, keep = pltpu.roll(v, n - step, 0), pos < CHUNK - step
        else:
            shifted, keep = pltpu.roll(v, step, 0), pos >= step
        v = v + jnp.where(keep, shifted, 0.0)
        step *= 2
    return v


def _tri_mask(n, rev):
    row = lax.broadcasted_iota(jnp.int32, (n, n), 0)
    col = lax.broadcasted_iota(jnp.int32, (n, n), 1)
    shift = CHUNK.bit_length() - 1
    same = jnp.right_shift(row, shift) == jnp.right_shift(col, shift)
    return same & ((row <= col) if rev else (row >= col))


def _chunk_order(ncb, rev):
    order = range(ncb - 1, -1, -1) if rev else range(ncb)
    return [(c, c * CHUNK if rev else c * CHUNK + CHUNK - 1) for c in order]


def _gla_fwd(proj, lb, *, f_off, rev, name):
    rows = proj.shape[0]
    tb = min(256, rows)
    nb, ncb = rows // tb, tb // CHUNK

    def tmap(n):
        return nb - 1 - n if rev else n

    def body(q_ref, f_ref, v_ref, lb_ref, o_ref, st_ref, s_ref):
        @pl.when(pl.program_id(1) == 0)
        def _():
            s_ref[...] = jnp.zeros_like(s_ref)

        tri = _tri_mask(tb, rev)
        for hh in range(GLA_HEADS_PER_STEP):
            cs = slice(hh * HEAD, (hh + 1) * HEAD)
            v = v_ref[:, cs]
            _, _, k, b, _, _, _, qt, kt = _gla_block_terms(q_ref[:, cs].astype(F32), f_ref[:, cs].astype(F32),
                                                           lb_ref[:, cs], rev)
            o_intra = _dot(jnp.where(tri, _dot_nt(qt, kt), 0.0), v)
            chunks = []
            for c, last in _chunk_order(ncb, rev):
                sl = slice(c * CHUNK, (c + 1) * CHUNK)
                bl = b[last:last + 1, :]
                kh = k[sl] * jnp.exp(bl - b[sl])
                chunks.append((c, sl, jnp.exp(bl), _dot_tn(v[sl], kh)))
            s_t = s_ref[hh]
            for c, sl, ebl, kv in chunks:
                st_ref[hh, c] = s_t
                o_ref[sl, cs] = o_intra[sl] + _dot_nt(qt[sl], s_t)
                s_t = ebl * s_t + kv
            s_ref[hh] = s_t

    hps = GLA_HEADS_PER_STEP
    col = lambda off: pl.BlockSpec((tb, hps * HEAD), lambda h, n: (tmap(n), off // (hps * HEAD) + h))
    return pl.pallas_call(
        body, name=name, grid=(HG_HEADS // hps, nb),
        in_specs=[col(C_HQ), col(f_off), col(C_HI), pl.BlockSpec((1, hps * HEAD), lambda h, n: (0, h))],
        out_specs=[pl.BlockSpec((tb, hps * HEAD), lambda h, n: (tmap(n), h)),
                   pl.BlockSpec((hps, ncb, HEAD, HEAD), lambda h, n: (h, tmap(n), 0, 0))],
        out_shape=[jax.ShapeDtypeStruct((rows, HG_HEADS * HEAD), F32),
                   jax.ShapeDtypeStruct((HG_HEADS, rows // CHUNK, HEAD, HEAD), F32)],
        scratch_shapes=[pltpu.VMEM((hps, HEAD, HEAD), F32)],
        compiler_params=_params(("parallel", "arbitrary")),
    )(proj, proj, proj, lb)


def _gla_bwd(proj, lb, do, states, prev, *, f_off, rev, name):
    rows = proj.shape[0]
    tb = min(256, rows)
    nb, ncb = rows // tb, tb // CHUNK
    has_prev = prev is not None
    qv_dtype = BF16 if has_prev else F32

    def tmap(n):
        return n if rev else nb - 1 - n

    def body(*refs):
        q_ref, f_ref, v_ref, lb_ref, do_ref, st_ref = refs[:6]
        pq_ref, pv_ref = refs[6:8] if has_prev else (None, None)
        (dq_ref, df_ref, dv_ref, dlb_ref,
         ds_ref, dqt_scr, dk_scr, db_scr, dbl_scr, dv_scr) = refs[6 + 2 * int(has_prev):]

        @pl.when(pl.program_id(1) == 0)
        def _():
            ds_ref[...] = jnp.zeros_like(ds_ref)
            dlb_ref[...] = jnp.zeros_like(dlb_ref)

        tri = _tri_mask(tb, rev)
        for hh in range(GLA_HEADS_PER_STEP):
            cs = slice(hh * HEAD, (hh + 1) * HEAD)
            lbv = lb_ref[:, cs]
            q_raw = q_ref[:, cs].astype(F32)
            v, dout = v_ref[:, cs], do_ref[:, cs].astype(BF16)
            sig, forget, k, b, qs, eb, emb, qt, kt = _gla_block_terms(q_raw, f_ref[:, cs].astype(F32), lbv, rev)
            a = jnp.where(tri, _dot_nt(qt, kt), 0.0)
            da = jnp.where(tri, _dot_nt(dout, v), 0.0)
            dv_intra = _dot_tn(a, dout)
            dqt_intra = _dot(da, kt)
            dkt = _dot_tn(da, qt)
            chunks = []
            for c, last in reversed(_chunk_order(ncb, rev)):
                sl = slice(c * CHUNK, (c + 1) * CHUNK)
                bl = b[last:last + 1, :]
                e = jnp.exp(bl - b[sl])
                s_t = st_ref[hh, c]
                dqt_scr[sl, cs] = dqt_intra[sl] + _dot(dout[sl], s_t)
                chunks.append((sl, jnp.exp(bl), e, k[sl] * e, s_t, _dot_tn(dout[sl], qt[sl])))
            ds_t = ds_ref[hh]
            for sl, ebl, e, kh, s_t, grow in chunks:
                dkh = _dot(v[sl], ds_t)
                dv_scr[sl, cs] = dv_intra[sl] + _dot_nt(kh, ds_t)
                dk_scr[sl, cs] = dkt[sl] * emb[sl] + dkh * e
                khd = kh * dkh
                dbl = jnp.sum(khd, axis=0, keepdims=True) + ebl * jnp.sum(ds_t * s_t, axis=0, keepdims=True)
                db_scr[sl, cs] = khd
                dbl_scr[sl, cs] = jnp.broadcast_to(dbl, (CHUNK, HEAD))
                ds_t = grow + ds_t * ebl
            ds_ref[hh] = ds_t
            dqt = dqt_scr[:, cs]
            dlogf = _chunk_cumsum(qt * dqt - kt * dkt - db_scr[:, cs], not rev) + dbl_scr[:, cs]
            dforget = dlogf / forget - dk_scr[:, cs]
            df_ref[:, cs] = (dforget * (1.0 - lbv) * sig * (1.0 - sig)).astype(BF16)
            dlb_ref[:, cs] += jnp.sum(dforget * (1.0 - sig), axis=0, keepdims=True)
            dqr = dqt * eb * (HG_SCALE * qs * (1.0 + q_raw * (1.0 - qs)))
            dv = dv_scr[:, cs]
            if has_prev:
                dqr = dqr + pq_ref[:, cs]
                dv = dv + pv_ref[:, cs]
            dq_ref[:, cs] = dqr.astype(qv_dtype)
            dv_ref[:, cs] = dv.astype(qv_dtype)

    hps = GLA_HEADS_PER_STEP
    col = lambda off: pl.BlockSpec((tb, hps * HEAD), lambda h, n: (tmap(n), off // (hps * HEAD) + h))
    blk = pl.BlockSpec((tb, hps * HEAD), lambda h, n: (tmap(n), h))
    vec = pl.BlockSpec((1, hps * HEAD), lambda h, n: (0, h))
    wide = HG_HEADS * HEAD
    return pl.pallas_call(
        body, name=name, grid=(HG_HEADS // hps, nb),
        in_specs=[col(C_HQ), col(f_off), col(C_HI), vec, blk,
                  pl.BlockSpec((hps, ncb, HEAD, HEAD), lambda h, n: (h, tmap(n), 0, 0))]
                 + ([blk, blk] if has_prev else []),
        out_specs=[blk, blk, blk, vec],
        out_shape=[jax.ShapeDtypeStruct((rows, wide), qv_dtype), jax.ShapeDtypeStruct((rows, wide), BF16),
                   jax.ShapeDtypeStruct((rows, wide), qv_dtype), jax.ShapeDtypeStruct((1, wide), F32)],
        scratch_shapes=[pltpu.VMEM((hps, HEAD, HEAD), F32)] + [pltpu.VMEM((tb, hps * HEAD), F32)] * 5,
        compiler_params=_params(("parallel", "arbitrary")),
    )(*([proj, proj, proj, lb, do, states] + (list(prev) if has_prev else [])))


def _hg_out_fwd(o_fw, o_bw, proj, gain, *, name):
    rows = o_fw.shape[0]
    tm = min(512, rows)
    wide = HG_HEADS * HEAD

    def body(a_ref, b_ref, g_ref, gain_ref, o_ref):
        for h in range(HG_HEADS):
            sl = slice(h * HEAD, (h + 1) * HEAD)
            xhat, _ = _rms(a_ref[:, sl] + b_ref[:, sl])
            gate = g_ref[:, sl].astype(F32)
            o_ref[:, sl] = (xhat * gain_ref[...] * (gate * _sigmoid(gate))).astype(BF16)

    row = pl.BlockSpec((tm, wide), lambda i: (i, 0))
    return pl.pallas_call(
        body, name=name, grid=(rows // tm,),
        in_specs=[row, row, pl.BlockSpec((tm, wide), lambda i: (i, C_HG // wide)),
                  pl.BlockSpec((1, HEAD), lambda i: (0, 0))],
        out_specs=row, out_shape=jax.ShapeDtypeStruct((rows, wide), BF16),
        compiler_params=_params(("parallel",)),
    )(o_fw, o_bw, proj, gain)


def _hg_out_bwd(dout, o_fw, o_bw, proj, gain, *, name):
    rows = o_fw.shape[0]
    tm = min(512, rows)
    wide = HG_HEADS * HEAD

    def body(d_ref, a_ref, b_ref, g_ref, gain_ref, dgate_ref, do_ref, dgain_ref):
        @pl.when(pl.program_id(0) == 0)
        def _():
            dgain_ref[...] = jnp.zeros_like(dgain_ref)

        dgain = jnp.zeros((1, HEAD), F32)
        for h in range(HG_HEADS):
            sl = slice(h * HEAD, (h + 1) * HEAD)
            xhat, r = _rms(a_ref[:, sl] + b_ref[:, sl])
            gate, dy = g_ref[:, sl].astype(F32), d_ref[:, sl]
            gs = _sigmoid(gate)
            dgate_ref[:, sl] = (dy * xhat * gain_ref[...] * (gs * (1.0 + gate * (1.0 - gs)))).astype(BF16)
            dx, dgr = _rms_bwd(dy * (gate * gs), xhat, r, gain_ref[...])
            do_ref[:, sl] = dx
            dgain = dgain + jnp.sum(dgr, axis=0, keepdims=True)
        dgain_ref[...] += dgain

    row = pl.BlockSpec((tm, wide), lambda i: (i, 0))
    vec = pl.BlockSpec((1, HEAD), lambda i: (0, 0))
    return pl.pallas_call(
        body, name=name, grid=(rows // tm,),
        in_specs=[row, row, row, pl.BlockSpec((tm, wide), lambda i: (i, C_HG // wide)), vec],
        out_specs=[row, row, vec],
        out_shape=[jax.ShapeDtypeStruct((rows, wide), BF16), jax.ShapeDtypeStruct((rows, wide), F32),
                   jax.ShapeDtypeStruct((1, HEAD), F32)],
        compiler_params=_params(("arbitrary",)),
    )(dout, o_fw, o_bw, proj, gain)


def _strided_rows(ref, r, count, d):
    return ref[...] if d == 1 else ref[pl.ds(r, count, stride=d), :]


def _store_strided(ref, r, count, d, val):
    if d == 1:
        ref[...] = val
    else:
        ref[pl.ds(r, count, stride=d), :] = val


def _da_prep(proj, q_gain, k_gain, g, *, name):
    d = DA_GROUPS[g][0]
    rows = proj.shape[0]
    rb = min(512, rows)
    tn = rb // d

    def body(q_ref, k_ref, v_ref, qg_ref, kg_ref, qo_ref, ko_ref, vo_ref, qf_ref, kf_ref, vf_ref):
        for h in range(DA_GH):
            cs = slice(h * HEAD, (h + 1) * HEAD)
            for src, dst in ((q_ref, qf_ref), (k_ref, kf_ref), (v_ref, vf_ref)):
                dst[...] = src[:, cs].astype(F32)
            for r in range(d):
                qhat, _ = _rms(_strided_rows(qf_ref, r, tn, d))
                khat, _ = _rms(_strided_rows(kf_ref, r, tn, d))
                qo_ref[h, r] = (qhat * qg_ref[...]).astype(BF16)
                ko_ref[h, r] = (khat * kg_ref[...]).astype(BF16)
                vo_ref[h, r] = _strided_rows(vf_ref, r, tn, d).astype(BF16)

    wide = DA_GH * HEAD
    col = lambda off: pl.BlockSpec((rb, wide), lambda i: (i, off // wide + g))
    vec = pl.BlockSpec((1, HEAD), lambda i: (0, 0))
    out = pl.BlockSpec((DA_GH, d, tn, HEAD), lambda i: (0, 0, i, 0))
    shape = jax.ShapeDtypeStruct((DA_GH, d, rows // d, HEAD), BF16)
    return pl.pallas_call(
        body, name=name, grid=(rows // rb,),
        in_specs=[col(C_DQ), col(C_DK), col(C_DV), vec, vec],
        out_specs=[out, out, out], out_shape=[shape, shape, shape],
        scratch_shapes=[pltpu.VMEM((rb, HEAD), F32)] * 3,
        compiler_params=_params(("parallel",)),
    )(proj, proj, proj, q_gain, k_gain)


def _slopes(g):
    idx = np.arange(g * DA_GH + 1, (g + 1) * DA_GH + 1)
    s = (2.0 ** (-8.0 * idx / (DA_GH * len(DA_GROUPS)))).astype(np.float32)
    return jnp.asarray(np.broadcast_to(s[:, None, None], (DA_GH, 8, HEAD)).copy())


def _band_window(ld, t, radius):
    win = min(2 * t, ld)
    assert t // 2 >= radius or win == ld
    return win


def _band_scores(q, k, q0, start, slope, d, radius):
    t, win = q.shape[0], k.shape[0]
    row = lax.broadcasted_iota(jnp.int32, (t, win), 0)
    col = lax.broadcasted_iota(jnp.int32, (t, win), 1)
    rel = jnp.abs((start - q0) + col - row)
    return _dot_nt(q, k) * ATT_SCALE + jnp.where(rel <= radius, -slope * (d * rel).astype(F32), NEG_INF)


def _band_fwd(qr, kr, vr, g, *, name):
    d, radius = DA_GROUPS[g]
    _, _, ld, _ = qr.shape
    t = min(HEAD, ld)
    qb = min(BAND_QBLOCKS, ld // t)
    win = _band_window(ld, t, radius)

    def body(q_ref, k_ref, v_ref, sl_ref, o_ref, lse_ref):
        i = pl.program_id(2)
        slope = sl_ref[0:1, 0:1]
        for j in range(qb):
            sl = slice(j * t, (j + 1) * t)
            q0 = (i * qb + j) * t
            start = pl.multiple_of(jnp.clip(q0 - t // 2, 0, ld - win), t // 2)
            s = _band_scores(q_ref[sl, :], k_ref[pl.ds(start, win), :], q0, start, slope, d, radius)
            m = jnp.max(s, axis=-1, keepdims=True)
            p = jnp.exp(s - m).astype(BF16)
            l = jnp.dot(p, _ones(win, HEAD), preferred_element_type=F32)
            o_ref[sl, :] = _dot(p, v_ref[pl.ds(start, win), :]) / l
            lse_ref[sl, :] = m + jnp.log(l)

    own = pl.BlockSpec((None, None, qb * t, HEAD), lambda h, r, i: (h, r, i, 0))
    seq = pl.BlockSpec((None, None, ld, HEAD), lambda h, r, i: (h, r, 0, 0))
    shape = jax.ShapeDtypeStruct(qr.shape, F32)
    return pl.pallas_call(
        body, name=name, grid=(DA_GH, d, ld // (qb * t)),
        in_specs=[own, seq, seq, pl.BlockSpec((None, 8, HEAD), lambda h, r, i: (h, 0, 0))],
        out_specs=[own, own], out_shape=[shape, shape],
        compiler_params=_params(("parallel", "parallel", "parallel")),
    )(qr, kr, vr, _slopes(g))


def _band_bwd(qr, kr, vr, dor, lser, deltar, g, *, name):
    d, radius = DA_GROUPS[g]
    _, _, ld, _ = qr.shape
    t = min(HEAD, ld)
    qb = min(BAND_QBLOCKS, ld // t)
    win = _band_window(ld, t, radius)

    def body(q_ref, k_ref, v_ref, do_ref, lse_ref, dl_ref, sl_ref, dq_ref, dk_ref, dv_ref):
        i = pl.program_id(2)

        @pl.when(i == 0)
        def _():
            dk_ref[...] = jnp.zeros_like(dk_ref)
            dv_ref[...] = jnp.zeros_like(dv_ref)

        slope = sl_ref[0:1, 0:1]
        for j in range(qb):
            sl = slice(j * t, (j + 1) * t)
            q0 = (i * qb + j) * t
            start = pl.multiple_of(jnp.clip(q0 - t // 2, 0, ld - win), t // 2)
            rows = pl.ds(start, win)
            q, dout, k, v = q_ref[sl, :], do_ref[sl, :], k_ref[rows, :], v_ref[rows, :]
            p = jnp.exp(_band_scores(q, k, q0, start, slope, d, radius) - lse_ref[sl, 0:1])
            ds = p * (_dot_nt(dout, v) - dl_ref[sl, 0:1])
            dq_ref[sl, :] = _dot(ds, k) * ATT_SCALE
            dk_ref[rows, :] += _dot_tn(ds, q) * ATT_SCALE
            dv_ref[rows, :] += _dot_tn(p, dout)

    own = pl.BlockSpec((None, None, qb * t, HEAD), lambda h, r, i: (h, r, i, 0))
    seq = pl.BlockSpec((None, None, ld, HEAD), lambda h, r, i: (h, r, 0, 0))
    shape = jax.ShapeDtypeStruct(qr.shape, F32)
    return pl.pallas_call(
        body, name=name, grid=(DA_GH, d, ld // (qb * t)),
        in_specs=[own, seq, seq, own, own, own, pl.BlockSpec((None, 8, HEAD), lambda h, r, i: (h, 0, 0))],
        out_specs=[own, seq, seq], out_shape=[shape, shape, shape],
        compiler_params=_params(("parallel", "parallel", "arbitrary")),
    )(qr, kr, vr, dor, lser, deltar, _slopes(g))


def _da_merge(outs, lses, rows, *, name):
    rb = min(512, rows)
    wide = DA_GH * HEAD

    def body(*refs):
        o_refs, l_refs = refs[0:3], refs[3:6]
        o_ref, lse_ref = refs[6:8]
        on_refs, ln_refs = refs[8:11], refs[11:14]
        for h in range(DA_GH):
            cs = slice(h * HEAD, (h + 1) * HEAD)
            for g, (d, _) in enumerate(DA_GROUPS):
                tn = rb // d
                for r in range(d):
                    _store_strided(on_refs[g], r, tn, d, o_refs[g][h, r])
                    _store_strided(ln_refs[g], r, tn, d, l_refs[g][h, r])
            l0, l1, l2 = ln_refs[0][...], ln_refs[1][...], ln_refs[2][...]
            m = jnp.maximum(jnp.maximum(l0, l1), l2)
            e0, e1, e2 = jnp.exp(l0 - m), jnp.exp(l1 - m), jnp.exp(l2 - m)
            tot = e0 + e1 + e2
            o_ref[:, cs] = (e0 * on_refs[0][...] + e1 * on_refs[1][...] + e2 * on_refs[2][...]) / tot
            lse_ref[:, cs] = m + jnp.log(tot)

    res = lambda d: pl.BlockSpec((DA_GH, d, rb // d, HEAD), lambda i: (0, 0, i, 0))
    nat = pl.BlockSpec((rb, wide), lambda i: (i, 0))
    shape = jax.ShapeDtypeStruct((rows, wide), F32)
    return pl.pallas_call(
        body, name=name, grid=(rows // rb,),
        in_specs=[res(d) for d, _ in DA_GROUPS] * 2,
        out_specs=[nat, nat], out_shape=[shape, shape],
        scratch_shapes=[pltpu.VMEM((rb, HEAD), F32)] * 6,
        compiler_params=_params(("parallel",)),
    )(*outs, *lses)


def _da_bwd_prep(dout, o, lse, *, name):
    rows = o.shape[0]
    rb = min(512, rows)
    wide = DA_GH * HEAD

    def body(d_ref, o_ref, l_ref, *outs):
        d_scr, l_scr, delta_scr = outs[9:12]
        for h in range(DA_GH):
            cs = slice(h * HEAD, (h + 1) * HEAD)
            dv = d_ref[:, cs]
            d_scr[...] = dv
            l_scr[...] = l_ref[:, cs]
            delta_scr[...] = _lane_sum(dv * o_ref[:, cs])
            for g, (d, _) in enumerate(DA_GROUPS):
                tn = rb // d
                for r in range(d):
                    outs[3 * g][h, r] = _strided_rows(d_scr, r, tn, d).astype(BF16)
                    outs[3 * g + 1][h, r] = _strided_rows(l_scr, r, tn, d)
                    outs[3 * g + 2][h, r] = _strided_rows(delta_scr, r, tn, d)

    nat = pl.BlockSpec((rb, wide), lambda i: (i, 0))
    out_specs, out_shape = [], []
    for d, _ in DA_GROUPS:
        for dt in (BF16, F32, F32):
            out_specs.append(pl.BlockSpec((DA_GH, d, rb // d, HEAD), lambda i: (0, 0, i, 0)))
            out_shape.append(jax.ShapeDtypeStruct((DA_GH, d, rows // d, HEAD), dt))
    return pl.pallas_call(
        body, name=name, grid=(rows // rb,),
        in_specs=[nat, nat, nat], out_specs=out_specs, out_shape=out_shape,
        scratch_shapes=[pltpu.VMEM((rb, HEAD), F32)] * 3,
        compiler_params=_params(("parallel",)),
    )(dout, o, lse)


def _da_prep_bwd(dqr, dkr, dvr, proj, q_gain, k_gain, g, *, name):
    d = DA_GROUPS[g][0]
    rows = proj.shape[0]
    rb = min(512, rows)
    tn = rb // d
    wide = DA_GH * HEAD

    def body(dq_ref, dk_ref, dv_ref, q_ref, k_ref, qg_ref, kg_ref, oq_ref, ok_ref, ov_ref, gq_ref, gk_ref, *nat_refs):
        @pl.when(pl.program_id(0) == 0)
        def _():
            gq_ref[...] = jnp.zeros_like(gq_ref)
            gk_ref[...] = jnp.zeros_like(gk_ref)

        for h in range(DA_GH):
            cs = slice(h * HEAD, (h + 1) * HEAD)
            for j, src in enumerate((dq_ref, dk_ref, dv_ref)):
                for r in range(d):
                    _store_strided(nat_refs[j], r, tn, d, src[h, r])
            ov_ref[:, cs] = nat_refs[2][...].astype(BF16)
            for j, (x_ref, gn_ref, out_ref, acc_ref) in enumerate(((q_ref, qg_ref, oq_ref, gq_ref),
                                                                    (k_ref, kg_ref, ok_ref, gk_ref))):
                xhat, r = _rms(x_ref[:, cs])
                dx, dgr = _rms_bwd(nat_refs[j][...], xhat, r, gn_ref[...])
                out_ref[:, cs] = dx.astype(BF16)
                acc_ref[...] += jnp.sum(dgr, axis=0, keepdims=True)

    res = pl.BlockSpec((DA_GH, d, tn, HEAD), lambda i: (0, 0, i, 0))
    col = lambda off: pl.BlockSpec((rb, wide), lambda i: (i, off // wide + g))
    vec = pl.BlockSpec((1, HEAD), lambda i: (0, 0))
    nat = pl.BlockSpec((rb, wide), lambda i: (i, 0))
    shape = jax.ShapeDtypeStruct((rows, wide), BF16)
    gshape = jax.ShapeDtypeStruct((1, HEAD), F32)
    return pl.pallas_call(
        body, name=name, grid=(rows // rb,),
        in_specs=[res, res, res, col(C_DQ), col(C_DK), vec, vec],
        out_specs=[nat, nat, nat, vec, vec], out_shape=[shape, shape, shape, gshape, gshape],
        scratch_shapes=[pltpu.VMEM((rb, HEAD), F32)] * 3,
        compiler_params=_params(("arbitrary",)),
    )(dqr, dkr, dvr, proj, proj, q_gain, k_gain)


def _mem_fwd(proj, kv, q_gain, k_gain, *, name):
    rows = proj.shape[0]
    tm = min(512, rows)
    n_mem = kv.shape[0]

    def body(q_ref, k_ref, v_ref, qg_ref, kg_ref, o_ref):
        qhat, _ = _rms(q_ref[...])
        khat, _ = _rms(k_ref[...])
        s = _dot_nt(qhat * qg_ref[...], khat * kg_ref[...]) * ATT_SCALE
        p = jnp.exp(s - jnp.max(s, axis=-1, keepdims=True))
        p = p / jnp.sum(p, axis=-1, keepdims=True)
        o_ref[...] = _dot(p, v_ref[...]).astype(BF16)

    vec = pl.BlockSpec((1, HEAD), lambda i, h: (0, 0))
    return pl.pallas_call(
        body, name=name, grid=(rows // tm, MEM_HEADS),
        in_specs=[pl.BlockSpec((tm, HEAD), lambda i, h: (i, C_MQ // HEAD + h)),
                  pl.BlockSpec((n_mem, HEAD), lambda i, h: (0, h)),
                  pl.BlockSpec((n_mem, HEAD), lambda i, h: (0, MEM_HEADS + h)), vec, vec],
        out_specs=pl.BlockSpec((tm, HEAD), lambda i, h: (i, h)),
        out_shape=jax.ShapeDtypeStruct((rows, MEM_HEADS * HEAD), BF16),
        compiler_params=_params(("parallel", "parallel")),
    )(proj, kv, kv, q_gain, k_gain)


def _mem_bwd(dout, proj, kv, q_gain, k_gain, *, name):
    rows = proj.shape[0]
    tm = min(512, rows)
    steps = rows // tm
    n_mem = kv.shape[0]

    def body(d_ref, q_ref, k_ref, v_ref, qg_ref, kg_ref, dq_ref, dk_ref, dv_ref, gq_ref, gk_ref, dkn_ref):
        h, i = pl.program_id(0), pl.program_id(1)

        @pl.when((h == 0) & (i == 0))
        def _():
            gq_ref[...] = jnp.zeros_like(gq_ref)
            gk_ref[...] = jnp.zeros_like(gk_ref)

        @pl.when(i == 0)
        def _():
            dkn_ref[...] = jnp.zeros_like(dkn_ref)
            dv_ref[...] = jnp.zeros_like(dv_ref)

        qhat, rq = _rms(q_ref[...])
        khat, rk = _rms(k_ref[...])
        qn, kn = qhat * qg_ref[...], khat * kg_ref[...]
        s = _dot_nt(qn, kn) * ATT_SCALE
        p = jnp.exp(s - jnp.max(s, axis=-1, keepdims=True))
        p = p / jnp.sum(p, axis=-1, keepdims=True)
        dout = d_ref[...]
        dp = _dot_nt(dout, v_ref[...])
        ds = p * (dp - jnp.sum(p * dp, axis=-1, keepdims=True))
        dv_ref[...] += _dot_tn(p, dout)
        dkn_ref[...] += _dot_tn(ds, qn) * ATT_SCALE
        dq, dgr = _rms_bwd(_dot(ds, kn) * ATT_SCALE, qhat, rq, qg_ref[...])
        dq_ref[...] = dq.astype(BF16)
        gq_ref[...] += jnp.sum(dgr, axis=0, keepdims=True)

        @pl.when(i == steps - 1)
        def _():
            dk, dgk = _rms_bwd(dkn_ref[...], khat, rk, kg_ref[...])
            dk_ref[...] = dk
            gk_ref[...] += jnp.sum(dgk, axis=0, keepdims=True)

    vec = pl.BlockSpec((1, HEAD), lambda h, i: (0, 0))
    memh = pl.BlockSpec((n_mem, HEAD), lambda h, i: (0, h))
    tok = pl.BlockSpec((tm, HEAD), lambda h, i: (i, h))
    gshape = jax.ShapeDtypeStruct((1, HEAD), F32)
    return pl.pallas_call(
        body, name=name, grid=(MEM_HEADS, steps),
        in_specs=[tok, pl.BlockSpec((tm, HEAD), lambda h, i: (i, C_MQ // HEAD + h)), memh,
                  pl.BlockSpec((n_mem, HEAD), lambda h, i: (0, MEM_HEADS + h)), vec, vec],
        out_specs=[tok, memh, memh, vec, vec],
        out_shape=[jax.ShapeDtypeStruct((rows, MEM_HEADS * HEAD), BF16),
                   jax.ShapeDtypeStruct((n_mem, MEM_HEADS * HEAD), F32),
                   jax.ShapeDtypeStruct((n_mem, MEM_HEADS * HEAD), F32), gshape, gshape],
        scratch_shapes=[pltpu.VMEM((n_mem, HEAD), F32)],
        compiler_params=_params(("arbitrary", "arbitrary")),
    )(dout, proj, kv, kv, q_gain, k_gain)


def _branch_fwd(o_hg, o_da, o_mem, proj, wp_hg, wp_da, wp_mem, *, name):
    rows = o_hg.shape[0]
    tm = min(256, rows)

    def body(a_ref, b_ref, c_ref, ga_ref, gb_ref, gc_ref, wa_ref, wb_ref, wc_ref, o_ref):
        merged = _sigmoid(ga_ref[...]) * _dot(a_ref[...], wa_ref[...])
        merged += _sigmoid(gb_ref[...]) * _dot(b_ref[...], wb_ref[...])
        merged += _sigmoid(gc_ref[...]) * _dot(c_ref[...], wc_ref[...])
        o_ref[...] = merged.astype(BF16)

    row = lambda w: pl.BlockSpec((tm, w), lambda i: (i, 0))
    gate = lambda off: pl.BlockSpec((tm, D_MODEL), lambda i: (i, off // D_MODEL))
    full = lambda a: pl.BlockSpec(a.shape, lambda i: (0, 0))
    return pl.pallas_call(
        body, name=name, grid=(rows // tm,),
        in_specs=[row(o_hg.shape[1]), row(o_da.shape[1]), row(o_mem.shape[1]),
                  gate(C_GHG), gate(C_GDA), gate(C_GMEM), full(wp_hg), full(wp_da), full(wp_mem)],
        out_specs=row(D_MODEL), out_shape=jax.ShapeDtypeStruct((rows, D_MODEL), BF16),
        compiler_params=_params(("parallel",)),
    )(o_hg, o_da, o_mem, proj, proj, proj, wp_hg, wp_da, wp_mem)


def _branch_bwd(dm, o_hg, o_da, o_mem, proj, wp_hg, wp_da, wp_mem, *, name):
    rows = o_hg.shape[0]
    tm = min(256, rows)

    def body(dm_ref, a_ref, b_ref, c_ref, ga_ref, gb_ref, gc_ref, wa_ref, wb_ref, wc_ref, *outs):
        dmv = dm_ref[...]
        for j, (o_ref, g_ref, w_ref) in enumerate(((a_ref, ga_ref, wa_ref), (b_ref, gb_ref, wb_ref),
                                                   (c_ref, gc_ref, wc_ref))):
            z = _dot(o_ref[...], w_ref[...])
            gs = _sigmoid(g_ref[...])
            dz = (dmv * gs).astype(BF16)
            outs[3 * j][...] = (dmv * z * gs * (1.0 - gs)).astype(BF16)
            outs[3 * j + 1][...] = dz
            outs[3 * j + 2][...] = _dot_nt(dz, w_ref[...])

    row = lambda w: pl.BlockSpec((tm, w), lambda i: (i, 0))
    gate = lambda off: pl.BlockSpec((tm, D_MODEL), lambda i: (i, off // D_MODEL))
    full = lambda a: pl.BlockSpec(a.shape, lambda i: (0, 0))
    out_specs, out_shape = [], []
    for o in (o_hg, o_da, o_mem):
        out_specs += [row(D_MODEL), row(D_MODEL), row(o.shape[1])]
        out_shape += [jax.ShapeDtypeStruct((rows, D_MODEL), BF16), jax.ShapeDtypeStruct((rows, D_MODEL), BF16),
                      jax.ShapeDtypeStruct((rows, o.shape[1]), F32)]
    return pl.pallas_call(
        body, name=name, grid=(rows // tm,),
        in_specs=[row(D_MODEL), row(o_hg.shape[1]), row(o_da.shape[1]), row(o_mem.shape[1]),
                  gate(C_GHG), gate(C_GDA), gate(C_GMEM), full(wp_hg), full(wp_da), full(wp_mem)],
        out_specs=out_specs, out_shape=out_shape,
        compiler_params=_params(("parallel",)),
    )(dm, o_hg, o_da, o_mem, proj, proj, proj, wp_hg, wp_da, wp_mem)


def _ffn_in(h2, w_ab, *, name):
    rows, dff = h2.shape[0], w_ab.shape[1] // 2
    tm, tn = min(1024, rows), 256

    def body(h_ref, wa_ref, wb_ref, a_ref, b_ref, u_ref):
        a = _dot(h_ref[...], wa_ref[...])
        b = _dot(h_ref[...], wb_ref[...])
        a_ref[...] = a.astype(BF16)
        b_ref[...] = b.astype(BF16)
        u_ref[...] = (a * _sigmoid(a) * b).astype(BF16)

    out = pl.BlockSpec((tm, tn), lambda i, j: (i, j))
    return pl.pallas_call(
        body, name=name, grid=(rows // tm, dff // tn),
        in_specs=[pl.BlockSpec((tm, D_MODEL), lambda i, j: (i, 0)),
                  pl.BlockSpec((D_MODEL, tn), lambda i, j: (0, j)),
                  pl.BlockSpec((D_MODEL, tn), lambda i, j: (0, dff // tn + j))],
        out_specs=[out, out, out],
        out_shape=[jax.ShapeDtypeStruct((rows, dff), BF16)] * 3,
        compiler_params=_params(("parallel", "parallel")),
    )(h2, w_ab, w_ab)


def _ffn_act_bwd(dy, w_out, a, b, *, name):
    rows, dff = a.shape
    tm, tn = min(1024, rows), 256

    def body(dy_ref, w_ref, a_ref, b_ref, da_ref, db_ref):
        du = _dot_nt(dy_ref[...], w_ref[...])
        av, bv = a_ref[...].astype(F32), b_ref[...].astype(F32)
        sa = _sigmoid(av)
        da_ref[...] = (du * bv * sa * (1.0 + av * (1.0 - sa))).astype(BF16)
        db_ref[...] = (du * av * sa).astype(BF16)

    tile = pl.BlockSpec((tm, tn), lambda i, j: (i, j))
    return pl.pallas_call(
        body, name=name, grid=(rows // tm, dff // tn),
        in_specs=[pl.BlockSpec((tm, D_MODEL), lambda i, j: (i, 0)),
                  pl.BlockSpec((tn, D_MODEL), lambda i, j: (j, 0)), tile, tile],
        out_specs=[tile, tile],
        out_shape=[jax.ShapeDtypeStruct((rows, dff), BF16), jax.ShapeDtypeStruct((rows, dff), BF16)],
        compiler_params=_params(("parallel", "parallel")),
    )(dy, w_out, a, b)


def _lower_bound(lb_fw, lb_bw, *, name):
    def body(a_ref, b_ref, oa_ref, ob_ref):
        for src, dst in ((a_ref, oa_ref), (b_ref, ob_ref)):
            dst[...] = _sigmoid(src[0:1, :] - src[1:2, :])

    shape = jax.ShapeDtypeStruct((1, lb_fw.shape[1]), F32)
    return pl.pallas_call(body, name=name, out_shape=[shape, shape])(lb_fw, lb_bw)


def _local_step(x, mem, tgt, p, w):
    rows = x.shape[0]
    lb_fw, lb_bw = _lower_bound(p["lb_logits_fw"], p["lb_logits_bw"], name="lower_bound")

    h = _rms_fwd(x, p["norm_mix_gain"], name="norm_mix")
    proj = _matmul([(h, w["w_in"])], "nn", BF16, tm=1024, tn=1024, name="proj_in")
    o_fw, st_fw = _gla_fwd(proj, lb_fw, f_off=C_FF, rev=False, name="gla_fwd_fw")
    o_bw, st_bw = _gla_fwd(proj, lb_bw, f_off=C_FB, rev=True, name="gla_fwd_bw")
    o_hg = _hg_out_fwd(o_fw, o_bw, proj, p["hg_norm_gain"], name="hg_out")

    qkv_r, outs, lses = [], [], []
    for g in range(len(DA_GROUPS)):
        qr, kr, vr = _da_prep(proj, p["da_q_gain"], p["da_k_gain"], g, name=f"da_prep{g}")
        og, lg = _band_fwd(qr, kr, vr, g, name=f"band_fwd{g}")
        qkv_r.append((qr, kr, vr))
        outs.append(og)
        lses.append(lg)
    o_da, lse_da = _da_merge(outs, lses, rows, name="da_merge")

    mem_n = _rms_fwd(mem, p["norm_mem_gain"], name="norm_mem")
    kv = _matmul([(mem_n, w["w_mem_kv"])], "nn", F32, tm=256, tn=512, name="mem_kv")
    o_mem = _mem_fwd(proj, kv, p["mem_q_gain"], p["mem_k_gain"], name="mem_attn")

    merged = _branch_fwd(o_hg, o_da, o_mem, proj, w["w_proj_hg"], w["w_proj_da"], w["w_proj_mem"],
                         name="branch_merge")
    x1 = _matmul([(merged, w["w_out"])], "nn", F32, tm=512, tn=512, residual=x, name="mix_out")
    h2 = _rms_fwd(x1, p["norm_ffn_gain"], name="norm_ffn")
    a, b, u = _ffn_in(h2, w["w_ffn_in"], name="ffn_in")
    y = _matmul([(u, w["w_ffn_out"])], "nn", F32, tm=512, tn=512, residual=x1, name="ffn_out")
    dy, dy_b, loss = _loss_grad(y, tgt, name="loss")

    gw, gs = {}, {}
    gw["w_ffn_out"] = _matmul([(u, dy_b)], "tn", F32, tm=256, tn=512, name="g_ffn_out")
    da, db = _ffn_act_bwd(dy_b, w["w_ffn_out"], a, b, name="ffn_act_bwd")
    gw["w_ffn_a"] = _matmul([(h2, da)], "tn", F32, tm=512, tn=256, name="g_ffn_a")
    gw["w_ffn_b"] = _matmul([(h2, db)], "tn", F32, tm=512, tn=256, name="g_ffn_b")
    dh2 = _matmul([(da, w["w_ffn_in"], 0), (db, w["w_ffn_in"], 1)], "nt", F32, tm=256, tn=1024, name="d_h2")
    dx1, dx1_b, gs["norm_ffn_gain"] = _rms_bwd_rows(dh2, x1, p["norm_ffn_gain"], dy, name="norm_ffn_bwd")
    gw["w_out"] = _matmul([(merged, dx1_b)], "tn", F32, tm=512, tn=512, name="g_out")
    dmerged = _matmul([(dx1_b, w["w_out"])], "nt", F32, tm=512, tn=512, name="d_merged")
    (dg_hg, dz_hg, do_hg, dg_da, dz_da, do_da, dg_mem, dz_mem, do_mem) = _branch_bwd(
        dmerged, o_hg, o_da, o_mem, proj, w["w_proj_hg"], w["w_proj_da"], w["w_proj_mem"], name="branch_bwd")
    gw["w_proj_hg"] = _matmul([(o_hg, dz_hg)], "tn", F32, tm=512, tn=512, name="g_proj_hg")
    gw["w_proj_da"] = _matmul([(o_da, dz_da)], "tn", F32, tm=512, tn=512, name="g_proj_da")
    gw["w_proj_mem"] = _matmul([(o_mem, dz_mem)], "tn", F32, tm=512, tn=512, name="g_proj_mem")

    dq_mem, dk_mem, dv_mem, gs["mem_q_gain"], gs["mem_k_gain"] = _mem_bwd(
        do_mem, proj, kv, p["mem_q_gain"], p["mem_k_gain"], name="mem_attn_bwd")
    dkv = jnp.concatenate([dk_mem, dv_mem], axis=1).astype(BF16)
    gw["w_mem_kv"] = _matmul([(mem_n, dkv)], "tn", F32, tm=512, tn=512, name="g_mem_kv")
    dmem_n = _matmul([(dkv, w["w_mem_kv"])], "nt", F32, tm=256, tn=512, name="d_mem_n")
    _, _, gs["norm_mem_gain"] = _rms_bwd_rows(dmem_n, mem, p["norm_mem_gain"], None, name="norm_mem_bwd")

    prep = _da_bwd_prep(do_da, o_da, lse_da, name="da_bwd_prep")
    d_da, gq_parts, gk_parts = [], [], []
    for g in range(len(DA_GROUPS)):
        qr, kr, vr = qkv_r[g]
        dor, lser, deltar = prep[3 * g:3 * g + 3]
        dqr, dkr, dvr = _band_bwd(qr, kr, vr, dor, lser, deltar, g, name=f"band_bwd{g}")
        dq, dk, dv, gq, gk = _da_prep_bwd(dqr, dkr, dvr, proj, p["da_q_gain"], p["da_k_gain"], g,
                                          name=f"da_prep_bwd{g}")
        d_da.append((dq, dk, dv))
        gq_parts.append(gq)
        gk_parts.append(gk)

    dg_hgate, do_gla, gs["hg_norm_gain"] = _hg_out_bwd(do_hg, o_fw, o_bw, proj, p["hg_norm_gain"], name="hg_out_bwd")
    dq_f, dfl_fw, dv_f, dlb_fw = _gla_bwd(proj, lb_fw, do_gla, st_fw, None, f_off=C_FF, rev=False, name="gla_bwd_fw")
    dq_hg, dfl_bw, dv_hg, dlb_bw = _gla_bwd(proj, lb_bw, do_gla, st_bw, (dq_f, dv_f), f_off=C_FB, rev=True,
                                            name="gla_bwd_bw")

    dproj = jnp.concatenate(
        [dq_hg, dfl_fw, dfl_bw, dv_hg, dg_hgate]
        + [t[0] for t in d_da] + [t[1] for t in d_da] + [t[2] for t in d_da]
        + [dq_mem, dg_hg, dg_da, dg_mem], axis=1)
    gw["w_in"] = _matmul([(h, dproj)], "tn", F32, tm=512, tn=1024, name="g_in")
    dh = _matmul([(dproj, w["w_in"])], "nt", F32, tm=1024, tn=1024, tk=IN_COLS // 8, name="d_h")
    grad_x, _, gs["norm_mix_gain"] = _rms_bwd_rows(dh, x, p["norm_mix_gain"], dx1, name="norm_mix_bwd")

    small = _small_pack(gs, gq_parts, gk_parts, dlb_fw, dlb_bw, lb_fw, lb_bw, name="small_pack")
    return loss, grad_x, gw, small


def _small_pack(gs, gq_parts, gk_parts, dlb_fw, dlb_bw, lb_fw, lb_bw, *, name):
    def body(g_mix, g_mem, g_ffn, dfw, dbw, lfw, lbw, g_hg, q0, q1, q2, k0, k1, k2, g_mq, g_mk, o_ref):
        o_ref[0:1, :] = g_mix[...]
        o_ref[1:2, :] = g_mem[...]
        o_ref[2:3, :] = g_ffn[...]
        for base, d_ref, l_ref in ((3, dfw, lfw), (5, dbw, lbw)):
            lbv = l_ref[...]
            t = d_ref[...] * lbv * (1.0 - lbv)
            o_ref[base:base + 1, :] = t
            o_ref[base + 1:base + 2, :] = -t
        o_ref[7:8, :] = jnp.zeros((1, D_MODEL), F32)
        o_ref[7:8, 0:HEAD] = g_hg[...]
        o_ref[7:8, HEAD:2 * HEAD] = q0[...] + q1[...] + q2[...]
        o_ref[7:8, 2 * HEAD:3 * HEAD] = k0[...] + k1[...] + k2[...]
        o_ref[7:8, 3 * HEAD:4 * HEAD] = g_mq[...]
        o_ref[7:8, 4 * HEAD:5 * HEAD] = g_mk[...]

    return pl.pallas_call(body, name=name, out_shape=jax.ShapeDtypeStruct((8, D_MODEL), F32))(
        gs["norm_mix_gain"], gs["norm_mem_gain"], gs["norm_ffn_gain"], dlb_fw, dlb_bw, lb_fw, lb_bw,
        gs["hg_norm_gain"], *gq_parts, *gk_parts, gs["mem_q_gain"], gs["mem_k_gain"])


def _row_tile(rows, cols, n_arrays):
    budget = (16 * 1024 * 1024) // (2 * 4 * cols * n_arrays)
    tr = rows
    while tr > budget and tr % 2 == 0 and (tr // 2) % 16 == 0:
        tr //= 2
    return tr


def _cast_into_full(a, chip, rows, cols, axis, *, name):
    sr, sc = a.shape
    tr = _row_tile(sr, sc, 2)

    def body(chip_ref, a_ref, o_ref):
        del chip_ref
        o_ref[...] = a_ref[...].astype(BF16)

    if axis == 1:
        out_map = lambda i, chip_ref: (i, chip_ref[0])
    else:
        out_map = lambda i, chip_ref: (chip_ref[0] * (sr // tr) + i, 0)
    return pl.pallas_call(
        body, name=name,
        grid_spec=pltpu.PrefetchScalarGridSpec(
            num_scalar_prefetch=1, grid=(sr // tr,),
            in_specs=[pl.BlockSpec((tr, sc), lambda i, chip_ref: (i, 0))],
            out_specs=pl.BlockSpec((tr, sc), out_map)),
        out_shape=jax.ShapeDtypeStruct((rows, cols), BF16),
        compiler_params=_params(("parallel",)))(chip, a)


def _add_halves(items, *, name):
    rows = items[0][3].shape[0]
    widths = [ra.shape[1] for _, _, _, ra in items]
    tr = _row_tile(rows, sum(widths), 4)

    def body(*refs):
        o_ref = refs[-1]
        first = lax.axis_index("c") == 0
        off = 0
        for j, wd in enumerate(widths):
            h0, h1, ra = refs[3 * j:3 * j + 3]
            o_ref[:, off:off + wd] = (jnp.where(first, h0[...], h1[...]) + ra[...]).astype(BF16)
            off += wd

    in_specs, ins = [], []
    for (g, haxis, hsize, ra), wd in zip(items, widths):
        if haxis == 0:
            in_specs += [pl.BlockSpec((tr, wd), lambda i: (i, 0)),
                         pl.BlockSpec((tr, wd), lambda i, o=hsize // tr: (o + i, 0))]
        else:
            in_specs += [pl.BlockSpec((tr, wd), lambda i: (i, 0)), pl.BlockSpec((tr, wd), lambda i: (i, 1))]
        in_specs.append(pl.BlockSpec((tr, wd), lambda i: (i, 0)))
        ins += [g, g, ra]
    return pl.pallas_call(body, name=name, grid=(rows // tr,), in_specs=in_specs,
                          out_specs=pl.BlockSpec((tr, sum(widths)), lambda i: (i, 0)),
                          out_shape=jax.ShapeDtypeStruct((rows, sum(widths)), BF16),
                          compiler_params=_params(("parallel",)))(*ins)


def _add_slots(rb, *, name):
    _, rows, cols = rb.shape
    tr = _row_tile(rows, cols, 5)

    def body(r0, r1, r2, r3, o_ref):
        o_ref[...] = ((r0[...].astype(F32) + r1[...].astype(F32)) + r2[...].astype(F32)) + r3[...].astype(F32)

    slot = lambda s: pl.BlockSpec((None, tr, cols), lambda i: (s, i, 0))
    return pl.pallas_call(body, name=name, grid=(rows // tr,), in_specs=[slot(s) for s in range(4)],
                          out_specs=pl.BlockSpec((tr, cols), lambda i: (i, 0)),
                          out_shape=jax.ShapeDtypeStruct((rows, cols), F32),
                          compiler_params=_params(("parallel",)))(rb, rb, rb, rb)


def _adamw(w, g, m, v, *, name):
    rows, cols = w.shape
    tr = _row_tile(rows, cols, 7) if rows % 16 == 0 else rows
    c1 = 1.0 - ADAM_B1 ** ADAM_STEP
    c2 = 1.0 - ADAM_B2 ** ADAM_STEP

    def body(w_ref, g_ref, m_ref, v_ref, d_ref, mo_ref, vo_ref):
        gv = g_ref[...]
        mn = ADAM_B1 * m_ref[...] + (1.0 - ADAM_B1) * gv
        vn = ADAM_B2 * v_ref[...] + (1.0 - ADAM_B2) * (gv * gv)
        mo_ref[...] = mn
        vo_ref[...] = vn
        d_ref[...] = -ADAM_LR * ((mn / c1) / (jnp.sqrt(vn / c2) + ADAM_EPS) + ADAM_WD * w_ref[...])

    spec = pl.BlockSpec((tr, cols), lambda i: (i, 0))
    shape = jax.ShapeDtypeStruct((rows, cols), F32)
    return pl.pallas_call(body, name=name, grid=(rows // tr,), in_specs=[spec] * 4, out_specs=[spec] * 3,
                          out_shape=[shape] * 3, compiler_params=_params(("parallel",)))(w, g, m, v)


W_SPECS = (
    ("w_in", 1024, IN_COLS, 1, IN_COLS // 4),
    ("w_mem_kv", 1024, 1024, 0, 256),
    ("w_proj_hg", 1024, 1024, 0, 256),
    ("w_proj_da", 512, 1024, 1, 256),
    ("w_proj_mem", 512, 1024, 1, 256),
    ("w_out", 1024, 1024, 0, 256),
    ("w_ffn_in", 1024, 2 * D_FF, 1, 2 * D_FF // 4),
    ("w_ffn_out", D_FF, 1024, 0, D_FF // 4),
)
CHIP_FLIPS = ((1, 0), (0, 1), (1, 1))
ANY = pl.BlockSpec(memory_space=pl.ANY)
DMA_CHUNK_BYTES = 1 << 20
STAGE_BYTES = 2 << 20


def _place():
    x, y, c = lax.axis_index("x"), lax.axis_index("y"), lax.axis_index("c")
    return x, y, c, 2 * x + y


def _flip(v, f):
    return 1 - v if f else v


def _slab(ref, axis, idx, size):
    start = pl.multiple_of(idx * size, size)
    return ref.at[pl.ds(start, size), :] if axis == 0 else ref.at[:, pl.ds(start, size)]


def _chunked(make, src, dst):
    rows, cols = src.shape
    row_bytes = cols * jnp.dtype(src.dtype).itemsize
    k = 1
    while rows % (2 * k) == 0 and (rows // (2 * k)) % 16 == 0 and (rows // k) * row_bytes > DMA_CHUNK_BYTES:
        k *= 2
    cr = rows // k
    parts = [make(src.at[pl.ds(j * cr, cr), :], dst.at[pl.ds(j * cr, cr), :]) for j in range(k)]
    return parts, make(src, dst)


def _run_copies(local, remote):
    for parts, _ in local + remote:
        for cp in parts:
            cp.start()
    for _, whole in remote:
        whole.wait_recv()
    for _, whole in remote:
        whole.wait_send()
    for _, whole in local:
        whole.wait()


def _half_spec(rows, cols, axis):
    return (0, rows // 2) if axis == 1 else (1, cols // 2)


def _staged(src, remote_dst, local_dst, sibling, load_sems, send_sems, store_sems, recv_sem):
    rows, cols = src.shape
    row_bytes = cols * jnp.dtype(src.dtype).itemsize
    k = 1
    while (rows // k) * row_bytes > STAGE_BYTES and rows % (2 * k) == 0 and (rows // (2 * k)) % 16 == 0:
        k *= 2
    cr = rows // k
    piece = lambda ref, j: ref.at[pl.ds(j * cr, cr), :]

    def run(buf):
        loads = [pltpu.make_async_copy(piece(src, j), buf.at[j % 2], load_sems.at[j % 2]) for j in range(k)]
        outs = [[pltpu.make_async_remote_copy(src_ref=buf.at[j % 2], dst_ref=piece(remote_dst, j),
                                              send_sem=send_sems.at[j % 2], recv_sem=recv_sem,
                                              device_id=sibling, device_id_type=MESH)] for j in range(k)]
        if local_dst is not None:
            for j in range(k):
                outs[j].append(pltpu.make_async_copy(buf.at[j % 2], piece(local_dst, j), store_sems.at[j % 2]))

        def drained(j):
            outs[j][0].wait_send()
            for cp in outs[j][1:]:
                cp.wait()

        loads[0].start()
        for j in range(k):
            loads[j].wait()
            for cp in outs[j]:
                cp.start()
            if j + 1 < k:
                if j >= 1:
                    drained(j - 1)
                loads[j + 1].start()
        for j in range(max(0, k - 2), k):
            drained(j)

    pl.run_scoped(run, pltpu.VMEM((2, cr, cols), src.dtype))


def _landed(ref, recv_sem, send_sem):
    pltpu.make_async_remote_copy(src_ref=ref, dst_ref=ref, send_sem=send_sem, recv_sem=recv_sem,
                                 device_id=(lax.axis_index("x"), lax.axis_index("y"), lax.axis_index("c")),
                                 device_id_type=MESH).wait_recv()


def _gather_weights(fulls):
    n = len(W_SPECS)

    def body(*refs):
        outs = refs[n:2 * n]
        ici_send, ici_recv, load_sems, d2d_send, d2d_recv = refs[2 * n:]
        x, y, c, p = _place()

        def half_slab(wi, chip, half):
            _, rows, cols, axis, size = W_SPECS[wi]
            haxis, hsize = _half_spec(rows, cols, axis)
            return _slab(_slab(outs[wi], axis, chip, size), haxis, half, hsize)

        sent = []
        for wi in range(n):
            mine = half_slab(wi, p, c)
            for k, (fx, fy) in enumerate(CHIP_FLIPS):
                sent.append(_chunked(lambda s, d, j=3 * wi + k, fx=fx, fy=fy: pltpu.make_async_remote_copy(
                    src_ref=s, dst_ref=d, send_sem=ici_send.at[j], recv_sem=ici_recv.at[j],
                    device_id=(_flip(x, fx), _flip(y, fy), c), device_id_type=MESH), mine, mine))
        for parts, _ in sent:
            for cp in parts:
                cp.start()
        for k, (fx, fy) in enumerate(CHIP_FLIPS):
            q = 2 * _flip(x, fx) + _flip(y, fy)
            for wi in range(n):
                sent[3 * wi + k][1].wait_recv()
                got = half_slab(wi, q, c)
                _staged(got, got, None, (x, y, 1 - c), load_sems, d2d_send, None, d2d_recv.at[3 * wi + k])
        for _, whole in sent:
            whole.wait_send()
        for k, (fx, fy) in enumerate(CHIP_FLIPS):
            q = 2 * _flip(x, fx) + _flip(y, fy)
            for wi in range(n):
                _landed(half_slab(wi, q, 1 - c), d2d_recv.at[3 * wi + k], d2d_send.at[0])

    return pl.pallas_call(
        body, name="gather_weights", in_specs=[ANY] * n, out_specs=[ANY] * n,
        out_shape=[jax.ShapeDtypeStruct(f.shape, f.dtype) for f in fulls],
        input_output_aliases={i: i for i in range(n)},
        scratch_shapes=[pltpu.SemaphoreType.DMA((3 * n,)), pltpu.SemaphoreType.DMA((3 * n,)),
                        pltpu.SemaphoreType.DMA((2,)), pltpu.SemaphoreType.DMA((2,)),
                        pltpu.SemaphoreType.DMA((3 * n,))],
    )(*fulls)


def _sibling_exchange(grads):
    n = len(grads)

    def body(*refs):
        ins, outs = refs[:n], refs[n:2 * n]
        load_sems, send_sems, recv_sems = refs[2 * n:]
        x, y, c, _ = _place()
        for i, (_, haxis, hsize) in enumerate(grads):
            _staged(_slab(ins[i], haxis, 1 - c, hsize), outs[i], None, (x, y, 1 - c),
                    load_sems, send_sems, None, recv_sems.at[i])
        for i in range(n):
            _landed(outs[i], recv_sems.at[i], send_sems.at[0])

    shapes = [jax.ShapeDtypeStruct((hsize, g.shape[1]) if haxis == 0 else (g.shape[0], hsize), F32)
              for g, haxis, hsize in grads]
    return pl.pallas_call(
        body, name="grad_sibling_exchange", in_specs=[ANY] * n, out_specs=[ANY] * n, out_shape=shapes,
        scratch_shapes=[pltpu.SemaphoreType.DMA((2,)), pltpu.SemaphoreType.DMA((2,)),
                        pltpu.SemaphoreType.DMA((n,))],
    )(*[g for g, _, _ in grads])


def _chip_exchange(parts):
    n = len(parts)

    def body(*refs):
        ins, outs = refs[:n], refs[n:2 * n]
        send_sems, recv_sems, local_sems = refs[2 * n:]
        x, y, c, p = _place()
        local, remote = [], []
        for i, (_, axis, size) in enumerate(parts):
            local.append(_chunked(lambda s, d, i=i: pltpu.make_async_copy(s, d, local_sems.at[i]),
                                  _slab(ins[i], axis, p, size), outs[i].at[p]))
            for k, (fx, fy) in enumerate(CHIP_FLIPS):
                px, py = _flip(x, fx), _flip(y, fy)
                remote.append(_chunked(lambda s, d, j=3 * i + k, px=px, py=py: pltpu.make_async_remote_copy(
                    src_ref=s, dst_ref=d, send_sem=send_sems.at[j], recv_sem=recv_sems.at[j],
                    device_id=(px, py, c), device_id_type=MESH), _slab(ins[i], axis, 2 * px + py, size), outs[i].at[p]))
        _run_copies(local, remote)

    shapes = []
    for a, axis, size in parts:
        shapes.append(jax.ShapeDtypeStruct((4, size, a.shape[1]) if axis == 0 else (4, a.shape[0], size), a.dtype))
    return pl.pallas_call(
        body, name="grad_chip_exchange", in_specs=[ANY] * n, out_specs=[ANY] * n, out_shape=shapes,
        scratch_shapes=[pltpu.SemaphoreType.DMA((3 * n,)), pltpu.SemaphoreType.DMA((3 * n,)),
                        pltpu.SemaphoreType.DMA((n,))],
    )(*[a for a, _, _ in parts])


def _sibling_share(sums):
    n = len(sums)

    def body(*refs):
        ins, outs = refs[:n], refs[n:2 * n]
        load_sems, send_sems, store_sems, recv_sems = refs[2 * n:]
        x, y, c, _ = _place()
        for i, (s, haxis) in enumerate(sums):
            place = _slab(outs[i], haxis, c, s.shape[haxis])
            _staged(ins[i], place, place, (x, y, 1 - c), load_sems, send_sems, store_sems, recv_sems.at[i])
        for i, (s, haxis) in enumerate(sums):
            _landed(_slab(outs[i], haxis, 1 - c, s.shape[haxis]), recv_sems.at[i], send_sems.at[0])

    shapes = []
    for s, haxis in sums:
        r, cc = s.shape
        shapes.append(jax.ShapeDtypeStruct((2 * r, cc) if haxis == 0 else (r, 2 * cc), F32))
    return pl.pallas_call(
        body, name="grad_sibling_share", in_specs=[ANY] * n, out_specs=[ANY] * n, out_shape=shapes,
        scratch_shapes=[pltpu.SemaphoreType.DMA((2,)), pltpu.SemaphoreType.DMA((2,)),
                        pltpu.SemaphoreType.DMA((2,)), pltpu.SemaphoreType.DMA((n,))],
    )(*[s for s, _ in sums])


def _reduce_scatter(gw):
    grads = []
    for name, _, _, axis, _ in W_SPECS:
        for g in ((gw["w_ffn_a"], gw["w_ffn_b"]) if name == "w_ffn_in" else (gw[name],)):
            grads.append((g,) + _half_spec(g.shape[0], g.shape[1], axis))
    theirs = _sibling_exchange(grads)
    parts, j = [], 0
    for name, _, _, axis, size in W_SPECS:
        take = 2 if name == "w_ffn_in" else 1
        items = [grads[i] + (theirs[i],) for i in range(j, j + take)]
        parts.append((_add_halves(items, name=f"half_sum_{name}"), axis, size))
        j += take
    slots = _chip_exchange(parts)
    sums = []
    for (name, rows, cols, axis, _), rb in zip(W_SPECS, slots):
        sums.append((_add_slots(rb, name=f"chip_sum_{name}"), _half_spec(rows, cols, axis)[0]))
    return dict(zip([s[0] for s in W_SPECS], _sibling_share(sums)))


def _small_allreduce(sv):
    def body(sv_ref, o_ref, slots_ref, send_sems, recv_sems):
        x, y, c, _ = _place()
        me = 4 * x + 2 * y + c
        slots_ref[me] = sv_ref[...]
        copies = []
        for k in range(1, 8):
            fx, fy, fc = (k >> 2) & 1, (k >> 1) & 1, k & 1
            copies.append(pltpu.make_async_remote_copy(
                src_ref=sv_ref, dst_ref=slots_ref.at[me], send_sem=send_sems.at[k - 1],
                recv_sem=recv_sems.at[k - 1], device_id=(_flip(x, fx), _flip(y, fy), _flip(c, fc)),
                device_id_type=MESH))
        for cp in copies:
            cp.start()
        for cp in copies:
            cp.wait_recv()
        for cp in copies:
            cp.wait_send()
        total = slots_ref[0]
        for s in range(1, 8):
            total = total + slots_ref[s]
        o_ref[...] = total

    vm = pl.BlockSpec(memory_space=pltpu.VMEM)
    return pl.pallas_call(
        body, name="small_allreduce", in_specs=[vm], out_specs=vm,
        out_shape=jax.ShapeDtypeStruct(sv.shape, F32),
        scratch_shapes=[pltpu.VMEM((8,) + sv.shape, F32), pltpu.SemaphoreType.DMA((7,)),
                        pltpu.SemaphoreType.DMA((7,))],
    )(sv)


SMALL_ROWS = (("norm_mix_gain", 0), ("norm_mem_gain", 1), ("norm_ffn_gain", 2))
SMALL_LB = (("lb_logits_fw", 3), ("lb_logits_bw", 5))
SMALL_HEAD = ("hg_norm_gain", "da_q_gain", "da_k_gain", "mem_q_gain", "mem_k_gain")


def _pack_small(d):
    last = jnp.concatenate([d[n] for n in SMALL_HEAD] + [jnp.zeros((1, D_MODEL - HEAD * len(SMALL_HEAD)), F32)], axis=1)
    return jnp.concatenate([d["norm_mix_gain"], d["norm_mem_gain"], d["norm_ffn_gain"],
                            d["lb_logits_fw"], d["lb_logits_bw"], last], axis=0)


def _unpack_small(a):
    out = {n: a[r:r + 1] for n, r in SMALL_ROWS}
    out.update({n: a[r:r + 2] for n, r in SMALL_LB})
    out.update({n: a[7:8, j * HEAD:(j + 1) * HEAD] for j, n in enumerate(SMALL_HEAD)})
    return out


PARAM_ORDER = ("norm_mix_gain", "norm_mem_gain", "w_in", "lb_logits_fw", "lb_logits_bw", "hg_norm_gain",
               "da_q_gain", "da_k_gain", "w_mem_kv", "mem_q_gain", "mem_k_gain", "w_proj_hg", "w_proj_da",
               "w_proj_mem", "w_out", "norm_ffn_gain", "w_ffn_in", "w_ffn_out")


def kernel(x, mem, norm_mix_gain, norm_mem_gain, w_in, lb_logits_fw, lb_logits_bw, hg_norm_gain, da_q_gain, da_k_gain, w_mem_kv, mem_q_gain, mem_k_gain, w_proj_hg, w_proj_da, w_proj_mem, w_out, norm_ffn_gain, w_ffn_in, w_ffn_out, loss_target, m_norm_mix_gain, m_norm_mem_gain, m_w_in, m_lb_logits_fw, m_lb_logits_bw, m_hg_norm_gain, m_da_q_gain, m_da_k_gain, m_w_mem_kv, m_mem_q_gain, m_mem_k_gain, m_w_proj_hg, m_w_proj_da, m_w_proj_mem, m_w_out, m_norm_ffn_gain, m_w_ffn_in, m_w_ffn_out, v_norm_mix_gain, v_norm_mem_gain, v_w_in, v_lb_logits_fw, v_lb_logits_bw, v_hg_norm_gain, v_da_q_gain, v_da_k_gain, v_w_mem_kv, v_mem_q_gain, v_mem_k_gain, v_w_proj_hg, v_w_proj_da, v_w_proj_mem, v_w_out, v_norm_ffn_gain, v_w_ffn_in, v_w_ffn_out):
    args = dict(locals())
    mats = tuple(s[0] for s in W_SPECS)
    flat = lambda a: a.reshape(a.shape[-2:])
    w = {n: flat(args[n]) for n in mats}
    m = {n: flat(args["m_" + n]) for n in mats}
    v = {n: flat(args["v_" + n]) for n in mats}
    small = {n: args[n] for n in PARAM_ORDER if n not in mats}

    chip = (2 * lax.axis_index("x") + lax.axis_index("y")).astype(jnp.int32).reshape(1)
    full = _gather_weights([_cast_into_full(w[n], chip, rows, cols, axis, name=f"cast_{n}")
                            for n, rows, cols, axis, _ in W_SPECS])
    loss, grad_x, gw, small_grads = _local_step(x[0], mem[0], loss_target[0], small, dict(zip(mats, full)))
    grads = _reduce_scatter(gw)
    small_sum = _small_allreduce(small_grads)

    delta, new_m, new_v = {}, {}, {}
    for n in mats:
        delta[n], new_m[n], new_v[n] = _adamw(w[n], grads[n], m[n], v[n], name=f"adamw_{n}")
    packed = _adamw(_pack_small(small), small_sum,
                    _pack_small({n: args["m_" + n] for n in small}),
                    _pack_small({n: args["v_" + n] for n in small}), name="adamw_small")
    grads.update(_unpack_small(small_sum))
    for dst, src in zip((delta, new_m, new_v), packed):
        dst.update(_unpack_small(src))

    def shaped(d, n):
        return d[n].reshape(args[n].shape)

    loss_sum = lax.psum(loss[0, 0], ("x", "y", "c"))
    return (loss_sum, grad_x[None], *[shaped(grads, n) for n in PARAM_ORDER], *[shaped(delta, n) for n in PARAM_ORDER],
            *[shaped(new_m, n) for n in PARAM_ORDER], *[shaped(new_v, n) for n in PARAM_ORDER])
```

```python
import functools
import math

import numpy as np
import jax
import jax.numpy as jnp
from jax import lax
from jax.experimental import pallas as pl
from jax.experimental.pallas import tpu as pltpu

F32 = jnp.float32
BF16 = jnp.bfloat16
MESH = pl.DeviceIdType.MESH

D_MODEL = 1024
HEAD = 128
HG_HEADS = 8
DA_GROUPS = ((1, 64), (4, 64), (16, 64))
DA_GH = 4
MEM_HEADS = 4
N_MEM = 256
D_FF = 2816
CHUNK = 64
BAND_QBLOCKS = 4
GLA_HEADS_PER_STEP = 2
RMS_EPS = 1e-6
NEG_INF = -1e30
HG_SCALE = HEAD ** -0.5
ATT_SCALE = HEAD ** -0.5
VMEM_LIMIT_V7X = 48 * 1024 * 1024

C_HQ, C_FF, C_FB, C_HI, C_HG = 0, 1024, 2048, 3072, 4096
C_DQ, C_DK, C_DV, C_MQ = 5120, 6656, 8192, 9728
C_GHG, C_GDA, C_GMEM = 10240, 11264, 12288
IN_COLS = 13312

ADAM_LR, ADAM_B1, ADAM_B2, ADAM_EPS, ADAM_WD, ADAM_STEP = 0.001, 0.9, 0.999, 1e-08, 0.01, 10


def _params(sem, vmem=VMEM_LIMIT_V7X):
    return pltpu.CompilerParams(dimension_semantics=sem, vmem_limit_bytes=vmem)


def _dot(a, b):
    return jnp.dot(a.astype(BF16), b.astype(BF16), preferred_element_type=F32)


def _dot_nt(a, b):
    return lax.dot_general(a.astype(BF16), b.astype(BF16), (((1,), (1,)), ((), ())),
                           preferred_element_type=F32)


def _dot_tn(a, b):
    return lax.dot_general(a.astype(BF16), b.astype(BF16), (((0,), (0,)), ((), ())),
                           preferred_element_type=F32)


def _sigmoid(v):
    return jax.nn.sigmoid(v.astype(F32))


def _ones(rows, cols):
    return (lax.broadcasted_iota(jnp.int32, (rows, cols), 0) >= 0).astype(BF16)


def _lane_sum(v):
    ones = _ones(HEAD, HEAD)
    hi = v.astype(BF16)
    mid = (v - hi.astype(F32)).astype(BF16)
    return jnp.dot(hi, ones, preferred_element_type=F32) + jnp.dot(mid, ones, preferred_element_type=F32)


def _row_mean(v):
    if v.shape[-1] == HEAD:
        return _lane_sum(v) * (1.0 / HEAD)
    return jnp.mean(v, axis=-1, keepdims=True)


def _rms(v):
    v = v.astype(F32)
    r = lax.rsqrt(_row_mean(v * v) + RMS_EPS)
    return v * r, r


def _rms_bwd(dy, xhat, r, gain):
    dxh = dy * gain
    dx = r * (dxh - xhat * _row_mean(dxh * xhat))
    return dx, dy * xhat


class _Carry:
    def __init__(self, arrays, out_shapes, aliases, sems, issue):
        self.arrays, self.out_shapes, self.aliases, self.sems, self.issue = arrays, out_shapes, aliases, sems, issue


NO_CARRY = object()


def _start_copies(copies):
    for parts, _, _ in copies:
        for cp in parts:
            cp.start()


def _wait_copies(copies):
    for _, whole, remote in copies:
        if remote:
            whole.wait_recv()
    for _, whole, remote in copies:
        if remote:
            whole.wait_send()
        else:
            whole.wait()


def _call(body, *, name, grid, in_specs, out_specs, out_shape, scratch_shapes, semantics, ins, carry=None):
    if carry is None:
        res = pl.pallas_call(body, name=name, grid=grid, in_specs=in_specs, out_specs=out_specs, out_shape=out_shape,
                             scratch_shapes=scratch_shapes, compiler_params=_params(semantics))(*ins)
        return list(res), []
    n_in, n_out, n_scr = len(ins), len(out_shape), len(scratch_shapes)
    c_in, c_out = len(carry.arrays), len(carry.out_shapes)

    def wrapped(*refs):
        pos = [0]

        def take(count):
            pos[0] += count
            return refs[pos[0] - count:pos[0]]

        own_in, carry_in = take(n_in), take(c_in)
        own_out, carry_out = take(n_out), take(c_out)
        own_scr, carry_sems = take(n_scr), take(len(carry.sems))
        ids = [pl.program_id(a) for a in range(len(grid))]
        first, last = ids[0] == 0, ids[0] == grid[0] - 1
        for a in range(1, len(grid)):
            first, last = first & (ids[a] == 0), last & (ids[a] == grid[a] - 1)

        @pl.when(first)
        def _():
            _start_copies(carry.issue(carry_in, carry_out, carry_sems, "start"))

        body(*own_in, *own_out, *own_scr)

        @pl.when(last)
        def _():
            _wait_copies(carry.issue(carry_in, carry_out, carry_sems, "wait"))

    res = pl.pallas_call(
        wrapped, name=name, grid=grid, in_specs=list(in_specs) + [ANY] * c_in,
        out_specs=list(out_specs) + [ANY] * c_out, out_shape=list(out_shape) + list(carry.out_shapes),
        input_output_aliases={n_in + i: n_out + j for i, j in carry.aliases.items()},
        scratch_shapes=list(scratch_shapes) + list(carry.sems),
        compiler_params=_params(("arbitrary",) * len(grid)))(*ins, *carry.arrays)
    return list(res[:n_out]), list(res[n_out:])


def _matmul(pairs, mode, out_dtype, *, tm, tn, tk=None, residual=None, carry=NO_CARRY, name):
    b_offs = [pr[2] if len(pr) > 2 else 0 for pr in pairs]
    pairs = [pr[:2] for pr in pairs]
    a0, b0 = pairs[0]
    if mode == "nn":
        (m, kk), n = a0.shape, b0.shape[1]
    elif mode == "nt":
        (m, kk), n = a0.shape, b0.shape[0]
    else:
        (kk, m), n = a0.shape, b0.shape[1]
    assert mode == "nt" or not any(b_offs)
    tm, tn = min(tm, m), min(tn, n)
    tk = kk if tk is None else tk
    nk = kk // tk
    assert m % tm == 0 and n % tn == 0 and kk % tk == 0, (name, m, n, kk)
    n_p = len(pairs)
    if mode == "tn":
        a_spec = pl.BlockSpec((tk, tm), lambda i, j, k: (k, i))
    else:
        a_spec = pl.BlockSpec((tm, tk), lambda i, j, k: (i, k))
    if mode == "nt":
        b_specs = [pl.BlockSpec((tn, tk), lambda i, j, k, o=o: (j, o * nk + k)) for o in b_offs]
    else:
        b_specs = [pl.BlockSpec((tk, tn), lambda i, j, k: (k, j))] * n_p
    o_spec = pl.BlockSpec((tm, tn), lambda i, j, k: (i, j))
    dot = {"nn": _dot, "nt": _dot_nt, "tn": _dot_tn}[mode]
    has_res = residual is not None

    def body(*refs):
        a_refs, b_refs = refs[:n_p], refs[n_p:2 * n_p]
        pos = 2 * n_p
        res_ref = refs[pos] if has_res else None
        pos += int(has_res)
        o_ref = refs[pos]
        part = dot(a_refs[0][...], b_refs[0][...])
        for a_r, b_r in zip(a_refs[1:], b_refs[1:]):
            part += dot(a_r[...], b_r[...])

        def finish(total):
            if has_res:
                total = total + res_ref[...]
            o_ref[...] = total.astype(out_dtype)

        if nk == 1:
            finish(part)
        else:
            acc_ref = refs[pos + 1]
            k = pl.program_id(2)

            @pl.when(k == 0)
            def _():
                acc_ref[...] = part

            @pl.when(k > 0)
            def _():
                acc_ref[...] += part

            @pl.when(k == nk - 1)
            def _():
                finish(acc_ref[...])

    ins = [a for a, _ in pairs] + [b for _, b in pairs]
    in_specs = [a_spec] * n_p + b_specs
    if has_res:
        ins.append(residual)
        in_specs.append(o_spec)
    (out,), carried = _call(
        body, name=name, grid=(m // tm, n // tn, nk), in_specs=in_specs, out_specs=[o_spec],
        out_shape=[jax.ShapeDtypeStruct((m, n), out_dtype)],
        scratch_shapes=[pltpu.VMEM((tm, tn), F32)] if nk > 1 else [],
        semantics=("parallel", "parallel", "arbitrary"), ins=ins, carry=None if carry is NO_CARRY else carry)
    return out if carry is NO_CARRY else (out, carried)


def _rms_fwd(x, gain, *, name):
    rows, dm = x.shape
    tm = min(512, rows)

    def body(x_ref, g_ref, h_ref):
        xhat, _ = _rms(x_ref[...])
        h_ref[...] = (xhat * g_ref[...]).astype(BF16)

    return pl.pallas_call(
        body, name=name, grid=(rows // tm,),
        in_specs=[pl.BlockSpec((tm, dm), lambda i: (i, 0)), pl.BlockSpec((1, dm), lambda i: (0, 0))],
        out_specs=pl.BlockSpec((tm, dm), lambda i: (i, 0)),
        out_shape=jax.ShapeDtypeStruct((rows, dm), BF16),
        compiler_params=_params(("parallel",)),
    )(x, gain)


def _rms_bwd_rows(dh, x, gain, dres, *, name):
    rows, dm = x.shape
    tm = min(512, rows)
    has_res = dres is not None

    def body(*refs):
        dh_ref, x_ref, g_ref = refs[:3]
        res_ref = refs[3] if has_res else None
        dx_ref, dxb_ref, dg_ref = refs[3 + int(has_res):]
        xhat, r = _rms(x_ref[...])
        dx, dgr = _rms_bwd(dh_ref[...], xhat, r, g_ref[...])
        if has_res:
            dx = dx + res_ref[...]
        dx_ref[...] = dx
        dxb_ref[...] = dx.astype(BF16)

        @pl.when(pl.program_id(0) == 0)
        def _():
            dg_ref[...] = jnp.zeros_like(dg_ref)

        dg_ref[...] += jnp.sum(dgr, axis=0, keepdims=True)

    row = pl.BlockSpec((tm, dm), lambda i: (i, 0))
    vec = pl.BlockSpec((1, dm), lambda i: (0, 0))
    return pl.pallas_call(
        body, name=name, grid=(rows // tm,),
        in_specs=[row, row, vec] + ([row] if has_res else []),
        out_specs=[row, row, vec],
        out_shape=[jax.ShapeDtypeStruct((rows, dm), F32), jax.ShapeDtypeStruct((rows, dm), BF16),
                   jax.ShapeDtypeStruct((1, dm), F32)],
        compiler_params=_params(("arbitrary",)),
    )(*([dh, x, gain] + ([dres] if has_res else [])))


def _loss_grad(y, tgt, *, name):
    rows, dm = y.shape
    tm = min(512, rows)
    steps = rows // tm

    def body(y_ref, t_ref, dy_ref, dyb_ref, loss_ref, acc_ref):
        i = pl.program_id(0)
        diff = y_ref[...] - t_ref[...]
        dy = diff * (1.0 / dm)
        dy_ref[...] = dy
        dyb_ref[...] = dy.astype(BF16)

        @pl.when(i == 0)
        def _():
            acc_ref[...] = jnp.zeros_like(acc_ref)

        acc_ref[...] += jnp.sum(diff * diff, axis=0, keepdims=True)

        @pl.when(i == steps - 1)
        def _():
            loss_ref[...] = jnp.full((1, HEAD), 0.5 / dm, F32) * jnp.sum(acc_ref[...])

    row = pl.BlockSpec((tm, dm), lambda i: (i, 0))
    return pl.pallas_call(
        body, name=name, grid=(steps,),
        in_specs=[row, row],
        out_specs=[row, row, pl.BlockSpec((1, HEAD), lambda i: (0, 0))],
        out_shape=[jax.ShapeDtypeStruct((rows, dm), F32), jax.ShapeDtypeStruct((rows, dm), BF16),
                   jax.ShapeDtypeStruct((1, HEAD), F32)],
        scratch_shapes=[pltpu.VMEM((1, dm), F32)],
        compiler_params=_params(("arbitrary",)),
    )(y, tgt)


def _gla_block_terms(q_raw, f_logit, lb, rev):
    sig = _sigmoid(f_logit)
    forget = lb + (1.0 - lb) * sig
    k = 1.0 - forget
    b = _chunk_cumsum(jnp.log(forget), rev)
    qs = _sigmoid(q_raw)
    eb = jnp.exp(b)
    emb = jnp.exp(-b)
    qt = (q_raw * qs * HG_SCALE) * eb
    kt = k * emb
    return sig, forget, k, b, qs, eb, emb, qt, kt


def _chunk_cumsum(v, rev):
    n = v.shape[0]
    pos = lax.broadcasted_iota(jnp.int32, v.shape, 0) & (CHUNK - 1)
    step = 1
    while step < CHUNK:
        if rev:
            shifted, keep = pltpu.roll(v, n - step, 0), pos < CHUNK - step
        else:
            shifted, keep = pltpu.roll(v, step, 0), pos >= step
        v = v + jnp.where(keep, shifted, 0.0)
        step *= 2
    return v


def _tri_mask(n, rev):
    row = lax.broadcasted_iota(jnp.int32, (n, n), 0)
    col = lax.broadcasted_iota(jnp.int32, (n, n), 1)
    shift = CHUNK.bit_length() - 1
    same = jnp.right_shift(row, shift) == jnp.right_shift(col, shift)
    return same & ((row <= col) if rev else (row >= col))


def _chunk_order(ncb, rev):
    order = range(ncb - 1, -1, -1) if rev else range(ncb)
    return [(c, c * CHUNK if rev else c * CHUNK + CHUNK - 1) for c in order]


def _gla_fwd(proj, lb, *, f_off, rev, name):
    rows = proj.shape[0]
    tb = min(256, rows)
    nb, ncb = rows // tb, tb // CHUNK

    def tmap(n):
        return nb - 1 - n if rev else n

    def body(q_ref, f_ref, v_ref, lb_ref, o_ref, st_ref, s_ref):
        @pl.when(pl.program_id(1) == 0)
        def _():
            s_ref[...] = jnp.zeros_like(s_ref)

        tri = _tri_mask(tb, rev)
        for hh in range(GLA_HEADS_PER_STEP):
            cs = slice(hh * HEAD, (hh + 1) * HEAD)
            v = v_ref[:, cs]
            _, _, k, b, _, _, _, qt, kt = _gla_block_terms(q_ref[:, cs].astype(F32), f_ref[:, cs].astype(F32),
                                                           lb_ref[:, cs], rev)
            o_intra = _dot(jnp.where(tri, _dot_nt(qt, kt), 0.0), v)
            chunks = []
            for c, last in _chunk_order(ncb, rev):
                sl = slice(c * CHUNK, (c + 1) * CHUNK)
                bl = b[last:last + 1, :]
                kh = k[sl] * jnp.exp(bl - b[sl])
                chunks.append((c, sl, jnp.exp(bl), _dot_tn(v[sl], kh)))
            s_t = s_ref[hh]
            for c, sl, ebl, kv in chunks:
                st_ref[hh, c] = s_t
                o_ref[sl, cs] = o_intra[sl] + _dot_nt(qt[sl], s_t)
                s_t = ebl * s_t + kv
            s_ref[hh] = s_t

    hps = GLA_HEADS_PER_STEP
    col = lambda off: pl.BlockSpec((tb, hps * HEAD), lambda h, n: (tmap(n), off // (hps * HEAD) + h))
    return pl.pallas_call(
        body, name=name, grid=(HG_HEADS // hps, nb),
        in_specs=[col(C_HQ), col(f_off), col(C_HI), pl.BlockSpec((1, hps * HEAD), lambda h, n: (0, h))],
        out_specs=[pl.BlockSpec((tb, hps * HEAD), lambda h, n: (tmap(n), h)),
                   pl.BlockSpec((hps, ncb, HEAD, HEAD), lambda h, n: (h, tmap(n), 0, 0))],
        out_shape=[jax.ShapeDtypeStruct((rows, HG_HEADS * HEAD), F32),
                   jax.ShapeDtypeStruct((HG_HEADS, rows // CHUNK, HEAD, HEAD), F32)],
        scratch_shapes=[pltpu.VMEM((hps, HEAD, HEAD), F32)],
        compiler_params=_params(("parallel", "arbitrary")),
    )(proj, proj, proj, lb)


def _gla_bwd(proj, lb, do, states, prev, *, f_off, rev, carry=NO_CARRY, name):
    rows = proj.shape[0]
    tb = min(256, rows)
    nb, ncb = rows // tb, tb // CHUNK
    has_prev = prev is not None
    qv_dtype = BF16 if has_prev else F32

    def tmap(n):
        return n if rev else nb - 1 - n

    def body(*refs):
        q_ref, f_ref, v_ref, lb_ref, do_ref, st_ref = refs[:6]
        pq_ref, pv_ref = refs[6:8] if has_prev else (None, None)
        (dq_ref, df_ref, dv_ref, dlb_ref,
         ds_ref, dqt_scr, dk_scr, db_scr, dbl_scr, dv_scr) = refs[6 + 2 * int(has_prev):]

        @pl.when(pl.program_id(1) == 0)
        def _():
            ds_ref[...] = jnp.zeros_like(ds_ref)
            dlb_ref[...] = jnp.zeros_like(dlb_ref)

        tri = _tri_mask(tb, rev)
        for hh in range(GLA_HEADS_PER_STEP):
            cs = slice(hh * HEAD, (hh + 1) * HEAD)
            lbv = lb_ref[:, cs]
            q_raw = q_ref[:, cs].astype(F32)
            v, dout = v_ref[:, cs], do_ref[:, cs].astype(BF16)
            sig, forget, k, b, qs, eb, emb, qt, kt = _gla_block_terms(q_raw, f_ref[:, cs].astype(F32), lbv, rev)
            a = jnp.where(tri, _dot_nt(qt, kt), 0.0)
            da = jnp.where(tri, _dot_nt(dout, v), 0.0)
            dv_intra = _dot_tn(a, dout)
            dqt_intra = _dot(da, kt)
            dkt = _dot_tn(da, qt)
            chunks = []
            for c, last in reversed(_chunk_order(ncb, rev)):
                sl = slice(c * CHUNK, (c + 1) * CHUNK)
                bl = b[last:last + 1, :]
                e = jnp.exp(bl - b[sl])
                s_t = st_ref[hh, c]
                dqt_scr[sl, cs] = dqt_intra[sl] + _dot(dout[sl], s_t)
                chunks.append((sl, jnp.exp(bl), e, k[sl] * e, s_t, _dot_tn(dout[sl], qt[sl])))
            ds_t = ds_ref[hh]
            for sl, ebl, e, kh, s_t, grow in chunks:
                dkh = _dot(v[sl], ds_t)
                dv_scr[sl, cs] = dv_intra[sl] + _dot_nt(kh, ds_t)
                dk_scr[sl, cs] = dkt[sl] * emb[sl] + dkh * e
                khd = kh * dkh
                dbl = jnp.sum(khd, axis=0, keepdims=True) + ebl * jnp.sum(ds_t * s_t, axis=0, keepdims=True)
                db_scr[sl, cs] = khd
                dbl_scr[sl, cs] = jnp.broadcast_to(dbl, (CHUNK, HEAD))
                ds_t = grow + ds_t * ebl
            ds_ref[hh] = ds_t
            dqt = dqt_scr[:, cs]
            dlogf = _chunk_cumsum(qt * dqt - kt * dkt - db_scr[:, cs], not rev) + dbl_scr[:, cs]
            dforget = dlogf / forget - dk_scr[:, cs]
            df_ref[:, cs] = (dforget * (1.0 - lbv) * sig * (1.0 - sig)).astype(BF16)
            dlb_ref[:, cs] += jnp.sum(dforget * (1.0 - sig), axis=0, keepdims=True)
            dqr = dqt * eb * (HG_SCALE * qs * (1.0 + q_raw * (1.0 - qs)))
            dv = dv_scr[:, cs]
            if has_prev:
                dqr = dqr + pq_ref[:, cs]
                dv = dv + pv_ref[:, cs]
            dq_ref[:, cs] = dqr.astype(qv_dtype)
            dv_ref[:, cs] = dv.astype(qv_dtype)

    hps = GLA_HEADS_PER_STEP
    col = lambda off: pl.BlockSpec((tb, hps * HEAD), lambda h, n: (tmap(n), off // (hps * HEAD) + h))
    blk = pl.BlockSpec((tb, hps * HEAD), lambda h, n: (tmap(n), h))
    vec = pl.BlockSpec((1, hps * HEAD), lambda h, n: (0, h))
    wide = HG_HEADS * HEAD
    outs, carried = _call(
        body, name=name, grid=(HG_HEADS // hps, nb),
        in_specs=[col(C_HQ), col(f_off), col(C_HI), vec, blk,
                  pl.BlockSpec((hps, ncb, HEAD, HEAD), lambda h, n: (h, tmap(n), 0, 0))]
                 + ([blk, blk] if has_prev else []),
        out_specs=[blk, blk, blk, vec],
        out_shape=[jax.ShapeDtypeStruct((rows, wide), qv_dtype), jax.ShapeDtypeStruct((rows, wide), BF16),
                   jax.ShapeDtypeStruct((rows, wide), qv_dtype), jax.ShapeDtypeStruct((1, wide), F32)],
        scratch_shapes=[pltpu.VMEM((hps, HEAD, HEAD), F32)] + [pltpu.VMEM((tb, hps * HEAD), F32)] * 5,
        semantics=("parallel", "arbitrary"),
        ins=[proj, proj, proj, lb, do, states] + (list(prev) if has_prev else []),
        carry=None if carry is NO_CARRY else carry)
    return outs if carry is NO_CARRY else (outs, carried)


def _hg_out_fwd(o_fw, o_bw, proj, gain, *, name):
    rows = o_fw.shape[0]
    tm = min(512, rows)
    wide = HG_HEADS * HEAD

    def body(a_ref, b_ref, g_ref, gain_ref, o_ref):
        for h in range(HG_HEADS):
            sl = slice(h * HEAD, (h + 1) * HEAD)
            xhat, _ = _rms(a_ref[:, sl] + b_ref[:, sl])
            gate = g_ref[:, sl].astype(F32)
            o_ref[:, sl] = (xhat * gain_ref[...] * (gate * _sigmoid(gate))).astype(BF16)

    row = pl.BlockSpec((tm, wide), lambda i: (i, 0))
    return pl.pallas_call(
        body, name=name, grid=(rows // tm,),
        in_specs=[row, row, pl.BlockSpec((tm, wide), lambda i: (i, C_HG // wide)),
                  pl.BlockSpec((1, HEAD), lambda i: (0, 0))],
        out_specs=row, out_shape=jax.ShapeDtypeStruct((rows, wide), BF16),
        compiler_params=_params(("parallel",)),
    )(o_fw, o_bw, proj, gain)


def _hg_out_bwd(dout, o_fw, o_bw, proj, gain, *, name):
    rows = o_fw.shape[0]
    tm = min(512, rows)
    wide = HG_HEADS * HEAD

    def body(d_ref, a_ref, b_ref, g_ref, gain_ref, dgate_ref, do_ref, dgain_ref):
        @pl.when(pl.program_id(0) == 0)
        def _():
            dgain_ref[...] = jnp.zeros_like(dgain_ref)

        dgain = jnp.zeros((1, HEAD), F32)
        for h in range(HG_HEADS):
            sl = slice(h * HEAD, (h + 1) * HEAD)
            xhat, r = _rms(a_ref[:, sl] + b_ref[:, sl])
            gate, dy = g_ref[:, sl].astype(F32), d_ref[:, sl]
            gs = _sigmoid(gate)
            dgate_ref[:, sl] = (dy * xhat * gain_ref[...] * (gs * (1.0 + gate * (1.0 - gs)))).astype(BF16)
            dx, dgr = _rms_bwd(dy * (gate * gs), xhat, r, gain_ref[...])
            do_ref[:, sl] = dx
            dgain = dgain + jnp.sum(dgr, axis=0, keepdims=True)
        dgain_ref[...] += dgain

    row = pl.BlockSpec((tm, wide), lambda i: (i, 0))
    vec = pl.BlockSpec((1, HEAD), lambda i: (0, 0))
    return pl.pallas_call(
        body, name=name, grid=(rows // tm,),
        in_specs=[row, row, row, pl.BlockSpec((tm, wide), lambda i: (i, C_HG // wide)), vec],
        out_specs=[row, row, vec],
        out_shape=[jax.ShapeDtypeStruct((rows, wide), BF16), jax.ShapeDtypeStruct((rows, wide), F32),
                   jax.ShapeDtypeStruct((1, HEAD), F32)],
        compiler_params=_params(("arbitrary",)),
    )(dout, o_fw, o_bw, proj, gain)


def _strided_rows(ref, r, count, d):
    return ref[...] if d == 1 else ref[pl.ds(r, count, stride=d), :]


def _store_strided(ref, r, count, d, val):
    if d == 1:
        ref[...] = val
    else:
        ref[pl.ds(r, count, stride=d), :] = val


def _da_prep(proj, q_gain, k_gain, g, *, name):
    d = DA_GROUPS[g][0]
    rows = proj.shape[0]
    rb = min(512, rows)
    tn = rb // d

    def body(q_ref, k_ref, v_ref, qg_ref, kg_ref, qo_ref, ko_ref, vo_ref, qf_ref, kf_ref, vf_ref):
        for h in range(DA_GH):
            cs = slice(h * HEAD, (h + 1) * HEAD)
            for src, dst in ((q_ref, qf_ref), (k_ref, kf_ref), (v_ref, vf_ref)):
                dst[...] = src[:, cs].astype(F32)
            for r in range(d):
                qhat, _ = _rms(_strided_rows(qf_ref, r, tn, d))
                khat, _ = _rms(_strided_rows(kf_ref, r, tn, d))
                qo_ref[h, r] = (qhat * qg_ref[...]).astype(BF16)
                ko_ref[h, r] = (khat * kg_ref[...]).astype(BF16)
                vo_ref[h, r] = _strided_rows(vf_ref, r, tn, d).astype(BF16)

    wide = DA_GH * HEAD
    col = lambda off: pl.BlockSpec((rb, wide), lambda i: (i, off // wide + g))
    vec = pl.BlockSpec((1, HEAD), lambda i: (0, 0))
    out = pl.BlockSpec((DA_GH, d, tn, HEAD), lambda i: (0, 0, i, 0))
    shape = jax.ShapeDtypeStruct((DA_GH, d, rows // d, HEAD), BF16)
    return pl.pallas_call(
        body, name=name, grid=(rows // rb,),
        in_specs=[col(C_DQ), col(C_DK), col(C_DV), vec, vec],
        out_specs=[out, out, out], out_shape=[shape, shape, shape],
        scratch_shapes=[pltpu.VMEM((rb, HEAD), F32)] * 3,
        compiler_params=_params(("parallel",)),
    )(proj, proj, proj, q_gain, k_gain)


def _slopes(g):
    idx = np.arange(g * DA_GH + 1, (g + 1) * DA_GH + 1)
    s = (2.0 ** (-8.0 * idx / (DA_GH * len(DA_GROUPS)))).astype(np.float32)
    return jnp.asarray(np.broadcast_to(s[:, None, None], (DA_GH, 8, HEAD)).copy())


def _band_window(ld, t, radius):
    win = min(2 * t, ld)
    assert t // 2 >= radius or win == ld
    return win


def _band_scores(q, k, q0, start, slope, d, radius):
    t, win = q.shape[0], k.shape[0]
    row = lax.broadcasted_iota(jnp.int32, (t, win), 0)
    col = lax.broadcasted_iota(jnp.int32, (t, win), 1)
    rel = jnp.abs((start - q0) + col - row)
    return _dot_nt(q, k) * ATT_SCALE + jnp.where(rel <= radius, -slope * (d * rel).astype(F32), NEG_INF)


def _band_fwd(qr, kr, vr, g, *, name):
    d, radius = DA_GROUPS[g]
    _, _, ld, _ = qr.shape
    t = min(HEAD, ld)
    qb = min(BAND_QBLOCKS, ld // t)
    win = _band_window(ld, t, radius)

    def body(q_ref, k_ref, v_ref, sl_ref, o_ref, lse_ref):
        i = pl.program_id(2)
        slope = sl_ref[0:1, 0:1]
        for j in range(qb):
            sl = slice(j * t, (j + 1) * t)
            q0 = (i * qb + j) * t
            start = pl.multiple_of(jnp.clip(q0 - t // 2, 0, ld - win), t // 2)
            s = _band_scores(q_ref[sl, :], k_ref[pl.ds(start, win), :], q0, start, slope, d, radius)
            m = jnp.max(s, axis=-1, keepdims=True)
            p = jnp.exp(s - m).astype(BF16)
            l = jnp.dot(p, _ones(win, HEAD), preferred_element_type=F32)
            o_ref[sl, :] = _dot(p, v_ref[pl.ds(start, win), :]) / l
            lse_ref[sl, :] = m + jnp.log(l)

    own = pl.BlockSpec((None, None, qb * t, HEAD), lambda h, r, i: (h, r, i, 0))
    seq = pl.BlockSpec((None, None, ld, HEAD), lambda h, r, i: (h, r, 0, 0))
    shape = jax.ShapeDtypeStruct(qr.shape, F32)
    return pl.pallas_call(
        body, name=name, grid=(DA_GH, d, ld // (qb * t)),
        in_specs=[own, seq, seq, pl.BlockSpec((None, 8, HEAD), lambda h, r, i: (h, 0, 0))],
        out_specs=[own, own], out_shape=[shape, shape],
        compiler_params=_params(("parallel", "parallel", "parallel")),
    )(qr, kr, vr, _slopes(g))


def _band_bwd(qr, kr, vr, dor, lser, deltar, g, *, name):
    d, radius = DA_GROUPS[g]
    _, _, ld, _ = qr.shape
    t = min(HEAD, ld)
    qb = min(BAND_QBLOCKS, ld // t)
    win = _band_window(ld, t, radius)

    def body(q_ref, k_ref, v_ref, do_ref, lse_ref, dl_ref, sl_ref, dq_ref, dk_ref, dv_ref):
        i = pl.program_id(2)

        @pl.when(i == 0)
        def _():
            dk_ref[...] = jnp.zeros_like(dk_ref)
            dv_ref[...] = jnp.zeros_like(dv_ref)

        slope = sl_ref[0:1, 0:1]
        for j in range(qb):
            sl = slice(j * t, (j + 1) * t)
            q0 = (i * qb + j) * t
            start = pl.multiple_of(jnp.clip(q0 - t // 2, 0, ld - win), t // 2)
            rows = pl.ds(start, win)
            q, dout, k, v = q_ref[sl, :], do_ref[sl, :], k_ref[rows, :], v_ref[rows, :]
            p = jnp.exp(_band_scores(q, k, q0, start, slope, d, radius) - lse_ref[sl, 0:1])
            ds = p * (_dot_nt(dout, v) - dl_ref[sl, 0:1])
            dq_ref[sl, :] = _dot(ds, k) * ATT_SCALE
            dk_ref[rows, :] += _dot_tn(ds, q) * ATT_SCALE
            dv_ref[rows, :] += _dot_tn(p, dout)

    own = pl.BlockSpec((None, None, qb * t, HEAD), lambda h, r, i: (h, r, i, 0))
    seq = pl.BlockSpec((None, None, ld, HEAD), lambda h, r, i: (h, r, 0, 0))
    shape = jax.ShapeDtypeStruct(qr.shape, F32)
    return pl.pallas_call(
        body, name=name, grid=(DA_GH, d, ld // (qb * t)),
        in_specs=[own, seq, seq, own, own, own, pl.BlockSpec((None, 8, HEAD), lambda h, r, i: (h, 0, 0))],
        out_specs=[own, seq, seq], out_shape=[shape, shape, shape],
        compiler_params=_params(("parallel", "parallel", "arbitrary")),
    )(qr, kr, vr, dor, lser, deltar, _slopes(g))


def _da_merge(outs, lses, rows, *, name):
    rb = min(512, rows)
    wide = DA_GH * HEAD

    def body(*refs):
        o_refs, l_refs = refs[0:3], refs[3:6]
        o_ref, lse_ref = refs[6:8]
        on_refs, ln_refs = refs[8:11], refs[11:14]
        for h in range(DA_GH):
            cs = slice(h * HEAD, (h + 1) * HEAD)
            for g, (d, _) in enumerate(DA_GROUPS):
                tn = rb // d
                for r in range(d):
                    _store_strided(on_refs[g], r, tn, d, o_refs[g][h, r])
                    _store_strided(ln_refs[g], r, tn, d, l_refs[g][h, r])
            l0, l1, l2 = ln_refs[0][...], ln_refs[1][...], ln_refs[2][...]
            m = jnp.maximum(jnp.maximum(l0, l1), l2)
            e0, e1, e2 = jnp.exp(l0 - m), jnp.exp(l1 - m), jnp.exp(l2 - m)
            tot = e0 + e1 + e2
            o_ref[:, cs] = (e0 * on_refs[0][...] + e1 * on_refs[1][...] + e2 * on_refs[2][...]) / tot
            lse_ref[:, cs] = m + jnp.log(tot)

    res = lambda d: pl.BlockSpec((DA_GH, d, rb // d, HEAD), lambda i: (0, 0, i, 0))
    nat = pl.BlockSpec((rb, wide), lambda i: (i, 0))
    shape = jax.ShapeDtypeStruct((rows, wide), F32)
    return pl.pallas_call(
        body, name=name, grid=(rows // rb,),
        in_specs=[res(d) for d, _ in DA_GROUPS] * 2,
        out_specs=[nat, nat], out_shape=[shape, shape],
        scratch_shapes=[pltpu.VMEM((rb, HEAD), F32)] * 6,
        compiler_params=_params(("parallel",)),
    )(*outs, *lses)


def _da_bwd_prep(dout, o, lse, *, name):
    rows = o.shape[0]
    rb = min(512, rows)
    wide = DA_GH * HEAD

    def body(d_ref, o_ref, l_ref, *outs):
        d_scr, l_scr, delta_scr = outs[9:12]
        for h in range(DA_GH):
            cs = slice(h * HEAD, (h + 1) * HEAD)
            dv = d_ref[:, cs]
            d_scr[...] = dv
            l_scr[...] = l_ref[:, cs]
            delta_scr[...] = _lane_sum(dv * o_ref[:, cs])
            for g, (d, _) in enumerate(DA_GROUPS):
                tn = rb // d
                for r in range(d):
                    outs[3 * g][h, r] = _strided_rows(d_scr, r, tn, d).astype(BF16)
                    outs[3 * g + 1][h, r] = _strided_rows(l_scr, r, tn, d)
                    outs[3 * g + 2][h, r] = _strided_rows(delta_scr, r, tn, d)

    nat = pl.BlockSpec((rb, wide), lambda i: (i, 0))
    out_specs, out_shape = [], []
    for d, _ in DA_GROUPS:
        for dt in (BF16, F32, F32):
            out_specs.append(pl.BlockSpec((DA_GH, d, rb // d, HEAD), lambda i: (0, 0, i, 0)))
            out_shape.append(jax.ShapeDtypeStruct((DA_GH, d, rows // d, HEAD), dt))
    return pl.pallas_call(
        body, name=name, grid=(rows // rb,),
        in_specs=[nat, nat, nat], out_specs=out_specs, out_shape=out_shape,
        scratch_shapes=[pltpu.VMEM((rb, HEAD), F32)] * 3,
        compiler_params=_params(("parallel",)),
    )(dout, o, lse)


def _da_prep_bwd(dqr, dkr, dvr, proj, q_gain, k_gain, g, *, name):
    d = DA_GROUPS[g][0]
    rows = proj.shape[0]
    rb = min(512, rows)
    tn = rb // d
    wide = DA_GH * HEAD

    def body(dq_ref, dk_ref, dv_ref, q_ref, k_ref, qg_ref, kg_ref, oq_ref, ok_ref, ov_ref, gq_ref, gk_ref, *nat_refs):
        @pl.when(pl.program_id(0) == 0)
        def _():
            gq_ref[...] = jnp.zeros_like(gq_ref)
            gk_ref[...] = jnp.zeros_like(gk_ref)

        for h in range(DA_GH):
            cs = slice(h * HEAD, (h + 1) * HEAD)
            for j, src in enumerate((dq_ref, dk_ref, dv_ref)):
                for r in range(d):
                    _store_strided(nat_refs[j], r, tn, d, src[h, r])
            ov_ref[:, cs] = nat_refs[2][...].astype(BF16)
            for j, (x_ref, gn_ref, out_ref, acc_ref) in enumerate(((q_ref, qg_ref, oq_ref, gq_ref),
                                                                    (k_ref, kg_ref, ok_ref, gk_ref))):
                xhat, r = _rms(x_ref[:, cs])
                dx, dgr = _rms_bwd(nat_refs[j][...], xhat, r, gn_ref[...])
                out_ref[:, cs] = dx.astype(BF16)
                acc_ref[...] += jnp.sum(dgr, axis=0, keepdims=True)

    res = pl.BlockSpec((DA_GH, d, tn, HEAD), lambda i: (0, 0, i, 0))
    col = lambda off: pl.BlockSpec((rb, wide), lambda i: (i, off // wide + g))
    vec = pl.BlockSpec((1, HEAD), lambda i: (0, 0))
    nat = pl.BlockSpec((rb, wide), lambda i: (i, 0))
    shape = jax.ShapeDtypeStruct((rows, wide), BF16)
    gshape = jax.ShapeDtypeStruct((1, HEAD), F32)
    return pl.pallas_call(
        body, name=name, grid=(rows // rb,),
        in_specs=[res, res, res, col(C_DQ), col(C_DK), vec, vec],
        out_specs=[nat, nat, nat, vec, vec], out_shape=[shape, shape, shape, gshape, gshape],
        scratch_shapes=[pltpu.VMEM((rb, HEAD), F32)] * 3,
        compiler_params=_params(("arbitrary",)),
    )(dqr, dkr, dvr, proj, proj, q_gain, k_gain)


def _mem_fwd(proj, kv, q_gain, k_gain, *, name):
    rows = proj.shape[0]
    tm = min(512, rows)
    n_mem = kv.shape[0]

    def body(q_ref, k_ref, v_ref, qg_ref, kg_ref, o_ref):
        qhat, _ = _rms(q_ref[...])
        khat, _ = _rms(k_ref[...])
        s = _dot_nt(qhat * qg_ref[...], khat * kg_ref[...]) * ATT_SCALE
        p = jnp.exp(s - jnp.max(s, axis=-1, keepdims=True))
        p = p / jnp.sum(p, axis=-1, keepdims=True)
        o_ref[...] = _dot(p, v_ref[...]).astype(BF16)

    vec = pl.BlockSpec((1, HEAD), lambda i, h: (0, 0))
    return pl.pallas_call(
        body, name=name, grid=(rows // tm, MEM_HEADS),
        in_specs=[pl.BlockSpec((tm, HEAD), lambda i, h: (i, C_MQ // HEAD + h)),
                  pl.BlockSpec((n_mem, HEAD), lambda i, h: (0, h)),
                  pl.BlockSpec((n_mem, HEAD), lambda i, h: (0, MEM_HEADS + h)), vec, vec],
        out_specs=pl.BlockSpec((tm, HEAD), lambda i, h: (i, h)),
        out_shape=jax.ShapeDtypeStruct((rows, MEM_HEADS * HEAD), BF16),
        compiler_params=_params(("parallel", "parallel")),
    )(proj, kv, kv, q_gain, k_gain)


def _mem_bwd(dout, proj, kv, q_gain, k_gain, *, name):
    rows = proj.shape[0]
    tm = min(512, rows)
    steps = rows // tm
    n_mem = kv.shape[0]

    def body(d_ref, q_ref, k_ref, v_ref, qg_ref, kg_ref, dq_ref, dk_ref, dv_ref, gq_ref, gk_ref, dkn_ref):
        h, i = pl.program_id(0), pl.program_id(1)

        @pl.when((h == 0) & (i == 0))
        def _():
            gq_ref[...] = jnp.zeros_like(gq_ref)
            gk_ref[...] = jnp.zeros_like(gk_ref)

        @pl.when(i == 0)
        def _():
            dkn_ref[...] = jnp.zeros_like(dkn_ref)
            dv_ref[...] = jnp.zeros_like(dv_ref)

        qhat, rq = _rms(q_ref[...])
        khat, rk = _rms(k_ref[...])
        qn, kn = qhat * qg_ref[...], khat * kg_ref[...]
        s = _dot_nt(qn, kn) * ATT_SCALE
        p = jnp.exp(s - jnp.max(s, axis=-1, keepdims=True))
        p = p / jnp.sum(p, axis=-1, keepdims=True)
        dout = d_ref[...]
        dp = _dot_nt(dout, v_ref[...])
        ds = p * (dp - jnp.sum(p * dp, axis=-1, keepdims=True))
        dv_ref[...] += _dot_tn(p, dout)
        dkn_ref[...] += _dot_tn(ds, qn) * ATT_SCALE
        dq, dgr = _rms_bwd(_dot(ds, kn) * ATT_SCALE, qhat, rq, qg_ref[...])
        dq_ref[...] = dq.astype(BF16)
        gq_ref[...] += jnp.sum(dgr, axis=0, keepdims=True)

        @pl.when(i == steps - 1)
        def _():
            dk, dgk = _rms_bwd(dkn_ref[...], khat, rk, kg_ref[...])
            dk_ref[...] = dk
            gk_ref[...] += jnp.sum(dgk, axis=0, keepdims=True)

    vec = pl.BlockSpec((1, HEAD), lambda h, i: (0, 0))
    memh = pl.BlockSpec((n_mem, HEAD), lambda h, i: (0, h))
    tok = pl.BlockSpec((tm, HEAD), lambda h, i: (i, h))
    gshape = jax.ShapeDtypeStruct((1, HEAD), F32)
    return pl.pallas_call(
        body, name=name, grid=(MEM_HEADS, steps),
        in_specs=[tok, pl.BlockSpec((tm, HEAD), lambda h, i: (i, C_MQ // HEAD + h)), memh,
                  pl.BlockSpec((n_mem, HEAD), lambda h, i: (0, MEM_HEADS + h)), vec, vec],
        out_specs=[tok, memh, memh, vec, vec],
        out_shape=[jax.ShapeDtypeStruct((rows, MEM_HEADS * HEAD), BF16),
                   jax.ShapeDtypeStruct((n_mem, MEM_HEADS * HEAD), F32),
                   jax.ShapeDtypeStruct((n_mem, MEM_HEADS * HEAD), F32), gshape, gshape],
        scratch_shapes=[pltpu.VMEM((n_mem, HEAD), F32)],
        compiler_params=_params(("arbitrary", "arbitrary")),
    )(dout, proj, kv, kv, q_gain, k_gain)


def _branch_fwd(o_hg, o_da, o_mem, proj, wp_hg, wp_da, wp_mem, *, name):
    rows = o_hg.shape[0]
    tm = min(256, rows)

    def body(a_ref, b_ref, c_ref, ga_ref, gb_ref, gc_ref, wa_ref, wb_ref, wc_ref, o_ref):
        merged = _sigmoid(ga_ref[...]) * _dot(a_ref[...], wa_ref[...])
        merged += _sigmoid(gb_ref[...]) * _dot(b_ref[...], wb_ref[...])
        merged += _sigmoid(gc_ref[...]) * _dot(c_ref[...], wc_ref[...])
        o_ref[...] = merged.astype(BF16)

    row = lambda w: pl.BlockSpec((tm, w), lambda i: (i, 0))
    gate = lambda off: pl.BlockSpec((tm, D_MODEL), lambda i: (i, off // D_MODEL))
    full = lambda a: pl.BlockSpec(a.shape, lambda i: (0, 0))
    return pl.pallas_call(
        body, name=name, grid=(rows // tm,),
        in_specs=[row(o_hg.shape[1]), row(o_da.shape[1]), row(o_mem.shape[1]),
                  gate(C_GHG), gate(C_GDA), gate(C_GMEM), full(wp_hg), full(wp_da), full(wp_mem)],
        out_specs=row(D_MODEL), out_shape=jax.ShapeDtypeStruct((rows, D_MODEL), BF16),
        compiler_params=_params(("parallel",)),
    )(o_hg, o_da, o_mem, proj, proj, proj, wp_hg, wp_da, wp_mem)


def _branch_bwd(dm, o_hg, o_da, o_mem, proj, wp_hg, wp_da, wp_mem, *, name):
    rows = o_hg.shape[0]
    tm = min(256, rows)

    def body(dm_ref, a_ref, b_ref, c_ref, ga_ref, gb_ref, gc_ref, wa_ref, wb_ref, wc_ref, *outs):
        dmv = dm_ref[...]
        for j, (o_ref, g_ref, w_ref) in enumerate(((a_ref, ga_ref, wa_ref), (b_ref, gb_ref, wb_ref),
                                                   (c_ref, gc_ref, wc_ref))):
            z = _dot(o_ref[...], w_ref[...])
            gs = _sigmoid(g_ref[...])
            dz = (dmv * gs).astype(BF16)
            outs[3 * j][...] = (dmv * z * gs * (1.0 - gs)).astype(BF16)
            outs[3 * j + 1][...] = dz
            outs[3 * j + 2][...] = _dot_nt(dz, w_ref[...])

    row = lambda w: pl.BlockSpec((tm, w), lambda i: (i, 0))
    gate = lambda off: pl.BlockSpec((tm, D_MODEL), lambda i: (i, off // D_MODEL))
    full = lambda a: pl.BlockSpec(a.shape, lambda i: (0, 0))
    out_specs, out_shape = [], []
    for o in (o_hg, o_da, o_mem):
        out_specs += [row(D_MODEL), row(D_MODEL), row(o.shape[1])]
        out_shape += [jax.ShapeDtypeStruct((rows, D_MODEL), BF16), jax.ShapeDtypeStruct((rows, D_MODEL), BF16),
                      jax.ShapeDtypeStruct((rows, o.shape[1]), F32)]
    return pl.pallas_call(
        body, name=name, grid=(rows // tm,),
        in_specs=[row(D_MODEL), row(o_hg.shape[1]), row(o_da.shape[1]), row(o_mem.shape[1]),
                  gate(C_GHG), gate(C_GDA), gate(C_GMEM), full(wp_hg), full(wp_da), full(wp_mem)],
        out_specs=out_specs, out_shape=out_shape,
        compiler_params=_params(("parallel",)),
    )(dm, o_hg, o_da, o_mem, proj, proj, proj, wp_hg, wp_da, wp_mem)


def _ffn_in(h2, w_ab, *, name):
    rows, dff = h2.shape[0], w_ab.shape[1] // 2
    tm, tn = min(1024, rows), 256

    def body(h_ref, wa_ref, wb_ref, a_ref, b_ref, u_ref):
        a = _dot(h_ref[...], wa_ref[...])
        b = _dot(h_ref[...], wb_ref[...])
        a_ref[...] = a.astype(BF16)
        b_ref[...] = b.astype(BF16)
        u_ref[...] = (a * _sigmoid(a) * b).astype(BF16)

    out = pl.BlockSpec((tm, tn), lambda i, j: (i, j))
    return pl.pallas_call(
        body, name=name, grid=(rows // tm, dff // tn),
        in_specs=[pl.BlockSpec((tm, D_MODEL), lambda i, j: (i, 0)),
                  pl.BlockSpec((D_MODEL, tn), lambda i, j: (0, j)),
                  pl.BlockSpec((D_MODEL, tn), lambda i, j: (0, dff // tn + j))],
        out_specs=[out, out, out],
        out_shape=[jax.ShapeDtypeStruct((rows, dff), BF16)] * 3,
        compiler_params=_params(("parallel", "parallel")),
    )(h2, w_ab, w_ab)


def _ffn_act_bwd(dy, w_out, a, b, *, name):
    rows, dff = a.shape
    tm, tn = min(1024, rows), 256

    def body(dy_ref, w_ref, a_ref, b_ref, da_ref, db_ref):
        du = _dot_nt(dy_ref[...], w_ref[...])
        av, bv = a_ref[...].astype(F32), b_ref[...].astype(F32)
        sa = _sigmoid(av)
        da_ref[...] = (du * bv * sa * (1.0 + av * (1.0 - sa))).astype(BF16)
        db_ref[...] = (du * av * sa).astype(BF16)

    tile = pl.BlockSpec((tm, tn), lambda i, j: (i, j))
    return pl.pallas_call(
        body, name=name, grid=(rows // tm, dff // tn),
        in_specs=[pl.BlockSpec((tm, D_MODEL), lambda i, j: (i, 0)),
                  pl.BlockSpec((tn, D_MODEL), lambda i, j: (j, 0)), tile, tile],
        out_specs=[tile, tile],
        out_shape=[jax.ShapeDtypeStruct((rows, dff), BF16), jax.ShapeDtypeStruct((rows, dff), BF16)],
        compiler_params=_params(("parallel", "parallel")),
    )(dy, w_out, a, b)


def _lower_bound(lb_fw, lb_bw, *, name):
    def body(a_ref, b_ref, oa_ref, ob_ref):
        for src, dst in ((a_ref, oa_ref), (b_ref, ob_ref)):
            dst[...] = _sigmoid(src[0:1, :] - src[1:2, :])

    shape = jax.ShapeDtypeStruct((1, lb_fw.shape[1]), F32)
    return pl.pallas_call(body, name=name, out_shape=[shape, shape])(lb_fw, lb_bw)


def _local_step(x, mem, tgt, p, ex):
    rows = x.shape[0]
    lb_fw, lb_bw = _lower_bound(p["lb_logits_fw"], p["lb_logits_bw"], name="lower_bound")

    h = _rms_fwd(x, p["norm_mix_gain"], name="norm_mix")
    w = {"w_in": ex.w_in}
    proj, carried = _matmul([(h, w["w_in"])], "nn", BF16, tm=1024, tn=1024, carry=ex.late_carry(), name="proj_in")
    w.update(ex.late_weights(carried))
    o_fw, st_fw = _gla_fwd(proj, lb_fw, f_off=C_FF, rev=False, name="gla_fwd_fw")
    o_bw, st_bw = _gla_fwd(proj, lb_bw, f_off=C_FB, rev=True, name="gla_fwd_bw")
    o_hg = _hg_out_fwd(o_fw, o_bw, proj, p["hg_norm_gain"], name="hg_out")

    qkv_r, outs, lses = [], [], []
    for g in range(len(DA_GROUPS)):
        qr, kr, vr = _da_prep(proj, p["da_q_gain"], p["da_k_gain"], g, name=f"da_prep{g}")
        og, lg = _band_fwd(qr, kr, vr, g, name=f"band_fwd{g}")
        qkv_r.append((qr, kr, vr))
        outs.append(og)
        lses.append(lg)
    o_da, lse_da = _da_merge(outs, lses, rows, name="da_merge")

    mem_n = _rms_fwd(mem, p["norm_mem_gain"], name="norm_mem")
    kv = _matmul([(mem_n, w["w_mem_kv"])], "nn", F32, tm=256, tn=512, name="mem_kv")
    o_mem = _mem_fwd(proj, kv, p["mem_q_gain"], p["mem_k_gain"], name="mem_attn")

    merged = _branch_fwd(o_hg, o_da, o_mem, proj, w["w_proj_hg"], w["w_proj_da"], w["w_proj_mem"],
                         name="branch_merge")
    x1 = _matmul([(merged, w["w_out"])], "nn", F32, tm=512, tn=512, residual=x, name="mix_out")
    h2 = _rms_fwd(x1, p["norm_ffn_gain"], name="norm_ffn")
    a, b, u = _ffn_in(h2, w["w_ffn_in"], name="ffn_in")
    y = _matmul([(u, w["w_ffn_out"])], "nn", F32, tm=512, tn=512, residual=x1, name="ffn_out")
    dy, dy_b, loss = _loss_grad(y, tgt, name="loss")

    gw, gs = {}, {}
    gw["w_ffn_out"] = _matmul([(u, dy_b)], "tn", F32, tm=256, tn=512, name="g_ffn_out")
    da, db = _ffn_act_bwd(dy_b, w["w_ffn_out"], a, b, name="ffn_act_bwd")
    gw["w_ffn_a"] = _matmul([(h2, da)], "tn", F32, tm=512, tn=256, name="g_ffn_a")
    gw["w_ffn_b"] = _matmul([(h2, db)], "tn", F32, tm=512, tn=256, name="g_ffn_b")
    dh2 = _matmul([(da, w["w_ffn_in"], 0), (db, w["w_ffn_in"], 1)], "nt", F32, tm=256, tn=1024, name="d_h2")
    dx1, dx1_b, gs["norm_ffn_gain"] = _rms_bwd_rows(dh2, x1, p["norm_ffn_gain"], dy, name="norm_ffn_bwd")
    gw["w_out"] = _matmul([(merged, dx1_b)], "tn", F32, tm=512, tn=512, name="g_out")
    dmerged = _matmul([(dx1_b, w["w_out"])], "nt", F32, tm=512, tn=512, name="d_merged")
    (dg_hg, dz_hg, do_hg, dg_da, dz_da, do_da, dg_mem, dz_mem, do_mem) = _branch_bwd(
        dmerged, o_hg, o_da, o_mem, proj, w["w_proj_hg"], w["w_proj_da"], w["w_proj_mem"], name="branch_bwd")
    gw["w_proj_hg"] = _matmul([(o_hg, dz_hg)], "tn", F32, tm=512, tn=512, name="g_proj_hg")
    gw["w_proj_da"] = _matmul([(o_da, dz_da)], "tn", F32, tm=512, tn=512, name="g_proj_da")
    gw["w_proj_mem"] = _matmul([(o_mem, dz_mem)], "tn", F32, tm=512, tn=512, name="g_proj_mem")

    dq_mem, dk_mem, dv_mem, gs["mem_q_gain"], gs["mem_k_gain"] = _mem_bwd(
        do_mem, proj, kv, p["mem_q_gain"], p["mem_k_gain"], name="mem_attn_bwd")
    dkv = jnp.concatenate([dk_mem, dv_mem], axis=1).astype(BF16)
    gw["w_mem_kv"] = _matmul([(mem_n, dkv)], "tn", F32, tm=512, tn=512, name="g_mem_kv")
    dmem_n = _matmul([(dkv, w["w_mem_kv"])], "nt", F32, tm=256, tn=512, name="d_mem_n")
    _, _, gs["norm_mem_gain"] = _rms_bwd_rows(dmem_n, mem, p["norm_mem_gain"], None, name="norm_mem_bwd")

    prep = _da_bwd_prep(do_da, o_da, lse_da, name="da_bwd_prep")
    d_da, gq_parts, gk_parts = [], [], []
    for g in range(len(DA_GROUPS)):
        qr, kr, vr = qkv_r[g]
        dor, lser, deltar = prep[3 * g:3 * g + 3]
        dqr, dkr, dvr = _band_bwd(qr, kr, vr, dor, lser, deltar, g, name=f"band_bwd{g}")
        dq, dk, dv, gq, gk = _da_prep_bwd(dqr, dkr, dvr, proj, p["da_q_gain"], p["da_k_gain"], g,
                                          name=f"da_prep_bwd{g}")
        d_da.append((dq, dk, dv))
        gq_parts.append(gq)
        gk_parts.append(gk)

    dg_hgate, do_gla, gs["hg_norm_gain"] = _hg_out_bwd(do_hg, o_fw, o_bw, proj, p["hg_norm_gain"], name="hg_out_bwd")
    (dq_f, dfl_fw, dv_f, dlb_fw), carried = _gla_bwd(proj, lb_fw, do_gla, st_fw, None, f_off=C_FF, rev=False,
                                                     carry=ex.early_carry(gw), name="gla_bwd_fw")
    ex.early_done(carried)
    dq_hg, dfl_bw, dv_hg, dlb_bw = _gla_bwd(proj, lb_bw, do_gla, st_bw, (dq_f, dv_f), f_off=C_FB, rev=True,
                                            name="gla_bwd_bw")

    dproj = jnp.concatenate(
        [dq_hg, dfl_fw, dfl_bw, dv_hg, dg_hgate]
        + [t[0] for t in d_da] + [t[1] for t in d_da] + [t[2] for t in d_da]
        + [dq_mem, dg_hg, dg_da, dg_mem], axis=1)
    g_in = _matmul([(h, dproj)], "tn", F32, tm=512, tn=1024, name="g_in")
    dh, carried = _matmul([(dproj, w["w_in"])], "nt", F32, tm=1024, tn=1024, tk=IN_COLS // 8,
                          carry=ex.w_in_carry(g_in), name="d_h")
    ex.w_in_done(carried)
    grad_x, _, gs["norm_mix_gain"] = _rms_bwd_rows(dh, x, p["norm_mix_gain"], dx1, name="norm_mix_bwd")

    small = _small_pack(gs, gq_parts, gk_parts, dlb_fw, dlb_bw, lb_fw, lb_bw, name="small_pack")
    return loss, grad_x, small


def _small_pack(gs, gq_parts, gk_parts, dlb_fw, dlb_bw, lb_fw, lb_bw, *, name):
    def body(g_mix, g_mem, g_ffn, dfw, dbw, lfw, lbw, g_hg, q0, q1, q2, k0, k1, k2, g_mq, g_mk, o_ref):
        o_ref[0:1, :] = g_mix[...]
        o_ref[1:2, :] = g_mem[...]
        o_ref[2:3, :] = g_ffn[...]
        for base, d_ref, l_ref in ((3, dfw, lfw), (5, dbw, lbw)):
            lbv = l_ref[...]
            t = d_ref[...] * lbv * (1.0 - lbv)
            o_ref[base:base + 1, :] = t
            o_ref[base + 1:base + 2, :] = -t
        o_ref[7:8, :] = jnp.zeros((1, D_MODEL), F32)
        o_ref[7:8, 0:HEAD] = g_hg[...]
        o_ref[7:8, HEAD:2 * HEAD] = q0[...] + q1[...] + q2[...]
        o_ref[7:8, 2 * HEAD:3 * HEAD] = k0[...] + k1[...] + k2[...]
        o_ref[7:8, 3 * HEAD:4 * HEAD] = g_mq[...]
        o_ref[7:8, 4 * HEAD:5 * HEAD] = g_mk[...]

    return pl.pallas_call(body, name=name, out_shape=jax.ShapeDtypeStruct((8, D_MODEL), F32))(
        gs["norm_mix_gain"], gs["norm_mem_gain"], gs["norm_ffn_gain"], dlb_fw, dlb_bw, lb_fw, lb_bw,
        gs["hg_norm_gain"], *gq_parts, *gk_parts, gs["mem_q_gain"], gs["mem_k_gain"])


def _row_tile(rows, cols, n_arrays):
    budget = (16 * 1024 * 1024) // (2 * 4 * cols * n_arrays)
    tr = rows
    while tr > budget and tr % 2 == 0 and (tr // 2) % 16 == 0:
        tr //= 2
    return tr


def _cast_into_full(a, chip, rows, cols, axis, *, name):
    sr, sc = a.shape
    tr = _row_tile(sr, sc, 2)

    def body(chip_ref, a_ref, o_ref):
        del chip_ref
        o_ref[...] = a_ref[...].astype(BF16)

    if axis == 1:
        out_map = lambda i, chip_ref: (i, chip_ref[0])
    else:
        out_map = lambda i, chip_ref: (chip_ref[0] * (sr // tr) + i, 0)
    return pl.pallas_call(
        body, name=name,
        grid_spec=pltpu.PrefetchScalarGridSpec(
            num_scalar_prefetch=1, grid=(sr // tr,),
            in_specs=[pl.BlockSpec((tr, sc), lambda i, chip_ref: (i, 0))],
            out_specs=pl.BlockSpec((tr, sc), out_map)),
        out_shape=jax.ShapeDtypeStruct((rows, cols), BF16),
        compiler_params=_params(("parallel",)))(chip, a)


def _add_halves(items, *, name):
    rows = items[0][3].shape[0]
    widths = [ra.shape[1] for _, _, _, ra in items]
    tr = _row_tile(rows, sum(widths), 4)

    def body(*refs):
        o_ref = refs[-1]
        first = lax.axis_index("c") == 0
        off = 0
        for j, wd in enumerate(widths):
            h0, h1, ra = refs[3 * j:3 * j + 3]
            o_ref[:, off:off + wd] = (jnp.where(first, h0[...], h1[...]) + ra[...]).astype(BF16)
            off += wd

    in_specs, ins = [], []
    for (g, haxis, hsize, ra), wd in zip(items, widths):
        if haxis == 0:
            in_specs += [pl.BlockSpec((tr, wd), lambda i: (i, 0)),
                         pl.BlockSpec((tr, wd), lambda i, o=hsize // tr: (o + i, 0))]
        else:
            in_specs += [pl.BlockSpec((tr, wd), lambda i: (i, 0)), pl.BlockSpec((tr, wd), lambda i: (i, 1))]
        in_specs.append(pl.BlockSpec((tr, wd), lambda i: (i, 0)))
        ins += [g, g, ra]
    return pl.pallas_call(body, name=name, grid=(rows // tr,), in_specs=in_specs,
                          out_specs=pl.BlockSpec((tr, sum(widths)), lambda i: (i, 0)),
                          out_shape=jax.ShapeDtypeStruct((rows, sum(widths)), BF16),
                          compiler_params=_params(("parallel",)))(*ins)


def _add_slots(rb, *, name):
    _, rows, cols = rb.shape
    tr = _row_tile(rows, cols, 5)

    def body(r0, r1, r2, r3, o_ref):
        o_ref[...] = ((r0[...].astype(F32) + r1[...].astype(F32)) + r2[...].astype(F32)) + r3[...].astype(F32)

    slot = lambda s: pl.BlockSpec((None, tr, cols), lambda i: (s, i, 0))
    return pl.pallas_call(body, name=name, grid=(rows // tr,), in_specs=[slot(s) for s in range(4)],
                          out_specs=pl.BlockSpec((tr, cols), lambda i: (i, 0)),
                          out_shape=jax.ShapeDtypeStruct((rows, cols), F32),
                          compiler_params=_params(("parallel",)))(rb, rb, rb, rb)


def _adamw(w, g, m, v, *, name):
    rows, cols = w.shape
    tr = _row_tile(rows, cols, 7) if rows % 16 == 0 else rows
    c1 = 1.0 - ADAM_B1 ** ADAM_STEP
    c2 = 1.0 - ADAM_B2 ** ADAM_STEP

    def body(w_ref, g_ref, m_ref, v_ref, d_ref, mo_ref, vo_ref):
        gv = g_ref[...]
        mn = ADAM_B1 * m_ref[...] + (1.0 - ADAM_B1) * gv
        vn = ADAM_B2 * v_ref[...] + (1.0 - ADAM_B2) * (gv * gv)
        mo_ref[...] = mn
        vo_ref[...] = vn
        d_ref[...] = -ADAM_LR * ((mn / c1) / (jnp.sqrt(vn / c2) + ADAM_EPS) + ADAM_WD * w_ref[...])

    spec = pl.BlockSpec((tr, cols), lambda i: (i, 0))
    shape = jax.ShapeDtypeStruct((rows, cols), F32)
    return pl.pallas_call(body, name=name, grid=(rows // tr,), in_specs=[spec] * 4, out_specs=[spec] * 3,
                          out_shape=[shape] * 3, compiler_params=_params(("parallel",)))(w, g, m, v)


W_SPECS = (
    ("w_in", 1024, IN_COLS, 1, IN_COLS // 4),
    ("w_mem_kv", 1024, 1024, 0, 256),
    ("w_proj_hg", 1024, 1024, 0, 256),
    ("w_proj_da", 512, 1024, 1, 256),
    ("w_proj_mem", 512, 1024, 1, 256),
    ("w_out", 1024, 1024, 0, 256),
    ("w_ffn_in", 1024, 2 * D_FF, 1, 2 * D_FF // 4),
    ("w_ffn_out", D_FF, 1024, 0, D_FF // 4),
)
CHIP_FLIPS = ((1, 0), (0, 1), (1, 1))
ANY = pl.BlockSpec(memory_space=pl.ANY)
DMA_CHUNK_BYTES = 1 << 20
STAGE_BYTES = 2 << 20


def _place():
    x, y, c = lax.axis_index("x"), lax.axis_index("y"), lax.axis_index("c")
    return x, y, c, 2 * x + y


def _flip(v, f):
    return 1 - v if f else v


def _slab(ref, axis, idx, size):
    start = pl.multiple_of(idx * size, size)
    return ref.at[pl.ds(start, size), :] if axis == 0 else ref.at[:, pl.ds(start, size)]


def _chunked(make, src, dst, want="both"):
    rows, cols = src.shape
    row_bytes = cols * jnp.dtype(src.dtype).itemsize
    k = 1
    while rows % (2 * k) == 0 and (rows // (2 * k)) % 16 == 0 and (rows // k) * row_bytes > DMA_CHUNK_BYTES:
        k *= 2
    cr = rows // k
    parts = []
    if want != "wait":
        parts = [make(src.at[pl.ds(j * cr, cr), :], dst.at[pl.ds(j * cr, cr), :]) for j in range(k)]
    return parts, (make(src, dst) if want != "start" else None)


def _half_spec(rows, cols, axis):
    return (0, rows // 2) if axis == 1 else (1, cols // 2)


def _staged(src, remote_dst, local_dst, sibling, load_sems, send_sems, store_sems, recv_sem):
    rows, cols = src.shape
    row_bytes = cols * jnp.dtype(src.dtype).itemsize
    k = 1
    while (rows // k) * row_bytes > STAGE_BYTES and rows % (2 * k) == 0 and (rows // (2 * k)) % 16 == 0:
        k *= 2
    cr = rows // k
    piece = lambda ref, j: ref.at[pl.ds(j * cr, cr), :]

    def run(buf):
        loads = [pltpu.make_async_copy(piece(src, j), buf.at[j % 2], load_sems.at[j % 2]) for j in range(k)]
        outs = [[pltpu.make_async_remote_copy(src_ref=buf.at[j % 2], dst_ref=piece(remote_dst, j),
                                              send_sem=send_sems.at[j % 2], recv_sem=recv_sem,
                                              device_id=sibling, device_id_type=MESH)] for j in range(k)]
        if local_dst is not None:
            for j in range(k):
                outs[j].append(pltpu.make_async_copy(buf.at[j % 2], piece(local_dst, j), store_sems.at[j % 2]))

        def drained(j):
            outs[j][0].wait_send()
            for cp in outs[j][1:]:
                cp.wait()

        loads[0].start()
        for j in range(k):
            loads[j].wait()
            for cp in outs[j]:
                cp.start()
            if j + 1 < k:
                if j >= 1:
                    drained(j - 1)
                loads[j + 1].start()
        for j in range(max(0, k - 2), k):
            drained(j)

    pl.run_scoped(run, pltpu.VMEM((2, cr, cols), src.dtype))


def _landed(ref, recv_sem, send_sem):
    pltpu.make_async_remote_copy(src_ref=ref, dst_ref=ref, send_sem=send_sem, recv_sem=recv_sem,
                                 device_id=(lax.axis_index("x"), lax.axis_index("y"), lax.axis_index("c")),
                                 device_id_type=MESH).wait_recv()


def _half_slab(ref, spec, chip, half):
    _, rows, cols, axis, size = spec
    haxis, hsize = _half_spec(rows, cols, axis)
    return _slab(_slab(ref, axis, chip, size), haxis, half, hsize)


def _gather_sends(outs, specs, ici_send, ici_recv, want="both"):
    x, y, c, p = _place()
    sent = []
    for wi, spec in enumerate(specs):
        mine = _half_slab(outs[wi], spec, p, c)
        for k, (fx, fy) in enumerate(CHIP_FLIPS):
            sent.append(_chunked(lambda s, d, j=3 * wi + k, fx=fx, fy=fy: pltpu.make_async_remote_copy(
                src_ref=s, dst_ref=d, send_sem=ici_send.at[j], recv_sem=ici_recv.at[j],
                device_id=(_flip(x, fx), _flip(y, fy), c), device_id_type=MESH), mine, mine, want))
    return sent


def _gather_weights(fulls, specs, *, name):
    n = len(specs)

    def body(*refs):
        outs = refs[n:2 * n]
        ici_send, ici_recv, load_sems, d2d_send, d2d_recv = refs[2 * n:]
        x, y, c, _ = _place()
        sent = _gather_sends(outs, specs, ici_send, ici_recv)
        for parts, _ in sent:
            for cp in parts:
                cp.start()
        for k, (fx, fy) in enumerate(CHIP_FLIPS):
            q = 2 * _flip(x, fx) + _flip(y, fy)
            for wi in range(n):
                sent[3 * wi + k][1].wait_recv()
                got = _half_slab(outs[wi], specs[wi], q, c)
                _staged(got, got, None, (x, y, 1 - c), load_sems, d2d_send, None, d2d_recv.at[3 * wi + k])
        for _, whole in sent:
            whole.wait_send()
        for k, (fx, fy) in enumerate(CHIP_FLIPS):
            q = 2 * _flip(x, fx) + _flip(y, fy)
            for wi in range(n):
                _landed(_half_slab(outs[wi], specs[wi], q, 1 - c), d2d_recv.at[3 * wi + k], d2d_send.at[0])

    return pl.pallas_call(
        body, name=name, in_specs=[ANY] * n, out_specs=[ANY] * n,
        out_shape=[jax.ShapeDtypeStruct(f.shape, f.dtype) for f in fulls],
        input_output_aliases={i: i for i in range(n)},
        scratch_shapes=[pltpu.SemaphoreType.DMA((3 * n,)), pltpu.SemaphoreType.DMA((3 * n,)),
                        pltpu.SemaphoreType.DMA((2,)), pltpu.SemaphoreType.DMA((2,)),
                        pltpu.SemaphoreType.DMA((3 * n,))],
    )(*fulls)


def _gather_chips_carry(fulls, specs):
    n = len(specs)

    def issue(ins, outs, sems, want):
        del ins
        return [(parts, whole, True) for parts, whole in _gather_sends(outs, specs, sems[0], sems[1], want)]

    return _Carry(list(fulls), [jax.ShapeDtypeStruct(f.shape, f.dtype) for f in fulls], {i: i for i in range(n)},
                  [pltpu.SemaphoreType.DMA((3 * n,)), pltpu.SemaphoreType.DMA((3 * n,))], issue)


def _gather_pass_on(fulls, specs, *, name):
    n = len(specs)

    def body(*refs):
        outs = refs[n:2 * n]
        load_sems, d2d_send, d2d_recv = refs[2 * n:]
        x, y, c, _ = _place()
        for k, (fx, fy) in enumerate(CHIP_FLIPS):
            q = 2 * _flip(x, fx) + _flip(y, fy)
            for wi in range(n):
                got = _half_slab(outs[wi], specs[wi], q, c)
                _staged(got, got, None, (x, y, 1 - c), load_sems, d2d_send, None, d2d_recv.at[3 * wi + k])
        for k, (fx, fy) in enumerate(CHIP_FLIPS):
            q = 2 * _flip(x, fx) + _flip(y, fy)
            for wi in range(n):
                _landed(_half_slab(outs[wi], specs[wi], q, 1 - c), d2d_recv.at[3 * wi + k], d2d_send.at[0])

    return pl.pallas_call(
        body, name=name, in_specs=[ANY] * n, out_specs=[ANY] * n,
        out_shape=[jax.ShapeDtypeStruct(f.shape, f.dtype) for f in fulls],
        input_output_aliases={i: i for i in range(n)},
        scratch_shapes=[pltpu.SemaphoreType.DMA((2,)), pltpu.SemaphoreType.DMA((2,)),
                        pltpu.SemaphoreType.DMA((3 * n,))],
    )(*fulls)


def _sibling_exchange(grads, *, name):
    n = len(grads)

    def body(*refs):
        ins, outs = refs[:n], refs[n:2 * n]
        load_sems, send_sems, recv_sems = refs[2 * n:]
        x, y, c, _ = _place()
        for i, (_, haxis, hsize) in enumerate(grads):
            _staged(_slab(ins[i], haxis, 1 - c, hsize), outs[i], None, (x, y, 1 - c),
                    load_sems, send_sems, None, recv_sems.at[i])
        for i in range(n):
            _landed(outs[i], recv_sems.at[i], send_sems.at[0])

    shapes = [jax.ShapeDtypeStruct((hsize, g.shape[1]) if haxis == 0 else (g.shape[0], hsize), F32)
              for g, haxis, hsize in grads]
    return pl.pallas_call(
        body, name=name, in_specs=[ANY] * n, out_specs=[ANY] * n, out_shape=shapes,
        scratch_shapes=[pltpu.SemaphoreType.DMA((2,)), pltpu.SemaphoreType.DMA((2,)),
                        pltpu.SemaphoreType.DMA((n,))],
    )(*[g for g, _, _ in grads])


def _chip_exchange_carry(parts):
    n = len(parts)

    def issue(ins, outs, sems, want):
        send_sems, recv_sems, local_sems = sems
        x, y, c, p = _place()
        copies = []
        for i, (_, axis, size) in enumerate(parts):
            copies.append(_chunked(lambda s, d, i=i: pltpu.make_async_copy(s, d, local_sems.at[i]),
                                   _slab(ins[i], axis, p, size), outs[i].at[p], want) + (False,))
            for k, (fx, fy) in enumerate(CHIP_FLIPS):
                px, py = _flip(x, fx), _flip(y, fy)
                copies.append(_chunked(lambda s, d, j=3 * i + k, px=px, py=py: pltpu.make_async_remote_copy(
                    src_ref=s, dst_ref=d, send_sem=send_sems.at[j], recv_sem=recv_sems.at[j],
                    device_id=(px, py, c), device_id_type=MESH),
                    _slab(ins[i], axis, 2 * px + py, size), outs[i].at[p], want) + (True,))
        return copies

    shapes = []
    for a, axis, size in parts:
        shapes.append(jax.ShapeDtypeStruct((4, size, a.shape[1]) if axis == 0 else (4, a.shape[0], size), a.dtype))
    return _Carry([a for a, _, _ in parts], shapes, {},
                  [pltpu.SemaphoreType.DMA((3 * n,)), pltpu.SemaphoreType.DMA((3 * n,)),
                   pltpu.SemaphoreType.DMA((n,))], issue)


def _sibling_share(sums):
    n = len(sums)

    def body(*refs):
        ins, outs = refs[:n], refs[n:2 * n]
        load_sems, send_sems, store_sems, recv_sems = refs[2 * n:]
        x, y, c, _ = _place()
        for i, (s, haxis) in enumerate(sums):
            place = _slab(outs[i], haxis, c, s.shape[haxis])
            _staged(ins[i], place, place, (x, y, 1 - c), load_sems, send_sems, store_sems, recv_sems.at[i])
        for i, (s, haxis) in enumerate(sums):
            _landed(_slab(outs[i], haxis, 1 - c, s.shape[haxis]), recv_sems.at[i], send_sems.at[0])

    shapes = []
    for s, haxis in sums:
        r, cc = s.shape
        shapes.append(jax.ShapeDtypeStruct((2 * r, cc) if haxis == 0 else (r, 2 * cc), F32))
    return pl.pallas_call(
        body, name="grad_sibling_share", in_specs=[ANY] * n, out_specs=[ANY] * n, out_shape=shapes,
        scratch_shapes=[pltpu.SemaphoreType.DMA((2,)), pltpu.SemaphoreType.DMA((2,)),
                        pltpu.SemaphoreType.DMA((2,)), pltpu.SemaphoreType.DMA((n,))],
    )(*[s for s, _ in sums])


class _Exchanges:
    def __init__(self, fulls):
        self.w_in = _gather_weights([fulls["w_in"]], W_SPECS[:1], name="gather_w_in")[0]
        self.late_specs = W_SPECS[1:]
        self.late = [fulls[s[0]] for s in self.late_specs]
        self.slots = {}

    def late_carry(self):
        return _gather_chips_carry(self.late, self.late_specs)

    def late_weights(self, carried):
        done = _gather_pass_on(carried, self.late_specs, name="gather_pass_on")
        return dict(zip([s[0] for s in self.late_specs], done))

    def _half_sums(self, gw, specs, tag):
        grads = []
        for name, _, _, axis, _ in specs:
            for g in ((gw["w_ffn_a"], gw["w_ffn_b"]) if name == "w_ffn_in" else (gw[name],)):
                grads.append((g,) + _half_spec(g.shape[0], g.shape[1], axis))
        theirs = _sibling_exchange(grads, name=f"grad_sibling_exchange_{tag}")
        parts, j = [], 0
        for name, _, _, axis, size in specs:
            take = 2 if name == "w_ffn_in" else 1
            items = [grads[i] + (theirs[i],) for i in range(j, j + take)]
            parts.append((_add_halves(items, name=f"half_sum_{name}"), axis, size))
            j += take
        return _chip_exchange_carry(parts)

    def early_carry(self, gw):
        return self._half_sums(gw, self.late_specs, "early")

    def early_done(self, carried):
        self.slots.update(zip([s[0] for s in self.late_specs], carried))

    def w_in_carry(self, g):
        return self._half_sums({"w_in": g}, W_SPECS[:1], "w_in")

    def w_in_done(self, carried):
        self.slots["w_in"] = carried[0]

    def reduced(self):
        sums = []
        for name, rows, cols, axis, _ in W_SPECS:
            sums.append((_add_slots(self.slots[name], name=f"chip_sum_{name}"), _half_spec(rows, cols, axis)[0]))
        return dict(zip([s[0] for s in W_SPECS], _sibling_share(sums)))


def _small_allreduce(sv):
    def body(sv_ref, o_ref, slots_ref, send_sems, recv_sems):
        x, y, c, _ = _place()
        me = 4 * x + 2 * y + c
        slots_ref[me] = sv_ref[...]
        copies = []
        for k in range(1, 8):
            fx, fy, fc = (k >> 2) & 1, (k >> 1) & 1, k & 1
            copies.append(pltpu.make_async_remote_copy(
                src_ref=sv_ref, dst_ref=slots_ref.at[me], send_sem=send_sems.at[k - 1],
                recv_sem=recv_sems.at[k - 1], device_id=(_flip(x, fx), _flip(y, fy), _flip(c, fc)),
                device_id_type=MESH))
        for cp in copies:
            cp.start()
        for cp in copies:
            cp.wait_recv()
        for cp in copies:
            cp.wait_send()
        total = slots_ref[0]
        for s in range(1, 8):
            total = total + slots_ref[s]
        o_ref[...] = total

    vm = pl.BlockSpec(memory_space=pltpu.VMEM)
    return pl.pallas_call(
        body, name="small_allreduce", in_specs=[vm], out_specs=vm,
        out_shape=jax.ShapeDtypeStruct(sv.shape, F32),
        scratch_shapes=[pltpu.VMEM((8,) + sv.shape, F32), pltpu.SemaphoreType.DMA((7,)),
                        pltpu.SemaphoreType.DMA((7,))],
    )(sv)


SMALL_ROWS = (("norm_mix_gain", 0), ("norm_mem_gain", 1), ("norm_ffn_gain", 2))
SMALL_LB = (("lb_logits_fw", 3), ("lb_logits_bw", 5))
SMALL_HEAD = ("hg_norm_gain", "da_q_gain", "da_k_gain", "mem_q_gain", "mem_k_gain")


def _pack_small(d):
    last = jnp.concatenate([d[n] for n in SMALL_HEAD] + [jnp.zeros((1, D_MODEL - HEAD * len(SMALL_HEAD)), F32)], axis=1)
    return jnp.concatenate([d["norm_mix_gain"], d["norm_mem_gain"], d["norm_ffn_gain"],
                            d["lb_logits_fw"], d["lb_logits_bw"], last], axis=0)


def _unpack_small(a):
    out = {n: a[r:r + 1] for n, r in SMALL_ROWS}
    out.update({n: a[r:r + 2] for n, r in SMALL_LB})
    out.update({n: a[7:8, j * HEAD:(j + 1) * HEAD] for j, n in enumerate(SMALL_HEAD)})
    return out


PARAM_ORDER = ("norm_mix_gain", "norm_mem_gain", "w_in", "lb_logits_fw", "lb_logits_bw", "hg_norm_gain",
               "da_q_gain", "da_k_gain", "w_mem_kv", "mem_q_gain", "mem_k_gain", "w_proj_hg", "w_proj_da",
               "w_proj_mem", "w_out", "norm_ffn_gain", "w_ffn_in", "w_ffn_out")


def kernel(x, mem, norm_mix_gain, norm_mem_gain, w_in, lb_logits_fw, lb_logits_bw, hg_norm_gain, da_q_gain, da_k_gain, w_mem_kv, mem_q_gain, mem_k_gain, w_proj_hg, w_proj_da, w_proj_mem, w_out, norm_ffn_gain, w_ffn_in, w_ffn_out, loss_target, m_norm_mix_gain, m_norm_mem_gain, m_w_in, m_lb_logits_fw, m_lb_logits_bw, m_hg_norm_gain, m_da_q_gain, m_da_k_gain, m_w_mem_kv, m_mem_q_gain, m_mem_k_gain, m_w_proj_hg, m_w_proj_da, m_w_proj_mem, m_w_out, m_norm_ffn_gain, m_w_ffn_in, m_w_ffn_out, v_norm_mix_gain, v_norm_mem_gain, v_w_in, v_lb_logits_fw, v_lb_logits_bw, v_hg_norm_gain, v_da_q_gain, v_da_k_gain, v_w_mem_kv, v_mem_q_gain, v_mem_k_gain, v_w_proj_hg, v_w_proj_da, v_w_proj_mem, v_w_out, v_norm_ffn_gain, v_w_ffn_in, v_w_ffn_out):
    args = dict(locals())
    mats = tuple(s[0] for s in W_SPECS)
    flat = lambda a: a.reshape(a.shape[-2:])
    w = {n: flat(args[n]) for n in mats}
    m = {n: flat(args["m_" + n]) for n in mats}
    v = {n: flat(args["v_" + n]) for n in mats}
    small = {n: args[n] for n in PARAM_ORDER if n not in mats}

    chip = (2 * lax.axis_index("x") + lax.axis_index("y")).astype(jnp.int32).reshape(1)
    ex = _Exchanges({n: _cast_into_full(w[n], chip, rows, cols, axis, name=f"cast_{n}")
                     for n, rows, cols, axis, _ in W_SPECS})
    loss, grad_x, small_grads = _local_step(x[0], mem[0], loss_target[0], small, ex)
    grads = ex.reduced()
    small_sum = _small_allreduce(small_grads)

    delta, new_m, new_v = {}, {}, {}
    for n in mats:
        delta[n], new_m[n], new_v[n] = _adamw(w[n], grads[n], m[n], v[n], name=f"adamw_{n}")
    packed = _adamw(_pack_small(small), small_sum,
                    _pack_small({n: args["m_" + n] for n in small}),
                    _pack_small({n: args["v_" + n] for n in small}), name="adamw_small")
    grads.update(_unpack_small(small_sum))
    for dst, src in zip((delta, new_m, new_v), packed):
        dst.update(_unpack_small(src))

    def shaped(d, n):
        return d[n].reshape(args[n].shape)

    loss_sum = lax.psum(loss[0, 0], ("x", "y", "c"))
    return (loss_sum, grad_x[None], *[shaped(grads, n) for n in PARAM_ORDER], *[shaped(delta, n) for n in PARAM_ORDER],
            *[shaped(new_m, n) for n in PARAM_ORDER], *[shaped(new_v, n) for n in PARAM_ORDER])
```

```python
import functools
import math

import numpy as np
import jax
import jax.numpy as jnp
from jax import lax
from jax.experimental import pallas as pl
from jax.experimental.pallas import tpu as pltpu

F32 = jnp.float32
BF16 = jnp.bfloat16
MESH = pl.DeviceIdType.MESH

D_MODEL = 1024
HEAD = 128
HG_HEADS = 8
DA_GROUPS = ((1, 64), (4, 64), (16, 64))
DA_GH = 4
MEM_HEADS = 4
N_MEM = 256
D_FF = 2816
CHUNK = 64
BAND_QBLOCKS = 4
GLA_HEADS_PER_STEP = 2
RMS_EPS = 1e-6
NEG_INF = -1e30
HG_SCALE = HEAD ** -0.5
ATT_SCALE = HEAD ** -0.5
VMEM_LIMIT_V7X = 48 * 1024 * 1024

C_HQ, C_FF, C_FB, C_HI, C_HG = 0, 1024, 2048, 3072, 4096
C_DQ, C_DK, C_DV, C_MQ = 5120, 6656, 8192, 9728
C_GHG, C_GDA, C_GMEM = 10240, 11264, 12288
IN_COLS = 13312

ADAM_LR, ADAM_B1, ADAM_B2, ADAM_EPS, ADAM_WD, ADAM_STEP = 0.001, 0.9, 0.999, 1e-08, 0.01, 10


def _params(sem, vmem=VMEM_LIMIT_V7X):
    return pltpu.CompilerParams(dimension_semantics=sem, vmem_limit_bytes=vmem)


def _dot(a, b):
    return jnp.dot(a.astype(BF16), b.astype(BF16), preferred_element_type=F32)


def _dot_nt(a, b):
    return lax.dot_general(a.astype(BF16), b.astype(BF16), (((1,), (1,)), ((), ())),
                           preferred_element_type=F32)


def _dot_tn(a, b):
    return lax.dot_general(a.astype(BF16), b.astype(BF16), (((0,), (0,)), ((), ())),
                           preferred_element_type=F32)


def _sigmoid(v):
    return jax.nn.sigmoid(v.astype(F32))


def _ones(rows, cols):
    return (lax.broadcasted_iota(jnp.int32, (rows, cols), 0) >= 0).astype(BF16)


def _lane_sum(v):
    ones = _ones(HEAD, HEAD)
    hi = v.astype(BF16)
    mid = (v - hi.astype(F32)).astype(BF16)
    return jnp.dot(hi, ones, preferred_element_type=F32) + jnp.dot(mid, ones, preferred_element_type=F32)


def _row_mean(v):
    if v.shape[-1] == HEAD:
        return _lane_sum(v) * (1.0 / HEAD)
    return jnp.mean(v, axis=-1, keepdims=True)


def _rms(v):
    v = v.astype(F32)
    r = lax.rsqrt(_row_mean(v * v) + RMS_EPS)
    return v * r, r


def _rms_bwd(dy, xhat, r, gain):
    dxh = dy * gain
    dx = r * (dxh - xhat * _row_mean(dxh * xhat))
    return dx, dy * xhat


class _Carry:
    def __init__(self, arrays, out_shapes, aliases, sems, issue):
        self.arrays, self.out_shapes, self.aliases, self.sems, self.issue = arrays, out_shapes, aliases, sems, issue


NO_CARRY = object()


def _start_copies(copies):
    for parts, _, _ in copies:
        for cp in parts:
            cp.start()


def _wait_copies(copies):
    for _, whole, remote in copies:
        if remote:
            whole.wait_recv()
    for _, whole, remote in copies:
        if remote:
            whole.wait_send()
        else:
            whole.wait()


def _call(body, *, name, grid, in_specs, out_specs, out_shape, scratch_shapes, semantics, ins, carry=None):
    if carry is None:
        res = pl.pallas_call(body, name=name, grid=grid, in_specs=in_specs, out_specs=out_specs, out_shape=out_shape,
                             scratch_shapes=scratch_shapes, compiler_params=_params(semantics))(*ins)
        return list(res), []
    n_in, n_out, n_scr = len(ins), len(out_shape), len(scratch_shapes)
    c_in, c_out = len(carry.arrays), len(carry.out_shapes)

    def wrapped(*refs):
        pos = [0]

        def take(count):
            pos[0] += count
            return refs[pos[0] - count:pos[0]]

        own_in, carry_in = take(n_in), take(c_in)
        own_out, carry_out = take(n_out), take(c_out)
        own_scr, carry_sems = take(n_scr), take(len(carry.sems))
        ids = [pl.program_id(a) for a in range(len(grid))]
        first, last = ids[0] == 0, ids[0] == grid[0] - 1
        for a in range(1, len(grid)):
            first, last = first & (ids[a] == 0), last & (ids[a] == grid[a] - 1)

        @pl.when(first)
        def _():
            _start_copies(carry.issue(carry_in, carry_out, carry_sems, "start"))

        body(*own_in, *own_out, *own_scr)

        @pl.when(last)
        def _():
            _wait_copies(carry.issue(carry_in, carry_out, carry_sems, "wait"))

    res = pl.pallas_call(
        wrapped, name=name, grid=grid, in_specs=list(in_specs) + [ANY] * c_in,
        out_specs=list(out_specs) + [ANY] * c_out, out_shape=list(out_shape) + list(carry.out_shapes),
        input_output_aliases={n_in + i: n_out + j for i, j in carry.aliases.items()},
        scratch_shapes=list(scratch_shapes) + list(carry.sems),
        compiler_params=_params(("arbitrary",) * len(grid)))(*ins, *carry.arrays)
    return list(res[:n_out]), list(res[n_out:])


def _matmul(pairs, mode, out_dtype, *, tm, tn, tk=None, residual=None, carry=NO_CARRY, name):
    b_offs = [pr[2] if len(pr) > 2 else 0 for pr in pairs]
    pairs = [pr[:2] for pr in pairs]
    a0, b0 = pairs[0]
    if mode == "nn":
        (m, kk), n = a0.shape, b0.shape[1]
    elif mode == "nt":
        (m, kk), n = a0.shape, b0.shape[0]
    else:
        (kk, m), n = a0.shape, b0.shape[1]
    assert mode == "nt" or not any(b_offs)
    tm, tn = min(tm, m), min(tn, n)
    tk = kk if tk is None else tk
    nk = kk // tk
    assert m % tm == 0 and n % tn == 0 and kk % tk == 0, (name, m, n, kk)
    n_p = len(pairs)
    if mode == "tn":
        a_spec = pl.BlockSpec((tk, tm), lambda i, j, k: (k, i))
    else:
        a_spec = pl.BlockSpec((tm, tk), lambda i, j, k: (i, k))
    if mode == "nt":
        b_specs = [pl.BlockSpec((tn, tk), lambda i, j, k, o=o: (j, o * nk + k)) for o in b_offs]
    else:
        b_specs = [pl.BlockSpec((tk, tn), lambda i, j, k: (k, j))] * n_p
    o_spec = pl.BlockSpec((tm, tn), lambda i, j, k: (i, j))
    dot = {"nn": _dot, "nt": _dot_nt, "tn": _dot_tn}[mode]
    has_res = residual is not None

    def body(*refs):
        a_refs, b_refs = refs[:n_p], refs[n_p:2 * n_p]
        pos = 2 * n_p
        res_ref = refs[pos] if has_res else None
        pos += int(has_res)
        o_ref = refs[pos]
        part = dot(a_refs[0][...], b_refs[0][...])
        for a_r, b_r in zip(a_refs[1:], b_refs[1:]):
            part += dot(a_r[...], b_r[...])

        def finish(total):
            if has_res:
                total = total + res_ref[...]
            o_ref[...] = total.astype(out_dtype)

        if nk == 1:
            finish(part)
        else:
            acc_ref = refs[pos + 1]
            k = pl.program_id(2)

            @pl.when(k == 0)
            def _():
                acc_ref[...] = part

            @pl.when(k > 0)
            def _():
                acc_ref[...] += part

            @pl.when(k == nk - 1)
            def _():
                finish(acc_ref[...])

    ins = [a for a, _ in pairs] + [b for _, b in pairs]
    in_specs = [a_spec] * n_p + b_specs
    if has_res:
        ins.append(residual)
        in_specs.append(o_spec)
    (out,), carried = _call(
        body, name=name, grid=(m // tm, n // tn, nk), in_specs=in_specs, out_specs=[o_spec],
        out_shape=[jax.ShapeDtypeStruct((m, n), out_dtype)],
        scratch_shapes=[pltpu.VMEM((tm, tn), F32)] if nk > 1 else [],
        semantics=("parallel", "parallel", "arbitrary"), ins=ins, carry=None if carry is NO_CARRY else carry)
    return out if carry is NO_CARRY else (out, carried)


def _rms_fwd(x, gain, *, name):
    rows, dm = x.shape
    tm = min(512, rows)

    def body(x_ref, g_ref, h_ref):
        xhat, _ = _rms(x_ref[...])
        h_ref[...] = (xhat * g_ref[...]).astype(BF16)

    return pl.pallas_call(
        body, name=name, grid=(rows // tm,),
        in_specs=[pl.BlockSpec((tm, dm), lambda i: (i, 0)), pl.BlockSpec((1, dm), lambda i: (0, 0))],
        out_specs=pl.BlockSpec((tm, dm), lambda i: (i, 0)),
        out_shape=jax.ShapeDtypeStruct((rows, dm), BF16),
        compiler_params=_params(("parallel",)),
    )(x, gain)


def _rms_bwd_rows(dh, x, gain, dres, *, name):
    rows, dm = x.shape
    tm = min(512, rows)
    has_res = dres is not None

    def body(*refs):
        dh_ref, x_ref, g_ref = refs[:3]
        res_ref = refs[3] if has_res else None
        dx_ref, dxb_ref, dg_ref = refs[3 + int(has_res):]
        xhat, r = _rms(x_ref[...])
        dx, dgr = _rms_bwd(dh_ref[...], xhat, r, g_ref[...])
        if has_res:
            dx = dx + res_ref[...]
        dx_ref[...] = dx
        dxb_ref[...] = dx.astype(BF16)

        @pl.when(pl.program_id(0) == 0)
        def _():
            dg_ref[...] = jnp.zeros_like(dg_ref)

        dg_ref[...] += jnp.sum(dgr, axis=0, keepdims=True)

    row = pl.BlockSpec((tm, dm), lambda i: (i, 0))
    vec = pl.BlockSpec((1, dm), lambda i: (0, 0))
    return pl.pallas_call(
        body, name=name, grid=(rows // tm,),
        in_specs=[row, row, vec] + ([row] if has_res else []),
        out_specs=[row, row, vec],
        out_shape=[jax.ShapeDtypeStruct((rows, dm), F32), jax.ShapeDtypeStruct((rows, dm), BF16),
                   jax.ShapeDtypeStruct((1, dm), F32)],
        compiler_params=_params(("arbitrary",)),
    )(*([dh, x, gain] + ([dres] if has_res else [])))


def _out_norm(a, w, x, gain, *, name):
    rows, dm = x.shape
    tm = min(512, rows)

    def body(a_ref, w_ref, x_ref, g_ref, x1_ref, h_ref):
        x1 = x_ref[...] + _dot(a_ref[...], w_ref[...])
        x1_ref[...] = x1
        xhat, _ = _rms(x1)
        h_ref[...] = (xhat * g_ref[...]).astype(BF16)

    row = lambda wd: pl.BlockSpec((tm, wd), lambda i: (i, 0))
    return pl.pallas_call(
        body, name=name, grid=(rows // tm,),
        in_specs=[row(a.shape[1]), pl.BlockSpec(w.shape, lambda i: (0, 0)), row(dm), pl.BlockSpec((1, dm), lambda i: (0, 0))],
        out_specs=[row(dm), row(dm)],
        out_shape=[jax.ShapeDtypeStruct((rows, dm), F32), jax.ShapeDtypeStruct((rows, dm), BF16)],
        compiler_params=_params(("parallel",)),
    )(a, w, x, gain)


def _out_loss(u, w, x1, tgt, *, name):
    rows, dm = x1.shape
    tm = min(256, rows)
    steps = rows // tm

    def body(u_ref, w_ref, x_ref, t_ref, dy_ref, dyb_ref, loss_ref, acc_ref):
        i = pl.program_id(0)
        diff = x_ref[...] + _dot(u_ref[...], w_ref[...]) - t_ref[...]
        dy = diff * (1.0 / dm)
        dy_ref[...] = dy
        dyb_ref[...] = dy.astype(BF16)

        @pl.when(i == 0)
        def _():
            acc_ref[...] = jnp.zeros_like(acc_ref)

        acc_ref[...] += jnp.sum(diff * diff, axis=0, keepdims=True)

        @pl.when(i == steps - 1)
        def _():
            loss_ref[...] = jnp.full((1, HEAD), 0.5 / dm, F32) * jnp.sum(acc_ref[...])

    row = lambda wd: pl.BlockSpec((tm, wd), lambda i: (i, 0))
    return pl.pallas_call(
        body, name=name, grid=(steps,),
        in_specs=[row(u.shape[1]), pl.BlockSpec(w.shape, lambda i: (0, 0)), row(dm), row(dm)],
        out_specs=[row(dm), row(dm), pl.BlockSpec((1, HEAD), lambda i: (0, 0))],
        out_shape=[jax.ShapeDtypeStruct((rows, dm), F32), jax.ShapeDtypeStruct((rows, dm), BF16),
                   jax.ShapeDtypeStruct((1, HEAD), F32)],
        scratch_shapes=[pltpu.VMEM((1, dm), F32)],
        compiler_params=_params(("arbitrary",)),
    )(u, w, x1, tgt)


def _gla_block_terms(q_raw, f_logit, lb, rev):
    sig = _sigmoid(f_logit)
    forget = lb + (1.0 - lb) * sig
    k = 1.0 - forget
    b = _chunk_cumsum(jnp.log(forget), rev)
    qs = _sigmoid(q_raw)
    eb = jnp.exp(b)
    emb = jnp.exp(-b)
    qt = (q_raw * qs * HG_SCALE) * eb
    kt = k * emb
    return sig, forget, k, b, qs, eb, emb, qt, kt


def _chunk_cumsum(v, rev):
    n = v.shape[0]
    pos = lax.broadcasted_iota(jnp.int32, v.shape, 0) & (CHUNK - 1)
    step = 1
    while step < CHUNK:
        if rev:
            shifted, keep = pltpu.roll(v, n - step, 0), pos < CHUNK - step
        else:
            shifted, keep = pltpu.roll(v, step, 0), pos >= step
        v = v + jnp.where(keep, shifted, 0.0)
        step *= 2
    return v


def _tri_mask(n, rev):
    row, col = np.arange(n)[:, None], np.arange(n)[None, :]
    same = (row // CHUNK) == (col // CHUNK)
    return jnp.asarray((same & ((row <= col) if rev else (row >= col))).astype(np.float32))


def _chunk_order(ncb, rev):
    order = range(ncb - 1, -1, -1) if rev else range(ncb)
    return [(c, c * CHUNK if rev else c * CHUNK + CHUNK - 1) for c in order]


def _gla_fwd(proj, lb, *, f_off, rev, name):
    rows = proj.shape[0]
    tb = min(256, rows)
    nb, ncb = rows // tb, tb // CHUNK

    def tmap(n):
        return nb - 1 - n if rev else n

    def body(tri_ref, q_ref, f_ref, v_ref, lb_ref, o_ref, st_ref, s_ref):
        @pl.when(pl.program_id(1) == 0)
        def _():
            s_ref[...] = jnp.zeros_like(s_ref)

        tri = tri_ref[...] > 0.5
        for hh in range(GLA_HEADS_PER_STEP):
            cs = slice(hh * HEAD, (hh + 1) * HEAD)
            v = v_ref[:, cs]
            _, _, k, b, _, _, _, qt, kt = _gla_block_terms(q_ref[:, cs].astype(F32), f_ref[:, cs].astype(F32),
                                                           lb_ref[:, cs], rev)
            o_intra = _dot(jnp.where(tri, _dot_nt(qt, kt), 0.0), v)
            chunks = []
            for c, last in _chunk_order(ncb, rev):
                sl = slice(c * CHUNK, (c + 1) * CHUNK)
                bl = b[last:last + 1, :]
                kh = k[sl] * jnp.exp(bl - b[sl])
                chunks.append((c, sl, jnp.exp(bl), _dot_tn(v[sl], kh)))
            s_t = s_ref[hh]
            for c, sl, ebl, kv in chunks:
                st_ref[hh, c] = s_t
                o_ref[sl, cs] = o_intra[sl] + _dot_nt(qt[sl], s_t)
                s_t = ebl * s_t + kv
            s_ref[hh] = s_t

    hps = GLA_HEADS_PER_STEP
    col = lambda off: pl.BlockSpec((tb, hps * HEAD), lambda h, n: (tmap(n), off // (hps * HEAD) + h))
    return pl.pallas_call(
        body, name=name, grid=(HG_HEADS // hps, nb),
        in_specs=[pl.BlockSpec((tb, tb), lambda h, n: (0, 0)), col(C_HQ), col(f_off), col(C_HI),
                  pl.BlockSpec((1, hps * HEAD), lambda h, n: (0, h))],
        out_specs=[pl.BlockSpec((tb, hps * HEAD), lambda h, n: (tmap(n), h)),
                   pl.BlockSpec((hps, ncb, HEAD, HEAD), lambda h, n: (h, tmap(n), 0, 0))],
        out_shape=[jax.ShapeDtypeStruct((rows, HG_HEADS * HEAD), F32),
                   jax.ShapeDtypeStruct((HG_HEADS, rows // CHUNK, HEAD, HEAD), F32)],
        scratch_shapes=[pltpu.VMEM((hps, HEAD, HEAD), F32)],
        compiler_params=_params(("parallel", "arbitrary")),
    )(_tri_mask(tb, rev), proj, proj, proj, lb)


def _gla_bwd(proj, lb, do, states, prev, *, f_off, rev, carry=NO_CARRY, name):
    rows = proj.shape[0]
    tb = min(256, rows)
    nb, ncb = rows // tb, tb // CHUNK
    has_prev = prev is not None
    qv_dtype = BF16 if has_prev else F32

    def tmap(n):
        return n if rev else nb - 1 - n

    def body(*refs):
        tri_ref, q_ref, f_ref, v_ref, lb_ref, do_ref, st_ref = refs[:7]
        pq_ref, pv_ref = refs[7:9] if has_prev else (None, None)
        (dq_ref, df_ref, dv_ref, dlb_ref,
         ds_ref, dqt_scr, dk_scr, db_scr, dbl_scr, dv_scr) = refs[7 + 2 * int(has_prev):]

        @pl.when(pl.program_id(1) == 0)
        def _():
            ds_ref[...] = jnp.zeros_like(ds_ref)
            dlb_ref[...] = jnp.zeros_like(dlb_ref)

        tri = tri_ref[...] > 0.5
        for hh in range(GLA_HEADS_PER_STEP):
            cs = slice(hh * HEAD, (hh + 1) * HEAD)
            lbv = lb_ref[:, cs]
            q_raw = q_ref[:, cs].astype(F32)
            v, dout = v_ref[:, cs], do_ref[:, cs].astype(BF16)
            sig, forget, k, b, qs, eb, emb, qt, kt = _gla_block_terms(q_raw, f_ref[:, cs].astype(F32), lbv, rev)
            a = jnp.where(tri, _dot_nt(qt, kt), 0.0)
            da = jnp.where(tri, _dot_nt(dout, v), 0.0)
            dv_intra = _dot_tn(a, dout)
            dqt_intra = _dot(da, kt)
            dkt = _dot_tn(da, qt)
            chunks = []
            for c, last in reversed(_chunk_order(ncb, rev)):
                sl = slice(c * CHUNK, (c + 1) * CHUNK)
                bl = b[last:last + 1, :]
                e = jnp.exp(bl - b[sl])
                s_t = st_ref[hh, c]
                dqt_scr[sl, cs] = dqt_intra[sl] + _dot(dout[sl], s_t)
                chunks.append((sl, jnp.exp(bl), e, k[sl] * e, s_t, _dot_tn(dout[sl], qt[sl])))
            ds_t = ds_ref[hh]
            for sl, ebl, e, kh, s_t, grow in chunks:
                dkh = _dot(v[sl], ds_t)
                dv_scr[sl, cs] = dv_intra[sl] + _dot_nt(kh, ds_t)
                dk_scr[sl, cs] = dkt[sl] * emb[sl] + dkh * e
                khd = kh * dkh
                dbl = jnp.sum(khd, axis=0, keepdims=True) + ebl * jnp.sum(ds_t * s_t, axis=0, keepdims=True)
                db_scr[sl, cs] = khd
                dbl_scr[sl, cs] = jnp.broadcast_to(dbl, (CHUNK, HEAD))
                ds_t = grow + ds_t * ebl
            ds_ref[hh] = ds_t
            dqt = dqt_scr[:, cs]
            dlogf = _chunk_cumsum(qt * dqt - kt * dkt - db_scr[:, cs], not rev) + dbl_scr[:, cs]
            dforget = dlogf / forget - dk_scr[:, cs]
            df_ref[:, cs] = (dforget * (1.0 - lbv) * sig * (1.0 - sig)).astype(BF16)
            dlb_ref[:, cs] += jnp.sum(dforget * (1.0 - sig), axis=0, keepdims=True)
            dqr = dqt * eb * (HG_SCALE * qs * (1.0 + q_raw * (1.0 - qs)))
            dv = dv_scr[:, cs]
            if has_prev:
                dqr = dqr + pq_ref[:, cs]
                dv = dv + pv_ref[:, cs]
            dq_ref[:, cs] = dqr.astype(qv_dtype)
            dv_ref[:, cs] = dv.astype(qv_dtype)

    hps = GLA_HEADS_PER_STEP
    col = lambda off: pl.BlockSpec((tb, hps * HEAD), lambda h, n: (tmap(n), off // (hps * HEAD) + h))
    blk = pl.BlockSpec((tb, hps * HEAD), lambda h, n: (tmap(n), h))
    vec = pl.BlockSpec((1, hps * HEAD), lambda h, n: (0, h))
    wide = HG_HEADS * HEAD
    outs, carried = _call(
        body, name=name, grid=(HG_HEADS // hps, nb),
        in_specs=[pl.BlockSpec((tb, tb), lambda h, n: (0, 0)), col(C_HQ), col(f_off), col(C_HI), vec, blk,
                  pl.BlockSpec((hps, ncb, HEAD, HEAD), lambda h, n: (h, tmap(n), 0, 0))]
                 + ([blk, blk] if has_prev else []),
        out_specs=[blk, blk, blk, vec],
        out_shape=[jax.ShapeDtypeStruct((rows, wide), qv_dtype), jax.ShapeDtypeStruct((rows, wide), BF16),
                   jax.ShapeDtypeStruct((rows, wide), qv_dtype), jax.ShapeDtypeStruct((1, wide), F32)],
        scratch_shapes=[pltpu.VMEM((hps, HEAD, HEAD), F32)] + [pltpu.VMEM((tb, hps * HEAD), F32)] * 5,
        semantics=("parallel", "arbitrary"),
        ins=[_tri_mask(tb, rev), proj, proj, proj, lb, do, states] + (list(prev) if has_prev else []),
        carry=None if carry is NO_CARRY else carry)
    return outs if carry is NO_CARRY else (outs, carried)


def _hg_out_fwd(o_fw, o_bw, proj, gain, *, name):
    rows = o_fw.shape[0]
    tm = min(512, rows)
    wide = HG_HEADS * HEAD

    def body(a_ref, b_ref, g_ref, gain_ref, o_ref):
        for h in range(HG_HEADS):
            sl = slice(h * HEAD, (h + 1) * HEAD)
            xhat, _ = _rms(a_ref[:, sl] + b_ref[:, sl])
            gate = g_ref[:, sl].astype(F32)
            o_ref[:, sl] = (xhat * gain_ref[...] * (gate * _sigmoid(gate))).astype(BF16)

    row = pl.BlockSpec((tm, wide), lambda i: (i, 0))
    return pl.pallas_call(
        body, name=name, grid=(rows // tm,),
        in_specs=[row, row, pl.BlockSpec((tm, wide), lambda i: (i, C_HG // wide)),
                  pl.BlockSpec((1, HEAD), lambda i: (0, 0))],
        out_specs=row, out_shape=jax.ShapeDtypeStruct((rows, wide), BF16),
        compiler_params=_params(("parallel",)),
    )(o_fw, o_bw, proj, gain)


def _hg_out_bwd(dout, o_fw, o_bw, proj, gain, *, name):
    rows = o_fw.shape[0]
    tm = min(512, rows)
    wide = HG_HEADS * HEAD

    def body(d_ref, a_ref, b_ref, g_ref, gain_ref, dgate_ref, do_ref, dgain_ref):
        @pl.when(pl.program_id(0) == 0)
        def _():
            dgain_ref[...] = jnp.zeros_like(dgain_ref)

        dgain = jnp.zeros((1, HEAD), F32)
        for h in range(HG_HEADS):
            sl = slice(h * HEAD, (h + 1) * HEAD)
            xhat, r = _rms(a_ref[:, sl] + b_ref[:, sl])
            gate, dy = g_ref[:, sl].astype(F32), d_ref[:, sl]
            gs = _sigmoid(gate)
            dgate_ref[:, sl] = (dy * xhat * gain_ref[...] * (gs * (1.0 + gate * (1.0 - gs)))).astype(BF16)
            dx, dgr = _rms_bwd(dy * (gate * gs), xhat, r, gain_ref[...])
            do_ref[:, sl] = dx
            dgain = dgain + jnp.sum(dgr, axis=0, keepdims=True)
        dgain_ref[...] += dgain

    row = pl.BlockSpec((tm, wide), lambda i: (i, 0))
    vec = pl.BlockSpec((1, HEAD), lambda i: (0, 0))
    return pl.pallas_call(
        body, name=name, grid=(rows // tm,),
        in_specs=[row, row, row, pl.BlockSpec((tm, wide), lambda i: (i, C_HG // wide)), vec],
        out_specs=[row, row, vec],
        out_shape=[jax.ShapeDtypeStruct((rows, wide), BF16), jax.ShapeDtypeStruct((rows, wide), F32),
                   jax.ShapeDtypeStruct((1, HEAD), F32)],
        compiler_params=_params(("arbitrary",)),
    )(dout, o_fw, o_bw, proj, gain)


def _strided_rows(ref, r, count, d):
    return ref[...] if d == 1 else ref[pl.ds(r, count, stride=d), :]


def _store_strided(ref, r, count, d, val):
    if d == 1:
        ref[...] = val
    else:
        ref[pl.ds(r, count, stride=d), :] = val


def _da_prep(proj, q_gain, k_gain, g, *, name):
    d = DA_GROUPS[g][0]
    rows = proj.shape[0]
    rb = min(512, rows)
    tn = rb // d

    def body(q_ref, k_ref, v_ref, qg_ref, kg_ref, qo_ref, ko_ref, vo_ref, qf_ref, kf_ref, vf_ref):
        for h in range(DA_GH):
            cs = slice(h * HEAD, (h + 1) * HEAD)
            for src, dst in ((q_ref, qf_ref), (k_ref, kf_ref), (v_ref, vf_ref)):
                dst[...] = src[:, cs].astype(F32)
            for r in range(d):
                qhat, _ = _rms(_strided_rows(qf_ref, r, tn, d))
                khat, _ = _rms(_strided_rows(kf_ref, r, tn, d))
                qo_ref[h, r] = (qhat * qg_ref[...]).astype(BF16)
                ko_ref[h, r] = (khat * kg_ref[...]).astype(BF16)
                vo_ref[h, r] = _strided_rows(vf_ref, r, tn, d).astype(BF16)

    wide = DA_GH * HEAD
    col = lambda off: pl.BlockSpec((rb, wide), lambda i: (i, off // wide + g))
    vec = pl.BlockSpec((1, HEAD), lambda i: (0, 0))
    out = pl.BlockSpec((DA_GH, d, tn, HEAD), lambda i: (0, 0, i, 0))
    shape = jax.ShapeDtypeStruct((DA_GH, d, rows // d, HEAD), BF16)
    return pl.pallas_call(
        body, name=name, grid=(rows // rb,),
        in_specs=[col(C_DQ), col(C_DK), col(C_DV), vec, vec],
        out_specs=[out, out, out], out_shape=[shape, shape, shape],
        scratch_shapes=[pltpu.VMEM((rb, HEAD), F32)] * 3,
        compiler_params=_params(("parallel",)),
    )(proj, proj, proj, q_gain, k_gain)


def _slopes(g):
    idx = np.arange(g * DA_GH + 1, (g + 1) * DA_GH + 1)
    s = (2.0 ** (-8.0 * idx / (DA_GH * len(DA_GROUPS)))).astype(np.float32)
    return jnp.asarray(np.broadcast_to(s[:, None, None], (DA_GH, 8, HEAD)).copy())


def _band_window(ld, t, radius):
    win = min(2 * t, ld)
    assert t // 2 >= radius or win == ld
    return win


def _band_scores(q, k, q0, start, slope, d, radius):
    t, win = q.shape[0], k.shape[0]
    row = lax.broadcasted_iota(jnp.int32, (t, win), 0)
    col = lax.broadcasted_iota(jnp.int32, (t, win), 1)
    rel = jnp.abs((start - q0) + col - row)
    return _dot_nt(q, k) * ATT_SCALE + jnp.where(rel <= radius, -slope * (d * rel).astype(F32), NEG_INF)


def _band_fwd(qr, kr, vr, g, *, name):
    d, radius = DA_GROUPS[g]
    _, _, ld, _ = qr.shape
    t = min(HEAD, ld)
    qb = min(BAND_QBLOCKS, ld // t)
    win = _band_window(ld, t, radius)

    def body(q_ref, k_ref, v_ref, sl_ref, o_ref, lse_ref):
        i = pl.program_id(2)
        slope = sl_ref[0:1, 0:1]
        for j in range(qb):
            sl = slice(j * t, (j + 1) * t)
            q0 = (i * qb + j) * t
            start = pl.multiple_of(jnp.clip(q0 - t // 2, 0, ld - win), t // 2)
            s = _band_scores(q_ref[sl, :], k_ref[pl.ds(start, win), :], q0, start, slope, d, radius)
            m = jnp.max(s, axis=-1, keepdims=True)
            p = jnp.exp(s - m).astype(BF16)
            l = jnp.dot(p, _ones(win, HEAD), preferred_element_type=F32)
            o_ref[sl, :] = _dot(p, v_ref[pl.ds(start, win), :]) / l
            lse_ref[sl, :] = m + jnp.log(l)

    own = pl.BlockSpec((None, None, qb * t, HEAD), lambda h, r, i: (h, r, i, 0))
    seq = pl.BlockSpec((None, None, ld, HEAD), lambda h, r, i: (h, r, 0, 0))
    shape = jax.ShapeDtypeStruct(qr.shape, F32)
    return pl.pallas_call(
        body, name=name, grid=(DA_GH, d, ld // (qb * t)),
        in_specs=[own, seq, seq, pl.BlockSpec((None, 8, HEAD), lambda h, r, i: (h, 0, 0))],
        out_specs=[own, own], out_shape=[shape, shape],
        compiler_params=_params(("parallel", "parallel", "parallel")),
    )(qr, kr, vr, _slopes(g))


def _band_bwd(qr, kr, vr, dor, lser, deltar, g, *, name):
    d, radius = DA_GROUPS[g]
    _, _, ld, _ = qr.shape
    t = min(HEAD, ld)
    qb = min(BAND_QBLOCKS, ld // t)
    win = _band_window(ld, t, radius)

    def body(q_ref, k_ref, v_ref, do_ref, lse_ref, dl_ref, sl_ref, dq_ref, dk_ref, dv_ref):
        i = pl.program_id(2)

        @pl.when(i == 0)
        def _():
            dk_ref[...] = jnp.zeros_like(dk_ref)
            dv_ref[...] = jnp.zeros_like(dv_ref)

        slope = sl_ref[0:1, 0:1]
        for j in range(qb):
            sl = slice(j * t, (j + 1) * t)
            q0 = (i * qb + j) * t
            start = pl.multiple_of(jnp.clip(q0 - t // 2, 0, ld - win), t // 2)
            rows = pl.ds(start, win)
            q, dout, k, v = q_ref[sl, :], do_ref[sl, :], k_ref[rows, :], v_ref[rows, :]
            p = jnp.exp(_band_scores(q, k, q0, start, slope, d, radius) - lse_ref[sl, 0:1])
            ds = p * (_dot_nt(dout, v) - dl_ref[sl, 0:1])
            dq_ref[sl, :] = _dot(ds, k) * ATT_SCALE
            dk_ref[rows, :] += _dot_tn(ds, q) * ATT_SCALE
            dv_ref[rows, :] += _dot_tn(p, dout)

    own = pl.BlockSpec((None, None, qb * t, HEAD), lambda h, r, i: (h, r, i, 0))
    seq = pl.BlockSpec((None, None, ld, HEAD), lambda h, r, i: (h, r, 0, 0))
    shape = jax.ShapeDtypeStruct(qr.shape, F32)
    return pl.pallas_call(
        body, name=name, grid=(DA_GH, d, ld // (qb * t)),
        in_specs=[own, seq, seq, own, own, own, pl.BlockSpec((None, 8, HEAD), lambda h, r, i: (h, 0, 0))],
        out_specs=[own, seq, seq], out_shape=[shape, shape, shape],
        compiler_params=_params(("parallel", "parallel", "arbitrary")),
    )(qr, kr, vr, dor, lser, deltar, _slopes(g))


def _da_merge(outs, lses, rows, *, name):
    rb = min(512, rows)
    wide = DA_GH * HEAD

    def body(*refs):
        o_refs, l_refs = refs[0:3], refs[3:6]
        o_ref, lse_ref = refs[6:8]
        on_refs, ln_refs = refs[8:11], refs[11:14]
        for h in range(DA_GH):
            cs = slice(h * HEAD, (h + 1) * HEAD)
            for g, (d, _) in enumerate(DA_GROUPS):
                tn = rb // d
                for r in range(d):
                    _store_strided(on_refs[g], r, tn, d, o_refs[g][h, r])
                    _store_strided(ln_refs[g], r, tn, d, l_refs[g][h, r])
            l0, l1, l2 = ln_refs[0][...], ln_refs[1][...], ln_refs[2][...]
            m = jnp.maximum(jnp.maximum(l0, l1), l2)
            e0, e1, e2 = jnp.exp(l0 - m), jnp.exp(l1 - m), jnp.exp(l2 - m)
            tot = e0 + e1 + e2
            o_ref[:, cs] = (e0 * on_refs[0][...] + e1 * on_refs[1][...] + e2 * on_refs[2][...]) / tot
            lse_ref[:, cs] = m + jnp.log(tot)

    res = lambda d: pl.BlockSpec((DA_GH, d, rb // d, HEAD), lambda i: (0, 0, i, 0))
    nat = pl.BlockSpec((rb, wide), lambda i: (i, 0))
    shape = jax.ShapeDtypeStruct((rows, wide), F32)
    return pl.pallas_call(
        body, name=name, grid=(rows // rb,),
        in_specs=[res(d) for d, _ in DA_GROUPS] * 2,
        out_specs=[nat, nat], out_shape=[shape, shape],
        scratch_shapes=[pltpu.VMEM((rb, HEAD), F32)] * 6,
        compiler_params=_params(("parallel",)),
    )(*outs, *lses)


def _da_bwd_prep(dout, o, lse, *, name):
    rows = o.shape[0]
    rb = min(512, rows)
    wide = DA_GH * HEAD

    def body(d_ref, o_ref, l_ref, *outs):
        d_scr, l_scr, delta_scr = outs[9:12]
        for h in range(DA_GH):
            cs = slice(h * HEAD, (h + 1) * HEAD)
            dv = d_ref[:, cs]
            d_scr[...] = dv
            l_scr[...] = l_ref[:, cs]
            delta_scr[...] = _lane_sum(dv * o_ref[:, cs])
            for g, (d, _) in enumerate(DA_GROUPS):
                tn = rb // d
                for r in range(d):
                    outs[3 * g][h, r] = _strided_rows(d_scr, r, tn, d).astype(BF16)
                    outs[3 * g + 1][h, r] = _strided_rows(l_scr, r, tn, d)
                    outs[3 * g + 2][h, r] = _strided_rows(delta_scr, r, tn, d)

    nat = pl.BlockSpec((rb, wide), lambda i: (i, 0))
    out_specs, out_shape = [], []
    for d, _ in DA_GROUPS:
        for dt in (BF16, F32, F32):
            out_specs.append(pl.BlockSpec((DA_GH, d, rb // d, HEAD), lambda i: (0, 0, i, 0)))
            out_shape.append(jax.ShapeDtypeStruct((DA_GH, d, rows // d, HEAD), dt))
    return pl.pallas_call(
        body, name=name, grid=(rows // rb,),
        in_specs=[nat, nat, nat], out_specs=out_specs, out_shape=out_shape,
        scratch_shapes=[pltpu.VMEM((rb, HEAD), F32)] * 3,
        compiler_params=_params(("parallel",)),
    )(dout, o, lse)


def _da_prep_bwd(dqr, dkr, dvr, proj, q_gain, k_gain, g, *, name):
    d = DA_GROUPS[g][0]
    rows = proj.shape[0]
    rb = min(512, rows)
    tn = rb // d
    wide = DA_GH * HEAD

    def body(dq_ref, dk_ref, dv_ref, q_ref, k_ref, qg_ref, kg_ref, oq_ref, ok_ref, ov_ref, gq_ref, gk_ref, *nat_refs):
        @pl.when(pl.program_id(0) == 0)
        def _():
            gq_ref[...] = jnp.zeros_like(gq_ref)
            gk_ref[...] = jnp.zeros_like(gk_ref)

        for h in range(DA_GH):
            cs = slice(h * HEAD, (h + 1) * HEAD)
            for j, src in enumerate((dq_ref, dk_ref, dv_ref)):
                for r in range(d):
                    _store_strided(nat_refs[j], r, tn, d, src[h, r])
            ov_ref[:, cs] = nat_refs[2][...].astype(BF16)
            for j, (x_ref, gn_ref, out_ref, acc_ref) in enumerate(((q_ref, qg_ref, oq_ref, gq_ref),
                                                                    (k_ref, kg_ref, ok_ref, gk_ref))):
                xhat, r = _rms(x_ref[:, cs])
                dx, dgr = _rms_bwd(nat_refs[j][...], xhat, r, gn_ref[...])
                out_ref[:, cs] = dx.astype(BF16)
                acc_ref[...] += jnp.sum(dgr, axis=0, keepdims=True)

    res = pl.BlockSpec((DA_GH, d, tn, HEAD), lambda i: (0, 0, i, 0))
    col = lambda off: pl.BlockSpec((rb, wide), lambda i: (i, off // wide + g))
    vec = pl.BlockSpec((1, HEAD), lambda i: (0, 0))
    nat = pl.BlockSpec((rb, wide), lambda i: (i, 0))
    shape = jax.ShapeDtypeStruct((rows, wide), BF16)
    gshape = jax.ShapeDtypeStruct((1, HEAD), F32)
    return pl.pallas_call(
        body, name=name, grid=(rows // rb,),
        in_specs=[res, res, res, col(C_DQ), col(C_DK), vec, vec],
        out_specs=[nat, nat, nat, vec, vec], out_shape=[shape, shape, shape, gshape, gshape],
        scratch_shapes=[pltpu.VMEM((rb, HEAD), F32)] * 3,
        compiler_params=_params(("arbitrary",)),
    )(dqr, dkr, dvr, proj, proj, q_gain, k_gain)


def _mem_fwd(proj, kv, q_gain, k_gain, *, name):
    rows = proj.shape[0]
    tm = min(512, rows)
    n_mem = kv.shape[0]

    def body(q_ref, k_ref, v_ref, qg_ref, kg_ref, o_ref):
        qhat, _ = _rms(q_ref[...])
        khat, _ = _rms(k_ref[...])
        s = _dot_nt(qhat * qg_ref[...], khat * kg_ref[...]) * ATT_SCALE
        p = jnp.exp(s - jnp.max(s, axis=-1, keepdims=True))
        p = p / jnp.sum(p, axis=-1, keepdims=True)
        o_ref[...] = _dot(p, v_ref[...]).astype(BF16)

    vec = pl.BlockSpec((1, HEAD), lambda i, h: (0, 0))
    return pl.pallas_call(
        body, name=name, grid=(rows // tm, MEM_HEADS),
        in_specs=[pl.BlockSpec((tm, HEAD), lambda i, h: (i, C_MQ // HEAD + h)),
                  pl.BlockSpec((n_mem, HEAD), lambda i, h: (0, h)),
                  pl.BlockSpec((n_mem, HEAD), lambda i, h: (0, MEM_HEADS + h)), vec, vec],
        out_specs=pl.BlockSpec((tm, HEAD), lambda i, h: (i, h)),
        out_shape=jax.ShapeDtypeStruct((rows, MEM_HEADS * HEAD), BF16),
        compiler_params=_params(("parallel", "parallel")),
    )(proj, kv, kv, q_gain, k_gain)


def _mem_bwd(dout, proj, kv, q_gain, k_gain, *, name):
    rows = proj.shape[0]
    tm = min(512, rows)
    steps = rows // tm
    n_mem = kv.shape[0]

    def body(d_ref, q_ref, k_ref, v_ref, qg_ref, kg_ref, dq_ref, dk_ref, dv_ref, gq_ref, gk_ref, dkn_ref):
        h, i = pl.program_id(0), pl.program_id(1)

        @pl.when((h == 0) & (i == 0))
        def _():
            gq_ref[...] = jnp.zeros_like(gq_ref)
            gk_ref[...] = jnp.zeros_like(gk_ref)

        @pl.when(i == 0)
        def _():
            dkn_ref[...] = jnp.zeros_like(dkn_ref)
            dv_ref[...] = jnp.zeros_like(dv_ref)

        qhat, rq = _rms(q_ref[...])
        khat, rk = _rms(k_ref[...])
        qn, kn = qhat * qg_ref[...], khat * kg_ref[...]
        s = _dot_nt(qn, kn) * ATT_SCALE
        p = jnp.exp(s - jnp.max(s, axis=-1, keepdims=True))
        p = p / jnp.sum(p, axis=-1, keepdims=True)
        dout = d_ref[...]
        dp = _dot_nt(dout, v_ref[...])
        ds = p * (dp - jnp.sum(p * dp, axis=-1, keepdims=True))
        dv_ref[...] += _dot_tn(p, dout)
        dkn_ref[...] += _dot_tn(ds, qn) * ATT_SCALE
        dq, dgr = _rms_bwd(_dot(ds, kn) * ATT_SCALE, qhat, rq, qg_ref[...])
        dq_ref[...] = dq.astype(BF16)
        gq_ref[...] += jnp.sum(dgr, axis=0, keepdims=True)

        @pl.when(i == steps - 1)
        def _():
            dk, dgk = _rms_bwd(dkn_ref[...], khat, rk, kg_ref[...])
            dk_ref[...] = dk
            gk_ref[...] += jnp.sum(dgk, axis=0, keepdims=True)

    vec = pl.BlockSpec((1, HEAD), lambda h, i: (0, 0))
    memh = pl.BlockSpec((n_mem, HEAD), lambda h, i: (0, h))
    tok = pl.BlockSpec((tm, HEAD), lambda h, i: (i, h))
    gshape = jax.ShapeDtypeStruct((1, HEAD), F32)
    return pl.pallas_call(
        body, name=name, grid=(MEM_HEADS, steps),
        in_specs=[tok, pl.BlockSpec((tm, HEAD), lambda h, i: (i, C_MQ // HEAD + h)), memh,
                  pl.BlockSpec((n_mem, HEAD), lambda h, i: (0, MEM_HEADS + h)), vec, vec],
        out_specs=[tok, memh, memh, vec, vec],
        out_shape=[jax.ShapeDtypeStruct((rows, MEM_HEADS * HEAD), BF16),
                   jax.ShapeDtypeStruct((n_mem, MEM_HEADS * HEAD), F32),
                   jax.ShapeDtypeStruct((n_mem, MEM_HEADS * HEAD), F32), gshape, gshape],
        scratch_shapes=[pltpu.VMEM((n_mem, HEAD), F32)],
        compiler_params=_params(("arbitrary", "arbitrary")),
    )(dout, proj, kv, kv, q_gain, k_gain)


def _branch_fwd(o_hg, o_da, o_mem, proj, wp_hg, wp_da, wp_mem, *, name):
    rows = o_hg.shape[0]
    tm = min(256, rows)

    def body(a_ref, b_ref, c_ref, ga_ref, gb_ref, gc_ref, wa_ref, wb_ref, wc_ref, o_ref):
        merged = _sigmoid(ga_ref[...]) * _dot(a_ref[...], wa_ref[...])
        merged += _sigmoid(gb_ref[...]) * _dot(b_ref[...], wb_ref[...])
        merged += _sigmoid(gc_ref[...]) * _dot(c_ref[...], wc_ref[...])
        o_ref[...] = merged.astype(BF16)

    row = lambda w: pl.BlockSpec((tm, w), lambda i: (i, 0))
    gate = lambda off: pl.BlockSpec((tm, D_MODEL), lambda i: (i, off // D_MODEL))
    full = lambda a: pl.BlockSpec(a.shape, lambda i: (0, 0))
    return pl.pallas_call(
        body, name=name, grid=(rows // tm,),
        in_specs=[row(o_hg.shape[1]), row(o_da.shape[1]), row(o_mem.shape[1]),
                  gate(C_GHG), gate(C_GDA), gate(C_GMEM), full(wp_hg), full(wp_da), full(wp_mem)],
        out_specs=row(D_MODEL), out_shape=jax.ShapeDtypeStruct((rows, D_MODEL), BF16),
        compiler_params=_params(("parallel",)),
    )(o_hg, o_da, o_mem, proj, proj, proj, wp_hg, wp_da, wp_mem)


def _branch_bwd(dm, o_hg, o_da, o_mem, proj, wp_hg, wp_da, wp_mem, *, name):
    rows = o_hg.shape[0]
    tm = min(256, rows)

    def body(dm_ref, a_ref, b_ref, c_ref, ga_ref, gb_ref, gc_ref, wa_ref, wb_ref, wc_ref, *outs):
        dmv = dm_ref[...]
        for j, (o_ref, g_ref, w_ref) in enumerate(((a_ref, ga_ref, wa_ref), (b_ref, gb_ref, wb_ref),
                                                   (c_ref, gc_ref, wc_ref))):
            z = _dot(o_ref[...], w_ref[...])
            gs = _sigmoid(g_ref[...])
            dz = (dmv * gs).astype(BF16)
            outs[3 * j][...] = (dmv * z * gs * (1.0 - gs)).astype(BF16)
            outs[3 * j + 1][...] = dz
            outs[3 * j + 2][...] = _dot_nt(dz, w_ref[...])

    row = lambda w: pl.BlockSpec((tm, w), lambda i: (i, 0))
    gate = lambda off: pl.BlockSpec((tm, D_MODEL), lambda i: (i, off // D_MODEL))
    full = lambda a: pl.BlockSpec(a.shape, lambda i: (0, 0))
    out_specs, out_shape = [], []
    for o in (o_hg, o_da, o_mem):
        out_specs += [row(D_MODEL), row(D_MODEL), row(o.shape[1])]
        out_shape += [jax.ShapeDtypeStruct((rows, D_MODEL), BF16), jax.ShapeDtypeStruct((rows, D_MODEL), BF16),
                      jax.ShapeDtypeStruct((rows, o.shape[1]), F32)]
    return pl.pallas_call(
        body, name=name, grid=(rows // tm,),
        in_specs=[row(D_MODEL), row(o_hg.shape[1]), row(o_da.shape[1]), row(o_mem.shape[1]),
                  gate(C_GHG), gate(C_GDA), gate(C_GMEM), full(wp_hg), full(wp_da), full(wp_mem)],
        out_specs=out_specs, out_shape=out_shape,
        compiler_params=_params(("parallel",)),
    )(dm, o_hg, o_da, o_mem, proj, proj, proj, wp_hg, wp_da, wp_mem)


def _ffn_in(h2, w_ab, *, name):
    rows, dff = h2.shape[0], w_ab.shape[1] // 2
    tm, tn = min(2048, rows), 256

    def body(h_ref, wa_ref, wb_ref, a_ref, b_ref, u_ref):
        a = _dot(h_ref[...], wa_ref[...])
        b = _dot(h_ref[...], wb_ref[...])
        a_ref[...] = a.astype(BF16)
        b_ref[...] = b.astype(BF16)
        u_ref[...] = (a * _sigmoid(a) * b).astype(BF16)

    out = pl.BlockSpec((tm, tn), lambda i, j: (i, j))
    return pl.pallas_call(
        body, name=name, grid=(rows // tm, dff // tn),
        in_specs=[pl.BlockSpec((tm, D_MODEL), lambda i, j: (i, 0)),
                  pl.BlockSpec((D_MODEL, tn), lambda i, j: (0, j)),
                  pl.BlockSpec((D_MODEL, tn), lambda i, j: (0, dff // tn + j))],
        out_specs=[out, out, out],
        out_shape=[jax.ShapeDtypeStruct((rows, dff), BF16)] * 3,
        compiler_params=_params(("parallel", "parallel")),
    )(h2, w_ab, w_ab)


def _ffn_act_bwd(dy, w_out, a, b, *, name):
    rows, dff = a.shape
    tm, tn = min(1024, rows), 256

    def body(dy_ref, w_ref, a_ref, b_ref, da_ref, db_ref):
        du = _dot_nt(dy_ref[...], w_ref[...])
        av, bv = a_ref[...].astype(F32), b_ref[...].astype(F32)
        sa = _sigmoid(av)
        da_ref[...] = (du * bv * sa * (1.0 + av * (1.0 - sa))).astype(BF16)
        db_ref[...] = (du * av * sa).astype(BF16)

    tile = pl.BlockSpec((tm, tn), lambda i, j: (i, j))
    return pl.pallas_call(
        body, name=name, grid=(rows // tm, dff // tn),
        in_specs=[pl.BlockSpec((tm, D_MODEL), lambda i, j: (i, 0)),
                  pl.BlockSpec((tn, D_MODEL), lambda i, j: (j, 0)), tile, tile],
        out_specs=[tile, tile],
        out_shape=[jax.ShapeDtypeStruct((rows, dff), BF16), jax.ShapeDtypeStruct((rows, dff), BF16)],
        compiler_params=_params(("parallel", "parallel")),
    )(dy, w_out, a, b)


def _lower_bound(lb_fw, lb_bw, *, name):
    def body(a_ref, b_ref, oa_ref, ob_ref):
        for src, dst in ((a_ref, oa_ref), (b_ref, ob_ref)):
            dst[...] = _sigmoid(src[0:1, :] - src[1:2, :])

    shape = jax.ShapeDtypeStruct((1, lb_fw.shape[1]), F32)
    return pl.pallas_call(body, name=name, out_shape=[shape, shape])(lb_fw, lb_bw)


def _local_step(x, mem, tgt, p, ex):
    rows = x.shape[0]
    lb_fw, lb_bw = _lower_bound(p["lb_logits_fw"], p["lb_logits_bw"], name="lower_bound")

    h = _rms_fwd(x, p["norm_mix_gain"], name="norm_mix")
    w = {"w_in": ex.w_in}
    proj, carried = _matmul([(h, w["w_in"])], "nn", BF16, tm=1024, tn=1024, carry=ex.late_carry(), name="proj_in")
    w.update(ex.late_weights(carried))
    o_fw, st_fw = _gla_fwd(proj, lb_fw, f_off=C_FF, rev=False, name="gla_fwd_fw")
    o_bw, st_bw = _gla_fwd(proj, lb_bw, f_off=C_FB, rev=True, name="gla_fwd_bw")
    o_hg = _hg_out_fwd(o_fw, o_bw, proj, p["hg_norm_gain"], name="hg_out")

    qkv_r, outs, lses = [], [], []
    for g in range(len(DA_GROUPS)):
        qr, kr, vr = _da_prep(proj, p["da_q_gain"], p["da_k_gain"], g, name=f"da_prep{g}")
        og, lg = _band_fwd(qr, kr, vr, g, name=f"band_fwd{g}")
        qkv_r.append((qr, kr, vr))
        outs.append(og)
        lses.append(lg)
    o_da, lse_da = _da_merge(outs, lses, rows, name="da_merge")

    mem_n = _rms_fwd(mem, p["norm_mem_gain"], name="norm_mem")
    kv = _matmul([(mem_n, w["w_mem_kv"])], "nn", F32, tm=256, tn=512, name="mem_kv")
    o_mem = _mem_fwd(proj, kv, p["mem_q_gain"], p["mem_k_gain"], name="mem_attn")

    merged = _branch_fwd(o_hg, o_da, o_mem, proj, w["w_proj_hg"], w["w_proj_da"], w["w_proj_mem"],
                         name="branch_merge")
    x1, h2 = _out_norm(merged, w["w_out"], x, p["norm_ffn_gain"], name="mix_out_norm")
    a, b, u = _ffn_in(h2, w["w_ffn_in"], name="ffn_in")
    dy, dy_b, loss = _out_loss(u, w["w_ffn_out"], x1, tgt, name="ffn_out_loss")

    gw, gs = {}, {}
    gw["w_ffn_out"] = _matmul([(u, dy_b)], "tn", F32, tm=256, tn=1024, name="g_ffn_out")
    da, db = _ffn_act_bwd(dy_b, w["w_ffn_out"], a, b, name="ffn_act_bwd")
    gw["w_ffn_a"] = _matmul([(h2, da)], "tn", F32, tm=512, tn=256, name="g_ffn_a")
    gw["w_ffn_b"] = _matmul([(h2, db)], "tn", F32, tm=512, tn=256, name="g_ffn_b")
    dh2 = _matmul([(da, w["w_ffn_in"], 0), (db, w["w_ffn_in"], 1)], "nt", F32, tm=256, tn=1024, name="d_h2")
    dx1, dx1_b, gs["norm_ffn_gain"] = _rms_bwd_rows(dh2, x1, p["norm_ffn_gain"], dy, name="norm_ffn_bwd")
    gw["w_out"] = _matmul([(merged, dx1_b)], "tn", F32, tm=512, tn=512, name="g_out")
    dmerged = _matmul([(dx1_b, w["w_out"])], "nt", F32, tm=512, tn=512, name="d_merged")
    (dg_hg, dz_hg, do_hg, dg_da, dz_da, do_da, dg_mem, dz_mem, do_mem) = _branch_bwd(
        dmerged, o_hg, o_da, o_mem, proj, w["w_proj_hg"], w["w_proj_da"], w["w_proj_mem"], name="branch_bwd")
    gw["w_proj_hg"] = _matmul([(o_hg, dz_hg)], "tn", F32, tm=512, tn=512, name="g_proj_hg")
    gw["w_proj_da"] = _matmul([(o_da, dz_da)], "tn", F32, tm=512, tn=512, name="g_proj_da")
    gw["w_proj_mem"] = _matmul([(o_mem, dz_mem)], "tn", F32, tm=512, tn=512, name="g_proj_mem")

    dq_mem, dk_mem, dv_mem, gs["mem_q_gain"], gs["mem_k_gain"] = _mem_bwd(
        do_mem, proj, kv, p["mem_q_gain"], p["mem_k_gain"], name="mem_attn_bwd")
    dkv = jnp.concatenate([dk_mem, dv_mem], axis=1).astype(BF16)
    gw["w_mem_kv"] = _matmul([(mem_n, dkv)], "tn", F32, tm=512, tn=512, name="g_mem_kv")
    dmem_n = _matmul([(dkv, w["w_mem_kv"])], "nt", F32, tm=256, tn=512, name="d_mem_n")
    _, _, gs["norm_mem_gain"] = _rms_bwd_rows(dmem_n, mem, p["norm_mem_gain"], None, name="norm_mem_bwd")

    prep = _da_bwd_prep(do_da, o_da, lse_da, name="da_bwd_prep")
    d_da, gq_parts, gk_parts = [], [], []
    for g in range(len(DA_GROUPS)):
        qr, kr, vr = qkv_r[g]
        dor, lser, deltar = prep[3 * g:3 * g + 3]
        dqr, dkr, dvr = _band_bwd(qr, kr, vr, dor, lser, deltar, g, name=f"band_bwd{g}")
        dq, dk, dv, gq, gk = _da_prep_bwd(dqr, dkr, dvr, proj, p["da_q_gain"], p["da_k_gain"], g,
                                          name=f"da_prep_bwd{g}")
        d_da.append((dq, dk, dv))
        gq_parts.append(gq)
        gk_parts.append(gk)

    dg_hgate, do_gla, gs["hg_norm_gain"] = _hg_out_bwd(do_hg, o_fw, o_bw, proj, p["hg_norm_gain"], name="hg_out_bwd")
    (dq_f, dfl_fw, dv_f, dlb_fw), carried = _gla_bwd(proj, lb_fw, do_gla, st_fw, None, f_off=C_FF, rev=False,
                                                     carry=ex.early_carry(gw), name="gla_bwd_fw")
    ex.early_done(carried)
    dq_hg, dfl_bw, dv_hg, dlb_bw = _gla_bwd(proj, lb_bw, do_gla, st_bw, (dq_f, dv_f), f_off=C_FB, rev=True,
                                            name="gla_bwd_bw")

    dproj = jnp.concatenate(
        [dq_hg, dfl_fw, dfl_bw, dv_hg, dg_hgate]
        + [t[0] for t in d_da] + [t[1] for t in d_da] + [t[2] for t in d_da]
        + [dq_mem, dg_hg, dg_da, dg_mem], axis=1)
    g_in = _matmul([(h, dproj)], "tn", F32, tm=512, tn=1024, name="g_in")
    dh, carried = _matmul([(dproj, w["w_in"])], "nt", F32, tm=1024, tn=1024, tk=IN_COLS // 8,
                          carry=ex.w_in_carry(g_in), name="d_h")
    ex.w_in_done(carried)
    grad_x, _, gs["norm_mix_gain"] = _rms_bwd_rows(dh, x, p["norm_mix_gain"], dx1, name="norm_mix_bwd")

    small = _small_pack(gs, gq_parts, gk_parts, dlb_fw, dlb_bw, lb_fw, lb_bw, name="small_pack")
    return loss, grad_x, small


def _small_pack(gs, gq_parts, gk_parts, dlb_fw, dlb_bw, lb_fw, lb_bw, *, name):
    def body(g_mix, g_mem, g_ffn, dfw, dbw, lfw, lbw, g_hg, q0, q1, q2, k0, k1, k2, g_mq, g_mk, o_ref):
        o_ref[0:1, :] = g_mix[...]
        o_ref[1:2, :] = g_mem[...]
        o_ref[2:3, :] = g_ffn[...]
        for base, d_ref, l_ref in ((3, dfw, lfw), (5, dbw, lbw)):
            lbv = l_ref[...]
            t = d_ref[...] * lbv * (1.0 - lbv)
            o_ref[base:base + 1, :] = t
            o_ref[base + 1:base + 2, :] = -t
        o_ref[7:8, :] = jnp.zeros((1, D_MODEL), F32)
        o_ref[7:8, 0:HEAD] = g_hg[...]
        o_ref[7:8, HEAD:2 * HEAD] = q0[...] + q1[...] + q2[...]
        o_ref[7:8, 2 * HEAD:3 * HEAD] = k0[...] + k1[...] + k2[...]
        o_ref[7:8, 3 * HEAD:4 * HEAD] = g_mq[...]
        o_ref[7:8, 4 * HEAD:5 * HEAD] = g_mk[...]

    return pl.pallas_call(body, name=name, out_shape=jax.ShapeDtypeStruct((8, D_MODEL), F32))(
        gs["norm_mix_gain"], gs["norm_mem_gain"], gs["norm_ffn_gain"], dlb_fw, dlb_bw, lb_fw, lb_bw,
        gs["hg_norm_gain"], *gq_parts, *gk_parts, gs["mem_q_gain"], gs["mem_k_gain"])


def _row_tile(rows, cols, n_arrays):
    budget = (16 * 1024 * 1024) // (2 * 4 * cols * n_arrays)
    tr = rows
    while tr > budget and tr % 2 == 0 and (tr // 2) % 16 == 0:
        tr //= 2
    return tr


def _cast_into_full(a, chip, rows, cols, axis, *, name):
    sr, sc = a.shape
    tr = _row_tile(sr, sc, 2)

    def body(chip_ref, a_ref, o_ref):
        del chip_ref
        o_ref[...] = a_ref[...].astype(BF16)

    if axis == 1:
        out_map = lambda i, chip_ref: (i, chip_ref[0])
    else:
        out_map = lambda i, chip_ref: (chip_ref[0] * (sr // tr) + i, 0)
    return pl.pallas_call(
        body, name=name,
        grid_spec=pltpu.PrefetchScalarGridSpec(
            num_scalar_prefetch=1, grid=(sr // tr,),
            in_specs=[pl.BlockSpec((tr, sc), lambda i, chip_ref: (i, 0))],
            out_specs=pl.BlockSpec((tr, sc), out_map)),
        out_shape=jax.ShapeDtypeStruct((rows, cols), BF16),
        compiler_params=_params(("parallel",)))(chip, a)


def _add_halves(items, *, name):
    rows = items[0][3].shape[0]
    widths = [ra.shape[1] for _, _, _, ra in items]
    tr = _row_tile(rows, sum(widths), 4)

    def body(*refs):
        o_ref = refs[-1]
        first = lax.axis_index("c") == 0
        off = 0
        for j, wd in enumerate(widths):
            h0, h1, ra = refs[3 * j:3 * j + 3]
            o_ref[:, off:off + wd] = (jnp.where(first, h0[...], h1[...]) + ra[...]).astype(BF16)
            off += wd

    in_specs, ins = [], []
    for (g, haxis, hsize, ra), wd in zip(items, widths):
        if haxis == 0:
            in_specs += [pl.BlockSpec((tr, wd), lambda i: (i, 0)),
                         pl.BlockSpec((tr, wd), lambda i, o=hsize // tr: (o + i, 0))]
        else:
            in_specs += [pl.BlockSpec((tr, wd), lambda i: (i, 0)), pl.BlockSpec((tr, wd), lambda i: (i, 1))]
        in_specs.append(pl.BlockSpec((tr, wd), lambda i: (i, 0)))
        ins += [g, g, ra]
    return pl.pallas_call(body, name=name, grid=(rows // tr,), in_specs=in_specs,
                          out_specs=pl.BlockSpec((tr, sum(widths)), lambda i: (i, 0)),
                          out_shape=jax.ShapeDtypeStruct((rows, sum(widths)), BF16),
                          compiler_params=_params(("parallel",)))(*ins)


def _add_slots(rb, *, name):
    _, rows, cols = rb.shape
    tr = _row_tile(rows, cols, 5)

    def body(r0, r1, r2, r3, o_ref):
        o_ref[...] = ((r0[...].astype(F32) + r1[...].astype(F32)) + r2[...].astype(F32)) + r3[...].astype(F32)

    slot = lambda s: pl.BlockSpec((None, tr, cols), lambda i: (s, i, 0))
    return pl.pallas_call(body, name=name, grid=(rows // tr,), in_specs=[slot(s) for s in range(4)],
                          out_specs=pl.BlockSpec((tr, cols), lambda i: (i, 0)),
                          out_shape=jax.ShapeDtypeStruct((rows, cols), F32),
                          compiler_params=_params(("parallel",)))(rb, rb, rb, rb)


def _adamw(w, g, m, v, *, name):
    rows, cols = w.shape
    tr = _row_tile(rows, cols, 7) if rows % 16 == 0 else rows
    c1 = 1.0 - ADAM_B1 ** ADAM_STEP
    c2 = 1.0 - ADAM_B2 ** ADAM_STEP

    def body(w_ref, g_ref, m_ref, v_ref, d_ref, mo_ref, vo_ref):
        gv = g_ref[...]
        mn = ADAM_B1 * m_ref[...] + (1.0 - ADAM_B1) * gv
        vn = ADAM_B2 * v_ref[...] + (1.0 - ADAM_B2) * (gv * gv)
        mo_ref[...] = mn
        vo_ref[...] = vn
        d_ref[...] = -ADAM_LR * ((mn / c1) / (jnp.sqrt(vn / c2) + ADAM_EPS) + ADAM_WD * w_ref[...])

    spec = pl.BlockSpec((tr, cols), lambda i: (i, 0))
    shape = jax.ShapeDtypeStruct((rows, cols), F32)
    return pl.pallas_call(body, name=name, grid=(rows // tr,), in_specs=[spec] * 4, out_specs=[spec] * 3,
                          out_shape=[shape] * 3, compiler_params=_params(("parallel",)))(w, g, m, v)


W_SPECS = (
    ("w_in", 1024, IN_COLS, 1, IN_COLS // 4),
    ("w_mem_kv", 1024, 1024, 0, 256),
    ("w_proj_hg", 1024, 1024, 0, 256),
    ("w_proj_da", 512, 1024, 1, 256),
    ("w_proj_mem", 512, 1024, 1, 256),
    ("w_out", 1024, 1024, 0, 256),
    ("w_ffn_in", 1024, 2 * D_FF, 1, 2 * D_FF // 4),
    ("w_ffn_out", D_FF, 1024, 0, D_FF // 4),
)
CHIP_FLIPS = ((1, 0), (0, 1), (1, 1))
ANY = pl.BlockSpec(memory_space=pl.ANY)
DMA_CHUNK_BYTES = 1 << 20
STAGE_BYTES = 2 << 20


def _place():
    x, y, c = lax.axis_index("x"), lax.axis_index("y"), lax.axis_index("c")
    return x, y, c, 2 * x + y


def _flip(v, f):
    return 1 - v if f else v


def _slab(ref, axis, idx, size):
    start = pl.multiple_of(idx * size, size)
    return ref.at[pl.ds(start, size), :] if axis == 0 else ref.at[:, pl.ds(start, size)]


def _chunked(make, src, dst, want="both"):
    rows, cols = src.shape
    row_bytes = cols * jnp.dtype(src.dtype).itemsize
    k = 1
    while rows % (2 * k) == 0 and (rows // (2 * k)) % 16 == 0 and (rows // k) * row_bytes > DMA_CHUNK_BYTES:
        k *= 2
    cr = rows // k
    parts = []
    if want != "wait":
        parts = [make(src.at[pl.ds(j * cr, cr), :], dst.at[pl.ds(j * cr, cr), :]) for j in range(k)]
    return parts, (make(src, dst) if want != "start" else None)


def _half_spec(rows, cols, axis):
    return (0, rows // 2) if axis == 1 else (1, cols // 2)


def _staged(src, remote_dst, local_dst, sibling, load_sems, send_sems, store_sems, recv_sem):
    rows, cols = src.shape
    row_bytes = cols * jnp.dtype(src.dtype).itemsize
    k = 1
    while (rows // k) * row_bytes > STAGE_BYTES and rows % (2 * k) == 0 and (rows // (2 * k)) % 16 == 0:
        k *= 2
    cr = rows // k
    piece = lambda ref, j: ref.at[pl.ds(j * cr, cr), :]

    def run(buf):
        loads = [pltpu.make_async_copy(piece(src, j), buf.at[j % 2], load_sems.at[j % 2]) for j in range(k)]
        outs = [[pltpu.make_async_remote_copy(src_ref=buf.at[j % 2], dst_ref=piece(remote_dst, j),
                                              send_sem=send_sems.at[j % 2], recv_sem=recv_sem,
                                              device_id=sibling, device_id_type=MESH)] for j in range(k)]
        if local_dst is not None:
            for j in range(k):
                outs[j].append(pltpu.make_async_copy(buf.at[j % 2], piece(local_dst, j), store_sems.at[j % 2]))

        def drained(j):
            outs[j][0].wait_send()
            for cp in outs[j][1:]:
                cp.wait()

        loads[0].start()
        for j in range(k):
            loads[j].wait()
            for cp in outs[j]:
                cp.start()
            if j + 1 < k:
                if j >= 1:
                    drained(j - 1)
                loads[j + 1].start()
        for j in range(max(0, k - 2), k):
            drained(j)

    pl.run_scoped(run, pltpu.VMEM((2, cr, cols), src.dtype))


def _landed(ref, recv_sem, send_sem):
    pltpu.make_async_remote_copy(src_ref=ref, dst_ref=ref, send_sem=send_sem, recv_sem=recv_sem,
                                 device_id=(lax.axis_index("x"), lax.axis_index("y"), lax.axis_index("c")),
                                 device_id_type=MESH).wait_recv()


def _half_slab(ref, spec, chip, half):
    _, rows, cols, axis, size = spec
    haxis, hsize = _half_spec(rows, cols, axis)
    return _slab(_slab(ref, axis, chip, size), haxis, half, hsize)


def _gather_sends(outs, specs, ici_send, ici_recv, want="both"):
    x, y, c, p = _place()
    sent = []
    for wi, spec in enumerate(specs):
        mine = _half_slab(outs[wi], spec, p, c)
        for k, (fx, fy) in enumerate(CHIP_FLIPS):
            sent.append(_chunked(lambda s, d, j=3 * wi + k, fx=fx, fy=fy: pltpu.make_async_remote_copy(
                src_ref=s, dst_ref=d, send_sem=ici_send.at[j], recv_sem=ici_recv.at[j],
                device_id=(_flip(x, fx), _flip(y, fy), c), device_id_type=MESH), mine, mine, want))
    return sent


def _gather_weights(fulls, specs, *, name):
    n = len(specs)

    def body(*refs):
        outs = refs[n:2 * n]
        ici_send, ici_recv, load_sems, d2d_send, d2d_recv = refs[2 * n:]
        x, y, c, _ = _place()
        sent = _gather_sends(outs, specs, ici_send, ici_recv)
        for parts, _ in sent:
            for cp in parts:
                cp.start()
        for k, (fx, fy) in enumerate(CHIP_FLIPS):
            q = 2 * _flip(x, fx) + _flip(y, fy)
            for wi in range(n):
                sent[3 * wi + k][1].wait_recv()
                got = _half_slab(outs[wi], specs[wi], q, c)
                _staged(got, got, None, (x, y, 1 - c), load_sems, d2d_send, None, d2d_recv.at[3 * wi + k])
        for _, whole in sent:
            whole.wait_send()
        for k, (fx, fy) in enumerate(CHIP_FLIPS):
            q = 2 * _flip(x, fx) + _flip(y, fy)
            for wi in range(n):
                _landed(_half_slab(outs[wi], specs[wi], q, 1 - c), d2d_recv.at[3 * wi + k], d2d_send.at[0])

    return pl.pallas_call(
        body, name=name, in_specs=[ANY] * n, out_specs=[ANY] * n,
        out_shape=[jax.ShapeDtypeStruct(f.shape, f.dtype) for f in fulls],
        input_output_aliases={i: i for i in range(n)},
        scratch_shapes=[pltpu.SemaphoreType.DMA((3 * n,)), pltpu.SemaphoreType.DMA((3 * n,)),
                        pltpu.SemaphoreType.DMA((2,)), pltpu.SemaphoreType.DMA((2,)),
                        pltpu.SemaphoreType.DMA((3 * n,))],
    )(*fulls)


def _gather_chips_carry(fulls, specs):
    n = len(specs)

    def issue(ins, outs, sems, want):
        del ins
        return [(parts, whole, True) for parts, whole in _gather_sends(outs, specs, sems[0], sems[1], want)]

    return _Carry(list(fulls), [jax.ShapeDtypeStruct(f.shape, f.dtype) for f in fulls], {i: i for i in range(n)},
                  [pltpu.SemaphoreType.DMA((3 * n,)), pltpu.SemaphoreType.DMA((3 * n,))], issue)


def _gather_pass_on(fulls, specs, *, name):
    n = len(specs)

    def body(*refs):
        outs = refs[n:2 * n]
        load_sems, d2d_send, d2d_recv = refs[2 * n:]
        x, y, c, _ = _place()
        for k, (fx, fy) in enumerate(CHIP_FLIPS):
            q = 2 * _flip(x, fx) + _flip(y, fy)
            for wi in range(n):
                got = _half_slab(outs[wi], specs[wi], q, c)
                _staged(got, got, None, (x, y, 1 - c), load_sems, d2d_send, None, d2d_recv.at[3 * wi + k])
        for k, (fx, fy) in enumerate(CHIP_FLIPS):
            q = 2 * _flip(x, fx) + _flip(y, fy)
            for wi in range(n):
                _landed(_half_slab(outs[wi], specs[wi], q, 1 - c), d2d_recv.at[3 * wi + k], d2d_send.at[0])

    return pl.pallas_call(
        body, name=name, in_specs=[ANY] * n, out_specs=[ANY] * n,
        out_shape=[jax.ShapeDtypeStruct(f.shape, f.dtype) for f in fulls],
        input_output_aliases={i: i for i in range(n)},
        scratch_shapes=[pltpu.SemaphoreType.DMA((2,)), pltpu.SemaphoreType.DMA((2,)),
                        pltpu.SemaphoreType.DMA((3 * n,))],
    )(*fulls)


def _sibling_exchange(grads, *, name):
    n = len(grads)

    def body(*refs):
        ins, outs = refs[:n], refs[n:2 * n]
        load_sems, send_sems, recv_sems = refs[2 * n:]
        x, y, c, _ = _place()
        for i, (_, haxis, hsize) in enumerate(grads):
            _staged(_slab(ins[i], haxis, 1 - c, hsize), outs[i], None, (x, y, 1 - c),
                    load_sems, send_sems, None, recv_sems.at[i])
        for i in range(n):
            _landed(outs[i], recv_sems.at[i], send_sems.at[0])

    shapes = [jax.ShapeDtypeStruct((hsize, g.shape[1]) if haxis == 0 else (g.shape[0], hsize), F32)
              for g, haxis, hsize in grads]
    return pl.pallas_call(
        body, name=name, in_specs=[ANY] * n, out_specs=[ANY] * n, out_shape=shapes,
        scratch_shapes=[pltpu.SemaphoreType.DMA((2,)), pltpu.SemaphoreType.DMA((2,)),
                        pltpu.SemaphoreType.DMA((n,))],
    )(*[g for g, _, _ in grads])


def _chip_exchange_carry(parts):
    n = len(parts)

    def issue(ins, outs, sems, want):
        send_sems, recv_sems, local_sems = sems
        x, y, c, p = _place()
        copies = []
        for i, (_, axis, size) in enumerate(parts):
            copies.append(_chunked(lambda s, d, i=i: pltpu.make_async_copy(s, d, local_sems.at[i]),
                                   _slab(ins[i], axis, p, size), outs[i].at[p], want) + (False,))
            for k, (fx, fy) in enumerate(CHIP_FLIPS):
                px, py = _flip(x, fx), _flip(y, fy)
                copies.append(_chunked(lambda s, d, j=3 * i + k, px=px, py=py: pltpu.make_async_remote_copy(
                    src_ref=s, dst_ref=d, send_sem=send_sems.at[j], recv_sem=recv_sems.at[j],
                    device_id=(px, py, c), device_id_type=MESH),
                    _slab(ins[i], axis, 2 * px + py, size), outs[i].at[p], want) + (True,))
        return copies

    shapes = []
    for a, axis, size in parts:
        shapes.append(jax.ShapeDtypeStruct((4, size, a.shape[1]) if axis == 0 else (4, a.shape[0], size), a.dtype))
    return _Carry([a for a, _, _ in parts], shapes, {},
                  [pltpu.SemaphoreType.DMA((3 * n,)), pltpu.SemaphoreType.DMA((3 * n,)),
                   pltpu.SemaphoreType.DMA((n,))], issue)


def _sibling_share(sums):
    n = len(sums)

    def body(*refs):
        ins, outs = refs[:n], refs[n:2 * n]
        load_sems, send_sems, store_sems, recv_sems = refs[2 * n:]
        x, y, c, _ = _place()
        for i, (s, haxis) in enumerate(sums):
            place = _slab(outs[i], haxis, c, s.shape[haxis])
            _staged(ins[i], place, place, (x, y, 1 - c), load_sems, send_sems, store_sems, recv_sems.at[i])
        for i, (s, haxis) in enumerate(sums):
            _landed(_slab(outs[i], haxis, 1 - c, s.shape[haxis]), recv_sems.at[i], send_sems.at[0])

    shapes = []
    for s, haxis in sums:
        r, cc = s.shape
        shapes.append(jax.ShapeDtypeStruct((2 * r, cc) if haxis == 0 else (r, 2 * cc), F32))
    return pl.pallas_call(
        body, name="grad_sibling_share", in_specs=[ANY] * n, out_specs=[ANY] * n, out_shape=shapes,
        scratch_shapes=[pltpu.SemaphoreType.DMA((2,)), pltpu.SemaphoreType.DMA((2,)),
                        pltpu.SemaphoreType.DMA((2,)), pltpu.SemaphoreType.DMA((n,))],
    )(*[s for s, _ in sums])


class _Exchanges:
    def __init__(self, fulls):
        self.w_in = _gather_weights([fulls["w_in"]], W_SPECS[:1], name="gather_w_in")[0]
        self.late_specs = W_SPECS[1:]
        self.late = [fulls[s[0]] for s in self.late_specs]
        self.slots = {}

    def late_carry(self):
        return _gather_chips_carry(self.late, self.late_specs)

    def late_weights(self, carried):
        done = _gather_pass_on(carried, self.late_specs, name="gather_pass_on")
        return dict(zip([s[0] for s in self.late_specs], done))

    def _half_sums(self, gw, specs, tag):
        grads = []
        for name, _, _, axis, _ in specs:
            for g in ((gw["w_ffn_a"], gw["w_ffn_b"]) if name == "w_ffn_in" else (gw[name],)):
                grads.append((g,) + _half_spec(g.shape[0], g.shape[1], axis))
        theirs = _sibling_exchange(grads, name=f"grad_sibling_exchange_{tag}")
        parts, j = [], 0
        for name, _, _, axis, size in specs:
            take = 2 if name == "w_ffn_in" else 1
            items = [grads[i] + (theirs[i],) for i in range(j, j + take)]
            parts.append((_add_halves(items, name=f"half_sum_{name}"), axis, size))
            j += take
        return _chip_exchange_carry(parts)

    def early_carry(self, gw):
        return self._half_sums(gw, self.late_specs, "early")

    def early_done(self, carried):
        self.slots.update(zip([s[0] for s in self.late_specs], carried))

    def w_in_carry(self, g):
        return self._half_sums({"w_in": g}, W_SPECS[:1], "w_in")

    def w_in_done(self, carried):
        self.slots["w_in"] = carried[0]

    def reduced(self):
        sums = []
        for name, rows, cols, axis, _ in W_SPECS:
            sums.append((_add_slots(self.slots[name], name=f"chip_sum_{name}"), _half_spec(rows, cols, axis)[0]))
        return dict(zip([s[0] for s in W_SPECS], _sibling_share(sums)))


def _small_allreduce(sv):
    def body(sv_ref, o_ref, slots_ref, send_sems, recv_sems):
        x, y, c, _ = _place()
        me = 4 * x + 2 * y + c
        slots_ref[me] = sv_ref[...]
        copies = []
        for k in range(1, 8):
            fx, fy, fc = (k >> 2) & 1, (k >> 1) & 1, k & 1
            copies.append(pltpu.make_async_remote_copy(
                src_ref=sv_ref, dst_ref=slots_ref.at[me], send_sem=send_sems.at[k - 1],
                recv_sem=recv_sems.at[k - 1], device_id=(_flip(x, fx), _flip(y, fy), _flip(c, fc)),
                device_id_type=MESH))
        for cp in copies:
            cp.start()
        for cp in copies:
            cp.wait_recv()
        for cp in copies:
            cp.wait_send()
        total = slots_ref[0]
        for s in range(1, 8):
            total = total + slots_ref[s]
        o_ref[...] = total

    vm = pl.BlockSpec(memory_space=pltpu.VMEM)
    return pl.pallas_call(
        body, name="small_allreduce", in_specs=[vm], out_specs=vm,
        out_shape=jax.ShapeDtypeStruct(sv.shape, F32),
        scratch_shapes=[pltpu.VMEM((8,) + sv.shape, F32), pltpu.SemaphoreType.DMA((7,)),
                        pltpu.SemaphoreType.DMA((7,))],
    )(sv)


SMALL_ROWS = (("norm_mix_gain", 0), ("norm_mem_gain", 1), ("norm_ffn_gain", 2))
SMALL_LB = (("lb_logits_fw", 3), ("lb_logits_bw", 5))
SMALL_HEAD = ("hg_norm_gain", "da_q_gain", "da_k_gain", "mem_q_gain", "mem_k_gain")


def _pack_small(d):
    last = jnp.concatenate([d[n] for n in SMALL_HEAD] + [jnp.zeros((1, D_MODEL - HEAD * len(SMALL_HEAD)), F32)], axis=1)
    return jnp.concatenate([d["norm_mix_gain"], d["norm_mem_gain"], d["norm_ffn_gain"],
                            d["lb_logits_fw"], d["lb_logits_bw"], last], axis=0)


def _unpack_small(a):
    out = {n: a[r:r + 1] for n, r in SMALL_ROWS}
    out.update({n: a[r:r + 2] for n, r in SMALL_LB})
    out.update({n: a[7:8, j * HEAD:(j + 1) * HEAD] for j, n in enumerate(SMALL_HEAD)})
    return out


PARAM_ORDER = ("norm_mix_gain", "norm_mem_gain", "w_in", "lb_logits_fw", "lb_logits_bw", "hg_norm_gain",
               "da_q_gain", "da_k_gain", "w_mem_kv", "mem_q_gain", "mem_k_gain", "w_proj_hg", "w_proj_da",
               "w_proj_mem", "w_out", "norm_ffn_gain", "w_ffn_in", "w_ffn_out")


def kernel(x, mem, norm_mix_gain, norm_mem_gain, w_in, lb_logits_fw, lb_logits_bw, hg_norm_gain, da_q_gain, da_k_gain, w_mem_kv, mem_q_gain, mem_k_gain, w_proj_hg, w_proj_da, w_proj_mem, w_out, norm_ffn_gain, w_ffn_in, w_ffn_out, loss_target, m_norm_mix_gain, m_norm_mem_gain, m_w_in, m_lb_logits_fw, m_lb_logits_bw, m_hg_norm_gain, m_da_q_gain, m_da_k_gain, m_w_mem_kv, m_mem_q_gain, m_mem_k_gain, m_w_proj_hg, m_w_proj_da, m_w_proj_mem, m_w_out, m_norm_ffn_gain, m_w_ffn_in, m_w_ffn_out, v_norm_mix_gain, v_norm_mem_gain, v_w_in, v_lb_logits_fw, v_lb_logits_bw, v_hg_norm_gain, v_da_q_gain, v_da_k_gain, v_w_mem_kv, v_mem_q_gain, v_mem_k_gain, v_w_proj_hg, v_w_proj_da, v_w_proj_mem, v_w_out, v_norm_ffn_gain, v_w_ffn_in, v_w_ffn_out):
    args = dict(locals())
    mats = tuple(s[0] for s in W_SPECS)
    flat = lambda a: a.reshape(a.shape[-2:])
    w = {n: flat(args[n]) for n in mats}
    m = {n: flat(args["m_" + n]) for n in mats}
    v = {n: flat(args["v_" + n]) for n in mats}
    small = {n: args[n] for n in PARAM_ORDER if n not in mats}

    chip = (2 * lax.axis_index("x") + lax.axis_index("y")).astype(jnp.int32).reshape(1)
    ex = _Exchanges({n: _cast_into_full(w[n], chip, rows, cols, axis, name=f"cast_{n}")
                     for n, rows, cols, axis, _ in W_SPECS})
    loss, grad_x, small_grads = _local_step(x[0], mem[0], loss_target[0], small, ex)
    grads = ex.reduced()
    small_sum = _small_allreduce(small_grads)

    delta, new_m, new_v = {}, {}, {}
    for n in mats:
        delta[n], new_m[n], new_v[n] = _adamw(w[n], grads[n], m[n], v[n], name=f"adamw_{n}")
    packed = _adamw(_pack_small(small), small_sum,
                    _pack_small({n: args["m_" + n] for n in small}),
                    _pack_small({n: args["v_" + n] for n in small}), name="adamw_small")
    grads.update(_unpack_small(small_sum))
    for dst, src in zip((delta, new_m, new_v), packed):
        dst.update(_unpack_small(src))

    def shaped(d, n):
        return d[n].reshape(args[n].shape)

    loss_sum = lax.psum(loss[0, 0], ("x", "y", "c"))
    return (loss_sum, grad_x[None], *[shaped(grads, n) for n in PARAM_ORDER], *[shaped(delta, n) for n in PARAM_ORDER],
            *[shaped(new_m, n) for n in PARAM_ORDER], *[shaped(new_v, n) for n in PARAM_ORDER])
```

```python
import functools
import math

import numpy as np
import jax
import jax.numpy as jnp
from jax import lax
from jax.experimental import pallas as pl
from jax.experimental.pallas import tpu as pltpu

F32 = jnp.float32
BF16 = jnp.bfloat16
MESH = pl.DeviceIdType.MESH

D_MODEL = 1024
HEAD = 128
HG_HEADS = 8
DA_GROUPS = ((1, 64), (4, 64), (16, 64))
DA_GH = 4
MEM_HEADS = 4
N_MEM = 256
D_FF = 2816
CHUNK = 64
BAND_QBLOCKS = 4
GLA_HEADS_PER_STEP = 2
RMS_EPS = 1e-6
NEG_INF = -1e30
HG_SCALE = HEAD ** -0.5
ATT_SCALE = HEAD ** -0.5
VMEM_LIMIT_V7X = 48 * 1024 * 1024

C_HQ, C_FF, C_FB, C_HI, C_HG = 0, 1024, 2048, 3072, 4096
C_DQ, C_DK, C_DV, C_MQ = 5120, 6656, 8192, 9728
C_GHG, C_GDA, C_GMEM = 10240, 11264, 12288
IN_COLS = 13312

ADAM_LR, ADAM_B1, ADAM_B2, ADAM_EPS, ADAM_WD, ADAM_STEP = 0.001, 0.9, 0.999, 1e-08, 0.01, 10


def _params(sem, vmem=VMEM_LIMIT_V7X):
    return pltpu.CompilerParams(dimension_semantics=sem, vmem_limit_bytes=vmem)


def _dot(a, b):
    return jnp.dot(a.astype(BF16), b.astype(BF16), preferred_element_type=F32)


def _dot_nt(a, b):
    return lax.dot_general(a.astype(BF16), b.astype(BF16), (((1,), (1,)), ((), ())),
                           preferred_element_type=F32)


def _dot_tn(a, b):
    return lax.dot_general(a.astype(BF16), b.astype(BF16), (((0,), (0,)), ((), ())),
                           preferred_element_type=F32)


def _sigmoid(v):
    return jax.nn.sigmoid(v.astype(F32))


def _ones(rows, cols):
    return (lax.broadcasted_iota(jnp.int32, (rows, cols), 0) >= 0).astype(BF16)


def _lane_sum(v):
    ones = _ones(HEAD, HEAD)
    hi = v.astype(BF16)
    mid = (v - hi.astype(F32)).astype(BF16)
    return jnp.dot(hi, ones, preferred_element_type=F32) + jnp.dot(mid, ones, preferred_element_type=F32)


def _row_mean(v):
    if v.shape[-1] == HEAD:
        return _lane_sum(v) * (1.0 / HEAD)
    return jnp.mean(v, axis=-1, keepdims=True)


def _rms(v):
    v = v.astype(F32)
    r = lax.rsqrt(_row_mean(v * v) + RMS_EPS)
    return v * r, r


def _rms_bwd(dy, xhat, r, gain):
    dxh = dy * gain
    dx = r * (dxh - xhat * _row_mean(dxh * xhat))
    return dx, dy * xhat


class _Carry:
    def __init__(self, arrays, out_shapes, aliases, sems, issue):
        self.arrays, self.out_shapes, self.aliases, self.sems, self.issue = arrays, out_shapes, aliases, sems, issue


NO_CARRY = object()


def _start_copies(copies):
    for parts, _, _ in copies:
        for cp in parts:
            cp.start()


def _wait_copies(copies):
    for _, whole, remote in copies:
        if remote:
            whole.wait_recv()
    for _, whole, remote in copies:
        if remote:
            whole.wait_send()
        else:
            whole.wait()


def _call(body, *, name, grid, in_specs, out_specs, out_shape, scratch_shapes, semantics, ins, carry=None,
          aliases=None):
    aliases = dict(aliases or {})
    if carry is None:
        res = pl.pallas_call(body, name=name, grid=grid, in_specs=in_specs, out_specs=out_specs, out_shape=out_shape,
                             scratch_shapes=scratch_shapes, input_output_aliases=aliases,
                             compiler_params=_params(semantics))(*ins)
        return list(res), []
    n_in, n_out, n_scr = len(ins), len(out_shape), len(scratch_shapes)
    c_in, c_out = len(carry.arrays), len(carry.out_shapes)

    def wrapped(*refs):
        pos = [0]

        def take(count):
            pos[0] += count
            return refs[pos[0] - count:pos[0]]

        own_in, carry_in = take(n_in), take(c_in)
        own_out, carry_out = take(n_out), take(c_out)
        own_scr, carry_sems = take(n_scr), take(len(carry.sems))
        ids = [pl.program_id(a) for a in range(len(grid))]
        first, last = ids[0] == 0, ids[0] == grid[0] - 1
        for a in range(1, len(grid)):
            first, last = first & (ids[a] == 0), last & (ids[a] == grid[a] - 1)

        @pl.when(first)
        def _():
            _start_copies(carry.issue(carry_in, carry_out, carry_sems, "start"))

        body(*own_in, *own_out, *own_scr)

        @pl.when(last)
        def _():
            _wait_copies(carry.issue(carry_in, carry_out, carry_sems, "wait"))

    res = pl.pallas_call(
        wrapped, name=name, grid=grid, in_specs=list(in_specs) + [ANY] * c_in,
        out_specs=list(out_specs) + [ANY] * c_out, out_shape=list(out_shape) + list(carry.out_shapes),
        input_output_aliases={**aliases, **{n_in + i: n_out + j for i, j in carry.aliases.items()}},
        scratch_shapes=list(scratch_shapes) + list(carry.sems),
        compiler_params=_params(("arbitrary",) * len(grid)))(*ins, *carry.arrays)
    return list(res[:n_out]), list(res[n_out:])


def _matmul(pairs, mode, out_dtype, *, tm, tn, tk=None, residual=None, carry=NO_CARRY, name):
    b_offs = [pr[2] if len(pr) > 2 else 0 for pr in pairs]
    pairs = [pr[:2] for pr in pairs]
    a0, b0 = pairs[0]
    if mode == "nn":
        (m, kk), n = a0.shape, b0.shape[1]
    elif mode == "nt":
        (m, kk), n = a0.shape, b0.shape[0]
    else:
        (kk, m), n = a0.shape, b0.shape[1]
    assert mode == "nt" or not any(b_offs)
    tm, tn = min(tm, m), min(tn, n)
    tk = kk if tk is None else tk
    nk = kk // tk
    assert m % tm == 0 and n % tn == 0 and kk % tk == 0, (name, m, n, kk)
    n_p = len(pairs)
    if mode == "tn":
        a_spec = pl.BlockSpec((tk, tm), lambda i, j, k: (k, i))
    else:
        a_spec = pl.BlockSpec((tm, tk), lambda i, j, k: (i, k))
    if mode == "nt":
        b_specs = [pl.BlockSpec((tn, tk), lambda i, j, k, o=o: (j, o * nk + k)) for o in b_offs]
    else:
        b_specs = [pl.BlockSpec((tk, tn), lambda i, j, k: (k, j))] * n_p
    o_spec = pl.BlockSpec((tm, tn), lambda i, j, k: (i, j))
    dot = {"nn": _dot, "nt": _dot_nt, "tn": _dot_tn}[mode]
    has_res = residual is not None

    def body(*refs):
        a_refs, b_refs = refs[:n_p], refs[n_p:2 * n_p]
        pos = 2 * n_p
        res_ref = refs[pos] if has_res else None
        pos += int(has_res)
        o_ref = refs[pos]
        part = dot(a_refs[0][...], b_refs[0][...])
        for a_r, b_r in zip(a_refs[1:], b_refs[1:]):
            part += dot(a_r[...], b_r[...])

        def finish(total):
            if has_res:
                total = total + res_ref[...]
            o_ref[...] = total.astype(out_dtype)

        if nk == 1:
            finish(part)
        else:
            acc_ref = refs[pos + 1]
            k = pl.program_id(2)

            @pl.when(k == 0)
            def _():
                acc_ref[...] = part

            @pl.when(k > 0)
            def _():
                acc_ref[...] += part

            @pl.when(k == nk - 1)
            def _():
                finish(acc_ref[...])

    ins = [a for a, _ in pairs] + [b for _, b in pairs]
    in_specs = [a_spec] * n_p + b_specs
    if has_res:
        ins.append(residual)
        in_specs.append(o_spec)
    (out,), carried = _call(
        body, name=name, grid=(m // tm, n // tn, nk), in_specs=in_specs, out_specs=[o_spec],
        out_shape=[jax.ShapeDtypeStruct((m, n), out_dtype)],
        scratch_shapes=[pltpu.VMEM((tm, tn), F32)] if nk > 1 else [],
        semantics=("parallel", "parallel", "arbitrary"), ins=ins, carry=None if carry is NO_CARRY else carry)
    return out if carry is NO_CARRY else (out, carried)


def _rms_fwd(x, gain, *, name):
    rows, dm = x.shape
    tm = min(512, rows)

    def body(x_ref, g_ref, h_ref):
        xhat, _ = _rms(x_ref[...])
        h_ref[...] = (xhat * g_ref[...]).astype(BF16)

    return pl.pallas_call(
        body, name=name, grid=(rows // tm,),
        in_specs=[pl.BlockSpec((tm, dm), lambda i: (i, 0)), pl.BlockSpec((1, dm), lambda i: (0, 0))],
        out_specs=pl.BlockSpec((tm, dm), lambda i: (i, 0)),
        out_shape=jax.ShapeDtypeStruct((rows, dm), BF16),
        compiler_params=_params(("parallel",)),
    )(x, gain)


def _rms_bwd_rows(dh, x, gain, dres, *, name):
    rows, dm = x.shape
    tm = min(512, rows)
    has_res = dres is not None

    def body(*refs):
        dh_ref, x_ref, g_ref = refs[:3]
        res_ref = refs[3] if has_res else None
        dx_ref, dxb_ref, dg_ref = refs[3 + int(has_res):]
        xhat, r = _rms(x_ref[...])
        dx, dgr = _rms_bwd(dh_ref[...], xhat, r, g_ref[...])
        if has_res:
            dx = dx + res_ref[...]
        dx_ref[...] = dx
        dxb_ref[...] = dx.astype(BF16)

        @pl.when(pl.program_id(0) == 0)
        def _():
            dg_ref[...] = jnp.zeros_like(dg_ref)

        dg_ref[...] += jnp.sum(dgr, axis=0, keepdims=True)

    row = pl.BlockSpec((tm, dm), lambda i: (i, 0))
    vec = pl.BlockSpec((1, dm), lambda i: (0, 0))
    return pl.pallas_call(
        body, name=name, grid=(rows // tm,),
        in_specs=[row, row, vec] + ([row] if has_res else []),
        out_specs=[row, row, vec],
        out_shape=[jax.ShapeDtypeStruct((rows, dm), F32), jax.ShapeDtypeStruct((rows, dm), BF16),
                   jax.ShapeDtypeStruct((1, dm), F32)],
        compiler_params=_params(("arbitrary",)),
    )(*([dh, x, gain] + ([dres] if has_res else [])))


def _out_norm(a, w, x, gain, *, name):
    rows, dm = x.shape
    tm = min(512, rows)

    def body(a_ref, w_ref, x_ref, g_ref, x1_ref, h_ref):
        x1 = x_ref[...] + _dot(a_ref[...], w_ref[...])
        x1_ref[...] = x1
        xhat, _ = _rms(x1)
        h_ref[...] = (xhat * g_ref[...]).astype(BF16)

    row = lambda wd: pl.BlockSpec((tm, wd), lambda i: (i, 0))
    return pl.pallas_call(
        body, name=name, grid=(rows // tm,),
        in_specs=[row(a.shape[1]), pl.BlockSpec(w.shape, lambda i: (0, 0)), row(dm), pl.BlockSpec((1, dm), lambda i: (0, 0))],
        out_specs=[row(dm), row(dm)],
        out_shape=[jax.ShapeDtypeStruct((rows, dm), F32), jax.ShapeDtypeStruct((rows, dm), BF16)],
        compiler_params=_params(("parallel",)),
    )(a, w, x, gain)


def _out_loss(u, w, x1, tgt, *, name):
    rows, dm = x1.shape
    tm = min(256, rows)
    steps = rows // tm

    def body(u_ref, w_ref, x_ref, t_ref, dy_ref, dyb_ref, loss_ref, acc_ref):
        i = pl.program_id(0)
        diff = x_ref[...] + _dot(u_ref[...], w_ref[...]) - t_ref[...]
        dy = diff * (1.0 / dm)
        dy_ref[...] = dy
        dyb_ref[...] = dy.astype(BF16)

        @pl.when(i == 0)
        def _():
            acc_ref[...] = jnp.zeros_like(acc_ref)

        acc_ref[...] += jnp.sum(diff * diff, axis=0, keepdims=True)

        @pl.when(i == steps - 1)
        def _():
            loss_ref[...] = jnp.full((1, HEAD), 0.5 / dm, F32) * jnp.sum(acc_ref[...])

    row = lambda wd: pl.BlockSpec((tm, wd), lambda i: (i, 0))
    return pl.pallas_call(
        body, name=name, grid=(steps,),
        in_specs=[row(u.shape[1]), pl.BlockSpec(w.shape, lambda i: (0, 0)), row(dm), row(dm)],
        out_specs=[row(dm), row(dm), pl.BlockSpec((1, HEAD), lambda i: (0, 0))],
        out_shape=[jax.ShapeDtypeStruct((rows, dm), F32), jax.ShapeDtypeStruct((rows, dm), BF16),
                   jax.ShapeDtypeStruct((1, HEAD), F32)],
        scratch_shapes=[pltpu.VMEM((1, dm), F32)],
        compiler_params=_params(("arbitrary",)),
    )(u, w, x1, tgt)


def _gla_block_terms(q_raw, f_logit, lb, rev):
    sig = _sigmoid(f_logit)
    forget = lb + (1.0 - lb) * sig
    k = 1.0 - forget
    b = _chunk_cumsum(jnp.log(forget), rev)
    qs = _sigmoid(q_raw)
    eb = jnp.exp(b)
    emb = jnp.exp(-b)
    qt = (q_raw * qs * HG_SCALE) * eb
    kt = k * emb
    return sig, forget, k, b, qs, eb, emb, qt, kt


def _chunk_cumsum(v, rev):
    n = v.shape[0]
    pos = lax.broadcasted_iota(jnp.int32, v.shape, 0) & (CHUNK - 1)
    step = 1
    while step < CHUNK:
        if rev:
            shifted, keep = pltpu.roll(v, n - step, 0), pos < CHUNK - step
        else:
            shifted, keep = pltpu.roll(v, step, 0), pos >= step
        v = v + jnp.where(keep, shifted, 0.0)
        step *= 2
    return v


def _tri_mask(n, rev):
    row, col = np.arange(n)[:, None], np.arange(n)[None, :]
    same = (row // CHUNK) == (col // CHUNK)
    return jnp.asarray((same & ((row <= col) if rev else (row >= col))).astype(np.float32))


def _chunk_order(ncb, rev):
    order = range(ncb - 1, -1, -1) if rev else range(ncb)
    return [(c, c * CHUNK if rev else c * CHUNK + CHUNK - 1) for c in order]


def _gla_fwd(proj, lb, *, f_off, rev, name):
    rows = proj.shape[0]
    tb = min(256, rows)
    nb, ncb = rows // tb, tb // CHUNK

    def tmap(n):
        return nb - 1 - n if rev else n

    def body(tri_ref, q_ref, f_ref, v_ref, lb_ref, o_ref, st_ref, s_ref):
        @pl.when(pl.program_id(1) == 0)
        def _():
            s_ref[...] = jnp.zeros_like(s_ref)

        tri = tri_ref[...] > 0.5
        for hh in range(GLA_HEADS_PER_STEP):
            cs = slice(hh * HEAD, (hh + 1) * HEAD)
            v = v_ref[:, cs]
            _, _, k, b, _, _, _, qt, kt = _gla_block_terms(q_ref[:, cs].astype(F32), f_ref[:, cs].astype(F32),
                                                           lb_ref[:, cs], rev)
            o_intra = _dot(jnp.where(tri, _dot_nt(qt, kt), 0.0), v)
            chunks = []
            for c, last in _chunk_order(ncb, rev):
                sl = slice(c * CHUNK, (c + 1) * CHUNK)
                bl = b[last:last + 1, :]
                kh = k[sl] * jnp.exp(bl - b[sl])
                chunks.append((c, sl, jnp.exp(bl), _dot_tn(v[sl], kh)))
            s_t = s_ref[hh]
            for c, sl, ebl, kv in chunks:
                st_ref[hh, c] = s_t
                o_ref[sl, cs] = o_intra[sl] + _dot_nt(qt[sl], s_t)
                s_t = ebl * s_t + kv
            s_ref[hh] = s_t

    hps = GLA_HEADS_PER_STEP
    col = lambda off: pl.BlockSpec((tb, hps * HEAD), lambda h, n: (tmap(n), off // (hps * HEAD) + h))
    return pl.pallas_call(
        body, name=name, grid=(HG_HEADS // hps, nb),
        in_specs=[pl.BlockSpec((tb, tb), lambda h, n: (0, 0)), col(C_HQ), col(f_off), col(C_HI),
                  pl.BlockSpec((1, hps * HEAD), lambda h, n: (0, h))],
        out_specs=[pl.BlockSpec((tb, hps * HEAD), lambda h, n: (tmap(n), h)),
                   pl.BlockSpec((hps, ncb, HEAD, HEAD), lambda h, n: (h, tmap(n), 0, 0))],
        out_shape=[jax.ShapeDtypeStruct((rows, HG_HEADS * HEAD), F32),
                   jax.ShapeDtypeStruct((HG_HEADS, rows // CHUNK, HEAD, HEAD), F32)],
        scratch_shapes=[pltpu.VMEM((hps, HEAD, HEAD), F32)],
        compiler_params=_params(("parallel", "arbitrary")),
    )(_tri_mask(tb, rev), proj, proj, proj, lb)


def _gla_bwd(proj, lb, do, states, prev, dproj, *, f_off, rev, carry=NO_CARRY, name):
    rows = proj.shape[0]
    tb = min(256, rows)
    nb, ncb = rows // tb, tb // CHUNK
    has_prev = prev is not None
    qv_dtype = BF16 if has_prev else F32

    def tmap(n):
        return n if rev else nb - 1 - n

    def body(*refs):
        tri_ref, q_ref, f_ref, v_ref, lb_ref, do_ref, st_ref = refs[:7]
        pq_ref, pv_ref = refs[8:10] if has_prev else (None, None)
        (dq_ref, df_ref, dv_ref, dlb_ref,
         ds_ref, dqt_scr, dk_scr, db_scr, dbl_scr, dv_scr) = refs[8 + 2 * int(has_prev):]

        @pl.when(pl.program_id(1) == 0)
        def _():
            ds_ref[...] = jnp.zeros_like(ds_ref)
            dlb_ref[...] = jnp.zeros_like(dlb_ref)

        tri = tri_ref[...] > 0.5
        for hh in range(GLA_HEADS_PER_STEP):
            cs = slice(hh * HEAD, (hh + 1) * HEAD)
            lbv = lb_ref[:, cs]
            q_raw = q_ref[:, cs].astype(F32)
            v, dout = v_ref[:, cs], do_ref[:, cs].astype(BF16)
            sig, forget, k, b, qs, eb, emb, qt, kt = _gla_block_terms(q_raw, f_ref[:, cs].astype(F32), lbv, rev)
            a = jnp.where(tri, _dot_nt(qt, kt), 0.0)
            da = jnp.where(tri, _dot_nt(dout, v), 0.0)
            dv_intra = _dot_tn(a, dout)
            dqt_intra = _dot(da, kt)
            dkt = _dot_tn(da, qt)
            chunks = []
            for c, last in reversed(_chunk_order(ncb, rev)):
                sl = slice(c * CHUNK, (c + 1) * CHUNK)
                bl = b[last:last + 1, :]
                e = jnp.exp(bl - b[sl])
                s_t = st_ref[hh, c]
                dqt_scr[sl, cs] = dqt_intra[sl] + _dot(dout[sl], s_t)
                chunks.append((sl, jnp.exp(bl), e, k[sl] * e, s_t, _dot_tn(dout[sl], qt[sl])))
            ds_t = ds_ref[hh]
            for sl, ebl, e, kh, s_t, grow in chunks:
                dkh = _dot(v[sl], ds_t)
                dv_scr[sl, cs] = dv_intra[sl] + _dot_nt(kh, ds_t)
                dk_scr[sl, cs] = dkt[sl] * emb[sl] + dkh * e
                khd = kh * dkh
                dbl = jnp.sum(khd, axis=0, keepdims=True) + ebl * jnp.sum(ds_t * s_t, axis=0, keepdims=True)
                db_scr[sl, cs] = khd
                dbl_scr[sl, cs] = jnp.broadcast_to(dbl, (CHUNK, HEAD))
                ds_t = grow + ds_t * ebl
            ds_ref[hh] = ds_t
            dqt = dqt_scr[:, cs]
            dlogf = _chunk_cumsum(qt * dqt - kt * dkt - db_scr[:, cs], not rev) + dbl_scr[:, cs]
            dforget = dlogf / forget - dk_scr[:, cs]
            df_ref[:, cs] = (dforget * (1.0 - lbv) * sig * (1.0 - sig)).astype(BF16)
            dlb_ref[:, cs] += jnp.sum(dforget * (1.0 - sig), axis=0, keepdims=True)
            dqr = dqt * eb * (HG_SCALE * qs * (1.0 + q_raw * (1.0 - qs)))
            dv = dv_scr[:, cs]
            if has_prev:
                dqr = dqr + pq_ref[:, cs]
                dv = dv + pv_ref[:, cs]
            dq_ref[:, cs] = dqr.astype(qv_dtype)
            dv_ref[:, cs] = dv.astype(qv_dtype)

    hps = GLA_HEADS_PER_STEP
    col = lambda off: pl.BlockSpec((tb, hps * HEAD), lambda h, n: (tmap(n), off // (hps * HEAD) + h))
    blk = pl.BlockSpec((tb, hps * HEAD), lambda h, n: (tmap(n), h))
    vec = pl.BlockSpec((1, hps * HEAD), lambda h, n: (0, h))
    wide = HG_HEADS * HEAD
    into = jax.ShapeDtypeStruct(dproj.shape, dproj.dtype)
    outs, carried = _call(
        body, name=name, grid=(HG_HEADS // hps, nb),
        in_specs=[pl.BlockSpec((tb, tb), lambda h, n: (0, 0)), col(C_HQ), col(f_off), col(C_HI), vec, blk,
                  pl.BlockSpec((hps, ncb, HEAD, HEAD), lambda h, n: (h, tmap(n), 0, 0)), ANY]
                 + ([blk, blk] if has_prev else []),
        out_specs=[col(C_HQ) if has_prev else blk, blk if has_prev else col(f_off), blk, vec],
        out_shape=[into if has_prev else jax.ShapeDtypeStruct((rows, wide), qv_dtype),
                   jax.ShapeDtypeStruct((rows, wide), BF16) if has_prev else into,
                   jax.ShapeDtypeStruct((rows, wide), qv_dtype), jax.ShapeDtypeStruct((1, wide), F32)],
        aliases={7: 0 if has_prev else 1},
        scratch_shapes=[pltpu.VMEM((hps, HEAD, HEAD), F32)] + [pltpu.VMEM((tb, hps * HEAD), F32)] * 5,
        semantics=("parallel", "arbitrary"),
        ins=[_tri_mask(tb, rev), proj, proj, proj, lb, do, states, dproj] + (list(prev) if has_prev else []),
        carry=None if carry is NO_CARRY else carry)
    return outs if carry is NO_CARRY else (outs, carried)


def _fill_columns(dproj, parts, col, *, name):
    rows, wd = parts[0].shape
    tm = min(512, rows)
    n = len(parts)

    def body(*refs):
        o_ref = refs[n + 1]
        for j in range(n):
            o_ref[:, j * wd:(j + 1) * wd] = refs[j][...]

    return pl.pallas_call(
        body, name=name, grid=(rows // tm,),
        in_specs=[pl.BlockSpec((tm, wd), lambda i: (i, 0))] * n + [ANY],
        out_specs=pl.BlockSpec((tm, n * wd), lambda i: (i, col // (n * wd))),
        out_shape=jax.ShapeDtypeStruct(dproj.shape, dproj.dtype), input_output_aliases={n: 0},
        compiler_params=_params(("parallel",)),
    )(*parts, dproj)


def _hg_out_fwd(o_fw, o_bw, proj, gain, *, name):
    rows = o_fw.shape[0]
    tm = min(512, rows)
    wide = HG_HEADS * HEAD

    def body(a_ref, b_ref, g_ref, gain_ref, o_ref):
        for h in range(HG_HEADS):
            sl = slice(h * HEAD, (h + 1) * HEAD)
            xhat, _ = _rms(a_ref[:, sl] + b_ref[:, sl])
            gate = g_ref[:, sl].astype(F32)
            o_ref[:, sl] = (xhat * gain_ref[...] * (gate * _sigmoid(gate))).astype(BF16)

    row = pl.BlockSpec((tm, wide), lambda i: (i, 0))
    return pl.pallas_call(
        body, name=name, grid=(rows // tm,),
        in_specs=[row, row, pl.BlockSpec((tm, wide), lambda i: (i, C_HG // wide)),
                  pl.BlockSpec((1, HEAD), lambda i: (0, 0))],
        out_specs=row, out_shape=jax.ShapeDtypeStruct((rows, wide), BF16),
        compiler_params=_params(("parallel",)),
    )(o_fw, o_bw, proj, gain)


def _hg_out_bwd(dout, o_fw, o_bw, proj, gain, dproj, *, name):
    rows = o_fw.shape[0]
    tm = min(512, rows)
    wide = HG_HEADS * HEAD

    def body(d_ref, a_ref, b_ref, g_ref, gain_ref, dp_in, dgate_ref, do_ref, dgain_ref):
        del dp_in

        @pl.when(pl.program_id(0) == 0)
        def _():
            dgain_ref[...] = jnp.zeros_like(dgain_ref)

        dgain = jnp.zeros((1, HEAD), F32)
        for h in range(HG_HEADS):
            sl = slice(h * HEAD, (h + 1) * HEAD)
            xhat, r = _rms(a_ref[:, sl] + b_ref[:, sl])
            gate, dy = g_ref[:, sl].astype(F32), d_ref[:, sl]
            gs = _sigmoid(gate)
            dgate_ref[:, sl] = (dy * xhat * gain_ref[...] * (gs * (1.0 + gate * (1.0 - gs)))).astype(BF16)
            dx, dgr = _rms_bwd(dy * (gate * gs), xhat, r, gain_ref[...])
            do_ref[:, sl] = dx
            dgain = dgain + jnp.sum(dgr, axis=0, keepdims=True)
        dgain_ref[...] += dgain

    row = pl.BlockSpec((tm, wide), lambda i: (i, 0))
    gate = pl.BlockSpec((tm, wide), lambda i: (i, C_HG // wide))
    vec = pl.BlockSpec((1, HEAD), lambda i: (0, 0))
    return pl.pallas_call(
        body, name=name, grid=(rows // tm,),
        in_specs=[row, row, row, gate, vec, ANY],
        out_specs=[gate, row, vec],
        out_shape=[jax.ShapeDtypeStruct(dproj.shape, dproj.dtype), jax.ShapeDtypeStruct((rows, wide), F32),
                   jax.ShapeDtypeStruct((1, HEAD), F32)],
        input_output_aliases={5: 0},
        compiler_params=_params(("arbitrary",)),
    )(dout, o_fw, o_bw, proj, gain, dproj)


def _strided_rows(ref, r, count, d):
    return ref[...] if d == 1 else ref[pl.ds(r, count, stride=d), :]


def _store_strided(ref, r, count, d, val):
    if d == 1:
        ref[...] = val
    else:
        ref[pl.ds(r, count, stride=d), :] = val


def _da_prep(proj, q_gain, k_gain, g, *, name):
    d = DA_GROUPS[g][0]
    rows = proj.shape[0]
    rb = min(512, rows)
    tn = rb // d

    def body(q_ref, k_ref, v_ref, qg_ref, kg_ref, qo_ref, ko_ref, vo_ref, qf_ref, kf_ref, vf_ref):
        for h in range(DA_GH):
            cs = slice(h * HEAD, (h + 1) * HEAD)
            for src, dst in ((q_ref, qf_ref), (k_ref, kf_ref), (v_ref, vf_ref)):
                dst[...] = src[:, cs].astype(F32)
            for r in range(d):
                qhat, _ = _rms(_strided_rows(qf_ref, r, tn, d))
                khat, _ = _rms(_strided_rows(kf_ref, r, tn, d))
                qo_ref[h, r] = (qhat * qg_ref[...]).astype(BF16)
                ko_ref[h, r] = (khat * kg_ref[...]).astype(BF16)
                vo_ref[h, r] = _strided_rows(vf_ref, r, tn, d).astype(BF16)

    wide = DA_GH * HEAD
    col = lambda off: pl.BlockSpec((rb, wide), lambda i: (i, off // wide + g))
    vec = pl.BlockSpec((1, HEAD), lambda i: (0, 0))
    out = pl.BlockSpec((DA_GH, d, tn, HEAD), lambda i: (0, 0, i, 0))
    shape = jax.ShapeDtypeStruct((DA_GH, d, rows // d, HEAD), BF16)
    return pl.pallas_call(
        body, name=name, grid=(rows // rb,),
        in_specs=[col(C_DQ), col(C_DK), col(C_DV), vec, vec],
        out_specs=[out, out, out], out_shape=[shape, shape, shape],
        scratch_shapes=[pltpu.VMEM((rb, HEAD), F32)] * 3,
        compiler_params=_params(("parallel",)),
    )(proj, proj, proj, q_gain, k_gain)


def _slopes(g):
    idx = np.arange(g * DA_GH + 1, (g + 1) * DA_GH + 1)
    s = (2.0 ** (-8.0 * idx / (DA_GH * len(DA_GROUPS)))).astype(np.float32)
    return jnp.asarray(np.broadcast_to(s[:, None, None], (DA_GH, 8, HEAD)).copy())


def _band_window(ld, t, radius):
    win = min(2 * t, ld)
    assert t // 2 >= radius or win == ld
    return win


def _band_scores(q, k, q0, start, slope, d, radius):
    t, win = q.shape[0], k.shape[0]
    row = lax.broadcasted_iota(jnp.int32, (t, win), 0)
    col = lax.broadcasted_iota(jnp.int32, (t, win), 1)
    rel = jnp.abs((start - q0) + col - row)
    return _dot_nt(q, k) * ATT_SCALE + jnp.where(rel <= radius, -slope * (d * rel).astype(F32), NEG_INF)


def _band_fwd(qr, kr, vr, g, *, name):
    d, radius = DA_GROUPS[g]
    _, _, ld, _ = qr.shape
    t = min(HEAD, ld)
    qb = min(BAND_QBLOCKS, ld // t)
    win = _band_window(ld, t, radius)

    def body(q_ref, k_ref, v_ref, sl_ref, o_ref, lse_ref):
        i = pl.program_id(2)
        slope = sl_ref[0:1, 0:1]
        for j in range(qb):
            sl = slice(j * t, (j + 1) * t)
            q0 = (i * qb + j) * t
            start = pl.multiple_of(jnp.clip(q0 - t // 2, 0, ld - win), t // 2)
            s = _band_scores(q_ref[sl, :], k_ref[pl.ds(start, win), :], q0, start, slope, d, radius)
            m = jnp.max(s, axis=-1, keepdims=True)
            p = jnp.exp(s - m).astype(BF16)
            l = jnp.dot(p, _ones(win, HEAD), preferred_element_type=F32)
            o_ref[sl, :] = _dot(p, v_ref[pl.ds(start, win), :]) / l
            lse_ref[sl, :] = m + jnp.log(l)

    own = pl.BlockSpec((None, None, qb * t, HEAD), lambda h, r, i: (h, r, i, 0))
    seq = pl.BlockSpec((None, None, ld, HEAD), lambda h, r, i: (h, r, 0, 0))
    shape = jax.ShapeDtypeStruct(qr.shape, F32)
    return pl.pallas_call(
        body, name=name, grid=(DA_GH, d, ld // (qb * t)),
        in_specs=[own, seq, seq, pl.BlockSpec((None, 8, HEAD), lambda h, r, i: (h, 0, 0))],
        out_specs=[own, own], out_shape=[shape, shape],
        compiler_params=_params(("parallel", "parallel", "parallel")),
    )(qr, kr, vr, _slopes(g))


def _band_bwd(qr, kr, vr, dor, lser, deltar, g, *, name):
    d, radius = DA_GROUPS[g]
    _, _, ld, _ = qr.shape
    t = min(HEAD, ld)
    qb = min(BAND_QBLOCKS, ld // t)
    win = _band_window(ld, t, radius)

    def body(q_ref, k_ref, v_ref, do_ref, lse_ref, dl_ref, sl_ref, dq_ref, dk_ref, dv_ref):
        i = pl.program_id(2)

        @pl.when(i == 0)
        def _():
            dk_ref[...] = jnp.zeros_like(dk_ref)
            dv_ref[...] = jnp.zeros_like(dv_ref)

        slope = sl_ref[0:1, 0:1]
        for j in range(qb):
            sl = slice(j * t, (j + 1) * t)
            q0 = (i * qb + j) * t
            start = pl.multiple_of(jnp.clip(q0 - t // 2, 0, ld - win), t // 2)
            rows = pl.ds(start, win)
            q, dout, k, v = q_ref[sl, :], do_ref[sl, :], k_ref[rows, :], v_ref[rows, :]
            p = jnp.exp(_band_scores(q, k, q0, start, slope, d, radius) - lse_ref[sl, 0:1])
            ds = p * (_dot_nt(dout, v) - dl_ref[sl, 0:1])
            dq_ref[sl, :] = _dot(ds, k) * ATT_SCALE
            dk_ref[rows, :] += _dot_tn(ds, q) * ATT_SCALE
            dv_ref[rows, :] += _dot_tn(p, dout)

    own = pl.BlockSpec((None, None, qb * t, HEAD), lambda h, r, i: (h, r, i, 0))
    seq = pl.BlockSpec((None, None, ld, HEAD), lambda h, r, i: (h, r, 0, 0))
    shape = jax.ShapeDtypeStruct(qr.shape, F32)
    return pl.pallas_call(
        body, name=name, grid=(DA_GH, d, ld // (qb * t)),
        in_specs=[own, seq, seq, own, own, own, pl.BlockSpec((None, 8, HEAD), lambda h, r, i: (h, 0, 0))],
        out_specs=[own, seq, seq], out_shape=[shape, shape, shape],
        compiler_params=_params(("parallel", "parallel", "arbitrary")),
    )(qr, kr, vr, dor, lser, deltar, _slopes(g))


def _da_merge(outs, lses, rows, *, name):
    rb = min(512, rows)
    wide = DA_GH * HEAD

    def body(*refs):
        o_refs, l_refs = refs[0:3], refs[3:6]
        o_ref, lse_ref = refs[6:8]
        on_refs, ln_refs = refs[8:11], refs[11:14]
        for h in range(DA_GH):
            cs = slice(h * HEAD, (h + 1) * HEAD)
            for g, (d, _) in enumerate(DA_GROUPS):
                tn = rb // d
                for r in range(d):
                    _store_strided(on_refs[g], r, tn, d, o_refs[g][h, r])
                    _store_strided(ln_refs[g], r, tn, d, l_refs[g][h, r])
            l0, l1, l2 = ln_refs[0][...], ln_refs[1][...], ln_refs[2][...]
            m = jnp.maximum(jnp.maximum(l0, l1), l2)
            e0, e1, e2 = jnp.exp(l0 - m), jnp.exp(l1 - m), jnp.exp(l2 - m)
            tot = e0 + e1 + e2
            o_ref[:, cs] = (e0 * on_refs[0][...] + e1 * on_refs[1][...] + e2 * on_refs[2][...]) / tot
            lse_ref[:, cs] = m + jnp.log(tot)

    res = lambda d: pl.BlockSpec((DA_GH, d, rb // d, HEAD), lambda i: (0, 0, i, 0))
    nat = pl.BlockSpec((rb, wide), lambda i: (i, 0))
    shape = jax.ShapeDtypeStruct((rows, wide), F32)
    return pl.pallas_call(
        body, name=name, grid=(rows // rb,),
        in_specs=[res(d) for d, _ in DA_GROUPS] * 2,
        out_specs=[nat, nat], out_shape=[shape, shape],
        scratch_shapes=[pltpu.VMEM((rb, HEAD), F32)] * 6,
        compiler_params=_params(("parallel",)),
    )(*outs, *lses)


def _da_bwd_prep(dout, o, lse, *, name):
    rows = o.shape[0]
    rb = min(512, rows)
    wide = DA_GH * HEAD

    def body(d_ref, o_ref, l_ref, *outs):
        d_scr, l_scr, delta_scr = outs[9:12]
        for h in range(DA_GH):
            cs = slice(h * HEAD, (h + 1) * HEAD)
            dv = d_ref[:, cs]
            d_scr[...] = dv
            l_scr[...] = l_ref[:, cs]
            delta_scr[...] = _lane_sum(dv * o_ref[:, cs])
            for g, (d, _) in enumerate(DA_GROUPS):
                tn = rb // d
                for r in range(d):
                    outs[3 * g][h, r] = _strided_rows(d_scr, r, tn, d).astype(BF16)
                    outs[3 * g + 1][h, r] = _strided_rows(l_scr, r, tn, d)
                    outs[3 * g + 2][h, r] = _strided_rows(delta_scr, r, tn, d)

    nat = pl.BlockSpec((rb, wide), lambda i: (i, 0))
    out_specs, out_shape = [], []
    for d, _ in DA_GROUPS:
        for dt in (BF16, F32, F32):
            out_specs.append(pl.BlockSpec((DA_GH, d, rb // d, HEAD), lambda i: (0, 0, i, 0)))
            out_shape.append(jax.ShapeDtypeStruct((DA_GH, d, rows // d, HEAD), dt))
    return pl.pallas_call(
        body, name=name, grid=(rows // rb,),
        in_specs=[nat, nat, nat], out_specs=out_specs, out_shape=out_shape,
        scratch_shapes=[pltpu.VMEM((rb, HEAD), F32)] * 3,
        compiler_params=_params(("parallel",)),
    )(dout, o, lse)


def _da_prep_bwd(dqr, dkr, dvr, proj, q_gain, k_gain, g, dproj, *, name):
    d = DA_GROUPS[g][0]
    rows = proj.shape[0]
    rb = min(512, rows)
    tn = rb // d
    wide = DA_GH * HEAD

    def body(dq_ref, dk_ref, dv_ref, q_ref, k_ref, qg_ref, kg_ref, dp_in, o_ref, gq_ref, gk_ref, nat_ref):
        del dp_in
        i, t = pl.program_id(0), pl.program_id(1)

        @pl.when((i == 0) & (t == 0))
        def _():
            gq_ref[...] = jnp.zeros_like(gq_ref)
            gk_ref[...] = jnp.zeros_like(gk_ref)

        def to_tokens(src, h):
            for r in range(d):
                _store_strided(nat_ref, r, tn, d, src[h, r])
            return nat_ref[...]

        for kind, (src, x_ref, gn_ref, acc_ref) in enumerate(((dq_ref, q_ref, qg_ref, gq_ref),
                                                              (dk_ref, k_ref, kg_ref, gk_ref))):
            @pl.when(t == kind)
            def _(src=src, x_ref=x_ref, gn_ref=gn_ref, acc_ref=acc_ref):
                for h in range(DA_GH):
                    cs = slice(h * HEAD, (h + 1) * HEAD)
                    xhat, r = _rms(x_ref[:, cs])
                    dx, dgr = _rms_bwd(to_tokens(src, h), xhat, r, gn_ref[...])
                    o_ref[:, cs] = dx.astype(BF16)
                    acc_ref[...] += jnp.sum(dgr, axis=0, keepdims=True)

        @pl.when(t == 2)
        def _():
            for h in range(DA_GH):
                o_ref[:, h * HEAD:(h + 1) * HEAD] = to_tokens(dv_ref, h).astype(BF16)

    res = pl.BlockSpec((DA_GH, d, tn, HEAD), lambda i, t: (0, 0, i, 0))
    col = lambda off: pl.BlockSpec((rb, wide), lambda i, t: (i, off // wide + g))
    vec = pl.BlockSpec((1, HEAD), lambda i, t: (0, 0))
    out = pl.BlockSpec((rb, wide), lambda i, t: (i, C_DQ // wide + 3 * t + g))
    gshape = jax.ShapeDtypeStruct((1, HEAD), F32)
    return pl.pallas_call(
        body, name=name, grid=(rows // rb, 3),
        in_specs=[res, res, res, col(C_DQ), col(C_DK), vec, vec, ANY],
        out_specs=[out, vec, vec], out_shape=[jax.ShapeDtypeStruct(dproj.shape, dproj.dtype), gshape, gshape],
        input_output_aliases={7: 0},
        scratch_shapes=[pltpu.VMEM((rb, HEAD), F32)],
        compiler_params=_params(("arbitrary", "arbitrary")),
    )(dqr, dkr, dvr, proj, proj, q_gain, k_gain, dproj)


def _mem_fwd(proj, kv, q_gain, k_gain, *, name):
    rows = proj.shape[0]
    tm = min(512, rows)
    n_mem = kv.shape[0]

    def body(q_ref, k_ref, v_ref, qg_ref, kg_ref, o_ref):
        qhat, _ = _rms(q_ref[...])
        khat, _ = _rms(k_ref[...])
        s = _dot_nt(qhat * qg_ref[...], khat * kg_ref[...]) * ATT_SCALE
        p = jnp.exp(s - jnp.max(s, axis=-1, keepdims=True))
        p = p / jnp.sum(p, axis=-1, keepdims=True)
        o_ref[...] = _dot(p, v_ref[...]).astype(BF16)

    vec = pl.BlockSpec((1, HEAD), lambda i, h: (0, 0))
    return pl.pallas_call(
        body, name=name, grid=(rows // tm, MEM_HEADS),
        in_specs=[pl.BlockSpec((tm, HEAD), lambda i, h: (i, C_MQ // HEAD + h)),
                  pl.BlockSpec((n_mem, HEAD), lambda i, h: (0, h)),
                  pl.BlockSpec((n_mem, HEAD), lambda i, h: (0, MEM_HEADS + h)), vec, vec],
        out_specs=pl.BlockSpec((tm, HEAD), lambda i, h: (i, h)),
        out_shape=jax.ShapeDtypeStruct((rows, MEM_HEADS * HEAD), BF16),
        compiler_params=_params(("parallel", "parallel")),
    )(proj, kv, kv, q_gain, k_gain)


def _mem_bwd(dout, proj, kv, q_gain, k_gain, dproj, *, name):
    rows = proj.shape[0]
    tm = min(512, rows)
    steps = rows // tm
    n_mem = kv.shape[0]

    def body(d_ref, q_ref, k_ref, v_ref, qg_ref, kg_ref, dp_in, dq_ref, dk_ref, dv_ref, gq_ref, gk_ref, dkn_ref):
        del dp_in
        h, i = pl.program_id(0), pl.program_id(1)

        @pl.when((h == 0) & (i == 0))
        def _():
            gq_ref[...] = jnp.zeros_like(gq_ref)
            gk_ref[...] = jnp.zeros_like(gk_ref)

        @pl.when(i == 0)
        def _():
            dkn_ref[...] = jnp.zeros_like(dkn_ref)
            dv_ref[...] = jnp.zeros_like(dv_ref)

        qhat, rq = _rms(q_ref[...])
        khat, rk = _rms(k_ref[...])
        qn, kn = qhat * qg_ref[...], khat * kg_ref[...]
        s = _dot_nt(qn, kn) * ATT_SCALE
        p = jnp.exp(s - jnp.max(s, axis=-1, keepdims=True))
        p = p / jnp.sum(p, axis=-1, keepdims=True)
        dout = d_ref[...]
        dp = _dot_nt(dout, v_ref[...])
        ds = p * (dp - jnp.sum(p * dp, axis=-1, keepdims=True))
        dv_ref[...] += _dot_tn(p, dout)
        dkn_ref[...] += _dot_tn(ds, qn) * ATT_SCALE
        dq, dgr = _rms_bwd(_dot(ds, kn) * ATT_SCALE, qhat, rq, qg_ref[...])
        dq_ref[...] = dq.astype(BF16)
        gq_ref[...] += jnp.sum(dgr, axis=0, keepdims=True)

        @pl.when(i == steps - 1)
        def _():
            dk, dgk = _rms_bwd(dkn_ref[...], khat, rk, kg_ref[...])
            dk_ref[...] = dk
            gk_ref[...] += jnp.sum(dgk, axis=0, keepdims=True)

    vec = pl.BlockSpec((1, HEAD), lambda h, i: (0, 0))
    memh = pl.BlockSpec((n_mem, HEAD), lambda h, i: (0, h))
    tok = pl.BlockSpec((tm, HEAD), lambda h, i: (i, h))
    mq = pl.BlockSpec((tm, HEAD), lambda h, i: (i, C_MQ // HEAD + h))
    gshape = jax.ShapeDtypeStruct((1, HEAD), F32)
    return pl.pallas_call(
        body, name=name, grid=(MEM_HEADS, steps),
        in_specs=[tok, mq, memh, pl.BlockSpec((n_mem, HEAD), lambda h, i: (0, MEM_HEADS + h)), vec, vec, ANY],
        out_specs=[mq, memh, memh, vec, vec],
        out_shape=[jax.ShapeDtypeStruct(dproj.shape, dproj.dtype),
                   jax.ShapeDtypeStruct((n_mem, MEM_HEADS * HEAD), F32),
                   jax.ShapeDtypeStruct((n_mem, MEM_HEADS * HEAD), F32), gshape, gshape],
        input_output_aliases={6: 0},
        scratch_shapes=[pltpu.VMEM((n_mem, HEAD), F32)],
        compiler_params=_params(("arbitrary", "arbitrary")),
    )(dout, proj, kv, kv, q_gain, k_gain, dproj)


def _branch_fwd(o_hg, o_da, o_mem, proj, wp_hg, wp_da, wp_mem, *, name):
    rows = o_hg.shape[0]
    tm = min(256, rows)

    def body(a_ref, b_ref, c_ref, ga_ref, gb_ref, gc_ref, wa_ref, wb_ref, wc_ref, o_ref):
        merged = _sigmoid(ga_ref[...]) * _dot(a_ref[...], wa_ref[...])
        merged += _sigmoid(gb_ref[...]) * _dot(b_ref[...], wb_ref[...])
        merged += _sigmoid(gc_ref[...]) * _dot(c_ref[...], wc_ref[...])
        o_ref[...] = merged.astype(BF16)

    row = lambda w: pl.BlockSpec((tm, w), lambda i: (i, 0))
    gate = lambda off: pl.BlockSpec((tm, D_MODEL), lambda i: (i, off // D_MODEL))
    full = lambda a: pl.BlockSpec(a.shape, lambda i: (0, 0))
    return pl.pallas_call(
        body, name=name, grid=(rows // tm,),
        in_specs=[row(o_hg.shape[1]), row(o_da.shape[1]), row(o_mem.shape[1]),
                  gate(C_GHG), gate(C_GDA), gate(C_GMEM), full(wp_hg), full(wp_da), full(wp_mem)],
        out_specs=row(D_MODEL), out_shape=jax.ShapeDtypeStruct((rows, D_MODEL), BF16),
        compiler_params=_params(("parallel",)),
    )(o_hg, o_da, o_mem, proj, proj, proj, wp_hg, wp_da, wp_mem)


def _branch_bwd(dm, o_hg, o_da, o_mem, proj, wp_hg, wp_da, wp_mem, *, name):
    rows = o_hg.shape[0]
    tm = min(256, rows)

    def body(dm_ref, a_ref, b_ref, c_ref, ga_ref, gb_ref, gc_ref, wa_ref, wb_ref, wc_ref, dp_ref, *outs):
        dmv = dm_ref[...]
        for j, (o_ref, g_ref, w_ref) in enumerate(((a_ref, ga_ref, wa_ref), (b_ref, gb_ref, wb_ref),
                                                   (c_ref, gc_ref, wc_ref))):
            z = _dot(o_ref[...], w_ref[...])
            gs = _sigmoid(g_ref[...])
            dz = (dmv * gs).astype(BF16)
            dp_ref[:, j * D_MODEL:(j + 1) * D_MODEL] = (dmv * z * gs * (1.0 - gs)).astype(BF16)
            outs[2 * j][...] = dz
            outs[2 * j + 1][...] = _dot_nt(dz, w_ref[...])

    row = lambda w: pl.BlockSpec((tm, w), lambda i: (i, 0))
    gate = lambda off: pl.BlockSpec((tm, D_MODEL), lambda i: (i, off // D_MODEL))
    full = lambda a: pl.BlockSpec(a.shape, lambda i: (0, 0))
    out_specs = [pl.BlockSpec((pl.Element(tm), pl.Element(3 * D_MODEL)), lambda i: (i * tm, C_GHG))]
    out_shape = [jax.ShapeDtypeStruct((rows, IN_COLS), BF16)]
    for o in (o_hg, o_da, o_mem):
        out_specs += [row(D_MODEL), row(o.shape[1])]
        out_shape += [jax.ShapeDtypeStruct((rows, D_MODEL), BF16), jax.ShapeDtypeStruct((rows, o.shape[1]), F32)]
    return pl.pallas_call(
        body, name=name, grid=(rows // tm,),
        in_specs=[row(D_MODEL), row(o_hg.shape[1]), row(o_da.shape[1]), row(o_mem.shape[1]),
                  gate(C_GHG), gate(C_GDA), gate(C_GMEM), full(wp_hg), full(wp_da), full(wp_mem)],
        out_specs=out_specs, out_shape=out_shape,
        compiler_params=_params(("parallel",)),
    )(dm, o_hg, o_da, o_mem, proj, proj, proj, wp_hg, wp_da, wp_mem)


def _ffn_in(h2, w_ab, *, name):
    rows, dff = h2.shape[0], w_ab.shape[1] // 2
    tm, tn = min(2048, rows), 256

    def body(h_ref, wa_ref, wb_ref, a_ref, b_ref, u_ref):
        a = _dot(h_ref[...], wa_ref[...])
        b = _dot(h_ref[...], wb_ref[...])
        a_ref[...] = a.astype(BF16)
        b_ref[...] = b.astype(BF16)
        u_ref[...] = (a * _sigmoid(a) * b).astype(BF16)

    out = pl.BlockSpec((tm, tn), lambda i, j: (i, j))
    return pl.pallas_call(
        body, name=name, grid=(rows // tm, dff // tn),
        in_specs=[pl.BlockSpec((tm, D_MODEL), lambda i, j: (i, 0)),
                  pl.BlockSpec((D_MODEL, tn), lambda i, j: (0, j)),
                  pl.BlockSpec((D_MODEL, tn), lambda i, j: (0, dff // tn + j))],
        out_specs=[out, out, out],
        out_shape=[jax.ShapeDtypeStruct((rows, dff), BF16)] * 3,
        compiler_params=_params(("parallel", "parallel")),
    )(h2, w_ab, w_ab)


def _ffn_act_bwd(dy, w_out, a, b, *, name):
    rows, dff = a.shape
    tm, tn = min(1024, rows), 256

    def body(dy_ref, w_ref, a_ref, b_ref, da_ref, db_ref):
        du = _dot_nt(dy_ref[...], w_ref[...])
        av, bv = a_ref[...].astype(F32), b_ref[...].astype(F32)
        sa = _sigmoid(av)
        da_ref[...] = (du * bv * sa * (1.0 + av * (1.0 - sa))).astype(BF16)
        db_ref[...] = (du * av * sa).astype(BF16)

    tile = pl.BlockSpec((tm, tn), lambda i, j: (i, j))
    return pl.pallas_call(
        body, name=name, grid=(rows // tm, dff // tn),
        in_specs=[pl.BlockSpec((tm, D_MODEL), lambda i, j: (i, 0)),
                  pl.BlockSpec((tn, D_MODEL), lambda i, j: (j, 0)), tile, tile],
        out_specs=[tile, tile],
        out_shape=[jax.ShapeDtypeStruct((rows, dff), BF16), jax.ShapeDtypeStruct((rows, dff), BF16)],
        compiler_params=_params(("parallel", "parallel")),
    )(dy, w_out, a, b)


def _lower_bound(lb_fw, lb_bw, *, name):
    def body(a_ref, b_ref, oa_ref, ob_ref):
        for src, dst in ((a_ref, oa_ref), (b_ref, ob_ref)):
            dst[...] = _sigmoid(src[0:1, :] - src[1:2, :])

    shape = jax.ShapeDtypeStruct((1, lb_fw.shape[1]), F32)
    return pl.pallas_call(body, name=name, out_shape=[shape, shape])(lb_fw, lb_bw)


def _local_step(x, mem, tgt, p, ex):
    rows = x.shape[0]
    lb_fw, lb_bw = _lower_bound(p["lb_logits_fw"], p["lb_logits_bw"], name="lower_bound")

    h = _rms_fwd(x, p["norm_mix_gain"], name="norm_mix")
    w = {"w_in": ex.w_in}
    proj, carried = _matmul([(h, w["w_in"])], "nn", BF16, tm=1024, tn=1024, carry=ex.late_carry(), name="proj_in")
    w.update(ex.late_weights(carried))
    o_fw, st_fw = _gla_fwd(proj, lb_fw, f_off=C_FF, rev=False, name="gla_fwd_fw")
    o_bw, st_bw = _gla_fwd(proj, lb_bw, f_off=C_FB, rev=True, name="gla_fwd_bw")
    o_hg = _hg_out_fwd(o_fw, o_bw, proj, p["hg_norm_gain"], name="hg_out")

    qkv_r, outs, lses = [], [], []
    for g in range(len(DA_GROUPS)):
        qr, kr, vr = _da_prep(proj, p["da_q_gain"], p["da_k_gain"], g, name=f"da_prep{g}")
        og, lg = _band_fwd(qr, kr, vr, g, name=f"band_fwd{g}")
        qkv_r.append((qr, kr, vr))
        outs.append(og)
        lses.append(lg)
    o_da, lse_da = _da_merge(outs, lses, rows, name="da_merge")

    mem_n = _rms_fwd(mem, p["norm_mem_gain"], name="norm_mem")
    kv = _matmul([(mem_n, w["w_mem_kv"])], "nn", F32, tm=256, tn=512, name="mem_kv")
    o_mem = _mem_fwd(proj, kv, p["mem_q_gain"], p["mem_k_gain"], name="mem_attn")

    merged = _branch_fwd(o_hg, o_da, o_mem, proj, w["w_proj_hg"], w["w_proj_da"], w["w_proj_mem"],
                         name="branch_merge")
    x1, h2 = _out_norm(merged, w["w_out"], x, p["norm_ffn_gain"], name="mix_out_norm")
    a, b, u = _ffn_in(h2, w["w_ffn_in"], name="ffn_in")
    dy, dy_b, loss = _out_loss(u, w["w_ffn_out"], x1, tgt, name="ffn_out_loss")

    gw, gs = {}, {}
    gw["w_ffn_out"] = _matmul([(u, dy_b)], "tn", F32, tm=256, tn=1024, name="g_ffn_out")
    da, db = _ffn_act_bwd(dy_b, w["w_ffn_out"], a, b, name="ffn_act_bwd")
    gw["w_ffn_a"] = _matmul([(h2, da)], "tn", F32, tm=512, tn=256, name="g_ffn_a")
    gw["w_ffn_b"] = _matmul([(h2, db)], "tn", F32, tm=512, tn=256, name="g_ffn_b")
    dh2 = _matmul([(da, w["w_ffn_in"], 0), (db, w["w_ffn_in"], 1)], "nt", F32, tm=256, tn=1024, name="d_h2")
    dx1, dx1_b, gs["norm_ffn_gain"] = _rms_bwd_rows(dh2, x1, p["norm_ffn_gain"], dy, name="norm_ffn_bwd")
    gw["w_out"] = _matmul([(merged, dx1_b)], "tn", F32, tm=512, tn=512, name="g_out")
    dmerged = _matmul([(dx1_b, w["w_out"])], "nt", F32, tm=512, tn=512, name="d_merged")
    dproj, dz_hg, do_hg, dz_da, do_da, dz_mem, do_mem = _branch_bwd(
        dmerged, o_hg, o_da, o_mem, proj, w["w_proj_hg"], w["w_proj_da"], w["w_proj_mem"], name="branch_bwd")
    gw["w_proj_hg"] = _matmul([(o_hg, dz_hg)], "tn", F32, tm=512, tn=512, name="g_proj_hg")
    gw["w_proj_da"] = _matmul([(o_da, dz_da)], "tn", F32, tm=512, tn=512, name="g_proj_da")
    gw["w_proj_mem"] = _matmul([(o_mem, dz_mem)], "tn", F32, tm=512, tn=512, name="g_proj_mem")

    dproj, dk_mem, dv_mem, gs["mem_q_gain"], gs["mem_k_gain"] = _mem_bwd(
        do_mem, proj, kv, p["mem_q_gain"], p["mem_k_gain"], dproj, name="mem_attn_bwd")
    dkv = jnp.concatenate([dk_mem, dv_mem], axis=1).astype(BF16)
    gw["w_mem_kv"] = _matmul([(mem_n, dkv)], "tn", F32, tm=512, tn=512, name="g_mem_kv")
    dmem_n = _matmul([(dkv, w["w_mem_kv"])], "nt", F32, tm=256, tn=512, name="d_mem_n")
    _, _, gs["norm_mem_gain"] = _rms_bwd_rows(dmem_n, mem, p["norm_mem_gain"], None, name="norm_mem_bwd")

    prep = _da_bwd_prep(do_da, o_da, lse_da, name="da_bwd_prep")
    gq_parts, gk_parts = [], []
    for g in range(len(DA_GROUPS)):
        qr, kr, vr = qkv_r[g]
        dor, lser, deltar = prep[3 * g:3 * g + 3]
        dqr, dkr, dvr = _band_bwd(qr, kr, vr, dor, lser, deltar, g, name=f"band_bwd{g}")
        dproj, gq, gk = _da_prep_bwd(dqr, dkr, dvr, proj, p["da_q_gain"], p["da_k_gain"], g, dproj,
                                     name=f"da_prep_bwd{g}")
        gq_parts.append(gq)
        gk_parts.append(gk)

    dproj, do_gla, gs["hg_norm_gain"] = _hg_out_bwd(do_hg, o_fw, o_bw, proj, p["hg_norm_gain"], dproj,
                                                    name="hg_out_bwd")
    (dq_f, dproj, dv_f, dlb_fw), carried = _gla_bwd(proj, lb_fw, do_gla, st_fw, None, dproj, f_off=C_FF, rev=False,
                                                    carry=ex.early_carry(gw), name="gla_bwd_fw")
    ex.early_done(carried)
    dproj, dfl_bw, dv_hg, dlb_bw = _gla_bwd(proj, lb_bw, do_gla, st_bw, (dq_f, dv_f), dproj, f_off=C_FB, rev=True,
                                            name="gla_bwd_bw")
    dproj = _fill_columns(dproj, [dfl_bw, dv_hg], C_FB, name="dproj_fill")
    g_in = _matmul([(h, dproj)], "tn", F32, tm=512, tn=1024, name="g_in")
    dh, carried = _matmul([(dproj, w["w_in"])], "nt", F32, tm=1024, tn=1024, tk=IN_COLS // 8,
                          carry=ex.w_in_carry(g_in), name="d_h")
    ex.w_in_done(carried)
    grad_x, _, gs["norm_mix_gain"] = _rms_bwd_rows(dh, x, p["norm_mix_gain"], dx1, name="norm_mix_bwd")

    small = _small_pack(gs, gq_parts, gk_parts, dlb_fw, dlb_bw, lb_fw, lb_bw, name="small_pack")
    return loss, grad_x, small


def _small_pack(gs, gq_parts, gk_parts, dlb_fw, dlb_bw, lb_fw, lb_bw, *, name):
    def body(g_mix, g_mem, g_ffn, dfw, dbw, lfw, lbw, g_hg, q0, q1, q2, k0, k1, k2, g_mq, g_mk, o_ref):
        o_ref[0:1, :] = g_mix[...]
        o_ref[1:2, :] = g_mem[...]
        o_ref[2:3, :] = g_ffn[...]
        for base, d_ref, l_ref in ((3, dfw, lfw), (5, dbw, lbw)):
            lbv = l_ref[...]
            t = d_ref[...] * lbv * (1.0 - lbv)
            o_ref[base:base + 1, :] = t
            o_ref[base + 1:base + 2, :] = -t
        o_ref[7:8, :] = jnp.zeros((1, D_MODEL), F32)
        o_ref[7:8, 0:HEAD] = g_hg[...]
        o_ref[7:8, HEAD:2 * HEAD] = q0[...] + q1[...] + q2[...]
        o_ref[7:8, 2 * HEAD:3 * HEAD] = k0[...] + k1[...] + k2[...]
        o_ref[7:8, 3 * HEAD:4 * HEAD] = g_mq[...]
        o_ref[7:8, 4 * HEAD:5 * HEAD] = g_mk[...]

    return pl.pallas_call(body, name=name, out_shape=jax.ShapeDtypeStruct((8, D_MODEL), F32))(
        gs["norm_mix_gain"], gs["norm_mem_gain"], gs["norm_ffn_gain"], dlb_fw, dlb_bw, lb_fw, lb_bw,
        gs["hg_norm_gain"], *gq_parts, *gk_parts, gs["mem_q_gain"], gs["mem_k_gain"])


def _row_tile(rows, cols, n_arrays):
    budget = (16 * 1024 * 1024) // (2 * 4 * cols * n_arrays)
    tr = rows
    while tr > budget and tr % 2 == 0 and (tr // 2) % 16 == 0:
        tr //= 2
    return tr


def _cast_into_full(a, chip, rows, cols, axis, *, name):
    sr, sc = a.shape
    tr = _row_tile(sr, sc, 2)

    def body(chip_ref, a_ref, o_ref):
        del chip_ref
        o_ref[...] = a_ref[...].astype(BF16)

    if axis == 1:
        out_map = lambda i, chip_ref: (i, chip_ref[0])
    else:
        out_map = lambda i, chip_ref: (chip_ref[0] * (sr // tr) + i, 0)
    return pl.pallas_call(
        body, name=name,
        grid_spec=pltpu.PrefetchScalarGridSpec(
            num_scalar_prefetch=1, grid=(sr // tr,),
            in_specs=[pl.BlockSpec((tr, sc), lambda i, chip_ref: (i, 0))],
            out_specs=pl.BlockSpec((tr, sc), out_map)),
        out_shape=jax.ShapeDtypeStruct((rows, cols), BF16),
        compiler_params=_params(("parallel",)))(chip, a)


def _add_halves(items, *, name):
    rows = items[0][3].shape[0]
    widths = [ra.shape[1] for _, _, _, ra in items]
    tr = _row_tile(rows, sum(widths), 4)

    def body(*refs):
        o_ref = refs[-1]
        first = lax.axis_index("c") == 0
        off = 0
        for j, wd in enumerate(widths):
            h0, h1, ra = refs[3 * j:3 * j + 3]
            o_ref[:, off:off + wd] = (jnp.where(first, h0[...], h1[...]) + ra[...]).astype(BF16)
            off += wd

    in_specs, ins = [], []
    for (g, haxis, hsize, ra), wd in zip(items, widths):
        if haxis == 0:
            in_specs += [pl.BlockSpec((tr, wd), lambda i: (i, 0)),
                         pl.BlockSpec((tr, wd), lambda i, o=hsize // tr: (o + i, 0))]
        else:
            in_specs += [pl.BlockSpec((tr, wd), lambda i: (i, 0)), pl.BlockSpec((tr, wd), lambda i: (i, 1))]
        in_specs.append(pl.BlockSpec((tr, wd), lambda i: (i, 0)))
        ins += [g, g, ra]
    return pl.pallas_call(body, name=name, grid=(rows // tr,), in_specs=in_specs,
                          out_specs=pl.BlockSpec((tr, sum(widths)), lambda i: (i, 0)),
                          out_shape=jax.ShapeDtypeStruct((rows, sum(widths)), BF16),
                          compiler_params=_params(("parallel",)))(*ins)


def _add_slots(rb, *, name):
    _, rows, cols = rb.shape
    tr = _row_tile(rows, cols, 5)

    def body(r0, r1, r2, r3, o_ref):
        o_ref[...] = ((r0[...].astype(F32) + r1[...].astype(F32)) + r2[...].astype(F32)) + r3[...].astype(F32)

    slot = lambda s: pl.BlockSpec((None, tr, cols), lambda i: (s, i, 0))
    return pl.pallas_call(body, name=name, grid=(rows // tr,), in_specs=[slot(s) for s in range(4)],
                          out_specs=pl.BlockSpec((tr, cols), lambda i: (i, 0)),
                          out_shape=jax.ShapeDtypeStruct((rows, cols), F32),
                          compiler_params=_params(("parallel",)))(rb, rb, rb, rb)


def _adamw(w, g, m, v, *, name):
    rows, cols = w.shape
    tr = _row_tile(rows, cols, 7) if rows % 16 == 0 else rows
    c1 = 1.0 - ADAM_B1 ** ADAM_STEP
    c2 = 1.0 - ADAM_B2 ** ADAM_STEP

    def body(w_ref, g_ref, m_ref, v_ref, d_ref, mo_ref, vo_ref):
        gv = g_ref[...]
        mn = ADAM_B1 * m_ref[...] + (1.0 - ADAM_B1) * gv
        vn = ADAM_B2 * v_ref[...] + (1.0 - ADAM_B2) * (gv * gv)
        mo_ref[...] = mn
        vo_ref[...] = vn
        d_ref[...] = -ADAM_LR * ((mn / c1) / (jnp.sqrt(vn / c2) + ADAM_EPS) + ADAM_WD * w_ref[...])

    spec = pl.BlockSpec((tr, cols), lambda i: (i, 0))
    shape = jax.ShapeDtypeStruct((rows, cols), F32)
    return pl.pallas_call(body, name=name, grid=(rows // tr,), in_specs=[spec] * 4, out_specs=[spec] * 3,
                          out_shape=[shape] * 3, compiler_params=_params(("parallel",)))(w, g, m, v)


W_SPECS = (
    ("w_in", 1024, IN_COLS, 1, IN_COLS // 4),
    ("w_mem_kv", 1024, 1024, 0, 256),
    ("w_proj_hg", 1024, 1024, 0, 256),
    ("w_proj_da", 512, 1024, 1, 256),
    ("w_proj_mem", 512, 1024, 1, 256),
    ("w_out", 1024, 1024, 0, 256),
    ("w_ffn_in", 1024, 2 * D_FF, 1, 2 * D_FF // 4),
    ("w_ffn_out", D_FF, 1024, 0, D_FF // 4),
)
CHIP_FLIPS = ((1, 0), (0, 1), (1, 1))
ANY = pl.BlockSpec(memory_space=pl.ANY)
DMA_CHUNK_BYTES = 1 << 20
STAGE_BYTES = 2 << 20


def _place():
    x, y, c = lax.axis_index("x"), lax.axis_index("y"), lax.axis_index("c")
    return x, y, c, 2 * x + y


def _flip(v, f):
    return 1 - v if f else v


def _slab(ref, axis, idx, size):
    start = pl.multiple_of(idx * size, size)
    return ref.at[pl.ds(start, size), :] if axis == 0 else ref.at[:, pl.ds(start, size)]


def _chunked(make, src, dst, want="both"):
    rows, cols = src.shape
    row_bytes = cols * jnp.dtype(src.dtype).itemsize
    k = 1
    while rows % (2 * k) == 0 and (rows // (2 * k)) % 16 == 0 and (rows // k) * row_bytes > DMA_CHUNK_BYTES:
        k *= 2
    cr = rows // k
    parts = []
    if want != "wait":
        parts = [make(src.at[pl.ds(j * cr, cr), :], dst.at[pl.ds(j * cr, cr), :]) for j in range(k)]
    return parts, (make(src, dst) if want != "start" else None)


def _half_spec(rows, cols, axis):
    return (0, rows // 2) if axis == 1 else (1, cols // 2)


def _staged(src, remote_dst, local_dst, sibling, load_sems, send_sems, store_sems, recv_sem):
    rows, cols = src.shape
    row_bytes = cols * jnp.dtype(src.dtype).itemsize
    k = 1
    while (rows // k) * row_bytes > STAGE_BYTES and rows % (2 * k) == 0 and (rows // (2 * k)) % 16 == 0:
        k *= 2
    cr = rows // k
    piece = lambda ref, j: ref.at[pl.ds(j * cr, cr), :]

    def run(buf):
        loads = [pltpu.make_async_copy(piece(src, j), buf.at[j % 2], load_sems.at[j % 2]) for j in range(k)]
        outs = [[pltpu.make_async_remote_copy(src_ref=buf.at[j % 2], dst_ref=piece(remote_dst, j),
                                              send_sem=send_sems.at[j % 2], recv_sem=recv_sem,
                                              device_id=sibling, device_id_type=MESH)] for j in range(k)]
        if local_dst is not None:
            for j in range(k):
                outs[j].append(pltpu.make_async_copy(buf.at[j % 2], piece(local_dst, j), store_sems.at[j % 2]))

        def drained(j):
            outs[j][0].wait_send()
            for cp in outs[j][1:]:
                cp.wait()

        loads[0].start()
        for j in range(k):
            loads[j].wait()
            for cp in outs[j]:
                cp.start()
            if j + 1 < k:
                if j >= 1:
                    drained(j - 1)
                loads[j + 1].start()
        for j in range(max(0, k - 2), k):
            drained(j)

    pl.run_scoped(run, pltpu.VMEM((2, cr, cols), src.dtype))


def _landed(ref, recv_sem, send_sem):
    pltpu.make_async_remote_copy(src_ref=ref, dst_ref=ref, send_sem=send_sem, recv_sem=recv_sem,
                                 device_id=(lax.axis_index("x"), lax.axis_index("y"), lax.axis_index("c")),
                                 device_id_type=MESH).wait_recv()


def _half_slab(ref, spec, chip, half):
    _, rows, cols, axis, size = spec
    haxis, hsize = _half_spec(rows, cols, axis)
    return _slab(_slab(ref, axis, chip, size), haxis, half, hsize)


def _gather_sends(outs, specs, ici_send, ici_recv, want="both"):
    x, y, c, p = _place()
    sent = []
    for wi, spec in enumerate(specs):
        mine = _half_slab(outs[wi], spec, p, c)
        for k, (fx, fy) in enumerate(CHIP_FLIPS):
            sent.append(_chunked(lambda s, d, j=3 * wi + k, fx=fx, fy=fy: pltpu.make_async_remote_copy(
                src_ref=s, dst_ref=d, send_sem=ici_send.at[j], recv_sem=ici_recv.at[j],
                device_id=(_flip(x, fx), _flip(y, fy), c), device_id_type=MESH), mine, mine, want))
    return sent


def _gather_weights(fulls, specs, *, name):
    n = len(specs)

    def body(*refs):
        outs = refs[n:2 * n]
        ici_send, ici_recv, load_sems, d2d_send, d2d_recv = refs[2 * n:]
        x, y, c, _ = _place()
        sent = _gather_sends(outs, specs, ici_send, ici_recv)
        for parts, _ in sent:
            for cp in parts:
                cp.start()
        for k, (fx, fy) in enumerate(CHIP_FLIPS):
            q = 2 * _flip(x, fx) + _flip(y, fy)
            for wi in range(n):
                sent[3 * wi + k][1].wait_recv()
                got = _half_slab(outs[wi], specs[wi], q, c)
                _staged(got, got, None, (x, y, 1 - c), load_sems, d2d_send, None, d2d_recv.at[3 * wi + k])
        for _, whole in sent:
            whole.wait_send()
        for k, (fx, fy) in enumerate(CHIP_FLIPS):
            q = 2 * _flip(x, fx) + _flip(y, fy)
            for wi in range(n):
                _landed(_half_slab(outs[wi], specs[wi], q, 1 - c), d2d_recv.at[3 * wi + k], d2d_send.at[0])

    return pl.pallas_call(
        body, name=name, in_specs=[ANY] * n, out_specs=[ANY] * n,
        out_shape=[jax.ShapeDtypeStruct(f.shape, f.dtype) for f in fulls],
        input_output_aliases={i: i for i in range(n)},
        scratch_shapes=[pltpu.SemaphoreType.DMA((3 * n,)), pltpu.SemaphoreType.DMA((3 * n,)),
                        pltpu.SemaphoreType.DMA((2,)), pltpu.SemaphoreType.DMA((2,)),
                        pltpu.SemaphoreType.DMA((3 * n,))],
    )(*fulls)


def _gather_chips_carry(fulls, specs):
    n = len(specs)

    def issue(ins, outs, sems, want):
        del ins
        return [(parts, whole, True) for parts, whole in _gather_sends(outs, specs, sems[0], sems[1], want)]

    return _Carry(list(fulls), [jax.ShapeDtypeStruct(f.shape, f.dtype) for f in fulls], {i: i for i in range(n)},
                  [pltpu.SemaphoreType.DMA((3 * n,)), pltpu.SemaphoreType.DMA((3 * n,))], issue)


def _gather_pass_on(fulls, specs, *, name):
    n = len(specs)

    def body(*refs):
        outs = refs[n:2 * n]
        load_sems, d2d_send, d2d_recv = refs[2 * n:2 * n + 3]
        bufs = refs[2 * n + 3:]
        x, y, c, _ = _place()
        loads, sends = [], []
        for k, (fx, fy) in enumerate(CHIP_FLIPS):
            q = 2 * _flip(x, fx) + _flip(y, fy)
            for wi in range(n):
                j = 3 * wi + k
                got = _half_slab(outs[wi], specs[wi], q, c)
                loads.append(pltpu.make_async_copy(got, bufs[j], load_sems.at[j]))
                sends.append(pltpu.make_async_remote_copy(
                    src_ref=bufs[j], dst_ref=got, send_sem=d2d_send.at[j], recv_sem=d2d_recv.at[j],
                    device_id=(x, y, 1 - c), device_id_type=MESH))
        for cp in loads:
            cp.start()
        for load, send in zip(loads, sends):
            load.wait()
            send.start()
        for cp in sends:
            cp.wait_send()
        for k, (fx, fy) in enumerate(CHIP_FLIPS):
            q = 2 * _flip(x, fx) + _flip(y, fy)
            for wi in range(n):
                _landed(_half_slab(outs[wi], specs[wi], q, 1 - c), d2d_recv.at[3 * wi + k], d2d_send.at[0])

    shapes = []
    for _, rows, cols, axis, size in specs:
        shapes += [(rows // 2, size) if axis == 1 else (size, cols // 2)] * 3
    assert sum(math.prod(s) for s in shapes) * 2 <= VMEM_LIMIT_V7X // 2
    stages = [pltpu.VMEM(s, BF16) for s in shapes]
    return pl.pallas_call(
        body, name=name, in_specs=[ANY] * n, out_specs=[ANY] * n,
        out_shape=[jax.ShapeDtypeStruct(f.shape, f.dtype) for f in fulls],
        input_output_aliases={i: i for i in range(n)},
        scratch_shapes=[pltpu.SemaphoreType.DMA((3 * n,))] * 3 + stages,
        compiler_params=pltpu.CompilerParams(vmem_limit_bytes=VMEM_LIMIT_V7X),
    )(*fulls)


def _sibling_exchange(grads, *, name):
    n = len(grads)

    def body(*refs):
        ins, outs = refs[:n], refs[n:2 * n]
        load_sems, send_sems, recv_sems = refs[2 * n:]
        x, y, c, _ = _place()
        for i, (_, haxis, hsize) in enumerate(grads):
            _staged(_slab(ins[i], haxis, 1 - c, hsize), outs[i], None, (x, y, 1 - c),
                    load_sems, send_sems, None, recv_sems.at[i])
        for i in range(n):
            _landed(outs[i], recv_sems.at[i], send_sems.at[0])

    shapes = [jax.ShapeDtypeStruct((hsize, g.shape[1]) if haxis == 0 else (g.shape[0], hsize), F32)
              for g, haxis, hsize in grads]
    return pl.pallas_call(
        body, name=name, in_specs=[ANY] * n, out_specs=[ANY] * n, out_shape=shapes,
        scratch_shapes=[pltpu.SemaphoreType.DMA((2,)), pltpu.SemaphoreType.DMA((2,)),
                        pltpu.SemaphoreType.DMA((n,))],
    )(*[g for g, _, _ in grads])


def _chip_exchange_carry(parts):
    n = len(parts)

    def issue(ins, outs, sems, want):
        send_sems, recv_sems, local_sems = sems
        x, y, c, p = _place()
        copies = []
        for i, (_, axis, size) in enumerate(parts):
            copies.append(_chunked(lambda s, d, i=i: pltpu.make_async_copy(s, d, local_sems.at[i]),
                                   _slab(ins[i], axis, p, size), outs[i].at[p], want) + (False,))
            for k, (fx, fy) in enumerate(CHIP_FLIPS):
                px, py = _flip(x, fx), _flip(y, fy)
                copies.append(_chunked(lambda s, d, j=3 * i + k, px=px, py=py: pltpu.make_async_remote_copy(
                    src_ref=s, dst_ref=d, send_sem=send_sems.at[j], recv_sem=recv_sems.at[j],
                    device_id=(px, py, c), device_id_type=MESH),
                    _slab(ins[i], axis, 2 * px + py, size), outs[i].at[p], want) + (True,))
        return copies

    shapes = []
    for a, axis, size in parts:
        shapes.append(jax.ShapeDtypeStruct((4, size, a.shape[1]) if axis == 0 else (4, a.shape[0], size), a.dtype))
    return _Carry([a for a, _, _ in parts], shapes, {},
                  [pltpu.SemaphoreType.DMA((3 * n,)), pltpu.SemaphoreType.DMA((3 * n,)),
                   pltpu.SemaphoreType.DMA((n,))], issue)


def _sibling_share(sums):
    n = len(sums)

    def body(*refs):
        ins, outs = refs[:n], refs[n:2 * n]
        load_sems, send_sems, store_sems, recv_sems = refs[2 * n:]
        x, y, c, _ = _place()
        for i, (s, haxis) in enumerate(sums):
            place = _slab(outs[i], haxis, c, s.shape[haxis])
            _staged(ins[i], place, place, (x, y, 1 - c), load_sems, send_sems, store_sems, recv_sems.at[i])
        for i, (s, haxis) in enumerate(sums):
            _landed(_slab(outs[i], haxis, 1 - c, s.shape[haxis]), recv_sems.at[i], send_sems.at[0])

    shapes = []
    for s, haxis in sums:
        r, cc = s.shape
        shapes.append(jax.ShapeDtypeStruct((2 * r, cc) if haxis == 0 else (r, 2 * cc), F32))
    return pl.pallas_call(
        body, name="grad_sibling_share", in_specs=[ANY] * n, out_specs=[ANY] * n, out_shape=shapes,
        scratch_shapes=[pltpu.SemaphoreType.DMA((2,)), pltpu.SemaphoreType.DMA((2,)),
                        pltpu.SemaphoreType.DMA((2,)), pltpu.SemaphoreType.DMA((n,))],
    )(*[s for s, _ in sums])


class _Exchanges:
    def __init__(self, fulls):
        self.w_in = _gather_weights([fulls["w_in"]], W_SPECS[:1], name="gather_w_in")[0]
        self.late_specs = W_SPECS[1:]
        self.late = [fulls[s[0]] for s in self.late_specs]
        self.slots = {}

    def late_carry(self):
        return _gather_chips_carry(self.late, self.late_specs)

    def late_weights(self, carried):
        done = _gather_pass_on(carried, self.late_specs, name="gather_pass_on")
        return dict(zip([s[0] for s in self.late_specs], done))

    def _half_sums(self, gw, specs, tag):
        grads = []
        for name, _, _, axis, _ in specs:
            for g in ((gw["w_ffn_a"], gw["w_ffn_b"]) if name == "w_ffn_in" else (gw[name],)):
                grads.append((g,) + _half_spec(g.shape[0], g.shape[1], axis))
        theirs = _sibling_exchange(grads, name=f"grad_sibling_exchange_{tag}")
        parts, j = [], 0
        for name, _, _, axis, size in specs:
            take = 2 if name == "w_ffn_in" else 1
            items = [grads[i] + (theirs[i],) for i in range(j, j + take)]
            parts.append((_add_halves(items, name=f"half_sum_{name}"), axis, size))
            j += take
        return _chip_exchange_carry(parts)

    def early_carry(self, gw):
        return self._half_sums(gw, self.late_specs, "early")

    def early_done(self, carried):
        self.slots.update(zip([s[0] for s in self.late_specs], carried))

    def w_in_carry(self, g):
        return self._half_sums({"w_in": g}, W_SPECS[:1], "w_in")

    def w_in_done(self, carried):
        self.slots["w_in"] = carried[0]

    def reduced(self):
        sums = []
        for name, rows, cols, axis, _ in W_SPECS:
            sums.append((_add_slots(self.slots[name], name=f"chip_sum_{name}"), _half_spec(rows, cols, axis)[0]))
        return dict(zip([s[0] for s in W_SPECS], _sibling_share(sums)))


def _small_allreduce(sv):
    def body(sv_ref, o_ref, slots_ref, send_sems, recv_sems):
        x, y, c, _ = _place()
        me = 4 * x + 2 * y + c
        slots_ref[me] = sv_ref[...]
        copies = []
        for k in range(1, 8):
            fx, fy, fc = (k >> 2) & 1, (k >> 1) & 1, k & 1
            copies.append(pltpu.make_async_remote_copy(
                src_ref=sv_ref, dst_ref=slots_ref.at[me], send_sem=send_sems.at[k - 1],
                recv_sem=recv_sems.at[k - 1], device_id=(_flip(x, fx), _flip(y, fy), _flip(c, fc)),
                device_id_type=MESH))
        for cp in copies:
            cp.start()
        for cp in copies:
            cp.wait_recv()
        for cp in copies:
            cp.wait_send()
        total = slots_ref[0]
        for s in range(1, 8):
            total = total + slots_ref[s]
        o_ref[...] = total

    vm = pl.BlockSpec(memory_space=pltpu.VMEM)
    return pl.pallas_call(
        body, name="small_allreduce", in_specs=[vm], out_specs=vm,
        out_shape=jax.ShapeDtypeStruct(sv.shape, F32),
        scratch_shapes=[pltpu.VMEM((8,) + sv.shape, F32), pltpu.SemaphoreType.DMA((7,)),
                        pltpu.SemaphoreType.DMA((7,))],
    )(sv)


SMALL_ROWS = (("norm_mix_gain", 0), ("norm_mem_gain", 1), ("norm_ffn_gain", 2))
SMALL_LB = (("lb_logits_fw", 3), ("lb_logits_bw", 5))
SMALL_HEAD = ("hg_norm_gain", "da_q_gain", "da_k_gain", "mem_q_gain", "mem_k_gain")


def _pack_small(d):
    last = jnp.concatenate([d[n] for n in SMALL_HEAD] + [jnp.zeros((1, D_MODEL - HEAD * len(SMALL_HEAD)), F32)], axis=1)
    return jnp.concatenate([d["norm_mix_gain"], d["norm_mem_gain"], d["norm_ffn_gain"],
                            d["lb_logits_fw"], d["lb_logits_bw"], last], axis=0)


def _unpack_small(a):
    out = {n: a[r:r + 1] for n, r in SMALL_ROWS}
    out.update({n: a[r:r + 2] for n, r in SMALL_LB})
    out.update({n: a[7:8, j * HEAD:(j + 1) * HEAD] for j, n in enumerate(SMALL_HEAD)})
    return out


PARAM_ORDER = ("norm_mix_gain", "norm_mem_gain", "w_in", "lb_logits_fw", "lb_logits_bw", "hg_norm_gain",
               "da_q_gain", "da_k_gain", "w_mem_kv", "mem_q_gain", "mem_k_gain", "w_proj_hg", "w_proj_da",
               "w_proj_mem", "w_out", "norm_ffn_gain", "w_ffn_in", "w_ffn_out")


def kernel(x, mem, norm_mix_gain, norm_mem_gain, w_in, lb_logits_fw, lb_logits_bw, hg_norm_gain, da_q_gain, da_k_gain, w_mem_kv, mem_q_gain, mem_k_gain, w_proj_hg, w_proj_da, w_proj_mem, w_out, norm_ffn_gain, w_ffn_in, w_ffn_out, loss_target, m_norm_mix_gain, m_norm_mem_gain, m_w_in, m_lb_logits_fw, m_lb_logits_bw, m_hg_norm_gain, m_da_q_gain, m_da_k_gain, m_w_mem_kv, m_mem_q_gain, m_mem_k_gain, m_w_proj_hg, m_w_proj_da, m_w_proj_mem, m_w_out, m_norm_ffn_gain, m_w_ffn_in, m_w_ffn_out, v_norm_mix_gain, v_norm_mem_gain, v_w_in, v_lb_logits_fw, v_lb_logits_bw, v_hg_norm_gain, v_da_q_gain, v_da_k_gain, v_w_mem_kv, v_mem_q_gain, v_mem_k_gain, v_w_proj_hg, v_w_proj_da, v_w_proj_mem, v_w_out, v_norm_ffn_gain, v_w_ffn_in, v_w_ffn_out):
    args = dict(locals())
    mats = tuple(s[0] for s in W_SPECS)
    flat = lambda a: a.reshape(a.shape[-2:])
    w = {n: flat(args[n]) for n in mats}
    m = {n: flat(args["m_" + n]) for n in mats}
    v = {n: flat(args["v_" + n]) for n in mats}
    small = {n: args[n] for n in PARAM_ORDER if n not in mats}

    chip = (2 * lax.axis_index("x") + lax.axis_index("y")).astype(jnp.int32).reshape(1)
    ex = _Exchanges({n: _cast_into_full(w[n], chip, rows, cols, axis, name=f"cast_{n}")
                     for n, rows, cols, axis, _ in W_SPECS})
    loss, grad_x, small_grads = _local_step(x[0], mem[0], loss_target[0], small, ex)
    grads = ex.reduced()
    small_sum = _small_allreduce(small_grads)

    delta, new_m, new_v = {}, {}, {}
    for n in mats:
        delta[n], new_m[n], new_v[n] = _adamw(w[n], grads[n], m[n], v[n], name=f"adamw_{n}")
    packed = _adamw(_pack_small(small), small_sum,
                    _pack_small({n: args["m_" + n] for n in small}),
                    _pack_small({n: args["v_" + n] for n in small}), name="adamw_small")
    grads.update(_unpack_small(small_sum))
    for dst, src in zip((delta, new_m, new_v), packed):
        dst.update(_unpack_small(src))

    def shaped(d, n):
        return d[n].reshape(args[n].shape)

    loss_sum = lax.psum(loss[0, 0], ("x", "y", "c"))
    return (loss_sum, grad_x[None], *[shaped(grads, n) for n in PARAM_ORDER], *[shaped(delta, n) for n in PARAM_ORDER],
            *[shaped(new_m, n) for n in PARAM_ORDER], *[shaped(new_v, n) for n in PARAM_ORDER])
```

```python
import functools
import math

import numpy as np
import jax
import jax.numpy as jnp
from jax import lax
from jax.experimental import pallas as pl
from jax.experimental.pallas import tpu as pltpu

F32 = jnp.float32
BF16 = jnp.bfloat16
MESH = pl.DeviceIdType.MESH

D_MODEL = 1024
HEAD = 128
HG_HEADS = 8
DA_GROUPS = ((1, 64), (4, 64), (16, 64))
DA_GH = 4
MEM_HEADS = 4
N_MEM = 256
D_FF = 2816
CHUNK = 64
BAND_QBLOCKS = 4
GLA_HEADS_PER_STEP = 4
RMS_EPS = 1e-6
NEG_INF = -1e30
HG_SCALE = HEAD ** -0.5
ATT_SCALE = HEAD ** -0.5
VMEM_LIMIT_V7X = 48 * 1024 * 1024

C_HQ, C_FF, C_FB, C_HI, C_HG = 0, 1024, 2048, 3072, 4096
C_DQ, C_DK, C_DV, C_MQ = 5120, 6656, 8192, 9728
C_GHG, C_GDA, C_GMEM = 10240, 11264, 12288
IN_COLS = 13312

ADAM_LR, ADAM_B1, ADAM_B2, ADAM_EPS, ADAM_WD, ADAM_STEP = 0.001, 0.9, 0.999, 1e-08, 0.01, 10


def _params(sem, vmem=VMEM_LIMIT_V7X):
    return pltpu.CompilerParams(dimension_semantics=sem, vmem_limit_bytes=vmem)


def _dot(a, b):
    return jnp.dot(a.astype(BF16), b.astype(BF16), preferred_element_type=F32)


def _dot_nt(a, b):
    return lax.dot_general(a.astype(BF16), b.astype(BF16), (((1,), (1,)), ((), ())),
                           preferred_element_type=F32)


def _dot_tn(a, b):
    return lax.dot_general(a.astype(BF16), b.astype(BF16), (((0,), (0,)), ((), ())),
                           preferred_element_type=F32)


def _sigmoid(v):
    return jax.nn.sigmoid(v.astype(F32))


def _ones(rows, cols):
    return (lax.broadcasted_iota(jnp.int32, (rows, cols), 0) >= 0).astype(BF16)


def _lane_sum(v):
    ones = _ones(HEAD, HEAD)
    hi = v.astype(BF16)
    mid = (v - hi.astype(F32)).astype(BF16)
    return jnp.dot(hi, ones, preferred_element_type=F32) + jnp.dot(mid, ones, preferred_element_type=F32)


def _row_mean(v):
    if v.shape[-1] == HEAD:
        return _lane_sum(v) * (1.0 / HEAD)
    return jnp.mean(v, axis=-1, keepdims=True)


def _rms(v):
    v = v.astype(F32)
    r = lax.rsqrt(_row_mean(v * v) + RMS_EPS)
    return v * r, r


def _rms_bwd(dy, xhat, r, gain):
    dxh = dy * gain
    dx = r * (dxh - xhat * _row_mean(dxh * xhat))
    return dx, dy * xhat


class _Carry:
    def __init__(self, arrays, out_shapes, aliases, sems, issue):
        self.arrays, self.out_shapes, self.aliases, self.sems, self.issue = arrays, out_shapes, aliases, sems, issue


NO_CARRY = object()


def _start_copies(copies):
    for parts, _, _ in copies:
        for cp in parts:
            cp.start()


def _wait_copies(copies):
    for _, whole, remote in copies:
        if remote:
            whole.wait_recv()
    for _, whole, remote in copies:
        if remote:
            whole.wait_send()
        else:
            whole.wait()


def _call(body, *, name, grid, in_specs, out_specs, out_shape, scratch_shapes, semantics, ins, carry=None,
          aliases=None):
    aliases = dict(aliases or {})
    if carry is None:
        res = pl.pallas_call(body, name=name, grid=grid, in_specs=in_specs, out_specs=out_specs, out_shape=out_shape,
                             scratch_shapes=scratch_shapes, input_output_aliases=aliases,
                             compiler_params=_params(semantics))(*ins)
        return list(res), []
    n_in, n_out, n_scr = len(ins), len(out_shape), len(scratch_shapes)
    c_in, c_out = len(carry.arrays), len(carry.out_shapes)

    def wrapped(*refs):
        pos = [0]

        def take(count):
            pos[0] += count
            return refs[pos[0] - count:pos[0]]

        own_in, carry_in = take(n_in), take(c_in)
        own_out, carry_out = take(n_out), take(c_out)
        own_scr, carry_sems = take(n_scr), take(len(carry.sems))
        ids = [pl.program_id(a) for a in range(len(grid))]
        first, last = ids[0] == 0, ids[0] == grid[0] - 1
        for a in range(1, len(grid)):
            first, last = first & (ids[a] == 0), last & (ids[a] == grid[a] - 1)

        @pl.when(first)
        def _():
            _start_copies(carry.issue(carry_in, carry_out, carry_sems, "start"))

        body(*own_in, *own_out, *own_scr)

        @pl.when(last)
        def _():
            _wait_copies(carry.issue(carry_in, carry_out, carry_sems, "wait"))

    res = pl.pallas_call(
        wrapped, name=name, grid=grid, in_specs=list(in_specs) + [ANY] * c_in,
        out_specs=list(out_specs) + [ANY] * c_out, out_shape=list(out_shape) + list(carry.out_shapes),
        input_output_aliases={**aliases, **{n_in + i: n_out + j for i, j in carry.aliases.items()}},
        scratch_shapes=list(scratch_shapes) + list(carry.sems),
        compiler_params=_params(("arbitrary",) * len(grid)))(*ins, *carry.arrays)
    return list(res[:n_out]), list(res[n_out:])


def _matmul(pairs, mode, out_dtype, *, tm, tn, tk=None, residual=None, carry=NO_CARRY, name):
    b_offs = [pr[2] if len(pr) > 2 else 0 for pr in pairs]
    pairs = [pr[:2] for pr in pairs]
    a0, b0 = pairs[0]
    if mode == "nn":
        (m, kk), n = a0.shape, b0.shape[1]
    elif mode == "nt":
        (m, kk), n = a0.shape, b0.shape[0]
    else:
        (kk, m), n = a0.shape, b0.shape[1]
    assert mode == "nt" or not any(b_offs)
    tm, tn = min(tm, m), min(tn, n)
    tk = kk if tk is None else tk
    nk = kk // tk
    assert m % tm == 0 and n % tn == 0 and kk % tk == 0, (name, m, n, kk)
    n_p = len(pairs)
    if mode == "tn":
        a_spec = pl.BlockSpec((tk, tm), lambda i, j, k: (k, i))
    else:
        a_spec = pl.BlockSpec((tm, tk), lambda i, j, k: (i, k))
    if mode == "nt":
        b_specs = [pl.BlockSpec((tn, tk), lambda i, j, k, o=o: (j, o * nk + k)) for o in b_offs]
    else:
        b_specs = [pl.BlockSpec((tk, tn), lambda i, j, k: (k, j))] * n_p
    o_spec = pl.BlockSpec((tm, tn), lambda i, j, k: (i, j))
    dot = {"nn": _dot, "nt": _dot_nt, "tn": _dot_tn}[mode]
    has_res = residual is not None

    def body(*refs):
        a_refs, b_refs = refs[:n_p], refs[n_p:2 * n_p]
        pos = 2 * n_p
        res_ref = refs[pos] if has_res else None
        pos += int(has_res)
        o_ref = refs[pos]
        part = dot(a_refs[0][...], b_refs[0][...])
        for a_r, b_r in zip(a_refs[1:], b_refs[1:]):
            part += dot(a_r[...], b_r[...])

        def finish(total):
            if has_res:
                total = total + res_ref[...]
            o_ref[...] = total.astype(out_dtype)

        if nk == 1:
            finish(part)
        else:
            acc_ref = refs[pos + 1]
            k = pl.program_id(2)

            @pl.when(k == 0)
            def _():
                acc_ref[...] = part

            @pl.when(k > 0)
            def _():
                acc_ref[...] += part

            @pl.when(k == nk - 1)
            def _():
                finish(acc_ref[...])

    ins = [a for a, _ in pairs] + [b for _, b in pairs]
    in_specs = [a_spec] * n_p + b_specs
    if has_res:
        ins.append(residual)
        in_specs.append(o_spec)
    (out,), carried = _call(
        body, name=name, grid=(m // tm, n // tn, nk), in_specs=in_specs, out_specs=[o_spec],
        out_shape=[jax.ShapeDtypeStruct((m, n), out_dtype)],
        scratch_shapes=[pltpu.VMEM((tm, tn), F32)] if nk > 1 else [],
        semantics=("parallel", "parallel", "arbitrary"), ins=ins, carry=None if carry is NO_CARRY else carry)
    return out if carry is NO_CARRY else (out, carried)


def _rms_fwd(x, gain, *, name):
    rows, dm = x.shape
    tm = min(512, rows)

    def body(x_ref, g_ref, h_ref):
        xhat, _ = _rms(x_ref[...])
        h_ref[...] = (xhat * g_ref[...]).astype(BF16)

    return pl.pallas_call(
        body, name=name, grid=(rows // tm,),
        in_specs=[pl.BlockSpec((tm, dm), lambda i: (i, 0)), pl.BlockSpec((1, dm), lambda i: (0, 0))],
        out_specs=pl.BlockSpec((tm, dm), lambda i: (i, 0)),
        out_shape=jax.ShapeDtypeStruct((rows, dm), BF16),
        compiler_params=_params(("parallel",)),
    )(x, gain)


def _rms_bwd_rows(dh, x, gain, dres, *, name):
    rows, dm = x.shape
    tm = min(512, rows)
    has_res = dres is not None

    def body(*refs):
        dh_ref, x_ref, g_ref = refs[:3]
        res_ref = refs[3] if has_res else None
        dx_ref, dxb_ref, dg_ref = refs[3 + int(has_res):]
        xhat, r = _rms(x_ref[...])
        dx, dgr = _rms_bwd(dh_ref[...], xhat, r, g_ref[...])
        if has_res:
            dx = dx + res_ref[...]
        dx_ref[...] = dx
        dxb_ref[...] = dx.astype(BF16)

        @pl.when(pl.program_id(0) == 0)
        def _():
            dg_ref[...] = jnp.zeros_like(dg_ref)

        dg_ref[...] += jnp.sum(dgr, axis=0, keepdims=True)

    row = pl.BlockSpec((tm, dm), lambda i: (i, 0))
    vec = pl.BlockSpec((1, dm), lambda i: (0, 0))
    return pl.pallas_call(
        body, name=name, grid=(rows // tm,),
        in_specs=[row, row, vec] + ([row] if has_res else []),
        out_specs=[row, row, vec],
        out_shape=[jax.ShapeDtypeStruct((rows, dm), F32), jax.ShapeDtypeStruct((rows, dm), BF16),
                   jax.ShapeDtypeStruct((1, dm), F32)],
        compiler_params=_params(("arbitrary",)),
    )(*([dh, x, gain] + ([dres] if has_res else [])))


def _out_norm(a, w, x, gain, *, name):
    rows, dm = x.shape
    tm = min(512, rows)

    def body(a_ref, w_ref, x_ref, g_ref, x1_ref, h_ref):
        x1 = x_ref[...] + _dot(a_ref[...], w_ref[...])
        x1_ref[...] = x1
        xhat, _ = _rms(x1)
        h_ref[...] = (xhat * g_ref[...]).astype(BF16)

    row = lambda wd: pl.BlockSpec((tm, wd), lambda i: (i, 0))
    return pl.pallas_call(
        body, name=name, grid=(rows // tm,),
        in_specs=[row(a.shape[1]), pl.BlockSpec(w.shape, lambda i: (0, 0)), row(dm), pl.BlockSpec((1, dm), lambda i: (0, 0))],
        out_specs=[row(dm), row(dm)],
        out_shape=[jax.ShapeDtypeStruct((rows, dm), F32), jax.ShapeDtypeStruct((rows, dm), BF16)],
        compiler_params=_params(("parallel",)),
    )(a, w, x, gain)


def _out_loss(u, w, x1, tgt, *, name):
    rows, dm = x1.shape
    tm = min(256, rows)
    steps = rows // tm

    def body(u_ref, w_ref, x_ref, t_ref, dy_ref, dyb_ref, loss_ref, acc_ref):
        i = pl.program_id(0)
        diff = x_ref[...] + _dot(u_ref[...], w_ref[...]) - t_ref[...]
        dy = diff * (1.0 / dm)
        dy_ref[...] = dy
        dyb_ref[...] = dy.astype(BF16)

        @pl.when(i == 0)
        def _():
            acc_ref[...] = jnp.zeros_like(acc_ref)

        acc_ref[...] += jnp.sum(diff * diff, axis=0, keepdims=True)

        @pl.when(i == steps - 1)
        def _():
            loss_ref[...] = jnp.full((1, HEAD), 0.5 / dm, F32) * jnp.sum(acc_ref[...])

    row = lambda wd: pl.BlockSpec((tm, wd), lambda i: (i, 0))
    return pl.pallas_call(
        body, name=name, grid=(steps,),
        in_specs=[row(u.shape[1]), pl.BlockSpec(w.shape, lambda i: (0, 0)), row(dm), row(dm)],
        out_specs=[row(dm), row(dm), pl.BlockSpec((1, HEAD), lambda i: (0, 0))],
        out_shape=[jax.ShapeDtypeStruct((rows, dm), F32), jax.ShapeDtypeStruct((rows, dm), BF16),
                   jax.ShapeDtypeStruct((1, HEAD), F32)],
        scratch_shapes=[pltpu.VMEM((1, dm), F32)],
        compiler_params=_params(("arbitrary",)),
    )(u, w, x1, tgt)


def _gla_block_terms(q_raw, f_logit, lb, rev):
    sig = _sigmoid(f_logit)
    forget = lb + (1.0 - lb) * sig
    k = 1.0 - forget
    b = _chunk_cumsum(jnp.log(forget), rev)
    qs = _sigmoid(q_raw)
    eb = jnp.exp(b)
    emb = jnp.exp(-b)
    qt = (q_raw * qs * HG_SCALE) * eb
    kt = k * emb
    return sig, forget, k, b, qs, eb, emb, qt, kt


def _chunk_cumsum(v, rev):
    n = v.shape[0]
    pos = lax.broadcasted_iota(jnp.int32, v.shape, 0) & (CHUNK - 1)
    step = 1
    while step < CHUNK:
        if rev:
            shifted, keep = pltpu.roll(v, n - step, 0), pos < CHUNK - step
        else:
            shifted, keep = pltpu.roll(v, step, 0), pos >= step
        v = v + jnp.where(keep, shifted, 0.0)
        step *= 2
    return v


def _tri_mask(n, rev):
    row, col = np.arange(n)[:, None], np.arange(n)[None, :]
    same = (row // CHUNK) == (col // CHUNK)
    return jnp.asarray((same & ((row <= col) if rev else (row >= col))).astype(np.float32))


def _chunk_order(ncb, rev):
    order = range(ncb - 1, -1, -1) if rev else range(ncb)
    return [(c, c * CHUNK if rev else c * CHUNK + CHUNK - 1) for c in order]


def _gla_fwd(proj, lb, *, f_off, rev, name):
    rows = proj.shape[0]
    tb = min(256, rows)
    nb, ncb = rows // tb, tb // CHUNK

    def tmap(n):
        return nb - 1 - n if rev else n

    def body(tri_ref, q_ref, f_ref, v_ref, lb_ref, o_ref, st_ref, s_ref):
        @pl.when(pl.program_id(1) == 0)
        def _():
            s_ref[...] = jnp.zeros_like(s_ref)

        tri = tri_ref[...] > 0.5
        for hh in range(GLA_HEADS_PER_STEP):
            cs = slice(hh * HEAD, (hh + 1) * HEAD)
            v = v_ref[:, cs]
            _, _, k, b, _, _, _, qt, kt = _gla_block_terms(q_ref[:, cs].astype(F32), f_ref[:, cs].astype(F32),
                                                           lb_ref[:, cs], rev)
            o_intra = _dot(jnp.where(tri, _dot_nt(qt, kt), 0.0), v)
            chunks = []
            for c, last in _chunk_order(ncb, rev):
                sl = slice(c * CHUNK, (c + 1) * CHUNK)
                bl = b[last:last + 1, :]
                kh = k[sl] * jnp.exp(bl - b[sl])
                chunks.append((c, sl, jnp.exp(bl), _dot_tn(v[sl], kh)))
            s_t = s_ref[hh]
            for c, sl, ebl, kv in chunks:
                st_ref[hh, c] = s_t
                o_ref[sl, cs] = o_intra[sl] + _dot_nt(qt[sl], s_t)
                s_t = ebl * s_t + kv
            s_ref[hh] = s_t

    hps = GLA_HEADS_PER_STEP
    col = lambda off: pl.BlockSpec((tb, hps * HEAD), lambda h, n: (tmap(n), off // (hps * HEAD) + h))
    return pl.pallas_call(
        body, name=name, grid=(HG_HEADS // hps, nb),
        in_specs=[pl.BlockSpec((tb, tb), lambda h, n: (0, 0)), col(C_HQ), col(f_off), col(C_HI),
                  pl.BlockSpec((1, hps * HEAD), lambda h, n: (0, h))],
        out_specs=[pl.BlockSpec((tb, hps * HEAD), lambda h, n: (tmap(n), h)),
                   pl.BlockSpec((hps, ncb, HEAD, HEAD), lambda h, n: (h, tmap(n), 0, 0))],
        out_shape=[jax.ShapeDtypeStruct((rows, HG_HEADS * HEAD), F32),
                   jax.ShapeDtypeStruct((HG_HEADS, rows // CHUNK, HEAD, HEAD), F32)],
        scratch_shapes=[pltpu.VMEM((hps, HEAD, HEAD), F32)],
        compiler_params=_params(("parallel", "arbitrary")),
    )(_tri_mask(tb, rev), proj, proj, proj, lb)


def _gla_bwd(proj, lb, do, states, prev, dproj, *, f_off, rev, carry=NO_CARRY, name):
    rows = proj.shape[0]
    tb = min(256, rows)
    nb, ncb = rows // tb, tb // CHUNK
    has_prev = prev is not None
    qv_dtype = BF16 if has_prev else F32

    def tmap(n):
        return n if rev else nb - 1 - n

    def body(*refs):
        tri_ref, q_ref, f_ref, v_ref, lb_ref, do_ref, st_ref = refs[:7]
        pq_ref, pv_ref = refs[8:10] if has_prev else (None, None)
        (dq_ref, df_ref, dv_ref, dlb_ref,
         ds_ref, dqt_scr, dk_scr, db_scr, dbl_scr, dv_scr) = refs[8 + 2 * int(has_prev):]

        @pl.when(pl.program_id(1) == 0)
        def _():
            ds_ref[...] = jnp.zeros_like(ds_ref)
            dlb_ref[...] = jnp.zeros_like(dlb_ref)

        tri = tri_ref[...] > 0.5
        for hh in range(GLA_HEADS_PER_STEP):
            cs = slice(hh * HEAD, (hh + 1) * HEAD)
            lbv = lb_ref[:, cs]
            q_raw = q_ref[:, cs].astype(F32)
            v, dout = v_ref[:, cs], do_ref[:, cs].astype(BF16)
            sig, forget, k, b, qs, eb, emb, qt, kt = _gla_block_terms(q_raw, f_ref[:, cs].astype(F32), lbv, rev)
            a = jnp.where(tri, _dot_nt(qt, kt), 0.0)
            da = jnp.where(tri, _dot_nt(dout, v), 0.0)
            dv_intra = _dot_tn(a, dout)
            dqt_intra = _dot(da, kt)
            dkt = _dot_tn(da, qt)
            chunks = []
            for c, last in reversed(_chunk_order(ncb, rev)):
                sl = slice(c * CHUNK, (c + 1) * CHUNK)
                bl = b[last:last + 1, :]
                e = jnp.exp(bl - b[sl])
                s_t = st_ref[hh, c]
                dqt_scr[sl, cs] = dqt_intra[sl] + _dot(dout[sl], s_t)
                chunks.append((sl, jnp.exp(bl), e, k[sl] * e, s_t, _dot_tn(dout[sl], qt[sl])))
            ds_t = ds_ref[hh]
            for sl, ebl, e, kh, s_t, grow in chunks:
                dkh = _dot(v[sl], ds_t)
                dv_scr[sl, cs] = dv_intra[sl] + _dot_nt(kh, ds_t)
                dk_scr[sl, cs] = dkt[sl] * emb[sl] + dkh * e
                khd = kh * dkh
                dbl = jnp.sum(khd, axis=0, keepdims=True) + ebl * jnp.sum(ds_t * s_t, axis=0, keepdims=True)
                db_scr[sl, cs] = khd
                dbl_scr[sl, cs] = jnp.broadcast_to(dbl, (CHUNK, HEAD))
                ds_t = grow + ds_t * ebl
            ds_ref[hh] = ds_t
            dqt = dqt_scr[:, cs]
            dlogf = _chunk_cumsum(qt * dqt - kt * dkt - db_scr[:, cs], not rev) + dbl_scr[:, cs]
            dforget = dlogf / forget - dk_scr[:, cs]
            df_ref[:, cs] = (dforget * (1.0 - lbv) * sig * (1.0 - sig)).astype(BF16)
            dlb_ref[:, cs] += jnp.sum(dforget * (1.0 - sig), axis=0, keepdims=True)
            dqr = dqt * eb * (HG_SCALE * qs * (1.0 + q_raw * (1.0 - qs)))
            dv = dv_scr[:, cs]
            if has_prev:
                dqr = dqr + pq_ref[:, cs]
                dv = dv + pv_ref[:, cs]
            dq_ref[:, cs] = dqr.astype(qv_dtype)
            dv_ref[:, cs] = dv.astype(qv_dtype)

    hps = GLA_HEADS_PER_STEP
    col = lambda off: pl.BlockSpec((tb, hps * HEAD), lambda h, n: (tmap(n), off // (hps * HEAD) + h))
    blk = pl.BlockSpec((tb, hps * HEAD), lambda h, n: (tmap(n), h))
    vec = pl.BlockSpec((1, hps * HEAD), lambda h, n: (0, h))
    wide = HG_HEADS * HEAD
    into = jax.ShapeDtypeStruct(dproj.shape, dproj.dtype)
    outs, carried = _call(
        body, name=name, grid=(HG_HEADS // hps, nb),
        in_specs=[pl.BlockSpec((tb, tb), lambda h, n: (0, 0)), col(C_HQ), col(f_off), col(C_HI), vec, blk,
                  pl.BlockSpec((hps, ncb, HEAD, HEAD), lambda h, n: (h, tmap(n), 0, 0)), ANY]
                 + ([blk, blk] if has_prev else []),
        out_specs=[col(C_HQ) if has_prev else blk, blk if has_prev else col(f_off), blk, vec],
        out_shape=[into if has_prev else jax.ShapeDtypeStruct((rows, wide), qv_dtype),
                   jax.ShapeDtypeStruct((rows, wide), BF16) if has_prev else into,
                   jax.ShapeDtypeStruct((rows, wide), qv_dtype), jax.ShapeDtypeStruct((1, wide), F32)],
        aliases={7: 0 if has_prev else 1},
        scratch_shapes=[pltpu.VMEM((hps, HEAD, HEAD), F32)] + [pltpu.VMEM((tb, hps * HEAD), F32)] * 5,
        semantics=("parallel", "arbitrary"),
        ins=[_tri_mask(tb, rev), proj, proj, proj, lb, do, states, dproj] + (list(prev) if has_prev else []),
        carry=None if carry is NO_CARRY else carry)
    return outs if carry is NO_CARRY else (outs, carried)


def _fill_columns(dproj, parts, col, *, name):
    rows, wd = parts[0].shape
    tm = min(512, rows)
    n = len(parts)

    def body(*refs):
        o_ref = refs[n + 1]
        for j in range(n):
            o_ref[:, j * wd:(j + 1) * wd] = refs[j][...]

    return pl.pallas_call(
        body, name=name, grid=(rows // tm,),
        in_specs=[pl.BlockSpec((tm, wd), lambda i: (i, 0))] * n + [ANY],
        out_specs=pl.BlockSpec((pl.Element(tm), pl.Element(n * wd)), lambda i: (i * tm, col)),
        out_shape=jax.ShapeDtypeStruct(dproj.shape, dproj.dtype), input_output_aliases={n: 0},
        compiler_params=_params(("parallel",)),
    )(*parts, dproj)


def _hg_out_fwd(o_fw, o_bw, proj, gain, *, name):
    rows = o_fw.shape[0]
    tm = min(512, rows)
    wide = HG_HEADS * HEAD

    def body(a_ref, b_ref, g_ref, gain_ref, o_ref):
        for h in range(HG_HEADS):
            sl = slice(h * HEAD, (h + 1) * HEAD)
            xhat, _ = _rms(a_ref[:, sl] + b_ref[:, sl])
            gate = g_ref[:, sl].astype(F32)
            o_ref[:, sl] = (xhat * gain_ref[...] * (gate * _sigmoid(gate))).astype(BF16)

    row = pl.BlockSpec((tm, wide), lambda i: (i, 0))
    return pl.pallas_call(
        body, name=name, grid=(rows // tm,),
        in_specs=[row, row, pl.BlockSpec((tm, wide), lambda i: (i, C_HG // wide)),
                  pl.BlockSpec((1, HEAD), lambda i: (0, 0))],
        out_specs=row, out_shape=jax.ShapeDtypeStruct((rows, wide), BF16),
        compiler_params=_params(("parallel",)),
    )(o_fw, o_bw, proj, gain)


def _hg_out_bwd(dout, o_fw, o_bw, proj, gain, dproj, *, name):
    rows = o_fw.shape[0]
    tm = min(512, rows)
    wide = HG_HEADS * HEAD

    def body(d_ref, a_ref, b_ref, g_ref, gain_ref, dp_in, dgate_ref, do_ref, dgain_ref):
        del dp_in

        @pl.when(pl.program_id(0) == 0)
        def _():
            dgain_ref[...] = jnp.zeros_like(dgain_ref)

        dgain = jnp.zeros((1, HEAD), F32)
        for h in range(HG_HEADS):
            sl = slice(h * HEAD, (h + 1) * HEAD)
            xhat, r = _rms(a_ref[:, sl] + b_ref[:, sl])
            gate, dy = g_ref[:, sl].astype(F32), d_ref[:, sl]
            gs = _sigmoid(gate)
            dgate_ref[:, sl] = (dy * xhat * gain_ref[...] * (gs * (1.0 + gate * (1.0 - gs)))).astype(BF16)
            dx, dgr = _rms_bwd(dy * (gate * gs), xhat, r, gain_ref[...])
            do_ref[:, sl] = dx
            dgain = dgain + jnp.sum(dgr, axis=0, keepdims=True)
        dgain_ref[...] += dgain

    row = pl.BlockSpec((tm, wide), lambda i: (i, 0))
    gate = pl.BlockSpec((tm, wide), lambda i: (i, C_HG // wide))
    vec = pl.BlockSpec((1, HEAD), lambda i: (0, 0))
    return pl.pallas_call(
        body, name=name, grid=(rows // tm,),
        in_specs=[row, row, row, gate, vec, ANY],
        out_specs=[gate, row, vec],
        out_shape=[jax.ShapeDtypeStruct(dproj.shape, dproj.dtype), jax.ShapeDtypeStruct((rows, wide), F32),
                   jax.ShapeDtypeStruct((1, HEAD), F32)],
        input_output_aliases={5: 0},
        compiler_params=_params(("arbitrary",)),
    )(dout, o_fw, o_bw, proj, gain, dproj)


def _strided_rows(ref, r, count, d):
    return ref[...] if d == 1 else ref[pl.ds(r, count, stride=d), :]


def _store_strided(ref, r, count, d, val):
    if d == 1:
        ref[...] = val
    else:
        ref[pl.ds(r, count, stride=d), :] = val


def _da_prep(proj, q_gain, k_gain, g, *, name):
    d = DA_GROUPS[g][0]
    rows = proj.shape[0]
    rb = min(512, rows)
    tn = rb // d

    def body(q_ref, k_ref, v_ref, qg_ref, kg_ref, qo_ref, ko_ref, vo_ref, qf_ref, kf_ref, vf_ref):
        for h in range(DA_GH):
            cs = slice(h * HEAD, (h + 1) * HEAD)
            for src, dst in ((q_ref, qf_ref), (k_ref, kf_ref), (v_ref, vf_ref)):
                dst[...] = src[:, cs].astype(F32)
            for r in range(d):
                qhat, _ = _rms(_strided_rows(qf_ref, r, tn, d))
                khat, _ = _rms(_strided_rows(kf_ref, r, tn, d))
                qo_ref[h, r] = (qhat * qg_ref[...]).astype(BF16)
                ko_ref[h, r] = (khat * kg_ref[...]).astype(BF16)
                vo_ref[h, r] = _strided_rows(vf_ref, r, tn, d).astype(BF16)

    wide = DA_GH * HEAD
    col = lambda off: pl.BlockSpec((rb, wide), lambda i: (i, off // wide + g))
    vec = pl.BlockSpec((1, HEAD), lambda i: (0, 0))
    out = pl.BlockSpec((DA_GH, d, tn, HEAD), lambda i: (0, 0, i, 0))
    shape = jax.ShapeDtypeStruct((DA_GH, d, rows // d, HEAD), BF16)
    return pl.pallas_call(
        body, name=name, grid=(rows // rb,),
        in_specs=[col(C_DQ), col(C_DK), col(C_DV), vec, vec],
        out_specs=[out, out, out], out_shape=[shape, shape, shape],
        scratch_shapes=[pltpu.VMEM((rb, HEAD), F32)] * 3,
        compiler_params=_params(("parallel",)),
    )(proj, proj, proj, q_gain, k_gain)


def _slopes(g):
    idx = np.arange(g * DA_GH + 1, (g + 1) * DA_GH + 1)
    s = (2.0 ** (-8.0 * idx / (DA_GH * len(DA_GROUPS)))).astype(np.float32)
    return jnp.asarray(np.broadcast_to(s[:, None, None], (DA_GH, 8, HEAD)).copy())


def _band_window(ld, t, radius):
    win = min(2 * t, ld)
    assert t // 2 >= radius or win == ld
    return win


def _band_scores(q, k, q0, start, slope, d, radius):
    t, win = q.shape[0], k.shape[0]
    row = lax.broadcasted_iota(jnp.int32, (t, win), 0)
    col = lax.broadcasted_iota(jnp.int32, (t, win), 1)
    rel = jnp.abs((start - q0) + col - row)
    return _dot_nt(q, k) * ATT_SCALE + jnp.where(rel <= radius, -slope * (d * rel).astype(F32), NEG_INF)


def _band_fwd(qr, kr, vr, g, *, name):
    d, radius = DA_GROUPS[g]
    _, _, ld, _ = qr.shape
    t = min(HEAD, ld)
    qb = min(BAND_QBLOCKS, ld // t)
    win = _band_window(ld, t, radius)

    def body(q_ref, k_ref, v_ref, sl_ref, o_ref, lse_ref):
        i = pl.program_id(2)
        slope = sl_ref[0:1, 0:1]
        for j in range(qb):
            sl = slice(j * t, (j + 1) * t)
            q0 = (i * qb + j) * t
            start = pl.multiple_of(jnp.clip(q0 - t // 2, 0, ld - win), t // 2)
            s = _band_scores(q_ref[sl, :], k_ref[pl.ds(start, win), :], q0, start, slope, d, radius)
            m = jnp.max(s, axis=-1, keepdims=True)
            p = jnp.exp(s - m).astype(BF16)
            l = jnp.dot(p, _ones(win, HEAD), preferred_element_type=F32)
            o_ref[sl, :] = _dot(p, v_ref[pl.ds(start, win), :]) / l
            lse_ref[sl, :] = m + jnp.log(l)

    own = pl.BlockSpec((None, None, qb * t, HEAD), lambda h, r, i: (h, r, i, 0))
    seq = pl.BlockSpec((None, None, ld, HEAD), lambda h, r, i: (h, r, 0, 0))
    shape = jax.ShapeDtypeStruct(qr.shape, F32)
    return pl.pallas_call(
        body, name=name, grid=(DA_GH, d, ld // (qb * t)),
        in_specs=[own, seq, seq, pl.BlockSpec((None, 8, HEAD), lambda h, r, i: (h, 0, 0))],
        out_specs=[own, own], out_shape=[shape, shape],
        compiler_params=_params(("parallel", "parallel", "parallel")),
    )(qr, kr, vr, _slopes(g))


def _band_bwd(qr, kr, vr, dor, lser, deltar, g, *, name):
    d, radius = DA_GROUPS[g]
    _, _, ld, _ = qr.shape
    t = min(HEAD, ld)
    qb = min(BAND_QBLOCKS, ld // t)
    win = _band_window(ld, t, radius)

    def body(q_ref, k_ref, v_ref, do_ref, lse_ref, dl_ref, sl_ref, dq_ref, dk_ref, dv_ref):
        i = pl.program_id(2)

        @pl.when(i == 0)
        def _():
            dk_ref[...] = jnp.zeros_like(dk_ref)
            dv_ref[...] = jnp.zeros_like(dv_ref)

        slope = sl_ref[0:1, 0:1]
        for j in range(qb):
            sl = slice(j * t, (j + 1) * t)
            q0 = (i * qb + j) * t
            start = pl.multiple_of(jnp.clip(q0 - t // 2, 0, ld - win), t // 2)
            rows = pl.ds(start, win)
            q, dout, k, v = q_ref[sl, :], do_ref[sl, :], k_ref[rows, :], v_ref[rows, :]
            p = jnp.exp(_band_scores(q, k, q0, start, slope, d, radius) - lse_ref[sl, 0:1])
            ds = p * (_dot_nt(dout, v) - dl_ref[sl, 0:1])
            dq_ref[sl, :] = _dot(ds, k) * ATT_SCALE
            dk_ref[rows, :] += _dot_tn(ds, q) * ATT_SCALE
            dv_ref[rows, :] += _dot_tn(p, dout)

    own = pl.BlockSpec((None, None, qb * t, HEAD), lambda h, r, i: (h, r, i, 0))
    seq = pl.BlockSpec((None, None, ld, HEAD), lambda h, r, i: (h, r, 0, 0))
    shape = jax.ShapeDtypeStruct(qr.shape, F32)
    return pl.pallas_call(
        body, name=name, grid=(DA_GH, d, ld // (qb * t)),
        in_specs=[own, seq, seq, own, own, own, pl.BlockSpec((None, 8, HEAD), lambda h, r, i: (h, 0, 0))],
        out_specs=[own, seq, seq], out_shape=[shape, shape, shape],
        compiler_params=_params(("parallel", "parallel", "arbitrary")),
    )(qr, kr, vr, dor, lser, deltar, _slopes(g))


def _da_merge(outs, lses, rows, *, name):
    rb = min(512, rows)
    wide = DA_GH * HEAD

    def body(*refs):
        o_refs, l_refs = refs[0:3], refs[3:6]
        o_ref, lse_ref = refs[6:8]
        on_refs, ln_refs = refs[8:11], refs[11:14]
        for h in range(DA_GH):
            cs = slice(h * HEAD, (h + 1) * HEAD)
            for g, (d, _) in enumerate(DA_GROUPS):
                tn = rb // d
                for r in range(d):
                    _store_strided(on_refs[g], r, tn, d, o_refs[g][h, r])
                    _store_strided(ln_refs[g], r, tn, d, l_refs[g][h, r])
            l0, l1, l2 = ln_refs[0][...], ln_refs[1][...], ln_refs[2][...]
            m = jnp.maximum(jnp.maximum(l0, l1), l2)
            e0, e1, e2 = jnp.exp(l0 - m), jnp.exp(l1 - m), jnp.exp(l2 - m)
            tot = e0 + e1 + e2
            o_ref[:, cs] = (e0 * on_refs[0][...] + e1 * on_refs[1][...] + e2 * on_refs[2][...]) / tot
            lse_ref[:, cs] = m + jnp.log(tot)

    res = lambda d: pl.BlockSpec((DA_GH, d, rb // d, HEAD), lambda i: (0, 0, i, 0))
    nat = pl.BlockSpec((rb, wide), lambda i: (i, 0))
    shape = jax.ShapeDtypeStruct((rows, wide), F32)
    return pl.pallas_call(
        body, name=name, grid=(rows // rb,),
        in_specs=[res(d) for d, _ in DA_GROUPS] * 2,
        out_specs=[nat, nat], out_shape=[shape, shape],
        scratch_shapes=[pltpu.VMEM((rb, HEAD), F32)] * 6,
        compiler_params=_params(("parallel",)),
    )(*outs, *lses)


def _da_bwd_prep(dout, o, lse, *, name):
    rows = o.shape[0]
    rb = min(512, rows)
    wide = DA_GH * HEAD

    def body(d_ref, o_ref, l_ref, *outs):
        d_scr, l_scr, delta_scr = outs[9:12]
        for h in range(DA_GH):
            cs = slice(h * HEAD, (h + 1) * HEAD)
            dv = d_ref[:, cs]
            d_scr[...] = dv
            l_scr[...] = l_ref[:, cs]
            delta_scr[...] = _lane_sum(dv * o_ref[:, cs])
            for g, (d, _) in enumerate(DA_GROUPS):
                tn = rb // d
                for r in range(d):
                    outs[3 * g][h, r] = _strided_rows(d_scr, r, tn, d).astype(BF16)
                    outs[3 * g + 1][h, r] = _strided_rows(l_scr, r, tn, d)
                    outs[3 * g + 2][h, r] = _strided_rows(delta_scr, r, tn, d)

    nat = pl.BlockSpec((rb, wide), lambda i: (i, 0))
    out_specs, out_shape = [], []
    for d, _ in DA_GROUPS:
        for dt in (BF16, F32, F32):
            out_specs.append(pl.BlockSpec((DA_GH, d, rb // d, HEAD), lambda i: (0, 0, i, 0)))
            out_shape.append(jax.ShapeDtypeStruct((DA_GH, d, rows // d, HEAD), dt))
    return pl.pallas_call(
        body, name=name, grid=(rows // rb,),
        in_specs=[nat, nat, nat], out_specs=out_specs, out_shape=out_shape,
        scratch_shapes=[pltpu.VMEM((rb, HEAD), F32)] * 3,
        compiler_params=_params(("parallel",)),
    )(dout, o, lse)


def _da_prep_bwd(dqr, dkr, dvr, proj, q_gain, k_gain, g, *, name):
    d = DA_GROUPS[g][0]
    rows = proj.shape[0]
    rb = min(512, rows)
    tn = rb // d
    wide = DA_GH * HEAD

    def body(dq_ref, dk_ref, dv_ref, q_ref, k_ref, qg_ref, kg_ref, oq_ref, ok_ref, ov_ref, gq_ref, gk_ref, *nat_refs):
        @pl.when(pl.program_id(0) == 0)
        def _():
            gq_ref[...] = jnp.zeros_like(gq_ref)
            gk_ref[...] = jnp.zeros_like(gk_ref)

        for h in range(DA_GH):
            cs = slice(h * HEAD, (h + 1) * HEAD)
            for j, src in enumerate((dq_ref, dk_ref, dv_ref)):
                for r in range(d):
                    _store_strided(nat_refs[j], r, tn, d, src[h, r])
            ov_ref[:, cs] = nat_refs[2][...].astype(BF16)
            for j, (x_ref, gn_ref, out_ref, acc_ref) in enumerate(((q_ref, qg_ref, oq_ref, gq_ref),
                                                                    (k_ref, kg_ref, ok_ref, gk_ref))):
                xhat, r = _rms(x_ref[:, cs])
                dx, dgr = _rms_bwd(nat_refs[j][...], xhat, r, gn_ref[...])
                out_ref[:, cs] = dx.astype(BF16)
                acc_ref[...] += jnp.sum(dgr, axis=0, keepdims=True)

    res = pl.BlockSpec((DA_GH, d, tn, HEAD), lambda i: (0, 0, i, 0))
    col = lambda off: pl.BlockSpec((rb, wide), lambda i: (i, off // wide + g))
    vec = pl.BlockSpec((1, HEAD), lambda i: (0, 0))
    nat = pl.BlockSpec((rb, wide), lambda i: (i, 0))
    shape = jax.ShapeDtypeStruct((rows, wide), BF16)
    gshape = jax.ShapeDtypeStruct((1, HEAD), F32)
    return pl.pallas_call(
        body, name=name, grid=(rows // rb,),
        in_specs=[res, res, res, col(C_DQ), col(C_DK), vec, vec],
        out_specs=[nat, nat, nat, vec, vec], out_shape=[shape, shape, shape, gshape, gshape],
        scratch_shapes=[pltpu.VMEM((rb, HEAD), F32)] * 3,
        compiler_params=_params(("arbitrary",)),
    )(dqr, dkr, dvr, proj, proj, q_gain, k_gain)


def _mem_fwd(proj, kv, q_gain, k_gain, *, name):
    rows = proj.shape[0]
    tm = min(512, rows)
    n_mem = kv.shape[0]

    def body(q_ref, k_ref, v_ref, qg_ref, kg_ref, o_ref):
        qhat, _ = _rms(q_ref[...])
        khat, _ = _rms(k_ref[...])
        s = _dot_nt(qhat * qg_ref[...], khat * kg_ref[...]) * ATT_SCALE
        p = jnp.exp(s - jnp.max(s, axis=-1, keepdims=True))
        p = p / jnp.sum(p, axis=-1, keepdims=True)
        o_ref[...] = _dot(p, v_ref[...]).astype(BF16)

    vec = pl.BlockSpec((1, HEAD), lambda i, h: (0, 0))
    return pl.pallas_call(
        body, name=name, grid=(rows // tm, MEM_HEADS),
        in_specs=[pl.BlockSpec((tm, HEAD), lambda i, h: (i, C_MQ // HEAD + h)),
                  pl.BlockSpec((n_mem, HEAD), lambda i, h: (0, h)),
                  pl.BlockSpec((n_mem, HEAD), lambda i, h: (0, MEM_HEADS + h)), vec, vec],
        out_specs=pl.BlockSpec((tm, HEAD), lambda i, h: (i, h)),
        out_shape=jax.ShapeDtypeStruct((rows, MEM_HEADS * HEAD), BF16),
        compiler_params=_params(("parallel", "parallel")),
    )(proj, kv, kv, q_gain, k_gain)


def _mem_bwd(dout, proj, kv, q_gain, k_gain, dproj, *, name):
    rows = proj.shape[0]
    tm = min(512, rows)
    steps = rows // tm
    n_mem = kv.shape[0]

    def body(d_ref, q_ref, k_ref, v_ref, qg_ref, kg_ref, dp_in, dq_ref, dk_ref, dv_ref, gq_ref, gk_ref, dkn_ref):
        del dp_in
        h, i = pl.program_id(0), pl.program_id(1)

        @pl.when((h == 0) & (i == 0))
        def _():
            gq_ref[...] = jnp.zeros_like(gq_ref)
            gk_ref[...] = jnp.zeros_like(gk_ref)

        @pl.when(i == 0)
        def _():
            dkn_ref[...] = jnp.zeros_like(dkn_ref)
            dv_ref[...] = jnp.zeros_like(dv_ref)

        qhat, rq = _rms(q_ref[...])
        khat, rk = _rms(k_ref[...])
        qn, kn = qhat * qg_ref[...], khat * kg_ref[...]
        s = _dot_nt(qn, kn) * ATT_SCALE
        p = jnp.exp(s - jnp.max(s, axis=-1, keepdims=True))
        p = p / jnp.sum(p, axis=-1, keepdims=True)
        dout = d_ref[...]
        dp = _dot_nt(dout, v_ref[...])
        ds = p * (dp - jnp.sum(p * dp, axis=-1, keepdims=True))
        dv_ref[...] += _dot_tn(p, dout)
        dkn_ref[...] += _dot_tn(ds, qn) * ATT_SCALE
        dq, dgr = _rms_bwd(_dot(ds, kn) * ATT_SCALE, qhat, rq, qg_ref[...])
        dq_ref[...] = dq.astype(BF16)
        gq_ref[...] += jnp.sum(dgr, axis=0, keepdims=True)

        @pl.when(i == steps - 1)
        def _():
            dk, dgk = _rms_bwd(dkn_ref[...], khat, rk, kg_ref[...])
            dk_ref[...] = dk
            gk_ref[...] += jnp.sum(dgk, axis=0, keepdims=True)

    vec = pl.BlockSpec((1, HEAD), lambda h, i: (0, 0))
    memh = pl.BlockSpec((n_mem, HEAD), lambda h, i: (0, h))
    tok = pl.BlockSpec((tm, HEAD), lambda h, i: (i, h))
    mq = pl.BlockSpec((tm, HEAD), lambda h, i: (i, C_MQ // HEAD + h))
    gshape = jax.ShapeDtypeStruct((1, HEAD), F32)
    return pl.pallas_call(
        body, name=name, grid=(MEM_HEADS, steps),
        in_specs=[tok, mq, memh, pl.BlockSpec((n_mem, HEAD), lambda h, i: (0, MEM_HEADS + h)), vec, vec, ANY],
        out_specs=[mq, memh, memh, vec, vec],
        out_shape=[jax.ShapeDtypeStruct(dproj.shape, dproj.dtype),
                   jax.ShapeDtypeStruct((n_mem, MEM_HEADS * HEAD), F32),
                   jax.ShapeDtypeStruct((n_mem, MEM_HEADS * HEAD), F32), gshape, gshape],
        input_output_aliases={6: 0},
        scratch_shapes=[pltpu.VMEM((n_mem, HEAD), F32)],
        compiler_params=_params(("arbitrary", "arbitrary")),
    )(dout, proj, kv, kv, q_gain, k_gain, dproj)


def _branch_fwd(o_hg, o_da, o_mem, proj, wp_hg, wp_da, wp_mem, *, name):
    rows = o_hg.shape[0]
    tm = min(256, rows)

    def body(a_ref, b_ref, c_ref, ga_ref, gb_ref, gc_ref, wa_ref, wb_ref, wc_ref, o_ref):
        merged = _sigmoid(ga_ref[...]) * _dot(a_ref[...], wa_ref[...])
        merged += _sigmoid(gb_ref[...]) * _dot(b_ref[...], wb_ref[...])
        merged += _sigmoid(gc_ref[...]) * _dot(c_ref[...], wc_ref[...])
        o_ref[...] = merged.astype(BF16)

    row = lambda w: pl.BlockSpec((tm, w), lambda i: (i, 0))
    gate = lambda off: pl.BlockSpec((tm, D_MODEL), lambda i: (i, off // D_MODEL))
    full = lambda a: pl.BlockSpec(a.shape, lambda i: (0, 0))
    return pl.pallas_call(
        body, name=name, grid=(rows // tm,),
        in_specs=[row(o_hg.shape[1]), row(o_da.shape[1]), row(o_mem.shape[1]),
                  gate(C_GHG), gate(C_GDA), gate(C_GMEM), full(wp_hg), full(wp_da), full(wp_mem)],
        out_specs=row(D_MODEL), out_shape=jax.ShapeDtypeStruct((rows, D_MODEL), BF16),
        compiler_params=_params(("parallel",)),
    )(o_hg, o_da, o_mem, proj, proj, proj, wp_hg, wp_da, wp_mem)


def _branch_bwd(dm, o_hg, o_da, o_mem, proj, wp_hg, wp_da, wp_mem, *, name):
    rows = o_hg.shape[0]
    tm = min(256, rows)

    def body(dm_ref, a_ref, b_ref, c_ref, ga_ref, gb_ref, gc_ref, wa_ref, wb_ref, wc_ref, dp_ref, *outs):
        dmv = dm_ref[...]
        for j, (o_ref, g_ref, w_ref) in enumerate(((a_ref, ga_ref, wa_ref), (b_ref, gb_ref, wb_ref),
                                                   (c_ref, gc_ref, wc_ref))):
            z = _dot(o_ref[...], w_ref[...])
            gs = _sigmoid(g_ref[...])
            dz = (dmv * gs).astype(BF16)
            dp_ref[:, j * D_MODEL:(j + 1) * D_MODEL] = (dmv * z * gs * (1.0 - gs)).astype(BF16)
            outs[2 * j][...] = dz
            outs[2 * j + 1][...] = _dot_nt(dz, w_ref[...])

    row = lambda w: pl.BlockSpec((tm, w), lambda i: (i, 0))
    gate = lambda off: pl.BlockSpec((tm, D_MODEL), lambda i: (i, off // D_MODEL))
    full = lambda a: pl.BlockSpec(a.shape, lambda i: (0, 0))
    out_specs = [pl.BlockSpec((pl.Element(tm), pl.Element(3 * D_MODEL)), lambda i: (i * tm, C_GHG))]
    out_shape = [jax.ShapeDtypeStruct((rows, IN_COLS), BF16)]
    for o in (o_hg, o_da, o_mem):
        out_specs += [row(D_MODEL), row(o.shape[1])]
        out_shape += [jax.ShapeDtypeStruct((rows, D_MODEL), BF16), jax.ShapeDtypeStruct((rows, o.shape[1]), F32)]
    return pl.pallas_call(
        body, name=name, grid=(rows // tm,),
        in_specs=[row(D_MODEL), row(o_hg.shape[1]), row(o_da.shape[1]), row(o_mem.shape[1]),
                  gate(C_GHG), gate(C_GDA), gate(C_GMEM), full(wp_hg), full(wp_da), full(wp_mem)],
        out_specs=out_specs, out_shape=out_shape,
        compiler_params=_params(("parallel",)),
    )(dm, o_hg, o_da, o_mem, proj, proj, proj, wp_hg, wp_da, wp_mem)


def _ffn_in(h2, w_ab, *, name):
    rows, dff = h2.shape[0], w_ab.shape[1] // 2
    tm, tn = min(2048, rows), 256

    def body(h_ref, wa_ref, wb_ref, a_ref, b_ref, u_ref):
        a = _dot(h_ref[...], wa_ref[...])
        b = _dot(h_ref[...], wb_ref[...])
        a_ref[...] = a.astype(BF16)
        b_ref[...] = b.astype(BF16)
        u_ref[...] = (a * _sigmoid(a) * b).astype(BF16)

    out = pl.BlockSpec((tm, tn), lambda i, j: (i, j))
    return pl.pallas_call(
        body, name=name, grid=(rows // tm, dff // tn),
        in_specs=[pl.BlockSpec((tm, D_MODEL), lambda i, j: (i, 0)),
                  pl.BlockSpec((D_MODEL, tn), lambda i, j: (0, j)),
                  pl.BlockSpec((D_MODEL, tn), lambda i, j: (0, dff // tn + j))],
        out_specs=[out, out, out],
        out_shape=[jax.ShapeDtypeStruct((rows, dff), BF16)] * 3,
        compiler_params=_params(("parallel", "parallel")),
    )(h2, w_ab, w_ab)


def _ffn_act_bwd(dy, w_out, a, b, *, name):
    rows, dff = a.shape
    tm, tn = min(1024, rows), 256

    def body(dy_ref, w_ref, a_ref, b_ref, da_ref, db_ref):
        du = _dot_nt(dy_ref[...], w_ref[...])
        av, bv = a_ref[...].astype(F32), b_ref[...].astype(F32)
        sa = _sigmoid(av)
        da_ref[...] = (du * bv * sa * (1.0 + av * (1.0 - sa))).astype(BF16)
        db_ref[...] = (du * av * sa).astype(BF16)

    tile = pl.BlockSpec((tm, tn), lambda i, j: (i, j))
    return pl.pallas_call(
        body, name=name, grid=(rows // tm, dff // tn),
        in_specs=[pl.BlockSpec((tm, D_MODEL), lambda i, j: (i, 0)),
                  pl.BlockSpec((tn, D_MODEL), lambda i, j: (j, 0)), tile, tile],
        out_specs=[tile, tile],
        out_shape=[jax.ShapeDtypeStruct((rows, dff), BF16), jax.ShapeDtypeStruct((rows, dff), BF16)],
        compiler_params=_params(("parallel", "parallel")),
    )(dy, w_out, a, b)


def _lower_bound(lb_fw, lb_bw, *, name):
    def body(a_ref, b_ref, oa_ref, ob_ref):
        for src, dst in ((a_ref, oa_ref), (b_ref, ob_ref)):
            dst[...] = _sigmoid(src[0:1, :] - src[1:2, :])

    shape = jax.ShapeDtypeStruct((1, lb_fw.shape[1]), F32)
    return pl.pallas_call(body, name=name, out_shape=[shape, shape])(lb_fw, lb_bw)


def _local_step(x, mem, tgt, p, ex):
    rows = x.shape[0]
    lb_fw, lb_bw = _lower_bound(p["lb_logits_fw"], p["lb_logits_bw"], name="lower_bound")

    h = _rms_fwd(x, p["norm_mix_gain"], name="norm_mix")
    w = {"w_in": ex.w_in}
    proj, carried = _matmul([(h, w["w_in"])], "nn", BF16, tm=1024, tn=1024, carry=ex.late_carry(), name="proj_in")
    w.update(ex.late_weights(carried))
    o_fw, st_fw = _gla_fwd(proj, lb_fw, f_off=C_FF, rev=False, name="gla_fwd_fw")
    o_bw, st_bw = _gla_fwd(proj, lb_bw, f_off=C_FB, rev=True, name="gla_fwd_bw")
    o_hg = _hg_out_fwd(o_fw, o_bw, proj, p["hg_norm_gain"], name="hg_out")

    qkv_r, outs, lses = [], [], []
    for g in range(len(DA_GROUPS)):
        qr, kr, vr = _da_prep(proj, p["da_q_gain"], p["da_k_gain"], g, name=f"da_prep{g}")
        og, lg = _band_fwd(qr, kr, vr, g, name=f"band_fwd{g}")
        qkv_r.append((qr, kr, vr))
        outs.append(og)
        lses.append(lg)
    o_da, lse_da = _da_merge(outs, lses, rows, name="da_merge")

    mem_n = _rms_fwd(mem, p["norm_mem_gain"], name="norm_mem")
    kv = _matmul([(mem_n, w["w_mem_kv"])], "nn", F32, tm=256, tn=512, name="mem_kv")
    o_mem = _mem_fwd(proj, kv, p["mem_q_gain"], p["mem_k_gain"], name="mem_attn")

    merged = _branch_fwd(o_hg, o_da, o_mem, proj, w["w_proj_hg"], w["w_proj_da"], w["w_proj_mem"],
                         name="branch_merge")
    x1, h2 = _out_norm(merged, w["w_out"], x, p["norm_ffn_gain"], name="mix_out_norm")
    a, b, u = _ffn_in(h2, w["w_ffn_in"], name="ffn_in")
    dy, dy_b, loss = _out_loss(u, w["w_ffn_out"], x1, tgt, name="ffn_out_loss")

    gw, gs = {}, {}
    gw["w_ffn_out"] = _matmul([(u, dy_b)], "tn", F32, tm=256, tn=1024, name="g_ffn_out")
    da, db = _ffn_act_bwd(dy_b, w["w_ffn_out"], a, b, name="ffn_act_bwd")
    gw["w_ffn_a"] = _matmul([(h2, da)], "tn", F32, tm=512, tn=256, name="g_ffn_a")
    gw["w_ffn_b"] = _matmul([(h2, db)], "tn", F32, tm=512, tn=256, name="g_ffn_b")
    dh2 = _matmul([(da, w["w_ffn_in"], 0), (db, w["w_ffn_in"], 1)], "nt", F32, tm=256, tn=1024, name="d_h2")
    dx1, dx1_b, gs["norm_ffn_gain"] = _rms_bwd_rows(dh2, x1, p["norm_ffn_gain"], dy, name="norm_ffn_bwd")
    gw["w_out"] = _matmul([(merged, dx1_b)], "tn", F32, tm=512, tn=512, name="g_out")
    dmerged = _matmul([(dx1_b, w["w_out"])], "nt", F32, tm=512, tn=512, name="d_merged")
    dproj, dz_hg, do_hg, dz_da, do_da, dz_mem, do_mem = _branch_bwd(
        dmerged, o_hg, o_da, o_mem, proj, w["w_proj_hg"], w["w_proj_da"], w["w_proj_mem"], name="branch_bwd")
    gw["w_proj_hg"] = _matmul([(o_hg, dz_hg)], "tn", F32, tm=512, tn=512, name="g_proj_hg")
    gw["w_proj_da"] = _matmul([(o_da, dz_da)], "tn", F32, tm=512, tn=512, name="g_proj_da")
    gw["w_proj_mem"] = _matmul([(o_mem, dz_mem)], "tn", F32, tm=512, tn=512, name="g_proj_mem")

    dproj, dk_mem, dv_mem, gs["mem_q_gain"], gs["mem_k_gain"] = _mem_bwd(
        do_mem, proj, kv, p["mem_q_gain"], p["mem_k_gain"], dproj, name="mem_attn_bwd")
    dkv = jnp.concatenate([dk_mem, dv_mem], axis=1).astype(BF16)
    gw["w_mem_kv"] = _matmul([(mem_n, dkv)], "tn", F32, tm=512, tn=512, name="g_mem_kv")
    dmem_n = _matmul([(dkv, w["w_mem_kv"])], "nt", F32, tm=256, tn=512, name="d_mem_n")
    _, _, gs["norm_mem_gain"] = _rms_bwd_rows(dmem_n, mem, p["norm_mem_gain"], None, name="norm_mem_bwd")

    prep = _da_bwd_prep(do_da, o_da, lse_da, name="da_bwd_prep")
    d_da, gq_parts, gk_parts = [], [], []
    for g in range(len(DA_GROUPS)):
        qr, kr, vr = qkv_r[g]
        dor, lser, deltar = prep[3 * g:3 * g + 3]
        dqr, dkr, dvr = _band_bwd(qr, kr, vr, dor, lser, deltar, g, name=f"band_bwd{g}")
        dq, dk, dv, gq, gk = _da_prep_bwd(dqr, dkr, dvr, proj, p["da_q_gain"], p["da_k_gain"], g,
                                          name=f"da_prep_bwd{g}")
        d_da.append((dq, dk, dv))
        gq_parts.append(gq)
        gk_parts.append(gk)
    for j, (off, tag) in enumerate(((C_DQ, "q"), (C_DK, "k"), (C_DV, "v"))):
        dproj = _fill_columns(dproj, [t[j] for t in d_da], off, name=f"dproj_fill_da_{tag}")

    dproj, do_gla, gs["hg_norm_gain"] = _hg_out_bwd(do_hg, o_fw, o_bw, proj, p["hg_norm_gain"], dproj,
                                                    name="hg_out_bwd")
    (dq_f, dproj, dv_f, dlb_fw), carried = _gla_bwd(proj, lb_fw, do_gla, st_fw, None, dproj, f_off=C_FF, rev=False,
                                                    carry=ex.early_carry(gw), name="gla_bwd_fw")
    ex.early_done(carried)
    dproj, dfl_bw, dv_hg, dlb_bw = _gla_bwd(proj, lb_bw, do_gla, st_bw, (dq_f, dv_f), dproj, f_off=C_FB, rev=True,
                                            name="gla_bwd_bw")
    dproj = _fill_columns(dproj, [dfl_bw, dv_hg], C_FB, name="dproj_fill")
    g_in = _matmul([(h, dproj)], "tn", F32, tm=512, tn=1024, name="g_in")
    dh, carried = _matmul([(dproj, w["w_in"])], "nt", F32, tm=1024, tn=1024, tk=IN_COLS // 8,
                          carry=ex.w_in_carry(g_in), name="d_h")
    ex.w_in_done(carried)
    grad_x, _, gs["norm_mix_gain"] = _rms_bwd_rows(dh, x, p["norm_mix_gain"], dx1, name="norm_mix_bwd")

    small = _small_pack(gs, gq_parts, gk_parts, dlb_fw, dlb_bw, lb_fw, lb_bw, name="small_pack")
    return loss, grad_x, small


def _small_pack(gs, gq_parts, gk_parts, dlb_fw, dlb_bw, lb_fw, lb_bw, *, name):
    def body(g_mix, g_mem, g_ffn, dfw, dbw, lfw, lbw, g_hg, q0, q1, q2, k0, k1, k2, g_mq, g_mk, o_ref):
        o_ref[0:1, :] = g_mix[...]
        o_ref[1:2, :] = g_mem[...]
        o_ref[2:3, :] = g_ffn[...]
        for base, d_ref, l_ref in ((3, dfw, lfw), (5, dbw, lbw)):
            lbv = l_ref[...]
            t = d_ref[...] * lbv * (1.0 - lbv)
            o_ref[base:base + 1, :] = t
            o_ref[base + 1:base + 2, :] = -t
        o_ref[7:8, :] = jnp.zeros((1, D_MODEL), F32)
        o_ref[7:8, 0:HEAD] = g_hg[...]
        o_ref[7:8, HEAD:2 * HEAD] = q0[...] + q1[...] + q2[...]
        o_ref[7:8, 2 * HEAD:3 * HEAD] = k0[...] + k1[...] + k2[...]
        o_ref[7:8, 3 * HEAD:4 * HEAD] = g_mq[...]
        o_ref[7:8, 4 * HEAD:5 * HEAD] = g_mk[...]

    return pl.pallas_call(body, name=name, out_shape=jax.ShapeDtypeStruct((8, D_MODEL), F32))(
        gs["norm_mix_gain"], gs["norm_mem_gain"], gs["norm_ffn_gain"], dlb_fw, dlb_bw, lb_fw, lb_bw,
        gs["hg_norm_gain"], *gq_parts, *gk_parts, gs["mem_q_gain"], gs["mem_k_gain"])


def _row_tile(rows, cols, n_arrays):
    budget = (16 * 1024 * 1024) // (2 * 4 * cols * n_arrays)
    tr = rows
    while tr > budget and tr % 2 == 0 and (tr // 2) % 16 == 0:
        tr //= 2
    return tr


def _cast_into_full(a, chip, rows, cols, axis, *, name):
    sr, sc = a.shape
    tr = _row_tile(sr, sc, 2)

    def body(chip_ref, a_ref, o_ref):
        del chip_ref
        o_ref[...] = a_ref[...].astype(BF16)

    if axis == 1:
        out_map = lambda i, chip_ref: (i, chip_ref[0])
    else:
        out_map = lambda i, chip_ref: (chip_ref[0] * (sr // tr) + i, 0)
    return pl.pallas_call(
        body, name=name,
        grid_spec=pltpu.PrefetchScalarGridSpec(
            num_scalar_prefetch=1, grid=(sr // tr,),
            in_specs=[pl.BlockSpec((tr, sc), lambda i, chip_ref: (i, 0))],
            out_specs=pl.BlockSpec((tr, sc), out_map)),
        out_shape=jax.ShapeDtypeStruct((rows, cols), BF16),
        compiler_params=_params(("parallel",)))(chip, a)


def _add_halves(items, *, name):
    rows = items[0][3].shape[0]
    widths = [ra.shape[1] for _, _, _, ra in items]
    tr = _row_tile(rows, sum(widths), 4)

    def body(*refs):
        o_ref = refs[-1]
        first = lax.axis_index("c") == 0
        off = 0
        for j, wd in enumerate(widths):
            h0, h1, ra = refs[3 * j:3 * j + 3]
            o_ref[:, off:off + wd] = (jnp.where(first, h0[...], h1[...]) + ra[...]).astype(BF16)
            off += wd

    in_specs, ins = [], []
    for (g, haxis, hsize, ra), wd in zip(items, widths):
        if haxis == 0:
            in_specs += [pl.BlockSpec((tr, wd), lambda i: (i, 0)),
                         pl.BlockSpec((tr, wd), lambda i, o=hsize // tr: (o + i, 0))]
        else:
            in_specs += [pl.BlockSpec((tr, wd), lambda i: (i, 0)), pl.BlockSpec((tr, wd), lambda i: (i, 1))]
        in_specs.append(pl.BlockSpec((tr, wd), lambda i: (i, 0)))
        ins += [g, g, ra]
    return pl.pallas_call(body, name=name, grid=(rows // tr,), in_specs=in_specs,
                          out_specs=pl.BlockSpec((tr, sum(widths)), lambda i: (i, 0)),
                          out_shape=jax.ShapeDtypeStruct((rows, sum(widths)), BF16),
                          compiler_params=_params(("parallel",)))(*ins)


def _add_slots(rb, *, name):
    _, rows, cols = rb.shape
    tr = _row_tile(rows, cols, 5)

    def body(r0, r1, r2, r3, o_ref):
        o_ref[...] = ((r0[...].astype(F32) + r1[...].astype(F32)) + r2[...].astype(F32)) + r3[...].astype(F32)

    slot = lambda s: pl.BlockSpec((None, tr, cols), lambda i: (s, i, 0))
    return pl.pallas_call(body, name=name, grid=(rows // tr,), in_specs=[slot(s) for s in range(4)],
                          out_specs=pl.BlockSpec((tr, cols), lambda i: (i, 0)),
                          out_shape=jax.ShapeDtypeStruct((rows, cols), F32),
                          compiler_params=_params(("parallel",)))(rb, rb, rb, rb)


def _adamw(w, g, m, v, *, name):
    rows, cols = w.shape
    tr = _row_tile(rows, cols, 7) if rows % 16 == 0 else rows
    c1 = 1.0 - ADAM_B1 ** ADAM_STEP
    c2 = 1.0 - ADAM_B2 ** ADAM_STEP

    def body(w_ref, g_ref, m_ref, v_ref, d_ref, mo_ref, vo_ref):
        gv = g_ref[...]
        mn = ADAM_B1 * m_ref[...] + (1.0 - ADAM_B1) * gv
        vn = ADAM_B2 * v_ref[...] + (1.0 - ADAM_B2) * (gv * gv)
        mo_ref[...] = mn
        vo_ref[...] = vn
        d_ref[...] = -ADAM_LR * ((mn / c1) / (jnp.sqrt(vn / c2) + ADAM_EPS) + ADAM_WD * w_ref[...])

    spec = pl.BlockSpec((tr, cols), lambda i: (i, 0))
    shape = jax.ShapeDtypeStruct((rows, cols), F32)
    return pl.pallas_call(body, name=name, grid=(rows // tr,), in_specs=[spec] * 4, out_specs=[spec] * 3,
                          out_shape=[shape] * 3, compiler_params=_params(("parallel",)))(w, g, m, v)


W_SPECS = (
    ("w_in", 1024, IN_COLS, 1, IN_COLS // 4),
    ("w_mem_kv", 1024, 1024, 0, 256),
    ("w_proj_hg", 1024, 1024, 0, 256),
    ("w_proj_da", 512, 1024, 1, 256),
    ("w_proj_mem", 512, 1024, 1, 256),
    ("w_out", 1024, 1024, 0, 256),
    ("w_ffn_in", 1024, 2 * D_FF, 1, 2 * D_FF // 4),
    ("w_ffn_out", D_FF, 1024, 0, D_FF // 4),
)
CHIP_FLIPS = ((1, 0), (0, 1), (1, 1))
ANY = pl.BlockSpec(memory_space=pl.ANY)
DMA_CHUNK_BYTES = 1 << 20
STAGE_BYTES = 2 << 20


def _place():
    x, y, c = lax.axis_index("x"), lax.axis_index("y"), lax.axis_index("c")
    return x, y, c, 2 * x + y


def _flip(v, f):
    return 1 - v if f else v


def _slab(ref, axis, idx, size):
    start = pl.multiple_of(idx * size, size)
    return ref.at[pl.ds(start, size), :] if axis == 0 else ref.at[:, pl.ds(start, size)]


def _chunked(make, src, dst, want="both"):
    rows, cols = src.shape
    row_bytes = cols * jnp.dtype(src.dtype).itemsize
    k = 1
    while rows % (2 * k) == 0 and (rows // (2 * k)) % 16 == 0 and (rows // k) * row_bytes > DMA_CHUNK_BYTES:
        k *= 2
    cr = rows // k
    parts = []
    if want != "wait":
        parts = [make(src.at[pl.ds(j * cr, cr), :], dst.at[pl.ds(j * cr, cr), :]) for j in range(k)]
    return parts, (make(src, dst) if want != "start" else None)


def _half_spec(rows, cols, axis):
    return (0, rows // 2) if axis == 1 else (1, cols // 2)


def _staged(src, remote_dst, local_dst, sibling, load_sems, send_sems, store_sems, recv_sem):
    rows, cols = src.shape
    row_bytes = cols * jnp.dtype(src.dtype).itemsize
    k = 1
    while (rows // k) * row_bytes > STAGE_BYTES and rows % (2 * k) == 0 and (rows // (2 * k)) % 16 == 0:
        k *= 2
    cr = rows // k
    piece = lambda ref, j: ref.at[pl.ds(j * cr, cr), :]

    def run(buf):
        loads = [pltpu.make_async_copy(piece(src, j), buf.at[j % 2], load_sems.at[j % 2]) for j in range(k)]
        outs = [[pltpu.make_async_remote_copy(src_ref=buf.at[j % 2], dst_ref=piece(remote_dst, j),
                                              send_sem=send_sems.at[j % 2], recv_sem=recv_sem,
                                              device_id=sibling, device_id_type=MESH)] for j in range(k)]
        if local_dst is not None:
            for j in range(k):
                outs[j].append(pltpu.make_async_copy(buf.at[j % 2], piece(local_dst, j), store_sems.at[j % 2]))

        def drained(j):
            outs[j][0].wait_send()
            for cp in outs[j][1:]:
                cp.wait()

        loads[0].start()
        for j in range(k):
            loads[j].wait()
            for cp in outs[j]:
                cp.start()
            if j + 1 < k:
                if j >= 1:
                    drained(j - 1)
                loads[j + 1].start()
        for j in range(max(0, k - 2), k):
            drained(j)

    pl.run_scoped(run, pltpu.VMEM((2, cr, cols), src.dtype))


def _landed(ref, recv_sem, send_sem):
    pltpu.make_async_remote_copy(src_ref=ref, dst_ref=ref, send_sem=send_sem, recv_sem=recv_sem,
                                 device_id=(lax.axis_index("x"), lax.axis_index("y"), lax.axis_index("c")),
                                 device_id_type=MESH).wait_recv()


def _half_slab(ref, spec, chip, half):
    _, rows, cols, axis, size = spec
    haxis, hsize = _half_spec(rows, cols, axis)
    return _slab(_slab(ref, axis, chip, size), haxis, half, hsize)


def _gather_sends(outs, specs, ici_send, ici_recv, want="both"):
    x, y, c, p = _place()
    sent = []
    for wi, spec in enumerate(specs):
        mine = _half_slab(outs[wi], spec, p, c)
        for k, (fx, fy) in enumerate(CHIP_FLIPS):
            sent.append(_chunked(lambda s, d, j=3 * wi + k, fx=fx, fy=fy: pltpu.make_async_remote_copy(
                src_ref=s, dst_ref=d, send_sem=ici_send.at[j], recv_sem=ici_recv.at[j],
                device_id=(_flip(x, fx), _flip(y, fy), c), device_id_type=MESH), mine, mine, want))
    return sent


def _gather_weights(fulls, specs, *, name):
    n = len(specs)

    def body(*refs):
        outs = refs[n:2 * n]
        ici_send, ici_recv, load_sems, d2d_send, d2d_recv = refs[2 * n:]
        x, y, c, _ = _place()
        sent = _gather_sends(outs, specs, ici_send, ici_recv)
        for parts, _ in sent:
            for cp in parts:
                cp.start()
        for k, (fx, fy) in enumerate(CHIP_FLIPS):
            q = 2 * _flip(x, fx) + _flip(y, fy)
            for wi in range(n):
                sent[3 * wi + k][1].wait_recv()
                got = _half_slab(outs[wi], specs[wi], q, c)
                _staged(got, got, None, (x, y, 1 - c), load_sems, d2d_send, None, d2d_recv.at[3 * wi + k])
        for _, whole in sent:
            whole.wait_send()
        for k, (fx, fy) in enumerate(CHIP_FLIPS):
            q = 2 * _flip(x, fx) + _flip(y, fy)
            for wi in range(n):
                _landed(_half_slab(outs[wi], specs[wi], q, 1 - c), d2d_recv.at[3 * wi + k], d2d_send.at[0])

    return pl.pallas_call(
        body, name=name, in_specs=[ANY] * n, out_specs=[ANY] * n,
        out_shape=[jax.ShapeDtypeStruct(f.shape, f.dtype) for f in fulls],
        input_output_aliases={i: i for i in range(n)},
        scratch_shapes=[pltpu.SemaphoreType.DMA((3 * n,)), pltpu.SemaphoreType.DMA((3 * n,)),
                        pltpu.SemaphoreType.DMA((2,)), pltpu.SemaphoreType.DMA((2,)),
                        pltpu.SemaphoreType.DMA((3 * n,))],
    )(*fulls)


def _gather_chips_carry(fulls, specs):
    n = len(specs)

    def issue(ins, outs, sems, want):
        del ins
        return [(parts, whole, True) for parts, whole in _gather_sends(outs, specs, sems[0], sems[1], want)]

    return _Carry(list(fulls), [jax.ShapeDtypeStruct(f.shape, f.dtype) for f in fulls], {i: i for i in range(n)},
                  [pltpu.SemaphoreType.DMA((3 * n,)), pltpu.SemaphoreType.DMA((3 * n,))], issue)


def _gather_pass_on(fulls, specs, *, name):
    n = len(specs)

    def body(*refs):
        outs = refs[n:2 * n]
        load_sems, d2d_send, d2d_recv = refs[2 * n:2 * n + 3]
        bufs = refs[2 * n + 3:]
        x, y, c, _ = _place()
        loads, sends = [], []
        for k, (fx, fy) in enumerate(CHIP_FLIPS):
            q = 2 * _flip(x, fx) + _flip(y, fy)
            for wi in range(n):
                j = 3 * wi + k
                got = _half_slab(outs[wi], specs[wi], q, c)
                loads.append(pltpu.make_async_copy(got, bufs[j], load_sems.at[j]))
                sends.append(pltpu.make_async_remote_copy(
                    src_ref=bufs[j], dst_ref=got, send_sem=d2d_send.at[j], recv_sem=d2d_recv.at[j],
                    device_id=(x, y, 1 - c), device_id_type=MESH))
        for cp in loads:
            cp.start()
        for load, send in zip(loads, sends):
            load.wait()
            send.start()
        for cp in sends:
            cp.wait_send()
        for k, (fx, fy) in enumerate(CHIP_FLIPS):
            q = 2 * _flip(x, fx) + _flip(y, fy)
            for wi in range(n):
                _landed(_half_slab(outs[wi], specs[wi], q, 1 - c), d2d_recv.at[3 * wi + k], d2d_send.at[0])

    shapes = []
    for _, rows, cols, axis, size in specs:
        shapes += [(rows // 2, size) if axis == 1 else (size, cols // 2)] * 3
    assert sum(math.prod(s) for s in shapes) * 2 <= VMEM_LIMIT_V7X // 2
    stages = [pltpu.VMEM(s, BF16) for s in shapes]
    return pl.pallas_call(
        body, name=name, in_specs=[ANY] * n, out_specs=[ANY] * n,
        out_shape=[jax.ShapeDtypeStruct(f.shape, f.dtype) for f in fulls],
        input_output_aliases={i: i for i in range(n)},
        scratch_shapes=[pltpu.SemaphoreType.DMA((3 * n,))] * 3 + stages,
        compiler_params=pltpu.CompilerParams(vmem_limit_bytes=VMEM_LIMIT_V7X),
    )(*fulls)


def _sibling_exchange(grads, *, name):
    n = len(grads)

    def body(*refs):
        ins, outs = refs[:n], refs[n:2 * n]
        load_sems, send_sems, recv_sems = refs[2 * n:]
        x, y, c, _ = _place()
        for i, (_, haxis, hsize) in enumerate(grads):
            _staged(_slab(ins[i], haxis, 1 - c, hsize), outs[i], None, (x, y, 1 - c),
                    load_sems, send_sems, None, recv_sems.at[i])
        for i in range(n):
            _landed(outs[i], recv_sems.at[i], send_sems.at[0])

    shapes = [jax.ShapeDtypeStruct((hsize, g.shape[1]) if haxis == 0 else (g.shape[0], hsize), F32)
              for g, haxis, hsize in grads]
    return pl.pallas_call(
        body, name=name, in_specs=[ANY] * n, out_specs=[ANY] * n, out_shape=shapes,
        scratch_shapes=[pltpu.SemaphoreType.DMA((2,)), pltpu.SemaphoreType.DMA((2,)),
                        pltpu.SemaphoreType.DMA((n,))],
    )(*[g for g, _, _ in grads])


def _chip_exchange_carry(parts):
    n = len(parts)

    def issue(ins, outs, sems, want):
        send_sems, recv_sems, local_sems = sems
        x, y, c, p = _place()
        copies = []
        for i, (_, axis, size) in enumerate(parts):
            copies.append(_chunked(lambda s, d, i=i: pltpu.make_async_copy(s, d, local_sems.at[i]),
                                   _slab(ins[i], axis, p, size), outs[i].at[p], want) + (False,))
            for k, (fx, fy) in enumerate(CHIP_FLIPS):
                px, py = _flip(x, fx), _flip(y, fy)
                copies.append(_chunked(lambda s, d, j=3 * i + k, px=px, py=py: pltpu.make_async_remote_copy(
                    src_ref=s, dst_ref=d, send_sem=send_sems.at[j], recv_sem=recv_sems.at[j],
                    device_id=(px, py, c), device_id_type=MESH),
                    _slab(ins[i], axis, 2 * px + py, size), outs[i].at[p], want) + (True,))
        return copies

    shapes = []
    for a, axis, size in parts:
        shapes.append(jax.ShapeDtypeStruct((4, size, a.shape[1]) if axis == 0 else (4, a.shape[0], size), a.dtype))
    return _Carry([a for a, _, _ in parts], shapes, {},
                  [pltpu.SemaphoreType.DMA((3 * n,)), pltpu.SemaphoreType.DMA((3 * n,)),
                   pltpu.SemaphoreType.DMA((n,))], issue)


def _sibling_share(sums):
    n = len(sums)

    def body(*refs):
        ins, outs = refs[:n], refs[n:2 * n]
        load_sems, send_sems, store_sems, recv_sems = refs[2 * n:]
        x, y, c, _ = _place()
        for i, (s, haxis) in enumerate(sums):
            place = _slab(outs[i], haxis, c, s.shape[haxis])
            _staged(ins[i], place, place, (x, y, 1 - c), load_sems, send_sems, store_sems, recv_sems.at[i])
        for i, (s, haxis) in enumerate(sums):
            _landed(_slab(outs[i], haxis, 1 - c, s.shape[haxis]), recv_sems.at[i], send_sems.at[0])

    shapes = []
    for s, haxis in sums:
        r, cc = s.shape
        shapes.append(jax.ShapeDtypeStruct((2 * r, cc) if haxis == 0 else (r, 2 * cc), F32))
    return pl.pallas_call(
        body, name="grad_sibling_share", in_specs=[ANY] * n, out_specs=[ANY] * n, out_shape=shapes,
        scratch_shapes=[pltpu.SemaphoreType.DMA((2,)), pltpu.SemaphoreType.DMA((2,)),
                        pltpu.SemaphoreType.DMA((2,)), pltpu.SemaphoreType.DMA((n,))],
    )(*[s for s, _ in sums])


class _Exchanges:
    def __init__(self, fulls):
        self.w_in = _gather_weights([fulls["w_in"]], W_SPECS[:1], name="gather_w_in")[0]
        self.late_specs = W_SPECS[1:]
        self.late = [fulls[s[0]] for s in self.late_specs]
        self.slots = {}

    def late_carry(self):
        return _gather_chips_carry(self.late, self.late_specs)

    def late_weights(self, carried):
        done = _gather_pass_on(carried, self.late_specs, name="gather_pass_on")
        return dict(zip([s[0] for s in self.late_specs], done))

    def _half_sums(self, gw, specs, tag):
        grads = []
        for name, _, _, axis, _ in specs:
            for g in ((gw["w_ffn_a"], gw["w_ffn_b"]) if name == "w_ffn_in" else (gw[name],)):
                grads.append((g,) + _half_spec(g.shape[0], g.shape[1], axis))
        theirs = _sibling_exchange(grads, name=f"grad_sibling_exchange_{tag}")
        parts, j = [], 0
        for name, _, _, axis, size in specs:
            take = 2 if name == "w_ffn_in" else 1
            items = [grads[i] + (theirs[i],) for i in range(j, j + take)]
            parts.append((_add_halves(items, name=f"half_sum_{name}"), axis, size))
            j += take
        return _chip_exchange_carry(parts)

    def early_carry(self, gw):
        return self._half_sums(gw, self.late_specs, "early")

    def early_done(self, carried):
        self.slots.update(zip([s[0] for s in self.late_specs], carried))

    def w_in_carry(self, g):
        return self._half_sums({"w_in": g}, W_SPECS[:1], "w_in")

    def w_in_done(self, carried):
        self.slots["w_in"] = carried[0]

    def reduced(self):
        sums = []
        for name, rows, cols, axis, _ in W_SPECS:
            sums.append((_add_slots(self.slots[name], name=f"chip_sum_{name}"), _half_spec(rows, cols, axis)[0]))
        return dict(zip([s[0] for s in W_SPECS], _sibling_share(sums)))


def _small_allreduce(sv):
    def body(sv_ref, o_ref, slots_ref, send_sems, recv_sems):
        x, y, c, _ = _place()
        me = 4 * x + 2 * y + c
        slots_ref[me] = sv_ref[...]
        copies = []
        for k in range(1, 8):
            fx, fy, fc = (k >> 2) & 1, (k >> 1) & 1, k & 1
            copies.append(pltpu.make_async_remote_copy(
                src_ref=sv_ref, dst_ref=slots_ref.at[me], send_sem=send_sems.at[k - 1],
                recv_sem=recv_sems.at[k - 1], device_id=(_flip(x, fx), _flip(y, fy), _flip(c, fc)),
                device_id_type=MESH))
        for cp in copies:
            cp.start()
        for cp in copies:
            cp.wait_recv()
        for cp in copies:
            cp.wait_send()
        total = slots_ref[0]
        for s in range(1, 8):
            total = total + slots_ref[s]
        o_ref[...] = total

    vm = pl.BlockSpec(memory_space=pltpu.VMEM)
    return pl.pallas_call(
        body, name="small_allreduce", in_specs=[vm], out_specs=vm,
        out_shape=jax.ShapeDtypeStruct(sv.shape, F32),
        scratch_shapes=[pltpu.VMEM((8,) + sv.shape, F32), pltpu.SemaphoreType.DMA((7,)),
                        pltpu.SemaphoreType.DMA((7,))],
    )(sv)


SMALL_ROWS = (("norm_mix_gain", 0), ("norm_mem_gain", 1), ("norm_ffn_gain", 2))
SMALL_LB = (("lb_logits_fw", 3), ("lb_logits_bw", 5))
SMALL_HEAD = ("hg_norm_gain", "da_q_gain", "da_k_gain", "mem_q_gain", "mem_k_gain")


def _pack_small(d):
    last = jnp.concatenate([d[n] for n in SMALL_HEAD] + [jnp.zeros((1, D_MODEL - HEAD * len(SMALL_HEAD)), F32)], axis=1)
    return jnp.concatenate([d["norm_mix_gain"], d["norm_mem_gain"], d["norm_ffn_gain"],
                            d["lb_logits_fw"], d["lb_logits_bw"], last], axis=0)


def _unpack_small(a):
    out = {n: a[r:r + 1] for n, r in SMALL_ROWS}
    out.update({n: a[r:r + 2] for n, r in SMALL_LB})
    out.update({n: a[7:8, j * HEAD:(j + 1) * HEAD] for j, n in enumerate(SMALL_HEAD)})
    return out


PARAM_ORDER = ("norm_mix_gain", "norm_mem_gain", "w_in", "lb_logits_fw", "lb_logits_bw", "hg_norm_gain",
               "da_q_gain", "da_k_gain", "w_mem_kv", "mem_q_gain", "mem_k_gain", "w_proj_hg", "w_proj_da",
               "w_proj_mem", "w_out", "norm_ffn_gain", "w_ffn_in", "w_ffn_out")


def kernel(x, mem, norm_mix_gain, norm_mem_gain, w_in, lb_logits_fw, lb_logits_bw, hg_norm_gain, da_q_gain, da_k_gain, w_mem_kv, mem_q_gain, mem_k_gain, w_proj_hg, w_proj_da, w_proj_mem, w_out, norm_ffn_gain, w_ffn_in, w_ffn_out, loss_target, m_norm_mix_gain, m_norm_mem_gain, m_w_in, m_lb_logits_fw, m_lb_logits_bw, m_hg_norm_gain, m_da_q_gain, m_da_k_gain, m_w_mem_kv, m_mem_q_gain, m_mem_k_gain, m_w_proj_hg, m_w_proj_da, m_w_proj_mem, m_w_out, m_norm_ffn_gain, m_w_ffn_in, m_w_ffn_out, v_norm_mix_gain, v_norm_mem_gain, v_w_in, v_lb_logits_fw, v_lb_logits_bw, v_hg_norm_gain, v_da_q_gain, v_da_k_gain, v_w_mem_kv, v_mem_q_gain, v_mem_k_gain, v_w_proj_hg, v_w_proj_da, v_w_proj_mem, v_w_out, v_norm_ffn_gain, v_w_ffn_in, v_w_ffn_out):
    args = dict(locals())
    mats = tuple(s[0] for s in W_SPECS)
    flat = lambda a: a.reshape(a.shape[-2:])
    w = {n: flat(args[n]) for n in mats}
    m = {n: flat(args["m_" + n]) for n in mats}
    v = {n: flat(args["v_" + n]) for n in mats}
    small = {n: args[n] for n in PARAM_ORDER if n not in mats}

    chip = (2 * lax.axis_index("x") + lax.axis_index("y")).astype(jnp.int32).reshape(1)
    ex = _Exchanges({n: _cast_into_full(w[n], chip, rows, cols, axis, name=f"cast_{n}")
                     for n, rows, cols, axis, _ in W_SPECS})
    loss, grad_x, small_grads = _local_step(x[0], mem[0], loss_target[0], small, ex)
    grads = ex.reduced()
    small_sum = _small_allreduce(small_grads)

    delta, new_m, new_v = {}, {}, {}
    for n in mats:
        delta[n], new_m[n], new_v[n] = _adamw(w[n], grads[n], m[n], v[n], name=f"adamw_{n}")
    packed = _adamw(_pack_small(small), small_sum,
                    _pack_small({n: args["m_" + n] for n in small}),
                    _pack_small({n: args["v_" + n] for n in small}), name="adamw_small")
    grads.update(_unpack_small(small_sum))
    for dst, src in zip((delta, new_m, new_v), packed):
        dst.update(_unpack_small(src))

    def shaped(d, n):
        return d[n].reshape(args[n].shape)

    loss_sum = lax.psum(loss[0, 0], ("x", "y", "c"))
    return (loss_sum, grad_x[None], *[shaped(grads, n) for n in PARAM_ORDER], *[shaped(delta, n) for n in PARAM_ORDER],
            *[shaped(new_m, n) for n in PARAM_ORDER], *[shaped(new_v, n) for n in PARAM_ORDER])
```

```python
import functools
import math

import numpy as np
import jax
import jax.numpy as jnp
from jax import lax
from jax.experimental import pallas as pl
from jax.experimental.pallas import tpu as pltpu

F32 = jnp.float32
BF16 = jnp.bfloat16
MESH = pl.DeviceIdType.MESH

D_MODEL = 1024
HEAD = 128
HG_HEADS = 8
DA_GROUPS = ((1, 64), (4, 64), (16, 64))
DA_GH = 4
MEM_HEADS = 4
N_MEM = 256
D_FF = 2816
CHUNK = 64
BAND_QBLOCKS = 4
GLA_HEADS_PER_STEP = 4
RMS_EPS = 1e-6
NEG_INF = -1e30
HG_SCALE = HEAD ** -0.5
ATT_SCALE = HEAD ** -0.5
VMEM_LIMIT_V7X = 48 * 1024 * 1024

C_HQ, C_FF, C_FB, C_HI, C_HG = 0, 1024, 2048, 3072, 4096
C_DQ, C_DK, C_DV, C_MQ = 5120, 6656, 8192, 9728
C_GHG, C_GDA, C_GMEM = 10240, 11264, 12288
IN_COLS = 13312

ADAM_LR, ADAM_B1, ADAM_B2, ADAM_EPS, ADAM_WD, ADAM_STEP = 0.001, 0.9, 0.999, 1e-08, 0.01, 10


def _params(sem, vmem=VMEM_LIMIT_V7X):
    return pltpu.CompilerParams(dimension_semantics=sem, vmem_limit_bytes=vmem)


def _dot(a, b):
    return jnp.dot(a.astype(BF16), b.astype(BF16), preferred_element_type=F32)


def _dot_nt(a, b):
    return lax.dot_general(a.astype(BF16), b.astype(BF16), (((1,), (1,)), ((), ())),
                           preferred_element_type=F32)


def _dot_tn(a, b):
    return lax.dot_general(a.astype(BF16), b.astype(BF16), (((0,), (0,)), ((), ())),
                           preferred_element_type=F32)


def _sigmoid(v):
    return jax.nn.sigmoid(v.astype(F32))


def _ones(rows, cols):
    return (lax.broadcasted_iota(jnp.int32, (rows, cols), 0) >= 0).astype(BF16)


def _lane_sum(v):
    ones = _ones(HEAD, HEAD)
    hi = v.astype(BF16)
    mid = (v - hi.astype(F32)).astype(BF16)
    return jnp.dot(hi, ones, preferred_element_type=F32) + jnp.dot(mid, ones, preferred_element_type=F32)


def _row_mean(v):
    if v.shape[-1] == HEAD:
        return _lane_sum(v) * (1.0 / HEAD)
    return jnp.mean(v, axis=-1, keepdims=True)


def _rms(v):
    v = v.astype(F32)
    r = lax.rsqrt(_row_mean(v * v) + RMS_EPS)
    return v * r, r


def _rms_bwd(dy, xhat, r, gain):
    dxh = dy * gain
    dx = r * (dxh - xhat * _row_mean(dxh * xhat))
    return dx, dy * xhat


class _Carry:
    def __init__(self, arrays, out_shapes, aliases, sems, issue):
        self.arrays, self.out_shapes, self.aliases, self.sems, self.issue = arrays, out_shapes, aliases, sems, issue


NO_CARRY = object()


def _start_copies(copies):
    for parts, _, _ in copies:
        for cp in parts:
            cp.start()


def _wait_copies(copies):
    for _, whole, remote in copies:
        if remote:
            whole.wait_recv()
    for _, whole, remote in copies:
        if remote:
            whole.wait_send()
        else:
            whole.wait()


def _call(body, *, name, grid, in_specs, out_specs, out_shape, scratch_shapes, semantics, ins, carry=None,
          aliases=None):
    aliases = dict(aliases or {})
    if carry is None:
        res = pl.pallas_call(body, name=name, grid=grid, in_specs=in_specs, out_specs=out_specs, out_shape=out_shape,
                             scratch_shapes=scratch_shapes, input_output_aliases=aliases,
                             compiler_params=_params(semantics))(*ins)
        return list(res), []
    n_in, n_out, n_scr = len(ins), len(out_shape), len(scratch_shapes)
    c_in, c_out = len(carry.arrays), len(carry.out_shapes)

    def wrapped(*refs):
        pos = [0]

        def take(count):
            pos[0] += count
            return refs[pos[0] - count:pos[0]]

        own_in, carry_in = take(n_in), take(c_in)
        own_out, carry_out = take(n_out), take(c_out)
        own_scr, carry_sems = take(n_scr), take(len(carry.sems))
        ids = [pl.program_id(a) for a in range(len(grid))]
        first, last = ids[0] == 0, ids[0] == grid[0] - 1
        for a in range(1, len(grid)):
            first, last = first & (ids[a] == 0), last & (ids[a] == grid[a] - 1)

        @pl.when(first)
        def _():
            _start_copies(carry.issue(carry_in, carry_out, carry_sems, "start"))

        body(*own_in, *own_out, *own_scr)

        @pl.when(last)
        def _():
            _wait_copies(carry.issue(carry_in, carry_out, carry_sems, "wait"))

    res = pl.pallas_call(
        wrapped, name=name, grid=grid, in_specs=list(in_specs) + [ANY] * c_in,
        out_specs=list(out_specs) + [ANY] * c_out, out_shape=list(out_shape) + list(carry.out_shapes),
        input_output_aliases={**aliases, **{n_in + i: n_out + j for i, j in carry.aliases.items()}},
        scratch_shapes=list(scratch_shapes) + list(carry.sems),
        compiler_params=_params(("arbitrary",) * len(grid)))(*ins, *carry.arrays)
    return list(res[:n_out]), list(res[n_out:])


def _matmul(pairs, mode, out_dtype, *, tm, tn, tk=None, residual=None, also=None, carry=NO_CARRY, name):
    b_offs = [pr[2] if len(pr) > 2 else 0 for pr in pairs]
    pairs = [pr[:2] for pr in pairs]
    a0, b0 = pairs[0]
    if mode == "nn":
        (m, kk), n = a0.shape, b0.shape[1]
    elif mode == "nt":
        (m, kk), n = a0.shape, b0.shape[0]
    else:
        (kk, m), n = a0.shape, b0.shape[1]
    assert mode == "nt" or not any(b_offs)
    tm, tn = min(tm, m), min(tn, n)
    tk = kk if tk is None else tk
    nk = kk // tk
    assert m % tm == 0 and n % tn == 0 and kk % tk == 0, (name, m, n, kk)
    n_p = len(pairs)
    if mode == "tn":
        a_spec = pl.BlockSpec((tk, tm), lambda i, j, k: (k, i))
    else:
        a_spec = pl.BlockSpec((tm, tk), lambda i, j, k: (i, k))
    if mode == "nt":
        b_specs = [pl.BlockSpec((tn, tk), lambda i, j, k, o=o: (j, o * nk + k)) for o in b_offs]
    else:
        b_specs = [pl.BlockSpec((tk, tn), lambda i, j, k: (k, j))] * n_p
    o_spec = pl.BlockSpec((tm, tn), lambda i, j, k: (i, j))
    dot = {"nn": _dot, "nt": _dot_nt, "tn": _dot_tn}[mode]
    has_res = residual is not None

    def body(*refs):
        a_refs, b_refs = refs[:n_p], refs[n_p:2 * n_p]
        pos = 2 * n_p
        res_ref = refs[pos] if has_res else None
        pos += int(has_res)
        o_ref = refs[pos]
        pos += int(also is not None)
        part = dot(a_refs[0][...], b_refs[0][...])
        for a_r, b_r in zip(a_refs[1:], b_refs[1:]):
            part += dot(a_r[...], b_r[...])

        def finish(total):
            if has_res:
                total = total + res_ref[...]
            o_ref[...] = total.astype(out_dtype)
            if also is not None:
                refs[pos][...] = total.astype(also)

        if nk == 1:
            finish(part)
        else:
            acc_ref = refs[pos + 1]
            k = pl.program_id(2)

            @pl.when(k == 0)
            def _():
                acc_ref[...] = part

            @pl.when(k > 0)
            def _():
                acc_ref[...] += part

            @pl.when(k == nk - 1)
            def _():
                finish(acc_ref[...])

    ins = [a for a, _ in pairs] + [b for _, b in pairs]
    in_specs = [a_spec] * n_p + b_specs
    if has_res:
        ins.append(residual)
        in_specs.append(o_spec)
    dtypes = [out_dtype] + ([also] if also is not None else [])
    outs, carried = _call(
        body, name=name, grid=(m // tm, n // tn, nk), in_specs=in_specs, out_specs=[o_spec] * len(dtypes),
        out_shape=[jax.ShapeDtypeStruct((m, n), dt) for dt in dtypes],
        scratch_shapes=[pltpu.VMEM((tm, tn), F32)] if nk > 1 else [],
        semantics=("parallel", "parallel", "arbitrary"), ins=ins, carry=None if carry is NO_CARRY else carry)
    out = outs[0] if also is None else tuple(outs)
    return out if carry is NO_CARRY else (out, carried)


def _rms_fwd(x, gain, *, name):
    rows, dm = x.shape
    tm = min(512, rows)

    def body(x_ref, g_ref, h_ref):
        xhat, _ = _rms(x_ref[...])
        h_ref[...] = (xhat * g_ref[...]).astype(BF16)

    return pl.pallas_call(
        body, name=name, grid=(rows // tm,),
        in_specs=[pl.BlockSpec((tm, dm), lambda i: (i, 0)), pl.BlockSpec((1, dm), lambda i: (0, 0))],
        out_specs=pl.BlockSpec((tm, dm), lambda i: (i, 0)),
        out_shape=jax.ShapeDtypeStruct((rows, dm), BF16),
        compiler_params=_params(("parallel",)),
    )(x, gain)


def _rms_bwd_rows(dh, x, gain, dres, *, name):
    rows, dm = x.shape
    tm = min(512, rows)
    has_res = dres is not None

    def body(*refs):
        dh_ref, x_ref, g_ref = refs[:3]
        res_ref = refs[3] if has_res else None
        dx_ref, dxb_ref, dg_ref = refs[3 + int(has_res):]
        xhat, r = _rms(x_ref[...])
        dx, dgr = _rms_bwd(dh_ref[...], xhat, r, g_ref[...])
        if has_res:
            dx = dx + res_ref[...]
        dx_ref[...] = dx
        dxb_ref[...] = dx.astype(BF16)

        @pl.when(pl.program_id(0) == 0)
        def _():
            dg_ref[...] = jnp.zeros_like(dg_ref)

        dg_ref[...] += jnp.sum(dgr, axis=0, keepdims=True)

    row = pl.BlockSpec((tm, dm), lambda i: (i, 0))
    vec = pl.BlockSpec((1, dm), lambda i: (0, 0))
    return pl.pallas_call(
        body, name=name, grid=(rows // tm,),
        in_specs=[row, row, vec] + ([row] if has_res else []),
        out_specs=[row, row, vec],
        out_shape=[jax.ShapeDtypeStruct((rows, dm), F32), jax.ShapeDtypeStruct((rows, dm), BF16),
                   jax.ShapeDtypeStruct((1, dm), F32)],
        compiler_params=_params(("arbitrary",)),
    )(*([dh, x, gain] + ([dres] if has_res else [])))


def _out_norm(a, w, x, gain, *, name):
    rows, dm = x.shape
    tm = min(512, rows)

    def body(a_ref, w_ref, x_ref, g_ref, x1_ref, h_ref):
        x1 = x_ref[...] + _dot(a_ref[...], w_ref[...])
        x1_ref[...] = x1
        xhat, _ = _rms(x1)
        h_ref[...] = (xhat * g_ref[...]).astype(BF16)

    row = lambda wd: pl.BlockSpec((tm, wd), lambda i: (i, 0))
    return pl.pallas_call(
        body, name=name, grid=(rows // tm,),
        in_specs=[row(a.shape[1]), pl.BlockSpec(w.shape, lambda i: (0, 0)), row(dm), pl.BlockSpec((1, dm), lambda i: (0, 0))],
        out_specs=[row(dm), row(dm)],
        out_shape=[jax.ShapeDtypeStruct((rows, dm), F32), jax.ShapeDtypeStruct((rows, dm), BF16)],
        compiler_params=_params(("parallel",)),
    )(a, w, x, gain)


def _out_loss(u, w, x1, tgt, *, name):
    rows, dm = x1.shape
    tm = min(256, rows)
    steps = rows // tm

    def body(u_ref, w_ref, x_ref, t_ref, dy_ref, dyb_ref, loss_ref, acc_ref):
        i = pl.program_id(0)
        diff = x_ref[...] + _dot(u_ref[...], w_ref[...]) - t_ref[...]
        dy = diff * (1.0 / dm)
        dy_ref[...] = dy
        dyb_ref[...] = dy.astype(BF16)

        @pl.when(i == 0)
        def _():
            acc_ref[...] = jnp.zeros_like(acc_ref)

        acc_ref[...] += jnp.sum(diff * diff, axis=0, keepdims=True)

        @pl.when(i == steps - 1)
        def _():
            loss_ref[...] = jnp.full((1, HEAD), 0.5 / dm, F32) * jnp.sum(acc_ref[...])

    row = lambda wd: pl.BlockSpec((tm, wd), lambda i: (i, 0))
    return pl.pallas_call(
        body, name=name, grid=(steps,),
        in_specs=[row(u.shape[1]), pl.BlockSpec(w.shape, lambda i: (0, 0)), row(dm), row(dm)],
        out_specs=[row(dm), row(dm), pl.BlockSpec((1, HEAD), lambda i: (0, 0))],
        out_shape=[jax.ShapeDtypeStruct((rows, dm), F32), jax.ShapeDtypeStruct((rows, dm), BF16),
                   jax.ShapeDtypeStruct((1, HEAD), F32)],
        scratch_shapes=[pltpu.VMEM((1, dm), F32)],
        compiler_params=_params(("arbitrary",)),
    )(u, w, x1, tgt)


def _gla_block_terms(q_raw, f_logit, lb, rev):
    sig = _sigmoid(f_logit)
    forget = lb + (1.0 - lb) * sig
    k = 1.0 - forget
    b = _chunk_cumsum(jnp.log(forget), rev)
    qs = _sigmoid(q_raw)
    eb = jnp.exp(b)
    emb = jnp.exp(-b)
    qt = (q_raw * qs * HG_SCALE) * eb
    kt = k * emb
    return sig, forget, k, b, qs, eb, emb, qt, kt


def _chunk_cumsum(v, rev):
    n = v.shape[0]
    pos = lax.broadcasted_iota(jnp.int32, v.shape, 0) & (CHUNK - 1)
    step = 1
    while step < CHUNK:
        if rev:
            shifted, keep = pltpu.roll(v, n - step, 0), pos < CHUNK - step
        else:
            shifted, keep = pltpu.roll(v, step, 0), pos >= step
        v = v + jnp.where(keep, shifted, 0.0)
        step *= 2
    return v


def _tri_mask(n, rev):
    row, col = np.arange(n)[:, None], np.arange(n)[None, :]
    same = (row // CHUNK) == (col // CHUNK)
    return jnp.asarray((same & ((row <= col) if rev else (row >= col))).astype(np.float32))


def _chunk_order(ncb, rev):
    order = range(ncb - 1, -1, -1) if rev else range(ncb)
    return [(c, c * CHUNK if rev else c * CHUNK + CHUNK - 1) for c in order]


def _gla_fwd(proj, lb, *, f_off, rev, name):
    rows = proj.shape[0]
    tb = min(256, rows)
    nb, ncb = rows // tb, tb // CHUNK

    def tmap(n):
        return nb - 1 - n if rev else n

    def body(tri_ref, q_ref, f_ref, v_ref, lb_ref, o_ref, st_ref, s_ref):
        @pl.when(pl.program_id(1) == 0)
        def _():
            s_ref[...] = jnp.zeros_like(s_ref)

        tri = tri_ref[...] > 0.5
        for hh in range(GLA_HEADS_PER_STEP):
            cs = slice(hh * HEAD, (hh + 1) * HEAD)
            v = v_ref[:, cs]
            _, _, k, b, _, _, _, qt, kt = _gla_block_terms(q_ref[:, cs].astype(F32), f_ref[:, cs].astype(F32),
                                                           lb_ref[:, cs], rev)
            o_intra = _dot(jnp.where(tri, _dot_nt(qt, kt), 0.0), v)
            chunks = []
            for c, last in _chunk_order(ncb, rev):
                sl = slice(c * CHUNK, (c + 1) * CHUNK)
                bl = b[last:last + 1, :]
                kh = k[sl] * jnp.exp(bl - b[sl])
                chunks.append((c, sl, jnp.exp(bl), _dot_tn(v[sl], kh)))
            s_t = s_ref[hh]
            for c, sl, ebl, kv in chunks:
                st_ref[hh, c] = s_t
                o_ref[sl, cs] = o_intra[sl] + _dot_nt(qt[sl], s_t)
                s_t = ebl * s_t + kv
            s_ref[hh] = s_t

    hps = GLA_HEADS_PER_STEP
    col = lambda off: pl.BlockSpec((tb, hps * HEAD), lambda h, n: (tmap(n), off // (hps * HEAD) + h))
    return pl.pallas_call(
        body, name=name, grid=(HG_HEADS // hps, nb),
        in_specs=[pl.BlockSpec((tb, tb), lambda h, n: (0, 0)), col(C_HQ), col(f_off), col(C_HI),
                  pl.BlockSpec((1, hps * HEAD), lambda h, n: (0, h))],
        out_specs=[pl.BlockSpec((tb, hps * HEAD), lambda h, n: (tmap(n), h)),
                   pl.BlockSpec((hps, ncb, HEAD, HEAD), lambda h, n: (h, tmap(n), 0, 0))],
        out_shape=[jax.ShapeDtypeStruct((rows, HG_HEADS * HEAD), F32),
                   jax.ShapeDtypeStruct((HG_HEADS, rows // CHUNK, HEAD, HEAD), F32)],
        scratch_shapes=[pltpu.VMEM((hps, HEAD, HEAD), F32)],
        compiler_params=_params(("parallel", "arbitrary")),
    )(_tri_mask(tb, rev), proj, proj, proj, lb)


def _gla_bwd(proj, lb, do, states, prev, dproj, *, f_off, rev, carry=NO_CARRY, name):
    rows = proj.shape[0]
    tb = min(256, rows)
    nb, ncb = rows // tb, tb // CHUNK
    has_prev = prev is not None
    qv_dtype = BF16 if has_prev else F32

    def tmap(n):
        return n if rev else nb - 1 - n

    def body(*refs):
        tri_ref, q_ref, f_ref, v_ref, lb_ref, do_ref, st_ref = refs[:7]
        pq_ref, pv_ref = refs[8:10] if has_prev else (None, None)
        (dq_ref, df_ref, dv_ref, dlb_ref,
         ds_ref, dqt_scr, dk_scr, db_scr, dbl_scr, dv_scr) = refs[8 + 2 * int(has_prev):]

        @pl.when(pl.program_id(1) == 0)
        def _():
            ds_ref[...] = jnp.zeros_like(ds_ref)
            dlb_ref[...] = jnp.zeros_like(dlb_ref)

        tri = tri_ref[...] > 0.5
        for hh in range(GLA_HEADS_PER_STEP):
            cs = slice(hh * HEAD, (hh + 1) * HEAD)
            lbv = lb_ref[:, cs]
            q_raw = q_ref[:, cs].astype(F32)
            v, dout = v_ref[:, cs], do_ref[:, cs].astype(BF16)
            sig, forget, k, b, qs, eb, emb, qt, kt = _gla_block_terms(q_raw, f_ref[:, cs].astype(F32), lbv, rev)
            a = jnp.where(tri, _dot_nt(qt, kt), 0.0)
            da = jnp.where(tri, _dot_nt(dout, v), 0.0)
            dv_intra = _dot_tn(a, dout)
            dqt_intra = _dot(da, kt)
            dkt = _dot_tn(da, qt)
            chunks = []
            for c, last in reversed(_chunk_order(ncb, rev)):
                sl = slice(c * CHUNK, (c + 1) * CHUNK)
                bl = b[last:last + 1, :]
                e = jnp.exp(bl - b[sl])
                s_t = st_ref[hh, c]
                dqt_scr[sl, cs] = dqt_intra[sl] + _dot(dout[sl], s_t)
                chunks.append((sl, jnp.exp(bl), e, k[sl] * e, s_t, _dot_tn(dout[sl], qt[sl])))
            ds_t = ds_ref[hh]
            for sl, ebl, e, kh, s_t, grow in chunks:
                dkh = _dot(v[sl], ds_t)
                dv_scr[sl, cs] = dv_intra[sl] + _dot_nt(kh, ds_t)
                dk_scr[sl, cs] = dkt[sl] * emb[sl] + dkh * e
                khd = kh * dkh
                dbl = jnp.sum(khd, axis=0, keepdims=True) + ebl * jnp.sum(ds_t * s_t, axis=0, keepdims=True)
                db_scr[sl, cs] = khd
                dbl_scr[sl, cs] = jnp.broadcast_to(dbl, (CHUNK, HEAD))
                ds_t = grow + ds_t * ebl
            ds_ref[hh] = ds_t
            dqt = dqt_scr[:, cs]
            dlogf = _chunk_cumsum(qt * dqt - kt * dkt - db_scr[:, cs], not rev) + dbl_scr[:, cs]
            dforget = dlogf / forget - dk_scr[:, cs]
            df_ref[:, cs] = (dforget * (1.0 - lbv) * sig * (1.0 - sig)).astype(BF16)
            dlb_ref[:, cs] += jnp.sum(dforget * (1.0 - sig), axis=0, keepdims=True)
            dqr = dqt * eb * (HG_SCALE * qs * (1.0 + q_raw * (1.0 - qs)))
            dv = dv_scr[:, cs]
            if has_prev:
                dqr = dqr + pq_ref[:, cs]
                dv = dv + pv_ref[:, cs]
            dq_ref[:, cs] = dqr.astype(qv_dtype)
            dv_ref[:, cs] = dv.astype(qv_dtype)

    hps = GLA_HEADS_PER_STEP
    col = lambda off: pl.BlockSpec((tb, hps * HEAD), lambda h, n: (tmap(n), off // (hps * HEAD) + h))
    blk = pl.BlockSpec((tb, hps * HEAD), lambda h, n: (tmap(n), h))
    vec = pl.BlockSpec((1, hps * HEAD), lambda h, n: (0, h))
    wide = HG_HEADS * HEAD
    into = jax.ShapeDtypeStruct(dproj.shape, dproj.dtype)
    outs, carried = _call(
        body, name=name, grid=(HG_HEADS // hps, nb),
        in_specs=[pl.BlockSpec((tb, tb), lambda h, n: (0, 0)), col(C_HQ), col(f_off), col(C_HI), vec, blk,
                  pl.BlockSpec((hps, ncb, HEAD, HEAD), lambda h, n: (h, tmap(n), 0, 0)), ANY]
                 + ([blk, blk] if has_prev else []),
        out_specs=[col(C_HQ) if has_prev else blk, blk if has_prev else col(f_off), blk, vec],
        out_shape=[into if has_prev else jax.ShapeDtypeStruct((rows, wide), qv_dtype),
                   jax.ShapeDtypeStruct((rows, wide), BF16) if has_prev else into,
                   jax.ShapeDtypeStruct((rows, wide), qv_dtype), jax.ShapeDtypeStruct((1, wide), F32)],
        aliases={7: 0 if has_prev else 1},
        scratch_shapes=[pltpu.VMEM((hps, HEAD, HEAD), F32)] + [pltpu.VMEM((tb, hps * HEAD), F32)] * 5,
        semantics=("parallel", "arbitrary"),
        ins=[_tri_mask(tb, rev), proj, proj, proj, lb, do, states, dproj] + (list(prev) if has_prev else []),
        carry=None if carry is NO_CARRY else carry)
    return outs if carry is NO_CARRY else (outs, carried)


def _fill_columns(dproj, parts, col, *, name):
    rows, wd = parts[0].shape
    tm = min(512, rows)
    n = len(parts)

    def body(*refs):
        o_ref = refs[n + 1]
        for j in range(n):
            o_ref[:, j * wd:(j + 1) * wd] = refs[j][...]

    return pl.pallas_call(
        body, name=name, grid=(rows // tm,),
        in_specs=[pl.BlockSpec((tm, wd), lambda i: (i, 0))] * n + [ANY],
        out_specs=pl.BlockSpec((pl.Element(tm), pl.Element(n * wd)), lambda i: (i * tm, col)),
        out_shape=jax.ShapeDtypeStruct(dproj.shape, dproj.dtype), input_output_aliases={n: 0},
        compiler_params=_params(("parallel",)),
    )(*parts, dproj)


def _hg_out_fwd(o_fw, o_bw, proj, gain, *, name):
    rows = o_fw.shape[0]
    tm = min(512, rows)
    wide = HG_HEADS * HEAD

    def body(a_ref, b_ref, g_ref, gain_ref, o_ref):
        for h in range(HG_HEADS):
            sl = slice(h * HEAD, (h + 1) * HEAD)
            xhat, _ = _rms(a_ref[:, sl] + b_ref[:, sl])
            gate = g_ref[:, sl].astype(F32)
            o_ref[:, sl] = (xhat * gain_ref[...] * (gate * _sigmoid(gate))).astype(BF16)

    row = pl.BlockSpec((tm, wide), lambda i: (i, 0))
    return pl.pallas_call(
        body, name=name, grid=(rows // tm,),
        in_specs=[row, row, pl.BlockSpec((tm, wide), lambda i: (i, C_HG // wide)),
                  pl.BlockSpec((1, HEAD), lambda i: (0, 0))],
        out_specs=row, out_shape=jax.ShapeDtypeStruct((rows, wide), BF16),
        compiler_params=_params(("parallel",)),
    )(o_fw, o_bw, proj, gain)


def _hg_out_bwd(dout, o_fw, o_bw, proj, gain, dproj, *, name):
    rows = o_fw.shape[0]
    tm = min(512, rows)
    wide = HG_HEADS * HEAD

    def body(d_ref, a_ref, b_ref, g_ref, gain_ref, dp_in, dgate_ref, do_ref, dgain_ref):
        del dp_in

        @pl.when(pl.program_id(0) == 0)
        def _():
            dgain_ref[...] = jnp.zeros_like(dgain_ref)

        dgain = jnp.zeros((1, HEAD), F32)
        for h in range(HG_HEADS):
            sl = slice(h * HEAD, (h + 1) * HEAD)
            xhat, r = _rms(a_ref[:, sl] + b_ref[:, sl])
            gate, dy = g_ref[:, sl].astype(F32), d_ref[:, sl]
            gs = _sigmoid(gate)
            dgate_ref[:, sl] = (dy * xhat * gain_ref[...] * (gs * (1.0 + gate * (1.0 - gs)))).astype(BF16)
            dx, dgr = _rms_bwd(dy * (gate * gs), xhat, r, gain_ref[...])
            do_ref[:, sl] = dx
            dgain = dgain + jnp.sum(dgr, axis=0, keepdims=True)
        dgain_ref[...] += dgain

    row = pl.BlockSpec((tm, wide), lambda i: (i, 0))
    gate = pl.BlockSpec((tm, wide), lambda i: (i, C_HG // wide))
    vec = pl.BlockSpec((1, HEAD), lambda i: (0, 0))
    return pl.pallas_call(
        body, name=name, grid=(rows // tm,),
        in_specs=[row, row, row, gate, vec, ANY],
        out_specs=[gate, row, vec],
        out_shape=[jax.ShapeDtypeStruct(dproj.shape, dproj.dtype), jax.ShapeDtypeStruct((rows, wide), F32),
                   jax.ShapeDtypeStruct((1, HEAD), F32)],
        input_output_aliases={5: 0},
        compiler_params=_params(("arbitrary",)),
    )(dout, o_fw, o_bw, proj, gain, dproj)


def _strided_rows(ref, r, count, d):
    return ref[...] if d == 1 else ref[pl.ds(r, count, stride=d), :]


def _store_strided(ref, r, count, d, val):
    if d == 1:
        ref[...] = val
    else:
        ref[pl.ds(r, count, stride=d), :] = val


def _da_prep(proj, q_gain, k_gain, g, *, name):
    d = DA_GROUPS[g][0]
    rows = proj.shape[0]
    rb = min(512, rows)
    tn = rb // d

    def body(q_ref, k_ref, v_ref, qg_ref, kg_ref, qo_ref, ko_ref, vo_ref, qf_ref, kf_ref, vf_ref):
        for h in range(DA_GH):
            cs = slice(h * HEAD, (h + 1) * HEAD)
            for src, dst in ((q_ref, qf_ref), (k_ref, kf_ref), (v_ref, vf_ref)):
                dst[...] = src[:, cs].astype(F32)
            for r in range(d):
                qhat, _ = _rms(_strided_rows(qf_ref, r, tn, d))
                khat, _ = _rms(_strided_rows(kf_ref, r, tn, d))
                qo_ref[h, r] = (qhat * qg_ref[...]).astype(BF16)
                ko_ref[h, r] = (khat * kg_ref[...]).astype(BF16)
                vo_ref[h, r] = _strided_rows(vf_ref, r, tn, d).astype(BF16)

    wide = DA_GH * HEAD
    col = lambda off: pl.BlockSpec((rb, wide), lambda i: (i, off // wide + g))
    vec = pl.BlockSpec((1, HEAD), lambda i: (0, 0))
    out = pl.BlockSpec((DA_GH, d, tn, HEAD), lambda i: (0, 0, i, 0))
    shape = jax.ShapeDtypeStruct((DA_GH, d, rows // d, HEAD), BF16)
    return pl.pallas_call(
        body, name=name, grid=(rows // rb,),
        in_specs=[col(C_DQ), col(C_DK), col(C_DV), vec, vec],
        out_specs=[out, out, out], out_shape=[shape, shape, shape],
        scratch_shapes=[pltpu.VMEM((rb, HEAD), F32)] * 3,
        compiler_params=_params(("parallel",)),
    )(proj, proj, proj, q_gain, k_gain)


def _slopes(g):
    idx = np.arange(g * DA_GH + 1, (g + 1) * DA_GH + 1)
    s = (2.0 ** (-8.0 * idx / (DA_GH * len(DA_GROUPS)))).astype(np.float32)
    return jnp.asarray(np.broadcast_to(s[:, None, None], (DA_GH, 8, HEAD)).copy())


def _band_window(ld, t, radius):
    win = min(2 * t, ld)
    assert t // 2 >= radius or win == ld
    return win


def _band_scores(q, k, q0, start, slope, d, radius):
    t, win = q.shape[0], k.shape[0]
    row = lax.broadcasted_iota(jnp.int32, (t, win), 0)
    col = lax.broadcasted_iota(jnp.int32, (t, win), 1)
    rel = jnp.abs((start - q0) + col - row)
    return _dot_nt(q, k) * ATT_SCALE + jnp.where(rel <= radius, -slope * (d * rel).astype(F32), NEG_INF)


def _band_fwd(qr, kr, vr, g, *, name):
    d, radius = DA_GROUPS[g]
    _, _, ld, _ = qr.shape
    t = min(HEAD, ld)
    qb = min(BAND_QBLOCKS, ld // t)
    win = _band_window(ld, t, radius)

    def body(q_ref, k_ref, v_ref, sl_ref, o_ref, lse_ref):
        i = pl.program_id(2)
        slope = sl_ref[0:1, 0:1]
        for j in range(qb):
            sl = slice(j * t, (j + 1) * t)
            q0 = (i * qb + j) * t
            start = pl.multiple_of(jnp.clip(q0 - t // 2, 0, ld - win), t // 2)
            s = _band_scores(q_ref[sl, :], k_ref[pl.ds(start, win), :], q0, start, slope, d, radius)
            m = jnp.max(s, axis=-1, keepdims=True)
            p = jnp.exp(s - m).astype(BF16)
            l = jnp.dot(p, _ones(win, HEAD), preferred_element_type=F32)
            o_ref[sl, :] = _dot(p, v_ref[pl.ds(start, win), :]) / l
            lse_ref[sl, :] = m + jnp.log(l)

    own = pl.BlockSpec((None, None, qb * t, HEAD), lambda h, r, i: (h, r, i, 0))
    seq = pl.BlockSpec((None, None, ld, HEAD), lambda h, r, i: (h, r, 0, 0))
    shape = jax.ShapeDtypeStruct(qr.shape, F32)
    return pl.pallas_call(
        body, name=name, grid=(DA_GH, d, ld // (qb * t)),
        in_specs=[own, seq, seq, pl.BlockSpec((None, 8, HEAD), lambda h, r, i: (h, 0, 0))],
        out_specs=[own, own], out_shape=[shape, shape],
        compiler_params=_params(("parallel", "parallel", "parallel")),
    )(qr, kr, vr, _slopes(g))


def _band_bwd(qr, kr, vr, dor, lser, deltar, g, *, name):
    d, radius = DA_GROUPS[g]
    _, _, ld, _ = qr.shape
    t = min(HEAD, ld)
    qb = min(BAND_QBLOCKS, ld // t)
    win = _band_window(ld, t, radius)

    def body(q_ref, k_ref, v_ref, do_ref, lse_ref, dl_ref, sl_ref, dq_ref, dk_ref, dv_ref):
        i = pl.program_id(2)

        @pl.when(i == 0)
        def _():
            dk_ref[...] = jnp.zeros_like(dk_ref)
            dv_ref[...] = jnp.zeros_like(dv_ref)

        slope = sl_ref[0:1, 0:1]
        for j in range(qb):
            sl = slice(j * t, (j + 1) * t)
            q0 = (i * qb + j) * t
            start = pl.multiple_of(jnp.clip(q0 - t // 2, 0, ld - win), t // 2)
            rows = pl.ds(start, win)
            q, dout, k, v = q_ref[sl, :], do_ref[sl, :], k_ref[rows, :], v_ref[rows, :]
            p = jnp.exp(_band_scores(q, k, q0, start, slope, d, radius) - lse_ref[sl, 0:1])
            ds = p * (_dot_nt(dout, v) - dl_ref[sl, 0:1])
            dq_ref[sl, :] = _dot(ds, k) * ATT_SCALE
            dk_ref[rows, :] += _dot_tn(ds, q) * ATT_SCALE
            dv_ref[rows, :] += _dot_tn(p, dout)

    own = pl.BlockSpec((None, None, qb * t, HEAD), lambda h, r, i: (h, r, i, 0))
    seq = pl.BlockSpec((None, None, ld, HEAD), lambda h, r, i: (h, r, 0, 0))
    shape = jax.ShapeDtypeStruct(qr.shape, F32)
    return pl.pallas_call(
        body, name=name, grid=(DA_GH, d, ld // (qb * t)),
        in_specs=[own, seq, seq, own, own, own, pl.BlockSpec((None, 8, HEAD), lambda h, r, i: (h, 0, 0))],
        out_specs=[own, seq, seq], out_shape=[shape, shape, shape],
        compiler_params=_params(("parallel", "parallel", "arbitrary")),
    )(qr, kr, vr, dor, lser, deltar, _slopes(g))


def _da_merge(outs, lses, rows, *, name):
    rb = min(512, rows)
    wide = DA_GH * HEAD

    def body(*refs):
        o_refs, l_refs = refs[0:3], refs[3:6]
        o_ref, lse_ref = refs[6:8]
        on_refs, ln_refs = refs[8:11], refs[11:14]
        for h in range(DA_GH):
            cs = slice(h * HEAD, (h + 1) * HEAD)
            for g, (d, _) in enumerate(DA_GROUPS):
                tn = rb // d
                for r in range(d):
                    _store_strided(on_refs[g], r, tn, d, o_refs[g][h, r])
                    _store_strided(ln_refs[g], r, tn, d, l_refs[g][h, r])
            l0, l1, l2 = ln_refs[0][...], ln_refs[1][...], ln_refs[2][...]
            m = jnp.maximum(jnp.maximum(l0, l1), l2)
            e0, e1, e2 = jnp.exp(l0 - m), jnp.exp(l1 - m), jnp.exp(l2 - m)
            tot = e0 + e1 + e2
            o_ref[:, cs] = (e0 * on_refs[0][...] + e1 * on_refs[1][...] + e2 * on_refs[2][...]) / tot
            lse_ref[:, cs] = m + jnp.log(tot)

    res = lambda d: pl.BlockSpec((DA_GH, d, rb // d, HEAD), lambda i: (0, 0, i, 0))
    nat = pl.BlockSpec((rb, wide), lambda i: (i, 0))
    shape = jax.ShapeDtypeStruct((rows, wide), F32)
    return pl.pallas_call(
        body, name=name, grid=(rows // rb,),
        in_specs=[res(d) for d, _ in DA_GROUPS] * 2,
        out_specs=[nat, nat], out_shape=[shape, shape],
        scratch_shapes=[pltpu.VMEM((rb, HEAD), F32)] * 6,
        compiler_params=_params(("parallel",)),
    )(*outs, *lses)


def _da_bwd_prep(dout, o, lse, *, name):
    rows = o.shape[0]
    rb = min(512, rows)
    wide = DA_GH * HEAD

    def body(d_ref, o_ref, l_ref, *outs):
        d_scr, l_scr, delta_scr = outs[9:12]
        for h in range(DA_GH):
            cs = slice(h * HEAD, (h + 1) * HEAD)
            dv = d_ref[:, cs]
            d_scr[...] = dv
            l_scr[...] = l_ref[:, cs]
            delta_scr[...] = _lane_sum(dv * o_ref[:, cs])
            for g, (d, _) in enumerate(DA_GROUPS):
                tn = rb // d
                for r in range(d):
                    outs[3 * g][h, r] = _strided_rows(d_scr, r, tn, d).astype(BF16)
                    outs[3 * g + 1][h, r] = _strided_rows(l_scr, r, tn, d)
                    outs[3 * g + 2][h, r] = _strided_rows(delta_scr, r, tn, d)

    nat = pl.BlockSpec((rb, wide), lambda i: (i, 0))
    out_specs, out_shape = [], []
    for d, _ in DA_GROUPS:
        for dt in (BF16, F32, F32):
            out_specs.append(pl.BlockSpec((DA_GH, d, rb // d, HEAD), lambda i: (0, 0, i, 0)))
            out_shape.append(jax.ShapeDtypeStruct((DA_GH, d, rows // d, HEAD), dt))
    return pl.pallas_call(
        body, name=name, grid=(rows // rb,),
        in_specs=[nat, nat, nat], out_specs=out_specs, out_shape=out_shape,
        scratch_shapes=[pltpu.VMEM((rb, HEAD), F32)] * 3,
        compiler_params=_params(("parallel",)),
    )(dout, o, lse)


def _da_prep_bwd(dqr, dkr, dvr, proj, q_gain, k_gain, g, *, name):
    d = DA_GROUPS[g][0]
    rows = proj.shape[0]
    rb = min(512, rows)
    tn = rb // d
    wide = DA_GH * HEAD

    def body(dq_ref, dk_ref, dv_ref, q_ref, k_ref, qg_ref, kg_ref, oq_ref, ok_ref, ov_ref, gq_ref, gk_ref, *nat_refs):
        @pl.when(pl.program_id(0) == 0)
        def _():
            gq_ref[...] = jnp.zeros_like(gq_ref)
            gk_ref[...] = jnp.zeros_like(gk_ref)

        for h in range(DA_GH):
            cs = slice(h * HEAD, (h + 1) * HEAD)
            for j, src in enumerate((dq_ref, dk_ref, dv_ref)):
                for r in range(d):
                    _store_strided(nat_refs[j], r, tn, d, src[h, r])
            ov_ref[:, cs] = nat_refs[2][...].astype(BF16)
            for j, (x_ref, gn_ref, out_ref, acc_ref) in enumerate(((q_ref, qg_ref, oq_ref, gq_ref),
                                                                    (k_ref, kg_ref, ok_ref, gk_ref))):
                xhat, r = _rms(x_ref[:, cs])
                dx, dgr = _rms_bwd(nat_refs[j][...], xhat, r, gn_ref[...])
                out_ref[:, cs] = dx.astype(BF16)
                acc_ref[...] += jnp.sum(dgr, axis=0, keepdims=True)

    res = pl.BlockSpec((DA_GH, d, tn, HEAD), lambda i: (0, 0, i, 0))
    col = lambda off: pl.BlockSpec((rb, wide), lambda i: (i, off // wide + g))
    vec = pl.BlockSpec((1, HEAD), lambda i: (0, 0))
    nat = pl.BlockSpec((rb, wide), lambda i: (i, 0))
    shape = jax.ShapeDtypeStruct((rows, wide), BF16)
    gshape = jax.ShapeDtypeStruct((1, HEAD), F32)
    return pl.pallas_call(
        body, name=name, grid=(rows // rb,),
        in_specs=[res, res, res, col(C_DQ), col(C_DK), vec, vec],
        out_specs=[nat, nat, nat, vec, vec], out_shape=[shape, shape, shape, gshape, gshape],
        scratch_shapes=[pltpu.VMEM((rb, HEAD), F32)] * 3,
        compiler_params=_params(("arbitrary",)),
    )(dqr, dkr, dvr, proj, proj, q_gain, k_gain)


def _mem_fwd(proj, kv, q_gain, k_gain, *, name):
    rows = proj.shape[0]
    tm = min(512, rows)
    n_mem = kv.shape[0]

    def body(q_ref, k_ref, v_ref, qg_ref, kg_ref, o_ref):
        qhat, _ = _rms(q_ref[...])
        khat, _ = _rms(k_ref[...])
        s = _dot_nt(qhat * qg_ref[...], khat * kg_ref[...]) * ATT_SCALE
        p = jnp.exp(s - jnp.max(s, axis=-1, keepdims=True))
        p = p / jnp.sum(p, axis=-1, keepdims=True)
        o_ref[...] = _dot(p, v_ref[...]).astype(BF16)

    vec = pl.BlockSpec((1, HEAD), lambda i, h: (0, 0))
    return pl.pallas_call(
        body, name=name, grid=(rows // tm, MEM_HEADS),
        in_specs=[pl.BlockSpec((tm, HEAD), lambda i, h: (i, C_MQ // HEAD + h)),
                  pl.BlockSpec((n_mem, HEAD), lambda i, h: (0, h)),
                  pl.BlockSpec((n_mem, HEAD), lambda i, h: (0, MEM_HEADS + h)), vec, vec],
        out_specs=pl.BlockSpec((tm, HEAD), lambda i, h: (i, h)),
        out_shape=jax.ShapeDtypeStruct((rows, MEM_HEADS * HEAD), BF16),
        compiler_params=_params(("parallel", "parallel")),
    )(proj, kv, kv, q_gain, k_gain)


def _mem_bwd(dout, proj, kv, q_gain, k_gain, dproj, *, name):
    rows = proj.shape[0]
    tm = min(512, rows)
    steps = rows // tm
    n_mem = kv.shape[0]

    def body(d_ref, q_ref, k_ref, v_ref, qg_ref, kg_ref, dp_in, dq_ref, dk_ref, dv_ref, gq_ref, gk_ref, dkn_ref):
        del dp_in
        h, i = pl.program_id(0), pl.program_id(1)

        @pl.when((h == 0) & (i == 0))
        def _():
            gq_ref[...] = jnp.zeros_like(gq_ref)
            gk_ref[...] = jnp.zeros_like(gk_ref)

        @pl.when(i == 0)
        def _():
            dkn_ref[...] = jnp.zeros_like(dkn_ref)
            dv_ref[...] = jnp.zeros_like(dv_ref)

        qhat, rq = _rms(q_ref[...])
        khat, rk = _rms(k_ref[...])
        qn, kn = qhat * qg_ref[...], khat * kg_ref[...]
        s = _dot_nt(qn, kn) * ATT_SCALE
        p = jnp.exp(s - jnp.max(s, axis=-1, keepdims=True))
        p = p / jnp.sum(p, axis=-1, keepdims=True)
        dout = d_ref[...]
        dp = _dot_nt(dout, v_ref[...])
        ds = p * (dp - jnp.sum(p * dp, axis=-1, keepdims=True))
        dv_ref[...] += _dot_tn(p, dout)
        dkn_ref[...] += _dot_tn(ds, qn) * ATT_SCALE
        dq, dgr = _rms_bwd(_dot(ds, kn) * ATT_SCALE, qhat, rq, qg_ref[...])
        dq_ref[...] = dq.astype(BF16)
        gq_ref[...] += jnp.sum(dgr, axis=0, keepdims=True)

        @pl.when(i == steps - 1)
        def _():
            dk, dgk = _rms_bwd(dkn_ref[...], khat, rk, kg_ref[...])
            dk_ref[...] = dk
            gk_ref[...] += jnp.sum(dgk, axis=0, keepdims=True)

    vec = pl.BlockSpec((1, HEAD), lambda h, i: (0, 0))
    memh = pl.BlockSpec((n_mem, HEAD), lambda h, i: (0, h))
    tok = pl.BlockSpec((tm, HEAD), lambda h, i: (i, h))
    mq = pl.BlockSpec((tm, HEAD), lambda h, i: (i, C_MQ // HEAD + h))
    gshape = jax.ShapeDtypeStruct((1, HEAD), F32)
    return pl.pallas_call(
        body, name=name, grid=(MEM_HEADS, steps),
        in_specs=[tok, mq, memh, pl.BlockSpec((n_mem, HEAD), lambda h, i: (0, MEM_HEADS + h)), vec, vec, ANY],
        out_specs=[mq, memh, memh, vec, vec],
        out_shape=[jax.ShapeDtypeStruct(dproj.shape, dproj.dtype),
                   jax.ShapeDtypeStruct((n_mem, MEM_HEADS * HEAD), F32),
                   jax.ShapeDtypeStruct((n_mem, MEM_HEADS * HEAD), F32), gshape, gshape],
        input_output_aliases={6: 0},
        scratch_shapes=[pltpu.VMEM((n_mem, HEAD), F32)],
        compiler_params=_params(("arbitrary", "arbitrary")),
    )(dout, proj, kv, kv, q_gain, k_gain, dproj)


def _branch_fwd(o_hg, o_da, o_mem, proj, wp_hg, wp_da, wp_mem, *, name):
    rows = o_hg.shape[0]
    tm = min(256, rows)

    def body(a_ref, b_ref, c_ref, ga_ref, gb_ref, gc_ref, wa_ref, wb_ref, wc_ref, o_ref):
        merged = _sigmoid(ga_ref[...]) * _dot(a_ref[...], wa_ref[...])
        merged += _sigmoid(gb_ref[...]) * _dot(b_ref[...], wb_ref[...])
        merged += _sigmoid(gc_ref[...]) * _dot(c_ref[...], wc_ref[...])
        o_ref[...] = merged.astype(BF16)

    row = lambda w: pl.BlockSpec((tm, w), lambda i: (i, 0))
    gate = lambda off: pl.BlockSpec((tm, D_MODEL), lambda i: (i, off // D_MODEL))
    full = lambda a: pl.BlockSpec(a.shape, lambda i: (0, 0))
    return pl.pallas_call(
        body, name=name, grid=(rows // tm,),
        in_specs=[row(o_hg.shape[1]), row(o_da.shape[1]), row(o_mem.shape[1]),
                  gate(C_GHG), gate(C_GDA), gate(C_GMEM), full(wp_hg), full(wp_da), full(wp_mem)],
        out_specs=row(D_MODEL), out_shape=jax.ShapeDtypeStruct((rows, D_MODEL), BF16),
        compiler_params=_params(("parallel",)),
    )(o_hg, o_da, o_mem, proj, proj, proj, wp_hg, wp_da, wp_mem)


def _branch_bwd(dm, o_hg, o_da, o_mem, proj, wp_hg, wp_da, wp_mem, *, name):
    rows = o_hg.shape[0]
    tm = min(256, rows)

    def body(dm_ref, a_ref, b_ref, c_ref, ga_ref, gb_ref, gc_ref, wa_ref, wb_ref, wc_ref, dp_ref, *outs):
        dmv = dm_ref[...]
        for j, (o_ref, g_ref, w_ref) in enumerate(((a_ref, ga_ref, wa_ref), (b_ref, gb_ref, wb_ref),
                                                   (c_ref, gc_ref, wc_ref))):
            z = _dot(o_ref[...], w_ref[...])
            gs = _sigmoid(g_ref[...])
            dz = (dmv * gs).astype(BF16)
            dp_ref[:, j * D_MODEL:(j + 1) * D_MODEL] = (dmv * z * gs * (1.0 - gs)).astype(BF16)
            outs[2 * j][...] = dz
            outs[2 * j + 1][...] = _dot_nt(dz, w_ref[...])

    row = lambda w: pl.BlockSpec((tm, w), lambda i: (i, 0))
    gate = lambda off: pl.BlockSpec((tm, D_MODEL), lambda i: (i, off // D_MODEL))
    full = lambda a: pl.BlockSpec(a.shape, lambda i: (0, 0))
    out_specs = [pl.BlockSpec((pl.Element(tm), pl.Element(3 * D_MODEL)), lambda i: (i * tm, C_GHG))]
    out_shape = [jax.ShapeDtypeStruct((rows, IN_COLS), BF16)]
    for o in (o_hg, o_da, o_mem):
        out_specs += [row(D_MODEL), row(o.shape[1])]
        out_shape += [jax.ShapeDtypeStruct((rows, D_MODEL), BF16), jax.ShapeDtypeStruct((rows, o.shape[1]), F32)]
    return pl.pallas_call(
        body, name=name, grid=(rows // tm,),
        in_specs=[row(D_MODEL), row(o_hg.shape[1]), row(o_da.shape[1]), row(o_mem.shape[1]),
                  gate(C_GHG), gate(C_GDA), gate(C_GMEM), full(wp_hg), full(wp_da), full(wp_mem)],
        out_specs=out_specs, out_shape=out_shape,
        compiler_params=_params(("parallel",)),
    )(dm, o_hg, o_da, o_mem, proj, proj, proj, wp_hg, wp_da, wp_mem)


def _ffn_in(h2, w_ab, *, name):
    rows, dff = h2.shape[0], w_ab.shape[1] // 2
    tm, tn = min(2048, rows), 256

    def body(h_ref, wa_ref, wb_ref, a_ref, b_ref, u_ref):
        a = _dot(h_ref[...], wa_ref[...])
        b = _dot(h_ref[...], wb_ref[...])
        a_ref[...] = a.astype(BF16)
        b_ref[...] = b.astype(BF16)
        u_ref[...] = (a * _sigmoid(a) * b).astype(BF16)

    out = pl.BlockSpec((tm, tn), lambda i, j: (i, j))
    return pl.pallas_call(
        body, name=name, grid=(rows // tm, dff // tn),
        in_specs=[pl.BlockSpec((tm, D_MODEL), lambda i, j: (i, 0)),
                  pl.BlockSpec((D_MODEL, tn), lambda i, j: (0, j)),
                  pl.BlockSpec((D_MODEL, tn), lambda i, j: (0, dff // tn + j))],
        out_specs=[out, out, out],
        out_shape=[jax.ShapeDtypeStruct((rows, dff), BF16)] * 3,
        compiler_params=_params(("parallel", "parallel")),
    )(h2, w_ab, w_ab)


def _ffn_act_bwd(dy, w_out, a, b, *, name):
    rows, dff = a.shape
    tm, tn = min(1024, rows), 256

    def body(dy_ref, w_ref, a_ref, b_ref, da_ref, db_ref):
        du = _dot_nt(dy_ref[...], w_ref[...])
        av, bv = a_ref[...].astype(F32), b_ref[...].astype(F32)
        sa = _sigmoid(av)
        da_ref[...] = (du * bv * sa * (1.0 + av * (1.0 - sa))).astype(BF16)
        db_ref[...] = (du * av * sa).astype(BF16)

    tile = pl.BlockSpec((tm, tn), lambda i, j: (i, j))
    return pl.pallas_call(
        body, name=name, grid=(rows // tm, dff // tn),
        in_specs=[pl.BlockSpec((tm, D_MODEL), lambda i, j: (i, 0)),
                  pl.BlockSpec((tn, D_MODEL), lambda i, j: (j, 0)), tile, tile],
        out_specs=[tile, tile],
        out_shape=[jax.ShapeDtypeStruct((rows, dff), BF16), jax.ShapeDtypeStruct((rows, dff), BF16)],
        compiler_params=_params(("parallel", "parallel")),
    )(dy, w_out, a, b)


def _lower_bound(lb_fw, lb_bw, *, name):
    def body(a_ref, b_ref, oa_ref, ob_ref):
        for src, dst in ((a_ref, oa_ref), (b_ref, ob_ref)):
            dst[...] = _sigmoid(src[0:1, :] - src[1:2, :])

    shape = jax.ShapeDtypeStruct((1, lb_fw.shape[1]), F32)
    return pl.pallas_call(body, name=name, out_shape=[shape, shape])(lb_fw, lb_bw)


def _local_step(x, mem, tgt, p, ex):
    rows = x.shape[0]
    lb_fw, lb_bw = _lower_bound(p["lb_logits_fw"], p["lb_logits_bw"], name="lower_bound")

    h = _rms_fwd(x, p["norm_mix_gain"], name="norm_mix")
    w = {"w_in": ex.w_in}
    proj, carried = _matmul([(h, w["w_in"])], "nn", BF16, tm=1024, tn=1024, carry=ex.late_carry(), name="proj_in")
    w.update(ex.late_weights(carried))
    o_fw, st_fw = _gla_fwd(proj, lb_fw, f_off=C_FF, rev=False, name="gla_fwd_fw")
    o_bw, st_bw = _gla_fwd(proj, lb_bw, f_off=C_FB, rev=True, name="gla_fwd_bw")
    o_hg = _hg_out_fwd(o_fw, o_bw, proj, p["hg_norm_gain"], name="hg_out")

    qkv_r, outs, lses = [], [], []
    for g in range(len(DA_GROUPS)):
        qr, kr, vr = _da_prep(proj, p["da_q_gain"], p["da_k_gain"], g, name=f"da_prep{g}")
        og, lg = _band_fwd(qr, kr, vr, g, name=f"band_fwd{g}")
        qkv_r.append((qr, kr, vr))
        outs.append(og)
        lses.append(lg)
    o_da, lse_da = _da_merge(outs, lses, rows, name="da_merge")

    mem_n = _rms_fwd(mem, p["norm_mem_gain"], name="norm_mem")
    kv = _matmul([(mem_n, w["w_mem_kv"])], "nn", F32, tm=256, tn=512, name="mem_kv")
    o_mem = _mem_fwd(proj, kv, p["mem_q_gain"], p["mem_k_gain"], name="mem_attn")

    merged = _branch_fwd(o_hg, o_da, o_mem, proj, w["w_proj_hg"], w["w_proj_da"], w["w_proj_mem"],
                         name="branch_merge")
    x1, h2 = _out_norm(merged, w["w_out"], x, p["norm_ffn_gain"], name="mix_out_norm")
    a, b, u = _ffn_in(h2, w["w_ffn_in"], name="ffn_in")
    dy, dy_b, loss = _out_loss(u, w["w_ffn_out"], x1, tgt, name="ffn_out_loss")

    gw, gwb, gs = {}, {}, {}
    gw["w_ffn_out"], gwb["w_ffn_out"] = _matmul([(u, dy_b)], "tn", F32, tm=256, tn=1024, also=BF16, name="g_ffn_out")
    da, db = _ffn_act_bwd(dy_b, w["w_ffn_out"], a, b, name="ffn_act_bwd")
    gw["w_ffn_a"], gwb["w_ffn_a"] = _matmul([(h2, da)], "tn", F32, tm=512, tn=256, also=BF16, name="g_ffn_a")
    gw["w_ffn_b"], gwb["w_ffn_b"] = _matmul([(h2, db)], "tn", F32, tm=512, tn=256, also=BF16, name="g_ffn_b")
    dh2 = _matmul([(da, w["w_ffn_in"], 0), (db, w["w_ffn_in"], 1)], "nt", F32, tm=256, tn=1024, name="d_h2")
    dx1, dx1_b, gs["norm_ffn_gain"] = _rms_bwd_rows(dh2, x1, p["norm_ffn_gain"], dy, name="norm_ffn_bwd")
    gw["w_out"], gwb["w_out"] = _matmul([(merged, dx1_b)], "tn", F32, tm=512, tn=512, also=BF16, name="g_out")
    dmerged = _matmul([(dx1_b, w["w_out"])], "nt", F32, tm=512, tn=512, name="d_merged")
    dproj, dz_hg, do_hg, dz_da, do_da, dz_mem, do_mem = _branch_bwd(
        dmerged, o_hg, o_da, o_mem, proj, w["w_proj_hg"], w["w_proj_da"], w["w_proj_mem"], name="branch_bwd")
    gw["w_proj_hg"], gwb["w_proj_hg"] = _matmul([(o_hg, dz_hg)], "tn", F32, tm=512, tn=512, also=BF16, name="g_proj_hg")
    gw["w_proj_da"], gwb["w_proj_da"] = _matmul([(o_da, dz_da)], "tn", F32, tm=512, tn=512, also=BF16, name="g_proj_da")
    gw["w_proj_mem"], gwb["w_proj_mem"] = _matmul([(o_mem, dz_mem)], "tn", F32, tm=512, tn=512, also=BF16, name="g_proj_mem")

    dproj, dk_mem, dv_mem, gs["mem_q_gain"], gs["mem_k_gain"] = _mem_bwd(
        do_mem, proj, kv, p["mem_q_gain"], p["mem_k_gain"], dproj, name="mem_attn_bwd")
    dkv = jnp.concatenate([dk_mem, dv_mem], axis=1).astype(BF16)
    gw["w_mem_kv"], gwb["w_mem_kv"] = _matmul([(mem_n, dkv)], "tn", F32, tm=512, tn=512, also=BF16, name="g_mem_kv")
    dmem_n = _matmul([(dkv, w["w_mem_kv"])], "nt", F32, tm=256, tn=512, name="d_mem_n")
    _, _, gs["norm_mem_gain"] = _rms_bwd_rows(dmem_n, mem, p["norm_mem_gain"], None, name="norm_mem_bwd")

    prep = _da_bwd_prep(do_da, o_da, lse_da, name="da_bwd_prep")
    d_da, gq_parts, gk_parts = [], [], []
    for g in range(len(DA_GROUPS)):
        qr, kr, vr = qkv_r[g]
        dor, lser, deltar = prep[3 * g:3 * g + 3]
        dqr, dkr, dvr = _band_bwd(qr, kr, vr, dor, lser, deltar, g, name=f"band_bwd{g}")
        dq, dk, dv, gq, gk = _da_prep_bwd(dqr, dkr, dvr, proj, p["da_q_gain"], p["da_k_gain"], g,
                                          name=f"da_prep_bwd{g}")
        d_da.append((dq, dk, dv))
        gq_parts.append(gq)
        gk_parts.append(gk)
    for j, (off, tag) in enumerate(((C_DQ, "q"), (C_DK, "k"), (C_DV, "v"))):
        dproj = _fill_columns(dproj, [t[j] for t in d_da], off, name=f"dproj_fill_da_{tag}")

    dproj, do_gla, gs["hg_norm_gain"] = _hg_out_bwd(do_hg, o_fw, o_bw, proj, p["hg_norm_gain"], dproj,
                                                    name="hg_out_bwd")
    (dq_f, dproj, dv_f, dlb_fw), carried = _gla_bwd(proj, lb_fw, do_gla, st_fw, None, dproj, f_off=C_FF, rev=False,
                                                    carry=ex.early_carry(gw, gwb), name="gla_bwd_fw")
    ex.early_done(carried)
    dproj, dfl_bw, dv_hg, dlb_bw = _gla_bwd(proj, lb_bw, do_gla, st_bw, (dq_f, dv_f), dproj, f_off=C_FB, rev=True,
                                            name="gla_bwd_bw")
    dproj = _fill_columns(dproj, [dfl_bw, dv_hg], C_FB, name="dproj_fill")
    g_in, g_in_b = _matmul([(h, dproj)], "tn", F32, tm=512, tn=1024, also=BF16, name="g_in")
    dh, carried = _matmul([(dproj, w["w_in"])], "nt", F32, tm=1024, tn=1024, tk=IN_COLS // 8,
                          carry=ex.w_in_carry(g_in, g_in_b), name="d_h")
    ex.w_in_done(carried)
    grad_x, _, gs["norm_mix_gain"] = _rms_bwd_rows(dh, x, p["norm_mix_gain"], dx1, name="norm_mix_bwd")

    small = _small_pack(gs, gq_parts, gk_parts, dlb_fw, dlb_bw, lb_fw, lb_bw, name="small_pack")
    return loss, grad_x, small


def _small_pack(gs, gq_parts, gk_parts, dlb_fw, dlb_bw, lb_fw, lb_bw, *, name):
    def body(g_mix, g_mem, g_ffn, dfw, dbw, lfw, lbw, g_hg, q0, q1, q2, k0, k1, k2, g_mq, g_mk, o_ref):
        o_ref[0:1, :] = g_mix[...]
        o_ref[1:2, :] = g_mem[...]
        o_ref[2:3, :] = g_ffn[...]
        for base, d_ref, l_ref in ((3, dfw, lfw), (5, dbw, lbw)):
            lbv = l_ref[...]
            t = d_ref[...] * lbv * (1.0 - lbv)
            o_ref[base:base + 1, :] = t
            o_ref[base + 1:base + 2, :] = -t
        o_ref[7:8, :] = jnp.zeros((1, D_MODEL), F32)
        o_ref[7:8, 0:HEAD] = g_hg[...]
        o_ref[7:8, HEAD:2 * HEAD] = q0[...] + q1[...] + q2[...]
        o_ref[7:8, 2 * HEAD:3 * HEAD] = k0[...] + k1[...] + k2[...]
        o_ref[7:8, 3 * HEAD:4 * HEAD] = g_mq[...]
        o_ref[7:8, 4 * HEAD:5 * HEAD] = g_mk[...]

    return pl.pallas_call(body, name=name, out_shape=jax.ShapeDtypeStruct((8, D_MODEL), F32))(
        gs["norm_mix_gain"], gs["norm_mem_gain"], gs["norm_ffn_gain"], dlb_fw, dlb_bw, lb_fw, lb_bw,
        gs["hg_norm_gain"], *gq_parts, *gk_parts, gs["mem_q_gain"], gs["mem_k_gain"])


def _row_tile(rows, cols, n_arrays):
    budget = (16 * 1024 * 1024) // (2 * 4 * cols * n_arrays)
    tr = rows
    while tr > budget and tr % 2 == 0 and (tr // 2) % 16 == 0:
        tr //= 2
    return tr


def _cast_into_full(a, chip, rows, cols, axis, *, name):
    sr, sc = a.shape
    tr = _row_tile(sr, sc, 2)

    def body(chip_ref, a_ref, o_ref):
        del chip_ref
        o_ref[...] = a_ref[...].astype(BF16)

    if axis == 1:
        out_map = lambda i, chip_ref: (i, chip_ref[0])
    else:
        out_map = lambda i, chip_ref: (chip_ref[0] * (sr // tr) + i, 0)
    return pl.pallas_call(
        body, name=name,
        grid_spec=pltpu.PrefetchScalarGridSpec(
            num_scalar_prefetch=1, grid=(sr // tr,),
            in_specs=[pl.BlockSpec((tr, sc), lambda i, chip_ref: (i, 0))],
            out_specs=pl.BlockSpec((tr, sc), out_map)),
        out_shape=jax.ShapeDtypeStruct((rows, cols), BF16),
        compiler_params=_params(("parallel",)))(chip, a)


def _add_halves(items, *, name):
    rows = items[0][3].shape[0]
    widths = [ra.shape[1] for _, _, _, ra in items]
    tr = _row_tile(rows, sum(widths), 4)

    def body(*refs):
        o_ref = refs[-1]
        first = lax.axis_index("c") == 0
        off = 0
        for j, wd in enumerate(widths):
            h0, h1, ra = refs[3 * j:3 * j + 3]
            o_ref[:, off:off + wd] = (jnp.where(first, h0[...], h1[...]) + ra[...]).astype(BF16)
            off += wd

    in_specs, ins = [], []
    for (g, haxis, hsize, ra), wd in zip(items, widths):
        if haxis == 0:
            in_specs += [pl.BlockSpec((tr, wd), lambda i: (i, 0)),
                         pl.BlockSpec((tr, wd), lambda i, o=hsize // tr: (o + i, 0))]
        else:
            in_specs += [pl.BlockSpec((tr, wd), lambda i: (i, 0)), pl.BlockSpec((tr, wd), lambda i: (i, 1))]
        in_specs.append(pl.BlockSpec((tr, wd), lambda i: (i, 0)))
        ins += [g, g, ra]
    return pl.pallas_call(body, name=name, grid=(rows // tr,), in_specs=in_specs,
                          out_specs=pl.BlockSpec((tr, sum(widths)), lambda i: (i, 0)),
                          out_shape=jax.ShapeDtypeStruct((rows, sum(widths)), BF16),
                          compiler_params=_params(("parallel",)))(*ins)


def _add_slots(rb, *, name):
    _, rows, cols = rb.shape
    tr = _row_tile(rows, cols, 5)

    def body(r0, r1, r2, r3, o_ref):
        o_ref[...] = ((r0[...].astype(F32) + r1[...].astype(F32)) + r2[...].astype(F32)) + r3[...].astype(F32)

    slot = lambda s: pl.BlockSpec((None, tr, cols), lambda i: (s, i, 0))
    return pl.pallas_call(body, name=name, grid=(rows // tr,), in_specs=[slot(s) for s in range(4)],
                          out_specs=pl.BlockSpec((tr, cols), lambda i: (i, 0)),
                          out_shape=jax.ShapeDtypeStruct((rows, cols), F32),
                          compiler_params=_params(("parallel",)))(rb, rb, rb, rb)


def _adamw(w, g, m, v, *, name):
    rows, cols = w.shape
    tr = _row_tile(rows, cols, 7) if rows % 16 == 0 else rows
    c1 = 1.0 - ADAM_B1 ** ADAM_STEP
    c2 = 1.0 - ADAM_B2 ** ADAM_STEP

    def body(w_ref, g_ref, m_ref, v_ref, d_ref, mo_ref, vo_ref):
        gv = g_ref[...]
        mn = ADAM_B1 * m_ref[...] + (1.0 - ADAM_B1) * gv
        vn = ADAM_B2 * v_ref[...] + (1.0 - ADAM_B2) * (gv * gv)
        mo_ref[...] = mn
        vo_ref[...] = vn
        d_ref[...] = -ADAM_LR * ((mn / c1) / (jnp.sqrt(vn / c2) + ADAM_EPS) + ADAM_WD * w_ref[...])

    spec = pl.BlockSpec((tr, cols), lambda i: (i, 0))
    shape = jax.ShapeDtypeStruct((rows, cols), F32)
    return pl.pallas_call(body, name=name, grid=(rows // tr,), in_specs=[spec] * 4, out_specs=[spec] * 3,
                          out_shape=[shape] * 3, compiler_params=_params(("parallel",)))(w, g, m, v)


W_SPECS = (
    ("w_in", 1024, IN_COLS, 1, IN_COLS // 4),
    ("w_mem_kv", 1024, 1024, 0, 256),
    ("w_proj_hg", 1024, 1024, 0, 256),
    ("w_proj_da", 512, 1024, 1, 256),
    ("w_proj_mem", 512, 1024, 1, 256),
    ("w_out", 1024, 1024, 0, 256),
    ("w_ffn_in", 1024, 2 * D_FF, 1, 2 * D_FF // 4),
    ("w_ffn_out", D_FF, 1024, 0, D_FF // 4),
)
CHIP_FLIPS = ((1, 0), (0, 1), (1, 1))
ANY = pl.BlockSpec(memory_space=pl.ANY)
DMA_CHUNK_BYTES = 1 << 20
STAGE_BYTES = 2 << 20


def _place():
    x, y, c = lax.axis_index("x"), lax.axis_index("y"), lax.axis_index("c")
    return x, y, c, 2 * x + y


def _flip(v, f):
    return 1 - v if f else v


def _slab(ref, axis, idx, size):
    start = pl.multiple_of(idx * size, size)
    return ref.at[pl.ds(start, size), :] if axis == 0 else ref.at[:, pl.ds(start, size)]


def _chunked(make, src, dst, want="both"):
    rows, cols = src.shape
    row_bytes = cols * jnp.dtype(src.dtype).itemsize
    k = 1
    while rows % (2 * k) == 0 and (rows // (2 * k)) % 16 == 0 and (rows // k) * row_bytes > DMA_CHUNK_BYTES:
        k *= 2
    cr = rows // k
    parts = []
    if want != "wait":
        parts = [make(src.at[pl.ds(j * cr, cr), :], dst.at[pl.ds(j * cr, cr), :]) for j in range(k)]
    return parts, (make(src, dst) if want != "start" else None)


def _half_spec(rows, cols, axis):
    return (0, rows // 2) if axis == 1 else (1, cols // 2)


def _staged(src, remote_dst, local_dst, sibling, load_sems, send_sems, store_sems, recv_sem):
    rows, cols = src.shape
    row_bytes = cols * jnp.dtype(src.dtype).itemsize
    k = 1
    while (rows // k) * row_bytes > STAGE_BYTES and rows % (2 * k) == 0 and (rows // (2 * k)) % 16 == 0:
        k *= 2
    cr = rows // k
    piece = lambda ref, j: ref.at[pl.ds(j * cr, cr), :]

    def run(buf):
        loads = [pltpu.make_async_copy(piece(src, j), buf.at[j % 2], load_sems.at[j % 2]) for j in range(k)]
        outs = [[pltpu.make_async_remote_copy(src_ref=buf.at[j % 2], dst_ref=piece(remote_dst, j),
                                              send_sem=send_sems.at[j % 2], recv_sem=recv_sem,
                                              device_id=sibling, device_id_type=MESH)] for j in range(k)]
        if local_dst is not None:
            for j in range(k):
                outs[j].append(pltpu.make_async_copy(buf.at[j % 2], piece(local_dst, j), store_sems.at[j % 2]))

        def drained(j):
            outs[j][0].wait_send()
            for cp in outs[j][1:]:
                cp.wait()

        loads[0].start()
        for j in range(k):
            loads[j].wait()
            for cp in outs[j]:
                cp.start()
            if j + 1 < k:
                if j >= 1:
                    drained(j - 1)
                loads[j + 1].start()
        for j in range(max(0, k - 2), k):
            drained(j)

    pl.run_scoped(run, pltpu.VMEM((2, cr, cols), src.dtype))


def _landed(ref, recv_sem, send_sem):
    pltpu.make_async_remote_copy(src_ref=ref, dst_ref=ref, send_sem=send_sem, recv_sem=recv_sem,
                                 device_id=(lax.axis_index("x"), lax.axis_index("y"), lax.axis_index("c")),
                                 device_id_type=MESH).wait_recv()


def _half_slab(ref, spec, chip, half):
    _, rows, cols, axis, size = spec
    haxis, hsize = _half_spec(rows, cols, axis)
    return _slab(_slab(ref, axis, chip, size), haxis, half, hsize)


def _gather_sends(outs, specs, ici_send, ici_recv, want="both"):
    x, y, c, p = _place()
    sent = []
    for wi, spec in enumerate(specs):
        mine = _half_slab(outs[wi], spec, p, c)
        for k, (fx, fy) in enumerate(CHIP_FLIPS):
            sent.append(_chunked(lambda s, d, j=3 * wi + k, fx=fx, fy=fy: pltpu.make_async_remote_copy(
                src_ref=s, dst_ref=d, send_sem=ici_send.at[j], recv_sem=ici_recv.at[j],
                device_id=(_flip(x, fx), _flip(y, fy), c), device_id_type=MESH), mine, mine, want))
    return sent


def _gather_weights(fulls, specs, *, name):
    n = len(specs)

    def body(*refs):
        outs = refs[n:2 * n]
        ici_send, ici_recv, load_sems, d2d_send, d2d_recv = refs[2 * n:]
        x, y, c, _ = _place()
        sent = _gather_sends(outs, specs, ici_send, ici_recv)
        for parts, _ in sent:
            for cp in parts:
                cp.start()
        for k, (fx, fy) in enumerate(CHIP_FLIPS):
            q = 2 * _flip(x, fx) + _flip(y, fy)
            for wi in range(n):
                sent[3 * wi + k][1].wait_recv()
                got = _half_slab(outs[wi], specs[wi], q, c)
                _staged(got, got, None, (x, y, 1 - c), load_sems, d2d_send, None, d2d_recv.at[3 * wi + k])
        for _, whole in sent:
            whole.wait_send()
        for k, (fx, fy) in enumerate(CHIP_FLIPS):
            q = 2 * _flip(x, fx) + _flip(y, fy)
            for wi in range(n):
                _landed(_half_slab(outs[wi], specs[wi], q, 1 - c), d2d_recv.at[3 * wi + k], d2d_send.at[0])

    return pl.pallas_call(
        body, name=name, in_specs=[ANY] * n, out_specs=[ANY] * n,
        out_shape=[jax.ShapeDtypeStruct(f.shape, f.dtype) for f in fulls],
        input_output_aliases={i: i for i in range(n)},
        scratch_shapes=[pltpu.SemaphoreType.DMA((3 * n,)), pltpu.SemaphoreType.DMA((3 * n,)),
                        pltpu.SemaphoreType.DMA((2,)), pltpu.SemaphoreType.DMA((2,)),
                        pltpu.SemaphoreType.DMA((3 * n,))],
    )(*fulls)


def _gather_chips_carry(fulls, specs):
    n = len(specs)

    def issue(ins, outs, sems, want):
        del ins
        return [(parts, whole, True) for parts, whole in _gather_sends(outs, specs, sems[0], sems[1], want)]

    return _Carry(list(fulls), [jax.ShapeDtypeStruct(f.shape, f.dtype) for f in fulls], {i: i for i in range(n)},
                  [pltpu.SemaphoreType.DMA((3 * n,)), pltpu.SemaphoreType.DMA((3 * n,))], issue)


def _gather_pass_on(fulls, specs, *, name):
    n = len(specs)

    def body(*refs):
        outs = refs[n:2 * n]
        load_sems, d2d_send, d2d_recv = refs[2 * n:2 * n + 3]
        bufs = refs[2 * n + 3:]
        x, y, c, _ = _place()
        loads, sends = [], []
        for k, (fx, fy) in enumerate(CHIP_FLIPS):
            q = 2 * _flip(x, fx) + _flip(y, fy)
            for wi in range(n):
                j = 3 * wi + k
                got = _half_slab(outs[wi], specs[wi], q, c)
                loads.append(pltpu.make_async_copy(got, bufs[j], load_sems.at[j]))
                sends.append(pltpu.make_async_remote_copy(
                    src_ref=bufs[j], dst_ref=got, send_sem=d2d_send.at[j], recv_sem=d2d_recv.at[j],
                    device_id=(x, y, 1 - c), device_id_type=MESH))
        for cp in loads:
            cp.start()
        for load, send in zip(loads, sends):
            load.wait()
            send.start()
        for cp in sends:
            cp.wait_send()
        for k, (fx, fy) in enumerate(CHIP_FLIPS):
            q = 2 * _flip(x, fx) + _flip(y, fy)
            for wi in range(n):
                _landed(_half_slab(outs[wi], specs[wi], q, 1 - c), d2d_recv.at[3 * wi + k], d2d_send.at[0])

    shapes = []
    for _, rows, cols, axis, size in specs:
        shapes += [(rows // 2, size) if axis == 1 else (size, cols // 2)] * 3
    assert sum(math.prod(s) for s in shapes) * 2 <= VMEM_LIMIT_V7X // 2
    stages = [pltpu.VMEM(s, BF16) for s in shapes]
    return pl.pallas_call(
        body, name=name, in_specs=[ANY] * n, out_specs=[ANY] * n,
        out_shape=[jax.ShapeDtypeStruct(f.shape, f.dtype) for f in fulls],
        input_output_aliases={i: i for i in range(n)},
        scratch_shapes=[pltpu.SemaphoreType.DMA((3 * n,))] * 3 + stages,
        compiler_params=pltpu.CompilerParams(vmem_limit_bytes=VMEM_LIMIT_V7X),
    )(*fulls)


def _sibling_exchange(grads, *, name):
    n = len(grads)

    def body(*refs):
        ins, outs = refs[:n], refs[n:2 * n]
        load_sems, send_sems, recv_sems = refs[2 * n:]
        x, y, c, _ = _place()
        for i, (_, haxis, hsize) in enumerate(grads):
            _staged(_slab(ins[i], haxis, 1 - c, hsize), outs[i], None, (x, y, 1 - c),
                    load_sems, send_sems, None, recv_sems.at[i])
        for i in range(n):
            _landed(outs[i], recv_sems.at[i], send_sems.at[0])

    shapes = [jax.ShapeDtypeStruct((hsize, g.shape[1]) if haxis == 0 else (g.shape[0], hsize), g.dtype)
              for g, haxis, hsize in grads]
    return pl.pallas_call(
        body, name=name, in_specs=[ANY] * n, out_specs=[ANY] * n, out_shape=shapes,
        scratch_shapes=[pltpu.SemaphoreType.DMA((2,)), pltpu.SemaphoreType.DMA((2,)),
                        pltpu.SemaphoreType.DMA((n,))],
    )(*[g for g, _, _ in grads])


def _chip_exchange_carry(parts):
    n = len(parts)

    def issue(ins, outs, sems, want):
        send_sems, recv_sems, local_sems = sems
        x, y, c, p = _place()
        copies = []
        for i, (_, axis, size) in enumerate(parts):
            copies.append(_chunked(lambda s, d, i=i: pltpu.make_async_copy(s, d, local_sems.at[i]),
                                   _slab(ins[i], axis, p, size), outs[i].at[p], want) + (False,))
            for k, (fx, fy) in enumerate(CHIP_FLIPS):
                px, py = _flip(x, fx), _flip(y, fy)
                copies.append(_chunked(lambda s, d, j=3 * i + k, px=px, py=py: pltpu.make_async_remote_copy(
                    src_ref=s, dst_ref=d, send_sem=send_sems.at[j], recv_sem=recv_sems.at[j],
                    device_id=(px, py, c), device_id_type=MESH),
                    _slab(ins[i], axis, 2 * px + py, size), outs[i].at[p], want) + (True,))
        return copies

    shapes = []
    for a, axis, size in parts:
        shapes.append(jax.ShapeDtypeStruct((4, size, a.shape[1]) if axis == 0 else (4, a.shape[0], size), a.dtype))
    return _Carry([a for a, _, _ in parts], shapes, {},
                  [pltpu.SemaphoreType.DMA((3 * n,)), pltpu.SemaphoreType.DMA((3 * n,)),
                   pltpu.SemaphoreType.DMA((n,))], issue)


def _sibling_share(sums):
    n = len(sums)

    def body(*refs):
        ins, outs = refs[:n], refs[n:2 * n]
        load_sems, send_sems, store_sems, recv_sems = refs[2 * n:]
        x, y, c, _ = _place()
        for i, (s, haxis) in enumerate(sums):
            place = _slab(outs[i], haxis, c, s.shape[haxis])
            _staged(ins[i], place, place, (x, y, 1 - c), load_sems, send_sems, store_sems, recv_sems.at[i])
        for i, (s, haxis) in enumerate(sums):
            _landed(_slab(outs[i], haxis, 1 - c, s.shape[haxis]), recv_sems.at[i], send_sems.at[0])

    shapes = []
    for s, haxis in sums:
        r, cc = s.shape
        shapes.append(jax.ShapeDtypeStruct((2 * r, cc) if haxis == 0 else (r, 2 * cc), F32))
    return pl.pallas_call(
        body, name="grad_sibling_share", in_specs=[ANY] * n, out_specs=[ANY] * n, out_shape=shapes,
        scratch_shapes=[pltpu.SemaphoreType.DMA((2,)), pltpu.SemaphoreType.DMA((2,)),
                        pltpu.SemaphoreType.DMA((2,)), pltpu.SemaphoreType.DMA((n,))],
    )(*[s for s, _ in sums])


class _Exchanges:
    def __init__(self, fulls):
        self.w_in = _gather_weights([fulls["w_in"]], W_SPECS[:1], name="gather_w_in")[0]
        self.late_specs = W_SPECS[1:]
        self.late = [fulls[s[0]] for s in self.late_specs]
        self.slots = {}

    def late_carry(self):
        return _gather_chips_carry(self.late, self.late_specs)

    def late_weights(self, carried):
        done = _gather_pass_on(carried, self.late_specs, name="gather_pass_on")
        return dict(zip([s[0] for s in self.late_specs], done))

    def _half_sums(self, gw, gwb, specs, tag):
        grads, sent = [], []
        for name, _, _, axis, _ in specs:
            for part in (("w_ffn_a", "w_ffn_b") if name == "w_ffn_in" else (name,)):
                half = _half_spec(gw[part].shape[0], gw[part].shape[1], axis)
                grads.append((gw[part],) + half)
                sent.append((gwb[part],) + half)
        theirs = _sibling_exchange(sent, name=f"grad_sibling_exchange_{tag}")
        parts, j = [], 0
        for name, _, _, axis, size in specs:
            take = 2 if name == "w_ffn_in" else 1
            items = [grads[i] + (theirs[i],) for i in range(j, j + take)]
            parts.append((_add_halves(items, name=f"half_sum_{name}"), axis, size))
            j += take
        return _chip_exchange_carry(parts)

    def early_carry(self, gw, gwb):
        return self._half_sums(gw, gwb, self.late_specs, "early")

    def early_done(self, carried):
        self.slots.update(zip([s[0] for s in self.late_specs], carried))

    def w_in_carry(self, g, gb):
        return self._half_sums({"w_in": g}, {"w_in": gb}, W_SPECS[:1], "w_in")

    def w_in_done(self, carried):
        self.slots["w_in"] = carried[0]

    def reduced(self):
        sums = []
        for name, rows, cols, axis, _ in W_SPECS:
            sums.append((_add_slots(self.slots[name], name=f"chip_sum_{name}"), _half_spec(rows, cols, axis)[0]))
        return dict(zip([s[0] for s in W_SPECS], _sibling_share(sums)))


def _small_allreduce(sv):
    def body(sv_ref, o_ref, slots_ref, send_sems, recv_sems):
        x, y, c, _ = _place()
        me = 4 * x + 2 * y + c
        slots_ref[me] = sv_ref[...]
        copies = []
        for k in range(1, 8):
            fx, fy, fc = (k >> 2) & 1, (k >> 1) & 1, k & 1
            copies.append(pltpu.make_async_remote_copy(
                src_ref=sv_ref, dst_ref=slots_ref.at[me], send_sem=send_sems.at[k - 1],
                recv_sem=recv_sems.at[k - 1], device_id=(_flip(x, fx), _flip(y, fy), _flip(c, fc)),
                device_id_type=MESH))
        for cp in copies:
            cp.start()
        for cp in copies:
            cp.wait_recv()
        for cp in copies:
            cp.wait_send()
        total = slots_ref[0]
        for s in range(1, 8):
            total = total + slots_ref[s]
        o_ref[...] = total

    vm = pl.BlockSpec(memory_space=pltpu.VMEM)
    return pl.pallas_call(
        body, name="small_allreduce", in_specs=[vm], out_specs=vm,
        out_shape=jax.ShapeDtypeStruct(sv.shape, F32),
        scratch_shapes=[pltpu.VMEM((8,) + sv.shape, F32), pltpu.SemaphoreType.DMA((7,)),
                        pltpu.SemaphoreType.DMA((7,))],
    )(sv)


SMALL_ROWS = (("norm_mix_gain", 0), ("norm_mem_gain", 1), ("norm_ffn_gain", 2))
SMALL_LB = (("lb_logits_fw", 3), ("lb_logits_bw", 5))
SMALL_HEAD = ("hg_norm_gain", "da_q_gain", "da_k_gain", "mem_q_gain", "mem_k_gain")


def _pack_small(d):
    last = jnp.concatenate([d[n] for n in SMALL_HEAD] + [jnp.zeros((1, D_MODEL - HEAD * len(SMALL_HEAD)), F32)], axis=1)
    return jnp.concatenate([d["norm_mix_gain"], d["norm_mem_gain"], d["norm_ffn_gain"],
                            d["lb_logits_fw"], d["lb_logits_bw"], last], axis=0)


def _unpack_small(a):
    out = {n: a[r:r + 1] for n, r in SMALL_ROWS}
    out.update({n: a[r:r + 2] for n, r in SMALL_LB})
    out.update({n: a[7:8, j * HEAD:(j + 1) * HEAD] for j, n in enumerate(SMALL_HEAD)})
    return out


PARAM_ORDER = ("norm_mix_gain", "norm_mem_gain", "w_in", "lb_logits_fw", "lb_logits_bw", "hg_norm_gain",
               "da_q_gain", "da_k_gain", "w_mem_kv", "mem_q_gain", "mem_k_gain", "w_proj_hg", "w_proj_da",
               "w_proj_mem", "w_out", "norm_ffn_gain", "w_ffn_in", "w_ffn_out")


def kernel(x, mem, norm_mix_gain, norm_mem_gain, w_in, lb_logits_fw, lb_logits_bw, hg_norm_gain, da_q_gain, da_k_gain, w_mem_kv, mem_q_gain, mem_k_gain, w_proj_hg, w_proj_da, w_proj_mem, w_out, norm_ffn_gain, w_ffn_in, w_ffn_out, loss_target, m_norm_mix_gain, m_norm_mem_gain, m_w_in, m_lb_logits_fw, m_lb_logits_bw, m_hg_norm_gain, m_da_q_gain, m_da_k_gain, m_w_mem_kv, m_mem_q_gain, m_mem_k_gain, m_w_proj_hg, m_w_proj_da, m_w_proj_mem, m_w_out, m_norm_ffn_gain, m_w_ffn_in, m_w_ffn_out, v_norm_mix_gain, v_norm_mem_gain, v_w_in, v_lb_logits_fw, v_lb_logits_bw, v_hg_norm_gain, v_da_q_gain, v_da_k_gain, v_w_mem_kv, v_mem_q_gain, v_mem_k_gain, v_w_proj_hg, v_w_proj_da, v_w_proj_mem, v_w_out, v_norm_ffn_gain, v_w_ffn_in, v_w_ffn_out):
    args = dict(locals())
    mats = tuple(s[0] for s in W_SPECS)
    flat = lambda a: a.reshape(a.shape[-2:])
    w = {n: flat(args[n]) for n in mats}
    m = {n: flat(args["m_" + n]) for n in mats}
    v = {n: flat(args["v_" + n]) for n in mats}
    small = {n: args[n] for n in PARAM_ORDER if n not in mats}

    chip = (2 * lax.axis_index("x") + lax.axis_index("y")).astype(jnp.int32).reshape(1)
    ex = _Exchanges({n: _cast_into_full(w[n], chip, rows, cols, axis, name=f"cast_{n}")
                     for n, rows, cols, axis, _ in W_SPECS})
    loss, grad_x, small_grads = _local_step(x[0], mem[0], loss_target[0], small, ex)
    grads = ex.reduced()
    small_sum = _small_allreduce(small_grads)

    delta, new_m, new_v = {}, {}, {}
    for n in mats:
        delta[n], new_m[n], new_v[n] = _adamw(w[n], grads[n], m[n], v[n], name=f"adamw_{n}")
    packed = _adamw(_pack_small(small), small_sum,
                    _pack_small({n: args["m_" + n] for n in small}),
                    _pack_small({n: args["v_" + n] for n in small}), name="adamw_small")
    grads.update(_unpack_small(small_sum))
    for dst, src in zip((delta, new_m, new_v), packed):
        dst.update(_unpack_small(src))

    def shaped(d, n):
        return d[n].reshape(args[n].shape)

    loss_sum = lax.psum(loss[0, 0], ("x", "y", "c"))
    return (loss_sum, grad_x[None], *[shaped(grads, n) for n in PARAM_ORDER], *[shaped(delta, n) for n in PARAM_ORDER],
            *[shaped(new_m, n) for n in PARAM_ORDER], *[shaped(new_v, n) for n in PARAM_ORDER])
```

```python
import functools
import math

import numpy as np
import jax
import jax.numpy as jnp
from jax import lax
from jax.experimental import pallas as pl
from jax.experimental.pallas import tpu as pltpu

F32 = jnp.float32
BF16 = jnp.bfloat16
MESH = pl.DeviceIdType.MESH

D_MODEL = 1024
HEAD = 128
HG_HEADS = 8
DA_GROUPS = ((1, 64), (4, 64), (16, 64))
DA_GH = 4
MEM_HEADS = 4
N_MEM = 256
D_FF = 2816
CHUNK = 64
BAND_QBLOCKS = 4
GLA_HEADS_PER_STEP = 4
RMS_EPS = 1e-6
NEG_INF = -1e30
HG_SCALE = HEAD ** -0.5
ATT_SCALE = HEAD ** -0.5
VMEM_LIMIT_V7X = 48 * 1024 * 1024

C_HQ, C_FF, C_FB, C_HI, C_HG = 0, 1024, 2048, 3072, 4096
C_DQ, C_DK, C_DV, C_MQ = 5120, 6656, 8192, 9728
C_GHG, C_GDA, C_GMEM = 10240, 11264, 12288
IN_COLS = 13312

ADAM_LR, ADAM_B1, ADAM_B2, ADAM_EPS, ADAM_WD, ADAM_STEP = 0.001, 0.9, 0.999, 1e-08, 0.01, 10


def _params(sem, vmem=VMEM_LIMIT_V7X):
    return pltpu.CompilerParams(dimension_semantics=sem, vmem_limit_bytes=vmem)


def _dot(a, b):
    return jnp.dot(a.astype(BF16), b.astype(BF16), preferred_element_type=F32)


def _dot_nt(a, b):
    return lax.dot_general(a.astype(BF16), b.astype(BF16), (((1,), (1,)), ((), ())),
                           preferred_element_type=F32)


def _dot_tn(a, b):
    return lax.dot_general(a.astype(BF16), b.astype(BF16), (((0,), (0,)), ((), ())),
                           preferred_element_type=F32)


def _sigmoid(v):
    return jax.nn.sigmoid(v.astype(F32))


def _ones(rows, cols):
    return (lax.broadcasted_iota(jnp.int32, (rows, cols), 0) >= 0).astype(BF16)


def _lane_sum(v):
    ones = _ones(HEAD, HEAD)
    hi = v.astype(BF16)
    mid = (v - hi.astype(F32)).astype(BF16)
    return jnp.dot(hi, ones, preferred_element_type=F32) + jnp.dot(mid, ones, preferred_element_type=F32)


def _row_mean(v):
    if v.shape[-1] == HEAD:
        return _lane_sum(v) * (1.0 / HEAD)
    return jnp.mean(v, axis=-1, keepdims=True)


def _rms(v):
    v = v.astype(F32)
    r = lax.rsqrt(_row_mean(v * v) + RMS_EPS)
    return v * r, r


def _rms_bwd(dy, xhat, r, gain):
    dxh = dy * gain
    dx = r * (dxh - xhat * _row_mean(dxh * xhat))
    return dx, dy * xhat


class _Carry:
    def __init__(self, arrays, out_shapes, aliases, sems, issue):
        self.arrays, self.out_shapes, self.aliases, self.sems, self.issue = arrays, out_shapes, aliases, sems, issue


NO_CARRY = object()


def _start_copies(copies):
    for parts, _, _ in copies:
        for cp in parts:
            cp.start()


def _wait_copies(copies):
    for _, whole, remote in copies:
        if remote:
            whole.wait_recv()
    for _, whole, remote in copies:
        if remote:
            whole.wait_send()
        else:
            whole.wait()


def _call(body, *, name, grid, in_specs, out_specs, out_shape, scratch_shapes, semantics, ins, carry=None,
          aliases=None):
    aliases = dict(aliases or {})
    if carry is None:
        res = pl.pallas_call(body, name=name, grid=grid, in_specs=in_specs, out_specs=out_specs, out_shape=out_shape,
                             scratch_shapes=scratch_shapes, input_output_aliases=aliases,
                             compiler_params=_params(semantics))(*ins)
        return list(res), []
    n_in, n_out, n_scr = len(ins), len(out_shape), len(scratch_shapes)
    c_in, c_out = len(carry.arrays), len(carry.out_shapes)

    def wrapped(*refs):
        pos = [0]

        def take(count):
            pos[0] += count
            return refs[pos[0] - count:pos[0]]

        own_in, carry_in = take(n_in), take(c_in)
        own_out, carry_out = take(n_out), take(c_out)
        own_scr, carry_sems = take(n_scr), take(len(carry.sems))
        ids = [pl.program_id(a) for a in range(len(grid))]
        first, last = ids[0] == 0, ids[0] == grid[0] - 1
        for a in range(1, len(grid)):
            first, last = first & (ids[a] == 0), last & (ids[a] == grid[a] - 1)

        @pl.when(first)
        def _():
            _start_copies(carry.issue(carry_in, carry_out, carry_sems, "start"))

        body(*own_in, *own_out, *own_scr)

        @pl.when(last)
        def _():
            _wait_copies(carry.issue(carry_in, carry_out, carry_sems, "wait"))

    res = pl.pallas_call(
        wrapped, name=name, grid=grid, in_specs=list(in_specs) + [ANY] * c_in,
        out_specs=list(out_specs) + [ANY] * c_out, out_shape=list(out_shape) + list(carry.out_shapes),
        input_output_aliases={**aliases, **{n_in + i: n_out + j for i, j in carry.aliases.items()}},
        scratch_shapes=list(scratch_shapes) + list(carry.sems),
        compiler_params=_params(("arbitrary",) * len(grid)))(*ins, *carry.arrays)
    return list(res[:n_out]), list(res[n_out:])


def _matmul(pairs, mode, out_dtype, *, tm, tn, tk=None, residual=None, also=None, carry=NO_CARRY, name):
    b_offs = [pr[2] if len(pr) > 2 else 0 for pr in pairs]
    pairs = [pr[:2] for pr in pairs]
    a0, b0 = pairs[0]
    if mode == "nn":
        (m, kk), n = a0.shape, b0.shape[1]
    elif mode == "nt":
        (m, kk), n = a0.shape, b0.shape[0]
    else:
        (kk, m), n = a0.shape, b0.shape[1]
    assert mode == "nt" or not any(b_offs)
    tm, tn = min(tm, m), min(tn, n)
    tk = kk if tk is None else tk
    nk = kk // tk
    assert m % tm == 0 and n % tn == 0 and kk % tk == 0, (name, m, n, kk)
    n_p = len(pairs)
    if mode == "tn":
        a_spec = pl.BlockSpec((tk, tm), lambda i, j, k: (k, i))
    else:
        a_spec = pl.BlockSpec((tm, tk), lambda i, j, k: (i, k))
    if mode == "nt":
        b_specs = [pl.BlockSpec((tn, tk), lambda i, j, k, o=o: (j, o * nk + k)) for o in b_offs]
    else:
        b_specs = [pl.BlockSpec((tk, tn), lambda i, j, k: (k, j))] * n_p
    o_spec = pl.BlockSpec((tm, tn), lambda i, j, k: (i, j))
    dot = {"nn": _dot, "nt": _dot_nt, "tn": _dot_tn}[mode]
    has_res = residual is not None

    def body(*refs):
        a_refs, b_refs = refs[:n_p], refs[n_p:2 * n_p]
        pos = 2 * n_p
        res_ref = refs[pos] if has_res else None
        pos += int(has_res)
        o_ref = refs[pos]
        pos += int(also is not None)
        part = dot(a_refs[0][...], b_refs[0][...])
        for a_r, b_r in zip(a_refs[1:], b_refs[1:]):
            part += dot(a_r[...], b_r[...])

        def finish(total):
            if has_res:
                total = total + res_ref[...]
            o_ref[...] = total.astype(out_dtype)
            if also is not None:
                refs[pos][...] = total.astype(also)

        if nk == 1:
            finish(part)
        else:
            acc_ref = refs[pos + 1]
            k = pl.program_id(2)

            @pl.when(k == 0)
            def _():
                acc_ref[...] = part

            @pl.when(k > 0)
            def _():
                acc_ref[...] += part

            @pl.when(k == nk - 1)
            def _():
                finish(acc_ref[...])

    ins = [a for a, _ in pairs] + [b for _, b in pairs]
    in_specs = [a_spec] * n_p + b_specs
    if has_res:
        ins.append(residual)
        in_specs.append(o_spec)
    dtypes = [out_dtype] + ([also] if also is not None else [])
    outs, carried = _call(
        body, name=name, grid=(m // tm, n // tn, nk), in_specs=in_specs, out_specs=[o_spec] * len(dtypes),
        out_shape=[jax.ShapeDtypeStruct((m, n), dt) for dt in dtypes],
        scratch_shapes=[pltpu.VMEM((tm, tn), F32)] if nk > 1 else [],
        semantics=("parallel", "parallel", "arbitrary"), ins=ins, carry=None if carry is NO_CARRY else carry)
    out = outs[0] if also is None else tuple(outs)
    return out if carry is NO_CARRY else (out, carried)


def _rms_fwd(x, gain, *, name):
    rows, dm = x.shape
    tm = min(512, rows)

    def body(x_ref, g_ref, h_ref):
        xhat, _ = _rms(x_ref[...])
        h_ref[...] = (xhat * g_ref[...]).astype(BF16)

    return pl.pallas_call(
        body, name=name, grid=(rows // tm,),
        in_specs=[pl.BlockSpec((tm, dm), lambda i: (i, 0)), pl.BlockSpec((1, dm), lambda i: (0, 0))],
        out_specs=pl.BlockSpec((tm, dm), lambda i: (i, 0)),
        out_shape=jax.ShapeDtypeStruct((rows, dm), BF16),
        compiler_params=_params(("parallel",)),
    )(x, gain)


def _rms_bwd_rows(dh, x, gain, dres, dx_dtypes, *, name):
    rows, dm = x.shape
    tm = min(512, rows)
    has_res = dres is not None

    def body(*refs):
        dh_ref, x_ref, g_ref = refs[:3]
        res_ref = refs[3] if has_res else None
        outs = refs[3 + int(has_res):]
        dg_ref = outs[-1]
        xhat, r = _rms(x_ref[...])
        dx, dgr = _rms_bwd(dh_ref[...], xhat, r, g_ref[...])
        if has_res:
            dx = dx + res_ref[...]
        for dx_ref, dt in zip(outs, dx_dtypes):
            dx_ref[...] = dx.astype(dt)

        @pl.when(pl.program_id(0) == 0)
        def _():
            dg_ref[...] = jnp.zeros_like(dg_ref)

        dg_ref[...] += jnp.sum(dgr, axis=0, keepdims=True)

    row = pl.BlockSpec((tm, dm), lambda i: (i, 0))
    vec = pl.BlockSpec((1, dm), lambda i: (0, 0))
    return pl.pallas_call(
        body, name=name, grid=(rows // tm,),
        in_specs=[row, row, vec] + ([row] if has_res else []),
        out_specs=[row] * len(dx_dtypes) + [vec],
        out_shape=[jax.ShapeDtypeStruct((rows, dm), dt) for dt in dx_dtypes] + [jax.ShapeDtypeStruct((1, dm), F32)],
        compiler_params=_params(("arbitrary",)),
    )(*([dh, x, gain] + ([dres] if has_res else [])))


def _out_norm(a, w, x, gain, *, name):
    rows, dm = x.shape
    tm = min(512, rows)

    def body(a_ref, w_ref, x_ref, g_ref, x1_ref, h_ref):
        x1 = x_ref[...] + _dot(a_ref[...], w_ref[...])
        x1_ref[...] = x1
        xhat, _ = _rms(x1)
        h_ref[...] = (xhat * g_ref[...]).astype(BF16)

    row = lambda wd: pl.BlockSpec((tm, wd), lambda i: (i, 0))
    return pl.pallas_call(
        body, name=name, grid=(rows // tm,),
        in_specs=[row(a.shape[1]), pl.BlockSpec(w.shape, lambda i: (0, 0)), row(dm), pl.BlockSpec((1, dm), lambda i: (0, 0))],
        out_specs=[row(dm), row(dm)],
        out_shape=[jax.ShapeDtypeStruct((rows, dm), F32), jax.ShapeDtypeStruct((rows, dm), BF16)],
        compiler_params=_params(("parallel",)),
    )(a, w, x, gain)


def _out_loss(u, w, x1, tgt, *, name):
    rows, dm = x1.shape
    tm = min(256, rows)
    steps = rows // tm

    def body(u_ref, w_ref, x_ref, t_ref, dy_ref, dyb_ref, loss_ref, acc_ref):
        i = pl.program_id(0)
        diff = x_ref[...] + _dot(u_ref[...], w_ref[...]) - t_ref[...]
        dy = diff * (1.0 / dm)
        dy_ref[...] = dy
        dyb_ref[...] = dy.astype(BF16)

        @pl.when(i == 0)
        def _():
            acc_ref[...] = jnp.zeros_like(acc_ref)

        acc_ref[...] += jnp.sum(diff * diff, axis=0, keepdims=True)

        @pl.when(i == steps - 1)
        def _():
            loss_ref[...] = jnp.full((1, HEAD), 0.5 / dm, F32) * jnp.sum(acc_ref[...])

    row = lambda wd: pl.BlockSpec((tm, wd), lambda i: (i, 0))
    return pl.pallas_call(
        body, name=name, grid=(steps,),
        in_specs=[row(u.shape[1]), pl.BlockSpec(w.shape, lambda i: (0, 0)), row(dm), row(dm)],
        out_specs=[row(dm), row(dm), pl.BlockSpec((1, HEAD), lambda i: (0, 0))],
        out_shape=[jax.ShapeDtypeStruct((rows, dm), F32), jax.ShapeDtypeStruct((rows, dm), BF16),
                   jax.ShapeDtypeStruct((1, HEAD), F32)],
        scratch_shapes=[pltpu.VMEM((1, dm), F32)],
        compiler_params=_params(("arbitrary",)),
    )(u, w, x1, tgt)


def _gla_block_terms(q_raw, f_logit, lb, rev):
    sig = _sigmoid(f_logit)
    forget = lb + (1.0 - lb) * sig
    k = 1.0 - forget
    b = _chunk_cumsum(jnp.log(forget), rev)
    qs = _sigmoid(q_raw)
    eb = jnp.exp(b)
    emb = jnp.exp(-b)
    qt = (q_raw * qs * HG_SCALE) * eb
    kt = k * emb
    return sig, forget, k, b, qs, eb, emb, qt, kt


def _chunk_cumsum(v, rev):
    n = v.shape[0]
    pos = lax.broadcasted_iota(jnp.int32, v.shape, 0) & (CHUNK - 1)
    step = 1
    while step < CHUNK:
        if rev:
            shifted, keep = pltpu.roll(v, n - step, 0), pos < CHUNK - step
        else:
            shifted, keep = pltpu.roll(v, step, 0), pos >= step
        v = v + jnp.where(keep, shifted, 0.0)
        step *= 2
    return v


def _tri_mask(n, rev):
    row, col = np.arange(n)[:, None], np.arange(n)[None, :]
    same = (row // CHUNK) == (col // CHUNK)
    return jnp.asarray((same & ((row <= col) if rev else (row >= col))).astype(np.float32))


def _chunk_order(ncb, rev):
    order = range(ncb - 1, -1, -1) if rev else range(ncb)
    return [(c, c * CHUNK if rev else c * CHUNK + CHUNK - 1) for c in order]


def _gla_fwd(proj, lb, *, f_off, rev, name):
    rows = proj.shape[0]
    tb = min(256, rows)
    nb, ncb = rows // tb, tb // CHUNK

    def tmap(n):
        return nb - 1 - n if rev else n

    def body(tri_ref, q_ref, f_ref, v_ref, lb_ref, o_ref, st_ref, s_ref):
        @pl.when(pl.program_id(1) == 0)
        def _():
            s_ref[...] = jnp.zeros_like(s_ref)

        tri = tri_ref[...] > 0.5
        for hh in range(GLA_HEADS_PER_STEP):
            cs = slice(hh * HEAD, (hh + 1) * HEAD)
            v = v_ref[:, cs]
            _, _, k, b, _, _, _, qt, kt = _gla_block_terms(q_ref[:, cs].astype(F32), f_ref[:, cs].astype(F32),
                                                           lb_ref[:, cs], rev)
            o_intra = _dot(jnp.where(tri, _dot_nt(qt, kt), 0.0), v)
            chunks = []
            for c, last in _chunk_order(ncb, rev):
                sl = slice(c * CHUNK, (c + 1) * CHUNK)
                bl = b[last:last + 1, :]
                kh = k[sl] * jnp.exp(bl - b[sl])
                chunks.append((c, sl, jnp.exp(bl), _dot_tn(v[sl], kh)))
            s_t = s_ref[hh]
            for c, sl, ebl, kv in chunks:
                st_ref[hh, c] = s_t
                o_ref[sl, cs] = o_intra[sl] + _dot_nt(qt[sl], s_t)
                s_t = ebl * s_t + kv
            s_ref[hh] = s_t

    hps = GLA_HEADS_PER_STEP
    col = lambda off: pl.BlockSpec((tb, hps * HEAD), lambda h, n: (tmap(n), off // (hps * HEAD) + h))
    return pl.pallas_call(
        body, name=name, grid=(HG_HEADS // hps, nb),
        in_specs=[pl.BlockSpec((tb, tb), lambda h, n: (0, 0)), col(C_HQ), col(f_off), col(C_HI),
                  pl.BlockSpec((1, hps * HEAD), lambda h, n: (0, h))],
        out_specs=[pl.BlockSpec((tb, hps * HEAD), lambda h, n: (tmap(n), h)),
                   pl.BlockSpec((hps, ncb, HEAD, HEAD), lambda h, n: (h, tmap(n), 0, 0))],
        out_shape=[jax.ShapeDtypeStruct((rows, HG_HEADS * HEAD), F32),
                   jax.ShapeDtypeStruct((HG_HEADS, rows // CHUNK, HEAD, HEAD), F32)],
        scratch_shapes=[pltpu.VMEM((hps, HEAD, HEAD), F32)],
        compiler_params=_params(("parallel", "arbitrary")),
    )(_tri_mask(tb, rev), proj, proj, proj, lb)


def _gla_bwd(proj, lb, do, states, prev, dproj, *, f_off, rev, carry=NO_CARRY, name):
    rows = proj.shape[0]
    tb = min(256, rows)
    nb, ncb = rows // tb, tb // CHUNK
    has_prev = prev is not None
    qv_dtype = BF16 if has_prev else F32

    def tmap(n):
        return n if rev else nb - 1 - n

    def body(*refs):
        tri_ref, q_ref, f_ref, v_ref, lb_ref, do_ref, st_ref = refs[:7]
        pq_ref, pv_ref = refs[8:10] if has_prev else (None, None)
        (dq_ref, df_ref, dv_ref, dlb_ref,
         ds_ref, dqt_scr, dk_scr, db_scr, dbl_scr, dv_scr) = refs[8 + 2 * int(has_prev):]

        @pl.when(pl.program_id(1) == 0)
        def _():
            ds_ref[...] = jnp.zeros_like(ds_ref)
            dlb_ref[...] = jnp.zeros_like(dlb_ref)

        tri = tri_ref[...] > 0.5
        for hh in range(GLA_HEADS_PER_STEP):
            cs = slice(hh * HEAD, (hh + 1) * HEAD)
            lbv = lb_ref[:, cs]
            q_raw = q_ref[:, cs].astype(F32)
            v, dout = v_ref[:, cs], do_ref[:, cs].astype(BF16)
            sig, forget, k, b, qs, eb, emb, qt, kt = _gla_block_terms(q_raw, f_ref[:, cs].astype(F32), lbv, rev)
            a = jnp.where(tri, _dot_nt(qt, kt), 0.0)
            da = jnp.where(tri, _dot_nt(dout, v), 0.0)
            dv_intra = _dot_tn(a, dout)
            dqt_intra = _dot(da, kt)
            dkt = _dot_tn(da, qt)
            chunks = []
            for c, last in reversed(_chunk_order(ncb, rev)):
                sl = slice(c * CHUNK, (c + 1) * CHUNK)
                bl = b[last:last + 1, :]
                e = jnp.exp(bl - b[sl])
                s_t = st_ref[hh, c]
                dqt_scr[sl, cs] = dqt_intra[sl] + _dot(dout[sl], s_t)
                chunks.append((sl, jnp.exp(bl), e, k[sl] * e, s_t, _dot_tn(dout[sl], qt[sl])))
            ds_t = ds_ref[hh]
            for sl, ebl, e, kh, s_t, grow in chunks:
                dkh = _dot(v[sl], ds_t)
                dv_scr[sl, cs] = dv_intra[sl] + _dot_nt(kh, ds_t)
                dk_scr[sl, cs] = dkt[sl] * emb[sl] + dkh * e
                khd = kh * dkh
                dbl = jnp.sum(khd, axis=0, keepdims=True) + ebl * jnp.sum(ds_t * s_t, axis=0, keepdims=True)
                db_scr[sl, cs] = khd
                dbl_scr[sl, cs] = jnp.broadcast_to(dbl, (CHUNK, HEAD))
                ds_t = grow + ds_t * ebl
            ds_ref[hh] = ds_t
            dqt = dqt_scr[:, cs]
            dlogf = _chunk_cumsum(qt * dqt - kt * dkt - db_scr[:, cs], not rev) + dbl_scr[:, cs]
            dforget = dlogf / forget - dk_scr[:, cs]
            df_ref[:, cs] = (dforget * (1.0 - lbv) * sig * (1.0 - sig)).astype(BF16)
            dlb_ref[:, cs] += jnp.sum(dforget * (1.0 - sig), axis=0, keepdims=True)
            dqr = dqt * eb * (HG_SCALE * qs * (1.0 + q_raw * (1.0 - qs)))
            dv = dv_scr[:, cs]
            if has_prev:
                dqr = dqr + pq_ref[:, cs]
                dv = dv + pv_ref[:, cs]
            dq_ref[:, cs] = dqr.astype(qv_dtype)
            dv_ref[:, cs] = dv.astype(qv_dtype)

    hps = GLA_HEADS_PER_STEP
    col = lambda off: pl.BlockSpec((tb, hps * HEAD), lambda h, n: (tmap(n), off // (hps * HEAD) + h))
    blk = pl.BlockSpec((tb, hps * HEAD), lambda h, n: (tmap(n), h))
    vec = pl.BlockSpec((1, hps * HEAD), lambda h, n: (0, h))
    wide = HG_HEADS * HEAD
    into = jax.ShapeDtypeStruct(dproj.shape, dproj.dtype)
    outs, carried = _call(
        body, name=name, grid=(HG_HEADS // hps, nb),
        in_specs=[pl.BlockSpec((tb, tb), lambda h, n: (0, 0)), col(C_HQ), col(f_off), col(C_HI), vec, blk,
                  pl.BlockSpec((hps, ncb, HEAD, HEAD), lambda h, n: (h, tmap(n), 0, 0)), ANY]
                 + ([blk, blk] if has_prev else []),
        out_specs=[col(C_HQ) if has_prev else blk, blk if has_prev else col(f_off), blk, vec],
        out_shape=[into if has_prev else jax.ShapeDtypeStruct((rows, wide), qv_dtype),
                   jax.ShapeDtypeStruct((rows, wide), BF16) if has_prev else into,
                   jax.ShapeDtypeStruct((rows, wide), qv_dtype), jax.ShapeDtypeStruct((1, wide), F32)],
        aliases={7: 0 if has_prev else 1},
        scratch_shapes=[pltpu.VMEM((hps, HEAD, HEAD), F32)] + [pltpu.VMEM((tb, hps * HEAD), F32)] * 5,
        semantics=("parallel", "arbitrary"),
        ins=[_tri_mask(tb, rev), proj, proj, proj, lb, do, states, dproj] + (list(prev) if has_prev else []),
        carry=None if carry is NO_CARRY else carry)
    return outs if carry is NO_CARRY else (outs, carried)


def _fill_columns(dproj, parts, col, *, name):
    rows, wd = parts[0].shape
    tm = min(512, rows)
    n = len(parts)

    def body(*refs):
        o_ref = refs[n + 1]
        for j in range(n):
            o_ref[:, j * wd:(j + 1) * wd] = refs[j][...]

    return pl.pallas_call(
        body, name=name, grid=(rows // tm,),
        in_specs=[pl.BlockSpec((tm, wd), lambda i: (i, 0))] * n + [ANY],
        out_specs=pl.BlockSpec((pl.Element(tm), pl.Element(n * wd)), lambda i: (i * tm, col)),
        out_shape=jax.ShapeDtypeStruct(dproj.shape, dproj.dtype), input_output_aliases={n: 0},
        compiler_params=_params(("parallel",)),
    )(*parts, dproj)


def _hg_out_fwd(o_fw, o_bw, proj, gain, *, name):
    rows = o_fw.shape[0]
    tm = min(512, rows)
    wide = HG_HEADS * HEAD

    def body(a_ref, b_ref, g_ref, gain_ref, o_ref):
        for h in range(HG_HEADS):
            sl = slice(h * HEAD, (h + 1) * HEAD)
            xhat, _ = _rms(a_ref[:, sl] + b_ref[:, sl])
            gate = g_ref[:, sl].astype(F32)
            o_ref[:, sl] = (xhat * gain_ref[...] * (gate * _sigmoid(gate))).astype(BF16)

    row = pl.BlockSpec((tm, wide), lambda i: (i, 0))
    return pl.pallas_call(
        body, name=name, grid=(rows // tm,),
        in_specs=[row, row, pl.BlockSpec((tm, wide), lambda i: (i, C_HG // wide)),
                  pl.BlockSpec((1, HEAD), lambda i: (0, 0))],
        out_specs=row, out_shape=jax.ShapeDtypeStruct((rows, wide), BF16),
        compiler_params=_params(("parallel",)),
    )(o_fw, o_bw, proj, gain)


def _hg_out_bwd(dout, o_fw, o_bw, proj, gain, dproj, *, name):
    rows = o_fw.shape[0]
    tm = min(512, rows)
    wide = HG_HEADS * HEAD

    def body(d_ref, a_ref, b_ref, g_ref, gain_ref, dp_in, dgate_ref, do_ref, dgain_ref):
        del dp_in

        @pl.when(pl.program_id(0) == 0)
        def _():
            dgain_ref[...] = jnp.zeros_like(dgain_ref)

        dgain = jnp.zeros((1, HEAD), F32)
        for h in range(HG_HEADS):
            sl = slice(h * HEAD, (h + 1) * HEAD)
            xhat, r = _rms(a_ref[:, sl] + b_ref[:, sl])
            gate, dy = g_ref[:, sl].astype(F32), d_ref[:, sl]
            gs = _sigmoid(gate)
            dgate_ref[:, sl] = (dy * xhat * gain_ref[...] * (gs * (1.0 + gate * (1.0 - gs)))).astype(BF16)
            dx, dgr = _rms_bwd(dy * (gate * gs), xhat, r, gain_ref[...])
            do_ref[:, sl] = dx
            dgain = dgain + jnp.sum(dgr, axis=0, keepdims=True)
        dgain_ref[...] += dgain

    row = pl.BlockSpec((tm, wide), lambda i: (i, 0))
    gate = pl.BlockSpec((tm, wide), lambda i: (i, C_HG // wide))
    vec = pl.BlockSpec((1, HEAD), lambda i: (0, 0))
    return pl.pallas_call(
        body, name=name, grid=(rows // tm,),
        in_specs=[row, row, row, gate, vec, ANY],
        out_specs=[gate, row, vec],
        out_shape=[jax.ShapeDtypeStruct(dproj.shape, dproj.dtype), jax.ShapeDtypeStruct((rows, wide), F32),
                   jax.ShapeDtypeStruct((1, HEAD), F32)],
        input_output_aliases={5: 0},
        compiler_params=_params(("arbitrary",)),
    )(dout, o_fw, o_bw, proj, gain, dproj)


def _strided_rows(ref, r, count, d):
    return ref[...] if d == 1 else ref[pl.ds(r, count, stride=d), :]


def _store_strided(ref, r, count, d, val):
    if d == 1:
        ref[...] = val
    else:
        ref[pl.ds(r, count, stride=d), :] = val


def _da_prep(proj, q_gain, k_gain, g, *, name):
    d = DA_GROUPS[g][0]
    rows = proj.shape[0]
    rb = min(512, rows)
    tn = rb // d

    def body(q_ref, k_ref, v_ref, qg_ref, kg_ref, qo_ref, ko_ref, vo_ref, qf_ref, kf_ref, vf_ref):
        for h in range(DA_GH):
            cs = slice(h * HEAD, (h + 1) * HEAD)
            for src, dst in ((q_ref, qf_ref), (k_ref, kf_ref), (v_ref, vf_ref)):
                dst[...] = src[:, cs].astype(F32)
            for r in range(d):
                qhat, _ = _rms(_strided_rows(qf_ref, r, tn, d))
                khat, _ = _rms(_strided_rows(kf_ref, r, tn, d))
                qo_ref[h, r] = (qhat * qg_ref[...]).astype(BF16)
                ko_ref[h, r] = (khat * kg_ref[...]).astype(BF16)
                vo_ref[h, r] = _strided_rows(vf_ref, r, tn, d).astype(BF16)

    wide = DA_GH * HEAD
    col = lambda off: pl.BlockSpec((rb, wide), lambda i: (i, off // wide + g))
    vec = pl.BlockSpec((1, HEAD), lambda i: (0, 0))
    out = pl.BlockSpec((DA_GH, d, tn, HEAD), lambda i: (0, 0, i, 0))
    shape = jax.ShapeDtypeStruct((DA_GH, d, rows // d, HEAD), BF16)
    return pl.pallas_call(
        body, name=name, grid=(rows // rb,),
        in_specs=[col(C_DQ), col(C_DK), col(C_DV), vec, vec],
        out_specs=[out, out, out], out_shape=[shape, shape, shape],
        scratch_shapes=[pltpu.VMEM((rb, HEAD), F32)] * 3,
        compiler_params=_params(("parallel",)),
    )(proj, proj, proj, q_gain, k_gain)


def _slopes(g):
    idx = np.arange(g * DA_GH + 1, (g + 1) * DA_GH + 1)
    s = (2.0 ** (-8.0 * idx / (DA_GH * len(DA_GROUPS)))).astype(np.float32)
    return jnp.asarray(np.broadcast_to(s[:, None, None], (DA_GH, 8, HEAD)).copy())


def _band_window(ld, t, radius):
    win = min(2 * t, ld)
    assert t // 2 >= radius or win == ld
    return win


def _band_scores(q, k, q0, start, slope, d, radius):
    t, win = q.shape[0], k.shape[0]
    row = lax.broadcasted_iota(jnp.int32, (t, win), 0)
    col = lax.broadcasted_iota(jnp.int32, (t, win), 1)
    rel = jnp.abs((start - q0) + col - row)
    return _dot_nt(q, k) * ATT_SCALE + jnp.where(rel <= radius, -slope * (d * rel).astype(F32), NEG_INF)


def _band_fwd(qr, kr, vr, g, *, name):
    d, radius = DA_GROUPS[g]
    _, _, ld, _ = qr.shape
    t = min(HEAD, ld)
    qb = min(BAND_QBLOCKS, ld // t)
    rps = max(1, min(d, BAND_QBLOCKS // qb))
    win = _band_window(ld, t, radius)

    def body(q_ref, k_ref, v_ref, sl_ref, o_ref, lse_ref):
        i = pl.program_id(2)
        slope = sl_ref[0:1, 0:1]
        for rr in range(rps):
            for j in range(qb):
                sl = slice(j * t, (j + 1) * t)
                q0 = (i * qb + j) * t
                start = pl.multiple_of(jnp.clip(q0 - t // 2, 0, ld - win), t // 2)
                s = _band_scores(q_ref[rr, sl, :], k_ref[rr, pl.ds(start, win), :], q0, start, slope, d, radius)
                m = jnp.max(s, axis=-1, keepdims=True)
                p = jnp.exp(s - m).astype(BF16)
                l = jnp.dot(p, _ones(win, HEAD), preferred_element_type=F32)
                o_ref[rr, sl, :] = _dot(p, v_ref[rr, pl.ds(start, win), :]) / l
                lse_ref[rr, sl, :] = m + jnp.log(l)

    own = pl.BlockSpec((None, rps, qb * t, HEAD), lambda h, r, i: (h, r, i, 0))
    seq = pl.BlockSpec((None, rps, ld, HEAD), lambda h, r, i: (h, r, 0, 0))
    shape = jax.ShapeDtypeStruct(qr.shape, F32)
    return pl.pallas_call(
        body, name=name, grid=(DA_GH, d // rps, ld // (qb * t)),
        in_specs=[own, seq, seq, pl.BlockSpec((None, 8, HEAD), lambda h, r, i: (h, 0, 0))],
        out_specs=[own, own], out_shape=[shape, shape],
        compiler_params=_params(("parallel", "parallel", "parallel")),
    )(qr, kr, vr, _slopes(g))


def _band_bwd(qr, kr, vr, dor, lser, deltar, g, *, name):
    d, radius = DA_GROUPS[g]
    _, _, ld, _ = qr.shape
    t = min(HEAD, ld)
    qb = min(BAND_QBLOCKS, ld // t)
    rps = max(1, min(d, BAND_QBLOCKS // qb))
    win = _band_window(ld, t, radius)

    def body(q_ref, k_ref, v_ref, do_ref, lse_ref, dl_ref, sl_ref, dq_ref, dk_ref, dv_ref):
        i = pl.program_id(2)

        @pl.when(i == 0)
        def _():
            dk_ref[...] = jnp.zeros_like(dk_ref)
            dv_ref[...] = jnp.zeros_like(dv_ref)

        slope = sl_ref[0:1, 0:1]
        for rr in range(rps):
            for j in range(qb):
                sl = slice(j * t, (j + 1) * t)
                q0 = (i * qb + j) * t
                start = pl.multiple_of(jnp.clip(q0 - t // 2, 0, ld - win), t // 2)
                rows = pl.ds(start, win)
                q, dout, k, v = q_ref[rr, sl, :], do_ref[rr, sl, :], k_ref[rr, rows, :], v_ref[rr, rows, :]
                p = jnp.exp(_band_scores(q, k, q0, start, slope, d, radius) - lse_ref[rr, sl, 0:1])
                ds = p * (_dot_nt(dout, v) - dl_ref[rr, sl, 0:1])
                dq_ref[rr, sl, :] = _dot(ds, k) * ATT_SCALE
                dk_ref[rr, rows, :] += _dot_tn(ds, q) * ATT_SCALE
                dv_ref[rr, rows, :] += _dot_tn(p, dout)

    own = pl.BlockSpec((None, rps, qb * t, HEAD), lambda h, r, i: (h, r, i, 0))
    seq = pl.BlockSpec((None, rps, ld, HEAD), lambda h, r, i: (h, r, 0, 0))
    shape = jax.ShapeDtypeStruct(qr.shape, F32)
    return pl.pallas_call(
        body, name=name, grid=(DA_GH, d // rps, ld // (qb * t)),
        in_specs=[own, seq, seq, own, own, own, pl.BlockSpec((None, 8, HEAD), lambda h, r, i: (h, 0, 0))],
        out_specs=[own, seq, seq], out_shape=[shape, shape, shape],
        compiler_params=_params(("parallel", "parallel", "arbitrary")),
    )(qr, kr, vr, dor, lser, deltar, _slopes(g))


def _da_merge(outs, lses, rows, *, name):
    rb = min(512, rows)
    wide = DA_GH * HEAD

    def body(*refs):
        o_refs, l_refs = refs[0:3], refs[3:6]
        o_ref, lse_ref = refs[6:8]
        on_refs, ln_refs = refs[8:11], refs[11:14]
        for h in range(DA_GH):
            cs = slice(h * HEAD, (h + 1) * HEAD)
            for g, (d, _) in enumerate(DA_GROUPS):
                tn = rb // d
                for r in range(d):
                    _store_strided(on_refs[g], r, tn, d, o_refs[g][h, r])
                    _store_strided(ln_refs[g], r, tn, d, l_refs[g][h, r])
            l0, l1, l2 = ln_refs[0][...], ln_refs[1][...], ln_refs[2][...]
            m = jnp.maximum(jnp.maximum(l0, l1), l2)
            e0, e1, e2 = jnp.exp(l0 - m), jnp.exp(l1 - m), jnp.exp(l2 - m)
            tot = e0 + e1 + e2
            o_ref[:, cs] = (e0 * on_refs[0][...] + e1 * on_refs[1][...] + e2 * on_refs[2][...]) / tot
            lse_ref[:, cs] = m + jnp.log(tot)

    res = lambda d: pl.BlockSpec((DA_GH, d, rb // d, HEAD), lambda i: (0, 0, i, 0))
    nat = pl.BlockSpec((rb, wide), lambda i: (i, 0))
    shape = jax.ShapeDtypeStruct((rows, wide), F32)
    return pl.pallas_call(
        body, name=name, grid=(rows // rb,),
        in_specs=[res(d) for d, _ in DA_GROUPS] * 2,
        out_specs=[nat, nat], out_shape=[shape, shape],
        scratch_shapes=[pltpu.VMEM((rb, HEAD), F32)] * 6,
        compiler_params=_params(("parallel",)),
    )(*outs, *lses)


def _da_bwd_prep(dout, o, lse, *, name):
    rows = o.shape[0]
    rb = min(512, rows)
    wide = DA_GH * HEAD

    def body(d_ref, o_ref, l_ref, *outs):
        d_scr, l_scr, delta_scr = outs[9:12]
        for h in range(DA_GH):
            cs = slice(h * HEAD, (h + 1) * HEAD)
            dv = d_ref[:, cs]
            d_scr[...] = dv
            l_scr[...] = l_ref[:, cs]
            delta_scr[...] = _lane_sum(dv * o_ref[:, cs])
            for g, (d, _) in enumerate(DA_GROUPS):
                tn = rb // d
                for r in range(d):
                    outs[3 * g][h, r] = _strided_rows(d_scr, r, tn, d).astype(BF16)
                    outs[3 * g + 1][h, r] = _strided_rows(l_scr, r, tn, d)
                    outs[3 * g + 2][h, r] = _strided_rows(delta_scr, r, tn, d)

    nat = pl.BlockSpec((rb, wide), lambda i: (i, 0))
    out_specs, out_shape = [], []
    for d, _ in DA_GROUPS:
        for dt in (BF16, F32, F32):
            out_specs.append(pl.BlockSpec((DA_GH, d, rb // d, HEAD), lambda i: (0, 0, i, 0)))
            out_shape.append(jax.ShapeDtypeStruct((DA_GH, d, rows // d, HEAD), dt))
    return pl.pallas_call(
        body, name=name, grid=(rows // rb,),
        in_specs=[nat, nat, nat], out_specs=out_specs, out_shape=out_shape,
        scratch_shapes=[pltpu.VMEM((rb, HEAD), F32)] * 3,
        compiler_params=_params(("parallel",)),
    )(dout, o, lse)


def _da_prep_bwd(dqr, dkr, dvr, proj, q_gain, k_gain, g, *, name):
    d = DA_GROUPS[g][0]
    rows = proj.shape[0]
    rb = min(512, rows)
    tn = rb // d
    wide = DA_GH * HEAD

    def body(dq_ref, dk_ref, dv_ref, q_ref, k_ref, qg_ref, kg_ref, oq_ref, ok_ref, ov_ref, gq_ref, gk_ref, *nat_refs):
        @pl.when(pl.program_id(0) == 0)
        def _():
            gq_ref[...] = jnp.zeros_like(gq_ref)
            gk_ref[...] = jnp.zeros_like(gk_ref)

        for h in range(DA_GH):
            cs = slice(h * HEAD, (h + 1) * HEAD)
            for j, src in enumerate((dq_ref, dk_ref, dv_ref)):
                for r in range(d):
                    _store_strided(nat_refs[j], r, tn, d, src[h, r])
            ov_ref[:, cs] = nat_refs[2][...].astype(BF16)
            for j, (x_ref, gn_ref, out_ref, acc_ref) in enumerate(((q_ref, qg_ref, oq_ref, gq_ref),
                                                                    (k_ref, kg_ref, ok_ref, gk_ref))):
                xhat, r = _rms(x_ref[:, cs])
                dx, dgr = _rms_bwd(nat_refs[j][...], xhat, r, gn_ref[...])
                out_ref[:, cs] = dx.astype(BF16)
                acc_ref[...] += jnp.sum(dgr, axis=0, keepdims=True)

    res = pl.BlockSpec((DA_GH, d, tn, HEAD), lambda i: (0, 0, i, 0))
    col = lambda off: pl.BlockSpec((rb, wide), lambda i: (i, off // wide + g))
    vec = pl.BlockSpec((1, HEAD), lambda i: (0, 0))
    nat = pl.BlockSpec((rb, wide), lambda i: (i, 0))
    shape = jax.ShapeDtypeStruct((rows, wide), BF16)
    gshape = jax.ShapeDtypeStruct((1, HEAD), F32)
    return pl.pallas_call(
        body, name=name, grid=(rows // rb,),
        in_specs=[res, res, res, col(C_DQ), col(C_DK), vec, vec],
        out_specs=[nat, nat, nat, vec, vec], out_shape=[shape, shape, shape, gshape, gshape],
        scratch_shapes=[pltpu.VMEM((rb, HEAD), F32)] * 3,
        compiler_params=_params(("arbitrary",)),
    )(dqr, dkr, dvr, proj, proj, q_gain, k_gain)


def _mem_fwd(proj, kv, q_gain, k_gain, *, name):
    rows = proj.shape[0]
    tm = min(1024, rows)
    n_mem = kv.shape[0]

    def body(q_ref, k_ref, v_ref, qg_ref, kg_ref, o_ref):
        qhat, _ = _rms(q_ref[...])
        khat, _ = _rms(k_ref[...])
        s = _dot_nt(qhat * qg_ref[...], khat * kg_ref[...]) * ATT_SCALE
        p = jnp.exp(s - jnp.max(s, axis=-1, keepdims=True))
        p = p / jnp.sum(p, axis=-1, keepdims=True)
        o_ref[...] = _dot(p, v_ref[...]).astype(BF16)

    vec = pl.BlockSpec((1, HEAD), lambda i, h: (0, 0))
    return pl.pallas_call(
        body, name=name, grid=(rows // tm, MEM_HEADS),
        in_specs=[pl.BlockSpec((tm, HEAD), lambda i, h: (i, C_MQ // HEAD + h)),
                  pl.BlockSpec((n_mem, HEAD), lambda i, h: (0, h)),
                  pl.BlockSpec((n_mem, HEAD), lambda i, h: (0, MEM_HEADS + h)), vec, vec],
        out_specs=pl.BlockSpec((tm, HEAD), lambda i, h: (i, h)),
        out_shape=jax.ShapeDtypeStruct((rows, MEM_HEADS * HEAD), BF16),
        compiler_params=_params(("parallel", "parallel")),
    )(proj, kv, kv, q_gain, k_gain)


def _mem_bwd(dout, proj, kv, q_gain, k_gain, dproj, *, name):
    rows = proj.shape[0]
    tm = min(1024, rows)
    steps = rows // tm
    n_mem = kv.shape[0]

    def body(d_ref, q_ref, k_ref, v_ref, qg_ref, kg_ref, dp_in, dq_ref, dk_ref, dv_ref, gq_ref, gk_ref, dkn_ref):
        del dp_in
        h, i = pl.program_id(0), pl.program_id(1)

        @pl.when((h == 0) & (i == 0))
        def _():
            gq_ref[...] = jnp.zeros_like(gq_ref)
            gk_ref[...] = jnp.zeros_like(gk_ref)

        @pl.when(i == 0)
        def _():
            dkn_ref[...] = jnp.zeros_like(dkn_ref)
            dv_ref[...] = jnp.zeros_like(dv_ref)

        qhat, rq = _rms(q_ref[...])
        khat, rk = _rms(k_ref[...])
        qn, kn = qhat * qg_ref[...], khat * kg_ref[...]
        s = _dot_nt(qn, kn) * ATT_SCALE
        p = jnp.exp(s - jnp.max(s, axis=-1, keepdims=True))
        p = p / jnp.sum(p, axis=-1, keepdims=True)
        dout = d_ref[...]
        dp = _dot_nt(dout, v_ref[...])
        ds = p * (dp - jnp.sum(p * dp, axis=-1, keepdims=True))
        dv_ref[...] += _dot_tn(p, dout)
        dkn_ref[...] += _dot_tn(ds, qn) * ATT_SCALE
        dq, dgr = _rms_bwd(_dot(ds, kn) * ATT_SCALE, qhat, rq, qg_ref[...])
        dq_ref[...] = dq.astype(BF16)
        gq_ref[...] += jnp.sum(dgr, axis=0, keepdims=True)

        @pl.when(i == steps - 1)
        def _():
            dk, dgk = _rms_bwd(dkn_ref[...], khat, rk, kg_ref[...])
            dk_ref[...] = dk
            gk_ref[...] += jnp.sum(dgk, axis=0, keepdims=True)

    vec = pl.BlockSpec((1, HEAD), lambda h, i: (0, 0))
    memh = pl.BlockSpec((n_mem, HEAD), lambda h, i: (0, h))
    tok = pl.BlockSpec((tm, HEAD), lambda h, i: (i, h))
    mq = pl.BlockSpec((tm, HEAD), lambda h, i: (i, C_MQ // HEAD + h))
    gshape = jax.ShapeDtypeStruct((1, HEAD), F32)
    return pl.pallas_call(
        body, name=name, grid=(MEM_HEADS, steps),
        in_specs=[tok, mq, memh, pl.BlockSpec((n_mem, HEAD), lambda h, i: (0, MEM_HEADS + h)), vec, vec, ANY],
        out_specs=[mq, memh, memh, vec, vec],
        out_shape=[jax.ShapeDtypeStruct(dproj.shape, dproj.dtype),
                   jax.ShapeDtypeStruct((n_mem, MEM_HEADS * HEAD), F32),
                   jax.ShapeDtypeStruct((n_mem, MEM_HEADS * HEAD), F32), gshape, gshape],
        input_output_aliases={6: 0},
        scratch_shapes=[pltpu.VMEM((n_mem, HEAD), F32)],
        compiler_params=_params(("arbitrary", "arbitrary")),
    )(dout, proj, kv, kv, q_gain, k_gain, dproj)


def _branch_fwd(o_hg, o_da, o_mem, proj, wp_hg, wp_da, wp_mem, *, name):
    rows = o_hg.shape[0]
    tm = min(256, rows)

    def body(a_ref, b_ref, c_ref, ga_ref, gb_ref, gc_ref, wa_ref, wb_ref, wc_ref, o_ref):
        merged = _sigmoid(ga_ref[...]) * _dot(a_ref[...], wa_ref[...])
        merged += _sigmoid(gb_ref[...]) * _dot(b_ref[...], wb_ref[...])
        merged += _sigmoid(gc_ref[...]) * _dot(c_ref[...], wc_ref[...])
        o_ref[...] = merged.astype(BF16)

    row = lambda w: pl.BlockSpec((tm, w), lambda i: (i, 0))
    gate = lambda off: pl.BlockSpec((tm, D_MODEL), lambda i: (i, off // D_MODEL))
    full = lambda a: pl.BlockSpec(a.shape, lambda i: (0, 0))
    return pl.pallas_call(
        body, name=name, grid=(rows // tm,),
        in_specs=[row(o_hg.shape[1]), row(o_da.shape[1]), row(o_mem.shape[1]),
                  gate(C_GHG), gate(C_GDA), gate(C_GMEM), full(wp_hg), full(wp_da), full(wp_mem)],
        out_specs=row(D_MODEL), out_shape=jax.ShapeDtypeStruct((rows, D_MODEL), BF16),
        compiler_params=_params(("parallel",)),
    )(o_hg, o_da, o_mem, proj, proj, proj, wp_hg, wp_da, wp_mem)


def _branch_bwd(dm, o_hg, o_da, o_mem, proj, wp_hg, wp_da, wp_mem, *, name):
    rows = o_hg.shape[0]
    tm = min(256, rows)

    def body(dm_ref, a_ref, b_ref, c_ref, ga_ref, gb_ref, gc_ref, wa_ref, wb_ref, wc_ref, dp_ref, *outs):
        dmv = dm_ref[...]
        for j, (o_ref, g_ref, w_ref) in enumerate(((a_ref, ga_ref, wa_ref), (b_ref, gb_ref, wb_ref),
                                                   (c_ref, gc_ref, wc_ref))):
            z = _dot(o_ref[...], w_ref[...])
            gs = _sigmoid(g_ref[...])
            dz = (dmv * gs).astype(BF16)
            dp_ref[:, j * D_MODEL:(j + 1) * D_MODEL] = (dmv * z * gs * (1.0 - gs)).astype(BF16)
            outs[2 * j][...] = dz
            outs[2 * j + 1][...] = _dot_nt(dz, w_ref[...])

    row = lambda w: pl.BlockSpec((tm, w), lambda i: (i, 0))
    gate = lambda off: pl.BlockSpec((tm, D_MODEL), lambda i: (i, off // D_MODEL))
    full = lambda a: pl.BlockSpec(a.shape, lambda i: (0, 0))
    out_specs = [pl.BlockSpec((pl.Element(tm), pl.Element(3 * D_MODEL)), lambda i: (i * tm, C_GHG))]
    out_shape = [jax.ShapeDtypeStruct((rows, IN_COLS), BF16)]
    for o in (o_hg, o_da, o_mem):
        out_specs += [row(D_MODEL), row(o.shape[1])]
        out_shape += [jax.ShapeDtypeStruct((rows, D_MODEL), BF16), jax.ShapeDtypeStruct((rows, o.shape[1]), F32)]
    return pl.pallas_call(
        body, name=name, grid=(rows // tm,),
        in_specs=[row(D_MODEL), row(o_hg.shape[1]), row(o_da.shape[1]), row(o_mem.shape[1]),
                  gate(C_GHG), gate(C_GDA), gate(C_GMEM), full(wp_hg), full(wp_da), full(wp_mem)],
        out_specs=out_specs, out_shape=out_shape,
        compiler_params=_params(("parallel",)),
    )(dm, o_hg, o_da, o_mem, proj, proj, proj, wp_hg, wp_da, wp_mem)


def _ffn_in(h2, w_ab, *, name):
    rows, dff = h2.shape[0], w_ab.shape[1] // 2
    tm, tn = min(2048, rows), 256

    def body(h_ref, wa_ref, wb_ref, a_ref, b_ref, u_ref):
        a = _dot(h_ref[...], wa_ref[...])
        b = _dot(h_ref[...], wb_ref[...])
        a_ref[...] = a.astype(BF16)
        b_ref[...] = b.astype(BF16)
        u_ref[...] = (a * _sigmoid(a) * b).astype(BF16)

    out = pl.BlockSpec((tm, tn), lambda i, j: (i, j))
    return pl.pallas_call(
        body, name=name, grid=(rows // tm, dff // tn),
        in_specs=[pl.BlockSpec((tm, D_MODEL), lambda i, j: (i, 0)),
                  pl.BlockSpec((D_MODEL, tn), lambda i, j: (0, j)),
                  pl.BlockSpec((D_MODEL, tn), lambda i, j: (0, dff // tn + j))],
        out_specs=[out, out, out],
        out_shape=[jax.ShapeDtypeStruct((rows, dff), BF16)] * 3,
        compiler_params=_params(("parallel", "parallel")),
    )(h2, w_ab, w_ab)


def _ffn_act_bwd(dy, w_out, a, b, *, name):
    rows, dff = a.shape
    tm, tn = min(1024, rows), 256

    def body(dy_ref, w_ref, a_ref, b_ref, da_ref, db_ref):
        du = _dot_nt(dy_ref[...], w_ref[...])
        av, bv = a_ref[...].astype(F32), b_ref[...].astype(F32)
        sa = _sigmoid(av)
        da_ref[...] = (du * bv * sa * (1.0 + av * (1.0 - sa))).astype(BF16)
        db_ref[...] = (du * av * sa).astype(BF16)

    tile = pl.BlockSpec((tm, tn), lambda i, j: (i, j))
    return pl.pallas_call(
        body, name=name, grid=(rows // tm, dff // tn),
        in_specs=[pl.BlockSpec((tm, D_MODEL), lambda i, j: (i, 0)),
                  pl.BlockSpec((tn, D_MODEL), lambda i, j: (j, 0)), tile, tile],
        out_specs=[tile, tile],
        out_shape=[jax.ShapeDtypeStruct((rows, dff), BF16), jax.ShapeDtypeStruct((rows, dff), BF16)],
        compiler_params=_params(("parallel", "parallel")),
    )(dy, w_out, a, b)


def _lower_bound(lb_fw, lb_bw, *, name):
    def body(a_ref, b_ref, oa_ref, ob_ref):
        for src, dst in ((a_ref, oa_ref), (b_ref, ob_ref)):
            dst[...] = _sigmoid(src[0:1, :] - src[1:2, :])

    shape = jax.ShapeDtypeStruct((1, lb_fw.shape[1]), F32)
    return pl.pallas_call(body, name=name, out_shape=[shape, shape])(lb_fw, lb_bw)


def _local_step(x, mem, tgt, p, ex):
    rows = x.shape[0]
    lb_fw, lb_bw = _lower_bound(p["lb_logits_fw"], p["lb_logits_bw"], name="lower_bound")

    h = _rms_fwd(x, p["norm_mix_gain"], name="norm_mix")
    w = {"w_in": ex.w_in}
    proj, carried = _matmul([(h, w["w_in"])], "nn", BF16, tm=1024, tn=1024, carry=ex.late_carry(), name="proj_in")
    w.update(ex.late_weights(carried))
    o_fw, st_fw = _gla_fwd(proj, lb_fw, f_off=C_FF, rev=False, name="gla_fwd_fw")
    o_bw, st_bw = _gla_fwd(proj, lb_bw, f_off=C_FB, rev=True, name="gla_fwd_bw")
    o_hg = _hg_out_fwd(o_fw, o_bw, proj, p["hg_norm_gain"], name="hg_out")

    qkv_r, outs, lses = [], [], []
    for g in range(len(DA_GROUPS)):
        qr, kr, vr = _da_prep(proj, p["da_q_gain"], p["da_k_gain"], g, name=f"da_prep{g}")
        og, lg = _band_fwd(qr, kr, vr, g, name=f"band_fwd{g}")
        qkv_r.append((qr, kr, vr))
        outs.append(og)
        lses.append(lg)
    o_da, lse_da = _da_merge(outs, lses, rows, name="da_merge")

    mem_n = _rms_fwd(mem, p["norm_mem_gain"], name="norm_mem")
    kv = _matmul([(mem_n, w["w_mem_kv"])], "nn", F32, tm=256, tn=512, name="mem_kv")
    o_mem = _mem_fwd(proj, kv, p["mem_q_gain"], p["mem_k_gain"], name="mem_attn")

    merged = _branch_fwd(o_hg, o_da, o_mem, proj, w["w_proj_hg"], w["w_proj_da"], w["w_proj_mem"],
                         name="branch_merge")
    x1, h2 = _out_norm(merged, w["w_out"], x, p["norm_ffn_gain"], name="mix_out_norm")
    a, b, u = _ffn_in(h2, w["w_ffn_in"], name="ffn_in")
    dy, dy_b, loss = _out_loss(u, w["w_ffn_out"], x1, tgt, name="ffn_out_loss")

    gw, gwb, gs = {}, {}, {}
    gw["w_ffn_out"], gwb["w_ffn_out"] = _matmul([(u, dy_b)], "tn", F32, tm=256, tn=1024, also=BF16, name="g_ffn_out")
    da, db = _ffn_act_bwd(dy_b, w["w_ffn_out"], a, b, name="ffn_act_bwd")
    gw["w_ffn_a"], gwb["w_ffn_a"] = _matmul([(h2, da)], "tn", F32, tm=512, tn=256, also=BF16, name="g_ffn_a")
    gw["w_ffn_b"], gwb["w_ffn_b"] = _matmul([(h2, db)], "tn", F32, tm=512, tn=256, also=BF16, name="g_ffn_b")
    dh2 = _matmul([(da, w["w_ffn_in"], 0), (db, w["w_ffn_in"], 1)], "nt", F32, tm=256, tn=1024, name="d_h2")
    dx1, dx1_b, gs["norm_ffn_gain"] = _rms_bwd_rows(dh2, x1, p["norm_ffn_gain"], dy, (F32, BF16), name="norm_ffn_bwd")
    gw["w_out"], gwb["w_out"] = _matmul([(merged, dx1_b)], "tn", F32, tm=512, tn=512, also=BF16, name="g_out")
    dmerged = _matmul([(dx1_b, w["w_out"])], "nt", F32, tm=512, tn=512, name="d_merged")
    dproj, dz_hg, do_hg, dz_da, do_da, dz_mem, do_mem = _branch_bwd(
        dmerged, o_hg, o_da, o_mem, proj, w["w_proj_hg"], w["w_proj_da"], w["w_proj_mem"], name="branch_bwd")
    gw["w_proj_hg"], gwb["w_proj_hg"] = _matmul([(o_hg, dz_hg)], "tn", F32, tm=512, tn=512, also=BF16, name="g_proj_hg")
    gw["w_proj_da"], gwb["w_proj_da"] = _matmul([(o_da, dz_da)], "tn", F32, tm=512, tn=512, also=BF16, name="g_proj_da")
    gw["w_proj_mem"], gwb["w_proj_mem"] = _matmul([(o_mem, dz_mem)], "tn", F32, tm=512, tn=512, also=BF16, name="g_proj_mem")

    dproj, dk_mem, dv_mem, gs["mem_q_gain"], gs["mem_k_gain"] = _mem_bwd(
        do_mem, proj, kv, p["mem_q_gain"], p["mem_k_gain"], dproj, name="mem_attn_bwd")
    dkv = jnp.concatenate([dk_mem, dv_mem], axis=1).astype(BF16)
    gw["w_mem_kv"], gwb["w_mem_kv"] = _matmul([(mem_n, dkv)], "tn", F32, tm=512, tn=512, also=BF16, name="g_mem_kv")
    dmem_n = _matmul([(dkv, w["w_mem_kv"])], "nt", F32, tm=256, tn=512, name="d_mem_n")
    (gs["norm_mem_gain"],) = _rms_bwd_rows(dmem_n, mem, p["norm_mem_gain"], None, (), name="norm_mem_bwd")

    prep = _da_bwd_prep(do_da, o_da, lse_da, name="da_bwd_prep")
    d_da, gq_parts, gk_parts = [], [], []
    for g in range(len(DA_GROUPS)):
        qr, kr, vr = qkv_r[g]
        dor, lser, deltar = prep[3 * g:3 * g + 3]
        dqr, dkr, dvr = _band_bwd(qr, kr, vr, dor, lser, deltar, g, name=f"band_bwd{g}")
        dq, dk, dv, gq, gk = _da_prep_bwd(dqr, dkr, dvr, proj, p["da_q_gain"], p["da_k_gain"], g,
                                          name=f"da_prep_bwd{g}")
        d_da.append((dq, dk, dv))
        gq_parts.append(gq)
        gk_parts.append(gk)
    for j, (off, tag) in enumerate(((C_DQ, "q"), (C_DK, "k"), (C_DV, "v"))):
        dproj = _fill_columns(dproj, [t[j] for t in d_da], off, name=f"dproj_fill_da_{tag}")

    dproj, do_gla, gs["hg_norm_gain"] = _hg_out_bwd(do_hg, o_fw, o_bw, proj, p["hg_norm_gain"], dproj,
                                                    name="hg_out_bwd")
    (dq_f, dproj, dv_f, dlb_fw), carried = _gla_bwd(proj, lb_fw, do_gla, st_fw, None, dproj, f_off=C_FF, rev=False,
                                                    carry=ex.early_carry(gw, gwb), name="gla_bwd_fw")
    ex.early_done(carried)
    dproj, dfl_bw, dv_hg, dlb_bw = _gla_bwd(proj, lb_bw, do_gla, st_bw, (dq_f, dv_f), dproj, f_off=C_FB, rev=True,
                                            name="gla_bwd_bw")
    dproj = _fill_columns(dproj, [dfl_bw, dv_hg], C_FB, name="dproj_fill")
    g_in, g_in_b = _matmul([(h, dproj)], "tn", F32, tm=512, tn=1024, also=BF16, name="g_in")
    dh, carried = _matmul([(dproj, w["w_in"])], "nt", F32, tm=1024, tn=1024, tk=IN_COLS // 8,
                          carry=ex.w_in_carry(g_in, g_in_b), name="d_h")
    ex.w_in_done(carried)
    grad_x, gs["norm_mix_gain"] = _rms_bwd_rows(dh, x, p["norm_mix_gain"], dx1, (F32,), name="norm_mix_bwd")

    small = _small_pack(gs, gq_parts, gk_parts, dlb_fw, dlb_bw, lb_fw, lb_bw, name="small_pack")
    return loss, grad_x, small


def _small_pack(gs, gq_parts, gk_parts, dlb_fw, dlb_bw, lb_fw, lb_bw, *, name):
    def body(g_mix, g_mem, g_ffn, dfw, dbw, lfw, lbw, g_hg, q0, q1, q2, k0, k1, k2, g_mq, g_mk, o_ref):
        o_ref[0:1, :] = g_mix[...]
        o_ref[1:2, :] = g_mem[...]
        o_ref[2:3, :] = g_ffn[...]
        for base, d_ref, l_ref in ((3, dfw, lfw), (5, dbw, lbw)):
            lbv = l_ref[...]
            t = d_ref[...] * lbv * (1.0 - lbv)
            o_ref[base:base + 1, :] = t
            o_ref[base + 1:base + 2, :] = -t
        o_ref[7:8, :] = jnp.zeros((1, D_MODEL), F32)
        o_ref[7:8, 0:HEAD] = g_hg[...]
        o_ref[7:8, HEAD:2 * HEAD] = q0[...] + q1[...] + q2[...]
        o_ref[7:8, 2 * HEAD:3 * HEAD] = k0[...] + k1[...] + k2[...]
        o_ref[7:8, 3 * HEAD:4 * HEAD] = g_mq[...]
        o_ref[7:8, 4 * HEAD:5 * HEAD] = g_mk[...]

    return pl.pallas_call(body, name=name, out_shape=jax.ShapeDtypeStruct((8, D_MODEL), F32))(
        gs["norm_mix_gain"], gs["norm_mem_gain"], gs["norm_ffn_gain"], dlb_fw, dlb_bw, lb_fw, lb_bw,
        gs["hg_norm_gain"], *gq_parts, *gk_parts, gs["mem_q_gain"], gs["mem_k_gain"])


def _row_tile(rows, cols, n_arrays):
    budget = (16 * 1024 * 1024) // (2 * 4 * cols * n_arrays)
    tr = rows
    while tr > budget and tr % 2 == 0 and (tr // 2) % 16 == 0:
        tr //= 2
    return tr


def _cast_into_full(a, chip, rows, cols, axis, *, name):
    sr, sc = a.shape
    tr = _row_tile(sr, sc, 2)

    def body(chip_ref, a_ref, o_ref):
        del chip_ref
        o_ref[...] = a_ref[...].astype(BF16)

    if axis == 1:
        out_map = lambda i, chip_ref: (i, chip_ref[0])
    else:
        out_map = lambda i, chip_ref: (chip_ref[0] * (sr // tr) + i, 0)
    return pl.pallas_call(
        body, name=name,
        grid_spec=pltpu.PrefetchScalarGridSpec(
            num_scalar_prefetch=1, grid=(sr // tr,),
            in_specs=[pl.BlockSpec((tr, sc), lambda i, chip_ref: (i, 0))],
            out_specs=pl.BlockSpec((tr, sc), out_map)),
        out_shape=jax.ShapeDtypeStruct((rows, cols), BF16),
        compiler_params=_params(("parallel",)))(chip, a)


def _add_halves(items, *, name):
    rows = items[0][3].shape[0]
    widths = [ra.shape[1] for _, _, _, ra in items]
    tr = _row_tile(rows, sum(widths), 4)

    def body(*refs):
        o_ref = refs[-1]
        first = lax.axis_index("c") == 0
        off = 0
        for j, wd in enumerate(widths):
            h0, h1, ra = refs[3 * j:3 * j + 3]
            o_ref[:, off:off + wd] = (jnp.where(first, h0[...], h1[...]) + ra[...]).astype(BF16)
            off += wd

    in_specs, ins = [], []
    for (g, haxis, hsize, ra), wd in zip(items, widths):
        if haxis == 0:
            in_specs += [pl.BlockSpec((tr, wd), lambda i: (i, 0)),
                         pl.BlockSpec((tr, wd), lambda i, o=hsize // tr: (o + i, 0))]
        else:
            in_specs += [pl.BlockSpec((tr, wd), lambda i: (i, 0)), pl.BlockSpec((tr, wd), lambda i: (i, 1))]
        in_specs.append(pl.BlockSpec((tr, wd), lambda i: (i, 0)))
        ins += [g, g, ra]
    return pl.pallas_call(body, name=name, grid=(rows // tr,), in_specs=in_specs,
                          out_specs=pl.BlockSpec((tr, sum(widths)), lambda i: (i, 0)),
                          out_shape=jax.ShapeDtypeStruct((rows, sum(widths)), BF16),
                          compiler_params=_params(("parallel",)))(*ins)


def _add_slots(rb, *, name):
    _, rows, cols = rb.shape
    tr = _row_tile(rows, cols, 5)

    def body(r0, r1, r2, r3, o_ref):
        o_ref[...] = ((r0[...].astype(F32) + r1[...].astype(F32)) + r2[...].astype(F32)) + r3[...].astype(F32)

    slot = lambda s: pl.BlockSpec((None, tr, cols), lambda i: (s, i, 0))
    return pl.pallas_call(body, name=name, grid=(rows // tr,), in_specs=[slot(s) for s in range(4)],
                          out_specs=pl.BlockSpec((tr, cols), lambda i: (i, 0)),
                          out_shape=jax.ShapeDtypeStruct((rows, cols), F32),
                          compiler_params=_params(("parallel",)))(rb, rb, rb, rb)


def _adamw(w, g, m, v, *, name):
    rows, cols = w.shape
    tr = _row_tile(rows, cols, 7) if rows % 16 == 0 else rows
    c1 = 1.0 - ADAM_B1 ** ADAM_STEP
    c2 = 1.0 - ADAM_B2 ** ADAM_STEP

    def body(w_ref, g_ref, m_ref, v_ref, d_ref, mo_ref, vo_ref):
        gv = g_ref[...]
        mn = ADAM_B1 * m_ref[...] + (1.0 - ADAM_B1) * gv
        vn = ADAM_B2 * v_ref[...] + (1.0 - ADAM_B2) * (gv * gv)
        mo_ref[...] = mn
        vo_ref[...] = vn
        d_ref[...] = -ADAM_LR * ((mn / c1) / (jnp.sqrt(vn / c2) + ADAM_EPS) + ADAM_WD * w_ref[...])

    spec = pl.BlockSpec((tr, cols), lambda i: (i, 0))
    shape = jax.ShapeDtypeStruct((rows, cols), F32)
    return pl.pallas_call(body, name=name, grid=(rows // tr,), in_specs=[spec] * 4, out_specs=[spec] * 3,
                          out_shape=[shape] * 3, compiler_params=_params(("parallel",)))(w, g, m, v)


W_SPECS = (
    ("w_in", 1024, IN_COLS, 1, IN_COLS // 4),
    ("w_mem_kv", 1024, 1024, 0, 256),
    ("w_proj_hg", 1024, 1024, 0, 256),
    ("w_proj_da", 512, 1024, 1, 256),
    ("w_proj_mem", 512, 1024, 1, 256),
    ("w_out", 1024, 1024, 0, 256),
    ("w_ffn_in", 1024, 2 * D_FF, 1, 2 * D_FF // 4),
    ("w_ffn_out", D_FF, 1024, 0, D_FF // 4),
)
CHIP_FLIPS = ((1, 0), (0, 1), (1, 1))
ANY = pl.BlockSpec(memory_space=pl.ANY)
DMA_CHUNK_BYTES = 1 << 20
STAGE_BYTES = 2 << 20


def _place():
    x, y, c = lax.axis_index("x"), lax.axis_index("y"), lax.axis_index("c")
    return x, y, c, 2 * x + y


def _flip(v, f):
    return 1 - v if f else v


def _slab(ref, axis, idx, size):
    start = pl.multiple_of(idx * size, size)
    return ref.at[pl.ds(start, size), :] if axis == 0 else ref.at[:, pl.ds(start, size)]


def _chunked(make, src, dst, want="both"):
    rows, cols = src.shape
    row_bytes = cols * jnp.dtype(src.dtype).itemsize
    k = 1
    while rows % (2 * k) == 0 and (rows // (2 * k)) % 16 == 0 and (rows // k) * row_bytes > DMA_CHUNK_BYTES:
        k *= 2
    cr = rows // k
    parts = []
    if want != "wait":
        parts = [make(src.at[pl.ds(j * cr, cr), :], dst.at[pl.ds(j * cr, cr), :]) for j in range(k)]
    return parts, (make(src, dst) if want != "start" else None)


def _half_spec(rows, cols, axis):
    return (0, rows // 2) if axis == 1 else (1, cols // 2)


def _staged(src, remote_dst, local_dst, sibling, load_sems, send_sems, store_sems, recv_sem):
    rows, cols = src.shape
    row_bytes = cols * jnp.dtype(src.dtype).itemsize
    k = 1
    while (rows // k) * row_bytes > STAGE_BYTES and rows % (2 * k) == 0 and (rows // (2 * k)) % 16 == 0:
        k *= 2
    cr = rows // k
    piece = lambda ref, j: ref.at[pl.ds(j * cr, cr), :]

    def run(buf):
        loads = [pltpu.make_async_copy(piece(src, j), buf.at[j % 2], load_sems.at[j % 2]) for j in range(k)]
        outs = [[pltpu.make_async_remote_copy(src_ref=buf.at[j % 2], dst_ref=piece(remote_dst, j),
                                              send_sem=send_sems.at[j % 2], recv_sem=recv_sem,
                                              device_id=sibling, device_id_type=MESH)] for j in range(k)]
        if local_dst is not None:
            for j in range(k):
                outs[j].append(pltpu.make_async_copy(buf.at[j % 2], piece(local_dst, j), store_sems.at[j % 2]))

        def drained(j):
            outs[j][0].wait_send()
            for cp in outs[j][1:]:
                cp.wait()

        loads[0].start()
        for j in range(k):
            loads[j].wait()
            for cp in outs[j]:
                cp.start()
            if j + 1 < k:
                if j >= 1:
                    drained(j - 1)
                loads[j + 1].start()
        for j in range(max(0, k - 2), k):
            drained(j)

    pl.run_scoped(run, pltpu.VMEM((2, cr, cols), src.dtype))


def _landed(ref, recv_sem, send_sem):
    pltpu.make_async_remote_copy(src_ref=ref, dst_ref=ref, send_sem=send_sem, recv_sem=recv_sem,
                                 device_id=(lax.axis_index("x"), lax.axis_index("y"), lax.axis_index("c")),
                                 device_id_type=MESH).wait_recv()


def _half_slab(ref, spec, chip, half):
    _, rows, cols, axis, size = spec
    haxis, hsize = _half_spec(rows, cols, axis)
    return _slab(_slab(ref, axis, chip, size), haxis, half, hsize)


def _gather_sends(outs, specs, ici_send, ici_recv, want="both"):
    x, y, c, p = _place()
    sent = []
    for wi, spec in enumerate(specs):
        mine = _half_slab(outs[wi], spec, p, c)
        for k, (fx, fy) in enumerate(CHIP_FLIPS):
            sent.append(_chunked(lambda s, d, j=3 * wi + k, fx=fx, fy=fy: pltpu.make_async_remote_copy(
                src_ref=s, dst_ref=d, send_sem=ici_send.at[j], recv_sem=ici_recv.at[j],
                device_id=(_flip(x, fx), _flip(y, fy), c), device_id_type=MESH), mine, mine, want))
    return sent


def _gather_weights(fulls, specs, *, name):
    n = len(specs)

    def body(*refs):
        outs = refs[n:2 * n]
        ici_send, ici_recv, load_sems, d2d_send, d2d_recv = refs[2 * n:]
        x, y, c, _ = _place()
        sent = _gather_sends(outs, specs, ici_send, ici_recv)
        for parts, _ in sent:
            for cp in parts:
                cp.start()
        for k, (fx, fy) in enumerate(CHIP_FLIPS):
            q = 2 * _flip(x, fx) + _flip(y, fy)
            for wi in range(n):
                sent[3 * wi + k][1].wait_recv()
                got = _half_slab(outs[wi], specs[wi], q, c)
                _staged(got, got, None, (x, y, 1 - c), load_sems, d2d_send, None, d2d_recv.at[3 * wi + k])
        for _, whole in sent:
            whole.wait_send()
        for k, (fx, fy) in enumerate(CHIP_FLIPS):
            q = 2 * _flip(x, fx) + _flip(y, fy)
            for wi in range(n):
                _landed(_half_slab(outs[wi], specs[wi], q, 1 - c), d2d_recv.at[3 * wi + k], d2d_send.at[0])

    return pl.pallas_call(
        body, name=name, in_specs=[ANY] * n, out_specs=[ANY] * n,
        out_shape=[jax.ShapeDtypeStruct(f.shape, f.dtype) for f in fulls],
        input_output_aliases={i: i for i in range(n)},
        scratch_shapes=[pltpu.SemaphoreType.DMA((3 * n,)), pltpu.SemaphoreType.DMA((3 * n,)),
                        pltpu.SemaphoreType.DMA((2,)), pltpu.SemaphoreType.DMA((2,)),
                        pltpu.SemaphoreType.DMA((3 * n,))],
    )(*fulls)


def _gather_chips_carry(fulls, specs):
    n = len(specs)

    def issue(ins, outs, sems, want):
        del ins
        return [(parts, whole, True) for parts, whole in _gather_sends(outs, specs, sems[0], sems[1], want)]

    return _Carry(list(fulls), [jax.ShapeDtypeStruct(f.shape, f.dtype) for f in fulls], {i: i for i in range(n)},
                  [pltpu.SemaphoreType.DMA((3 * n,)), pltpu.SemaphoreType.DMA((3 * n,))], issue)


def _gather_pass_on(fulls, specs, *, name):
    n = len(specs)

    def body(*refs):
        outs = refs[n:2 * n]
        load_sems, d2d_send, d2d_recv = refs[2 * n:2 * n + 3]
        bufs = refs[2 * n + 3:]
        x, y, c, _ = _place()
        loads, sends = [], []
        for k, (fx, fy) in enumerate(CHIP_FLIPS):
            q = 2 * _flip(x, fx) + _flip(y, fy)
            for wi in range(n):
                j = 3 * wi + k
                got = _half_slab(outs[wi], specs[wi], q, c)
                loads.append(pltpu.make_async_copy(got, bufs[j], load_sems.at[j]))
                sends.append(pltpu.make_async_remote_copy(
                    src_ref=bufs[j], dst_ref=got, send_sem=d2d_send.at[j], recv_sem=d2d_recv.at[j],
                    device_id=(x, y, 1 - c), device_id_type=MESH))
        for cp in loads:
            cp.start()
        for load, send in zip(loads, sends):
            load.wait()
            send.start()
        for cp in sends:
            cp.wait_send()
        for k, (fx, fy) in enumerate(CHIP_FLIPS):
            q = 2 * _flip(x, fx) + _flip(y, fy)
            for wi in range(n):
                _landed(_half_slab(outs[wi], specs[wi], q, 1 - c), d2d_recv.at[3 * wi + k], d2d_send.at[0])

    shapes = []
    for _, rows, cols, axis, size in specs:
        shapes += [(rows // 2, size) if axis == 1 else (size, cols // 2)] * 3
    assert sum(math.prod(s) for s in shapes) * 2 <= VMEM_LIMIT_V7X // 2
    stages = [pltpu.VMEM(s, BF16) for s in shapes]
    return pl.pallas_call(
        body, name=name, in_specs=[ANY] * n, out_specs=[ANY] * n,
        out_shape=[jax.ShapeDtypeStruct(f.shape, f.dtype) for f in fulls],
        input_output_aliases={i: i for i in range(n)},
        scratch_shapes=[pltpu.SemaphoreType.DMA((3 * n,))] * 3 + stages,
        compiler_params=pltpu.CompilerParams(vmem_limit_bytes=VMEM_LIMIT_V7X),
    )(*fulls)


def _sibling_exchange(grads, *, name):
    n = len(grads)

    def body(*refs):
        ins, outs = refs[:n], refs[n:2 * n]
        load_sems, send_sems, recv_sems = refs[2 * n:]
        x, y, c, _ = _place()
        for i, (_, haxis, hsize) in enumerate(grads):
            _staged(_slab(ins[i], haxis, 1 - c, hsize), outs[i], None, (x, y, 1 - c),
                    load_sems, send_sems, None, recv_sems.at[i])
        for i in range(n):
            _landed(outs[i], recv_sems.at[i], send_sems.at[0])

    shapes = [jax.ShapeDtypeStruct((hsize, g.shape[1]) if haxis == 0 else (g.shape[0], hsize), g.dtype)
              for g, haxis, hsize in grads]
    return pl.pallas_call(
        body, name=name, in_specs=[ANY] * n, out_specs=[ANY] * n, out_shape=shapes,
        scratch_shapes=[pltpu.SemaphoreType.DMA((2,)), pltpu.SemaphoreType.DMA((2,)),
                        pltpu.SemaphoreType.DMA((n,))],
    )(*[g for g, _, _ in grads])


def _chip_exchange_carry(parts):
    n = len(parts)

    def issue(ins, outs, sems, want):
        send_sems, recv_sems, local_sems = sems
        x, y, c, p = _place()
        copies = []
        for i, (_, axis, size) in enumerate(parts):
            copies.append(_chunked(lambda s, d, i=i: pltpu.make_async_copy(s, d, local_sems.at[i]),
                                   _slab(ins[i], axis, p, size), outs[i].at[p], want) + (False,))
            for k, (fx, fy) in enumerate(CHIP_FLIPS):
                px, py = _flip(x, fx), _flip(y, fy)
                copies.append(_chunked(lambda s, d, j=3 * i + k, px=px, py=py: pltpu.make_async_remote_copy(
                    src_ref=s, dst_ref=d, send_sem=send_sems.at[j], recv_sem=recv_sems.at[j],
                    device_id=(px, py, c), device_id_type=MESH),
                    _slab(ins[i], axis, 2 * px + py, size), outs[i].at[p], want) + (True,))
        return copies

    shapes = []
    for a, axis, size in parts:
        shapes.append(jax.ShapeDtypeStruct((4, size, a.shape[1]) if axis == 0 else (4, a.shape[0], size), a.dtype))
    return _Carry([a for a, _, _ in parts], shapes, {},
                  [pltpu.SemaphoreType.DMA((3 * n,)), pltpu.SemaphoreType.DMA((3 * n,)),
                   pltpu.SemaphoreType.DMA((n,))], issue)


def _sibling_share(sums):
    n = len(sums)

    def body(*refs):
        ins, outs = refs[:n], refs[n:2 * n]
        load_sems, send_sems, store_sems, recv_sems = refs[2 * n:]
        x, y, c, _ = _place()
        for i, (s, haxis) in enumerate(sums):
            place = _slab(outs[i], haxis, c, s.shape[haxis])
            _staged(ins[i], place, place, (x, y, 1 - c), load_sems, send_sems, store_sems, recv_sems.at[i])
        for i, (s, haxis) in enumerate(sums):
            _landed(_slab(outs[i], haxis, 1 - c, s.shape[haxis]), recv_sems.at[i], send_sems.at[0])

    shapes = []
    for s, haxis in sums:
        r, cc = s.shape
        shapes.append(jax.ShapeDtypeStruct((2 * r, cc) if haxis == 0 else (r, 2 * cc), F32))
    return pl.pallas_call(
        body, name="grad_sibling_share", in_specs=[ANY] * n, out_specs=[ANY] * n, out_shape=shapes,
        scratch_shapes=[pltpu.SemaphoreType.DMA((2,)), pltpu.SemaphoreType.DMA((2,)),
                        pltpu.SemaphoreType.DMA((2,)), pltpu.SemaphoreType.DMA((n,))],
    )(*[s for s, _ in sums])


class _Exchanges:
    def __init__(self, fulls):
        self.w_in = _gather_weights([fulls["w_in"]], W_SPECS[:1], name="gather_w_in")[0]
        self.late_specs = W_SPECS[1:]
        self.late = [fulls[s[0]] for s in self.late_specs]
        self.slots = {}

    def late_carry(self):
        return _gather_chips_carry(self.late, self.late_specs)

    def late_weights(self, carried):
        done = _gather_pass_on(carried, self.late_specs, name="gather_pass_on")
        return dict(zip([s[0] for s in self.late_specs], done))

    def _half_sums(self, gw, gwb, specs, tag):
        grads, sent = [], []
        for name, _, _, axis, _ in specs:
            for part in (("w_ffn_a", "w_ffn_b") if name == "w_ffn_in" else (name,)):
                half = _half_spec(gw[part].shape[0], gw[part].shape[1], axis)
                grads.append((gw[part],) + half)
                sent.append((gwb[part],) + half)
        theirs = _sibling_exchange(sent, name=f"grad_sibling_exchange_{tag}")
        parts, j = [], 0
        for name, _, _, axis, size in specs:
            take = 2 if name == "w_ffn_in" else 1
            items = [grads[i] + (theirs[i],) for i in range(j, j + take)]
            parts.append((_add_halves(items, name=f"half_sum_{name}"), axis, size))
            j += take
        return _chip_exchange_carry(parts)

    def early_carry(self, gw, gwb):
        return self._half_sums(gw, gwb, self.late_specs, "early")

    def early_done(self, carried):
        self.slots.update(zip([s[0] for s in self.late_specs], carried))

    def w_in_carry(self, g, gb):
        return self._half_sums({"w_in": g}, {"w_in": gb}, W_SPECS[:1], "w_in")

    def w_in_done(self, carried):
        self.slots["w_in"] = carried[0]

    def reduced(self):
        sums = []
        for name, rows, cols, axis, _ in W_SPECS:
            sums.append((_add_slots(self.slots[name], name=f"chip_sum_{name}"), _half_spec(rows, cols, axis)[0]))
        return dict(zip([s[0] for s in W_SPECS], _sibling_share(sums)))


def _small_allreduce(sv):
    def body(sv_ref, o_ref, slots_ref, send_sems, recv_sems):
        x, y, c, _ = _place()
        me = 4 * x + 2 * y + c
        slots_ref[me] = sv_ref[...]
        copies = []
        for k in range(1, 8):
            fx, fy, fc = (k >> 2) & 1, (k >> 1) & 1, k & 1
            copies.append(pltpu.make_async_remote_copy(
                src_ref=sv_ref, dst_ref=slots_ref.at[me], send_sem=send_sems.at[k - 1],
                recv_sem=recv_sems.at[k - 1], device_id=(_flip(x, fx), _flip(y, fy), _flip(c, fc)),
                device_id_type=MESH))
        for cp in copies:
            cp.start()
        for cp in copies:
            cp.wait_recv()
        for cp in copies:
            cp.wait_send()
        total = slots_ref[0]
        for s in range(1, 8):
            total = total + slots_ref[s]
        o_ref[...] = total

    vm = pl.BlockSpec(memory_space=pltpu.VMEM)
    return pl.pallas_call(
        body, name="small_allreduce", in_specs=[vm], out_specs=vm,
        out_shape=jax.ShapeDtypeStruct(sv.shape, F32),
        scratch_shapes=[pltpu.VMEM((8,) + sv.shape, F32), pltpu.SemaphoreType.DMA((7,)),
                        pltpu.SemaphoreType.DMA((7,))],
    )(sv)


SMALL_ROWS = (("norm_mix_gain", 0), ("norm_mem_gain", 1), ("norm_ffn_gain", 2))
SMALL_LB = (("lb_logits_fw", 3), ("lb_logits_bw", 5))
SMALL_HEAD = ("hg_norm_gain", "da_q_gain", "da_k_gain", "mem_q_gain", "mem_k_gain")


def _pack_small(d):
    last = jnp.concatenate([d[n] for n in SMALL_HEAD] + [jnp.zeros((1, D_MODEL - HEAD * len(SMALL_HEAD)), F32)], axis=1)
    return jnp.concatenate([d["norm_mix_gain"], d["norm_mem_gain"], d["norm_ffn_gain"],
                            d["lb_logits_fw"], d["lb_logits_bw"], last], axis=0)


def _unpack_small(a):
    out = {n: a[r:r + 1] for n, r in SMALL_ROWS}
    out.update({n: a[r:r + 2] for n, r in SMALL_LB})
    out.update({n: a[7:8, j * HEAD:(j + 1) * HEAD] for j, n in enumerate(SMALL_HEAD)})
    return out


PARAM_ORDER = ("norm_mix_gain", "norm_mem_gain", "w_in", "lb_logits_fw", "lb_logits_bw", "hg_norm_gain",
               "da_q_gain", "da_k_gain", "w_mem_kv", "mem_q_gain", "mem_k_gain", "w_proj_hg", "w_proj_da",
               "w_proj_mem", "w_out", "norm_ffn_gain", "w_ffn_in", "w_ffn_out")


def kernel(x, mem, norm_mix_gain, norm_mem_gain, w_in, lb_logits_fw, lb_logits_bw, hg_norm_gain, da_q_gain, da_k_gain, w_mem_kv, mem_q_gain, mem_k_gain, w_proj_hg, w_proj_da, w_proj_mem, w_out, norm_ffn_gain, w_ffn_in, w_ffn_out, loss_target, m_norm_mix_gain, m_norm_mem_gain, m_w_in, m_lb_logits_fw, m_lb_logits_bw, m_hg_norm_gain, m_da_q_gain, m_da_k_gain, m_w_mem_kv, m_mem_q_gain, m_mem_k_gain, m_w_proj_hg, m_w_proj_da, m_w_proj_mem, m_w_out, m_norm_ffn_gain, m_w_ffn_in, m_w_ffn_out, v_norm_mix_gain, v_norm_mem_gain, v_w_in, v_lb_logits_fw, v_lb_logits_bw, v_hg_norm_gain, v_da_q_gain, v_da_k_gain, v_w_mem_kv, v_mem_q_gain, v_mem_k_gain, v_w_proj_hg, v_w_proj_da, v_w_proj_mem, v_w_out, v_norm_ffn_gain, v_w_ffn_in, v_w_ffn_out):
    args = dict(locals())
    mats = tuple(s[0] for s in W_SPECS)
    flat = lambda a: a.reshape(a.shape[-2:])
    w = {n: flat(args[n]) for n in mats}
    m = {n: flat(args["m_" + n]) for n in mats}
    v = {n: flat(args["v_" + n]) for n in mats}
    small = {n: args[n] for n in PARAM_ORDER if n not in mats}

    chip = (2 * lax.axis_index("x") + lax.axis_index("y")).astype(jnp.int32).reshape(1)
    ex = _Exchanges({n: _cast_into_full(w[n], chip, rows, cols, axis, name=f"cast_{n}")
                     for n, rows, cols, axis, _ in W_SPECS})
    loss, grad_x, small_grads = _local_step(x[0], mem[0], loss_target[0], small, ex)
    grads = ex.reduced()
    small_sum = _small_allreduce(small_grads)

    delta, new_m, new_v = {}, {}, {}
    for n in mats:
        delta[n], new_m[n], new_v[n] = _adamw(w[n], grads[n], m[n], v[n], name=f"adamw_{n}")
    packed = _adamw(_pack_small(small), small_sum,
                    _pack_small({n: args["m_" + n] for n in small}),
                    _pack_small({n: args["v_" + n] for n in small}), name="adamw_small")
    grads.update(_unpack_small(small_sum))
    for dst, src in zip((delta, new_m, new_v), packed):
        dst.update(_unpack_small(src))

    def shaped(d, n):
        return d[n].reshape(args[n].shape)

    loss_sum = lax.psum(loss[0, 0], ("x", "y", "c"))
    return (loss_sum, grad_x[None], *[shaped(grads, n) for n in PARAM_ORDER], *[shaped(delta, n) for n in PARAM_ORDER],
            *[shaped(new_m, n) for n in PARAM_ORDER], *[shaped(new_v, n) for n in PARAM_ORDER])
```

```python
import functools
import math

import numpy as np
import jax
import jax.numpy as jnp
from jax import lax
from jax.experimental import pallas as pl
from jax.experimental.pallas import tpu as pltpu

F32 = jnp.float32
BF16 = jnp.bfloat16
MESH = pl.DeviceIdType.MESH

D_MODEL = 1024
HEAD = 128
HG_HEADS = 8
DA_GROUPS = ((1, 64), (4, 64), (16, 64))
DA_GH = 4
MEM_HEADS = 4
N_MEM = 256
D_FF = 2816
CHUNK = 64
BAND_QBLOCKS = 8
GLA_HEADS_PER_STEP = 4
RMS_EPS = 1e-6
NEG_INF = -1e30
HG_SCALE = HEAD ** -0.5
ATT_SCALE = HEAD ** -0.5
VMEM_LIMIT_V7X = 48 * 1024 * 1024

C_HQ, C_FF, C_FB, C_HI, C_HG = 0, 1024, 2048, 3072, 4096
C_DQ, C_DK, C_DV, C_MQ = 5120, 6656, 8192, 9728
C_GHG, C_GDA, C_GMEM = 10240, 11264, 12288
IN_COLS = 13312

ADAM_LR, ADAM_B1, ADAM_B2, ADAM_EPS, ADAM_WD, ADAM_STEP = 0.001, 0.9, 0.999, 1e-08, 0.01, 10


def _params(sem, vmem=VMEM_LIMIT_V7X):
    return pltpu.CompilerParams(dimension_semantics=sem, vmem_limit_bytes=vmem)


def _dot(a, b):
    return jnp.dot(a.astype(BF16), b.astype(BF16), preferred_element_type=F32)


def _dot_nt(a, b):
    return lax.dot_general(a.astype(BF16), b.astype(BF16), (((1,), (1,)), ((), ())),
                           preferred_element_type=F32)


def _dot_tn(a, b):
    return lax.dot_general(a.astype(BF16), b.astype(BF16), (((0,), (0,)), ((), ())),
                           preferred_element_type=F32)


def _sigmoid(v):
    return jax.nn.sigmoid(v.astype(F32))


def _ones(rows, cols):
    return (lax.broadcasted_iota(jnp.int32, (rows, cols), 0) >= 0).astype(BF16)


def _lane_sum(v):
    ones = _ones(HEAD, HEAD)
    hi = v.astype(BF16)
    mid = (v - hi.astype(F32)).astype(BF16)
    return jnp.dot(hi, ones, preferred_element_type=F32) + jnp.dot(mid, ones, preferred_element_type=F32)


def _row_mean(v):
    if v.shape[-1] == HEAD:
        return _lane_sum(v) * (1.0 / HEAD)
    return jnp.mean(v, axis=-1, keepdims=True)


def _rms(v):
    v = v.astype(F32)
    r = lax.rsqrt(_row_mean(v * v) + RMS_EPS)
    return v * r, r


def _rms_bwd(dy, xhat, r, gain):
    dxh = dy * gain
    dx = r * (dxh - xhat * _row_mean(dxh * xhat))
    return dx, dy * xhat


class _Carry:
    def __init__(self, arrays, out_shapes, aliases, sems, issue):
        self.arrays, self.out_shapes, self.aliases, self.sems, self.issue = arrays, out_shapes, aliases, sems, issue


NO_CARRY = object()


def _start_copies(copies):
    for parts, _, _ in copies:
        for cp in parts:
            cp.start()


def _wait_copies(copies):
    for _, whole, remote in copies:
        if remote:
            whole.wait_recv()
    for _, whole, remote in copies:
        if remote:
            whole.wait_send()
        else:
            whole.wait()


def _call(body, *, name, grid, in_specs, out_specs, out_shape, scratch_shapes, semantics, ins, carry=None,
          aliases=None):
    aliases = dict(aliases or {})
    if carry is None:
        res = pl.pallas_call(body, name=name, grid=grid, in_specs=in_specs, out_specs=out_specs, out_shape=out_shape,
                             scratch_shapes=scratch_shapes, input_output_aliases=aliases,
                             compiler_params=_params(semantics))(*ins)
        return list(res), []
    n_in, n_out, n_scr = len(ins), len(out_shape), len(scratch_shapes)
    c_in, c_out = len(carry.arrays), len(carry.out_shapes)

    def wrapped(*refs):
        pos = [0]

        def take(count):
            pos[0] += count
            return refs[pos[0] - count:pos[0]]

        own_in, carry_in = take(n_in), take(c_in)
        own_out, carry_out = take(n_out), take(c_out)
        own_scr, carry_sems = take(n_scr), take(len(carry.sems))
        ids = [pl.program_id(a) for a in range(len(grid))]
        first, last = ids[0] == 0, ids[0] == grid[0] - 1
        for a in range(1, len(grid)):
            first, last = first & (ids[a] == 0), last & (ids[a] == grid[a] - 1)

        @pl.when(first)
        def _():
            _start_copies(carry.issue(carry_in, carry_out, carry_sems, "start"))

        body(*own_in, *own_out, *own_scr)

        @pl.when(last)
        def _():
            _wait_copies(carry.issue(carry_in, carry_out, carry_sems, "wait"))

    res = pl.pallas_call(
        wrapped, name=name, grid=grid, in_specs=list(in_specs) + [ANY] * c_in,
        out_specs=list(out_specs) + [ANY] * c_out, out_shape=list(out_shape) + list(carry.out_shapes),
        input_output_aliases={**aliases, **{n_in + i: n_out + j for i, j in carry.aliases.items()}},
        scratch_shapes=list(scratch_shapes) + list(carry.sems),
        compiler_params=_params(("arbitrary",) * len(grid)))(*ins, *carry.arrays)
    return list(res[:n_out]), list(res[n_out:])


def _matmul(pairs, mode, out_dtype, *, tm, tn, tk=None, residual=None, also=None, carry=NO_CARRY, name):
    b_offs = [pr[2] if len(pr) > 2 else 0 for pr in pairs]
    pairs = [pr[:2] for pr in pairs]
    a0, b0 = pairs[0]
    if mode == "nn":
        (m, kk), n = a0.shape, b0.shape[1]
    elif mode == "nt":
        (m, kk), n = a0.shape, b0.shape[0]
    else:
        (kk, m), n = a0.shape, b0.shape[1]
    assert mode == "nt" or not any(b_offs)
    tm, tn = min(tm, m), min(tn, n)
    tk = kk if tk is None else tk
    nk = kk // tk
    assert m % tm == 0 and n % tn == 0 and kk % tk == 0, (name, m, n, kk)
    n_p = len(pairs)
    if mode == "tn":
        a_spec = pl.BlockSpec((tk, tm), lambda i, j, k: (k, i))
    else:
        a_spec = pl.BlockSpec((tm, tk), lambda i, j, k: (i, k))
    if mode == "nt":
        b_specs = [pl.BlockSpec((tn, tk), lambda i, j, k, o=o: (j, o * nk + k)) for o in b_offs]
    else:
        b_specs = [pl.BlockSpec((tk, tn), lambda i, j, k: (k, j))] * n_p
    o_spec = pl.BlockSpec((tm, tn), lambda i, j, k: (i, j))
    dot = {"nn": _dot, "nt": _dot_nt, "tn": _dot_tn}[mode]
    has_res = residual is not None

    def body(*refs):
        a_refs, b_refs = refs[:n_p], refs[n_p:2 * n_p]
        pos = 2 * n_p
        res_ref = refs[pos] if has_res else None
        pos += int(has_res)
        o_ref = refs[pos]
        pos += int(also is not None)
        part = dot(a_refs[0][...], b_refs[0][...])
        for a_r, b_r in zip(a_refs[1:], b_refs[1:]):
            part += dot(a_r[...], b_r[...])

        def finish(total):
            if has_res:
                total = total + res_ref[...]
            o_ref[...] = total.astype(out_dtype)
            if also is not None:
                refs[pos][...] = total.astype(also)

        if nk == 1:
            finish(part)
        else:
            acc_ref = refs[pos + 1]
            k = pl.program_id(2)

            @pl.when(k == 0)
            def _():
                acc_ref[...] = part

            @pl.when(k > 0)
            def _():
                acc_ref[...] += part

            @pl.when(k == nk - 1)
            def _():
                finish(acc_ref[...])

    ins = [a for a, _ in pairs] + [b for _, b in pairs]
    in_specs = [a_spec] * n_p + b_specs
    if has_res:
        ins.append(residual)
        in_specs.append(o_spec)
    dtypes = [out_dtype] + ([also] if also is not None else [])
    outs, carried = _call(
        body, name=name, grid=(m // tm, n // tn, nk), in_specs=in_specs, out_specs=[o_spec] * len(dtypes),
        out_shape=[jax.ShapeDtypeStruct((m, n), dt) for dt in dtypes],
        scratch_shapes=[pltpu.VMEM((tm, tn), F32)] if nk > 1 else [],
        semantics=("parallel", "parallel", "arbitrary"), ins=ins, carry=None if carry is NO_CARRY else carry)
    out = outs[0] if also is None else tuple(outs)
    return out if carry is NO_CARRY else (out, carried)


def _rms_fwd(x, gain, *, name):
    rows, dm = x.shape
    tm = min(1024, rows)

    def body(x_ref, g_ref, h_ref):
        xhat, _ = _rms(x_ref[...])
        h_ref[...] = (xhat * g_ref[...]).astype(BF16)

    return pl.pallas_call(
        body, name=name, grid=(rows // tm,),
        in_specs=[pl.BlockSpec((tm, dm), lambda i: (i, 0)), pl.BlockSpec((1, dm), lambda i: (0, 0))],
        out_specs=pl.BlockSpec((tm, dm), lambda i: (i, 0)),
        out_shape=jax.ShapeDtypeStruct((rows, dm), BF16),
        compiler_params=_params(("parallel",)),
    )(x, gain)


def _rms_bwd_rows(dh, x, gain, dres, dx_dtypes, *, name):
    rows, dm = x.shape
    tm = min(512, rows)
    has_res = dres is not None

    def body(*refs):
        dh_ref, x_ref, g_ref = refs[:3]
        res_ref = refs[3] if has_res else None
        outs = refs[3 + int(has_res):]
        dg_ref = outs[-1]
        xhat, r = _rms(x_ref[...])
        dx, dgr = _rms_bwd(dh_ref[...], xhat, r, g_ref[...])
        if has_res:
            dx = dx + res_ref[...]
        for dx_ref, dt in zip(outs, dx_dtypes):
            dx_ref[...] = dx.astype(dt)

        @pl.when(pl.program_id(0) == 0)
        def _():
            dg_ref[...] = jnp.zeros_like(dg_ref)

        dg_ref[...] += jnp.sum(dgr, axis=0, keepdims=True)

    row = pl.BlockSpec((tm, dm), lambda i: (i, 0))
    vec = pl.BlockSpec((1, dm), lambda i: (0, 0))
    return pl.pallas_call(
        body, name=name, grid=(rows // tm,),
        in_specs=[row, row, vec] + ([row] if has_res else []),
        out_specs=[row] * len(dx_dtypes) + [vec],
        out_shape=[jax.ShapeDtypeStruct((rows, dm), dt) for dt in dx_dtypes] + [jax.ShapeDtypeStruct((1, dm), F32)],
        compiler_params=_params(("arbitrary",)),
    )(*([dh, x, gain] + ([dres] if has_res else [])))


def _out_norm(a, w, x, gain, *, name):
    rows, dm = x.shape
    tm = min(512, rows)

    def body(a_ref, w_ref, x_ref, g_ref, x1_ref, h_ref):
        x1 = x_ref[...] + _dot(a_ref[...], w_ref[...])
        x1_ref[...] = x1
        xhat, _ = _rms(x1)
        h_ref[...] = (xhat * g_ref[...]).astype(BF16)

    row = lambda wd: pl.BlockSpec((tm, wd), lambda i: (i, 0))
    return pl.pallas_call(
        body, name=name, grid=(rows // tm,),
        in_specs=[row(a.shape[1]), pl.BlockSpec(w.shape, lambda i: (0, 0)), row(dm), pl.BlockSpec((1, dm), lambda i: (0, 0))],
        out_specs=[row(dm), row(dm)],
        out_shape=[jax.ShapeDtypeStruct((rows, dm), F32), jax.ShapeDtypeStruct((rows, dm), BF16)],
        compiler_params=_params(("parallel",)),
    )(a, w, x, gain)


def _out_loss(u, w, x1, tgt, *, name):
    rows, dm = x1.shape
    tm = min(256, rows)
    steps = rows // tm

    def body(u_ref, w_ref, x_ref, t_ref, dy_ref, dyb_ref, loss_ref, acc_ref):
        i = pl.program_id(0)
        diff = x_ref[...] + _dot(u_ref[...], w_ref[...]) - t_ref[...]
        dy = diff * (1.0 / dm)
        dy_ref[...] = dy
        dyb_ref[...] = dy.astype(BF16)

        @pl.when(i == 0)
        def _():
            acc_ref[...] = jnp.zeros_like(acc_ref)

        acc_ref[...] += jnp.sum(diff * diff, axis=0, keepdims=True)

        @pl.when(i == steps - 1)
        def _():
            loss_ref[...] = jnp.full((1, HEAD), 0.5 / dm, F32) * jnp.sum(acc_ref[...])

    row = lambda wd: pl.BlockSpec((tm, wd), lambda i: (i, 0))
    return pl.pallas_call(
        body, name=name, grid=(steps,),
        in_specs=[row(u.shape[1]), pl.BlockSpec(w.shape, lambda i: (0, 0)), row(dm), row(dm)],
        out_specs=[row(dm), row(dm), pl.BlockSpec((1, HEAD), lambda i: (0, 0))],
        out_shape=[jax.ShapeDtypeStruct((rows, dm), F32), jax.ShapeDtypeStruct((rows, dm), BF16),
                   jax.ShapeDtypeStruct((1, HEAD), F32)],
        scratch_shapes=[pltpu.VMEM((1, dm), F32)],
        compiler_params=_params(("arbitrary",)),
    )(u, w, x1, tgt)


def _gla_block_terms(q_raw, f_logit, lb, rev):
    sig = _sigmoid(f_logit)
    forget = lb + (1.0 - lb) * sig
    k = 1.0 - forget
    b = _chunk_cumsum(jnp.log(forget), rev)
    qs = _sigmoid(q_raw)
    eb = jnp.exp(b)
    emb = jnp.exp(-b)
    qt = (q_raw * qs * HG_SCALE) * eb
    kt = k * emb
    return sig, forget, k, b, qs, eb, emb, qt, kt


def _chunk_cumsum(v, rev):
    n = v.shape[0]
    pos = lax.broadcasted_iota(jnp.int32, v.shape, 0) & (CHUNK - 1)
    step = 1
    while step < CHUNK:
        if rev:
            shifted, keep = pltpu.roll(v, n - step, 0), pos < CHUNK - step
        else:
            shifted, keep = pltpu.roll(v, step, 0), pos >= step
        v = v + jnp.where(keep, shifted, 0.0)
        step *= 2
    return v


def _tri_mask(n, rev):
    row, col = np.arange(n)[:, None], np.arange(n)[None, :]
    same = (row // CHUNK) == (col // CHUNK)
    return jnp.asarray((same & ((row <= col) if rev else (row >= col))).astype(np.float32))


def _chunk_order(ncb, rev):
    order = range(ncb - 1, -1, -1) if rev else range(ncb)
    return [(c, c * CHUNK if rev else c * CHUNK + CHUNK - 1) for c in order]


def _gla_fwd(proj, lb, *, f_off, rev, name):
    rows = proj.shape[0]
    tb = min(256, rows)
    nb, ncb = rows // tb, tb // CHUNK

    def tmap(n):
        return nb - 1 - n if rev else n

    def body(tri_ref, q_ref, f_ref, v_ref, lb_ref, o_ref, st_ref, s_ref):
        @pl.when(pl.program_id(1) == 0)
        def _():
            s_ref[...] = jnp.zeros_like(s_ref)

        tri = tri_ref[...] > 0.5
        for hh in range(GLA_HEADS_PER_STEP):
            cs = slice(hh * HEAD, (hh + 1) * HEAD)
            v = v_ref[:, cs]
            _, _, k, b, _, _, _, qt, kt = _gla_block_terms(q_ref[:, cs].astype(F32), f_ref[:, cs].astype(F32),
                                                           lb_ref[:, cs], rev)
            o_intra = _dot(jnp.where(tri, _dot_nt(qt, kt), 0.0), v)
            chunks = []
            for c, last in _chunk_order(ncb, rev):
                sl = slice(c * CHUNK, (c + 1) * CHUNK)
                bl = b[last:last + 1, :]
                kh = k[sl] * jnp.exp(bl - b[sl])
                chunks.append((c, sl, jnp.exp(bl), _dot_tn(v[sl], kh)))
            s_t = s_ref[hh]
            for c, sl, ebl, kv in chunks:
                st_ref[hh, c] = s_t
                o_ref[sl, cs] = o_intra[sl] + _dot_nt(qt[sl], s_t)
                s_t = ebl * s_t + kv
            s_ref[hh] = s_t

    hps = GLA_HEADS_PER_STEP
    col = lambda off: pl.BlockSpec((tb, hps * HEAD), lambda h, n: (tmap(n), off // (hps * HEAD) + h))
    return pl.pallas_call(
        body, name=name, grid=(HG_HEADS // hps, nb),
        in_specs=[pl.BlockSpec((tb, tb), lambda h, n: (0, 0)), col(C_HQ), col(f_off), col(C_HI),
                  pl.BlockSpec((1, hps * HEAD), lambda h, n: (0, h))],
        out_specs=[pl.BlockSpec((tb, hps * HEAD), lambda h, n: (tmap(n), h)),
                   pl.BlockSpec((hps, ncb, HEAD, HEAD), lambda h, n: (h, tmap(n), 0, 0))],
        out_shape=[jax.ShapeDtypeStruct((rows, HG_HEADS * HEAD), F32),
                   jax.ShapeDtypeStruct((HG_HEADS, rows // CHUNK, HEAD, HEAD), F32)],
        scratch_shapes=[pltpu.VMEM((hps, HEAD, HEAD), F32)],
        compiler_params=_params(("parallel", "arbitrary")),
    )(_tri_mask(tb, rev), proj, proj, proj, lb)


def _gla_bwd(proj, lb, do, states, prev, dproj, *, f_off, rev, carry=NO_CARRY, name):
    rows = proj.shape[0]
    tb = min(256, rows)
    nb, ncb = rows // tb, tb // CHUNK
    has_prev = prev is not None
    qv_dtype = BF16 if has_prev else F32

    def tmap(n):
        return n if rev else nb - 1 - n

    def body(*refs):
        tri_ref, q_ref, f_ref, v_ref, lb_ref, do_ref, st_ref = refs[:7]
        pq_ref, pv_ref = refs[8:10] if has_prev else (None, None)
        (dq_ref, df_ref, dv_ref, dlb_ref,
         ds_ref, dqt_scr, dk_scr, db_scr, dbl_scr, dv_scr) = refs[8 + 2 * int(has_prev):]

        @pl.when(pl.program_id(1) == 0)
        def _():
            ds_ref[...] = jnp.zeros_like(ds_ref)
            dlb_ref[...] = jnp.zeros_like(dlb_ref)

        tri = tri_ref[...] > 0.5
        for hh in range(GLA_HEADS_PER_STEP):
            cs = slice(hh * HEAD, (hh + 1) * HEAD)
            lbv = lb_ref[:, cs]
            q_raw = q_ref[:, cs].astype(F32)
            v, dout = v_ref[:, cs], do_ref[:, cs].astype(BF16)
            sig, forget, k, b, qs, eb, emb, qt, kt = _gla_block_terms(q_raw, f_ref[:, cs].astype(F32), lbv, rev)
            a = jnp.where(tri, _dot_nt(qt, kt), 0.0)
            da = jnp.where(tri, _dot_nt(dout, v), 0.0)
            dv_intra = _dot_tn(a, dout)
            dqt_intra = _dot(da, kt)
            dkt = _dot_tn(da, qt)
            chunks = []
            for c, last in reversed(_chunk_order(ncb, rev)):
                sl = slice(c * CHUNK, (c + 1) * CHUNK)
                bl = b[last:last + 1, :]
                e = jnp.exp(bl - b[sl])
                s_t = st_ref[hh, c]
                dqt_scr[sl, cs] = dqt_intra[sl] + _dot(dout[sl], s_t)
                chunks.append((sl, jnp.exp(bl), e, k[sl] * e, s_t, _dot_tn(dout[sl], qt[sl])))
            ds_t = ds_ref[hh]
            for sl, ebl, e, kh, s_t, grow in chunks:
                dkh = _dot(v[sl], ds_t)
                dv_scr[sl, cs] = dv_intra[sl] + _dot_nt(kh, ds_t)
                dk_scr[sl, cs] = dkt[sl] * emb[sl] + dkh * e
                khd = kh * dkh
                dbl = jnp.sum(khd, axis=0, keepdims=True) + ebl * jnp.sum(ds_t * s_t, axis=0, keepdims=True)
                db_scr[sl, cs] = khd
                dbl_scr[sl, cs] = jnp.broadcast_to(dbl, (CHUNK, HEAD))
                ds_t = grow + ds_t * ebl
            ds_ref[hh] = ds_t
            dqt = dqt_scr[:, cs]
            dlogf = _chunk_cumsum(qt * dqt - kt * dkt - db_scr[:, cs], not rev) + dbl_scr[:, cs]
            dforget = dlogf / forget - dk_scr[:, cs]
            df_ref[:, cs] = (dforget * (1.0 - lbv) * sig * (1.0 - sig)).astype(BF16)
            dlb_ref[:, cs] += jnp.sum(dforget * (1.0 - sig), axis=0, keepdims=True)
            dqr = dqt * eb * (HG_SCALE * qs * (1.0 + q_raw * (1.0 - qs)))
            dv = dv_scr[:, cs]
            if has_prev:
                dqr = dqr + pq_ref[:, cs]
                dv = dv + pv_ref[:, cs]
            dq_ref[:, cs] = dqr.astype(qv_dtype)
            dv_ref[:, cs] = dv.astype(qv_dtype)

    hps = GLA_HEADS_PER_STEP
    col = lambda off: pl.BlockSpec((tb, hps * HEAD), lambda h, n: (tmap(n), off // (hps * HEAD) + h))
    blk = pl.BlockSpec((tb, hps * HEAD), lambda h, n: (tmap(n), h))
    vec = pl.BlockSpec((1, hps * HEAD), lambda h, n: (0, h))
    wide = HG_HEADS * HEAD
    into = jax.ShapeDtypeStruct(dproj.shape, dproj.dtype)
    outs, carried = _call(
        body, name=name, grid=(HG_HEADS // hps, nb),
        in_specs=[pl.BlockSpec((tb, tb), lambda h, n: (0, 0)), col(C_HQ), col(f_off), col(C_HI), vec, blk,
                  pl.BlockSpec((hps, ncb, HEAD, HEAD), lambda h, n: (h, tmap(n), 0, 0)), ANY]
                 + ([blk, blk] if has_prev else []),
        out_specs=[col(C_HQ) if has_prev else blk, blk if has_prev else col(f_off), blk, vec],
        out_shape=[into if has_prev else jax.ShapeDtypeStruct((rows, wide), qv_dtype),
                   jax.ShapeDtypeStruct((rows, wide), BF16) if has_prev else into,
                   jax.ShapeDtypeStruct((rows, wide), qv_dtype), jax.ShapeDtypeStruct((1, wide), F32)],
        aliases={7: 0 if has_prev else 1},
        scratch_shapes=[pltpu.VMEM((hps, HEAD, HEAD), F32)] + [pltpu.VMEM((tb, hps * HEAD), F32)] * 5,
        semantics=("parallel", "arbitrary"),
        ins=[_tri_mask(tb, rev), proj, proj, proj, lb, do, states, dproj] + (list(prev) if has_prev else []),
        carry=None if carry is NO_CARRY else carry)
    return outs if carry is NO_CARRY else (outs, carried)


def _fill_columns(dproj, parts, col, *, name):
    rows, wd = parts[0].shape
    tm = min(1024, rows)
    n = len(parts)

    def body(*refs):
        o_ref = refs[n + 1]
        for j in range(n):
            o_ref[:, j * wd:(j + 1) * wd] = refs[j][...]

    return pl.pallas_call(
        body, name=name, grid=(rows // tm,),
        in_specs=[pl.BlockSpec((tm, wd), lambda i: (i, 0))] * n + [ANY],
        out_specs=pl.BlockSpec((pl.Element(tm), pl.Element(n * wd)), lambda i: (i * tm, col)),
        out_shape=jax.ShapeDtypeStruct(dproj.shape, dproj.dtype), input_output_aliases={n: 0},
        compiler_params=_params(("parallel",)),
    )(*parts, dproj)


def _hg_out_fwd(o_fw, o_bw, proj, gain, *, name):
    rows = o_fw.shape[0]
    tm = min(1024, rows)
    wide = HG_HEADS * HEAD

    def body(a_ref, b_ref, g_ref, gain_ref, o_ref):
        for h in range(HG_HEADS):
            sl = slice(h * HEAD, (h + 1) * HEAD)
            xhat, _ = _rms(a_ref[:, sl] + b_ref[:, sl])
            gate = g_ref[:, sl].astype(F32)
            o_ref[:, sl] = (xhat * gain_ref[...] * (gate * _sigmoid(gate))).astype(BF16)

    row = pl.BlockSpec((tm, wide), lambda i: (i, 0))
    return pl.pallas_call(
        body, name=name, grid=(rows // tm,),
        in_specs=[row, row, pl.BlockSpec((tm, wide), lambda i: (i, C_HG // wide)),
                  pl.BlockSpec((1, HEAD), lambda i: (0, 0))],
        out_specs=row, out_shape=jax.ShapeDtypeStruct((rows, wide), BF16),
        compiler_params=_params(("parallel",)),
    )(o_fw, o_bw, proj, gain)


def _hg_out_bwd(dout, o_fw, o_bw, proj, gain, dproj, *, name):
    rows = o_fw.shape[0]
    tm = min(512, rows)
    wide = HG_HEADS * HEAD

    def body(d_ref, a_ref, b_ref, g_ref, gain_ref, dp_in, dgate_ref, do_ref, dgain_ref):
        del dp_in

        @pl.when(pl.program_id(0) == 0)
        def _():
            dgain_ref[...] = jnp.zeros_like(dgain_ref)

        dgain = jnp.zeros((1, HEAD), F32)
        for h in range(HG_HEADS):
            sl = slice(h * HEAD, (h + 1) * HEAD)
            xhat, r = _rms(a_ref[:, sl] + b_ref[:, sl])
            gate, dy = g_ref[:, sl].astype(F32), d_ref[:, sl]
            gs = _sigmoid(gate)
            dgate_ref[:, sl] = (dy * xhat * gain_ref[...] * (gs * (1.0 + gate * (1.0 - gs)))).astype(BF16)
            dx, dgr = _rms_bwd(dy * (gate * gs), xhat, r, gain_ref[...])
            do_ref[:, sl] = dx
            dgain = dgain + jnp.sum(dgr, axis=0, keepdims=True)
        dgain_ref[...] += dgain

    row = pl.BlockSpec((tm, wide), lambda i: (i, 0))
    gate = pl.BlockSpec((tm, wide), lambda i: (i, C_HG // wide))
    vec = pl.BlockSpec((1, HEAD), lambda i: (0, 0))
    return pl.pallas_call(
        body, name=name, grid=(rows // tm,),
        in_specs=[row, row, row, gate, vec, ANY],
        out_specs=[gate, row, vec],
        out_shape=[jax.ShapeDtypeStruct(dproj.shape, dproj.dtype), jax.ShapeDtypeStruct((rows, wide), F32),
                   jax.ShapeDtypeStruct((1, HEAD), F32)],
        input_output_aliases={5: 0},
        compiler_params=_params(("arbitrary",)),
    )(dout, o_fw, o_bw, proj, gain, dproj)


def _strided_rows(ref, r, count, d):
    return ref[...] if d == 1 else ref[pl.ds(r, count, stride=d), :]


def _store_strided(ref, r, count, d, val):
    if d == 1:
        ref[...] = val
    else:
        ref[pl.ds(r, count, stride=d), :] = val


def _da_prep(proj, q_gain, k_gain, g, *, name):
    d = DA_GROUPS[g][0]
    rows = proj.shape[0]
    rb = min(1024, rows)
    tn = rb // d

    def body(q_ref, k_ref, v_ref, qg_ref, kg_ref, qo_ref, ko_ref, vo_ref, qf_ref, kf_ref, vf_ref):
        for h in range(DA_GH):
            cs = slice(h * HEAD, (h + 1) * HEAD)
            for src, dst in ((q_ref, qf_ref), (k_ref, kf_ref), (v_ref, vf_ref)):
                dst[...] = src[:, cs].astype(F32)
            for r in range(d):
                qhat, _ = _rms(_strided_rows(qf_ref, r, tn, d))
                khat, _ = _rms(_strided_rows(kf_ref, r, tn, d))
                qo_ref[h, r] = (qhat * qg_ref[...]).astype(BF16)
                ko_ref[h, r] = (khat * kg_ref[...]).astype(BF16)
                vo_ref[h, r] = _strided_rows(vf_ref, r, tn, d).astype(BF16)

    wide = DA_GH * HEAD
    col = lambda off: pl.BlockSpec((rb, wide), lambda i: (i, off // wide + g))
    vec = pl.BlockSpec((1, HEAD), lambda i: (0, 0))
    out = pl.BlockSpec((DA_GH, d, tn, HEAD), lambda i: (0, 0, i, 0))
    shape = jax.ShapeDtypeStruct((DA_GH, d, rows // d, HEAD), BF16)
    return pl.pallas_call(
        body, name=name, grid=(rows // rb,),
        in_specs=[col(C_DQ), col(C_DK), col(C_DV), vec, vec],
        out_specs=[out, out, out], out_shape=[shape, shape, shape],
        scratch_shapes=[pltpu.VMEM((rb, HEAD), F32)] * 3,
        compiler_params=_params(("parallel",)),
    )(proj, proj, proj, q_gain, k_gain)


def _slopes(g):
    idx = np.arange(g * DA_GH + 1, (g + 1) * DA_GH + 1)
    s = (2.0 ** (-8.0 * idx / (DA_GH * len(DA_GROUPS)))).astype(np.float32)
    return jnp.asarray(np.broadcast_to(s[:, None, None], (DA_GH, 8, HEAD)).copy())


def _band_window(ld, t, radius):
    win = min(2 * t, ld)
    assert t // 2 >= radius or win == ld
    return win


def _band_scores(q, k, q0, start, slope, d, radius):
    t, win = q.shape[0], k.shape[0]
    row = lax.broadcasted_iota(jnp.int32, (t, win), 0)
    col = lax.broadcasted_iota(jnp.int32, (t, win), 1)
    rel = jnp.abs((start - q0) + col - row)
    return _dot_nt(q, k) * ATT_SCALE + jnp.where(rel <= radius, -slope * (d * rel).astype(F32), NEG_INF)


def _band_fwd(qr, kr, vr, g, *, name):
    d, radius = DA_GROUPS[g]
    _, _, ld, _ = qr.shape
    t = min(HEAD, ld)
    qb = min(BAND_QBLOCKS, ld // t)
    rps = max(1, min(d, BAND_QBLOCKS // qb))
    win = _band_window(ld, t, radius)

    def body(q_ref, k_ref, v_ref, sl_ref, o_ref, lse_ref):
        i = pl.program_id(2)
        slope = sl_ref[0:1, 0:1]
        for rr in range(rps):
            for j in range(qb):
                sl = slice(j * t, (j + 1) * t)
                q0 = (i * qb + j) * t
                start = pl.multiple_of(jnp.clip(q0 - t // 2, 0, ld - win), t // 2)
                s = _band_scores(q_ref[rr, sl, :], k_ref[rr, pl.ds(start, win), :], q0, start, slope, d, radius)
                m = jnp.max(s, axis=-1, keepdims=True)
                p = jnp.exp(s - m).astype(BF16)
                l = jnp.dot(p, _ones(win, HEAD), preferred_element_type=F32)
                o_ref[rr, sl, :] = _dot(p, v_ref[rr, pl.ds(start, win), :]) / l
                lse_ref[rr, sl, :] = m + jnp.log(l)

    own = pl.BlockSpec((None, rps, qb * t, HEAD), lambda h, r, i: (h, r, i, 0))
    seq = pl.BlockSpec((None, rps, ld, HEAD), lambda h, r, i: (h, r, 0, 0))
    shape = jax.ShapeDtypeStruct(qr.shape, F32)
    return pl.pallas_call(
        body, name=name, grid=(DA_GH, d // rps, ld // (qb * t)),
        in_specs=[own, seq, seq, pl.BlockSpec((None, 8, HEAD), lambda h, r, i: (h, 0, 0))],
        out_specs=[own, own], out_shape=[shape, shape],
        compiler_params=_params(("parallel", "parallel", "parallel")),
    )(qr, kr, vr, _slopes(g))


def _band_bwd(qr, kr, vr, dor, lser, deltar, g, *, name):
    d, radius = DA_GROUPS[g]
    _, _, ld, _ = qr.shape
    t = min(HEAD, ld)
    qb = min(BAND_QBLOCKS, ld // t)
    rps = max(1, min(d, BAND_QBLOCKS // qb))
    win = _band_window(ld, t, radius)

    def body(q_ref, k_ref, v_ref, do_ref, lse_ref, dl_ref, sl_ref, dq_ref, dk_ref, dv_ref):
        i = pl.program_id(2)

        @pl.when(i == 0)
        def _():
            dk_ref[...] = jnp.zeros_like(dk_ref)
            dv_ref[...] = jnp.zeros_like(dv_ref)

        slope = sl_ref[0:1, 0:1]
        for rr in range(rps):
            for j in range(qb):
                sl = slice(j * t, (j + 1) * t)
                q0 = (i * qb + j) * t
                start = pl.multiple_of(jnp.clip(q0 - t // 2, 0, ld - win), t // 2)
                rows = pl.ds(start, win)
                q, dout, k, v = q_ref[rr, sl, :], do_ref[rr, sl, :], k_ref[rr, rows, :], v_ref[rr, rows, :]
                p = jnp.exp(_band_scores(q, k, q0, start, slope, d, radius) - lse_ref[rr, sl, 0:1])
                ds = p * (_dot_nt(dout, v) - dl_ref[rr, sl, 0:1])
                dq_ref[rr, sl, :] = _dot(ds, k) * ATT_SCALE
                dk_ref[rr, rows, :] += _dot_tn(ds, q) * ATT_SCALE
                dv_ref[rr, rows, :] += _dot_tn(p, dout)

    own = pl.BlockSpec((None, rps, qb * t, HEAD), lambda h, r, i: (h, r, i, 0))
    seq = pl.BlockSpec((None, rps, ld, HEAD), lambda h, r, i: (h, r, 0, 0))
    shape = jax.ShapeDtypeStruct(qr.shape, F32)
    return pl.pallas_call(
        body, name=name, grid=(DA_GH, d // rps, ld // (qb * t)),
        in_specs=[own, seq, seq, own, own, own, pl.BlockSpec((None, 8, HEAD), lambda h, r, i: (h, 0, 0))],
        out_specs=[own, seq, seq], out_shape=[shape, shape, shape],
        compiler_params=_params(("parallel", "parallel", "arbitrary")),
    )(qr, kr, vr, dor, lser, deltar, _slopes(g))


def _da_merge(outs, lses, rows, *, name):
    rb = min(1024, rows)
    wide = DA_GH * HEAD

    def body(*refs):
        o_refs, l_refs = refs[0:3], refs[3:6]
        o_ref, lse_ref = refs[6:8]
        on_refs, ln_refs = refs[8:11], refs[11:14]
        for h in range(DA_GH):
            cs = slice(h * HEAD, (h + 1) * HEAD)
            for g, (d, _) in enumerate(DA_GROUPS):
                tn = rb // d
                for r in range(d):
                    _store_strided(on_refs[g], r, tn, d, o_refs[g][h, r])
                    _store_strided(ln_refs[g], r, tn, d, l_refs[g][h, r])
            l0, l1, l2 = ln_refs[0][...], ln_refs[1][...], ln_refs[2][...]
            m = jnp.maximum(jnp.maximum(l0, l1), l2)
            e0, e1, e2 = jnp.exp(l0 - m), jnp.exp(l1 - m), jnp.exp(l2 - m)
            tot = e0 + e1 + e2
            o_ref[:, cs] = (e0 * on_refs[0][...] + e1 * on_refs[1][...] + e2 * on_refs[2][...]) / tot
            lse_ref[:, cs] = m + jnp.log(tot)

    res = lambda d: pl.BlockSpec((DA_GH, d, rb // d, HEAD), lambda i: (0, 0, i, 0))
    nat = pl.BlockSpec((rb, wide), lambda i: (i, 0))
    shape = jax.ShapeDtypeStruct((rows, wide), F32)
    return pl.pallas_call(
        body, name=name, grid=(rows // rb,),
        in_specs=[res(d) for d, _ in DA_GROUPS] * 2,
        out_specs=[nat, nat], out_shape=[shape, shape],
        scratch_shapes=[pltpu.VMEM((rb, HEAD), F32)] * 6,
        compiler_params=_params(("parallel",)),
    )(*outs, *lses)


def _da_bwd_prep(dout, o, lse, *, name):
    rows = o.shape[0]
    rb = min(512, rows)
    wide = DA_GH * HEAD

    def body(d_ref, o_ref, l_ref, *outs):
        d_scr, l_scr, delta_scr = outs[9:12]
        for h in range(DA_GH):
            cs = slice(h * HEAD, (h + 1) * HEAD)
            dv = d_ref[:, cs]
            d_scr[...] = dv
            l_scr[...] = l_ref[:, cs]
            delta_scr[...] = _lane_sum(dv * o_ref[:, cs])
            for g, (d, _) in enumerate(DA_GROUPS):
                tn = rb // d
                for r in range(d):
                    outs[3 * g][h, r] = _strided_rows(d_scr, r, tn, d).astype(BF16)
                    outs[3 * g + 1][h, r] = _strided_rows(l_scr, r, tn, d)
                    outs[3 * g + 2][h, r] = _strided_rows(delta_scr, r, tn, d)

    nat = pl.BlockSpec((rb, wide), lambda i: (i, 0))
    out_specs, out_shape = [], []
    for d, _ in DA_GROUPS:
        for dt in (BF16, F32, F32):
            out_specs.append(pl.BlockSpec((DA_GH, d, rb // d, HEAD), lambda i: (0, 0, i, 0)))
            out_shape.append(jax.ShapeDtypeStruct((DA_GH, d, rows // d, HEAD), dt))
    return pl.pallas_call(
        body, name=name, grid=(rows // rb,),
        in_specs=[nat, nat, nat], out_specs=out_specs, out_shape=out_shape,
        scratch_shapes=[pltpu.VMEM((rb, HEAD), F32)] * 3,
        compiler_params=_params(("parallel",)),
    )(dout, o, lse)


def _da_prep_bwd(dqr, dkr, dvr, proj, q_gain, k_gain, g, *, name):
    d = DA_GROUPS[g][0]
    rows = proj.shape[0]
    rb = min(1024, rows)
    tn = rb // d
    wide = DA_GH * HEAD

    def body(dq_ref, dk_ref, dv_ref, q_ref, k_ref, qg_ref, kg_ref, oq_ref, ok_ref, ov_ref, gq_ref, gk_ref, *nat_refs):
        @pl.when(pl.program_id(0) == 0)
        def _():
            gq_ref[...] = jnp.zeros_like(gq_ref)
            gk_ref[...] = jnp.zeros_like(gk_ref)

        for h in range(DA_GH):
            cs = slice(h * HEAD, (h + 1) * HEAD)
            for j, src in enumerate((dq_ref, dk_ref, dv_ref)):
                for r in range(d):
                    _store_strided(nat_refs[j], r, tn, d, src[h, r])
            ov_ref[:, cs] = nat_refs[2][...].astype(BF16)
            for j, (x_ref, gn_ref, out_ref, acc_ref) in enumerate(((q_ref, qg_ref, oq_ref, gq_ref),
                                                                    (k_ref, kg_ref, ok_ref, gk_ref))):
                xhat, r = _rms(x_ref[:, cs])
                dx, dgr = _rms_bwd(nat_refs[j][...], xhat, r, gn_ref[...])
                out_ref[:, cs] = dx.astype(BF16)
                acc_ref[...] += jnp.sum(dgr, axis=0, keepdims=True)

    res = pl.BlockSpec((DA_GH, d, tn, HEAD), lambda i: (0, 0, i, 0))
    col = lambda off: pl.BlockSpec((rb, wide), lambda i: (i, off // wide + g))
    vec = pl.BlockSpec((1, HEAD), lambda i: (0, 0))
    nat = pl.BlockSpec((rb, wide), lambda i: (i, 0))
    shape = jax.ShapeDtypeStruct((rows, wide), BF16)
    gshape = jax.ShapeDtypeStruct((1, HEAD), F32)
    return pl.pallas_call(
        body, name=name, grid=(rows // rb,),
        in_specs=[res, res, res, col(C_DQ), col(C_DK), vec, vec],
        out_specs=[nat, nat, nat, vec, vec], out_shape=[shape, shape, shape, gshape, gshape],
        scratch_shapes=[pltpu.VMEM((rb, HEAD), F32)] * 3,
        compiler_params=_params(("arbitrary",)),
    )(dqr, dkr, dvr, proj, proj, q_gain, k_gain)


def _mem_fwd(proj, kv, q_gain, k_gain, *, name):
    rows = proj.shape[0]
    tm = min(1024, rows)
    n_mem = kv.shape[0]

    def body(q_ref, k_ref, v_ref, qg_ref, kg_ref, o_ref):
        qhat, _ = _rms(q_ref[...])
        khat, _ = _rms(k_ref[...])
        s = _dot_nt(qhat * qg_ref[...], khat * kg_ref[...]) * ATT_SCALE
        p = jnp.exp(s - jnp.max(s, axis=-1, keepdims=True))
        p = p / jnp.sum(p, axis=-1, keepdims=True)
        o_ref[...] = _dot(p, v_ref[...]).astype(BF16)

    vec = pl.BlockSpec((1, HEAD), lambda i, h: (0, 0))
    return pl.pallas_call(
        body, name=name, grid=(rows // tm, MEM_HEADS),
        in_specs=[pl.BlockSpec((tm, HEAD), lambda i, h: (i, C_MQ // HEAD + h)),
                  pl.BlockSpec((n_mem, HEAD), lambda i, h: (0, h)),
                  pl.BlockSpec((n_mem, HEAD), lambda i, h: (0, MEM_HEADS + h)), vec, vec],
        out_specs=pl.BlockSpec((tm, HEAD), lambda i, h: (i, h)),
        out_shape=jax.ShapeDtypeStruct((rows, MEM_HEADS * HEAD), BF16),
        compiler_params=_params(("parallel", "parallel")),
    )(proj, kv, kv, q_gain, k_gain)


def _mem_bwd(dout, proj, kv, q_gain, k_gain, dproj, *, name):
    rows = proj.shape[0]
    tm = min(1024, rows)
    steps = rows // tm
    n_mem = kv.shape[0]

    def body(d_ref, q_ref, k_ref, v_ref, qg_ref, kg_ref, dp_in, dq_ref, dk_ref, dv_ref, gq_ref, gk_ref, dkn_ref):
        del dp_in
        h, i = pl.program_id(0), pl.program_id(1)

        @pl.when((h == 0) & (i == 0))
        def _():
            gq_ref[...] = jnp.zeros_like(gq_ref)
            gk_ref[...] = jnp.zeros_like(gk_ref)

        @pl.when(i == 0)
        def _():
            dkn_ref[...] = jnp.zeros_like(dkn_ref)
            dv_ref[...] = jnp.zeros_like(dv_ref)

        qhat, rq = _rms(q_ref[...])
        khat, rk = _rms(k_ref[...])
        qn, kn = qhat * qg_ref[...], khat * kg_ref[...]
        s = _dot_nt(qn, kn) * ATT_SCALE
        p = jnp.exp(s - jnp.max(s, axis=-1, keepdims=True))
        p = p / jnp.sum(p, axis=-1, keepdims=True)
        dout = d_ref[...]
        dp = _dot_nt(dout, v_ref[...])
        ds = p * (dp - jnp.sum(p * dp, axis=-1, keepdims=True))
        dv_ref[...] += _dot_tn(p, dout)
        dkn_ref[...] += _dot_tn(ds, qn) * ATT_SCALE
        dq, dgr = _rms_bwd(_dot(ds, kn) * ATT_SCALE, qhat, rq, qg_ref[...])
        dq_ref[...] = dq.astype(BF16)
        gq_ref[...] += jnp.sum(dgr, axis=0, keepdims=True)

        @pl.when(i == steps - 1)
        def _():
            dk, dgk = _rms_bwd(dkn_ref[...], khat, rk, kg_ref[...])
            dk_ref[...] = dk
            gk_ref[...] += jnp.sum(dgk, axis=0, keepdims=True)

    vec = pl.BlockSpec((1, HEAD), lambda h, i: (0, 0))
    memh = pl.BlockSpec((n_mem, HEAD), lambda h, i: (0, h))
    tok = pl.BlockSpec((tm, HEAD), lambda h, i: (i, h))
    mq = pl.BlockSpec((tm, HEAD), lambda h, i: (i, C_MQ // HEAD + h))
    gshape = jax.ShapeDtypeStruct((1, HEAD), F32)
    return pl.pallas_call(
        body, name=name, grid=(MEM_HEADS, steps),
        in_specs=[tok, mq, memh, pl.BlockSpec((n_mem, HEAD), lambda h, i: (0, MEM_HEADS + h)), vec, vec, ANY],
        out_specs=[mq, memh, memh, vec, vec],
        out_shape=[jax.ShapeDtypeStruct(dproj.shape, dproj.dtype),
                   jax.ShapeDtypeStruct((n_mem, MEM_HEADS * HEAD), F32),
                   jax.ShapeDtypeStruct((n_mem, MEM_HEADS * HEAD), F32), gshape, gshape],
        input_output_aliases={6: 0},
        scratch_shapes=[pltpu.VMEM((n_mem, HEAD), F32)],
        compiler_params=_params(("arbitrary", "arbitrary")),
    )(dout, proj, kv, kv, q_gain, k_gain, dproj)


def _branch_fwd(o_hg, o_da, o_mem, proj, wp_hg, wp_da, wp_mem, *, name):
    rows = o_hg.shape[0]
    tm = min(512, rows)

    def body(a_ref, b_ref, c_ref, ga_ref, gb_ref, gc_ref, wa_ref, wb_ref, wc_ref, o_ref):
        merged = _sigmoid(ga_ref[...]) * _dot(a_ref[...], wa_ref[...])
        merged += _sigmoid(gb_ref[...]) * _dot(b_ref[...], wb_ref[...])
        merged += _sigmoid(gc_ref[...]) * _dot(c_ref[...], wc_ref[...])
        o_ref[...] = merged.astype(BF16)

    row = lambda w: pl.BlockSpec((tm, w), lambda i: (i, 0))
    gate = lambda off: pl.BlockSpec((tm, D_MODEL), lambda i: (i, off // D_MODEL))
    full = lambda a: pl.BlockSpec(a.shape, lambda i: (0, 0))
    return pl.pallas_call(
        body, name=name, grid=(rows // tm,),
        in_specs=[row(o_hg.shape[1]), row(o_da.shape[1]), row(o_mem.shape[1]),
                  gate(C_GHG), gate(C_GDA), gate(C_GMEM), full(wp_hg), full(wp_da), full(wp_mem)],
        out_specs=row(D_MODEL), out_shape=jax.ShapeDtypeStruct((rows, D_MODEL), BF16),
        compiler_params=_params(("parallel",)),
    )(o_hg, o_da, o_mem, proj, proj, proj, wp_hg, wp_da, wp_mem)


def _branch_bwd(dm, o_hg, o_da, o_mem, proj, wp_hg, wp_da, wp_mem, *, name):
    rows = o_hg.shape[0]
    tm = min(256, rows)

    def body(dm_ref, a_ref, b_ref, c_ref, ga_ref, gb_ref, gc_ref, wa_ref, wb_ref, wc_ref, dp_ref, *outs):
        dmv = dm_ref[...]
        for j, (o_ref, g_ref, w_ref) in enumerate(((a_ref, ga_ref, wa_ref), (b_ref, gb_ref, wb_ref),
                                                   (c_ref, gc_ref, wc_ref))):
            z = _dot(o_ref[...], w_ref[...])
            gs = _sigmoid(g_ref[...])
            dz = (dmv * gs).astype(BF16)
            dp_ref[:, j * D_MODEL:(j + 1) * D_MODEL] = (dmv * z * gs * (1.0 - gs)).astype(BF16)
            outs[2 * j][...] = dz
            outs[2 * j + 1][...] = _dot_nt(dz, w_ref[...])

    row = lambda w: pl.BlockSpec((tm, w), lambda i: (i, 0))
    gate = lambda off: pl.BlockSpec((tm, D_MODEL), lambda i: (i, off // D_MODEL))
    full = lambda a: pl.BlockSpec(a.shape, lambda i: (0, 0))
    out_specs = [pl.BlockSpec((pl.Element(tm), pl.Element(3 * D_MODEL)), lambda i: (i * tm, C_GHG))]
    out_shape = [jax.ShapeDtypeStruct((rows, IN_COLS), BF16)]
    for o in (o_hg, o_da, o_mem):
        out_specs += [row(D_MODEL), row(o.shape[1])]
        out_shape += [jax.ShapeDtypeStruct((rows, D_MODEL), BF16), jax.ShapeDtypeStruct((rows, o.shape[1]), F32)]
    return pl.pallas_call(
        body, name=name, grid=(rows // tm,),
        in_specs=[row(D_MODEL), row(o_hg.shape[1]), row(o_da.shape[1]), row(o_mem.shape[1]),
                  gate(C_GHG), gate(C_GDA), gate(C_GMEM), full(wp_hg), full(wp_da), full(wp_mem)],
        out_specs=out_specs, out_shape=out_shape,
        compiler_params=_params(("parallel",)),
    )(dm, o_hg, o_da, o_mem, proj, proj, proj, wp_hg, wp_da, wp_mem)


def _ffn_in(h2, w_ab, *, name):
    rows, dff = h2.shape[0], w_ab.shape[1] // 2
    tm, tn = min(2048, rows), 256

    def body(h_ref, wa_ref, wb_ref, a_ref, b_ref, u_ref):
        a = _dot(h_ref[...], wa_ref[...])
        b = _dot(h_ref[...], wb_ref[...])
        a_ref[...] = a.astype(BF16)
        b_ref[...] = b.astype(BF16)
        u_ref[...] = (a * _sigmoid(a) * b).astype(BF16)

    out = pl.BlockSpec((tm, tn), lambda i, j: (i, j))
    return pl.pallas_call(
        body, name=name, grid=(rows // tm, dff // tn),
        in_specs=[pl.BlockSpec((tm, D_MODEL), lambda i, j: (i, 0)),
                  pl.BlockSpec((D_MODEL, tn), lambda i, j: (0, j)),
                  pl.BlockSpec((D_MODEL, tn), lambda i, j: (0, dff // tn + j))],
        out_specs=[out, out, out],
        out_shape=[jax.ShapeDtypeStruct((rows, dff), BF16)] * 3,
        compiler_params=_params(("parallel", "parallel")),
    )(h2, w_ab, w_ab)


def _ffn_act_bwd(dy, w_out, a, b, *, name):
    rows, dff = a.shape
    tm, tn = min(1024, rows), 256

    def body(dy_ref, w_ref, a_ref, b_ref, da_ref, db_ref):
        du = _dot_nt(dy_ref[...], w_ref[...])
        av, bv = a_ref[...].astype(F32), b_ref[...].astype(F32)
        sa = _sigmoid(av)
        da_ref[...] = (du * bv * sa * (1.0 + av * (1.0 - sa))).astype(BF16)
        db_ref[...] = (du * av * sa).astype(BF16)

    tile = pl.BlockSpec((tm, tn), lambda i, j: (i, j))
    return pl.pallas_call(
        body, name=name, grid=(rows // tm, dff // tn),
        in_specs=[pl.BlockSpec((tm, D_MODEL), lambda i, j: (i, 0)),
                  pl.BlockSpec((tn, D_MODEL), lambda i, j: (j, 0)), tile, tile],
        out_specs=[tile, tile],
        out_shape=[jax.ShapeDtypeStruct((rows, dff), BF16), jax.ShapeDtypeStruct((rows, dff), BF16)],
        compiler_params=_params(("parallel", "parallel")),
    )(dy, w_out, a, b)


def _lower_bound(lb_fw, lb_bw, *, name):
    def body(a_ref, b_ref, oa_ref, ob_ref):
        for src, dst in ((a_ref, oa_ref), (b_ref, ob_ref)):
            dst[...] = _sigmoid(src[0:1, :] - src[1:2, :])

    shape = jax.ShapeDtypeStruct((1, lb_fw.shape[1]), F32)
    return pl.pallas_call(body, name=name, out_shape=[shape, shape])(lb_fw, lb_bw)


def _local_step(x, mem, tgt, p, ex):
    rows = x.shape[0]
    lb_fw, lb_bw = _lower_bound(p["lb_logits_fw"], p["lb_logits_bw"], name="lower_bound")

    h = _rms_fwd(x, p["norm_mix_gain"], name="norm_mix")
    w = {"w_in": ex.w_in}
    proj, carried = _matmul([(h, w["w_in"])], "nn", BF16, tm=1024, tn=1024, carry=ex.late_carry(), name="proj_in")
    w.update(ex.late_weights(carried))
    o_fw, st_fw = _gla_fwd(proj, lb_fw, f_off=C_FF, rev=False, name="gla_fwd_fw")
    o_bw, st_bw = _gla_fwd(proj, lb_bw, f_off=C_FB, rev=True, name="gla_fwd_bw")
    o_hg = _hg_out_fwd(o_fw, o_bw, proj, p["hg_norm_gain"], name="hg_out")

    qkv_r, outs, lses = [], [], []
    for g in range(len(DA_GROUPS)):
        qr, kr, vr = _da_prep(proj, p["da_q_gain"], p["da_k_gain"], g, name=f"da_prep{g}")
        og, lg = _band_fwd(qr, kr, vr, g, name=f"band_fwd{g}")
        qkv_r.append((qr, kr, vr))
        outs.append(og)
        lses.append(lg)
    o_da, lse_da = _da_merge(outs, lses, rows, name="da_merge")

    mem_n = _rms_fwd(mem, p["norm_mem_gain"], name="norm_mem")
    kv = _matmul([(mem_n, w["w_mem_kv"])], "nn", F32, tm=256, tn=512, name="mem_kv")
    o_mem = _mem_fwd(proj, kv, p["mem_q_gain"], p["mem_k_gain"], name="mem_attn")

    merged = _branch_fwd(o_hg, o_da, o_mem, proj, w["w_proj_hg"], w["w_proj_da"], w["w_proj_mem"],
                         name="branch_merge")
    x1, h2 = _out_norm(merged, w["w_out"], x, p["norm_ffn_gain"], name="mix_out_norm")
    a, b, u = _ffn_in(h2, w["w_ffn_in"], name="ffn_in")
    dy, dy_b, loss = _out_loss(u, w["w_ffn_out"], x1, tgt, name="ffn_out_loss")

    gw, gwb, gs = {}, {}, {}
    gw["w_ffn_out"], gwb["w_ffn_out"] = _matmul([(u, dy_b)], "tn", F32, tm=256, tn=1024, also=BF16, name="g_ffn_out")
    da, db = _ffn_act_bwd(dy_b, w["w_ffn_out"], a, b, name="ffn_act_bwd")
    gw["w_ffn_a"], gwb["w_ffn_a"] = _matmul([(h2, da)], "tn", F32, tm=512, tn=256, also=BF16, name="g_ffn_a")
    gw["w_ffn_b"], gwb["w_ffn_b"] = _matmul([(h2, db)], "tn", F32, tm=512, tn=256, also=BF16, name="g_ffn_b")
    dh2 = _matmul([(da, w["w_ffn_in"], 0), (db, w["w_ffn_in"], 1)], "nt", F32, tm=256, tn=1024, name="d_h2")
    dx1, dx1_b, gs["norm_ffn_gain"] = _rms_bwd_rows(dh2, x1, p["norm_ffn_gain"], dy, (F32, BF16), name="norm_ffn_bwd")
    gw["w_out"], gwb["w_out"] = _matmul([(merged, dx1_b)], "tn", F32, tm=512, tn=512, also=BF16, name="g_out")
    dmerged = _matmul([(dx1_b, w["w_out"])], "nt", F32, tm=512, tn=512, name="d_merged")
    dproj, dz_hg, do_hg, dz_da, do_da, dz_mem, do_mem = _branch_bwd(
        dmerged, o_hg, o_da, o_mem, proj, w["w_proj_hg"], w["w_proj_da"], w["w_proj_mem"], name="branch_bwd")
    gw["w_proj_hg"], gwb["w_proj_hg"] = _matmul([(o_hg, dz_hg)], "tn", F32, tm=512, tn=512, also=BF16, name="g_proj_hg")
    gw["w_proj_da"], gwb["w_proj_da"] = _matmul([(o_da, dz_da)], "tn", F32, tm=512, tn=512, also=BF16, name="g_proj_da")
    gw["w_proj_mem"], gwb["w_proj_mem"] = _matmul([(o_mem, dz_mem)], "tn", F32, tm=512, tn=512, also=BF16, name="g_proj_mem")

    dproj, dk_mem, dv_mem, gs["mem_q_gain"], gs["mem_k_gain"] = _mem_bwd(
        do_mem, proj, kv, p["mem_q_gain"], p["mem_k_gain"], dproj, name="mem_attn_bwd")
    dkv = jnp.concatenate([dk_mem, dv_mem], axis=1).astype(BF16)
    gw["w_mem_kv"], gwb["w_mem_kv"] = _matmul([(mem_n, dkv)], "tn", F32, tm=512, tn=512, also=BF16, name="g_mem_kv")
    dmem_n = _matmul([(dkv, w["w_mem_kv"])], "nt", F32, tm=256, tn=512, name="d_mem_n")
    (gs["norm_mem_gain"],) = _rms_bwd_rows(dmem_n, mem, p["norm_mem_gain"], None, (), name="norm_mem_bwd")

    prep = _da_bwd_prep(do_da, o_da, lse_da, name="da_bwd_prep")
    d_da, gq_parts, gk_parts = [], [], []
    for g in range(len(DA_GROUPS)):
        qr, kr, vr = qkv_r[g]
        dor, lser, deltar = prep[3 * g:3 * g + 3]
        dqr, dkr, dvr = _band_bwd(qr, kr, vr, dor, lser, deltar, g, name=f"band_bwd{g}")
        dq, dk, dv, gq, gk = _da_prep_bwd(dqr, dkr, dvr, proj, p["da_q_gain"], p["da_k_gain"], g,
                                          name=f"da_prep_bwd{g}")
        d_da.append((dq, dk, dv))
        gq_parts.append(gq)
        gk_parts.append(gk)
    for j, (off, tag) in enumerate(((C_DQ, "q"), (C_DK, "k"), (C_DV, "v"))):
        dproj = _fill_columns(dproj, [t[j] for t in d_da], off, name=f"dproj_fill_da_{tag}")

    dproj, do_gla, gs["hg_norm_gain"] = _hg_out_bwd(do_hg, o_fw, o_bw, proj, p["hg_norm_gain"], dproj,
                                                    name="hg_out_bwd")
    (dq_f, dproj, dv_f, dlb_fw), carried = _gla_bwd(proj, lb_fw, do_gla, st_fw, None, dproj, f_off=C_FF, rev=False,
                                                    carry=ex.early_carry(gw, gwb), name="gla_bwd_fw")
    ex.early_done(carried)
    dproj, dfl_bw, dv_hg, dlb_bw = _gla_bwd(proj, lb_bw, do_gla, st_bw, (dq_f, dv_f), dproj, f_off=C_FB, rev=True,
                                            name="gla_bwd_bw")
    dproj = _fill_columns(dproj, [dfl_bw, dv_hg], C_FB, name="dproj_fill")
    g_in, g_in_b = _matmul([(h, dproj)], "tn", F32, tm=512, tn=1024, also=BF16, name="g_in")
    dh, carried = _matmul([(dproj, w["w_in"])], "nt", F32, tm=1024, tn=1024, tk=IN_COLS // 8,
                          carry=ex.w_in_carry(g_in, g_in_b), name="d_h")
    ex.w_in_done(carried)
    grad_x, gs["norm_mix_gain"] = _rms_bwd_rows(dh, x, p["norm_mix_gain"], dx1, (F32,), name="norm_mix_bwd")

    small = _small_pack(gs, gq_parts, gk_parts, dlb_fw, dlb_bw, lb_fw, lb_bw, name="small_pack")
    return loss, grad_x, small


def _small_pack(gs, gq_parts, gk_parts, dlb_fw, dlb_bw, lb_fw, lb_bw, *, name):
    def body(g_mix, g_mem, g_ffn, dfw, dbw, lfw, lbw, g_hg, q0, q1, q2, k0, k1, k2, g_mq, g_mk, o_ref):
        o_ref[0:1, :] = g_mix[...]
        o_ref[1:2, :] = g_mem[...]
        o_ref[2:3, :] = g_ffn[...]
        for base, d_ref, l_ref in ((3, dfw, lfw), (5, dbw, lbw)):
            lbv = l_ref[...]
            t = d_ref[...] * lbv * (1.0 - lbv)
            o_ref[base:base + 1, :] = t
            o_ref[base + 1:base + 2, :] = -t
        o_ref[7:8, :] = jnp.zeros((1, D_MODEL), F32)
        o_ref[7:8, 0:HEAD] = g_hg[...]
        o_ref[7:8, HEAD:2 * HEAD] = q0[...] + q1[...] + q2[...]
        o_ref[7:8, 2 * HEAD:3 * HEAD] = k0[...] + k1[...] + k2[...]
        o_ref[7:8, 3 * HEAD:4 * HEAD] = g_mq[...]
        o_ref[7:8, 4 * HEAD:5 * HEAD] = g_mk[...]

    return pl.pallas_call(body, name=name, out_shape=jax.ShapeDtypeStruct((8, D_MODEL), F32))(
        gs["norm_mix_gain"], gs["norm_mem_gain"], gs["norm_ffn_gain"], dlb_fw, dlb_bw, lb_fw, lb_bw,
        gs["hg_norm_gain"], *gq_parts, *gk_parts, gs["mem_q_gain"], gs["mem_k_gain"])


def _row_tile(rows, cols, n_arrays):
    budget = (16 * 1024 * 1024) // (2 * 4 * cols * n_arrays)
    tr = rows
    while tr > budget and tr % 2 == 0 and (tr // 2) % 16 == 0:
        tr //= 2
    return tr


def _cast_into_full(a, chip, rows, cols, axis, *, name):
    sr, sc = a.shape
    tr = _row_tile(sr, sc, 2)

    def body(chip_ref, a_ref, o_ref):
        del chip_ref
        o_ref[...] = a_ref[...].astype(BF16)

    if axis == 1:
        out_map = lambda i, chip_ref: (i, chip_ref[0])
    else:
        out_map = lambda i, chip_ref: (chip_ref[0] * (sr // tr) + i, 0)
    return pl.pallas_call(
        body, name=name,
        grid_spec=pltpu.PrefetchScalarGridSpec(
            num_scalar_prefetch=1, grid=(sr // tr,),
            in_specs=[pl.BlockSpec((tr, sc), lambda i, chip_ref: (i, 0))],
            out_specs=pl.BlockSpec((tr, sc), out_map)),
        out_shape=jax.ShapeDtypeStruct((rows, cols), BF16),
        compiler_params=_params(("parallel",)))(chip, a)


def _add_halves(items, *, name):
    rows = items[0][3].shape[0]
    widths = [ra.shape[1] for _, _, _, ra in items]
    tr = _row_tile(rows, sum(widths), 4)

    def body(*refs):
        o_ref = refs[-1]
        first = lax.axis_index("c") == 0
        off = 0
        for j, wd in enumerate(widths):
            h0, h1, ra = refs[3 * j:3 * j + 3]
            o_ref[:, off:off + wd] = (jnp.where(first, h0[...], h1[...]) + ra[...]).astype(BF16)
            off += wd

    in_specs, ins = [], []
    for (g, haxis, hsize, ra), wd in zip(items, widths):
        if haxis == 0:
            in_specs += [pl.BlockSpec((tr, wd), lambda i: (i, 0)),
                         pl.BlockSpec((tr, wd), lambda i, o=hsize // tr: (o + i, 0))]
        else:
            in_specs += [pl.BlockSpec((tr, wd), lambda i: (i, 0)), pl.BlockSpec((tr, wd), lambda i: (i, 1))]
        in_specs.append(pl.BlockSpec((tr, wd), lambda i: (i, 0)))
        ins += [g, g, ra]
    return pl.pallas_call(body, name=name, grid=(rows // tr,), in_specs=in_specs,
                          out_specs=pl.BlockSpec((tr, sum(widths)), lambda i: (i, 0)),
                          out_shape=jax.ShapeDtypeStruct((rows, sum(widths)), BF16),
                          compiler_params=_params(("parallel",)))(*ins)


def _add_slots(rb, *, name):
    _, rows, cols = rb.shape
    tr = _row_tile(rows, cols, 5)

    def body(r0, r1, r2, r3, o_ref):
        o_ref[...] = ((r0[...].astype(F32) + r1[...].astype(F32)) + r2[...].astype(F32)) + r3[...].astype(F32)

    slot = lambda s: pl.BlockSpec((None, tr, cols), lambda i: (s, i, 0))
    return pl.pallas_call(body, name=name, grid=(rows // tr,), in_specs=[slot(s) for s in range(4)],
                          out_specs=pl.BlockSpec((tr, cols), lambda i: (i, 0)),
                          out_shape=jax.ShapeDtypeStruct((rows, cols), F32),
                          compiler_params=_params(("parallel",)))(rb, rb, rb, rb)


def _adamw(w, g, m, v, *, name):
    rows, cols = w.shape
    tr = _row_tile(rows, cols, 7) if rows % 16 == 0 else rows
    c1 = 1.0 - ADAM_B1 ** ADAM_STEP
    c2 = 1.0 - ADAM_B2 ** ADAM_STEP

    def body(w_ref, g_ref, m_ref, v_ref, d_ref, mo_ref, vo_ref):
        gv = g_ref[...]
        mn = ADAM_B1 * m_ref[...] + (1.0 - ADAM_B1) * gv
        vn = ADAM_B2 * v_ref[...] + (1.0 - ADAM_B2) * (gv * gv)
        mo_ref[...] = mn
        vo_ref[...] = vn
        d_ref[...] = -ADAM_LR * ((mn / c1) / (jnp.sqrt(vn / c2) + ADAM_EPS) + ADAM_WD * w_ref[...])

    spec = pl.BlockSpec((tr, cols), lambda i: (i, 0))
    shape = jax.ShapeDtypeStruct((rows, cols), F32)
    return pl.pallas_call(body, name=name, grid=(rows // tr,), in_specs=[spec] * 4, out_specs=[spec] * 3,
                          out_shape=[shape] * 3, compiler_params=_params(("parallel",)))(w, g, m, v)


W_SPECS = (
    ("w_in", 1024, IN_COLS, 1, IN_COLS // 4),
    ("w_mem_kv", 1024, 1024, 0, 256),
    ("w_proj_hg", 1024, 1024, 0, 256),
    ("w_proj_da", 512, 1024, 1, 256),
    ("w_proj_mem", 512, 1024, 1, 256),
    ("w_out", 1024, 1024, 0, 256),
    ("w_ffn_in", 1024, 2 * D_FF, 1, 2 * D_FF // 4),
    ("w_ffn_out", D_FF, 1024, 0, D_FF // 4),
)
CHIP_FLIPS = ((1, 0), (0, 1), (1, 1))
ANY = pl.BlockSpec(memory_space=pl.ANY)
DMA_CHUNK_BYTES = 1 << 20
STAGE_BYTES = 2 << 20


def _place():
    x, y, c = lax.axis_index("x"), lax.axis_index("y"), lax.axis_index("c")
    return x, y, c, 2 * x + y


def _flip(v, f):
    return 1 - v if f else v


def _slab(ref, axis, idx, size):
    start = pl.multiple_of(idx * size, size)
    return ref.at[pl.ds(start, size), :] if axis == 0 else ref.at[:, pl.ds(start, size)]


def _chunked(make, src, dst, want="both"):
    rows, cols = src.shape
    row_bytes = cols * jnp.dtype(src.dtype).itemsize
    k = 1
    while rows % (2 * k) == 0 and (rows // (2 * k)) % 16 == 0 and (rows // k) * row_bytes > DMA_CHUNK_BYTES:
        k *= 2
    cr = rows // k
    parts = []
    if want != "wait":
        parts = [make(src.at[pl.ds(j * cr, cr), :], dst.at[pl.ds(j * cr, cr), :]) for j in range(k)]
    return parts, (make(src, dst) if want != "start" else None)


def _half_spec(rows, cols, axis):
    return (0, rows // 2) if axis == 1 else (1, cols // 2)


def _staged(src, remote_dst, local_dst, sibling, load_sems, send_sems, store_sems, recv_sem):
    rows, cols = src.shape
    row_bytes = cols * jnp.dtype(src.dtype).itemsize
    k = 1
    while (rows // k) * row_bytes > STAGE_BYTES and rows % (2 * k) == 0 and (rows // (2 * k)) % 16 == 0:
        k *= 2
    cr = rows // k
    piece = lambda ref, j: ref.at[pl.ds(j * cr, cr), :]

    def run(buf):
        loads = [pltpu.make_async_copy(piece(src, j), buf.at[j % 2], load_sems.at[j % 2]) for j in range(k)]
        outs = [[pltpu.make_async_remote_copy(src_ref=buf.at[j % 2], dst_ref=piece(remote_dst, j),
                                              send_sem=send_sems.at[j % 2], recv_sem=recv_sem,
                                              device_id=sibling, device_id_type=MESH)] for j in range(k)]
        if local_dst is not None:
            for j in range(k):
                outs[j].append(pltpu.make_async_copy(buf.at[j % 2], piece(local_dst, j), store_sems.at[j % 2]))

        def drained(j):
            outs[j][0].wait_send()
            for cp in outs[j][1:]:
                cp.wait()

        loads[0].start()
        for j in range(k):
            loads[j].wait()
            for cp in outs[j]:
                cp.start()
            if j + 1 < k:
                if j >= 1:
                    drained(j - 1)
                loads[j + 1].start()
        for j in range(max(0, k - 2), k):
            drained(j)

    pl.run_scoped(run, pltpu.VMEM((2, cr, cols), src.dtype))


def _landed(ref, recv_sem, send_sem):
    pltpu.make_async_remote_copy(src_ref=ref, dst_ref=ref, send_sem=send_sem, recv_sem=recv_sem,
                                 device_id=(lax.axis_index("x"), lax.axis_index("y"), lax.axis_index("c")),
                                 device_id_type=MESH).wait_recv()


def _half_slab(ref, spec, chip, half):
    _, rows, cols, axis, size = spec
    haxis, hsize = _half_spec(rows, cols, axis)
    return _slab(_slab(ref, axis, chip, size), haxis, half, hsize)


def _gather_sends(outs, specs, ici_send, ici_recv, want="both"):
    x, y, c, p = _place()
    sent = []
    for wi, spec in enumerate(specs):
        mine = _half_slab(outs[wi], spec, p, c)
        for k, (fx, fy) in enumerate(CHIP_FLIPS):
            sent.append(_chunked(lambda s, d, j=3 * wi + k, fx=fx, fy=fy: pltpu.make_async_remote_copy(
                src_ref=s, dst_ref=d, send_sem=ici_send.at[j], recv_sem=ici_recv.at[j],
                device_id=(_flip(x, fx), _flip(y, fy), c), device_id_type=MESH), mine, mine, want))
    return sent


def _gather_weights(fulls, specs, *, name):
    n = len(specs)

    def body(*refs):
        outs = refs[n:2 * n]
        ici_send, ici_recv, load_sems, d2d_send, d2d_recv = refs[2 * n:]
        x, y, c, _ = _place()
        sent = _gather_sends(outs, specs, ici_send, ici_recv)
        for parts, _ in sent:
            for cp in parts:
                cp.start()
        for k, (fx, fy) in enumerate(CHIP_FLIPS):
            q = 2 * _flip(x, fx) + _flip(y, fy)
            for wi in range(n):
                sent[3 * wi + k][1].wait_recv()
                got = _half_slab(outs[wi], specs[wi], q, c)
                _staged(got, got, None, (x, y, 1 - c), load_sems, d2d_send, None, d2d_recv.at[3 * wi + k])
        for _, whole in sent:
            whole.wait_send()
        for k, (fx, fy) in enumerate(CHIP_FLIPS):
            q = 2 * _flip(x, fx) + _flip(y, fy)
            for wi in range(n):
                _landed(_half_slab(outs[wi], specs[wi], q, 1 - c), d2d_recv.at[3 * wi + k], d2d_send.at[0])

    return pl.pallas_call(
        body, name=name, in_specs=[ANY] * n, out_specs=[ANY] * n,
        out_shape=[jax.ShapeDtypeStruct(f.shape, f.dtype) for f in fulls],
        input_output_aliases={i: i for i in range(n)},
        scratch_shapes=[pltpu.SemaphoreType.DMA((3 * n,)), pltpu.SemaphoreType.DMA((3 * n,)),
                        pltpu.SemaphoreType.DMA((2,)), pltpu.SemaphoreType.DMA((2,)),
                        pltpu.SemaphoreType.DMA((3 * n,))],
    )(*fulls)


def _gather_chips_carry(fulls, specs):
    n = len(specs)

    def issue(ins, outs, sems, want):
        del ins
        return [(parts, whole, True) for parts, whole in _gather_sends(outs, specs, sems[0], sems[1], want)]

    return _Carry(list(fulls), [jax.ShapeDtypeStruct(f.shape, f.dtype) for f in fulls], {i: i for i in range(n)},
                  [pltpu.SemaphoreType.DMA((3 * n,)), pltpu.SemaphoreType.DMA((3 * n,))], issue)


def _gather_pass_on(fulls, specs, *, name):
    n = len(specs)

    def body(*refs):
        outs = refs[n:2 * n]
        load_sems, d2d_send, d2d_recv = refs[2 * n:2 * n + 3]
        bufs = refs[2 * n + 3:]
        x, y, c, _ = _place()
        loads, sends = [], []
        for k, (fx, fy) in enumerate(CHIP_FLIPS):
            q = 2 * _flip(x, fx) + _flip(y, fy)
            for wi in range(n):
                j = 3 * wi + k
                got = _half_slab(outs[wi], specs[wi], q, c)
                loads.append(pltpu.make_async_copy(got, bufs[j], load_sems.at[j]))
                sends.append(pltpu.make_async_remote_copy(
                    src_ref=bufs[j], dst_ref=got, send_sem=d2d_send.at[j], recv_sem=d2d_recv.at[j],
                    device_id=(x, y, 1 - c), device_id_type=MESH))
        for cp in loads:
            cp.start()
        for load, send in zip(loads, sends):
            load.wait()
            send.start()
        for cp in sends:
            cp.wait_send()
        for k, (fx, fy) in enumerate(CHIP_FLIPS):
            q = 2 * _flip(x, fx) + _flip(y, fy)
            for wi in range(n):
                _landed(_half_slab(outs[wi], specs[wi], q, 1 - c), d2d_recv.at[3 * wi + k], d2d_send.at[0])

    shapes = []
    for _, rows, cols, axis, size in specs:
        shapes += [(rows // 2, size) if axis == 1 else (size, cols // 2)] * 3
    assert sum(math.prod(s) for s in shapes) * 2 <= VMEM_LIMIT_V7X // 2
    stages = [pltpu.VMEM(s, BF16) for s in shapes]
    return pl.pallas_call(
        body, name=name, in_specs=[ANY] * n, out_specs=[ANY] * n,
        out_shape=[jax.ShapeDtypeStruct(f.shape, f.dtype) for f in fulls],
        input_output_aliases={i: i for i in range(n)},
        scratch_shapes=[pltpu.SemaphoreType.DMA((3 * n,))] * 3 + stages,
        compiler_params=pltpu.CompilerParams(vmem_limit_bytes=VMEM_LIMIT_V7X),
    )(*fulls)


def _sibling_exchange(grads, *, name):
    n = len(grads)

    def body(*refs):
        ins, outs = refs[:n], refs[n:2 * n]
        load_sems, send_sems, recv_sems = refs[2 * n:]
        x, y, c, _ = _place()
        for i, (_, haxis, hsize) in enumerate(grads):
            _staged(_slab(ins[i], haxis, 1 - c, hsize), outs[i], None, (x, y, 1 - c),
                    load_sems, send_sems, None, recv_sems.at[i])
        for i in range(n):
            _landed(outs[i], recv_sems.at[i], send_sems.at[0])

    shapes = [jax.ShapeDtypeStruct((hsize, g.shape[1]) if haxis == 0 else (g.shape[0], hsize), g.dtype)
              for g, haxis, hsize in grads]
    return pl.pallas_call(
        body, name=name, in_specs=[ANY] * n, out_specs=[ANY] * n, out_shape=shapes,
        scratch_shapes=[pltpu.SemaphoreType.DMA((2,)), pltpu.SemaphoreType.DMA((2,)),
                        pltpu.SemaphoreType.DMA((n,))],
    )(*[g for g, _, _ in grads])


def _chip_exchange_carry(parts):
    n = len(parts)

    def issue(ins, outs, sems, want):
        send_sems, recv_sems, local_sems = sems
        x, y, c, p = _place()
        copies = []
        for i, (_, axis, size) in enumerate(parts):
            copies.append(_chunked(lambda s, d, i=i: pltpu.make_async_copy(s, d, local_sems.at[i]),
                                   _slab(ins[i], axis, p, size), outs[i].at[p], want) + (False,))
            for k, (fx, fy) in enumerate(CHIP_FLIPS):
                px, py = _flip(x, fx), _flip(y, fy)
                copies.append(_chunked(lambda s, d, j=3 * i + k, px=px, py=py: pltpu.make_async_remote_copy(
                    src_ref=s, dst_ref=d, send_sem=send_sems.at[j], recv_sem=recv_sems.at[j],
                    device_id=(px, py, c), device_id_type=MESH),
                    _slab(ins[i], axis, 2 * px + py, size), outs[i].at[p], want) + (True,))
        return copies

    shapes = []
    for a, axis, size in parts:
        shapes.append(jax.ShapeDtypeStruct((4, size, a.shape[1]) if axis == 0 else (4, a.shape[0], size), a.dtype))
    return _Carry([a for a, _, _ in parts], shapes, {},
                  [pltpu.SemaphoreType.DMA((3 * n,)), pltpu.SemaphoreType.DMA((3 * n,)),
                   pltpu.SemaphoreType.DMA((n,))], issue)


def _sibling_share(sums):
    n = len(sums)

    def body(*refs):
        ins, outs = refs[:n], refs[n:2 * n]
        load_sems, send_sems, store_sems, recv_sems = refs[2 * n:]
        x, y, c, _ = _place()
        for i, (s, haxis) in enumerate(sums):
            place = _slab(outs[i], haxis, c, s.shape[haxis])
            _staged(ins[i], place, place, (x, y, 1 - c), load_sems, send_sems, store_sems, recv_sems.at[i])
        for i, (s, haxis) in enumerate(sums):
            _landed(_slab(outs[i], haxis, 1 - c, s.shape[haxis]), recv_sems.at[i], send_sems.at[0])

    shapes = []
    for s, haxis in sums:
        r, cc = s.shape
        shapes.append(jax.ShapeDtypeStruct((2 * r, cc) if haxis == 0 else (r, 2 * cc), F32))
    return pl.pallas_call(
        body, name="grad_sibling_share", in_specs=[ANY] * n, out_specs=[ANY] * n, out_shape=shapes,
        scratch_shapes=[pltpu.SemaphoreType.DMA((2,)), pltpu.SemaphoreType.DMA((2,)),
                        pltpu.SemaphoreType.DMA((2,)), pltpu.SemaphoreType.DMA((n,))],
    )(*[s for s, _ in sums])


class _Exchanges:
    def __init__(self, fulls):
        self.w_in = _gather_weights([fulls["w_in"]], W_SPECS[:1], name="gather_w_in")[0]
        self.late_specs = W_SPECS[1:]
        self.late = [fulls[s[0]] for s in self.late_specs]
        self.slots = {}

    def late_carry(self):
        return _gather_chips_carry(self.late, self.late_specs)

    def late_weights(self, carried):
        done = _gather_pass_on(carried, self.late_specs, name="gather_pass_on")
        return dict(zip([s[0] for s in self.late_specs], done))

    def _half_sums(self, gw, gwb, specs, tag):
        grads, sent = [], []
        for name, _, _, axis, _ in specs:
            for part in (("w_ffn_a", "w_ffn_b") if name == "w_ffn_in" else (name,)):
                half = _half_spec(gw[part].shape[0], gw[part].shape[1], axis)
                grads.append((gw[part],) + half)
                sent.append((gwb[part],) + half)
        theirs = _sibling_exchange(sent, name=f"grad_sibling_exchange_{tag}")
        parts, j = [], 0
        for name, _, _, axis, size in specs:
            take = 2 if name == "w_ffn_in" else 1
            items = [grads[i] + (theirs[i],) for i in range(j, j + take)]
            parts.append((_add_halves(items, name=f"half_sum_{name}"), axis, size))
            j += take
        return _chip_exchange_carry(parts)

    def early_carry(self, gw, gwb):
        return self._half_sums(gw, gwb, self.late_specs, "early")

    def early_done(self, carried):
        self.slots.update(zip([s[0] for s in self.late_specs], carried))

    def w_in_carry(self, g, gb):
        return self._half_sums({"w_in": g}, {"w_in": gb}, W_SPECS[:1], "w_in")

    def w_in_done(self, carried):
        self.slots["w_in"] = carried[0]

    def reduced(self):
        sums = []
        for name, rows, cols, axis, _ in W_SPECS:
            sums.append((_add_slots(self.slots[name], name=f"chip_sum_{name}"), _half_spec(rows, cols, axis)[0]))
        return dict(zip([s[0] for s in W_SPECS], _sibling_share(sums)))


def _small_allreduce(sv):
    def body(sv_ref, o_ref, slots_ref, send_sems, recv_sems):
        x, y, c, _ = _place()
        me = 4 * x + 2 * y + c
        slots_ref[me] = sv_ref[...]
        copies = []
        for k in range(1, 8):
            fx, fy, fc = (k >> 2) & 1, (k >> 1) & 1, k & 1
            copies.append(pltpu.make_async_remote_copy(
                src_ref=sv_ref, dst_ref=slots_ref.at[me], send_sem=send_sems.at[k - 1],
                recv_sem=recv_sems.at[k - 1], device_id=(_flip(x, fx), _flip(y, fy), _flip(c, fc)),
                device_id_type=MESH))
        for cp in copies:
            cp.start()
        for cp in copies:
            cp.wait_recv()
        for cp in copies:
            cp.wait_send()
        total = slots_ref[0]
        for s in range(1, 8):
            total = total + slots_ref[s]
        o_ref[...] = total

    vm = pl.BlockSpec(memory_space=pltpu.VMEM)
    return pl.pallas_call(
        body, name="small_allreduce", in_specs=[vm], out_specs=vm,
        out_shape=jax.ShapeDtypeStruct(sv.shape, F32),
        scratch_shapes=[pltpu.VMEM((8,) + sv.shape, F32), pltpu.SemaphoreType.DMA((7,)),
                        pltpu.SemaphoreType.DMA((7,))],
    )(sv)


SMALL_ROWS = (("norm_mix_gain", 0), ("norm_mem_gain", 1), ("norm_ffn_gain", 2))
SMALL_LB = (("lb_logits_fw", 3), ("lb_logits_bw", 5))
SMALL_HEAD = ("hg_norm_gain", "da_q_gain", "da_k_gain", "mem_q_gain", "mem_k_gain")


def _pack_small(d):
    last = jnp.concatenate([d[n] for n in SMALL_HEAD] + [jnp.zeros((1, D_MODEL - HEAD * len(SMALL_HEAD)), F32)], axis=1)
    return jnp.concatenate([d["norm_mix_gain"], d["norm_mem_gain"], d["norm_ffn_gain"],
                            d["lb_logits_fw"], d["lb_logits_bw"], last], axis=0)


def _unpack_small(a):
    out = {n: a[r:r + 1] for n, r in SMALL_ROWS}
    out.update({n: a[r:r + 2] for n, r in SMALL_LB})
    out.update({n: a[7:8, j * HEAD:(j + 1) * HEAD] for j, n in enumerate(SMALL_HEAD)})
    return out


PARAM_ORDER = ("norm_mix_gain", "norm_mem_gain", "w_in", "lb_logits_fw", "lb_logits_bw", "hg_norm_gain",
               "da_q_gain", "da_k_gain", "w_mem_kv", "mem_q_gain", "mem_k_gain", "w_proj_hg", "w_proj_da",
               "w_proj_mem", "w_out", "norm_ffn_gain", "w_ffn_in", "w_ffn_out")


def kernel(x, mem, norm_mix_gain, norm_mem_gain, w_in, lb_logits_fw, lb_logits_bw, hg_norm_gain, da_q_gain, da_k_gain, w_mem_kv, mem_q_gain, mem_k_gain, w_proj_hg, w_proj_da, w_proj_mem, w_out, norm_ffn_gain, w_ffn_in, w_ffn_out, loss_target, m_norm_mix_gain, m_norm_mem_gain, m_w_in, m_lb_logits_fw, m_lb_logits_bw, m_hg_norm_gain, m_da_q_gain, m_da_k_gain, m_w_mem_kv, m_mem_q_gain, m_mem_k_gain, m_w_proj_hg, m_w_proj_da, m_w_proj_mem, m_w_out, m_norm_ffn_gain, m_w_ffn_in, m_w_ffn_out, v_norm_mix_gain, v_norm_mem_gain, v_w_in, v_lb_logits_fw, v_lb_logits_bw, v_hg_norm_gain, v_da_q_gain, v_da_k_gain, v_w_mem_kv, v_mem_q_gain, v_mem_k_gain, v_w_proj_hg, v_w_proj_da, v_w_proj_mem, v_w_out, v_norm_ffn_gain, v_w_ffn_in, v_w_ffn_out):
    args = dict(locals())
    mats = tuple(s[0] for s in W_SPECS)
    flat = lambda a: a.reshape(a.shape[-2:])
    w = {n: flat(args[n]) for n in mats}
    m = {n: flat(args["m_" + n]) for n in mats}
    v = {n: flat(args["v_" + n]) for n in mats}
    small = {n: args[n] for n in PARAM_ORDER if n not in mats}

    chip = (2 * lax.axis_index("x") + lax.axis_index("y")).astype(jnp.int32).reshape(1)
    ex = _Exchanges({n: _cast_into_full(w[n], chip, rows, cols, axis, name=f"cast_{n}")
                     for n, rows, cols, axis, _ in W_SPECS})
    loss, grad_x, small_grads = _local_step(x[0], mem[0], loss_target[0], small, ex)
    grads = ex.reduced()
    small_sum = _small_allreduce(small_grads)

    delta, new_m, new_v = {}, {}, {}
    for n in mats:
        delta[n], new_m[n], new_v[n] = _adamw(w[n], grads[n], m[n], v[n], name=f"adamw_{n}")
    packed = _adamw(_pack_small(small), small_sum,
                    _pack_small({n: args["m_" + n] for n in small}),
                    _pack_small({n: args["v_" + n] for n in small}), name="adamw_small")
    grads.update(_unpack_small(small_sum))
    for dst, src in zip((delta, new_m, new_v), packed):
        dst.update(_unpack_small(src))

    def shaped(d, n):
        return d[n].reshape(args[n].shape)

    loss_sum = lax.psum(loss[0, 0], ("x", "y", "c"))
    return (loss_sum, grad_x[None], *[shaped(grads, n) for n in PARAM_ORDER], *[shaped(delta, n) for n in PARAM_ORDER],
            *[shaped(new_m, n) for n in PARAM_ORDER], *[shaped(new_v, n) for n in PARAM_ORDER])
```

```python
import functools
import math

import numpy as np
import jax
import jax.numpy as jnp
from jax import lax
from jax.experimental import pallas as pl
from jax.experimental.pallas import tpu as pltpu

F32 = jnp.float32
BF16 = jnp.bfloat16
MESH = pl.DeviceIdType.MESH

D_MODEL = 1024
HEAD = 128
HG_HEADS = 8
DA_GROUPS = ((1, 64), (4, 64), (16, 64))
DA_GH = 4
MEM_HEADS = 4
N_MEM = 256
D_FF = 2816
CHUNK = 64
BAND_QBLOCKS = 8
GLA_HEADS_PER_STEP = 4
RMS_EPS = 1e-6
NEG_INF = -1e30
HG_SCALE = HEAD ** -0.5
ATT_SCALE = HEAD ** -0.5
VMEM_LIMIT_V7X = 48 * 1024 * 1024

C_HQ, C_FF, C_FB, C_HI, C_HG = 0, 1024, 2048, 3072, 4096
C_DQ, C_DK, C_DV, C_MQ = 5120, 6656, 8192, 9728
C_GHG, C_GDA, C_GMEM = 10240, 11264, 12288
IN_COLS = 13312

ADAM_LR, ADAM_B1, ADAM_B2, ADAM_EPS, ADAM_WD, ADAM_STEP = 0.001, 0.9, 0.999, 1e-08, 0.01, 10


def _params(sem, vmem=VMEM_LIMIT_V7X):
    return pltpu.CompilerParams(dimension_semantics=sem, vmem_limit_bytes=vmem)


def _dot(a, b):
    return jnp.dot(a.astype(BF16), b.astype(BF16), preferred_element_type=F32)


def _dot_nt(a, b):
    return lax.dot_general(a.astype(BF16), b.astype(BF16), (((1,), (1,)), ((), ())),
                           preferred_element_type=F32)


def _dot_tn(a, b):
    return lax.dot_general(a.astype(BF16), b.astype(BF16), (((0,), (0,)), ((), ())),
                           preferred_element_type=F32)


def _sigmoid(v):
    return jax.nn.sigmoid(v.astype(F32))


def _ones(rows, cols):
    return (lax.broadcasted_iota(jnp.int32, (rows, cols), 0) >= 0).astype(BF16)


def _lane_sum(v):
    ones = _ones(HEAD, HEAD)
    hi = v.astype(BF16)
    mid = (v - hi.astype(F32)).astype(BF16)
    return jnp.dot(hi, ones, preferred_element_type=F32) + jnp.dot(mid, ones, preferred_element_type=F32)


def _row_mean(v):
    if v.shape[-1] == HEAD:
        return _lane_sum(v) * (1.0 / HEAD)
    return jnp.mean(v, axis=-1, keepdims=True)


def _rms(v):
    v = v.astype(F32)
    r = lax.rsqrt(_row_mean(v * v) + RMS_EPS)
    return v * r, r


def _rms_bwd(dy, xhat, r, gain):
    dxh = dy * gain
    dx = r * (dxh - xhat * _row_mean(dxh * xhat))
    return dx, dy * xhat


class _Carry:
    def __init__(self, arrays, out_shapes, aliases, sems, issue):
        self.arrays, self.out_shapes, self.aliases, self.sems, self.issue = arrays, out_shapes, aliases, sems, issue


NO_CARRY = object()


def _start_copies(copies):
    for parts, _, _ in copies:
        for cp in parts:
            cp.start()


def _wait_copies(copies):
    for _, whole, remote in copies:
        if remote:
            whole.wait_recv()
    for _, whole, remote in copies:
        if remote:
            whole.wait_send()
        else:
            whole.wait()


def _call(body, *, name, grid, in_specs, out_specs, out_shape, scratch_shapes, semantics, ins, carry=None,
          aliases=None):
    aliases = dict(aliases or {})
    if carry is None:
        res = pl.pallas_call(body, name=name, grid=grid, in_specs=in_specs, out_specs=out_specs, out_shape=out_shape,
                             scratch_shapes=scratch_shapes, input_output_aliases=aliases,
                             compiler_params=_params(semantics))(*ins)
        return list(res), []
    n_in, n_out, n_scr = len(ins), len(out_shape), len(scratch_shapes)
    c_in, c_out = len(carry.arrays), len(carry.out_shapes)

    def wrapped(*refs):
        pos = [0]

        def take(count):
            pos[0] += count
            return refs[pos[0] - count:pos[0]]

        own_in, carry_in = take(n_in), take(c_in)
        own_out, carry_out = take(n_out), take(c_out)
        own_scr, carry_sems = take(n_scr), take(len(carry.sems))
        ids = [pl.program_id(a) for a in range(len(grid))]
        first, last = ids[0] == 0, ids[0] == grid[0] - 1
        for a in range(1, len(grid)):
            first, last = first & (ids[a] == 0), last & (ids[a] == grid[a] - 1)

        @pl.when(first)
        def _():
            _start_copies(carry.issue(carry_in, carry_out, carry_sems, "start"))

        body(*own_in, *own_out, *own_scr)

        @pl.when(last)
        def _():
            _wait_copies(carry.issue(carry_in, carry_out, carry_sems, "wait"))

    res = pl.pallas_call(
        wrapped, name=name, grid=grid, in_specs=list(in_specs) + [ANY] * c_in,
        out_specs=list(out_specs) + [ANY] * c_out, out_shape=list(out_shape) + list(carry.out_shapes),
        input_output_aliases={**aliases, **{n_in + i: n_out + j for i, j in carry.aliases.items()}},
        scratch_shapes=list(scratch_shapes) + list(carry.sems),
        compiler_params=_params(("arbitrary",) * len(grid)))(*ins, *carry.arrays)
    return list(res[:n_out]), list(res[n_out:])


def _matmul(pairs, mode, out_dtype, *, tm, tn, tk=None, residual=None, also=None, carry=NO_CARRY, name):
    b_offs = [pr[2] if len(pr) > 2 else 0 for pr in pairs]
    pairs = [pr[:2] for pr in pairs]
    a0, b0 = pairs[0]
    if mode == "nn":
        (m, kk), n = a0.shape, b0.shape[1]
    elif mode == "nt":
        (m, kk), n = a0.shape, b0.shape[0]
    else:
        (kk, m), n = a0.shape, b0.shape[1]
    assert mode == "nt" or not any(b_offs)
    tm, tn = min(tm, m), min(tn, n)
    tk = kk if tk is None else tk
    nk = kk // tk
    assert m % tm == 0 and n % tn == 0 and kk % tk == 0, (name, m, n, kk)
    n_p = len(pairs)
    if mode == "tn":
        a_spec = pl.BlockSpec((tk, tm), lambda i, j, k: (k, i))
    else:
        a_spec = pl.BlockSpec((tm, tk), lambda i, j, k: (i, k))
    if mode == "nt":
        b_specs = [pl.BlockSpec((tn, tk), lambda i, j, k, o=o: (j, o * nk + k)) for o in b_offs]
    else:
        b_specs = [pl.BlockSpec((tk, tn), lambda i, j, k: (k, j))] * n_p
    o_spec = pl.BlockSpec((tm, tn), lambda i, j, k: (i, j))
    dot = {"nn": _dot, "nt": _dot_nt, "tn": _dot_tn}[mode]
    has_res = residual is not None

    def body(*refs):
        a_refs, b_refs = refs[:n_p], refs[n_p:2 * n_p]
        pos = 2 * n_p
        res_ref = refs[pos] if has_res else None
        pos += int(has_res)
        o_ref = refs[pos]
        pos += int(also is not None)
        part = dot(a_refs[0][...], b_refs[0][...])
        for a_r, b_r in zip(a_refs[1:], b_refs[1:]):
            part += dot(a_r[...], b_r[...])

        def finish(total):
            if has_res:
                total = total + res_ref[...]
            o_ref[...] = total.astype(out_dtype)
            if also is not None:
                refs[pos][...] = total.astype(also)

        if nk == 1:
            finish(part)
        else:
            acc_ref = refs[pos + 1]
            k = pl.program_id(2)

            @pl.when(k == 0)
            def _():
                acc_ref[...] = part

            @pl.when(k > 0)
            def _():
                acc_ref[...] += part

            @pl.when(k == nk - 1)
            def _():
                finish(acc_ref[...])

    ins = [a for a, _ in pairs] + [b for _, b in pairs]
    in_specs = [a_spec] * n_p + b_specs
    if has_res:
        ins.append(residual)
        in_specs.append(o_spec)
    dtypes = [out_dtype] + ([also] if also is not None else [])
    outs, carried = _call(
        body, name=name, grid=(m // tm, n // tn, nk), in_specs=in_specs, out_specs=[o_spec] * len(dtypes),
        out_shape=[jax.ShapeDtypeStruct((m, n), dt) for dt in dtypes],
        scratch_shapes=[pltpu.VMEM((tm, tn), F32)] if nk > 1 else [],
        semantics=("parallel", "parallel", "arbitrary"), ins=ins, carry=None if carry is NO_CARRY else carry)
    out = outs[0] if also is None else tuple(outs)
    return out if carry is NO_CARRY else (out, carried)


def _rms_fwd(x, gain, *, name):
    rows, dm = x.shape
    tm = min(1024, rows)

    def body(x_ref, g_ref, h_ref):
        xhat, _ = _rms(x_ref[...])
        h_ref[...] = (xhat * g_ref[...]).astype(BF16)

    return pl.pallas_call(
        body, name=name, grid=(rows // tm,),
        in_specs=[pl.BlockSpec((tm, dm), lambda i: (i, 0)), pl.BlockSpec((1, dm), lambda i: (0, 0))],
        out_specs=pl.BlockSpec((tm, dm), lambda i: (i, 0)),
        out_shape=jax.ShapeDtypeStruct((rows, dm), BF16),
        compiler_params=_params(("parallel",)),
    )(x, gain)


def _rms_bwd_rows(dh, x, gain, dres, dx_dtypes, *, name):
    rows, dm = x.shape
    tm = min(512, rows)
    has_res = dres is not None

    def body(*refs):
        dh_ref, x_ref, g_ref = refs[:3]
        res_ref = refs[3] if has_res else None
        outs = refs[3 + int(has_res):]
        dg_ref = outs[-1]
        xhat, r = _rms(x_ref[...])
        dx, dgr = _rms_bwd(dh_ref[...], xhat, r, g_ref[...])
        if has_res:
            dx = dx + res_ref[...]
        for dx_ref, dt in zip(outs, dx_dtypes):
            dx_ref[...] = dx.astype(dt)

        @pl.when(pl.program_id(0) == 0)
        def _():
            dg_ref[...] = jnp.zeros_like(dg_ref)

        dg_ref[...] += jnp.sum(dgr, axis=0, keepdims=True)

    row = pl.BlockSpec((tm, dm), lambda i: (i, 0))
    vec = pl.BlockSpec((1, dm), lambda i: (0, 0))
    return pl.pallas_call(
        body, name=name, grid=(rows // tm,),
        in_specs=[row, row, vec] + ([row] if has_res else []),
        out_specs=[row] * len(dx_dtypes) + [vec],
        out_shape=[jax.ShapeDtypeStruct((rows, dm), dt) for dt in dx_dtypes] + [jax.ShapeDtypeStruct((1, dm), F32)],
        compiler_params=_params(("arbitrary",)),
    )(*([dh, x, gain] + ([dres] if has_res else [])))


def _out_norm(a, w, x, gain, *, name):
    rows, dm = x.shape
    tm = min(512, rows)

    def body(a_ref, w_ref, x_ref, g_ref, x1_ref, h_ref):
        x1 = x_ref[...] + _dot(a_ref[...], w_ref[...])
        x1_ref[...] = x1
        xhat, _ = _rms(x1)
        h_ref[...] = (xhat * g_ref[...]).astype(BF16)

    row = lambda wd: pl.BlockSpec((tm, wd), lambda i: (i, 0))
    return pl.pallas_call(
        body, name=name, grid=(rows // tm,),
        in_specs=[row(a.shape[1]), pl.BlockSpec(w.shape, lambda i: (0, 0)), row(dm), pl.BlockSpec((1, dm), lambda i: (0, 0))],
        out_specs=[row(dm), row(dm)],
        out_shape=[jax.ShapeDtypeStruct((rows, dm), F32), jax.ShapeDtypeStruct((rows, dm), BF16)],
        compiler_params=_params(("parallel",)),
    )(a, w, x, gain)


def _out_loss(u, w, x1, tgt, *, name):
    rows, dm = x1.shape
    tm = min(256, rows)
    steps = rows // tm

    def body(u_ref, w_ref, x_ref, t_ref, dy_ref, dyb_ref, loss_ref, acc_ref):
        i = pl.program_id(0)
        diff = x_ref[...] + _dot(u_ref[...], w_ref[...]) - t_ref[...]
        dy = diff * (1.0 / dm)
        dy_ref[...] = dy
        dyb_ref[...] = dy.astype(BF16)

        @pl.when(i == 0)
        def _():
            acc_ref[...] = jnp.zeros_like(acc_ref)

        acc_ref[...] += jnp.sum(diff * diff, axis=0, keepdims=True)

        @pl.when(i == steps - 1)
        def _():
            loss_ref[...] = jnp.full((1, HEAD), 0.5 / dm, F32) * jnp.sum(acc_ref[...])

    row = lambda wd: pl.BlockSpec((tm, wd), lambda i: (i, 0))
    return pl.pallas_call(
        body, name=name, grid=(steps,),
        in_specs=[row(u.shape[1]), pl.BlockSpec(w.shape, lambda i: (0, 0)), row(dm), row(dm)],
        out_specs=[row(dm), row(dm), pl.BlockSpec((1, HEAD), lambda i: (0, 0))],
        out_shape=[jax.ShapeDtypeStruct((rows, dm), F32), jax.ShapeDtypeStruct((rows, dm), BF16),
                   jax.ShapeDtypeStruct((1, HEAD), F32)],
        scratch_shapes=[pltpu.VMEM((1, dm), F32)],
        compiler_params=_params(("arbitrary",)),
    )(u, w, x1, tgt)


def _gla_block_terms(q_raw, f_logit, lb, rev):
    sig = _sigmoid(f_logit)
    forget = lb + (1.0 - lb) * sig
    k = 1.0 - forget
    b = _chunk_cumsum(jnp.log(forget), rev)
    qs = _sigmoid(q_raw)
    eb = jnp.exp(b)
    emb = jnp.exp(-b)
    qt = (q_raw * qs * HG_SCALE) * eb
    kt = k * emb
    return sig, forget, k, b, qs, eb, emb, qt, kt


def _chunk_cumsum(v, rev):
    n = v.shape[0]
    pos = lax.broadcasted_iota(jnp.int32, v.shape, 0) & (CHUNK - 1)
    step = 1
    while step < CHUNK:
        if rev:
            shifted, keep = pltpu.roll(v, n - step, 0), pos < CHUNK - step
        else:
            shifted, keep = pltpu.roll(v, step, 0), pos >= step
        v = v + jnp.where(keep, shifted, 0.0)
        step *= 2
    return v


def _tri_mask(n, rev):
    row, col = np.arange(n)[:, None], np.arange(n)[None, :]
    same = (row // CHUNK) == (col // CHUNK)
    return jnp.asarray((same & ((row <= col) if rev else (row >= col))).astype(np.float32))


def _chunk_order(ncb, rev):
    order = range(ncb - 1, -1, -1) if rev else range(ncb)
    return [(c, c * CHUNK if rev else c * CHUNK + CHUNK - 1) for c in order]


def _gla_fwd(proj, lb, *, f_off, rev, name):
    rows = proj.shape[0]
    tb = min(256, rows)
    nb, ncb = rows // tb, tb // CHUNK

    def tmap(n):
        return nb - 1 - n if rev else n

    def body(tri_ref, q_ref, f_ref, v_ref, lb_ref, o_ref, st_ref, s_ref):
        @pl.when(pl.program_id(1) == 0)
        def _():
            s_ref[...] = jnp.zeros_like(s_ref)

        tri = tri_ref[...] > 0.5
        for hh in range(GLA_HEADS_PER_STEP):
            cs = slice(hh * HEAD, (hh + 1) * HEAD)
            v = v_ref[:, cs]
            _, _, k, b, _, _, _, qt, kt = _gla_block_terms(q_ref[:, cs].astype(F32), f_ref[:, cs].astype(F32),
                                                           lb_ref[:, cs], rev)
            o_intra = _dot(jnp.where(tri, _dot_nt(qt, kt), 0.0), v)
            chunks = []
            for c, last in _chunk_order(ncb, rev):
                sl = slice(c * CHUNK, (c + 1) * CHUNK)
                bl = b[last:last + 1, :]
                kh = k[sl] * jnp.exp(bl - b[sl])
                chunks.append((c, sl, jnp.exp(bl), _dot_tn(v[sl], kh)))
            s_t = s_ref[hh]
            for c, sl, ebl, kv in chunks:
                st_ref[hh, c] = s_t
                o_ref[sl, cs] = o_intra[sl] + _dot_nt(qt[sl], s_t)
                s_t = ebl * s_t + kv
            s_ref[hh] = s_t

    hps = GLA_HEADS_PER_STEP
    col = lambda off: pl.BlockSpec((tb, hps * HEAD), lambda h, n: (tmap(n), off // (hps * HEAD) + h))
    return pl.pallas_call(
        body, name=name, grid=(HG_HEADS // hps, nb),
        in_specs=[pl.BlockSpec((tb, tb), lambda h, n: (0, 0)), col(C_HQ), col(f_off), col(C_HI),
                  pl.BlockSpec((1, hps * HEAD), lambda h, n: (0, h))],
        out_specs=[pl.BlockSpec((tb, hps * HEAD), lambda h, n: (tmap(n), h)),
                   pl.BlockSpec((hps, ncb, HEAD, HEAD), lambda h, n: (h, tmap(n), 0, 0))],
        out_shape=[jax.ShapeDtypeStruct((rows, HG_HEADS * HEAD), F32),
                   jax.ShapeDtypeStruct((HG_HEADS, rows // CHUNK, HEAD, HEAD), F32)],
        scratch_shapes=[pltpu.VMEM((hps, HEAD, HEAD), F32)],
        compiler_params=_params(("parallel", "arbitrary")),
    )(_tri_mask(tb, rev), proj, proj, proj, lb)


def _gla_bwd(proj, lb, do, states, prev, dproj, *, f_off, rev, carry=NO_CARRY, name):
    rows = proj.shape[0]
    tb = min(256, rows)
    nb, ncb = rows // tb, tb // CHUNK
    has_prev = prev is not None
    qv_dtype = BF16 if has_prev else F32

    def tmap(n):
        return n if rev else nb - 1 - n

    def body(*refs):
        tri_ref, q_ref, f_ref, v_ref, lb_ref, do_ref, st_ref = refs[:7]
        pq_ref, pv_ref = refs[8:10] if has_prev else (None, None)
        (dq_ref, df_ref, dv_ref, dlb_ref,
         ds_ref, dqt_scr, dk_scr, db_scr, dbl_scr, dv_scr) = refs[8 + 2 * int(has_prev):]

        @pl.when(pl.program_id(1) == 0)
        def _():
            ds_ref[...] = jnp.zeros_like(ds_ref)
            dlb_ref[...] = jnp.zeros_like(dlb_ref)

        tri = tri_ref[...] > 0.5
        for hh in range(GLA_HEADS_PER_STEP):
            cs = slice(hh * HEAD, (hh + 1) * HEAD)
            lbv = lb_ref[:, cs]
            q_raw = q_ref[:, cs].astype(F32)
            v, dout = v_ref[:, cs], do_ref[:, cs].astype(BF16)
            sig, forget, k, b, qs, eb, emb, qt, kt = _gla_block_terms(q_raw, f_ref[:, cs].astype(F32), lbv, rev)
            a = jnp.where(tri, _dot_nt(qt, kt), 0.0)
            da = jnp.where(tri, _dot_nt(dout, v), 0.0)
            dv_intra = _dot_tn(a, dout)
            dqt_intra = _dot(da, kt)
            dkt = _dot_tn(da, qt)
            chunks = []
            for c, last in reversed(_chunk_order(ncb, rev)):
                sl = slice(c * CHUNK, (c + 1) * CHUNK)
                bl = b[last:last + 1, :]
                e = jnp.exp(bl - b[sl])
                s_t = st_ref[hh, c]
                dqt_scr[sl, cs] = dqt_intra[sl] + _dot(dout[sl], s_t)
                chunks.append((sl, jnp.exp(bl), e, k[sl] * e, s_t, _dot_tn(dout[sl], qt[sl])))
            ds_t = ds_ref[hh]
            for sl, ebl, e, kh, s_t, grow in chunks:
                dkh = _dot(v[sl], ds_t)
                dv_scr[sl, cs] = dv_intra[sl] + _dot_nt(kh, ds_t)
                dk_scr[sl, cs] = dkt[sl] * emb[sl] + dkh * e
                khd = kh * dkh
                dbl = jnp.sum(khd, axis=0, keepdims=True) + ebl * jnp.sum(ds_t * s_t, axis=0, keepdims=True)
                db_scr[sl, cs] = khd
                dbl_scr[sl, cs] = jnp.broadcast_to(dbl, (CHUNK, HEAD))
                ds_t = grow + ds_t * ebl
            ds_ref[hh] = ds_t
            dqt = dqt_scr[:, cs]
            dlogf = _chunk_cumsum(qt * dqt - kt * dkt - db_scr[:, cs], not rev) + dbl_scr[:, cs]
            dforget = dlogf / forget - dk_scr[:, cs]
            df_ref[:, cs] = (dforget * (1.0 - lbv) * sig * (1.0 - sig)).astype(BF16)
            dlb_ref[:, cs] += jnp.sum(dforget * (1.0 - sig), axis=0, keepdims=True)
            dqr = dqt * eb * (HG_SCALE * qs * (1.0 + q_raw * (1.0 - qs)))
            dv = dv_scr[:, cs]
            if has_prev:
                dqr = dqr + pq_ref[:, cs]
                dv = dv + pv_ref[:, cs]
            dq_ref[:, cs] = dqr.astype(qv_dtype)
            dv_ref[:, cs] = dv.astype(qv_dtype)

    hps = GLA_HEADS_PER_STEP
    col = lambda off: pl.BlockSpec((tb, hps * HEAD), lambda h, n: (tmap(n), off // (hps * HEAD) + h))
    blk = pl.BlockSpec((tb, hps * HEAD), lambda h, n: (tmap(n), h))
    vec = pl.BlockSpec((1, hps * HEAD), lambda h, n: (0, h))
    wide = HG_HEADS * HEAD
    into = jax.ShapeDtypeStruct(dproj.shape, dproj.dtype)
    outs, carried = _call(
        body, name=name, grid=(HG_HEADS // hps, nb),
        in_specs=[pl.BlockSpec((tb, tb), lambda h, n: (0, 0)), col(C_HQ), col(f_off), col(C_HI), vec, blk,
                  pl.BlockSpec((hps, ncb, HEAD, HEAD), lambda h, n: (h, tmap(n), 0, 0)), ANY]
                 + ([blk, blk] if has_prev else []),
        out_specs=[col(C_HQ) if has_prev else blk, blk if has_prev else col(f_off), blk, vec],
        out_shape=[into if has_prev else jax.ShapeDtypeStruct((rows, wide), qv_dtype),
                   jax.ShapeDtypeStruct((rows, wide), BF16) if has_prev else into,
                   jax.ShapeDtypeStruct((rows, wide), qv_dtype), jax.ShapeDtypeStruct((1, wide), F32)],
        aliases={7: 0 if has_prev else 1},
        scratch_shapes=[pltpu.VMEM((hps, HEAD, HEAD), F32)] + [pltpu.VMEM((tb, hps * HEAD), F32)] * 5,
        semantics=("parallel", "arbitrary"),
        ins=[_tri_mask(tb, rev), proj, proj, proj, lb, do, states, dproj] + (list(prev) if has_prev else []),
        carry=None if carry is NO_CARRY else carry)
    return outs if carry is NO_CARRY else (outs, carried)


def _fill_columns(dproj, parts, col, *, name):
    rows, wd = parts[0].shape
    tm = min(1024, rows)
    n = len(parts)

    def body(*refs):
        o_ref = refs[n + 1]
        for j in range(n):
            o_ref[:, j * wd:(j + 1) * wd] = refs[j][...]

    return pl.pallas_call(
        body, name=name, grid=(rows // tm,),
        in_specs=[pl.BlockSpec((tm, wd), lambda i: (i, 0))] * n + [ANY],
        out_specs=pl.BlockSpec((pl.Element(tm), pl.Element(n * wd)), lambda i: (i * tm, col)),
        out_shape=jax.ShapeDtypeStruct(dproj.shape, dproj.dtype), input_output_aliases={n: 0},
        compiler_params=_params(("parallel",)),
    )(*parts, dproj)


def _hg_out_fwd(o_fw, o_bw, proj, gain, *, name):
    rows = o_fw.shape[0]
    tm = min(1024, rows)
    wide = HG_HEADS * HEAD

    def body(a_ref, b_ref, g_ref, gain_ref, o_ref):
        for h in range(HG_HEADS):
            sl = slice(h * HEAD, (h + 1) * HEAD)
            xhat, _ = _rms(a_ref[:, sl] + b_ref[:, sl])
            gate = g_ref[:, sl].astype(F32)
            o_ref[:, sl] = (xhat * gain_ref[...] * (gate * _sigmoid(gate))).astype(BF16)

    row = pl.BlockSpec((tm, wide), lambda i: (i, 0))
    return pl.pallas_call(
        body, name=name, grid=(rows // tm,),
        in_specs=[row, row, pl.BlockSpec((tm, wide), lambda i: (i, C_HG // wide)),
                  pl.BlockSpec((1, HEAD), lambda i: (0, 0))],
        out_specs=row, out_shape=jax.ShapeDtypeStruct((rows, wide), BF16),
        compiler_params=_params(("parallel",)),
    )(o_fw, o_bw, proj, gain)


def _hg_out_bwd(dout, o_fw, o_bw, proj, gain, dproj, *, name):
    rows = o_fw.shape[0]
    tm = min(512, rows)
    wide = HG_HEADS * HEAD

    def body(d_ref, a_ref, b_ref, g_ref, gain_ref, dp_in, dgate_ref, do_ref, dgain_ref):
        del dp_in

        @pl.when(pl.program_id(0) == 0)
        def _():
            dgain_ref[...] = jnp.zeros_like(dgain_ref)

        dgain = jnp.zeros((1, HEAD), F32)
        for h in range(HG_HEADS):
            sl = slice(h * HEAD, (h + 1) * HEAD)
            xhat, r = _rms(a_ref[:, sl] + b_ref[:, sl])
            gate, dy = g_ref[:, sl].astype(F32), d_ref[:, sl]
            gs = _sigmoid(gate)
            dgate_ref[:, sl] = (dy * xhat * gain_ref[...] * (gs * (1.0 + gate * (1.0 - gs)))).astype(BF16)
            dx, dgr = _rms_bwd(dy * (gate * gs), xhat, r, gain_ref[...])
            do_ref[:, sl] = dx
            dgain = dgain + jnp.sum(dgr, axis=0, keepdims=True)
        dgain_ref[...] += dgain

    row = pl.BlockSpec((tm, wide), lambda i: (i, 0))
    gate = pl.BlockSpec((tm, wide), lambda i: (i, C_HG // wide))
    vec = pl.BlockSpec((1, HEAD), lambda i: (0, 0))
    return pl.pallas_call(
        body, name=name, grid=(rows // tm,),
        in_specs=[row, row, row, gate, vec, ANY],
        out_specs=[gate, row, vec],
        out_shape=[jax.ShapeDtypeStruct(dproj.shape, dproj.dtype), jax.ShapeDtypeStruct((rows, wide), F32),
                   jax.ShapeDtypeStruct((1, HEAD), F32)],
        input_output_aliases={5: 0},
        compiler_params=_params(("arbitrary",)),
    )(dout, o_fw, o_bw, proj, gain, dproj)


def _strided_rows(ref, r, count, d):
    return ref[...] if d == 1 else ref[pl.ds(r, count, stride=d), :]


def _store_strided(ref, r, count, d, val):
    if d == 1:
        ref[...] = val
    else:
        ref[pl.ds(r, count, stride=d), :] = val


def _da_prep(proj, q_gain, k_gain, g, *, name):
    d = DA_GROUPS[g][0]
    rows = proj.shape[0]
    rb = min(1024, rows)
    tn = rb // d

    def body(q_ref, k_ref, v_ref, qg_ref, kg_ref, qo_ref, ko_ref, vo_ref, qf_ref, kf_ref, vf_ref):
        for h in range(DA_GH):
            cs = slice(h * HEAD, (h + 1) * HEAD)
            for src, dst in ((q_ref, qf_ref), (k_ref, kf_ref), (v_ref, vf_ref)):
                dst[...] = src[:, cs].astype(F32)
            for r in range(d):
                qhat, _ = _rms(_strided_rows(qf_ref, r, tn, d))
                khat, _ = _rms(_strided_rows(kf_ref, r, tn, d))
                qo_ref[h, r] = (qhat * qg_ref[...]).astype(BF16)
                ko_ref[h, r] = (khat * kg_ref[...]).astype(BF16)
                vo_ref[h, r] = _strided_rows(vf_ref, r, tn, d).astype(BF16)

    wide = DA_GH * HEAD
    col = lambda off: pl.BlockSpec((rb, wide), lambda i: (i, off // wide + g))
    vec = pl.BlockSpec((1, HEAD), lambda i: (0, 0))
    out = pl.BlockSpec((DA_GH, d, tn, HEAD), lambda i: (0, 0, i, 0))
    shape = jax.ShapeDtypeStruct((DA_GH, d, rows // d, HEAD), BF16)
    return pl.pallas_call(
        body, name=name, grid=(rows // rb,),
        in_specs=[col(C_DQ), col(C_DK), col(C_DV), vec, vec],
        out_specs=[out, out, out], out_shape=[shape, shape, shape],
        scratch_shapes=[pltpu.VMEM((rb, HEAD), F32)] * 3,
        compiler_params=_params(("parallel",)),
    )(proj, proj, proj, q_gain, k_gain)


def _slopes(g):
    idx = np.arange(g * DA_GH + 1, (g + 1) * DA_GH + 1)
    s = (2.0 ** (-8.0 * idx / (DA_GH * len(DA_GROUPS)))).astype(np.float32)
    return jnp.asarray(np.broadcast_to(s[:, None, None], (DA_GH, 8, HEAD)).copy())


def _band_window(ld, t, radius):
    win = min(2 * t, ld)
    assert t // 2 >= radius or win == ld
    return win


def _band_scores(q, k, q0, start, slope, d, radius):
    t, win = q.shape[0], k.shape[0]
    row = lax.broadcasted_iota(jnp.int32, (t, win), 0)
    col = lax.broadcasted_iota(jnp.int32, (t, win), 1)
    rel = jnp.abs((start - q0) + col - row)
    return _dot_nt(q, k) * ATT_SCALE + jnp.where(rel <= radius, -slope * (d * rel).astype(F32), NEG_INF)


def _band_fwd(qr, kr, vr, g, *, name):
    d, radius = DA_GROUPS[g]
    _, _, ld, _ = qr.shape
    t = min(HEAD, ld)
    qb = min(BAND_QBLOCKS, ld // t)
    rps = max(1, min(d, BAND_QBLOCKS // qb))
    win = _band_window(ld, t, radius)

    def body(q_ref, k_ref, v_ref, sl_ref, o_ref, lse_ref):
        i = pl.program_id(2)
        slope = sl_ref[0:1, 0:1]
        for rr in range(rps):
            for j in range(qb):
                sl = slice(j * t, (j + 1) * t)
                q0 = (i * qb + j) * t
                start = pl.multiple_of(jnp.clip(q0 - t // 2, 0, ld - win), t // 2)
                s = _band_scores(q_ref[rr, sl, :], k_ref[rr, pl.ds(start, win), :], q0, start, slope, d, radius)
                m = jnp.max(s, axis=-1, keepdims=True)
                p = jnp.exp(s - m).astype(BF16)
                l = jnp.dot(p, _ones(win, HEAD), preferred_element_type=F32)
                o_ref[rr, sl, :] = _dot(p, v_ref[rr, pl.ds(start, win), :]) / l
                lse_ref[rr, sl, :] = m + jnp.log(l)

    own = pl.BlockSpec((None, rps, qb * t, HEAD), lambda h, r, i: (h, r, i, 0))
    seq = pl.BlockSpec((None, rps, ld, HEAD), lambda h, r, i: (h, r, 0, 0))
    shape = jax.ShapeDtypeStruct(qr.shape, F32)
    return pl.pallas_call(
        body, name=name, grid=(DA_GH, d // rps, ld // (qb * t)),
        in_specs=[own, seq, seq, pl.BlockSpec((None, 8, HEAD), lambda h, r, i: (h, 0, 0))],
        out_specs=[own, own], out_shape=[shape, shape],
        compiler_params=_params(("parallel", "parallel", "parallel")),
    )(qr, kr, vr, _slopes(g))


def _band_bwd(qr, kr, vr, dor, lser, deltar, g, *, name):
    d, radius = DA_GROUPS[g]
    _, _, ld, _ = qr.shape
    t = min(HEAD, ld)
    qb = min(BAND_QBLOCKS, ld // t)
    rps = max(1, min(d, BAND_QBLOCKS // qb))
    win = _band_window(ld, t, radius)

    def body(q_ref, k_ref, v_ref, do_ref, lse_ref, dl_ref, sl_ref, dq_ref, dk_ref, dv_ref):
        i = pl.program_id(2)

        @pl.when(i == 0)
        def _():
            dk_ref[...] = jnp.zeros_like(dk_ref)
            dv_ref[...] = jnp.zeros_like(dv_ref)

        slope = sl_ref[0:1, 0:1]
        for rr in range(rps):
            for j in range(qb):
                sl = slice(j * t, (j + 1) * t)
                q0 = (i * qb + j) * t
                start = pl.multiple_of(jnp.clip(q0 - t // 2, 0, ld - win), t // 2)
                rows = pl.ds(start, win)
                q, dout, k, v = q_ref[rr, sl, :], do_ref[rr, sl, :], k_ref[rr, rows, :], v_ref[rr, rows, :]
                p = jnp.exp(_band_scores(q, k, q0, start, slope, d, radius) - lse_ref[rr, sl, 0:1])
                ds = p * (_dot_nt(dout, v) - dl_ref[rr, sl, 0:1])
                dq_ref[rr, sl, :] = _dot(ds, k) * ATT_SCALE
                dk_ref[rr, rows, :] += _dot_tn(ds, q) * ATT_SCALE
                dv_ref[rr, rows, :] += _dot_tn(p, dout)

    own = pl.BlockSpec((None, rps, qb * t, HEAD), lambda h, r, i: (h, r, i, 0))
    seq = pl.BlockSpec((None, rps, ld, HEAD), lambda h, r, i: (h, r, 0, 0))
    shape = jax.ShapeDtypeStruct(qr.shape, F32)
    return pl.pallas_call(
        body, name=name, grid=(DA_GH, d // rps, ld // (qb * t)),
        in_specs=[own, seq, seq, own, own, own, pl.BlockSpec((None, 8, HEAD), lambda h, r, i: (h, 0, 0))],
        out_specs=[own, seq, seq], out_shape=[shape, shape, shape],
        compiler_params=_params(("parallel", "parallel", "arbitrary")),
    )(qr, kr, vr, dor, lser, deltar, _slopes(g))


def _da_merge(outs, lses, rows, *, name):
    rb = min(1024, rows)
    wide = DA_GH * HEAD

    def body(*refs):
        o_refs, l_refs = refs[0:3], refs[3:6]
        o_ref, lse_ref = refs[6:8]
        on_refs, ln_refs = refs[8:11], refs[11:14]
        for h in range(DA_GH):
            cs = slice(h * HEAD, (h + 1) * HEAD)
            for g, (d, _) in enumerate(DA_GROUPS):
                tn = rb // d
                for r in range(d):
                    _store_strided(on_refs[g], r, tn, d, o_refs[g][h, r])
                    _store_strided(ln_refs[g], r, tn, d, l_refs[g][h, r])
            l0, l1, l2 = ln_refs[0][...], ln_refs[1][...], ln_refs[2][...]
            m = jnp.maximum(jnp.maximum(l0, l1), l2)
            e0, e1, e2 = jnp.exp(l0 - m), jnp.exp(l1 - m), jnp.exp(l2 - m)
            tot = e0 + e1 + e2
            o_ref[:, cs] = (e0 * on_refs[0][...] + e1 * on_refs[1][...] + e2 * on_refs[2][...]) / tot
            lse_ref[:, cs] = m + jnp.log(tot)

    res = lambda d: pl.BlockSpec((DA_GH, d, rb // d, HEAD), lambda i: (0, 0, i, 0))
    nat = pl.BlockSpec((rb, wide), lambda i: (i, 0))
    shape = jax.ShapeDtypeStruct((rows, wide), F32)
    return pl.pallas_call(
        body, name=name, grid=(rows // rb,),
        in_specs=[res(d) for d, _ in DA_GROUPS] * 2,
        out_specs=[nat, nat], out_shape=[shape, shape],
        scratch_shapes=[pltpu.VMEM((rb, HEAD), F32)] * 6,
        compiler_params=_params(("parallel",)),
    )(*outs, *lses)


def _da_bwd_prep(dout, o, lse, *, name):
    rows = o.shape[0]
    rb = min(512, rows)
    wide = DA_GH * HEAD

    def body(d_ref, o_ref, l_ref, *outs):
        d_scr, l_scr, delta_scr = outs[9:12]
        for h in range(DA_GH):
            cs = slice(h * HEAD, (h + 1) * HEAD)
            dv = d_ref[:, cs]
            d_scr[...] = dv
            l_scr[...] = l_ref[:, cs]
            delta_scr[...] = _lane_sum(dv * o_ref[:, cs])
            for g, (d, _) in enumerate(DA_GROUPS):
                tn = rb // d
                for r in range(d):
                    outs[3 * g][h, r] = _strided_rows(d_scr, r, tn, d).astype(BF16)
                    outs[3 * g + 1][h, r] = _strided_rows(l_scr, r, tn, d)
                    outs[3 * g + 2][h, r] = _strided_rows(delta_scr, r, tn, d)

    nat = pl.BlockSpec((rb, wide), lambda i: (i, 0))
    out_specs, out_shape = [], []
    for d, _ in DA_GROUPS:
        for dt in (BF16, F32, F32):
            out_specs.append(pl.BlockSpec((DA_GH, d, rb // d, HEAD), lambda i: (0, 0, i, 0)))
            out_shape.append(jax.ShapeDtypeStruct((DA_GH, d, rows // d, HEAD), dt))
    return pl.pallas_call(
        body, name=name, grid=(rows // rb,),
        in_specs=[nat, nat, nat], out_specs=out_specs, out_shape=out_shape,
        scratch_shapes=[pltpu.VMEM((rb, HEAD), F32)] * 3,
        compiler_params=_params(("parallel",)),
    )(dout, o, lse)


def _da_prep_bwd(dqr, dkr, dvr, proj, q_gain, k_gain, g, *, name):
    d = DA_GROUPS[g][0]
    rows = proj.shape[0]
    rb = min(1024, rows)
    tn = rb // d
    wide = DA_GH * HEAD

    def body(dq_ref, dk_ref, dv_ref, q_ref, k_ref, qg_ref, kg_ref, oq_ref, ok_ref, ov_ref, gq_ref, gk_ref, *nat_refs):
        @pl.when(pl.program_id(0) == 0)
        def _():
            gq_ref[...] = jnp.zeros_like(gq_ref)
            gk_ref[...] = jnp.zeros_like(gk_ref)

        for h in range(DA_GH):
            cs = slice(h * HEAD, (h + 1) * HEAD)
            for j, src in enumerate((dq_ref, dk_ref, dv_ref)):
                for r in range(d):
                    _store_strided(nat_refs[j], r, tn, d, src[h, r])
            ov_ref[:, cs] = nat_refs[2][...].astype(BF16)
            for j, (x_ref, gn_ref, out_ref, acc_ref) in enumerate(((q_ref, qg_ref, oq_ref, gq_ref),
                                                                    (k_ref, kg_ref, ok_ref, gk_ref))):
                xhat, r = _rms(x_ref[:, cs])
                dx, dgr = _rms_bwd(nat_refs[j][...], xhat, r, gn_ref[...])
                out_ref[:, cs] = dx.astype(BF16)
                acc_ref[...] += jnp.sum(dgr, axis=0, keepdims=True)

    res = pl.BlockSpec((DA_GH, d, tn, HEAD), lambda i: (0, 0, i, 0))
    col = lambda off: pl.BlockSpec((rb, wide), lambda i: (i, off // wide + g))
    vec = pl.BlockSpec((1, HEAD), lambda i: (0, 0))
    nat = pl.BlockSpec((rb, wide), lambda i: (i, 0))
    shape = jax.ShapeDtypeStruct((rows, wide), BF16)
    gshape = jax.ShapeDtypeStruct((1, HEAD), F32)
    return pl.pallas_call(
        body, name=name, grid=(rows // rb,),
        in_specs=[res, res, res, col(C_DQ), col(C_DK), vec, vec],
        out_specs=[nat, nat, nat, vec, vec], out_shape=[shape, shape, shape, gshape, gshape],
        scratch_shapes=[pltpu.VMEM((rb, HEAD), F32)] * 3,
        compiler_params=_params(("arbitrary",)),
    )(dqr, dkr, dvr, proj, proj, q_gain, k_gain)


def _mem_fwd(proj, kv, q_gain, k_gain, *, name):
    rows = proj.shape[0]
    tm = min(1024, rows)
    n_mem = kv.shape[0]

    def body(q_ref, k_ref, v_ref, qg_ref, kg_ref, o_ref):
        qhat, _ = _rms(q_ref[...])
        khat, _ = _rms(k_ref[...])
        s = _dot_nt(qhat * qg_ref[...], khat * kg_ref[...]) * ATT_SCALE
        p = jnp.exp(s - jnp.max(s, axis=-1, keepdims=True))
        p = p / jnp.sum(p, axis=-1, keepdims=True)
        o_ref[...] = _dot(p, v_ref[...]).astype(BF16)

    vec = pl.BlockSpec((1, HEAD), lambda i, h: (0, 0))
    return pl.pallas_call(
        body, name=name, grid=(rows // tm, MEM_HEADS),
        in_specs=[pl.BlockSpec((tm, HEAD), lambda i, h: (i, C_MQ // HEAD + h)),
                  pl.BlockSpec((n_mem, HEAD), lambda i, h: (0, h)),
                  pl.BlockSpec((n_mem, HEAD), lambda i, h: (0, MEM_HEADS + h)), vec, vec],
        out_specs=pl.BlockSpec((tm, HEAD), lambda i, h: (i, h)),
        out_shape=jax.ShapeDtypeStruct((rows, MEM_HEADS * HEAD), BF16),
        compiler_params=_params(("parallel", "parallel")),
    )(proj, kv, kv, q_gain, k_gain)


def _mem_bwd(dout, proj, kv, q_gain, k_gain, dproj, *, name):
    rows = proj.shape[0]
    tm = min(1024, rows)
    steps = rows // tm
    n_mem = kv.shape[0]

    def body(d_ref, q_ref, k_ref, v_ref, qg_ref, kg_ref, dp_in, dq_ref, dk_ref, dv_ref, gq_ref, gk_ref, dkn_ref):
        del dp_in
        h, i = pl.program_id(0), pl.program_id(1)

        @pl.when((h == 0) & (i == 0))
        def _():
            gq_ref[...] = jnp.zeros_like(gq_ref)
            gk_ref[...] = jnp.zeros_like(gk_ref)

        @pl.when(i == 0)
        def _():
            dkn_ref[...] = jnp.zeros_like(dkn_ref)
            dv_ref[...] = jnp.zeros_like(dv_ref)

        qhat, rq = _rms(q_ref[...])
        khat, rk = _rms(k_ref[...])
        qn, kn = qhat * qg_ref[...], khat * kg_ref[...]
        s = _dot_nt(qn, kn) * ATT_SCALE
        p = jnp.exp(s - jnp.max(s, axis=-1, keepdims=True))
        p = p / jnp.sum(p, axis=-1, keepdims=True)
        dout = d_ref[...]
        dp = _dot_nt(dout, v_ref[...])
        ds = p * (dp - jnp.sum(p * dp, axis=-1, keepdims=True))
        dv_ref[...] += _dot_tn(p, dout)
        dkn_ref[...] += _dot_tn(ds, qn) * ATT_SCALE
        dq, dgr = _rms_bwd(_dot(ds, kn) * ATT_SCALE, qhat, rq, qg_ref[...])
        dq_ref[...] = dq.astype(BF16)
        gq_ref[...] += jnp.sum(dgr, axis=0, keepdims=True)

        @pl.when(i == steps - 1)
        def _():
            dk, dgk = _rms_bwd(dkn_ref[...], khat, rk, kg_ref[...])
            dk_ref[...] = dk
            gk_ref[...] += jnp.sum(dgk, axis=0, keepdims=True)

    vec = pl.BlockSpec((1, HEAD), lambda h, i: (0, 0))
    memh = pl.BlockSpec((n_mem, HEAD), lambda h, i: (0, h))
    tok = pl.BlockSpec((tm, HEAD), lambda h, i: (i, h))
    mq = pl.BlockSpec((tm, HEAD), lambda h, i: (i, C_MQ // HEAD + h))
    gshape = jax.ShapeDtypeStruct((1, HEAD), F32)
    return pl.pallas_call(
        body, name=name, grid=(MEM_HEADS, steps),
        in_specs=[tok, mq, memh, pl.BlockSpec((n_mem, HEAD), lambda h, i: (0, MEM_HEADS + h)), vec, vec, ANY],
        out_specs=[mq, memh, memh, vec, vec],
        out_shape=[jax.ShapeDtypeStruct(dproj.shape, dproj.dtype),
                   jax.ShapeDtypeStruct((n_mem, MEM_HEADS * HEAD), F32),
                   jax.ShapeDtypeStruct((n_mem, MEM_HEADS * HEAD), F32), gshape, gshape],
        input_output_aliases={6: 0},
        scratch_shapes=[pltpu.VMEM((n_mem, HEAD), F32)],
        compiler_params=_params(("arbitrary", "arbitrary")),
    )(dout, proj, kv, kv, q_gain, k_gain, dproj)


def _branch_fwd(o_hg, o_da, o_mem, proj, wp_hg, wp_da, wp_mem, *, name):
    rows = o_hg.shape[0]
    tm = min(512, rows)

    def body(a_ref, b_ref, c_ref, ga_ref, gb_ref, gc_ref, wa_ref, wb_ref, wc_ref, o_ref):
        merged = _sigmoid(ga_ref[...]) * _dot(a_ref[...], wa_ref[...])
        merged += _sigmoid(gb_ref[...]) * _dot(b_ref[...], wb_ref[...])
        merged += _sigmoid(gc_ref[...]) * _dot(c_ref[...], wc_ref[...])
        o_ref[...] = merged.astype(BF16)

    row = lambda w: pl.BlockSpec((tm, w), lambda i: (i, 0))
    gate = lambda off: pl.BlockSpec((tm, D_MODEL), lambda i: (i, off // D_MODEL))
    full = lambda a: pl.BlockSpec(a.shape, lambda i: (0, 0))
    return pl.pallas_call(
        body, name=name, grid=(rows // tm,),
        in_specs=[row(o_hg.shape[1]), row(o_da.shape[1]), row(o_mem.shape[1]),
                  gate(C_GHG), gate(C_GDA), gate(C_GMEM), full(wp_hg), full(wp_da), full(wp_mem)],
        out_specs=row(D_MODEL), out_shape=jax.ShapeDtypeStruct((rows, D_MODEL), BF16),
        compiler_params=_params(("parallel",)),
    )(o_hg, o_da, o_mem, proj, proj, proj, wp_hg, wp_da, wp_mem)


def _branch_bwd(dm, o_hg, o_da, o_mem, proj, wp_hg, wp_da, wp_mem, *, name):
    rows = o_hg.shape[0]
    tm = min(256, rows)

    def body(dm_ref, a_ref, b_ref, c_ref, ga_ref, gb_ref, gc_ref, wa_ref, wb_ref, wc_ref, dp_ref, *outs):
        dmv = dm_ref[...]
        for j, (o_ref, g_ref, w_ref) in enumerate(((a_ref, ga_ref, wa_ref), (b_ref, gb_ref, wb_ref),
                                                   (c_ref, gc_ref, wc_ref))):
            z = _dot(o_ref[...], w_ref[...])
            gs = _sigmoid(g_ref[...])
            dz = (dmv * gs).astype(BF16)
            dp_ref[:, j * D_MODEL:(j + 1) * D_MODEL] = (dmv * z * gs * (1.0 - gs)).astype(BF16)
            outs[2 * j][...] = dz
            outs[2 * j + 1][...] = _dot_nt(dz, w_ref[...])

    row = lambda w: pl.BlockSpec((tm, w), lambda i: (i, 0))
    gate = lambda off: pl.BlockSpec((tm, D_MODEL), lambda i: (i, off // D_MODEL))
    full = lambda a: pl.BlockSpec(a.shape, lambda i: (0, 0))
    out_specs = [pl.BlockSpec((pl.Element(tm), pl.Element(3 * D_MODEL)), lambda i: (i * tm, C_GHG))]
    out_shape = [jax.ShapeDtypeStruct((rows, IN_COLS), BF16)]
    for o in (o_hg, o_da, o_mem):
        out_specs += [row(D_MODEL), row(o.shape[1])]
        out_shape += [jax.ShapeDtypeStruct((rows, D_MODEL), BF16), jax.ShapeDtypeStruct((rows, o.shape[1]), F32)]
    return pl.pallas_call(
        body, name=name, grid=(rows // tm,),
        in_specs=[row(D_MODEL), row(o_hg.shape[1]), row(o_da.shape[1]), row(o_mem.shape[1]),
                  gate(C_GHG), gate(C_GDA), gate(C_GMEM), full(wp_hg), full(wp_da), full(wp_mem)],
        out_specs=out_specs, out_shape=out_shape,
        compiler_params=_params(("parallel",)),
    )(dm, o_hg, o_da, o_mem, proj, proj, proj, wp_hg, wp_da, wp_mem)


def _ffn_in(h2, w_ab, *, name):
    rows, dff = h2.shape[0], w_ab.shape[1] // 2
    tm, tn = min(2048, rows), 256

    def body(h_ref, wa_ref, wb_ref, a_ref, b_ref, u_ref):
        a = _dot(h_ref[...], wa_ref[...])
        b = _dot(h_ref[...], wb_ref[...])
        a_ref[...] = a.astype(BF16)
        b_ref[...] = b.astype(BF16)
        u_ref[...] = (a * _sigmoid(a) * b).astype(BF16)

    out = pl.BlockSpec((tm, tn), lambda i, j: (i, j))
    return pl.pallas_call(
        body, name=name, grid=(rows // tm, dff // tn),
        in_specs=[pl.BlockSpec((tm, D_MODEL), lambda i, j: (i, 0)),
                  pl.BlockSpec((D_MODEL, tn), lambda i, j: (0, j)),
                  pl.BlockSpec((D_MODEL, tn), lambda i, j: (0, dff // tn + j))],
        out_specs=[out, out, out],
        out_shape=[jax.ShapeDtypeStruct((rows, dff), BF16)] * 3,
        compiler_params=_params(("parallel", "parallel")),
    )(h2, w_ab, w_ab)


def _ffn_act_bwd(dy, w_out, a, b, *, name):
    rows, dff = a.shape
    tm, tn = min(1024, rows), 256

    def body(dy_ref, w_ref, a_ref, b_ref, da_ref, db_ref):
        du = _dot_nt(dy_ref[...], w_ref[...])
        av, bv = a_ref[...].astype(F32), b_ref[...].astype(F32)
        sa = _sigmoid(av)
        da_ref[...] = (du * bv * sa * (1.0 + av * (1.0 - sa))).astype(BF16)
        db_ref[...] = (du * av * sa).astype(BF16)

    tile = pl.BlockSpec((tm, tn), lambda i, j: (i, j))
    return pl.pallas_call(
        body, name=name, grid=(rows // tm, dff // tn),
        in_specs=[pl.BlockSpec((tm, D_MODEL), lambda i, j: (i, 0)),
                  pl.BlockSpec((tn, D_MODEL), lambda i, j: (j, 0)), tile, tile],
        out_specs=[tile, tile],
        out_shape=[jax.ShapeDtypeStruct((rows, dff), BF16), jax.ShapeDtypeStruct((rows, dff), BF16)],
        compiler_params=_params(("parallel", "parallel")),
    )(dy, w_out, a, b)


def _lower_bound(lb_fw, lb_bw, *, name):
    def body(a_ref, b_ref, oa_ref, ob_ref):
        for src, dst in ((a_ref, oa_ref), (b_ref, ob_ref)):
            dst[...] = _sigmoid(src[0:1, :] - src[1:2, :])

    shape = jax.ShapeDtypeStruct((1, lb_fw.shape[1]), F32)
    return pl.pallas_call(body, name=name, out_shape=[shape, shape])(lb_fw, lb_bw)


def _local_step(x, mem, tgt, p, ex):
    rows = x.shape[0]
    lb_fw, lb_bw = _lower_bound(p["lb_logits_fw"], p["lb_logits_bw"], name="lower_bound")

    h = _rms_fwd(x, p["norm_mix_gain"], name="norm_mix")
    w = {"w_in": ex.w_in}
    proj, carried = _matmul([(h, w["w_in"])], "nn", BF16, tm=1024, tn=1024, carry=ex.late_carry(), name="proj_in")
    w.update(ex.late_weights(carried))
    o_fw, st_fw = _gla_fwd(proj, lb_fw, f_off=C_FF, rev=False, name="gla_fwd_fw")
    o_bw, st_bw = _gla_fwd(proj, lb_bw, f_off=C_FB, rev=True, name="gla_fwd_bw")
    o_hg = _hg_out_fwd(o_fw, o_bw, proj, p["hg_norm_gain"], name="hg_out")

    qkv_r, outs, lses = [], [], []
    for g in range(len(DA_GROUPS)):
        qr, kr, vr = _da_prep(proj, p["da_q_gain"], p["da_k_gain"], g, name=f"da_prep{g}")
        og, lg = _band_fwd(qr, kr, vr, g, name=f"band_fwd{g}")
        qkv_r.append((qr, kr, vr))
        outs.append(og)
        lses.append(lg)
    o_da, lse_da = _da_merge(outs, lses, rows, name="da_merge")

    mem_n = _rms_fwd(mem, p["norm_mem_gain"], name="norm_mem")
    kv = _matmul([(mem_n, w["w_mem_kv"])], "nn", F32, tm=256, tn=512, name="mem_kv")
    o_mem = _mem_fwd(proj, kv, p["mem_q_gain"], p["mem_k_gain"], name="mem_attn")

    merged = _branch_fwd(o_hg, o_da, o_mem, proj, w["w_proj_hg"], w["w_proj_da"], w["w_proj_mem"],
                         name="branch_merge")
    x1, h2 = _out_norm(merged, w["w_out"], x, p["norm_ffn_gain"], name="mix_out_norm")
    a, b, u = _ffn_in(h2, w["w_ffn_in"], name="ffn_in")
    dy, dy_b, loss = _out_loss(u, w["w_ffn_out"], x1, tgt, name="ffn_out_loss")

    gw, gwb, gs = {}, {}, {}
    gw["w_ffn_out"], gwb["w_ffn_out"] = _matmul([(u, dy_b)], "tn", F32, tm=256, tn=1024, also=BF16, name="g_ffn_out")
    da, db = _ffn_act_bwd(dy_b, w["w_ffn_out"], a, b, name="ffn_act_bwd")
    gw["w_ffn_a"], gwb["w_ffn_a"] = _matmul([(h2, da)], "tn", F32, tm=512, tn=256, also=BF16, name="g_ffn_a")
    gw["w_ffn_b"], gwb["w_ffn_b"] = _matmul([(h2, db)], "tn", F32, tm=512, tn=256, also=BF16, name="g_ffn_b")
    dh2 = _matmul([(da, w["w_ffn_in"], 0), (db, w["w_ffn_in"], 1)], "nt", F32, tm=256, tn=1024, name="d_h2")
    dx1, dx1_b, gs["norm_ffn_gain"] = _rms_bwd_rows(dh2, x1, p["norm_ffn_gain"], dy, (F32, BF16), name="norm_ffn_bwd")
    gw["w_out"], gwb["w_out"] = _matmul([(merged, dx1_b)], "tn", F32, tm=512, tn=512, also=BF16, name="g_out")
    dmerged = _matmul([(dx1_b, w["w_out"])], "nt", F32, tm=512, tn=512, name="d_merged")
    dproj, dz_hg, do_hg, dz_da, do_da, dz_mem, do_mem = _branch_bwd(
        dmerged, o_hg, o_da, o_mem, proj, w["w_proj_hg"], w["w_proj_da"], w["w_proj_mem"], name="branch_bwd")
    gw["w_proj_hg"], gwb["w_proj_hg"] = _matmul([(o_hg, dz_hg)], "tn", F32, tm=512, tn=512, also=BF16, name="g_proj_hg")
    gw["w_proj_da"], gwb["w_proj_da"] = _matmul([(o_da, dz_da)], "tn", F32, tm=512, tn=512, also=BF16, name="g_proj_da")
    gw["w_proj_mem"], gwb["w_proj_mem"] = _matmul([(o_mem, dz_mem)], "tn", F32, tm=512, tn=512, also=BF16, name="g_proj_mem")

    dproj, dk_mem, dv_mem, gs["mem_q_gain"], gs["mem_k_gain"] = _mem_bwd(
        do_mem, proj, kv, p["mem_q_gain"], p["mem_k_gain"], dproj, name="mem_attn_bwd")
    dkv = jnp.concatenate([dk_mem, dv_mem], axis=1).astype(BF16)
    gw["w_mem_kv"], gwb["w_mem_kv"] = _matmul([(mem_n, dkv)], "tn", F32, tm=512, tn=512, also=BF16, name="g_mem_kv")
    dmem_n = _matmul([(dkv, w["w_mem_kv"])], "nt", F32, tm=256, tn=512, name="d_mem_n")
    (gs["norm_mem_gain"],) = _rms_bwd_rows(dmem_n, mem, p["norm_mem_gain"], None, (), name="norm_mem_bwd")

    prep = _da_bwd_prep(do_da, o_da, lse_da, name="da_bwd_prep")
    d_da, gq_parts, gk_parts = [], [], []
    for g in range(len(DA_GROUPS)):
        qr, kr, vr = qkv_r[g]
        dor, lser, deltar = prep[3 * g:3 * g + 3]
        dqr, dkr, dvr = _band_bwd(qr, kr, vr, dor, lser, deltar, g, name=f"band_bwd{g}")
        dq, dk, dv, gq, gk = _da_prep_bwd(dqr, dkr, dvr, proj, p["da_q_gain"], p["da_k_gain"], g,
                                          name=f"da_prep_bwd{g}")
        d_da.append((dq, dk, dv))
        gq_parts.append(gq)
        gk_parts.append(gk)
    for j, (off, tag) in enumerate(((C_DQ, "q"), (C_DK, "k"), (C_DV, "v"))):
        dproj = _fill_columns(dproj, [t[j] for t in d_da], off, name=f"dproj_fill_da_{tag}")

    dproj, do_gla, gs["hg_norm_gain"] = _hg_out_bwd(do_hg, o_fw, o_bw, proj, p["hg_norm_gain"], dproj,
                                                    name="hg_out_bwd")
    carry_bulk, carry_rest = ex.early_carries(gw, gwb)
    (dq_f, dproj, dv_f, dlb_fw), bulk = _gla_bwd(proj, lb_fw, do_gla, st_fw, None, dproj, f_off=C_FF, rev=False,
                                                 carry=carry_bulk, name="gla_bwd_fw")
    (dproj, dfl_bw, dv_hg, dlb_bw), rest = _gla_bwd(proj, lb_bw, do_gla, st_bw, (dq_f, dv_f), dproj, f_off=C_FB,
                                                    rev=True, carry=carry_rest, name="gla_bwd_bw")
    ex.early_done(bulk, rest)
    dproj = _fill_columns(dproj, [dfl_bw, dv_hg], C_FB, name="dproj_fill")
    g_in, g_in_b = _matmul([(h, dproj)], "tn", F32, tm=512, tn=1024, also=BF16, name="g_in")
    dh, carried = _matmul([(dproj, w["w_in"])], "nt", F32, tm=1024, tn=1024, tk=IN_COLS // 8,
                          carry=ex.w_in_carry(g_in, g_in_b), name="d_h")
    ex.w_in_done(carried)
    grad_x, gs["norm_mix_gain"] = _rms_bwd_rows(dh, x, p["norm_mix_gain"], dx1, (F32,), name="norm_mix_bwd")

    small = _small_pack(gs, gq_parts, gk_parts, dlb_fw, dlb_bw, lb_fw, lb_bw, name="small_pack")
    return loss, grad_x, small


def _small_pack(gs, gq_parts, gk_parts, dlb_fw, dlb_bw, lb_fw, lb_bw, *, name):
    def body(g_mix, g_mem, g_ffn, dfw, dbw, lfw, lbw, g_hg, q0, q1, q2, k0, k1, k2, g_mq, g_mk, o_ref):
        o_ref[0:1, :] = g_mix[...]
        o_ref[1:2, :] = g_mem[...]
        o_ref[2:3, :] = g_ffn[...]
        for base, d_ref, l_ref in ((3, dfw, lfw), (5, dbw, lbw)):
            lbv = l_ref[...]
            t = d_ref[...] * lbv * (1.0 - lbv)
            o_ref[base:base + 1, :] = t
            o_ref[base + 1:base + 2, :] = -t
        o_ref[7:8, :] = jnp.zeros((1, D_MODEL), F32)
        o_ref[7:8, 0:HEAD] = g_hg[...]
        o_ref[7:8, HEAD:2 * HEAD] = q0[...] + q1[...] + q2[...]
        o_ref[7:8, 2 * HEAD:3 * HEAD] = k0[...] + k1[...] + k2[...]
        o_ref[7:8, 3 * HEAD:4 * HEAD] = g_mq[...]
        o_ref[7:8, 4 * HEAD:5 * HEAD] = g_mk[...]

    return pl.pallas_call(body, name=name, out_shape=jax.ShapeDtypeStruct((8, D_MODEL), F32))(
        gs["norm_mix_gain"], gs["norm_mem_gain"], gs["norm_ffn_gain"], dlb_fw, dlb_bw, lb_fw, lb_bw,
        gs["hg_norm_gain"], *gq_parts, *gk_parts, gs["mem_q_gain"], gs["mem_k_gain"])


def _row_tile(rows, cols, n_arrays):
    budget = (16 * 1024 * 1024) // (2 * 4 * cols * n_arrays)
    tr = rows
    while tr > budget and tr % 2 == 0 and (tr // 2) % 16 == 0:
        tr //= 2
    return tr


def _cast_into_full(a, chip, rows, cols, axis, *, name):
    sr, sc = a.shape
    tr = _row_tile(sr, sc, 2)

    def body(chip_ref, a_ref, o_ref):
        del chip_ref
        o_ref[...] = a_ref[...].astype(BF16)

    if axis == 1:
        out_map = lambda i, chip_ref: (i, chip_ref[0])
    else:
        out_map = lambda i, chip_ref: (chip_ref[0] * (sr // tr) + i, 0)
    return pl.pallas_call(
        body, name=name,
        grid_spec=pltpu.PrefetchScalarGridSpec(
            num_scalar_prefetch=1, grid=(sr // tr,),
            in_specs=[pl.BlockSpec((tr, sc), lambda i, chip_ref: (i, 0))],
            out_specs=pl.BlockSpec((tr, sc), out_map)),
        out_shape=jax.ShapeDtypeStruct((rows, cols), BF16),
        compiler_params=_params(("parallel",)))(chip, a)


def _add_halves(items, *, name):
    rows = items[0][3].shape[0]
    widths = [ra.shape[1] for _, _, _, ra in items]
    tr = _row_tile(rows, sum(widths), 4)

    def body(*refs):
        o_ref = refs[-1]
        first = lax.axis_index("c") == 0
        off = 0
        for j, wd in enumerate(widths):
            h0, h1, ra = refs[3 * j:3 * j + 3]
            o_ref[:, off:off + wd] = (jnp.where(first, h0[...], h1[...]) + ra[...]).astype(BF16)
            off += wd

    in_specs, ins = [], []
    for (g, haxis, hsize, ra), wd in zip(items, widths):
        if haxis == 0:
            in_specs += [pl.BlockSpec((tr, wd), lambda i: (i, 0)),
                         pl.BlockSpec((tr, wd), lambda i, o=hsize // tr: (o + i, 0))]
        else:
            in_specs += [pl.BlockSpec((tr, wd), lambda i: (i, 0)), pl.BlockSpec((tr, wd), lambda i: (i, 1))]
        in_specs.append(pl.BlockSpec((tr, wd), lambda i: (i, 0)))
        ins += [g, g, ra]
    return pl.pallas_call(body, name=name, grid=(rows // tr,), in_specs=in_specs,
                          out_specs=pl.BlockSpec((tr, sum(widths)), lambda i: (i, 0)),
                          out_shape=jax.ShapeDtypeStruct((rows, sum(widths)), BF16),
                          compiler_params=_params(("parallel",)))(*ins)


def _add_slots(rb, *, name):
    _, rows, cols = rb.shape
    tr = _row_tile(rows, cols, 5)

    def body(r0, r1, r2, r3, o_ref):
        o_ref[...] = ((r0[...].astype(F32) + r1[...].astype(F32)) + r2[...].astype(F32)) + r3[...].astype(F32)

    slot = lambda s: pl.BlockSpec((None, tr, cols), lambda i: (s, i, 0))
    return pl.pallas_call(body, name=name, grid=(rows // tr,), in_specs=[slot(s) for s in range(4)],
                          out_specs=pl.BlockSpec((tr, cols), lambda i: (i, 0)),
                          out_shape=jax.ShapeDtypeStruct((rows, cols), F32),
                          compiler_params=_params(("parallel",)))(rb, rb, rb, rb)


def _adamw(w, g, m, v, *, name):
    rows, cols = w.shape
    tr = _row_tile(rows, cols, 7) if rows % 16 == 0 else rows
    c1 = 1.0 - ADAM_B1 ** ADAM_STEP
    c2 = 1.0 - ADAM_B2 ** ADAM_STEP

    def body(w_ref, g_ref, m_ref, v_ref, d_ref, mo_ref, vo_ref):
        gv = g_ref[...]
        mn = ADAM_B1 * m_ref[...] + (1.0 - ADAM_B1) * gv
        vn = ADAM_B2 * v_ref[...] + (1.0 - ADAM_B2) * (gv * gv)
        mo_ref[...] = mn
        vo_ref[...] = vn
        d_ref[...] = -ADAM_LR * ((mn / c1) / (jnp.sqrt(vn / c2) + ADAM_EPS) + ADAM_WD * w_ref[...])

    spec = pl.BlockSpec((tr, cols), lambda i: (i, 0))
    shape = jax.ShapeDtypeStruct((rows, cols), F32)
    return pl.pallas_call(body, name=name, grid=(rows // tr,), in_specs=[spec] * 4, out_specs=[spec] * 3,
                          out_shape=[shape] * 3, compiler_params=_params(("parallel",)))(w, g, m, v)


W_SPECS = (
    ("w_in", 1024, IN_COLS, 1, IN_COLS // 4),
    ("w_mem_kv", 1024, 1024, 0, 256),
    ("w_proj_hg", 1024, 1024, 0, 256),
    ("w_proj_da", 512, 1024, 1, 256),
    ("w_proj_mem", 512, 1024, 1, 256),
    ("w_out", 1024, 1024, 0, 256),
    ("w_ffn_in", 1024, 2 * D_FF, 1, 2 * D_FF // 4),
    ("w_ffn_out", D_FF, 1024, 0, D_FF // 4),
)
CHIP_FLIPS = ((1, 0), (0, 1), (1, 1))
ANY = pl.BlockSpec(memory_space=pl.ANY)
DMA_CHUNK_BYTES = 1 << 20
STAGE_BYTES = 2 << 20


def _place():
    x, y, c = lax.axis_index("x"), lax.axis_index("y"), lax.axis_index("c")
    return x, y, c, 2 * x + y


def _flip(v, f):
    return 1 - v if f else v


def _slab(ref, axis, idx, size):
    start = pl.multiple_of(idx * size, size)
    return ref.at[pl.ds(start, size), :] if axis == 0 else ref.at[:, pl.ds(start, size)]


def _chunked(make, src, dst, want="both"):
    rows, cols = src.shape
    row_bytes = cols * jnp.dtype(src.dtype).itemsize
    k = 1
    while rows % (2 * k) == 0 and (rows // (2 * k)) % 16 == 0 and (rows // k) * row_bytes > DMA_CHUNK_BYTES:
        k *= 2
    cr = rows // k
    parts = []
    if want != "wait":
        parts = [make(src.at[pl.ds(j * cr, cr), :], dst.at[pl.ds(j * cr, cr), :]) for j in range(k)]
    return parts, (make(src, dst) if want != "start" else None)


def _half_spec(rows, cols, axis):
    return (0, rows // 2) if axis == 1 else (1, cols // 2)


def _staged(src, remote_dst, local_dst, sibling, load_sems, send_sems, store_sems, recv_sem):
    rows, cols = src.shape
    row_bytes = cols * jnp.dtype(src.dtype).itemsize
    k = 1
    while (rows // k) * row_bytes > STAGE_BYTES and rows % (2 * k) == 0 and (rows // (2 * k)) % 16 == 0:
        k *= 2
    cr = rows // k
    piece = lambda ref, j: ref.at[pl.ds(j * cr, cr), :]

    def run(buf):
        loads = [pltpu.make_async_copy(piece(src, j), buf.at[j % 2], load_sems.at[j % 2]) for j in range(k)]
        outs = [[pltpu.make_async_remote_copy(src_ref=buf.at[j % 2], dst_ref=piece(remote_dst, j),
                                              send_sem=send_sems.at[j % 2], recv_sem=recv_sem,
                                              device_id=sibling, device_id_type=MESH)] for j in range(k)]
        if local_dst is not None:
            for j in range(k):
                outs[j].append(pltpu.make_async_copy(buf.at[j % 2], piece(local_dst, j), store_sems.at[j % 2]))

        def drained(j):
            outs[j][0].wait_send()
            for cp in outs[j][1:]:
                cp.wait()

        loads[0].start()
        for j in range(k):
            loads[j].wait()
            for cp in outs[j]:
                cp.start()
            if j + 1 < k:
                if j >= 1:
                    drained(j - 1)
                loads[j + 1].start()
        for j in range(max(0, k - 2), k):
            drained(j)

    pl.run_scoped(run, pltpu.VMEM((2, cr, cols), src.dtype))


def _landed(ref, recv_sem, send_sem):
    pltpu.make_async_remote_copy(src_ref=ref, dst_ref=ref, send_sem=send_sem, recv_sem=recv_sem,
                                 device_id=(lax.axis_index("x"), lax.axis_index("y"), lax.axis_index("c")),
                                 device_id_type=MESH).wait_recv()


def _half_slab(ref, spec, chip, half):
    _, rows, cols, axis, size = spec
    haxis, hsize = _half_spec(rows, cols, axis)
    return _slab(_slab(ref, axis, chip, size), haxis, half, hsize)


def _gather_sends(outs, specs, ici_send, ici_recv, want="both"):
    x, y, c, p = _place()
    sent = []
    for wi, spec in enumerate(specs):
        mine = _half_slab(outs[wi], spec, p, c)
        for k, (fx, fy) in enumerate(CHIP_FLIPS):
            sent.append(_chunked(lambda s, d, j=3 * wi + k, fx=fx, fy=fy: pltpu.make_async_remote_copy(
                src_ref=s, dst_ref=d, send_sem=ici_send.at[j], recv_sem=ici_recv.at[j],
                device_id=(_flip(x, fx), _flip(y, fy), c), device_id_type=MESH), mine, mine, want))
    return sent


def _gather_weights(fulls, specs, *, name):
    n = len(specs)

    def body(*refs):
        outs = refs[n:2 * n]
        ici_send, ici_recv, load_sems, d2d_send, d2d_recv = refs[2 * n:]
        x, y, c, _ = _place()
        sent = _gather_sends(outs, specs, ici_send, ici_recv)
        for parts, _ in sent:
            for cp in parts:
                cp.start()
        for k, (fx, fy) in enumerate(CHIP_FLIPS):
            q = 2 * _flip(x, fx) + _flip(y, fy)
            for wi in range(n):
                sent[3 * wi + k][1].wait_recv()
                got = _half_slab(outs[wi], specs[wi], q, c)
                _staged(got, got, None, (x, y, 1 - c), load_sems, d2d_send, None, d2d_recv.at[3 * wi + k])
        for _, whole in sent:
            whole.wait_send()
        for k, (fx, fy) in enumerate(CHIP_FLIPS):
            q = 2 * _flip(x, fx) + _flip(y, fy)
            for wi in range(n):
                _landed(_half_slab(outs[wi], specs[wi], q, 1 - c), d2d_recv.at[3 * wi + k], d2d_send.at[0])

    return pl.pallas_call(
        body, name=name, in_specs=[ANY] * n, out_specs=[ANY] * n,
        out_shape=[jax.ShapeDtypeStruct(f.shape, f.dtype) for f in fulls],
        input_output_aliases={i: i for i in range(n)},
        scratch_shapes=[pltpu.SemaphoreType.DMA((3 * n,)), pltpu.SemaphoreType.DMA((3 * n,)),
                        pltpu.SemaphoreType.DMA((2,)), pltpu.SemaphoreType.DMA((2,)),
                        pltpu.SemaphoreType.DMA((3 * n,))],
    )(*fulls)


def _gather_chips_carry(fulls, specs):
    n = len(specs)

    def issue(ins, outs, sems, want):
        del ins
        return [(parts, whole, True) for parts, whole in _gather_sends(outs, specs, sems[0], sems[1], want)]

    return _Carry(list(fulls), [jax.ShapeDtypeStruct(f.shape, f.dtype) for f in fulls], {i: i for i in range(n)},
                  [pltpu.SemaphoreType.DMA((3 * n,)), pltpu.SemaphoreType.DMA((3 * n,))], issue)


def _gather_pass_on(fulls, specs, *, name):
    n = len(specs)

    def body(*refs):
        outs = refs[n:2 * n]
        load_sems, d2d_send, d2d_recv = refs[2 * n:2 * n + 3]
        bufs = refs[2 * n + 3:]
        x, y, c, _ = _place()
        loads, sends = [], []
        for k, (fx, fy) in enumerate(CHIP_FLIPS):
            q = 2 * _flip(x, fx) + _flip(y, fy)
            for wi in range(n):
                j = 3 * wi + k
                got = _half_slab(outs[wi], specs[wi], q, c)
                loads.append(pltpu.make_async_copy(got, bufs[j], load_sems.at[j]))
                sends.append(pltpu.make_async_remote_copy(
                    src_ref=bufs[j], dst_ref=got, send_sem=d2d_send.at[j], recv_sem=d2d_recv.at[j],
                    device_id=(x, y, 1 - c), device_id_type=MESH))
        for cp in loads:
            cp.start()
        for load, send in zip(loads, sends):
            load.wait()
            send.start()
        for cp in sends:
            cp.wait_send()
        for k, (fx, fy) in enumerate(CHIP_FLIPS):
            q = 2 * _flip(x, fx) + _flip(y, fy)
            for wi in range(n):
                _landed(_half_slab(outs[wi], specs[wi], q, 1 - c), d2d_recv.at[3 * wi + k], d2d_send.at[0])

    shapes = []
    for _, rows, cols, axis, size in specs:
        shapes += [(rows // 2, size) if axis == 1 else (size, cols // 2)] * 3
    assert sum(math.prod(s) for s in shapes) * 2 <= VMEM_LIMIT_V7X // 2
    stages = [pltpu.VMEM(s, BF16) for s in shapes]
    return pl.pallas_call(
        body, name=name, in_specs=[ANY] * n, out_specs=[ANY] * n,
        out_shape=[jax.ShapeDtypeStruct(f.shape, f.dtype) for f in fulls],
        input_output_aliases={i: i for i in range(n)},
        scratch_shapes=[pltpu.SemaphoreType.DMA((3 * n,))] * 3 + stages,
        compiler_params=pltpu.CompilerParams(vmem_limit_bytes=VMEM_LIMIT_V7X),
    )(*fulls)


def _sibling_exchange(grads, *, name):
    n = len(grads)

    def body(*refs):
        ins, outs = refs[:n], refs[n:2 * n]
        load_sems, send_sems, recv_sems = refs[2 * n:]
        x, y, c, _ = _place()
        for i, (_, haxis, hsize) in enumerate(grads):
            _staged(_slab(ins[i], haxis, 1 - c, hsize), outs[i], None, (x, y, 1 - c),
                    load_sems, send_sems, None, recv_sems.at[i])
        for i in range(n):
            _landed(outs[i], recv_sems.at[i], send_sems.at[0])

    shapes = [jax.ShapeDtypeStruct((hsize, g.shape[1]) if haxis == 0 else (g.shape[0], hsize), g.dtype)
              for g, haxis, hsize in grads]
    return pl.pallas_call(
        body, name=name, in_specs=[ANY] * n, out_specs=[ANY] * n, out_shape=shapes,
        scratch_shapes=[pltpu.SemaphoreType.DMA((2,)), pltpu.SemaphoreType.DMA((2,)),
                        pltpu.SemaphoreType.DMA((n,))],
    )(*[g for g, _, _ in grads])


def _chip_exchange_carry(parts):
    n = len(parts)

    def issue(ins, outs, sems, want):
        send_sems, recv_sems, local_sems = sems
        x, y, c, p = _place()
        copies = []
        for i, (_, axis, size) in enumerate(parts):
            copies.append(_chunked(lambda s, d, i=i: pltpu.make_async_copy(s, d, local_sems.at[i]),
                                   _slab(ins[i], axis, p, size), outs[i].at[p], want) + (False,))
            for k, (fx, fy) in enumerate(CHIP_FLIPS):
                px, py = _flip(x, fx), _flip(y, fy)
                copies.append(_chunked(lambda s, d, j=3 * i + k, px=px, py=py: pltpu.make_async_remote_copy(
                    src_ref=s, dst_ref=d, send_sem=send_sems.at[j], recv_sem=recv_sems.at[j],
                    device_id=(px, py, c), device_id_type=MESH),
                    _slab(ins[i], axis, 2 * px + py, size), outs[i].at[p], want) + (True,))
        return copies

    shapes = []
    for a, axis, size in parts:
        shapes.append(jax.ShapeDtypeStruct((4, size, a.shape[1]) if axis == 0 else (4, a.shape[0], size), a.dtype))
    return _Carry([a for a, _, _ in parts], shapes, {},
                  [pltpu.SemaphoreType.DMA((3 * n,)), pltpu.SemaphoreType.DMA((3 * n,)),
                   pltpu.SemaphoreType.DMA((n,))], issue)


def _sibling_share(sums):
    n = len(sums)

    def body(*refs):
        ins, outs = refs[:n], refs[n:2 * n]
        load_sems, send_sems, store_sems, recv_sems = refs[2 * n:]
        x, y, c, _ = _place()
        for i, (s, haxis) in enumerate(sums):
            place = _slab(outs[i], haxis, c, s.shape[haxis])
            _staged(ins[i], place, place, (x, y, 1 - c), load_sems, send_sems, store_sems, recv_sems.at[i])
        for i, (s, haxis) in enumerate(sums):
            _landed(_slab(outs[i], haxis, 1 - c, s.shape[haxis]), recv_sems.at[i], send_sems.at[0])

    shapes = []
    for s, haxis in sums:
        r, cc = s.shape
        shapes.append(jax.ShapeDtypeStruct((2 * r, cc) if haxis == 0 else (r, 2 * cc), F32))
    return pl.pallas_call(
        body, name="grad_sibling_share", in_specs=[ANY] * n, out_specs=[ANY] * n, out_shape=shapes,
        scratch_shapes=[pltpu.SemaphoreType.DMA((2,)), pltpu.SemaphoreType.DMA((2,)),
                        pltpu.SemaphoreType.DMA((2,)), pltpu.SemaphoreType.DMA((n,))],
    )(*[s for s, _ in sums])


class _Exchanges:
    def __init__(self, fulls):
        self.w_in = _gather_weights([fulls["w_in"]], W_SPECS[:1], name="gather_w_in")[0]
        self.late_specs = W_SPECS[1:]
        self.late = [fulls[s[0]] for s in self.late_specs]
        self.slots = {}

    def late_carry(self):
        return _gather_chips_carry(self.late, self.late_specs)

    def late_weights(self, carried):
        done = _gather_pass_on(carried, self.late_specs, name="gather_pass_on")
        return dict(zip([s[0] for s in self.late_specs], done))

    def _half_sums(self, gw, gwb, specs, tag):
        grads, sent = [], []
        for name, _, _, axis, _ in specs:
            for part in (("w_ffn_a", "w_ffn_b") if name == "w_ffn_in" else (name,)):
                half = _half_spec(gw[part].shape[0], gw[part].shape[1], axis)
                grads.append((gw[part],) + half)
                sent.append((gwb[part],) + half)
        theirs = _sibling_exchange(sent, name=f"grad_sibling_exchange_{tag}")
        parts, j = [], 0
        for name, _, _, axis, size in specs:
            take = 2 if name == "w_ffn_in" else 1
            items = [grads[i] + (theirs[i],) for i in range(j, j + take)]
            parts.append((_add_halves(items, name=f"half_sum_{name}"), axis, size))
            j += take
        return parts

    def early_carries(self, gw, gwb):
        parts = self._half_sums(gw, gwb, self.late_specs, "early")
        self.early_names = [s[0] for s in self.late_specs]
        cut = self.early_names.index("w_ffn_in")
        return _chip_exchange_carry(parts[cut:]), _chip_exchange_carry(parts[:cut])

    def early_done(self, carried_bulk, carried_rest):
        cut = self.early_names.index("w_ffn_in")
        self.slots.update(zip(self.early_names[cut:], carried_bulk))
        self.slots.update(zip(self.early_names[:cut], carried_rest))

    def w_in_carry(self, g, gb):
        return _chip_exchange_carry(self._half_sums({"w_in": g}, {"w_in": gb}, W_SPECS[:1], "w_in"))

    def w_in_done(self, carried):
        self.slots["w_in"] = carried[0]

    def reduced(self):
        sums = []
        for name, rows, cols, axis, _ in W_SPECS:
            sums.append((_add_slots(self.slots[name], name=f"chip_sum_{name}"), _half_spec(rows, cols, axis)[0]))
        return dict(zip([s[0] for s in W_SPECS], _sibling_share(sums)))


def _small_allreduce(sv):
    def body(sv_ref, o_ref, slots_ref, send_sems, recv_sems):
        x, y, c, _ = _place()
        me = 4 * x + 2 * y + c
        slots_ref[me] = sv_ref[...]
        copies = []
        for k in range(1, 8):
            fx, fy, fc = (k >> 2) & 1, (k >> 1) & 1, k & 1
            copies.append(pltpu.make_async_remote_copy(
                src_ref=sv_ref, dst_ref=slots_ref.at[me], send_sem=send_sems.at[k - 1],
                recv_sem=recv_sems.at[k - 1], device_id=(_flip(x, fx), _flip(y, fy), _flip(c, fc)),
                device_id_type=MESH))
        for cp in copies:
            cp.start()
        for cp in copies:
            cp.wait_recv()
        for cp in copies:
            cp.wait_send()
        total = slots_ref[0]
        for s in range(1, 8):
            total = total + slots_ref[s]
        o_ref[...] = total

    vm = pl.BlockSpec(memory_space=pltpu.VMEM)
    return pl.pallas_call(
        body, name="small_allreduce", in_specs=[vm], out_specs=vm,
        out_shape=jax.ShapeDtypeStruct(sv.shape, F32),
        scratch_shapes=[pltpu.VMEM((8,) + sv.shape, F32), pltpu.SemaphoreType.DMA((7,)),
                        pltpu.SemaphoreType.DMA((7,))],
    )(sv)


SMALL_ROWS = (("norm_mix_gain", 0), ("norm_mem_gain", 1), ("norm_ffn_gain", 2))
SMALL_LB = (("lb_logits_fw", 3), ("lb_logits_bw", 5))
SMALL_HEAD = ("hg_norm_gain", "da_q_gain", "da_k_gain", "mem_q_gain", "mem_k_gain")


def _pack_small(d):
    last = jnp.concatenate([d[n] for n in SMALL_HEAD] + [jnp.zeros((1, D_MODEL - HEAD * len(SMALL_HEAD)), F32)], axis=1)
    return jnp.concatenate([d["norm_mix_gain"], d["norm_mem_gain"], d["norm_ffn_gain"],
                            d["lb_logits_fw"], d["lb_logits_bw"], last], axis=0)


def _unpack_small(a):
    out = {n: a[r:r + 1] for n, r in SMALL_ROWS}
    out.update({n: a[r:r + 2] for n, r in SMALL_LB})
    out.update({n: a[7:8, j * HEAD:(j + 1) * HEAD] for j, n in enumerate(SMALL_HEAD)})
    return out


PARAM_ORDER = ("norm_mix_gain", "norm_mem_gain", "w_in", "lb_logits_fw", "lb_logits_bw", "hg_norm_gain",
               "da_q_gain", "da_k_gain", "w_mem_kv", "mem_q_gain", "mem_k_gain", "w_proj_hg", "w_proj_da",
               "w_proj_mem", "w_out", "norm_ffn_gain", "w_ffn_in", "w_ffn_out")


def kernel(x, mem, norm_mix_gain, norm_mem_gain, w_in, lb_logits_fw, lb_logits_bw, hg_norm_gain, da_q_gain, da_k_gain, w_mem_kv, mem_q_gain, mem_k_gain, w_proj_hg, w_proj_da, w_proj_mem, w_out, norm_ffn_gain, w_ffn_in, w_ffn_out, loss_target, m_norm_mix_gain, m_norm_mem_gain, m_w_in, m_lb_logits_fw, m_lb_logits_bw, m_hg_norm_gain, m_da_q_gain, m_da_k_gain, m_w_mem_kv, m_mem_q_gain, m_mem_k_gain, m_w_proj_hg, m_w_proj_da, m_w_proj_mem, m_w_out, m_norm_ffn_gain, m_w_ffn_in, m_w_ffn_out, v_norm_mix_gain, v_norm_mem_gain, v_w_in, v_lb_logits_fw, v_lb_logits_bw, v_hg_norm_gain, v_da_q_gain, v_da_k_gain, v_w_mem_kv, v_mem_q_gain, v_mem_k_gain, v_w_proj_hg, v_w_proj_da, v_w_proj_mem, v_w_out, v_norm_ffn_gain, v_w_ffn_in, v_w_ffn_out):
    args = dict(locals())
    mats = tuple(s[0] for s in W_SPECS)
    flat = lambda a: a.reshape(a.shape[-2:])
    w = {n: flat(args[n]) for n in mats}
    m = {n: flat(args["m_" + n]) for n in mats}
    v = {n: flat(args["v_" + n]) for n in mats}
    small = {n: args[n] for n in PARAM_ORDER if n not in mats}

    chip = (2 * lax.axis_index("x") + lax.axis_index("y")).astype(jnp.int32).reshape(1)
    ex = _Exchanges({n: _cast_into_full(w[n], chip, rows, cols, axis, name=f"cast_{n}")
                     for n, rows, cols, axis, _ in W_SPECS})
    loss, grad_x, small_grads = _local_step(x[0], mem[0], loss_target[0], small, ex)
    grads = ex.reduced()
    small_sum = _small_allreduce(small_grads)

    delta, new_m, new_v = {}, {}, {}
    for n in mats:
        delta[n], new_m[n], new_v[n] = _adamw(w[n], grads[n], m[n], v[n], name=f"adamw_{n}")
    packed = _adamw(_pack_small(small), small_sum,
                    _pack_small({n: args["m_" + n] for n in small}),
                    _pack_small({n: args["v_" + n] for n in small}), name="adamw_small")
    grads.update(_unpack_small(small_sum))
    for dst, src in zip((delta, new_m, new_v), packed):
        dst.update(_unpack_small(src))

    def shaped(d, n):
        return d[n].reshape(args[n].shape)

    loss_sum = lax.psum(loss[0, 0], ("x", "y", "c"))
    return (loss_sum, grad_x[None], *[shaped(grads, n) for n in PARAM_ORDER], *[shaped(delta, n) for n in PARAM_ORDER],
            *[shaped(new_m, n) for n in PARAM_ORDER], *[shaped(new_v, n) for n in PARAM_ORDER])
```

```python
import functools
import math

import numpy as np
import jax
import jax.numpy as jnp
from jax import lax
from jax.experimental import pallas as pl
from jax.experimental.pallas import tpu as pltpu

F32 = jnp.float32
BF16 = jnp.bfloat16
MESH = pl.DeviceIdType.MESH

D_MODEL = 1024
HEAD = 128
HG_HEADS = 8
DA_GROUPS = ((1, 64), (4, 64), (16, 64))
DA_GH = 4
MEM_HEADS = 4
N_MEM = 256
D_FF = 2816
CHUNK = 64
BAND_QBLOCKS = 8
GLA_HEADS_PER_STEP = 4
RMS_EPS = 1e-6
NEG_INF = -1e30
HG_SCALE = HEAD ** -0.5
ATT_SCALE = HEAD ** -0.5
VMEM_LIMIT_V7X = 48 * 1024 * 1024

C_HQ, C_FF, C_FB, C_HI, C_HG = 0, 1024, 2048, 3072, 4096
C_DQ, C_DK, C_DV, C_MQ = 5120, 6656, 8192, 9728
C_GHG, C_GDA, C_GMEM = 10240, 11264, 12288
IN_COLS = 13312

ADAM_LR, ADAM_B1, ADAM_B2, ADAM_EPS, ADAM_WD, ADAM_STEP = 0.001, 0.9, 0.999, 1e-08, 0.01, 10


def _pallas_call(body, **kw):
    call = pl.pallas_call(body, **kw)

    def run(*ins):
        return call(*[pltpu.with_memory_space_constraint(a, pltpu.HBM) if jnp.issubdtype(a.dtype, jnp.floating) else a
                      for a in ins])

    return run


def _params(sem, vmem=VMEM_LIMIT_V7X):
    return pltpu.CompilerParams(dimension_semantics=sem, vmem_limit_bytes=vmem)


def _dot(a, b):
    return jnp.dot(a.astype(BF16), b.astype(BF16), preferred_element_type=F32)


def _dot_nt(a, b):
    return lax.dot_general(a.astype(BF16), b.astype(BF16), (((1,), (1,)), ((), ())),
                           preferred_element_type=F32)


def _dot_tn(a, b):
    return lax.dot_general(a.astype(BF16), b.astype(BF16), (((0,), (0,)), ((), ())),
                           preferred_element_type=F32)


def _sigmoid(v):
    return jax.nn.sigmoid(v.astype(F32))


def _ones(rows, cols):
    return (lax.broadcasted_iota(jnp.int32, (rows, cols), 0) >= 0).astype(BF16)


def _lane_sum(v):
    ones = _ones(HEAD, HEAD)
    hi = v.astype(BF16)
    mid = (v - hi.astype(F32)).astype(BF16)
    return jnp.dot(hi, ones, preferred_element_type=F32) + jnp.dot(mid, ones, preferred_element_type=F32)


def _row_mean(v):
    if v.shape[-1] == HEAD:
        return _lane_sum(v) * (1.0 / HEAD)
    return jnp.mean(v, axis=-1, keepdims=True)


def _rms(v):
    v = v.astype(F32)
    r = lax.rsqrt(_row_mean(v * v) + RMS_EPS)
    return v * r, r


def _rms_bwd(dy, xhat, r, gain):
    dxh = dy * gain
    dx = r * (dxh - xhat * _row_mean(dxh * xhat))
    return dx, dy * xhat


class _Carry:
    def __init__(self, arrays, out_shapes, aliases, sems, issue):
        self.arrays, self.out_shapes, self.aliases, self.sems, self.issue = arrays, out_shapes, aliases, sems, issue


NO_CARRY = object()


def _start_copies(copies):
    for parts, _, _ in copies:
        for cp in parts:
            cp.start()


def _wait_copies(copies):
    for _, whole, remote in copies:
        if remote:
            whole.wait_recv()
    for _, whole, remote in copies:
        if remote:
            whole.wait_send()
        else:
            whole.wait()


def _call(body, *, name, grid, in_specs, out_specs, out_shape, scratch_shapes, semantics, ins, carry=None,
          aliases=None):
    aliases = dict(aliases or {})
    if carry is None:
        res = _pallas_call(body, name=name, grid=grid, in_specs=in_specs, out_specs=out_specs, out_shape=out_shape,
                             scratch_shapes=scratch_shapes, input_output_aliases=aliases,
                             compiler_params=_params(semantics))(*ins)
        return list(res), []
    n_in, n_out, n_scr = len(ins), len(out_shape), len(scratch_shapes)
    c_in, c_out = len(carry.arrays), len(carry.out_shapes)

    def wrapped(*refs):
        pos = [0]

        def take(count):
            pos[0] += count
            return refs[pos[0] - count:pos[0]]

        own_in, carry_in = take(n_in), take(c_in)
        own_out, carry_out = take(n_out), take(c_out)
        own_scr, carry_sems = take(n_scr), take(len(carry.sems))
        ids = [pl.program_id(a) for a in range(len(grid))]
        first, last = ids[0] == 0, ids[0] == grid[0] - 1
        for a in range(1, len(grid)):
            first, last = first & (ids[a] == 0), last & (ids[a] == grid[a] - 1)

        @pl.when(first)
        def _():
            _start_copies(carry.issue(carry_in, carry_out, carry_sems, "start"))

        body(*own_in, *own_out, *own_scr)

        @pl.when(last)
        def _():
            _wait_copies(carry.issue(carry_in, carry_out, carry_sems, "wait"))

    res = _pallas_call(
        wrapped, name=name, grid=grid, in_specs=list(in_specs) + [ANY] * c_in,
        out_specs=list(out_specs) + [ANY] * c_out, out_shape=list(out_shape) + list(carry.out_shapes),
        input_output_aliases={**aliases, **{n_in + i: n_out + j for i, j in carry.aliases.items()}},
        scratch_shapes=list(scratch_shapes) + list(carry.sems),
        compiler_params=_params(("arbitrary",) * len(grid)))(*ins, *carry.arrays)
    return list(res[:n_out]), list(res[n_out:])


def _matmul(pairs, mode, out_dtype, *, tm, tn, tk=None, residual=None, also=None, carry=NO_CARRY, name):
    b_offs = [pr[2] if len(pr) > 2 else 0 for pr in pairs]
    pairs = [pr[:2] for pr in pairs]
    a0, b0 = pairs[0]
    if mode == "nn":
        (m, kk), n = a0.shape, b0.shape[1]
    elif mode == "nt":
        (m, kk), n = a0.shape, b0.shape[0]
    else:
        (kk, m), n = a0.shape, b0.shape[1]
    assert mode == "nt" or not any(b_offs)
    tm, tn = min(tm, m), min(tn, n)
    tk = kk if tk is None else tk
    nk = kk // tk
    assert m % tm == 0 and n % tn == 0 and kk % tk == 0, (name, m, n, kk)
    n_p = len(pairs)
    if mode == "tn":
        a_spec = pl.BlockSpec((tk, tm), lambda i, j, k: (k, i))
    else:
        a_spec = pl.BlockSpec((tm, tk), lambda i, j, k: (i, k))
    if mode == "nt":
        b_specs = [pl.BlockSpec((tn, tk), lambda i, j, k, o=o: (j, o * nk + k)) for o in b_offs]
    else:
        b_specs = [pl.BlockSpec((tk, tn), lambda i, j, k: (k, j))] * n_p
    o_spec = pl.BlockSpec((tm, tn), lambda i, j, k: (i, j))
    dot = {"nn": _dot, "nt": _dot_nt, "tn": _dot_tn}[mode]
    has_res = residual is not None

    def body(*refs):
        a_refs, b_refs = refs[:n_p], refs[n_p:2 * n_p]
        pos = 2 * n_p
        res_ref = refs[pos] if has_res else None
        pos += int(has_res)
        o_ref = refs[pos]
        pos += int(also is not None)
        part = dot(a_refs[0][...], b_refs[0][...])
        for a_r, b_r in zip(a_refs[1:], b_refs[1:]):
            part += dot(a_r[...], b_r[...])

        def finish(total):
            if has_res:
                total = total + res_ref[...]
            o_ref[...] = total.astype(out_dtype)
            if also is not None:
                refs[pos][...] = total.astype(also)

        if nk == 1:
            finish(part)
        else:
            acc_ref = refs[pos + 1]
            k = pl.program_id(2)

            @pl.when(k == 0)
            def _():
                acc_ref[...] = part

            @pl.when(k > 0)
            def _():
                acc_ref[...] += part

            @pl.when(k == nk - 1)
            def _():
                finish(acc_ref[...])

    ins = [a for a, _ in pairs] + [b for _, b in pairs]
    in_specs = [a_spec] * n_p + b_specs
    if has_res:
        ins.append(residual)
        in_specs.append(o_spec)
    dtypes = [out_dtype] + ([also] if also is not None else [])
    outs, carried = _call(
        body, name=name, grid=(m // tm, n // tn, nk), in_specs=in_specs, out_specs=[o_spec] * len(dtypes),
        out_shape=[jax.ShapeDtypeStruct((m, n), dt) for dt in dtypes],
        scratch_shapes=[pltpu.VMEM((tm, tn), F32)] if nk > 1 else [],
        semantics=("parallel", "parallel", "arbitrary"), ins=ins, carry=None if carry is NO_CARRY else carry)
    out = outs[0] if also is None else tuple(outs)
    return out if carry is NO_CARRY else (out, carried)


def _rms_fwd(x, gain, *, name):
    rows, dm = x.shape
    tm = min(1024, rows)

    def body(x_ref, g_ref, h_ref):
        xhat, _ = _rms(x_ref[...])
        h_ref[...] = (xhat * g_ref[...]).astype(BF16)

    return _pallas_call(
        body, name=name, grid=(rows // tm,),
        in_specs=[pl.BlockSpec((tm, dm), lambda i: (i, 0)), pl.BlockSpec((1, dm), lambda i: (0, 0))],
        out_specs=pl.BlockSpec((tm, dm), lambda i: (i, 0)),
        out_shape=jax.ShapeDtypeStruct((rows, dm), BF16),
        compiler_params=_params(("parallel",)),
    )(x, gain)


def _rms_bwd_rows(dh, x, gain, dres, dx_dtypes, *, name):
    rows, dm = x.shape
    tm = min(512, rows)
    has_res = dres is not None

    def body(*refs):
        dh_ref, x_ref, g_ref = refs[:3]
        res_ref = refs[3] if has_res else None
        outs = refs[3 + int(has_res):]
        dg_ref = outs[-1]
        xhat, r = _rms(x_ref[...])
        dx, dgr = _rms_bwd(dh_ref[...], xhat, r, g_ref[...])
        if has_res:
            dx = dx + res_ref[...]
        for dx_ref, dt in zip(outs, dx_dtypes):
            dx_ref[...] = dx.astype(dt)

        @pl.when(pl.program_id(0) == 0)
        def _():
            dg_ref[...] = jnp.zeros_like(dg_ref)

        dg_ref[...] += jnp.sum(dgr, axis=0, keepdims=True)

    row = pl.BlockSpec((tm, dm), lambda i: (i, 0))
    vec = pl.BlockSpec((1, dm), lambda i: (0, 0))
    return _pallas_call(
        body, name=name, grid=(rows // tm,),
        in_specs=[row, row, vec] + ([row] if has_res else []),
        out_specs=[row] * len(dx_dtypes) + [vec],
        out_shape=[jax.ShapeDtypeStruct((rows, dm), dt) for dt in dx_dtypes] + [jax.ShapeDtypeStruct((1, dm), F32)],
        compiler_params=_params(("arbitrary",)),
    )(*([dh, x, gain] + ([dres] if has_res else [])))


def _out_norm(a, w, x, gain, *, name):
    rows, dm = x.shape
    tm = min(512, rows)

    def body(a_ref, w_ref, x_ref, g_ref, x1_ref, h_ref):
        x1 = x_ref[...] + _dot(a_ref[...], w_ref[...])
        x1_ref[...] = x1
        xhat, _ = _rms(x1)
        h_ref[...] = (xhat * g_ref[...]).astype(BF16)

    row = lambda wd: pl.BlockSpec((tm, wd), lambda i: (i, 0))
    return _pallas_call(
        body, name=name, grid=(rows // tm,),
        in_specs=[row(a.shape[1]), pl.BlockSpec(w.shape, lambda i: (0, 0)), row(dm), pl.BlockSpec((1, dm), lambda i: (0, 0))],
        out_specs=[row(dm), row(dm)],
        out_shape=[jax.ShapeDtypeStruct((rows, dm), F32), jax.ShapeDtypeStruct((rows, dm), BF16)],
        compiler_params=_params(("parallel",)),
    )(a, w, x, gain)


def _out_loss(u, w, x1, tgt, *, name):
    rows, dm = x1.shape
    tm = min(256, rows)
    steps = rows // tm

    def body(u_ref, w_ref, x_ref, t_ref, dy_ref, dyb_ref, loss_ref, acc_ref):
        i = pl.program_id(0)
        diff = x_ref[...] + _dot(u_ref[...], w_ref[...]) - t_ref[...]
        dy = diff * (1.0 / dm)
        dy_ref[...] = dy
        dyb_ref[...] = dy.astype(BF16)

        @pl.when(i == 0)
        def _():
            acc_ref[...] = jnp.zeros_like(acc_ref)

        acc_ref[...] += jnp.sum(diff * diff, axis=0, keepdims=True)

        @pl.when(i == steps - 1)
        def _():
            loss_ref[...] = jnp.full((1, HEAD), 0.5 / dm, F32) * jnp.sum(acc_ref[...])

    row = lambda wd: pl.BlockSpec((tm, wd), lambda i: (i, 0))
    return _pallas_call(
        body, name=name, grid=(steps,),
        in_specs=[row(u.shape[1]), pl.BlockSpec(w.shape, lambda i: (0, 0)), row(dm), row(dm)],
        out_specs=[row(dm), row(dm), pl.BlockSpec((1, HEAD), lambda i: (0, 0))],
        out_shape=[jax.ShapeDtypeStruct((rows, dm), F32), jax.ShapeDtypeStruct((rows, dm), BF16),
                   jax.ShapeDtypeStruct((1, HEAD), F32)],
        scratch_shapes=[pltpu.VMEM((1, dm), F32)],
        compiler_params=_params(("arbitrary",)),
    )(u, w, x1, tgt)


def _gla_block_terms(q_raw, f_logit, lb, rev):
    sig = _sigmoid(f_logit)
    forget = lb + (1.0 - lb) * sig
    k = 1.0 - forget
    b = _chunk_cumsum(jnp.log(forget), rev)
    qs = _sigmoid(q_raw)
    eb = jnp.exp(b)
    emb = jnp.exp(-b)
    qt = (q_raw * qs * HG_SCALE) * eb
    kt = k * emb
    return sig, forget, k, b, qs, eb, emb, qt, kt


def _chunk_cumsum(v, rev):
    n = v.shape[0]
    pos = lax.broadcasted_iota(jnp.int32, v.shape, 0) & (CHUNK - 1)
    step = 1
    while step < CHUNK:
        if rev:
            shifted, keep = pltpu.roll(v, n - step, 0), pos < CHUNK - step
        else:
            shifted, keep = pltpu.roll(v, step, 0), pos >= step
        v = v + jnp.where(keep, shifted, 0.0)
        step *= 2
    return v


def _tri_mask(n, rev):
    row, col = np.arange(n)[:, None], np.arange(n)[None, :]
    same = (row // CHUNK) == (col // CHUNK)
    return jnp.asarray((same & ((row <= col) if rev else (row >= col))).astype(np.float32))


def _chunk_order(ncb, rev):
    order = range(ncb - 1, -1, -1) if rev else range(ncb)
    return [(c, c * CHUNK if rev else c * CHUNK + CHUNK - 1) for c in order]


def _gla_fwd(proj, lb, *, f_off, rev, name):
    rows = proj.shape[0]
    tb = min(256, rows)
    nb, ncb = rows // tb, tb // CHUNK

    def tmap(n):
        return nb - 1 - n if rev else n

    def body(tri_ref, q_ref, f_ref, v_ref, lb_ref, o_ref, st_ref, s_ref):
        @pl.when(pl.program_id(1) == 0)
        def _():
            s_ref[...] = jnp.zeros_like(s_ref)

        tri = tri_ref[...] > 0.5
        for hh in range(GLA_HEADS_PER_STEP):
            cs = slice(hh * HEAD, (hh + 1) * HEAD)
            v = v_ref[:, cs]
            _, _, k, b, _, _, _, qt, kt = _gla_block_terms(q_ref[:, cs].astype(F32), f_ref[:, cs].astype(F32),
                                                           lb_ref[:, cs], rev)
            o_intra = _dot(jnp.where(tri, _dot_nt(qt, kt), 0.0), v)
            chunks = []
            for c, last in _chunk_order(ncb, rev):
                sl = slice(c * CHUNK, (c + 1) * CHUNK)
                bl = b[last:last + 1, :]
                kh = k[sl] * jnp.exp(bl - b[sl])
                chunks.append((c, sl, jnp.exp(bl), _dot_tn(v[sl], kh)))
            s_t = s_ref[hh]
            for c, sl, ebl, kv in chunks:
                st_ref[hh, c] = s_t
                o_ref[sl, cs] = o_intra[sl] + _dot_nt(qt[sl], s_t)
                s_t = ebl * s_t + kv
            s_ref[hh] = s_t

    hps = GLA_HEADS_PER_STEP
    col = lambda off: pl.BlockSpec((tb, hps * HEAD), lambda h, n: (tmap(n), off // (hps * HEAD) + h))
    return _pallas_call(
        body, name=name, grid=(HG_HEADS // hps, nb),
        in_specs=[pl.BlockSpec((tb, tb), lambda h, n: (0, 0)), col(C_HQ), col(f_off), col(C_HI),
                  pl.BlockSpec((1, hps * HEAD), lambda h, n: (0, h))],
        out_specs=[pl.BlockSpec((tb, hps * HEAD), lambda h, n: (tmap(n), h)),
                   pl.BlockSpec((hps, ncb, HEAD, HEAD), lambda h, n: (h, tmap(n), 0, 0))],
        out_shape=[jax.ShapeDtypeStruct((rows, HG_HEADS * HEAD), F32),
                   jax.ShapeDtypeStruct((HG_HEADS, rows // CHUNK, HEAD, HEAD), F32)],
        scratch_shapes=[pltpu.VMEM((hps, HEAD, HEAD), F32)],
        compiler_params=_params(("parallel", "arbitrary")),
    )(_tri_mask(tb, rev), proj, proj, proj, lb)


def _gla_bwd(proj, lb, do, states, prev, dproj, *, f_off, rev, carry=NO_CARRY, name):
    rows = proj.shape[0]
    tb = min(256, rows)
    nb, ncb = rows // tb, tb // CHUNK
    has_prev = prev is not None
    qv_dtype = BF16 if has_prev else F32

    def tmap(n):
        return n if rev else nb - 1 - n

    def body(*refs):
        tri_ref, q_ref, f_ref, v_ref, lb_ref, do_ref, st_ref = refs[:7]
        pq_ref, pv_ref = refs[8:10] if has_prev else (None, None)
        (dq_ref, df_ref, dv_ref, dlb_ref,
         ds_ref, dqt_scr, dk_scr, db_scr, dbl_scr, dv_scr) = refs[8 + 2 * int(has_prev):]

        @pl.when(pl.program_id(1) == 0)
        def _():
            ds_ref[...] = jnp.zeros_like(ds_ref)
            dlb_ref[...] = jnp.zeros_like(dlb_ref)

        tri = tri_ref[...] > 0.5
        for hh in range(GLA_HEADS_PER_STEP):
            cs = slice(hh * HEAD, (hh + 1) * HEAD)
            lbv = lb_ref[:, cs]
            q_raw = q_ref[:, cs].astype(F32)
            v, dout = v_ref[:, cs], do_ref[:, cs].astype(BF16)
            sig, forget, k, b, qs, eb, emb, qt, kt = _gla_block_terms(q_raw, f_ref[:, cs].astype(F32), lbv, rev)
            a = jnp.where(tri, _dot_nt(qt, kt), 0.0)
            da = jnp.where(tri, _dot_nt(dout, v), 0.0)
            dv_intra = _dot_tn(a, dout)
            dqt_intra = _dot(da, kt)
            dkt = _dot_tn(da, qt)
            chunks = []
            for c, last in reversed(_chunk_order(ncb, rev)):
                sl = slice(c * CHUNK, (c + 1) * CHUNK)
                bl = b[last:last + 1, :]
                e = jnp.exp(bl - b[sl])
                s_t = st_ref[hh, c]
                dqt_scr[sl, cs] = dqt_intra[sl] + _dot(dout[sl], s_t)
                chunks.append((sl, jnp.exp(bl), e, k[sl] * e, s_t, _dot_tn(dout[sl], qt[sl])))
            ds_t = ds_ref[hh]
            for sl, ebl, e, kh, s_t, grow in chunks:
                dkh = _dot(v[sl], ds_t)
                dv_scr[sl, cs] = dv_intra[sl] + _dot_nt(kh, ds_t)
                dk_scr[sl, cs] = dkt[sl] * emb[sl] + dkh * e
                khd = kh * dkh
                dbl = jnp.sum(khd, axis=0, keepdims=True) + ebl * jnp.sum(ds_t * s_t, axis=0, keepdims=True)
                db_scr[sl, cs] = khd
                dbl_scr[sl, cs] = jnp.broadcast_to(dbl, (CHUNK, HEAD))
                ds_t = grow + ds_t * ebl
            ds_ref[hh] = ds_t
            dqt = dqt_scr[:, cs]
            dlogf = _chunk_cumsum(qt * dqt - kt * dkt - db_scr[:, cs], not rev) + dbl_scr[:, cs]
            dforget = dlogf / forget - dk_scr[:, cs]
            df_ref[:, cs] = (dforget * (1.0 - lbv) * sig * (1.0 - sig)).astype(BF16)
            dlb_ref[:, cs] += jnp.sum(dforget * (1.0 - sig), axis=0, keepdims=True)
            dqr = dqt * eb * (HG_SCALE * qs * (1.0 + q_raw * (1.0 - qs)))
            dv = dv_scr[:, cs]
            if has_prev:
                dqr = dqr + pq_ref[:, cs]
                dv = dv + pv_ref[:, cs]
            dq_ref[:, cs] = dqr.astype(qv_dtype)
            dv_ref[:, cs] = dv.astype(qv_dtype)

    hps = GLA_HEADS_PER_STEP
    col = lambda off: pl.BlockSpec((tb, hps * HEAD), lambda h, n: (tmap(n), off // (hps * HEAD) + h))
    blk = pl.BlockSpec((tb, hps * HEAD), lambda h, n: (tmap(n), h))
    vec = pl.BlockSpec((1, hps * HEAD), lambda h, n: (0, h))
    wide = HG_HEADS * HEAD
    into = jax.ShapeDtypeStruct(dproj.shape, dproj.dtype)
    outs, carried = _call(
        body, name=name, grid=(HG_HEADS // hps, nb),
        in_specs=[pl.BlockSpec((tb, tb), lambda h, n: (0, 0)), col(C_HQ), col(f_off), col(C_HI), vec, blk,
                  pl.BlockSpec((hps, ncb, HEAD, HEAD), lambda h, n: (h, tmap(n), 0, 0)), ANY]
                 + ([blk, blk] if has_prev else []),
        out_specs=[col(C_HQ) if has_prev else blk, blk if has_prev else col(f_off), blk, vec],
        out_shape=[into if has_prev else jax.ShapeDtypeStruct((rows, wide), qv_dtype),
                   jax.ShapeDtypeStruct((rows, wide), BF16) if has_prev else into,
                   jax.ShapeDtypeStruct((rows, wide), qv_dtype), jax.ShapeDtypeStruct((1, wide), F32)],
        aliases={7: 0 if has_prev else 1},
        scratch_shapes=[pltpu.VMEM((hps, HEAD, HEAD), F32)] + [pltpu.VMEM((tb, hps * HEAD), F32)] * 5,
        semantics=("parallel", "arbitrary"),
        ins=[_tri_mask(tb, rev), proj, proj, proj, lb, do, states, dproj] + (list(prev) if has_prev else []),
        carry=None if carry is NO_CARRY else carry)
    return outs if carry is NO_CARRY else (outs, carried)


def _fill_columns(dproj, parts, col, *, name):
    rows, wd = parts[0].shape
    tm = min(1024, rows)
    n = len(parts)

    def body(*refs):
        o_ref = refs[n + 1]
        for j in range(n):
            o_ref[:, j * wd:(j + 1) * wd] = refs[j][...]

    return _pallas_call(
        body, name=name, grid=(rows // tm,),
        in_specs=[pl.BlockSpec((tm, wd), lambda i: (i, 0))] * n + [ANY],
        out_specs=pl.BlockSpec((pl.Element(tm), pl.Element(n * wd)), lambda i: (i * tm, col)),
        out_shape=jax.ShapeDtypeStruct(dproj.shape, dproj.dtype), input_output_aliases={n: 0},
        compiler_params=_params(("parallel",)),
    )(*parts, dproj)


def _hg_out_fwd(o_fw, o_bw, proj, gain, *, name):
    rows = o_fw.shape[0]
    tm = min(1024, rows)
    wide = HG_HEADS * HEAD

    def body(a_ref, b_ref, g_ref, gain_ref, o_ref):
        for h in range(HG_HEADS):
            sl = slice(h * HEAD, (h + 1) * HEAD)
            xhat, _ = _rms(a_ref[:, sl] + b_ref[:, sl])
            gate = g_ref[:, sl].astype(F32)
            o_ref[:, sl] = (xhat * gain_ref[...] * (gate * _sigmoid(gate))).astype(BF16)

    row = pl.BlockSpec((tm, wide), lambda i: (i, 0))
    return _pallas_call(
        body, name=name, grid=(rows // tm,),
        in_specs=[row, row, pl.BlockSpec((tm, wide), lambda i: (i, C_HG // wide)),
                  pl.BlockSpec((1, HEAD), lambda i: (0, 0))],
        out_specs=row, out_shape=jax.ShapeDtypeStruct((rows, wide), BF16),
        compiler_params=_params(("parallel",)),
    )(o_fw, o_bw, proj, gain)


def _hg_out_bwd(dout, o_fw, o_bw, proj, gain, dproj, *, name):
    rows = o_fw.shape[0]
    tm = min(512, rows)
    wide = HG_HEADS * HEAD

    def body(d_ref, a_ref, b_ref, g_ref, gain_ref, dp_in, dgate_ref, do_ref, dgain_ref):
        del dp_in

        @pl.when(pl.program_id(0) == 0)
        def _():
            dgain_ref[...] = jnp.zeros_like(dgain_ref)

        dgain = jnp.zeros((1, HEAD), F32)
        for h in range(HG_HEADS):
            sl = slice(h * HEAD, (h + 1) * HEAD)
            xhat, r = _rms(a_ref[:, sl] + b_ref[:, sl])
            gate, dy = g_ref[:, sl].astype(F32), d_ref[:, sl]
            gs = _sigmoid(gate)
            dgate_ref[:, sl] = (dy * xhat * gain_ref[...] * (gs * (1.0 + gate * (1.0 - gs)))).astype(BF16)
            dx, dgr = _rms_bwd(dy * (gate * gs), xhat, r, gain_ref[...])
            do_ref[:, sl] = dx
            dgain = dgain + jnp.sum(dgr, axis=0, keepdims=True)
        dgain_ref[...] += dgain

    row = pl.BlockSpec((tm, wide), lambda i: (i, 0))
    gate = pl.BlockSpec((tm, wide), lambda i: (i, C_HG // wide))
    vec = pl.BlockSpec((1, HEAD), lambda i: (0, 0))
    return _pallas_call(
        body, name=name, grid=(rows // tm,),
        in_specs=[row, row, row, gate, vec, ANY],
        out_specs=[gate, row, vec],
        out_shape=[jax.ShapeDtypeStruct(dproj.shape, dproj.dtype), jax.ShapeDtypeStruct((rows, wide), F32),
                   jax.ShapeDtypeStruct((1, HEAD), F32)],
        input_output_aliases={5: 0},
        compiler_params=_params(("arbitrary",)),
    )(dout, o_fw, o_bw, proj, gain, dproj)


def _strided_rows(ref, r, count, d):
    return ref[...] if d == 1 else ref[pl.ds(r, count, stride=d), :]


def _store_strided(ref, r, count, d, val):
    if d == 1:
        ref[...] = val
    else:
        ref[pl.ds(r, count, stride=d), :] = val


def _da_prep(proj, q_gain, k_gain, g, *, name):
    d = DA_GROUPS[g][0]
    rows = proj.shape[0]
    rb = min(1024, rows)
    tn = rb // d

    def body(q_ref, k_ref, v_ref, qg_ref, kg_ref, qo_ref, ko_ref, vo_ref, qf_ref, kf_ref, vf_ref):
        for h in range(DA_GH):
            cs = slice(h * HEAD, (h + 1) * HEAD)
            for src, dst in ((q_ref, qf_ref), (k_ref, kf_ref), (v_ref, vf_ref)):
                dst[...] = src[:, cs].astype(F32)
            for r in range(d):
                qhat, _ = _rms(_strided_rows(qf_ref, r, tn, d))
                khat, _ = _rms(_strided_rows(kf_ref, r, tn, d))
                qo_ref[h, r] = (qhat * qg_ref[...]).astype(BF16)
                ko_ref[h, r] = (khat * kg_ref[...]).astype(BF16)
                vo_ref[h, r] = _strided_rows(vf_ref, r, tn, d).astype(BF16)

    wide = DA_GH * HEAD
    col = lambda off: pl.BlockSpec((rb, wide), lambda i: (i, off // wide + g))
    vec = pl.BlockSpec((1, HEAD), lambda i: (0, 0))
    out = pl.BlockSpec((DA_GH, d, tn, HEAD), lambda i: (0, 0, i, 0))
    shape = jax.ShapeDtypeStruct((DA_GH, d, rows // d, HEAD), BF16)
    return _pallas_call(
        body, name=name, grid=(rows // rb,),
        in_specs=[col(C_DQ), col(C_DK), col(C_DV), vec, vec],
        out_specs=[out, out, out], out_shape=[shape, shape, shape],
        scratch_shapes=[pltpu.VMEM((rb, HEAD), F32)] * 3,
        compiler_params=_params(("parallel",)),
    )(proj, proj, proj, q_gain, k_gain)


def _slopes(g):
    idx = np.arange(g * DA_GH + 1, (g + 1) * DA_GH + 1)
    s = (2.0 ** (-8.0 * idx / (DA_GH * len(DA_GROUPS)))).astype(np.float32)
    return jnp.asarray(np.broadcast_to(s[:, None, None], (DA_GH, 8, HEAD)).copy())


def _band_window(ld, t, radius):
    win = min(2 * t, ld)
    assert t // 2 >= radius or win == ld
    return win


def _band_scores(q, k, q0, start, slope, d, radius):
    t, win = q.shape[0], k.shape[0]
    row = lax.broadcasted_iota(jnp.int32, (t, win), 0)
    col = lax.broadcasted_iota(jnp.int32, (t, win), 1)
    rel = jnp.abs((start - q0) + col - row)
    return _dot_nt(q, k) * ATT_SCALE + jnp.where(rel <= radius, -slope * (d * rel).astype(F32), NEG_INF)


def _band_fwd(qr, kr, vr, g, *, name):
    d, radius = DA_GROUPS[g]
    _, _, ld, _ = qr.shape
    t = min(HEAD, ld)
    qb = min(BAND_QBLOCKS, ld // t)
    rps = max(1, min(d, BAND_QBLOCKS // qb))
    win = _band_window(ld, t, radius)

    def body(q_ref, k_ref, v_ref, sl_ref, o_ref, lse_ref):
        i = pl.program_id(2)
        slope = sl_ref[0:1, 0:1]
        for rr in range(rps):
            for j in range(qb):
                sl = slice(j * t, (j + 1) * t)
                q0 = (i * qb + j) * t
                start = pl.multiple_of(jnp.clip(q0 - t // 2, 0, ld - win), t // 2)
                s = _band_scores(q_ref[rr, sl, :], k_ref[rr, pl.ds(start, win), :], q0, start, slope, d, radius)
                m = jnp.max(s, axis=-1, keepdims=True)
                p = jnp.exp(s - m).astype(BF16)
                l = jnp.dot(p, _ones(win, HEAD), preferred_element_type=F32)
                o_ref[rr, sl, :] = _dot(p, v_ref[rr, pl.ds(start, win), :]) / l
                lse_ref[rr, sl, :] = m + jnp.log(l)

    own = pl.BlockSpec((None, rps, qb * t, HEAD), lambda h, r, i: (h, r, i, 0))
    seq = pl.BlockSpec((None, rps, ld, HEAD), lambda h, r, i: (h, r, 0, 0))
    shape = jax.ShapeDtypeStruct(qr.shape, F32)
    return _pallas_call(
        body, name=name, grid=(DA_GH, d // rps, ld // (qb * t)),
        in_specs=[own, seq, seq, pl.BlockSpec((None, 8, HEAD), lambda h, r, i: (h, 0, 0))],
        out_specs=[own, own], out_shape=[shape, shape],
        compiler_params=_params(("parallel", "parallel", "parallel")),
    )(qr, kr, vr, _slopes(g))


def _band_bwd(qr, kr, vr, dor, lser, deltar, g, *, name):
    d, radius = DA_GROUPS[g]
    _, _, ld, _ = qr.shape
    t = min(HEAD, ld)
    qb = min(BAND_QBLOCKS, ld // t)
    rps = max(1, min(d, BAND_QBLOCKS // qb))
    win = _band_window(ld, t, radius)

    def body(q_ref, k_ref, v_ref, do_ref, lse_ref, dl_ref, sl_ref, dq_ref, dk_ref, dv_ref):
        i = pl.program_id(2)

        @pl.when(i == 0)
        def _():
            dk_ref[...] = jnp.zeros_like(dk_ref)
            dv_ref[...] = jnp.zeros_like(dv_ref)

        slope = sl_ref[0:1, 0:1]
        for rr in range(rps):
            for j in range(qb):
                sl = slice(j * t, (j + 1) * t)
                q0 = (i * qb + j) * t
                start = pl.multiple_of(jnp.clip(q0 - t // 2, 0, ld - win), t // 2)
                rows = pl.ds(start, win)
                q, dout, k, v = q_ref[rr, sl, :], do_ref[rr, sl, :], k_ref[rr, rows, :], v_ref[rr, rows, :]
                p = jnp.exp(_band_scores(q, k, q0, start, slope, d, radius) - lse_ref[rr, sl, 0:1])
                ds = p * (_dot_nt(dout, v) - dl_ref[rr, sl, 0:1])
                dq_ref[rr, sl, :] = _dot(ds, k) * ATT_SCALE
                dk_ref[rr, rows, :] += _dot_tn(ds, q) * ATT_SCALE
                dv_ref[rr, rows, :] += _dot_tn(p, dout)

    own = pl.BlockSpec((None, rps, qb * t, HEAD), lambda h, r, i: (h, r, i, 0))
    seq = pl.BlockSpec((None, rps, ld, HEAD), lambda h, r, i: (h, r, 0, 0))
    shape = jax.ShapeDtypeStruct(qr.shape, F32)
    return _pallas_call(
        body, name=name, grid=(DA_GH, d // rps, ld // (qb * t)),
        in_specs=[own, seq, seq, own, own, own, pl.BlockSpec((None, 8, HEAD), lambda h, r, i: (h, 0, 0))],
        out_specs=[own, seq, seq], out_shape=[shape, shape, shape],
        compiler_params=_params(("parallel", "parallel", "arbitrary")),
    )(qr, kr, vr, dor, lser, deltar, _slopes(g))


def _da_merge(outs, lses, rows, *, name):
    rb = min(1024, rows)
    wide = DA_GH * HEAD

    def body(*refs):
        o_refs, l_refs = refs[0:3], refs[3:6]
        o_ref, lse_ref = refs[6:8]
        on_refs, ln_refs = refs[8:11], refs[11:14]
        for h in range(DA_GH):
            cs = slice(h * HEAD, (h + 1) * HEAD)
            for g, (d, _) in enumerate(DA_GROUPS):
                tn = rb // d
                for r in range(d):
                    _store_strided(on_refs[g], r, tn, d, o_refs[g][h, r])
                    _store_strided(ln_refs[g], r, tn, d, l_refs[g][h, r])
            l0, l1, l2 = ln_refs[0][...], ln_refs[1][...], ln_refs[2][...]
            m = jnp.maximum(jnp.maximum(l0, l1), l2)
            e0, e1, e2 = jnp.exp(l0 - m), jnp.exp(l1 - m), jnp.exp(l2 - m)
            tot = e0 + e1 + e2
            o_ref[:, cs] = (e0 * on_refs[0][...] + e1 * on_refs[1][...] + e2 * on_refs[2][...]) / tot
            lse_ref[:, cs] = m + jnp.log(tot)

    res = lambda d: pl.BlockSpec((DA_GH, d, rb // d, HEAD), lambda i: (0, 0, i, 0))
    nat = pl.BlockSpec((rb, wide), lambda i: (i, 0))
    shape = jax.ShapeDtypeStruct((rows, wide), F32)
    return _pallas_call(
        body, name=name, grid=(rows // rb,),
        in_specs=[res(d) for d, _ in DA_GROUPS] * 2,
        out_specs=[nat, nat], out_shape=[shape, shape],
        scratch_shapes=[pltpu.VMEM((rb, HEAD), F32)] * 6,
        compiler_params=_params(("parallel",)),
    )(*outs, *lses)


def _da_bwd_prep(dout, o, lse, *, name):
    rows = o.shape[0]
    rb = min(512, rows)
    wide = DA_GH * HEAD

    def body(d_ref, o_ref, l_ref, *outs):
        d_scr, l_scr, delta_scr = outs[9:12]
        for h in range(DA_GH):
            cs = slice(h * HEAD, (h + 1) * HEAD)
            dv = d_ref[:, cs]
            d_scr[...] = dv
            l_scr[...] = l_ref[:, cs]
            delta_scr[...] = _lane_sum(dv * o_ref[:, cs])
            for g, (d, _) in enumerate(DA_GROUPS):
                tn = rb // d
                for r in range(d):
                    outs[3 * g][h, r] = _strided_rows(d_scr, r, tn, d).astype(BF16)
                    outs[3 * g + 1][h, r] = _strided_rows(l_scr, r, tn, d)
                    outs[3 * g + 2][h, r] = _strided_rows(delta_scr, r, tn, d)

    nat = pl.BlockSpec((rb, wide), lambda i: (i, 0))
    out_specs, out_shape = [], []
    for d, _ in DA_GROUPS:
        for dt in (BF16, F32, F32):
            out_specs.append(pl.BlockSpec((DA_GH, d, rb // d, HEAD), lambda i: (0, 0, i, 0)))
            out_shape.append(jax.ShapeDtypeStruct((DA_GH, d, rows // d, HEAD), dt))
    return _pallas_call(
        body, name=name, grid=(rows // rb,),
        in_specs=[nat, nat, nat], out_specs=out_specs, out_shape=out_shape,
        scratch_shapes=[pltpu.VMEM((rb, HEAD), F32)] * 3,
        compiler_params=_params(("parallel",)),
    )(dout, o, lse)


def _da_prep_bwd(dqr, dkr, dvr, proj, q_gain, k_gain, g, *, name):
    d = DA_GROUPS[g][0]
    rows = proj.shape[0]
    rb = min(1024, rows)
    tn = rb // d
    wide = DA_GH * HEAD

    def body(dq_ref, dk_ref, dv_ref, q_ref, k_ref, qg_ref, kg_ref, oq_ref, ok_ref, ov_ref, gq_ref, gk_ref, *nat_refs):
        @pl.when(pl.program_id(0) == 0)
        def _():
            gq_ref[...] = jnp.zeros_like(gq_ref)
            gk_ref[...] = jnp.zeros_like(gk_ref)

        for h in range(DA_GH):
            cs = slice(h * HEAD, (h + 1) * HEAD)
            for j, src in enumerate((dq_ref, dk_ref, dv_ref)):
                for r in range(d):
                    _store_strided(nat_refs[j], r, tn, d, src[h, r])
            ov_ref[:, cs] = nat_refs[2][...].astype(BF16)
            for j, (x_ref, gn_ref, out_ref, acc_ref) in enumerate(((q_ref, qg_ref, oq_ref, gq_ref),
                                                                    (k_ref, kg_ref, ok_ref, gk_ref))):
                xhat, r = _rms(x_ref[:, cs])
                dx, dgr = _rms_bwd(nat_refs[j][...], xhat, r, gn_ref[...])
                out_ref[:, cs] = dx.astype(BF16)
                acc_ref[...] += jnp.sum(dgr, axis=0, keepdims=True)

    res = pl.BlockSpec((DA_GH, d, tn, HEAD), lambda i: (0, 0, i, 0))
    col = lambda off: pl.BlockSpec((rb, wide), lambda i: (i, off // wide + g))
    vec = pl.BlockSpec((1, HEAD), lambda i: (0, 0))
    nat = pl.BlockSpec((rb, wide), lambda i: (i, 0))
    shape = jax.ShapeDtypeStruct((rows, wide), BF16)
    gshape = jax.ShapeDtypeStruct((1, HEAD), F32)
    return _pallas_call(
        body, name=name, grid=(rows // rb,),
        in_specs=[res, res, res, col(C_DQ), col(C_DK), vec, vec],
        out_specs=[nat, nat, nat, vec, vec], out_shape=[shape, shape, shape, gshape, gshape],
        scratch_shapes=[pltpu.VMEM((rb, HEAD), F32)] * 3,
        compiler_params=_params(("arbitrary",)),
    )(dqr, dkr, dvr, proj, proj, q_gain, k_gain)


def _mem_fwd(proj, kv, q_gain, k_gain, *, name):
    rows = proj.shape[0]
    tm = min(1024, rows)
    n_mem = kv.shape[0]

    def body(q_ref, k_ref, v_ref, qg_ref, kg_ref, o_ref):
        qhat, _ = _rms(q_ref[...])
        khat, _ = _rms(k_ref[...])
        s = _dot_nt(qhat * qg_ref[...], khat * kg_ref[...]) * ATT_SCALE
        p = jnp.exp(s - jnp.max(s, axis=-1, keepdims=True))
        p = p / jnp.sum(p, axis=-1, keepdims=True)
        o_ref[...] = _dot(p, v_ref[...]).astype(BF16)

    vec = pl.BlockSpec((1, HEAD), lambda i, h: (0, 0))
    return _pallas_call(
        body, name=name, grid=(rows // tm, MEM_HEADS),
        in_specs=[pl.BlockSpec((tm, HEAD), lambda i, h: (i, C_MQ // HEAD + h)),
                  pl.BlockSpec((n_mem, HEAD), lambda i, h: (0, h)),
                  pl.BlockSpec((n_mem, HEAD), lambda i, h: (0, MEM_HEADS + h)), vec, vec],
        out_specs=pl.BlockSpec((tm, HEAD), lambda i, h: (i, h)),
        out_shape=jax.ShapeDtypeStruct((rows, MEM_HEADS * HEAD), BF16),
        compiler_params=_params(("parallel", "parallel")),
    )(proj, kv, kv, q_gain, k_gain)


def _mem_bwd(dout, proj, kv, q_gain, k_gain, dproj, *, name):
    rows = proj.shape[0]
    tm = min(1024, rows)
    steps = rows // tm
    n_mem = kv.shape[0]

    def body(d_ref, q_ref, k_ref, v_ref, qg_ref, kg_ref, dp_in, dq_ref, dk_ref, dv_ref, gq_ref, gk_ref, dkn_ref):
        del dp_in
        h, i = pl.program_id(0), pl.program_id(1)

        @pl.when((h == 0) & (i == 0))
        def _():
            gq_ref[...] = jnp.zeros_like(gq_ref)
            gk_ref[...] = jnp.zeros_like(gk_ref)

        @pl.when(i == 0)
        def _():
            dkn_ref[...] = jnp.zeros_like(dkn_ref)
            dv_ref[...] = jnp.zeros_like(dv_ref)

        qhat, rq = _rms(q_ref[...])
        khat, rk = _rms(k_ref[...])
        qn, kn = qhat * qg_ref[...], khat * kg_ref[...]
        s = _dot_nt(qn, kn) * ATT_SCALE
        p = jnp.exp(s - jnp.max(s, axis=-1, keepdims=True))
        p = p / jnp.sum(p, axis=-1, keepdims=True)
        dout = d_ref[...]
        dp = _dot_nt(dout, v_ref[...])
        ds = p * (dp - jnp.sum(p * dp, axis=-1, keepdims=True))
        dv_ref[...] += _dot_tn(p, dout)
        dkn_ref[...] += _dot_tn(ds, qn) * ATT_SCALE
        dq, dgr = _rms_bwd(_dot(ds, kn) * ATT_SCALE, qhat, rq, qg_ref[...])
        dq_ref[...] = dq.astype(BF16)
        gq_ref[...] += jnp.sum(dgr, axis=0, keepdims=True)

        @pl.when(i == steps - 1)
        def _():
            dk, dgk = _rms_bwd(dkn_ref[...], khat, rk, kg_ref[...])
            dk_ref[...] = dk
            gk_ref[...] += jnp.sum(dgk, axis=0, keepdims=True)

    vec = pl.BlockSpec((1, HEAD), lambda h, i: (0, 0))
    memh = pl.BlockSpec((n_mem, HEAD), lambda h, i: (0, h))
    tok = pl.BlockSpec((tm, HEAD), lambda h, i: (i, h))
    mq = pl.BlockSpec((tm, HEAD), lambda h, i: (i, C_MQ // HEAD + h))
    gshape = jax.ShapeDtypeStruct((1, HEAD), F32)
    return _pallas_call(
        body, name=name, grid=(MEM_HEADS, steps),
        in_specs=[tok, mq, memh, pl.BlockSpec((n_mem, HEAD), lambda h, i: (0, MEM_HEADS + h)), vec, vec, ANY],
        out_specs=[mq, memh, memh, vec, vec],
        out_shape=[jax.ShapeDtypeStruct(dproj.shape, dproj.dtype),
                   jax.ShapeDtypeStruct((n_mem, MEM_HEADS * HEAD), F32),
                   jax.ShapeDtypeStruct((n_mem, MEM_HEADS * HEAD), F32), gshape, gshape],
        input_output_aliases={6: 0},
        scratch_shapes=[pltpu.VMEM((n_mem, HEAD), F32)],
        compiler_params=_params(("arbitrary", "arbitrary")),
    )(dout, proj, kv, kv, q_gain, k_gain, dproj)


def _branch_fwd(o_hg, o_da, o_mem, proj, wp_hg, wp_da, wp_mem, *, name):
    rows = o_hg.shape[0]
    tm = min(512, rows)

    def body(a_ref, b_ref, c_ref, ga_ref, gb_ref, gc_ref, wa_ref, wb_ref, wc_ref, o_ref):
        merged = _sigmoid(ga_ref[...]) * _dot(a_ref[...], wa_ref[...])
        merged += _sigmoid(gb_ref[...]) * _dot(b_ref[...], wb_ref[...])
        merged += _sigmoid(gc_ref[...]) * _dot(c_ref[...], wc_ref[...])
        o_ref[...] = merged.astype(BF16)

    row = lambda w: pl.BlockSpec((tm, w), lambda i: (i, 0))
    gate = lambda off: pl.BlockSpec((tm, D_MODEL), lambda i: (i, off // D_MODEL))
    full = lambda a: pl.BlockSpec(a.shape, lambda i: (0, 0))
    return _pallas_call(
        body, name=name, grid=(rows // tm,),
        in_specs=[row(o_hg.shape[1]), row(o_da.shape[1]), row(o_mem.shape[1]),
                  gate(C_GHG), gate(C_GDA), gate(C_GMEM), full(wp_hg), full(wp_da), full(wp_mem)],
        out_specs=row(D_MODEL), out_shape=jax.ShapeDtypeStruct((rows, D_MODEL), BF16),
        compiler_params=_params(("parallel",)),
    )(o_hg, o_da, o_mem, proj, proj, proj, wp_hg, wp_da, wp_mem)


def _branch_bwd(dm, o_hg, o_da, o_mem, proj, wp_hg, wp_da, wp_mem, *, name):
    rows = o_hg.shape[0]
    tm = min(256, rows)

    def body(dm_ref, a_ref, b_ref, c_ref, ga_ref, gb_ref, gc_ref, wa_ref, wb_ref, wc_ref, dp_ref, *outs):
        dmv = dm_ref[...]
        for j, (o_ref, g_ref, w_ref) in enumerate(((a_ref, ga_ref, wa_ref), (b_ref, gb_ref, wb_ref),
                                                   (c_ref, gc_ref, wc_ref))):
            z = _dot(o_ref[...], w_ref[...])
            gs = _sigmoid(g_ref[...])
            dz = (dmv * gs).astype(BF16)
            dp_ref[:, j * D_MODEL:(j + 1) * D_MODEL] = (dmv * z * gs * (1.0 - gs)).astype(BF16)
            outs[2 * j][...] = dz
            outs[2 * j + 1][...] = _dot_nt(dz, w_ref[...])

    row = lambda w: pl.BlockSpec((tm, w), lambda i: (i, 0))
    gate = lambda off: pl.BlockSpec((tm, D_MODEL), lambda i: (i, off // D_MODEL))
    full = lambda a: pl.BlockSpec(a.shape, lambda i: (0, 0))
    out_specs = [pl.BlockSpec((pl.Element(tm), pl.Element(3 * D_MODEL)), lambda i: (i * tm, C_GHG))]
    out_shape = [jax.ShapeDtypeStruct((rows, IN_COLS), BF16)]
    for o in (o_hg, o_da, o_mem):
        out_specs += [row(D_MODEL), row(o.shape[1])]
        out_shape += [jax.ShapeDtypeStruct((rows, D_MODEL), BF16), jax.ShapeDtypeStruct((rows, o.shape[1]), F32)]
    return _pallas_call(
        body, name=name, grid=(rows // tm,),
        in_specs=[row(D_MODEL), row(o_hg.shape[1]), row(o_da.shape[1]), row(o_mem.shape[1]),
                  gate(C_GHG), gate(C_GDA), gate(C_GMEM), full(wp_hg), full(wp_da), full(wp_mem)],
        out_specs=out_specs, out_shape=out_shape,
        compiler_params=_params(("parallel",)),
    )(dm, o_hg, o_da, o_mem, proj, proj, proj, wp_hg, wp_da, wp_mem)


def _ffn_in(h2, w_ab, *, name):
    rows, dff = h2.shape[0], w_ab.shape[1] // 2
    tm, tn = min(2048, rows), 256

    def body(h_ref, wa_ref, wb_ref, a_ref, b_ref, u_ref):
        a = _dot(h_ref[...], wa_ref[...])
        b = _dot(h_ref[...], wb_ref[...])
        a_ref[...] = a.astype(BF16)
        b_ref[...] = b.astype(BF16)
        u_ref[...] = (a * _sigmoid(a) * b).astype(BF16)

    out = pl.BlockSpec((tm, tn), lambda i, j: (i, j))
    return _pallas_call(
        body, name=name, grid=(rows // tm, dff // tn),
        in_specs=[pl.BlockSpec((tm, D_MODEL), lambda i, j: (i, 0)),
                  pl.BlockSpec((D_MODEL, tn), lambda i, j: (0, j)),
                  pl.BlockSpec((D_MODEL, tn), lambda i, j: (0, dff // tn + j))],
        out_specs=[out, out, out],
        out_shape=[jax.ShapeDtypeStruct((rows, dff), BF16)] * 3,
        compiler_params=_params(("parallel", "parallel")),
    )(h2, w_ab, w_ab)


def _ffn_act_bwd(dy, w_out, a, b, *, name):
    rows, dff = a.shape
    tm, tn = min(1024, rows), 256

    def body(dy_ref, w_ref, a_ref, b_ref, da_ref, db_ref):
        du = _dot_nt(dy_ref[...], w_ref[...])
        av, bv = a_ref[...].astype(F32), b_ref[...].astype(F32)
        sa = _sigmoid(av)
        da_ref[...] = (du * bv * sa * (1.0 + av * (1.0 - sa))).astype(BF16)
        db_ref[...] = (du * av * sa).astype(BF16)

    tile = pl.BlockSpec((tm, tn), lambda i, j: (i, j))
    return _pallas_call(
        body, name=name, grid=(rows // tm, dff // tn),
        in_specs=[pl.BlockSpec((tm, D_MODEL), lambda i, j: (i, 0)),
                  pl.BlockSpec((tn, D_MODEL), lambda i, j: (j, 0)), tile, tile],
        out_specs=[tile, tile],
        out_shape=[jax.ShapeDtypeStruct((rows, dff), BF16), jax.ShapeDtypeStruct((rows, dff), BF16)],
        compiler_params=_params(("parallel", "parallel")),
    )(dy, w_out, a, b)


def _lower_bound(lb_fw, lb_bw, *, name):
    def body(a_ref, b_ref, oa_ref, ob_ref):
        for src, dst in ((a_ref, oa_ref), (b_ref, ob_ref)):
            dst[...] = _sigmoid(src[0:1, :] - src[1:2, :])

    shape = jax.ShapeDtypeStruct((1, lb_fw.shape[1]), F32)
    return pl.pallas_call(body, name=name, out_shape=[shape, shape])(lb_fw, lb_bw)


def _local_step(x, mem, tgt, p, ex):
    rows = x.shape[0]
    lb_fw, lb_bw = _lower_bound(p["lb_logits_fw"], p["lb_logits_bw"], name="lower_bound")

    h = _rms_fwd(x, p["norm_mix_gain"], name="norm_mix")
    w = {"w_in": ex.w_in}
    proj, carried = _matmul([(h, w["w_in"])], "nn", BF16, tm=1024, tn=1024, carry=ex.late_carry(), name="proj_in")
    w.update(ex.late_weights(carried))
    o_fw, st_fw = _gla_fwd(proj, lb_fw, f_off=C_FF, rev=False, name="gla_fwd_fw")
    o_bw, st_bw = _gla_fwd(proj, lb_bw, f_off=C_FB, rev=True, name="gla_fwd_bw")
    o_hg = _hg_out_fwd(o_fw, o_bw, proj, p["hg_norm_gain"], name="hg_out")

    qkv_r, outs, lses = [], [], []
    for g in range(len(DA_GROUPS)):
        qr, kr, vr = _da_prep(proj, p["da_q_gain"], p["da_k_gain"], g, name=f"da_prep{g}")
        og, lg = _band_fwd(qr, kr, vr, g, name=f"band_fwd{g}")
        qkv_r.append((qr, kr, vr))
        outs.append(og)
        lses.append(lg)
    o_da, lse_da = _da_merge(outs, lses, rows, name="da_merge")

    mem_n = _rms_fwd(mem, p["norm_mem_gain"], name="norm_mem")
    kv = _matmul([(mem_n, w["w_mem_kv"])], "nn", F32, tm=256, tn=512, name="mem_kv")
    o_mem = _mem_fwd(proj, kv, p["mem_q_gain"], p["mem_k_gain"], name="mem_attn")

    merged = _branch_fwd(o_hg, o_da, o_mem, proj, w["w_proj_hg"], w["w_proj_da"], w["w_proj_mem"],
                         name="branch_merge")
    x1, h2 = _out_norm(merged, w["w_out"], x, p["norm_ffn_gain"], name="mix_out_norm")
    a, b, u = _ffn_in(h2, w["w_ffn_in"], name="ffn_in")
    dy, dy_b, loss = _out_loss(u, w["w_ffn_out"], x1, tgt, name="ffn_out_loss")

    gw, gwb, gs = {}, {}, {}
    gw["w_ffn_out"], gwb["w_ffn_out"] = _matmul([(u, dy_b)], "tn", F32, tm=256, tn=1024, also=BF16, name="g_ffn_out")
    da, db = _ffn_act_bwd(dy_b, w["w_ffn_out"], a, b, name="ffn_act_bwd")
    gw["w_ffn_a"], gwb["w_ffn_a"] = _matmul([(h2, da)], "tn", F32, tm=512, tn=256, also=BF16, name="g_ffn_a")
    gw["w_ffn_b"], gwb["w_ffn_b"] = _matmul([(h2, db)], "tn", F32, tm=512, tn=256, also=BF16, name="g_ffn_b")
    dh2 = _matmul([(da, w["w_ffn_in"], 0), (db, w["w_ffn_in"], 1)], "nt", F32, tm=256, tn=1024, name="d_h2")
    dx1, dx1_b, gs["norm_ffn_gain"] = _rms_bwd_rows(dh2, x1, p["norm_ffn_gain"], dy, (F32, BF16), name="norm_ffn_bwd")
    gw["w_out"], gwb["w_out"] = _matmul([(merged, dx1_b)], "tn", F32, tm=512, tn=512, also=BF16, name="g_out")
    dmerged = _matmul([(dx1_b, w["w_out"])], "nt", F32, tm=512, tn=512, name="d_merged")
    dproj, dz_hg, do_hg, dz_da, do_da, dz_mem, do_mem = _branch_bwd(
        dmerged, o_hg, o_da, o_mem, proj, w["w_proj_hg"], w["w_proj_da"], w["w_proj_mem"], name="branch_bwd")
    gw["w_proj_hg"], gwb["w_proj_hg"] = _matmul([(o_hg, dz_hg)], "tn", F32, tm=512, tn=512, also=BF16, name="g_proj_hg")
    gw["w_proj_da"], gwb["w_proj_da"] = _matmul([(o_da, dz_da)], "tn", F32, tm=512, tn=512, also=BF16, name="g_proj_da")
    gw["w_proj_mem"], gwb["w_proj_mem"] = _matmul([(o_mem, dz_mem)], "tn", F32, tm=512, tn=512, also=BF16, name="g_proj_mem")

    dproj, dk_mem, dv_mem, gs["mem_q_gain"], gs["mem_k_gain"] = _mem_bwd(
        do_mem, proj, kv, p["mem_q_gain"], p["mem_k_gain"], dproj, name="mem_attn_bwd")
    dkv = jnp.concatenate([dk_mem, dv_mem], axis=1).astype(BF16)
    gw["w_mem_kv"], gwb["w_mem_kv"] = _matmul([(mem_n, dkv)], "tn", F32, tm=512, tn=512, also=BF16, name="g_mem_kv")
    dmem_n = _matmul([(dkv, w["w_mem_kv"])], "nt", F32, tm=256, tn=512, name="d_mem_n")
    (gs["norm_mem_gain"],) = _rms_bwd_rows(dmem_n, mem, p["norm_mem_gain"], None, (), name="norm_mem_bwd")

    prep = _da_bwd_prep(do_da, o_da, lse_da, name="da_bwd_prep")
    d_da, gq_parts, gk_parts = [], [], []
    for g in range(len(DA_GROUPS)):
        qr, kr, vr = qkv_r[g]
        dor, lser, deltar = prep[3 * g:3 * g + 3]
        dqr, dkr, dvr = _band_bwd(qr, kr, vr, dor, lser, deltar, g, name=f"band_bwd{g}")
        dq, dk, dv, gq, gk = _da_prep_bwd(dqr, dkr, dvr, proj, p["da_q_gain"], p["da_k_gain"], g,
                                          name=f"da_prep_bwd{g}")
        d_da.append((dq, dk, dv))
        gq_parts.append(gq)
        gk_parts.append(gk)
    for j, (off, tag) in enumerate(((C_DQ, "q"), (C_DK, "k"), (C_DV, "v"))):
        dproj = _fill_columns(dproj, [t[j] for t in d_da], off, name=f"dproj_fill_da_{tag}")

    dproj, do_gla, gs["hg_norm_gain"] = _hg_out_bwd(do_hg, o_fw, o_bw, proj, p["hg_norm_gain"], dproj,
                                                    name="hg_out_bwd")
    carry_bulk, carry_rest = ex.early_carries(gw, gwb)
    (dq_f, dproj, dv_f, dlb_fw), bulk = _gla_bwd(proj, lb_fw, do_gla, st_fw, None, dproj, f_off=C_FF, rev=False,
                                                 carry=carry_bulk, name="gla_bwd_fw")
    (dproj, dfl_bw, dv_hg, dlb_bw), rest = _gla_bwd(proj, lb_bw, do_gla, st_bw, (dq_f, dv_f), dproj, f_off=C_FB,
                                                    rev=True, carry=carry_rest, name="gla_bwd_bw")
    ex.early_done(bulk, rest)
    dproj = _fill_columns(dproj, [dfl_bw, dv_hg], C_FB, name="dproj_fill")
    g_in, g_in_b = _matmul([(h, dproj)], "tn", F32, tm=512, tn=1024, also=BF16, name="g_in")
    dh, carried = _matmul([(dproj, w["w_in"])], "nt", F32, tm=1024, tn=1024, tk=IN_COLS // 8,
                          carry=ex.w_in_carry(g_in, g_in_b), name="d_h")
    ex.w_in_done(carried)
    grad_x, gs["norm_mix_gain"] = _rms_bwd_rows(dh, x, p["norm_mix_gain"], dx1, (F32,), name="norm_mix_bwd")

    small = _small_pack(gs, gq_parts, gk_parts, dlb_fw, dlb_bw, lb_fw, lb_bw, name="small_pack")
    return loss, grad_x, small


def _small_pack(gs, gq_parts, gk_parts, dlb_fw, dlb_bw, lb_fw, lb_bw, *, name):
    def body(g_mix, g_mem, g_ffn, dfw, dbw, lfw, lbw, g_hg, q0, q1, q2, k0, k1, k2, g_mq, g_mk, o_ref):
        o_ref[0:1, :] = g_mix[...]
        o_ref[1:2, :] = g_mem[...]
        o_ref[2:3, :] = g_ffn[...]
        for base, d_ref, l_ref in ((3, dfw, lfw), (5, dbw, lbw)):
            lbv = l_ref[...]
            t = d_ref[...] * lbv * (1.0 - lbv)
            o_ref[base:base + 1, :] = t
            o_ref[base + 1:base + 2, :] = -t
        o_ref[7:8, :] = jnp.zeros((1, D_MODEL), F32)
        o_ref[7:8, 0:HEAD] = g_hg[...]
        o_ref[7:8, HEAD:2 * HEAD] = q0[...] + q1[...] + q2[...]
        o_ref[7:8, 2 * HEAD:3 * HEAD] = k0[...] + k1[...] + k2[...]
        o_ref[7:8, 3 * HEAD:4 * HEAD] = g_mq[...]
        o_ref[7:8, 4 * HEAD:5 * HEAD] = g_mk[...]

    return pl.pallas_call(body, name=name, out_shape=jax.ShapeDtypeStruct((8, D_MODEL), F32))(
        gs["norm_mix_gain"], gs["norm_mem_gain"], gs["norm_ffn_gain"], dlb_fw, dlb_bw, lb_fw, lb_bw,
        gs["hg_norm_gain"], *gq_parts, *gk_parts, gs["mem_q_gain"], gs["mem_k_gain"])


def _row_tile(rows, cols, n_arrays):
    budget = (16 * 1024 * 1024) // (2 * 4 * cols * n_arrays)
    tr = rows
    while tr > budget and tr % 2 == 0 and (tr // 2) % 16 == 0:
        tr //= 2
    return tr


def _cast_into_full(a, chip, rows, cols, axis, *, name):
    sr, sc = a.shape
    tr = _row_tile(sr, sc, 2)

    def body(chip_ref, a_ref, o_ref):
        del chip_ref
        o_ref[...] = a_ref[...].astype(BF16)

    if axis == 1:
        out_map = lambda i, chip_ref: (i, chip_ref[0])
    else:
        out_map = lambda i, chip_ref: (chip_ref[0] * (sr // tr) + i, 0)
    return _pallas_call(
        body, name=name,
        grid_spec=pltpu.PrefetchScalarGridSpec(
            num_scalar_prefetch=1, grid=(sr // tr,),
            in_specs=[pl.BlockSpec((tr, sc), lambda i, chip_ref: (i, 0))],
            out_specs=pl.BlockSpec((tr, sc), out_map)),
        out_shape=jax.ShapeDtypeStruct((rows, cols), BF16),
        compiler_params=_params(("parallel",)))(chip, a)


def _add_halves(items, *, name):
    rows = items[0][3].shape[0]
    widths = [ra.shape[1] for _, _, _, ra in items]
    tr = _row_tile(rows, sum(widths), 4)

    def body(*refs):
        o_ref = refs[-1]
        first = lax.axis_index("c") == 0
        off = 0
        for j, wd in enumerate(widths):
            h0, h1, ra = refs[3 * j:3 * j + 3]
            o_ref[:, off:off + wd] = (jnp.where(first, h0[...], h1[...]) + ra[...]).astype(BF16)
            off += wd

    in_specs, ins = [], []
    for (g, haxis, hsize, ra), wd in zip(items, widths):
        if haxis == 0:
            in_specs += [pl.BlockSpec((tr, wd), lambda i: (i, 0)),
                         pl.BlockSpec((tr, wd), lambda i, o=hsize // tr: (o + i, 0))]
        else:
            in_specs += [pl.BlockSpec((tr, wd), lambda i: (i, 0)), pl.BlockSpec((tr, wd), lambda i: (i, 1))]
        in_specs.append(pl.BlockSpec((tr, wd), lambda i: (i, 0)))
        ins += [g, g, ra]
    return _pallas_call(body, name=name, grid=(rows // tr,), in_specs=in_specs,
                          out_specs=pl.BlockSpec((tr, sum(widths)), lambda i: (i, 0)),
                          out_shape=jax.ShapeDtypeStruct((rows, sum(widths)), BF16),
                          compiler_params=_params(("parallel",)))(*ins)


def _add_slots(rb, *, name):
    _, rows, cols = rb.shape
    tr = _row_tile(rows, cols, 5)

    def body(r0, r1, r2, r3, o_ref):
        o_ref[...] = ((r0[...].astype(F32) + r1[...].astype(F32)) + r2[...].astype(F32)) + r3[...].astype(F32)

    slot = lambda s: pl.BlockSpec((None, tr, cols), lambda i: (s, i, 0))
    return _pallas_call(body, name=name, grid=(rows // tr,), in_specs=[slot(s) for s in range(4)],
                          out_specs=pl.BlockSpec((tr, cols), lambda i: (i, 0)),
                          out_shape=jax.ShapeDtypeStruct((rows, cols), F32),
                          compiler_params=_params(("parallel",)))(rb, rb, rb, rb)


def _adamw(w, g, m, v, *, name):
    rows, cols = w.shape
    tr = _row_tile(rows, cols, 8) if rows % 16 == 0 else rows
    c1 = 1.0 - ADAM_B1 ** ADAM_STEP
    c2 = 1.0 - ADAM_B2 ** ADAM_STEP

    def body(w_ref, g_ref, m_ref, v_ref, d_ref, mo_ref, vo_ref, go_ref):
        gv = g_ref[...]
        go_ref[...] = gv
        mn = ADAM_B1 * m_ref[...] + (1.0 - ADAM_B1) * gv
        vn = ADAM_B2 * v_ref[...] + (1.0 - ADAM_B2) * (gv * gv)
        mo_ref[...] = mn
        vo_ref[...] = vn
        d_ref[...] = -ADAM_LR * ((mn / c1) / (jnp.sqrt(vn / c2) + ADAM_EPS) + ADAM_WD * w_ref[...])

    spec = pl.BlockSpec((tr, cols), lambda i: (i, 0))
    shape = jax.ShapeDtypeStruct((rows, cols), F32)
    return _pallas_call(body, name=name, grid=(rows // tr,), in_specs=[spec] * 4, out_specs=[spec] * 4,
                        out_shape=[shape] * 4, compiler_params=_params(("parallel",)))(w, g, m, v)


W_SPECS = (
    ("w_in", 1024, IN_COLS, 1, IN_COLS // 4),
    ("w_mem_kv", 1024, 1024, 0, 256),
    ("w_proj_hg", 1024, 1024, 0, 256),
    ("w_proj_da", 512, 1024, 1, 256),
    ("w_proj_mem", 512, 1024, 1, 256),
    ("w_out", 1024, 1024, 0, 256),
    ("w_ffn_in", 1024, 2 * D_FF, 1, 2 * D_FF // 4),
    ("w_ffn_out", D_FF, 1024, 0, D_FF // 4),
)
CHIP_FLIPS = ((1, 0), (0, 1), (1, 1))
ANY = pl.BlockSpec(memory_space=pl.ANY)
DMA_CHUNK_BYTES = 1 << 20
STAGE_BYTES = 2 << 20


def _place():
    x, y, c = lax.axis_index("x"), lax.axis_index("y"), lax.axis_index("c")
    return x, y, c, 2 * x + y


def _flip(v, f):
    return 1 - v if f else v


def _slab(ref, axis, idx, size):
    start = pl.multiple_of(idx * size, size)
    return ref.at[pl.ds(start, size), :] if axis == 0 else ref.at[:, pl.ds(start, size)]


def _chunked(make, src, dst, want="both"):
    rows, cols = src.shape
    row_bytes = cols * jnp.dtype(src.dtype).itemsize
    k = 1
    while rows % (2 * k) == 0 and (rows // (2 * k)) % 16 == 0 and (rows // k) * row_bytes > DMA_CHUNK_BYTES:
        k *= 2
    cr = rows // k
    parts = []
    if want != "wait":
        parts = [make(src.at[pl.ds(j * cr, cr), :], dst.at[pl.ds(j * cr, cr), :]) for j in range(k)]
    return parts, (make(src, dst) if want != "start" else None)


def _half_spec(rows, cols, axis):
    return (0, rows // 2) if axis == 1 else (1, cols // 2)


def _staged(src, remote_dst, local_dst, sibling, load_sems, send_sems, store_sems, recv_sem):
    rows, cols = src.shape
    row_bytes = cols * jnp.dtype(src.dtype).itemsize
    k = 1
    while (rows // k) * row_bytes > STAGE_BYTES and rows % (2 * k) == 0 and (rows // (2 * k)) % 16 == 0:
        k *= 2
    cr = rows // k
    piece = lambda ref, j: ref.at[pl.ds(j * cr, cr), :]

    def run(buf):
        loads = [pltpu.make_async_copy(piece(src, j), buf.at[j % 2], load_sems.at[j % 2]) for j in range(k)]
        outs = [[pltpu.make_async_remote_copy(src_ref=buf.at[j % 2], dst_ref=piece(remote_dst, j),
                                              send_sem=send_sems.at[j % 2], recv_sem=recv_sem,
                                              device_id=sibling, device_id_type=MESH)] for j in range(k)]
        if local_dst is not None:
            for j in range(k):
                outs[j].append(pltpu.make_async_copy(buf.at[j % 2], piece(local_dst, j), store_sems.at[j % 2]))

        def drained(j):
            outs[j][0].wait_send()
            for cp in outs[j][1:]:
                cp.wait()

        loads[0].start()
        for j in range(k):
            loads[j].wait()
            for cp in outs[j]:
                cp.start()
            if j + 1 < k:
                if j >= 1:
                    drained(j - 1)
                loads[j + 1].start()
        for j in range(max(0, k - 2), k):
            drained(j)

    pl.run_scoped(run, pltpu.VMEM((2, cr, cols), src.dtype))


def _landed(ref, recv_sem, send_sem):
    pltpu.make_async_remote_copy(src_ref=ref, dst_ref=ref, send_sem=send_sem, recv_sem=recv_sem,
                                 device_id=(lax.axis_index("x"), lax.axis_index("y"), lax.axis_index("c")),
                                 device_id_type=MESH).wait_recv()


def _half_slab(ref, spec, chip, half):
    _, rows, cols, axis, size = spec
    haxis, hsize = _half_spec(rows, cols, axis)
    return _slab(_slab(ref, axis, chip, size), haxis, half, hsize)


def _gather_sends(outs, specs, ici_send, ici_recv, want="both"):
    x, y, c, p = _place()
    sent = []
    for wi, spec in enumerate(specs):
        mine = _half_slab(outs[wi], spec, p, c)
        for k, (fx, fy) in enumerate(CHIP_FLIPS):
            sent.append(_chunked(lambda s, d, j=3 * wi + k, fx=fx, fy=fy: pltpu.make_async_remote_copy(
                src_ref=s, dst_ref=d, send_sem=ici_send.at[j], recv_sem=ici_recv.at[j],
                device_id=(_flip(x, fx), _flip(y, fy), c), device_id_type=MESH), mine, mine, want))
    return sent


def _gather_weights(fulls, specs, *, name):
    n = len(specs)

    def body(*refs):
        outs = refs[n:2 * n]
        ici_send, ici_recv, load_sems, d2d_send, d2d_recv = refs[2 * n:]
        x, y, c, _ = _place()
        sent = _gather_sends(outs, specs, ici_send, ici_recv)
        for parts, _ in sent:
            for cp in parts:
                cp.start()
        for k, (fx, fy) in enumerate(CHIP_FLIPS):
            q = 2 * _flip(x, fx) + _flip(y, fy)
            for wi in range(n):
                sent[3 * wi + k][1].wait_recv()
                got = _half_slab(outs[wi], specs[wi], q, c)
                _staged(got, got, None, (x, y, 1 - c), load_sems, d2d_send, None, d2d_recv.at[3 * wi + k])
        for _, whole in sent:
            whole.wait_send()
        for k, (fx, fy) in enumerate(CHIP_FLIPS):
            q = 2 * _flip(x, fx) + _flip(y, fy)
            for wi in range(n):
                _landed(_half_slab(outs[wi], specs[wi], q, 1 - c), d2d_recv.at[3 * wi + k], d2d_send.at[0])

    return _pallas_call(
        body, name=name, in_specs=[ANY] * n, out_specs=[ANY] * n,
        out_shape=[jax.ShapeDtypeStruct(f.shape, f.dtype) for f in fulls],
        input_output_aliases={i: i for i in range(n)},
        scratch_shapes=[pltpu.SemaphoreType.DMA((3 * n,)), pltpu.SemaphoreType.DMA((3 * n,)),
                        pltpu.SemaphoreType.DMA((2,)), pltpu.SemaphoreType.DMA((2,)),
                        pltpu.SemaphoreType.DMA((3 * n,))],
    )(*fulls)


def _gather_chips_carry(fulls, specs):
    n = len(specs)

    def issue(ins, outs, sems, want):
        del ins
        return [(parts, whole, True) for parts, whole in _gather_sends(outs, specs, sems[0], sems[1], want)]

    return _Carry(list(fulls), [jax.ShapeDtypeStruct(f.shape, f.dtype) for f in fulls], {i: i for i in range(n)},
                  [pltpu.SemaphoreType.DMA((3 * n,)), pltpu.SemaphoreType.DMA((3 * n,))], issue)


def _gather_pass_on(fulls, specs, *, name):
    n = len(specs)

    def body(*refs):
        outs = refs[n:2 * n]
        load_sems, d2d_send, d2d_recv = refs[2 * n:2 * n + 3]
        bufs = refs[2 * n + 3:]
        x, y, c, _ = _place()
        loads, sends = [], []
        for k, (fx, fy) in enumerate(CHIP_FLIPS):
            q = 2 * _flip(x, fx) + _flip(y, fy)
            for wi in range(n):
                j = 3 * wi + k
                got = _half_slab(outs[wi], specs[wi], q, c)
                loads.append(pltpu.make_async_copy(got, bufs[j], load_sems.at[j]))
                sends.append(pltpu.make_async_remote_copy(
                    src_ref=bufs[j], dst_ref=got, send_sem=d2d_send.at[j], recv_sem=d2d_recv.at[j],
                    device_id=(x, y, 1 - c), device_id_type=MESH))
        for cp in loads:
            cp.start()
        for load, send in zip(loads, sends):
            load.wait()
            send.start()
        for cp in sends:
            cp.wait_send()
        for k, (fx, fy) in enumerate(CHIP_FLIPS):
            q = 2 * _flip(x, fx) + _flip(y, fy)
            for wi in range(n):
                _landed(_half_slab(outs[wi], specs[wi], q, 1 - c), d2d_recv.at[3 * wi + k], d2d_send.at[0])

    shapes = []
    for _, rows, cols, axis, size in specs:
        shapes += [(rows // 2, size) if axis == 1 else (size, cols // 2)] * 3
    assert sum(math.prod(s) for s in shapes) * 2 <= VMEM_LIMIT_V7X // 2
    stages = [pltpu.VMEM(s, BF16) for s in shapes]
    return _pallas_call(
        body, name=name, in_specs=[ANY] * n, out_specs=[ANY] * n,
        out_shape=[jax.ShapeDtypeStruct(f.shape, f.dtype) for f in fulls],
        input_output_aliases={i: i for i in range(n)},
        scratch_shapes=[pltpu.SemaphoreType.DMA((3 * n,))] * 3 + stages,
        compiler_params=pltpu.CompilerParams(vmem_limit_bytes=VMEM_LIMIT_V7X),
    )(*fulls)


def _sibling_exchange(grads, *, name):
    n = len(grads)

    def body(*refs):
        ins, outs = refs[:n], refs[n:2 * n]
        load_sems, send_sems, recv_sems = refs[2 * n:]
        x, y, c, _ = _place()
        for i, (_, haxis, hsize) in enumerate(grads):
            _staged(_slab(ins[i], haxis, 1 - c, hsize), outs[i], None, (x, y, 1 - c),
                    load_sems, send_sems, None, recv_sems.at[i])
        for i in range(n):
            _landed(outs[i], recv_sems.at[i], send_sems.at[0])

    shapes = [jax.ShapeDtypeStruct((hsize, g.shape[1]) if haxis == 0 else (g.shape[0], hsize), g.dtype)
              for g, haxis, hsize in grads]
    return _pallas_call(
        body, name=name, in_specs=[ANY] * n, out_specs=[ANY] * n, out_shape=shapes,
        scratch_shapes=[pltpu.SemaphoreType.DMA((2,)), pltpu.SemaphoreType.DMA((2,)),
                        pltpu.SemaphoreType.DMA((n,))],
    )(*[g for g, _, _ in grads])


def _chip_exchange_carry(parts):
    n = len(parts)

    def issue(ins, outs, sems, want):
        send_sems, recv_sems, local_sems = sems
        x, y, c, p = _place()
        copies = []
        for i, (_, axis, size) in enumerate(parts):
            copies.append(_chunked(lambda s, d, i=i: pltpu.make_async_copy(s, d, local_sems.at[i]),
                                   _slab(ins[i], axis, p, size), outs[i].at[p], want) + (False,))
            for k, (fx, fy) in enumerate(CHIP_FLIPS):
                px, py = _flip(x, fx), _flip(y, fy)
                copies.append(_chunked(lambda s, d, j=3 * i + k, px=px, py=py: pltpu.make_async_remote_copy(
                    src_ref=s, dst_ref=d, send_sem=send_sems.at[j], recv_sem=recv_sems.at[j],
                    device_id=(px, py, c), device_id_type=MESH),
                    _slab(ins[i], axis, 2 * px + py, size), outs[i].at[p], want) + (True,))
        return copies

    shapes = []
    for a, axis, size in parts:
        shapes.append(jax.ShapeDtypeStruct((4, size, a.shape[1]) if axis == 0 else (4, a.shape[0], size), a.dtype))
    return _Carry([a for a, _, _ in parts], shapes, {},
                  [pltpu.SemaphoreType.DMA((3 * n,)), pltpu.SemaphoreType.DMA((3 * n,)),
                   pltpu.SemaphoreType.DMA((n,))], issue)


def _sibling_share(sums):
    n = len(sums)

    def body(*refs):
        ins, outs = refs[:n], refs[n:2 * n]
        load_sems, send_sems, store_sems, recv_sems = refs[2 * n:]
        x, y, c, _ = _place()
        for i, (s, haxis) in enumerate(sums):
            place = _slab(outs[i], haxis, c, s.shape[haxis])
            _staged(ins[i], place, place, (x, y, 1 - c), load_sems, send_sems, store_sems, recv_sems.at[i])
        for i, (s, haxis) in enumerate(sums):
            _landed(_slab(outs[i], haxis, 1 - c, s.shape[haxis]), recv_sems.at[i], send_sems.at[0])

    shapes = []
    for s, haxis in sums:
        r, cc = s.shape
        shapes.append(jax.ShapeDtypeStruct((2 * r, cc) if haxis == 0 else (r, 2 * cc), F32))
    return _pallas_call(
        body, name="grad_sibling_share", in_specs=[ANY] * n, out_specs=[ANY] * n, out_shape=shapes,
        scratch_shapes=[pltpu.SemaphoreType.DMA((2,)), pltpu.SemaphoreType.DMA((2,)),
                        pltpu.SemaphoreType.DMA((2,)), pltpu.SemaphoreType.DMA((n,))],
    )(*[s for s, _ in sums])


class _Exchanges:
    def __init__(self, fulls):
        self.w_in = _gather_weights([fulls["w_in"]], W_SPECS[:1], name="gather_w_in")[0]
        self.late_specs = W_SPECS[1:]
        self.late = [fulls[s[0]] for s in self.late_specs]
        self.slots = {}

    def late_carry(self):
        return _gather_chips_carry(self.late, self.late_specs)

    def late_weights(self, carried):
        done = _gather_pass_on(carried, self.late_specs, name="gather_pass_on")
        return dict(zip([s[0] for s in self.late_specs], done))

    def _half_sums(self, gw, gwb, specs, tag):
        grads, sent = [], []
        for name, _, _, axis, _ in specs:
            for part in (("w_ffn_a", "w_ffn_b") if name == "w_ffn_in" else (name,)):
                half = _half_spec(gw[part].shape[0], gw[part].shape[1], axis)
                grads.append((gw[part],) + half)
                sent.append((gwb[part],) + half)
        theirs = _sibling_exchange(sent, name=f"grad_sibling_exchange_{tag}")
        parts, j = [], 0
        for name, _, _, axis, size in specs:
            take = 2 if name == "w_ffn_in" else 1
            items = [grads[i] + (theirs[i],) for i in range(j, j + take)]
            parts.append((_add_halves(items, name=f"half_sum_{name}"), axis, size))
            j += take
        return parts

    def early_carries(self, gw, gwb):
        parts = self._half_sums(gw, gwb, self.late_specs, "early")
        self.early_names = [s[0] for s in self.late_specs]
        cut = self.early_names.index("w_ffn_in")
        return _chip_exchange_carry(parts[cut:]), _chip_exchange_carry(parts[:cut])

    def early_done(self, carried_bulk, carried_rest):
        cut = self.early_names.index("w_ffn_in")
        self.slots.update(zip(self.early_names[cut:], carried_bulk))
        self.slots.update(zip(self.early_names[:cut], carried_rest))

    def w_in_carry(self, g, gb):
        return _chip_exchange_carry(self._half_sums({"w_in": g}, {"w_in": gb}, W_SPECS[:1], "w_in"))

    def w_in_done(self, carried):
        self.slots["w_in"] = carried[0]

    def reduced(self):
        sums = []
        for name, rows, cols, axis, _ in W_SPECS:
            sums.append((_add_slots(self.slots[name], name=f"chip_sum_{name}"), _half_spec(rows, cols, axis)[0]))
        return dict(zip([s[0] for s in W_SPECS], _sibling_share(sums)))


def _small_allreduce(sv):
    def body(sv_ref, o_ref, slots_ref, send_sems, recv_sems):
        x, y, c, _ = _place()
        me = 4 * x + 2 * y + c
        slots_ref[me] = sv_ref[...]
        copies = []
        for k in range(1, 8):
            fx, fy, fc = (k >> 2) & 1, (k >> 1) & 1, k & 1
            copies.append(pltpu.make_async_remote_copy(
                src_ref=sv_ref, dst_ref=slots_ref.at[me], send_sem=send_sems.at[k - 1],
                recv_sem=recv_sems.at[k - 1], device_id=(_flip(x, fx), _flip(y, fy), _flip(c, fc)),
                device_id_type=MESH))
        for cp in copies:
            cp.start()
        for cp in copies:
            cp.wait_recv()
        for cp in copies:
            cp.wait_send()
        total = slots_ref[0]
        for s in range(1, 8):
            total = total + slots_ref[s]
        o_ref[...] = total

    vm = pl.BlockSpec(memory_space=pltpu.VMEM)
    return pl.pallas_call(
        body, name="small_allreduce", in_specs=[vm], out_specs=vm,
        out_shape=jax.ShapeDtypeStruct(sv.shape, F32),
        scratch_shapes=[pltpu.VMEM((8,) + sv.shape, F32), pltpu.SemaphoreType.DMA((7,)),
                        pltpu.SemaphoreType.DMA((7,))],
    )(sv)


SMALL_ROWS = (("norm_mix_gain", 0), ("norm_mem_gain", 1), ("norm_ffn_gain", 2))
SMALL_LB = (("lb_logits_fw", 3), ("lb_logits_bw", 5))
SMALL_HEAD = ("hg_norm_gain", "da_q_gain", "da_k_gain", "mem_q_gain", "mem_k_gain")


def _pack_small(d):
    last = jnp.concatenate([d[n] for n in SMALL_HEAD] + [jnp.zeros((1, D_MODEL - HEAD * len(SMALL_HEAD)), F32)], axis=1)
    return jnp.concatenate([d["norm_mix_gain"], d["norm_mem_gain"], d["norm_ffn_gain"],
                            d["lb_logits_fw"], d["lb_logits_bw"], last], axis=0)


def _unpack_small(a):
    out = {n: a[r:r + 1] for n, r in SMALL_ROWS}
    out.update({n: a[r:r + 2] for n, r in SMALL_LB})
    out.update({n: a[7:8, j * HEAD:(j + 1) * HEAD] for j, n in enumerate(SMALL_HEAD)})
    return out


PARAM_ORDER = ("norm_mix_gain", "norm_mem_gain", "w_in", "lb_logits_fw", "lb_logits_bw", "hg_norm_gain",
               "da_q_gain", "da_k_gain", "w_mem_kv", "mem_q_gain", "mem_k_gain", "w_proj_hg", "w_proj_da",
               "w_proj_mem", "w_out", "norm_ffn_gain", "w_ffn_in", "w_ffn_out")


def kernel(x, mem, norm_mix_gain, norm_mem_gain, w_in, lb_logits_fw, lb_logits_bw, hg_norm_gain, da_q_gain, da_k_gain, w_mem_kv, mem_q_gain, mem_k_gain, w_proj_hg, w_proj_da, w_proj_mem, w_out, norm_ffn_gain, w_ffn_in, w_ffn_out, loss_target, m_norm_mix_gain, m_norm_mem_gain, m_w_in, m_lb_logits_fw, m_lb_logits_bw, m_hg_norm_gain, m_da_q_gain, m_da_k_gain, m_w_mem_kv, m_mem_q_gain, m_mem_k_gain, m_w_proj_hg, m_w_proj_da, m_w_proj_mem, m_w_out, m_norm_ffn_gain, m_w_ffn_in, m_w_ffn_out, v_norm_mix_gain, v_norm_mem_gain, v_w_in, v_lb_logits_fw, v_lb_logits_bw, v_hg_norm_gain, v_da_q_gain, v_da_k_gain, v_w_mem_kv, v_mem_q_gain, v_mem_k_gain, v_w_proj_hg, v_w_proj_da, v_w_proj_mem, v_w_out, v_norm_ffn_gain, v_w_ffn_in, v_w_ffn_out):
    args = dict(locals())
    mats = tuple(s[0] for s in W_SPECS)
    flat = lambda a: a.reshape(a.shape[-2:])
    w = {n: flat(args[n]) for n in mats}
    m = {n: flat(args["m_" + n]) for n in mats}
    v = {n: flat(args["v_" + n]) for n in mats}
    small = {n: args[n] for n in PARAM_ORDER if n not in mats}

    chip = (2 * lax.axis_index("x") + lax.axis_index("y")).astype(jnp.int32).reshape(1)
    ex = _Exchanges({n: _cast_into_full(w[n], chip, rows, cols, axis, name=f"cast_{n}")
                     for n, rows, cols, axis, _ in W_SPECS})
    loss, grad_x, small_grads = _local_step(x[0], mem[0], loss_target[0], small, ex)
    grads = ex.reduced()
    small_sum = _small_allreduce(small_grads)

    delta, new_m, new_v = {}, {}, {}
    for n in mats:
        delta[n], new_m[n], new_v[n], grads[n] = _adamw(w[n], grads[n], m[n], v[n], name=f"adamw_{n}")
    packed = _adamw(_pack_small(small), small_sum,
                    _pack_small({n: args["m_" + n] for n in small}),
                    _pack_small({n: args["v_" + n] for n in small}), name="adamw_small")
    for dst, src in zip((delta, new_m, new_v, grads), packed):
        dst.update(_unpack_small(src))

    def shaped(d, n):
        return d[n].reshape(args[n].shape)

    loss_sum = lax.psum(loss[0, 0], ("x", "y", "c"))
    return (loss_sum, grad_x[None], *[shaped(grads, n) for n in PARAM_ORDER], *[shaped(delta, n) for n in PARAM_ORDER],
            *[shaped(new_m, n) for n in PARAM_ORDER], *[shaped(new_v, n) for n in PARAM_ORDER])
```

```python
import functools
import math

import numpy as np
import jax
import jax.numpy as jnp
from jax import lax
from jax.experimental import pallas as pl
from jax.experimental.pallas import tpu as pltpu

F32 = jnp.float32
BF16 = jnp.bfloat16
MESH = pl.DeviceIdType.MESH

D_MODEL = 1024
HEAD = 128
HG_HEADS = 8
DA_GROUPS = ((1, 64), (4, 64), (16, 64))
DA_GH = 4
MEM_HEADS = 4
N_MEM = 256
D_FF = 2816
CHUNK = 64
BAND_QBLOCKS = 8
GLA_HEADS_PER_STEP = 4
RMS_EPS = 1e-6
NEG_INF = -1e30
HG_SCALE = HEAD ** -0.5
ATT_SCALE = HEAD ** -0.5
VMEM_LIMIT_V7X = 48 * 1024 * 1024

C_HQ, C_FF, C_FB, C_HI, C_HG = 0, 1024, 2048, 3072, 4096
C_DQ, C_DK, C_DV, C_MQ = 5120, 6656, 8192, 9728
C_GHG, C_GDA, C_GMEM = 10240, 11264, 12288
IN_COLS = 13312

ADAM_LR, ADAM_B1, ADAM_B2, ADAM_EPS, ADAM_WD, ADAM_STEP = 0.001, 0.9, 0.999, 1e-08, 0.01, 10


def _params(sem, vmem=VMEM_LIMIT_V7X):
    return pltpu.CompilerParams(dimension_semantics=sem, vmem_limit_bytes=vmem)


def _dot(a, b):
    return jnp.dot(a.astype(BF16), b.astype(BF16), preferred_element_type=F32)


def _dot_nt(a, b):
    return lax.dot_general(a.astype(BF16), b.astype(BF16), (((1,), (1,)), ((), ())),
                           preferred_element_type=F32)


def _dot_tn(a, b):
    return lax.dot_general(a.astype(BF16), b.astype(BF16), (((0,), (0,)), ((), ())),
                           preferred_element_type=F32)


def _sigmoid(v):
    return jax.nn.sigmoid(v.astype(F32))


def _ones(rows, cols):
    return (lax.broadcasted_iota(jnp.int32, (rows, cols), 0) >= 0).astype(BF16)


def _lane_sum(v):
    ones = _ones(HEAD, HEAD)
    hi = v.astype(BF16)
    mid = (v - hi.astype(F32)).astype(BF16)
    return jnp.dot(hi, ones, preferred_element_type=F32) + jnp.dot(mid, ones, preferred_element_type=F32)


def _row_mean(v):
    if v.shape[-1] == HEAD:
        return _lane_sum(v) * (1.0 / HEAD)
    return jnp.mean(v, axis=-1, keepdims=True)


def _rms(v):
    v = v.astype(F32)
    r = lax.rsqrt(_row_mean(v * v) + RMS_EPS)
    return v * r, r


def _rms_bwd(dy, xhat, r, gain):
    dxh = dy * gain
    dx = r * (dxh - xhat * _row_mean(dxh * xhat))
    return dx, dy * xhat


class _Carry:
    def __init__(self, arrays, out_shapes, aliases, sems, issue):
        self.arrays, self.out_shapes, self.aliases, self.sems, self.issue = arrays, out_shapes, aliases, sems, issue


NO_CARRY = object()


def _start_copies(copies):
    for parts, _, _ in copies:
        for cp in parts:
            cp.start()


def _wait_copies(copies):
    for _, whole, remote in copies:
        if remote:
            whole.wait_recv()
    for _, whole, remote in copies:
        if remote:
            whole.wait_send()
        else:
            whole.wait()


def _call(body, *, name, grid, in_specs, out_specs, out_shape, scratch_shapes, semantics, ins, carry=None,
          aliases=None):
    aliases = dict(aliases or {})
    if carry is None:
        res = pl.pallas_call(body, name=name, grid=grid, in_specs=in_specs, out_specs=out_specs, out_shape=out_shape,
                             scratch_shapes=scratch_shapes, input_output_aliases=aliases,
                             compiler_params=_params(semantics))(*ins)
        return list(res), []
    n_in, n_out, n_scr = len(ins), len(out_shape), len(scratch_shapes)
    c_in, c_out = len(carry.arrays), len(carry.out_shapes)

    def wrapped(*refs):
        pos = [0]

        def take(count):
            pos[0] += count
            return refs[pos[0] - count:pos[0]]

        own_in, carry_in = take(n_in), take(c_in)
        own_out, carry_out = take(n_out), take(c_out)
        own_scr, carry_sems = take(n_scr), take(len(carry.sems))
        ids = [pl.program_id(a) for a in range(len(grid))]
        first, last = ids[0] == 0, ids[0] == grid[0] - 1
        for a in range(1, len(grid)):
            first, last = first & (ids[a] == 0), last & (ids[a] == grid[a] - 1)

        @pl.when(first)
        def _():
            _start_copies(carry.issue(carry_in, carry_out, carry_sems, "start"))

        body(*own_in, *own_out, *own_scr)

        @pl.when(last)
        def _():
            _wait_copies(carry.issue(carry_in, carry_out, carry_sems, "wait"))

    res = pl.pallas_call(
        wrapped, name=name, grid=grid, in_specs=list(in_specs) + [ANY] * c_in,
        out_specs=list(out_specs) + [ANY] * c_out, out_shape=list(out_shape) + list(carry.out_shapes),
        input_output_aliases={**aliases, **{n_in + i: n_out + j for i, j in carry.aliases.items()}},
        scratch_shapes=list(scratch_shapes) + list(carry.sems),
        compiler_params=_params(("arbitrary",) * len(grid)))(*ins, *carry.arrays)
    return list(res[:n_out]), list(res[n_out:])


def _matmul(pairs, mode, out_dtype, *, tm, tn, tk=None, residual=None, also=None, carry=NO_CARRY, name):
    b_offs = [pr[2] if len(pr) > 2 else 0 for pr in pairs]
    pairs = [pr[:2] for pr in pairs]
    a0, b0 = pairs[0]
    if mode == "nn":
        (m, kk), n = a0.shape, b0.shape[1]
    elif mode == "nt":
        (m, kk), n = a0.shape, b0.shape[0]
    else:
        (kk, m), n = a0.shape, b0.shape[1]
    assert mode == "nt" or not any(b_offs)
    tm, tn = min(tm, m), min(tn, n)
    tk = kk if tk is None else tk
    nk = kk // tk
    assert m % tm == 0 and n % tn == 0 and kk % tk == 0, (name, m, n, kk)
    n_p = len(pairs)
    if mode == "tn":
        a_spec = pl.BlockSpec((tk, tm), lambda i, j, k: (k, i))
    else:
        a_spec = pl.BlockSpec((tm, tk), lambda i, j, k: (i, k))
    if mode == "nt":
        b_specs = [pl.BlockSpec((tn, tk), lambda i, j, k, o=o: (j, o * nk + k)) for o in b_offs]
    else:
        b_specs = [pl.BlockSpec((tk, tn), lambda i, j, k: (k, j))] * n_p
    o_spec = pl.BlockSpec((tm, tn), lambda i, j, k: (i, j))
    dot = {"nn": _dot, "nt": _dot_nt, "tn": _dot_tn}[mode]
    has_res = residual is not None

    def body(*refs):
        a_refs, b_refs = refs[:n_p], refs[n_p:2 * n_p]
        pos = 2 * n_p
        res_ref = refs[pos] if has_res else None
        pos += int(has_res)
        o_ref = refs[pos]
        pos += int(also is not None)
        part = dot(a_refs[0][...], b_refs[0][...])
        for a_r, b_r in zip(a_refs[1:], b_refs[1:]):
            part += dot(a_r[...], b_r[...])

        def finish(total):
            if has_res:
                total = total + res_ref[...]
            o_ref[...] = total.astype(out_dtype)
            if also is not None:
                refs[pos][...] = total.astype(also)

        if nk == 1:
            finish(part)
        else:
            acc_ref = refs[pos + 1]
            k = pl.program_id(2)

            @pl.when(k == 0)
            def _():
                acc_ref[...] = part

            @pl.when(k > 0)
            def _():
                acc_ref[...] += part

            @pl.when(k == nk - 1)
            def _():
                finish(acc_ref[...])

    ins = [a for a, _ in pairs] + [b for _, b in pairs]
    in_specs = [a_spec] * n_p + b_specs
    if has_res:
        ins.append(residual)
        in_specs.append(o_spec)
    dtypes = [out_dtype] + ([also] if also is not None else [])
    outs, carried = _call(
        body, name=name, grid=(m // tm, n // tn, nk), in_specs=in_specs, out_specs=[o_spec] * len(dtypes),
        out_shape=[jax.ShapeDtypeStruct((m, n), dt) for dt in dtypes],
        scratch_shapes=[pltpu.VMEM((tm, tn), F32)] if nk > 1 else [],
        semantics=("parallel", "parallel", "arbitrary"), ins=ins, carry=None if carry is NO_CARRY else carry)
    out = outs[0] if also is None else tuple(outs)
    return out if carry is NO_CARRY else (out, carried)


def _rms_fwd(x, gain, *, name):
    rows, dm = x.shape
    tm = min(1024, rows)

    def body(x_ref, g_ref, h_ref):
        xhat, _ = _rms(x_ref[...])
        h_ref[...] = (xhat * g_ref[...]).astype(BF16)

    return pl.pallas_call(
        body, name=name, grid=(rows // tm,),
        in_specs=[pl.BlockSpec((tm, dm), lambda i: (i, 0)), pl.BlockSpec((1, dm), lambda i: (0, 0))],
        out_specs=pl.BlockSpec((tm, dm), lambda i: (i, 0)),
        out_shape=jax.ShapeDtypeStruct((rows, dm), BF16),
        compiler_params=_params(("parallel",)),
    )(x, gain)


def _rms_bwd_rows(dh, x, gain, dres, dx_dtypes, *, name):
    rows, dm = x.shape
    tm = min(512, rows)
    has_res = dres is not None

    def body(*refs):
        dh_ref, x_ref, g_ref = refs[:3]
        res_ref = refs[3] if has_res else None
        outs = refs[3 + int(has_res):]
        dg_ref = outs[-1]
        xhat, r = _rms(x_ref[...])
        dx, dgr = _rms_bwd(dh_ref[...], xhat, r, g_ref[...])
        if has_res:
            dx = dx + res_ref[...]
        for dx_ref, dt in zip(outs, dx_dtypes):
            dx_ref[...] = dx.astype(dt)

        @pl.when(pl.program_id(0) == 0)
        def _():
            dg_ref[...] = jnp.zeros_like(dg_ref)

        dg_ref[...] += jnp.sum(dgr, axis=0, keepdims=True)

    row = pl.BlockSpec((tm, dm), lambda i: (i, 0))
    vec = pl.BlockSpec((1, dm), lambda i: (0, 0))
    return pl.pallas_call(
        body, name=name, grid=(rows // tm,),
        in_specs=[row, row, vec] + ([row] if has_res else []),
        out_specs=[row] * len(dx_dtypes) + [vec],
        out_shape=[jax.ShapeDtypeStruct((rows, dm), dt) for dt in dx_dtypes] + [jax.ShapeDtypeStruct((1, dm), F32)],
        compiler_params=_params(("arbitrary",)),
    )(*([dh, x, gain] + ([dres] if has_res else [])))


def _out_norm(a, w, x, gain, *, name):
    rows, dm = x.shape
    tm = min(512, rows)

    def body(a_ref, w_ref, x_ref, g_ref, x1_ref, h_ref):
        x1 = x_ref[...] + _dot(a_ref[...], w_ref[...])
        x1_ref[...] = x1
        xhat, _ = _rms(x1)
        h_ref[...] = (xhat * g_ref[...]).astype(BF16)

    row = lambda wd: pl.BlockSpec((tm, wd), lambda i: (i, 0))
    return pl.pallas_call(
        body, name=name, grid=(rows // tm,),
        in_specs=[row(a.shape[1]), pl.BlockSpec(w.shape, lambda i: (0, 0)), row(dm), pl.BlockSpec((1, dm), lambda i: (0, 0))],
        out_specs=[row(dm), row(dm)],
        out_shape=[jax.ShapeDtypeStruct((rows, dm), F32), jax.ShapeDtypeStruct((rows, dm), BF16)],
        compiler_params=_params(("parallel",)),
    )(a, w, x, gain)


def _out_loss(u, w, x1, tgt, *, name):
    rows, dm = x1.shape
    tm = min(256, rows)
    steps = rows // tm

    def body(u_ref, w_ref, x_ref, t_ref, dy_ref, dyb_ref, loss_ref, acc_ref):
        i = pl.program_id(0)
        diff = x_ref[...] + _dot(u_ref[...], w_ref[...]) - t_ref[...]
        dy = diff * (1.0 / dm)
        dy_ref[...] = dy
        dyb_ref[...] = dy.astype(BF16)

        @pl.when(i == 0)
        def _():
            acc_ref[...] = jnp.zeros_like(acc_ref)

        acc_ref[...] += jnp.sum(diff * diff, axis=0, keepdims=True)

        @pl.when(i == steps - 1)
        def _():
            loss_ref[...] = jnp.full((1, HEAD), 0.5 / dm, F32) * jnp.sum(acc_ref[...])

    row = lambda wd: pl.BlockSpec((tm, wd), lambda i: (i, 0))
    return pl.pallas_call(
        body, name=name, grid=(steps,),
        in_specs=[row(u.shape[1]), pl.BlockSpec(w.shape, lambda i: (0, 0)), row(dm), row(dm)],
        out_specs=[row(dm), row(dm), pl.BlockSpec((1, HEAD), lambda i: (0, 0))],
        out_shape=[jax.ShapeDtypeStruct((rows, dm), F32), jax.ShapeDtypeStruct((rows, dm), BF16),
                   jax.ShapeDtypeStruct((1, HEAD), F32)],
        scratch_shapes=[pltpu.VMEM((1, dm), F32)],
        compiler_params=_params(("arbitrary",)),
    )(u, w, x1, tgt)


def _gla_block_terms(q_raw, f_logit, lb, rev):
    sig = _sigmoid(f_logit)
    forget = lb + (1.0 - lb) * sig
    k = 1.0 - forget
    b = _chunk_cumsum(jnp.log(forget), rev)
    qs = _sigmoid(q_raw)
    eb = jnp.exp(b)
    emb = jnp.exp(-b)
    qt = (q_raw * qs * HG_SCALE) * eb
    kt = k * emb
    return sig, forget, k, b, qs, eb, emb, qt, kt


def _chunk_cumsum(v, rev):
    n = v.shape[0]
    pos = lax.broadcasted_iota(jnp.int32, v.shape, 0) & (CHUNK - 1)
    step = 1
    while step < CHUNK:
        if rev:
            shifted, keep = pltpu.roll(v, n - step, 0), pos < CHUNK - step
        else:
            shifted, keep = pltpu.roll(v, step, 0), pos >= step
        v = v + jnp.where(keep, shifted, 0.0)
        step *= 2
    return v


def _tri_mask(n, rev):
    row, col = np.arange(n)[:, None], np.arange(n)[None, :]
    same = (row // CHUNK) == (col // CHUNK)
    return jnp.asarray((same & ((row <= col) if rev else (row >= col))).astype(np.float32))


def _chunk_order(ncb, rev):
    order = range(ncb - 1, -1, -1) if rev else range(ncb)
    return [(c, c * CHUNK if rev else c * CHUNK + CHUNK - 1) for c in order]


def _gla_fwd(proj, lb, *, f_off, rev, name):
    rows = proj.shape[0]
    tb = min(256, rows)
    nb, ncb = rows // tb, tb // CHUNK

    def tmap(n):
        return nb - 1 - n if rev else n

    def body(tri_ref, q_ref, f_ref, v_ref, lb_ref, o_ref, st_ref, s_ref):
        @pl.when(pl.program_id(1) == 0)
        def _():
            s_ref[...] = jnp.zeros_like(s_ref)

        tri = tri_ref[...] > 0.5
        for hh in range(GLA_HEADS_PER_STEP):
            cs = slice(hh * HEAD, (hh + 1) * HEAD)
            v = v_ref[:, cs]
            _, _, k, b, _, _, _, qt, kt = _gla_block_terms(q_ref[:, cs].astype(F32), f_ref[:, cs].astype(F32),
                                                           lb_ref[:, cs], rev)
            o_intra = _dot(jnp.where(tri, _dot_nt(qt, kt), 0.0), v)
            chunks = []
            for c, last in _chunk_order(ncb, rev):
                sl = slice(c * CHUNK, (c + 1) * CHUNK)
                bl = b[last:last + 1, :]
                kh = k[sl] * jnp.exp(bl - b[sl])
                chunks.append((c, sl, jnp.exp(bl), _dot_tn(v[sl], kh)))
            s_t = s_ref[hh]
            for c, sl, ebl, kv in chunks:
                st_ref[hh, c] = s_t
                o_ref[sl, cs] = o_intra[sl] + _dot_nt(qt[sl], s_t)
                s_t = ebl * s_t + kv
            s_ref[hh] = s_t

    hps = GLA_HEADS_PER_STEP
    col = lambda off: pl.BlockSpec((tb, hps * HEAD), lambda h, n: (tmap(n), off // (hps * HEAD) + h))
    return pl.pallas_call(
        body, name=name, grid=(HG_HEADS // hps, nb),
        in_specs=[pl.BlockSpec((tb, tb), lambda h, n: (0, 0)), col(C_HQ), col(f_off), col(C_HI),
                  pl.BlockSpec((1, hps * HEAD), lambda h, n: (0, h))],
        out_specs=[pl.BlockSpec((tb, hps * HEAD), lambda h, n: (tmap(n), h)),
                   pl.BlockSpec((hps, ncb, HEAD, HEAD), lambda h, n: (h, tmap(n), 0, 0))],
        out_shape=[jax.ShapeDtypeStruct((rows, HG_HEADS * HEAD), F32),
                   jax.ShapeDtypeStruct((HG_HEADS, rows // CHUNK, HEAD, HEAD), F32)],
        scratch_shapes=[pltpu.VMEM((hps, HEAD, HEAD), F32)],
        compiler_params=_params(("parallel", "arbitrary")),
    )(_tri_mask(tb, rev), proj, proj, proj, lb)


def _gla_bwd(proj, lb, do, states, prev, dproj, *, f_off, rev, carry=NO_CARRY, name):
    rows = proj.shape[0]
    tb = min(256, rows)
    nb, ncb = rows // tb, tb // CHUNK
    has_prev = prev is not None
    qv_dtype = BF16 if has_prev else F32

    def tmap(n):
        return n if rev else nb - 1 - n

    def body(*refs):
        tri_ref, q_ref, f_ref, v_ref, lb_ref, do_ref, st_ref = refs[:7]
        pq_ref, pv_ref = refs[8:10] if has_prev else (None, None)
        (dq_ref, df_ref, dv_ref, dlb_ref,
         ds_ref, dqt_scr, dk_scr, db_scr, dbl_scr, dv_scr) = refs[8 + 2 * int(has_prev):]

        @pl.when(pl.program_id(1) == 0)
        def _():
            ds_ref[...] = jnp.zeros_like(ds_ref)
            dlb_ref[...] = jnp.zeros_like(dlb_ref)

        tri = tri_ref[...] > 0.5
        for hh in range(GLA_HEADS_PER_STEP):
            cs = slice(hh * HEAD, (hh + 1) * HEAD)
            lbv = lb_ref[:, cs]
            q_raw = q_ref[:, cs].astype(F32)
            v, dout = v_ref[:, cs], do_ref[:, cs].astype(BF16)
            sig, forget, k, b, qs, eb, emb, qt, kt = _gla_block_terms(q_raw, f_ref[:, cs].astype(F32), lbv, rev)
            a = jnp.where(tri, _dot_nt(qt, kt), 0.0)
            da = jnp.where(tri, _dot_nt(dout, v), 0.0)
            dv_intra = _dot_tn(a, dout)
            dqt_intra = _dot(da, kt)
            dkt = _dot_tn(da, qt)
            chunks = []
            for c, last in reversed(_chunk_order(ncb, rev)):
                sl = slice(c * CHUNK, (c + 1) * CHUNK)
                bl = b[last:last + 1, :]
                e = jnp.exp(bl - b[sl])
                s_t = st_ref[hh, c]
                dqt_scr[sl, cs] = dqt_intra[sl] + _dot(dout[sl], s_t)
                chunks.append((sl, jnp.exp(bl), e, k[sl] * e, s_t, _dot_tn(dout[sl], qt[sl])))
            ds_t = ds_ref[hh]
            for sl, ebl, e, kh, s_t, grow in chunks:
                dkh = _dot(v[sl], ds_t)
                dv_scr[sl, cs] = dv_intra[sl] + _dot_nt(kh, ds_t)
                dk_scr[sl, cs] = dkt[sl] * emb[sl] + dkh * e
                khd = kh * dkh
                dbl = jnp.sum(khd, axis=0, keepdims=True) + ebl * jnp.sum(ds_t * s_t, axis=0, keepdims=True)
                db_scr[sl, cs] = khd
                dbl_scr[sl, cs] = jnp.broadcast_to(dbl, (CHUNK, HEAD))
                ds_t = grow + ds_t * ebl
            ds_ref[hh] = ds_t
            dqt = dqt_scr[:, cs]
            dlogf = _chunk_cumsum(qt * dqt - kt * dkt - db_scr[:, cs], not rev) + dbl_scr[:, cs]
            dforget = dlogf / forget - dk_scr[:, cs]
            df_ref[:, cs] = (dforget * (1.0 - lbv) * sig * (1.0 - sig)).astype(BF16)
            dlb_ref[:, cs] += jnp.sum(dforget * (1.0 - sig), axis=0, keepdims=True)
            dqr = dqt * eb * (HG_SCALE * qs * (1.0 + q_raw * (1.0 - qs)))
            dv = dv_scr[:, cs]
            if has_prev:
                dqr = dqr + pq_ref[:, cs]
                dv = dv + pv_ref[:, cs]
            dq_ref[:, cs] = dqr.astype(qv_dtype)
            dv_ref[:, cs] = dv.astype(qv_dtype)

    hps = GLA_HEADS_PER_STEP
    col = lambda off: pl.BlockSpec((tb, hps * HEAD), lambda h, n: (tmap(n), off // (hps * HEAD) + h))
    blk = pl.BlockSpec((tb, hps * HEAD), lambda h, n: (tmap(n), h))
    vec = pl.BlockSpec((1, hps * HEAD), lambda h, n: (0, h))
    wide = HG_HEADS * HEAD
    into = jax.ShapeDtypeStruct(dproj.shape, dproj.dtype)
    outs, carried = _call(
        body, name=name, grid=(HG_HEADS // hps, nb),
        in_specs=[pl.BlockSpec((tb, tb), lambda h, n: (0, 0)), col(C_HQ), col(f_off), col(C_HI), vec, blk,
                  pl.BlockSpec((hps, ncb, HEAD, HEAD), lambda h, n: (h, tmap(n), 0, 0)), ANY]
                 + ([blk, blk] if has_prev else []),
        out_specs=[col(C_HQ) if has_prev else blk, blk if has_prev else col(f_off), blk, vec],
        out_shape=[into if has_prev else jax.ShapeDtypeStruct((rows, wide), qv_dtype),
                   jax.ShapeDtypeStruct((rows, wide), BF16) if has_prev else into,
                   jax.ShapeDtypeStruct((rows, wide), qv_dtype), jax.ShapeDtypeStruct((1, wide), F32)],
        aliases={7: 0 if has_prev else 1},
        scratch_shapes=[pltpu.VMEM((hps, HEAD, HEAD), F32)] + [pltpu.VMEM((tb, hps * HEAD), F32)] * 5,
        semantics=("parallel", "arbitrary"),
        ins=[_tri_mask(tb, rev), proj, proj, proj, lb, do, states, dproj] + (list(prev) if has_prev else []),
        carry=None if carry is NO_CARRY else carry)
    return outs if carry is NO_CARRY else (outs, carried)


def _fill_columns(dproj, parts, col, *, name):
    rows, wd = parts[0].shape
    tm = min(1024, rows)
    n = len(parts)

    def body(*refs):
        o_ref = refs[n + 1]
        for j in range(n):
            o_ref[:, j * wd:(j + 1) * wd] = refs[j][...]

    return pl.pallas_call(
        body, name=name, grid=(rows // tm,),
        in_specs=[pl.BlockSpec((tm, wd), lambda i: (i, 0))] * n + [ANY],
        out_specs=pl.BlockSpec((pl.Element(tm), pl.Element(n * wd)), lambda i: (i * tm, col)),
        out_shape=jax.ShapeDtypeStruct(dproj.shape, dproj.dtype), input_output_aliases={n: 0},
        compiler_params=_params(("parallel",)),
    )(*parts, dproj)


def _hg_out_fwd(o_fw, o_bw, proj, gain, *, name):
    rows = o_fw.shape[0]
    tm = min(1024, rows)
    wide = HG_HEADS * HEAD

    def body(a_ref, b_ref, g_ref, gain_ref, o_ref):
        for h in range(HG_HEADS):
            sl = slice(h * HEAD, (h + 1) * HEAD)
            xhat, _ = _rms(a_ref[:, sl] + b_ref[:, sl])
            gate = g_ref[:, sl].astype(F32)
            o_ref[:, sl] = (xhat * gain_ref[...] * (gate * _sigmoid(gate))).astype(BF16)

    row = pl.BlockSpec((tm, wide), lambda i: (i, 0))
    return pl.pallas_call(
        body, name=name, grid=(rows // tm,),
        in_specs=[row, row, pl.BlockSpec((tm, wide), lambda i: (i, C_HG // wide)),
                  pl.BlockSpec((1, HEAD), lambda i: (0, 0))],
        out_specs=row, out_shape=jax.ShapeDtypeStruct((rows, wide), BF16),
        compiler_params=_params(("parallel",)),
    )(o_fw, o_bw, proj, gain)


def _hg_out_bwd(dout, o_fw, o_bw, proj, gain, dproj, *, name):
    rows = o_fw.shape[0]
    tm = min(512, rows)
    wide = HG_HEADS * HEAD

    def body(d_ref, a_ref, b_ref, g_ref, gain_ref, dp_in, dgate_ref, do_ref, dgain_ref):
        del dp_in

        @pl.when(pl.program_id(0) == 0)
        def _():
            dgain_ref[...] = jnp.zeros_like(dgain_ref)

        dgain = jnp.zeros((1, HEAD), F32)
        for h in range(HG_HEADS):
            sl = slice(h * HEAD, (h + 1) * HEAD)
            xhat, r = _rms(a_ref[:, sl] + b_ref[:, sl])
            gate, dy = g_ref[:, sl].astype(F32), d_ref[:, sl]
            gs = _sigmoid(gate)
            dgate_ref[:, sl] = (dy * xhat * gain_ref[...] * (gs * (1.0 + gate * (1.0 - gs)))).astype(BF16)
            dx, dgr = _rms_bwd(dy * (gate * gs), xhat, r, gain_ref[...])
            do_ref[:, sl] = dx
            dgain = dgain + jnp.sum(dgr, axis=0, keepdims=True)
        dgain_ref[...] += dgain

    row = pl.BlockSpec((tm, wide), lambda i: (i, 0))
    gate = pl.BlockSpec((tm, wide), lambda i: (i, C_HG // wide))
    vec = pl.BlockSpec((1, HEAD), lambda i: (0, 0))
    return pl.pallas_call(
        body, name=name, grid=(rows // tm,),
        in_specs=[row, row, row, gate, vec, ANY],
        out_specs=[gate, row, vec],
        out_shape=[jax.ShapeDtypeStruct(dproj.shape, dproj.dtype), jax.ShapeDtypeStruct((rows, wide), F32),
                   jax.ShapeDtypeStruct((1, HEAD), F32)],
        input_output_aliases={5: 0},
        compiler_params=_params(("arbitrary",)),
    )(dout, o_fw, o_bw, proj, gain, dproj)


def _strided_rows(ref, r, count, d):
    return ref[...] if d == 1 else ref[pl.ds(r, count, stride=d), :]


def _store_strided(ref, r, count, d, val):
    if d == 1:
        ref[...] = val
    else:
        ref[pl.ds(r, count, stride=d), :] = val


def _da_prep(proj, q_gain, k_gain, g, *, name):
    d = DA_GROUPS[g][0]
    rows = proj.shape[0]
    rb = min(1024, rows)
    tn = rb // d

    def body(q_ref, k_ref, v_ref, qg_ref, kg_ref, qo_ref, ko_ref, vo_ref, qf_ref, kf_ref, vf_ref):
        for h in range(DA_GH):
            cs = slice(h * HEAD, (h + 1) * HEAD)
            for src, dst in ((q_ref, qf_ref), (k_ref, kf_ref), (v_ref, vf_ref)):
                dst[...] = src[:, cs].astype(F32)
            for r in range(d):
                qhat, _ = _rms(_strided_rows(qf_ref, r, tn, d))
                khat, _ = _rms(_strided_rows(kf_ref, r, tn, d))
                qo_ref[h, r] = (qhat * qg_ref[...]).astype(BF16)
                ko_ref[h, r] = (khat * kg_ref[...]).astype(BF16)
                vo_ref[h, r] = _strided_rows(vf_ref, r, tn, d).astype(BF16)

    wide = DA_GH * HEAD
    col = lambda off: pl.BlockSpec((rb, wide), lambda i: (i, off // wide + g))
    vec = pl.BlockSpec((1, HEAD), lambda i: (0, 0))
    out = pl.BlockSpec((DA_GH, d, tn, HEAD), lambda i: (0, 0, i, 0))
    shape = jax.ShapeDtypeStruct((DA_GH, d, rows // d, HEAD), BF16)
    return pl.pallas_call(
        body, name=name, grid=(rows // rb,),
        in_specs=[col(C_DQ), col(C_DK), col(C_DV), vec, vec],
        out_specs=[out, out, out], out_shape=[shape, shape, shape],
        scratch_shapes=[pltpu.VMEM((rb, HEAD), F32)] * 3,
        compiler_params=_params(("parallel",)),
    )(proj, proj, proj, q_gain, k_gain)


def _slopes(g):
    idx = np.arange(g * DA_GH + 1, (g + 1) * DA_GH + 1)
    s = (2.0 ** (-8.0 * idx / (DA_GH * len(DA_GROUPS)))).astype(np.float32)
    return jnp.asarray(np.broadcast_to(s[:, None, None], (DA_GH, 8, HEAD)).copy())


def _band_window(ld, t, radius):
    win = min(2 * t, ld)
    assert t // 2 >= radius or win == ld
    return win


def _band_scores(q, k, q0, start, slope, d, radius):
    t, win = q.shape[0], k.shape[0]
    row = lax.broadcasted_iota(jnp.int32, (t, win), 0)
    col = lax.broadcasted_iota(jnp.int32, (t, win), 1)
    rel = jnp.abs((start - q0) + col - row)
    return _dot_nt(q, k) * ATT_SCALE + jnp.where(rel <= radius, -slope * (d * rel).astype(F32), NEG_INF)


def _band_fwd(qr, kr, vr, g, *, name):
    d, radius = DA_GROUPS[g]
    _, _, ld, _ = qr.shape
    t = min(HEAD, ld)
    qb = min(BAND_QBLOCKS, ld // t)
    rps = max(1, min(d, BAND_QBLOCKS // qb))
    win = _band_window(ld, t, radius)

    def body(q_ref, k_ref, v_ref, sl_ref, o_ref, lse_ref):
        i = pl.program_id(2)
        slope = sl_ref[0:1, 0:1]
        for rr in range(rps):
            for j in range(qb):
                sl = slice(j * t, (j + 1) * t)
                q0 = (i * qb + j) * t
                start = pl.multiple_of(jnp.clip(q0 - t // 2, 0, ld - win), t // 2)
                s = _band_scores(q_ref[rr, sl, :], k_ref[rr, pl.ds(start, win), :], q0, start, slope, d, radius)
                m = jnp.max(s, axis=-1, keepdims=True)
                p = jnp.exp(s - m).astype(BF16)
                l = jnp.dot(p, _ones(win, HEAD), preferred_element_type=F32)
                o_ref[rr, sl, :] = _dot(p, v_ref[rr, pl.ds(start, win), :]) / l
                lse_ref[rr, sl, :] = m + jnp.log(l)

    own = pl.BlockSpec((None, rps, qb * t, HEAD), lambda h, r, i: (h, r, i, 0))
    seq = pl.BlockSpec((None, rps, ld, HEAD), lambda h, r, i: (h, r, 0, 0))
    shape = jax.ShapeDtypeStruct(qr.shape, F32)
    return pl.pallas_call(
        body, name=name, grid=(DA_GH, d // rps, ld // (qb * t)),
        in_specs=[own, seq, seq, pl.BlockSpec((None, 8, HEAD), lambda h, r, i: (h, 0, 0))],
        out_specs=[own, own], out_shape=[shape, shape],
        compiler_params=_params(("parallel", "parallel", "parallel")),
    )(qr, kr, vr, _slopes(g))


def _band_bwd(qr, kr, vr, dor, lser, deltar, g, *, name):
    d, radius = DA_GROUPS[g]
    _, _, ld, _ = qr.shape
    t = min(HEAD, ld)
    qb = min(BAND_QBLOCKS, ld // t)
    rps = max(1, min(d, BAND_QBLOCKS // qb))
    win = _band_window(ld, t, radius)

    def body(q_ref, k_ref, v_ref, do_ref, lse_ref, dl_ref, sl_ref, dq_ref, dk_ref, dv_ref):
        i = pl.program_id(2)

        @pl.when(i == 0)
        def _():
            dk_ref[...] = jnp.zeros_like(dk_ref)
            dv_ref[...] = jnp.zeros_like(dv_ref)

        slope = sl_ref[0:1, 0:1]
        for rr in range(rps):
            for j in range(qb):
                sl = slice(j * t, (j + 1) * t)
                q0 = (i * qb + j) * t
                start = pl.multiple_of(jnp.clip(q0 - t // 2, 0, ld - win), t // 2)
                rows = pl.ds(start, win)
                q, dout, k, v = q_ref[rr, sl, :], do_ref[rr, sl, :], k_ref[rr, rows, :], v_ref[rr, rows, :]
                p = jnp.exp(_band_scores(q, k, q0, start, slope, d, radius) - lse_ref[rr, sl, 0:1])
                ds = p * (_dot_nt(dout, v) - dl_ref[rr, sl, 0:1])
                dq_ref[rr, sl, :] = _dot(ds, k) * ATT_SCALE
                dk_ref[rr, rows, :] += _dot_tn(ds, q) * ATT_SCALE
                dv_ref[rr, rows, :] += _dot_tn(p, dout)

    own = pl.BlockSpec((None, rps, qb * t, HEAD), lambda h, r, i: (h, r, i, 0))
    seq = pl.BlockSpec((None, rps, ld, HEAD), lambda h, r, i: (h, r, 0, 0))
    shape = jax.ShapeDtypeStruct(qr.shape, F32)
    return pl.pallas_call(
        body, name=name, grid=(DA_GH, d // rps, ld // (qb * t)),
        in_specs=[own, seq, seq, own, own, own, pl.BlockSpec((None, 8, HEAD), lambda h, r, i: (h, 0, 0))],
        out_specs=[own, seq, seq], out_shape=[shape, shape, shape],
        compiler_params=_params(("parallel", "parallel", "arbitrary")),
    )(qr, kr, vr, dor, lser, deltar, _slopes(g))


def _da_merge(outs, lses, rows, *, name):
    rb = min(1024, rows)
    wide = DA_GH * HEAD

    def body(*refs):
        o_refs, l_refs = refs[0:3], refs[3:6]
        o_ref, lse_ref = refs[6:8]
        on_refs, ln_refs = refs[8:11], refs[11:14]
        for h in range(DA_GH):
            cs = slice(h * HEAD, (h + 1) * HEAD)
            for g, (d, _) in enumerate(DA_GROUPS):
                tn = rb // d
                for r in range(d):
                    _store_strided(on_refs[g], r, tn, d, o_refs[g][h, r])
                    _store_strided(ln_refs[g], r, tn, d, l_refs[g][h, r])
            l0, l1, l2 = ln_refs[0][...], ln_refs[1][...], ln_refs[2][...]
            m = jnp.maximum(jnp.maximum(l0, l1), l2)
            e0, e1, e2 = jnp.exp(l0 - m), jnp.exp(l1 - m), jnp.exp(l2 - m)
            tot = e0 + e1 + e2
            o_ref[:, cs] = (e0 * on_refs[0][...] + e1 * on_refs[1][...] + e2 * on_refs[2][...]) / tot
            lse_ref[:, cs] = m + jnp.log(tot)

    res = lambda d: pl.BlockSpec((DA_GH, d, rb // d, HEAD), lambda i: (0, 0, i, 0))
    nat = pl.BlockSpec((rb, wide), lambda i: (i, 0))
    shape = jax.ShapeDtypeStruct((rows, wide), F32)
    return pl.pallas_call(
        body, name=name, grid=(rows // rb,),
        in_specs=[res(d) for d, _ in DA_GROUPS] * 2,
        out_specs=[nat, nat], out_shape=[shape, shape],
        scratch_shapes=[pltpu.VMEM((rb, HEAD), F32)] * 6,
        compiler_params=_params(("parallel",)),
    )(*outs, *lses)


def _da_bwd_prep(dout, o, lse, *, name):
    rows = o.shape[0]
    rb = min(512, rows)
    wide = DA_GH * HEAD

    def body(d_ref, o_ref, l_ref, *outs):
        d_scr, l_scr, delta_scr = outs[9:12]
        for h in range(DA_GH):
            cs = slice(h * HEAD, (h + 1) * HEAD)
            dv = d_ref[:, cs]
            d_scr[...] = dv
            l_scr[...] = l_ref[:, cs]
            delta_scr[...] = _lane_sum(dv * o_ref[:, cs])
            for g, (d, _) in enumerate(DA_GROUPS):
                tn = rb // d
                for r in range(d):
                    outs[3 * g][h, r] = _strided_rows(d_scr, r, tn, d).astype(BF16)
                    outs[3 * g + 1][h, r] = _strided_rows(l_scr, r, tn, d)
                    outs[3 * g + 2][h, r] = _strided_rows(delta_scr, r, tn, d)

    nat = pl.BlockSpec((rb, wide), lambda i: (i, 0))
    out_specs, out_shape = [], []
    for d, _ in DA_GROUPS:
        for dt in (BF16, F32, F32):
            out_specs.append(pl.BlockSpec((DA_GH, d, rb // d, HEAD), lambda i: (0, 0, i, 0)))
            out_shape.append(jax.ShapeDtypeStruct((DA_GH, d, rows // d, HEAD), dt))
    return pl.pallas_call(
        body, name=name, grid=(rows // rb,),
        in_specs=[nat, nat, nat], out_specs=out_specs, out_shape=out_shape,
        scratch_shapes=[pltpu.VMEM((rb, HEAD), F32)] * 3,
        compiler_params=_params(("parallel",)),
    )(dout, o, lse)


def _da_prep_bwd(dqr, dkr, dvr, proj, q_gain, k_gain, g, *, name):
    d = DA_GROUPS[g][0]
    rows = proj.shape[0]
    rb = min(1024, rows)
    tn = rb // d
    wide = DA_GH * HEAD

    def body(dq_ref, dk_ref, dv_ref, q_ref, k_ref, qg_ref, kg_ref, oq_ref, ok_ref, ov_ref, gq_ref, gk_ref, *nat_refs):
        @pl.when(pl.program_id(0) == 0)
        def _():
            gq_ref[...] = jnp.zeros_like(gq_ref)
            gk_ref[...] = jnp.zeros_like(gk_ref)

        for h in range(DA_GH):
            cs = slice(h * HEAD, (h + 1) * HEAD)
            for j, src in enumerate((dq_ref, dk_ref, dv_ref)):
                for r in range(d):
                    _store_strided(nat_refs[j], r, tn, d, src[h, r])
            ov_ref[:, cs] = nat_refs[2][...].astype(BF16)
            for j, (x_ref, gn_ref, out_ref, acc_ref) in enumerate(((q_ref, qg_ref, oq_ref, gq_ref),
                                                                    (k_ref, kg_ref, ok_ref, gk_ref))):
                xhat, r = _rms(x_ref[:, cs])
                dx, dgr = _rms_bwd(nat_refs[j][...], xhat, r, gn_ref[...])
                out_ref[:, cs] = dx.astype(BF16)
                acc_ref[...] += jnp.sum(dgr, axis=0, keepdims=True)

    res = pl.BlockSpec((DA_GH, d, tn, HEAD), lambda i: (0, 0, i, 0))
    col = lambda off: pl.BlockSpec((rb, wide), lambda i: (i, off // wide + g))
    vec = pl.BlockSpec((1, HEAD), lambda i: (0, 0))
    nat = pl.BlockSpec((rb, wide), lambda i: (i, 0))
    shape = jax.ShapeDtypeStruct((rows, wide), BF16)
    gshape = jax.ShapeDtypeStruct((1, HEAD), F32)
    return pl.pallas_call(
        body, name=name, grid=(rows // rb,),
        in_specs=[res, res, res, col(C_DQ), col(C_DK), vec, vec],
        out_specs=[nat, nat, nat, vec, vec], out_shape=[shape, shape, shape, gshape, gshape],
        scratch_shapes=[pltpu.VMEM((rb, HEAD), F32)] * 3,
        compiler_params=_params(("arbitrary",)),
    )(dqr, dkr, dvr, proj, proj, q_gain, k_gain)


def _mem_fwd(proj, kv, q_gain, k_gain, *, name):
    rows = proj.shape[0]
    tm = min(1024, rows)
    n_mem = kv.shape[0]

    def body(q_ref, k_ref, v_ref, qg_ref, kg_ref, o_ref):
        qhat, _ = _rms(q_ref[...])
        khat, _ = _rms(k_ref[...])
        s = _dot_nt(qhat * qg_ref[...], khat * kg_ref[...]) * ATT_SCALE
        p = jnp.exp(s - jnp.max(s, axis=-1, keepdims=True))
        p = p / jnp.sum(p, axis=-1, keepdims=True)
        o_ref[...] = _dot(p, v_ref[...]).astype(BF16)

    vec = pl.BlockSpec((1, HEAD), lambda i, h: (0, 0))
    return pl.pallas_call(
        body, name=name, grid=(rows // tm, MEM_HEADS),
        in_specs=[pl.BlockSpec((tm, HEAD), lambda i, h: (i, C_MQ // HEAD + h)),
                  pl.BlockSpec((n_mem, HEAD), lambda i, h: (0, h)),
                  pl.BlockSpec((n_mem, HEAD), lambda i, h: (0, MEM_HEADS + h)), vec, vec],
        out_specs=pl.BlockSpec((tm, HEAD), lambda i, h: (i, h)),
        out_shape=jax.ShapeDtypeStruct((rows, MEM_HEADS * HEAD), BF16),
        compiler_params=_params(("parallel", "parallel")),
    )(proj, kv, kv, q_gain, k_gain)


def _mem_bwd(dout, proj, kv, q_gain, k_gain, dproj, *, name):
    rows = proj.shape[0]
    tm = min(1024, rows)
    steps = rows // tm
    n_mem = kv.shape[0]

    def body(d_ref, q_ref, k_ref, v_ref, qg_ref, kg_ref, dp_in, dq_ref, dk_ref, dv_ref, gq_ref, gk_ref, dkn_ref):
        del dp_in
        h, i = pl.program_id(0), pl.program_id(1)

        @pl.when((h == 0) & (i == 0))
        def _():
            gq_ref[...] = jnp.zeros_like(gq_ref)
            gk_ref[...] = jnp.zeros_like(gk_ref)

        @pl.when(i == 0)
        def _():
            dkn_ref[...] = jnp.zeros_like(dkn_ref)
            dv_ref[...] = jnp.zeros_like(dv_ref)

        qhat, rq = _rms(q_ref[...])
        khat, rk = _rms(k_ref[...])
        qn, kn = qhat * qg_ref[...], khat * kg_ref[...]
        s = _dot_nt(qn, kn) * ATT_SCALE
        p = jnp.exp(s - jnp.max(s, axis=-1, keepdims=True))
        p = p / jnp.sum(p, axis=-1, keepdims=True)
        dout = d_ref[...]
        dp = _dot_nt(dout, v_ref[...])
        ds = p * (dp - jnp.sum(p * dp, axis=-1, keepdims=True))
        dv_ref[...] += _dot_tn(p, dout)
        dkn_ref[...] += _dot_tn(ds, qn) * ATT_SCALE
        dq, dgr = _rms_bwd(_dot(ds, kn) * ATT_SCALE, qhat, rq, qg_ref[...])
        dq_ref[...] = dq.astype(BF16)
        gq_ref[...] += jnp.sum(dgr, axis=0, keepdims=True)

        @pl.when(i == steps - 1)
        def _():
            dk, dgk = _rms_bwd(dkn_ref[...], khat, rk, kg_ref[...])
            dk_ref[...] = dk
            gk_ref[...] += jnp.sum(dgk, axis=0, keepdims=True)

    vec = pl.BlockSpec((1, HEAD), lambda h, i: (0, 0))
    memh = pl.BlockSpec((n_mem, HEAD), lambda h, i: (0, h))
    tok = pl.BlockSpec((tm, HEAD), lambda h, i: (i, h))
    mq = pl.BlockSpec((tm, HEAD), lambda h, i: (i, C_MQ // HEAD + h))
    gshape = jax.ShapeDtypeStruct((1, HEAD), F32)
    return pl.pallas_call(
        body, name=name, grid=(MEM_HEADS, steps),
        in_specs=[tok, mq, memh, pl.BlockSpec((n_mem, HEAD), lambda h, i: (0, MEM_HEADS + h)), vec, vec, ANY],
        out_specs=[mq, memh, memh, vec, vec],
        out_shape=[jax.ShapeDtypeStruct(dproj.shape, dproj.dtype),
                   jax.ShapeDtypeStruct((n_mem, MEM_HEADS * HEAD), F32),
                   jax.ShapeDtypeStruct((n_mem, MEM_HEADS * HEAD), F32), gshape, gshape],
        input_output_aliases={6: 0},
        scratch_shapes=[pltpu.VMEM((n_mem, HEAD), F32)],
        compiler_params=_params(("arbitrary", "arbitrary")),
    )(dout, proj, kv, kv, q_gain, k_gain, dproj)


def _branch_fwd(o_hg, o_da, o_mem, proj, wp_hg, wp_da, wp_mem, *, name):
    rows = o_hg.shape[0]
    tm = min(512, rows)

    def body(a_ref, b_ref, c_ref, ga_ref, gb_ref, gc_ref, wa_ref, wb_ref, wc_ref, o_ref):
        merged = _sigmoid(ga_ref[...]) * _dot(a_ref[...], wa_ref[...])
        merged += _sigmoid(gb_ref[...]) * _dot(b_ref[...], wb_ref[...])
        merged += _sigmoid(gc_ref[...]) * _dot(c_ref[...], wc_ref[...])
        o_ref[...] = merged.astype(BF16)

    row = lambda w: pl.BlockSpec((tm, w), lambda i: (i, 0))
    gate = lambda off: pl.BlockSpec((tm, D_MODEL), lambda i: (i, off // D_MODEL))
    full = lambda a: pl.BlockSpec(a.shape, lambda i: (0, 0))
    return pl.pallas_call(
        body, name=name, grid=(rows // tm,),
        in_specs=[row(o_hg.shape[1]), row(o_da.shape[1]), row(o_mem.shape[1]),
                  gate(C_GHG), gate(C_GDA), gate(C_GMEM), full(wp_hg), full(wp_da), full(wp_mem)],
        out_specs=row(D_MODEL), out_shape=jax.ShapeDtypeStruct((rows, D_MODEL), BF16),
        compiler_params=_params(("parallel",)),
    )(o_hg, o_da, o_mem, proj, proj, proj, wp_hg, wp_da, wp_mem)


def _branch_bwd(dm, o_hg, o_da, o_mem, proj, wp_hg, wp_da, wp_mem, *, name):
    rows = o_hg.shape[0]
    tm = min(256, rows)

    def body(dm_ref, a_ref, b_ref, c_ref, ga_ref, gb_ref, gc_ref, wa_ref, wb_ref, wc_ref, dp_ref, *outs):
        dmv = dm_ref[...]
        for j, (o_ref, g_ref, w_ref) in enumerate(((a_ref, ga_ref, wa_ref), (b_ref, gb_ref, wb_ref),
                                                   (c_ref, gc_ref, wc_ref))):
            z = _dot(o_ref[...], w_ref[...])
            gs = _sigmoid(g_ref[...])
            dz = (dmv * gs).astype(BF16)
            dp_ref[:, j * D_MODEL:(j + 1) * D_MODEL] = (dmv * z * gs * (1.0 - gs)).astype(BF16)
            outs[2 * j][...] = dz
            outs[2 * j + 1][...] = _dot_nt(dz, w_ref[...])

    row = lambda w: pl.BlockSpec((tm, w), lambda i: (i, 0))
    gate = lambda off: pl.BlockSpec((tm, D_MODEL), lambda i: (i, off // D_MODEL))
    full = lambda a: pl.BlockSpec(a.shape, lambda i: (0, 0))
    out_specs = [pl.BlockSpec((pl.Element(tm), pl.Element(3 * D_MODEL)), lambda i: (i * tm, C_GHG))]
    out_shape = [jax.ShapeDtypeStruct((rows, IN_COLS), BF16)]
    for o in (o_hg, o_da, o_mem):
        out_specs += [row(D_MODEL), row(o.shape[1])]
        out_shape += [jax.ShapeDtypeStruct((rows, D_MODEL), BF16), jax.ShapeDtypeStruct((rows, o.shape[1]), F32)]
    return pl.pallas_call(
        body, name=name, grid=(rows // tm,),
        in_specs=[row(D_MODEL), row(o_hg.shape[1]), row(o_da.shape[1]), row(o_mem.shape[1]),
                  gate(C_GHG), gate(C_GDA), gate(C_GMEM), full(wp_hg), full(wp_da), full(wp_mem)],
        out_specs=out_specs, out_shape=out_shape,
        compiler_params=_params(("parallel",)),
    )(dm, o_hg, o_da, o_mem, proj, proj, proj, wp_hg, wp_da, wp_mem)


def _ffn_in(h2, w_ab, *, name):
    rows, dff = h2.shape[0], w_ab.shape[1] // 2
    tm, tn = min(2048, rows), 256

    def body(h_ref, wa_ref, wb_ref, a_ref, b_ref, u_ref):
        a = _dot(h_ref[...], wa_ref[...])
        b = _dot(h_ref[...], wb_ref[...])
        a_ref[...] = a.astype(BF16)
        b_ref[...] = b.astype(BF16)
        u_ref[...] = (a * _sigmoid(a) * b).astype(BF16)

    out = pl.BlockSpec((tm, tn), lambda i, j: (i, j))
    return pl.pallas_call(
        body, name=name, grid=(rows // tm, dff // tn),
        in_specs=[pl.BlockSpec((tm, D_MODEL), lambda i, j: (i, 0)),
                  pl.BlockSpec((D_MODEL, tn), lambda i, j: (0, j)),
                  pl.BlockSpec((D_MODEL, tn), lambda i, j: (0, dff // tn + j))],
        out_specs=[out, out, out],
        out_shape=[jax.ShapeDtypeStruct((rows, dff), BF16)] * 3,
        compiler_params=_params(("parallel", "parallel")),
    )(h2, w_ab, w_ab)


def _ffn_act_bwd(dy, w_out, a, b, *, name):
    rows, dff = a.shape
    tm, tn = min(1024, rows), 256

    def body(dy_ref, w_ref, a_ref, b_ref, da_ref, db_ref):
        du = _dot_nt(dy_ref[...], w_ref[...])
        av, bv = a_ref[...].astype(F32), b_ref[...].astype(F32)
        sa = _sigmoid(av)
        da_ref[...] = (du * bv * sa * (1.0 + av * (1.0 - sa))).astype(BF16)
        db_ref[...] = (du * av * sa).astype(BF16)

    tile = pl.BlockSpec((tm, tn), lambda i, j: (i, j))
    return pl.pallas_call(
        body, name=name, grid=(rows // tm, dff // tn),
        in_specs=[pl.BlockSpec((tm, D_MODEL), lambda i, j: (i, 0)),
                  pl.BlockSpec((tn, D_MODEL), lambda i, j: (j, 0)), tile, tile],
        out_specs=[tile, tile],
        out_shape=[jax.ShapeDtypeStruct((rows, dff), BF16), jax.ShapeDtypeStruct((rows, dff), BF16)],
        compiler_params=_params(("parallel", "parallel")),
    )(dy, w_out, a, b)


def _lower_bound(lb_fw, lb_bw, *, name):
    def body(a_ref, b_ref, oa_ref, ob_ref):
        for src, dst in ((a_ref, oa_ref), (b_ref, ob_ref)):
            dst[...] = _sigmoid(src[0:1, :] - src[1:2, :])

    shape = jax.ShapeDtypeStruct((1, lb_fw.shape[1]), F32)
    return pl.pallas_call(body, name=name, out_shape=[shape, shape])(lb_fw, lb_bw)


def _local_step(x, mem, tgt, p, ex):
    rows = x.shape[0]
    lb_fw, lb_bw = _lower_bound(p["lb_logits_fw"], p["lb_logits_bw"], name="lower_bound")

    h = _rms_fwd(x, p["norm_mix_gain"], name="norm_mix")
    w = {"w_in": ex.w_in}
    proj, carried = _matmul([(h, w["w_in"])], "nn", BF16, tm=1024, tn=1024, carry=ex.late_carry(), name="proj_in")
    w.update(ex.late_weights(carried))
    o_fw, st_fw = _gla_fwd(proj, lb_fw, f_off=C_FF, rev=False, name="gla_fwd_fw")
    o_bw, st_bw = _gla_fwd(proj, lb_bw, f_off=C_FB, rev=True, name="gla_fwd_bw")
    o_hg = _hg_out_fwd(o_fw, o_bw, proj, p["hg_norm_gain"], name="hg_out")

    qkv_r, outs, lses = [], [], []
    for g in range(len(DA_GROUPS)):
        qr, kr, vr = _da_prep(proj, p["da_q_gain"], p["da_k_gain"], g, name=f"da_prep{g}")
        og, lg = _band_fwd(qr, kr, vr, g, name=f"band_fwd{g}")
        qkv_r.append((qr, kr, vr))
        outs.append(og)
        lses.append(lg)
    o_da, lse_da = _da_merge(outs, lses, rows, name="da_merge")

    mem_n = _rms_fwd(mem, p["norm_mem_gain"], name="norm_mem")
    kv = _matmul([(mem_n, w["w_mem_kv"])], "nn", F32, tm=256, tn=512, name="mem_kv")
    o_mem = _mem_fwd(proj, kv, p["mem_q_gain"], p["mem_k_gain"], name="mem_attn")

    merged = _branch_fwd(o_hg, o_da, o_mem, proj, w["w_proj_hg"], w["w_proj_da"], w["w_proj_mem"],
                         name="branch_merge")
    x1, h2 = _out_norm(merged, w["w_out"], x, p["norm_ffn_gain"], name="mix_out_norm")
    a, b, u = _ffn_in(h2, w["w_ffn_in"], name="ffn_in")
    dy, dy_b, loss = _out_loss(u, w["w_ffn_out"], x1, tgt, name="ffn_out_loss")

    gw, gwb, gs = {}, {}, {}
    gw["w_ffn_out"], gwb["w_ffn_out"] = _matmul([(u, dy_b)], "tn", F32, tm=256, tn=1024, also=BF16, name="g_ffn_out")
    da, db = _ffn_act_bwd(dy_b, w["w_ffn_out"], a, b, name="ffn_act_bwd")
    gw["w_ffn_a"], gwb["w_ffn_a"] = _matmul([(h2, da)], "tn", F32, tm=512, tn=256, also=BF16, name="g_ffn_a")
    gw["w_ffn_b"], gwb["w_ffn_b"] = _matmul([(h2, db)], "tn", F32, tm=512, tn=256, also=BF16, name="g_ffn_b")
    dh2 = _matmul([(da, w["w_ffn_in"], 0), (db, w["w_ffn_in"], 1)], "nt", F32, tm=256, tn=1024, name="d_h2")
    dx1, dx1_b, gs["norm_ffn_gain"] = _rms_bwd_rows(dh2, x1, p["norm_ffn_gain"], dy, (F32, BF16), name="norm_ffn_bwd")
    gw["w_out"], gwb["w_out"] = _matmul([(merged, dx1_b)], "tn", F32, tm=512, tn=512, also=BF16, name="g_out")
    dmerged = _matmul([(dx1_b, w["w_out"])], "nt", F32, tm=512, tn=512, name="d_merged")
    dproj, dz_hg, do_hg, dz_da, do_da, dz_mem, do_mem = _branch_bwd(
        dmerged, o_hg, o_da, o_mem, proj, w["w_proj_hg"], w["w_proj_da"], w["w_proj_mem"], name="branch_bwd")
    gw["w_proj_hg"], gwb["w_proj_hg"] = _matmul([(o_hg, dz_hg)], "tn", F32, tm=512, tn=512, also=BF16, name="g_proj_hg")
    gw["w_proj_da"], gwb["w_proj_da"] = _matmul([(o_da, dz_da)], "tn", F32, tm=512, tn=512, also=BF16, name="g_proj_da")
    gw["w_proj_mem"], gwb["w_proj_mem"] = _matmul([(o_mem, dz_mem)], "tn", F32, tm=512, tn=512, also=BF16, name="g_proj_mem")

    dproj, dk_mem, dv_mem, gs["mem_q_gain"], gs["mem_k_gain"] = _mem_bwd(
        do_mem, proj, kv, p["mem_q_gain"], p["mem_k_gain"], dproj, name="mem_attn_bwd")
    dkv = jnp.concatenate([dk_mem, dv_mem], axis=1).astype(BF16)
    gw["w_mem_kv"], gwb["w_mem_kv"] = _matmul([(mem_n, dkv)], "tn", F32, tm=512, tn=512, also=BF16, name="g_mem_kv")
    dmem_n = _matmul([(dkv, w["w_mem_kv"])], "nt", F32, tm=256, tn=512, name="d_mem_n")
    (gs["norm_mem_gain"],) = _rms_bwd_rows(dmem_n, mem, p["norm_mem_gain"], None, (), name="norm_mem_bwd")

    prep = _da_bwd_prep(do_da, o_da, lse_da, name="da_bwd_prep")
    d_da, gq_parts, gk_parts = [], [], []
    for g in range(len(DA_GROUPS)):
        qr, kr, vr = qkv_r[g]
        dor, lser, deltar = prep[3 * g:3 * g + 3]
        dqr, dkr, dvr = _band_bwd(qr, kr, vr, dor, lser, deltar, g, name=f"band_bwd{g}")
        dq, dk, dv, gq, gk = _da_prep_bwd(dqr, dkr, dvr, proj, p["da_q_gain"], p["da_k_gain"], g,
                                          name=f"da_prep_bwd{g}")
        d_da.append((dq, dk, dv))
        gq_parts.append(gq)
        gk_parts.append(gk)
    for j, (off, tag) in enumerate(((C_DQ, "q"), (C_DK, "k"), (C_DV, "v"))):
        dproj = _fill_columns(dproj, [t[j] for t in d_da], off, name=f"dproj_fill_da_{tag}")

    dproj, do_gla, gs["hg_norm_gain"] = _hg_out_bwd(do_hg, o_fw, o_bw, proj, p["hg_norm_gain"], dproj,
                                                    name="hg_out_bwd")
    carry_bulk, carry_rest = ex.early_carries(gw, gwb)
    (dq_f, dproj, dv_f, dlb_fw), bulk = _gla_bwd(proj, lb_fw, do_gla, st_fw, None, dproj, f_off=C_FF, rev=False,
                                                 carry=carry_bulk, name="gla_bwd_fw")
    (dproj, dfl_bw, dv_hg, dlb_bw), rest = _gla_bwd(proj, lb_bw, do_gla, st_bw, (dq_f, dv_f), dproj, f_off=C_FB,
                                                    rev=True, carry=carry_rest, name="gla_bwd_bw")
    ex.early_done(bulk, rest)
    dproj = _fill_columns(dproj, [dfl_bw, dv_hg], C_FB, name="dproj_fill")
    g_in, g_in_b = _matmul([(h, dproj)], "tn", F32, tm=512, tn=1024, also=BF16, name="g_in")
    dh, carried = _matmul([(dproj, w["w_in"])], "nt", F32, tm=1024, tn=1024, tk=IN_COLS // 8,
                          carry=ex.w_in_carry(g_in, g_in_b), name="d_h")
    ex.w_in_done(carried)
    grad_x, gs["norm_mix_gain"] = _rms_bwd_rows(dh, x, p["norm_mix_gain"], dx1, (F32,), name="norm_mix_bwd")

    small = _small_pack(gs, gq_parts, gk_parts, dlb_fw, dlb_bw, lb_fw, lb_bw, name="small_pack")
    return loss, grad_x, small


def _small_pack(gs, gq_parts, gk_parts, dlb_fw, dlb_bw, lb_fw, lb_bw, *, name):
    def body(g_mix, g_mem, g_ffn, dfw, dbw, lfw, lbw, g_hg, q0, q1, q2, k0, k1, k2, g_mq, g_mk, o_ref):
        o_ref[0:1, :] = g_mix[...]
        o_ref[1:2, :] = g_mem[...]
        o_ref[2:3, :] = g_ffn[...]
        for base, d_ref, l_ref in ((3, dfw, lfw), (5, dbw, lbw)):
            lbv = l_ref[...]
            t = d_ref[...] * lbv * (1.0 - lbv)
            o_ref[base:base + 1, :] = t
            o_ref[base + 1:base + 2, :] = -t
        o_ref[7:8, :] = jnp.zeros((1, D_MODEL), F32)
        o_ref[7:8, 0:HEAD] = g_hg[...]
        o_ref[7:8, HEAD:2 * HEAD] = q0[...] + q1[...] + q2[...]
        o_ref[7:8, 2 * HEAD:3 * HEAD] = k0[...] + k1[...] + k2[...]
        o_ref[7:8, 3 * HEAD:4 * HEAD] = g_mq[...]
        o_ref[7:8, 4 * HEAD:5 * HEAD] = g_mk[...]

    return pl.pallas_call(body, name=name, out_shape=jax.ShapeDtypeStruct((8, D_MODEL), F32))(
        gs["norm_mix_gain"], gs["norm_mem_gain"], gs["norm_ffn_gain"], dlb_fw, dlb_bw, lb_fw, lb_bw,
        gs["hg_norm_gain"], *gq_parts, *gk_parts, gs["mem_q_gain"], gs["mem_k_gain"])


def _row_tile(rows, cols, n_arrays):
    budget = (16 * 1024 * 1024) // (2 * 4 * cols * n_arrays)
    tr = rows
    while tr > budget and tr % 2 == 0 and (tr // 2) % 16 == 0:
        tr //= 2
    return tr


def _cast_into_full(a, chip, rows, cols, axis, *, name):
    sr, sc = a.shape
    tr = _row_tile(sr, sc, 2)

    def body(chip_ref, a_ref, o_ref):
        del chip_ref
        o_ref[...] = a_ref[...].astype(BF16)

    if axis == 1:
        out_map = lambda i, chip_ref: (i, chip_ref[0])
    else:
        out_map = lambda i, chip_ref: (chip_ref[0] * (sr // tr) + i, 0)
    return pl.pallas_call(
        body, name=name,
        grid_spec=pltpu.PrefetchScalarGridSpec(
            num_scalar_prefetch=1, grid=(sr // tr,),
            in_specs=[pl.BlockSpec((tr, sc), lambda i, chip_ref: (i, 0))],
            out_specs=pl.BlockSpec((tr, sc), out_map)),
        out_shape=jax.ShapeDtypeStruct((rows, cols), BF16),
        compiler_params=_params(("parallel",)))(chip, a)


def _add_halves(items, *, name):
    rows = items[0][3].shape[0]
    widths = [ra.shape[1] for _, _, _, ra in items]
    tr = _row_tile(rows, sum(widths), 4)

    def body(*refs):
        o_ref = refs[-1]
        first = lax.axis_index("c") == 0
        off = 0
        for j, wd in enumerate(widths):
            h0, h1, ra = refs[3 * j:3 * j + 3]
            o_ref[:, off:off + wd] = (jnp.where(first, h0[...], h1[...]) + ra[...]).astype(BF16)
            off += wd

    in_specs, ins = [], []
    for (g, haxis, hsize, ra), wd in zip(items, widths):
        if haxis == 0:
            in_specs += [pl.BlockSpec((tr, wd), lambda i: (i, 0)),
                         pl.BlockSpec((tr, wd), lambda i, o=hsize // tr: (o + i, 0))]
        else:
            in_specs += [pl.BlockSpec((tr, wd), lambda i: (i, 0)), pl.BlockSpec((tr, wd), lambda i: (i, 1))]
        in_specs.append(pl.BlockSpec((tr, wd), lambda i: (i, 0)))
        ins += [g, g, ra]
    return pl.pallas_call(body, name=name, grid=(rows // tr,), in_specs=in_specs,
                          out_specs=pl.BlockSpec((tr, sum(widths)), lambda i: (i, 0)),
                          out_shape=jax.ShapeDtypeStruct((rows, sum(widths)), BF16),
                          compiler_params=_params(("parallel",)))(*ins)


def _add_slots(rb, *, name):
    _, rows, cols = rb.shape
    tr = _row_tile(rows, cols, 5)

    def body(r0, r1, r2, r3, o_ref):
        o_ref[...] = ((r0[...].astype(F32) + r1[...].astype(F32)) + r2[...].astype(F32)) + r3[...].astype(F32)

    slot = lambda s: pl.BlockSpec((None, tr, cols), lambda i: (s, i, 0))
    return pl.pallas_call(body, name=name, grid=(rows // tr,), in_specs=[slot(s) for s in range(4)],
                          out_specs=pl.BlockSpec((tr, cols), lambda i: (i, 0)),
                          out_shape=jax.ShapeDtypeStruct((rows, cols), F32),
                          compiler_params=_params(("parallel",)))(rb, rb, rb, rb)


def _adamw(w, g, m, v, *, name):
    rows, cols = w.shape
    tr = _row_tile(rows, cols, 8) if rows % 16 == 0 else rows
    c1 = 1.0 - ADAM_B1 ** ADAM_STEP
    c2 = 1.0 - ADAM_B2 ** ADAM_STEP

    def body(w_ref, g_ref, m_ref, v_ref, d_ref, mo_ref, vo_ref, go_ref):
        gv = g_ref[...]
        go_ref[...] = gv
        mn = ADAM_B1 * m_ref[...] + (1.0 - ADAM_B1) * gv
        vn = ADAM_B2 * v_ref[...] + (1.0 - ADAM_B2) * (gv * gv)
        mo_ref[...] = mn
        vo_ref[...] = vn
        d_ref[...] = -ADAM_LR * ((mn / c1) / (jnp.sqrt(vn / c2) + ADAM_EPS) + ADAM_WD * w_ref[...])

    spec = pl.BlockSpec((tr, cols), lambda i: (i, 0))
    shape = jax.ShapeDtypeStruct((rows, cols), F32)
    return pl.pallas_call(body, name=name, grid=(rows // tr,), in_specs=[spec] * 4, out_specs=[spec] * 4,
                        out_shape=[shape] * 4, compiler_params=_params(("parallel",)))(w, g, m, v)


W_SPECS = (
    ("w_in", 1024, IN_COLS, 1, IN_COLS // 4),
    ("w_mem_kv", 1024, 1024, 0, 256),
    ("w_proj_hg", 1024, 1024, 0, 256),
    ("w_proj_da", 512, 1024, 1, 256),
    ("w_proj_mem", 512, 1024, 1, 256),
    ("w_out", 1024, 1024, 0, 256),
    ("w_ffn_in", 1024, 2 * D_FF, 1, 2 * D_FF // 4),
    ("w_ffn_out", D_FF, 1024, 0, D_FF // 4),
)
CHIP_FLIPS = ((1, 0), (0, 1), (1, 1))
ANY = pl.BlockSpec(memory_space=pl.ANY)
DMA_CHUNK_BYTES = 1 << 20
STAGE_BYTES = 2 << 20


def _place():
    x, y, c = lax.axis_index("x"), lax.axis_index("y"), lax.axis_index("c")
    return x, y, c, 2 * x + y


def _flip(v, f):
    return 1 - v if f else v


def _slab(ref, axis, idx, size):
    start = pl.multiple_of(idx * size, size)
    return ref.at[pl.ds(start, size), :] if axis == 0 else ref.at[:, pl.ds(start, size)]


def _chunked(make, src, dst, want="both"):
    rows, cols = src.shape
    row_bytes = cols * jnp.dtype(src.dtype).itemsize
    k = 1
    while rows % (2 * k) == 0 and (rows // (2 * k)) % 16 == 0 and (rows // k) * row_bytes > DMA_CHUNK_BYTES:
        k *= 2
    cr = rows // k
    parts = []
    if want != "wait":
        parts = [make(src.at[pl.ds(j * cr, cr), :], dst.at[pl.ds(j * cr, cr), :]) for j in range(k)]
    return parts, (make(src, dst) if want != "start" else None)


def _half_spec(rows, cols, axis):
    return (0, rows // 2) if axis == 1 else (1, cols // 2)


def _staged(src, remote_dst, local_dst, sibling, load_sems, send_sems, store_sems, recv_sem):
    rows, cols = src.shape
    row_bytes = cols * jnp.dtype(src.dtype).itemsize
    k = 1
    while (rows // k) * row_bytes > STAGE_BYTES and rows % (2 * k) == 0 and (rows // (2 * k)) % 16 == 0:
        k *= 2
    cr = rows // k
    piece = lambda ref, j: ref.at[pl.ds(j * cr, cr), :]

    def run(buf):
        loads = [pltpu.make_async_copy(piece(src, j), buf.at[j % 2], load_sems.at[j % 2]) for j in range(k)]
        outs = [[pltpu.make_async_remote_copy(src_ref=buf.at[j % 2], dst_ref=piece(remote_dst, j),
                                              send_sem=send_sems.at[j % 2], recv_sem=recv_sem,
                                              device_id=sibling, device_id_type=MESH)] for j in range(k)]
        if local_dst is not None:
            for j in range(k):
                outs[j].append(pltpu.make_async_copy(buf.at[j % 2], piece(local_dst, j), store_sems.at[j % 2]))

        def drained(j):
            outs[j][0].wait_send()
            for cp in outs[j][1:]:
                cp.wait()

        loads[0].start()
        for j in range(k):
            loads[j].wait()
            for cp in outs[j]:
                cp.start()
            if j + 1 < k:
                if j >= 1:
                    drained(j - 1)
                loads[j + 1].start()
        for j in range(max(0, k - 2), k):
            drained(j)

    pl.run_scoped(run, pltpu.VMEM((2, cr, cols), src.dtype))


def _landed(ref, recv_sem, send_sem):
    pltpu.make_async_remote_copy(src_ref=ref, dst_ref=ref, send_sem=send_sem, recv_sem=recv_sem,
                                 device_id=(lax.axis_index("x"), lax.axis_index("y"), lax.axis_index("c")),
                                 device_id_type=MESH).wait_recv()


def _half_slab(ref, spec, chip, half):
    _, rows, cols, axis, size = spec
    haxis, hsize = _half_spec(rows, cols, axis)
    return _slab(_slab(ref, axis, chip, size), haxis, half, hsize)


def _gather_sends(outs, specs, ici_send, ici_recv, want="both"):
    x, y, c, p = _place()
    sent = []
    for wi, spec in enumerate(specs):
        mine = _half_slab(outs[wi], spec, p, c)
        for k, (fx, fy) in enumerate(CHIP_FLIPS):
            sent.append(_chunked(lambda s, d, j=3 * wi + k, fx=fx, fy=fy: pltpu.make_async_remote_copy(
                src_ref=s, dst_ref=d, send_sem=ici_send.at[j], recv_sem=ici_recv.at[j],
                device_id=(_flip(x, fx), _flip(y, fy), c), device_id_type=MESH), mine, mine, want))
    return sent


def _gather_weights(fulls, specs, *, name):
    n = len(specs)

    def body(*refs):
        outs = refs[n:2 * n]
        ici_send, ici_recv, load_sems, d2d_send, d2d_recv = refs[2 * n:]
        x, y, c, _ = _place()
        sent = _gather_sends(outs, specs, ici_send, ici_recv)
        for parts, _ in sent:
            for cp in parts:
                cp.start()
        for k, (fx, fy) in enumerate(CHIP_FLIPS):
            q = 2 * _flip(x, fx) + _flip(y, fy)
            for wi in range(n):
                sent[3 * wi + k][1].wait_recv()
                got = _half_slab(outs[wi], specs[wi], q, c)
                _staged(got, got, None, (x, y, 1 - c), load_sems, d2d_send, None, d2d_recv.at[3 * wi + k])
        for _, whole in sent:
            whole.wait_send()
        for k, (fx, fy) in enumerate(CHIP_FLIPS):
            q = 2 * _flip(x, fx) + _flip(y, fy)
            for wi in range(n):
                _landed(_half_slab(outs[wi], specs[wi], q, 1 - c), d2d_recv.at[3 * wi + k], d2d_send.at[0])

    return pl.pallas_call(
        body, name=name, in_specs=[ANY] * n, out_specs=[ANY] * n,
        out_shape=[jax.ShapeDtypeStruct(f.shape, f.dtype) for f in fulls],
        input_output_aliases={i: i for i in range(n)},
        scratch_shapes=[pltpu.SemaphoreType.DMA((3 * n,)), pltpu.SemaphoreType.DMA((3 * n,)),
                        pltpu.SemaphoreType.DMA((2,)), pltpu.SemaphoreType.DMA((2,)),
                        pltpu.SemaphoreType.DMA((3 * n,))],
    )(*fulls)


def _gather_chips_carry(fulls, specs):
    n = len(specs)

    def issue(ins, outs, sems, want):
        del ins
        return [(parts, whole, True) for parts, whole in _gather_sends(outs, specs, sems[0], sems[1], want)]

    return _Carry(list(fulls), [jax.ShapeDtypeStruct(f.shape, f.dtype) for f in fulls], {i: i for i in range(n)},
                  [pltpu.SemaphoreType.DMA((3 * n,)), pltpu.SemaphoreType.DMA((3 * n,))], issue)


def _gather_pass_on(fulls, specs, *, name):
    n = len(specs)

    def body(*refs):
        outs = refs[n:2 * n]
        load_sems, d2d_send, d2d_recv = refs[2 * n:2 * n + 3]
        bufs = refs[2 * n + 3:]
        x, y, c, _ = _place()
        loads, sends = [], []
        for k, (fx, fy) in enumerate(CHIP_FLIPS):
            q = 2 * _flip(x, fx) + _flip(y, fy)
            for wi in range(n):
                j = 3 * wi + k
                got = _half_slab(outs[wi], specs[wi], q, c)
                loads.append(pltpu.make_async_copy(got, bufs[j], load_sems.at[j]))
                sends.append(pltpu.make_async_remote_copy(
                    src_ref=bufs[j], dst_ref=got, send_sem=d2d_send.at[j], recv_sem=d2d_recv.at[j],
                    device_id=(x, y, 1 - c), device_id_type=MESH))
        for cp in loads:
            cp.start()
        for load, send in zip(loads, sends):
            load.wait()
            send.start()
        for cp in sends:
            cp.wait_send()
        for k, (fx, fy) in enumerate(CHIP_FLIPS):
            q = 2 * _flip(x, fx) + _flip(y, fy)
            for wi in range(n):
                _landed(_half_slab(outs[wi], specs[wi], q, 1 - c), d2d_recv.at[3 * wi + k], d2d_send.at[0])

    shapes = []
    for _, rows, cols, axis, size in specs:
        shapes += [(rows // 2, size) if axis == 1 else (size, cols // 2)] * 3
    assert sum(math.prod(s) for s in shapes) * 2 <= VMEM_LIMIT_V7X // 2
    stages = [pltpu.VMEM(s, BF16) for s in shapes]
    return pl.pallas_call(
        body, name=name, in_specs=[ANY] * n, out_specs=[ANY] * n,
        out_shape=[jax.ShapeDtypeStruct(f.shape, f.dtype) for f in fulls],
        input_output_aliases={i: i for i in range(n)},
        scratch_shapes=[pltpu.SemaphoreType.DMA((3 * n,))] * 3 + stages,
        compiler_params=pltpu.CompilerParams(vmem_limit_bytes=VMEM_LIMIT_V7X),
    )(*fulls)


def _sibling_exchange(grads, *, name):
    n = len(grads)

    def body(*refs):
        ins, outs = refs[:n], refs[n:2 * n]
        load_sems, send_sems, recv_sems = refs[2 * n:]
        x, y, c, _ = _place()
        for i, (_, haxis, hsize) in enumerate(grads):
            _staged(_slab(ins[i], haxis, 1 - c, hsize), outs[i], None, (x, y, 1 - c),
                    load_sems, send_sems, None, recv_sems.at[i])
        for i in range(n):
            _landed(outs[i], recv_sems.at[i], send_sems.at[0])

    shapes = [jax.ShapeDtypeStruct((hsize, g.shape[1]) if haxis == 0 else (g.shape[0], hsize), g.dtype)
              for g, haxis, hsize in grads]
    return pl.pallas_call(
        body, name=name, in_specs=[ANY] * n, out_specs=[ANY] * n, out_shape=shapes,
        scratch_shapes=[pltpu.SemaphoreType.DMA((2,)), pltpu.SemaphoreType.DMA((2,)),
                        pltpu.SemaphoreType.DMA((n,))],
    )(*[g for g, _, _ in grads])


def _chip_exchange_carry(parts):
    n = len(parts)

    def issue(ins, outs, sems, want):
        send_sems, recv_sems, local_sems = sems
        x, y, c, p = _place()
        copies = []
        for i, (_, axis, size) in enumerate(parts):
            copies.append(_chunked(lambda s, d, i=i: pltpu.make_async_copy(s, d, local_sems.at[i]),
                                   _slab(ins[i], axis, p, size), outs[i].at[p], want) + (False,))
            for k, (fx, fy) in enumerate(CHIP_FLIPS):
                px, py = _flip(x, fx), _flip(y, fy)
                copies.append(_chunked(lambda s, d, j=3 * i + k, px=px, py=py: pltpu.make_async_remote_copy(
                    src_ref=s, dst_ref=d, send_sem=send_sems.at[j], recv_sem=recv_sems.at[j],
                    device_id=(px, py, c), device_id_type=MESH),
                    _slab(ins[i], axis, 2 * px + py, size), outs[i].at[p], want) + (True,))
        return copies

    shapes = []
    for a, axis, size in parts:
        shapes.append(jax.ShapeDtypeStruct((4, size, a.shape[1]) if axis == 0 else (4, a.shape[0], size), a.dtype))
    return _Carry([a for a, _, _ in parts], shapes, {},
                  [pltpu.SemaphoreType.DMA((3 * n,)), pltpu.SemaphoreType.DMA((3 * n,)),
                   pltpu.SemaphoreType.DMA((n,))], issue)


def _sibling_share(sums):
    n = len(sums)

    def body(*refs):
        ins, outs = refs[:n], refs[n:2 * n]
        load_sems, send_sems, store_sems, recv_sems = refs[2 * n:]
        x, y, c, _ = _place()
        for i, (s, haxis) in enumerate(sums):
            place = _slab(outs[i], haxis, c, s.shape[haxis])
            _staged(ins[i], place, place, (x, y, 1 - c), load_sems, send_sems, store_sems, recv_sems.at[i])
        for i, (s, haxis) in enumerate(sums):
            _landed(_slab(outs[i], haxis, 1 - c, s.shape[haxis]), recv_sems.at[i], send_sems.at[0])

    shapes = []
    for s, haxis in sums:
        r, cc = s.shape
        shapes.append(jax.ShapeDtypeStruct((2 * r, cc) if haxis == 0 else (r, 2 * cc), F32))
    return pl.pallas_call(
        body, name="grad_sibling_share", in_specs=[ANY] * n, out_specs=[ANY] * n, out_shape=shapes,
        scratch_shapes=[pltpu.SemaphoreType.DMA((2,)), pltpu.SemaphoreType.DMA((2,)),
                        pltpu.SemaphoreType.DMA((2,)), pltpu.SemaphoreType.DMA((n,))],
    )(*[s for s, _ in sums])


class _Exchanges:
    def __init__(self, fulls):
        self.w_in = _gather_weights([fulls["w_in"]], W_SPECS[:1], name="gather_w_in")[0]
        self.late_specs = W_SPECS[1:]
        self.late = [fulls[s[0]] for s in self.late_specs]
        self.slots = {}

    def late_carry(self):
        return _gather_chips_carry(self.late, self.late_specs)

    def late_weights(self, carried):
        done = _gather_pass_on(carried, self.late_specs, name="gather_pass_on")
        return dict(zip([s[0] for s in self.late_specs], done))

    def _half_sums(self, gw, gwb, specs, tag):
        grads, sent = [], []
        for name, _, _, axis, _ in specs:
            for part in (("w_ffn_a", "w_ffn_b") if name == "w_ffn_in" else (name,)):
                half = _half_spec(gw[part].shape[0], gw[part].shape[1], axis)
                grads.append((gw[part],) + half)
                sent.append((gwb[part],) + half)
        theirs = _sibling_exchange(sent, name=f"grad_sibling_exchange_{tag}")
        parts, j = [], 0
        for name, _, _, axis, size in specs:
            take = 2 if name == "w_ffn_in" else 1
            items = [grads[i] + (theirs[i],) for i in range(j, j + take)]
            parts.append((_add_halves(items, name=f"half_sum_{name}"), axis, size))
            j += take
        return parts

    def early_carries(self, gw, gwb):
        parts = self._half_sums(gw, gwb, self.late_specs, "early")
        self.early_names = [s[0] for s in self.late_specs]
        cut = self.early_names.index("w_ffn_in")
        return _chip_exchange_carry(parts[cut:]), _chip_exchange_carry(parts[:cut])

    def early_done(self, carried_bulk, carried_rest):
        cut = self.early_names.index("w_ffn_in")
        self.slots.update(zip(self.early_names[cut:], carried_bulk))
        self.slots.update(zip(self.early_names[:cut], carried_rest))

    def w_in_carry(self, g, gb):
        return _chip_exchange_carry(self._half_sums({"w_in": g}, {"w_in": gb}, W_SPECS[:1], "w_in"))

    def w_in_done(self, carried):
        self.slots["w_in"] = carried[0]

    def reduced(self):
        sums = []
        for name, rows, cols, axis, _ in W_SPECS:
            sums.append((_add_slots(self.slots[name], name=f"chip_sum_{name}"), _half_spec(rows, cols, axis)[0]))
        return dict(zip([s[0] for s in W_SPECS], _sibling_share(sums)))


def _small_allreduce(sv):
    def body(sv_ref, o_ref, slots_ref, send_sems, recv_sems):
        x, y, c, _ = _place()
        me = 4 * x + 2 * y + c
        slots_ref[me] = sv_ref[...]
        copies = []
        for k in range(1, 8):
            fx, fy, fc = (k >> 2) & 1, (k >> 1) & 1, k & 1
            copies.append(pltpu.make_async_remote_copy(
                src_ref=sv_ref, dst_ref=slots_ref.at[me], send_sem=send_sems.at[k - 1],
                recv_sem=recv_sems.at[k - 1], device_id=(_flip(x, fx), _flip(y, fy), _flip(c, fc)),
                device_id_type=MESH))
        for cp in copies:
            cp.start()
        for cp in copies:
            cp.wait_recv()
        for cp in copies:
            cp.wait_send()
        total = slots_ref[0]
        for s in range(1, 8):
            total = total + slots_ref[s]
        o_ref[...] = total

    vm = pl.BlockSpec(memory_space=pltpu.VMEM)
    return pl.pallas_call(
        body, name="small_allreduce", in_specs=[vm], out_specs=vm,
        out_shape=jax.ShapeDtypeStruct(sv.shape, F32),
        scratch_shapes=[pltpu.VMEM((8,) + sv.shape, F32), pltpu.SemaphoreType.DMA((7,)),
                        pltpu.SemaphoreType.DMA((7,))],
    )(sv)


SMALL_ROWS = (("norm_mix_gain", 0), ("norm_mem_gain", 1), ("norm_ffn_gain", 2))
SMALL_LB = (("lb_logits_fw", 3), ("lb_logits_bw", 5))
SMALL_HEAD = ("hg_norm_gain", "da_q_gain", "da_k_gain", "mem_q_gain", "mem_k_gain")


def _pack_small(d):
    last = jnp.concatenate([d[n] for n in SMALL_HEAD] + [jnp.zeros((1, D_MODEL - HEAD * len(SMALL_HEAD)), F32)], axis=1)
    return jnp.concatenate([d["norm_mix_gain"], d["norm_mem_gain"], d["norm_ffn_gain"],
                            d["lb_logits_fw"], d["lb_logits_bw"], last], axis=0)


def _unpack_small(a):
    out = {n: a[r:r + 1] for n, r in SMALL_ROWS}
    out.update({n: a[r:r + 2] for n, r in SMALL_LB})
    out.update({n: a[7:8, j * HEAD:(j + 1) * HEAD] for j, n in enumerate(SMALL_HEAD)})
    return out


PARAM_ORDER = ("norm_mix_gain", "norm_mem_gain", "w_in", "lb_logits_fw", "lb_logits_bw", "hg_norm_gain",
               "da_q_gain", "da_k_gain", "w_mem_kv", "mem_q_gain", "mem_k_gain", "w_proj_hg", "w_proj_da",
               "w_proj_mem", "w_out", "norm_ffn_gain", "w_ffn_in", "w_ffn_out")


def kernel(x, mem, norm_mix_gain, norm_mem_gain, w_in, lb_logits_fw, lb_logits_bw, hg_norm_gain, da_q_gain, da_k_gain, w_mem_kv, mem_q_gain, mem_k_gain, w_proj_hg, w_proj_da, w_proj_mem, w_out, norm_ffn_gain, w_ffn_in, w_ffn_out, loss_target, m_norm_mix_gain, m_norm_mem_gain, m_w_in, m_lb_logits_fw, m_lb_logits_bw, m_hg_norm_gain, m_da_q_gain, m_da_k_gain, m_w_mem_kv, m_mem_q_gain, m_mem_k_gain, m_w_proj_hg, m_w_proj_da, m_w_proj_mem, m_w_out, m_norm_ffn_gain, m_w_ffn_in, m_w_ffn_out, v_norm_mix_gain, v_norm_mem_gain, v_w_in, v_lb_logits_fw, v_lb_logits_bw, v_hg_norm_gain, v_da_q_gain, v_da_k_gain, v_w_mem_kv, v_mem_q_gain, v_mem_k_gain, v_w_proj_hg, v_w_proj_da, v_w_proj_mem, v_w_out, v_norm_ffn_gain, v_w_ffn_in, v_w_ffn_out):
    args = dict(locals())
    mats = tuple(s[0] for s in W_SPECS)
    flat = lambda a: a.reshape(a.shape[-2:])
    w = {n: flat(args[n]) for n in mats}
    m = {n: flat(args["m_" + n]) for n in mats}
    v = {n: flat(args["v_" + n]) for n in mats}
    small = {n: args[n] for n in PARAM_ORDER if n not in mats}

    chip = (2 * lax.axis_index("x") + lax.axis_index("y")).astype(jnp.int32).reshape(1)
    ex = _Exchanges({n: _cast_into_full(w[n], chip, rows, cols, axis, name=f"cast_{n}")
                     for n, rows, cols, axis, _ in W_SPECS})
    loss, grad_x, small_grads = _local_step(x[0], mem[0], loss_target[0], small, ex)
    grads = ex.reduced()
    small_sum = _small_allreduce(small_grads)

    delta, new_m, new_v = {}, {}, {}
    for n in mats:
        delta[n], new_m[n], new_v[n], grads[n] = _adamw(w[n], grads[n], m[n], v[n], name=f"adamw_{n}")
    packed = _adamw(_pack_small(small), small_sum,
                    _pack_small({n: args["m_" + n] for n in small}),
                    _pack_small({n: args["v_" + n] for n in small}), name="adamw_small")
    for dst, src in zip((delta, new_m, new_v, grads), packed):
        dst.update(_unpack_small(src))

    def shaped(d, n):
        return d[n].reshape(args[n].shape)

    loss_sum = lax.psum(loss[0, 0], ("x", "y", "c"))
    return (loss_sum, grad_x[None], *[shaped(grads, n) for n in PARAM_ORDER], *[shaped(delta, n) for n in PARAM_ORDER],
            *[shaped(new_m, n) for n in PARAM_ORDER], *[shaped(new_v, n) for n in PARAM_ORDER])
```

```python
import functools
import math

import numpy as np
import jax
import jax.numpy as jnp
from jax import lax
from jax.experimental import pallas as pl
from jax.experimental.pallas import tpu as pltpu

F32 = jnp.float32
BF16 = jnp.bfloat16
MESH = pl.DeviceIdType.MESH

D_MODEL = 1024
HEAD = 128
HG_HEADS = 8
DA_GROUPS = ((1, 64), (4, 64), (16, 64))
DA_GH = 4
MEM_HEADS = 4
N_MEM = 256
D_FF = 2816
CHUNK = 64
BAND_QBLOCKS = 8
GLA_HEADS_PER_STEP = 4
RMS_EPS = 1e-6
NEG_INF = -1e30
HG_SCALE = HEAD ** -0.5
ATT_SCALE = HEAD ** -0.5
VMEM_LIMIT_V7X = 48 * 1024 * 1024

C_HQ, C_FF, C_FB, C_HI, C_HG = 0, 1024, 2048, 3072, 4096
C_DQ, C_DK, C_DV, C_MQ = 5120, 6656, 8192, 9728
C_GHG, C_GDA, C_GMEM = 10240, 11264, 12288
IN_COLS = 13312

ADAM_LR, ADAM_B1, ADAM_B2, ADAM_EPS, ADAM_WD, ADAM_STEP = 0.001, 0.9, 0.999, 1e-08, 0.01, 10


def _params(sem, vmem=VMEM_LIMIT_V7X):
    return pltpu.CompilerParams(dimension_semantics=sem, vmem_limit_bytes=vmem)


def _dot(a, b):
    return jnp.dot(a.astype(BF16), b.astype(BF16), preferred_element_type=F32)


def _dot_nt(a, b):
    return lax.dot_general(a.astype(BF16), b.astype(BF16), (((1,), (1,)), ((), ())),
                           preferred_element_type=F32)


def _dot_tn(a, b):
    return lax.dot_general(a.astype(BF16), b.astype(BF16), (((0,), (0,)), ((), ())),
                           preferred_element_type=F32)


def _sigmoid(v):
    return jax.nn.sigmoid(v.astype(F32))


def _ones(rows, cols):
    return (lax.broadcasted_iota(jnp.int32, (rows, cols), 0) >= 0).astype(BF16)


def _lane_sum(v):
    ones = _ones(HEAD, HEAD)
    hi = v.astype(BF16)
    mid = (v - hi.astype(F32)).astype(BF16)
    return jnp.dot(hi, ones, preferred_element_type=F32) + jnp.dot(mid, ones, preferred_element_type=F32)


def _row_mean(v):
    if v.shape[-1] == HEAD:
        return _lane_sum(v) * (1.0 / HEAD)
    return jnp.mean(v, axis=-1, keepdims=True)


def _rms(v):
    v = v.astype(F32)
    r = lax.rsqrt(_row_mean(v * v) + RMS_EPS)
    return v * r, r


def _rms_bwd(dy, xhat, r, gain):
    dxh = dy * gain
    dx = r * (dxh - xhat * _row_mean(dxh * xhat))
    return dx, dy * xhat


class _Carry:
    def __init__(self, arrays, out_shapes, aliases, sems, issue):
        self.arrays, self.out_shapes, self.aliases, self.sems, self.issue = arrays, out_shapes, aliases, sems, issue


NO_CARRY = object()


def _start_copies(copies):
    for parts, _, _ in copies:
        for cp in parts:
            cp.start()


def _wait_copies(copies):
    for _, whole, remote in copies:
        if remote:
            whole.wait_recv()
    for _, whole, remote in copies:
        if remote:
            whole.wait_send()
        else:
            whole.wait()


def _call(body, *, name, grid, in_specs, out_specs, out_shape, scratch_shapes, semantics, ins, carry=None,
          aliases=None):
    aliases = dict(aliases or {})
    if carry is None:
        res = pl.pallas_call(body, name=name, grid=grid, in_specs=in_specs, out_specs=out_specs, out_shape=out_shape,
                             scratch_shapes=scratch_shapes, input_output_aliases=aliases,
                             compiler_params=_params(semantics))(*ins)
        return list(res), []
    n_in, n_out, n_scr = len(ins), len(out_shape), len(scratch_shapes)
    c_in, c_out = len(carry.arrays), len(carry.out_shapes)

    def wrapped(*refs):
        pos = [0]

        def take(count):
            pos[0] += count
            return refs[pos[0] - count:pos[0]]

        own_in, carry_in = take(n_in), take(c_in)
        own_out, carry_out = take(n_out), take(c_out)
        own_scr, carry_sems = take(n_scr), take(len(carry.sems))
        ids = [pl.program_id(a) for a in range(len(grid))]
        first, last = ids[0] == 0, ids[0] == grid[0] - 1
        for a in range(1, len(grid)):
            first, last = first & (ids[a] == 0), last & (ids[a] == grid[a] - 1)

        @pl.when(first)
        def _():
            _start_copies(carry.issue(carry_in, carry_out, carry_sems, "start"))

        body(*own_in, *own_out, *own_scr)

        @pl.when(last)
        def _():
            _wait_copies(carry.issue(carry_in, carry_out, carry_sems, "wait"))

    res = pl.pallas_call(
        wrapped, name=name, grid=grid, in_specs=list(in_specs) + [ANY] * c_in,
        out_specs=list(out_specs) + [ANY] * c_out, out_shape=list(out_shape) + list(carry.out_shapes),
        input_output_aliases={**aliases, **{n_in + i: n_out + j for i, j in carry.aliases.items()}},
        scratch_shapes=list(scratch_shapes) + list(carry.sems),
        compiler_params=_params(("arbitrary",) * len(grid)))(*ins, *carry.arrays)
    return list(res[:n_out]), list(res[n_out:])


def _matmul(pairs, mode, out_dtype, *, tm, tn, tk=None, residual=None, also=None, carry=NO_CARRY, name):
    b_offs = [pr[2] if len(pr) > 2 else 0 for pr in pairs]
    pairs = [pr[:2] for pr in pairs]
    a0, b0 = pairs[0]
    if mode == "nn":
        (m, kk), n = a0.shape, b0.shape[1]
    elif mode == "nt":
        (m, kk), n = a0.shape, b0.shape[0]
    else:
        (kk, m), n = a0.shape, b0.shape[1]
    assert mode == "nt" or not any(b_offs)
    tm, tn = min(tm, m), min(tn, n)
    tk = kk if tk is None else tk
    nk = kk // tk
    assert m % tm == 0 and n % tn == 0 and kk % tk == 0, (name, m, n, kk)
    n_p = len(pairs)
    if mode == "tn":
        a_spec = pl.BlockSpec((tk, tm), lambda i, j, k: (k, i))
    else:
        a_spec = pl.BlockSpec((tm, tk), lambda i, j, k: (i, k))
    if mode == "nt":
        b_specs = [pl.BlockSpec((tn, tk), lambda i, j, k, o=o: (j, o * nk + k)) for o in b_offs]
    else:
        b_specs = [pl.BlockSpec((tk, tn), lambda i, j, k: (k, j))] * n_p
    o_spec = pl.BlockSpec((tm, tn), lambda i, j, k: (i, j))
    dot = {"nn": _dot, "nt": _dot_nt, "tn": _dot_tn}[mode]
    has_res = residual is not None

    def body(*refs):
        a_refs, b_refs = refs[:n_p], refs[n_p:2 * n_p]
        pos = 2 * n_p
        res_ref = refs[pos] if has_res else None
        pos += int(has_res)
        o_ref = refs[pos]
        pos += int(also is not None)
        part = dot(a_refs[0][...], b_refs[0][...])
        for a_r, b_r in zip(a_refs[1:], b_refs[1:]):
            part += dot(a_r[...], b_r[...])

        def finish(total):
            if has_res:
                total = total + res_ref[...]
            o_ref[...] = total.astype(out_dtype)
            if also is not None:
                refs[pos][...] = total.astype(also)

        if nk == 1:
            finish(part)
        else:
            acc_ref = refs[pos + 1]
            k = pl.program_id(2)

            @pl.when(k == 0)
            def _():
                acc_ref[...] = part

            @pl.when(k > 0)
            def _():
                acc_ref[...] += part

            @pl.when(k == nk - 1)
            def _():
                finish(acc_ref[...])

    ins = [a for a, _ in pairs] + [b for _, b in pairs]
    in_specs = [a_spec] * n_p + b_specs
    if has_res:
        ins.append(residual)
        in_specs.append(o_spec)
    dtypes = [out_dtype] + ([also] if also is not None else [])
    outs, carried = _call(
        body, name=name, grid=(m // tm, n // tn, nk), in_specs=in_specs, out_specs=[o_spec] * len(dtypes),
        out_shape=[jax.ShapeDtypeStruct((m, n), dt) for dt in dtypes],
        scratch_shapes=[pltpu.VMEM((tm, tn), F32)] if nk > 1 else [],
        semantics=("parallel", "parallel", "arbitrary"), ins=ins, carry=None if carry is NO_CARRY else carry)
    out = outs[0] if also is None else tuple(outs)
    return out if carry is NO_CARRY else (out, carried)


def _rms_fwd(x, gain, *, name):
    rows, dm = x.shape
    tm = min(1024, rows)

    def body(x_ref, g_ref, h_ref):
        xhat, _ = _rms(x_ref[...])
        h_ref[...] = (xhat * g_ref[...]).astype(BF16)

    return pl.pallas_call(
        body, name=name, grid=(rows // tm,),
        in_specs=[pl.BlockSpec((tm, dm), lambda i: (i, 0)), pl.BlockSpec((1, dm), lambda i: (0, 0))],
        out_specs=pl.BlockSpec((tm, dm), lambda i: (i, 0)),
        out_shape=jax.ShapeDtypeStruct((rows, dm), BF16),
        compiler_params=_params(("parallel",)),
    )(x, gain)


def _rms_bwd_rows(dh, x, gain, dres, dx_dtypes, *, name):
    rows, dm = x.shape
    tm = min(512, rows)
    has_res = dres is not None

    def body(*refs):
        dh_ref, x_ref, g_ref = refs[:3]
        res_ref = refs[3] if has_res else None
        outs = refs[3 + int(has_res):]
        dg_ref = outs[-1]
        xhat, r = _rms(x_ref[...])
        dx, dgr = _rms_bwd(dh_ref[...], xhat, r, g_ref[...])
        if has_res:
            dx = dx + res_ref[...]
        for dx_ref, dt in zip(outs, dx_dtypes):
            dx_ref[...] = dx.astype(dt)

        @pl.when(pl.program_id(0) == 0)
        def _():
            dg_ref[...] = jnp.zeros_like(dg_ref)

        dg_ref[...] += jnp.sum(dgr, axis=0, keepdims=True)

    row = pl.BlockSpec((tm, dm), lambda i: (i, 0))
    vec = pl.BlockSpec((1, dm), lambda i: (0, 0))
    return pl.pallas_call(
        body, name=name, grid=(rows // tm,),
        in_specs=[row, row, vec] + ([row] if has_res else []),
        out_specs=[row] * len(dx_dtypes) + [vec],
        out_shape=[jax.ShapeDtypeStruct((rows, dm), dt) for dt in dx_dtypes] + [jax.ShapeDtypeStruct((1, dm), F32)],
        compiler_params=_params(("arbitrary",)),
    )(*([dh, x, gain] + ([dres] if has_res else [])))


def _out_norm(a, w, x, gain, *, name):
    rows, dm = x.shape
    tm = min(512, rows)

    def body(a_ref, w_ref, x_ref, g_ref, x1_ref, h_ref):
        x1 = x_ref[...] + _dot(a_ref[...], w_ref[...])
        x1_ref[...] = x1
        xhat, _ = _rms(x1)
        h_ref[...] = (xhat * g_ref[...]).astype(BF16)

    row = lambda wd: pl.BlockSpec((tm, wd), lambda i: (i, 0))
    return pl.pallas_call(
        body, name=name, grid=(rows // tm,),
        in_specs=[row(a.shape[1]), pl.BlockSpec(w.shape, lambda i: (0, 0)), row(dm), pl.BlockSpec((1, dm), lambda i: (0, 0))],
        out_specs=[row(dm), row(dm)],
        out_shape=[jax.ShapeDtypeStruct((rows, dm), F32), jax.ShapeDtypeStruct((rows, dm), BF16)],
        compiler_params=_params(("parallel",)),
    )(a, w, x, gain)


def _out_loss(u, w, x1, tgt, *, name):
    rows, dm = x1.shape
    tm = min(256, rows)
    steps = rows // tm

    def body(u_ref, w_ref, x_ref, t_ref, dy_ref, dyb_ref, loss_ref, acc_ref):
        i = pl.program_id(0)
        diff = x_ref[...] + _dot(u_ref[...], w_ref[...]) - t_ref[...]
        dy = diff * (1.0 / dm)
        dy_ref[...] = dy
        dyb_ref[...] = dy.astype(BF16)

        @pl.when(i == 0)
        def _():
            acc_ref[...] = jnp.zeros_like(acc_ref)

        acc_ref[...] += jnp.sum(diff * diff, axis=0, keepdims=True)

        @pl.when(i == steps - 1)
        def _():
            loss_ref[...] = jnp.full((1, HEAD), 0.5 / dm, F32) * jnp.sum(acc_ref[...])

    row = lambda wd: pl.BlockSpec((tm, wd), lambda i: (i, 0))
    return pl.pallas_call(
        body, name=name, grid=(steps,),
        in_specs=[row(u.shape[1]), pl.BlockSpec(w.shape, lambda i: (0, 0)), row(dm), row(dm)],
        out_specs=[row(dm), row(dm), pl.BlockSpec((1, HEAD), lambda i: (0, 0))],
        out_shape=[jax.ShapeDtypeStruct((rows, dm), F32), jax.ShapeDtypeStruct((rows, dm), BF16),
                   jax.ShapeDtypeStruct((1, HEAD), F32)],
        scratch_shapes=[pltpu.VMEM((1, dm), F32)],
        compiler_params=_params(("arbitrary",)),
    )(u, w, x1, tgt)


def _gla_block_terms(q_raw, f_logit, lb, rev):
    sig = _sigmoid(f_logit)
    forget = lb + (1.0 - lb) * sig
    k = 1.0 - forget
    b = _chunk_cumsum(jnp.log(forget), rev)
    qs = _sigmoid(q_raw)
    eb = jnp.exp(b)
    emb = jnp.exp(-b)
    qt = (q_raw * qs * HG_SCALE) * eb
    kt = k * emb
    return sig, forget, k, b, qs, eb, emb, qt, kt


def _chunk_cumsum(v, rev):
    n = v.shape[0]
    pos = lax.broadcasted_iota(jnp.int32, v.shape, 0) & (CHUNK - 1)
    step = 1
    while step < CHUNK:
        if rev:
            shifted, keep = pltpu.roll(v, n - step, 0), pos < CHUNK - step
        else:
            shifted, keep = pltpu.roll(v, step, 0), pos >= step
        v = v + jnp.where(keep, shifted, 0.0)
        step *= 2
    return v


def _tri_mask(n, rev):
    row, col = np.arange(n)[:, None], np.arange(n)[None, :]
    same = (row // CHUNK) == (col // CHUNK)
    return jnp.asarray((same & ((row <= col) if rev else (row >= col))).astype(np.float32))


def _chunk_order(ncb, rev):
    order = range(ncb - 1, -1, -1) if rev else range(ncb)
    return [(c, c * CHUNK if rev else c * CHUNK + CHUNK - 1) for c in order]


def _gla_fwd(proj, lb, *, f_off, rev, name):
    rows = proj.shape[0]
    tb = min(256, rows)
    nb, ncb = rows // tb, tb // CHUNK

    def tmap(n):
        return nb - 1 - n if rev else n

    def body(tri_ref, q_ref, f_ref, v_ref, lb_ref, o_ref, st_ref, s_ref):
        @pl.when(pl.program_id(1) == 0)
        def _():
            s_ref[...] = jnp.zeros_like(s_ref)

        tri = tri_ref[...] > 0.5
        for hh in range(GLA_HEADS_PER_STEP):
            cs = slice(hh * HEAD, (hh + 1) * HEAD)
            v = v_ref[:, cs]
            _, _, k, b, _, _, _, qt, kt = _gla_block_terms(q_ref[:, cs].astype(F32), f_ref[:, cs].astype(F32),
                                                           lb_ref[:, cs], rev)
            qt = qt.astype(BF16)
            o_intra = _dot(jnp.where(tri, _dot_nt(qt, kt), 0.0), v)
            chunks = []
            for c, last in _chunk_order(ncb, rev):
                sl = slice(c * CHUNK, (c + 1) * CHUNK)
                bl = b[last:last + 1, :]
                kh = k[sl] * jnp.exp(bl - b[sl])
                chunks.append((c, sl, jnp.exp(bl), _dot_tn(v[sl], kh)))
            s_t = s_ref[hh]
            for c, sl, ebl, kv in chunks:
                st_ref[hh, c] = s_t
                o_ref[sl, cs] = o_intra[sl] + _dot_nt(qt[sl], s_t)
                s_t = ebl * s_t + kv
            s_ref[hh] = s_t

    hps = GLA_HEADS_PER_STEP
    col = lambda off: pl.BlockSpec((tb, hps * HEAD), lambda h, n: (tmap(n), off // (hps * HEAD) + h))
    return pl.pallas_call(
        body, name=name, grid=(HG_HEADS // hps, nb),
        in_specs=[pl.BlockSpec((tb, tb), lambda h, n: (0, 0)), col(C_HQ), col(f_off), col(C_HI),
                  pl.BlockSpec((1, hps * HEAD), lambda h, n: (0, h))],
        out_specs=[pl.BlockSpec((tb, hps * HEAD), lambda h, n: (tmap(n), h)),
                   pl.BlockSpec((hps, ncb, HEAD, HEAD), lambda h, n: (h, tmap(n), 0, 0))],
        out_shape=[jax.ShapeDtypeStruct((rows, HG_HEADS * HEAD), F32),
                   jax.ShapeDtypeStruct((HG_HEADS, rows // CHUNK, HEAD, HEAD), F32)],
        scratch_shapes=[pltpu.VMEM((hps, HEAD, HEAD), F32)],
        compiler_params=_params(("parallel", "arbitrary")),
    )(_tri_mask(tb, rev), proj, proj, proj, lb)


def _gla_bwd(proj, lb, do, states, prev, dproj, *, f_off, rev, carry=NO_CARRY, name):
    rows = proj.shape[0]
    tb = min(256, rows)
    nb, ncb = rows // tb, tb // CHUNK
    has_prev = prev is not None
    qv_dtype = BF16 if has_prev else F32

    def tmap(n):
        return n if rev else nb - 1 - n

    def body(*refs):
        tri_ref, q_ref, f_ref, v_ref, lb_ref, do_ref, st_ref = refs[:7]
        pq_ref, pv_ref = refs[8:10] if has_prev else (None, None)
        (dq_ref, df_ref, dv_ref, dlb_ref,
         ds_ref, dqt_scr, dk_scr, db_scr, dbl_scr, dv_scr) = refs[8 + 2 * int(has_prev):]

        @pl.when(pl.program_id(1) == 0)
        def _():
            ds_ref[...] = jnp.zeros_like(ds_ref)
            dlb_ref[...] = jnp.zeros_like(dlb_ref)

        tri = tri_ref[...] > 0.5
        for hh in range(GLA_HEADS_PER_STEP):
            cs = slice(hh * HEAD, (hh + 1) * HEAD)
            lbv = lb_ref[:, cs]
            q_raw = q_ref[:, cs].astype(F32)
            v, dout = v_ref[:, cs], do_ref[:, cs].astype(BF16)
            sig, forget, k, b, qs, eb, emb, qt, kt = _gla_block_terms(q_raw, f_ref[:, cs].astype(F32), lbv, rev)
            qt_b, kt_b = qt.astype(BF16), kt.astype(BF16)
            a = jnp.where(tri, _dot_nt(qt_b, kt_b), 0.0)
            da = jnp.where(tri, _dot_nt(dout, v), 0.0).astype(BF16)
            dv_intra = _dot_tn(a, dout)
            dqt_intra = _dot(da, kt_b)
            dkt = _dot_tn(da, qt_b)
            chunks = []
            for c, last in reversed(_chunk_order(ncb, rev)):
                sl = slice(c * CHUNK, (c + 1) * CHUNK)
                bl = b[last:last + 1, :]
                e = jnp.exp(bl - b[sl])
                s_t = st_ref[hh, c]
                dqt_scr[sl, cs] = dqt_intra[sl] + _dot(dout[sl], s_t)
                chunks.append((sl, jnp.exp(bl), e, k[sl] * e, s_t, _dot_tn(dout[sl], qt_b[sl])))
            ds_t = ds_ref[hh]
            for sl, ebl, e, kh, s_t, grow in chunks:
                dkh = _dot(v[sl], ds_t)
                dv_scr[sl, cs] = dv_intra[sl] + _dot_nt(kh, ds_t)
                dk_scr[sl, cs] = dkt[sl] * emb[sl] + dkh * e
                khd = kh * dkh
                dbl = jnp.sum(khd, axis=0, keepdims=True) + ebl * jnp.sum(ds_t * s_t, axis=0, keepdims=True)
                db_scr[sl, cs] = khd
                dbl_scr[sl, cs] = jnp.broadcast_to(dbl, (CHUNK, HEAD))
                ds_t = grow + ds_t * ebl
            ds_ref[hh] = ds_t
            dqt = dqt_scr[:, cs]
            dlogf = _chunk_cumsum(qt * dqt - kt * dkt - db_scr[:, cs], not rev) + dbl_scr[:, cs]
            dforget = dlogf / forget - dk_scr[:, cs]
            df_ref[:, cs] = (dforget * (1.0 - lbv) * sig * (1.0 - sig)).astype(BF16)
            dlb_ref[:, cs] += jnp.sum(dforget * (1.0 - sig), axis=0, keepdims=True)
            dqr = dqt * eb * (HG_SCALE * qs * (1.0 + q_raw * (1.0 - qs)))
            dv = dv_scr[:, cs]
            if has_prev:
                dqr = dqr + pq_ref[:, cs]
                dv = dv + pv_ref[:, cs]
            dq_ref[:, cs] = dqr.astype(qv_dtype)
            dv_ref[:, cs] = dv.astype(qv_dtype)

    hps = GLA_HEADS_PER_STEP
    col = lambda off: pl.BlockSpec((tb, hps * HEAD), lambda h, n: (tmap(n), off // (hps * HEAD) + h))
    blk = pl.BlockSpec((tb, hps * HEAD), lambda h, n: (tmap(n), h))
    vec = pl.BlockSpec((1, hps * HEAD), lambda h, n: (0, h))
    wide = HG_HEADS * HEAD
    into = jax.ShapeDtypeStruct(dproj.shape, dproj.dtype)
    outs, carried = _call(
        body, name=name, grid=(HG_HEADS // hps, nb),
        in_specs=[pl.BlockSpec((tb, tb), lambda h, n: (0, 0)), col(C_HQ), col(f_off), col(C_HI), vec, blk,
                  pl.BlockSpec((hps, ncb, HEAD, HEAD), lambda h, n: (h, tmap(n), 0, 0)), ANY]
                 + ([blk, blk] if has_prev else []),
        out_specs=[col(C_HQ) if has_prev else blk, blk if has_prev else col(f_off), blk, vec],
        out_shape=[into if has_prev else jax.ShapeDtypeStruct((rows, wide), qv_dtype),
                   jax.ShapeDtypeStruct((rows, wide), BF16) if has_prev else into,
                   jax.ShapeDtypeStruct((rows, wide), qv_dtype), jax.ShapeDtypeStruct((1, wide), F32)],
        aliases={7: 0 if has_prev else 1},
        scratch_shapes=[pltpu.VMEM((hps, HEAD, HEAD), F32)] + [pltpu.VMEM((tb, hps * HEAD), F32)] * 5,
        semantics=("parallel", "arbitrary"),
        ins=[_tri_mask(tb, rev), proj, proj, proj, lb, do, states, dproj] + (list(prev) if has_prev else []),
        carry=None if carry is NO_CARRY else carry)
    return outs if carry is NO_CARRY else (outs, carried)


def _fill_columns(dproj, parts, col, *, name):
    rows, wd = parts[0].shape
    tm = min(1024, rows)
    n = len(parts)

    def body(*refs):
        o_ref = refs[n + 1]
        for j in range(n):
            o_ref[:, j * wd:(j + 1) * wd] = refs[j][...]

    return pl.pallas_call(
        body, name=name, grid=(rows // tm,),
        in_specs=[pl.BlockSpec((tm, wd), lambda i: (i, 0))] * n + [ANY],
        out_specs=pl.BlockSpec((pl.Element(tm), pl.Element(n * wd)), lambda i: (i * tm, col)),
        out_shape=jax.ShapeDtypeStruct(dproj.shape, dproj.dtype), input_output_aliases={n: 0},
        compiler_params=_params(("parallel",)),
    )(*parts, dproj)


def _hg_out_fwd(o_fw, o_bw, proj, gain, *, name):
    rows = o_fw.shape[0]
    tm = min(1024, rows)
    wide = HG_HEADS * HEAD

    def body(a_ref, b_ref, g_ref, gain_ref, o_ref):
        for h in range(HG_HEADS):
            sl = slice(h * HEAD, (h + 1) * HEAD)
            xhat, _ = _rms(a_ref[:, sl] + b_ref[:, sl])
            gate = g_ref[:, sl].astype(F32)
            o_ref[:, sl] = (xhat * gain_ref[...] * (gate * _sigmoid(gate))).astype(BF16)

    row = pl.BlockSpec((tm, wide), lambda i: (i, 0))
    return pl.pallas_call(
        body, name=name, grid=(rows // tm,),
        in_specs=[row, row, pl.BlockSpec((tm, wide), lambda i: (i, C_HG // wide)),
                  pl.BlockSpec((1, HEAD), lambda i: (0, 0))],
        out_specs=row, out_shape=jax.ShapeDtypeStruct((rows, wide), BF16),
        compiler_params=_params(("parallel",)),
    )(o_fw, o_bw, proj, gain)


def _hg_out_bwd(dout, o_fw, o_bw, proj, gain, dproj, *, name):
    rows = o_fw.shape[0]
    tm = min(512, rows)
    wide = HG_HEADS * HEAD

    def body(d_ref, a_ref, b_ref, g_ref, gain_ref, dp_in, dgate_ref, do_ref, dgain_ref):
        del dp_in

        @pl.when(pl.program_id(0) == 0)
        def _():
            dgain_ref[...] = jnp.zeros_like(dgain_ref)

        dgain = jnp.zeros((1, HEAD), F32)
        for h in range(HG_HEADS):
            sl = slice(h * HEAD, (h + 1) * HEAD)
            xhat, r = _rms(a_ref[:, sl] + b_ref[:, sl])
            gate, dy = g_ref[:, sl].astype(F32), d_ref[:, sl]
            gs = _sigmoid(gate)
            dgate_ref[:, sl] = (dy * xhat * gain_ref[...] * (gs * (1.0 + gate * (1.0 - gs)))).astype(BF16)
            dx, dgr = _rms_bwd(dy * (gate * gs), xhat, r, gain_ref[...])
            do_ref[:, sl] = dx
            dgain = dgain + jnp.sum(dgr, axis=0, keepdims=True)
        dgain_ref[...] += dgain

    row = pl.BlockSpec((tm, wide), lambda i: (i, 0))
    gate = pl.BlockSpec((tm, wide), lambda i: (i, C_HG // wide))
    vec = pl.BlockSpec((1, HEAD), lambda i: (0, 0))
    return pl.pallas_call(
        body, name=name, grid=(rows // tm,),
        in_specs=[row, row, row, gate, vec, ANY],
        out_specs=[gate, row, vec],
        out_shape=[jax.ShapeDtypeStruct(dproj.shape, dproj.dtype), jax.ShapeDtypeStruct((rows, wide), F32),
                   jax.ShapeDtypeStruct((1, HEAD), F32)],
        input_output_aliases={5: 0},
        compiler_params=_params(("arbitrary",)),
    )(dout, o_fw, o_bw, proj, gain, dproj)


def _strided_rows(ref, r, count, d):
    return ref[...] if d == 1 else ref[pl.ds(r, count, stride=d), :]


def _store_strided(ref, r, count, d, val):
    if d == 1:
        ref[...] = val
    else:
        ref[pl.ds(r, count, stride=d), :] = val


def _da_prep(proj, q_gain, k_gain, g, *, name):
    d = DA_GROUPS[g][0]
    rows = proj.shape[0]
    rb = min(1024, rows)
    tn = rb // d

    def body(q_ref, k_ref, v_ref, qg_ref, kg_ref, qo_ref, ko_ref, vo_ref, qf_ref, kf_ref, vf_ref):
        for h in range(DA_GH):
            cs = slice(h * HEAD, (h + 1) * HEAD)
            for src, dst in ((q_ref, qf_ref), (k_ref, kf_ref), (v_ref, vf_ref)):
                dst[...] = src[:, cs].astype(F32)
            for r in range(d):
                qhat, _ = _rms(_strided_rows(qf_ref, r, tn, d))
                khat, _ = _rms(_strided_rows(kf_ref, r, tn, d))
                qo_ref[h, r] = (qhat * qg_ref[...]).astype(BF16)
                ko_ref[h, r] = (khat * kg_ref[...]).astype(BF16)
                vo_ref[h, r] = _strided_rows(vf_ref, r, tn, d).astype(BF16)

    wide = DA_GH * HEAD
    col = lambda off: pl.BlockSpec((rb, wide), lambda i: (i, off // wide + g))
    vec = pl.BlockSpec((1, HEAD), lambda i: (0, 0))
    out = pl.BlockSpec((DA_GH, d, tn, HEAD), lambda i: (0, 0, i, 0))
    shape = jax.ShapeDtypeStruct((DA_GH, d, rows // d, HEAD), BF16)
    return pl.pallas_call(
        body, name=name, grid=(rows // rb,),
        in_specs=[col(C_DQ), col(C_DK), col(C_DV), vec, vec],
        out_specs=[out, out, out], out_shape=[shape, shape, shape],
        scratch_shapes=[pltpu.VMEM((rb, HEAD), F32)] * 3,
        compiler_params=_params(("parallel",)),
    )(proj, proj, proj, q_gain, k_gain)


def _slopes(g):
    idx = np.arange(g * DA_GH + 1, (g + 1) * DA_GH + 1)
    s = (2.0 ** (-8.0 * idx / (DA_GH * len(DA_GROUPS)))).astype(np.float32)
    return jnp.asarray(np.broadcast_to(s[:, None, None], (DA_GH, 8, HEAD)).copy())


def _band_window(ld, t, radius):
    win = min(2 * t, ld)
    assert t // 2 >= radius or win == ld
    return win


def _band_scores(q, k, q0, start, slope, d, radius):
    t, win = q.shape[0], k.shape[0]
    row = lax.broadcasted_iota(jnp.int32, (t, win), 0)
    col = lax.broadcasted_iota(jnp.int32, (t, win), 1)
    rel = jnp.abs((start - q0) + col - row)
    return _dot_nt(q, k) * ATT_SCALE + jnp.where(rel <= radius, -slope * (d * rel).astype(F32), NEG_INF)


def _band_fwd(qr, kr, vr, g, *, name):
    d, radius = DA_GROUPS[g]
    _, _, ld, _ = qr.shape
    t = min(HEAD, ld)
    qb = min(BAND_QBLOCKS, ld // t)
    rps = max(1, min(d, BAND_QBLOCKS // qb))
    win = _band_window(ld, t, radius)

    def body(q_ref, k_ref, v_ref, sl_ref, o_ref, lse_ref):
        i = pl.program_id(2)
        slope = sl_ref[0:1, 0:1]
        for rr in range(rps):
            for j in range(qb):
                sl = slice(j * t, (j + 1) * t)
                q0 = (i * qb + j) * t
                start = pl.multiple_of(jnp.clip(q0 - t // 2, 0, ld - win), t // 2)
                s = _band_scores(q_ref[rr, sl, :], k_ref[rr, pl.ds(start, win), :], q0, start, slope, d, radius)
                m = jnp.max(s, axis=-1, keepdims=True)
                p = jnp.exp(s - m).astype(BF16)
                l = jnp.dot(p, _ones(win, HEAD), preferred_element_type=F32)
                o_ref[rr, sl, :] = _dot(p, v_ref[rr, pl.ds(start, win), :]) / l
                lse_ref[rr, sl, :] = m + jnp.log(l)

    own = pl.BlockSpec((None, rps, qb * t, HEAD), lambda h, r, i: (h, r, i, 0))
    seq = pl.BlockSpec((None, rps, ld, HEAD), lambda h, r, i: (h, r, 0, 0))
    shape = jax.ShapeDtypeStruct(qr.shape, F32)
    return pl.pallas_call(
        body, name=name, grid=(DA_GH, d // rps, ld // (qb * t)),
        in_specs=[own, seq, seq, pl.BlockSpec((None, 8, HEAD), lambda h, r, i: (h, 0, 0))],
        out_specs=[own, own], out_shape=[shape, shape],
        compiler_params=_params(("parallel", "parallel", "parallel")),
    )(qr, kr, vr, _slopes(g))


def _band_bwd(qr, kr, vr, dor, lser, deltar, g, *, name):
    d, radius = DA_GROUPS[g]
    _, _, ld, _ = qr.shape
    t = min(HEAD, ld)
    qb = min(BAND_QBLOCKS, ld // t)
    rps = max(1, min(d, BAND_QBLOCKS // qb))
    win = _band_window(ld, t, radius)

    def body(q_ref, k_ref, v_ref, do_ref, lse_ref, dl_ref, sl_ref, dq_ref, dk_ref, dv_ref):
        i = pl.program_id(2)

        @pl.when(i == 0)
        def _():
            dk_ref[...] = jnp.zeros_like(dk_ref)
            dv_ref[...] = jnp.zeros_like(dv_ref)

        slope = sl_ref[0:1, 0:1]
        for rr in range(rps):
            for j in range(qb):
                sl = slice(j * t, (j + 1) * t)
                q0 = (i * qb + j) * t
                start = pl.multiple_of(jnp.clip(q0 - t // 2, 0, ld - win), t // 2)
                rows = pl.ds(start, win)
                q, dout, k, v = q_ref[rr, sl, :], do_ref[rr, sl, :], k_ref[rr, rows, :], v_ref[rr, rows, :]
                p = jnp.exp(_band_scores(q, k, q0, start, slope, d, radius) - lse_ref[rr, sl, 0:1])
                ds = p * (_dot_nt(dout, v) - dl_ref[rr, sl, 0:1])
                dq_ref[rr, sl, :] = _dot(ds, k) * ATT_SCALE
                dk_ref[rr, rows, :] += _dot_tn(ds, q) * ATT_SCALE
                dv_ref[rr, rows, :] += _dot_tn(p, dout)

    own = pl.BlockSpec((None, rps, qb * t, HEAD), lambda h, r, i: (h, r, i, 0))
    seq = pl.BlockSpec((None, rps, ld, HEAD), lambda h, r, i: (h, r, 0, 0))
    shape = jax.ShapeDtypeStruct(qr.shape, F32)
    return pl.pallas_call(
        body, name=name, grid=(DA_GH, d // rps, ld // (qb * t)),
        in_specs=[own, seq, seq, own, own, own, pl.BlockSpec((None, 8, HEAD), lambda h, r, i: (h, 0, 0))],
        out_specs=[own, seq, seq], out_shape=[shape, shape, shape],
        compiler_params=_params(("parallel", "parallel", "arbitrary")),
    )(qr, kr, vr, dor, lser, deltar, _slopes(g))


def _da_merge(outs, lses, rows, *, name):
    rb = min(1024, rows)
    wide = DA_GH * HEAD

    def body(*refs):
        o_refs, l_refs = refs[0:3], refs[3:6]
        o_ref, lse_ref = refs[6:8]
        on_refs, ln_refs = refs[8:11], refs[11:14]
        for h in range(DA_GH):
            cs = slice(h * HEAD, (h + 1) * HEAD)
            for g, (d, _) in enumerate(DA_GROUPS):
                tn = rb // d
                for r in range(d):
                    _store_strided(on_refs[g], r, tn, d, o_refs[g][h, r])
                    _store_strided(ln_refs[g], r, tn, d, l_refs[g][h, r])
            l0, l1, l2 = ln_refs[0][...], ln_refs[1][...], ln_refs[2][...]
            m = jnp.maximum(jnp.maximum(l0, l1), l2)
            e0, e1, e2 = jnp.exp(l0 - m), jnp.exp(l1 - m), jnp.exp(l2 - m)
            tot = e0 + e1 + e2
            o_ref[:, cs] = (e0 * on_refs[0][...] + e1 * on_refs[1][...] + e2 * on_refs[2][...]) / tot
            lse_ref[:, cs] = m + jnp.log(tot)

    res = lambda d: pl.BlockSpec((DA_GH, d, rb // d, HEAD), lambda i: (0, 0, i, 0))
    nat = pl.BlockSpec((rb, wide), lambda i: (i, 0))
    shape = jax.ShapeDtypeStruct((rows, wide), F32)
    return pl.pallas_call(
        body, name=name, grid=(rows // rb,),
        in_specs=[res(d) for d, _ in DA_GROUPS] * 2,
        out_specs=[nat, nat], out_shape=[shape, shape],
        scratch_shapes=[pltpu.VMEM((rb, HEAD), F32)] * 6,
        compiler_params=_params(("parallel",)),
    )(*outs, *lses)


def _da_bwd_prep(dout, o, lse, *, name):
    rows = o.shape[0]
    rb = min(512, rows)
    wide = DA_GH * HEAD

    def body(d_ref, o_ref, l_ref, *outs):
        d_scr, l_scr, delta_scr = outs[9:12]
        for h in range(DA_GH):
            cs = slice(h * HEAD, (h + 1) * HEAD)
            dv = d_ref[:, cs]
            d_scr[...] = dv
            l_scr[...] = l_ref[:, cs]
            delta_scr[...] = _lane_sum(dv * o_ref[:, cs])
            for g, (d, _) in enumerate(DA_GROUPS):
                tn = rb // d
                for r in range(d):
                    outs[3 * g][h, r] = _strided_rows(d_scr, r, tn, d).astype(BF16)
                    outs[3 * g + 1][h, r] = _strided_rows(l_scr, r, tn, d)
                    outs[3 * g + 2][h, r] = _strided_rows(delta_scr, r, tn, d)

    nat = pl.BlockSpec((rb, wide), lambda i: (i, 0))
    out_specs, out_shape = [], []
    for d, _ in DA_GROUPS:
        for dt in (BF16, F32, F32):
            out_specs.append(pl.BlockSpec((DA_GH, d, rb // d, HEAD), lambda i: (0, 0, i, 0)))
            out_shape.append(jax.ShapeDtypeStruct((DA_GH, d, rows // d, HEAD), dt))
    return pl.pallas_call(
        body, name=name, grid=(rows // rb,),
        in_specs=[nat, nat, nat], out_specs=out_specs, out_shape=out_shape,
        scratch_shapes=[pltpu.VMEM((rb, HEAD), F32)] * 3,
        compiler_params=_params(("parallel",)),
    )(dout, o, lse)


def _da_prep_bwd(dqr, dkr, dvr, proj, q_gain, k_gain, g, *, name):
    d = DA_GROUPS[g][0]
    rows = proj.shape[0]
    rb = min(1024, rows)
    tn = rb // d
    wide = DA_GH * HEAD

    def body(dq_ref, dk_ref, dv_ref, q_ref, k_ref, qg_ref, kg_ref, oq_ref, ok_ref, ov_ref, gq_ref, gk_ref, *nat_refs):
        @pl.when(pl.program_id(0) == 0)
        def _():
            gq_ref[...] = jnp.zeros_like(gq_ref)
            gk_ref[...] = jnp.zeros_like(gk_ref)

        for h in range(DA_GH):
            cs = slice(h * HEAD, (h + 1) * HEAD)
            for j, src in enumerate((dq_ref, dk_ref, dv_ref)):
                for r in range(d):
                    _store_strided(nat_refs[j], r, tn, d, src[h, r])
            ov_ref[:, cs] = nat_refs[2][...].astype(BF16)
            for j, (x_ref, gn_ref, out_ref, acc_ref) in enumerate(((q_ref, qg_ref, oq_ref, gq_ref),
                                                                    (k_ref, kg_ref, ok_ref, gk_ref))):
                xhat, r = _rms(x_ref[:, cs])
                dx, dgr = _rms_bwd(nat_refs[j][...], xhat, r, gn_ref[...])
                out_ref[:, cs] = dx.astype(BF16)
                acc_ref[...] += jnp.sum(dgr, axis=0, keepdims=True)

    res = pl.BlockSpec((DA_GH, d, tn, HEAD), lambda i: (0, 0, i, 0))
    col = lambda off: pl.BlockSpec((rb, wide), lambda i: (i, off // wide + g))
    vec = pl.BlockSpec((1, HEAD), lambda i: (0, 0))
    nat = pl.BlockSpec((rb, wide), lambda i: (i, 0))
    shape = jax.ShapeDtypeStruct((rows, wide), BF16)
    gshape = jax.ShapeDtypeStruct((1, HEAD), F32)
    return pl.pallas_call(
        body, name=name, grid=(rows // rb,),
        in_specs=[res, res, res, col(C_DQ), col(C_DK), vec, vec],
        out_specs=[nat, nat, nat, vec, vec], out_shape=[shape, shape, shape, gshape, gshape],
        scratch_shapes=[pltpu.VMEM((rb, HEAD), F32)] * 3,
        compiler_params=_params(("arbitrary",)),
    )(dqr, dkr, dvr, proj, proj, q_gain, k_gain)


def _mem_fwd(proj, kv, q_gain, k_gain, *, name):
    rows = proj.shape[0]
    tm = min(1024, rows)
    n_mem = kv.shape[0]

    def body(q_ref, k_ref, v_ref, qg_ref, kg_ref, o_ref):
        qhat, _ = _rms(q_ref[...])
        khat, _ = _rms(k_ref[...])
        s = _dot_nt(qhat * qg_ref[...], khat * kg_ref[...]) * ATT_SCALE
        p = jnp.exp(s - jnp.max(s, axis=-1, keepdims=True))
        p = p / jnp.sum(p, axis=-1, keepdims=True)
        o_ref[...] = _dot(p, v_ref[...]).astype(BF16)

    vec = pl.BlockSpec((1, HEAD), lambda i, h: (0, 0))
    return pl.pallas_call(
        body, name=name, grid=(rows // tm, MEM_HEADS),
        in_specs=[pl.BlockSpec((tm, HEAD), lambda i, h: (i, C_MQ // HEAD + h)),
                  pl.BlockSpec((n_mem, HEAD), lambda i, h: (0, h)),
                  pl.BlockSpec((n_mem, HEAD), lambda i, h: (0, MEM_HEADS + h)), vec, vec],
        out_specs=pl.BlockSpec((tm, HEAD), lambda i, h: (i, h)),
        out_shape=jax.ShapeDtypeStruct((rows, MEM_HEADS * HEAD), BF16),
        compiler_params=_params(("parallel", "parallel")),
    )(proj, kv, kv, q_gain, k_gain)


def _mem_bwd(dout, proj, kv, q_gain, k_gain, dproj, *, name):
    rows = proj.shape[0]
    tm = min(1024, rows)
    steps = rows // tm
    n_mem = kv.shape[0]

    def body(d_ref, q_ref, k_ref, v_ref, qg_ref, kg_ref, dp_in, dq_ref, dk_ref, dv_ref, gq_ref, gk_ref, dkn_ref):
        del dp_in
        h, i = pl.program_id(0), pl.program_id(1)

        @pl.when((h == 0) & (i == 0))
        def _():
            gq_ref[...] = jnp.zeros_like(gq_ref)
            gk_ref[...] = jnp.zeros_like(gk_ref)

        @pl.when(i == 0)
        def _():
            dkn_ref[...] = jnp.zeros_like(dkn_ref)
            dv_ref[...] = jnp.zeros_like(dv_ref)

        qhat, rq = _rms(q_ref[...])
        khat, rk = _rms(k_ref[...])
        qn, kn = qhat * qg_ref[...], khat * kg_ref[...]
        s = _dot_nt(qn, kn) * ATT_SCALE
        p = jnp.exp(s - jnp.max(s, axis=-1, keepdims=True))
        p = p / jnp.sum(p, axis=-1, keepdims=True)
        dout = d_ref[...]
        dp = _dot_nt(dout, v_ref[...])
        ds = p * (dp - jnp.sum(p * dp, axis=-1, keepdims=True))
        dv_ref[...] += _dot_tn(p, dout)
        dkn_ref[...] += _dot_tn(ds, qn) * ATT_SCALE
        dq, dgr = _rms_bwd(_dot(ds, kn) * ATT_SCALE, qhat, rq, qg_ref[...])
        dq_ref[...] = dq.astype(BF16)
        gq_ref[...] += jnp.sum(dgr, axis=0, keepdims=True)

        @pl.when(i == steps - 1)
        def _():
            dk, dgk = _rms_bwd(dkn_ref[...], khat, rk, kg_ref[...])
            dk_ref[...] = dk
            gk_ref[...] += jnp.sum(dgk, axis=0, keepdims=True)

    vec = pl.BlockSpec((1, HEAD), lambda h, i: (0, 0))
    memh = pl.BlockSpec((n_mem, HEAD), lambda h, i: (0, h))
    tok = pl.BlockSpec((tm, HEAD), lambda h, i: (i, h))
    mq = pl.BlockSpec((tm, HEAD), lambda h, i: (i, C_MQ // HEAD + h))
    gshape = jax.ShapeDtypeStruct((1, HEAD), F32)
    return pl.pallas_call(
        body, name=name, grid=(MEM_HEADS, steps),
        in_specs=[tok, mq, memh, pl.BlockSpec((n_mem, HEAD), lambda h, i: (0, MEM_HEADS + h)), vec, vec, ANY],
        out_specs=[mq, memh, memh, vec, vec],
        out_shape=[jax.ShapeDtypeStruct(dproj.shape, dproj.dtype),
                   jax.ShapeDtypeStruct((n_mem, MEM_HEADS * HEAD), F32),
                   jax.ShapeDtypeStruct((n_mem, MEM_HEADS * HEAD), F32), gshape, gshape],
        input_output_aliases={6: 0},
        scratch_shapes=[pltpu.VMEM((n_mem, HEAD), F32)],
        compiler_params=_params(("arbitrary", "arbitrary")),
    )(dout, proj, kv, kv, q_gain, k_gain, dproj)


def _branch_fwd(o_hg, o_da, o_mem, proj, wp_hg, wp_da, wp_mem, *, name):
    rows = o_hg.shape[0]
    tm = min(512, rows)

    def body(a_ref, b_ref, c_ref, ga_ref, gb_ref, gc_ref, wa_ref, wb_ref, wc_ref, o_ref):
        merged = _sigmoid(ga_ref[...]) * _dot(a_ref[...], wa_ref[...])
        merged += _sigmoid(gb_ref[...]) * _dot(b_ref[...], wb_ref[...])
        merged += _sigmoid(gc_ref[...]) * _dot(c_ref[...], wc_ref[...])
        o_ref[...] = merged.astype(BF16)

    row = lambda w: pl.BlockSpec((tm, w), lambda i: (i, 0))
    gate = lambda off: pl.BlockSpec((tm, D_MODEL), lambda i: (i, off // D_MODEL))
    full = lambda a: pl.BlockSpec(a.shape, lambda i: (0, 0))
    return pl.pallas_call(
        body, name=name, grid=(rows // tm,),
        in_specs=[row(o_hg.shape[1]), row(o_da.shape[1]), row(o_mem.shape[1]),
                  gate(C_GHG), gate(C_GDA), gate(C_GMEM), full(wp_hg), full(wp_da), full(wp_mem)],
        out_specs=row(D_MODEL), out_shape=jax.ShapeDtypeStruct((rows, D_MODEL), BF16),
        compiler_params=_params(("parallel",)),
    )(o_hg, o_da, o_mem, proj, proj, proj, wp_hg, wp_da, wp_mem)


def _branch_bwd(dm, o_hg, o_da, o_mem, proj, wp_hg, wp_da, wp_mem, *, name):
    rows = o_hg.shape[0]
    tm = min(256, rows)

    def body(dm_ref, a_ref, b_ref, c_ref, ga_ref, gb_ref, gc_ref, wa_ref, wb_ref, wc_ref, dp_ref, *outs):
        dmv = dm_ref[...]
        for j, (o_ref, g_ref, w_ref) in enumerate(((a_ref, ga_ref, wa_ref), (b_ref, gb_ref, wb_ref),
                                                   (c_ref, gc_ref, wc_ref))):
            z = _dot(o_ref[...], w_ref[...])
            gs = _sigmoid(g_ref[...])
            dz = (dmv * gs).astype(BF16)
            dp_ref[:, j * D_MODEL:(j + 1) * D_MODEL] = (dmv * z * gs * (1.0 - gs)).astype(BF16)
            outs[2 * j][...] = dz
            outs[2 * j + 1][...] = _dot_nt(dz, w_ref[...])

    row = lambda w: pl.BlockSpec((tm, w), lambda i: (i, 0))
    gate = lambda off: pl.BlockSpec((tm, D_MODEL), lambda i: (i, off // D_MODEL))
    full = lambda a: pl.BlockSpec(a.shape, lambda i: (0, 0))
    out_specs = [pl.BlockSpec((pl.Element(tm), pl.Element(3 * D_MODEL)), lambda i: (i * tm, C_GHG))]
    out_shape = [jax.ShapeDtypeStruct((rows, IN_COLS), BF16)]
    for o in (o_hg, o_da, o_mem):
        out_specs += [row(D_MODEL), row(o.shape[1])]
        out_shape += [jax.ShapeDtypeStruct((rows, D_MODEL), BF16), jax.ShapeDtypeStruct((rows, o.shape[1]), F32)]
    return pl.pallas_call(
        body, name=name, grid=(rows // tm,),
        in_specs=[row(D_MODEL), row(o_hg.shape[1]), row(o_da.shape[1]), row(o_mem.shape[1]),
                  gate(C_GHG), gate(C_GDA), gate(C_GMEM), full(wp_hg), full(wp_da), full(wp_mem)],
        out_specs=out_specs, out_shape=out_shape,
        compiler_params=_params(("parallel",)),
    )(dm, o_hg, o_da, o_mem, proj, proj, proj, wp_hg, wp_da, wp_mem)


def _ffn_in(h2, w_ab, *, name):
    rows, dff = h2.shape[0], w_ab.shape[1] // 2
    tm, tn = min(2048, rows), 256

    def body(h_ref, wa_ref, wb_ref, a_ref, b_ref, u_ref):
        a = _dot(h_ref[...], wa_ref[...])
        b = _dot(h_ref[...], wb_ref[...])
        a_ref[...] = a.astype(BF16)
        b_ref[...] = b.astype(BF16)
        u_ref[...] = (a * _sigmoid(a) * b).astype(BF16)

    out = pl.BlockSpec((tm, tn), lambda i, j: (i, j))
    return pl.pallas_call(
        body, name=name, grid=(rows // tm, dff // tn),
        in_specs=[pl.BlockSpec((tm, D_MODEL), lambda i, j: (i, 0)),
                  pl.BlockSpec((D_MODEL, tn), lambda i, j: (0, j)),
                  pl.BlockSpec((D_MODEL, tn), lambda i, j: (0, dff // tn + j))],
        out_specs=[out, out, out],
        out_shape=[jax.ShapeDtypeStruct((rows, dff), BF16)] * 3,
        compiler_params=_params(("parallel", "parallel")),
    )(h2, w_ab, w_ab)


def _ffn_act_bwd(dy, w_out, a, b, *, name):
    rows, dff = a.shape
    tm, tn = min(1024, rows), 256

    def body(dy_ref, w_ref, a_ref, b_ref, da_ref, db_ref):
        du = _dot_nt(dy_ref[...], w_ref[...])
        av, bv = a_ref[...].astype(F32), b_ref[...].astype(F32)
        sa = _sigmoid(av)
        da_ref[...] = (du * bv * sa * (1.0 + av * (1.0 - sa))).astype(BF16)
        db_ref[...] = (du * av * sa).astype(BF16)

    tile = pl.BlockSpec((tm, tn), lambda i, j: (i, j))
    return pl.pallas_call(
        body, name=name, grid=(rows // tm, dff // tn),
        in_specs=[pl.BlockSpec((tm, D_MODEL), lambda i, j: (i, 0)),
                  pl.BlockSpec((tn, D_MODEL), lambda i, j: (j, 0)), tile, tile],
        out_specs=[tile, tile],
        out_shape=[jax.ShapeDtypeStruct((rows, dff), BF16), jax.ShapeDtypeStruct((rows, dff), BF16)],
        compiler_params=_params(("parallel", "parallel")),
    )(dy, w_out, a, b)


def _lower_bound(lb_fw, lb_bw, *, name):
    def body(a_ref, b_ref, oa_ref, ob_ref):
        for src, dst in ((a_ref, oa_ref), (b_ref, ob_ref)):
            dst[...] = _sigmoid(src[0:1, :] - src[1:2, :])

    shape = jax.ShapeDtypeStruct((1, lb_fw.shape[1]), F32)
    return pl.pallas_call(body, name=name, out_shape=[shape, shape])(lb_fw, lb_bw)


def _local_step(x, mem, tgt, p, ex):
    rows = x.shape[0]
    lb_fw, lb_bw = _lower_bound(p["lb_logits_fw"], p["lb_logits_bw"], name="lower_bound")

    h = _rms_fwd(x, p["norm_mix_gain"], name="norm_mix")
    w = {"w_in": ex.w_in}
    proj, carried = _matmul([(h, w["w_in"])], "nn", BF16, tm=1024, tn=1024, carry=ex.late_carry(), name="proj_in")
    w.update(ex.late_weights(carried))
    o_fw, st_fw = _gla_fwd(proj, lb_fw, f_off=C_FF, rev=False, name="gla_fwd_fw")
    o_bw, st_bw = _gla_fwd(proj, lb_bw, f_off=C_FB, rev=True, name="gla_fwd_bw")
    o_hg = _hg_out_fwd(o_fw, o_bw, proj, p["hg_norm_gain"], name="hg_out")

    qkv_r, outs, lses = [], [], []
    for g in range(len(DA_GROUPS)):
        qr, kr, vr = _da_prep(proj, p["da_q_gain"], p["da_k_gain"], g, name=f"da_prep{g}")
        og, lg = _band_fwd(qr, kr, vr, g, name=f"band_fwd{g}")
        qkv_r.append((qr, kr, vr))
        outs.append(og)
        lses.append(lg)
    o_da, lse_da = _da_merge(outs, lses, rows, name="da_merge")

    mem_n = _rms_fwd(mem, p["norm_mem_gain"], name="norm_mem")
    kv = _matmul([(mem_n, w["w_mem_kv"])], "nn", F32, tm=256, tn=512, name="mem_kv")
    o_mem = _mem_fwd(proj, kv, p["mem_q_gain"], p["mem_k_gain"], name="mem_attn")

    merged = _branch_fwd(o_hg, o_da, o_mem, proj, w["w_proj_hg"], w["w_proj_da"], w["w_proj_mem"],
                         name="branch_merge")
    x1, h2 = _out_norm(merged, w["w_out"], x, p["norm_ffn_gain"], name="mix_out_norm")
    a, b, u = _ffn_in(h2, w["w_ffn_in"], name="ffn_in")
    dy, dy_b, loss = _out_loss(u, w["w_ffn_out"], x1, tgt, name="ffn_out_loss")

    gw, gwb, gs = {}, {}, {}
    gw["w_ffn_out"], gwb["w_ffn_out"] = _matmul([(u, dy_b)], "tn", F32, tm=256, tn=1024, also=BF16, name="g_ffn_out")
    da, db = _ffn_act_bwd(dy_b, w["w_ffn_out"], a, b, name="ffn_act_bwd")
    gw["w_ffn_a"], gwb["w_ffn_a"] = _matmul([(h2, da)], "tn", F32, tm=512, tn=256, also=BF16, name="g_ffn_a")
    gw["w_ffn_b"], gwb["w_ffn_b"] = _matmul([(h2, db)], "tn", F32, tm=512, tn=256, also=BF16, name="g_ffn_b")
    dh2 = _matmul([(da, w["w_ffn_in"], 0), (db, w["w_ffn_in"], 1)], "nt", F32, tm=256, tn=1024, name="d_h2")
    dx1, dx1_b, gs["norm_ffn_gain"] = _rms_bwd_rows(dh2, x1, p["norm_ffn_gain"], dy, (F32, BF16), name="norm_ffn_bwd")
    gw["w_out"], gwb["w_out"] = _matmul([(merged, dx1_b)], "tn", F32, tm=512, tn=512, also=BF16, name="g_out")
    dmerged = _matmul([(dx1_b, w["w_out"])], "nt", F32, tm=512, tn=512, name="d_merged")
    dproj, dz_hg, do_hg, dz_da, do_da, dz_mem, do_mem = _branch_bwd(
        dmerged, o_hg, o_da, o_mem, proj, w["w_proj_hg"], w["w_proj_da"], w["w_proj_mem"], name="branch_bwd")
    gw["w_proj_hg"], gwb["w_proj_hg"] = _matmul([(o_hg, dz_hg)], "tn", F32, tm=512, tn=512, also=BF16, name="g_proj_hg")
    gw["w_proj_da"], gwb["w_proj_da"] = _matmul([(o_da, dz_da)], "tn", F32, tm=512, tn=512, also=BF16, name="g_proj_da")
    gw["w_proj_mem"], gwb["w_proj_mem"] = _matmul([(o_mem, dz_mem)], "tn", F32, tm=512, tn=512, also=BF16, name="g_proj_mem")

    dproj, dk_mem, dv_mem, gs["mem_q_gain"], gs["mem_k_gain"] = _mem_bwd(
        do_mem, proj, kv, p["mem_q_gain"], p["mem_k_gain"], dproj, name="mem_attn_bwd")
    dkv = jnp.concatenate([dk_mem, dv_mem], axis=1).astype(BF16)
    gw["w_mem_kv"], gwb["w_mem_kv"] = _matmul([(mem_n, dkv)], "tn", F32, tm=512, tn=512, also=BF16, name="g_mem_kv")
    dmem_n = _matmul([(dkv, w["w_mem_kv"])], "nt", F32, tm=256, tn=512, name="d_mem_n")
    (gs["norm_mem_gain"],) = _rms_bwd_rows(dmem_n, mem, p["norm_mem_gain"], None, (), name="norm_mem_bwd")

    prep = _da_bwd_prep(do_da, o_da, lse_da, name="da_bwd_prep")
    d_da, gq_parts, gk_parts = [], [], []
    for g in range(len(DA_GROUPS)):
        qr, kr, vr = qkv_r[g]
        dor, lser, deltar = prep[3 * g:3 * g + 3]
        dqr, dkr, dvr = _band_bwd(qr, kr, vr, dor, lser, deltar, g, name=f"band_bwd{g}")
        dq, dk, dv, gq, gk = _da_prep_bwd(dqr, dkr, dvr, proj, p["da_q_gain"], p["da_k_gain"], g,
                                          name=f"da_prep_bwd{g}")
        d_da.append((dq, dk, dv))
        gq_parts.append(gq)
        gk_parts.append(gk)
    for j, (off, tag) in enumerate(((C_DQ, "q"), (C_DK, "k"), (C_DV, "v"))):
        dproj = _fill_columns(dproj, [t[j] for t in d_da], off, name=f"dproj_fill_da_{tag}")

    dproj, do_gla, gs["hg_norm_gain"] = _hg_out_bwd(do_hg, o_fw, o_bw, proj, p["hg_norm_gain"], dproj,
                                                    name="hg_out_bwd")
    carry_bulk, carry_rest = ex.early_carries(gw, gwb)
    (dq_f, dproj, dv_f, dlb_fw), bulk = _gla_bwd(proj, lb_fw, do_gla, st_fw, None, dproj, f_off=C_FF, rev=False,
                                                 carry=carry_bulk, name="gla_bwd_fw")
    (dproj, dfl_bw, dv_hg, dlb_bw), rest = _gla_bwd(proj, lb_bw, do_gla, st_bw, (dq_f, dv_f), dproj, f_off=C_FB,
                                                    rev=True, carry=carry_rest, name="gla_bwd_bw")
    ex.early_done(bulk, rest)
    dproj = _fill_columns(dproj, [dfl_bw, dv_hg], C_FB, name="dproj_fill")
    g_in, g_in_b = _matmul([(h, dproj)], "tn", F32, tm=512, tn=1024, also=BF16, name="g_in")
    dh, carried = _matmul([(dproj, w["w_in"])], "nt", F32, tm=1024, tn=1024, tk=IN_COLS // 8,
                          carry=ex.w_in_carry(g_in, g_in_b), name="d_h")
    ex.w_in_done(carried)
    grad_x, gs["norm_mix_gain"] = _rms_bwd_rows(dh, x, p["norm_mix_gain"], dx1, (F32,), name="norm_mix_bwd")

    small = _small_pack(gs, gq_parts, gk_parts, dlb_fw, dlb_bw, lb_fw, lb_bw, loss, name="small_pack")
    return grad_x, small


def _small_pack(gs, gq_parts, gk_parts, dlb_fw, dlb_bw, lb_fw, lb_bw, loss, *, name):
    def body(g_mix, g_mem, g_ffn, dfw, dbw, lfw, lbw, g_hg, q0, q1, q2, k0, k1, k2, g_mq, g_mk, loss_ref, o_ref):
        o_ref[0:1, :] = g_mix[...]
        o_ref[1:2, :] = g_mem[...]
        o_ref[2:3, :] = g_ffn[...]
        for base, d_ref, l_ref in ((3, dfw, lfw), (5, dbw, lbw)):
            lbv = l_ref[...]
            t = d_ref[...] * lbv * (1.0 - lbv)
            o_ref[base:base + 1, :] = t
            o_ref[base + 1:base + 2, :] = -t
        o_ref[7:8, :] = jnp.zeros((1, D_MODEL), F32)
        o_ref[7:8, 0:HEAD] = g_hg[...]
        o_ref[7:8, HEAD:2 * HEAD] = q0[...] + q1[...] + q2[...]
        o_ref[7:8, 2 * HEAD:3 * HEAD] = k0[...] + k1[...] + k2[...]
        o_ref[7:8, 3 * HEAD:4 * HEAD] = g_mq[...]
        o_ref[7:8, 4 * HEAD:5 * HEAD] = g_mk[...]
        o_ref[7:8, LOSS_COL:LOSS_COL + HEAD] = loss_ref[...]

    return pl.pallas_call(body, name=name, out_shape=jax.ShapeDtypeStruct((8, D_MODEL), F32))(
        gs["norm_mix_gain"], gs["norm_mem_gain"], gs["norm_ffn_gain"], dlb_fw, dlb_bw, lb_fw, lb_bw,
        gs["hg_norm_gain"], *gq_parts, *gk_parts, gs["mem_q_gain"], gs["mem_k_gain"], loss)


def _row_tile(rows, cols, n_arrays):
    budget = (16 * 1024 * 1024) // (2 * 4 * cols * n_arrays)
    tr = rows
    while tr > budget and tr % 2 == 0 and (tr // 2) % 16 == 0:
        tr //= 2
    return tr


def _cast_into_full(a, chip, rows, cols, axis, *, name):
    sr, sc = a.shape
    tr = _row_tile(sr, sc, 2)

    def body(chip_ref, a_ref, o_ref):
        del chip_ref
        o_ref[...] = a_ref[...].astype(BF16)

    if axis == 1:
        out_map = lambda i, chip_ref: (i, chip_ref[0])
    else:
        out_map = lambda i, chip_ref: (chip_ref[0] * (sr // tr) + i, 0)
    return pl.pallas_call(
        body, name=name,
        grid_spec=pltpu.PrefetchScalarGridSpec(
            num_scalar_prefetch=1, grid=(sr // tr,),
            in_specs=[pl.BlockSpec((tr, sc), lambda i, chip_ref: (i, 0))],
            out_specs=pl.BlockSpec((tr, sc), out_map)),
        out_shape=jax.ShapeDtypeStruct((rows, cols), BF16),
        compiler_params=_params(("parallel",)))(chip, a)


def _add_halves(items, *, name):
    rows = items[0][3].shape[0]
    widths = [ra.shape[1] for _, _, _, ra in items]
    tr = _row_tile(rows, sum(widths), 4)

    def body(*refs):
        o_ref = refs[-1]
        first = lax.axis_index("c") == 0
        off = 0
        for j, wd in enumerate(widths):
            h0, h1, ra = refs[3 * j:3 * j + 3]
            o_ref[:, off:off + wd] = (jnp.where(first, h0[...], h1[...]) + ra[...]).astype(BF16)
            off += wd

    in_specs, ins = [], []
    for (g, haxis, hsize, ra), wd in zip(items, widths):
        if haxis == 0:
            in_specs += [pl.BlockSpec((tr, wd), lambda i: (i, 0)),
                         pl.BlockSpec((tr, wd), lambda i, o=hsize // tr: (o + i, 0))]
        else:
            in_specs += [pl.BlockSpec((tr, wd), lambda i: (i, 0)), pl.BlockSpec((tr, wd), lambda i: (i, 1))]
        in_specs.append(pl.BlockSpec((tr, wd), lambda i: (i, 0)))
        ins += [g, g, ra]
    return pl.pallas_call(body, name=name, grid=(rows // tr,), in_specs=in_specs,
                          out_specs=pl.BlockSpec((tr, sum(widths)), lambda i: (i, 0)),
                          out_shape=jax.ShapeDtypeStruct((rows, sum(widths)), BF16),
                          compiler_params=_params(("parallel",)))(*ins)


def _add_slots(rb, *, name):
    _, rows, cols = rb.shape
    tr = _row_tile(rows, cols, 5)

    def body(r0, r1, r2, r3, o_ref):
        o_ref[...] = ((r0[...].astype(F32) + r1[...].astype(F32)) + r2[...].astype(F32)) + r3[...].astype(F32)

    slot = lambda s: pl.BlockSpec((None, tr, cols), lambda i: (s, i, 0))
    return pl.pallas_call(body, name=name, grid=(rows // tr,), in_specs=[slot(s) for s in range(4)],
                          out_specs=pl.BlockSpec((tr, cols), lambda i: (i, 0)),
                          out_shape=jax.ShapeDtypeStruct((rows, cols), F32),
                          compiler_params=_params(("parallel",)))(rb, rb, rb, rb)


def _adamw(w, g, m, v, *, name):
    rows, cols = w.shape
    tr = _row_tile(rows, cols, 8) if rows % 16 == 0 else rows
    c1 = 1.0 - ADAM_B1 ** ADAM_STEP
    c2 = 1.0 - ADAM_B2 ** ADAM_STEP

    def body(w_ref, g_ref, m_ref, v_ref, d_ref, mo_ref, vo_ref, go_ref):
        gv = g_ref[...]
        go_ref[...] = gv
        mn = ADAM_B1 * m_ref[...] + (1.0 - ADAM_B1) * gv
        vn = ADAM_B2 * v_ref[...] + (1.0 - ADAM_B2) * (gv * gv)
        mo_ref[...] = mn
        vo_ref[...] = vn
        d_ref[...] = -ADAM_LR * ((mn / c1) / (jnp.sqrt(vn / c2) + ADAM_EPS) + ADAM_WD * w_ref[...])

    spec = pl.BlockSpec((tr, cols), lambda i: (i, 0))
    shape = jax.ShapeDtypeStruct((rows, cols), F32)
    return pl.pallas_call(body, name=name, grid=(rows // tr,), in_specs=[spec] * 4, out_specs=[spec] * 4,
                        out_shape=[shape] * 4, compiler_params=_params(("parallel",)))(w, g, m, v)


W_SPECS = (
    ("w_in", 1024, IN_COLS, 1, IN_COLS // 4),
    ("w_mem_kv", 1024, 1024, 0, 256),
    ("w_proj_hg", 1024, 1024, 0, 256),
    ("w_proj_da", 512, 1024, 1, 256),
    ("w_proj_mem", 512, 1024, 1, 256),
    ("w_out", 1024, 1024, 0, 256),
    ("w_ffn_in", 1024, 2 * D_FF, 1, 2 * D_FF // 4),
    ("w_ffn_out", D_FF, 1024, 0, D_FF // 4),
)
CHIP_FLIPS = ((1, 0), (0, 1), (1, 1))
ANY = pl.BlockSpec(memory_space=pl.ANY)
DMA_CHUNK_BYTES = 1 << 20
STAGE_BYTES = 2 << 20


def _place():
    x, y, c = lax.axis_index("x"), lax.axis_index("y"), lax.axis_index("c")
    return x, y, c, 2 * x + y


def _flip(v, f):
    return 1 - v if f else v


def _slab(ref, axis, idx, size):
    start = pl.multiple_of(idx * size, size)
    return ref.at[pl.ds(start, size), :] if axis == 0 else ref.at[:, pl.ds(start, size)]


def _chunked(make, src, dst, want="both"):
    rows, cols = src.shape
    row_bytes = cols * jnp.dtype(src.dtype).itemsize
    k = 1
    while rows % (2 * k) == 0 and (rows // (2 * k)) % 16 == 0 and (rows // k) * row_bytes > DMA_CHUNK_BYTES:
        k *= 2
    cr = rows // k
    parts = []
    if want != "wait":
        parts = [make(src.at[pl.ds(j * cr, cr), :], dst.at[pl.ds(j * cr, cr), :]) for j in range(k)]
    return parts, (make(src, dst) if want != "start" else None)


def _half_spec(rows, cols, axis):
    return (0, rows // 2) if axis == 1 else (1, cols // 2)


def _staged(src, remote_dst, local_dst, sibling, load_sems, send_sems, store_sems, recv_sem):
    rows, cols = src.shape
    row_bytes = cols * jnp.dtype(src.dtype).itemsize
    k = 1
    while (rows // k) * row_bytes > STAGE_BYTES and rows % (2 * k) == 0 and (rows // (2 * k)) % 16 == 0:
        k *= 2
    cr = rows // k
    piece = lambda ref, j: ref.at[pl.ds(j * cr, cr), :]

    def run(buf):
        loads = [pltpu.make_async_copy(piece(src, j), buf.at[j % 2], load_sems.at[j % 2]) for j in range(k)]
        outs = [[pltpu.make_async_remote_copy(src_ref=buf.at[j % 2], dst_ref=piece(remote_dst, j),
                                              send_sem=send_sems.at[j % 2], recv_sem=recv_sem,
                                              device_id=sibling, device_id_type=MESH)] for j in range(k)]
        if local_dst is not None:
            for j in range(k):
                outs[j].append(pltpu.make_async_copy(buf.at[j % 2], piece(local_dst, j), store_sems.at[j % 2]))

        def drained(j):
            outs[j][0].wait_send()
            for cp in outs[j][1:]:
                cp.wait()

        loads[0].start()
        for j in range(k):
            loads[j].wait()
            for cp in outs[j]:
                cp.start()
            if j + 1 < k:
                if j >= 1:
                    drained(j - 1)
                loads[j + 1].start()
        for j in range(max(0, k - 2), k):
            drained(j)

    pl.run_scoped(run, pltpu.VMEM((2, cr, cols), src.dtype))


def _landed(ref, recv_sem, send_sem):
    pltpu.make_async_remote_copy(src_ref=ref, dst_ref=ref, send_sem=send_sem, recv_sem=recv_sem,
                                 device_id=(lax.axis_index("x"), lax.axis_index("y"), lax.axis_index("c")),
                                 device_id_type=MESH).wait_recv()


def _half_slab(ref, spec, chip, half):
    _, rows, cols, axis, size = spec
    haxis, hsize = _half_spec(rows, cols, axis)
    return _slab(_slab(ref, axis, chip, size), haxis, half, hsize)


def _gather_sends(outs, specs, ici_send, ici_recv, want="both"):
    x, y, c, p = _place()
    sent = []
    for wi, spec in enumerate(specs):
        mine = _half_slab(outs[wi], spec, p, c)
        for k, (fx, fy) in enumerate(CHIP_FLIPS):
            sent.append(_chunked(lambda s, d, j=3 * wi + k, fx=fx, fy=fy: pltpu.make_async_remote_copy(
                src_ref=s, dst_ref=d, send_sem=ici_send.at[j], recv_sem=ici_recv.at[j],
                device_id=(_flip(x, fx), _flip(y, fy), c), device_id_type=MESH), mine, mine, want))
    return sent


def _gather_weights(fulls, specs, *, name):
    n = len(specs)

    def body(*refs):
        outs = refs[n:2 * n]
        ici_send, ici_recv, load_sems, d2d_send, d2d_recv = refs[2 * n:]
        x, y, c, _ = _place()
        sent = _gather_sends(outs, specs, ici_send, ici_recv)
        for parts, _ in sent:
            for cp in parts:
                cp.start()
        for k, (fx, fy) in enumerate(CHIP_FLIPS):
            q = 2 * _flip(x, fx) + _flip(y, fy)
            for wi in range(n):
                sent[3 * wi + k][1].wait_recv()
                got = _half_slab(outs[wi], specs[wi], q, c)
                _staged(got, got, None, (x, y, 1 - c), load_sems, d2d_send, None, d2d_recv.at[3 * wi + k])
        for _, whole in sent:
            whole.wait_send()
        for k, (fx, fy) in enumerate(CHIP_FLIPS):
            q = 2 * _flip(x, fx) + _flip(y, fy)
            for wi in range(n):
                _landed(_half_slab(outs[wi], specs[wi], q, 1 - c), d2d_recv.at[3 * wi + k], d2d_send.at[0])

    return pl.pallas_call(
        body, name=name, in_specs=[ANY] * n, out_specs=[ANY] * n,
        out_shape=[jax.ShapeDtypeStruct(f.shape, f.dtype) for f in fulls],
        input_output_aliases={i: i for i in range(n)},
        scratch_shapes=[pltpu.SemaphoreType.DMA((3 * n,)), pltpu.SemaphoreType.DMA((3 * n,)),
                        pltpu.SemaphoreType.DMA((2,)), pltpu.SemaphoreType.DMA((2,)),
                        pltpu.SemaphoreType.DMA((3 * n,))],
    )(*fulls)


def _gather_chips_carry(fulls, specs):
    n = len(specs)

    def issue(ins, outs, sems, want):
        del ins
        return [(parts, whole, True) for parts, whole in _gather_sends(outs, specs, sems[0], sems[1], want)]

    return _Carry(list(fulls), [jax.ShapeDtypeStruct(f.shape, f.dtype) for f in fulls], {i: i for i in range(n)},
                  [pltpu.SemaphoreType.DMA((3 * n,)), pltpu.SemaphoreType.DMA((3 * n,))], issue)


def _gather_pass_on(fulls, specs, *, name):
    n = len(specs)

    def body(*refs):
        outs = refs[n:2 * n]
        load_sems, d2d_send, d2d_recv = refs[2 * n:2 * n + 3]
        bufs = refs[2 * n + 3:]
        x, y, c, _ = _place()
        loads, sends = [], []
        for k, (fx, fy) in enumerate(CHIP_FLIPS):
            q = 2 * _flip(x, fx) + _flip(y, fy)
            for wi in range(n):
                j = 3 * wi + k
                got = _half_slab(outs[wi], specs[wi], q, c)
                loads.append(pltpu.make_async_copy(got, bufs[j], load_sems.at[j]))
                sends.append(pltpu.make_async_remote_copy(
                    src_ref=bufs[j], dst_ref=got, send_sem=d2d_send.at[j], recv_sem=d2d_recv.at[j],
                    device_id=(x, y, 1 - c), device_id_type=MESH))
        for cp in loads:
            cp.start()
        for load, send in zip(loads, sends):
            load.wait()
            send.start()
        for cp in sends:
            cp.wait_send()
        for k, (fx, fy) in enumerate(CHIP_FLIPS):
            q = 2 * _flip(x, fx) + _flip(y, fy)
            for wi in range(n):
                _landed(_half_slab(outs[wi], specs[wi], q, 1 - c), d2d_recv.at[3 * wi + k], d2d_send.at[0])

    shapes = []
    for _, rows, cols, axis, size in specs:
        shapes += [(rows // 2, size) if axis == 1 else (size, cols // 2)] * 3
    assert sum(math.prod(s) for s in shapes) * 2 <= VMEM_LIMIT_V7X // 2
    stages = [pltpu.VMEM(s, BF16) for s in shapes]
    return pl.pallas_call(
        body, name=name, in_specs=[ANY] * n, out_specs=[ANY] * n,
        out_shape=[jax.ShapeDtypeStruct(f.shape, f.dtype) for f in fulls],
        input_output_aliases={i: i for i in range(n)},
        scratch_shapes=[pltpu.SemaphoreType.DMA((3 * n,))] * 3 + stages,
        compiler_params=pltpu.CompilerParams(vmem_limit_bytes=VMEM_LIMIT_V7X),
    )(*fulls)


def _sibling_exchange(grads, *, name):
    n = len(grads)

    def body(*refs):
        ins, outs = refs[:n], refs[n:2 * n]
        load_sems, send_sems, recv_sems = refs[2 * n:]
        x, y, c, _ = _place()
        for i, (_, haxis, hsize) in enumerate(grads):
            _staged(_slab(ins[i], haxis, 1 - c, hsize), outs[i], None, (x, y, 1 - c),
                    load_sems, send_sems, None, recv_sems.at[i])
        for i in range(n):
            _landed(outs[i], recv_sems.at[i], send_sems.at[0])

    shapes = [jax.ShapeDtypeStruct((hsize, g.shape[1]) if haxis == 0 else (g.shape[0], hsize), g.dtype)
              for g, haxis, hsize in grads]
    return pl.pallas_call(
        body, name=name, in_specs=[ANY] * n, out_specs=[ANY] * n, out_shape=shapes,
        scratch_shapes=[pltpu.SemaphoreType.DMA((2,)), pltpu.SemaphoreType.DMA((2,)),
                        pltpu.SemaphoreType.DMA((n,))],
    )(*[g for g, _, _ in grads])


def _chip_exchange_carry(parts):
    n = len(parts)

    def issue(ins, outs, sems, want):
        send_sems, recv_sems, local_sems = sems
        x, y, c, p = _place()
        copies = []
        for i, (_, axis, size) in enumerate(parts):
            copies.append(_chunked(lambda s, d, i=i: pltpu.make_async_copy(s, d, local_sems.at[i]),
                                   _slab(ins[i], axis, p, size), outs[i].at[p], want) + (False,))
            for k, (fx, fy) in enumerate(CHIP_FLIPS):
                px, py = _flip(x, fx), _flip(y, fy)
                copies.append(_chunked(lambda s, d, j=3 * i + k, px=px, py=py: pltpu.make_async_remote_copy(
                    src_ref=s, dst_ref=d, send_sem=send_sems.at[j], recv_sem=recv_sems.at[j],
                    device_id=(px, py, c), device_id_type=MESH),
                    _slab(ins[i], axis, 2 * px + py, size), outs[i].at[p], want) + (True,))
        return copies

    shapes = []
    for a, axis, size in parts:
        shapes.append(jax.ShapeDtypeStruct((4, size, a.shape[1]) if axis == 0 else (4, a.shape[0], size), a.dtype))
    return _Carry([a for a, _, _ in parts], shapes, {},
                  [pltpu.SemaphoreType.DMA((3 * n,)), pltpu.SemaphoreType.DMA((3 * n,)),
                   pltpu.SemaphoreType.DMA((n,))], issue)


def _sibling_share(sums):
    n = len(sums)

    def body(*refs):
        ins, outs = refs[:n], refs[n:2 * n]
        load_sems, send_sems, store_sems, recv_sems = refs[2 * n:]
        x, y, c, _ = _place()
        for i, (s, haxis) in enumerate(sums):
            place = _slab(outs[i], haxis, c, s.shape[haxis])
            _staged(ins[i], place, place, (x, y, 1 - c), load_sems, send_sems, store_sems, recv_sems.at[i])
        for i, (s, haxis) in enumerate(sums):
            _landed(_slab(outs[i], haxis, 1 - c, s.shape[haxis]), recv_sems.at[i], send_sems.at[0])

    shapes = []
    for s, haxis in sums:
        r, cc = s.shape
        shapes.append(jax.ShapeDtypeStruct((2 * r, cc) if haxis == 0 else (r, 2 * cc), F32))
    return pl.pallas_call(
        body, name="grad_sibling_share", in_specs=[ANY] * n, out_specs=[ANY] * n, out_shape=shapes,
        scratch_shapes=[pltpu.SemaphoreType.DMA((2,)), pltpu.SemaphoreType.DMA((2,)),
                        pltpu.SemaphoreType.DMA((2,)), pltpu.SemaphoreType.DMA((n,))],
    )(*[s for s, _ in sums])


class _Exchanges:
    def __init__(self, fulls):
        self.w_in = _gather_weights([fulls["w_in"]], W_SPECS[:1], name="gather_w_in")[0]
        self.late_specs = W_SPECS[1:]
        self.late = [fulls[s[0]] for s in self.late_specs]
        self.slots = {}

    def late_carry(self):
        return _gather_chips_carry(self.late, self.late_specs)

    def late_weights(self, carried):
        done = _gather_pass_on(carried, self.late_specs, name="gather_pass_on")
        return dict(zip([s[0] for s in self.late_specs], done))

    def _half_sums(self, gw, gwb, specs, tag):
        grads, sent = [], []
        for name, _, _, axis, _ in specs:
            for part in (("w_ffn_a", "w_ffn_b") if name == "w_ffn_in" else (name,)):
                half = _half_spec(gw[part].shape[0], gw[part].shape[1], axis)
                grads.append((gw[part],) + half)
                sent.append((gwb[part],) + half)
        theirs = _sibling_exchange(sent, name=f"grad_sibling_exchange_{tag}")
        parts, j = [], 0
        for name, _, _, axis, size in specs:
            take = 2 if name == "w_ffn_in" else 1
            items = [grads[i] + (theirs[i],) for i in range(j, j + take)]
            parts.append((_add_halves(items, name=f"half_sum_{name}"), axis, size))
            j += take
        return parts

    def early_carries(self, gw, gwb):
        parts = self._half_sums(gw, gwb, self.late_specs, "early")
        self.early_names = [s[0] for s in self.late_specs]
        cut = self.early_names.index("w_ffn_in")
        return _chip_exchange_carry(parts[cut:]), _chip_exchange_carry(parts[:cut])

    def early_done(self, carried_bulk, carried_rest):
        cut = self.early_names.index("w_ffn_in")
        self.slots.update(zip(self.early_names[cut:], carried_bulk))
        self.slots.update(zip(self.early_names[:cut], carried_rest))

    def w_in_carry(self, g, gb):
        return _chip_exchange_carry(self._half_sums({"w_in": g}, {"w_in": gb}, W_SPECS[:1], "w_in"))

    def w_in_done(self, carried):
        self.slots["w_in"] = carried[0]

    def reduced(self):
        sums = []
        for name, rows, cols, axis, _ in W_SPECS:
            sums.append((_add_slots(self.slots[name], name=f"chip_sum_{name}"), _half_spec(rows, cols, axis)[0]))
        return dict(zip([s[0] for s in W_SPECS], _sibling_share(sums)))


def _small_allreduce(sv):
    def body(sv_ref, o_ref, slots_ref, send_sems, recv_sems):
        x, y, c, _ = _place()
        me = 4 * x + 2 * y + c
        slots_ref[me] = sv_ref[...]
        copies = []
        for k in range(1, 8):
            fx, fy, fc = (k >> 2) & 1, (k >> 1) & 1, k & 1
            copies.append(pltpu.make_async_remote_copy(
                src_ref=sv_ref, dst_ref=slots_ref.at[me], send_sem=send_sems.at[k - 1],
                recv_sem=recv_sems.at[k - 1], device_id=(_flip(x, fx), _flip(y, fy), _flip(c, fc)),
                device_id_type=MESH))
        for cp in copies:
            cp.start()
        for cp in copies:
            cp.wait_recv()
        for cp in copies:
            cp.wait_send()
        total = slots_ref[0]
        for s in range(1, 8):
            total = total + slots_ref[s]
        o_ref[...] = total

    vm = pl.BlockSpec(memory_space=pltpu.VMEM)
    return pl.pallas_call(
        body, name="small_allreduce", in_specs=[vm], out_specs=vm,
        out_shape=jax.ShapeDtypeStruct(sv.shape, F32),
        scratch_shapes=[pltpu.VMEM((8,) + sv.shape, F32), pltpu.SemaphoreType.DMA((7,)),
                        pltpu.SemaphoreType.DMA((7,))],
    )(sv)


SMALL_ROWS = (("norm_mix_gain", 0), ("norm_mem_gain", 1), ("norm_ffn_gain", 2))
SMALL_LB = (("lb_logits_fw", 3), ("lb_logits_bw", 5))
SMALL_HEAD = ("hg_norm_gain", "da_q_gain", "da_k_gain", "mem_q_gain", "mem_k_gain")
LOSS_COL = HEAD * len(SMALL_HEAD)


def _pack_small(d):
    last = jnp.concatenate([d[n] for n in SMALL_HEAD] + [jnp.zeros((1, D_MODEL - HEAD * len(SMALL_HEAD)), F32)], axis=1)
    return jnp.concatenate([d["norm_mix_gain"], d["norm_mem_gain"], d["norm_ffn_gain"],
                            d["lb_logits_fw"], d["lb_logits_bw"], last], axis=0)


def _unpack_small(a):
    out = {n: a[r:r + 1] for n, r in SMALL_ROWS}
    out.update({n: a[r:r + 2] for n, r in SMALL_LB})
    out.update({n: a[7:8, j * HEAD:(j + 1) * HEAD] for j, n in enumerate(SMALL_HEAD)})
    return out


PARAM_ORDER = ("norm_mix_gain", "norm_mem_gain", "w_in", "lb_logits_fw", "lb_logits_bw", "hg_norm_gain",
               "da_q_gain", "da_k_gain", "w_mem_kv", "mem_q_gain", "mem_k_gain", "w_proj_hg", "w_proj_da",
               "w_proj_mem", "w_out", "norm_ffn_gain", "w_ffn_in", "w_ffn_out")


def kernel(x, mem, norm_mix_gain, norm_mem_gain, w_in, lb_logits_fw, lb_logits_bw, hg_norm_gain, da_q_gain, da_k_gain, w_mem_kv, mem_q_gain, mem_k_gain, w_proj_hg, w_proj_da, w_proj_mem, w_out, norm_ffn_gain, w_ffn_in, w_ffn_out, loss_target, m_norm_mix_gain, m_norm_mem_gain, m_w_in, m_lb_logits_fw, m_lb_logits_bw, m_hg_norm_gain, m_da_q_gain, m_da_k_gain, m_w_mem_kv, m_mem_q_gain, m_mem_k_gain, m_w_proj_hg, m_w_proj_da, m_w_proj_mem, m_w_out, m_norm_ffn_gain, m_w_ffn_in, m_w_ffn_out, v_norm_mix_gain, v_norm_mem_gain, v_w_in, v_lb_logits_fw, v_lb_logits_bw, v_hg_norm_gain, v_da_q_gain, v_da_k_gain, v_w_mem_kv, v_mem_q_gain, v_mem_k_gain, v_w_proj_hg, v_w_proj_da, v_w_proj_mem, v_w_out, v_norm_ffn_gain, v_w_ffn_in, v_w_ffn_out):
    args = dict(locals())
    mats = tuple(s[0] for s in W_SPECS)
    flat = lambda a: a.reshape(a.shape[-2:])
    w = {n: flat(args[n]) for n in mats}
    m = {n: flat(args["m_" + n]) for n in mats}
    v = {n: flat(args["v_" + n]) for n in mats}
    small = {n: args[n] for n in PARAM_ORDER if n not in mats}

    chip = (2 * lax.axis_index("x") + lax.axis_index("y")).astype(jnp.int32).reshape(1)
    ex = _Exchanges({n: _cast_into_full(w[n], chip, rows, cols, axis, name=f"cast_{n}")
                     for n, rows, cols, axis, _ in W_SPECS})
    grad_x, small_grads = _local_step(x[0], mem[0], loss_target[0], small, ex)
    grads = ex.reduced()
    small_sum = _small_allreduce(small_grads)

    delta, new_m, new_v = {}, {}, {}
    for n in mats:
        delta[n], new_m[n], new_v[n], grads[n] = _adamw(w[n], grads[n], m[n], v[n], name=f"adamw_{n}")
    packed = _adamw(_pack_small(small), small_sum,
                    _pack_small({n: args["m_" + n] for n in small}),
                    _pack_small({n: args["v_" + n] for n in small}), name="adamw_small")
    for dst, src in zip((delta, new_m, new_v, grads), packed):
        dst.update(_unpack_small(src))

    def shaped(d, n):
        return d[n].reshape(args[n].shape)

    return (small_sum[7, LOSS_COL], grad_x[None], *[shaped(grads, n) for n in PARAM_ORDER], *[shaped(delta, n) for n in PARAM_ORDER],
            *[shaped(new_m, n) for n in PARAM_ORDER], *[shaped(new_v, n) for n in PARAM_ORDER])
```

```python
import math

import numpy as np
import jax
import jax.numpy as jnp
from jax import lax
from jax.experimental import pallas as pl
from jax.experimental.pallas import tpu as pltpu

F32 = jnp.float32
BF16 = jnp.bfloat16
MESH = pl.DeviceIdType.MESH

D_MODEL = 1024
HEAD = 128
HG_HEADS = 8
DA_GROUPS = ((1, 64), (4, 64), (16, 64))
DA_GH = 4
MEM_HEADS = 4
N_MEM = 256
D_FF = 2816
CHUNK = 64
BAND_QBLOCKS = 8
GLA_HEADS_PER_STEP = 8
RMS_EPS = 1e-6
NEG_INF = -1e30
HG_SCALE = HEAD ** -0.5
ATT_SCALE = HEAD ** -0.5
VMEM_LIMIT_V7X = 48 * 1024 * 1024

C_HQ, C_FF, C_FB, C_HI, C_HG = 0, 1024, 2048, 3072, 4096
C_DQ, C_DK, C_DV, C_MQ = 5120, 6656, 8192, 9728
C_GHG, C_GDA, C_GMEM = 10240, 11264, 12288
IN_COLS = 13312

ADAM_LR, ADAM_B1, ADAM_B2, ADAM_EPS, ADAM_WD, ADAM_STEP = 0.001, 0.9, 0.999, 1e-08, 0.01, 10


def _params(sem, vmem=VMEM_LIMIT_V7X):
    return pltpu.CompilerParams(dimension_semantics=sem, vmem_limit_bytes=vmem)


def _dot(a, b):
    return jnp.dot(a.astype(BF16), b.astype(BF16), preferred_element_type=F32)


def _dot_nt(a, b):
    return lax.dot_general(a.astype(BF16), b.astype(BF16), (((1,), (1,)), ((), ())),
                           preferred_element_type=F32)


def _dot_tn(a, b):
    return lax.dot_general(a.astype(BF16), b.astype(BF16), (((0,), (0,)), ((), ())),
                           preferred_element_type=F32)


def _sigmoid(v):
    return jax.nn.sigmoid(v.astype(F32))


def _ones(rows, cols):
    return (lax.broadcasted_iota(jnp.int32, (rows, cols), 0) >= 0).astype(BF16)


def _lane_sum(v):
    ones = _ones(HEAD, HEAD)
    hi = v.astype(BF16)
    mid = (v - hi.astype(F32)).astype(BF16)
    return jnp.dot(hi, ones, preferred_element_type=F32) + jnp.dot(mid, ones, preferred_element_type=F32)


def _row_mean(v):
    if v.shape[-1] == HEAD:
        return _lane_sum(v) * (1.0 / HEAD)
    return jnp.mean(v, axis=-1, keepdims=True)


def _rms(v):
    v = v.astype(F32)
    r = lax.rsqrt(_row_mean(v * v) + RMS_EPS)
    return v * r, r


def _rms_bwd(dy, xhat, r, gain):
    dxh = dy * gain
    dx = r * (dxh - xhat * _row_mean(dxh * xhat))
    return dx, dy * xhat


class _Carry:
    def __init__(self, arrays, out_shapes, aliases, sems, issue):
        self.arrays, self.out_shapes, self.aliases, self.sems, self.issue = arrays, out_shapes, aliases, sems, issue


NO_CARRY = object()


def _start_copies(copies):
    for parts, _, _ in copies:
        for cp in parts:
            cp.start()


def _wait_copies(copies):
    for _, whole, remote in copies:
        if remote:
            whole.wait_recv()
    for _, whole, remote in copies:
        if remote:
            whole.wait_send()
        else:
            whole.wait()


def _call(body, *, name, grid, in_specs, out_specs, out_shape, scratch_shapes, semantics, ins, carry=None,
          aliases=None):
    aliases = dict(aliases or {})
    if carry is None:
        res = pl.pallas_call(body, name=name, grid=grid, in_specs=in_specs, out_specs=out_specs, out_shape=out_shape,
                             scratch_shapes=scratch_shapes, input_output_aliases=aliases,
                             compiler_params=_params(semantics))(*ins)
        return list(res), []
    n_in, n_out, n_scr = len(ins), len(out_shape), len(scratch_shapes)
    c_in, c_out = len(carry.arrays), len(carry.out_shapes)

    def wrapped(*refs):
        pos = [0]

        def take(count):
            pos[0] += count
            return refs[pos[0] - count:pos[0]]

        own_in, carry_in = take(n_in), take(c_in)
        own_out, carry_out = take(n_out), take(c_out)
        own_scr, carry_sems = take(n_scr), take(len(carry.sems))
        ids = [pl.program_id(a) for a in range(len(grid))]
        first, last = ids[0] == 0, ids[0] == grid[0] - 1
        for a in range(1, len(grid)):
            first, last = first & (ids[a] == 0), last & (ids[a] == grid[a] - 1)

        @pl.when(first)
        def _():
            _start_copies(carry.issue(carry_in, carry_out, carry_sems, "start"))

        body(*own_in, *own_out, *own_scr)

        @pl.when(last)
        def _():
            _wait_copies(carry.issue(carry_in, carry_out, carry_sems, "wait"))

    res = pl.pallas_call(
        wrapped, name=name, grid=grid, in_specs=list(in_specs) + [ANY] * c_in,
        out_specs=list(out_specs) + [ANY] * c_out, out_shape=list(out_shape) + list(carry.out_shapes),
        input_output_aliases={**aliases, **{n_in + i: n_out + j for i, j in carry.aliases.items()}},
        scratch_shapes=list(scratch_shapes) + list(carry.sems),
        compiler_params=_params(("arbitrary",) * len(grid)))(*ins, *carry.arrays)
    return list(res[:n_out]), list(res[n_out:])


def _matmul(pairs, mode, out_dtype, *, tm, tn, tk=None, residual=None, also=None, carry=NO_CARRY, name):
    b_offs = [pr[2] if len(pr) > 2 else 0 for pr in pairs]
    pairs = [pr[:2] for pr in pairs]
    a0, b0 = pairs[0]
    if mode == "nn":
        (m, kk), n = a0.shape, b0.shape[1]
    elif mode == "nt":
        (m, kk), n = a0.shape, b0.shape[0]
    else:
        (kk, m), n = a0.shape, b0.shape[1]
    assert mode == "nt" or not any(b_offs)
    tm, tn = min(tm, m), min(tn, n)
    tk = kk if tk is None else tk
    nk = kk // tk
    assert m % tm == 0 and n % tn == 0 and kk % tk == 0, (name, m, n, kk)
    n_p = len(pairs)
    if mode == "tn":
        a_spec = pl.BlockSpec((tk, tm), lambda i, j, k: (k, i))
    else:
        a_spec = pl.BlockSpec((tm, tk), lambda i, j, k: (i, k))
    if mode == "nt":
        b_specs = [pl.BlockSpec((tn, tk), lambda i, j, k, o=o: (j, o * nk + k)) for o in b_offs]
    else:
        b_specs = [pl.BlockSpec((tk, tn), lambda i, j, k: (k, j))] * n_p
    o_spec = pl.BlockSpec((tm, tn), lambda i, j, k: (i, j))
    dot = {"nn": _dot, "nt": _dot_nt, "tn": _dot_tn}[mode]
    has_res = residual is not None

    def body(*refs):
        a_refs, b_refs = refs[:n_p], refs[n_p:2 * n_p]
        pos = 2 * n_p
        res_ref = refs[pos] if has_res else None
        pos += int(has_res)
        o_ref = refs[pos]
        pos += int(also is not None)
        part = dot(a_refs[0][...], b_refs[0][...])
        for a_r, b_r in zip(a_refs[1:], b_refs[1:]):
            part += dot(a_r[...], b_r[...])

        def finish(total):
            if has_res:
                total = total + res_ref[...]
            o_ref[...] = total.astype(out_dtype)
            if also is not None:
                refs[pos][...] = total.astype(also)

        if nk == 1:
            finish(part)
        else:
            acc_ref = refs[pos + 1]
            k = pl.program_id(2)

            @pl.when(k == 0)
            def _():
                acc_ref[...] = part

            @pl.when(k > 0)
            def _():
                acc_ref[...] += part

            @pl.when(k == nk - 1)
            def _():
                finish(acc_ref[...])

    ins = [a for a, _ in pairs] + [b for _, b in pairs]
    in_specs = [a_spec] * n_p + b_specs
    if has_res:
        ins.append(residual)
        in_specs.append(o_spec)
    dtypes = [out_dtype] + ([also] if also is not None else [])
    outs, carried = _call(
        body, name=name, grid=(m // tm, n // tn, nk), in_specs=in_specs, out_specs=[o_spec] * len(dtypes),
        out_shape=[jax.ShapeDtypeStruct((m, n), dt) for dt in dtypes],
        scratch_shapes=[pltpu.VMEM((tm, tn), F32)] if nk > 1 else [],
        semantics=("parallel", "parallel", "arbitrary"), ins=ins, carry=None if carry is NO_CARRY else carry)
    out = outs[0] if also is None else tuple(outs)
    return out if carry is NO_CARRY else (out, carried)


def _rms_fwd(x, gain, *, name):
    rows, dm = x.shape
    tm = min(1024, rows)

    def body(x_ref, g_ref, h_ref):
        xhat, _ = _rms(x_ref[...])
        h_ref[...] = (xhat * g_ref[...]).astype(BF16)

    return pl.pallas_call(
        body, name=name, grid=(rows // tm,),
        in_specs=[pl.BlockSpec((tm, dm), lambda i: (i, 0)), pl.BlockSpec((1, dm), lambda i: (0, 0))],
        out_specs=pl.BlockSpec((tm, dm), lambda i: (i, 0)),
        out_shape=jax.ShapeDtypeStruct((rows, dm), BF16),
        compiler_params=_params(("parallel",)),
    )(x, gain)


def _rms_bwd_rows(dh, x, gain, dres, dx_dtypes, *, name):
    rows, dm = x.shape
    tm = min(512, rows)
    has_res = dres is not None

    def body(*refs):
        dh_ref, x_ref, g_ref = refs[:3]
        res_ref = refs[3] if has_res else None
        outs = refs[3 + int(has_res):]
        dg_ref = outs[-1]
        xhat, r = _rms(x_ref[...])
        dx, dgr = _rms_bwd(dh_ref[...], xhat, r, g_ref[...])
        if has_res:
            dx = dx + res_ref[...]
        for dx_ref, dt in zip(outs, dx_dtypes):
            dx_ref[...] = dx.astype(dt)

        @pl.when(pl.program_id(0) == 0)
        def _():
            dg_ref[...] = jnp.zeros_like(dg_ref)

        dg_ref[...] += jnp.sum(dgr, axis=0, keepdims=True)

    row = pl.BlockSpec((tm, dm), lambda i: (i, 0))
    vec = pl.BlockSpec((1, dm), lambda i: (0, 0))
    return pl.pallas_call(
        body, name=name, grid=(rows // tm,),
        in_specs=[row, row, vec] + ([row] if has_res else []),
        out_specs=[row] * len(dx_dtypes) + [vec],
        out_shape=[jax.ShapeDtypeStruct((rows, dm), dt) for dt in dx_dtypes] + [jax.ShapeDtypeStruct((1, dm), F32)],
        compiler_params=_params(("arbitrary",)),
    )(*([dh, x, gain] + ([dres] if has_res else [])))


def _out_norm(a, w, x, gain, *, name):
    rows, dm = x.shape
    tm = min(512, rows)

    def body(a_ref, w_ref, x_ref, g_ref, x1_ref, h_ref):
        x1 = x_ref[...] + _dot(a_ref[...], w_ref[...])
        x1_ref[...] = x1
        xhat, _ = _rms(x1)
        h_ref[...] = (xhat * g_ref[...]).astype(BF16)

    row = lambda wd: pl.BlockSpec((tm, wd), lambda i: (i, 0))
    return pl.pallas_call(
        body, name=name, grid=(rows // tm,),
        in_specs=[row(a.shape[1]), pl.BlockSpec(w.shape, lambda i: (0, 0)), row(dm), pl.BlockSpec((1, dm), lambda i: (0, 0))],
        out_specs=[row(dm), row(dm)],
        out_shape=[jax.ShapeDtypeStruct((rows, dm), F32), jax.ShapeDtypeStruct((rows, dm), BF16)],
        compiler_params=_params(("parallel",)),
    )(a, w, x, gain)


def _out_loss(u, w, x1, tgt, *, name):
    rows, dm = x1.shape
    tm = min(256, rows)
    steps = rows // tm

    def body(u_ref, w_ref, x_ref, t_ref, dy_ref, dyb_ref, loss_ref, acc_ref):
        i = pl.program_id(0)
        diff = x_ref[...] + _dot(u_ref[...], w_ref[...]) - t_ref[...]
        dy = diff * (1.0 / dm)
        dy_ref[...] = dy
        dyb_ref[...] = dy.astype(BF16)

        @pl.when(i == 0)
        def _():
            acc_ref[...] = jnp.zeros_like(acc_ref)

        acc_ref[...] += jnp.sum(diff * diff, axis=0, keepdims=True)

        @pl.when(i == steps - 1)
        def _():
            loss_ref[...] = jnp.full((1, HEAD), 0.5 / dm, F32) * jnp.sum(acc_ref[...])

    row = lambda wd: pl.BlockSpec((tm, wd), lambda i: (i, 0))
    return pl.pallas_call(
        body, name=name, grid=(steps,),
        in_specs=[row(u.shape[1]), pl.BlockSpec(w.shape, lambda i: (0, 0)), row(dm), row(dm)],
        out_specs=[row(dm), row(dm), pl.BlockSpec((1, HEAD), lambda i: (0, 0))],
        out_shape=[jax.ShapeDtypeStruct((rows, dm), F32), jax.ShapeDtypeStruct((rows, dm), BF16),
                   jax.ShapeDtypeStruct((1, HEAD), F32)],
        scratch_shapes=[pltpu.VMEM((1, dm), F32)],
        compiler_params=_params(("arbitrary",)),
    )(u, w, x1, tgt)


def _gla_block_terms(q_raw, f_logit, lb, rev):
    sig = _sigmoid(f_logit)
    forget = lb + (1.0 - lb) * sig
    k = 1.0 - forget
    b = _chunk_cumsum(jnp.log(forget), rev)
    qs = _sigmoid(q_raw)
    eb = jnp.exp(b)
    emb = jnp.exp(-b)
    qt = (q_raw * qs * HG_SCALE) * eb
    kt = k * emb
    return sig, forget, k, b, qs, eb, emb, qt, kt


def _chunk_cumsum(v, rev):
    n = v.shape[0]
    pos = lax.broadcasted_iota(jnp.int32, v.shape, 0) & (CHUNK - 1)
    step = 1
    while step < CHUNK:
        if rev:
            shifted, keep = pltpu.roll(v, n - step, 0), pos < CHUNK - step
        else:
            shifted, keep = pltpu.roll(v, step, 0), pos >= step
        v = v + jnp.where(keep, shifted, 0.0)
        step *= 2
    return v


def _tri_mask(n, rev):
    row, col = np.arange(n)[:, None], np.arange(n)[None, :]
    same = (row // CHUNK) == (col // CHUNK)
    return jnp.asarray((same & ((row <= col) if rev else (row >= col))).astype(np.float32))


def _chunk_order(ncb, rev):
    order = range(ncb - 1, -1, -1) if rev else range(ncb)
    return [(c, c * CHUNK if rev else c * CHUNK + CHUNK - 1) for c in order]


def _gla_fwd(proj, lb, *, f_off, rev, name):
    rows = proj.shape[0]
    tb = min(256, rows)
    nb, ncb = rows // tb, tb // CHUNK

    def tmap(n):
        return nb - 1 - n if rev else n

    def body(tri_ref, q_ref, f_ref, v_ref, lb_ref, o_ref, st_ref, s_ref):
        @pl.when(pl.program_id(1) == 0)
        def _():
            s_ref[...] = jnp.zeros_like(s_ref)

        tri = tri_ref[...] > 0.5
        for hh in range(GLA_HEADS_PER_STEP):
            cs = slice(hh * HEAD, (hh + 1) * HEAD)
            v = v_ref[:, cs]
            _, _, k, b, _, _, _, qt, kt = _gla_block_terms(q_ref[:, cs].astype(F32), f_ref[:, cs].astype(F32),
                                                           lb_ref[:, cs], rev)
            qt = qt.astype(BF16)
            o_intra = _dot(jnp.where(tri, _dot_nt(qt, kt), 0.0), v)
            chunks = []
            for c, last in _chunk_order(ncb, rev):
                sl = slice(c * CHUNK, (c + 1) * CHUNK)
                bl = b[last:last + 1, :]
                kh = k[sl] * jnp.exp(bl - b[sl])
                chunks.append((c, sl, jnp.exp(bl), _dot_tn(v[sl], kh)))
            s_t = s_ref[hh]
            for c, sl, ebl, kv in chunks:
                st_ref[hh, c] = s_t
                o_ref[sl, cs] = o_intra[sl] + _dot_nt(qt[sl], s_t)
                s_t = ebl * s_t + kv
            s_ref[hh] = s_t

    hps = GLA_HEADS_PER_STEP
    col = lambda off: pl.BlockSpec((tb, hps * HEAD), lambda h, n: (tmap(n), off // (hps * HEAD) + h))
    return pl.pallas_call(
        body, name=name, grid=(HG_HEADS // hps, nb),
        in_specs=[pl.BlockSpec((tb, tb), lambda h, n: (0, 0)), col(C_HQ), col(f_off), col(C_HI),
                  pl.BlockSpec((1, hps * HEAD), lambda h, n: (0, h))],
        out_specs=[pl.BlockSpec((tb, hps * HEAD), lambda h, n: (tmap(n), h)),
                   pl.BlockSpec((hps, ncb, HEAD, HEAD), lambda h, n: (h, tmap(n), 0, 0))],
        out_shape=[jax.ShapeDtypeStruct((rows, HG_HEADS * HEAD), F32),
                   jax.ShapeDtypeStruct((HG_HEADS, rows // CHUNK, HEAD, HEAD), F32)],
        scratch_shapes=[pltpu.VMEM((hps, HEAD, HEAD), F32)],
        compiler_params=_params(("parallel", "arbitrary")),
    )(_tri_mask(tb, rev), proj, proj, proj, lb)


def _gla_bwd(proj, lb, do, states, prev, dproj, *, f_off, rev, carry=NO_CARRY, name):
    rows = proj.shape[0]
    tb = min(256, rows)
    nb, ncb = rows // tb, tb // CHUNK
    has_prev = prev is not None
    qv_dtype = BF16 if has_prev else F32

    def tmap(n):
        return n if rev else nb - 1 - n

    def body(*refs):
        tri_ref, q_ref, f_ref, v_ref, lb_ref, do_ref, st_ref = refs[:7]
        pq_ref, pv_ref = refs[8:10] if has_prev else (None, None)
        (dq_ref, df_ref, dv_ref, dlb_ref,
         ds_ref, dqt_scr, dk_scr, db_scr, dbl_scr, dv_scr) = refs[8 + 2 * int(has_prev):]

        @pl.when(pl.program_id(1) == 0)
        def _():
            ds_ref[...] = jnp.zeros_like(ds_ref)
            dlb_ref[...] = jnp.zeros_like(dlb_ref)

        tri = tri_ref[...] > 0.5
        for hh in range(GLA_HEADS_PER_STEP):
            cs = slice(hh * HEAD, (hh + 1) * HEAD)
            lbv = lb_ref[:, cs]
            q_raw = q_ref[:, cs].astype(F32)
            v, dout = v_ref[:, cs], do_ref[:, cs].astype(BF16)
            sig, forget, k, b, qs, eb, emb, qt, kt = _gla_block_terms(q_raw, f_ref[:, cs].astype(F32), lbv, rev)
            qt_b, kt_b = qt.astype(BF16), kt.astype(BF16)
            a = jnp.where(tri, _dot_nt(qt_b, kt_b), 0.0)
            da = jnp.where(tri, _dot_nt(dout, v), 0.0).astype(BF16)
            dv_intra = _dot_tn(a, dout)
            dqt_intra = _dot(da, kt_b)
            dkt = _dot_tn(da, qt_b)
            chunks = []
            for c, last in reversed(_chunk_order(ncb, rev)):
                sl = slice(c * CHUNK, (c + 1) * CHUNK)
                bl = b[last:last + 1, :]
                e = jnp.exp(bl - b[sl])
                s_t = st_ref[hh, c]
                dqt_scr[sl, cs] = dqt_intra[sl] + _dot(dout[sl], s_t)
                chunks.append((sl, jnp.exp(bl), e, k[sl] * e, s_t, _dot_tn(dout[sl], qt_b[sl])))
            ds_t = ds_ref[hh]
            for sl, ebl, e, kh, s_t, grow in chunks:
                dkh = _dot(v[sl], ds_t)
                dv_scr[sl, cs] = dv_intra[sl] + _dot_nt(kh, ds_t)
                dk_scr[sl, cs] = dkt[sl] * emb[sl] + dkh * e
                khd = kh * dkh
                dbl = jnp.sum(khd, axis=0, keepdims=True) + ebl * jnp.sum(ds_t * s_t, axis=0, keepdims=True)
                db_scr[sl, cs] = khd
                dbl_scr[sl, cs] = jnp.broadcast_to(dbl, (CHUNK, HEAD))
                ds_t = grow + ds_t * ebl
            ds_ref[hh] = ds_t
            dqt = dqt_scr[:, cs]
            dlogf = _chunk_cumsum(qt * dqt - kt * dkt - db_scr[:, cs], not rev) + dbl_scr[:, cs]
            dforget = dlogf / forget - dk_scr[:, cs]
            df_ref[:, cs] = (dforget * (1.0 - lbv) * sig * (1.0 - sig)).astype(BF16)
            dlb_ref[:, cs] += jnp.sum(dforget * (1.0 - sig), axis=0, keepdims=True)
            dqr = dqt * eb * (HG_SCALE * qs * (1.0 + q_raw * (1.0 - qs)))
            dv = dv_scr[:, cs]
            if has_prev:
                dqr = dqr + pq_ref[:, cs]
                dv = dv + pv_ref[:, cs]
            dq_ref[:, cs] = dqr.astype(qv_dtype)
            dv_ref[:, cs] = dv.astype(qv_dtype)

    hps = GLA_HEADS_PER_STEP
    col = lambda off: pl.BlockSpec((tb, hps * HEAD), lambda h, n: (tmap(n), off // (hps * HEAD) + h))
    blk = pl.BlockSpec((tb, hps * HEAD), lambda h, n: (tmap(n), h))
    vec = pl.BlockSpec((1, hps * HEAD), lambda h, n: (0, h))
    wide = HG_HEADS * HEAD
    into = jax.ShapeDtypeStruct(dproj.shape, dproj.dtype)
    outs, carried = _call(
        body, name=name, grid=(HG_HEADS // hps, nb),
        in_specs=[pl.BlockSpec((tb, tb), lambda h, n: (0, 0)), col(C_HQ), col(f_off), col(C_HI), vec, blk,
                  pl.BlockSpec((hps, ncb, HEAD, HEAD), lambda h, n: (h, tmap(n), 0, 0)), ANY]
                 + ([blk, blk] if has_prev else []),
        out_specs=[col(C_HQ) if has_prev else blk, blk if has_prev else col(f_off), blk, vec],
        out_shape=[into if has_prev else jax.ShapeDtypeStruct((rows, wide), qv_dtype),
                   jax.ShapeDtypeStruct((rows, wide), BF16) if has_prev else into,
                   jax.ShapeDtypeStruct((rows, wide), qv_dtype), jax.ShapeDtypeStruct((1, wide), F32)],
        aliases={7: 0 if has_prev else 1},
        scratch_shapes=[pltpu.VMEM((hps, HEAD, HEAD), F32)] + [pltpu.VMEM((tb, hps * HEAD), F32)] * 5,
        semantics=("parallel", "arbitrary"),
        ins=[_tri_mask(tb, rev), proj, proj, proj, lb, do, states, dproj] + (list(prev) if has_prev else []),
        carry=None if carry is NO_CARRY else carry)
    return outs if carry is NO_CARRY else (outs, carried)


def _fill_columns(dproj, parts, col, *, name):
    rows, wd = parts[0].shape
    tm = min(1024, rows)
    n = len(parts)

    def body(*refs):
        o_ref = refs[n + 1]
        for j in range(n):
            o_ref[:, j * wd:(j + 1) * wd] = refs[j][...]

    return pl.pallas_call(
        body, name=name, grid=(rows // tm,),
        in_specs=[pl.BlockSpec((tm, wd), lambda i: (i, 0))] * n + [ANY],
        out_specs=pl.BlockSpec((pl.Element(tm), pl.Element(n * wd)), lambda i: (i * tm, col)),
        out_shape=jax.ShapeDtypeStruct(dproj.shape, dproj.dtype), input_output_aliases={n: 0},
        compiler_params=_params(("parallel",)),
    )(*parts, dproj)


def _hg_out_fwd(o_fw, o_bw, proj, gain, *, name):
    rows = o_fw.shape[0]
    tm = min(1024, rows)
    wide = HG_HEADS * HEAD

    def body(a_ref, b_ref, g_ref, gain_ref, o_ref):
        for h in range(HG_HEADS):
            sl = slice(h * HEAD, (h + 1) * HEAD)
            xhat, _ = _rms(a_ref[:, sl] + b_ref[:, sl])
            gate = g_ref[:, sl].astype(F32)
            o_ref[:, sl] = (xhat * gain_ref[...] * (gate * _sigmoid(gate))).astype(BF16)

    row = pl.BlockSpec((tm, wide), lambda i: (i, 0))
    return pl.pallas_call(
        body, name=name, grid=(rows // tm,),
        in_specs=[row, row, pl.BlockSpec((tm, wide), lambda i: (i, C_HG // wide)),
                  pl.BlockSpec((1, HEAD), lambda i: (0, 0))],
        out_specs=row, out_shape=jax.ShapeDtypeStruct((rows, wide), BF16),
        compiler_params=_params(("parallel",)),
    )(o_fw, o_bw, proj, gain)


def _hg_out_bwd(dout, o_fw, o_bw, proj, gain, dproj, *, name):
    rows = o_fw.shape[0]
    tm = min(512, rows)
    wide = HG_HEADS * HEAD

    def body(d_ref, a_ref, b_ref, g_ref, gain_ref, dp_in, dgate_ref, do_ref, dgain_ref):
        del dp_in

        @pl.when(pl.program_id(0) == 0)
        def _():
            dgain_ref[...] = jnp.zeros_like(dgain_ref)

        dgain = jnp.zeros((1, HEAD), F32)
        for h in range(HG_HEADS):
            sl = slice(h * HEAD, (h + 1) * HEAD)
            xhat, r = _rms(a_ref[:, sl] + b_ref[:, sl])
            gate, dy = g_ref[:, sl].astype(F32), d_ref[:, sl]
            gs = _sigmoid(gate)
            dgate_ref[:, sl] = (dy * xhat * gain_ref[...] * (gs * (1.0 + gate * (1.0 - gs)))).astype(BF16)
            dx, dgr = _rms_bwd(dy * (gate * gs), xhat, r, gain_ref[...])
            do_ref[:, sl] = dx
            dgain = dgain + jnp.sum(dgr, axis=0, keepdims=True)
        dgain_ref[...] += dgain

    row = pl.BlockSpec((tm, wide), lambda i: (i, 0))
    gate = pl.BlockSpec((tm, wide), lambda i: (i, C_HG // wide))
    vec = pl.BlockSpec((1, HEAD), lambda i: (0, 0))
    return pl.pallas_call(
        body, name=name, grid=(rows // tm,),
        in_specs=[row, row, row, gate, vec, ANY],
        out_specs=[gate, row, vec],
        out_shape=[jax.ShapeDtypeStruct(dproj.shape, dproj.dtype), jax.ShapeDtypeStruct((rows, wide), F32),
                   jax.ShapeDtypeStruct((1, HEAD), F32)],
        input_output_aliases={5: 0},
        compiler_params=_params(("arbitrary",)),
    )(dout, o_fw, o_bw, proj, gain, dproj)


def _strided_rows(ref, r, count, d):
    return ref[...] if d == 1 else ref[pl.ds(r, count, stride=d), :]


def _store_strided(ref, r, count, d, val):
    if d == 1:
        ref[...] = val
    else:
        ref[pl.ds(r, count, stride=d), :] = val


def _da_prep(proj, q_gain, k_gain, g, *, name):
    d = DA_GROUPS[g][0]
    rows = proj.shape[0]
    rb = min(1024, rows)
    tn = rb // d

    def body(q_ref, k_ref, v_ref, qg_ref, kg_ref, qo_ref, ko_ref, vo_ref, qf_ref, kf_ref, vf_ref):
        for h in range(DA_GH):
            cs = slice(h * HEAD, (h + 1) * HEAD)
            for src, dst in ((q_ref, qf_ref), (k_ref, kf_ref), (v_ref, vf_ref)):
                dst[...] = src[:, cs].astype(F32)
            for r in range(d):
                qhat, _ = _rms(_strided_rows(qf_ref, r, tn, d))
                khat, _ = _rms(_strided_rows(kf_ref, r, tn, d))
                qo_ref[h, r] = (qhat * qg_ref[...]).astype(BF16)
                ko_ref[h, r] = (khat * kg_ref[...]).astype(BF16)
                vo_ref[h, r] = _strided_rows(vf_ref, r, tn, d).astype(BF16)

    wide = DA_GH * HEAD
    col = lambda off: pl.BlockSpec((rb, wide), lambda i: (i, off // wide + g))
    vec = pl.BlockSpec((1, HEAD), lambda i: (0, 0))
    out = pl.BlockSpec((DA_GH, d, tn, HEAD), lambda i: (0, 0, i, 0))
    shape = jax.ShapeDtypeStruct((DA_GH, d, rows // d, HEAD), BF16)
    return pl.pallas_call(
        body, name=name, grid=(rows // rb,),
        in_specs=[col(C_DQ), col(C_DK), col(C_DV), vec, vec],
        out_specs=[out, out, out], out_shape=[shape, shape, shape],
        scratch_shapes=[pltpu.VMEM((rb, HEAD), F32)] * 3,
        compiler_params=_params(("parallel",)),
    )(proj, proj, proj, q_gain, k_gain)


def _slopes(g):
    idx = np.arange(g * DA_GH + 1, (g + 1) * DA_GH + 1)
    s = (2.0 ** (-8.0 * idx / (DA_GH * len(DA_GROUPS)))).astype(np.float32)
    return jnp.asarray(np.broadcast_to(s[:, None, None], (DA_GH, 8, HEAD)).copy())


def _band_window(ld, t, radius):
    win = min(2 * t, ld)
    assert t // 2 >= radius or win == ld
    return win


def _band_scores(q, k, q0, start, slope, d, radius):
    t, win = q.shape[0], k.shape[0]
    row = lax.broadcasted_iota(jnp.int32, (t, win), 0)
    col = lax.broadcasted_iota(jnp.int32, (t, win), 1)
    rel = jnp.abs((start - q0) + col - row)
    return _dot_nt(q, k) * ATT_SCALE + jnp.where(rel <= radius, -slope * (d * rel).astype(F32), NEG_INF)


def _band_fwd(qr, kr, vr, g, *, name):
    d, radius = DA_GROUPS[g]
    _, _, ld, _ = qr.shape
    t = min(HEAD, ld)
    qb = min(BAND_QBLOCKS, ld // t)
    rps = max(1, min(d, BAND_QBLOCKS // qb))
    win = _band_window(ld, t, radius)

    def body(q_ref, k_ref, v_ref, sl_ref, o_ref, lse_ref):
        i = pl.program_id(2)
        slope = sl_ref[0:1, 0:1]
        for rr in range(rps):
            for j in range(qb):
                sl = slice(j * t, (j + 1) * t)
                q0 = (i * qb + j) * t
                start = pl.multiple_of(jnp.clip(q0 - t // 2, 0, ld - win), t // 2)
                s = _band_scores(q_ref[rr, sl, :], k_ref[rr, pl.ds(start, win), :], q0, start, slope, d, radius)
                m = jnp.max(s, axis=-1, keepdims=True)
                p = jnp.exp(s - m).astype(BF16)
                l = jnp.dot(p, _ones(win, HEAD), preferred_element_type=F32)
                o_ref[rr, sl, :] = _dot(p, v_ref[rr, pl.ds(start, win), :]) / l
                lse_ref[rr, sl, :] = m + jnp.log(l)

    own = pl.BlockSpec((None, rps, qb * t, HEAD), lambda h, r, i: (h, r, i, 0))
    seq = pl.BlockSpec((None, rps, ld, HEAD), lambda h, r, i: (h, r, 0, 0))
    shape = jax.ShapeDtypeStruct(qr.shape, F32)
    return pl.pallas_call(
        body, name=name, grid=(DA_GH, d // rps, ld // (qb * t)),
        in_specs=[own, seq, seq, pl.BlockSpec((None, 8, HEAD), lambda h, r, i: (h, 0, 0))],
        out_specs=[own, own], out_shape=[shape, shape],
        compiler_params=_params(("parallel", "parallel", "parallel")),
    )(qr, kr, vr, _slopes(g))


def _band_bwd(qr, kr, vr, dor, lser, deltar, g, *, name):
    d, radius = DA_GROUPS[g]
    _, _, ld, _ = qr.shape
    t = min(HEAD, ld)
    qb = min(BAND_QBLOCKS, ld // t)
    rps = max(1, min(d, BAND_QBLOCKS // qb))
    win = _band_window(ld, t, radius)

    def body(q_ref, k_ref, v_ref, do_ref, lse_ref, dl_ref, sl_ref, dq_ref, dk_ref, dv_ref):
        i = pl.program_id(2)

        @pl.when(i == 0)
        def _():
            dk_ref[...] = jnp.zeros_like(dk_ref)
            dv_ref[...] = jnp.zeros_like(dv_ref)

        slope = sl_ref[0:1, 0:1]
        for rr in range(rps):
            for j in range(qb):
                sl = slice(j * t, (j + 1) * t)
                q0 = (i * qb + j) * t
                start = pl.multiple_of(jnp.clip(q0 - t // 2, 0, ld - win), t // 2)
                rows = pl.ds(start, win)
                q, dout, k, v = q_ref[rr, sl, :], do_ref[rr, sl, :], k_ref[rr, rows, :], v_ref[rr, rows, :]
                p = jnp.exp(_band_scores(q, k, q0, start, slope, d, radius) - lse_ref[rr, sl, 0:1])
                ds = p * (_dot_nt(dout, v) - dl_ref[rr, sl, 0:1])
                dq_ref[rr, sl, :] = _dot(ds, k) * ATT_SCALE
                dk_ref[rr, rows, :] += _dot_tn(ds, q) * ATT_SCALE
                dv_ref[rr, rows, :] += _dot_tn(p, dout)

    own = pl.BlockSpec((None, rps, qb * t, HEAD), lambda h, r, i: (h, r, i, 0))
    seq = pl.BlockSpec((None, rps, ld, HEAD), lambda h, r, i: (h, r, 0, 0))
    shape = jax.ShapeDtypeStruct(qr.shape, F32)
    return pl.pallas_call(
        body, name=name, grid=(DA_GH, d // rps, ld // (qb * t)),
        in_specs=[own, seq, seq, own, own, own, pl.BlockSpec((None, 8, HEAD), lambda h, r, i: (h, 0, 0))],
        out_specs=[own, seq, seq], out_shape=[shape, shape, shape],
        compiler_params=_params(("parallel", "parallel", "arbitrary")),
    )(qr, kr, vr, dor, lser, deltar, _slopes(g))


def _da_merge(outs, lses, rows, *, name):
    rb = min(1024, rows)
    wide = DA_GH * HEAD

    def body(*refs):
        o_refs, l_refs = refs[0:3], refs[3:6]
        o_ref, lse_ref = refs[6:8]
        on_refs, ln_refs = refs[8:11], refs[11:14]
        for h in range(DA_GH):
            cs = slice(h * HEAD, (h + 1) * HEAD)
            for g, (d, _) in enumerate(DA_GROUPS):
                tn = rb // d
                for r in range(d):
                    _store_strided(on_refs[g], r, tn, d, o_refs[g][h, r])
                    _store_strided(ln_refs[g], r, tn, d, l_refs[g][h, r])
            l0, l1, l2 = ln_refs[0][...], ln_refs[1][...], ln_refs[2][...]
            m = jnp.maximum(jnp.maximum(l0, l1), l2)
            e0, e1, e2 = jnp.exp(l0 - m), jnp.exp(l1 - m), jnp.exp(l2 - m)
            tot = e0 + e1 + e2
            o_ref[:, cs] = (e0 * on_refs[0][...] + e1 * on_refs[1][...] + e2 * on_refs[2][...]) / tot
            lse_ref[:, cs] = m + jnp.log(tot)

    res = lambda d: pl.BlockSpec((DA_GH, d, rb // d, HEAD), lambda i: (0, 0, i, 0))
    nat = pl.BlockSpec((rb, wide), lambda i: (i, 0))
    shape = jax.ShapeDtypeStruct((rows, wide), F32)
    return pl.pallas_call(
        body, name=name, grid=(rows // rb,),
        in_specs=[res(d) for d, _ in DA_GROUPS] * 2,
        out_specs=[nat, nat], out_shape=[shape, shape],
        scratch_shapes=[pltpu.VMEM((rb, HEAD), F32)] * 6,
        compiler_params=_params(("parallel",)),
    )(*outs, *lses)


def _da_bwd_prep(dout, o, lse, *, name):
    rows = o.shape[0]
    rb = min(512, rows)
    wide = DA_GH * HEAD

    def body(d_ref, o_ref, l_ref, *outs):
        d_scr, l_scr, delta_scr = outs[9:12]
        for h in range(DA_GH):
            cs = slice(h * HEAD, (h + 1) * HEAD)
            dv = d_ref[:, cs]
            d_scr[...] = dv
            l_scr[...] = l_ref[:, cs]
            delta_scr[...] = _lane_sum(dv * o_ref[:, cs])
            for g, (d, _) in enumerate(DA_GROUPS):
                tn = rb // d
                for r in range(d):
                    outs[3 * g][h, r] = _strided_rows(d_scr, r, tn, d).astype(BF16)
                    outs[3 * g + 1][h, r] = _strided_rows(l_scr, r, tn, d)
                    outs[3 * g + 2][h, r] = _strided_rows(delta_scr, r, tn, d)

    nat = pl.BlockSpec((rb, wide), lambda i: (i, 0))
    out_specs, out_shape = [], []
    for d, _ in DA_GROUPS:
        for dt in (BF16, F32, F32):
            out_specs.append(pl.BlockSpec((DA_GH, d, rb // d, HEAD), lambda i: (0, 0, i, 0)))
            out_shape.append(jax.ShapeDtypeStruct((DA_GH, d, rows // d, HEAD), dt))
    return pl.pallas_call(
        body, name=name, grid=(rows // rb,),
        in_specs=[nat, nat, nat], out_specs=out_specs, out_shape=out_shape,
        scratch_shapes=[pltpu.VMEM((rb, HEAD), F32)] * 3,
        compiler_params=_params(("parallel",)),
    )(dout, o, lse)


def _da_prep_bwd(dqr, dkr, dvr, proj, q_gain, k_gain, g, *, name):
    d = DA_GROUPS[g][0]
    rows = proj.shape[0]
    rb = min(1024, rows)
    tn = rb // d
    wide = DA_GH * HEAD

    def body(dq_ref, dk_ref, dv_ref, q_ref, k_ref, qg_ref, kg_ref, oq_ref, ok_ref, ov_ref, gq_ref, gk_ref, *nat_refs):
        @pl.when(pl.program_id(0) == 0)
        def _():
            gq_ref[...] = jnp.zeros_like(gq_ref)
            gk_ref[...] = jnp.zeros_like(gk_ref)

        for h in range(DA_GH):
            cs = slice(h * HEAD, (h + 1) * HEAD)
            for j, src in enumerate((dq_ref, dk_ref, dv_ref)):
                for r in range(d):
                    _store_strided(nat_refs[j], r, tn, d, src[h, r])
            ov_ref[:, cs] = nat_refs[2][...].astype(BF16)
            for j, (x_ref, gn_ref, out_ref, acc_ref) in enumerate(((q_ref, qg_ref, oq_ref, gq_ref),
                                                                    (k_ref, kg_ref, ok_ref, gk_ref))):
                xhat, r = _rms(x_ref[:, cs])
                dx, dgr = _rms_bwd(nat_refs[j][...], xhat, r, gn_ref[...])
                out_ref[:, cs] = dx.astype(BF16)
                acc_ref[...] += jnp.sum(dgr, axis=0, keepdims=True)

    res = pl.BlockSpec((DA_GH, d, tn, HEAD), lambda i: (0, 0, i, 0))
    col = lambda off: pl.BlockSpec((rb, wide), lambda i: (i, off // wide + g))
    vec = pl.BlockSpec((1, HEAD), lambda i: (0, 0))
    nat = pl.BlockSpec((rb, wide), lambda i: (i, 0))
    shape = jax.ShapeDtypeStruct((rows, wide), BF16)
    gshape = jax.ShapeDtypeStruct((1, HEAD), F32)
    return pl.pallas_call(
        body, name=name, grid=(rows // rb,),
        in_specs=[res, res, res, col(C_DQ), col(C_DK), vec, vec],
        out_specs=[nat, nat, nat, vec, vec], out_shape=[shape, shape, shape, gshape, gshape],
        scratch_shapes=[pltpu.VMEM((rb, HEAD), F32)] * 3,
        compiler_params=_params(("arbitrary",)),
    )(dqr, dkr, dvr, proj, proj, q_gain, k_gain)


def _mem_fwd(proj, kv, q_gain, k_gain, *, name):
    rows = proj.shape[0]
    tm = min(1024, rows)
    n_mem = kv.shape[0]

    def body(q_ref, k_ref, v_ref, qg_ref, kg_ref, o_ref):
        qhat, _ = _rms(q_ref[...])
        khat, _ = _rms(k_ref[...])
        s = _dot_nt(qhat * qg_ref[...], khat * kg_ref[...]) * ATT_SCALE
        p = jnp.exp(s - jnp.max(s, axis=-1, keepdims=True))
        p = p / jnp.sum(p, axis=-1, keepdims=True)
        o_ref[...] = _dot(p, v_ref[...]).astype(BF16)

    vec = pl.BlockSpec((1, HEAD), lambda i, h: (0, 0))
    return pl.pallas_call(
        body, name=name, grid=(rows // tm, MEM_HEADS),
        in_specs=[pl.BlockSpec((tm, HEAD), lambda i, h: (i, C_MQ // HEAD + h)),
                  pl.BlockSpec((n_mem, HEAD), lambda i, h: (0, h)),
                  pl.BlockSpec((n_mem, HEAD), lambda i, h: (0, MEM_HEADS + h)), vec, vec],
        out_specs=pl.BlockSpec((tm, HEAD), lambda i, h: (i, h)),
        out_shape=jax.ShapeDtypeStruct((rows, MEM_HEADS * HEAD), BF16),
        compiler_params=_params(("parallel", "parallel")),
    )(proj, kv, kv, q_gain, k_gain)


def _mem_bwd(dout, proj, kv, q_gain, k_gain, dproj, *, name):
    rows = proj.shape[0]
    tm = min(1024, rows)
    steps = rows // tm
    n_mem = kv.shape[0]

    def body(d_ref, q_ref, k_ref, v_ref, qg_ref, kg_ref, dp_in, dq_ref, dk_ref, dv_ref, gq_ref, gk_ref, dkn_ref):
        del dp_in
        h, i = pl.program_id(0), pl.program_id(1)

        @pl.when((h == 0) & (i == 0))
        def _():
            gq_ref[...] = jnp.zeros_like(gq_ref)
            gk_ref[...] = jnp.zeros_like(gk_ref)

        @pl.when(i == 0)
        def _():
            dkn_ref[...] = jnp.zeros_like(dkn_ref)
            dv_ref[...] = jnp.zeros_like(dv_ref)

        qhat, rq = _rms(q_ref[...])
        khat, rk = _rms(k_ref[...])
        qn, kn = qhat * qg_ref[...], khat * kg_ref[...]
        s = _dot_nt(qn, kn) * ATT_SCALE
        p = jnp.exp(s - jnp.max(s, axis=-1, keepdims=True))
        p = p / jnp.sum(p, axis=-1, keepdims=True)
        dout = d_ref[...]
        dp = _dot_nt(dout, v_ref[...])
        ds = p * (dp - jnp.sum(p * dp, axis=-1, keepdims=True))
        dv_ref[...] += _dot_tn(p, dout)
        dkn_ref[...] += _dot_tn(ds, qn) * ATT_SCALE
        dq, dgr = _rms_bwd(_dot(ds, kn) * ATT_SCALE, qhat, rq, qg_ref[...])
        dq_ref[...] = dq.astype(BF16)
        gq_ref[...] += jnp.sum(dgr, axis=0, keepdims=True)

        @pl.when(i == steps - 1)
        def _():
            dk, dgk = _rms_bwd(dkn_ref[...], khat, rk, kg_ref[...])
            dk_ref[...] = dk
            gk_ref[...] += jnp.sum(dgk, axis=0, keepdims=True)

    vec = pl.BlockSpec((1, HEAD), lambda h, i: (0, 0))
    memh = pl.BlockSpec((n_mem, HEAD), lambda h, i: (0, h))
    tok = pl.BlockSpec((tm, HEAD), lambda h, i: (i, h))
    mq = pl.BlockSpec((tm, HEAD), lambda h, i: (i, C_MQ // HEAD + h))
    gshape = jax.ShapeDtypeStruct((1, HEAD), F32)
    return pl.pallas_call(
        body, name=name, grid=(MEM_HEADS, steps),
        in_specs=[tok, mq, memh, pl.BlockSpec((n_mem, HEAD), lambda h, i: (0, MEM_HEADS + h)), vec, vec, ANY],
        out_specs=[mq, memh, memh, vec, vec],
        out_shape=[jax.ShapeDtypeStruct(dproj.shape, dproj.dtype),
                   jax.ShapeDtypeStruct((n_mem, MEM_HEADS * HEAD), F32),
                   jax.ShapeDtypeStruct((n_mem, MEM_HEADS * HEAD), F32), gshape, gshape],
        input_output_aliases={6: 0},
        scratch_shapes=[pltpu.VMEM((n_mem, HEAD), F32)],
        compiler_params=_params(("arbitrary", "arbitrary")),
    )(dout, proj, kv, kv, q_gain, k_gain, dproj)


def _branch_fwd(o_hg, o_da, o_mem, proj, wp_hg, wp_da, wp_mem, *, name):
    rows = o_hg.shape[0]
    tm = min(512, rows)

    def body(a_ref, b_ref, c_ref, ga_ref, gb_ref, gc_ref, wa_ref, wb_ref, wc_ref, o_ref):
        merged = _sigmoid(ga_ref[...]) * _dot(a_ref[...], wa_ref[...])
        merged += _sigmoid(gb_ref[...]) * _dot(b_ref[...], wb_ref[...])
        merged += _sigmoid(gc_ref[...]) * _dot(c_ref[...], wc_ref[...])
        o_ref[...] = merged.astype(BF16)

    row = lambda w: pl.BlockSpec((tm, w), lambda i: (i, 0))
    gate = lambda off: pl.BlockSpec((tm, D_MODEL), lambda i: (i, off // D_MODEL))
    full = lambda a: pl.BlockSpec(a.shape, lambda i: (0, 0))
    return pl.pallas_call(
        body, name=name, grid=(rows // tm,),
        in_specs=[row(o_hg.shape[1]), row(o_da.shape[1]), row(o_mem.shape[1]),
                  gate(C_GHG), gate(C_GDA), gate(C_GMEM), full(wp_hg), full(wp_da), full(wp_mem)],
        out_specs=row(D_MODEL), out_shape=jax.ShapeDtypeStruct((rows, D_MODEL), BF16),
        compiler_params=_params(("parallel",)),
    )(o_hg, o_da, o_mem, proj, proj, proj, wp_hg, wp_da, wp_mem)


def _branch_bwd(dm, o_hg, o_da, o_mem, proj, wp_hg, wp_da, wp_mem, *, name):
    rows = o_hg.shape[0]
    tm = min(256, rows)

    def body(dm_ref, a_ref, b_ref, c_ref, ga_ref, gb_ref, gc_ref, wa_ref, wb_ref, wc_ref, dp_ref, *outs):
        dmv = dm_ref[...]
        for j, (o_ref, g_ref, w_ref) in enumerate(((a_ref, ga_ref, wa_ref), (b_ref, gb_ref, wb_ref),
                                                   (c_ref, gc_ref, wc_ref))):
            z = _dot(o_ref[...], w_ref[...])
            gs = _sigmoid(g_ref[...])
            dz = (dmv * gs).astype(BF16)
            dp_ref[:, j * D_MODEL:(j + 1) * D_MODEL] = (dmv * z * gs * (1.0 - gs)).astype(BF16)
            outs[2 * j][...] = dz
            outs[2 * j + 1][...] = _dot_nt(dz, w_ref[...])

    row = lambda w: pl.BlockSpec((tm, w), lambda i: (i, 0))
    gate = lambda off: pl.BlockSpec((tm, D_MODEL), lambda i: (i, off // D_MODEL))
    full = lambda a: pl.BlockSpec(a.shape, lambda i: (0, 0))
    out_specs = [pl.BlockSpec((pl.Element(tm), pl.Element(3 * D_MODEL)), lambda i: (i * tm, C_GHG))]
    out_shape = [jax.ShapeDtypeStruct((rows, IN_COLS), BF16)]
    for o in (o_hg, o_da, o_mem):
        out_specs += [row(D_MODEL), row(o.shape[1])]
        out_shape += [jax.ShapeDtypeStruct((rows, D_MODEL), BF16), jax.ShapeDtypeStruct((rows, o.shape[1]), F32)]
    return pl.pallas_call(
        body, name=name, grid=(rows // tm,),
        in_specs=[row(D_MODEL), row(o_hg.shape[1]), row(o_da.shape[1]), row(o_mem.shape[1]),
                  gate(C_GHG), gate(C_GDA), gate(C_GMEM), full(wp_hg), full(wp_da), full(wp_mem)],
        out_specs=out_specs, out_shape=out_shape,
        compiler_params=_params(("parallel",)),
    )(dm, o_hg, o_da, o_mem, proj, proj, proj, wp_hg, wp_da, wp_mem)


def _ffn_in(h2, w_ab, *, name):
    rows, dff = h2.shape[0], w_ab.shape[1] // 2
    tm, tn = min(2048, rows), 256

    def body(h_ref, wa_ref, wb_ref, a_ref, b_ref, u_ref):
        a = _dot(h_ref[...], wa_ref[...])
        b = _dot(h_ref[...], wb_ref[...])
        a_ref[...] = a.astype(BF16)
        b_ref[...] = b.astype(BF16)
        u_ref[...] = (a * _sigmoid(a) * b).astype(BF16)

    out = pl.BlockSpec((tm, tn), lambda i, j: (i, j))
    return pl.pallas_call(
        body, name=name, grid=(rows // tm, dff // tn),
        in_specs=[pl.BlockSpec((tm, D_MODEL), lambda i, j: (i, 0)),
                  pl.BlockSpec((D_MODEL, tn), lambda i, j: (0, j)),
                  pl.BlockSpec((D_MODEL, tn), lambda i, j: (0, dff // tn + j))],
        out_specs=[out, out, out],
        out_shape=[jax.ShapeDtypeStruct((rows, dff), BF16)] * 3,
        compiler_params=_params(("parallel", "parallel")),
    )(h2, w_ab, w_ab)


def _ffn_act_bwd(dy, w_out, a, b, *, name):
    rows, dff = a.shape
    tm, tn = min(1024, rows), 256

    def body(dy_ref, w_ref, a_ref, b_ref, da_ref, db_ref):
        du = _dot_nt(dy_ref[...], w_ref[...])
        av, bv = a_ref[...].astype(F32), b_ref[...].astype(F32)
        sa = _sigmoid(av)
        da_ref[...] = (du * bv * sa * (1.0 + av * (1.0 - sa))).astype(BF16)
        db_ref[...] = (du * av * sa).astype(BF16)

    tile = pl.BlockSpec((tm, tn), lambda i, j: (i, j))
    return pl.pallas_call(
        body, name=name, grid=(rows // tm, dff // tn),
        in_specs=[pl.BlockSpec((tm, D_MODEL), lambda i, j: (i, 0)),
                  pl.BlockSpec((tn, D_MODEL), lambda i, j: (j, 0)), tile, tile],
        out_specs=[tile, tile],
        out_shape=[jax.ShapeDtypeStruct((rows, dff), BF16), jax.ShapeDtypeStruct((rows, dff), BF16)],
        compiler_params=_params(("parallel", "parallel")),
    )(dy, w_out, a, b)


def _lower_bound(lb_fw, lb_bw, *, name):
    def body(a_ref, b_ref, oa_ref, ob_ref):
        for src, dst in ((a_ref, oa_ref), (b_ref, ob_ref)):
            dst[...] = _sigmoid(src[0:1, :] - src[1:2, :])

    shape = jax.ShapeDtypeStruct((1, lb_fw.shape[1]), F32)
    return pl.pallas_call(body, name=name, out_shape=[shape, shape])(lb_fw, lb_bw)


def _local_step(x, mem, tgt, p, ex):
    rows = x.shape[0]
    lb_fw, lb_bw = _lower_bound(p["lb_logits_fw"], p["lb_logits_bw"], name="lower_bound")

    h = _rms_fwd(x, p["norm_mix_gain"], name="norm_mix")
    w = {"w_in": ex.w_in}
    proj, carried = _matmul([(h, w["w_in"])], "nn", BF16, tm=1024, tn=1024, carry=ex.late_carry(), name="proj_in")
    w.update(ex.late_weights(carried))
    o_fw, st_fw = _gla_fwd(proj, lb_fw, f_off=C_FF, rev=False, name="gla_fwd_fw")
    o_bw, st_bw = _gla_fwd(proj, lb_bw, f_off=C_FB, rev=True, name="gla_fwd_bw")
    o_hg = _hg_out_fwd(o_fw, o_bw, proj, p["hg_norm_gain"], name="hg_out")

    qkv_r, outs, lses = [], [], []
    for g in range(len(DA_GROUPS)):
        qr, kr, vr = _da_prep(proj, p["da_q_gain"], p["da_k_gain"], g, name=f"da_prep{g}")
        og, lg = _band_fwd(qr, kr, vr, g, name=f"band_fwd{g}")
        qkv_r.append((qr, kr, vr))
        outs.append(og)
        lses.append(lg)
    o_da, lse_da = _da_merge(outs, lses, rows, name="da_merge")

    mem_n = _rms_fwd(mem, p["norm_mem_gain"], name="norm_mem")
    kv = _matmul([(mem_n, w["w_mem_kv"])], "nn", F32, tm=256, tn=512, name="mem_kv")
    o_mem = _mem_fwd(proj, kv, p["mem_q_gain"], p["mem_k_gain"], name="mem_attn")

    merged = _branch_fwd(o_hg, o_da, o_mem, proj, w["w_proj_hg"], w["w_proj_da"], w["w_proj_mem"],
                         name="branch_merge")
    x1, h2 = _out_norm(merged, w["w_out"], x, p["norm_ffn_gain"], name="mix_out_norm")
    a, b, u = _ffn_in(h2, w["w_ffn_in"], name="ffn_in")
    dy, dy_b, loss = _out_loss(u, w["w_ffn_out"], x1, tgt, name="ffn_out_loss")

    gw, gwb, gs = {}, {}, {}
    gw["w_ffn_out"], gwb["w_ffn_out"] = _matmul([(u, dy_b)], "tn", F32, tm=256, tn=1024, also=BF16, name="g_ffn_out")
    da, db = _ffn_act_bwd(dy_b, w["w_ffn_out"], a, b, name="ffn_act_bwd")
    gw["w_ffn_a"], gwb["w_ffn_a"] = _matmul([(h2, da)], "tn", F32, tm=512, tn=256, also=BF16, name="g_ffn_a")
    gw["w_ffn_b"], gwb["w_ffn_b"] = _matmul([(h2, db)], "tn", F32, tm=512, tn=256, also=BF16, name="g_ffn_b")
    dh2 = _matmul([(da, w["w_ffn_in"], 0), (db, w["w_ffn_in"], 1)], "nt", F32, tm=256, tn=1024, name="d_h2")
    dx1, dx1_b, gs["norm_ffn_gain"] = _rms_bwd_rows(dh2, x1, p["norm_ffn_gain"], dy, (F32, BF16), name="norm_ffn_bwd")
    gw["w_out"], gwb["w_out"] = _matmul([(merged, dx1_b)], "tn", F32, tm=512, tn=512, also=BF16, name="g_out")
    dmerged = _matmul([(dx1_b, w["w_out"])], "nt", F32, tm=512, tn=512, name="d_merged")
    dproj, dz_hg, do_hg, dz_da, do_da, dz_mem, do_mem = _branch_bwd(
        dmerged, o_hg, o_da, o_mem, proj, w["w_proj_hg"], w["w_proj_da"], w["w_proj_mem"], name="branch_bwd")
    gw["w_proj_hg"], gwb["w_proj_hg"] = _matmul([(o_hg, dz_hg)], "tn", F32, tm=512, tn=512, also=BF16, name="g_proj_hg")
    gw["w_proj_da"], gwb["w_proj_da"] = _matmul([(o_da, dz_da)], "tn", F32, tm=512, tn=512, also=BF16, name="g_proj_da")
    gw["w_proj_mem"], gwb["w_proj_mem"] = _matmul([(o_mem, dz_mem)], "tn", F32, tm=512, tn=512, also=BF16, name="g_proj_mem")

    dproj, dk_mem, dv_mem, gs["mem_q_gain"], gs["mem_k_gain"] = _mem_bwd(
        do_mem, proj, kv, p["mem_q_gain"], p["mem_k_gain"], dproj, name="mem_attn_bwd")
    dkv = jnp.concatenate([dk_mem, dv_mem], axis=1).astype(BF16)
    gw["w_mem_kv"], gwb["w_mem_kv"] = _matmul([(mem_n, dkv)], "tn", F32, tm=512, tn=512, also=BF16, name="g_mem_kv")
    dmem_n = _matmul([(dkv, w["w_mem_kv"])], "nt", F32, tm=256, tn=512, name="d_mem_n")
    (gs["norm_mem_gain"],) = _rms_bwd_rows(dmem_n, mem, p["norm_mem_gain"], None, (), name="norm_mem_bwd")

    prep = _da_bwd_prep(do_da, o_da, lse_da, name="da_bwd_prep")
    d_da, gq_parts, gk_parts = [], [], []
    for g in range(len(DA_GROUPS)):
        qr, kr, vr = qkv_r[g]
        dor, lser, deltar = prep[3 * g:3 * g + 3]
        dqr, dkr, dvr = _band_bwd(qr, kr, vr, dor, lser, deltar, g, name=f"band_bwd{g}")
        dq, dk, dv, gq, gk = _da_prep_bwd(dqr, dkr, dvr, proj, p["da_q_gain"], p["da_k_gain"], g,
                                          name=f"da_prep_bwd{g}")
        d_da.append((dq, dk, dv))
        gq_parts.append(gq)
        gk_parts.append(gk)
    for j, (off, tag) in enumerate(((C_DQ, "q"), (C_DK, "k"), (C_DV, "v"))):
        dproj = _fill_columns(dproj, [t[j] for t in d_da], off, name=f"dproj_fill_da_{tag}")

    dproj, do_gla, gs["hg_norm_gain"] = _hg_out_bwd(do_hg, o_fw, o_bw, proj, p["hg_norm_gain"], dproj,
                                                    name="hg_out_bwd")
    carry_bulk, carry_rest = ex.early_carries(gw, gwb)
    (dq_f, dproj, dv_f, dlb_fw), bulk = _gla_bwd(proj, lb_fw, do_gla, st_fw, None, dproj, f_off=C_FF, rev=False,
                                                 carry=carry_bulk, name="gla_bwd_fw")
    (dproj, dfl_bw, dv_hg, dlb_bw), rest = _gla_bwd(proj, lb_bw, do_gla, st_bw, (dq_f, dv_f), dproj, f_off=C_FB,
                                                    rev=True, carry=carry_rest, name="gla_bwd_bw")
    ex.early_done(bulk, rest)
    dproj = _fill_columns(dproj, [dfl_bw, dv_hg], C_FB, name="dproj_fill")
    g_in, g_in_b = _matmul([(h, dproj)], "tn", F32, tm=1024, tn=512, also=BF16, name="g_in")
    dh, carried = _matmul([(dproj, w["w_in"])], "nt", F32, tm=1024, tn=1024, tk=IN_COLS // 8,
                          carry=ex.w_in_carry(g_in, g_in_b), name="d_h")
    ex.w_in_done(carried)
    grad_x, gs["norm_mix_gain"] = _rms_bwd_rows(dh, x, p["norm_mix_gain"], dx1, (F32,), name="norm_mix_bwd")

    small = _small_pack(gs, gq_parts, gk_parts, dlb_fw, dlb_bw, lb_fw, lb_bw, loss, name="small_pack")
    return grad_x, small


def _small_pack(gs, gq_parts, gk_parts, dlb_fw, dlb_bw, lb_fw, lb_bw, loss, *, name):
    def body(g_mix, g_mem, g_ffn, dfw, dbw, lfw, lbw, g_hg, q0, q1, q2, k0, k1, k2, g_mq, g_mk, loss_ref, o_ref):
        o_ref[0:1, :] = g_mix[...]
        o_ref[1:2, :] = g_mem[...]
        o_ref[2:3, :] = g_ffn[...]
        for base, d_ref, l_ref in ((3, dfw, lfw), (5, dbw, lbw)):
            lbv = l_ref[...]
            t = d_ref[...] * lbv * (1.0 - lbv)
            o_ref[base:base + 1, :] = t
            o_ref[base + 1:base + 2, :] = -t
        o_ref[7:8, :] = jnp.zeros((1, D_MODEL), F32)
        o_ref[7:8, 0:HEAD] = g_hg[...]
        o_ref[7:8, HEAD:2 * HEAD] = q0[...] + q1[...] + q2[...]
        o_ref[7:8, 2 * HEAD:3 * HEAD] = k0[...] + k1[...] + k2[...]
        o_ref[7:8, 3 * HEAD:4 * HEAD] = g_mq[...]
        o_ref[7:8, 4 * HEAD:5 * HEAD] = g_mk[...]
        o_ref[7:8, LOSS_COL:LOSS_COL + HEAD] = loss_ref[...]

    return pl.pallas_call(body, name=name, out_shape=jax.ShapeDtypeStruct((8, D_MODEL), F32))(
        gs["norm_mix_gain"], gs["norm_mem_gain"], gs["norm_ffn_gain"], dlb_fw, dlb_bw, lb_fw, lb_bw,
        gs["hg_norm_gain"], *gq_parts, *gk_parts, gs["mem_q_gain"], gs["mem_k_gain"], loss)


def _row_tile(rows, cols, n_arrays):
    budget = (16 * 1024 * 1024) // (2 * 4 * cols * n_arrays)
    tr = rows
    while tr > budget and tr % 2 == 0 and (tr // 2) % 16 == 0:
        tr //= 2
    return tr


def _cast_into_full(a, chip, rows, cols, axis, *, name):
    sr, sc = a.shape
    tr = _row_tile(sr, sc, 2)

    def body(chip_ref, a_ref, o_ref):
        del chip_ref
        o_ref[...] = a_ref[...].astype(BF16)

    if axis == 1:
        out_map = lambda i, chip_ref: (i, chip_ref[0])
    else:
        out_map = lambda i, chip_ref: (chip_ref[0] * (sr // tr) + i, 0)
    return pl.pallas_call(
        body, name=name,
        grid_spec=pltpu.PrefetchScalarGridSpec(
            num_scalar_prefetch=1, grid=(sr // tr,),
            in_specs=[pl.BlockSpec((tr, sc), lambda i, chip_ref: (i, 0))],
            out_specs=pl.BlockSpec((tr, sc), out_map)),
        out_shape=jax.ShapeDtypeStruct((rows, cols), BF16),
        compiler_params=_params(("parallel",)))(chip, a)


def _add_halves(items, *, name):
    rows = items[0][3].shape[0]
    widths = [ra.shape[1] for _, _, _, ra in items]
    tr = _row_tile(rows, sum(widths), 4)

    def body(*refs):
        o_ref = refs[-1]
        first = lax.axis_index("c") == 0
        off = 0
        for j, wd in enumerate(widths):
            h0, h1, ra = refs[3 * j:3 * j + 3]
            o_ref[:, off:off + wd] = (jnp.where(first, h0[...], h1[...]) + ra[...]).astype(BF16)
            off += wd

    in_specs, ins = [], []
    for (g, haxis, hsize, ra), wd in zip(items, widths):
        if haxis == 0:
            in_specs += [pl.BlockSpec((tr, wd), lambda i: (i, 0)),
                         pl.BlockSpec((tr, wd), lambda i, o=hsize // tr: (o + i, 0))]
        else:
            in_specs += [pl.BlockSpec((tr, wd), lambda i: (i, 0)), pl.BlockSpec((tr, wd), lambda i: (i, 1))]
        in_specs.append(pl.BlockSpec((tr, wd), lambda i: (i, 0)))
        ins += [g, g, ra]
    return pl.pallas_call(body, name=name, grid=(rows // tr,), in_specs=in_specs,
                          out_specs=pl.BlockSpec((tr, sum(widths)), lambda i: (i, 0)),
                          out_shape=jax.ShapeDtypeStruct((rows, sum(widths)), BF16),
                          compiler_params=_params(("parallel",)))(*ins)


def _add_slots(rb, *, name):
    _, rows, cols = rb.shape
    tr = _row_tile(rows, cols, 5)

    def body(r0, r1, r2, r3, o_ref):
        o_ref[...] = ((r0[...].astype(F32) + r1[...].astype(F32)) + r2[...].astype(F32)) + r3[...].astype(F32)

    slot = lambda s: pl.BlockSpec((None, tr, cols), lambda i: (s, i, 0))
    return pl.pallas_call(body, name=name, grid=(rows // tr,), in_specs=[slot(s) for s in range(4)],
                          out_specs=pl.BlockSpec((tr, cols), lambda i: (i, 0)),
                          out_shape=jax.ShapeDtypeStruct((rows, cols), F32),
                          compiler_params=_params(("parallel",)))(rb, rb, rb, rb)


def _adamw(w, g, m, v, *, name):
    rows, cols = w.shape
    tr = _row_tile(rows, cols, 8) if rows % 16 == 0 else rows
    c1 = 1.0 - ADAM_B1 ** ADAM_STEP
    c2 = 1.0 - ADAM_B2 ** ADAM_STEP

    def body(w_ref, g_ref, m_ref, v_ref, d_ref, mo_ref, vo_ref, go_ref):
        gv = g_ref[...]
        go_ref[...] = gv
        mn = ADAM_B1 * m_ref[...] + (1.0 - ADAM_B1) * gv
        vn = ADAM_B2 * v_ref[...] + (1.0 - ADAM_B2) * (gv * gv)
        mo_ref[...] = mn
        vo_ref[...] = vn
        d_ref[...] = -ADAM_LR * ((mn / c1) / (jnp.sqrt(vn / c2) + ADAM_EPS) + ADAM_WD * w_ref[...])

    spec = pl.BlockSpec((tr, cols), lambda i: (i, 0))
    shape = jax.ShapeDtypeStruct((rows, cols), F32)
    return pl.pallas_call(body, name=name, grid=(rows // tr,), in_specs=[spec] * 4, out_specs=[spec] * 4,
                        out_shape=[shape] * 4, compiler_params=_params(("parallel",)))(w, g, m, v)


W_SPECS = (
    ("w_in", 1024, IN_COLS, 1, IN_COLS // 4),
    ("w_mem_kv", 1024, 1024, 0, 256),
    ("w_proj_hg", 1024, 1024, 0, 256),
    ("w_proj_da", 512, 1024, 1, 256),
    ("w_proj_mem", 512, 1024, 1, 256),
    ("w_out", 1024, 1024, 0, 256),
    ("w_ffn_in", 1024, 2 * D_FF, 1, 2 * D_FF // 4),
    ("w_ffn_out", D_FF, 1024, 0, D_FF // 4),
)
CHIP_FLIPS = ((1, 0), (0, 1), (1, 1))
ANY = pl.BlockSpec(memory_space=pl.ANY)
DMA_CHUNK_BYTES = 1 << 20
STAGE_BYTES = 2 << 20


def _place():
    x, y, c = lax.axis_index("x"), lax.axis_index("y"), lax.axis_index("c")
    return x, y, c, 2 * x + y


def _flip(v, f):
    return 1 - v if f else v


def _slab(ref, axis, idx, size):
    start = pl.multiple_of(idx * size, size)
    return ref.at[pl.ds(start, size), :] if axis == 0 else ref.at[:, pl.ds(start, size)]


def _chunked(make, src, dst, want="both"):
    rows, cols = src.shape
    row_bytes = cols * jnp.dtype(src.dtype).itemsize
    k = 1
    while rows % (2 * k) == 0 and (rows // (2 * k)) % 16 == 0 and (rows // k) * row_bytes > DMA_CHUNK_BYTES:
        k *= 2
    cr = rows // k
    parts = []
    if want != "wait":
        parts = [make(src.at[pl.ds(j * cr, cr), :], dst.at[pl.ds(j * cr, cr), :]) for j in range(k)]
    return parts, (make(src, dst) if want != "start" else None)


def _half_spec(rows, cols, axis):
    return (0, rows // 2) if axis == 1 else (1, cols // 2)


def _staged(src, remote_dst, local_dst, sibling, load_sems, send_sems, store_sems, recv_sem):
    rows, cols = src.shape
    row_bytes = cols * jnp.dtype(src.dtype).itemsize
    k = 1
    while (rows // k) * row_bytes > STAGE_BYTES and rows % (2 * k) == 0 and (rows // (2 * k)) % 16 == 0:
        k *= 2
    cr = rows // k
    piece = lambda ref, j: ref.at[pl.ds(j * cr, cr), :]

    def run(buf):
        loads = [pltpu.make_async_copy(piece(src, j), buf.at[j % 2], load_sems.at[j % 2]) for j in range(k)]
        outs = [[pltpu.make_async_remote_copy(src_ref=buf.at[j % 2], dst_ref=piece(remote_dst, j),
                                              send_sem=send_sems.at[j % 2], recv_sem=recv_sem,
                                              device_id=sibling, device_id_type=MESH)] for j in range(k)]
        if local_dst is not None:
            for j in range(k):
                outs[j].append(pltpu.make_async_copy(buf.at[j % 2], piece(local_dst, j), store_sems.at[j % 2]))

        def drained(j):
            outs[j][0].wait_send()
            for cp in outs[j][1:]:
                cp.wait()

        loads[0].start()
        for j in range(k):
            loads[j].wait()
            for cp in outs[j]:
                cp.start()
            if j + 1 < k:
                if j >= 1:
                    drained(j - 1)
                loads[j + 1].start()
        for j in range(max(0, k - 2), k):
            drained(j)

    pl.run_scoped(run, pltpu.VMEM((2, cr, cols), src.dtype))


def _landed(ref, recv_sem, send_sem):
    pltpu.make_async_remote_copy(src_ref=ref, dst_ref=ref, send_sem=send_sem, recv_sem=recv_sem,
                                 device_id=(lax.axis_index("x"), lax.axis_index("y"), lax.axis_index("c")),
                                 device_id_type=MESH).wait_recv()


def _half_slab(ref, spec, chip, half):
    _, rows, cols, axis, size = spec
    haxis, hsize = _half_spec(rows, cols, axis)
    return _slab(_slab(ref, axis, chip, size), haxis, half, hsize)


def _gather_sends(outs, specs, ici_send, ici_recv, want="both"):
    x, y, c, p = _place()
    sent = []
    for wi, spec in enumerate(specs):
        mine = _half_slab(outs[wi], spec, p, c)
        for k, (fx, fy) in enumerate(CHIP_FLIPS):
            sent.append(_chunked(lambda s, d, j=3 * wi + k, fx=fx, fy=fy: pltpu.make_async_remote_copy(
                src_ref=s, dst_ref=d, send_sem=ici_send.at[j], recv_sem=ici_recv.at[j],
                device_id=(_flip(x, fx), _flip(y, fy), c), device_id_type=MESH), mine, mine, want))
    return sent


def _gather_weights(fulls, specs, *, name):
    n = len(specs)

    def body(*refs):
        outs = refs[n:2 * n]
        ici_send, ici_recv, load_sems, d2d_send, d2d_recv = refs[2 * n:]
        x, y, c, _ = _place()
        sent = _gather_sends(outs, specs, ici_send, ici_recv)
        for parts, _ in sent:
            for cp in parts:
                cp.start()
        for k, (fx, fy) in enumerate(CHIP_FLIPS):
            q = 2 * _flip(x, fx) + _flip(y, fy)
            for wi in range(n):
                sent[3 * wi + k][1].wait_recv()
                got = _half_slab(outs[wi], specs[wi], q, c)
                _staged(got, got, None, (x, y, 1 - c), load_sems, d2d_send, None, d2d_recv.at[3 * wi + k])
        for _, whole in sent:
            whole.wait_send()
        for k, (fx, fy) in enumerate(CHIP_FLIPS):
            q = 2 * _flip(x, fx) + _flip(y, fy)
            for wi in range(n):
                _landed(_half_slab(outs[wi], specs[wi], q, 1 - c), d2d_recv.at[3 * wi + k], d2d_send.at[0])

    return pl.pallas_call(
        body, name=name, in_specs=[ANY] * n, out_specs=[ANY] * n,
        out_shape=[jax.ShapeDtypeStruct(f.shape, f.dtype) for f in fulls],
        input_output_aliases={i: i for i in range(n)},
        scratch_shapes=[pltpu.SemaphoreType.DMA((3 * n,)), pltpu.SemaphoreType.DMA((3 * n,)),
                        pltpu.SemaphoreType.DMA((2,)), pltpu.SemaphoreType.DMA((2,)),
                        pltpu.SemaphoreType.DMA((3 * n,))],
    )(*fulls)


def _gather_chips_carry(fulls, specs):
    n = len(specs)

    def issue(ins, outs, sems, want):
        del ins
        return [(parts, whole, True) for parts, whole in _gather_sends(outs, specs, sems[0], sems[1], want)]

    return _Carry(list(fulls), [jax.ShapeDtypeStruct(f.shape, f.dtype) for f in fulls], {i: i for i in range(n)},
                  [pltpu.SemaphoreType.DMA((3 * n,)), pltpu.SemaphoreType.DMA((3 * n,))], issue)


def _gather_pass_on(fulls, specs, *, name):
    n = len(specs)

    def body(*refs):
        outs = refs[n:2 * n]
        load_sems, d2d_send, d2d_recv = refs[2 * n:2 * n + 3]
        bufs = refs[2 * n + 3:]
        x, y, c, _ = _place()
        loads, sends = [], []
        for k, (fx, fy) in enumerate(CHIP_FLIPS):
            q = 2 * _flip(x, fx) + _flip(y, fy)
            for wi in range(n):
                j = 3 * wi + k
                got = _half_slab(outs[wi], specs[wi], q, c)
                loads.append(pltpu.make_async_copy(got, bufs[j], load_sems.at[j]))
                sends.append(pltpu.make_async_remote_copy(
                    src_ref=bufs[j], dst_ref=got, send_sem=d2d_send.at[j], recv_sem=d2d_recv.at[j],
                    device_id=(x, y, 1 - c), device_id_type=MESH))
        for cp in loads:
            cp.start()
        for load, send in zip(loads, sends):
            load.wait()
            send.start()
        for cp in sends:
            cp.wait_send()
        for k, (fx, fy) in enumerate(CHIP_FLIPS):
            q = 2 * _flip(x, fx) + _flip(y, fy)
            for wi in range(n):
                _landed(_half_slab(outs[wi], specs[wi], q, 1 - c), d2d_recv.at[3 * wi + k], d2d_send.at[0])

    shapes = []
    for _, rows, cols, axis, size in specs:
        shapes += [(rows // 2, size) if axis == 1 else (size, cols // 2)] * 3
    assert sum(math.prod(s) for s in shapes) * 2 <= VMEM_LIMIT_V7X // 2
    stages = [pltpu.VMEM(s, BF16) for s in shapes]
    return pl.pallas_call(
        body, name=name, in_specs=[ANY] * n, out_specs=[ANY] * n,
        out_shape=[jax.ShapeDtypeStruct(f.shape, f.dtype) for f in fulls],
        input_output_aliases={i: i for i in range(n)},
        scratch_shapes=[pltpu.SemaphoreType.DMA((3 * n,))] * 3 + stages,
        compiler_params=pltpu.CompilerParams(vmem_limit_bytes=VMEM_LIMIT_V7X),
    )(*fulls)


def _sibling_exchange(grads, *, name):
    n = len(grads)

    def body(*refs):
        ins, outs = refs[:n], refs[n:2 * n]
        load_sems, send_sems, recv_sems = refs[2 * n:]
        x, y, c, _ = _place()
        for i, (_, haxis, hsize) in enumerate(grads):
            _staged(_slab(ins[i], haxis, 1 - c, hsize), outs[i], None, (x, y, 1 - c),
                    load_sems, send_sems, None, recv_sems.at[i])
        for i in range(n):
            _landed(outs[i], recv_sems.at[i], send_sems.at[0])

    shapes = [jax.ShapeDtypeStruct((hsize, g.shape[1]) if haxis == 0 else (g.shape[0], hsize), g.dtype)
              for g, haxis, hsize in grads]
    return pl.pallas_call(
        body, name=name, in_specs=[ANY] * n, out_specs=[ANY] * n, out_shape=shapes,
        scratch_shapes=[pltpu.SemaphoreType.DMA((2,)), pltpu.SemaphoreType.DMA((2,)),
                        pltpu.SemaphoreType.DMA((n,))],
    )(*[g for g, _, _ in grads])


def _chip_exchange_carry(parts):
    n = len(parts)

    def issue(ins, outs, sems, want):
        send_sems, recv_sems, local_sems = sems
        x, y, c, p = _place()
        copies = []
        for i, (_, axis, size) in enumerate(parts):
            copies.append(_chunked(lambda s, d, i=i: pltpu.make_async_copy(s, d, local_sems.at[i]),
                                   _slab(ins[i], axis, p, size), outs[i].at[p], want) + (False,))
            for k, (fx, fy) in enumerate(CHIP_FLIPS):
                px, py = _flip(x, fx), _flip(y, fy)
                copies.append(_chunked(lambda s, d, j=3 * i + k, px=px, py=py: pltpu.make_async_remote_copy(
                    src_ref=s, dst_ref=d, send_sem=send_sems.at[j], recv_sem=recv_sems.at[j],
                    device_id=(px, py, c), device_id_type=MESH),
                    _slab(ins[i], axis, 2 * px + py, size), outs[i].at[p], want) + (True,))
        return copies

    shapes = []
    for a, axis, size in parts:
        shapes.append(jax.ShapeDtypeStruct((4, size, a.shape[1]) if axis == 0 else (4, a.shape[0], size), a.dtype))
    return _Carry([a for a, _, _ in parts], shapes, {},
                  [pltpu.SemaphoreType.DMA((3 * n,)), pltpu.SemaphoreType.DMA((3 * n,)),
                   pltpu.SemaphoreType.DMA((n,))], issue)


def _sibling_share(sums):
    n = len(sums)

    def body(*refs):
        ins, outs = refs[:n], refs[n:2 * n]
        load_sems, send_sems, store_sems, recv_sems = refs[2 * n:]
        x, y, c, _ = _place()
        for i, (s, haxis) in enumerate(sums):
            place = _slab(outs[i], haxis, c, s.shape[haxis])
            _staged(ins[i], place, place, (x, y, 1 - c), load_sems, send_sems, store_sems, recv_sems.at[i])
        for i, (s, haxis) in enumerate(sums):
            _landed(_slab(outs[i], haxis, 1 - c, s.shape[haxis]), recv_sems.at[i], send_sems.at[0])

    shapes = []
    for s, haxis in sums:
        r, cc = s.shape
        shapes.append(jax.ShapeDtypeStruct((2 * r, cc) if haxis == 0 else (r, 2 * cc), F32))
    return pl.pallas_call(
        body, name="grad_sibling_share", in_specs=[ANY] * n, out_specs=[ANY] * n, out_shape=shapes,
        scratch_shapes=[pltpu.SemaphoreType.DMA((2,)), pltpu.SemaphoreType.DMA((2,)),
                        pltpu.SemaphoreType.DMA((2,)), pltpu.SemaphoreType.DMA((n,))],
    )(*[s for s, _ in sums])


class _Exchanges:
    def __init__(self, fulls):
        self.w_in = _gather_weights([fulls["w_in"]], W_SPECS[:1], name="gather_w_in")[0]
        self.late_specs = W_SPECS[1:]
        self.late = [fulls[s[0]] for s in self.late_specs]
        self.slots = {}

    def late_carry(self):
        return _gather_chips_carry(self.late, self.late_specs)

    def late_weights(self, carried):
        done = _gather_pass_on(carried, self.late_specs, name="gather_pass_on")
        return dict(zip([s[0] for s in self.late_specs], done))

    def _half_sums(self, gw, gwb, specs, tag):
        grads, sent = [], []
        for name, _, _, axis, _ in specs:
            for part in (("w_ffn_a", "w_ffn_b") if name == "w_ffn_in" else (name,)):
                half = _half_spec(gw[part].shape[0], gw[part].shape[1], axis)
                grads.append((gw[part],) + half)
                sent.append((gwb[part],) + half)
        theirs = _sibling_exchange(sent, name=f"grad_sibling_exchange_{tag}")
        parts, j = [], 0
        for name, _, _, axis, size in specs:
            take = 2 if name == "w_ffn_in" else 1
            items = [grads[i] + (theirs[i],) for i in range(j, j + take)]
            parts.append((_add_halves(items, name=f"half_sum_{name}"), axis, size))
            j += take
        return parts

    def early_carries(self, gw, gwb):
        parts = self._half_sums(gw, gwb, self.late_specs, "early")
        self.early_names = [s[0] for s in self.late_specs]
        cut = self.early_names.index("w_ffn_in")
        return _chip_exchange_carry(parts[cut:]), _chip_exchange_carry(parts[:cut])

    def early_done(self, carried_bulk, carried_rest):
        cut = self.early_names.index("w_ffn_in")
        self.slots.update(zip(self.early_names[cut:], carried_bulk))
        self.slots.update(zip(self.early_names[:cut], carried_rest))

    def w_in_carry(self, g, gb):
        return _chip_exchange_carry(self._half_sums({"w_in": g}, {"w_in": gb}, W_SPECS[:1], "w_in"))

    def w_in_done(self, carried):
        self.slots["w_in"] = carried[0]

    def reduced(self):
        sums = []
        for name, rows, cols, axis, _ in W_SPECS:
            sums.append((_add_slots(self.slots[name], name=f"chip_sum_{name}"), _half_spec(rows, cols, axis)[0]))
        return dict(zip([s[0] for s in W_SPECS], _sibling_share(sums)))


def _small_allreduce(sv):
    def body(sv_ref, o_ref, slots_ref, send_sems, recv_sems):
        x, y, c, _ = _place()
        me = 4 * x + 2 * y + c
        slots_ref[me] = sv_ref[...]
        copies = []
        for k in range(1, 8):
            fx, fy, fc = (k >> 2) & 1, (k >> 1) & 1, k & 1
            copies.append(pltpu.make_async_remote_copy(
                src_ref=sv_ref, dst_ref=slots_ref.at[me], send_sem=send_sems.at[k - 1],
                recv_sem=recv_sems.at[k - 1], device_id=(_flip(x, fx), _flip(y, fy), _flip(c, fc)),
                device_id_type=MESH))
        for cp in copies:
            cp.start()
        for cp in copies:
            cp.wait_recv()
        for cp in copies:
            cp.wait_send()
        total = slots_ref[0]
        for s in range(1, 8):
            total = total + slots_ref[s]
        o_ref[...] = total

    vm = pl.BlockSpec(memory_space=pltpu.VMEM)
    return pl.pallas_call(
        body, name="small_allreduce", in_specs=[vm], out_specs=vm,
        out_shape=jax.ShapeDtypeStruct(sv.shape, F32),
        scratch_shapes=[pltpu.VMEM((8,) + sv.shape, F32), pltpu.SemaphoreType.DMA((7,)),
                        pltpu.SemaphoreType.DMA((7,))],
    )(sv)


SMALL_ROWS = (("norm_mix_gain", 0), ("norm_mem_gain", 1), ("norm_ffn_gain", 2))
SMALL_LB = (("lb_logits_fw", 3), ("lb_logits_bw", 5))
SMALL_HEAD = ("hg_norm_gain", "da_q_gain", "da_k_gain", "mem_q_gain", "mem_k_gain")
LOSS_COL = HEAD * len(SMALL_HEAD)


def _pack_small(d):
    last = jnp.concatenate([d[n] for n in SMALL_HEAD] + [jnp.zeros((1, D_MODEL - HEAD * len(SMALL_HEAD)), F32)], axis=1)
    return jnp.concatenate([d["norm_mix_gain"], d["norm_mem_gain"], d["norm_ffn_gain"],
                            d["lb_logits_fw"], d["lb_logits_bw"], last], axis=0)


def _unpack_small(a):
    out = {n: a[r:r + 1] for n, r in SMALL_ROWS}
    out.update({n: a[r:r + 2] for n, r in SMALL_LB})
    out.update({n: a[7:8, j * HEAD:(j + 1) * HEAD] for j, n in enumerate(SMALL_HEAD)})
    return out


PARAM_ORDER = ("norm_mix_gain", "norm_mem_gain", "w_in", "lb_logits_fw", "lb_logits_bw", "hg_norm_gain",
               "da_q_gain", "da_k_gain", "w_mem_kv", "mem_q_gain", "mem_k_gain", "w_proj_hg", "w_proj_da",
               "w_proj_mem", "w_out", "norm_ffn_gain", "w_ffn_in", "w_ffn_out")


def kernel(x, mem, norm_mix_gain, norm_mem_gain, w_in, lb_logits_fw, lb_logits_bw, hg_norm_gain, da_q_gain, da_k_gain, w_mem_kv, mem_q_gain, mem_k_gain, w_proj_hg, w_proj_da, w_proj_mem, w_out, norm_ffn_gain, w_ffn_in, w_ffn_out, loss_target, m_norm_mix_gain, m_norm_mem_gain, m_w_in, m_lb_logits_fw, m_lb_logits_bw, m_hg_norm_gain, m_da_q_gain, m_da_k_gain, m_w_mem_kv, m_mem_q_gain, m_mem_k_gain, m_w_proj_hg, m_w_proj_da, m_w_proj_mem, m_w_out, m_norm_ffn_gain, m_w_ffn_in, m_w_ffn_out, v_norm_mix_gain, v_norm_mem_gain, v_w_in, v_lb_logits_fw, v_lb_logits_bw, v_hg_norm_gain, v_da_q_gain, v_da_k_gain, v_w_mem_kv, v_mem_q_gain, v_mem_k_gain, v_w_proj_hg, v_w_proj_da, v_w_proj_mem, v_w_out, v_norm_ffn_gain, v_w_ffn_in, v_w_ffn_out):
    args = dict(locals())
    mats = tuple(s[0] for s in W_SPECS)
    flat = lambda a: a.reshape(a.shape[-2:])
    w = {n: flat(args[n]) for n in mats}
    m = {n: flat(args["m_" + n]) for n in mats}
    v = {n: flat(args["v_" + n]) for n in mats}
    small = {n: args[n] for n in PARAM_ORDER if n not in mats}

    chip = (2 * lax.axis_index("x") + lax.axis_index("y")).astype(jnp.int32).reshape(1)
    ex = _Exchanges({n: _cast_into_full(w[n], chip, rows, cols, axis, name=f"cast_{n}")
                     for n, rows, cols, axis, _ in W_SPECS})
    grad_x, small_grads = _local_step(x[0], mem[0], loss_target[0], small, ex)
    grads = ex.reduced()
    small_sum = _small_allreduce(small_grads)

    delta, new_m, new_v = {}, {}, {}
    for n in mats:
        delta[n], new_m[n], new_v[n], grads[n] = _adamw(w[n], grads[n], m[n], v[n], name=f"adamw_{n}")
    packed = _adamw(_pack_small(small), small_sum,
                    _pack_small({n: args["m_" + n] for n in small}),
                    _pack_small({n: args["v_" + n] for n in small}), name="adamw_small")
    for dst, src in zip((delta, new_m, new_v, grads), packed):
        dst.update(_unpack_small(src))

    def shaped(d, n):
        return d[n].reshape(args[n].shape)

    return (small_sum[7, LOSS_COL], grad_x[None], *[shaped(grads, n) for n in PARAM_ORDER], *[shaped(delta, n) for n in PARAM_ORDER],
            *[shaped(new_m, n) for n in PARAM_ORDER], *[shaped(new_v, n) for n in PARAM_ORDER])
```

```python
import math

import numpy as np
import jax
import jax.numpy as jnp
from jax import lax
from jax.experimental import pallas as pl
from jax.experimental.pallas import tpu as pltpu

F32 = jnp.float32
BF16 = jnp.bfloat16
MESH = pl.DeviceIdType.MESH

D_MODEL = 1024
HEAD = 128
HG_HEADS = 8
DA_GROUPS = ((1, 64), (4, 64), (16, 64))
DA_GH = 4
MEM_HEADS = 4
N_MEM = 256
D_FF = 2816
CHUNK = 64
BAND_QBLOCKS = 16
GLA_HEADS_PER_STEP = 8
RMS_EPS = 1e-6
NEG_INF = -1e30
HG_SCALE = HEAD ** -0.5
ATT_SCALE = HEAD ** -0.5
VMEM_LIMIT_V7X = 48 * 1024 * 1024

C_HQ, C_FF, C_FB, C_HI, C_HG = 0, 1024, 2048, 3072, 4096
C_DQ, C_DK, C_DV, C_MQ = 5120, 6656, 8192, 9728
C_GHG, C_GDA, C_GMEM = 10240, 11264, 12288
IN_COLS = 13312

ADAM_LR, ADAM_B1, ADAM_B2, ADAM_EPS, ADAM_WD, ADAM_STEP = 0.001, 0.9, 0.999, 1e-08, 0.01, 10


def _params(sem, vmem=VMEM_LIMIT_V7X):
    return pltpu.CompilerParams(dimension_semantics=sem, vmem_limit_bytes=vmem)


def _dot(a, b):
    return jnp.dot(a.astype(BF16), b.astype(BF16), preferred_element_type=F32)


def _dot_nt(a, b):
    return lax.dot_general(a.astype(BF16), b.astype(BF16), (((1,), (1,)), ((), ())),
                           preferred_element_type=F32)


def _dot_tn(a, b):
    return lax.dot_general(a.astype(BF16), b.astype(BF16), (((0,), (0,)), ((), ())),
                           preferred_element_type=F32)


def _sigmoid(v):
    return jax.nn.sigmoid(v.astype(F32))


def _ones(rows, cols):
    return (lax.broadcasted_iota(jnp.int32, (rows, cols), 0) >= 0).astype(BF16)


def _lane_sum(v):
    ones = _ones(HEAD, HEAD)
    hi = v.astype(BF16)
    mid = (v - hi.astype(F32)).astype(BF16)
    return jnp.dot(hi, ones, preferred_element_type=F32) + jnp.dot(mid, ones, preferred_element_type=F32)


def _row_mean(v):
    if v.shape[-1] == HEAD:
        return _lane_sum(v) * (1.0 / HEAD)
    return jnp.mean(v, axis=-1, keepdims=True)


def _rms(v):
    v = v.astype(F32)
    r = lax.rsqrt(_row_mean(v * v) + RMS_EPS)
    return v * r, r


def _rms_bwd(dy, xhat, r, gain):
    dxh = dy * gain
    dx = r * (dxh - xhat * _row_mean(dxh * xhat))
    return dx, dy * xhat


class _Carry:
    def __init__(self, arrays, out_shapes, aliases, sems, issue):
        self.arrays, self.out_shapes, self.aliases, self.sems, self.issue = arrays, out_shapes, aliases, sems, issue


NO_CARRY = object()


def _start_copies(copies):
    for parts, _, _ in copies:
        for cp in parts:
            cp.start()


def _wait_copies(copies):
    for _, whole, remote in copies:
        if remote:
            whole.wait_recv()
    for _, whole, remote in copies:
        if remote:
            whole.wait_send()
        else:
            whole.wait()


def _call(body, *, name, grid, in_specs, out_specs, out_shape, scratch_shapes, semantics, ins, carry=None,
          aliases=None):
    aliases = dict(aliases or {})
    if carry is None:
        res = pl.pallas_call(body, name=name, grid=grid, in_specs=in_specs, out_specs=out_specs, out_shape=out_shape,
                             scratch_shapes=scratch_shapes, input_output_aliases=aliases,
                             compiler_params=_params(semantics))(*ins)
        return list(res), []
    n_in, n_out, n_scr = len(ins), len(out_shape), len(scratch_shapes)
    c_in, c_out = len(carry.arrays), len(carry.out_shapes)

    def wrapped(*refs):
        pos = [0]

        def take(count):
            pos[0] += count
            return refs[pos[0] - count:pos[0]]

        own_in, carry_in = take(n_in), take(c_in)
        own_out, carry_out = take(n_out), take(c_out)
        own_scr, carry_sems = take(n_scr), take(len(carry.sems))
        ids = [pl.program_id(a) for a in range(len(grid))]
        first, last = ids[0] == 0, ids[0] == grid[0] - 1
        for a in range(1, len(grid)):
            first, last = first & (ids[a] == 0), last & (ids[a] == grid[a] - 1)

        @pl.when(first)
        def _():
            _start_copies(carry.issue(carry_in, carry_out, carry_sems, "start"))

        body(*own_in, *own_out, *own_scr)

        @pl.when(last)
        def _():
            _wait_copies(carry.issue(carry_in, carry_out, carry_sems, "wait"))

    res = pl.pallas_call(
        wrapped, name=name, grid=grid, in_specs=list(in_specs) + [ANY] * c_in,
        out_specs=list(out_specs) + [ANY] * c_out, out_shape=list(out_shape) + list(carry.out_shapes),
        input_output_aliases={**aliases, **{n_in + i: n_out + j for i, j in carry.aliases.items()}},
        scratch_shapes=list(scratch_shapes) + list(carry.sems),
        compiler_params=_params(("arbitrary",) * len(grid)))(*ins, *carry.arrays)
    return list(res[:n_out]), list(res[n_out:])


def _matmul(pairs, mode, out_dtype, *, tm, tn, tk=None, residual=None, also=None, carry=NO_CARRY, name):
    b_offs = [pr[2] if len(pr) > 2 else 0 for pr in pairs]
    pairs = [pr[:2] for pr in pairs]
    a0, b0 = pairs[0]
    if mode == "nn":
        (m, kk), n = a0.shape, b0.shape[1]
    elif mode == "nt":
        (m, kk), n = a0.shape, b0.shape[0]
    else:
        (kk, m), n = a0.shape, b0.shape[1]
    assert mode == "nt" or not any(b_offs)
    tm, tn = min(tm, m), min(tn, n)
    tk = kk if tk is None else tk
    nk = kk // tk
    assert m % tm == 0 and n % tn == 0 and kk % tk == 0, (name, m, n, kk)
    n_p = len(pairs)
    if mode == "tn":
        a_spec = pl.BlockSpec((tk, tm), lambda i, j, k: (k, i))
    else:
        a_spec = pl.BlockSpec((tm, tk), lambda i, j, k: (i, k))
    if mode == "nt":
        b_specs = [pl.BlockSpec((tn, tk), lambda i, j, k, o=o: (j, o * nk + k)) for o in b_offs]
    else:
        b_specs = [pl.BlockSpec((tk, tn), lambda i, j, k: (k, j))] * n_p
    o_spec = pl.BlockSpec((tm, tn), lambda i, j, k: (i, j))
    dot = {"nn": _dot, "nt": _dot_nt, "tn": _dot_tn}[mode]
    has_res = residual is not None

    def body(*refs):
        a_refs, b_refs = refs[:n_p], refs[n_p:2 * n_p]
        pos = 2 * n_p
        res_ref = refs[pos] if has_res else None
        pos += int(has_res)
        o_ref = refs[pos]
        pos += int(also is not None)
        part = dot(a_refs[0][...], b_refs[0][...])
        for a_r, b_r in zip(a_refs[1:], b_refs[1:]):
            part += dot(a_r[...], b_r[...])

        def finish(total):
            if has_res:
                total = total + res_ref[...]
            o_ref[...] = total.astype(out_dtype)
            if also is not None:
                refs[pos][...] = total.astype(also)

        if nk == 1:
            finish(part)
        else:
            acc_ref = refs[pos + 1]
            k = pl.program_id(2)

            @pl.when(k == 0)
            def _():
                acc_ref[...] = part

            @pl.when(k > 0)
            def _():
                acc_ref[...] += part

            @pl.when(k == nk - 1)
            def _():
                finish(acc_ref[...])

    ins = [a for a, _ in pairs] + [b for _, b in pairs]
    in_specs = [a_spec] * n_p + b_specs
    if has_res:
        ins.append(residual)
        in_specs.append(o_spec)
    dtypes = [out_dtype] + ([also] if also is not None else [])
    outs, carried = _call(
        body, name=name, grid=(m // tm, n // tn, nk), in_specs=in_specs, out_specs=[o_spec] * len(dtypes),
        out_shape=[jax.ShapeDtypeStruct((m, n), dt) for dt in dtypes],
        scratch_shapes=[pltpu.VMEM((tm, tn), F32)] if nk > 1 else [],
        semantics=("parallel", "parallel", "arbitrary"), ins=ins, carry=None if carry is NO_CARRY else carry)
    out = outs[0] if also is None else tuple(outs)
    return out if carry is NO_CARRY else (out, carried)


def _rms_fwd(x, gain, *, name):
    rows, dm = x.shape
    tm = min(1024, rows)

    def body(x_ref, g_ref, h_ref):
        xhat, _ = _rms(x_ref[...])
        h_ref[...] = (xhat * g_ref[...]).astype(BF16)

    return pl.pallas_call(
        body, name=name, grid=(rows // tm,),
        in_specs=[pl.BlockSpec((tm, dm), lambda i: (i, 0)), pl.BlockSpec((1, dm), lambda i: (0, 0))],
        out_specs=pl.BlockSpec((tm, dm), lambda i: (i, 0)),
        out_shape=jax.ShapeDtypeStruct((rows, dm), BF16),
        compiler_params=_params(("parallel",)),
    )(x, gain)


def _rms_bwd_rows(dh, x, gain, dres, dx_dtypes, *, name):
    rows, dm = x.shape
    tm = min(512, rows)
    has_res = dres is not None

    def body(*refs):
        dh_ref, x_ref, g_ref = refs[:3]
        res_ref = refs[3] if has_res else None
        outs = refs[3 + int(has_res):]
        dg_ref = outs[-1]
        xhat, r = _rms(x_ref[...])
        dx, dgr = _rms_bwd(dh_ref[...], xhat, r, g_ref[...])
        if has_res:
            dx = dx + res_ref[...]
        for dx_ref, dt in zip(outs, dx_dtypes):
            dx_ref[...] = dx.astype(dt)

        @pl.when(pl.program_id(0) == 0)
        def _():
            dg_ref[...] = jnp.zeros_like(dg_ref)

        dg_ref[...] += jnp.sum(dgr, axis=0, keepdims=True)

    row = pl.BlockSpec((tm, dm), lambda i: (i, 0))
    vec = pl.BlockSpec((1, dm), lambda i: (0, 0))
    return pl.pallas_call(
        body, name=name, grid=(rows // tm,),
        in_specs=[row, row, vec] + ([row] if has_res else []),
        out_specs=[row] * len(dx_dtypes) + [vec],
        out_shape=[jax.ShapeDtypeStruct((rows, dm), dt) for dt in dx_dtypes] + [jax.ShapeDtypeStruct((1, dm), F32)],
        compiler_params=_params(("arbitrary",)),
    )(*([dh, x, gain] + ([dres] if has_res else [])))


def _out_norm(a, w, x, gain, *, name):
    rows, dm = x.shape
    tm = min(512, rows)

    def body(a_ref, w_ref, x_ref, g_ref, x1_ref, h_ref):
        x1 = x_ref[...] + _dot(a_ref[...], w_ref[...])
        x1_ref[...] = x1
        xhat, _ = _rms(x1)
        h_ref[...] = (xhat * g_ref[...]).astype(BF16)

    row = lambda wd: pl.BlockSpec((tm, wd), lambda i: (i, 0))
    return pl.pallas_call(
        body, name=name, grid=(rows // tm,),
        in_specs=[row(a.shape[1]), pl.BlockSpec(w.shape, lambda i: (0, 0)), row(dm), pl.BlockSpec((1, dm), lambda i: (0, 0))],
        out_specs=[row(dm), row(dm)],
        out_shape=[jax.ShapeDtypeStruct((rows, dm), F32), jax.ShapeDtypeStruct((rows, dm), BF16)],
        compiler_params=_params(("parallel",)),
    )(a, w, x, gain)


def _out_loss(u, w, x1, tgt, *, name):
    rows, dm = x1.shape
    tm = min(256, rows)
    steps = rows // tm

    def body(u_ref, w_ref, x_ref, t_ref, dy_ref, dyb_ref, loss_ref, acc_ref):
        i = pl.program_id(0)
        diff = x_ref[...] + _dot(u_ref[...], w_ref[...]) - t_ref[...]
        dy = diff * (1.0 / dm)
        dy_ref[...] = dy
        dyb_ref[...] = dy.astype(BF16)

        @pl.when(i == 0)
        def _():
            acc_ref[...] = jnp.zeros_like(acc_ref)

        acc_ref[...] += jnp.sum(diff * diff, axis=0, keepdims=True)

        @pl.when(i == steps - 1)
        def _():
            loss_ref[...] = jnp.full((1, HEAD), 0.5 / dm, F32) * jnp.sum(acc_ref[...])

    row = lambda wd: pl.BlockSpec((tm, wd), lambda i: (i, 0))
    return pl.pallas_call(
        body, name=name, grid=(steps,),
        in_specs=[row(u.shape[1]), pl.BlockSpec(w.shape, lambda i: (0, 0)), row(dm), row(dm)],
        out_specs=[row(dm), row(dm), pl.BlockSpec((1, HEAD), lambda i: (0, 0))],
        out_shape=[jax.ShapeDtypeStruct((rows, dm), F32), jax.ShapeDtypeStruct((rows, dm), BF16),
                   jax.ShapeDtypeStruct((1, HEAD), F32)],
        scratch_shapes=[pltpu.VMEM((1, dm), F32)],
        compiler_params=_params(("arbitrary",)),
    )(u, w, x1, tgt)


def _gla_block_terms(q_raw, f_logit, lb, rev):
    sig = _sigmoid(f_logit)
    forget = lb + (1.0 - lb) * sig
    k = 1.0 - forget
    b = _chunk_cumsum(jnp.log(forget), rev)
    qs = _sigmoid(q_raw)
    eb = jnp.exp(b)
    emb = jnp.exp(-b)
    qt = (q_raw * qs * HG_SCALE) * eb
    kt = k * emb
    return sig, forget, k, b, qs, eb, emb, qt, kt


def _chunk_cumsum(v, rev):
    n = v.shape[0]
    pos = lax.broadcasted_iota(jnp.int32, v.shape, 0) & (CHUNK - 1)
    step = 1
    while step < CHUNK:
        if rev:
            shifted, keep = pltpu.roll(v, n - step, 0), pos < CHUNK - step
        else:
            shifted, keep = pltpu.roll(v, step, 0), pos >= step
        v = v + jnp.where(keep, shifted, 0.0)
        step *= 2
    return v


def _tri_mask(n, rev):
    row, col = np.arange(n)[:, None], np.arange(n)[None, :]
    same = (row // CHUNK) == (col // CHUNK)
    return jnp.asarray((same & ((row <= col) if rev else (row >= col))).astype(np.float32))


def _chunk_order(ncb, rev):
    order = range(ncb - 1, -1, -1) if rev else range(ncb)
    return [(c, c * CHUNK if rev else c * CHUNK + CHUNK - 1) for c in order]


def _gla_fwd(proj, lb, *, f_off, rev, name):
    rows = proj.shape[0]
    tb = min(256, rows)
    nb, ncb = rows // tb, tb // CHUNK

    def tmap(n):
        return nb - 1 - n if rev else n

    def body(tri_ref, q_ref, f_ref, v_ref, lb_ref, o_ref, st_ref, s_ref):
        @pl.when(pl.program_id(1) == 0)
        def _():
            s_ref[...] = jnp.zeros_like(s_ref)

        tri = tri_ref[...] > 0.5
        for hh in range(GLA_HEADS_PER_STEP):
            cs = slice(hh * HEAD, (hh + 1) * HEAD)
            v = v_ref[:, cs]
            _, _, k, b, _, _, _, qt, kt = _gla_block_terms(q_ref[:, cs].astype(F32), f_ref[:, cs].astype(F32),
                                                           lb_ref[:, cs], rev)
            qt = qt.astype(BF16)
            o_intra = _dot(jnp.where(tri, _dot_nt(qt, kt), 0.0), v)
            chunks = []
            for c, last in _chunk_order(ncb, rev):
                sl = slice(c * CHUNK, (c + 1) * CHUNK)
                bl = b[last:last + 1, :]
                kh = k[sl] * jnp.exp(bl - b[sl])
                chunks.append((c, sl, jnp.exp(bl), _dot_tn(v[sl], kh)))
            s_t = s_ref[hh]
            for c, sl, ebl, kv in chunks:
                st_ref[hh, c] = s_t
                o_ref[sl, cs] = o_intra[sl] + _dot_nt(qt[sl], s_t)
                s_t = ebl * s_t + kv
            s_ref[hh] = s_t

    hps = GLA_HEADS_PER_STEP
    col = lambda off: pl.BlockSpec((tb, hps * HEAD), lambda h, n: (tmap(n), off // (hps * HEAD) + h))
    return pl.pallas_call(
        body, name=name, grid=(HG_HEADS // hps, nb),
        in_specs=[pl.BlockSpec((tb, tb), lambda h, n: (0, 0)), col(C_HQ), col(f_off), col(C_HI),
                  pl.BlockSpec((1, hps * HEAD), lambda h, n: (0, h))],
        out_specs=[pl.BlockSpec((tb, hps * HEAD), lambda h, n: (tmap(n), h)),
                   pl.BlockSpec((hps, ncb, HEAD, HEAD), lambda h, n: (h, tmap(n), 0, 0))],
        out_shape=[jax.ShapeDtypeStruct((rows, HG_HEADS * HEAD), F32),
                   jax.ShapeDtypeStruct((HG_HEADS, rows // CHUNK, HEAD, HEAD), F32)],
        scratch_shapes=[pltpu.VMEM((hps, HEAD, HEAD), F32)],
        compiler_params=_params(("parallel", "arbitrary")),
    )(_tri_mask(tb, rev), proj, proj, proj, lb)


def _gla_bwd(proj, lb, do, states, prev, dproj, *, f_off, rev, carry=NO_CARRY, name):
    rows = proj.shape[0]
    tb = min(256, rows)
    nb, ncb = rows // tb, tb // CHUNK
    has_prev = prev is not None
    qv_dtype = BF16 if has_prev else F32

    def tmap(n):
        return n if rev else nb - 1 - n

    def body(*refs):
        tri_ref, q_ref, f_ref, v_ref, lb_ref, do_ref, st_ref = refs[:7]
        pq_ref, pv_ref = refs[8:10] if has_prev else (None, None)
        (dq_ref, df_ref, dv_ref, dlb_ref,
         ds_ref, dqt_scr, dk_scr, db_scr, dbl_scr, dv_scr) = refs[8 + 2 * int(has_prev):]

        @pl.when(pl.program_id(1) == 0)
        def _():
            ds_ref[...] = jnp.zeros_like(ds_ref)
            dlb_ref[...] = jnp.zeros_like(dlb_ref)

        tri = tri_ref[...] > 0.5
        for hh in range(GLA_HEADS_PER_STEP):
            cs = slice(hh * HEAD, (hh + 1) * HEAD)
            lbv = lb_ref[:, cs]
            q_raw = q_ref[:, cs].astype(F32)
            v, dout = v_ref[:, cs], do_ref[:, cs].astype(BF16)
            sig, forget, k, b, qs, eb, emb, qt, kt = _gla_block_terms(q_raw, f_ref[:, cs].astype(F32), lbv, rev)
            qt_b, kt_b = qt.astype(BF16), kt.astype(BF16)
            a = jnp.where(tri, _dot_nt(qt_b, kt_b), 0.0)
            da = jnp.where(tri, _dot_nt(dout, v), 0.0).astype(BF16)
            dv_intra = _dot_tn(a, dout)
            dqt_intra = _dot(da, kt_b)
            dkt = _dot_tn(da, qt_b)
            chunks = []
            for c, last in reversed(_chunk_order(ncb, rev)):
                sl = slice(c * CHUNK, (c + 1) * CHUNK)
                bl = b[last:last + 1, :]
                e = jnp.exp(bl - b[sl])
                s_t = st_ref[hh, c]
                dqt_scr[sl, cs] = dqt_intra[sl] + _dot(dout[sl], s_t)
                chunks.append((sl, jnp.exp(bl), e, k[sl] * e, s_t, _dot_tn(dout[sl], qt_b[sl])))
            ds_t = ds_ref[hh]
            for sl, ebl, e, kh, s_t, grow in chunks:
                dkh = _dot(v[sl], ds_t)
                dv_scr[sl, cs] = dv_intra[sl] + _dot_nt(kh, ds_t)
                dk_scr[sl, cs] = dkt[sl] * emb[sl] + dkh * e
                khd = kh * dkh
                dbl = jnp.sum(khd, axis=0, keepdims=True) + ebl * jnp.sum(ds_t * s_t, axis=0, keepdims=True)
                db_scr[sl, cs] = khd
                dbl_scr[sl, cs] = jnp.broadcast_to(dbl, (CHUNK, HEAD))
                ds_t = grow + ds_t * ebl
            ds_ref[hh] = ds_t
            dqt = dqt_scr[:, cs]
            dlogf = _chunk_cumsum(qt * dqt - kt * dkt - db_scr[:, cs], not rev) + dbl_scr[:, cs]
            dforget = dlogf / forget - dk_scr[:, cs]
            df_ref[:, cs] = (dforget * (1.0 - lbv) * sig * (1.0 - sig)).astype(BF16)
            dlb_ref[:, cs] += jnp.sum(dforget * (1.0 - sig), axis=0, keepdims=True)
            dqr = dqt * eb * (HG_SCALE * qs * (1.0 + q_raw * (1.0 - qs)))
            dv = dv_scr[:, cs]
            if has_prev:
                dqr = dqr + pq_ref[:, cs]
                dv = dv + pv_ref[:, cs]
            dq_ref[:, cs] = dqr.astype(qv_dtype)
            dv_ref[:, cs] = dv.astype(qv_dtype)

    hps = GLA_HEADS_PER_STEP
    col = lambda off: pl.BlockSpec((tb, hps * HEAD), lambda h, n: (tmap(n), off // (hps * HEAD) + h))
    blk = pl.BlockSpec((tb, hps * HEAD), lambda h, n: (tmap(n), h))
    vec = pl.BlockSpec((1, hps * HEAD), lambda h, n: (0, h))
    wide = HG_HEADS * HEAD
    into = jax.ShapeDtypeStruct(dproj.shape, dproj.dtype)
    outs, carried = _call(
        body, name=name, grid=(HG_HEADS // hps, nb),
        in_specs=[pl.BlockSpec((tb, tb), lambda h, n: (0, 0)), col(C_HQ), col(f_off), col(C_HI), vec, blk,
                  pl.BlockSpec((hps, ncb, HEAD, HEAD), lambda h, n: (h, tmap(n), 0, 0)), ANY]
                 + ([blk, blk] if has_prev else []),
        out_specs=[col(C_HQ) if has_prev else blk, blk if has_prev else col(f_off), blk, vec],
        out_shape=[into if has_prev else jax.ShapeDtypeStruct((rows, wide), qv_dtype),
                   jax.ShapeDtypeStruct((rows, wide), BF16) if has_prev else into,
                   jax.ShapeDtypeStruct((rows, wide), qv_dtype), jax.ShapeDtypeStruct((1, wide), F32)],
        aliases={7: 0 if has_prev else 1},
        scratch_shapes=[pltpu.VMEM((hps, HEAD, HEAD), F32)] + [pltpu.VMEM((tb, hps * HEAD), F32)] * 5,
        semantics=("parallel", "arbitrary"),
        ins=[_tri_mask(tb, rev), proj, proj, proj, lb, do, states, dproj] + (list(prev) if has_prev else []),
        carry=None if carry is NO_CARRY else carry)
    return outs if carry is NO_CARRY else (outs, carried)


def _fill_columns(dproj, parts, col, *, name):
    rows, wd = parts[0].shape
    tm = min(1024, rows)
    n = len(parts)

    def body(*refs):
        o_ref = refs[n + 1]
        for j in range(n):
            o_ref[:, j * wd:(j + 1) * wd] = refs[j][...]

    return pl.pallas_call(
        body, name=name, grid=(rows // tm,),
        in_specs=[pl.BlockSpec((tm, wd), lambda i: (i, 0))] * n + [ANY],
        out_specs=pl.BlockSpec((pl.Element(tm), pl.Element(n * wd)), lambda i: (i * tm, col)),
        out_shape=jax.ShapeDtypeStruct(dproj.shape, dproj.dtype), input_output_aliases={n: 0},
        compiler_params=_params(("parallel",)),
    )(*parts, dproj)


def _hg_out_fwd(o_fw, o_bw, proj, gain, *, name):
    rows = o_fw.shape[0]
    tm = min(1024, rows)
    wide = HG_HEADS * HEAD

    def body(a_ref, b_ref, g_ref, gain_ref, o_ref):
        for h in range(HG_HEADS):
            sl = slice(h * HEAD, (h + 1) * HEAD)
            xhat, _ = _rms(a_ref[:, sl] + b_ref[:, sl])
            gate = g_ref[:, sl].astype(F32)
            o_ref[:, sl] = (xhat * gain_ref[...] * (gate * _sigmoid(gate))).astype(BF16)

    row = pl.BlockSpec((tm, wide), lambda i: (i, 0))
    return pl.pallas_call(
        body, name=name, grid=(rows // tm,),
        in_specs=[row, row, pl.BlockSpec((tm, wide), lambda i: (i, C_HG // wide)),
                  pl.BlockSpec((1, HEAD), lambda i: (0, 0))],
        out_specs=row, out_shape=jax.ShapeDtypeStruct((rows, wide), BF16),
        compiler_params=_params(("parallel",)),
    )(o_fw, o_bw, proj, gain)


def _hg_out_bwd(dout, o_fw, o_bw, proj, gain, dproj, *, name):
    rows = o_fw.shape[0]
    tm = min(512, rows)
    wide = HG_HEADS * HEAD

    def body(d_ref, a_ref, b_ref, g_ref, gain_ref, dp_in, dgate_ref, do_ref, dgain_ref):
        del dp_in

        @pl.when(pl.program_id(0) == 0)
        def _():
            dgain_ref[...] = jnp.zeros_like(dgain_ref)

        dgain = jnp.zeros((1, HEAD), F32)
        for h in range(HG_HEADS):
            sl = slice(h * HEAD, (h + 1) * HEAD)
            xhat, r = _rms(a_ref[:, sl] + b_ref[:, sl])
            gate, dy = g_ref[:, sl].astype(F32), d_ref[:, sl]
            gs = _sigmoid(gate)
            dgate_ref[:, sl] = (dy * xhat * gain_ref[...] * (gs * (1.0 + gate * (1.0 - gs)))).astype(BF16)
            dx, dgr = _rms_bwd(dy * (gate * gs), xhat, r, gain_ref[...])
            do_ref[:, sl] = dx
            dgain = dgain + jnp.sum(dgr, axis=0, keepdims=True)
        dgain_ref[...] += dgain

    row = pl.BlockSpec((tm, wide), lambda i: (i, 0))
    gate = pl.BlockSpec((tm, wide), lambda i: (i, C_HG // wide))
    vec = pl.BlockSpec((1, HEAD), lambda i: (0, 0))
    return pl.pallas_call(
        body, name=name, grid=(rows // tm,),
        in_specs=[row, row, row, gate, vec, ANY],
        out_specs=[gate, row, vec],
        out_shape=[jax.ShapeDtypeStruct(dproj.shape, dproj.dtype), jax.ShapeDtypeStruct((rows, wide), F32),
                   jax.ShapeDtypeStruct((1, HEAD), F32)],
        input_output_aliases={5: 0},
        compiler_params=_params(("arbitrary",)),
    )(dout, o_fw, o_bw, proj, gain, dproj)


def _strided_rows(ref, r, count, d):
    return ref[...] if d == 1 else ref[pl.ds(r, count, stride=d), :]


def _store_strided(ref, r, count, d, val):
    if d == 1:
        ref[...] = val
    else:
        ref[pl.ds(r, count, stride=d), :] = val


def _da_prep(proj, q_gain, k_gain, g, *, name):
    d = DA_GROUPS[g][0]
    rows = proj.shape[0]
    rb = min(1024, rows)
    tn = rb // d

    def body(q_ref, k_ref, v_ref, qg_ref, kg_ref, qo_ref, ko_ref, vo_ref, qf_ref, kf_ref, vf_ref):
        for h in range(DA_GH):
            cs = slice(h * HEAD, (h + 1) * HEAD)
            for src, dst in ((q_ref, qf_ref), (k_ref, kf_ref), (v_ref, vf_ref)):
                dst[...] = src[:, cs].astype(F32)
            for r in range(d):
                qhat, _ = _rms(_strided_rows(qf_ref, r, tn, d))
                khat, _ = _rms(_strided_rows(kf_ref, r, tn, d))
                qo_ref[h, r] = (qhat * qg_ref[...]).astype(BF16)
                ko_ref[h, r] = (khat * kg_ref[...]).astype(BF16)
                vo_ref[h, r] = _strided_rows(vf_ref, r, tn, d).astype(BF16)

    wide = DA_GH * HEAD
    col = lambda off: pl.BlockSpec((rb, wide), lambda i: (i, off // wide + g))
    vec = pl.BlockSpec((1, HEAD), lambda i: (0, 0))
    out = pl.BlockSpec((DA_GH, d, tn, HEAD), lambda i: (0, 0, i, 0))
    shape = jax.ShapeDtypeStruct((DA_GH, d, rows // d, HEAD), BF16)
    return pl.pallas_call(
        body, name=name, grid=(rows // rb,),
        in_specs=[col(C_DQ), col(C_DK), col(C_DV), vec, vec],
        out_specs=[out, out, out], out_shape=[shape, shape, shape],
        scratch_shapes=[pltpu.VMEM((rb, HEAD), F32)] * 3,
        compiler_params=_params(("parallel",)),
    )(proj, proj, proj, q_gain, k_gain)


def _slopes(g):
    idx = np.arange(g * DA_GH + 1, (g + 1) * DA_GH + 1)
    s = (2.0 ** (-8.0 * idx / (DA_GH * len(DA_GROUPS)))).astype(np.float32)
    return jnp.asarray(np.broadcast_to(s[:, None, None], (DA_GH, 8, HEAD)).copy())


def _band_window(ld, t, radius):
    win = min(2 * t, ld)
    assert t // 2 >= radius or win == ld
    return win


def _band_scores(q, k, q0, start, slope, d, radius):
    t, win = q.shape[0], k.shape[0]
    row = lax.broadcasted_iota(jnp.int32, (t, win), 0)
    col = lax.broadcasted_iota(jnp.int32, (t, win), 1)
    rel = jnp.abs((start - q0) + col - row)
    return _dot_nt(q, k) * ATT_SCALE + jnp.where(rel <= radius, -slope * (d * rel).astype(F32), NEG_INF)


def _band_fwd(qr, kr, vr, g, *, name):
    d, radius = DA_GROUPS[g]
    _, _, ld, _ = qr.shape
    t = min(HEAD, ld)
    qb = min(BAND_QBLOCKS, ld // t)
    rps = max(1, min(d, BAND_QBLOCKS // qb))
    win = _band_window(ld, t, radius)

    def body(q_ref, k_ref, v_ref, sl_ref, o_ref, lse_ref):
        i = pl.program_id(2)
        slope = sl_ref[0:1, 0:1]
        for rr in range(rps):
            for j in range(qb):
                sl = slice(j * t, (j + 1) * t)
                q0 = (i * qb + j) * t
                start = pl.multiple_of(jnp.clip(q0 - t // 2, 0, ld - win), t // 2)
                s = _band_scores(q_ref[rr, sl, :], k_ref[rr, pl.ds(start, win), :], q0, start, slope, d, radius)
                m = jnp.max(s, axis=-1, keepdims=True)
                p = jnp.exp(s - m).astype(BF16)
                l = jnp.dot(p, _ones(win, HEAD), preferred_element_type=F32)
                o_ref[rr, sl, :] = _dot(p, v_ref[rr, pl.ds(start, win), :]) / l
                lse_ref[rr, sl, :] = m + jnp.log(l)

    own = pl.BlockSpec((None, rps, qb * t, HEAD), lambda h, r, i: (h, r, i, 0))
    seq = pl.BlockSpec((None, rps, ld, HEAD), lambda h, r, i: (h, r, 0, 0))
    shape = jax.ShapeDtypeStruct(qr.shape, F32)
    return pl.pallas_call(
        body, name=name, grid=(DA_GH, d // rps, ld // (qb * t)),
        in_specs=[own, seq, seq, pl.BlockSpec((None, 8, HEAD), lambda h, r, i: (h, 0, 0))],
        out_specs=[own, own], out_shape=[shape, shape],
        compiler_params=_params(("parallel", "parallel", "parallel")),
    )(qr, kr, vr, _slopes(g))


def _band_bwd(qr, kr, vr, dor, lser, deltar, g, *, name):
    d, radius = DA_GROUPS[g]
    _, _, ld, _ = qr.shape
    t = min(HEAD, ld)
    qb = min(BAND_QBLOCKS, ld // t)
    rps = max(1, min(d, BAND_QBLOCKS // qb))
    win = _band_window(ld, t, radius)

    def body(q_ref, k_ref, v_ref, do_ref, lse_ref, dl_ref, sl_ref, dq_ref, dk_ref, dv_ref):
        i = pl.program_id(2)

        @pl.when(i == 0)
        def _():
            dk_ref[...] = jnp.zeros_like(dk_ref)
            dv_ref[...] = jnp.zeros_like(dv_ref)

        slope = sl_ref[0:1, 0:1]
        for rr in range(rps):
            for j in range(qb):
                sl = slice(j * t, (j + 1) * t)
                q0 = (i * qb + j) * t
                start = pl.multiple_of(jnp.clip(q0 - t // 2, 0, ld - win), t // 2)
                rows = pl.ds(start, win)
                q, dout, k, v = q_ref[rr, sl, :], do_ref[rr, sl, :], k_ref[rr, rows, :], v_ref[rr, rows, :]
                p = jnp.exp(_band_scores(q, k, q0, start, slope, d, radius) - lse_ref[rr, sl, 0:1])
                ds = p * (_dot_nt(dout, v) - dl_ref[rr, sl, 0:1])
                dq_ref[rr, sl, :] = _dot(ds, k) * ATT_SCALE
                dk_ref[rr, rows, :] += _dot_tn(ds, q) * ATT_SCALE
                dv_ref[rr, rows, :] += _dot_tn(p, dout)

    own = pl.BlockSpec((None, rps, qb * t, HEAD), lambda h, r, i: (h, r, i, 0))
    seq = pl.BlockSpec((None, rps, ld, HEAD), lambda h, r, i: (h, r, 0, 0))
    shape = jax.ShapeDtypeStruct(qr.shape, F32)
    return pl.pallas_call(
        body, name=name, grid=(DA_GH, d // rps, ld // (qb * t)),
        in_specs=[own, seq, seq, own, own, own, pl.BlockSpec((None, 8, HEAD), lambda h, r, i: (h, 0, 0))],
        out_specs=[own, seq, seq], out_shape=[shape, shape, shape],
        compiler_params=_params(("parallel", "parallel", "arbitrary")),
    )(qr, kr, vr, dor, lser, deltar, _slopes(g))


def _da_merge(outs, lses, rows, *, name):
    rb = min(1024, rows)
    wide = DA_GH * HEAD

    def body(*refs):
        o_refs, l_refs = refs[0:3], refs[3:6]
        o_ref, lse_ref = refs[6:8]
        on_refs, ln_refs = refs[8:11], refs[11:14]
        for h in range(DA_GH):
            cs = slice(h * HEAD, (h + 1) * HEAD)
            for g, (d, _) in enumerate(DA_GROUPS):
                tn = rb // d
                for r in range(d):
                    _store_strided(on_refs[g], r, tn, d, o_refs[g][h, r])
                    _store_strided(ln_refs[g], r, tn, d, l_refs[g][h, r])
            l0, l1, l2 = ln_refs[0][...], ln_refs[1][...], ln_refs[2][...]
            m = jnp.maximum(jnp.maximum(l0, l1), l2)
            e0, e1, e2 = jnp.exp(l0 - m), jnp.exp(l1 - m), jnp.exp(l2 - m)
            tot = e0 + e1 + e2
            o_ref[:, cs] = (e0 * on_refs[0][...] + e1 * on_refs[1][...] + e2 * on_refs[2][...]) / tot
            lse_ref[:, cs] = m + jnp.log(tot)

    res = lambda d: pl.BlockSpec((DA_GH, d, rb // d, HEAD), lambda i: (0, 0, i, 0))
    nat = pl.BlockSpec((rb, wide), lambda i: (i, 0))
    shape = jax.ShapeDtypeStruct((rows, wide), F32)
    return pl.pallas_call(
        body, name=name, grid=(rows // rb,),
        in_specs=[res(d) for d, _ in DA_GROUPS] * 2,
        out_specs=[nat, nat], out_shape=[shape, shape],
        scratch_shapes=[pltpu.VMEM((rb, HEAD), F32)] * 6,
        compiler_params=_params(("parallel",)),
    )(*outs, *lses)


def _da_bwd_prep(dout, o, lse, *, name):
    rows = o.shape[0]
    rb = min(512, rows)
    wide = DA_GH * HEAD

    def body(d_ref, o_ref, l_ref, *outs):
        d_scr, l_scr, delta_scr = outs[9:12]
        for h in range(DA_GH):
            cs = slice(h * HEAD, (h + 1) * HEAD)
            dv = d_ref[:, cs]
            d_scr[...] = dv
            l_scr[...] = l_ref[:, cs]
            delta_scr[...] = _lane_sum(dv * o_ref[:, cs])
            for g, (d, _) in enumerate(DA_GROUPS):
                tn = rb // d
                for r in range(d):
                    outs[3 * g][h, r] = _strided_rows(d_scr, r, tn, d).astype(BF16)
                    outs[3 * g + 1][h, r] = _strided_rows(l_scr, r, tn, d)
                    outs[3 * g + 2][h, r] = _strided_rows(delta_scr, r, tn, d)

    nat = pl.BlockSpec((rb, wide), lambda i: (i, 0))
    out_specs, out_shape = [], []
    for d, _ in DA_GROUPS:
        for dt in (BF16, F32, F32):
            out_specs.append(pl.BlockSpec((DA_GH, d, rb // d, HEAD), lambda i: (0, 0, i, 0)))
            out_shape.append(jax.ShapeDtypeStruct((DA_GH, d, rows // d, HEAD), dt))
    return pl.pallas_call(
        body, name=name, grid=(rows // rb,),
        in_specs=[nat, nat, nat], out_specs=out_specs, out_shape=out_shape,
        scratch_shapes=[pltpu.VMEM((rb, HEAD), F32)] * 3,
        compiler_params=_params(("parallel",)),
    )(dout, o, lse)


def _da_prep_bwd(dqr, dkr, dvr, proj, q_gain, k_gain, g, *, name):
    d = DA_GROUPS[g][0]
    rows = proj.shape[0]
    rb = min(1024, rows)
    tn = rb // d
    wide = DA_GH * HEAD

    def body(dq_ref, dk_ref, dv_ref, q_ref, k_ref, qg_ref, kg_ref, oq_ref, ok_ref, ov_ref, gq_ref, gk_ref, *nat_refs):
        @pl.when(pl.program_id(0) == 0)
        def _():
            gq_ref[...] = jnp.zeros_like(gq_ref)
            gk_ref[...] = jnp.zeros_like(gk_ref)

        for h in range(DA_GH):
            cs = slice(h * HEAD, (h + 1) * HEAD)
            for j, src in enumerate((dq_ref, dk_ref, dv_ref)):
                for r in range(d):
                    _store_strided(nat_refs[j], r, tn, d, src[h, r])
            ov_ref[:, cs] = nat_refs[2][...].astype(BF16)
            for j, (x_ref, gn_ref, out_ref, acc_ref) in enumerate(((q_ref, qg_ref, oq_ref, gq_ref),
                                                                    (k_ref, kg_ref, ok_ref, gk_ref))):
                xhat, r = _rms(x_ref[:, cs])
                dx, dgr = _rms_bwd(nat_refs[j][...], xhat, r, gn_ref[...])
                out_ref[:, cs] = dx.astype(BF16)
                acc_ref[...] += jnp.sum(dgr, axis=0, keepdims=True)

    res = pl.BlockSpec((DA_GH, d, tn, HEAD), lambda i: (0, 0, i, 0))
    col = lambda off: pl.BlockSpec((rb, wide), lambda i: (i, off // wide + g))
    vec = pl.BlockSpec((1, HEAD), lambda i: (0, 0))
    nat = pl.BlockSpec((rb, wide), lambda i: (i, 0))
    shape = jax.ShapeDtypeStruct((rows, wide), BF16)
    gshape = jax.ShapeDtypeStruct((1, HEAD), F32)
    return pl.pallas_call(
        body, name=name, grid=(rows // rb,),
        in_specs=[res, res, res, col(C_DQ), col(C_DK), vec, vec],
        out_specs=[nat, nat, nat, vec, vec], out_shape=[shape, shape, shape, gshape, gshape],
        scratch_shapes=[pltpu.VMEM((rb, HEAD), F32)] * 3,
        compiler_params=_params(("arbitrary",)),
    )(dqr, dkr, dvr, proj, proj, q_gain, k_gain)


def _mem_fwd(proj, kv, q_gain, k_gain, *, name):
    rows = proj.shape[0]
    tm = min(1024, rows)
    n_mem = kv.shape[0]

    def body(q_ref, k_ref, v_ref, qg_ref, kg_ref, o_ref):
        qhat, _ = _rms(q_ref[...])
        khat, _ = _rms(k_ref[...])
        s = _dot_nt(qhat * qg_ref[...], khat * kg_ref[...]) * ATT_SCALE
        p = jnp.exp(s - jnp.max(s, axis=-1, keepdims=True))
        p = p / jnp.sum(p, axis=-1, keepdims=True)
        o_ref[...] = _dot(p, v_ref[...]).astype(BF16)

    vec = pl.BlockSpec((1, HEAD), lambda i, h: (0, 0))
    return pl.pallas_call(
        body, name=name, grid=(rows // tm, MEM_HEADS),
        in_specs=[pl.BlockSpec((tm, HEAD), lambda i, h: (i, C_MQ // HEAD + h)),
                  pl.BlockSpec((n_mem, HEAD), lambda i, h: (0, h)),
                  pl.BlockSpec((n_mem, HEAD), lambda i, h: (0, MEM_HEADS + h)), vec, vec],
        out_specs=pl.BlockSpec((tm, HEAD), lambda i, h: (i, h)),
        out_shape=jax.ShapeDtypeStruct((rows, MEM_HEADS * HEAD), BF16),
        compiler_params=_params(("parallel", "parallel")),
    )(proj, kv, kv, q_gain, k_gain)


def _mem_bwd(dout, proj, kv, q_gain, k_gain, dproj, *, name):
    rows = proj.shape[0]
    tm = min(1024, rows)
    steps = rows // tm
    n_mem = kv.shape[0]

    def body(d_ref, q_ref, k_ref, v_ref, qg_ref, kg_ref, dp_in, dq_ref, dk_ref, dv_ref, gq_ref, gk_ref, dkn_ref):
        del dp_in
        h, i = pl.program_id(0), pl.program_id(1)

        @pl.when((h == 0) & (i == 0))
        def _():
            gq_ref[...] = jnp.zeros_like(gq_ref)
            gk_ref[...] = jnp.zeros_like(gk_ref)

        @pl.when(i == 0)
        def _():
            dkn_ref[...] = jnp.zeros_like(dkn_ref)
            dv_ref[...] = jnp.zeros_like(dv_ref)

        qhat, rq = _rms(q_ref[...])
        khat, rk = _rms(k_ref[...])
        qn, kn = qhat * qg_ref[...], khat * kg_ref[...]
        s = _dot_nt(qn, kn) * ATT_SCALE
        p = jnp.exp(s - jnp.max(s, axis=-1, keepdims=True))
        p = p / jnp.sum(p, axis=-1, keepdims=True)
        dout = d_ref[...]
        dp = _dot_nt(dout, v_ref[...])
        ds = p * (dp - jnp.sum(p * dp, axis=-1, keepdims=True))
        dv_ref[...] += _dot_tn(p, dout)
        dkn_ref[...] += _dot_tn(ds, qn) * ATT_SCALE
        dq, dgr = _rms_bwd(_dot(ds, kn) * ATT_SCALE, qhat, rq, qg_ref[...])
        dq_ref[...] = dq.astype(BF16)
        gq_ref[...] += jnp.sum(dgr, axis=0, keepdims=True)

        @pl.when(i == steps - 1)
        def _():
            dk, dgk = _rms_bwd(dkn_ref[...], khat, rk, kg_ref[...])
            dk_ref[...] = dk
            gk_ref[...] += jnp.sum(dgk, axis=0, keepdims=True)

    vec = pl.BlockSpec((1, HEAD), lambda h, i: (0, 0))
    memh = pl.BlockSpec((n_mem, HEAD), lambda h, i: (0, h))
    tok = pl.BlockSpec((tm, HEAD), lambda h, i: (i, h))
    mq = pl.BlockSpec((tm, HEAD), lambda h, i: (i, C_MQ // HEAD + h))
    gshape = jax.ShapeDtypeStruct((1, HEAD), F32)
    return pl.pallas_call(
        body, name=name, grid=(MEM_HEADS, steps),
        in_specs=[tok, mq, memh, pl.BlockSpec((n_mem, HEAD), lambda h, i: (0, MEM_HEADS + h)), vec, vec, ANY],
        out_specs=[mq, memh, memh, vec, vec],
        out_shape=[jax.ShapeDtypeStruct(dproj.shape, dproj.dtype),
                   jax.ShapeDtypeStruct((n_mem, MEM_HEADS * HEAD), F32),
                   jax.ShapeDtypeStruct((n_mem, MEM_HEADS * HEAD), F32), gshape, gshape],
        input_output_aliases={6: 0},
        scratch_shapes=[pltpu.VMEM((n_mem, HEAD), F32)],
        compiler_params=_params(("arbitrary", "arbitrary")),
    )(dout, proj, kv, kv, q_gain, k_gain, dproj)


def _branch_fwd(o_hg, o_da, o_mem, proj, wp_hg, wp_da, wp_mem, *, name):
    rows = o_hg.shape[0]
    tm = min(512, rows)

    def body(a_ref, b_ref, c_ref, ga_ref, gb_ref, gc_ref, wa_ref, wb_ref, wc_ref, o_ref):
        merged = _sigmoid(ga_ref[...]) * _dot(a_ref[...], wa_ref[...])
        merged += _sigmoid(gb_ref[...]) * _dot(b_ref[...], wb_ref[...])
        merged += _sigmoid(gc_ref[...]) * _dot(c_ref[...], wc_ref[...])
        o_ref[...] = merged.astype(BF16)

    row = lambda w: pl.BlockSpec((tm, w), lambda i: (i, 0))
    gate = lambda off: pl.BlockSpec((tm, D_MODEL), lambda i: (i, off // D_MODEL))
    full = lambda a: pl.BlockSpec(a.shape, lambda i: (0, 0))
    return pl.pallas_call(
        body, name=name, grid=(rows // tm,),
        in_specs=[row(o_hg.shape[1]), row(o_da.shape[1]), row(o_mem.shape[1]),
                  gate(C_GHG), gate(C_GDA), gate(C_GMEM), full(wp_hg), full(wp_da), full(wp_mem)],
        out_specs=row(D_MODEL), out_shape=jax.ShapeDtypeStruct((rows, D_MODEL), BF16),
        compiler_params=_params(("parallel",)),
    )(o_hg, o_da, o_mem, proj, proj, proj, wp_hg, wp_da, wp_mem)


def _branch_bwd(dm, o_hg, o_da, o_mem, proj, wp_hg, wp_da, wp_mem, *, name):
    rows = o_hg.shape[0]
    tm = min(256, rows)

    def body(dm_ref, a_ref, b_ref, c_ref, ga_ref, gb_ref, gc_ref, wa_ref, wb_ref, wc_ref, dp_ref, *outs):
        dmv = dm_ref[...]
        for j, (o_ref, g_ref, w_ref) in enumerate(((a_ref, ga_ref, wa_ref), (b_ref, gb_ref, wb_ref),
                                                   (c_ref, gc_ref, wc_ref))):
            z = _dot(o_ref[...], w_ref[...])
            gs = _sigmoid(g_ref[...])
            dz = (dmv * gs).astype(BF16)
            dp_ref[:, j * D_MODEL:(j + 1) * D_MODEL] = (dmv * z * gs * (1.0 - gs)).astype(BF16)
            outs[2 * j][...] = dz
            outs[2 * j + 1][...] = _dot_nt(dz, w_ref[...])

    row = lambda w: pl.BlockSpec((tm, w), lambda i: (i, 0))
    gate = lambda off: pl.BlockSpec((tm, D_MODEL), lambda i: (i, off // D_MODEL))
    full = lambda a: pl.BlockSpec(a.shape, lambda i: (0, 0))
    out_specs = [pl.BlockSpec((pl.Element(tm), pl.Element(3 * D_MODEL)), lambda i: (i * tm, C_GHG))]
    out_shape = [jax.ShapeDtypeStruct((rows, IN_COLS), BF16)]
    for o in (o_hg, o_da, o_mem):
        out_specs += [row(D_MODEL), row(o.shape[1])]
        out_shape += [jax.ShapeDtypeStruct((rows, D_MODEL), BF16), jax.ShapeDtypeStruct((rows, o.shape[1]), F32)]
    return pl.pallas_call(
        body, name=name, grid=(rows // tm,),
        in_specs=[row(D_MODEL), row(o_hg.shape[1]), row(o_da.shape[1]), row(o_mem.shape[1]),
                  gate(C_GHG), gate(C_GDA), gate(C_GMEM), full(wp_hg), full(wp_da), full(wp_mem)],
        out_specs=out_specs, out_shape=out_shape,
        compiler_params=_params(("parallel",)),
    )(dm, o_hg, o_da, o_mem, proj, proj, proj, wp_hg, wp_da, wp_mem)


def _ffn_in(h2, w_ab, *, name):
    rows, dff = h2.shape[0], w_ab.shape[1] // 2
    tm, tn = min(2048, rows), 256

    def body(h_ref, wa_ref, wb_ref, a_ref, b_ref, u_ref):
        a = _dot(h_ref[...], wa_ref[...])
        b = _dot(h_ref[...], wb_ref[...])
        a_ref[...] = a.astype(BF16)
        b_ref[...] = b.astype(BF16)
        u_ref[...] = (a * _sigmoid(a) * b).astype(BF16)

    out = pl.BlockSpec((tm, tn), lambda i, j: (i, j))
    return pl.pallas_call(
        body, name=name, grid=(rows // tm, dff // tn),
        in_specs=[pl.BlockSpec((tm, D_MODEL), lambda i, j: (i, 0)),
                  pl.BlockSpec((D_MODEL, tn), lambda i, j: (0, j)),
                  pl.BlockSpec((D_MODEL, tn), lambda i, j: (0, dff // tn + j))],
        out_specs=[out, out, out],
        out_shape=[jax.ShapeDtypeStruct((rows, dff), BF16)] * 3,
        compiler_params=_params(("parallel", "parallel")),
    )(h2, w_ab, w_ab)


def _ffn_act_bwd(dy, w_out, a, b, *, name):
    rows, dff = a.shape
    tm, tn = min(2048, rows), 256

    def body(dy_ref, w_ref, a_ref, b_ref, da_ref, db_ref):
        du = _dot_nt(dy_ref[...], w_ref[...])
        av, bv = a_ref[...].astype(F32), b_ref[...].astype(F32)
        sa = _sigmoid(av)
        da_ref[...] = (du * bv * sa * (1.0 + av * (1.0 - sa))).astype(BF16)
        db_ref[...] = (du * av * sa).astype(BF16)

    tile = pl.BlockSpec((tm, tn), lambda i, j: (i, j))
    return pl.pallas_call(
        body, name=name, grid=(rows // tm, dff // tn),
        in_specs=[pl.BlockSpec((tm, D_MODEL), lambda i, j: (i, 0)),
                  pl.BlockSpec((tn, D_MODEL), lambda i, j: (j, 0)), tile, tile],
        out_specs=[tile, tile],
        out_shape=[jax.ShapeDtypeStruct((rows, dff), BF16), jax.ShapeDtypeStruct((rows, dff), BF16)],
        compiler_params=_params(("parallel", "parallel")),
    )(dy, w_out, a, b)


def _lower_bound(lb_fw, lb_bw, *, name):
    def body(a_ref, b_ref, oa_ref, ob_ref):
        for src, dst in ((a_ref, oa_ref), (b_ref, ob_ref)):
            dst[...] = _sigmoid(src[0:1, :] - src[1:2, :])

    shape = jax.ShapeDtypeStruct((1, lb_fw.shape[1]), F32)
    return pl.pallas_call(body, name=name, out_shape=[shape, shape])(lb_fw, lb_bw)


def _local_step(x, mem, tgt, p, ex):
    rows = x.shape[0]
    lb_fw, lb_bw = _lower_bound(p["lb_logits_fw"], p["lb_logits_bw"], name="lower_bound")

    h = _rms_fwd(x, p["norm_mix_gain"], name="norm_mix")
    w = {"w_in": ex.w_in}
    proj, carried = _matmul([(h, w["w_in"])], "nn", BF16, tm=1024, tn=1024, carry=ex.late_carry(), name="proj_in")
    w.update(ex.late_weights(carried))
    o_fw, st_fw = _gla_fwd(proj, lb_fw, f_off=C_FF, rev=False, name="gla_fwd_fw")
    o_bw, st_bw = _gla_fwd(proj, lb_bw, f_off=C_FB, rev=True, name="gla_fwd_bw")
    o_hg = _hg_out_fwd(o_fw, o_bw, proj, p["hg_norm_gain"], name="hg_out")

    qkv_r, outs, lses = [], [], []
    for g in range(len(DA_GROUPS)):
        qr, kr, vr = _da_prep(proj, p["da_q_gain"], p["da_k_gain"], g, name=f"da_prep{g}")
        og, lg = _band_fwd(qr, kr, vr, g, name=f"band_fwd{g}")
        qkv_r.append((qr, kr, vr))
        outs.append(og)
        lses.append(lg)
    o_da, lse_da = _da_merge(outs, lses, rows, name="da_merge")

    mem_n = _rms_fwd(mem, p["norm_mem_gain"], name="norm_mem")
    kv = _matmul([(mem_n, w["w_mem_kv"])], "nn", F32, tm=256, tn=512, name="mem_kv")
    o_mem = _mem_fwd(proj, kv, p["mem_q_gain"], p["mem_k_gain"], name="mem_attn")

    merged = _branch_fwd(o_hg, o_da, o_mem, proj, w["w_proj_hg"], w["w_proj_da"], w["w_proj_mem"],
                         name="branch_merge")
    x1, h2 = _out_norm(merged, w["w_out"], x, p["norm_ffn_gain"], name="mix_out_norm")
    a, b, u = _ffn_in(h2, w["w_ffn_in"], name="ffn_in")
    dy, dy_b, loss = _out_loss(u, w["w_ffn_out"], x1, tgt, name="ffn_out_loss")

    gw, gwb, gs = {}, {}, {}
    gw["w_ffn_out"], gwb["w_ffn_out"] = _matmul([(u, dy_b)], "tn", F32, tm=256, tn=1024, also=BF16, name="g_ffn_out")
    da, db = _ffn_act_bwd(dy_b, w["w_ffn_out"], a, b, name="ffn_act_bwd")
    gw["w_ffn_a"], gwb["w_ffn_a"] = _matmul([(h2, da)], "tn", F32, tm=1024, tn=256, also=BF16, name="g_ffn_a")
    gw["w_ffn_b"], gwb["w_ffn_b"] = _matmul([(h2, db)], "tn", F32, tm=1024, tn=256, also=BF16, name="g_ffn_b")
    dh2 = _matmul([(da, w["w_ffn_in"], 0), (db, w["w_ffn_in"], 1)], "nt", F32, tm=256, tn=1024, name="d_h2")
    dx1, dx1_b, gs["norm_ffn_gain"] = _rms_bwd_rows(dh2, x1, p["norm_ffn_gain"], dy, (F32, BF16), name="norm_ffn_bwd")
    gw["w_out"], gwb["w_out"] = _matmul([(merged, dx1_b)], "tn", F32, tm=512, tn=512, also=BF16, name="g_out")
    dmerged = _matmul([(dx1_b, w["w_out"])], "nt", F32, tm=512, tn=512, name="d_merged")
    dproj, dz_hg, do_hg, dz_da, do_da, dz_mem, do_mem = _branch_bwd(
        dmerged, o_hg, o_da, o_mem, proj, w["w_proj_hg"], w["w_proj_da"], w["w_proj_mem"], name="branch_bwd")
    gw["w_proj_hg"], gwb["w_proj_hg"] = _matmul([(o_hg, dz_hg)], "tn", F32, tm=512, tn=512, also=BF16, name="g_proj_hg")
    gw["w_proj_da"], gwb["w_proj_da"] = _matmul([(o_da, dz_da)], "tn", F32, tm=512, tn=512, also=BF16, name="g_proj_da")
    gw["w_proj_mem"], gwb["w_proj_mem"] = _matmul([(o_mem, dz_mem)], "tn", F32, tm=512, tn=512, also=BF16, name="g_proj_mem")

    dproj, dk_mem, dv_mem, gs["mem_q_gain"], gs["mem_k_gain"] = _mem_bwd(
        do_mem, proj, kv, p["mem_q_gain"], p["mem_k_gain"], dproj, name="mem_attn_bwd")
    dkv = jnp.concatenate([dk_mem, dv_mem], axis=1).astype(BF16)
    gw["w_mem_kv"], gwb["w_mem_kv"] = _matmul([(mem_n, dkv)], "tn", F32, tm=512, tn=512, also=BF16, name="g_mem_kv")
    dmem_n = _matmul([(dkv, w["w_mem_kv"])], "nt", F32, tm=256, tn=512, name="d_mem_n")
    (gs["norm_mem_gain"],) = _rms_bwd_rows(dmem_n, mem, p["norm_mem_gain"], None, (), name="norm_mem_bwd")

    prep = _da_bwd_prep(do_da, o_da, lse_da, name="da_bwd_prep")
    d_da, gq_parts, gk_parts = [], [], []
    for g in range(len(DA_GROUPS)):
        qr, kr, vr = qkv_r[g]
        dor, lser, deltar = prep[3 * g:3 * g + 3]
        dqr, dkr, dvr = _band_bwd(qr, kr, vr, dor, lser, deltar, g, name=f"band_bwd{g}")
        dq, dk, dv, gq, gk = _da_prep_bwd(dqr, dkr, dvr, proj, p["da_q_gain"], p["da_k_gain"], g,
                                          name=f"da_prep_bwd{g}")
        d_da.append((dq, dk, dv))
        gq_parts.append(gq)
        gk_parts.append(gk)
    for j, (off, tag) in enumerate(((C_DQ, "q"), (C_DK, "k"), (C_DV, "v"))):
        dproj = _fill_columns(dproj, [t[j] for t in d_da], off, name=f"dproj_fill_da_{tag}")

    dproj, do_gla, gs["hg_norm_gain"] = _hg_out_bwd(do_hg, o_fw, o_bw, proj, p["hg_norm_gain"], dproj,
                                                    name="hg_out_bwd")
    carry_bulk, carry_rest = ex.early_carries(gw, gwb)
    (dq_f, dproj, dv_f, dlb_fw), bulk = _gla_bwd(proj, lb_fw, do_gla, st_fw, None, dproj, f_off=C_FF, rev=False,
                                                 carry=carry_bulk, name="gla_bwd_fw")
    (dproj, dfl_bw, dv_hg, dlb_bw), rest = _gla_bwd(proj, lb_bw, do_gla, st_bw, (dq_f, dv_f), dproj, f_off=C_FB,
                                                    rev=True, carry=carry_rest, name="gla_bwd_bw")
    ex.early_done(bulk, rest)
    dproj = _fill_columns(dproj, [dfl_bw, dv_hg], C_FB, name="dproj_fill")
    g_in, g_in_b = _matmul([(h, dproj)], "tn", F32, tm=1024, tn=512, also=BF16, name="g_in")
    dh, carried = _matmul([(dproj, w["w_in"])], "nt", F32, tm=1024, tn=1024, tk=IN_COLS // 8,
                          carry=ex.w_in_carry(g_in, g_in_b), name="d_h")
    ex.w_in_done(carried)
    grad_x, gs["norm_mix_gain"] = _rms_bwd_rows(dh, x, p["norm_mix_gain"], dx1, (F32,), name="norm_mix_bwd")

    small = _small_pack(gs, gq_parts, gk_parts, dlb_fw, dlb_bw, lb_fw, lb_bw, loss, name="small_pack")
    return grad_x, small


def _small_pack(gs, gq_parts, gk_parts, dlb_fw, dlb_bw, lb_fw, lb_bw, loss, *, name):
    def body(g_mix, g_mem, g_ffn, dfw, dbw, lfw, lbw, g_hg, q0, q1, q2, k0, k1, k2, g_mq, g_mk, loss_ref, o_ref):
        o_ref[0:1, :] = g_mix[...]
        o_ref[1:2, :] = g_mem[...]
        o_ref[2:3, :] = g_ffn[...]
        for base, d_ref, l_ref in ((3, dfw, lfw), (5, dbw, lbw)):
            lbv = l_ref[...]
            t = d_ref[...] * lbv * (1.0 - lbv)
            o_ref[base:base + 1, :] = t
            o_ref[base + 1:base + 2, :] = -t
        o_ref[7:8, :] = jnp.zeros((1, D_MODEL), F32)
        o_ref[7:8, 0:HEAD] = g_hg[...]
        o_ref[7:8, HEAD:2 * HEAD] = q0[...] + q1[...] + q2[...]
        o_ref[7:8, 2 * HEAD:3 * HEAD] = k0[...] + k1[...] + k2[...]
        o_ref[7:8, 3 * HEAD:4 * HEAD] = g_mq[...]
        o_ref[7:8, 4 * HEAD:5 * HEAD] = g_mk[...]
        o_ref[7:8, LOSS_COL:LOSS_COL + HEAD] = loss_ref[...]

    return pl.pallas_call(body, name=name, out_shape=jax.ShapeDtypeStruct((8, D_MODEL), F32))(
        gs["norm_mix_gain"], gs["norm_mem_gain"], gs["norm_ffn_gain"], dlb_fw, dlb_bw, lb_fw, lb_bw,
        gs["hg_norm_gain"], *gq_parts, *gk_parts, gs["mem_q_gain"], gs["mem_k_gain"], loss)


def _row_tile(rows, cols, n_arrays):
    budget = (16 * 1024 * 1024) // (2 * 4 * cols * n_arrays)
    tr = rows
    while tr > budget and tr % 2 == 0 and (tr // 2) % 16 == 0:
        tr //= 2
    return tr


def _cast_into_full(a, chip, rows, cols, axis, *, name):
    sr, sc = a.shape
    tr = _row_tile(sr, sc, 2)

    def body(chip_ref, a_ref, o_ref):
        del chip_ref
        o_ref[...] = a_ref[...].astype(BF16)

    if axis == 1:
        out_map = lambda i, chip_ref: (i, chip_ref[0])
    else:
        out_map = lambda i, chip_ref: (chip_ref[0] * (sr // tr) + i, 0)
    return pl.pallas_call(
        body, name=name,
        grid_spec=pltpu.PrefetchScalarGridSpec(
            num_scalar_prefetch=1, grid=(sr // tr,),
            in_specs=[pl.BlockSpec((tr, sc), lambda i, chip_ref: (i, 0))],
            out_specs=pl.BlockSpec((tr, sc), out_map)),
        out_shape=jax.ShapeDtypeStruct((rows, cols), BF16),
        compiler_params=_params(("parallel",)))(chip, a)


def _add_halves(items, *, name):
    rows = items[0][3].shape[0]
    widths = [ra.shape[1] for _, _, _, ra in items]
    tr = _row_tile(rows, sum(widths), 4)

    def body(*refs):
        o_ref = refs[-1]
        first = lax.axis_index("c") == 0
        off = 0
        for j, wd in enumerate(widths):
            h0, h1, ra = refs[3 * j:3 * j + 3]
            o_ref[:, off:off + wd] = (jnp.where(first, h0[...], h1[...]) + ra[...]).astype(BF16)
            off += wd

    in_specs, ins = [], []
    for (g, haxis, hsize, ra), wd in zip(items, widths):
        if haxis == 0:
            in_specs += [pl.BlockSpec((tr, wd), lambda i: (i, 0)),
                         pl.BlockSpec((tr, wd), lambda i, o=hsize // tr: (o + i, 0))]
        else:
            in_specs += [pl.BlockSpec((tr, wd), lambda i: (i, 0)), pl.BlockSpec((tr, wd), lambda i: (i, 1))]
        in_specs.append(pl.BlockSpec((tr, wd), lambda i: (i, 0)))
        ins += [g, g, ra]
    return pl.pallas_call(body, name=name, grid=(rows // tr,), in_specs=in_specs,
                          out_specs=pl.BlockSpec((tr, sum(widths)), lambda i: (i, 0)),
                          out_shape=jax.ShapeDtypeStruct((rows, sum(widths)), BF16),
                          compiler_params=_params(("parallel",)))(*ins)


def _add_slots(rb, *, name):
    _, rows, cols = rb.shape
    tr = _row_tile(rows, cols, 5)

    def body(r0, r1, r2, r3, o_ref):
        o_ref[...] = ((r0[...].astype(F32) + r1[...].astype(F32)) + r2[...].astype(F32)) + r3[...].astype(F32)

    slot = lambda s: pl.BlockSpec((None, tr, cols), lambda i: (s, i, 0))
    return pl.pallas_call(body, name=name, grid=(rows // tr,), in_specs=[slot(s) for s in range(4)],
                          out_specs=pl.BlockSpec((tr, cols), lambda i: (i, 0)),
                          out_shape=jax.ShapeDtypeStruct((rows, cols), F32),
                          compiler_params=_params(("parallel",)))(rb, rb, rb, rb)


def _adamw(w, g, m, v, *, name):
    rows, cols = w.shape
    tr = _row_tile(rows, cols, 8) if rows % 16 == 0 else rows
    c1 = 1.0 - ADAM_B1 ** ADAM_STEP
    c2 = 1.0 - ADAM_B2 ** ADAM_STEP

    def body(w_ref, g_ref, m_ref, v_ref, d_ref, mo_ref, vo_ref, go_ref):
        gv = g_ref[...]
        go_ref[...] = gv
        mn = ADAM_B1 * m_ref[...] + (1.0 - ADAM_B1) * gv
        vn = ADAM_B2 * v_ref[...] + (1.0 - ADAM_B2) * (gv * gv)
        mo_ref[...] = mn
        vo_ref[...] = vn
        d_ref[...] = -ADAM_LR * ((mn / c1) / (jnp.sqrt(vn / c2) + ADAM_EPS) + ADAM_WD * w_ref[...])

    spec = pl.BlockSpec((tr, cols), lambda i: (i, 0))
    shape = jax.ShapeDtypeStruct((rows, cols), F32)
    return pl.pallas_call(body, name=name, grid=(rows // tr,), in_specs=[spec] * 4, out_specs=[spec] * 4,
                        out_shape=[shape] * 4, compiler_params=_params(("parallel",)))(w, g, m, v)


W_SPECS = (
    ("w_in", 1024, IN_COLS, 1, IN_COLS // 4),
    ("w_mem_kv", 1024, 1024, 0, 256),
    ("w_proj_hg", 1024, 1024, 0, 256),
    ("w_proj_da", 512, 1024, 1, 256),
    ("w_proj_mem", 512, 1024, 1, 256),
    ("w_out", 1024, 1024, 0, 256),
    ("w_ffn_in", 1024, 2 * D_FF, 1, 2 * D_FF // 4),
    ("w_ffn_out", D_FF, 1024, 0, D_FF // 4),
)
CHIP_FLIPS = ((1, 0), (0, 1), (1, 1))
ANY = pl.BlockSpec(memory_space=pl.ANY)
DMA_CHUNK_BYTES = 1 << 20
STAGE_BYTES = 2 << 20


def _place():
    x, y, c = lax.axis_index("x"), lax.axis_index("y"), lax.axis_index("c")
    return x, y, c, 2 * x + y


def _flip(v, f):
    return 1 - v if f else v


def _slab(ref, axis, idx, size):
    start = pl.multiple_of(idx * size, size)
    return ref.at[pl.ds(start, size), :] if axis == 0 else ref.at[:, pl.ds(start, size)]


def _chunked(make, src, dst, want="both"):
    rows, cols = src.shape
    row_bytes = cols * jnp.dtype(src.dtype).itemsize
    k = 1
    while rows % (2 * k) == 0 and (rows // (2 * k)) % 16 == 0 and (rows // k) * row_bytes > DMA_CHUNK_BYTES:
        k *= 2
    cr = rows // k
    parts = []
    if want != "wait":
        parts = [make(src.at[pl.ds(j * cr, cr), :], dst.at[pl.ds(j * cr, cr), :]) for j in range(k)]
    return parts, (make(src, dst) if want != "start" else None)


def _half_spec(rows, cols, axis):
    return (0, rows // 2) if axis == 1 else (1, cols // 2)


def _staged(src, remote_dst, local_dst, sibling, load_sems, send_sems, store_sems, recv_sem):
    rows, cols = src.shape
    row_bytes = cols * jnp.dtype(src.dtype).itemsize
    k = 1
    while (rows // k) * row_bytes > STAGE_BYTES and rows % (2 * k) == 0 and (rows // (2 * k)) % 16 == 0:
        k *= 2
    cr = rows // k
    piece = lambda ref, j: ref.at[pl.ds(j * cr, cr), :]

    def run(buf):
        loads = [pltpu.make_async_copy(piece(src, j), buf.at[j % 2], load_sems.at[j % 2]) for j in range(k)]
        outs = [[pltpu.make_async_remote_copy(src_ref=buf.at[j % 2], dst_ref=piece(remote_dst, j),
                                              send_sem=send_sems.at[j % 2], recv_sem=recv_sem,
                                              device_id=sibling, device_id_type=MESH)] for j in range(k)]
        if local_dst is not None:
            for j in range(k):
                outs[j].append(pltpu.make_async_copy(buf.at[j % 2], piece(local_dst, j), store_sems.at[j % 2]))

        def drained(j):
            outs[j][0].wait_send()
            for cp in outs[j][1:]:
                cp.wait()

        loads[0].start()
        for j in range(k):
            loads[j].wait()
            for cp in outs[j]:
                cp.start()
            if j + 1 < k:
                if j >= 1:
                    drained(j - 1)
                loads[j + 1].start()
        for j in range(max(0, k - 2), k):
            drained(j)

    pl.run_scoped(run, pltpu.VMEM((2, cr, cols), src.dtype))


def _landed(ref, recv_sem, send_sem):
    pltpu.make_async_remote_copy(src_ref=ref, dst_ref=ref, send_sem=send_sem, recv_sem=recv_sem,
                                 device_id=(lax.axis_index("x"), lax.axis_index("y"), lax.axis_index("c")),
                                 device_id_type=MESH).wait_recv()


def _half_slab(ref, spec, chip, half):
    _, rows, cols, axis, size = spec
    haxis, hsize = _half_spec(rows, cols, axis)
    return _slab(_slab(ref, axis, chip, size), haxis, half, hsize)


def _gather_sends(outs, specs, ici_send, ici_recv, want="both"):
    x, y, c, p = _place()
    sent = []
    for wi, spec in enumerate(specs):
        mine = _half_slab(outs[wi], spec, p, c)
        for k, (fx, fy) in enumerate(CHIP_FLIPS):
            sent.append(_chunked(lambda s, d, j=3 * wi + k, fx=fx, fy=fy: pltpu.make_async_remote_copy(
                src_ref=s, dst_ref=d, send_sem=ici_send.at[j], recv_sem=ici_recv.at[j],
                device_id=(_flip(x, fx), _flip(y, fy), c), device_id_type=MESH), mine, mine, want))
    return sent


def _gather_weights(fulls, specs, *, name):
    n = len(specs)

    def body(*refs):
        outs = refs[n:2 * n]
        ici_send, ici_recv, load_sems, d2d_send, d2d_recv = refs[2 * n:]
        x, y, c, _ = _place()
        sent = _gather_sends(outs, specs, ici_send, ici_recv)
        for parts, _ in sent:
            for cp in parts:
                cp.start()
        for k, (fx, fy) in enumerate(CHIP_FLIPS):
            q = 2 * _flip(x, fx) + _flip(y, fy)
            for wi in range(n):
                sent[3 * wi + k][1].wait_recv()
                got = _half_slab(outs[wi], specs[wi], q, c)
                _staged(got, got, None, (x, y, 1 - c), load_sems, d2d_send, None, d2d_recv.at[3 * wi + k])
        for _, whole in sent:
            whole.wait_send()
        for k, (fx, fy) in enumerate(CHIP_FLIPS):
            q = 2 * _flip(x, fx) + _flip(y, fy)
            for wi in range(n):
                _landed(_half_slab(outs[wi], specs[wi], q, 1 - c), d2d_recv.at[3 * wi + k], d2d_send.at[0])

    return pl.pallas_call(
        body, name=name, in_specs=[ANY] * n, out_specs=[ANY] * n,
        out_shape=[jax.ShapeDtypeStruct(f.shape, f.dtype) for f in fulls],
        input_output_aliases={i: i for i in range(n)},
        scratch_shapes=[pltpu.SemaphoreType.DMA((3 * n,)), pltpu.SemaphoreType.DMA((3 * n,)),
                        pltpu.SemaphoreType.DMA((2,)), pltpu.SemaphoreType.DMA((2,)),
                        pltpu.SemaphoreType.DMA((3 * n,))],
    )(*fulls)


def _gather_chips_carry(fulls, specs):
    n = len(specs)

    def issue(ins, outs, sems, want):
        del ins
        return [(parts, whole, True) for parts, whole in _gather_sends(outs, specs, sems[0], sems[1], want)]

    return _Carry(list(fulls), [jax.ShapeDtypeStruct(f.shape, f.dtype) for f in fulls], {i: i for i in range(n)},
                  [pltpu.SemaphoreType.DMA((3 * n,)), pltpu.SemaphoreType.DMA((3 * n,))], issue)


def _gather_pass_on(fulls, specs, *, name):
    n = len(specs)

    def body(*refs):
        outs = refs[n:2 * n]
        load_sems, d2d_send, d2d_recv = refs[2 * n:2 * n + 3]
        bufs = refs[2 * n + 3:]
        x, y, c, _ = _place()
        loads, sends = [], []
        for k, (fx, fy) in enumerate(CHIP_FLIPS):
            q = 2 * _flip(x, fx) + _flip(y, fy)
            for wi in range(n):
                j = 3 * wi + k
                got = _half_slab(outs[wi], specs[wi], q, c)
                loads.append(pltpu.make_async_copy(got, bufs[j], load_sems.at[j]))
                sends.append(pltpu.make_async_remote_copy(
                    src_ref=bufs[j], dst_ref=got, send_sem=d2d_send.at[j], recv_sem=d2d_recv.at[j],
                    device_id=(x, y, 1 - c), device_id_type=MESH))
        for cp in loads:
            cp.start()
        for load, send in zip(loads, sends):
            load.wait()
            send.start()
        for cp in sends:
            cp.wait_send()
        for k, (fx, fy) in enumerate(CHIP_FLIPS):
            q = 2 * _flip(x, fx) + _flip(y, fy)
            for wi in range(n):
                _landed(_half_slab(outs[wi], specs[wi], q, 1 - c), d2d_recv.at[3 * wi + k], d2d_send.at[0])

    shapes = []
    for _, rows, cols, axis, size in specs:
        shapes += [(rows // 2, size) if axis == 1 else (size, cols // 2)] * 3
    assert sum(math.prod(s) for s in shapes) * 2 <= VMEM_LIMIT_V7X // 2
    stages = [pltpu.VMEM(s, BF16) for s in shapes]
    return pl.pallas_call(
        body, name=name, in_specs=[ANY] * n, out_specs=[ANY] * n,
        out_shape=[jax.ShapeDtypeStruct(f.shape, f.dtype) for f in fulls],
        input_output_aliases={i: i for i in range(n)},
        scratch_shapes=[pltpu.SemaphoreType.DMA((3 * n,))] * 3 + stages,
        compiler_params=pltpu.CompilerParams(vmem_limit_bytes=VMEM_LIMIT_V7X),
    )(*fulls)


def _sibling_exchange(grads, *, name):
    n = len(grads)

    def body(*refs):
        ins, outs = refs[:n], refs[n:2 * n]
        load_sems, send_sems, recv_sems = refs[2 * n:]
        x, y, c, _ = _place()
        for i, (_, haxis, hsize) in enumerate(grads):
            _staged(_slab(ins[i], haxis, 1 - c, hsize), outs[i], None, (x, y, 1 - c),
                    load_sems, send_sems, None, recv_sems.at[i])
        for i in range(n):
            _landed(outs[i], recv_sems.at[i], send_sems.at[0])

    shapes = [jax.ShapeDtypeStruct((hsize, g.shape[1]) if haxis == 0 else (g.shape[0], hsize), g.dtype)
              for g, haxis, hsize in grads]
    return pl.pallas_call(
        body, name=name, in_specs=[ANY] * n, out_specs=[ANY] * n, out_shape=shapes,
        scratch_shapes=[pltpu.SemaphoreType.DMA((2,)), pltpu.SemaphoreType.DMA((2,)),
                        pltpu.SemaphoreType.DMA((n,))],
    )(*[g for g, _, _ in grads])


def _chip_exchange_carry(parts):
    n = len(parts)

    def issue(ins, outs, sems, want):
        send_sems, recv_sems, local_sems = sems
        x, y, c, p = _place()
        copies = []
        for i, (_, axis, size) in enumerate(parts):
            copies.append(_chunked(lambda s, d, i=i: pltpu.make_async_copy(s, d, local_sems.at[i]),
                                   _slab(ins[i], axis, p, size), outs[i].at[p], want) + (False,))
            for k, (fx, fy) in enumerate(CHIP_FLIPS):
                px, py = _flip(x, fx), _flip(y, fy)
                copies.append(_chunked(lambda s, d, j=3 * i + k, px=px, py=py: pltpu.make_async_remote_copy(
                    src_ref=s, dst_ref=d, send_sem=send_sems.at[j], recv_sem=recv_sems.at[j],
                    device_id=(px, py, c), device_id_type=MESH),
                    _slab(ins[i], axis, 2 * px + py, size), outs[i].at[p], want) + (True,))
        return copies

    shapes = []
    for a, axis, size in parts:
        shapes.append(jax.ShapeDtypeStruct((4, size, a.shape[1]) if axis == 0 else (4, a.shape[0], size), a.dtype))
    return _Carry([a for a, _, _ in parts], shapes, {},
                  [pltpu.SemaphoreType.DMA((3 * n,)), pltpu.SemaphoreType.DMA((3 * n,)),
                   pltpu.SemaphoreType.DMA((n,))], issue)


def _sibling_share(sums):
    n = len(sums)

    def body(*refs):
        ins, outs = refs[:n], refs[n:2 * n]
        load_sems, send_sems, store_sems, recv_sems = refs[2 * n:]
        x, y, c, _ = _place()
        for i, (s, haxis) in enumerate(sums):
            place = _slab(outs[i], haxis, c, s.shape[haxis])
            _staged(ins[i], place, place, (x, y, 1 - c), load_sems, send_sems, store_sems, recv_sems.at[i])
        for i, (s, haxis) in enumerate(sums):
            _landed(_slab(outs[i], haxis, 1 - c, s.shape[haxis]), recv_sems.at[i], send_sems.at[0])

    shapes = []
    for s, haxis in sums:
        r, cc = s.shape
        shapes.append(jax.ShapeDtypeStruct((2 * r, cc) if haxis == 0 else (r, 2 * cc), F32))
    return pl.pallas_call(
        body, name="grad_sibling_share", in_specs=[ANY] * n, out_specs=[ANY] * n, out_shape=shapes,
        scratch_shapes=[pltpu.SemaphoreType.DMA((2,)), pltpu.SemaphoreType.DMA((2,)),
                        pltpu.SemaphoreType.DMA((2,)), pltpu.SemaphoreType.DMA((n,))],
    )(*[s for s, _ in sums])


class _Exchanges:
    def __init__(self, fulls):
        self.w_in = _gather_weights([fulls["w_in"]], W_SPECS[:1], name="gather_w_in")[0]
        self.late_specs = W_SPECS[1:]
        self.late = [fulls[s[0]] for s in self.late_specs]
        self.slots = {}

    def late_carry(self):
        return _gather_chips_carry(self.late, self.late_specs)

    def late_weights(self, carried):
        done = _gather_pass_on(carried, self.late_specs, name="gather_pass_on")
        return dict(zip([s[0] for s in self.late_specs], done))

    def _half_sums(self, gw, gwb, specs, tag):
        grads, sent = [], []
        for name, _, _, axis, _ in specs:
            for part in (("w_ffn_a", "w_ffn_b") if name == "w_ffn_in" else (name,)):
                half = _half_spec(gw[part].shape[0], gw[part].shape[1], axis)
                grads.append((gw[part],) + half)
                sent.append((gwb[part],) + half)
        theirs = _sibling_exchange(sent, name=f"grad_sibling_exchange_{tag}")
        parts, j = [], 0
        for name, _, _, axis, size in specs:
            take = 2 if name == "w_ffn_in" else 1
            items = [grads[i] + (theirs[i],) for i in range(j, j + take)]
            parts.append((_add_halves(items, name=f"half_sum_{name}"), axis, size))
            j += take
        return parts

    def early_carries(self, gw, gwb):
        parts = self._half_sums(gw, gwb, self.late_specs, "early")
        self.early_names = [s[0] for s in self.late_specs]
        cut = self.early_names.index("w_ffn_in")
        return _chip_exchange_carry(parts[cut:]), _chip_exchange_carry(parts[:cut])

    def early_done(self, carried_bulk, carried_rest):
        cut = self.early_names.index("w_ffn_in")
        self.slots.update(zip(self.early_names[cut:], carried_bulk))
        self.slots.update(zip(self.early_names[:cut], carried_rest))

    def w_in_carry(self, g, gb):
        return _chip_exchange_carry(self._half_sums({"w_in": g}, {"w_in": gb}, W_SPECS[:1], "w_in"))

    def w_in_done(self, carried):
        self.slots["w_in"] = carried[0]

    def reduced(self):
        sums = []
        for name, rows, cols, axis, _ in W_SPECS:
            sums.append((_add_slots(self.slots[name], name=f"chip_sum_{name}"), _half_spec(rows, cols, axis)[0]))
        return dict(zip([s[0] for s in W_SPECS], _sibling_share(sums)))


def _small_allreduce(sv):
    def body(sv_ref, o_ref, slots_ref, send_sems, recv_sems):
        x, y, c, _ = _place()
        me = 4 * x + 2 * y + c
        slots_ref[me] = sv_ref[...]
        copies = []
        for k in range(1, 8):
            fx, fy, fc = (k >> 2) & 1, (k >> 1) & 1, k & 1
            copies.append(pltpu.make_async_remote_copy(
                src_ref=sv_ref, dst_ref=slots_ref.at[me], send_sem=send_sems.at[k - 1],
                recv_sem=recv_sems.at[k - 1], device_id=(_flip(x, fx), _flip(y, fy), _flip(c, fc)),
                device_id_type=MESH))
        for cp in copies:
            cp.start()
        for cp in copies:
            cp.wait_recv()
        for cp in copies:
            cp.wait_send()
        total = slots_ref[0]
        for s in range(1, 8):
            total = total + slots_ref[s]
        o_ref[...] = total

    vm = pl.BlockSpec(memory_space=pltpu.VMEM)
    return pl.pallas_call(
        body, name="small_allreduce", in_specs=[vm], out_specs=vm,
        out_shape=jax.ShapeDtypeStruct(sv.shape, F32),
        scratch_shapes=[pltpu.VMEM((8,) + sv.shape, F32), pltpu.SemaphoreType.DMA((7,)),
                        pltpu.SemaphoreType.DMA((7,))],
    )(sv)


SMALL_ROWS = (("norm_mix_gain", 0), ("norm_mem_gain", 1), ("norm_ffn_gain", 2))
SMALL_LB = (("lb_logits_fw", 3), ("lb_logits_bw", 5))
SMALL_HEAD = ("hg_norm_gain", "da_q_gain", "da_k_gain", "mem_q_gain", "mem_k_gain")
LOSS_COL = HEAD * len(SMALL_HEAD)


def _pack_small(d):
    last = jnp.concatenate([d[n] for n in SMALL_HEAD] + [jnp.zeros((1, D_MODEL - HEAD * len(SMALL_HEAD)), F32)], axis=1)
    return jnp.concatenate([d["norm_mix_gain"], d["norm_mem_gain"], d["norm_ffn_gain"],
                            d["lb_logits_fw"], d["lb_logits_bw"], last], axis=0)


def _unpack_small(a):
    out = {n: a[r:r + 1] for n, r in SMALL_ROWS}
    out.update({n: a[r:r + 2] for n, r in SMALL_LB})
    out.update({n: a[7:8, j * HEAD:(j + 1) * HEAD] for j, n in enumerate(SMALL_HEAD)})
    return out


PARAM_ORDER = ("norm_mix_gain", "norm_mem_gain", "w_in", "lb_logits_fw", "lb_logits_bw", "hg_norm_gain",
               "da_q_gain", "da_k_gain", "w_mem_kv", "mem_q_gain", "mem_k_gain", "w_proj_hg", "w_proj_da",
               "w_proj_mem", "w_out", "norm_ffn_gain", "w_ffn_in", "w_ffn_out")


def kernel(x, mem, norm_mix_gain, norm_mem_gain, w_in, lb_logits_fw, lb_logits_bw, hg_norm_gain, da_q_gain, da_k_gain, w_mem_kv, mem_q_gain, mem_k_gain, w_proj_hg, w_proj_da, w_proj_mem, w_out, norm_ffn_gain, w_ffn_in, w_ffn_out, loss_target, m_norm_mix_gain, m_norm_mem_gain, m_w_in, m_lb_logits_fw, m_lb_logits_bw, m_hg_norm_gain, m_da_q_gain, m_da_k_gain, m_w_mem_kv, m_mem_q_gain, m_mem_k_gain, m_w_proj_hg, m_w_proj_da, m_w_proj_mem, m_w_out, m_norm_ffn_gain, m_w_ffn_in, m_w_ffn_out, v_norm_mix_gain, v_norm_mem_gain, v_w_in, v_lb_logits_fw, v_lb_logits_bw, v_hg_norm_gain, v_da_q_gain, v_da_k_gain, v_w_mem_kv, v_mem_q_gain, v_mem_k_gain, v_w_proj_hg, v_w_proj_da, v_w_proj_mem, v_w_out, v_norm_ffn_gain, v_w_ffn_in, v_w_ffn_out):
    args = dict(locals())
    mats = tuple(s[0] for s in W_SPECS)
    flat = lambda a: a.reshape(a.shape[-2:])
    w = {n: flat(args[n]) for n in mats}
    m = {n: flat(args["m_" + n]) for n in mats}
    v = {n: flat(args["v_" + n]) for n in mats}
    small = {n: args[n] for n in PARAM_ORDER if n not in mats}

    chip = (2 * lax.axis_index("x") + lax.axis_index("y")).astype(jnp.int32).reshape(1)
    ex = _Exchanges({n: _cast_into_full(w[n], chip, rows, cols, axis, name=f"cast_{n}")
                     for n, rows, cols, axis, _ in W_SPECS})
    grad_x, small_grads = _local_step(x[0], mem[0], loss_target[0], small, ex)
    grads = ex.reduced()
    small_sum = _small_allreduce(small_grads)

    delta, new_m, new_v = {}, {}, {}
    for n in mats:
        delta[n], new_m[n], new_v[n], grads[n] = _adamw(w[n], grads[n], m[n], v[n], name=f"adamw_{n}")
    packed = _adamw(_pack_small(small), small_sum,
                    _pack_small({n: args["m_" + n] for n in small}),
                    _pack_small({n: args["v_" + n] for n in small}), name="adamw_small")
    for dst, src in zip((delta, new_m, new_v, grads), packed):
        dst.update(_unpack_small(src))

    def shaped(d, n):
        return d[n].reshape(args[n].shape)

    return (small_sum[7, LOSS_COL], grad_x[None], *[shaped(grads, n) for n in PARAM_ORDER], *[shaped(delta, n) for n in PARAM_ORDER],
            *[shaped(new_m, n) for n in PARAM_ORDER], *[shaped(new_v, n) for n in PARAM_ORDER])
```

```python
import math

import numpy as np
import jax
import jax.numpy as jnp
from jax import lax
from jax.experimental import pallas as pl
from jax.experimental.pallas import tpu as pltpu

F32 = jnp.float32
BF16 = jnp.bfloat16
MESH = pl.DeviceIdType.MESH

D_MODEL = 1024
HEAD = 128
HG_HEADS = 8
DA_GROUPS = ((1, 64), (4, 64), (16, 64))
DA_GH = 4
MEM_HEADS = 4
N_MEM = 256
D_FF = 2816
CHUNK = 64
BAND_QBLOCKS = 16
GLA_HEADS_PER_STEP = 8
RMS_EPS = 1e-6
NEG_INF = -1e30
HG_SCALE = HEAD ** -0.5
ATT_SCALE = HEAD ** -0.5
VMEM_LIMIT_V7X = 48 * 1024 * 1024

C_HQ, C_FF, C_FB, C_HI, C_HG = 0, 1024, 2048, 3072, 4096
C_DQ, C_DK, C_DV, C_MQ = 5120, 6656, 8192, 9728
C_GHG, C_GDA, C_GMEM = 10240, 11264, 12288
IN_COLS = 13312

ADAM_LR, ADAM_B1, ADAM_B2, ADAM_EPS, ADAM_WD, ADAM_STEP = 0.001, 0.9, 0.999, 1e-08, 0.01, 10


def _params(sem, vmem=VMEM_LIMIT_V7X):
    return pltpu.CompilerParams(dimension_semantics=sem, vmem_limit_bytes=vmem)


def _dot(a, b):
    return jnp.dot(a.astype(BF16), b.astype(BF16), preferred_element_type=F32)


def _dot_nt(a, b):
    return lax.dot_general(a.astype(BF16), b.astype(BF16), (((1,), (1,)), ((), ())),
                           preferred_element_type=F32)


def _dot_tn(a, b):
    return lax.dot_general(a.astype(BF16), b.astype(BF16), (((0,), (0,)), ((), ())),
                           preferred_element_type=F32)


def _sigmoid(v):
    return jax.nn.sigmoid(v.astype(F32))


def _ones(rows, cols):
    return (lax.broadcasted_iota(jnp.int32, (rows, cols), 0) >= 0).astype(BF16)


def _lane_sum(v):
    ones = _ones(HEAD, HEAD)
    hi = v.astype(BF16)
    mid = (v - hi.astype(F32)).astype(BF16)
    return jnp.dot(hi, ones, preferred_element_type=F32) + jnp.dot(mid, ones, preferred_element_type=F32)


def _row_mean(v):
    if v.shape[-1] == HEAD:
        return _lane_sum(v) * (1.0 / HEAD)
    return jnp.mean(v, axis=-1, keepdims=True)


def _rms(v):
    v = v.astype(F32)
    r = lax.rsqrt(_row_mean(v * v) + RMS_EPS)
    return v * r, r


def _rms_bwd(dy, xhat, r, gain):
    dxh = dy * gain
    dx = r * (dxh - xhat * _row_mean(dxh * xhat))
    return dx, dy * xhat


class _Carry:
    def __init__(self, arrays, out_shapes, aliases, sems, issue):
        self.arrays, self.out_shapes, self.aliases, self.sems, self.issue = arrays, out_shapes, aliases, sems, issue


NO_CARRY = object()


def _start_copies(copies):
    for parts, _, _ in copies:
        for cp in parts:
            cp.start()


def _wait_copies(copies):
    for _, whole, remote in copies:
        if remote:
            whole.wait_recv()
    for _, whole, remote in copies:
        if remote:
            whole.wait_send()
        else:
            whole.wait()


def _call(body, *, name, grid, in_specs, out_specs, out_shape, scratch_shapes, semantics, ins, carry=None,
          aliases=None):
    aliases = dict(aliases or {})
    if carry is None:
        res = pl.pallas_call(body, name=name, grid=grid, in_specs=in_specs, out_specs=out_specs, out_shape=out_shape,
                             scratch_shapes=scratch_shapes, input_output_aliases=aliases,
                             compiler_params=_params(semantics))(*ins)
        return list(res), []
    n_in, n_out, n_scr = len(ins), len(out_shape), len(scratch_shapes)
    c_in, c_out = len(carry.arrays), len(carry.out_shapes)

    def wrapped(*refs):
        pos = [0]

        def take(count):
            pos[0] += count
            return refs[pos[0] - count:pos[0]]

        own_in, carry_in = take(n_in), take(c_in)
        own_out, carry_out = take(n_out), take(c_out)
        own_scr, carry_sems = take(n_scr), take(len(carry.sems))
        ids = [pl.program_id(a) for a in range(len(grid))]
        first, last = ids[0] == 0, ids[0] == grid[0] - 1
        for a in range(1, len(grid)):
            first, last = first & (ids[a] == 0), last & (ids[a] == grid[a] - 1)

        @pl.when(first)
        def _():
            _start_copies(carry.issue(carry_in, carry_out, carry_sems, "start"))

        body(*own_in, *own_out, *own_scr)

        @pl.when(last)
        def _():
            _wait_copies(carry.issue(carry_in, carry_out, carry_sems, "wait"))

    res = pl.pallas_call(
        wrapped, name=name, grid=grid, in_specs=list(in_specs) + [ANY] * c_in,
        out_specs=list(out_specs) + [ANY] * c_out, out_shape=list(out_shape) + list(carry.out_shapes),
        input_output_aliases={**aliases, **{n_in + i: n_out + j for i, j in carry.aliases.items()}},
        scratch_shapes=list(scratch_shapes) + list(carry.sems),
        compiler_params=_params(("arbitrary",) * len(grid)))(*ins, *carry.arrays)
    return list(res[:n_out]), list(res[n_out:])


def _matmul(pairs, mode, out_dtype, *, tm, tn, tk=None, residual=None, also=None, carry=NO_CARRY, name):
    b_offs = [pr[2] if len(pr) > 2 else 0 for pr in pairs]
    pairs = [pr[:2] for pr in pairs]
    a0, b0 = pairs[0]
    if mode == "nn":
        (m, kk), n = a0.shape, b0.shape[1]
    elif mode == "nt":
        (m, kk), n = a0.shape, b0.shape[0]
    else:
        (kk, m), n = a0.shape, b0.shape[1]
    assert mode == "nt" or not any(b_offs)
    tm, tn = min(tm, m), min(tn, n)
    tk = kk if tk is None else tk
    nk = kk // tk
    assert m % tm == 0 and n % tn == 0 and kk % tk == 0, (name, m, n, kk)
    n_p = len(pairs)
    if mode == "tn":
        a_spec = pl.BlockSpec((tk, tm), lambda i, j, k: (k, i))
    else:
        a_spec = pl.BlockSpec((tm, tk), lambda i, j, k: (i, k))
    if mode == "nt":
        b_specs = [pl.BlockSpec((tn, tk), lambda i, j, k, o=o: (j, o * nk + k)) for o in b_offs]
    else:
        b_specs = [pl.BlockSpec((tk, tn), lambda i, j, k: (k, j))] * n_p
    o_spec = pl.BlockSpec((tm, tn), lambda i, j, k: (i, j))
    dot = {"nn": _dot, "nt": _dot_nt, "tn": _dot_tn}[mode]
    has_res = residual is not None

    def body(*refs):
        a_refs, b_refs = refs[:n_p], refs[n_p:2 * n_p]
        pos = 2 * n_p
        res_ref = refs[pos] if has_res else None
        pos += int(has_res)
        o_ref = refs[pos]
        pos += int(also is not None)
        part = dot(a_refs[0][...], b_refs[0][...])
        for a_r, b_r in zip(a_refs[1:], b_refs[1:]):
            part += dot(a_r[...], b_r[...])

        def finish(total):
            if has_res:
                total = total + res_ref[...]
            o_ref[...] = total.astype(out_dtype)
            if also is not None:
                refs[pos][...] = total.astype(also)

        if nk == 1:
            finish(part)
        else:
            acc_ref = refs[pos + 1]
            k = pl.program_id(2)

            @pl.when(k == 0)
            def _():
                acc_ref[...] = part

            @pl.when(k > 0)
            def _():
                acc_ref[...] += part

            @pl.when(k == nk - 1)
            def _():
                finish(acc_ref[...])

    ins = [a for a, _ in pairs] + [b for _, b in pairs]
    in_specs = [a_spec] * n_p + b_specs
    if has_res:
        ins.append(residual)
        in_specs.append(o_spec)
    dtypes = [out_dtype] + ([also] if also is not None else [])
    outs, carried = _call(
        body, name=name, grid=(m // tm, n // tn, nk), in_specs=in_specs, out_specs=[o_spec] * len(dtypes),
        out_shape=[jax.ShapeDtypeStruct((m, n), dt) for dt in dtypes],
        scratch_shapes=[pltpu.VMEM((tm, tn), F32)] if nk > 1 else [],
        semantics=("parallel", "parallel", "arbitrary"), ins=ins, carry=None if carry is NO_CARRY else carry)
    out = outs[0] if also is None else tuple(outs)
    return out if carry is NO_CARRY else (out, carried)


def _rms_fwd(x, gain, *, name):
    rows, dm = x.shape
    tm = min(1024, rows)

    def body(x_ref, g_ref, h_ref):
        xhat, _ = _rms(x_ref[...])
        h_ref[...] = (xhat * g_ref[...]).astype(BF16)

    return pl.pallas_call(
        body, name=name, grid=(rows // tm,),
        in_specs=[pl.BlockSpec((tm, dm), lambda i: (i, 0)), pl.BlockSpec((1, dm), lambda i: (0, 0))],
        out_specs=pl.BlockSpec((tm, dm), lambda i: (i, 0)),
        out_shape=jax.ShapeDtypeStruct((rows, dm), BF16),
        compiler_params=_params(("parallel",)),
    )(x, gain)


def _rms_bwd_rows(dh, x, gain, dres, dx_dtypes, *, name):
    rows, dm = x.shape
    tm = min(512, rows)
    has_res = dres is not None

    def body(*refs):
        dh_ref, x_ref, g_ref = refs[:3]
        res_ref = refs[3] if has_res else None
        outs = refs[3 + int(has_res):]
        dg_ref = outs[-1]
        xhat, r = _rms(x_ref[...])
        dx, dgr = _rms_bwd(dh_ref[...], xhat, r, g_ref[...])
        if has_res:
            dx = dx + res_ref[...]
        for dx_ref, dt in zip(outs, dx_dtypes):
            dx_ref[...] = dx.astype(dt)

        @pl.when(pl.program_id(0) == 0)
        def _():
            dg_ref[...] = jnp.zeros_like(dg_ref)

        dg_ref[...] += jnp.sum(dgr, axis=0, keepdims=True)

    row = pl.BlockSpec((tm, dm), lambda i: (i, 0))
    vec = pl.BlockSpec((1, dm), lambda i: (0, 0))
    return pl.pallas_call(
        body, name=name, grid=(rows // tm,),
        in_specs=[row, row, vec] + ([row] if has_res else []),
        out_specs=[row] * len(dx_dtypes) + [vec],
        out_shape=[jax.ShapeDtypeStruct((rows, dm), dt) for dt in dx_dtypes] + [jax.ShapeDtypeStruct((1, dm), F32)],
        compiler_params=_params(("arbitrary",)),
    )(*([dh, x, gain] + ([dres] if has_res else [])))


def _out_norm(a, w, x, gain, *, name):
    rows, dm = x.shape
    tm = min(512, rows)

    def body(a_ref, w_ref, x_ref, g_ref, x1_ref, h_ref):
        x1 = x_ref[...] + _dot(a_ref[...], w_ref[...])
        x1_ref[...] = x1
        xhat, _ = _rms(x1)
        h_ref[...] = (xhat * g_ref[...]).astype(BF16)

    row = lambda wd: pl.BlockSpec((tm, wd), lambda i: (i, 0))
    return pl.pallas_call(
        body, name=name, grid=(rows // tm,),
        in_specs=[row(a.shape[1]), pl.BlockSpec(w.shape, lambda i: (0, 0)), row(dm), pl.BlockSpec((1, dm), lambda i: (0, 0))],
        out_specs=[row(dm), row(dm)],
        out_shape=[jax.ShapeDtypeStruct((rows, dm), F32), jax.ShapeDtypeStruct((rows, dm), BF16)],
        compiler_params=_params(("parallel",)),
    )(a, w, x, gain)


def _out_loss(u, w, x1, tgt, *, name):
    rows, dm = x1.shape
    tm = min(256, rows)
    steps = rows // tm

    def body(u_ref, w_ref, x_ref, t_ref, dy_ref, dyb_ref, loss_ref, acc_ref):
        i = pl.program_id(0)
        diff = x_ref[...] + _dot(u_ref[...], w_ref[...]) - t_ref[...]
        dy = diff * (1.0 / dm)
        dy_ref[...] = dy
        dyb_ref[...] = dy.astype(BF16)

        @pl.when(i == 0)
        def _():
            acc_ref[...] = jnp.zeros_like(acc_ref)

        acc_ref[...] += jnp.sum(diff * diff, axis=0, keepdims=True)

        @pl.when(i == steps - 1)
        def _():
            loss_ref[...] = jnp.full((1, HEAD), 0.5 / dm, F32) * jnp.sum(acc_ref[...])

    row = lambda wd: pl.BlockSpec((tm, wd), lambda i: (i, 0))
    return pl.pallas_call(
        body, name=name, grid=(steps,),
        in_specs=[row(u.shape[1]), pl.BlockSpec(w.shape, lambda i: (0, 0)), row(dm), row(dm)],
        out_specs=[row(dm), row(dm), pl.BlockSpec((1, HEAD), lambda i: (0, 0))],
        out_shape=[jax.ShapeDtypeStruct((rows, dm), F32), jax.ShapeDtypeStruct((rows, dm), BF16),
                   jax.ShapeDtypeStruct((1, HEAD), F32)],
        scratch_shapes=[pltpu.VMEM((1, dm), F32)],
        compiler_params=_params(("arbitrary",)),
    )(u, w, x1, tgt)


def _gla_block_terms(q_raw, f_logit, lb, rev):
    sig = _sigmoid(f_logit)
    forget = lb + (1.0 - lb) * sig
    k = 1.0 - forget
    b = _chunk_cumsum(jnp.log(forget), rev)
    qs = _sigmoid(q_raw)
    eb = jnp.exp(b)
    emb = jnp.exp(-b)
    qt = (q_raw * qs * HG_SCALE) * eb
    kt = k * emb
    return sig, forget, k, b, qs, eb, emb, qt, kt


def _chunk_cumsum(v, rev):
    n = v.shape[0]
    pos = lax.broadcasted_iota(jnp.int32, v.shape, 0) & (CHUNK - 1)
    step = 1
    while step < CHUNK:
        if rev:
            shifted, keep = pltpu.roll(v, n - step, 0), pos < CHUNK - step
        else:
            shifted, keep = pltpu.roll(v, step, 0), pos >= step
        v = v + jnp.where(keep, shifted, 0.0)
        step *= 2
    return v


def _tri_mask(n, rev):
    row, col = np.arange(n)[:, None], np.arange(n)[None, :]
    same = (row // CHUNK) == (col // CHUNK)
    return jnp.asarray((same & ((row <= col) if rev else (row >= col))).astype(np.float32))


def _chunk_order(ncb, rev):
    order = range(ncb - 1, -1, -1) if rev else range(ncb)
    return [(c, c * CHUNK if rev else c * CHUNK + CHUNK - 1) for c in order]


def _gla_fwd(proj, lb, *, f_off, rev, name):
    rows = proj.shape[0]
    tb = min(256, rows)
    nb, ncb = rows // tb, tb // CHUNK

    def tmap(n):
        return nb - 1 - n if rev else n

    def body(tri_ref, q_ref, f_ref, v_ref, lb_ref, o_ref, st_ref, s_ref):
        @pl.when(pl.program_id(1) == 0)
        def _():
            s_ref[...] = jnp.zeros_like(s_ref)

        tri = tri_ref[...] > 0.5
        for hh in range(GLA_HEADS_PER_STEP):
            cs = slice(hh * HEAD, (hh + 1) * HEAD)
            v = v_ref[:, cs]
            _, _, k, b, _, _, _, qt, kt = _gla_block_terms(q_ref[:, cs].astype(F32), f_ref[:, cs].astype(F32),
                                                           lb_ref[:, cs], rev)
            qt = qt.astype(BF16)
            o_intra = _dot(jnp.where(tri, _dot_nt(qt, kt), 0.0), v)
            chunks = []
            for c, last in _chunk_order(ncb, rev):
                sl = slice(c * CHUNK, (c + 1) * CHUNK)
                bl = b[last:last + 1, :]
                kh = k[sl] * jnp.exp(bl - b[sl])
                chunks.append((c, sl, jnp.exp(bl), _dot_tn(v[sl], kh)))
            s_t = s_ref[hh]
            for c, sl, ebl, kv in chunks:
                st_ref[hh, c] = s_t
                o_ref[sl, cs] = o_intra[sl] + _dot_nt(qt[sl], s_t)
                s_t = ebl * s_t + kv
            s_ref[hh] = s_t

    hps = GLA_HEADS_PER_STEP
    col = lambda off: pl.BlockSpec((tb, hps * HEAD), lambda h, n: (tmap(n), off // (hps * HEAD) + h))
    return pl.pallas_call(
        body, name=name, grid=(HG_HEADS // hps, nb),
        in_specs=[pl.BlockSpec((tb, tb), lambda h, n: (0, 0)), col(C_HQ), col(f_off), col(C_HI),
                  pl.BlockSpec((1, hps * HEAD), lambda h, n: (0, h))],
        out_specs=[pl.BlockSpec((tb, hps * HEAD), lambda h, n: (tmap(n), h)),
                   pl.BlockSpec((hps, ncb, HEAD, HEAD), lambda h, n: (h, tmap(n), 0, 0))],
        out_shape=[jax.ShapeDtypeStruct((rows, HG_HEADS * HEAD), F32),
                   jax.ShapeDtypeStruct((HG_HEADS, rows // CHUNK, HEAD, HEAD), F32)],
        scratch_shapes=[pltpu.VMEM((hps, HEAD, HEAD), F32)],
        compiler_params=_params(("parallel", "arbitrary")),
    )(_tri_mask(tb, rev), proj, proj, proj, lb)


def _gla_bwd(proj, lb, do, states, prev, dproj, *, f_off, rev, carry=NO_CARRY, name):
    rows = proj.shape[0]
    tb = min(256, rows)
    nb, ncb = rows // tb, tb // CHUNK
    has_prev = prev is not None
    qv_dtype = BF16 if has_prev else F32

    def tmap(n):
        return n if rev else nb - 1 - n

    def body(*refs):
        tri_ref, q_ref, f_ref, v_ref, lb_ref, do_ref, st_ref = refs[:7]
        pq_ref, pv_ref = refs[8:10] if has_prev else (None, None)
        (dq_ref, df_ref, dv_ref, dlb_ref,
         ds_ref, dqt_scr, dk_scr, db_scr, dbl_scr, dv_scr) = refs[8 + 2 * int(has_prev):]

        @pl.when(pl.program_id(1) == 0)
        def _():
            ds_ref[...] = jnp.zeros_like(ds_ref)
            dlb_ref[...] = jnp.zeros_like(dlb_ref)

        tri = tri_ref[...] > 0.5
        for hh in range(GLA_HEADS_PER_STEP):
            cs = slice(hh * HEAD, (hh + 1) * HEAD)
            lbv = lb_ref[:, cs]
            q_raw = q_ref[:, cs].astype(F32)
            v, dout = v_ref[:, cs], do_ref[:, cs].astype(BF16)
            sig, forget, k, b, qs, eb, emb, qt, kt = _gla_block_terms(q_raw, f_ref[:, cs].astype(F32), lbv, rev)
            qt_b, kt_b = qt.astype(BF16), kt.astype(BF16)
            a = jnp.where(tri, _dot_nt(qt_b, kt_b), 0.0)
            da = jnp.where(tri, _dot_nt(dout, v), 0.0).astype(BF16)
            dv_intra = _dot_tn(a, dout)
            dqt_intra = _dot(da, kt_b)
            dkt = _dot_tn(da, qt_b)
            chunks = []
            for c, last in reversed(_chunk_order(ncb, rev)):
                sl = slice(c * CHUNK, (c + 1) * CHUNK)
                bl = b[last:last + 1, :]
                e = jnp.exp(bl - b[sl])
                s_t = st_ref[hh, c]
                dqt_scr[sl, cs] = dqt_intra[sl] + _dot(dout[sl], s_t)
                chunks.append((sl, jnp.exp(bl), e, k[sl] * e, s_t, _dot_tn(dout[sl], qt_b[sl])))
            ds_t = ds_ref[hh]
            for sl, ebl, e, kh, s_t, grow in chunks:
                dkh = _dot(v[sl], ds_t)
                dv_scr[sl, cs] = dv_intra[sl] + _dot_nt(kh, ds_t)
                dk_scr[sl, cs] = dkt[sl] * emb[sl] + dkh * e
                khd = kh * dkh
                dbl = jnp.sum(khd, axis=0, keepdims=True) + ebl * jnp.sum(ds_t * s_t, axis=0, keepdims=True)
                db_scr[sl, cs] = khd
                dbl_scr[sl, cs] = jnp.broadcast_to(dbl, (CHUNK, HEAD))
                ds_t = grow + ds_t * ebl
            ds_ref[hh] = ds_t
            dqt = dqt_scr[:, cs]
            dlogf = _chunk_cumsum(qt * dqt - kt * dkt - db_scr[:, cs], not rev) + dbl_scr[:, cs]
            dforget = dlogf / forget - dk_scr[:, cs]
            df_ref[:, cs] = (dforget * (1.0 - lbv) * sig * (1.0 - sig)).astype(BF16)
            dlb_ref[:, cs] += jnp.sum(dforget * (1.0 - sig), axis=0, keepdims=True)
            dqr = dqt * eb * (HG_SCALE * qs * (1.0 + q_raw * (1.0 - qs)))
            dv = dv_scr[:, cs]
            if has_prev:
                dqr = dqr + pq_ref[:, cs]
                dv = dv + pv_ref[:, cs]
            dq_ref[:, cs] = dqr.astype(qv_dtype)
            dv_ref[:, cs] = dv.astype(qv_dtype)

    hps = GLA_HEADS_PER_STEP
    col = lambda off: pl.BlockSpec((tb, hps * HEAD), lambda h, n: (tmap(n), off // (hps * HEAD) + h))
    blk = pl.BlockSpec((tb, hps * HEAD), lambda h, n: (tmap(n), h))
    vec = pl.BlockSpec((1, hps * HEAD), lambda h, n: (0, h))
    wide = HG_HEADS * HEAD
    into = jax.ShapeDtypeStruct(dproj.shape, dproj.dtype)
    outs, carried = _call(
        body, name=name, grid=(HG_HEADS // hps, nb),
        in_specs=[pl.BlockSpec((tb, tb), lambda h, n: (0, 0)), col(C_HQ), col(f_off), col(C_HI), vec, blk,
                  pl.BlockSpec((hps, ncb, HEAD, HEAD), lambda h, n: (h, tmap(n), 0, 0)), ANY]
                 + ([blk, blk] if has_prev else []),
        out_specs=[col(C_HQ) if has_prev else blk, blk if has_prev else col(f_off), blk, vec],
        out_shape=[into if has_prev else jax.ShapeDtypeStruct((rows, wide), qv_dtype),
                   jax.ShapeDtypeStruct((rows, wide), BF16) if has_prev else into,
                   jax.ShapeDtypeStruct((rows, wide), qv_dtype), jax.ShapeDtypeStruct((1, wide), F32)],
        aliases={7: 0 if has_prev else 1},
        scratch_shapes=[pltpu.VMEM((hps, HEAD, HEAD), F32)] + [pltpu.VMEM((tb, hps * HEAD), F32)] * 5,
        semantics=("parallel", "arbitrary"),
        ins=[_tri_mask(tb, rev), proj, proj, proj, lb, do, states, dproj] + (list(prev) if has_prev else []),
        carry=None if carry is NO_CARRY else carry)
    return outs if carry is NO_CARRY else (outs, carried)


def _fill_columns(dproj, parts, col, *, name):
    rows, wd = parts[0].shape
    tm = min(1024, rows)
    n = len(parts)

    def body(*refs):
        o_ref = refs[n + 1]
        for j in range(n):
            o_ref[:, j * wd:(j + 1) * wd] = refs[j][...]

    return pl.pallas_call(
        body, name=name, grid=(rows // tm,),
        in_specs=[pl.BlockSpec((tm, wd), lambda i: (i, 0))] * n + [ANY],
        out_specs=pl.BlockSpec((pl.Element(tm), pl.Element(n * wd)), lambda i: (i * tm, col)),
        out_shape=jax.ShapeDtypeStruct(dproj.shape, dproj.dtype), input_output_aliases={n: 0},
        compiler_params=_params(("parallel",)),
    )(*parts, dproj)


def _hg_out_fwd(o_fw, o_bw, proj, gain, *, name):
    rows = o_fw.shape[0]
    tm = min(1024, rows)
    wide = HG_HEADS * HEAD

    def body(a_ref, b_ref, g_ref, gain_ref, o_ref):
        for h in range(HG_HEADS):
            sl = slice(h * HEAD, (h + 1) * HEAD)
            xhat, _ = _rms(a_ref[:, sl] + b_ref[:, sl])
            gate = g_ref[:, sl].astype(F32)
            o_ref[:, sl] = (xhat * gain_ref[...] * (gate * _sigmoid(gate))).astype(BF16)

    row = pl.BlockSpec((tm, wide), lambda i: (i, 0))
    return pl.pallas_call(
        body, name=name, grid=(rows // tm,),
        in_specs=[row, row, pl.BlockSpec((tm, wide), lambda i: (i, C_HG // wide)),
                  pl.BlockSpec((1, HEAD), lambda i: (0, 0))],
        out_specs=row, out_shape=jax.ShapeDtypeStruct((rows, wide), BF16),
        compiler_params=_params(("parallel",)),
    )(o_fw, o_bw, proj, gain)


def _hg_out_bwd(dout, o_fw, o_bw, proj, gain, dproj, *, name):
    rows = o_fw.shape[0]
    tm = min(512, rows)
    wide = HG_HEADS * HEAD

    def body(d_ref, a_ref, b_ref, g_ref, gain_ref, dp_in, dgate_ref, do_ref, dgain_ref):
        del dp_in

        @pl.when(pl.program_id(0) == 0)
        def _():
            dgain_ref[...] = jnp.zeros_like(dgain_ref)

        dgain = jnp.zeros((1, HEAD), F32)
        for h in range(HG_HEADS):
            sl = slice(h * HEAD, (h + 1) * HEAD)
            xhat, r = _rms(a_ref[:, sl] + b_ref[:, sl])
            gate, dy = g_ref[:, sl].astype(F32), d_ref[:, sl]
            gs = _sigmoid(gate)
            dgate_ref[:, sl] = (dy * xhat * gain_ref[...] * (gs * (1.0 + gate * (1.0 - gs)))).astype(BF16)
            dx, dgr = _rms_bwd(dy * (gate * gs), xhat, r, gain_ref[...])
            do_ref[:, sl] = dx
            dgain = dgain + jnp.sum(dgr, axis=0, keepdims=True)
        dgain_ref[...] += dgain

    row = pl.BlockSpec((tm, wide), lambda i: (i, 0))
    gate = pl.BlockSpec((tm, wide), lambda i: (i, C_HG // wide))
    vec = pl.BlockSpec((1, HEAD), lambda i: (0, 0))
    return pl.pallas_call(
        body, name=name, grid=(rows // tm,),
        in_specs=[row, row, row, gate, vec, ANY],
        out_specs=[gate, row, vec],
        out_shape=[jax.ShapeDtypeStruct(dproj.shape, dproj.dtype), jax.ShapeDtypeStruct((rows, wide), F32),
                   jax.ShapeDtypeStruct((1, HEAD), F32)],
        input_output_aliases={5: 0},
        compiler_params=_params(("arbitrary",)),
    )(dout, o_fw, o_bw, proj, gain, dproj)


def _strided_rows(ref, r, count, d):
    return ref[...] if d == 1 else ref[pl.ds(r, count, stride=d), :]


def _store_strided(ref, r, count, d, val):
    if d == 1:
        ref[...] = val
    else:
        ref[pl.ds(r, count, stride=d), :] = val


def _da_prep(proj, q_gain, k_gain, g, *, name):
    d = DA_GROUPS[g][0]
    rows = proj.shape[0]
    rb = min(1024, rows)
    tn = rb // d

    def body(q_ref, k_ref, v_ref, qg_ref, kg_ref, qo_ref, ko_ref, vo_ref, qf_ref, kf_ref, vf_ref):
        for h in range(DA_GH):
            cs = slice(h * HEAD, (h + 1) * HEAD)
            for src, dst in ((q_ref, qf_ref), (k_ref, kf_ref), (v_ref, vf_ref)):
                dst[...] = src[:, cs].astype(F32)
            for r in range(d):
                qhat, _ = _rms(_strided_rows(qf_ref, r, tn, d))
                khat, _ = _rms(_strided_rows(kf_ref, r, tn, d))
                qo_ref[h, r] = (qhat * qg_ref[...]).astype(BF16)
                ko_ref[h, r] = (khat * kg_ref[...]).astype(BF16)
                vo_ref[h, r] = _strided_rows(vf_ref, r, tn, d).astype(BF16)

    wide = DA_GH * HEAD
    col = lambda off: pl.BlockSpec((rb, wide), lambda i: (i, off // wide + g))
    vec = pl.BlockSpec((1, HEAD), lambda i: (0, 0))
    out = pl.BlockSpec((DA_GH, d, tn, HEAD), lambda i: (0, 0, i, 0))
    shape = jax.ShapeDtypeStruct((DA_GH, d, rows // d, HEAD), BF16)
    return pl.pallas_call(
        body, name=name, grid=(rows // rb,),
        in_specs=[col(C_DQ), col(C_DK), col(C_DV), vec, vec],
        out_specs=[out, out, out], out_shape=[shape, shape, shape],
        scratch_shapes=[pltpu.VMEM((rb, HEAD), F32)] * 3,
        compiler_params=_params(("parallel",)),
    )(proj, proj, proj, q_gain, k_gain)


def _slopes(g):
    idx = np.arange(g * DA_GH + 1, (g + 1) * DA_GH + 1)
    s = (2.0 ** (-8.0 * idx / (DA_GH * len(DA_GROUPS)))).astype(np.float32)
    return jnp.asarray(np.broadcast_to(s[:, None, None], (DA_GH, 8, HEAD)).copy())


def _band_window(ld, t, radius):
    win = min(2 * t, ld)
    assert t // 2 >= radius or win == ld
    return win


def _band_scores(q, k, q0, start, slope, d, radius):
    t, win = q.shape[0], k.shape[0]
    row = lax.broadcasted_iota(jnp.int32, (t, win), 0)
    col = lax.broadcasted_iota(jnp.int32, (t, win), 1)
    rel = jnp.abs((start - q0) + col - row)
    return _dot_nt(q, k) * ATT_SCALE + jnp.where(rel <= radius, -slope * (d * rel).astype(F32), NEG_INF)


def _band_fwd(qr, kr, vr, g, *, name):
    d, radius = DA_GROUPS[g]
    _, _, ld, _ = qr.shape
    t = min(HEAD, ld)
    qb = min(BAND_QBLOCKS, ld // t)
    rps = max(1, min(d, BAND_QBLOCKS // qb))
    win = _band_window(ld, t, radius)

    def body(q_ref, k_ref, v_ref, sl_ref, o_ref, lse_ref):
        i = pl.program_id(2)
        slope = sl_ref[0:1, 0:1]
        for rr in range(rps):
            for j in range(qb):
                sl = slice(j * t, (j + 1) * t)
                q0 = (i * qb + j) * t
                start = pl.multiple_of(jnp.clip(q0 - t // 2, 0, ld - win), t // 2)
                s = _band_scores(q_ref[rr, sl, :], k_ref[rr, pl.ds(start, win), :], q0, start, slope, d, radius)
                m = jnp.max(s, axis=-1, keepdims=True)
                p = jnp.exp(s - m).astype(BF16)
                l = jnp.dot(p, _ones(win, HEAD), preferred_element_type=F32)
                o_ref[rr, sl, :] = _dot(p, v_ref[rr, pl.ds(start, win), :]) / l
                lse_ref[rr, sl, :] = m + jnp.log(l)

    own = pl.BlockSpec((None, rps, qb * t, HEAD), lambda h, r, i: (h, r, i, 0))
    seq = pl.BlockSpec((None, rps, ld, HEAD), lambda h, r, i: (h, r, 0, 0))
    shape = jax.ShapeDtypeStruct(qr.shape, F32)
    return pl.pallas_call(
        body, name=name, grid=(DA_GH, d // rps, ld // (qb * t)),
        in_specs=[own, seq, seq, pl.BlockSpec((None, 8, HEAD), lambda h, r, i: (h, 0, 0))],
        out_specs=[own, own], out_shape=[shape, shape],
        compiler_params=_params(("parallel", "parallel", "parallel")),
    )(qr, kr, vr, _slopes(g))


def _band_bwd(qr, kr, vr, dor, lser, deltar, g, *, name):
    d, radius = DA_GROUPS[g]
    _, _, ld, _ = qr.shape
    t = min(HEAD, ld)
    qb = min(BAND_QBLOCKS, ld // t)
    rps = max(1, min(d, BAND_QBLOCKS // qb))
    win = _band_window(ld, t, radius)

    def body(q_ref, k_ref, v_ref, do_ref, lse_ref, dl_ref, sl_ref, dq_ref, dk_ref, dv_ref):
        i = pl.program_id(2)

        @pl.when(i == 0)
        def _():
            dk_ref[...] = jnp.zeros_like(dk_ref)
            dv_ref[...] = jnp.zeros_like(dv_ref)

        slope = sl_ref[0:1, 0:1]
        for rr in range(rps):
            for j in range(qb):
                sl = slice(j * t, (j + 1) * t)
                q0 = (i * qb + j) * t
                start = pl.multiple_of(jnp.clip(q0 - t // 2, 0, ld - win), t // 2)
                rows = pl.ds(start, win)
                q, dout, k, v = q_ref[rr, sl, :], do_ref[rr, sl, :], k_ref[rr, rows, :], v_ref[rr, rows, :]
                p = jnp.exp(_band_scores(q, k, q0, start, slope, d, radius) - lse_ref[rr, sl, 0:1])
                ds = p * (_dot_nt(dout, v) - dl_ref[rr, sl, 0:1])
                dq_ref[rr, sl, :] = _dot(ds, k) * ATT_SCALE
                dk_ref[rr, rows, :] += _dot_tn(ds, q) * ATT_SCALE
                dv_ref[rr, rows, :] += _dot_tn(p, dout)

    own = pl.BlockSpec((None, rps, qb * t, HEAD), lambda h, r, i: (h, r, i, 0))
    seq = pl.BlockSpec((None, rps, ld, HEAD), lambda h, r, i: (h, r, 0, 0))
    shape = jax.ShapeDtypeStruct(qr.shape, F32)
    return pl.pallas_call(
        body, name=name, grid=(DA_GH, d // rps, ld // (qb * t)),
        in_specs=[own, seq, seq, own, own, own, pl.BlockSpec((None, 8, HEAD), lambda h, r, i: (h, 0, 0))],
        out_specs=[own, seq, seq], out_shape=[shape, shape, shape],
        compiler_params=_params(("parallel", "parallel", "arbitrary")),
    )(qr, kr, vr, dor, lser, deltar, _slopes(g))


def _da_merge(outs, lses, rows, *, name):
    rb = min(1024, rows)
    wide = DA_GH * HEAD

    def body(*refs):
        o_refs, l_refs = refs[0:3], refs[3:6]
        o_ref, lse_ref = refs[6:8]
        on_refs, ln_refs = refs[8:11], refs[11:14]
        for h in range(DA_GH):
            cs = slice(h * HEAD, (h + 1) * HEAD)
            for g, (d, _) in enumerate(DA_GROUPS):
                tn = rb // d
                for r in range(d):
                    _store_strided(on_refs[g], r, tn, d, o_refs[g][h, r])
                    _store_strided(ln_refs[g], r, tn, d, l_refs[g][h, r])
            l0, l1, l2 = ln_refs[0][...], ln_refs[1][...], ln_refs[2][...]
            m = jnp.maximum(jnp.maximum(l0, l1), l2)
            e0, e1, e2 = jnp.exp(l0 - m), jnp.exp(l1 - m), jnp.exp(l2 - m)
            tot = e0 + e1 + e2
            o_ref[:, cs] = (e0 * on_refs[0][...] + e1 * on_refs[1][...] + e2 * on_refs[2][...]) / tot
            lse_ref[:, cs] = m + jnp.log(tot)

    res = lambda d: pl.BlockSpec((DA_GH, d, rb // d, HEAD), lambda i: (0, 0, i, 0))
    nat = pl.BlockSpec((rb, wide), lambda i: (i, 0))
    shape = jax.ShapeDtypeStruct((rows, wide), F32)
    return pl.pallas_call(
        body, name=name, grid=(rows // rb,),
        in_specs=[res(d) for d, _ in DA_GROUPS] * 2,
        out_specs=[nat, nat], out_shape=[shape, shape],
        scratch_shapes=[pltpu.VMEM((rb, HEAD), F32)] * 6,
        compiler_params=_params(("parallel",)),
    )(*outs, *lses)


def _da_bwd_prep(dout, o, lse, *, name):
    rows = o.shape[0]
    rb = min(512, rows)
    wide = DA_GH * HEAD

    def body(d_ref, o_ref, l_ref, *outs):
        d_scr, l_scr, delta_scr = outs[9:12]
        for h in range(DA_GH):
            cs = slice(h * HEAD, (h + 1) * HEAD)
            dv = d_ref[:, cs]
            d_scr[...] = dv
            l_scr[...] = l_ref[:, cs]
            delta_scr[...] = _lane_sum(dv * o_ref[:, cs])
            for g, (d, _) in enumerate(DA_GROUPS):
                tn = rb // d
                for r in range(d):
                    outs[3 * g][h, r] = _strided_rows(d_scr, r, tn, d).astype(BF16)
                    outs[3 * g + 1][h, r] = _strided_rows(l_scr, r, tn, d)
                    outs[3 * g + 2][h, r] = _strided_rows(delta_scr, r, tn, d)

    nat = pl.BlockSpec((rb, wide), lambda i: (i, 0))
    out_specs, out_shape = [], []
    for d, _ in DA_GROUPS:
        for dt in (BF16, F32, F32):
            out_specs.append(pl.BlockSpec((DA_GH, d, rb // d, HEAD), lambda i: (0, 0, i, 0)))
            out_shape.append(jax.ShapeDtypeStruct((DA_GH, d, rows // d, HEAD), dt))
    return pl.pallas_call(
        body, name=name, grid=(rows // rb,),
        in_specs=[nat, nat, nat], out_specs=out_specs, out_shape=out_shape,
        scratch_shapes=[pltpu.VMEM((rb, HEAD), F32)] * 3,
        compiler_params=_params(("parallel",)),
    )(dout, o, lse)


def _da_prep_bwd(dqr, dkr, dvr, proj, q_gain, k_gain, g, *, name):
    d = DA_GROUPS[g][0]
    rows = proj.shape[0]
    rb = min(1024, rows)
    tn = rb // d
    wide = DA_GH * HEAD

    def body(dq_ref, dk_ref, dv_ref, q_ref, k_ref, qg_ref, kg_ref, oq_ref, ok_ref, ov_ref, gq_ref, gk_ref, *nat_refs):
        @pl.when(pl.program_id(0) == 0)
        def _():
            gq_ref[...] = jnp.zeros_like(gq_ref)
            gk_ref[...] = jnp.zeros_like(gk_ref)

        for h in range(DA_GH):
            cs = slice(h * HEAD, (h + 1) * HEAD)
            for j, src in enumerate((dq_ref, dk_ref, dv_ref)):
                for r in range(d):
                    _store_strided(nat_refs[j], r, tn, d, src[h, r])
            ov_ref[:, cs] = nat_refs[2][...].astype(BF16)
            for j, (x_ref, gn_ref, out_ref, acc_ref) in enumerate(((q_ref, qg_ref, oq_ref, gq_ref),
                                                                    (k_ref, kg_ref, ok_ref, gk_ref))):
                xhat, r = _rms(x_ref[:, cs])
                dx, dgr = _rms_bwd(nat_refs[j][...], xhat, r, gn_ref[...])
                out_ref[:, cs] = dx.astype(BF16)
                acc_ref[...] += jnp.sum(dgr, axis=0, keepdims=True)

    res = pl.BlockSpec((DA_GH, d, tn, HEAD), lambda i: (0, 0, i, 0))
    col = lambda off: pl.BlockSpec((rb, wide), lambda i: (i, off // wide + g))
    vec = pl.BlockSpec((1, HEAD), lambda i: (0, 0))
    nat = pl.BlockSpec((rb, wide), lambda i: (i, 0))
    shape = jax.ShapeDtypeStruct((rows, wide), BF16)
    gshape = jax.ShapeDtypeStruct((1, HEAD), F32)
    return pl.pallas_call(
        body, name=name, grid=(rows // rb,),
        in_specs=[res, res, res, col(C_DQ), col(C_DK), vec, vec],
        out_specs=[nat, nat, nat, vec, vec], out_shape=[shape, shape, shape, gshape, gshape],
        scratch_shapes=[pltpu.VMEM((rb, HEAD), F32)] * 3,
        compiler_params=_params(("arbitrary",)),
    )(dqr, dkr, dvr, proj, proj, q_gain, k_gain)


def _mem_fwd(proj, kv, q_gain, k_gain, *, name):
    rows = proj.shape[0]
    tm = min(1024, rows)
    n_mem = kv.shape[0]

    def body(q_ref, k_ref, v_ref, qg_ref, kg_ref, o_ref):
        qhat, _ = _rms(q_ref[...])
        khat, _ = _rms(k_ref[...])
        s = _dot_nt(qhat * qg_ref[...], khat * kg_ref[...]) * ATT_SCALE
        p = jnp.exp(s - jnp.max(s, axis=-1, keepdims=True))
        p = p / jnp.sum(p, axis=-1, keepdims=True)
        o_ref[...] = _dot(p, v_ref[...]).astype(BF16)

    vec = pl.BlockSpec((1, HEAD), lambda i, h: (0, 0))
    return pl.pallas_call(
        body, name=name, grid=(rows // tm, MEM_HEADS),
        in_specs=[pl.BlockSpec((tm, HEAD), lambda i, h: (i, C_MQ // HEAD + h)),
                  pl.BlockSpec((n_mem, HEAD), lambda i, h: (0, h)),
                  pl.BlockSpec((n_mem, HEAD), lambda i, h: (0, MEM_HEADS + h)), vec, vec],
        out_specs=pl.BlockSpec((tm, HEAD), lambda i, h: (i, h)),
        out_shape=jax.ShapeDtypeStruct((rows, MEM_HEADS * HEAD), BF16),
        compiler_params=_params(("parallel", "parallel")),
    )(proj, kv, kv, q_gain, k_gain)


def _mem_bwd(dout, proj, kv, q_gain, k_gain, dproj, *, name):
    rows = proj.shape[0]
    tm = min(1024, rows)
    steps = rows // tm
    n_mem = kv.shape[0]

    def body(d_ref, q_ref, k_ref, v_ref, qg_ref, kg_ref, dp_in, dq_ref, dk_ref, dv_ref, gq_ref, gk_ref, dkn_ref):
        del dp_in
        h, i = pl.program_id(0), pl.program_id(1)

        @pl.when((h == 0) & (i == 0))
        def _():
            gq_ref[...] = jnp.zeros_like(gq_ref)
            gk_ref[...] = jnp.zeros_like(gk_ref)

        @pl.when(i == 0)
        def _():
            dkn_ref[...] = jnp.zeros_like(dkn_ref)
            dv_ref[...] = jnp.zeros_like(dv_ref)

        qhat, rq = _rms(q_ref[...])
        khat, rk = _rms(k_ref[...])
        qn, kn = qhat * qg_ref[...], khat * kg_ref[...]
        s = _dot_nt(qn, kn) * ATT_SCALE
        p = jnp.exp(s - jnp.max(s, axis=-1, keepdims=True))
        p = p / jnp.sum(p, axis=-1, keepdims=True)
        dout = d_ref[...]
        dp = _dot_nt(dout, v_ref[...])
        ds = p * (dp - jnp.sum(p * dp, axis=-1, keepdims=True))
        dv_ref[...] += _dot_tn(p, dout)
        dkn_ref[...] += _dot_tn(ds, qn) * ATT_SCALE
        dq, dgr = _rms_bwd(_dot(ds, kn) * ATT_SCALE, qhat, rq, qg_ref[...])
        dq_ref[...] = dq.astype(BF16)
        gq_ref[...] += jnp.sum(dgr, axis=0, keepdims=True)

        @pl.when(i == steps - 1)
        def _():
            dk, dgk = _rms_bwd(dkn_ref[...], khat, rk, kg_ref[...])
            dk_ref[...] = dk
            gk_ref[...] += jnp.sum(dgk, axis=0, keepdims=True)

    vec = pl.BlockSpec((1, HEAD), lambda h, i: (0, 0))
    memh = pl.BlockSpec((n_mem, HEAD), lambda h, i: (0, h))
    tok = pl.BlockSpec((tm, HEAD), lambda h, i: (i, h))
    mq = pl.BlockSpec((tm, HEAD), lambda h, i: (i, C_MQ // HEAD + h))
    gshape = jax.ShapeDtypeStruct((1, HEAD), F32)
    return pl.pallas_call(
        body, name=name, grid=(MEM_HEADS, steps),
        in_specs=[tok, mq, memh, pl.BlockSpec((n_mem, HEAD), lambda h, i: (0, MEM_HEADS + h)), vec, vec, ANY],
        out_specs=[mq, memh, memh, vec, vec],
        out_shape=[jax.ShapeDtypeStruct(dproj.shape, dproj.dtype),
                   jax.ShapeDtypeStruct((n_mem, MEM_HEADS * HEAD), F32),
                   jax.ShapeDtypeStruct((n_mem, MEM_HEADS * HEAD), F32), gshape, gshape],
        input_output_aliases={6: 0},
        scratch_shapes=[pltpu.VMEM((n_mem, HEAD), F32)],
        compiler_params=_params(("arbitrary", "arbitrary")),
    )(dout, proj, kv, kv, q_gain, k_gain, dproj)


def _branch_fwd(o_hg, o_da, o_mem, proj, wp_hg, wp_da, wp_mem, *, name):
    rows = o_hg.shape[0]
    tm = min(512, rows)

    def body(a_ref, b_ref, c_ref, ga_ref, gb_ref, gc_ref, wa_ref, wb_ref, wc_ref, o_ref):
        merged = _sigmoid(ga_ref[...]) * _dot(a_ref[...], wa_ref[...])
        merged += _sigmoid(gb_ref[...]) * _dot(b_ref[...], wb_ref[...])
        merged += _sigmoid(gc_ref[...]) * _dot(c_ref[...], wc_ref[...])
        o_ref[...] = merged.astype(BF16)

    row = lambda w: pl.BlockSpec((tm, w), lambda i: (i, 0))
    gate = lambda off: pl.BlockSpec((tm, D_MODEL), lambda i: (i, off // D_MODEL))
    full = lambda a: pl.BlockSpec(a.shape, lambda i: (0, 0))
    return pl.pallas_call(
        body, name=name, grid=(rows // tm,),
        in_specs=[row(o_hg.shape[1]), row(o_da.shape[1]), row(o_mem.shape[1]),
                  gate(C_GHG), gate(C_GDA), gate(C_GMEM), full(wp_hg), full(wp_da), full(wp_mem)],
        out_specs=row(D_MODEL), out_shape=jax.ShapeDtypeStruct((rows, D_MODEL), BF16),
        compiler_params=_params(("parallel",)),
    )(o_hg, o_da, o_mem, proj, proj, proj, wp_hg, wp_da, wp_mem)


def _branch_bwd(dm, o_hg, o_da, o_mem, proj, wp_hg, wp_da, wp_mem, *, name):
    rows = o_hg.shape[0]
    tm = min(256, rows)

    def body(dm_ref, a_ref, b_ref, c_ref, ga_ref, gb_ref, gc_ref, wa_ref, wb_ref, wc_ref, dp_ref, *outs):
        dmv = dm_ref[...]
        for j, (o_ref, g_ref, w_ref) in enumerate(((a_ref, ga_ref, wa_ref), (b_ref, gb_ref, wb_ref),
                                                   (c_ref, gc_ref, wc_ref))):
            z = _dot(o_ref[...], w_ref[...])
            gs = _sigmoid(g_ref[...])
            dz = (dmv * gs).astype(BF16)
            dp_ref[:, j * D_MODEL:(j + 1) * D_MODEL] = (dmv * z * gs * (1.0 - gs)).astype(BF16)
            outs[2 * j][...] = dz
            outs[2 * j + 1][...] = _dot_nt(dz, w_ref[...])

    row = lambda w: pl.BlockSpec((tm, w), lambda i: (i, 0))
    gate = lambda off: pl.BlockSpec((tm, D_MODEL), lambda i: (i, off // D_MODEL))
    full = lambda a: pl.BlockSpec(a.shape, lambda i: (0, 0))
    out_specs = [pl.BlockSpec((pl.Element(tm), pl.Element(3 * D_MODEL)), lambda i: (i * tm, C_GHG))]
    out_shape = [jax.ShapeDtypeStruct((rows, IN_COLS), BF16)]
    for o in (o_hg, o_da, o_mem):
        out_specs += [row(D_MODEL), row(o.shape[1])]
        out_shape += [jax.ShapeDtypeStruct((rows, D_MODEL), BF16), jax.ShapeDtypeStruct((rows, o.shape[1]), F32)]
    return pl.pallas_call(
        body, name=name, grid=(rows // tm,),
        in_specs=[row(D_MODEL), row(o_hg.shape[1]), row(o_da.shape[1]), row(o_mem.shape[1]),
                  gate(C_GHG), gate(C_GDA), gate(C_GMEM), full(wp_hg), full(wp_da), full(wp_mem)],
        out_specs=out_specs, out_shape=out_shape,
        compiler_params=_params(("parallel",)),
    )(dm, o_hg, o_da, o_mem, proj, proj, proj, wp_hg, wp_da, wp_mem)


def _ffn_in(h2, w_ab, *, name):
    rows, dff = h2.shape[0], w_ab.shape[1] // 2
    tm, tn = min(2048, rows), 256

    def body(h_ref, wa_ref, wb_ref, a_ref, b_ref, u_ref):
        a = _dot(h_ref[...], wa_ref[...])
        b = _dot(h_ref[...], wb_ref[...])
        a_ref[...] = a.astype(BF16)
        b_ref[...] = b.astype(BF16)
        u_ref[...] = (a * _sigmoid(a) * b).astype(BF16)

    out = pl.BlockSpec((tm, tn), lambda i, j: (i, j))
    return pl.pallas_call(
        body, name=name, grid=(rows // tm, dff // tn),
        in_specs=[pl.BlockSpec((tm, D_MODEL), lambda i, j: (i, 0)),
                  pl.BlockSpec((D_MODEL, tn), lambda i, j: (0, j)),
                  pl.BlockSpec((D_MODEL, tn), lambda i, j: (0, dff // tn + j))],
        out_specs=[out, out, out],
        out_shape=[jax.ShapeDtypeStruct((rows, dff), BF16)] * 3,
        compiler_params=_params(("parallel", "parallel")),
    )(h2, w_ab, w_ab)


def _ffn_act_bwd(dy, w_out, a, b, *, name):
    rows, dff = a.shape
    tm, tn = min(2048, rows), 256

    def body(dy_ref, w_ref, a_ref, b_ref, da_ref, db_ref):
        du = _dot_nt(dy_ref[...], w_ref[...])
        av, bv = a_ref[...].astype(F32), b_ref[...].astype(F32)
        sa = _sigmoid(av)
        da_ref[...] = (du * bv * sa * (1.0 + av * (1.0 - sa))).astype(BF16)
        db_ref[...] = (du * av * sa).astype(BF16)

    tile = pl.BlockSpec((tm, tn), lambda i, j: (i, j))
    return pl.pallas_call(
        body, name=name, grid=(rows // tm, dff // tn),
        in_specs=[pl.BlockSpec((tm, D_MODEL), lambda i, j: (i, 0)),
                  pl.BlockSpec((tn, D_MODEL), lambda i, j: (j, 0)), tile, tile],
        out_specs=[tile, tile],
        out_shape=[jax.ShapeDtypeStruct((rows, dff), BF16), jax.ShapeDtypeStruct((rows, dff), BF16)],
        compiler_params=_params(("parallel", "parallel")),
    )(dy, w_out, a, b)


def _lower_bound(lb_fw, lb_bw, *, name):
    def body(a_ref, b_ref, oa_ref, ob_ref):
        for src, dst in ((a_ref, oa_ref), (b_ref, ob_ref)):
            dst[...] = _sigmoid(src[0:1, :] - src[1:2, :])

    shape = jax.ShapeDtypeStruct((1, lb_fw.shape[1]), F32)
    return pl.pallas_call(body, name=name, out_shape=[shape, shape])(lb_fw, lb_bw)


def _local_step(x, mem, tgt, p, ex):
    rows = x.shape[0]
    lb_fw, lb_bw = _lower_bound(p["lb_logits_fw"], p["lb_logits_bw"], name="lower_bound")

    h = _rms_fwd(x, p["norm_mix_gain"], name="norm_mix")
    w = {"w_in": ex.w_in}
    proj, carried = _matmul([(h, w["w_in"])], "nn", BF16, tm=1024, tn=1024, carry=ex.late_carry(), name="proj_in")
    w.update(ex.late_weights(carried))
    o_fw, st_fw = _gla_fwd(proj, lb_fw, f_off=C_FF, rev=False, name="gla_fwd_fw")
    o_bw, st_bw = _gla_fwd(proj, lb_bw, f_off=C_FB, rev=True, name="gla_fwd_bw")
    o_hg = _hg_out_fwd(o_fw, o_bw, proj, p["hg_norm_gain"], name="hg_out")

    qkv_r, outs, lses = [], [], []
    for g in range(len(DA_GROUPS)):
        qr, kr, vr = _da_prep(proj, p["da_q_gain"], p["da_k_gain"], g, name=f"da_prep{g}")
        og, lg = _band_fwd(qr, kr, vr, g, name=f"band_fwd{g}")
        qkv_r.append((qr, kr, vr))
        outs.append(og)
        lses.append(lg)
    o_da, lse_da = _da_merge(outs, lses, rows, name="da_merge")

    mem_n = _rms_fwd(mem, p["norm_mem_gain"], name="norm_mem")
    kv = _matmul([(mem_n, w["w_mem_kv"])], "nn", F32, tm=256, tn=512, name="mem_kv")
    o_mem = _mem_fwd(proj, kv, p["mem_q_gain"], p["mem_k_gain"], name="mem_attn")

    merged = _branch_fwd(o_hg, o_da, o_mem, proj, w["w_proj_hg"], w["w_proj_da"], w["w_proj_mem"],
                         name="branch_merge")
    x1, h2 = _out_norm(merged, w["w_out"], x, p["norm_ffn_gain"], name="mix_out_norm")
    a, b, u = _ffn_in(h2, w["w_ffn_in"], name="ffn_in")
    dy, dy_b, loss = _out_loss(u, w["w_ffn_out"], x1, tgt, name="ffn_out_loss")

    gw, gwb, gs = {}, {}, {}
    gw["w_ffn_out"], gwb["w_ffn_out"] = _matmul([(u, dy_b)], "tn", F32, tm=256, tn=1024, also=BF16, name="g_ffn_out")
    da, db = _ffn_act_bwd(dy_b, w["w_ffn_out"], a, b, name="ffn_act_bwd")
    gw["w_ffn_a"], gwb["w_ffn_a"] = _matmul([(h2, da)], "tn", F32, tm=1024, tn=256, also=BF16, name="g_ffn_a")
    gw["w_ffn_b"], gwb["w_ffn_b"] = _matmul([(h2, db)], "tn", F32, tm=1024, tn=256, also=BF16, name="g_ffn_b")
    dh2 = _matmul([(da, w["w_ffn_in"], 0), (db, w["w_ffn_in"], 1)], "nt", F32, tm=256, tn=1024, name="d_h2")
    dx1, dx1_b, gs["norm_ffn_gain"] = _rms_bwd_rows(dh2, x1, p["norm_ffn_gain"], dy, (F32, BF16), name="norm_ffn_bwd")
    gw["w_out"], gwb["w_out"] = _matmul([(merged, dx1_b)], "tn", F32, tm=512, tn=512, also=BF16, name="g_out")
    dmerged = _matmul([(dx1_b, w["w_out"])], "nt", F32, tm=512, tn=512, name="d_merged")
    dproj, dz_hg, do_hg, dz_da, do_da, dz_mem, do_mem = _branch_bwd(
        dmerged, o_hg, o_da, o_mem, proj, w["w_proj_hg"], w["w_proj_da"], w["w_proj_mem"], name="branch_bwd")
    gw["w_proj_hg"], gwb["w_proj_hg"] = _matmul([(o_hg, dz_hg)], "tn", F32, tm=512, tn=512, also=BF16, name="g_proj_hg")
    gw["w_proj_da"], gwb["w_proj_da"] = _matmul([(o_da, dz_da)], "tn", F32, tm=512, tn=512, also=BF16, name="g_proj_da")
    gw["w_proj_mem"], gwb["w_proj_mem"] = _matmul([(o_mem, dz_mem)], "tn", F32, tm=512, tn=512, also=BF16, name="g_proj_mem")

    dproj, dk_mem, dv_mem, gs["mem_q_gain"], gs["mem_k_gain"] = _mem_bwd(
        do_mem, proj, kv, p["mem_q_gain"], p["mem_k_gain"], dproj, name="mem_attn_bwd")
    dkv = jnp.concatenate([dk_mem, dv_mem], axis=1).astype(BF16)
    gw["w_mem_kv"], gwb["w_mem_kv"] = _matmul([(mem_n, dkv)], "tn", F32, tm=512, tn=512, also=BF16, name="g_mem_kv")
    dmem_n = _matmul([(dkv, w["w_mem_kv"])], "nt", F32, tm=256, tn=512, name="d_mem_n")
    (gs["norm_mem_gain"],) = _rms_bwd_rows(dmem_n, mem, p["norm_mem_gain"], None, (), name="norm_mem_bwd")

    prep = _da_bwd_prep(do_da, o_da, lse_da, name="da_bwd_prep")
    d_da, gq_parts, gk_parts = [], [], []
    for g in range(len(DA_GROUPS)):
        qr, kr, vr = qkv_r[g]
        dor, lser, deltar = prep[3 * g:3 * g + 3]
        dqr, dkr, dvr = _band_bwd(qr, kr, vr, dor, lser, deltar, g, name=f"band_bwd{g}")
        dq, dk, dv, gq, gk = _da_prep_bwd(dqr, dkr, dvr, proj, p["da_q_gain"], p["da_k_gain"], g,
                                          name=f"da_prep_bwd{g}")
        d_da.append((dq, dk, dv))
        gq_parts.append(gq)
        gk_parts.append(gk)
    for j, (off, tag) in enumerate(((C_DQ, "q"), (C_DK, "k"), (C_DV, "v"))):
        dproj = _fill_columns(dproj, [t[j] for t in d_da], off, name=f"dproj_fill_da_{tag}")

    dproj, do_gla, gs["hg_norm_gain"] = _hg_out_bwd(do_hg, o_fw, o_bw, proj, p["hg_norm_gain"], dproj,
                                                    name="hg_out_bwd")
    carry_bulk, carry_rest = ex.early_carries(gw, gwb)
    (dq_f, dproj, dv_f, dlb_fw), bulk = _gla_bwd(proj, lb_fw, do_gla, st_fw, None, dproj, f_off=C_FF, rev=False,
                                                 carry=carry_bulk, name="gla_bwd_fw")
    (dproj, dfl_bw, dv_hg, dlb_bw), rest = _gla_bwd(proj, lb_bw, do_gla, st_bw, (dq_f, dv_f), dproj, f_off=C_FB,
                                                    rev=True, carry=carry_rest, name="gla_bwd_bw")
    ex.early_done(bulk, rest)
    dproj = _fill_columns(dproj, [dfl_bw, dv_hg], C_FB, name="dproj_fill")
    g_in, g_in_b = _matmul([(h, dproj)], "tn", F32, tm=1024, tn=512, also=BF16, name="g_in")
    dh, carried = _matmul([(dproj, w["w_in"])], "nt", F32, tm=1024, tn=1024, tk=IN_COLS // 8,
                          carry=ex.w_in_carry(g_in, g_in_b), name="d_h")
    ex.w_in_done(carried)
    grad_x, gs["norm_mix_gain"] = _rms_bwd_rows(dh, x, p["norm_mix_gain"], dx1, (F32,), name="norm_mix_bwd")

    small = _small_pack(gs, gq_parts, gk_parts, dlb_fw, dlb_bw, lb_fw, lb_bw, loss, name="small_pack")
    return grad_x, small


def _small_pack(gs, gq_parts, gk_parts, dlb_fw, dlb_bw, lb_fw, lb_bw, loss, *, name):
    def body(g_mix, g_mem, g_ffn, dfw, dbw, lfw, lbw, g_hg, q0, q1, q2, k0, k1, k2, g_mq, g_mk, loss_ref, o_ref):
        o_ref[0:1, :] = g_mix[...]
        o_ref[1:2, :] = g_mem[...]
        o_ref[2:3, :] = g_ffn[...]
        for base, d_ref, l_ref in ((3, dfw, lfw), (5, dbw, lbw)):
            lbv = l_ref[...]
            t = d_ref[...] * lbv * (1.0 - lbv)
            o_ref[base:base + 1, :] = t
            o_ref[base + 1:base + 2, :] = -t
        o_ref[7:8, :] = jnp.zeros((1, D_MODEL), F32)
        o_ref[7:8, 0:HEAD] = g_hg[...]
        o_ref[7:8, HEAD:2 * HEAD] = q0[...] + q1[...] + q2[...]
        o_ref[7:8, 2 * HEAD:3 * HEAD] = k0[...] + k1[...] + k2[...]
        o_ref[7:8, 3 * HEAD:4 * HEAD] = g_mq[...]
        o_ref[7:8, 4 * HEAD:5 * HEAD] = g_mk[...]
        o_ref[7:8, LOSS_COL:LOSS_COL + HEAD] = loss_ref[...]

    return pl.pallas_call(body, name=name, out_shape=jax.ShapeDtypeStruct((8, D_MODEL), F32))(
        gs["norm_mix_gain"], gs["norm_mem_gain"], gs["norm_ffn_gain"], dlb_fw, dlb_bw, lb_fw, lb_bw,
        gs["hg_norm_gain"], *gq_parts, *gk_parts, gs["mem_q_gain"], gs["mem_k_gain"], loss)


def _row_tile(rows, cols, n_arrays):
    budget = (16 * 1024 * 1024) // (2 * 4 * cols * n_arrays)
    tr = rows
    while tr > budget and tr % 2 == 0 and (tr // 2) % 16 == 0:
        tr //= 2
    return tr


def _cast_into_full(a, chip, rows, cols, axis, *, name):
    sr, sc = a.shape
    tr = _row_tile(sr, sc, 2)

    def body(chip_ref, a_ref, o_ref):
        del chip_ref
        o_ref[...] = a_ref[...].astype(BF16)

    if axis == 1:
        out_map = lambda i, chip_ref: (i, chip_ref[0])
    else:
        out_map = lambda i, chip_ref: (chip_ref[0] * (sr // tr) + i, 0)
    return pl.pallas_call(
        body, name=name,
        grid_spec=pltpu.PrefetchScalarGridSpec(
            num_scalar_prefetch=1, grid=(sr // tr,),
            in_specs=[pl.BlockSpec((tr, sc), lambda i, chip_ref: (i, 0))],
            out_specs=pl.BlockSpec((tr, sc), out_map)),
        out_shape=jax.ShapeDtypeStruct((rows, cols), BF16),
        compiler_params=_params(("parallel",)))(chip, a)


def _add_halves(items, *, name):
    rows = items[0][3].shape[0]
    widths = [ra.shape[1] for _, _, _, ra in items]
    tr = _row_tile(rows, sum(widths), 4)

    def body(*refs):
        o_ref = refs[-1]
        first = lax.axis_index("c") == 0
        off = 0
        for j, wd in enumerate(widths):
            h0, h1, ra = refs[3 * j:3 * j + 3]
            o_ref[:, off:off + wd] = (jnp.where(first, h0[...], h1[...]) + ra[...]).astype(BF16)
            off += wd

    in_specs, ins = [], []
    for (g, haxis, hsize, ra), wd in zip(items, widths):
        if haxis == 0:
            in_specs += [pl.BlockSpec((tr, wd), lambda i: (i, 0)),
                         pl.BlockSpec((tr, wd), lambda i, o=hsize // tr: (o + i, 0))]
        else:
            in_specs += [pl.BlockSpec((tr, wd), lambda i: (i, 0)), pl.BlockSpec((tr, wd), lambda i: (i, 1))]
        in_specs.append(pl.BlockSpec((tr, wd), lambda i: (i, 0)))
        ins += [g, g, ra]
    return pl.pallas_call(body, name=name, grid=(rows // tr,), in_specs=in_specs,
                          out_specs=pl.BlockSpec((tr, sum(widths)), lambda i: (i, 0)),
                          out_shape=jax.ShapeDtypeStruct((rows, sum(widths)), BF16),
                          compiler_params=_params(("parallel",)))(*ins)


def _add_slots(rb, *, name):
    _, rows, cols = rb.shape
    tr = _row_tile(rows, cols, 5)

    def body(r0, r1, r2, r3, o_ref):
        o_ref[...] = ((r0[...].astype(F32) + r1[...].astype(F32)) + r2[...].astype(F32)) + r3[...].astype(F32)

    slot = lambda s: pl.BlockSpec((None, tr, cols), lambda i: (s, i, 0))
    return pl.pallas_call(body, name=name, grid=(rows // tr,), in_specs=[slot(s) for s in range(4)],
                          out_specs=pl.BlockSpec((tr, cols), lambda i: (i, 0)),
                          out_shape=jax.ShapeDtypeStruct((rows, cols), F32),
                          compiler_params=_params(("parallel",)))(rb, rb, rb, rb)


def _adamw(w, g, m, v, *, name):
    rows, cols = w.shape
    tr = _row_tile(rows, cols, 8) if rows % 16 == 0 else rows
    c1 = 1.0 - ADAM_B1 ** ADAM_STEP
    c2 = 1.0 - ADAM_B2 ** ADAM_STEP

    def body(w_ref, g_ref, m_ref, v_ref, d_ref, mo_ref, vo_ref, go_ref):
        gv = g_ref[...]
        go_ref[...] = gv
        mn = ADAM_B1 * m_ref[...] + (1.0 - ADAM_B1) * gv
        vn = ADAM_B2 * v_ref[...] + (1.0 - ADAM_B2) * (gv * gv)
        mo_ref[...] = mn
        vo_ref[...] = vn
        d_ref[...] = -ADAM_LR * ((mn / c1) / (jnp.sqrt(vn / c2) + ADAM_EPS) + ADAM_WD * w_ref[...])

    spec = pl.BlockSpec((tr, cols), lambda i: (i, 0))
    shape = jax.ShapeDtypeStruct((rows, cols), F32)
    return pl.pallas_call(body, name=name, grid=(rows // tr,), in_specs=[spec] * 4, out_specs=[spec] * 4,
                        out_shape=[shape] * 4, compiler_params=_params(("parallel",)))(w, g, m, v)


W_SPECS = (
    ("w_in", 1024, IN_COLS, 1, IN_COLS // 4),
    ("w_mem_kv", 1024, 1024, 0, 256),
    ("w_proj_hg", 1024, 1024, 0, 256),
    ("w_proj_da", 512, 1024, 1, 256),
    ("w_proj_mem", 512, 1024, 1, 256),
    ("w_out", 1024, 1024, 0, 256),
    ("w_ffn_in", 1024, 2 * D_FF, 1, 2 * D_FF // 4),
    ("w_ffn_out", D_FF, 1024, 0, D_FF // 4),
)
CHIP_FLIPS = ((1, 0), (0, 1), (1, 1))
ANY = pl.BlockSpec(memory_space=pl.ANY)
DMA_CHUNK_BYTES = 1 << 20
STAGE_BYTES = 2 << 20


def _place():
    x, y, c = lax.axis_index("x"), lax.axis_index("y"), lax.axis_index("c")
    return x, y, c, 2 * x + y


def _flip(v, f):
    return 1 - v if f else v


def _slab(ref, axis, idx, size):
    start = pl.multiple_of(idx * size, size)
    return ref.at[pl.ds(start, size), :] if axis == 0 else ref.at[:, pl.ds(start, size)]


def _chunked(make, src, dst, want="both"):
    rows, cols = src.shape
    row_bytes = cols * jnp.dtype(src.dtype).itemsize
    k = 1
    while rows % (2 * k) == 0 and (rows // (2 * k)) % 16 == 0 and (rows // k) * row_bytes > DMA_CHUNK_BYTES:
        k *= 2
    cr = rows // k
    parts = []
    if want != "wait":
        parts = [make(src.at[pl.ds(j * cr, cr), :], dst.at[pl.ds(j * cr, cr), :]) for j in range(k)]
    return parts, (make(src, dst) if want != "start" else None)


def _half_spec(rows, cols, axis):
    return (0, rows // 2) if axis == 1 else (1, cols // 2)


def _staged(src, remote_dst, local_dst, sibling, load_sems, send_sems, store_sems, recv_sem):
    rows, cols = src.shape
    row_bytes = cols * jnp.dtype(src.dtype).itemsize
    k = 1
    while (rows // k) * row_bytes > STAGE_BYTES and rows % (2 * k) == 0 and (rows // (2 * k)) % 16 == 0:
        k *= 2
    cr = rows // k
    piece = lambda ref, j: ref.at[pl.ds(j * cr, cr), :]

    def run(buf):
        loads = [pltpu.make_async_copy(piece(src, j), buf.at[j % 2], load_sems.at[j % 2]) for j in range(k)]
        outs = [[pltpu.make_async_remote_copy(src_ref=buf.at[j % 2], dst_ref=piece(remote_dst, j),
                                              send_sem=send_sems.at[j % 2], recv_sem=recv_sem,
                                              device_id=sibling, device_id_type=MESH)] for j in range(k)]
        if local_dst is not None:
            for j in range(k):
                outs[j].append(pltpu.make_async_copy(buf.at[j % 2], piece(local_dst, j), store_sems.at[j % 2]))

        def drained(j):
            outs[j][0].wait_send()
            for cp in outs[j][1:]:
                cp.wait()

        loads[0].start()
        for j in range(k):
            loads[j].wait()
            for cp in outs[j]:
                cp.start()
            if j + 1 < k:
                if j >= 1:
                    drained(j - 1)
                loads[j + 1].start()
        for j in range(max(0, k - 2), k):
            drained(j)

    pl.run_scoped(run, pltpu.VMEM((2, cr, cols), src.dtype))


def _landed(ref, recv_sem, send_sem):
    pltpu.make_async_remote_copy(src_ref=ref, dst_ref=ref, send_sem=send_sem, recv_sem=recv_sem,
                                 device_id=(lax.axis_index("x"), lax.axis_index("y"), lax.axis_index("c")),
                                 device_id_type=MESH).wait_recv()


def _stage_pieces(shape, dtype):
    rows, cols = shape
    row_bytes = cols * jnp.dtype(dtype).itemsize
    k = 1
    while (rows // k) * row_bytes > STAGE_BYTES and rows % (2 * k) == 0 and (rows // (2 * k)) % 16 == 0:
        k *= 2
    return k


def _push_all(transfers, sibling, bufs, load_sems, send_sems, store_sems, recv_sems):
    pieces, j = [], 0
    for i, (src, remote_dst, local_dst) in enumerate(transfers):
        k = _stage_pieces(src.shape, src.dtype)
        cr = src.shape[0] // k
        for q in range(k):
            rows = pl.ds(q * cr, cr)
            load = pltpu.make_async_copy(src.at[rows, :], bufs[i].at[rows, :], load_sems.at[j])
            outs = [pltpu.make_async_remote_copy(src_ref=bufs[i].at[rows, :], dst_ref=remote_dst.at[rows, :],
                                                 send_sem=send_sems.at[j], recv_sem=recv_sems.at[i],
                                                 device_id=sibling, device_id_type=MESH)]
            if local_dst is not None:
                outs.append(pltpu.make_async_copy(bufs[i].at[rows, :], local_dst.at[rows, :], store_sems.at[j]))
            pieces.append((load, outs))
            j += 1
    for load, _ in pieces:
        load.start()
    for load, outs in pieces:
        load.wait()
        for cp in outs:
            cp.start()
    for _, outs in pieces:
        outs[0].wait_send()
        for cp in outs[1:]:
            cp.wait()


def _half_slab(ref, spec, chip, half):
    _, rows, cols, axis, size = spec
    haxis, hsize = _half_spec(rows, cols, axis)
    return _slab(_slab(ref, axis, chip, size), haxis, half, hsize)


def _gather_sends(outs, specs, ici_send, ici_recv, want="both"):
    x, y, c, p = _place()
    sent = []
    for wi, spec in enumerate(specs):
        mine = _half_slab(outs[wi], spec, p, c)
        for k, (fx, fy) in enumerate(CHIP_FLIPS):
            sent.append(_chunked(lambda s, d, j=3 * wi + k, fx=fx, fy=fy: pltpu.make_async_remote_copy(
                src_ref=s, dst_ref=d, send_sem=ici_send.at[j], recv_sem=ici_recv.at[j],
                device_id=(_flip(x, fx), _flip(y, fy), c), device_id_type=MESH), mine, mine, want))
    return sent


def _gather_weights(fulls, specs, *, name):
    n = len(specs)

    def body(*refs):
        outs = refs[n:2 * n]
        ici_send, ici_recv, load_sems, d2d_send, d2d_recv = refs[2 * n:]
        x, y, c, _ = _place()
        sent = _gather_sends(outs, specs, ici_send, ici_recv)
        for parts, _ in sent:
            for cp in parts:
                cp.start()
        for k, (fx, fy) in enumerate(CHIP_FLIPS):
            q = 2 * _flip(x, fx) + _flip(y, fy)
            for wi in range(n):
                sent[3 * wi + k][1].wait_recv()
                got = _half_slab(outs[wi], specs[wi], q, c)
                _staged(got, got, None, (x, y, 1 - c), load_sems, d2d_send, None, d2d_recv.at[3 * wi + k])
        for _, whole in sent:
            whole.wait_send()
        for k, (fx, fy) in enumerate(CHIP_FLIPS):
            q = 2 * _flip(x, fx) + _flip(y, fy)
            for wi in range(n):
                _landed(_half_slab(outs[wi], specs[wi], q, 1 - c), d2d_recv.at[3 * wi + k], d2d_send.at[0])

    return pl.pallas_call(
        body, name=name, in_specs=[ANY] * n, out_specs=[ANY] * n,
        out_shape=[jax.ShapeDtypeStruct(f.shape, f.dtype) for f in fulls],
        input_output_aliases={i: i for i in range(n)},
        scratch_shapes=[pltpu.SemaphoreType.DMA((3 * n,)), pltpu.SemaphoreType.DMA((3 * n,)),
                        pltpu.SemaphoreType.DMA((2,)), pltpu.SemaphoreType.DMA((2,)),
                        pltpu.SemaphoreType.DMA((3 * n,))],
    )(*fulls)


def _gather_chips_carry(fulls, specs):
    n = len(specs)

    def issue(ins, outs, sems, want):
        del ins
        return [(parts, whole, True) for parts, whole in _gather_sends(outs, specs, sems[0], sems[1], want)]

    return _Carry(list(fulls), [jax.ShapeDtypeStruct(f.shape, f.dtype) for f in fulls], {i: i for i in range(n)},
                  [pltpu.SemaphoreType.DMA((3 * n,)), pltpu.SemaphoreType.DMA((3 * n,))], issue)


def _gather_pass_on(fulls, specs, *, name):
    n = len(specs)

    def body(*refs):
        outs = refs[n:2 * n]
        load_sems, d2d_send, d2d_recv = refs[2 * n:2 * n + 3]
        bufs = refs[2 * n + 3:]
        x, y, c, _ = _place()
        loads, sends = [], []
        for k, (fx, fy) in enumerate(CHIP_FLIPS):
            q = 2 * _flip(x, fx) + _flip(y, fy)
            for wi in range(n):
                j = 3 * wi + k
                got = _half_slab(outs[wi], specs[wi], q, c)
                loads.append(pltpu.make_async_copy(got, bufs[j], load_sems.at[j]))
                sends.append(pltpu.make_async_remote_copy(
                    src_ref=bufs[j], dst_ref=got, send_sem=d2d_send.at[j], recv_sem=d2d_recv.at[j],
                    device_id=(x, y, 1 - c), device_id_type=MESH))
        for cp in loads:
            cp.start()
        for load, send in zip(loads, sends):
            load.wait()
            send.start()
        for cp in sends:
            cp.wait_send()
        for k, (fx, fy) in enumerate(CHIP_FLIPS):
            q = 2 * _flip(x, fx) + _flip(y, fy)
            for wi in range(n):
                _landed(_half_slab(outs[wi], specs[wi], q, 1 - c), d2d_recv.at[3 * wi + k], d2d_send.at[0])

    shapes = []
    for _, rows, cols, axis, size in specs:
        shapes += [(rows // 2, size) if axis == 1 else (size, cols // 2)] * 3
    assert sum(math.prod(s) for s in shapes) * 2 <= VMEM_LIMIT_V7X // 2
    stages = [pltpu.VMEM(s, BF16) for s in shapes]
    return pl.pallas_call(
        body, name=name, in_specs=[ANY] * n, out_specs=[ANY] * n,
        out_shape=[jax.ShapeDtypeStruct(f.shape, f.dtype) for f in fulls],
        input_output_aliases={i: i for i in range(n)},
        scratch_shapes=[pltpu.SemaphoreType.DMA((3 * n,))] * 3 + stages,
        compiler_params=pltpu.CompilerParams(vmem_limit_bytes=VMEM_LIMIT_V7X),
    )(*fulls)


def _sibling_exchange(grads, *, name):
    n = len(grads)

    def body(*refs):
        ins, outs = refs[:n], refs[n:2 * n]
        load_sems, send_sems, recv_sems = refs[2 * n:2 * n + 3]
        bufs = refs[2 * n + 3:]
        x, y, c, _ = _place()
        _push_all([(_slab(ins[i], haxis, 1 - c, hsize), outs[i], None) for i, (_, haxis, hsize) in enumerate(grads)],
                  (x, y, 1 - c), bufs, load_sems, send_sems, None, recv_sems)
        for i in range(n):
            _landed(outs[i], recv_sems.at[i], send_sems.at[0])

    shapes = [jax.ShapeDtypeStruct((hsize, g.shape[1]) if haxis == 0 else (g.shape[0], hsize), g.dtype)
              for g, haxis, hsize in grads]
    n_pieces = sum(_stage_pieces(sh.shape, sh.dtype) for sh in shapes)
    return pl.pallas_call(
        body, name=name, in_specs=[ANY] * n, out_specs=[ANY] * n, out_shape=shapes,
        scratch_shapes=[pltpu.SemaphoreType.DMA((n_pieces,)), pltpu.SemaphoreType.DMA((n_pieces,)),
                        pltpu.SemaphoreType.DMA((n,))] + [pltpu.VMEM(sh.shape, sh.dtype) for sh in shapes],
        compiler_params=pltpu.CompilerParams(vmem_limit_bytes=VMEM_LIMIT_V7X),
    )(*[g for g, _, _ in grads])


def _chip_exchange_carry(parts):
    n = len(parts)

    def issue(ins, outs, sems, want):
        send_sems, recv_sems, local_sems = sems
        x, y, c, p = _place()
        copies = []
        for i, (_, axis, size) in enumerate(parts):
            copies.append(_chunked(lambda s, d, i=i: pltpu.make_async_copy(s, d, local_sems.at[i]),
                                   _slab(ins[i], axis, p, size), outs[i].at[p], want) + (False,))
            for k, (fx, fy) in enumerate(CHIP_FLIPS):
                px, py = _flip(x, fx), _flip(y, fy)
                copies.append(_chunked(lambda s, d, j=3 * i + k, px=px, py=py: pltpu.make_async_remote_copy(
                    src_ref=s, dst_ref=d, send_sem=send_sems.at[j], recv_sem=recv_sems.at[j],
                    device_id=(px, py, c), device_id_type=MESH),
                    _slab(ins[i], axis, 2 * px + py, size), outs[i].at[p], want) + (True,))
        return copies

    shapes = []
    for a, axis, size in parts:
        shapes.append(jax.ShapeDtypeStruct((4, size, a.shape[1]) if axis == 0 else (4, a.shape[0], size), a.dtype))
    return _Carry([a for a, _, _ in parts], shapes, {},
                  [pltpu.SemaphoreType.DMA((3 * n,)), pltpu.SemaphoreType.DMA((3 * n,)),
                   pltpu.SemaphoreType.DMA((n,))], issue)


def _sibling_share(sums):
    n = len(sums)

    def body(*refs):
        ins, outs = refs[:n], refs[n:2 * n]
        load_sems, send_sems, store_sems, recv_sems = refs[2 * n:2 * n + 4]
        bufs = refs[2 * n + 4:]
        x, y, c, _ = _place()
        places = [_slab(outs[i], haxis, c, s.shape[haxis]) for i, (s, haxis) in enumerate(sums)]
        _push_all([(ins[i], places[i], places[i]) for i in range(n)], (x, y, 1 - c), bufs,
                  load_sems, send_sems, store_sems, recv_sems)
        for i, (s, haxis) in enumerate(sums):
            _landed(_slab(outs[i], haxis, 1 - c, s.shape[haxis]), recv_sems.at[i], send_sems.at[0])

    shapes = []
    for s, haxis in sums:
        r, cc = s.shape
        shapes.append(jax.ShapeDtypeStruct((2 * r, cc) if haxis == 0 else (r, 2 * cc), F32))
    n_pieces = sum(_stage_pieces(s.shape, s.dtype) for s, _ in sums)
    return pl.pallas_call(
        body, name="grad_sibling_share", in_specs=[ANY] * n, out_specs=[ANY] * n, out_shape=shapes,
        scratch_shapes=[pltpu.SemaphoreType.DMA((n_pieces,))] * 3 + [pltpu.SemaphoreType.DMA((n,))]
                       + [pltpu.VMEM(s.shape, s.dtype) for s, _ in sums],
        compiler_params=pltpu.CompilerParams(vmem_limit_bytes=VMEM_LIMIT_V7X),
    )(*[s for s, _ in sums])


class _Exchanges:
    def __init__(self, fulls):
        self.w_in = _gather_weights([fulls["w_in"]], W_SPECS[:1], name="gather_w_in")[0]
        self.late_specs = W_SPECS[1:]
        self.late = [fulls[s[0]] for s in self.late_specs]
        self.slots = {}

    def late_carry(self):
        return _gather_chips_carry(self.late, self.late_specs)

    def late_weights(self, carried):
        done = _gather_pass_on(carried, self.late_specs, name="gather_pass_on")
        return dict(zip([s[0] for s in self.late_specs], done))

    def _half_sums(self, gw, gwb, specs, tag):
        grads, sent = [], []
        for name, _, _, axis, _ in specs:
            for part in (("w_ffn_a", "w_ffn_b") if name == "w_ffn_in" else (name,)):
                half = _half_spec(gw[part].shape[0], gw[part].shape[1], axis)
                grads.append((gw[part],) + half)
                sent.append((gwb[part],) + half)
        theirs = _sibling_exchange(sent, name=f"grad_sibling_exchange_{tag}")
        parts, j = [], 0
        for name, _, _, axis, size in specs:
            take = 2 if name == "w_ffn_in" else 1
            items = [grads[i] + (theirs[i],) for i in range(j, j + take)]
            parts.append((_add_halves(items, name=f"half_sum_{name}"), axis, size))
            j += take
        return parts

    def early_carries(self, gw, gwb):
        parts = self._half_sums(gw, gwb, self.late_specs, "early")
        self.early_names = [s[0] for s in self.late_specs]
        cut = self.early_names.index("w_ffn_in")
        return _chip_exchange_carry(parts[cut:]), _chip_exchange_carry(parts[:cut])

    def early_done(self, carried_bulk, carried_rest):
        cut = self.early_names.index("w_ffn_in")
        self.slots.update(zip(self.early_names[cut:], carried_bulk))
        self.slots.update(zip(self.early_names[:cut], carried_rest))

    def w_in_carry(self, g, gb):
        return _chip_exchange_carry(self._half_sums({"w_in": g}, {"w_in": gb}, W_SPECS[:1], "w_in"))

    def w_in_done(self, carried):
        self.slots["w_in"] = carried[0]

    def reduced(self):
        sums = []
        for name, rows, cols, axis, _ in W_SPECS:
            sums.append((_add_slots(self.slots[name], name=f"chip_sum_{name}"), _half_spec(rows, cols, axis)[0]))
        return dict(zip([s[0] for s in W_SPECS], _sibling_share(sums)))


def _small_allreduce(sv):
    def body(sv_ref, o_ref, slots_ref, send_sems, recv_sems):
        x, y, c, _ = _place()
        me = 4 * x + 2 * y + c
        slots_ref[me] = sv_ref[...]
        copies = []
        for k in range(1, 8):
            fx, fy, fc = (k >> 2) & 1, (k >> 1) & 1, k & 1
            copies.append(pltpu.make_async_remote_copy(
                src_ref=sv_ref, dst_ref=slots_ref.at[me], send_sem=send_sems.at[k - 1],
                recv_sem=recv_sems.at[k - 1], device_id=(_flip(x, fx), _flip(y, fy), _flip(c, fc)),
                device_id_type=MESH))
        for cp in copies:
            cp.start()
        for cp in copies:
            cp.wait_recv()
        for cp in copies:
            cp.wait_send()
        total = slots_ref[0]
        for s in range(1, 8):
            total = total + slots_ref[s]
        o_ref[...] = total

    vm = pl.BlockSpec(memory_space=pltpu.VMEM)
    return pl.pallas_call(
        body, name="small_allreduce", in_specs=[vm], out_specs=vm,
        out_shape=jax.ShapeDtypeStruct(sv.shape, F32),
        scratch_shapes=[pltpu.VMEM((8,) + sv.shape, F32), pltpu.SemaphoreType.DMA((7,)),
                        pltpu.SemaphoreType.DMA((7,))],
    )(sv)


SMALL_ROWS = (("norm_mix_gain", 0), ("norm_mem_gain", 1), ("norm_ffn_gain", 2))
SMALL_LB = (("lb_logits_fw", 3), ("lb_logits_bw", 5))
SMALL_HEAD = ("hg_norm_gain", "da_q_gain", "da_k_gain", "mem_q_gain", "mem_k_gain")
LOSS_COL = HEAD * len(SMALL_HEAD)


def _pack_small(d):
    last = jnp.concatenate([d[n] for n in SMALL_HEAD] + [jnp.zeros((1, D_MODEL - HEAD * len(SMALL_HEAD)), F32)], axis=1)
    return jnp.concatenate([d["norm_mix_gain"], d["norm_mem_gain"], d["norm_ffn_gain"],
                            d["lb_logits_fw"], d["lb_logits_bw"], last], axis=0)


def _unpack_small(a):
    out = {n: a[r:r + 1] for n, r in SMALL_ROWS}
    out.update({n: a[r:r + 2] for n, r in SMALL_LB})
    out.update({n: a[7:8, j * HEAD:(j + 1) * HEAD] for j, n in enumerate(SMALL_HEAD)})
    return out


PARAM_ORDER = ("norm_mix_gain", "norm_mem_gain", "w_in", "lb_logits_fw", "lb_logits_bw", "hg_norm_gain",
               "da_q_gain", "da_k_gain", "w_mem_kv", "mem_q_gain", "mem_k_gain", "w_proj_hg", "w_proj_da",
               "w_proj_mem", "w_out", "norm_ffn_gain", "w_ffn_in", "w_ffn_out")


def kernel(x, mem, norm_mix_gain, norm_mem_gain, w_in, lb_logits_fw, lb_logits_bw, hg_norm_gain, da_q_gain, da_k_gain, w_mem_kv, mem_q_gain, mem_k_gain, w_proj_hg, w_proj_da, w_proj_mem, w_out, norm_ffn_gain, w_ffn_in, w_ffn_out, loss_target, m_norm_mix_gain, m_norm_mem_gain, m_w_in, m_lb_logits_fw, m_lb_logits_bw, m_hg_norm_gain, m_da_q_gain, m_da_k_gain, m_w_mem_kv, m_mem_q_gain, m_mem_k_gain, m_w_proj_hg, m_w_proj_da, m_w_proj_mem, m_w_out, m_norm_ffn_gain, m_w_ffn_in, m_w_ffn_out, v_norm_mix_gain, v_norm_mem_gain, v_w_in, v_lb_logits_fw, v_lb_logits_bw, v_hg_norm_gain, v_da_q_gain, v_da_k_gain, v_w_mem_kv, v_mem_q_gain, v_mem_k_gain, v_w_proj_hg, v_w_proj_da, v_w_proj_mem, v_w_out, v_norm_ffn_gain, v_w_ffn_in, v_w_ffn_out):
    args = dict(locals())
    mats = tuple(s[0] for s in W_SPECS)
    flat = lambda a: a.reshape(a.shape[-2:])
    w = {n: flat(args[n]) for n in mats}
    m = {n: flat(args["m_" + n]) for n in mats}
    v = {n: flat(args["v_" + n]) for n in mats}
    small = {n: args[n] for n in PARAM_ORDER if n not in mats}

    chip = (2 * lax.axis_index("x") + lax.axis_index("y")).astype(jnp.int32).reshape(1)
    ex = _Exchanges({n: _cast_into_full(w[n], chip, rows, cols, axis, name=f"cast_{n}")
                     for n, rows, cols, axis, _ in W_SPECS})
    grad_x, small_grads = _local_step(x[0], mem[0], loss_target[0], small, ex)
    grads = ex.reduced()
    small_sum = _small_allreduce(small_grads)

    delta, new_m, new_v = {}, {}, {}
    for n in mats:
        delta[n], new_m[n], new_v[n], grads[n] = _adamw(w[n], grads[n], m[n], v[n], name=f"adamw_{n}")
    packed = _adamw(_pack_small(small), small_sum,
                    _pack_small({n: args["m_" + n] for n in small}),
                    _pack_small({n: args["v_" + n] for n in small}), name="adamw_small")
    for dst, src in zip((delta, new_m, new_v, grads), packed):
        dst.update(_unpack_small(src))

    def shaped(d, n):
        return d[n].reshape(args[n].shape)

    return (small_sum[7, LOSS_COL], grad_x[None], *[shaped(grads, n) for n in PARAM_ORDER], *[shaped(delta, n) for n in PARAM_ORDER],
            *[shaped(new_m, n) for n in PARAM_ORDER], *[shaped(new_v, n) for n in PARAM_ORDER])
```

```python
import math

import numpy as np
import jax
import jax.numpy as jnp
from jax import lax
from jax.experimental import pallas as pl
from jax.experimental.pallas import tpu as pltpu

F32 = jnp.float32
BF16 = jnp.bfloat16
MESH = pl.DeviceIdType.MESH

D_MODEL = 1024
HEAD = 128
HG_HEADS = 8
DA_GROUPS = ((1, 64), (4, 64), (16, 64))
DA_GH = 4
MEM_HEADS = 4
N_MEM = 256
D_FF = 2816
CHUNK = 64
BAND_QBLOCKS = 16
GLA_HEADS_PER_STEP = 8
RMS_EPS = 1e-6
NEG_INF = -1e30
HG_SCALE = HEAD ** -0.5
ATT_SCALE = HEAD ** -0.5
VMEM_LIMIT_V7X = 48 * 1024 * 1024

C_HQ, C_FF, C_FB, C_HI, C_HG = 0, 1024, 2048, 3072, 4096
C_DQ, C_DK, C_DV, C_MQ = 5120, 6656, 8192, 9728
C_GHG, C_GDA, C_GMEM = 10240, 11264, 12288
IN_COLS = 13312

ADAM_LR, ADAM_B1, ADAM_B2, ADAM_EPS, ADAM_WD, ADAM_STEP = 0.001, 0.9, 0.999, 1e-08, 0.01, 10


def _params(sem, vmem=VMEM_LIMIT_V7X):
    return pltpu.CompilerParams(dimension_semantics=sem, vmem_limit_bytes=vmem)


def _dot(a, b):
    return jnp.dot(a.astype(BF16), b.astype(BF16), preferred_element_type=F32)


def _dot_nt(a, b):
    return lax.dot_general(a.astype(BF16), b.astype(BF16), (((1,), (1,)), ((), ())),
                           preferred_element_type=F32)


def _dot_tn(a, b):
    return lax.dot_general(a.astype(BF16), b.astype(BF16), (((0,), (0,)), ((), ())),
                           preferred_element_type=F32)


def _sigmoid(v):
    return jax.nn.sigmoid(v.astype(F32))


def _ones(rows, cols):
    return (lax.broadcasted_iota(jnp.int32, (rows, cols), 0) >= 0).astype(BF16)


def _lane_sum(v):
    ones = _ones(HEAD, HEAD)
    hi = v.astype(BF16)
    mid = (v - hi.astype(F32)).astype(BF16)
    return jnp.dot(hi, ones, preferred_element_type=F32) + jnp.dot(mid, ones, preferred_element_type=F32)


def _row_mean(v):
    if v.shape[-1] == HEAD:
        return _lane_sum(v) * (1.0 / HEAD)
    return jnp.mean(v, axis=-1, keepdims=True)


def _rms(v):
    v = v.astype(F32)
    r = lax.rsqrt(_row_mean(v * v) + RMS_EPS)
    return v * r, r


def _rms_bwd(dy, xhat, r, gain):
    dxh = dy * gain
    dx = r * (dxh - xhat * _row_mean(dxh * xhat))
    return dx, dy * xhat


class _Carry:
    def __init__(self, arrays, out_shapes, aliases, sems, issue):
        self.arrays, self.out_shapes, self.aliases, self.sems, self.issue = arrays, out_shapes, aliases, sems, issue


NO_CARRY = object()


def _start_copies(copies):
    for parts, _, _ in copies:
        for cp in parts:
            cp.start()


def _wait_copies(copies):
    for _, whole, remote in copies:
        if remote:
            whole.wait_recv()
    for _, whole, remote in copies:
        if remote:
            whole.wait_send()
        else:
            whole.wait()


def _call(body, *, name, grid, in_specs, out_specs, out_shape, scratch_shapes, semantics, ins, carry=None,
          aliases=None):
    aliases = dict(aliases or {})
    if carry is None:
        res = pl.pallas_call(body, name=name, grid=grid, in_specs=in_specs, out_specs=out_specs, out_shape=out_shape,
                             scratch_shapes=scratch_shapes, input_output_aliases=aliases,
                             compiler_params=_params(semantics))(*ins)
        return list(res), []
    n_in, n_out, n_scr = len(ins), len(out_shape), len(scratch_shapes)
    c_in, c_out = len(carry.arrays), len(carry.out_shapes)

    def wrapped(*refs):
        pos = [0]

        def take(count):
            pos[0] += count
            return refs[pos[0] - count:pos[0]]

        own_in, carry_in = take(n_in), take(c_in)
        own_out, carry_out = take(n_out), take(c_out)
        own_scr, carry_sems = take(n_scr), take(len(carry.sems))
        ids = [pl.program_id(a) for a in range(len(grid))]
        first, last = ids[0] == 0, ids[0] == grid[0] - 1
        for a in range(1, len(grid)):
            first, last = first & (ids[a] == 0), last & (ids[a] == grid[a] - 1)

        @pl.when(first)
        def _():
            _start_copies(carry.issue(carry_in, carry_out, carry_sems, "start"))

        body(*own_in, *own_out, *own_scr)

        @pl.when(last)
        def _():
            _wait_copies(carry.issue(carry_in, carry_out, carry_sems, "wait"))

    res = pl.pallas_call(
        wrapped, name=name, grid=grid, in_specs=list(in_specs) + [ANY] * c_in,
        out_specs=list(out_specs) + [ANY] * c_out, out_shape=list(out_shape) + list(carry.out_shapes),
        input_output_aliases={**aliases, **{n_in + i: n_out + j for i, j in carry.aliases.items()}},
        scratch_shapes=list(scratch_shapes) + list(carry.sems),
        compiler_params=_params(("arbitrary",) * len(grid)))(*ins, *carry.arrays)
    return list(res[:n_out]), list(res[n_out:])


def _matmul(pairs, mode, out_dtype, *, tm, tn, tk=None, residual=None, also=None, carry=NO_CARRY, name):
    b_offs = [pr[2] if len(pr) > 2 else 0 for pr in pairs]
    pairs = [pr[:2] for pr in pairs]
    a0, b0 = pairs[0]
    if mode == "nn":
        (m, kk), n = a0.shape, b0.shape[1]
    elif mode == "nt":
        (m, kk), n = a0.shape, b0.shape[0]
    else:
        (kk, m), n = a0.shape, b0.shape[1]
    assert mode == "nt" or not any(b_offs)
    tm, tn = min(tm, m), min(tn, n)
    tk = kk if tk is None else tk
    nk = kk // tk
    assert m % tm == 0 and n % tn == 0 and kk % tk == 0, (name, m, n, kk)
    n_p = len(pairs)
    if mode == "tn":
        a_spec = pl.BlockSpec((tk, tm), lambda i, j, k: (k, i))
    else:
        a_spec = pl.BlockSpec((tm, tk), lambda i, j, k: (i, k))
    if mode == "nt":
        b_specs = [pl.BlockSpec((tn, tk), lambda i, j, k, o=o: (j, o * nk + k)) for o in b_offs]
    else:
        b_specs = [pl.BlockSpec((tk, tn), lambda i, j, k: (k, j))] * n_p
    o_spec = pl.BlockSpec((tm, tn), lambda i, j, k: (i, j))
    dot = {"nn": _dot, "nt": _dot_nt, "tn": _dot_tn}[mode]
    has_res = residual is not None

    def body(*refs):
        a_refs, b_refs = refs[:n_p], refs[n_p:2 * n_p]
        pos = 2 * n_p
        res_ref = refs[pos] if has_res else None
        pos += int(has_res)
        o_ref = refs[pos]
        pos += int(also is not None)
        part = dot(a_refs[0][...], b_refs[0][...])
        for a_r, b_r in zip(a_refs[1:], b_refs[1:]):
            part += dot(a_r[...], b_r[...])

        def finish(total):
            if has_res:
                total = total + res_ref[...]
            o_ref[...] = total.astype(out_dtype)
            if also is not None:
                refs[pos][...] = total.astype(also)

        if nk == 1:
            finish(part)
        else:
            acc_ref = refs[pos + 1]
            k = pl.program_id(2)

            @pl.when(k == 0)
            def _():
                acc_ref[...] = part

            @pl.when(k > 0)
            def _():
                acc_ref[...] += part

            @pl.when(k == nk - 1)
            def _():
                finish(acc_ref[...])

    ins = [a for a, _ in pairs] + [b for _, b in pairs]
    in_specs = [a_spec] * n_p + b_specs
    if has_res:
        ins.append(residual)
        in_specs.append(o_spec)
    dtypes = [out_dtype] + ([also] if also is not None else [])
    outs, carried = _call(
        body, name=name, grid=(m // tm, n // tn, nk), in_specs=in_specs, out_specs=[o_spec] * len(dtypes),
        out_shape=[jax.ShapeDtypeStruct((m, n), dt) for dt in dtypes],
        scratch_shapes=[pltpu.VMEM((tm, tn), F32)] if nk > 1 else [],
        semantics=("parallel", "parallel", "arbitrary"), ins=ins, carry=None if carry is NO_CARRY else carry)
    out = outs[0] if also is None else tuple(outs)
    return out if carry is NO_CARRY else (out, carried)


def _rms_fwd(x, gain, *, name):
    rows, dm = x.shape
    tm = min(1024, rows)

    def body(x_ref, g_ref, h_ref):
        xhat, _ = _rms(x_ref[...])
        h_ref[...] = (xhat * g_ref[...]).astype(BF16)

    return pl.pallas_call(
        body, name=name, grid=(rows // tm,),
        in_specs=[pl.BlockSpec((tm, dm), lambda i: (i, 0)), pl.BlockSpec((1, dm), lambda i: (0, 0))],
        out_specs=pl.BlockSpec((tm, dm), lambda i: (i, 0)),
        out_shape=jax.ShapeDtypeStruct((rows, dm), BF16),
        compiler_params=_params(("parallel",)),
    )(x, gain)


def _rms_bwd_rows(dh, x, gain, dres, dx_dtypes, *, name):
    rows, dm = x.shape
    tm = min(512, rows)
    has_res = dres is not None

    def body(*refs):
        dh_ref, x_ref, g_ref = refs[:3]
        res_ref = refs[3] if has_res else None
        outs = refs[3 + int(has_res):]
        dg_ref = outs[-1]
        xhat, r = _rms(x_ref[...])
        dx, dgr = _rms_bwd(dh_ref[...], xhat, r, g_ref[...])
        if has_res:
            dx = dx + res_ref[...]
        for dx_ref, dt in zip(outs, dx_dtypes):
            dx_ref[...] = dx.astype(dt)

        @pl.when(pl.program_id(0) == 0)
        def _():
            dg_ref[...] = jnp.zeros_like(dg_ref)

        dg_ref[...] += jnp.sum(dgr, axis=0, keepdims=True)

    row = pl.BlockSpec((tm, dm), lambda i: (i, 0))
    vec = pl.BlockSpec((1, dm), lambda i: (0, 0))
    return pl.pallas_call(
        body, name=name, grid=(rows // tm,),
        in_specs=[row, row, vec] + ([row] if has_res else []),
        out_specs=[row] * len(dx_dtypes) + [vec],
        out_shape=[jax.ShapeDtypeStruct((rows, dm), dt) for dt in dx_dtypes] + [jax.ShapeDtypeStruct((1, dm), F32)],
        compiler_params=_params(("arbitrary",)),
    )(*([dh, x, gain] + ([dres] if has_res else [])))


def _out_norm(a, w, x, gain, *, name):
    rows, dm = x.shape
    tm = min(512, rows)

    def body(a_ref, w_ref, x_ref, g_ref, x1_ref, h_ref):
        x1 = x_ref[...] + _dot(a_ref[...], w_ref[...])
        x1_ref[...] = x1
        xhat, _ = _rms(x1)
        h_ref[...] = (xhat * g_ref[...]).astype(BF16)

    row = lambda wd: pl.BlockSpec((tm, wd), lambda i: (i, 0))
    return pl.pallas_call(
        body, name=name, grid=(rows // tm,),
        in_specs=[row(a.shape[1]), pl.BlockSpec(w.shape, lambda i: (0, 0)), row(dm), pl.BlockSpec((1, dm), lambda i: (0, 0))],
        out_specs=[row(dm), row(dm)],
        out_shape=[jax.ShapeDtypeStruct((rows, dm), F32), jax.ShapeDtypeStruct((rows, dm), BF16)],
        compiler_params=_params(("parallel",)),
    )(a, w, x, gain)


def _out_loss(u, w, x1, tgt, *, name):
    rows, dm = x1.shape
    tm = min(256, rows)
    steps = rows // tm

    def body(u_ref, w_ref, x_ref, t_ref, dy_ref, dyb_ref, loss_ref, acc_ref):
        i = pl.program_id(0)
        diff = x_ref[...] + _dot(u_ref[...], w_ref[...]) - t_ref[...]
        dy = diff * (1.0 / dm)
        dy_ref[...] = dy
        dyb_ref[...] = dy.astype(BF16)

        @pl.when(i == 0)
        def _():
            acc_ref[...] = jnp.zeros_like(acc_ref)

        acc_ref[...] += jnp.sum(diff * diff, axis=0, keepdims=True)

        @pl.when(i == steps - 1)
        def _():
            loss_ref[...] = jnp.full((1, HEAD), 0.5 / dm, F32) * jnp.sum(acc_ref[...])

    row = lambda wd: pl.BlockSpec((tm, wd), lambda i: (i, 0))
    return pl.pallas_call(
        body, name=name, grid=(steps,),
        in_specs=[row(u.shape[1]), pl.BlockSpec(w.shape, lambda i: (0, 0)), row(dm), row(dm)],
        out_specs=[row(dm), row(dm), pl.BlockSpec((1, HEAD), lambda i: (0, 0))],
        out_shape=[jax.ShapeDtypeStruct((rows, dm), F32), jax.ShapeDtypeStruct((rows, dm), BF16),
                   jax.ShapeDtypeStruct((1, HEAD), F32)],
        scratch_shapes=[pltpu.VMEM((1, dm), F32)],
        compiler_params=_params(("arbitrary",)),
    )(u, w, x1, tgt)


def _gla_block_terms(q_raw, f_logit, lb, rev):
    sig = _sigmoid(f_logit)
    forget = lb + (1.0 - lb) * sig
    k = 1.0 - forget
    b = _chunk_cumsum(jnp.log(forget), rev)
    qs = _sigmoid(q_raw)
    eb = jnp.exp(b)
    emb = jnp.exp(-b)
    qt = (q_raw * qs * HG_SCALE) * eb
    kt = k * emb
    return sig, forget, k, b, qs, eb, emb, qt, kt


def _chunk_cumsum(v, rev):
    n = v.shape[0]
    pos = lax.broadcasted_iota(jnp.int32, v.shape, 0) & (CHUNK - 1)
    step = 1
    while step < CHUNK:
        if rev:
            shifted, keep = pltpu.roll(v, n - step, 0), pos < CHUNK - step
        else:
            shifted, keep = pltpu.roll(v, step, 0), pos >= step
        v = v + jnp.where(keep, shifted, 0.0)
        step *= 2
    return v


def _tri_mask(n, rev):
    row, col = np.arange(n)[:, None], np.arange(n)[None, :]
    same = (row // CHUNK) == (col // CHUNK)
    return jnp.asarray((same & ((row <= col) if rev else (row >= col))).astype(np.float32))


def _chunk_order(ncb, rev):
    order = range(ncb - 1, -1, -1) if rev else range(ncb)
    return [(c, c * CHUNK if rev else c * CHUNK + CHUNK - 1) for c in order]


def _gla_fwd(proj, lb, *, f_off, rev, name):
    rows = proj.shape[0]
    tb = min(256, rows)
    nb, ncb = rows // tb, tb // CHUNK

    def tmap(n):
        return nb - 1 - n if rev else n

    def body(tri_ref, q_ref, f_ref, v_ref, lb_ref, o_ref, st_ref, s_ref):
        @pl.when(pl.program_id(1) == 0)
        def _():
            s_ref[...] = jnp.zeros_like(s_ref)

        tri = tri_ref[...] > 0.5
        for hh in range(GLA_HEADS_PER_STEP):
            cs = slice(hh * HEAD, (hh + 1) * HEAD)
            v = v_ref[:, cs]
            _, _, k, b, _, _, _, qt, kt = _gla_block_terms(q_ref[:, cs].astype(F32), f_ref[:, cs].astype(F32),
                                                           lb_ref[:, cs], rev)
            qt = qt.astype(BF16)
            o_intra = _dot(jnp.where(tri, _dot_nt(qt, kt), 0.0), v)
            chunks = []
            for c, last in _chunk_order(ncb, rev):
                sl = slice(c * CHUNK, (c + 1) * CHUNK)
                bl = b[last:last + 1, :]
                kh = k[sl] * jnp.exp(bl - b[sl])
                chunks.append((c, sl, jnp.exp(bl), _dot_tn(v[sl], kh)))
            s_t = s_ref[hh]
            for c, sl, ebl, kv in chunks:
                st_ref[hh, c] = s_t
                o_ref[sl, cs] = o_intra[sl] + _dot_nt(qt[sl], s_t)
                s_t = ebl * s_t + kv
            s_ref[hh] = s_t

    hps = GLA_HEADS_PER_STEP
    col = lambda off: pl.BlockSpec((tb, hps * HEAD), lambda h, n: (tmap(n), off // (hps * HEAD) + h))
    return pl.pallas_call(
        body, name=name, grid=(HG_HEADS // hps, nb),
        in_specs=[pl.BlockSpec((tb, tb), lambda h, n: (0, 0)), col(C_HQ), col(f_off), col(C_HI),
                  pl.BlockSpec((1, hps * HEAD), lambda h, n: (0, h))],
        out_specs=[pl.BlockSpec((tb, hps * HEAD), lambda h, n: (tmap(n), h)),
                   pl.BlockSpec((hps, ncb, HEAD, HEAD), lambda h, n: (h, tmap(n), 0, 0))],
        out_shape=[jax.ShapeDtypeStruct((rows, HG_HEADS * HEAD), F32),
                   jax.ShapeDtypeStruct((HG_HEADS, rows // CHUNK, HEAD, HEAD), F32)],
        scratch_shapes=[pltpu.VMEM((hps, HEAD, HEAD), F32)],
        compiler_params=_params(("parallel", "arbitrary")),
    )(_tri_mask(tb, rev), proj, proj, proj, lb)


def _gla_bwd(proj, lb, do, states, prev, dproj, *, f_off, rev, carry=NO_CARRY, name):
    rows = proj.shape[0]
    tb = min(256, rows)
    nb, ncb = rows // tb, tb // CHUNK
    has_prev = prev is not None
    qv_dtype = BF16 if has_prev else F32

    def tmap(n):
        return n if rev else nb - 1 - n

    def body(*refs):
        tri_ref, q_ref, f_ref, v_ref, lb_ref, do_ref, st_ref = refs[:7]
        pq_ref, pv_ref = refs[8:10] if has_prev else (None, None)
        (dq_ref, df_ref, dv_ref, dlb_ref,
         ds_ref, dqt_scr, dk_scr, db_scr, dbl_scr, dv_scr) = refs[8 + 2 * int(has_prev):]

        @pl.when(pl.program_id(1) == 0)
        def _():
            ds_ref[...] = jnp.zeros_like(ds_ref)
            dlb_ref[...] = jnp.zeros_like(dlb_ref)

        tri = tri_ref[...] > 0.5
        for hh in range(GLA_HEADS_PER_STEP):
            cs = slice(hh * HEAD, (hh + 1) * HEAD)
            lbv = lb_ref[:, cs]
            q_raw = q_ref[:, cs].astype(F32)
            v, dout = v_ref[:, cs], do_ref[:, cs].astype(BF16)
            sig, forget, k, b, qs, eb, emb, qt, kt = _gla_block_terms(q_raw, f_ref[:, cs].astype(F32), lbv, rev)
            qt_b, kt_b = qt.astype(BF16), kt.astype(BF16)
            a = jnp.where(tri, _dot_nt(qt_b, kt_b), 0.0)
            da = jnp.where(tri, _dot_nt(dout, v), 0.0).astype(BF16)
            dv_intra = _dot_tn(a, dout)
            dqt_intra = _dot(da, kt_b)
            dkt = _dot_tn(da, qt_b)
            chunks = []
            for c, last in reversed(_chunk_order(ncb, rev)):
                sl = slice(c * CHUNK, (c + 1) * CHUNK)
                bl = b[last:last + 1, :]
                e = jnp.exp(bl - b[sl])
                s_t = st_ref[hh, c]
                dqt_scr[sl, cs] = dqt_intra[sl] + _dot(dout[sl], s_t)
                chunks.append((sl, jnp.exp(bl), e, k[sl] * e, s_t, _dot_tn(dout[sl], qt_b[sl])))
            ds_t = ds_ref[hh]
            for sl, ebl, e, kh, s_t, grow in chunks:
                dkh = _dot(v[sl], ds_t)
                dv_scr[sl, cs] = dv_intra[sl] + _dot_nt(kh, ds_t)
                dk_scr[sl, cs] = dkt[sl] * emb[sl] + dkh * e
                khd = kh * dkh
                dbl = jnp.sum(khd, axis=0, keepdims=True) + ebl * jnp.sum(ds_t * s_t, axis=0, keepdims=True)
                db_scr[sl, cs] = khd
                dbl_scr[sl, cs] = jnp.broadcast_to(dbl, (CHUNK, HEAD))
                ds_t = grow + ds_t * ebl
            ds_ref[hh] = ds_t
            dqt = dqt_scr[:, cs]
            dlogf = _chunk_cumsum(qt * dqt - kt * dkt - db_scr[:, cs], not rev) + dbl_scr[:, cs]
            dforget = dlogf / forget - dk_scr[:, cs]
            df_ref[:, cs] = (dforget * (1.0 - lbv) * sig * (1.0 - sig)).astype(BF16)
            dlb_ref[:, cs] += jnp.sum(dforget * (1.0 - sig), axis=0, keepdims=True)
            dqr = dqt * eb * (HG_SCALE * qs * (1.0 + q_raw * (1.0 - qs)))
            dv = dv_scr[:, cs]
            if has_prev:
                dqr = dqr + pq_ref[:, cs]
                dv = dv + pv_ref[:, cs]
            dq_ref[:, cs] = dqr.astype(qv_dtype)
            dv_ref[:, cs] = dv.astype(qv_dtype)

    hps = GLA_HEADS_PER_STEP
    col = lambda off: pl.BlockSpec((tb, hps * HEAD), lambda h, n: (tmap(n), off // (hps * HEAD) + h))
    blk = pl.BlockSpec((tb, hps * HEAD), lambda h, n: (tmap(n), h))
    vec = pl.BlockSpec((1, hps * HEAD), lambda h, n: (0, h))
    wide = HG_HEADS * HEAD
    into = jax.ShapeDtypeStruct(dproj.shape, dproj.dtype)
    outs, carried = _call(
        body, name=name, grid=(HG_HEADS // hps, nb),
        in_specs=[pl.BlockSpec((tb, tb), lambda h, n: (0, 0)), col(C_HQ), col(f_off), col(C_HI), vec, blk,
                  pl.BlockSpec((hps, ncb, HEAD, HEAD), lambda h, n: (h, tmap(n), 0, 0)), ANY]
                 + ([blk, blk] if has_prev else []),
        out_specs=[col(C_HQ) if has_prev else blk, blk if has_prev else col(f_off), blk, vec],
        out_shape=[into if has_prev else jax.ShapeDtypeStruct((rows, wide), qv_dtype),
                   jax.ShapeDtypeStruct((rows, wide), BF16) if has_prev else into,
                   jax.ShapeDtypeStruct((rows, wide), qv_dtype), jax.ShapeDtypeStruct((1, wide), F32)],
        aliases={7: 0 if has_prev else 1},
        scratch_shapes=[pltpu.VMEM((hps, HEAD, HEAD), F32)] + [pltpu.VMEM((tb, hps * HEAD), F32)] * 5,
        semantics=("parallel", "arbitrary"),
        ins=[_tri_mask(tb, rev), proj, proj, proj, lb, do, states, dproj] + (list(prev) if has_prev else []),
        carry=None if carry is NO_CARRY else carry)
    return outs if carry is NO_CARRY else (outs, carried)


def _fill_columns(dproj, parts, col, *, name):
    rows, wd = parts[0].shape
    tm = min(1024, rows)
    n = len(parts)

    def body(*refs):
        o_ref = refs[n + 1]
        for j in range(n):
            o_ref[:, j * wd:(j + 1) * wd] = refs[j][...]

    return pl.pallas_call(
        body, name=name, grid=(rows // tm,),
        in_specs=[pl.BlockSpec((tm, wd), lambda i: (i, 0))] * n + [ANY],
        out_specs=pl.BlockSpec((pl.Element(tm), pl.Element(n * wd)), lambda i: (i * tm, col)),
        out_shape=jax.ShapeDtypeStruct(dproj.shape, dproj.dtype), input_output_aliases={n: 0},
        compiler_params=_params(("parallel",)),
    )(*parts, dproj)


def _hg_out_fwd(o_fw, o_bw, proj, gain, *, name):
    rows = o_fw.shape[0]
    tm = min(1024, rows)
    wide = HG_HEADS * HEAD

    def body(a_ref, b_ref, g_ref, gain_ref, o_ref):
        for h in range(HG_HEADS):
            sl = slice(h * HEAD, (h + 1) * HEAD)
            xhat, _ = _rms(a_ref[:, sl] + b_ref[:, sl])
            gate = g_ref[:, sl].astype(F32)
            o_ref[:, sl] = (xhat * gain_ref[...] * (gate * _sigmoid(gate))).astype(BF16)

    row = pl.BlockSpec((tm, wide), lambda i: (i, 0))
    return pl.pallas_call(
        body, name=name, grid=(rows // tm,),
        in_specs=[row, row, pl.BlockSpec((tm, wide), lambda i: (i, C_HG // wide)),
                  pl.BlockSpec((1, HEAD), lambda i: (0, 0))],
        out_specs=row, out_shape=jax.ShapeDtypeStruct((rows, wide), BF16),
        compiler_params=_params(("parallel",)),
    )(o_fw, o_bw, proj, gain)


def _hg_out_bwd(dout, o_fw, o_bw, proj, gain, dproj, *, name):
    rows = o_fw.shape[0]
    tm = min(512, rows)
    wide = HG_HEADS * HEAD

    def body(d_ref, a_ref, b_ref, g_ref, gain_ref, dp_in, dgate_ref, do_ref, dgain_ref):
        del dp_in

        @pl.when(pl.program_id(0) == 0)
        def _():
            dgain_ref[...] = jnp.zeros_like(dgain_ref)

        dgain = jnp.zeros((1, HEAD), F32)
        for h in range(HG_HEADS):
            sl = slice(h * HEAD, (h + 1) * HEAD)
            xhat, r = _rms(a_ref[:, sl] + b_ref[:, sl])
            gate, dy = g_ref[:, sl].astype(F32), d_ref[:, sl]
            gs = _sigmoid(gate)
            dgate_ref[:, sl] = (dy * xhat * gain_ref[...] * (gs * (1.0 + gate * (1.0 - gs)))).astype(BF16)
            dx, dgr = _rms_bwd(dy * (gate * gs), xhat, r, gain_ref[...])
            do_ref[:, sl] = dx
            dgain = dgain + jnp.sum(dgr, axis=0, keepdims=True)
        dgain_ref[...] += dgain

    row = pl.BlockSpec((tm, wide), lambda i: (i, 0))
    gate = pl.BlockSpec((tm, wide), lambda i: (i, C_HG // wide))
    vec = pl.BlockSpec((1, HEAD), lambda i: (0, 0))
    return pl.pallas_call(
        body, name=name, grid=(rows // tm,),
        in_specs=[row, row, row, gate, vec, ANY],
        out_specs=[gate, row, vec],
        out_shape=[jax.ShapeDtypeStruct(dproj.shape, dproj.dtype), jax.ShapeDtypeStruct((rows, wide), F32),
                   jax.ShapeDtypeStruct((1, HEAD), F32)],
        input_output_aliases={5: 0},
        compiler_params=_params(("arbitrary",)),
    )(dout, o_fw, o_bw, proj, gain, dproj)


def _strided_rows(ref, r, count, d):
    return ref[...] if d == 1 else ref[pl.ds(r, count, stride=d), :]


def _store_strided(ref, r, count, d, val):
    if d == 1:
        ref[...] = val
    else:
        ref[pl.ds(r, count, stride=d), :] = val


def _da_prep(proj, q_gain, k_gain, g, *, name):
    d = DA_GROUPS[g][0]
    rows = proj.shape[0]
    rb = min(1024, rows)
    tn = rb // d

    def body(q_ref, k_ref, v_ref, qg_ref, kg_ref, qo_ref, ko_ref, vo_ref, qf_ref, kf_ref, vf_ref):
        for h in range(DA_GH):
            cs = slice(h * HEAD, (h + 1) * HEAD)
            for src, dst in ((q_ref, qf_ref), (k_ref, kf_ref), (v_ref, vf_ref)):
                dst[...] = src[:, cs].astype(F32)
            for r in range(d):
                qhat, _ = _rms(_strided_rows(qf_ref, r, tn, d))
                khat, _ = _rms(_strided_rows(kf_ref, r, tn, d))
                qo_ref[h, r] = (qhat * qg_ref[...]).astype(BF16)
                ko_ref[h, r] = (khat * kg_ref[...]).astype(BF16)
                vo_ref[h, r] = _strided_rows(vf_ref, r, tn, d).astype(BF16)

    wide = DA_GH * HEAD
    col = lambda off: pl.BlockSpec((rb, wide), lambda i: (i, off // wide + g))
    vec = pl.BlockSpec((1, HEAD), lambda i: (0, 0))
    out = pl.BlockSpec((DA_GH, d, tn, HEAD), lambda i: (0, 0, i, 0))
    shape = jax.ShapeDtypeStruct((DA_GH, d, rows // d, HEAD), BF16)
    return pl.pallas_call(
        body, name=name, grid=(rows // rb,),
        in_specs=[col(C_DQ), col(C_DK), col(C_DV), vec, vec],
        out_specs=[out, out, out], out_shape=[shape, shape, shape],
        scratch_shapes=[pltpu.VMEM((rb, HEAD), F32)] * 3,
        compiler_params=_params(("parallel",)),
    )(proj, proj, proj, q_gain, k_gain)


def _slopes(g):
    idx = np.arange(g * DA_GH + 1, (g + 1) * DA_GH + 1)
    s = (2.0 ** (-8.0 * idx / (DA_GH * len(DA_GROUPS)))).astype(np.float32)
    return jnp.asarray(np.broadcast_to(s[:, None, None], (DA_GH, 8, HEAD)).copy())


def _band_window(ld, t, radius):
    win = min(2 * t, ld)
    assert t // 2 >= radius or win == ld
    return win


def _band_scores(q, k, q0, start, slope, d, radius):
    t, win = q.shape[0], k.shape[0]
    row = lax.broadcasted_iota(jnp.int32, (t, win), 0)
    col = lax.broadcasted_iota(jnp.int32, (t, win), 1)
    rel = jnp.abs((start - q0) + col - row)
    return _dot_nt(q, k) * ATT_SCALE + jnp.where(rel <= radius, -slope * (d * rel).astype(F32), NEG_INF)


def _band_fwd(qr, kr, vr, g, *, name):
    d, radius = DA_GROUPS[g]
    _, _, ld, _ = qr.shape
    t = min(HEAD, ld)
    qb = min(BAND_QBLOCKS, ld // t)
    rps = max(1, min(d, BAND_QBLOCKS // qb))
    win = _band_window(ld, t, radius)

    def body(q_ref, k_ref, v_ref, sl_ref, o_ref, lse_ref):
        i = pl.program_id(2)
        slope = sl_ref[0:1, 0:1]
        for rr in range(rps):
            for j in range(qb):
                sl = slice(j * t, (j + 1) * t)
                q0 = (i * qb + j) * t
                start = pl.multiple_of(jnp.clip(q0 - t // 2, 0, ld - win), t // 2)
                s = _band_scores(q_ref[rr, sl, :], k_ref[rr, pl.ds(start, win), :], q0, start, slope, d, radius)
                m = jnp.max(s, axis=-1, keepdims=True)
                p = jnp.exp(s - m).astype(BF16)
                l = jnp.dot(p, _ones(win, HEAD), preferred_element_type=F32)
                o_ref[rr, sl, :] = _dot(p, v_ref[rr, pl.ds(start, win), :]) / l
                lse_ref[rr, sl, :] = m + jnp.log(l)

    own = pl.BlockSpec((None, rps, qb * t, HEAD), lambda h, r, i: (h, r, i, 0))
    seq = pl.BlockSpec((None, rps, ld, HEAD), lambda h, r, i: (h, r, 0, 0))
    shape = jax.ShapeDtypeStruct(qr.shape, F32)
    return pl.pallas_call(
        body, name=name, grid=(DA_GH, d // rps, ld // (qb * t)),
        in_specs=[own, seq, seq, pl.BlockSpec((None, 8, HEAD), lambda h, r, i: (h, 0, 0))],
        out_specs=[own, own], out_shape=[shape, shape],
        compiler_params=_params(("parallel", "parallel", "parallel")),
    )(qr, kr, vr, _slopes(g))


def _band_bwd(qr, kr, vr, dor, lser, deltar, g, *, name):
    d, radius = DA_GROUPS[g]
    _, _, ld, _ = qr.shape
    t = min(HEAD, ld)
    qb = min(BAND_QBLOCKS, ld // t)
    rps = max(1, min(d, BAND_QBLOCKS // qb))
    win = _band_window(ld, t, radius)

    def body(q_ref, k_ref, v_ref, do_ref, lse_ref, dl_ref, sl_ref, dq_ref, dk_ref, dv_ref):
        i = pl.program_id(2)

        @pl.when(i == 0)
        def _():
            dk_ref[...] = jnp.zeros_like(dk_ref)
            dv_ref[...] = jnp.zeros_like(dv_ref)

        slope = sl_ref[0:1, 0:1]
        for rr in range(rps):
            for j in range(qb):
                sl = slice(j * t, (j + 1) * t)
                q0 = (i * qb + j) * t
                start = pl.multiple_of(jnp.clip(q0 - t // 2, 0, ld - win), t // 2)
                rows = pl.ds(start, win)
                q, dout, k, v = q_ref[rr, sl, :], do_ref[rr, sl, :], k_ref[rr, rows, :], v_ref[rr, rows, :]
                p = jnp.exp(_band_scores(q, k, q0, start, slope, d, radius) - lse_ref[rr, sl, 0:1])
                ds = p * (_dot_nt(dout, v) - dl_ref[rr, sl, 0:1])
                dq_ref[rr, sl, :] = _dot(ds, k) * ATT_SCALE
                dk_ref[rr, rows, :] += _dot_tn(ds, q) * ATT_SCALE
                dv_ref[rr, rows, :] += _dot_tn(p, dout)

    own = pl.BlockSpec((None, rps, qb * t, HEAD), lambda h, r, i: (h, r, i, 0))
    seq = pl.BlockSpec((None, rps, ld, HEAD), lambda h, r, i: (h, r, 0, 0))
    shape = jax.ShapeDtypeStruct(qr.shape, F32)
    return pl.pallas_call(
        body, name=name, grid=(DA_GH, d // rps, ld // (qb * t)),
        in_specs=[own, seq, seq, own, own, own, pl.BlockSpec((None, 8, HEAD), lambda h, r, i: (h, 0, 0))],
        out_specs=[own, seq, seq], out_shape=[shape, shape, shape],
        compiler_params=_params(("parallel", "parallel", "arbitrary")),
    )(qr, kr, vr, dor, lser, deltar, _slopes(g))


def _da_merge(outs, lses, rows, *, name):
    rb = min(1024, rows)
    wide = DA_GH * HEAD

    def body(*refs):
        o_refs, l_refs = refs[0:3], refs[3:6]
        o_ref, lse_ref = refs[6:8]
        on_refs, ln_refs = refs[8:11], refs[11:14]
        for h in range(DA_GH):
            cs = slice(h * HEAD, (h + 1) * HEAD)
            for g, (d, _) in enumerate(DA_GROUPS):
                tn = rb // d
                for r in range(d):
                    _store_strided(on_refs[g], r, tn, d, o_refs[g][h, r])
                    _store_strided(ln_refs[g], r, tn, d, l_refs[g][h, r])
            l0, l1, l2 = ln_refs[0][...], ln_refs[1][...], ln_refs[2][...]
            m = jnp.maximum(jnp.maximum(l0, l1), l2)
            e0, e1, e2 = jnp.exp(l0 - m), jnp.exp(l1 - m), jnp.exp(l2 - m)
            tot = e0 + e1 + e2
            o_ref[:, cs] = (e0 * on_refs[0][...] + e1 * on_refs[1][...] + e2 * on_refs[2][...]) / tot
            lse_ref[:, cs] = m + jnp.log(tot)

    res = lambda d: pl.BlockSpec((DA_GH, d, rb // d, HEAD), lambda i: (0, 0, i, 0))
    nat = pl.BlockSpec((rb, wide), lambda i: (i, 0))
    shape = jax.ShapeDtypeStruct((rows, wide), F32)
    return pl.pallas_call(
        body, name=name, grid=(rows // rb,),
        in_specs=[res(d) for d, _ in DA_GROUPS] * 2,
        out_specs=[nat, nat], out_shape=[shape, shape],
        scratch_shapes=[pltpu.VMEM((rb, HEAD), F32)] * 6,
        compiler_params=_params(("parallel",)),
    )(*outs, *lses)


def _da_bwd_prep(dout, o, lse, *, name):
    rows = o.shape[0]
    rb = min(512, rows)
    wide = DA_GH * HEAD

    def body(d_ref, o_ref, l_ref, *outs):
        d_scr, l_scr, delta_scr = outs[9:12]
        for h in range(DA_GH):
            cs = slice(h * HEAD, (h + 1) * HEAD)
            dv = d_ref[:, cs]
            d_scr[...] = dv
            l_scr[...] = l_ref[:, cs]
            delta_scr[...] = _lane_sum(dv * o_ref[:, cs])
            for g, (d, _) in enumerate(DA_GROUPS):
                tn = rb // d
                for r in range(d):
                    outs[3 * g][h, r] = _strided_rows(d_scr, r, tn, d).astype(BF16)
                    outs[3 * g + 1][h, r] = _strided_rows(l_scr, r, tn, d)
                    outs[3 * g + 2][h, r] = _strided_rows(delta_scr, r, tn, d)

    nat = pl.BlockSpec((rb, wide), lambda i: (i, 0))
    out_specs, out_shape = [], []
    for d, _ in DA_GROUPS:
        for dt in (BF16, F32, F32):
            out_specs.append(pl.BlockSpec((DA_GH, d, rb // d, HEAD), lambda i: (0, 0, i, 0)))
            out_shape.append(jax.ShapeDtypeStruct((DA_GH, d, rows // d, HEAD), dt))
    return pl.pallas_call(
        body, name=name, grid=(rows // rb,),
        in_specs=[nat, nat, nat], out_specs=out_specs, out_shape=out_shape,
        scratch_shapes=[pltpu.VMEM((rb, HEAD), F32)] * 3,
        compiler_params=_params(("parallel",)),
    )(dout, o, lse)


def _da_prep_bwd(dqr, dkr, dvr, proj, q_gain, k_gain, g, *, name):
    d = DA_GROUPS[g][0]
    rows = proj.shape[0]
    rb = min(1024, rows)
    tn = rb // d
    wide = DA_GH * HEAD

    def body(dq_ref, dk_ref, dv_ref, q_ref, k_ref, qg_ref, kg_ref, oq_ref, ok_ref, ov_ref, gq_ref, gk_ref, *nat_refs):
        @pl.when(pl.program_id(0) == 0)
        def _():
            gq_ref[...] = jnp.zeros_like(gq_ref)
            gk_ref[...] = jnp.zeros_like(gk_ref)

        for h in range(DA_GH):
            cs = slice(h * HEAD, (h + 1) * HEAD)
            for j, src in enumerate((dq_ref, dk_ref, dv_ref)):
                for r in range(d):
                    _store_strided(nat_refs[j], r, tn, d, src[h, r])
            ov_ref[:, cs] = nat_refs[2][...].astype(BF16)
            for j, (x_ref, gn_ref, out_ref, acc_ref) in enumerate(((q_ref, qg_ref, oq_ref, gq_ref),
                                                                    (k_ref, kg_ref, ok_ref, gk_ref))):
                xhat, r = _rms(x_ref[:, cs])
                dx, dgr = _rms_bwd(nat_refs[j][...], xhat, r, gn_ref[...])
                out_ref[:, cs] = dx.astype(BF16)
                acc_ref[...] += jnp.sum(dgr, axis=0, keepdims=True)

    res = pl.BlockSpec((DA_GH, d, tn, HEAD), lambda i: (0, 0, i, 0))
    col = lambda off: pl.BlockSpec((rb, wide), lambda i: (i, off // wide + g))
    vec = pl.BlockSpec((1, HEAD), lambda i: (0, 0))
    nat = pl.BlockSpec((rb, wide), lambda i: (i, 0))
    shape = jax.ShapeDtypeStruct((rows, wide), BF16)
    gshape = jax.ShapeDtypeStruct((1, HEAD), F32)
    return pl.pallas_call(
        body, name=name, grid=(rows // rb,),
        in_specs=[res, res, res, col(C_DQ), col(C_DK), vec, vec],
        out_specs=[nat, nat, nat, vec, vec], out_shape=[shape, shape, shape, gshape, gshape],
        scratch_shapes=[pltpu.VMEM((rb, HEAD), F32)] * 3,
        compiler_params=_params(("arbitrary",)),
    )(dqr, dkr, dvr, proj, proj, q_gain, k_gain)


def _mem_fwd(proj, kv, q_gain, k_gain, *, name):
    rows = proj.shape[0]
    tm = min(1024, rows)
    n_mem = kv.shape[0]

    def body(q_ref, k_ref, v_ref, qg_ref, kg_ref, o_ref):
        qhat, _ = _rms(q_ref[...])
        khat, _ = _rms(k_ref[...])
        s = _dot_nt(qhat * qg_ref[...], khat * kg_ref[...]) * ATT_SCALE
        p = jnp.exp(s - jnp.max(s, axis=-1, keepdims=True))
        p = p / jnp.sum(p, axis=-1, keepdims=True)
        o_ref[...] = _dot(p, v_ref[...]).astype(BF16)

    vec = pl.BlockSpec((1, HEAD), lambda i, h: (0, 0))
    return pl.pallas_call(
        body, name=name, grid=(rows // tm, MEM_HEADS),
        in_specs=[pl.BlockSpec((tm, HEAD), lambda i, h: (i, C_MQ // HEAD + h)),
                  pl.BlockSpec((n_mem, HEAD), lambda i, h: (0, h)),
                  pl.BlockSpec((n_mem, HEAD), lambda i, h: (0, MEM_HEADS + h)), vec, vec],
        out_specs=pl.BlockSpec((tm, HEAD), lambda i, h: (i, h)),
        out_shape=jax.ShapeDtypeStruct((rows, MEM_HEADS * HEAD), BF16),
        compiler_params=_params(("parallel", "parallel")),
    )(proj, kv, kv, q_gain, k_gain)


def _mem_bwd(dout, proj, kv, q_gain, k_gain, dproj, *, name):
    rows = proj.shape[0]
    tm = min(1024, rows)
    steps = rows // tm
    n_mem = kv.shape[0]

    def body(d_ref, q_ref, k_ref, v_ref, qg_ref, kg_ref, dp_in, dq_ref, dk_ref, dv_ref, gq_ref, gk_ref, dkn_ref):
        del dp_in
        h, i = pl.program_id(0), pl.program_id(1)

        @pl.when((h == 0) & (i == 0))
        def _():
            gq_ref[...] = jnp.zeros_like(gq_ref)
            gk_ref[...] = jnp.zeros_like(gk_ref)

        @pl.when(i == 0)
        def _():
            dkn_ref[...] = jnp.zeros_like(dkn_ref)
            dv_ref[...] = jnp.zeros_like(dv_ref)

        qhat, rq = _rms(q_ref[...])
        khat, rk = _rms(k_ref[...])
        qn, kn = qhat * qg_ref[...], khat * kg_ref[...]
        s = _dot_nt(qn, kn) * ATT_SCALE
        p = jnp.exp(s - jnp.max(s, axis=-1, keepdims=True))
        p = p / jnp.sum(p, axis=-1, keepdims=True)
        dout = d_ref[...]
        dp = _dot_nt(dout, v_ref[...])
        ds = p * (dp - jnp.sum(p * dp, axis=-1, keepdims=True))
        dv_ref[...] += _dot_tn(p, dout)
        dkn_ref[...] += _dot_tn(ds, qn) * ATT_SCALE
        dq, dgr = _rms_bwd(_dot(ds, kn) * ATT_SCALE, qhat, rq, qg_ref[...])
        dq_ref[...] = dq.astype(BF16)
        gq_ref[...] += jnp.sum(dgr, axis=0, keepdims=True)

        @pl.when(i == steps - 1)
        def _():
            dk, dgk = _rms_bwd(dkn_ref[...], khat, rk, kg_ref[...])
            dk_ref[...] = dk
            gk_ref[...] += jnp.sum(dgk, axis=0, keepdims=True)

    vec = pl.BlockSpec((1, HEAD), lambda h, i: (0, 0))
    memh = pl.BlockSpec((n_mem, HEAD), lambda h, i: (0, h))
    tok = pl.BlockSpec((tm, HEAD), lambda h, i: (i, h))
    mq = pl.BlockSpec((tm, HEAD), lambda h, i: (i, C_MQ // HEAD + h))
    gshape = jax.ShapeDtypeStruct((1, HEAD), F32)
    return pl.pallas_call(
        body, name=name, grid=(MEM_HEADS, steps),
        in_specs=[tok, mq, memh, pl.BlockSpec((n_mem, HEAD), lambda h, i: (0, MEM_HEADS + h)), vec, vec, ANY],
        out_specs=[mq, memh, memh, vec, vec],
        out_shape=[jax.ShapeDtypeStruct(dproj.shape, dproj.dtype),
                   jax.ShapeDtypeStruct((n_mem, MEM_HEADS * HEAD), F32),
                   jax.ShapeDtypeStruct((n_mem, MEM_HEADS * HEAD), F32), gshape, gshape],
        input_output_aliases={6: 0},
        scratch_shapes=[pltpu.VMEM((n_mem, HEAD), F32)],
        compiler_params=_params(("arbitrary", "arbitrary")),
    )(dout, proj, kv, kv, q_gain, k_gain, dproj)


def _branch_fwd(o_hg, o_da, o_mem, proj, wp_hg, wp_da, wp_mem, *, name):
    rows = o_hg.shape[0]
    tm = min(512, rows)

    def body(a_ref, b_ref, c_ref, ga_ref, gb_ref, gc_ref, wa_ref, wb_ref, wc_ref, o_ref):
        merged = _sigmoid(ga_ref[...]) * _dot(a_ref[...], wa_ref[...])
        merged += _sigmoid(gb_ref[...]) * _dot(b_ref[...], wb_ref[...])
        merged += _sigmoid(gc_ref[...]) * _dot(c_ref[...], wc_ref[...])
        o_ref[...] = merged.astype(BF16)

    row = lambda w: pl.BlockSpec((tm, w), lambda i: (i, 0))
    gate = lambda off: pl.BlockSpec((tm, D_MODEL), lambda i: (i, off // D_MODEL))
    full = lambda a: pl.BlockSpec(a.shape, lambda i: (0, 0))
    return pl.pallas_call(
        body, name=name, grid=(rows // tm,),
        in_specs=[row(o_hg.shape[1]), row(o_da.shape[1]), row(o_mem.shape[1]),
                  gate(C_GHG), gate(C_GDA), gate(C_GMEM), full(wp_hg), full(wp_da), full(wp_mem)],
        out_specs=row(D_MODEL), out_shape=jax.ShapeDtypeStruct((rows, D_MODEL), BF16),
        compiler_params=_params(("parallel",)),
    )(o_hg, o_da, o_mem, proj, proj, proj, wp_hg, wp_da, wp_mem)


def _branch_bwd(dm, o_hg, o_da, o_mem, proj, wp_hg, wp_da, wp_mem, *, name):
    rows = o_hg.shape[0]
    tm = min(256, rows)

    def body(dm_ref, a_ref, b_ref, c_ref, ga_ref, gb_ref, gc_ref, wa_ref, wb_ref, wc_ref, dp_ref, *outs):
        dmv = dm_ref[...]
        for j, (o_ref, g_ref, w_ref) in enumerate(((a_ref, ga_ref, wa_ref), (b_ref, gb_ref, wb_ref),
                                                   (c_ref, gc_ref, wc_ref))):
            z = _dot(o_ref[...], w_ref[...])
            gs = _sigmoid(g_ref[...])
            dz = (dmv * gs).astype(BF16)
            dp_ref[:, j * D_MODEL:(j + 1) * D_MODEL] = (dmv * z * gs * (1.0 - gs)).astype(BF16)
            outs[2 * j][...] = dz
            outs[2 * j + 1][...] = _dot_nt(dz, w_ref[...])

    row = lambda w: pl.BlockSpec((tm, w), lambda i: (i, 0))
    gate = lambda off: pl.BlockSpec((tm, D_MODEL), lambda i: (i, off // D_MODEL))
    full = lambda a: pl.BlockSpec(a.shape, lambda i: (0, 0))
    out_specs = [pl.BlockSpec((pl.Element(tm), pl.Element(3 * D_MODEL)), lambda i: (i * tm, C_GHG))]
    out_shape = [jax.ShapeDtypeStruct((rows, IN_COLS), BF16)]
    for o in (o_hg, o_da, o_mem):
        out_specs += [row(D_MODEL), row(o.shape[1])]
        out_shape += [jax.ShapeDtypeStruct((rows, D_MODEL), BF16), jax.ShapeDtypeStruct((rows, o.shape[1]), F32)]
    return pl.pallas_call(
        body, name=name, grid=(rows // tm,),
        in_specs=[row(D_MODEL), row(o_hg.shape[1]), row(o_da.shape[1]), row(o_mem.shape[1]),
                  gate(C_GHG), gate(C_GDA), gate(C_GMEM), full(wp_hg), full(wp_da), full(wp_mem)],
        out_specs=out_specs, out_shape=out_shape,
        compiler_params=_params(("parallel",)),
    )(dm, o_hg, o_da, o_mem, proj, proj, proj, wp_hg, wp_da, wp_mem)


def _ffn_in(h2, w_ab, *, name):
    rows, dff = h2.shape[0], w_ab.shape[1] // 2
    tm, tn = min(2048, rows), 256

    def body(h_ref, wa_ref, wb_ref, a_ref, b_ref, u_ref):
        a = _dot(h_ref[...], wa_ref[...])
        b = _dot(h_ref[...], wb_ref[...])
        a_ref[...] = a.astype(BF16)
        b_ref[...] = b.astype(BF16)
        u_ref[...] = (a * _sigmoid(a) * b).astype(BF16)

    out = pl.BlockSpec((tm, tn), lambda i, j: (i, j))
    return pl.pallas_call(
        body, name=name, grid=(rows // tm, dff // tn),
        in_specs=[pl.BlockSpec((tm, D_MODEL), lambda i, j: (i, 0)),
                  pl.BlockSpec((D_MODEL, tn), lambda i, j: (0, j)),
                  pl.BlockSpec((D_MODEL, tn), lambda i, j: (0, dff // tn + j))],
        out_specs=[out, out, out],
        out_shape=[jax.ShapeDtypeStruct((rows, dff), BF16)] * 3,
        compiler_params=_params(("parallel", "parallel")),
    )(h2, w_ab, w_ab)


def _ffn_act_bwd(dy, w_out, a, b, *, name):
    rows, dff = a.shape
    tm, tn = min(2048, rows), 256

    def body(dy_ref, w_ref, a_ref, b_ref, da_ref, db_ref):
        du = _dot_nt(dy_ref[...], w_ref[...])
        av, bv = a_ref[...].astype(F32), b_ref[...].astype(F32)
        sa = _sigmoid(av)
        da_ref[...] = (du * bv * sa * (1.0 + av * (1.0 - sa))).astype(BF16)
        db_ref[...] = (du * av * sa).astype(BF16)

    tile = pl.BlockSpec((tm, tn), lambda i, j: (i, j))
    return pl.pallas_call(
        body, name=name, grid=(rows // tm, dff // tn),
        in_specs=[pl.BlockSpec((tm, D_MODEL), lambda i, j: (i, 0)),
                  pl.BlockSpec((tn, D_MODEL), lambda i, j: (j, 0)), tile, tile],
        out_specs=[tile, tile],
        out_shape=[jax.ShapeDtypeStruct((rows, dff), BF16), jax.ShapeDtypeStruct((rows, dff), BF16)],
        compiler_params=_params(("parallel", "parallel")),
    )(dy, w_out, a, b)


def _lower_bound(lb_fw, lb_bw, *, name):
    def body(a_ref, b_ref, oa_ref, ob_ref):
        for src, dst in ((a_ref, oa_ref), (b_ref, ob_ref)):
            dst[...] = _sigmoid(src[0:1, :] - src[1:2, :])

    shape = jax.ShapeDtypeStruct((1, lb_fw.shape[1]), F32)
    return pl.pallas_call(body, name=name, out_shape=[shape, shape])(lb_fw, lb_bw)


def _local_step(x, mem, tgt, p, ex):
    rows = x.shape[0]
    lb_fw, lb_bw = _lower_bound(p["lb_logits_fw"], p["lb_logits_bw"], name="lower_bound")

    h = _rms_fwd(x, p["norm_mix_gain"], name="norm_mix")
    w = {"w_in": ex.w_in}
    proj, carried = _matmul([(h, w["w_in"])], "nn", BF16, tm=1024, tn=1024, carry=ex.late_carry(), name="proj_in")
    w.update(ex.late_weights(carried))
    o_fw, st_fw = _gla_fwd(proj, lb_fw, f_off=C_FF, rev=False, name="gla_fwd_fw")
    o_bw, st_bw = _gla_fwd(proj, lb_bw, f_off=C_FB, rev=True, name="gla_fwd_bw")
    o_hg = _hg_out_fwd(o_fw, o_bw, proj, p["hg_norm_gain"], name="hg_out")

    qkv_r, outs, lses = [], [], []
    for g in range(len(DA_GROUPS)):
        qr, kr, vr = _da_prep(proj, p["da_q_gain"], p["da_k_gain"], g, name=f"da_prep{g}")
        og, lg = _band_fwd(qr, kr, vr, g, name=f"band_fwd{g}")
        qkv_r.append((qr, kr, vr))
        outs.append(og)
        lses.append(lg)
    o_da, lse_da = _da_merge(outs, lses, rows, name="da_merge")

    mem_n = _rms_fwd(mem, p["norm_mem_gain"], name="norm_mem")
    kv = _matmul([(mem_n, w["w_mem_kv"])], "nn", F32, tm=256, tn=512, name="mem_kv")
    o_mem = _mem_fwd(proj, kv, p["mem_q_gain"], p["mem_k_gain"], name="mem_attn")

    merged = _branch_fwd(o_hg, o_da, o_mem, proj, w["w_proj_hg"], w["w_proj_da"], w["w_proj_mem"],
                         name="branch_merge")
    x1, h2 = _out_norm(merged, w["w_out"], x, p["norm_ffn_gain"], name="mix_out_norm")
    a, b, u = _ffn_in(h2, w["w_ffn_in"], name="ffn_in")
    dy, dy_b, loss = _out_loss(u, w["w_ffn_out"], x1, tgt, name="ffn_out_loss")

    gw, gwb, gs = {}, {}, {}
    gw["w_ffn_out"], gwb["w_ffn_out"] = _matmul([(u, dy_b)], "tn", F32, tm=256, tn=1024, also=BF16, name="g_ffn_out")
    da, db = _ffn_act_bwd(dy_b, w["w_ffn_out"], a, b, name="ffn_act_bwd")
    gw["w_ffn_a"], gwb["w_ffn_a"] = _matmul([(h2, da)], "tn", F32, tm=1024, tn=256, also=BF16, name="g_ffn_a")
    gw["w_ffn_b"], gwb["w_ffn_b"] = _matmul([(h2, db)], "tn", F32, tm=1024, tn=256, also=BF16, name="g_ffn_b")
    dh2 = _matmul([(da, w["w_ffn_in"], 0), (db, w["w_ffn_in"], 1)], "nt", F32, tm=256, tn=1024, name="d_h2")
    dx1, dx1_b, gs["norm_ffn_gain"] = _rms_bwd_rows(dh2, x1, p["norm_ffn_gain"], dy, (F32, BF16), name="norm_ffn_bwd")
    gw["w_out"], gwb["w_out"] = _matmul([(merged, dx1_b)], "tn", F32, tm=512, tn=512, also=BF16, name="g_out")
    dmerged = _matmul([(dx1_b, w["w_out"])], "nt", F32, tm=512, tn=512, name="d_merged")
    dproj, dz_hg, do_hg, dz_da, do_da, dz_mem, do_mem = _branch_bwd(
        dmerged, o_hg, o_da, o_mem, proj, w["w_proj_hg"], w["w_proj_da"], w["w_proj_mem"], name="branch_bwd")
    gw["w_proj_hg"], gwb["w_proj_hg"] = _matmul([(o_hg, dz_hg)], "tn", F32, tm=512, tn=512, also=BF16, name="g_proj_hg")
    gw["w_proj_da"], gwb["w_proj_da"] = _matmul([(o_da, dz_da)], "tn", F32, tm=512, tn=512, also=BF16, name="g_proj_da")
    gw["w_proj_mem"], gwb["w_proj_mem"] = _matmul([(o_mem, dz_mem)], "tn", F32, tm=512, tn=512, also=BF16, name="g_proj_mem")

    dproj, dk_mem, dv_mem, gs["mem_q_gain"], gs["mem_k_gain"] = _mem_bwd(
        do_mem, proj, kv, p["mem_q_gain"], p["mem_k_gain"], dproj, name="mem_attn_bwd")
    dkv = jnp.concatenate([dk_mem, dv_mem], axis=1).astype(BF16)
    gw["w_mem_kv"], gwb["w_mem_kv"] = _matmul([(mem_n, dkv)], "tn", F32, tm=512, tn=512, also=BF16, name="g_mem_kv")
    dmem_n = _matmul([(dkv, w["w_mem_kv"])], "nt", F32, tm=256, tn=512, name="d_mem_n")
    (gs["norm_mem_gain"],) = _rms_bwd_rows(dmem_n, mem, p["norm_mem_gain"], None, (), name="norm_mem_bwd")

    prep = _da_bwd_prep(do_da, o_da, lse_da, name="da_bwd_prep")
    d_da, gq_parts, gk_parts = [], [], []
    for g in range(len(DA_GROUPS)):
        qr, kr, vr = qkv_r[g]
        dor, lser, deltar = prep[3 * g:3 * g + 3]
        dqr, dkr, dvr = _band_bwd(qr, kr, vr, dor, lser, deltar, g, name=f"band_bwd{g}")
        dq, dk, dv, gq, gk = _da_prep_bwd(dqr, dkr, dvr, proj, p["da_q_gain"], p["da_k_gain"], g,
                                          name=f"da_prep_bwd{g}")
        d_da.append((dq, dk, dv))
        gq_parts.append(gq)
        gk_parts.append(gk)
    for j, (off, tag) in enumerate(((C_DQ, "q"), (C_DK, "k"), (C_DV, "v"))):
        dproj = _fill_columns(dproj, [t[j] for t in d_da], off, name=f"dproj_fill_da_{tag}")

    dproj, do_gla, gs["hg_norm_gain"] = _hg_out_bwd(do_hg, o_fw, o_bw, proj, p["hg_norm_gain"], dproj,
                                                    name="hg_out_bwd")
    carry_bulk, carry_rest = ex.early_carries(gw, gwb)
    (dq_f, dproj, dv_f, dlb_fw), bulk = _gla_bwd(proj, lb_fw, do_gla, st_fw, None, dproj, f_off=C_FF, rev=False,
                                                 carry=carry_bulk, name="gla_bwd_fw")
    (dproj, dfl_bw, dv_hg, dlb_bw), rest = _gla_bwd(proj, lb_bw, do_gla, st_bw, (dq_f, dv_f), dproj, f_off=C_FB,
                                                    rev=True, carry=carry_rest, name="gla_bwd_bw")
    ex.early_done(bulk, rest)
    dproj = _fill_columns(dproj, [dfl_bw, dv_hg], C_FB, name="dproj_fill")
    g_in, g_in_b = _matmul([(h, dproj)], "tn", F32, tm=1024, tn=512, also=BF16, name="g_in")
    dh, carried = _matmul([(dproj, w["w_in"])], "nt", F32, tm=1024, tn=1024, tk=IN_COLS // 8,
                          carry=ex.w_in_carry(g_in, g_in_b), name="d_h")
    ex.w_in_done(carried)
    grad_x, gs["norm_mix_gain"] = _rms_bwd_rows(dh, x, p["norm_mix_gain"], dx1, (F32,), name="norm_mix_bwd")

    small = _small_pack(gs, gq_parts, gk_parts, dlb_fw, dlb_bw, lb_fw, lb_bw, loss, name="small_pack")
    return grad_x, small


def _small_pack(gs, gq_parts, gk_parts, dlb_fw, dlb_bw, lb_fw, lb_bw, loss, *, name):
    def body(g_mix, g_mem, g_ffn, dfw, dbw, lfw, lbw, g_hg, q0, q1, q2, k0, k1, k2, g_mq, g_mk, loss_ref, o_ref):
        o_ref[0:1, :] = g_mix[...]
        o_ref[1:2, :] = g_mem[...]
        o_ref[2:3, :] = g_ffn[...]
        for base, d_ref, l_ref in ((3, dfw, lfw), (5, dbw, lbw)):
            lbv = l_ref[...]
            t = d_ref[...] * lbv * (1.0 - lbv)
            o_ref[base:base + 1, :] = t
            o_ref[base + 1:base + 2, :] = -t
        o_ref[7:8, :] = jnp.zeros((1, D_MODEL), F32)
        o_ref[7:8, 0:HEAD] = g_hg[...]
        o_ref[7:8, HEAD:2 * HEAD] = q0[...] + q1[...] + q2[...]
        o_ref[7:8, 2 * HEAD:3 * HEAD] = k0[...] + k1[...] + k2[...]
        o_ref[7:8, 3 * HEAD:4 * HEAD] = g_mq[...]
        o_ref[7:8, 4 * HEAD:5 * HEAD] = g_mk[...]
        o_ref[7:8, LOSS_COL:LOSS_COL + HEAD] = loss_ref[...]

    return pl.pallas_call(body, name=name, out_shape=jax.ShapeDtypeStruct((8, D_MODEL), F32))(
        gs["norm_mix_gain"], gs["norm_mem_gain"], gs["norm_ffn_gain"], dlb_fw, dlb_bw, lb_fw, lb_bw,
        gs["hg_norm_gain"], *gq_parts, *gk_parts, gs["mem_q_gain"], gs["mem_k_gain"], loss)


def _row_tile(rows, cols, n_arrays):
    budget = (16 * 1024 * 1024) // (2 * 4 * cols * n_arrays)
    tr = rows
    while tr > budget and tr % 2 == 0 and (tr // 2) % 16 == 0:
        tr //= 2
    return tr


def _cast_into_full(a, chip, rows, cols, axis, *, name):
    sr, sc = a.shape
    tr = _row_tile(sr, sc, 2)

    def body(chip_ref, a_ref, o_ref):
        del chip_ref
        o_ref[...] = a_ref[...].astype(BF16)

    if axis == 1:
        out_map = lambda i, chip_ref: (i, chip_ref[0])
    else:
        out_map = lambda i, chip_ref: (chip_ref[0] * (sr // tr) + i, 0)
    return pl.pallas_call(
        body, name=name,
        grid_spec=pltpu.PrefetchScalarGridSpec(
            num_scalar_prefetch=1, grid=(sr // tr,),
            in_specs=[pl.BlockSpec((tr, sc), lambda i, chip_ref: (i, 0))],
            out_specs=pl.BlockSpec((tr, sc), out_map)),
        out_shape=jax.ShapeDtypeStruct((rows, cols), BF16),
        compiler_params=_params(("parallel",)))(chip, a)


def _add_halves(items, core, *, name):
    rows = items[0][3].shape[0]
    widths = [ra.shape[1] for _, _, _, ra in items]
    tr = _row_tile(rows, sum(widths), 3)

    def body(core_ref, *refs):
        del core_ref
        o_ref = refs[-1]
        off = 0
        for j, wd in enumerate(widths):
            mine, theirs = refs[2 * j:2 * j + 2]
            o_ref[:, off:off + wd] = (mine[...] + theirs[...]).astype(BF16)
            off += wd

    in_specs, ins = [], []
    for (g, haxis, hsize, ra), wd in zip(items, widths):
        if haxis == 0:
            in_specs.append(pl.BlockSpec((tr, wd), lambda i, core_ref, o=hsize // tr: (core_ref[0] * o + i, 0)))
        else:
            in_specs.append(pl.BlockSpec((tr, wd), lambda i, core_ref: (i, core_ref[0])))
        in_specs.append(pl.BlockSpec((tr, wd), lambda i, core_ref: (i, 0)))
        ins += [g, ra]
    return pl.pallas_call(
        body, name=name,
        grid_spec=pltpu.PrefetchScalarGridSpec(
            num_scalar_prefetch=1, grid=(rows // tr,), in_specs=in_specs,
            out_specs=pl.BlockSpec((tr, sum(widths)), lambda i, core_ref: (i, 0))),
        out_shape=jax.ShapeDtypeStruct((rows, sum(widths)), BF16),
        compiler_params=_params(("parallel",)))(core, *ins)


def _add_slots(rb, *, name):
    _, rows, cols = rb.shape
    tr = _row_tile(rows, cols, 5)

    def body(r0, r1, r2, r3, o_ref):
        o_ref[...] = ((r0[...].astype(F32) + r1[...].astype(F32)) + r2[...].astype(F32)) + r3[...].astype(F32)

    slot = lambda s: pl.BlockSpec((None, tr, cols), lambda i: (s, i, 0))
    return pl.pallas_call(body, name=name, grid=(rows // tr,), in_specs=[slot(s) for s in range(4)],
                          out_specs=pl.BlockSpec((tr, cols), lambda i: (i, 0)),
                          out_shape=jax.ShapeDtypeStruct((rows, cols), F32),
                          compiler_params=_params(("parallel",)))(rb, rb, rb, rb)


def _adamw(w, g, m, v, *, name):
    rows, cols = w.shape
    tr = _row_tile(rows, cols, 8) if rows % 16 == 0 else rows
    c1 = 1.0 - ADAM_B1 ** ADAM_STEP
    c2 = 1.0 - ADAM_B2 ** ADAM_STEP

    def body(w_ref, g_ref, m_ref, v_ref, d_ref, mo_ref, vo_ref, go_ref):
        gv = g_ref[...]
        go_ref[...] = gv
        mn = ADAM_B1 * m_ref[...] + (1.0 - ADAM_B1) * gv
        vn = ADAM_B2 * v_ref[...] + (1.0 - ADAM_B2) * (gv * gv)
        mo_ref[...] = mn
        vo_ref[...] = vn
        d_ref[...] = -ADAM_LR * ((mn / c1) / (jnp.sqrt(vn / c2) + ADAM_EPS) + ADAM_WD * w_ref[...])

    spec = pl.BlockSpec((tr, cols), lambda i: (i, 0))
    shape = jax.ShapeDtypeStruct((rows, cols), F32)
    return pl.pallas_call(body, name=name, grid=(rows // tr,), in_specs=[spec] * 4, out_specs=[spec] * 4,
                        out_shape=[shape] * 4, compiler_params=_params(("parallel",)))(w, g, m, v)


W_SPECS = (
    ("w_in", 1024, IN_COLS, 1, IN_COLS // 4),
    ("w_mem_kv", 1024, 1024, 0, 256),
    ("w_proj_hg", 1024, 1024, 0, 256),
    ("w_proj_da", 512, 1024, 1, 256),
    ("w_proj_mem", 512, 1024, 1, 256),
    ("w_out", 1024, 1024, 0, 256),
    ("w_ffn_in", 1024, 2 * D_FF, 1, 2 * D_FF // 4),
    ("w_ffn_out", D_FF, 1024, 0, D_FF // 4),
)
CHIP_FLIPS = ((1, 0), (0, 1), (1, 1))
ANY = pl.BlockSpec(memory_space=pl.ANY)
DMA_CHUNK_BYTES = 1 << 20
STAGE_BYTES = 2 << 20


def _place():
    x, y, c = lax.axis_index("x"), lax.axis_index("y"), lax.axis_index("c")
    return x, y, c, 2 * x + y


def _flip(v, f):
    return 1 - v if f else v


def _slab(ref, axis, idx, size):
    start = pl.multiple_of(idx * size, size)
    return ref.at[pl.ds(start, size), :] if axis == 0 else ref.at[:, pl.ds(start, size)]


def _chunked(make, src, dst, want="both"):
    rows, cols = src.shape
    row_bytes = cols * jnp.dtype(src.dtype).itemsize
    k = 1
    while rows % (2 * k) == 0 and (rows // (2 * k)) % 16 == 0 and (rows // k) * row_bytes > DMA_CHUNK_BYTES:
        k *= 2
    cr = rows // k
    parts = []
    if want != "wait":
        parts = [make(src.at[pl.ds(j * cr, cr), :], dst.at[pl.ds(j * cr, cr), :]) for j in range(k)]
    return parts, (make(src, dst) if want != "start" else None)


def _half_spec(rows, cols, axis):
    return (0, rows // 2) if axis == 1 else (1, cols // 2)


def _staged(src, remote_dst, local_dst, sibling, load_sems, send_sems, store_sems, recv_sem):
    rows, cols = src.shape
    row_bytes = cols * jnp.dtype(src.dtype).itemsize
    k = 1
    while (rows // k) * row_bytes > STAGE_BYTES and rows % (2 * k) == 0 and (rows // (2 * k)) % 16 == 0:
        k *= 2
    cr = rows // k
    piece = lambda ref, j: ref.at[pl.ds(j * cr, cr), :]

    def run(buf):
        loads = [pltpu.make_async_copy(piece(src, j), buf.at[j % 2], load_sems.at[j % 2]) for j in range(k)]
        outs = [[pltpu.make_async_remote_copy(src_ref=buf.at[j % 2], dst_ref=piece(remote_dst, j),
                                              send_sem=send_sems.at[j % 2], recv_sem=recv_sem,
                                              device_id=sibling, device_id_type=MESH)] for j in range(k)]
        if local_dst is not None:
            for j in range(k):
                outs[j].append(pltpu.make_async_copy(buf.at[j % 2], piece(local_dst, j), store_sems.at[j % 2]))

        def drained(j):
            outs[j][0].wait_send()
            for cp in outs[j][1:]:
                cp.wait()

        loads[0].start()
        for j in range(k):
            loads[j].wait()
            for cp in outs[j]:
                cp.start()
            if j + 1 < k:
                if j >= 1:
                    drained(j - 1)
                loads[j + 1].start()
        for j in range(max(0, k - 2), k):
            drained(j)

    pl.run_scoped(run, pltpu.VMEM((2, cr, cols), src.dtype))


def _landed(ref, recv_sem, send_sem):
    pltpu.make_async_remote_copy(src_ref=ref, dst_ref=ref, send_sem=send_sem, recv_sem=recv_sem,
                                 device_id=(lax.axis_index("x"), lax.axis_index("y"), lax.axis_index("c")),
                                 device_id_type=MESH).wait_recv()


def _stage_pieces(shape, dtype):
    rows, cols = shape
    row_bytes = cols * jnp.dtype(dtype).itemsize
    k = 1
    while (rows // k) * row_bytes > STAGE_BYTES and rows % (2 * k) == 0 and (rows // (2 * k)) % 16 == 0:
        k *= 2
    return k


def _push_all(transfers, sibling, bufs, load_sems, send_sems, store_sems, recv_sems):
    pieces, j = [], 0
    for i, (src, remote_dst, local_dst) in enumerate(transfers):
        k = _stage_pieces(src.shape, src.dtype)
        cr = src.shape[0] // k
        for q in range(k):
            rows = pl.ds(q * cr, cr)
            load = pltpu.make_async_copy(src.at[rows, :], bufs[i].at[rows, :], load_sems.at[j])
            outs = [pltpu.make_async_remote_copy(src_ref=bufs[i].at[rows, :], dst_ref=remote_dst.at[rows, :],
                                                 send_sem=send_sems.at[j], recv_sem=recv_sems.at[i],
                                                 device_id=sibling, device_id_type=MESH)]
            if local_dst is not None:
                outs.append(pltpu.make_async_copy(bufs[i].at[rows, :], local_dst.at[rows, :], store_sems.at[j]))
            pieces.append((load, outs))
            j += 1
    for load, _ in pieces:
        load.start()
    for load, outs in pieces:
        load.wait()
        for cp in outs:
            cp.start()
    for _, outs in pieces:
        outs[0].wait_send()
        for cp in outs[1:]:
            cp.wait()


def _half_slab(ref, spec, chip, half):
    _, rows, cols, axis, size = spec
    haxis, hsize = _half_spec(rows, cols, axis)
    return _slab(_slab(ref, axis, chip, size), haxis, half, hsize)


def _gather_sends(outs, specs, ici_send, ici_recv, want="both"):
    x, y, c, p = _place()
    sent = []
    for wi, spec in enumerate(specs):
        mine = _half_slab(outs[wi], spec, p, c)
        for k, (fx, fy) in enumerate(CHIP_FLIPS):
            sent.append(_chunked(lambda s, d, j=3 * wi + k, fx=fx, fy=fy: pltpu.make_async_remote_copy(
                src_ref=s, dst_ref=d, send_sem=ici_send.at[j], recv_sem=ici_recv.at[j],
                device_id=(_flip(x, fx), _flip(y, fy), c), device_id_type=MESH), mine, mine, want))
    return sent


def _gather_weights(fulls, specs, *, name):
    n = len(specs)

    def body(*refs):
        outs = refs[n:2 * n]
        ici_send, ici_recv, load_sems, d2d_send, d2d_recv = refs[2 * n:]
        x, y, c, _ = _place()
        sent = _gather_sends(outs, specs, ici_send, ici_recv)
        for parts, _ in sent:
            for cp in parts:
                cp.start()
        for k, (fx, fy) in enumerate(CHIP_FLIPS):
            q = 2 * _flip(x, fx) + _flip(y, fy)
            for wi in range(n):
                sent[3 * wi + k][1].wait_recv()
                got = _half_slab(outs[wi], specs[wi], q, c)
                _staged(got, got, None, (x, y, 1 - c), load_sems, d2d_send, None, d2d_recv.at[3 * wi + k])
        for _, whole in sent:
            whole.wait_send()
        for k, (fx, fy) in enumerate(CHIP_FLIPS):
            q = 2 * _flip(x, fx) + _flip(y, fy)
            for wi in range(n):
                _landed(_half_slab(outs[wi], specs[wi], q, 1 - c), d2d_recv.at[3 * wi + k], d2d_send.at[0])

    return pl.pallas_call(
        body, name=name, in_specs=[ANY] * n, out_specs=[ANY] * n,
        out_shape=[jax.ShapeDtypeStruct(f.shape, f.dtype) for f in fulls],
        input_output_aliases={i: i for i in range(n)},
        scratch_shapes=[pltpu.SemaphoreType.DMA((3 * n,)), pltpu.SemaphoreType.DMA((3 * n,)),
                        pltpu.SemaphoreType.DMA((2,)), pltpu.SemaphoreType.DMA((2,)),
                        pltpu.SemaphoreType.DMA((3 * n,))],
    )(*fulls)


def _gather_chips_carry(fulls, specs):
    n = len(specs)

    def issue(ins, outs, sems, want):
        del ins
        return [(parts, whole, True) for parts, whole in _gather_sends(outs, specs, sems[0], sems[1], want)]

    return _Carry(list(fulls), [jax.ShapeDtypeStruct(f.shape, f.dtype) for f in fulls], {i: i for i in range(n)},
                  [pltpu.SemaphoreType.DMA((3 * n,)), pltpu.SemaphoreType.DMA((3 * n,))], issue)


def _gather_pass_on(fulls, specs, *, name):
    n = len(specs)

    def body(*refs):
        outs = refs[n:2 * n]
        load_sems, d2d_send, d2d_recv = refs[2 * n:2 * n + 3]
        bufs = refs[2 * n + 3:]
        x, y, c, _ = _place()
        loads, sends = [], []
        for k, (fx, fy) in enumerate(CHIP_FLIPS):
            q = 2 * _flip(x, fx) + _flip(y, fy)
            for wi in range(n):
                j = 3 * wi + k
                got = _half_slab(outs[wi], specs[wi], q, c)
                loads.append(pltpu.make_async_copy(got, bufs[j], load_sems.at[j]))
                sends.append(pltpu.make_async_remote_copy(
                    src_ref=bufs[j], dst_ref=got, send_sem=d2d_send.at[j], recv_sem=d2d_recv.at[j],
                    device_id=(x, y, 1 - c), device_id_type=MESH))
        for cp in loads:
            cp.start()
        for load, send in zip(loads, sends):
            load.wait()
            send.start()
        for cp in sends:
            cp.wait_send()
        for k, (fx, fy) in enumerate(CHIP_FLIPS):
            q = 2 * _flip(x, fx) + _flip(y, fy)
            for wi in range(n):
                _landed(_half_slab(outs[wi], specs[wi], q, 1 - c), d2d_recv.at[3 * wi + k], d2d_send.at[0])

    shapes = []
    for _, rows, cols, axis, size in specs:
        shapes += [(rows // 2, size) if axis == 1 else (size, cols // 2)] * 3
    assert sum(math.prod(s) for s in shapes) * 2 <= VMEM_LIMIT_V7X // 2
    stages = [pltpu.VMEM(s, BF16) for s in shapes]
    return pl.pallas_call(
        body, name=name, in_specs=[ANY] * n, out_specs=[ANY] * n,
        out_shape=[jax.ShapeDtypeStruct(f.shape, f.dtype) for f in fulls],
        input_output_aliases={i: i for i in range(n)},
        scratch_shapes=[pltpu.SemaphoreType.DMA((3 * n,))] * 3 + stages,
        compiler_params=pltpu.CompilerParams(vmem_limit_bytes=VMEM_LIMIT_V7X),
    )(*fulls)


def _sibling_exchange(grads, *, name):
    n = len(grads)

    def body(*refs):
        ins, outs = refs[:n], refs[n:2 * n]
        load_sems, send_sems, recv_sems = refs[2 * n:2 * n + 3]
        bufs = refs[2 * n + 3:]
        x, y, c, _ = _place()
        _push_all([(_slab(ins[i], haxis, 1 - c, hsize), outs[i], None) for i, (_, haxis, hsize) in enumerate(grads)],
                  (x, y, 1 - c), bufs, load_sems, send_sems, None, recv_sems)
        for i in range(n):
            _landed(outs[i], recv_sems.at[i], send_sems.at[0])

    shapes = [jax.ShapeDtypeStruct((hsize, g.shape[1]) if haxis == 0 else (g.shape[0], hsize), g.dtype)
              for g, haxis, hsize in grads]
    n_pieces = sum(_stage_pieces(sh.shape, sh.dtype) for sh in shapes)
    return pl.pallas_call(
        body, name=name, in_specs=[ANY] * n, out_specs=[ANY] * n, out_shape=shapes,
        scratch_shapes=[pltpu.SemaphoreType.DMA((n_pieces,)), pltpu.SemaphoreType.DMA((n_pieces,)),
                        pltpu.SemaphoreType.DMA((n,))] + [pltpu.VMEM(sh.shape, sh.dtype) for sh in shapes],
        compiler_params=pltpu.CompilerParams(vmem_limit_bytes=VMEM_LIMIT_V7X),
    )(*[g for g, _, _ in grads])


def _chip_exchange_carry(parts):
    n = len(parts)

    def issue(ins, outs, sems, want):
        send_sems, recv_sems, local_sems = sems
        x, y, c, p = _place()
        copies = []
        for i, (_, axis, size) in enumerate(parts):
            copies.append(_chunked(lambda s, d, i=i: pltpu.make_async_copy(s, d, local_sems.at[i]),
                                   _slab(ins[i], axis, p, size), outs[i].at[p], want) + (False,))
            for k, (fx, fy) in enumerate(CHIP_FLIPS):
                px, py = _flip(x, fx), _flip(y, fy)
                copies.append(_chunked(lambda s, d, j=3 * i + k, px=px, py=py: pltpu.make_async_remote_copy(
                    src_ref=s, dst_ref=d, send_sem=send_sems.at[j], recv_sem=recv_sems.at[j],
                    device_id=(px, py, c), device_id_type=MESH),
                    _slab(ins[i], axis, 2 * px + py, size), outs[i].at[p], want) + (True,))
        return copies

    shapes = []
    for a, axis, size in parts:
        shapes.append(jax.ShapeDtypeStruct((4, size, a.shape[1]) if axis == 0 else (4, a.shape[0], size), a.dtype))
    return _Carry([a for a, _, _ in parts], shapes, {},
                  [pltpu.SemaphoreType.DMA((3 * n,)), pltpu.SemaphoreType.DMA((3 * n,)),
                   pltpu.SemaphoreType.DMA((n,))], issue)


def _sibling_share(sums):
    n = len(sums)

    def body(*refs):
        ins, outs = refs[:n], refs[n:2 * n]
        load_sems, send_sems, store_sems, recv_sems = refs[2 * n:2 * n + 4]
        bufs = refs[2 * n + 4:]
        x, y, c, _ = _place()
        places = [_slab(outs[i], haxis, c, s.shape[haxis]) for i, (s, haxis) in enumerate(sums)]
        _push_all([(ins[i], places[i], places[i]) for i in range(n)], (x, y, 1 - c), bufs,
                  load_sems, send_sems, store_sems, recv_sems)
        for i, (s, haxis) in enumerate(sums):
            _landed(_slab(outs[i], haxis, 1 - c, s.shape[haxis]), recv_sems.at[i], send_sems.at[0])

    shapes = []
    for s, haxis in sums:
        r, cc = s.shape
        shapes.append(jax.ShapeDtypeStruct((2 * r, cc) if haxis == 0 else (r, 2 * cc), F32))
    n_pieces = sum(_stage_pieces(s.shape, s.dtype) for s, _ in sums)
    return pl.pallas_call(
        body, name="grad_sibling_share", in_specs=[ANY] * n, out_specs=[ANY] * n, out_shape=shapes,
        scratch_shapes=[pltpu.SemaphoreType.DMA((n_pieces,))] * 3 + [pltpu.SemaphoreType.DMA((n,))]
                       + [pltpu.VMEM(s.shape, s.dtype) for s, _ in sums],
        compiler_params=pltpu.CompilerParams(vmem_limit_bytes=VMEM_LIMIT_V7X),
    )(*[s for s, _ in sums])


class _Exchanges:
    def __init__(self, fulls):
        self.w_in = _gather_weights([fulls["w_in"]], W_SPECS[:1], name="gather_w_in")[0]
        self.late_specs = W_SPECS[1:]
        self.late = [fulls[s[0]] for s in self.late_specs]
        self.slots = {}
        self.core = lax.axis_index("c").astype(jnp.int32).reshape(1)

    def late_carry(self):
        return _gather_chips_carry(self.late, self.late_specs)

    def late_weights(self, carried):
        done = _gather_pass_on(carried, self.late_specs, name="gather_pass_on")
        return dict(zip([s[0] for s in self.late_specs], done))

    def _half_sums(self, gw, gwb, specs, tag):
        grads, sent = [], []
        for name, _, _, axis, _ in specs:
            for part in (("w_ffn_a", "w_ffn_b") if name == "w_ffn_in" else (name,)):
                half = _half_spec(gw[part].shape[0], gw[part].shape[1], axis)
                grads.append((gw[part],) + half)
                sent.append((gwb[part],) + half)
        theirs = _sibling_exchange(sent, name=f"grad_sibling_exchange_{tag}")
        parts, j = [], 0
        for name, _, _, axis, size in specs:
            take = 2 if name == "w_ffn_in" else 1
            items = [grads[i] + (theirs[i],) for i in range(j, j + take)]
            parts.append((_add_halves(items, self.core, name=f"half_sum_{name}"), axis, size))
            j += take
        return parts

    def early_carries(self, gw, gwb):
        parts = self._half_sums(gw, gwb, self.late_specs, "early")
        self.early_names = [s[0] for s in self.late_specs]
        cut = self.early_names.index("w_ffn_in")
        return _chip_exchange_carry(parts[cut:]), _chip_exchange_carry(parts[:cut])

    def early_done(self, carried_bulk, carried_rest):
        cut = self.early_names.index("w_ffn_in")
        self.slots.update(zip(self.early_names[cut:], carried_bulk))
        self.slots.update(zip(self.early_names[:cut], carried_rest))

    def w_in_carry(self, g, gb):
        return _chip_exchange_carry(self._half_sums({"w_in": g}, {"w_in": gb}, W_SPECS[:1], "w_in"))

    def w_in_done(self, carried):
        self.slots["w_in"] = carried[0]

    def reduced(self):
        sums = []
        for name, rows, cols, axis, _ in W_SPECS:
            sums.append((_add_slots(self.slots[name], name=f"chip_sum_{name}"), _half_spec(rows, cols, axis)[0]))
        return dict(zip([s[0] for s in W_SPECS], _sibling_share(sums)))


def _small_allreduce(sv):
    def body(sv_ref, o_ref, slots_ref, send_sems, recv_sems):
        x, y, c, _ = _place()
        me = 4 * x + 2 * y + c
        slots_ref[me] = sv_ref[...]
        copies = []
        for k in range(1, 8):
            fx, fy, fc = (k >> 2) & 1, (k >> 1) & 1, k & 1
            copies.append(pltpu.make_async_remote_copy(
                src_ref=sv_ref, dst_ref=slots_ref.at[me], send_sem=send_sems.at[k - 1],
                recv_sem=recv_sems.at[k - 1], device_id=(_flip(x, fx), _flip(y, fy), _flip(c, fc)),
                device_id_type=MESH))
        for cp in copies:
            cp.start()
        for cp in copies:
            cp.wait_recv()
        for cp in copies:
            cp.wait_send()
        total = slots_ref[0]
        for s in range(1, 8):
            total = total + slots_ref[s]
        o_ref[...] = total

    vm = pl.BlockSpec(memory_space=pltpu.VMEM)
    return pl.pallas_call(
        body, name="small_allreduce", in_specs=[vm], out_specs=vm,
        out_shape=jax.ShapeDtypeStruct(sv.shape, F32),
        scratch_shapes=[pltpu.VMEM((8,) + sv.shape, F32), pltpu.SemaphoreType.DMA((7,)),
                        pltpu.SemaphoreType.DMA((7,))],
    )(sv)


SMALL_ROWS = (("norm_mix_gain", 0), ("norm_mem_gain", 1), ("norm_ffn_gain", 2))
SMALL_LB = (("lb_logits_fw", 3), ("lb_logits_bw", 5))
SMALL_HEAD = ("hg_norm_gain", "da_q_gain", "da_k_gain", "mem_q_gain", "mem_k_gain")
LOSS_COL = HEAD * len(SMALL_HEAD)


def _pack_small(d):
    last = jnp.concatenate([d[n] for n in SMALL_HEAD] + [jnp.zeros((1, D_MODEL - HEAD * len(SMALL_HEAD)), F32)], axis=1)
    return jnp.concatenate([d["norm_mix_gain"], d["norm_mem_gain"], d["norm_ffn_gain"],
                            d["lb_logits_fw"], d["lb_logits_bw"], last], axis=0)


def _unpack_small(a):
    out = {n: a[r:r + 1] for n, r in SMALL_ROWS}
    out.update({n: a[r:r + 2] for n, r in SMALL_LB})
    out.update({n: a[7:8, j * HEAD:(j + 1) * HEAD] for j, n in enumerate(SMALL_HEAD)})
    return out


PARAM_ORDER = ("norm_mix_gain", "norm_mem_gain", "w_in", "lb_logits_fw", "lb_logits_bw", "hg_norm_gain",
               "da_q_gain", "da_k_gain", "w_mem_kv", "mem_q_gain", "mem_k_gain", "w_proj_hg", "w_proj_da",
               "w_proj_mem", "w_out", "norm_ffn_gain", "w_ffn_in", "w_ffn_out")


def kernel(x, mem, norm_mix_gain, norm_mem_gain, w_in, lb_logits_fw, lb_logits_bw, hg_norm_gain, da_q_gain, da_k_gain, w_mem_kv, mem_q_gain, mem_k_gain, w_proj_hg, w_proj_da, w_proj_mem, w_out, norm_ffn_gain, w_ffn_in, w_ffn_out, loss_target, m_norm_mix_gain, m_norm_mem_gain, m_w_in, m_lb_logits_fw, m_lb_logits_bw, m_hg_norm_gain, m_da_q_gain, m_da_k_gain, m_w_mem_kv, m_mem_q_gain, m_mem_k_gain, m_w_proj_hg, m_w_proj_da, m_w_proj_mem, m_w_out, m_norm_ffn_gain, m_w_ffn_in, m_w_ffn_out, v_norm_mix_gain, v_norm_mem_gain, v_w_in, v_lb_logits_fw, v_lb_logits_bw, v_hg_norm_gain, v_da_q_gain, v_da_k_gain, v_w_mem_kv, v_mem_q_gain, v_mem_k_gain, v_w_proj_hg, v_w_proj_da, v_w_proj_mem, v_w_out, v_norm_ffn_gain, v_w_ffn_in, v_w_ffn_out):
    args = dict(locals())
    mats = tuple(s[0] for s in W_SPECS)
    flat = lambda a: a.reshape(a.shape[-2:])
    w = {n: flat(args[n]) for n in mats}
    m = {n: flat(args["m_" + n]) for n in mats}
    v = {n: flat(args["v_" + n]) for n in mats}
    small = {n: args[n] for n in PARAM_ORDER if n not in mats}

    chip = (2 * lax.axis_index("x") + lax.axis_index("y")).astype(jnp.int32).reshape(1)
    ex = _Exchanges({n: _cast_into_full(w[n], chip, rows, cols, axis, name=f"cast_{n}")
                     for n, rows, cols, axis, _ in W_SPECS})
    grad_x, small_grads = _local_step(x[0], mem[0], loss_target[0], small, ex)
    grads = ex.reduced()
    small_sum = _small_allreduce(small_grads)

    delta, new_m, new_v = {}, {}, {}
    for n in mats:
        delta[n], new_m[n], new_v[n], grads[n] = _adamw(w[n], grads[n], m[n], v[n], name=f"adamw_{n}")
    packed = _adamw(_pack_small(small), small_sum,
                    _pack_small({n: args["m_" + n] for n in small}),
                    _pack_small({n: args["v_" + n] for n in small}), name="adamw_small")
    for dst, src in zip((delta, new_m, new_v, grads), packed):
        dst.update(_unpack_small(src))

    def shaped(d, n):
        return d[n].reshape(args[n].shape)

    return (small_sum[7, LOSS_COL], grad_x[None], *[shaped(grads, n) for n in PARAM_ORDER], *[shaped(delta, n) for n in PARAM_ORDER],
            *[shaped(new_m, n) for n in PARAM_ORDER], *[shaped(new_v, n) for n in PARAM_ORDER])
```

```python
import math

import numpy as np
import jax
import jax.numpy as jnp
from jax import lax
from jax.experimental import pallas as pl
from jax.experimental.pallas import tpu as pltpu

F32 = jnp.float32
BF16 = jnp.bfloat16
MESH = pl.DeviceIdType.MESH

D_MODEL = 1024
HEAD = 128
HG_HEADS = 8
DA_GROUPS = ((1, 64), (4, 64), (16, 64))
DA_GH = 4
MEM_HEADS = 4
N_MEM = 256
D_FF = 2816
CHUNK = 64
BAND_QBLOCKS = 16
GLA_HEADS_PER_STEP = 8
RMS_EPS = 1e-6
NEG_INF = -1e30
HG_SCALE = HEAD ** -0.5
ATT_SCALE = HEAD ** -0.5
VMEM_LIMIT_V7X = 48 * 1024 * 1024

C_HQ, C_FF, C_FB, C_HI, C_HG = 0, 1024, 2048, 3072, 4096
C_DQ, C_DK, C_DV, C_MQ = 5120, 6656, 8192, 9728
C_GHG, C_GDA, C_GMEM = 10240, 11264, 12288
IN_COLS = 13312

ADAM_LR, ADAM_B1, ADAM_B2, ADAM_EPS, ADAM_WD, ADAM_STEP = 0.001, 0.9, 0.999, 1e-08, 0.01, 10


def _params(sem, vmem=VMEM_LIMIT_V7X):
    return pltpu.CompilerParams(dimension_semantics=sem, vmem_limit_bytes=vmem)


def _dot(a, b):
    return jnp.dot(a.astype(BF16), b.astype(BF16), preferred_element_type=F32)


def _dot_nt(a, b):
    return lax.dot_general(a.astype(BF16), b.astype(BF16), (((1,), (1,)), ((), ())),
                           preferred_element_type=F32)


def _dot_tn(a, b):
    return lax.dot_general(a.astype(BF16), b.astype(BF16), (((0,), (0,)), ((), ())),
                           preferred_element_type=F32)


def _sigmoid(v):
    return jax.nn.sigmoid(v.astype(F32))


def _ones(rows, cols):
    return (lax.broadcasted_iota(jnp.int32, (rows, cols), 0) >= 0).astype(BF16)


def _lane_sum(v):
    ones = _ones(HEAD, HEAD)
    hi = v.astype(BF16)
    mid = (v - hi.astype(F32)).astype(BF16)
    return jnp.dot(hi, ones, preferred_element_type=F32) + jnp.dot(mid, ones, preferred_element_type=F32)


def _row_mean(v):
    if v.shape[-1] == HEAD:
        return _lane_sum(v) * (1.0 / HEAD)
    return jnp.mean(v, axis=-1, keepdims=True)


def _rms(v):
    v = v.astype(F32)
    r = lax.rsqrt(_row_mean(v * v) + RMS_EPS)
    return v * r, r


def _rms_bwd(dy, xhat, r, gain):
    dxh = dy * gain
    dx = r * (dxh - xhat * _row_mean(dxh * xhat))
    return dx, dy * xhat


class _Carry:
    def __init__(self, arrays, out_shapes, aliases, sems, issue):
        self.arrays, self.out_shapes, self.aliases, self.sems, self.issue = arrays, out_shapes, aliases, sems, issue


NO_CARRY = object()


def _start_copies(copies):
    for parts, _, _ in copies:
        for cp in parts:
            cp.start()


def _wait_copies(copies):
    for _, whole, remote in copies:
        if remote:
            whole.wait_recv()
    for _, whole, remote in copies:
        if remote:
            whole.wait_send()
        else:
            whole.wait()


def _call(body, *, name, grid, in_specs, out_specs, out_shape, scratch_shapes, semantics, ins, carry=None,
          aliases=None):
    aliases = dict(aliases or {})
    if carry is None:
        res = pl.pallas_call(body, name=name, grid=grid, in_specs=in_specs, out_specs=out_specs, out_shape=out_shape,
                             scratch_shapes=scratch_shapes, input_output_aliases=aliases,
                             compiler_params=_params(semantics))(*ins)
        return list(res), []
    n_in, n_out, n_scr = len(ins), len(out_shape), len(scratch_shapes)
    c_in, c_out = len(carry.arrays), len(carry.out_shapes)

    def wrapped(*refs):
        pos = [0]

        def take(count):
            pos[0] += count
            return refs[pos[0] - count:pos[0]]

        own_in, carry_in = take(n_in), take(c_in)
        own_out, carry_out = take(n_out), take(c_out)
        own_scr, carry_sems = take(n_scr), take(len(carry.sems))
        ids = [pl.program_id(a) for a in range(len(grid))]
        first, last = ids[0] == 0, ids[0] == grid[0] - 1
        for a in range(1, len(grid)):
            first, last = first & (ids[a] == 0), last & (ids[a] == grid[a] - 1)

        @pl.when(first)
        def _():
            _start_copies(carry.issue(carry_in, carry_out, carry_sems, "start"))

        body(*own_in, *own_out, *own_scr)

        @pl.when(last)
        def _():
            _wait_copies(carry.issue(carry_in, carry_out, carry_sems, "wait"))

    res = pl.pallas_call(
        wrapped, name=name, grid=grid, in_specs=list(in_specs) + [ANY] * c_in,
        out_specs=list(out_specs) + [ANY] * c_out, out_shape=list(out_shape) + list(carry.out_shapes),
        input_output_aliases={**aliases, **{n_in + i: n_out + j for i, j in carry.aliases.items()}},
        scratch_shapes=list(scratch_shapes) + list(carry.sems),
        compiler_params=_params(("arbitrary",) * len(grid)))(*ins, *carry.arrays)
    return list(res[:n_out]), list(res[n_out:])


def _matmul(pairs, mode, out_dtype, *, tm, tn, tk=None, residual=None, also=None, carry=NO_CARRY, name):
    b_offs = [pr[2] if len(pr) > 2 else 0 for pr in pairs]
    pairs = [pr[:2] for pr in pairs]
    a0, b0 = pairs[0]
    if mode == "nn":
        (m, kk), n = a0.shape, b0.shape[1]
    elif mode == "nt":
        (m, kk), n = a0.shape, b0.shape[0]
    else:
        (kk, m), n = a0.shape, b0.shape[1]
    assert mode == "nt" or not any(b_offs)
    tm, tn = min(tm, m), min(tn, n)
    tk = kk if tk is None else tk
    nk = kk // tk
    assert m % tm == 0 and n % tn == 0 and kk % tk == 0, (name, m, n, kk)
    n_p = len(pairs)
    if mode == "tn":
        a_spec = pl.BlockSpec((tk, tm), lambda i, j, k: (k, i))
    else:
        a_spec = pl.BlockSpec((tm, tk), lambda i, j, k: (i, k))
    if mode == "nt":
        b_specs = [pl.BlockSpec((tn, tk), lambda i, j, k, o=o: (j, o * nk + k)) for o in b_offs]
    else:
        b_specs = [pl.BlockSpec((tk, tn), lambda i, j, k: (k, j))] * n_p
    o_spec = pl.BlockSpec((tm, tn), lambda i, j, k: (i, j))
    dot = {"nn": _dot, "nt": _dot_nt, "tn": _dot_tn}[mode]
    has_res = residual is not None

    def body(*refs):
        a_refs, b_refs = refs[:n_p], refs[n_p:2 * n_p]
        pos = 2 * n_p
        res_ref = refs[pos] if has_res else None
        pos += int(has_res)
        o_ref = refs[pos]
        pos += int(also is not None)
        part = dot(a_refs[0][...], b_refs[0][...])
        for a_r, b_r in zip(a_refs[1:], b_refs[1:]):
            part += dot(a_r[...], b_r[...])

        def finish(total):
            if has_res:
                total = total + res_ref[...]
            o_ref[...] = total.astype(out_dtype)
            if also is not None:
                refs[pos][...] = total.astype(also)

        if nk == 1:
            finish(part)
        else:
            acc_ref = refs[pos + 1]
            k = pl.program_id(2)

            @pl.when(k == 0)
            def _():
                acc_ref[...] = part

            @pl.when(k > 0)
            def _():
                acc_ref[...] += part

            @pl.when(k == nk - 1)
            def _():
                finish(acc_ref[...])

    ins = [a for a, _ in pairs] + [b for _, b in pairs]
    in_specs = [a_spec] * n_p + b_specs
    if has_res:
        ins.append(residual)
        in_specs.append(o_spec)
    dtypes = [out_dtype] + ([also] if also is not None else [])
    outs, carried = _call(
        body, name=name, grid=(m // tm, n // tn, nk), in_specs=in_specs, out_specs=[o_spec] * len(dtypes),
        out_shape=[jax.ShapeDtypeStruct((m, n), dt) for dt in dtypes],
        scratch_shapes=[pltpu.VMEM((tm, tn), F32)] if nk > 1 else [],
        semantics=("parallel", "parallel", "arbitrary"), ins=ins, carry=None if carry is NO_CARRY else carry)
    out = outs[0] if also is None else tuple(outs)
    return out if carry is NO_CARRY else (out, carried)


def _rms_fwd(x, gain, *, name):
    rows, dm = x.shape
    tm = min(1024, rows)

    def body(x_ref, g_ref, h_ref):
        xhat, _ = _rms(x_ref[...])
        h_ref[...] = (xhat * g_ref[...]).astype(BF16)

    return pl.pallas_call(
        body, name=name, grid=(rows // tm,),
        in_specs=[pl.BlockSpec((tm, dm), lambda i: (i, 0)), pl.BlockSpec((1, dm), lambda i: (0, 0))],
        out_specs=pl.BlockSpec((tm, dm), lambda i: (i, 0)),
        out_shape=jax.ShapeDtypeStruct((rows, dm), BF16),
        compiler_params=_params(("parallel",)),
    )(x, gain)


def _rms_bwd_rows(dh, x, gain, dres, dx_dtypes, *, name):
    rows, dm = x.shape
    tm = min(512, rows)
    has_res = dres is not None

    def body(*refs):
        dh_ref, x_ref, g_ref = refs[:3]
        res_ref = refs[3] if has_res else None
        outs = refs[3 + int(has_res):]
        dg_ref = outs[-1]
        xhat, r = _rms(x_ref[...])
        dx, dgr = _rms_bwd(dh_ref[...], xhat, r, g_ref[...])
        if has_res:
            dx = dx + res_ref[...]
        for dx_ref, dt in zip(outs, dx_dtypes):
            dx_ref[...] = dx.astype(dt)

        @pl.when(pl.program_id(0) == 0)
        def _():
            dg_ref[...] = jnp.zeros_like(dg_ref)

        dg_ref[...] += jnp.sum(dgr, axis=0, keepdims=True)

    row = pl.BlockSpec((tm, dm), lambda i: (i, 0))
    vec = pl.BlockSpec((1, dm), lambda i: (0, 0))
    return pl.pallas_call(
        body, name=name, grid=(rows // tm,),
        in_specs=[row, row, vec] + ([row] if has_res else []),
        out_specs=[row] * len(dx_dtypes) + [vec],
        out_shape=[jax.ShapeDtypeStruct((rows, dm), dt) for dt in dx_dtypes] + [jax.ShapeDtypeStruct((1, dm), F32)],
        compiler_params=_params(("arbitrary",)),
    )(*([dh, x, gain] + ([dres] if has_res else [])))


def _out_norm(a, w, x, gain, *, name):
    rows, dm = x.shape
    tm = min(512, rows)

    def body(a_ref, w_ref, x_ref, g_ref, x1_ref, h_ref):
        x1 = x_ref[...] + _dot(a_ref[...], w_ref[...])
        x1_ref[...] = x1
        xhat, _ = _rms(x1)
        h_ref[...] = (xhat * g_ref[...]).astype(BF16)

    row = lambda wd: pl.BlockSpec((tm, wd), lambda i: (i, 0))
    return pl.pallas_call(
        body, name=name, grid=(rows // tm,),
        in_specs=[row(a.shape[1]), pl.BlockSpec(w.shape, lambda i: (0, 0)), row(dm), pl.BlockSpec((1, dm), lambda i: (0, 0))],
        out_specs=[row(dm), row(dm)],
        out_shape=[jax.ShapeDtypeStruct((rows, dm), F32), jax.ShapeDtypeStruct((rows, dm), BF16)],
        compiler_params=_params(("parallel",)),
    )(a, w, x, gain)


def _out_loss(u, w, x1, tgt, *, name):
    rows, dm = x1.shape
    tm = min(256, rows)
    steps = rows // tm

    def body(u_ref, w_ref, x_ref, t_ref, dy_ref, dyb_ref, loss_ref, acc_ref):
        i = pl.program_id(0)
        diff = x_ref[...] + _dot(u_ref[...], w_ref[...]) - t_ref[...]
        dy = diff * (1.0 / dm)
        dy_ref[...] = dy
        dyb_ref[...] = dy.astype(BF16)

        @pl.when(i == 0)
        def _():
            acc_ref[...] = jnp.zeros_like(acc_ref)

        acc_ref[...] += jnp.sum(diff * diff, axis=0, keepdims=True)

        @pl.when(i == steps - 1)
        def _():
            loss_ref[...] = jnp.full((1, HEAD), 0.5 / dm, F32) * jnp.sum(acc_ref[...])

    row = lambda wd: pl.BlockSpec((tm, wd), lambda i: (i, 0))
    return pl.pallas_call(
        body, name=name, grid=(steps,),
        in_specs=[row(u.shape[1]), pl.BlockSpec(w.shape, lambda i: (0, 0)), row(dm), row(dm)],
        out_specs=[row(dm), row(dm), pl.BlockSpec((1, HEAD), lambda i: (0, 0))],
        out_shape=[jax.ShapeDtypeStruct((rows, dm), F32), jax.ShapeDtypeStruct((rows, dm), BF16),
                   jax.ShapeDtypeStruct((1, HEAD), F32)],
        scratch_shapes=[pltpu.VMEM((1, dm), F32)],
        compiler_params=_params(("arbitrary",)),
    )(u, w, x1, tgt)


def _gla_block_terms(q_raw, f_logit, lb, rev):
    sig = _sigmoid(f_logit)
    forget = lb + (1.0 - lb) * sig
    k = 1.0 - forget
    b = _chunk_cumsum(jnp.log(forget), rev)
    qs = _sigmoid(q_raw)
    eb = jnp.exp(b)
    emb = jnp.exp(-b)
    qt = (q_raw * qs * HG_SCALE) * eb
    kt = k * emb
    return sig, forget, k, b, qs, eb, emb, qt, kt


def _masked_sum(mask, v):
    mb = mask.astype(BF16)
    hi = v.astype(BF16)
    mid = (v - hi.astype(F32)).astype(BF16)
    return jnp.dot(mb, hi, preferred_element_type=F32) + jnp.dot(mb, mid, preferred_element_type=F32)


def _chunk_cumsum(v, rev):
    n = v.shape[0]
    pos = lax.broadcasted_iota(jnp.int32, v.shape, 0) & (CHUNK - 1)
    step = 1
    while step < CHUNK:
        if rev:
            shifted, keep = pltpu.roll(v, n - step, 0), pos < CHUNK - step
        else:
            shifted, keep = pltpu.roll(v, step, 0), pos >= step
        v = v + jnp.where(keep, shifted, 0.0)
        step *= 2
    return v


def _tri_mask(n, rev):
    row, col = np.arange(n)[:, None], np.arange(n)[None, :]
    same = (row // CHUNK) == (col // CHUNK)
    return jnp.asarray((same & ((row <= col) if rev else (row >= col))).astype(np.float32))


def _chunk_order(ncb, rev):
    order = range(ncb - 1, -1, -1) if rev else range(ncb)
    return [(c, c * CHUNK if rev else c * CHUNK + CHUNK - 1) for c in order]


def _gla_fwd(proj, lb, *, f_off, rev, name):
    rows = proj.shape[0]
    tb = min(256, rows)
    nb, ncb = rows // tb, tb // CHUNK

    def tmap(n):
        return nb - 1 - n if rev else n

    def body(tri_ref, q_ref, f_ref, v_ref, lb_ref, o_ref, st_ref, s_ref):
        @pl.when(pl.program_id(1) == 0)
        def _():
            s_ref[...] = jnp.zeros_like(s_ref)

        tri = tri_ref[...] > 0.5
        for hh in range(GLA_HEADS_PER_STEP):
            cs = slice(hh * HEAD, (hh + 1) * HEAD)
            v = v_ref[:, cs]
            _, _, k, b, _, _, _, qt, kt = _gla_block_terms(q_ref[:, cs].astype(F32), f_ref[:, cs].astype(F32),
                                                           lb_ref[:, cs], rev)
            qt = qt.astype(BF16)
            o_intra = _dot(jnp.where(tri, _dot_nt(qt, kt), 0.0), v)
            chunks = []
            for c, last in _chunk_order(ncb, rev):
                sl = slice(c * CHUNK, (c + 1) * CHUNK)
                bl = b[last:last + 1, :]
                kh = k[sl] * jnp.exp(bl - b[sl])
                chunks.append((c, sl, jnp.exp(bl), _dot_tn(v[sl], kh)))
            s_t = s_ref[hh]
            for c, sl, ebl, kv in chunks:
                st_ref[hh, c] = s_t
                o_ref[sl, cs] = o_intra[sl] + _dot_nt(qt[sl], s_t)
                s_t = ebl * s_t + kv
            s_ref[hh] = s_t

    hps = GLA_HEADS_PER_STEP
    col = lambda off: pl.BlockSpec((tb, hps * HEAD), lambda h, n: (tmap(n), off // (hps * HEAD) + h))
    return pl.pallas_call(
        body, name=name, grid=(HG_HEADS // hps, nb),
        in_specs=[pl.BlockSpec((tb, tb), lambda h, n: (0, 0)), col(C_HQ), col(f_off), col(C_HI),
                  pl.BlockSpec((1, hps * HEAD), lambda h, n: (0, h))],
        out_specs=[pl.BlockSpec((tb, hps * HEAD), lambda h, n: (tmap(n), h)),
                   pl.BlockSpec((hps, ncb, HEAD, HEAD), lambda h, n: (h, tmap(n), 0, 0))],
        out_shape=[jax.ShapeDtypeStruct((rows, HG_HEADS * HEAD), F32),
                   jax.ShapeDtypeStruct((HG_HEADS, rows // CHUNK, HEAD, HEAD), F32)],
        scratch_shapes=[pltpu.VMEM((hps, HEAD, HEAD), F32)],
        compiler_params=_params(("parallel", "arbitrary")),
    )(_tri_mask(tb, rev), proj, proj, proj, lb)


def _gla_bwd(proj, lb, do, states, prev, dproj, *, f_off, rev, carry=NO_CARRY, name):
    rows = proj.shape[0]
    tb = min(256, rows)
    nb, ncb = rows // tb, tb // CHUNK
    has_prev = prev is not None
    qv_dtype = BF16 if has_prev else F32

    def tmap(n):
        return n if rev else nb - 1 - n

    def body(*refs):
        tri_ref, q_ref, f_ref, v_ref, lb_ref, do_ref, st_ref = refs[:7]
        pq_ref, pv_ref = refs[8:10] if has_prev else (None, None)
        trit_ref = refs[8 + 2 * int(has_prev)]
        (dq_ref, df_ref, dv_ref, dlb_ref,
         ds_ref, dqt_scr, dk_scr, db_scr, dbl_scr, dv_scr) = refs[9 + 2 * int(has_prev):]

        @pl.when(pl.program_id(1) == 0)
        def _():
            ds_ref[...] = jnp.zeros_like(ds_ref)
            dlb_ref[...] = jnp.zeros_like(dlb_ref)

        tri = tri_ref[...] > 0.5
        for hh in range(GLA_HEADS_PER_STEP):
            cs = slice(hh * HEAD, (hh + 1) * HEAD)
            lbv = lb_ref[:, cs]
            q_raw = q_ref[:, cs].astype(F32)
            v, dout = v_ref[:, cs], do_ref[:, cs].astype(BF16)
            sig, forget, k, b, qs, eb, emb, qt, kt = _gla_block_terms(q_raw, f_ref[:, cs].astype(F32), lbv, rev)
            qt_b, kt_b = qt.astype(BF16), kt.astype(BF16)
            a = jnp.where(tri, _dot_nt(qt_b, kt_b), 0.0)
            da = jnp.where(tri, _dot_nt(dout, v), 0.0).astype(BF16)
            dv_intra = _dot_tn(a, dout)
            dqt_intra = _dot(da, kt_b)
            dkt = _dot_tn(da, qt_b)
            chunks = []
            for c, last in reversed(_chunk_order(ncb, rev)):
                sl = slice(c * CHUNK, (c + 1) * CHUNK)
                bl = b[last:last + 1, :]
                e = jnp.exp(bl - b[sl])
                s_t = st_ref[hh, c]
                dqt_scr[sl, cs] = dqt_intra[sl] + _dot(dout[sl], s_t)
                chunks.append((sl, jnp.exp(bl), e, k[sl] * e, s_t, _dot_tn(dout[sl], qt_b[sl])))
            ds_t = ds_ref[hh]
            for sl, ebl, e, kh, s_t, grow in chunks:
                dkh = _dot(v[sl], ds_t)
                dv_scr[sl, cs] = dv_intra[sl] + _dot_nt(kh, ds_t)
                dk_scr[sl, cs] = dkt[sl] * emb[sl] + dkh * e
                khd = kh * dkh
                dbl = jnp.sum(khd, axis=0, keepdims=True) + ebl * jnp.sum(ds_t * s_t, axis=0, keepdims=True)
                db_scr[sl, cs] = khd
                dbl_scr[sl, cs] = jnp.broadcast_to(dbl, (CHUNK, HEAD))
                ds_t = grow + ds_t * ebl
            ds_ref[hh] = ds_t
            dqt = dqt_scr[:, cs]
            dlogf = _masked_sum(trit_ref[...], qt * dqt - kt * dkt - db_scr[:, cs]) + dbl_scr[:, cs]
            dforget = dlogf / forget - dk_scr[:, cs]
            df_ref[:, cs] = (dforget * (1.0 - lbv) * sig * (1.0 - sig)).astype(BF16)
            dlb_ref[:, cs] += jnp.sum(dforget * (1.0 - sig), axis=0, keepdims=True)
            dqr = dqt * eb * (HG_SCALE * qs * (1.0 + q_raw * (1.0 - qs)))
            dv = dv_scr[:, cs]
            if has_prev:
                dqr = dqr + pq_ref[:, cs]
                dv = dv + pv_ref[:, cs]
            dq_ref[:, cs] = dqr.astype(qv_dtype)
            dv_ref[:, cs] = dv.astype(qv_dtype)

    hps = GLA_HEADS_PER_STEP
    col = lambda off: pl.BlockSpec((tb, hps * HEAD), lambda h, n: (tmap(n), off // (hps * HEAD) + h))
    blk = pl.BlockSpec((tb, hps * HEAD), lambda h, n: (tmap(n), h))
    vec = pl.BlockSpec((1, hps * HEAD), lambda h, n: (0, h))
    wide = HG_HEADS * HEAD
    into = jax.ShapeDtypeStruct(dproj.shape, dproj.dtype)
    outs, carried = _call(
        body, name=name, grid=(HG_HEADS // hps, nb),
        in_specs=[pl.BlockSpec((tb, tb), lambda h, n: (0, 0)), col(C_HQ), col(f_off), col(C_HI), vec, blk,
                  pl.BlockSpec((hps, ncb, HEAD, HEAD), lambda h, n: (h, tmap(n), 0, 0)), ANY]
                 + ([blk, blk] if has_prev else []) + [pl.BlockSpec((tb, tb), lambda h, n: (0, 0))],
        out_specs=[col(C_HQ) if has_prev else blk, blk if has_prev else col(f_off), blk, vec],
        out_shape=[into if has_prev else jax.ShapeDtypeStruct((rows, wide), qv_dtype),
                   jax.ShapeDtypeStruct((rows, wide), BF16) if has_prev else into,
                   jax.ShapeDtypeStruct((rows, wide), qv_dtype), jax.ShapeDtypeStruct((1, wide), F32)],
        aliases={7: 0 if has_prev else 1},
        scratch_shapes=[pltpu.VMEM((hps, HEAD, HEAD), F32)] + [pltpu.VMEM((tb, hps * HEAD), F32)] * 5,
        semantics=("parallel", "arbitrary"),
        ins=[_tri_mask(tb, rev), proj, proj, proj, lb, do, states, dproj] + (list(prev) if has_prev else [])
            + [_tri_mask(tb, not rev)],
        carry=None if carry is NO_CARRY else carry)
    return outs if carry is NO_CARRY else (outs, carried)


def _fill_columns(dproj, parts, col, *, name):
    rows, wd = parts[0].shape
    tm = min(1024, rows)
    n = len(parts)

    def body(*refs):
        o_ref = refs[n + 1]
        for j in range(n):
            o_ref[:, j * wd:(j + 1) * wd] = refs[j][...]

    return pl.pallas_call(
        body, name=name, grid=(rows // tm,),
        in_specs=[pl.BlockSpec((tm, wd), lambda i: (i, 0))] * n + [ANY],
        out_specs=pl.BlockSpec((pl.Element(tm), pl.Element(n * wd)), lambda i: (i * tm, col)),
        out_shape=jax.ShapeDtypeStruct(dproj.shape, dproj.dtype), input_output_aliases={n: 0},
        compiler_params=_params(("parallel",)),
    )(*parts, dproj)


def _hg_out_fwd(o_fw, o_bw, proj, gain, *, name):
    rows = o_fw.shape[0]
    tm = min(1024, rows)
    wide = HG_HEADS * HEAD

    def body(a_ref, b_ref, g_ref, gain_ref, o_ref):
        for h in range(HG_HEADS):
            sl = slice(h * HEAD, (h + 1) * HEAD)
            xhat, _ = _rms(a_ref[:, sl] + b_ref[:, sl])
            gate = g_ref[:, sl].astype(F32)
            o_ref[:, sl] = (xhat * gain_ref[...] * (gate * _sigmoid(gate))).astype(BF16)

    row = pl.BlockSpec((tm, wide), lambda i: (i, 0))
    return pl.pallas_call(
        body, name=name, grid=(rows // tm,),
        in_specs=[row, row, pl.BlockSpec((tm, wide), lambda i: (i, C_HG // wide)),
                  pl.BlockSpec((1, HEAD), lambda i: (0, 0))],
        out_specs=row, out_shape=jax.ShapeDtypeStruct((rows, wide), BF16),
        compiler_params=_params(("parallel",)),
    )(o_fw, o_bw, proj, gain)


def _hg_out_bwd(dout, o_fw, o_bw, proj, gain, dproj, *, name):
    rows = o_fw.shape[0]
    tm = min(512, rows)
    wide = HG_HEADS * HEAD

    def body(d_ref, a_ref, b_ref, g_ref, gain_ref, dp_in, dgate_ref, do_ref, dgain_ref):
        del dp_in

        @pl.when(pl.program_id(0) == 0)
        def _():
            dgain_ref[...] = jnp.zeros_like(dgain_ref)

        dgain = jnp.zeros((1, HEAD), F32)
        for h in range(HG_HEADS):
            sl = slice(h * HEAD, (h + 1) * HEAD)
            xhat, r = _rms(a_ref[:, sl] + b_ref[:, sl])
            gate, dy = g_ref[:, sl].astype(F32), d_ref[:, sl]
            gs = _sigmoid(gate)
            dgate_ref[:, sl] = (dy * xhat * gain_ref[...] * (gs * (1.0 + gate * (1.0 - gs)))).astype(BF16)
            dx, dgr = _rms_bwd(dy * (gate * gs), xhat, r, gain_ref[...])
            do_ref[:, sl] = dx
            dgain = dgain + jnp.sum(dgr, axis=0, keepdims=True)
        dgain_ref[...] += dgain

    row = pl.BlockSpec((tm, wide), lambda i: (i, 0))
    gate = pl.BlockSpec((tm, wide), lambda i: (i, C_HG // wide))
    vec = pl.BlockSpec((1, HEAD), lambda i: (0, 0))
    return pl.pallas_call(
        body, name=name, grid=(rows // tm,),
        in_specs=[row, row, row, gate, vec, ANY],
        out_specs=[gate, row, vec],
        out_shape=[jax.ShapeDtypeStruct(dproj.shape, dproj.dtype), jax.ShapeDtypeStruct((rows, wide), F32),
                   jax.ShapeDtypeStruct((1, HEAD), F32)],
        input_output_aliases={5: 0},
        compiler_params=_params(("arbitrary",)),
    )(dout, o_fw, o_bw, proj, gain, dproj)


def _strided_rows(ref, r, count, d):
    return ref[...] if d == 1 else ref[pl.ds(r, count, stride=d), :]


def _store_strided(ref, r, count, d, val):
    if d == 1:
        ref[...] = val
    else:
        ref[pl.ds(r, count, stride=d), :] = val


def _da_prep(proj, q_gain, k_gain, g, *, name):
    d = DA_GROUPS[g][0]
    rows = proj.shape[0]
    rb = min(1024, rows)
    tn = rb // d

    def body(q_ref, k_ref, v_ref, qg_ref, kg_ref, qo_ref, ko_ref, vo_ref, qf_ref, kf_ref, vf_ref):
        for h in range(DA_GH):
            cs = slice(h * HEAD, (h + 1) * HEAD)
            for src, dst in ((q_ref, qf_ref), (k_ref, kf_ref), (v_ref, vf_ref)):
                dst[...] = src[:, cs].astype(F32)
            for r in range(d):
                qhat, _ = _rms(_strided_rows(qf_ref, r, tn, d))
                khat, _ = _rms(_strided_rows(kf_ref, r, tn, d))
                qo_ref[h, r] = (qhat * qg_ref[...]).astype(BF16)
                ko_ref[h, r] = (khat * kg_ref[...]).astype(BF16)
                vo_ref[h, r] = _strided_rows(vf_ref, r, tn, d).astype(BF16)

    wide = DA_GH * HEAD
    col = lambda off: pl.BlockSpec((rb, wide), lambda i: (i, off // wide + g))
    vec = pl.BlockSpec((1, HEAD), lambda i: (0, 0))
    out = pl.BlockSpec((DA_GH, d, tn, HEAD), lambda i: (0, 0, i, 0))
    shape = jax.ShapeDtypeStruct((DA_GH, d, rows // d, HEAD), BF16)
    return pl.pallas_call(
        body, name=name, grid=(rows // rb,),
        in_specs=[col(C_DQ), col(C_DK), col(C_DV), vec, vec],
        out_specs=[out, out, out], out_shape=[shape, shape, shape],
        scratch_shapes=[pltpu.VMEM((rb, HEAD), F32)] * 3,
        compiler_params=_params(("parallel",)),
    )(proj, proj, proj, q_gain, k_gain)


def _slopes(g):
    idx = np.arange(g * DA_GH + 1, (g + 1) * DA_GH + 1)
    s = (2.0 ** (-8.0 * idx / (DA_GH * len(DA_GROUPS)))).astype(np.float32)
    return jnp.asarray(np.broadcast_to(s[:, None, None], (DA_GH, 8, HEAD)).copy())


def _band_window(ld, t, radius):
    win = min(2 * t, ld)
    assert t // 2 >= radius or win == ld
    return win


def _band_scores(q, k, q0, start, slope, d, radius):
    t, win = q.shape[0], k.shape[0]
    row = lax.broadcasted_iota(jnp.int32, (t, win), 0)
    col = lax.broadcasted_iota(jnp.int32, (t, win), 1)
    rel = jnp.abs((start - q0) + col - row)
    return _dot_nt(q, k) * ATT_SCALE + jnp.where(rel <= radius, -slope * (d * rel).astype(F32), NEG_INF)


def _band_fwd(qr, kr, vr, g, *, name):
    d, radius = DA_GROUPS[g]
    _, _, ld, _ = qr.shape
    t = min(HEAD, ld)
    qb = min(BAND_QBLOCKS, ld // t)
    rps = max(1, min(d, BAND_QBLOCKS // qb))
    win = _band_window(ld, t, radius)

    def body(q_ref, k_ref, v_ref, sl_ref, o_ref, lse_ref):
        i = pl.program_id(2)
        slope = sl_ref[0:1, 0:1]
        for rr in range(rps):
            for j in range(qb):
                sl = slice(j * t, (j + 1) * t)
                q0 = (i * qb + j) * t
                start = pl.multiple_of(jnp.clip(q0 - t // 2, 0, ld - win), t // 2)
                s = _band_scores(q_ref[rr, sl, :], k_ref[rr, pl.ds(start, win), :], q0, start, slope, d, radius)
                m = jnp.max(s, axis=-1, keepdims=True)
                p = jnp.exp(s - m).astype(BF16)
                l = jnp.dot(p, _ones(win, HEAD), preferred_element_type=F32)
                o_ref[rr, sl, :] = _dot(p, v_ref[rr, pl.ds(start, win), :]) / l
                lse_ref[rr, sl, :] = m + jnp.log(l)

    own = pl.BlockSpec((None, rps, qb * t, HEAD), lambda h, r, i: (h, r, i, 0))
    seq = pl.BlockSpec((None, rps, ld, HEAD), lambda h, r, i: (h, r, 0, 0))
    shape = jax.ShapeDtypeStruct(qr.shape, F32)
    return pl.pallas_call(
        body, name=name, grid=(DA_GH, d // rps, ld // (qb * t)),
        in_specs=[own, seq, seq, pl.BlockSpec((None, 8, HEAD), lambda h, r, i: (h, 0, 0))],
        out_specs=[own, own], out_shape=[shape, shape],
        compiler_params=_params(("parallel", "parallel", "parallel")),
    )(qr, kr, vr, _slopes(g))


def _band_bwd(qr, kr, vr, dor, lser, deltar, g, *, name):
    d, radius = DA_GROUPS[g]
    _, _, ld, _ = qr.shape
    t = min(HEAD, ld)
    qb = min(BAND_QBLOCKS, ld // t)
    rps = max(1, min(d, BAND_QBLOCKS // qb))
    win = _band_window(ld, t, radius)

    def body(q_ref, k_ref, v_ref, do_ref, lse_ref, dl_ref, sl_ref, dq_ref, dk_ref, dv_ref):
        i = pl.program_id(2)

        @pl.when(i == 0)
        def _():
            dk_ref[...] = jnp.zeros_like(dk_ref)
            dv_ref[...] = jnp.zeros_like(dv_ref)

        slope = sl_ref[0:1, 0:1]
        for rr in range(rps):
            for j in range(qb):
                sl = slice(j * t, (j + 1) * t)
                q0 = (i * qb + j) * t
                start = pl.multiple_of(jnp.clip(q0 - t // 2, 0, ld - win), t // 2)
                rows = pl.ds(start, win)
                q, dout, k, v = q_ref[rr, sl, :], do_ref[rr, sl, :], k_ref[rr, rows, :], v_ref[rr, rows, :]
                p = jnp.exp(_band_scores(q, k, q0, start, slope, d, radius) - lse_ref[rr, sl, 0:1])
                ds = p * (_dot_nt(dout, v) - dl_ref[rr, sl, 0:1])
                dq_ref[rr, sl, :] = _dot(ds, k) * ATT_SCALE
                dk_ref[rr, rows, :] += _dot_tn(ds, q) * ATT_SCALE
                dv_ref[rr, rows, :] += _dot_tn(p, dout)

    own = pl.BlockSpec((None, rps, qb * t, HEAD), lambda h, r, i: (h, r, i, 0))
    seq = pl.BlockSpec((None, rps, ld, HEAD), lambda h, r, i: (h, r, 0, 0))
    shape = jax.ShapeDtypeStruct(qr.shape, F32)
    return pl.pallas_call(
        body, name=name, grid=(DA_GH, d // rps, ld // (qb * t)),
        in_specs=[own, seq, seq, own, own, own, pl.BlockSpec((None, 8, HEAD), lambda h, r, i: (h, 0, 0))],
        out_specs=[own, seq, seq], out_shape=[shape, shape, shape],
        compiler_params=_params(("parallel", "parallel", "arbitrary")),
    )(qr, kr, vr, dor, lser, deltar, _slopes(g))


def _da_merge(outs, lses, rows, *, name):
    rb = min(1024, rows)
    wide = DA_GH * HEAD

    def body(*refs):
        o_refs, l_refs = refs[0:3], refs[3:6]
        o_ref, lse_ref = refs[6:8]
        on_refs, ln_refs = refs[8:11], refs[11:14]
        for h in range(DA_GH):
            cs = slice(h * HEAD, (h + 1) * HEAD)
            for g, (d, _) in enumerate(DA_GROUPS):
                tn = rb // d
                for r in range(d):
                    _store_strided(on_refs[g], r, tn, d, o_refs[g][h, r])
                    _store_strided(ln_refs[g], r, tn, d, l_refs[g][h, r])
            l0, l1, l2 = ln_refs[0][...], ln_refs[1][...], ln_refs[2][...]
            m = jnp.maximum(jnp.maximum(l0, l1), l2)
            e0, e1, e2 = jnp.exp(l0 - m), jnp.exp(l1 - m), jnp.exp(l2 - m)
            tot = e0 + e1 + e2
            o_ref[:, cs] = (e0 * on_refs[0][...] + e1 * on_refs[1][...] + e2 * on_refs[2][...]) / tot
            lse_ref[:, cs] = m + jnp.log(tot)

    res = lambda d: pl.BlockSpec((DA_GH, d, rb // d, HEAD), lambda i: (0, 0, i, 0))
    nat = pl.BlockSpec((rb, wide), lambda i: (i, 0))
    shape = jax.ShapeDtypeStruct((rows, wide), F32)
    return pl.pallas_call(
        body, name=name, grid=(rows // rb,),
        in_specs=[res(d) for d, _ in DA_GROUPS] * 2,
        out_specs=[nat, nat], out_shape=[shape, shape],
        scratch_shapes=[pltpu.VMEM((rb, HEAD), F32)] * 6,
        compiler_params=_params(("parallel",)),
    )(*outs, *lses)


def _da_bwd_prep(dout, o, lse, *, name):
    rows = o.shape[0]
    rb = min(512, rows)
    wide = DA_GH * HEAD

    def body(d_ref, o_ref, l_ref, *outs):
        d_scr, l_scr, delta_scr = outs[9:12]
        for h in range(DA_GH):
            cs = slice(h * HEAD, (h + 1) * HEAD)
            dv = d_ref[:, cs]
            d_scr[...] = dv
            l_scr[...] = l_ref[:, cs]
            delta_scr[...] = _lane_sum(dv * o_ref[:, cs])
            for g, (d, _) in enumerate(DA_GROUPS):
                tn = rb // d
                for r in range(d):
                    outs[3 * g][h, r] = _strided_rows(d_scr, r, tn, d).astype(BF16)
                    outs[3 * g + 1][h, r] = _strided_rows(l_scr, r, tn, d)
                    outs[3 * g + 2][h, r] = _strided_rows(delta_scr, r, tn, d)

    nat = pl.BlockSpec((rb, wide), lambda i: (i, 0))
    out_specs, out_shape = [], []
    for d, _ in DA_GROUPS:
        for dt in (BF16, F32, F32):
            out_specs.append(pl.BlockSpec((DA_GH, d, rb // d, HEAD), lambda i: (0, 0, i, 0)))
            out_shape.append(jax.ShapeDtypeStruct((DA_GH, d, rows // d, HEAD), dt))
    return pl.pallas_call(
        body, name=name, grid=(rows // rb,),
        in_specs=[nat, nat, nat], out_specs=out_specs, out_shape=out_shape,
        scratch_shapes=[pltpu.VMEM((rb, HEAD), F32)] * 3,
        compiler_params=_params(("parallel",)),
    )(dout, o, lse)


def _da_prep_bwd(dqr, dkr, dvr, proj, q_gain, k_gain, g, *, name):
    d = DA_GROUPS[g][0]
    rows = proj.shape[0]
    rb = min(1024, rows)
    tn = rb // d
    wide = DA_GH * HEAD

    def body(dq_ref, dk_ref, dv_ref, q_ref, k_ref, qg_ref, kg_ref, oq_ref, ok_ref, ov_ref, gq_ref, gk_ref, *nat_refs):
        @pl.when(pl.program_id(0) == 0)
        def _():
            gq_ref[...] = jnp.zeros_like(gq_ref)
            gk_ref[...] = jnp.zeros_like(gk_ref)

        for h in range(DA_GH):
            cs = slice(h * HEAD, (h + 1) * HEAD)
            for j, src in enumerate((dq_ref, dk_ref, dv_ref)):
                for r in range(d):
                    _store_strided(nat_refs[j], r, tn, d, src[h, r])
            ov_ref[:, cs] = nat_refs[2][...].astype(BF16)
            for j, (x_ref, gn_ref, out_ref, acc_ref) in enumerate(((q_ref, qg_ref, oq_ref, gq_ref),
                                                                    (k_ref, kg_ref, ok_ref, gk_ref))):
                xhat, r = _rms(x_ref[:, cs])
                dx, dgr = _rms_bwd(nat_refs[j][...], xhat, r, gn_ref[...])
                out_ref[:, cs] = dx.astype(BF16)
                acc_ref[...] += jnp.sum(dgr, axis=0, keepdims=True)

    res = pl.BlockSpec((DA_GH, d, tn, HEAD), lambda i: (0, 0, i, 0))
    col = lambda off: pl.BlockSpec((rb, wide), lambda i: (i, off // wide + g))
    vec = pl.BlockSpec((1, HEAD), lambda i: (0, 0))
    nat = pl.BlockSpec((rb, wide), lambda i: (i, 0))
    shape = jax.ShapeDtypeStruct((rows, wide), BF16)
    gshape = jax.ShapeDtypeStruct((1, HEAD), F32)
    return pl.pallas_call(
        body, name=name, grid=(rows // rb,),
        in_specs=[res, res, res, col(C_DQ), col(C_DK), vec, vec],
        out_specs=[nat, nat, nat, vec, vec], out_shape=[shape, shape, shape, gshape, gshape],
        scratch_shapes=[pltpu.VMEM((rb, HEAD), F32)] * 3,
        compiler_params=_params(("arbitrary",)),
    )(dqr, dkr, dvr, proj, proj, q_gain, k_gain)


def _mem_fwd(proj, kv, q_gain, k_gain, *, name):
    rows = proj.shape[0]
    tm = min(1024, rows)
    n_mem = kv.shape[0]

    def body(q_ref, k_ref, v_ref, qg_ref, kg_ref, o_ref):
        qhat, _ = _rms(q_ref[...])
        khat, _ = _rms(k_ref[...])
        s = _dot_nt(qhat * qg_ref[...], khat * kg_ref[...]) * ATT_SCALE
        p = jnp.exp(s - jnp.max(s, axis=-1, keepdims=True))
        p = p / jnp.sum(p, axis=-1, keepdims=True)
        o_ref[...] = _dot(p, v_ref[...]).astype(BF16)

    vec = pl.BlockSpec((1, HEAD), lambda i, h: (0, 0))
    return pl.pallas_call(
        body, name=name, grid=(rows // tm, MEM_HEADS),
        in_specs=[pl.BlockSpec((tm, HEAD), lambda i, h: (i, C_MQ // HEAD + h)),
                  pl.BlockSpec((n_mem, HEAD), lambda i, h: (0, h)),
                  pl.BlockSpec((n_mem, HEAD), lambda i, h: (0, MEM_HEADS + h)), vec, vec],
        out_specs=pl.BlockSpec((tm, HEAD), lambda i, h: (i, h)),
        out_shape=jax.ShapeDtypeStruct((rows, MEM_HEADS * HEAD), BF16),
        compiler_params=_params(("parallel", "parallel")),
    )(proj, kv, kv, q_gain, k_gain)


def _mem_bwd(dout, proj, kv, q_gain, k_gain, dproj, *, name):
    rows = proj.shape[0]
    tm = min(1024, rows)
    steps = rows // tm
    n_mem = kv.shape[0]

    def body(d_ref, q_ref, k_ref, v_ref, qg_ref, kg_ref, dp_in, dq_ref, dk_ref, dv_ref, gq_ref, gk_ref, dkn_ref):
        del dp_in
        h, i = pl.program_id(0), pl.program_id(1)

        @pl.when((h == 0) & (i == 0))
        def _():
            gq_ref[...] = jnp.zeros_like(gq_ref)
            gk_ref[...] = jnp.zeros_like(gk_ref)

        @pl.when(i == 0)
        def _():
            dkn_ref[...] = jnp.zeros_like(dkn_ref)
            dv_ref[...] = jnp.zeros_like(dv_ref)

        qhat, rq = _rms(q_ref[...])
        khat, rk = _rms(k_ref[...])
        qn, kn = qhat * qg_ref[...], khat * kg_ref[...]
        s = _dot_nt(qn, kn) * ATT_SCALE
        p = jnp.exp(s - jnp.max(s, axis=-1, keepdims=True))
        p = p / jnp.sum(p, axis=-1, keepdims=True)
        dout = d_ref[...]
        dp = _dot_nt(dout, v_ref[...])
        ds = p * (dp - jnp.sum(p * dp, axis=-1, keepdims=True))
        dv_ref[...] += _dot_tn(p, dout)
        dkn_ref[...] += _dot_tn(ds, qn) * ATT_SCALE
        dq, dgr = _rms_bwd(_dot(ds, kn) * ATT_SCALE, qhat, rq, qg_ref[...])
        dq_ref[...] = dq.astype(BF16)
        gq_ref[...] += jnp.sum(dgr, axis=0, keepdims=True)

        @pl.when(i == steps - 1)
        def _():
            dk, dgk = _rms_bwd(dkn_ref[...], khat, rk, kg_ref[...])
            dk_ref[...] = dk
            gk_ref[...] += jnp.sum(dgk, axis=0, keepdims=True)

    vec = pl.BlockSpec((1, HEAD), lambda h, i: (0, 0))
    memh = pl.BlockSpec((n_mem, HEAD), lambda h, i: (0, h))
    tok = pl.BlockSpec((tm, HEAD), lambda h, i: (i, h))
    mq = pl.BlockSpec((tm, HEAD), lambda h, i: (i, C_MQ // HEAD + h))
    gshape = jax.ShapeDtypeStruct((1, HEAD), F32)
    return pl.pallas_call(
        body, name=name, grid=(MEM_HEADS, steps),
        in_specs=[tok, mq, memh, pl.BlockSpec((n_mem, HEAD), lambda h, i: (0, MEM_HEADS + h)), vec, vec, ANY],
        out_specs=[mq, memh, memh, vec, vec],
        out_shape=[jax.ShapeDtypeStruct(dproj.shape, dproj.dtype),
                   jax.ShapeDtypeStruct((n_mem, MEM_HEADS * HEAD), F32),
                   jax.ShapeDtypeStruct((n_mem, MEM_HEADS * HEAD), F32), gshape, gshape],
        input_output_aliases={6: 0},
        scratch_shapes=[pltpu.VMEM((n_mem, HEAD), F32)],
        compiler_params=_params(("arbitrary", "arbitrary")),
    )(dout, proj, kv, kv, q_gain, k_gain, dproj)


def _branch_fwd(o_hg, o_da, o_mem, proj, wp_hg, wp_da, wp_mem, *, name):
    rows = o_hg.shape[0]
    tm = min(512, rows)

    def body(a_ref, b_ref, c_ref, ga_ref, gb_ref, gc_ref, wa_ref, wb_ref, wc_ref, o_ref):
        merged = _sigmoid(ga_ref[...]) * _dot(a_ref[...], wa_ref[...])
        merged += _sigmoid(gb_ref[...]) * _dot(b_ref[...], wb_ref[...])
        merged += _sigmoid(gc_ref[...]) * _dot(c_ref[...], wc_ref[...])
        o_ref[...] = merged.astype(BF16)

    row = lambda w: pl.BlockSpec((tm, w), lambda i: (i, 0))
    gate = lambda off: pl.BlockSpec((tm, D_MODEL), lambda i: (i, off // D_MODEL))
    full = lambda a: pl.BlockSpec(a.shape, lambda i: (0, 0))
    return pl.pallas_call(
        body, name=name, grid=(rows // tm,),
        in_specs=[row(o_hg.shape[1]), row(o_da.shape[1]), row(o_mem.shape[1]),
                  gate(C_GHG), gate(C_GDA), gate(C_GMEM), full(wp_hg), full(wp_da), full(wp_mem)],
        out_specs=row(D_MODEL), out_shape=jax.ShapeDtypeStruct((rows, D_MODEL), BF16),
        compiler_params=_params(("parallel",)),
    )(o_hg, o_da, o_mem, proj, proj, proj, wp_hg, wp_da, wp_mem)


def _branch_bwd(dm, o_hg, o_da, o_mem, proj, wp_hg, wp_da, wp_mem, *, name):
    rows = o_hg.shape[0]
    tm = min(256, rows)

    def body(dm_ref, a_ref, b_ref, c_ref, ga_ref, gb_ref, gc_ref, wa_ref, wb_ref, wc_ref, dp_ref, *outs):
        dmv = dm_ref[...]
        for j, (o_ref, g_ref, w_ref) in enumerate(((a_ref, ga_ref, wa_ref), (b_ref, gb_ref, wb_ref),
                                                   (c_ref, gc_ref, wc_ref))):
            z = _dot(o_ref[...], w_ref[...])
            gs = _sigmoid(g_ref[...])
            dz = (dmv * gs).astype(BF16)
            dp_ref[:, j * D_MODEL:(j + 1) * D_MODEL] = (dmv * z * gs * (1.0 - gs)).astype(BF16)
            outs[2 * j][...] = dz
            outs[2 * j + 1][...] = _dot_nt(dz, w_ref[...])

    row = lambda w: pl.BlockSpec((tm, w), lambda i: (i, 0))
    gate = lambda off: pl.BlockSpec((tm, D_MODEL), lambda i: (i, off // D_MODEL))
    full = lambda a: pl.BlockSpec(a.shape, lambda i: (0, 0))
    out_specs = [pl.BlockSpec((pl.Element(tm), pl.Element(3 * D_MODEL)), lambda i: (i * tm, C_GHG))]
    out_shape = [jax.ShapeDtypeStruct((rows, IN_COLS), BF16)]
    for o in (o_hg, o_da, o_mem):
        out_specs += [row(D_MODEL), row(o.shape[1])]
        out_shape += [jax.ShapeDtypeStruct((rows, D_MODEL), BF16), jax.ShapeDtypeStruct((rows, o.shape[1]), F32)]
    return pl.pallas_call(
        body, name=name, grid=(rows // tm,),
        in_specs=[row(D_MODEL), row(o_hg.shape[1]), row(o_da.shape[1]), row(o_mem.shape[1]),
                  gate(C_GHG), gate(C_GDA), gate(C_GMEM), full(wp_hg), full(wp_da), full(wp_mem)],
        out_specs=out_specs, out_shape=out_shape,
        compiler_params=_params(("parallel",)),
    )(dm, o_hg, o_da, o_mem, proj, proj, proj, wp_hg, wp_da, wp_mem)


def _ffn_in(h2, w_ab, *, name):
    rows, dff = h2.shape[0], w_ab.shape[1] // 2
    tm, tn = min(2048, rows), 256

    def body(h_ref, wa_ref, wb_ref, a_ref, b_ref, u_ref):
        a = _dot(h_ref[...], wa_ref[...])
        b = _dot(h_ref[...], wb_ref[...])
        a_ref[...] = a.astype(BF16)
        b_ref[...] = b.astype(BF16)
        u_ref[...] = (a * _sigmoid(a) * b).astype(BF16)

    out = pl.BlockSpec((tm, tn), lambda i, j: (i, j))
    return pl.pallas_call(
        body, name=name, grid=(rows // tm, dff // tn),
        in_specs=[pl.BlockSpec((tm, D_MODEL), lambda i, j: (i, 0)),
                  pl.BlockSpec((D_MODEL, tn), lambda i, j: (0, j)),
                  pl.BlockSpec((D_MODEL, tn), lambda i, j: (0, dff // tn + j))],
        out_specs=[out, out, out],
        out_shape=[jax.ShapeDtypeStruct((rows, dff), BF16)] * 3,
        compiler_params=_params(("parallel", "parallel")),
    )(h2, w_ab, w_ab)


def _ffn_act_bwd(dy, w_out, a, b, *, name):
    rows, dff = a.shape
    tm, tn = min(2048, rows), 256

    def body(dy_ref, w_ref, a_ref, b_ref, da_ref, db_ref):
        du = _dot_nt(dy_ref[...], w_ref[...])
        av, bv = a_ref[...].astype(F32), b_ref[...].astype(F32)
        sa = _sigmoid(av)
        da_ref[...] = (du * bv * sa * (1.0 + av * (1.0 - sa))).astype(BF16)
        db_ref[...] = (du * av * sa).astype(BF16)

    tile = pl.BlockSpec((tm, tn), lambda i, j: (i, j))
    return pl.pallas_call(
        body, name=name, grid=(rows // tm, dff // tn),
        in_specs=[pl.BlockSpec((tm, D_MODEL), lambda i, j: (i, 0)),
                  pl.BlockSpec((tn, D_MODEL), lambda i, j: (j, 0)), tile, tile],
        out_specs=[tile, tile],
        out_shape=[jax.ShapeDtypeStruct((rows, dff), BF16), jax.ShapeDtypeStruct((rows, dff), BF16)],
        compiler_params=_params(("parallel", "parallel")),
    )(dy, w_out, a, b)


def _lower_bound(lb_fw, lb_bw, *, name):
    def body(a_ref, b_ref, oa_ref, ob_ref):
        for src, dst in ((a_ref, oa_ref), (b_ref, ob_ref)):
            dst[...] = _sigmoid(src[0:1, :] - src[1:2, :])

    shape = jax.ShapeDtypeStruct((1, lb_fw.shape[1]), F32)
    return pl.pallas_call(body, name=name, out_shape=[shape, shape])(lb_fw, lb_bw)


def _local_step(x, mem, tgt, p, ex):
    rows = x.shape[0]
    lb_fw, lb_bw = _lower_bound(p["lb_logits_fw"], p["lb_logits_bw"], name="lower_bound")

    h = _rms_fwd(x, p["norm_mix_gain"], name="norm_mix")
    w = {"w_in": ex.w_in}
    proj, carried = _matmul([(h, w["w_in"])], "nn", BF16, tm=1024, tn=1024, carry=ex.late_carry(), name="proj_in")
    w.update(ex.late_weights(carried))
    o_fw, st_fw = _gla_fwd(proj, lb_fw, f_off=C_FF, rev=False, name="gla_fwd_fw")
    o_bw, st_bw = _gla_fwd(proj, lb_bw, f_off=C_FB, rev=True, name="gla_fwd_bw")
    o_hg = _hg_out_fwd(o_fw, o_bw, proj, p["hg_norm_gain"], name="hg_out")

    qkv_r, outs, lses = [], [], []
    for g in range(len(DA_GROUPS)):
        qr, kr, vr = _da_prep(proj, p["da_q_gain"], p["da_k_gain"], g, name=f"da_prep{g}")
        og, lg = _band_fwd(qr, kr, vr, g, name=f"band_fwd{g}")
        qkv_r.append((qr, kr, vr))
        outs.append(og)
        lses.append(lg)
    o_da, lse_da = _da_merge(outs, lses, rows, name="da_merge")

    mem_n = _rms_fwd(mem, p["norm_mem_gain"], name="norm_mem")
    kv = _matmul([(mem_n, w["w_mem_kv"])], "nn", F32, tm=256, tn=512, name="mem_kv")
    o_mem = _mem_fwd(proj, kv, p["mem_q_gain"], p["mem_k_gain"], name="mem_attn")

    merged = _branch_fwd(o_hg, o_da, o_mem, proj, w["w_proj_hg"], w["w_proj_da"], w["w_proj_mem"],
                         name="branch_merge")
    x1, h2 = _out_norm(merged, w["w_out"], x, p["norm_ffn_gain"], name="mix_out_norm")
    a, b, u = _ffn_in(h2, w["w_ffn_in"], name="ffn_in")
    dy, dy_b, loss = _out_loss(u, w["w_ffn_out"], x1, tgt, name="ffn_out_loss")

    gw, gwb, gs = {}, {}, {}
    gw["w_ffn_out"], gwb["w_ffn_out"] = _matmul([(u, dy_b)], "tn", F32, tm=256, tn=1024, also=BF16, name="g_ffn_out")
    da, db = _ffn_act_bwd(dy_b, w["w_ffn_out"], a, b, name="ffn_act_bwd")
    gw["w_ffn_a"], gwb["w_ffn_a"] = _matmul([(h2, da)], "tn", F32, tm=1024, tn=256, also=BF16, name="g_ffn_a")
    gw["w_ffn_b"], gwb["w_ffn_b"] = _matmul([(h2, db)], "tn", F32, tm=1024, tn=256, also=BF16, name="g_ffn_b")
    dh2 = _matmul([(da, w["w_ffn_in"], 0), (db, w["w_ffn_in"], 1)], "nt", F32, tm=256, tn=1024, name="d_h2")
    dx1, dx1_b, gs["norm_ffn_gain"] = _rms_bwd_rows(dh2, x1, p["norm_ffn_gain"], dy, (F32, BF16), name="norm_ffn_bwd")
    gw["w_out"], gwb["w_out"] = _matmul([(merged, dx1_b)], "tn", F32, tm=512, tn=512, also=BF16, name="g_out")
    dmerged = _matmul([(dx1_b, w["w_out"])], "nt", F32, tm=512, tn=512, name="d_merged")
    dproj, dz_hg, do_hg, dz_da, do_da, dz_mem, do_mem = _branch_bwd(
        dmerged, o_hg, o_da, o_mem, proj, w["w_proj_hg"], w["w_proj_da"], w["w_proj_mem"], name="branch_bwd")
    gw["w_proj_hg"], gwb["w_proj_hg"] = _matmul([(o_hg, dz_hg)], "tn", F32, tm=512, tn=512, also=BF16, name="g_proj_hg")
    gw["w_proj_da"], gwb["w_proj_da"] = _matmul([(o_da, dz_da)], "tn", F32, tm=512, tn=512, also=BF16, name="g_proj_da")
    gw["w_proj_mem"], gwb["w_proj_mem"] = _matmul([(o_mem, dz_mem)], "tn", F32, tm=512, tn=512, also=BF16, name="g_proj_mem")

    dproj, dk_mem, dv_mem, gs["mem_q_gain"], gs["mem_k_gain"] = _mem_bwd(
        do_mem, proj, kv, p["mem_q_gain"], p["mem_k_gain"], dproj, name="mem_attn_bwd")
    dkv = jnp.concatenate([dk_mem, dv_mem], axis=1).astype(BF16)
    gw["w_mem_kv"], gwb["w_mem_kv"] = _matmul([(mem_n, dkv)], "tn", F32, tm=512, tn=512, also=BF16, name="g_mem_kv")
    dmem_n = _matmul([(dkv, w["w_mem_kv"])], "nt", F32, tm=256, tn=512, name="d_mem_n")
    (gs["norm_mem_gain"],) = _rms_bwd_rows(dmem_n, mem, p["norm_mem_gain"], None, (), name="norm_mem_bwd")

    prep = _da_bwd_prep(do_da, o_da, lse_da, name="da_bwd_prep")
    d_da, gq_parts, gk_parts = [], [], []
    for g in range(len(DA_GROUPS)):
        qr, kr, vr = qkv_r[g]
        dor, lser, deltar = prep[3 * g:3 * g + 3]
        dqr, dkr, dvr = _band_bwd(qr, kr, vr, dor, lser, deltar, g, name=f"band_bwd{g}")
        dq, dk, dv, gq, gk = _da_prep_bwd(dqr, dkr, dvr, proj, p["da_q_gain"], p["da_k_gain"], g,
                                          name=f"da_prep_bwd{g}")
        d_da.append((dq, dk, dv))
        gq_parts.append(gq)
        gk_parts.append(gk)
    for j, (off, tag) in enumerate(((C_DQ, "q"), (C_DK, "k"), (C_DV, "v"))):
        dproj = _fill_columns(dproj, [t[j] for t in d_da], off, name=f"dproj_fill_da_{tag}")

    dproj, do_gla, gs["hg_norm_gain"] = _hg_out_bwd(do_hg, o_fw, o_bw, proj, p["hg_norm_gain"], dproj,
                                                    name="hg_out_bwd")
    carry_bulk, carry_rest = ex.early_carries(gw, gwb)
    (dq_f, dproj, dv_f, dlb_fw), bulk = _gla_bwd(proj, lb_fw, do_gla, st_fw, None, dproj, f_off=C_FF, rev=False,
                                                 carry=carry_bulk, name="gla_bwd_fw")
    (dproj, dfl_bw, dv_hg, dlb_bw), rest = _gla_bwd(proj, lb_bw, do_gla, st_bw, (dq_f, dv_f), dproj, f_off=C_FB,
                                                    rev=True, carry=carry_rest, name="gla_bwd_bw")
    ex.early_done(bulk, rest)
    dproj = _fill_columns(dproj, [dfl_bw, dv_hg], C_FB, name="dproj_fill")
    g_in, g_in_b = _matmul([(h, dproj)], "tn", F32, tm=1024, tn=512, also=BF16, name="g_in")
    dh, carried = _matmul([(dproj, w["w_in"])], "nt", F32, tm=1024, tn=1024, tk=IN_COLS // 8,
                          carry=ex.w_in_carry(g_in, g_in_b), name="d_h")
    ex.w_in_done(carried)
    grad_x, gs["norm_mix_gain"] = _rms_bwd_rows(dh, x, p["norm_mix_gain"], dx1, (F32,), name="norm_mix_bwd")

    small = _small_pack(gs, gq_parts, gk_parts, dlb_fw, dlb_bw, lb_fw, lb_bw, loss, name="small_pack")
    return grad_x, small


def _small_pack(gs, gq_parts, gk_parts, dlb_fw, dlb_bw, lb_fw, lb_bw, loss, *, name):
    def body(g_mix, g_mem, g_ffn, dfw, dbw, lfw, lbw, g_hg, q0, q1, q2, k0, k1, k2, g_mq, g_mk, loss_ref, o_ref):
        o_ref[0:1, :] = g_mix[...]
        o_ref[1:2, :] = g_mem[...]
        o_ref[2:3, :] = g_ffn[...]
        for base, d_ref, l_ref in ((3, dfw, lfw), (5, dbw, lbw)):
            lbv = l_ref[...]
            t = d_ref[...] * lbv * (1.0 - lbv)
            o_ref[base:base + 1, :] = t
            o_ref[base + 1:base + 2, :] = -t
        o_ref[7:8, :] = jnp.zeros((1, D_MODEL), F32)
        o_ref[7:8, 0:HEAD] = g_hg[...]
        o_ref[7:8, HEAD:2 * HEAD] = q0[...] + q1[...] + q2[...]
        o_ref[7:8, 2 * HEAD:3 * HEAD] = k0[...] + k1[...] + k2[...]
        o_ref[7:8, 3 * HEAD:4 * HEAD] = g_mq[...]
        o_ref[7:8, 4 * HEAD:5 * HEAD] = g_mk[...]
        o_ref[7:8, LOSS_COL:LOSS_COL + HEAD] = loss_ref[...]

    return pl.pallas_call(body, name=name, out_shape=jax.ShapeDtypeStruct((8, D_MODEL), F32))(
        gs["norm_mix_gain"], gs["norm_mem_gain"], gs["norm_ffn_gain"], dlb_fw, dlb_bw, lb_fw, lb_bw,
        gs["hg_norm_gain"], *gq_parts, *gk_parts, gs["mem_q_gain"], gs["mem_k_gain"], loss)


def _row_tile(rows, cols, n_arrays):
    budget = (16 * 1024 * 1024) // (2 * 4 * cols * n_arrays)
    tr = rows
    while tr > budget and tr % 2 == 0 and (tr // 2) % 16 == 0:
        tr //= 2
    return tr


def _cast_into_full(a, chip, rows, cols, axis, *, name):
    sr, sc = a.shape
    tr = _row_tile(sr, sc, 2)

    def body(chip_ref, a_ref, o_ref):
        del chip_ref
        o_ref[...] = a_ref[...].astype(BF16)

    if axis == 1:
        out_map = lambda i, chip_ref: (i, chip_ref[0])
    else:
        out_map = lambda i, chip_ref: (chip_ref[0] * (sr // tr) + i, 0)
    return pl.pallas_call(
        body, name=name,
        grid_spec=pltpu.PrefetchScalarGridSpec(
            num_scalar_prefetch=1, grid=(sr // tr,),
            in_specs=[pl.BlockSpec((tr, sc), lambda i, chip_ref: (i, 0))],
            out_specs=pl.BlockSpec((tr, sc), out_map)),
        out_shape=jax.ShapeDtypeStruct((rows, cols), BF16),
        compiler_params=_params(("parallel",)))(chip, a)


def _add_halves(items, core, *, name):
    rows = items[0][3].shape[0]
    widths = [ra.shape[1] for _, _, _, ra in items]
    tr = _row_tile(rows, sum(widths), 3)

    def body(core_ref, *refs):
        del core_ref
        o_ref = refs[-1]
        off = 0
        for j, wd in enumerate(widths):
            mine, theirs = refs[2 * j:2 * j + 2]
            o_ref[:, off:off + wd] = (mine[...] + theirs[...]).astype(BF16)
            off += wd

    in_specs, ins = [], []
    for (g, haxis, hsize, ra), wd in zip(items, widths):
        if haxis == 0:
            in_specs.append(pl.BlockSpec((tr, wd), lambda i, core_ref, o=hsize // tr: (core_ref[0] * o + i, 0)))
        else:
            in_specs.append(pl.BlockSpec((tr, wd), lambda i, core_ref: (i, core_ref[0])))
        in_specs.append(pl.BlockSpec((tr, wd), lambda i, core_ref: (i, 0)))
        ins += [g, ra]
    return pl.pallas_call(
        body, name=name,
        grid_spec=pltpu.PrefetchScalarGridSpec(
            num_scalar_prefetch=1, grid=(rows // tr,), in_specs=in_specs,
            out_specs=pl.BlockSpec((tr, sum(widths)), lambda i, core_ref: (i, 0))),
        out_shape=jax.ShapeDtypeStruct((rows, sum(widths)), BF16),
        compiler_params=_params(("parallel",)))(core, *ins)


def _add_slots(rb, *, name):
    _, rows, cols = rb.shape
    tr = _row_tile(rows, cols, 5)

    def body(r0, r1, r2, r3, o_ref):
        o_ref[...] = ((r0[...].astype(F32) + r1[...].astype(F32)) + r2[...].astype(F32)) + r3[...].astype(F32)

    slot = lambda s: pl.BlockSpec((None, tr, cols), lambda i: (s, i, 0))
    return pl.pallas_call(body, name=name, grid=(rows // tr,), in_specs=[slot(s) for s in range(4)],
                          out_specs=pl.BlockSpec((tr, cols), lambda i: (i, 0)),
                          out_shape=jax.ShapeDtypeStruct((rows, cols), F32),
                          compiler_params=_params(("parallel",)))(rb, rb, rb, rb)


def _adamw(w, g, m, v, *, name):
    rows, cols = w.shape
    tr = _row_tile(rows, cols, 8) if rows % 16 == 0 else rows
    c1 = 1.0 - ADAM_B1 ** ADAM_STEP
    c2 = 1.0 - ADAM_B2 ** ADAM_STEP

    def body(w_ref, g_ref, m_ref, v_ref, d_ref, mo_ref, vo_ref, go_ref):
        gv = g_ref[...]
        go_ref[...] = gv
        mn = ADAM_B1 * m_ref[...] + (1.0 - ADAM_B1) * gv
        vn = ADAM_B2 * v_ref[...] + (1.0 - ADAM_B2) * (gv * gv)
        mo_ref[...] = mn
        vo_ref[...] = vn
        d_ref[...] = -ADAM_LR * ((mn / c1) / (jnp.sqrt(vn / c2) + ADAM_EPS) + ADAM_WD * w_ref[...])

    spec = pl.BlockSpec((tr, cols), lambda i: (i, 0))
    shape = jax.ShapeDtypeStruct((rows, cols), F32)
    return pl.pallas_call(body, name=name, grid=(rows // tr,), in_specs=[spec] * 4, out_specs=[spec] * 4,
                        out_shape=[shape] * 4, compiler_params=_params(("parallel",)))(w, g, m, v)


W_SPECS = (
    ("w_in", 1024, IN_COLS, 1, IN_COLS // 4),
    ("w_mem_kv", 1024, 1024, 0, 256),
    ("w_proj_hg", 1024, 1024, 0, 256),
    ("w_proj_da", 512, 1024, 1, 256),
    ("w_proj_mem", 512, 1024, 1, 256),
    ("w_out", 1024, 1024, 0, 256),
    ("w_ffn_in", 1024, 2 * D_FF, 1, 2 * D_FF // 4),
    ("w_ffn_out", D_FF, 1024, 0, D_FF // 4),
)
CHIP_FLIPS = ((1, 0), (0, 1), (1, 1))
ANY = pl.BlockSpec(memory_space=pl.ANY)
DMA_CHUNK_BYTES = 1 << 20
STAGE_BYTES = 2 << 20


def _place():
    x, y, c = lax.axis_index("x"), lax.axis_index("y"), lax.axis_index("c")
    return x, y, c, 2 * x + y


def _flip(v, f):
    return 1 - v if f else v


def _slab(ref, axis, idx, size):
    start = pl.multiple_of(idx * size, size)
    return ref.at[pl.ds(start, size), :] if axis == 0 else ref.at[:, pl.ds(start, size)]


def _chunked(make, src, dst, want="both"):
    rows, cols = src.shape
    row_bytes = cols * jnp.dtype(src.dtype).itemsize
    k = 1
    while rows % (2 * k) == 0 and (rows // (2 * k)) % 16 == 0 and (rows // k) * row_bytes > DMA_CHUNK_BYTES:
        k *= 2
    cr = rows // k
    parts = []
    if want != "wait":
        parts = [make(src.at[pl.ds(j * cr, cr), :], dst.at[pl.ds(j * cr, cr), :]) for j in range(k)]
    return parts, (make(src, dst) if want != "start" else None)


def _half_spec(rows, cols, axis):
    return (0, rows // 2) if axis == 1 else (1, cols // 2)


def _staged(src, remote_dst, local_dst, sibling, load_sems, send_sems, store_sems, recv_sem):
    rows, cols = src.shape
    row_bytes = cols * jnp.dtype(src.dtype).itemsize
    k = 1
    while (rows // k) * row_bytes > STAGE_BYTES and rows % (2 * k) == 0 and (rows // (2 * k)) % 16 == 0:
        k *= 2
    cr = rows // k
    piece = lambda ref, j: ref.at[pl.ds(j * cr, cr), :]

    def run(buf):
        loads = [pltpu.make_async_copy(piece(src, j), buf.at[j % 2], load_sems.at[j % 2]) for j in range(k)]
        outs = [[pltpu.make_async_remote_copy(src_ref=buf.at[j % 2], dst_ref=piece(remote_dst, j),
                                              send_sem=send_sems.at[j % 2], recv_sem=recv_sem,
                                              device_id=sibling, device_id_type=MESH)] for j in range(k)]
        if local_dst is not None:
            for j in range(k):
                outs[j].append(pltpu.make_async_copy(buf.at[j % 2], piece(local_dst, j), store_sems.at[j % 2]))

        def drained(j):
            outs[j][0].wait_send()
            for cp in outs[j][1:]:
                cp.wait()

        loads[0].start()
        for j in range(k):
            loads[j].wait()
            for cp in outs[j]:
                cp.start()
            if j + 1 < k:
                if j >= 1:
                    drained(j - 1)
                loads[j + 1].start()
        for j in range(max(0, k - 2), k):
            drained(j)

    pl.run_scoped(run, pltpu.VMEM((2, cr, cols), src.dtype))


def _landed(ref, recv_sem, send_sem):
    pltpu.make_async_remote_copy(src_ref=ref, dst_ref=ref, send_sem=send_sem, recv_sem=recv_sem,
                                 device_id=(lax.axis_index("x"), lax.axis_index("y"), lax.axis_index("c")),
                                 device_id_type=MESH).wait_recv()


def _stage_pieces(shape, dtype):
    rows, cols = shape
    row_bytes = cols * jnp.dtype(dtype).itemsize
    k = 1
    while (rows // k) * row_bytes > STAGE_BYTES and rows % (2 * k) == 0 and (rows // (2 * k)) % 16 == 0:
        k *= 2
    return k


def _push_all(transfers, sibling, bufs, load_sems, send_sems, store_sems, recv_sems):
    pieces, j = [], 0
    for i, (src, remote_dst, local_dst) in enumerate(transfers):
        k = _stage_pieces(src.shape, src.dtype)
        cr = src.shape[0] // k
        for q in range(k):
            rows = pl.ds(q * cr, cr)
            load = pltpu.make_async_copy(src.at[rows, :], bufs[i].at[rows, :], load_sems.at[j])
            outs = [pltpu.make_async_remote_copy(src_ref=bufs[i].at[rows, :], dst_ref=remote_dst.at[rows, :],
                                                 send_sem=send_sems.at[j], recv_sem=recv_sems.at[i],
                                                 device_id=sibling, device_id_type=MESH)]
            if local_dst is not None:
                outs.append(pltpu.make_async_copy(bufs[i].at[rows, :], local_dst.at[rows, :], store_sems.at[j]))
            pieces.append((load, outs))
            j += 1
    for load, _ in pieces:
        load.start()
    for load, outs in pieces:
        load.wait()
        for cp in outs:
            cp.start()
    for _, outs in pieces:
        outs[0].wait_send()
        for cp in outs[1:]:
            cp.wait()


def _half_slab(ref, spec, chip, half):
    _, rows, cols, axis, size = spec
    haxis, hsize = _half_spec(rows, cols, axis)
    return _slab(_slab(ref, axis, chip, size), haxis, half, hsize)


def _gather_sends(outs, specs, ici_send, ici_recv, want="both"):
    x, y, c, p = _place()
    sent = []
    for wi, spec in enumerate(specs):
        mine = _half_slab(outs[wi], spec, p, c)
        for k, (fx, fy) in enumerate(CHIP_FLIPS):
            sent.append(_chunked(lambda s, d, j=3 * wi + k, fx=fx, fy=fy: pltpu.make_async_remote_copy(
                src_ref=s, dst_ref=d, send_sem=ici_send.at[j], recv_sem=ici_recv.at[j],
                device_id=(_flip(x, fx), _flip(y, fy), c), device_id_type=MESH), mine, mine, want))
    return sent


def _gather_weights(fulls, specs, *, name):
    n = len(specs)

    def body(*refs):
        outs = refs[n:2 * n]
        ici_send, ici_recv, load_sems, d2d_send, d2d_recv = refs[2 * n:]
        x, y, c, _ = _place()
        sent = _gather_sends(outs, specs, ici_send, ici_recv)
        for parts, _ in sent:
            for cp in parts:
                cp.start()
        for k, (fx, fy) in enumerate(CHIP_FLIPS):
            q = 2 * _flip(x, fx) + _flip(y, fy)
            for wi in range(n):
                sent[3 * wi + k][1].wait_recv()
                got = _half_slab(outs[wi], specs[wi], q, c)
                _staged(got, got, None, (x, y, 1 - c), load_sems, d2d_send, None, d2d_recv.at[3 * wi + k])
        for _, whole in sent:
            whole.wait_send()
        for k, (fx, fy) in enumerate(CHIP_FLIPS):
            q = 2 * _flip(x, fx) + _flip(y, fy)
            for wi in range(n):
                _landed(_half_slab(outs[wi], specs[wi], q, 1 - c), d2d_recv.at[3 * wi + k], d2d_send.at[0])

    return pl.pallas_call(
        body, name=name, in_specs=[ANY] * n, out_specs=[ANY] * n,
        out_shape=[jax.ShapeDtypeStruct(f.shape, f.dtype) for f in fulls],
        input_output_aliases={i: i for i in range(n)},
        scratch_shapes=[pltpu.SemaphoreType.DMA((3 * n,)), pltpu.SemaphoreType.DMA((3 * n,)),
                        pltpu.SemaphoreType.DMA((2,)), pltpu.SemaphoreType.DMA((2,)),
                        pltpu.SemaphoreType.DMA((3 * n,))],
    )(*fulls)


def _gather_chips_carry(fulls, specs):
    n = len(specs)

    def issue(ins, outs, sems, want):
        del ins
        return [(parts, whole, True) for parts, whole in _gather_sends(outs, specs, sems[0], sems[1], want)]

    return _Carry(list(fulls), [jax.ShapeDtypeStruct(f.shape, f.dtype) for f in fulls], {i: i for i in range(n)},
                  [pltpu.SemaphoreType.DMA((3 * n,)), pltpu.SemaphoreType.DMA((3 * n,))], issue)


def _gather_pass_on(fulls, specs, *, name):
    n = len(specs)

    def body(*refs):
        outs = refs[n:2 * n]
        load_sems, d2d_send, d2d_recv = refs[2 * n:2 * n + 3]
        bufs = refs[2 * n + 3:]
        x, y, c, _ = _place()
        loads, sends = [], []
        for k, (fx, fy) in enumerate(CHIP_FLIPS):
            q = 2 * _flip(x, fx) + _flip(y, fy)
            for wi in range(n):
                j = 3 * wi + k
                got = _half_slab(outs[wi], specs[wi], q, c)
                loads.append(pltpu.make_async_copy(got, bufs[j], load_sems.at[j]))
                sends.append(pltpu.make_async_remote_copy(
                    src_ref=bufs[j], dst_ref=got, send_sem=d2d_send.at[j], recv_sem=d2d_recv.at[j],
                    device_id=(x, y, 1 - c), device_id_type=MESH))
        for cp in loads:
            cp.start()
        for load, send in zip(loads, sends):
            load.wait()
            send.start()
        for cp in sends:
            cp.wait_send()
        for k, (fx, fy) in enumerate(CHIP_FLIPS):
            q = 2 * _flip(x, fx) + _flip(y, fy)
            for wi in range(n):
                _landed(_half_slab(outs[wi], specs[wi], q, 1 - c), d2d_recv.at[3 * wi + k], d2d_send.at[0])

    shapes = []
    for _, rows, cols, axis, size in specs:
        shapes += [(rows // 2, size) if axis == 1 else (size, cols // 2)] * 3
    assert sum(math.prod(s) for s in shapes) * 2 <= VMEM_LIMIT_V7X // 2
    stages = [pltpu.VMEM(s, BF16) for s in shapes]
    return pl.pallas_call(
        body, name=name, in_specs=[ANY] * n, out_specs=[ANY] * n,
        out_shape=[jax.ShapeDtypeStruct(f.shape, f.dtype) for f in fulls],
        input_output_aliases={i: i for i in range(n)},
        scratch_shapes=[pltpu.SemaphoreType.DMA((3 * n,))] * 3 + stages,
        compiler_params=pltpu.CompilerParams(vmem_limit_bytes=VMEM_LIMIT_V7X),
    )(*fulls)


def _sibling_exchange(grads, *, name):
    n = len(grads)

    def body(*refs):
        ins, outs = refs[:n], refs[n:2 * n]
        load_sems, send_sems, recv_sems = refs[2 * n:2 * n + 3]
        bufs = refs[2 * n + 3:]
        x, y, c, _ = _place()
        _push_all([(_slab(ins[i], haxis, 1 - c, hsize), outs[i], None) for i, (_, haxis, hsize) in enumerate(grads)],
                  (x, y, 1 - c), bufs, load_sems, send_sems, None, recv_sems)
        for i in range(n):
            _landed(outs[i], recv_sems.at[i], send_sems.at[0])

    shapes = [jax.ShapeDtypeStruct((hsize, g.shape[1]) if haxis == 0 else (g.shape[0], hsize), g.dtype)
              for g, haxis, hsize in grads]
    n_pieces = sum(_stage_pieces(sh.shape, sh.dtype) for sh in shapes)
    return pl.pallas_call(
        body, name=name, in_specs=[ANY] * n, out_specs=[ANY] * n, out_shape=shapes,
        scratch_shapes=[pltpu.SemaphoreType.DMA((n_pieces,)), pltpu.SemaphoreType.DMA((n_pieces,)),
                        pltpu.SemaphoreType.DMA((n,))] + [pltpu.VMEM(sh.shape, sh.dtype) for sh in shapes],
        compiler_params=pltpu.CompilerParams(vmem_limit_bytes=VMEM_LIMIT_V7X),
    )(*[g for g, _, _ in grads])


def _chip_exchange_carry(parts):
    n = len(parts)

    def issue(ins, outs, sems, want):
        send_sems, recv_sems, local_sems = sems
        x, y, c, p = _place()
        copies = []
        for i, (_, axis, size) in enumerate(parts):
            copies.append(_chunked(lambda s, d, i=i: pltpu.make_async_copy(s, d, local_sems.at[i]),
                                   _slab(ins[i], axis, p, size), outs[i].at[p], want) + (False,))
            for k, (fx, fy) in enumerate(CHIP_FLIPS):
                px, py = _flip(x, fx), _flip(y, fy)
                copies.append(_chunked(lambda s, d, j=3 * i + k, px=px, py=py: pltpu.make_async_remote_copy(
                    src_ref=s, dst_ref=d, send_sem=send_sems.at[j], recv_sem=recv_sems.at[j],
                    device_id=(px, py, c), device_id_type=MESH),
                    _slab(ins[i], axis, 2 * px + py, size), outs[i].at[p], want) + (True,))
        return copies

    shapes = []
    for a, axis, size in parts:
        shapes.append(jax.ShapeDtypeStruct((4, size, a.shape[1]) if axis == 0 else (4, a.shape[0], size), a.dtype))
    return _Carry([a for a, _, _ in parts], shapes, {},
                  [pltpu.SemaphoreType.DMA((3 * n,)), pltpu.SemaphoreType.DMA((3 * n,)),
                   pltpu.SemaphoreType.DMA((n,))], issue)


def _sibling_share(sums):
    n = len(sums)

    def body(*refs):
        ins, outs = refs[:n], refs[n:2 * n]
        load_sems, send_sems, store_sems, recv_sems = refs[2 * n:2 * n + 4]
        bufs = refs[2 * n + 4:]
        x, y, c, _ = _place()
        places = [_slab(outs[i], haxis, c, s.shape[haxis]) for i, (s, haxis) in enumerate(sums)]
        _push_all([(ins[i], places[i], places[i]) for i in range(n)], (x, y, 1 - c), bufs,
                  load_sems, send_sems, store_sems, recv_sems)
        for i, (s, haxis) in enumerate(sums):
            _landed(_slab(outs[i], haxis, 1 - c, s.shape[haxis]), recv_sems.at[i], send_sems.at[0])

    shapes = []
    for s, haxis in sums:
        r, cc = s.shape
        shapes.append(jax.ShapeDtypeStruct((2 * r, cc) if haxis == 0 else (r, 2 * cc), F32))
    n_pieces = sum(_stage_pieces(s.shape, s.dtype) for s, _ in sums)
    return pl.pallas_call(
        body, name="grad_sibling_share", in_specs=[ANY] * n, out_specs=[ANY] * n, out_shape=shapes,
        scratch_shapes=[pltpu.SemaphoreType.DMA((n_pieces,))] * 3 + [pltpu.SemaphoreType.DMA((n,))]
                       + [pltpu.VMEM(s.shape, s.dtype) for s, _ in sums],
        compiler_params=pltpu.CompilerParams(vmem_limit_bytes=VMEM_LIMIT_V7X),
    )(*[s for s, _ in sums])


class _Exchanges:
    def __init__(self, fulls):
        self.w_in = _gather_weights([fulls["w_in"]], W_SPECS[:1], name="gather_w_in")[0]
        self.late_specs = W_SPECS[1:]
        self.late = [fulls[s[0]] for s in self.late_specs]
        self.slots = {}
        self.core = lax.axis_index("c").astype(jnp.int32).reshape(1)

    def late_carry(self):
        return _gather_chips_carry(self.late, self.late_specs)

    def late_weights(self, carried):
        done = _gather_pass_on(carried, self.late_specs, name="gather_pass_on")
        return dict(zip([s[0] for s in self.late_specs], done))

    def _half_sums(self, gw, gwb, specs, tag):
        grads, sent = [], []
        for name, _, _, axis, _ in specs:
            for part in (("w_ffn_a", "w_ffn_b") if name == "w_ffn_in" else (name,)):
                half = _half_spec(gw[part].shape[0], gw[part].shape[1], axis)
                grads.append((gw[part],) + half)
                sent.append((gwb[part],) + half)
        theirs = _sibling_exchange(sent, name=f"grad_sibling_exchange_{tag}")
        parts, j = [], 0
        for name, _, _, axis, size in specs:
            take = 2 if name == "w_ffn_in" else 1
            items = [grads[i] + (theirs[i],) for i in range(j, j + take)]
            parts.append((_add_halves(items, self.core, name=f"half_sum_{name}"), axis, size))
            j += take
        return parts

    def early_carries(self, gw, gwb):
        parts = self._half_sums(gw, gwb, self.late_specs, "early")
        self.early_names = [s[0] for s in self.late_specs]
        cut = self.early_names.index("w_ffn_in")
        return _chip_exchange_carry(parts[cut:]), _chip_exchange_carry(parts[:cut])

    def early_done(self, carried_bulk, carried_rest):
        cut = self.early_names.index("w_ffn_in")
        self.slots.update(zip(self.early_names[cut:], carried_bulk))
        self.slots.update(zip(self.early_names[:cut], carried_rest))

    def w_in_carry(self, g, gb):
        return _chip_exchange_carry(self._half_sums({"w_in": g}, {"w_in": gb}, W_SPECS[:1], "w_in"))

    def w_in_done(self, carried):
        self.slots["w_in"] = carried[0]

    def reduced(self):
        sums = []
        for name, rows, cols, axis, _ in W_SPECS:
            sums.append((_add_slots(self.slots[name], name=f"chip_sum_{name}"), _half_spec(rows, cols, axis)[0]))
        return dict(zip([s[0] for s in W_SPECS], _sibling_share(sums)))


def _small_allreduce(sv):
    def body(sv_ref, o_ref, slots_ref, send_sems, recv_sems):
        x, y, c, _ = _place()
        me = 4 * x + 2 * y + c
        slots_ref[me] = sv_ref[...]
        copies = []
        for k in range(1, 8):
            fx, fy, fc = (k >> 2) & 1, (k >> 1) & 1, k & 1
            copies.append(pltpu.make_async_remote_copy(
                src_ref=sv_ref, dst_ref=slots_ref.at[me], send_sem=send_sems.at[k - 1],
                recv_sem=recv_sems.at[k - 1], device_id=(_flip(x, fx), _flip(y, fy), _flip(c, fc)),
                device_id_type=MESH))
        for cp in copies:
            cp.start()
        for cp in copies:
            cp.wait_recv()
        for cp in copies:
            cp.wait_send()
        total = slots_ref[0]
        for s in range(1, 8):
            total = total + slots_ref[s]
        o_ref[...] = total

    vm = pl.BlockSpec(memory_space=pltpu.VMEM)
    return pl.pallas_call(
        body, name="small_allreduce", in_specs=[vm], out_specs=vm,
        out_shape=jax.ShapeDtypeStruct(sv.shape, F32),
        scratch_shapes=[pltpu.VMEM((8,) + sv.shape, F32), pltpu.SemaphoreType.DMA((7,)),
                        pltpu.SemaphoreType.DMA((7,))],
    )(sv)


SMALL_ROWS = (("norm_mix_gain", 0), ("norm_mem_gain", 1), ("norm_ffn_gain", 2))
SMALL_LB = (("lb_logits_fw", 3), ("lb_logits_bw", 5))
SMALL_HEAD = ("hg_norm_gain", "da_q_gain", "da_k_gain", "mem_q_gain", "mem_k_gain")
LOSS_COL = HEAD * len(SMALL_HEAD)


def _pack_small(d):
    last = jnp.concatenate([d[n] for n in SMALL_HEAD] + [jnp.zeros((1, D_MODEL - HEAD * len(SMALL_HEAD)), F32)], axis=1)
    return jnp.concatenate([d["norm_mix_gain"], d["norm_mem_gain"], d["norm_ffn_gain"],
                            d["lb_logits_fw"], d["lb_logits_bw"], last], axis=0)


def _unpack_small(a):
    out = {n: a[r:r + 1] for n, r in SMALL_ROWS}
    out.update({n: a[r:r + 2] for n, r in SMALL_LB})
    out.update({n: a[7:8, j * HEAD:(j + 1) * HEAD] for j, n in enumerate(SMALL_HEAD)})
    return out


PARAM_ORDER = ("norm_mix_gain", "norm_mem_gain", "w_in", "lb_logits_fw", "lb_logits_bw", "hg_norm_gain",
               "da_q_gain", "da_k_gain", "w_mem_kv", "mem_q_gain", "mem_k_gain", "w_proj_hg", "w_proj_da",
               "w_proj_mem", "w_out", "norm_ffn_gain", "w_ffn_in", "w_ffn_out")


def kernel(x, mem, norm_mix_gain, norm_mem_gain, w_in, lb_logits_fw, lb_logits_bw, hg_norm_gain, da_q_gain, da_k_gain, w_mem_kv, mem_q_gain, mem_k_gain, w_proj_hg, w_proj_da, w_proj_mem, w_out, norm_ffn_gain, w_ffn_in, w_ffn_out, loss_target, m_norm_mix_gain, m_norm_mem_gain, m_w_in, m_lb_logits_fw, m_lb_logits_bw, m_hg_norm_gain, m_da_q_gain, m_da_k_gain, m_w_mem_kv, m_mem_q_gain, m_mem_k_gain, m_w_proj_hg, m_w_proj_da, m_w_proj_mem, m_w_out, m_norm_ffn_gain, m_w_ffn_in, m_w_ffn_out, v_norm_mix_gain, v_norm_mem_gain, v_w_in, v_lb_logits_fw, v_lb_logits_bw, v_hg_norm_gain, v_da_q_gain, v_da_k_gain, v_w_mem_kv, v_mem_q_gain, v_mem_k_gain, v_w_proj_hg, v_w_proj_da, v_w_proj_mem, v_w_out, v_norm_ffn_gain, v_w_ffn_in, v_w_ffn_out):
    args = dict(locals())
    mats = tuple(s[0] for s in W_SPECS)
    flat = lambda a: a.reshape(a.shape[-2:])
    w = {n: flat(args[n]) for n in mats}
    m = {n: flat(args["m_" + n]) for n in mats}
    v = {n: flat(args["v_" + n]) for n in mats}
    small = {n: args[n] for n in PARAM_ORDER if n not in mats}

    chip = (2 * lax.axis_index("x") + lax.axis_index("y")).astype(jnp.int32).reshape(1)
    ex = _Exchanges({n: _cast_into_full(w[n], chip, rows, cols, axis, name=f"cast_{n}")
                     for n, rows, cols, axis, _ in W_SPECS})
    grad_x, small_grads = _local_step(x[0], mem[0], loss_target[0], small, ex)
    grads = ex.reduced()
    small_sum = _small_allreduce(small_grads)

    delta, new_m, new_v = {}, {}, {}
    for n in mats:
        delta[n], new_m[n], new_v[n], grads[n] = _adamw(w[n], grads[n], m[n], v[n], name=f"adamw_{n}")
    packed = _adamw(_pack_small(small), small_sum,
                    _pack_small({n: args["m_" + n] for n in small}),
                    _pack_small({n: args["v_" + n] for n in small}), name="adamw_small")
    for dst, src in zip((delta, new_m, new_v, grads), packed):
        dst.update(_unpack_small(src))

    def shaped(d, n):
        return d[n].reshape(args[n].shape)

    return (small_sum[7, LOSS_COL], grad_x[None], *[shaped(grads, n) for n in PARAM_ORDER], *[shaped(delta, n) for n in PARAM_ORDER],
            *[shaped(new_m, n) for n in PARAM_ORDER], *[shaped(new_v, n) for n in PARAM_ORDER])
```
